```python
import math
import jax, jax.numpy as jnp
from jax import lax
import numpy as np

D_MODEL = 1024
BATCH = 8
SEQ = 8192
DEPTH = 1

MEM_LEN = 256
HG_HEADS = 8
HG_DK = 128
HG_DV = D_MODEL // HG_HEADS
HG_WIDTH = HG_HEADS * HG_DK
HG_VWIDTH = HG_HEADS * HG_DV
HG_CHUNK = 64
SWA_HEADS = 16
SWA_KV_HEADS = 2
SWA_HEAD_DIM = 64
SWA_GROUP = SWA_HEADS // SWA_KV_HEADS
SWA_WIDTH = SWA_HEADS * SWA_HEAD_DIM
SWA_KV_WIDTH = SWA_KV_HEADS * SWA_HEAD_DIM
SWA_WINDOW = 128
SWA_BLOCK = 128
MEM_HEADS = 4
MEM_HEAD_DIM = 256
MEM_WIDTH = MEM_HEADS * MEM_HEAD_DIM
N_BRANCHES = 3
NUM_BUCKETS = 32
MAX_DISTANCE = 128
D_FF = 4 * D_MODEL
LN_EPS = 1e-5
RMS_EPS = 1e-6
IN_SPLITS = (HG_WIDTH, HG_WIDTH, HG_VWIDTH, HG_VWIDTH, SWA_WIDTH, SWA_KV_WIDTH, SWA_KV_WIDTH, MEM_WIDTH, N_BRANCHES * D_MODEL)
IN_COLS = sum(IN_SPLITS)

kernel_name = "hybrid_hgrn2_swa_sink_memory_deepnorm"


def split_cols(z, sizes):
    out = []
    start = 0
    for s in sizes:
        out.append(z[..., start:start + s])
        start += s
    return out


def layer_norm(x, g, b):
    x = x.astype(jnp.float32)
    mu = jnp.mean(x, axis=-1, keepdims=True)
    xc = x - mu
    var = jnp.mean(xc * xc, axis=-1, keepdims=True)
    return xc * lax.rsqrt(var + LN_EPS) * g.astype(jnp.float32) + b.astype(jnp.float32)


def t5_bucket(n):
    max_exact = NUM_BUCKETS // 2
    nf = jnp.maximum(n, 1).astype(jnp.float32)
    large = max_exact + (jnp.log(nf / max_exact) / math.log(MAX_DISTANCE / max_exact)
                         * (NUM_BUCKETS - max_exact)).astype(jnp.int32)
    large = jnp.minimum(large, NUM_BUCKETS - 1)
    return jnp.where(n < max_exact, n, large)


def hgrn2(q, f_logit, v, lb):
    B, S = q.shape[0], q.shape[1]
    n_chunks = S // HG_CHUNK
    f = lb + (1.0 - lb) * jax.nn.sigmoid(f_logit.astype(jnp.float32))
    log_f = jnp.log(f)
    k = 1.0 - f

    def to_chunks(t):
        return t.astype(jnp.float32).reshape(B, n_chunks, HG_CHUNK, HG_HEADS, -1).transpose(1, 0, 3, 2, 4)

    qc, kc, vc, gc = to_chunks(q), to_chunks(k), to_chunks(v), to_chunks(log_f)
    b = jnp.cumsum(gc, axis=3)
    b_last = b[:, :, :, -1:, :]
    q_in = qc * jnp.exp(b)
    k_in = kc * jnp.exp(-b)
    k_out = kc * jnp.exp(b_last - b)
    causal = jnp.tril(jnp.ones((HG_CHUNK, HG_CHUNK), dtype=bool))
    attn = jnp.einsum('nbhcd,nbhsd->nbhcs', q_in, k_in)
    attn = jnp.where(causal, attn, 0.0)
    o_intra = jnp.einsum('nbhcs,nbhsv->nbhcv', attn, vc)

    def step(state, inp):
        k_o, v_n, decay = inp
        new = decay[..., :, None] * state + jnp.einsum('bhsd,bhsv->bhdv', k_o, v_n)
        return new, state

    init = jnp.zeros((B, HG_HEADS, HG_DK, HG_DV), jnp.float32)
    _, states = lax.scan(step, init, (k_out, vc, jnp.exp(b_last[:, :, :, 0, :])))
    o_inter = jnp.einsum('nbhcd,nbhdv->nbhcv', q_in, states)
    o = o_intra + o_inter
    return o.transpose(1, 0, 3, 2, 4).reshape(B, S, HG_HEADS, HG_DV)


def sliding_window_attention(q, k, v, rel_bias, sinks):
    B, S = q.shape[0], q.shape[1]
    nb = S // SWA_BLOCK
    scale = SWA_HEAD_DIM ** -0.5
    qb = q.astype(jnp.float32).reshape(B, nb, SWA_BLOCK, SWA_KV_HEADS, SWA_GROUP, SWA_HEAD_DIM) * scale
    kb = k.astype(jnp.float32).reshape(B, nb, SWA_BLOCK, SWA_KV_HEADS, SWA_HEAD_DIM)
    vb = v.astype(jnp.float32).reshape(B, nb, SWA_BLOCK, SWA_KV_HEADS, SWA_HEAD_DIM)
    pad = jnp.zeros_like(kb[:, :1])
    kk = jnp.concatenate([jnp.concatenate([pad, kb[:, :-1]], axis=1), kb], axis=2)
    vv = jnp.concatenate([jnp.concatenate([pad, vb[:, :-1]], axis=1), vb], axis=2)
    s = jnp.einsum('bnqkgd,bnskd->bkgnqs', qb, kk)
    qi = jnp.arange(SWA_BLOCK)[:, None] + SWA_BLOCK
    kj = jnp.arange(2 * SWA_BLOCK)[None, :]
    dist = qi - kj
    band = (dist >= 0) & (dist < SWA_WINDOW)
    valid = band[None] & ((jnp.arange(nb)[:, None, None] > 0) | (kj[None] >= SWA_BLOCK))
    bucket = t5_bucket(jnp.clip(dist, 0, SWA_WINDOW - 1))
    bias = rel_bias.astype(jnp.float32)[bucket].transpose(2, 0, 1)
    bias = bias.reshape(SWA_KV_HEADS, SWA_GROUP, 1, SWA_BLOCK, 2 * SWA_BLOCK)
    s = jnp.where(valid, s + bias, -jnp.inf)
    sink = sinks.astype(jnp.float32).reshape(SWA_KV_HEADS, SWA_GROUP, 1, 1, 1)
    m = jnp.maximum(jnp.max(s, axis=-1, keepdims=True), sink)
    p = jnp.exp(s - m)
    p = p / (jnp.sum(p, axis=-1, keepdims=True) + jnp.exp(sink - m))
    o = jnp.einsum('bkgnqs,bnskd->bnqkgd', p, vv)
    return o.reshape(B, S, SWA_WIDTH)


def memory_attention(q, mk, mv):
    B, S = q.shape[0], q.shape[1]
    M = mk.shape[1]
    qh = q.astype(jnp.float32).reshape(B, S, MEM_HEADS, MEM_HEAD_DIM) * (MEM_HEAD_DIM ** -0.5)
    kh = mk.astype(jnp.float32).reshape(B, M, MEM_HEADS, MEM_HEAD_DIM)
    vh = mv.astype(jnp.float32).reshape(B, M, MEM_HEADS, MEM_HEAD_DIM)
    p = jax.nn.softmax(jnp.einsum('bshd,bmhd->bhsm', qh, kh), axis=-1)
    return jnp.einsum('bhsm,bmhd->bshd', p, vh).reshape(B, S, MEM_WIDTH)


def _fwd_setup_inputs(seed: int = 0) -> dict:
    key = jax.random.key(seed)
    ks = jax.random.split(key, 18)
    beta = (8.0 * DEPTH) ** -0.25

    def nrm(k, shape, scale):
        return jax.random.normal(k, shape, jnp.float32) * scale

    return {
        "x": nrm(ks[0], (BATCH, SEQ, D_MODEL), 1.0),
        "mem": nrm(ks[1], (BATCH, MEM_LEN, D_MODEL), 1.0),
        "w_in": nrm(ks[2], (DEPTH, D_MODEL, IN_COLS), D_MODEL ** -0.5),
        "lb_logits": nrm(ks[3], (DEPTH + 1, HG_WIDTH), 0.1),
        "hg_norm_gain": 1.0 + nrm(ks[4], (DEPTH, HG_VWIDTH), 0.02),
        "swa_sinks": nrm(ks[5], (DEPTH, SWA_HEADS), 0.5),
        "rel_bias": nrm(ks[6], (NUM_BUCKETS, SWA_HEADS), 0.5),
        "w_mem_kv": nrm(ks[7], (DEPTH, D_MODEL, 2 * MEM_WIDTH), D_MODEL ** -0.5),
        "w_branch_hg": nrm(ks[8], (DEPTH, HG_VWIDTH, D_MODEL), HG_VWIDTH ** -0.5),
        "w_branch_swa": nrm(ks[9], (DEPTH, SWA_WIDTH, D_MODEL), SWA_WIDTH ** -0.5),
        "w_branch_mem": nrm(ks[10], (DEPTH, MEM_WIDTH, D_MODEL), MEM_WIDTH ** -0.5),
        "w_out": nrm(ks[11], (DEPTH, D_MODEL, D_MODEL), (D_MODEL ** -0.5) * beta),
        "ln1_g": 1.0 + nrm(ks[12], (DEPTH, D_MODEL), 0.02),
        "ln1_b": nrm(ks[13], (DEPTH, D_MODEL), 0.02),
        "w_up": nrm(ks[14], (DEPTH, D_MODEL, D_FF), D_MODEL ** -0.5),
        "w_down": nrm(ks[15], (DEPTH, D_FF, D_MODEL), (D_FF ** -0.5) * beta),
        "ln2_g": 1.0 + nrm(ks[16], (DEPTH, D_MODEL), 0.02),
        "ln2_b": nrm(ks[17], (DEPTH, D_MODEL), 0.02),
    }


def _fwd_reference(x, mem, w_in, lb_logits, hg_norm_gain, swa_sinks, rel_bias, w_mem_kv,
              w_branch_hg, w_branch_swa, w_branch_mem, w_out, ln1_g, ln1_b,
              w_up, w_down, ln2_g, ln2_b):
    B, S = x.shape[0], x.shape[1]
    out_dtype = x.dtype
    alpha = (2.0 * DEPTH) ** 0.25
    lb_all = jnp.cumsum(jax.nn.softmax(lb_logits.astype(jnp.float32), axis=0), axis=0)
    memf = mem.astype(jnp.float32)
    h = x.astype(jnp.float32)
    for layer in range(DEPTH):
        z = h @ w_in[layer]
        hq, hf, hi, hg, sq, sk, sv, mq, gl = split_cols(z, IN_SPLITS)
        o_a = hgrn2(hq.reshape(B, S, HG_HEADS, HG_DK), hf.reshape(B, S, HG_HEADS, HG_DK),
                    hi.reshape(B, S, HG_HEADS, HG_DV), lb_all[layer].reshape(HG_HEADS, HG_DK))
        o_a = o_a * lax.rsqrt(jnp.mean(o_a * o_a, axis=-1, keepdims=True) + RMS_EPS)
        o_a = o_a.reshape(B, S, HG_VWIDTH) * hg_norm_gain[layer].astype(jnp.float32) * jax.nn.silu(hg.astype(jnp.float32))
        o_b = sliding_window_attention(sq, sk, sv, rel_bias, swa_sinks[layer])
        mk, mv = split_cols(memf @ w_mem_kv[layer], (MEM_WIDTH, MEM_WIDTH))
        o_c = memory_attention(mq, mk, mv)
        gates = jax.nn.sigmoid(gl.astype(jnp.float32).reshape(B, S, N_BRANCHES, D_MODEL))
        merged = (gates[:, :, 0] * (o_a @ w_branch_hg[layer])
                  + gates[:, :, 1] * (o_b @ w_branch_swa[layer])
                  + gates[:, :, 2] * (o_c @ w_branch_mem[layer]))
        mix = merged @ w_out[layer]
        h = layer_norm(alpha * h + mix, ln1_g[layer], ln1_b[layer])
        ff = jnp.square(jax.nn.relu(h @ w_up[layer])) @ w_down[layer]
        h = layer_norm(alpha * h + ff, ln2_g[layer], ln2_b[layer])
    return h.astype(out_dtype)


import jax as _jax
import jax.numpy as _jnp

TWIN_FORMAT = 'train_step'
FWD_PARAMS = ['x', 'mem', 'w_in', 'lb_logits', 'hg_norm_gain', 'swa_sinks', 'rel_bias', 'w_mem_kv', 'w_branch_hg', 'w_branch_swa', 'w_branch_mem', 'w_out', 'ln1_g', 'ln1_b', 'w_up', 'w_down', 'ln2_g', 'ln2_b']
TWIN_WEIGHTS = ['w_in', 'lb_logits', 'hg_norm_gain', 'swa_sinks', 'rel_bias', 'w_mem_kv', 'w_branch_hg', 'w_branch_swa', 'w_branch_mem', 'w_out', 'ln1_g', 'ln1_b', 'w_up', 'w_down', 'ln2_g', 'ln2_b']
TWIN_DIFF_INPUT = 'x'
TWIN_INPUTS = ['x', 'mem', 'w_in', 'lb_logits', 'hg_norm_gain', 'swa_sinks', 'rel_bias', 'w_mem_kv', 'w_branch_hg', 'w_branch_swa', 'w_branch_mem', 'w_out', 'ln1_g', 'ln1_b', 'w_up', 'w_down', 'ln2_g', 'ln2_b', 'loss_target', 'm_w_in', 'm_lb_logits', 'm_hg_norm_gain', 'm_swa_sinks', 'm_rel_bias', 'm_w_mem_kv', 'm_w_branch_hg', 'm_w_branch_swa', 'm_w_branch_mem', 'm_w_out', 'm_ln1_g', 'm_ln1_b', 'm_w_up', 'm_w_down', 'm_ln2_g', 'm_ln2_b', 'v_w_in', 'v_lb_logits', 'v_hg_norm_gain', 'v_swa_sinks', 'v_rel_bias', 'v_w_mem_kv', 'v_w_branch_hg', 'v_w_branch_swa', 'v_w_branch_mem', 'v_w_out', 'v_ln1_g', 'v_ln1_b', 'v_w_up', 'v_w_down', 'v_ln2_g', 'v_ln2_b']
TWIN_OUTPUTS = ['loss', 'grad_x', 'grad_w_in', 'grad_lb_logits', 'grad_hg_norm_gain', 'grad_swa_sinks', 'grad_rel_bias', 'grad_w_mem_kv', 'grad_w_branch_hg', 'grad_w_branch_swa', 'grad_w_branch_mem', 'grad_w_out', 'grad_ln1_g', 'grad_ln1_b', 'grad_w_up', 'grad_w_down', 'grad_ln2_g', 'grad_ln2_b', 'delta_w_in', 'delta_lb_logits', 'delta_hg_norm_gain', 'delta_swa_sinks', 'delta_rel_bias', 'delta_w_mem_kv', 'delta_w_branch_hg', 'delta_w_branch_swa', 'delta_w_branch_mem', 'delta_w_out', 'delta_ln1_g', 'delta_ln1_b', 'delta_w_up', 'delta_w_down', 'delta_ln2_g', 'delta_ln2_b', 'new_m_w_in', 'new_m_lb_logits', 'new_m_hg_norm_gain', 'new_m_swa_sinks', 'new_m_rel_bias', 'new_m_w_mem_kv', 'new_m_w_branch_hg', 'new_m_w_branch_swa', 'new_m_w_branch_mem', 'new_m_w_out', 'new_m_ln1_g', 'new_m_ln1_b', 'new_m_w_up', 'new_m_w_down', 'new_m_ln2_g', 'new_m_ln2_b', 'new_v_w_in', 'new_v_lb_logits', 'new_v_hg_norm_gain', 'new_v_swa_sinks', 'new_v_rel_bias', 'new_v_w_mem_kv', 'new_v_w_branch_hg', 'new_v_w_branch_swa', 'new_v_w_branch_mem', 'new_v_w_out', 'new_v_ln1_g', 'new_v_ln1_b', 'new_v_w_up', 'new_v_w_down', 'new_v_ln2_g', 'new_v_ln2_b']
TWIN_LEAF_KINDS = {'loss': 'loss', 'grad_x': 'grad_x', 'grad_w_in': 'grad_w', 'grad_lb_logits': 'grad_w', 'grad_hg_norm_gain': 'grad_w', 'grad_swa_sinks': 'grad_w', 'grad_rel_bias': 'grad_w', 'grad_w_mem_kv': 'grad_w', 'grad_w_branch_hg': 'grad_w', 'grad_w_branch_swa': 'grad_w', 'grad_w_branch_mem': 'grad_w', 'grad_w_out': 'grad_w', 'grad_ln1_g': 'grad_w', 'grad_ln1_b': 'grad_w', 'grad_w_up': 'grad_w', 'grad_w_down': 'grad_w', 'grad_ln2_g': 'grad_w', 'grad_ln2_b': 'grad_w', 'delta_w_in': 'delta_w', 'delta_lb_logits': 'delta_w', 'delta_hg_norm_gain': 'delta_w', 'delta_swa_sinks': 'delta_w', 'delta_rel_bias': 'delta_w', 'delta_w_mem_kv': 'delta_w', 'delta_w_branch_hg': 'delta_w', 'delta_w_branch_swa': 'delta_w', 'delta_w_branch_mem': 'delta_w', 'delta_w_out': 'delta_w', 'delta_ln1_g': 'delta_w', 'delta_ln1_b': 'delta_w', 'delta_w_up': 'delta_w', 'delta_w_down': 'delta_w', 'delta_ln2_g': 'delta_w', 'delta_ln2_b': 'delta_w', 'new_m_w_in': 'new_m', 'new_m_lb_logits': 'new_m', 'new_m_hg_norm_gain': 'new_m', 'new_m_swa_sinks': 'new_m', 'new_m_rel_bias': 'new_m', 'new_m_w_mem_kv': 'new_m', 'new_m_w_branch_hg': 'new_m', 'new_m_w_branch_swa': 'new_m', 'new_m_w_branch_mem': 'new_m', 'new_m_w_out': 'new_m', 'new_m_ln1_g': 'new_m', 'new_m_ln1_b': 'new_m', 'new_m_w_up': 'new_m', 'new_m_w_down': 'new_m', 'new_m_ln2_g': 'new_m', 'new_m_ln2_b': 'new_m', 'new_v_w_in': 'new_v', 'new_v_lb_logits': 'new_v', 'new_v_hg_norm_gain': 'new_v', 'new_v_swa_sinks': 'new_v', 'new_v_rel_bias': 'new_v', 'new_v_w_mem_kv': 'new_v', 'new_v_w_branch_hg': 'new_v', 'new_v_w_branch_swa': 'new_v', 'new_v_w_branch_mem': 'new_v', 'new_v_w_out': 'new_v', 'new_v_ln1_g': 'new_v', 'new_v_ln1_b': 'new_v', 'new_v_w_up': 'new_v', 'new_v_w_down': 'new_v', 'new_v_ln2_g': 'new_v', 'new_v_ln2_b': 'new_v'}


def _forward(args):
    return _fwd_reference(*[args[k] for k in FWD_PARAMS])


def _output_shape():
    def fwd():
        inp = _fwd_setup_inputs(0)
        return _fwd_reference(*[inp[k] for k in FWD_PARAMS])
    out = _jax.eval_shape(fwd)
    return out.shape, out.dtype

N_MICROBATCH = 1
ADAM_LR = 0.001
ADAM_B1 = 0.9
ADAM_B2 = 0.999
ADAM_EPS = 1e-08
ADAM_WD = 0.01
ADAM_STEP = 10
PER_EXAMPLE_BATCH_AXIS = {'x': 0, 'mem': 0, 'loss_target': 0}
SHARED_INPUTS = []
_WEIGHT_DTYPES = {'w_in': _jnp.float32, 'lb_logits': _jnp.float32, 'hg_norm_gain': _jnp.float32, 'swa_sinks': _jnp.float32, 'rel_bias': _jnp.float32, 'w_mem_kv': _jnp.float32, 'w_branch_hg': _jnp.float32, 'w_branch_swa': _jnp.float32, 'w_branch_mem': _jnp.float32, 'w_out': _jnp.float32, 'ln1_g': _jnp.float32, 'ln1_b': _jnp.float32, 'w_up': _jnp.float32, 'w_down': _jnp.float32, 'ln2_g': _jnp.float32, 'ln2_b': _jnp.float32}
MOMENT_SCALE = {'w_in': 3.138170e-02, 'lb_logits': 2.743268e-02, 'hg_norm_gain': 4.226602e-02, 'swa_sinks': 1.119279e-02, 'rel_bias': 1.834935e-02, 'w_mem_kv': 7.862468e-03, 'w_branch_hg': 4.178244e-02, 'w_branch_swa': 1.555354e-02, 'w_branch_mem': 8.336841e-03, 'w_out': 7.570236e-02, 'ln1_g': 1.516884e+00, 'ln1_b': 9.976868e-01, 'w_up': 7.619610e-02, 'w_down': 3.706490e-01, 'ln2_g': 6.414630e+01, 'ln2_b': 1.370181e+01}


def _to_microbatches(a, axis):
    t = _jnp.moveaxis(a, axis, 0)
    t = t.reshape((N_MICROBATCH, t.shape[0] // N_MICROBATCH) + t.shape[1:])
    return _jnp.moveaxis(t, 1, axis + 1)


def setup_inputs(seed: int = 0) -> dict:
    inp = _fwd_setup_inputs(seed)
    key = _jax.random.fold_in(_jax.random.key(seed), 7919)
    shape, _ = _output_shape()
    out = dict(inp)
    out["loss_target"] = _jax.random.normal(_jax.random.fold_in(key, 0), shape, _jnp.float32)
    for i, name in enumerate(TWIN_WEIGHTS):
        w = inp[name].astype(_jnp.float32)
        if MOMENT_SCALE is None:
            s = _jnp.sqrt(_jnp.mean(_jnp.square(w)) + 1e-30)
        else:
            s = MOMENT_SCALE[name]
        km, kv = _jax.random.split(_jax.random.fold_in(key, i + 1))
        out[name] = w
        out["m_" + name] = s * _jax.random.normal(km, w.shape, _jnp.float32)
        out["v_" + name] = (s * s) * _jax.random.uniform(kv, w.shape, _jnp.float32, 0.5, 1.5)
    if N_MICROBATCH > 1:
        for name, axis in PER_EXAMPLE_BATCH_AXIS.items():
            out[name] = _to_microbatches(out[name], axis)
    return {'x': out['x'], 'mem': out['mem'], 'w_in': out['w_in'], 'lb_logits': out['lb_logits'], 'hg_norm_gain': out['hg_norm_gain'], 'swa_sinks': out['swa_sinks'], 'rel_bias': out['rel_bias'], 'w_mem_kv': out['w_mem_kv'], 'w_branch_hg': out['w_branch_hg'], 'w_branch_swa': out['w_branch_swa'], 'w_branch_mem': out['w_branch_mem'], 'w_out': out['w_out'], 'ln1_g': out['ln1_g'], 'ln1_b': out['ln1_b'], 'w_up': out['w_up'], 'w_down': out['w_down'], 'ln2_g': out['ln2_g'], 'ln2_b': out['ln2_b'], 'loss_target': out['loss_target'], 'm_w_in': out['m_w_in'], 'm_lb_logits': out['m_lb_logits'], 'm_hg_norm_gain': out['m_hg_norm_gain'], 'm_swa_sinks': out['m_swa_sinks'], 'm_rel_bias': out['m_rel_bias'], 'm_w_mem_kv': out['m_w_mem_kv'], 'm_w_branch_hg': out['m_w_branch_hg'], 'm_w_branch_swa': out['m_w_branch_swa'], 'm_w_branch_mem': out['m_w_branch_mem'], 'm_w_out': out['m_w_out'], 'm_ln1_g': out['m_ln1_g'], 'm_ln1_b': out['m_ln1_b'], 'm_w_up': out['m_w_up'], 'm_w_down': out['m_w_down'], 'm_ln2_g': out['m_ln2_g'], 'm_ln2_b': out['m_ln2_b'], 'v_w_in': out['v_w_in'], 'v_lb_logits': out['v_lb_logits'], 'v_hg_norm_gain': out['v_hg_norm_gain'], 'v_swa_sinks': out['v_swa_sinks'], 'v_rel_bias': out['v_rel_bias'], 'v_w_mem_kv': out['v_w_mem_kv'], 'v_w_branch_hg': out['v_w_branch_hg'], 'v_w_branch_swa': out['v_w_branch_swa'], 'v_w_branch_mem': out['v_w_branch_mem'], 'v_w_out': out['v_w_out'], 'v_ln1_g': out['v_ln1_g'], 'v_ln1_b': out['v_ln1_b'], 'v_w_up': out['v_w_up'], 'v_w_down': out['v_w_down'], 'v_ln2_g': out['v_ln2_g'], 'v_ln2_b': out['v_ln2_b']}


def _loss(weights, diff, rest, loss_target):
    with _jax.named_scope("forward"):
        args = {**rest, TWIN_DIFF_INPUT: diff, **{k: w.astype(_WEIGHT_DTYPES[k]) for k, w in weights.items()}}
        y = _forward(args)
    with _jax.named_scope("loss_head"):
        err = _jnp.square(y.astype(_jnp.float32) - loss_target)
        return 0.5 * _jnp.sum(_jnp.mean(err, axis=-1)) if err.ndim else 0.5 * err


def _adamw(w, g, m, v):
    m = ADAM_B1 * m + (1.0 - ADAM_B1) * g
    v = ADAM_B2 * v + (1.0 - ADAM_B2) * _jnp.square(g)
    m_hat = m / (1.0 - ADAM_B1 ** ADAM_STEP)
    v_hat = v / (1.0 - ADAM_B2 ** ADAM_STEP)
    delta = -ADAM_LR * (m_hat / (_jnp.sqrt(v_hat) + ADAM_EPS) + ADAM_WD * w)
    return delta, m, v


def reference(x, mem, w_in, lb_logits, hg_norm_gain, swa_sinks, rel_bias, w_mem_kv, w_branch_hg, w_branch_swa, w_branch_mem, w_out, ln1_g, ln1_b, w_up, w_down, ln2_g, ln2_b, loss_target, m_w_in, m_lb_logits, m_hg_norm_gain, m_swa_sinks, m_rel_bias, m_w_mem_kv, m_w_branch_hg, m_w_branch_swa, m_w_branch_mem, m_w_out, m_ln1_g, m_ln1_b, m_w_up, m_w_down, m_ln2_g, m_ln2_b, v_w_in, v_lb_logits, v_hg_norm_gain, v_swa_sinks, v_rel_bias, v_w_mem_kv, v_w_branch_hg, v_w_branch_swa, v_w_branch_mem, v_w_out, v_ln1_g, v_ln1_b, v_w_up, v_w_down, v_ln2_g, v_ln2_b):
    given = dict(x=x, mem=mem, w_in=w_in, lb_logits=lb_logits, hg_norm_gain=hg_norm_gain, swa_sinks=swa_sinks, rel_bias=rel_bias, w_mem_kv=w_mem_kv, w_branch_hg=w_branch_hg, w_branch_swa=w_branch_swa, w_branch_mem=w_branch_mem, w_out=w_out, ln1_g=ln1_g, ln1_b=ln1_b, w_up=w_up, w_down=w_down, ln2_g=ln2_g, ln2_b=ln2_b, loss_target=loss_target, m_w_in=m_w_in, m_lb_logits=m_lb_logits, m_hg_norm_gain=m_hg_norm_gain, m_swa_sinks=m_swa_sinks, m_rel_bias=m_rel_bias, m_w_mem_kv=m_w_mem_kv, m_w_branch_hg=m_w_branch_hg, m_w_branch_swa=m_w_branch_swa, m_w_branch_mem=m_w_branch_mem, m_w_out=m_w_out, m_ln1_g=m_ln1_g, m_ln1_b=m_ln1_b, m_w_up=m_w_up, m_w_down=m_w_down, m_ln2_g=m_ln2_g, m_ln2_b=m_ln2_b, v_w_in=v_w_in, v_lb_logits=v_lb_logits, v_hg_norm_gain=v_hg_norm_gain, v_swa_sinks=v_swa_sinks, v_rel_bias=v_rel_bias, v_w_mem_kv=v_w_mem_kv, v_w_branch_hg=v_w_branch_hg, v_w_branch_swa=v_w_branch_swa, v_w_branch_mem=v_w_branch_mem, v_w_out=v_w_out, v_ln1_g=v_ln1_g, v_ln1_b=v_ln1_b, v_w_up=v_w_up, v_w_down=v_w_down, v_ln2_g=v_ln2_g, v_ln2_b=v_ln2_b)
    weights = {n: given[n] for n in TWIN_WEIGHTS}
    shared = {n: given[n] for n in SHARED_INPUTS}
    per_example = {n: given[n] for n in ['x', 'mem']}
    grad_fn = _jax.value_and_grad(_loss, argnums=(0, 1))

    def one_microbatch(ex, loss_target):
        ex = dict(ex)
        diff = ex.pop(TWIN_DIFF_INPUT)
        return grad_fn(weights, diff, {**shared, **ex}, loss_target)

    if N_MICROBATCH == 1:
        loss, (grad_w, grad_x) = one_microbatch(per_example, given["loss_target"])
    else:
        def body(carry, xs):
            loss_sum, grad_sum = carry
            l_k, (gw_k, gx_k) = one_microbatch(xs[0], xs[1])
            with _jax.named_scope("update"):
                return (loss_sum + l_k, _jax.tree.map(_jnp.add, grad_sum, gw_k)), gx_k

        init = (_jnp.zeros((), _jnp.float32), _jax.tree.map(_jnp.zeros_like, weights))
        (loss, grad_w), grad_x = _jax.lax.scan(body, init, (per_example, given["loss_target"]))
    with _jax.named_scope("update"):
        delta_w, new_m, new_v = {}, {}, {}
        for n in TWIN_WEIGHTS:
            delta_w[n], new_m[n], new_v[n] = _adamw(weights[n], grad_w[n], given["m_" + n], given["v_" + n])
    return (loss, grad_x, *[grad_w[n] for n in TWIN_WEIGHTS], *[delta_w[n] for n in TWIN_WEIGHTS],
            *[new_m[n] for n in TWIN_WEIGHTS], *[new_v[n] for n in TWIN_WEIGHTS])
```

```python
import functools
import math

import jax
import jax.numpy as jnp
from jax import lax
from jax.experimental import pallas as pl
from jax.experimental.pallas import tpu as pltpu

F32 = jnp.float32
BF16 = jnp.bfloat16

D_MODEL = 1024
MEM_LEN = 256
HG_HEADS = 8
HG_DK = 128
HG_CHUNK = 64
SWA_HEADS = 16
SWA_HEAD_DIM = 64
SWA_BLOCK = 128
SWA_WINDOW = 128
MEM_HEADS = 4
MEM_HEAD_DIM = 256
NUM_BUCKETS = 32
MAX_DISTANCE = 128
D_FF = 4096
LN_EPS = 1e-5
RMS_EPS = 1e-6
ALPHA = 2.0 ** 0.25
N_DEV = 8

C_HQ, C_HF, C_HI, C_HG, C_SQ, C_SK, C_SV, C_MQ, C_GL = 0, 1024, 2048, 3072, 4096, 5120, 5248, 5376, 6400
IN_COLS = 9472
IN_SHARD = IN_COLS // N_DEV

ADAM_LR = 0.001
ADAM_B1 = 0.9
ADAM_B2 = 0.999
ADAM_EPS = 1e-08
ADAM_WD = 0.01
ADAM_STEP = 10

VMEM_LIMIT = 56 * 1024 * 1024

R_DN, R_UP, R_BH, R_BS, R_BM, R_OUT, R_KV, R_OTHER = 0, 512, 1024, 1152, 1280, 1408, 1536, 1792

SM_LB, SM_GAIN, SM_SINK, SM_RB, SM_L1G, SM_L1B, SM_L2G, SM_L2B, SM_LOSS, SM_ROWS = 0, 2, 3, 4, 5, 6, 7, 8, 9, 16


def _bf(v):
    return v.astype(BF16)


def _dot(a, b):
    return jnp.dot(a, b, preferred_element_type=F32)


def _dot_nt(a, b):
    return lax.dot_general(a, b, (((1,), (1,)), ((), ())), preferred_element_type=F32)


def _dot_tn(a, b):
    return lax.dot_general(a, b, (((0,), (0,)), ((), ())), preferred_element_type=F32)


def _sig(v):
    return 1.0 / (1.0 + jnp.exp(-v))


def _cparams(*sem):
    return pltpu.CompilerParams(dimension_semantics=sem, vmem_limit_bytes=VMEM_LIMIT)


def _const_spec(shape):
    nd = len(shape)
    return pl.BlockSpec(shape, lambda *_: (0,) * nd, pipeline_mode=pl.Buffered(1))


def _mm_nt(a, bt, *, tm, tn, out_dtype, name):
    M, K = a.shape
    N = bt.shape[0]

    def body(a_ref, b_ref, o_ref):
        o_ref[...] = _dot_nt(_bf(a_ref[...]), _bf(b_ref[...])).astype(o_ref.dtype)

    return pl.pallas_call(
        body,
        grid=(N // tn, M // tm),
        in_specs=[pl.BlockSpec((tm, K), lambda j, i: (i, 0)), pl.BlockSpec((tn, K), lambda j, i: (j, 0))],
        out_specs=pl.BlockSpec((tm, tn), lambda j, i: (i, j)),
        out_shape=jax.ShapeDtypeStruct((M, N), out_dtype),
        compiler_params=_cparams("parallel", "parallel"),
        name=name,
    )(a, bt)


def _mm_tn_resident(a, b, *, tm, kc, name):
    K, M = a.shape
    N = b.shape[1]
    nk = K // kc

    def body(a_ref, b_ref, o_ref):
        acc = jnp.zeros((tm, N), F32)
        for kk in range(nk):
            sl = pl.ds(kk * kc, kc)
            acc = acc + _dot_tn(_bf(a_ref[sl, :]), _bf(b_ref[sl, :]))
        o_ref[...] = acc

    return pl.pallas_call(
        body,
        grid=(M // tm,),
        in_specs=[pl.BlockSpec((K, tm), lambda i: (0, i)), _const_spec((K, N))],
        out_specs=pl.BlockSpec((tm, N), lambda i: (i, 0)),
        out_shape=jax.ShapeDtypeStruct((M, N), F32),
        compiler_params=_cparams("parallel"),
        name=name,
    )(a, b)


def _mm_nn_resident(a, b, add, *, tm, kc, name):
    M, K = a.shape
    N = b.shape[1]
    nk = K // kc

    def body(a_ref, b_ref, add_ref, o_ref):
        acc = add_ref[...]
        for kk in range(nk):
            sl = pl.ds(kk * kc, kc)
            acc = acc + _dot(_bf(a_ref[:, sl]), b_ref[sl, :])
        o_ref[...] = acc

    return pl.pallas_call(
        body,
        grid=(M // tm,),
        in_specs=[pl.BlockSpec((tm, K), lambda i: (i, 0)), _const_spec((K, N)), pl.BlockSpec((tm, N), lambda i: (i, 0))],
        out_specs=pl.BlockSpec((tm, N), lambda i: (i, 0)),
        out_shape=jax.ShapeDtypeStruct((M, N), F32),
        compiler_params=_cparams("parallel"),
        name=name,
    )(a, b, add)


def _lower_bound(lbl_ref):
    l0 = lbl_ref[0:1, :]
    l1 = lbl_ref[1:2, :]
    mx = jnp.maximum(l0, l1)
    e0 = jnp.exp(l0 - mx)
    e1 = jnp.exp(l1 - mx)
    return e0 / (e0 + e1)


def _tri(lower):
    r = lax.broadcasted_iota(jnp.int32, (HG_CHUNK, HG_CHUNK), 0)
    c = lax.broadcasted_iota(jnp.int32, (HG_CHUNK, HG_CHUNK), 1)
    return (r >= c) if lower else (r <= c)


def _hg_gates(fl, lb):
    sg = _sig(fl)
    f = lb + (1.0 - lb) * sg
    return sg, f, jnp.log(f), 1.0 - f


def _cumsum_chunk(tri_f32, g):
    return jnp.dot(tri_f32, g, preferred_element_type=F32, precision=lax.Precision.HIGHEST)


def _hgrn_fwd(zmain, lb_logits, *, T):
    S = zmain.shape[0]
    nc = T // HG_CHUNK

    def body(q_ref, f_ref, v_ref, lbl_ref, o_ref, st_ref, state):
        @pl.when(pl.program_id(1) == 0)
        def _():
            state[...] = jnp.zeros_like(state)

        lb = _lower_bound(lbl_ref)
        tril = _tri(True)
        tril_f = tril.astype(F32)
        for c in range(nc):
            sl = pl.ds(c * HG_CHUNK, HG_CHUNK)
            _, _, g, k = _hg_gates(f_ref[sl, :], lb)
            b = _cumsum_chunk(tril_f, g)
            bl = jnp.sum(g, axis=0, keepdims=True)
            qi = _bf(q_ref[sl, :] * jnp.exp(b))
            ki = _bf(k * jnp.exp(-b))
            ko = _bf(k * jnp.exp(bl - b))
            vb = _bf(v_ref[sl, :])
            att = jnp.where(tril, _dot_nt(qi, ki), 0.0)
            st = state[...]
            st_ref[0, c] = st
            o_ref[sl, :] = _dot(_bf(att), vb) + _dot_nt(qi, _bf(st))
            state[...] = st * jnp.exp(bl) + _dot_tn(vb, ko)

    col = lambda base: pl.BlockSpec((T, HG_DK), lambda h, t: (t, base + h))
    return pl.pallas_call(
        body,
        grid=(HG_HEADS, S // T),
        in_specs=[col(0), col(8), col(16), pl.BlockSpec((2, HG_DK), lambda h, t: (0, h))],
        out_specs=[
            pl.BlockSpec((T, HG_DK), lambda h, t: (t, h)),
            pl.BlockSpec((1, nc, HG_DK, HG_DK), lambda h, t: (h, t, 0, 0)),
        ],
        out_shape=[
            jax.ShapeDtypeStruct((S, D_MODEL), F32),
            jax.ShapeDtypeStruct((HG_HEADS, S // HG_CHUNK, HG_DK, HG_DK), F32),
        ],
        scratch_shapes=[pltpu.VMEM((HG_DK, HG_DK), F32)],
        compiler_params=_cparams("parallel", "arbitrary"),
        name="hgrn_fwd",
    )(zmain, zmain, zmain, lb_logits)


def _hgrn_bwd(zmain, lb_logits, states, d_o, *, T):
    S = zmain.shape[0]
    nc = T // HG_CHUNK
    nt = S // T

    def body(q_ref, f_ref, v_ref, lbl_ref, st_ref, do_ref, dq_ref, df_ref, dv_ref, dlb_ref, dstate):
        @pl.when(pl.program_id(1) == 0)
        def _():
            dstate[...] = jnp.zeros_like(dstate)
            dlb_ref[...] = jnp.zeros_like(dlb_ref)

        lb = _lower_bound(lbl_ref)
        tril = _tri(True)
        tril_f = tril.astype(F32)
        triu_f = _tri(False).astype(F32)
        last_row = lax.broadcasted_iota(jnp.int32, (HG_CHUNK, HG_DK), 0) == HG_CHUNK - 1
        dlb = jnp.zeros((1, HG_DK), F32)
        for c in reversed(range(nc)):
            sl = pl.ds(c * HG_CHUNK, HG_CHUNK)
            sg, f, g, k = _hg_gates(f_ref[sl, :], lb)
            b = _cumsum_chunk(tril_f, g)
            bl = jnp.sum(g, axis=0, keepdims=True)
            eb = jnp.exp(b)
            enb = jnp.exp(-b)
            eo = jnp.exp(bl - b)
            ebl = jnp.exp(bl)
            q_in = q_ref[sl, :] * eb
            k_in = k * enb
            k_out = k * eo
            qi, ki, ko = _bf(q_in), _bf(k_in), _bf(k_out)
            vb = _bf(v_ref[sl, :])
            dob = do_ref[sl, :]
            att = jnp.where(tril, _dot_nt(qi, ki), 0.0)
            st = st_ref[0, c]
            dst = dstate[...]
            dstb = _bf(dst)
            d_att = _bf(jnp.where(tril, _dot_nt(dob, vb), 0.0))
            d_v = _dot_tn(_bf(att), dob) + _dot_nt(ko, dstb)
            d_qin = _dot(d_att, ki) + _dot(dob, _bf(st))
            d_kin = _dot_tn(d_att, qi)
            d_kout = _dot(vb, dstb)
            d_decay = jnp.sum(dst * st, axis=0, keepdims=True)
            dstate[...] = dst * ebl + _dot_tn(dob, qi)
            kk = d_kout * k_out
            d_b = d_qin * q_in - d_kin * k_in - kk
            d_bl = jnp.sum(kk, axis=0, keepdims=True) + d_decay * ebl
            d_b = d_b + jnp.where(last_row, d_bl, 0.0)
            d_g = _cumsum_chunk(triu_f, d_b)
            d_f = d_g / f - (d_kin * enb + d_kout * eo)
            dq_ref[sl, :] = _bf(d_qin * eb)
            df_ref[sl, :] = _bf(d_f * (1.0 - lb) * sg * (1.0 - sg))
            dv_ref[sl, :] = _bf(d_v)
            dlb = dlb + jnp.sum(d_f * (1.0 - sg), axis=0, keepdims=True)
        dlb_ref[...] += dlb

    rev = lambda base: pl.BlockSpec((T, HG_DK), lambda h, t: (nt - 1 - t, base + h))
    outc = pl.BlockSpec((T, HG_DK), lambda h, t: (nt - 1 - t, h))
    return pl.pallas_call(
        body,
        grid=(HG_HEADS, nt),
        in_specs=[
            rev(0), rev(8), rev(16),
            pl.BlockSpec((2, HG_DK), lambda h, t: (0, h)),
            pl.BlockSpec((1, nc, HG_DK, HG_DK), lambda h, t: (h, nt - 1 - t, 0, 0)),
            outc,
        ],
        out_specs=[outc, outc, outc, pl.BlockSpec((1, HG_DK), lambda h, t: (0, h))],
        out_shape=[jax.ShapeDtypeStruct((S, D_MODEL), BF16)] * 3 + [jax.ShapeDtypeStruct((1, D_MODEL), F32)],
        scratch_shapes=[pltpu.VMEM((HG_DK, HG_DK), F32)],
        compiler_params=_cparams("parallel", "arbitrary"),
        name="hgrn_bwd",
    )(zmain, zmain, zmain, lb_logits, states, d_o)


def _t5_bucket_table():
    qi = jnp.arange(SWA_BLOCK)[:, None] + SWA_BLOCK
    kj = jnp.arange(2 * SWA_BLOCK)[None, :]
    n = jnp.clip(qi - kj, 0, SWA_WINDOW - 1)
    max_exact = NUM_BUCKETS // 2
    nf = jnp.maximum(n, 1).astype(F32)
    large = max_exact + (jnp.log(nf / max_exact) / math.log(MAX_DISTANCE / max_exact)
                         * (NUM_BUCKETS - max_exact)).astype(jnp.int32)
    large = jnp.minimum(large, NUM_BUCKETS - 1)
    return jnp.where(n < max_exact, n, large).astype(jnp.int32)


def _swa_valid(n):
    qi = lax.broadcasted_iota(jnp.int32, (SWA_BLOCK, 2 * SWA_BLOCK), 0) + SWA_BLOCK
    kj = lax.broadcasted_iota(jnp.int32, (SWA_BLOCK, 2 * SWA_BLOCK), 1)
    dist = qi - kj
    return (dist >= 0) & (dist < SWA_WINDOW) & ((n > 0) | (kj >= SWA_BLOCK))


def _swa_bias_init(bias, bucket_ref, rb_ref):
    bk = bucket_ref[...]
    for h in range(SWA_HEADS):
        def sel(b, acc, h=h):
            return jnp.where(bk == b, rb_ref[b, h], acc)
        bias[h] = lax.fori_loop(0, NUM_BUCKETS, sel, jnp.zeros(bk.shape, F32))


def _lane_halves(t, kv_head):
    lane = lax.broadcasted_iota(jnp.int32, t.shape, 1)
    rolled = pltpu.roll(t, 64, 1)
    zero = jnp.zeros_like(t)
    if kv_head == 0:
        return jnp.where(lane < 64, t, zero), jnp.where(lane >= 64, rolled, zero)
    return jnp.where(lane < 64, rolled, zero), jnp.where(lane >= 64, t, zero)


def _swa_probs(s, bias_h, valid, sink):
    s = jnp.where(valid, s + bias_h, -jnp.inf)
    m = jnp.maximum(jnp.max(s, axis=-1, keepdims=True), sink)
    p = jnp.exp(s - m)
    es = jnp.exp(sink - m)
    inv = 1.0 / (jnp.sum(p, axis=-1, keepdims=True) + es)
    return p * inv, es * inv


def _swa_fwd(zmain, bucket, rel_bias, sinks):
    S = zmain.shape[0]
    nb = S // SWA_BLOCK
    scale = SWA_HEAD_DIM ** -0.5

    def body(q_ref, kvc_ref, kvp_ref, bucket_ref, rb_ref, sk_ref, o_ref, bias):
        n = pl.program_id(0)

        @pl.when(n == 0)
        def _():
            _swa_bias_init(bias, bucket_ref, rb_ref)

        valid = _swa_valid(n)
        kk = _bf(jnp.concatenate([kvp_ref[:, 0:128], kvc_ref[:, 0:128]], axis=0))
        vv = _bf(jnp.concatenate([kvp_ref[:, 128:256], kvc_ref[:, 128:256]], axis=0))
        for kvh in range(2):
            ka, kb = _lane_halves(kk, kvh)
            va, vb = _lane_halves(vv, kvh)
            for jj in range(4):
                j = kvh * 4 + jj
                cols = pl.ds(j * 128, 128)
                qp = _bf(q_ref[:, cols] * scale)
                pe, _ = _swa_probs(_dot_nt(qp, ka), bias[2 * j], valid, sk_ref[0, 2 * j])
                po, _ = _swa_probs(_dot_nt(qp, kb), bias[2 * j + 1], valid, sk_ref[0, 2 * j + 1])
                o_ref[:, cols] = _dot(_bf(pe), va) + _dot(_bf(po), vb)

    smem = pl.BlockSpec(memory_space=pltpu.SMEM)
    return pl.pallas_call(
        body,
        grid=(nb,),
        in_specs=[
            pl.BlockSpec((SWA_BLOCK, 1024), lambda n: (n, C_SQ // 1024)),
            pl.BlockSpec((SWA_BLOCK, 256), lambda n: (n, C_SK // 256)),
            pl.BlockSpec((SWA_BLOCK, 256), lambda n: (jnp.maximum(n - 1, 0), C_SK // 256)),
            _const_spec((SWA_BLOCK, 2 * SWA_BLOCK)), smem, smem,
        ],
        out_specs=pl.BlockSpec((SWA_BLOCK, 1024), lambda n: (n, 0)),
        out_shape=jax.ShapeDtypeStruct((S, 1024), F32),
        scratch_shapes=[pltpu.VMEM((SWA_HEADS, SWA_BLOCK, 2 * SWA_BLOCK), F32)],
        compiler_params=_cparams("arbitrary"),
        name="swa_fwd",
    )(zmain, zmain, zmain, bucket, rel_bias, sinks)


def _swa_bwd(zmain, o_b, d_o, bucket, rel_bias, sinks):
    S = zmain.shape[0]
    nb = S // SWA_BLOCK
    scale = SWA_HEAD_DIM ** -0.5

    def body(q_ref, kvc_ref, kvp_ref, o_ref, do_ref, bucket_ref, rb_ref, sk_ref,
             dq_ref, dkv_ref, drb_ref, dsk_ref, bias, dbias, carry):
        n = pl.program_id(0)

        @pl.when(n == 0)
        def _():
            _swa_bias_init(bias, bucket_ref, rb_ref)
            dbias[...] = jnp.zeros_like(dbias)
            carry[...] = jnp.zeros_like(carry)
            dsk_ref[...] = jnp.zeros_like(dsk_ref)

        @pl.when(n < nb)
        def _():
            valid = _swa_valid(n)
            kk = _bf(jnp.concatenate([kvp_ref[:, 0:128], kvc_ref[:, 0:128]], axis=0))
            vv = _bf(jnp.concatenate([kvp_ref[:, 128:256], kvc_ref[:, 128:256]], axis=0))
            lane = lax.broadcasted_iota(jnp.int32, (2 * SWA_BLOCK, 128), 1)
            lane_q = lax.broadcasted_iota(jnp.int32, (SWA_BLOCK, 128), 1)
            dk_parts, dv_parts = [], []
            for kvh in range(2):
                ka, kb = _lane_halves(kk, kvh)
                va, vb = _lane_halves(vv, kvh)
                zk = jnp.zeros((2 * SWA_BLOCK, 128), F32)
                zv = jnp.zeros((2 * SWA_BLOCK, 128), F32)
                for jj in range(4):
                    j = kvh * 4 + jj
                    cols = pl.ds(j * 128, 128)
                    qp = _bf(q_ref[:, cols] * scale)
                    dop = do_ref[:, cols]
                    prod = dop.astype(F32) * o_ref[:, cols]
                    dq_acc = jnp.zeros((SWA_BLOCK, 128), F32)
                    for odd, (kx, vx) in enumerate(((ka, va), (kb, vb))):
                        h = 2 * j + odd
                        p, ps = _swa_probs(_dot_nt(qp, kx), bias[h], valid, sk_ref[0, h])
                        keep = (lane_q >= 64) if odd else (lane_q < 64)
                        delta = jnp.sum(jnp.where(keep, prod, 0.0), axis=-1, keepdims=True)
                        ds = p * (_dot_nt(dop, vx) - delta)
                        dbias[h] += ds
                        dsk_ref[h] += jnp.broadcast_to(-jnp.sum(ps * delta, axis=0, keepdims=True), (8, 128))
                        dsb = _bf(ds)
                        dq_acc = dq_acc + _dot(dsb, kx)
                        keep_k = (lane >= 64) if odd else (lane < 64)
                        zk = zk + jnp.where(keep_k, _dot_tn(dsb, qp), 0.0)
                        zv = zv + jnp.where(keep_k, _dot_tn(_bf(p), dop), 0.0)
                    dq_ref[:, cols] = _bf(dq_acc * scale)
                dk_parts.append(zk + pltpu.roll(zk, 64, 1))
                dv_parts.append(zv + pltpu.roll(zv, 64, 1))
            dk = jnp.where(lane < 64, dk_parts[0], dk_parts[1])
            dv = jnp.where(lane < 64, dv_parts[0], dv_parts[1])
            dkv = jnp.concatenate([dk, dv], axis=1)
            dkv_ref[...] = _bf(carry[...] + dkv[0:SWA_BLOCK])
            carry[...] = dkv[SWA_BLOCK:]

        @pl.when(n == nb)
        def _():
            dkv_ref[...] = _bf(carry[...])
            bk = bucket_ref[...]

            def per_head(h, _):
                db = dbias[h]

                def per_bucket(b, _):
                    tot = jnp.sum(jnp.where(bk == b, db, 0.0), axis=1, keepdims=True)
                    tot = jnp.sum(tot, axis=0, keepdims=True)
                    drb_ref[h * NUM_BUCKETS + b] = jnp.broadcast_to(tot, (8, 128))
                    return 0

                return lax.fori_loop(0, NUM_BUCKETS, per_bucket, 0)

            lax.fori_loop(0, SWA_HEADS, per_head, 0)

    smem = pl.BlockSpec(memory_space=pltpu.SMEM)
    cur = lambda n: jnp.minimum(n, nb - 1)
    prev = lambda n: jnp.maximum(jnp.minimum(n, nb - 1) - 1, 0)
    return pl.pallas_call(
        body,
        grid=(nb + 1,),
        in_specs=[
            pl.BlockSpec((SWA_BLOCK, 1024), lambda n: (cur(n), C_SQ // 1024)),
            pl.BlockSpec((SWA_BLOCK, 256), lambda n: (cur(n), C_SK // 256)),
            pl.BlockSpec((SWA_BLOCK, 256), lambda n: (prev(n), C_SK // 256)),
            pl.BlockSpec((SWA_BLOCK, 1024), lambda n: (cur(n), 0)),
            pl.BlockSpec((SWA_BLOCK, 1024), lambda n: (cur(n), 0)),
            _const_spec((SWA_BLOCK, 2 * SWA_BLOCK)), smem, smem,
        ],
        out_specs=[
            pl.BlockSpec((SWA_BLOCK, 1024), lambda n: (cur(n), 0)),
            pl.BlockSpec((SWA_BLOCK, 256), lambda n: (jnp.maximum(n - 1, 0), 0)),
            pl.BlockSpec((SWA_HEADS * NUM_BUCKETS, 8, 128), lambda n: (0, 0, 0)),
            pl.BlockSpec((SWA_HEADS, 8, 128), lambda n: (0, 0, 0)),
        ],
        out_shape=[
            jax.ShapeDtypeStruct((S, 1024), BF16),
            jax.ShapeDtypeStruct((S, 256), BF16),
            jax.ShapeDtypeStruct((SWA_HEADS * NUM_BUCKETS, 8, 128), F32),
            jax.ShapeDtypeStruct((SWA_HEADS, 8, 128), F32),
        ],
        scratch_shapes=[
            pltpu.VMEM((SWA_HEADS, SWA_BLOCK, 2 * SWA_BLOCK), F32),
            pltpu.VMEM((SWA_HEADS, SWA_BLOCK, 2 * SWA_BLOCK), F32),
            pltpu.VMEM((SWA_BLOCK, 256), F32),
        ],
        compiler_params=_cparams("arbitrary"),
        name="swa_bwd",
    )(zmain, zmain, zmain, o_b, d_o, bucket, rel_bias, sinks)


def _mem_probs(q_ref, k_ref):
    qs = _bf(q_ref[...] * (MEM_HEAD_DIM ** -0.5))
    s = _dot_nt(qs, _bf(k_ref[...]))
    e = jnp.exp(s - jnp.max(s, axis=-1, keepdims=True))
    return qs, e / jnp.sum(e, axis=-1, keepdims=True)


def _mem_fwd(zmain, mkv, *, T):
    S = zmain.shape[0]

    def body(q_ref, k_ref, v_ref, o_ref):
        _, p = _mem_probs(q_ref, k_ref)
        o_ref[...] = _dot(_bf(p), _bf(v_ref[...]))

    return pl.pallas_call(
        body,
        grid=(MEM_HEADS, S // T),
        in_specs=[
            pl.BlockSpec((T, MEM_HEAD_DIM), lambda h, t: (t, C_MQ // MEM_HEAD_DIM + h)),
            pl.BlockSpec((MEM_LEN, MEM_HEAD_DIM), lambda h, t: (0, h)),
            pl.BlockSpec((MEM_LEN, MEM_HEAD_DIM), lambda h, t: (0, MEM_HEADS + h)),
        ],
        out_specs=pl.BlockSpec((T, MEM_HEAD_DIM), lambda h, t: (t, h)),
        out_shape=jax.ShapeDtypeStruct((S, 1024), F32),
        compiler_params=_cparams("parallel", "parallel"),
        name="mem_fwd",
    )(zmain, mkv, mkv)


def _mem_bwd(zmain, mkv, o_c, d_o, *, T):
    S = zmain.shape[0]
    scale = MEM_HEAD_DIM ** -0.5

    def body(q_ref, k_ref, v_ref, o_ref, do_ref, dq_ref, dk_ref, dv_ref):
        @pl.when(pl.program_id(1) == 0)
        def _():
            dk_ref[...] = jnp.zeros_like(dk_ref)
            dv_ref[...] = jnp.zeros_like(dv_ref)

        qs, p = _mem_probs(q_ref, k_ref)
        dob = do_ref[...]
        delta = jnp.sum(dob.astype(F32) * o_ref[...], axis=-1, keepdims=True)
        ds = _bf(p * (_dot_nt(dob, _bf(v_ref[...])) - delta))
        dq_ref[...] = _bf(_dot(ds, _bf(k_ref[...])) * scale)
        dk_ref[...] += _dot_tn(ds, qs)
        dv_ref[...] += _dot_tn(_bf(p), dob)

    tile = lambda base: pl.BlockSpec((T, MEM_HEAD_DIM), lambda h, t: (t, base + h))
    return pl.pallas_call(
        body,
        grid=(MEM_HEADS, S // T),
        in_specs=[
            tile(C_MQ // MEM_HEAD_DIM),
            pl.BlockSpec((MEM_LEN, MEM_HEAD_DIM), lambda h, t: (0, h)),
            pl.BlockSpec((MEM_LEN, MEM_HEAD_DIM), lambda h, t: (0, MEM_HEADS + h)),
            tile(0), tile(0),
        ],
        out_specs=[
            tile(0),
            pl.BlockSpec((MEM_LEN, MEM_HEAD_DIM), lambda h, t: (0, h)),
            pl.BlockSpec((MEM_LEN, MEM_HEAD_DIM), lambda h, t: (0, h)),
        ],
        out_shape=[jax.ShapeDtypeStruct((S, 1024), BF16)] + [jax.ShapeDtypeStruct((MEM_LEN, 1024), F32)] * 2,
        compiler_params=_cparams("parallel", "arbitrary"),
        name="mem_bwd",
    )(zmain, mkv, mkv, o_c, d_o)


def _layer_norm(u):
    mu = jnp.mean(u, axis=-1, keepdims=True)
    xc = u - mu
    rstd = lax.rsqrt(jnp.mean(xc * xc, axis=-1, keepdims=True) + LN_EPS)
    return xc * rstd, rstd


def _layer_norm_bwd(dy, gamma, xhat, rstd):
    dxh = dy * gamma
    return rstd * (dxh - jnp.mean(dxh, axis=-1, keepdims=True) - xhat * jnp.mean(dxh * xhat, axis=-1, keepdims=True))


def _merge_forward(oraw_ref, hg_ref, ob_ref, oc_ref, gl_ref, x_ref, gain_ref, wbh, wbs, wbm, wout):
    ys, rs = [], []
    for h in range(HG_HEADS):
        oh = oraw_ref[:, pl.ds(h * HG_DK, HG_DK)]
        r = lax.rsqrt(jnp.mean(oh * oh, axis=-1, keepdims=True) + RMS_EPS)
        ys.append(oh * r)
        rs.append(r)
    y = jnp.concatenate(ys, axis=1)
    hg = hg_ref[...]
    sg = _sig(hg)
    silu = hg * sg
    oa = _bf(y * gain_ref[...] * silu)
    pa = _dot(oa, wbh[...])
    pb = _dot(_bf(ob_ref[...]), wbs[...])
    pc = _dot(_bf(oc_ref[...]), wbm[...])
    g0 = _sig(gl_ref[:, 0:1024])
    g1 = _sig(gl_ref[:, 1024:2048])
    g2 = _sig(gl_ref[:, 2048:3072])
    m = _bf(g0 * pa + g1 * pb + g2 * pc)
    u1 = ALPHA * x_ref[...] + _dot(m, wout[...])
    xhat, rstd = _layer_norm(u1)
    return dict(y=y, rs=rs, hg=hg, sg=sg, silu=silu, oa=oa, pa=pa, pb=pb, pc=pc,
                g0=g0, g1=g1, g2=g2, m=m, xhat=xhat, rstd=rstd)


def _merge_in_specs(T):
    row = lambda w, c=0: pl.BlockSpec((T, w), lambda i: (i, c))
    vec = pl.BlockSpec((1, D_MODEL), lambda i: (0, 0))
    w = _const_spec((D_MODEL, D_MODEL))
    return [row(1024), row(1024, C_HG // 1024), row(1024), row(1024), row(3072), row(1024), vec, w, w, w, w, vec, vec]


def _merge_fwd(o_raw, zmain, o_b, o_c, gl, x, gain, wbh, wbs, wbm, wout, ln_g, ln_b, *, T):
    S = x.shape[0]

    def body(oraw_ref, hg_ref, ob_ref, oc_ref, gl_ref, x_ref, gain_ref, wbh_r, wbs_r, wbm_r, wout_r, g_ref, b_ref, h1_ref):
        f = _merge_forward(oraw_ref, hg_ref, ob_ref, oc_ref, gl_ref, x_ref, gain_ref, wbh_r, wbs_r, wbm_r, wout_r)
        h1_ref[...] = f["xhat"] * g_ref[...] + b_ref[...]

    return pl.pallas_call(
        body,
        grid=(S // T,),
        in_specs=_merge_in_specs(T),
        out_specs=pl.BlockSpec((T, D_MODEL), lambda i: (i, 0)),
        out_shape=jax.ShapeDtypeStruct((S, D_MODEL), F32),
        compiler_params=_cparams("parallel"),
        name="merge_fwd",
    )(o_raw, zmain, o_b, o_c, gl, x, gain, wbh, wbs, wbm, wout, ln_g, ln_b)


def _merge_bwd(d_h1, o_raw, zmain, o_b, o_c, gl, x, gain, wbh, wbs, wbm, wout, ln_g, ln_b, *, T):
    S = x.shape[0]

    def body(dh1_ref, oraw_ref, hg_ref, ob_ref, oc_ref, gl_ref, x_ref, gain_ref, wbh_r, wbs_r, wbm_r, wout_r, g_ref, b_ref,
             dx_ref, du1_ref, m_ref, oa_ref, dpa_ref, dpb_ref, dpc_ref, dhg_ref, doraw_ref, dob_ref, doc_ref, dgl_ref,
             dgain_ref, dg_ref, db_ref):
        del b_ref

        @pl.when(pl.program_id(0) == 0)
        def _():
            dgain_ref[...] = jnp.zeros_like(dgain_ref)
            dg_ref[...] = jnp.zeros_like(dg_ref)
            db_ref[...] = jnp.zeros_like(db_ref)

        f = _merge_forward(oraw_ref, hg_ref, ob_ref, oc_ref, gl_ref, x_ref, gain_ref, wbh_r, wbs_r, wbm_r, wout_r)
        dh1 = dh1_ref[...]
        dg_ref[...] += jnp.sum(dh1 * f["xhat"], axis=0, keepdims=True)
        db_ref[...] += jnp.sum(dh1, axis=0, keepdims=True)
        du1 = _layer_norm_bwd(dh1, g_ref[...], f["xhat"], f["rstd"])
        dx_ref[...] = ALPHA * du1
        du1b = _bf(du1)
        du1_ref[...] = du1b
        m_ref[...] = f["m"]
        oa_ref[...] = f["oa"]
        dm = _dot_nt(du1b, wout_r[...])
        for i, (g, p, dp_ref, dob_r, w_r) in enumerate((
                (f["g0"], f["pa"], dpa_ref, None, wbh_r),
                (f["g1"], f["pb"], dpb_ref, dob_ref, wbs_r),
                (f["g2"], f["pc"], dpc_ref, doc_ref, wbm_r))):
            dgl_ref[:, pl.ds(i * 1024, 1024)] = _bf(dm * p * g * (1.0 - g))
            dp = _bf(dm * g)
            dp_ref[...] = dp
            d_branch = _dot_nt(dp, w_r[...])
            if dob_r is not None:
                dob_r[...] = _bf(d_branch)
            else:
                doa = d_branch
        gain = gain_ref[...]
        t = doa * f["y"]
        dgain_ref[...] += jnp.sum(t * f["silu"], axis=0, keepdims=True)
        sg = f["sg"]
        dhg_ref[...] = _bf(t * gain * sg * (1.0 + f["hg"] * (1.0 - sg)))
        dy = doa * gain * f["silu"]
        for h in range(HG_HEADS):
            cols = slice(h * HG_DK, (h + 1) * HG_DK)
            yh = f["y"][:, cols]
            dyh = dy[:, cols]
            doraw_ref[:, pl.ds(h * HG_DK, HG_DK)] = _bf(
                f["rs"][h] * (dyh - yh * jnp.mean(dyh * yh, axis=-1, keepdims=True)))

    row = lambda w: pl.BlockSpec((T, w), lambda i: (i, 0))
    vec = pl.BlockSpec((1, D_MODEL), lambda i: (0, 0))
    bshape = jax.ShapeDtypeStruct((S, D_MODEL), BF16)
    vshape = jax.ShapeDtypeStruct((1, D_MODEL), F32)
    return pl.pallas_call(
        body,
        grid=(S // T,),
        in_specs=[row(1024)] + _merge_in_specs(T),
        out_specs=[row(1024)] * 11 + [row(3072), vec, vec, vec],
        out_shape=[jax.ShapeDtypeStruct((S, D_MODEL), F32)] + [bshape] * 10
        + [jax.ShapeDtypeStruct((S, 3072), BF16), vshape, vshape, vshape],
        compiler_params=_cparams("arbitrary"),
        name="merge_bwd",
    )(d_h1, o_raw, zmain, o_b, o_c, gl, x, gain, wbh, wbs, wbm, wout, ln_g, ln_b)


def _mlp_fwd_bwd(h1, target, wup_t, wdn, ln_g, ln_b, *, T, FC):
    S = h1.shape[0]
    nf = D_FF // FC

    def body(h1_ref, t_ref, wup_ref, wdn_ref, g_ref, b_ref, dh1_ref, a_ref, dup_ref, du2_ref, loss_ref, dg_ref, db_ref, up_scr):
        @pl.when(pl.program_id(0) == 0)
        def _():
            loss_ref[...] = jnp.zeros_like(loss_ref)
            dg_ref[...] = jnp.zeros_like(dg_ref)
            db_ref[...] = jnp.zeros_like(db_ref)

        h1v = h1_ref[...]
        h1b = _bf(h1v)
        ff = jnp.zeros((T, D_MODEL), F32)
        for j in range(nf):
            rows = pl.ds(j * FC, FC)
            up = jnp.maximum(_dot_nt(h1b, wup_ref[rows, :]), 0.0)
            up_scr[:, rows] = up
            a = _bf(up * up)
            a_ref[:, rows] = a
            ff = ff + _dot(a, wdn_ref[rows, :])
        xhat, rstd = _layer_norm(ALPHA * h1v + ff)
        gamma = g_ref[...]
        err = xhat * gamma + b_ref[...] - t_ref[...]
        loss_ref[...] += jnp.sum(jnp.sum(err * err, axis=-1, keepdims=True), axis=0, keepdims=True) * (0.5 / D_MODEL)
        dy = err * (1.0 / D_MODEL)
        dg_ref[...] += jnp.sum(dy * xhat, axis=0, keepdims=True)
        db_ref[...] += jnp.sum(dy, axis=0, keepdims=True)
        du2 = _layer_norm_bwd(dy, gamma, xhat, rstd)
        du2b = _bf(du2)
        du2_ref[...] = du2b
        dh1 = ALPHA * du2
        for j in range(nf):
            rows = pl.ds(j * FC, FC)
            dup = _bf(_dot_nt(du2b, wdn_ref[rows, :]) * (2.0 * up_scr[:, rows]))
            dup_ref[:, rows] = dup
            dh1 = dh1 + _dot(dup, wup_ref[rows, :])
        dh1_ref[...] = dh1

    row = lambda w: pl.BlockSpec((T, w), lambda i: (i, 0))
    vec = pl.BlockSpec((1, D_MODEL), lambda i: (0, 0))
    vshape = jax.ShapeDtypeStruct((1, D_MODEL), F32)
    return pl.pallas_call(
        body,
        grid=(S // T,),
        in_specs=[row(1024), row(1024), _const_spec((D_FF, D_MODEL)), _const_spec((D_FF, D_MODEL)), vec, vec],
        out_specs=[row(1024), row(D_FF), row(D_FF), row(1024), pl.BlockSpec((8, 128), lambda i: (0, 0)), vec, vec],
        out_shape=[
            jax.ShapeDtypeStruct((S, D_MODEL), F32),
            jax.ShapeDtypeStruct((S, D_FF), BF16),
            jax.ShapeDtypeStruct((S, D_FF), BF16),
            jax.ShapeDtypeStruct((S, D_MODEL), BF16),
            jax.ShapeDtypeStruct((8, 128), F32), vshape, vshape,
        ],
        scratch_shapes=[pltpu.VMEM((T, D_FF), F32)],
        compiler_params=_cparams("arbitrary"),
        name="mlp_fwd_bwd",
    )(h1, target, wup_t, wdn, ln_g, ln_b)


def _local_step(x, mem, target, lb_logits, gain, sinks, rel_bias, ln1_g, ln1_b, ln2_g, ln2_b,
                win_t, wkv_t, wbh, wbs, wbm, wout, wup_t, wdn):
    S = x.shape[0]
    T = min(256, S)
    KC = min(1024, S)
    xb = _bf(x)
    zmain = _mm_nt(x, win_t[:C_GL], tm=min(512, S), tn=3200, out_dtype=F32, name="in_proj_main")
    gl = _mm_nt(x, win_t[C_GL:], tm=min(512, S), tn=1536, out_dtype=F32, name="in_proj_gates")
    mkv = _mm_nt(mem, wkv_t, tm=MEM_LEN, tn=1024, out_dtype=F32, name="mem_kv_proj")
    bucket = _t5_bucket_table()

    o_raw, states = _hgrn_fwd(zmain, lb_logits, T=T)
    o_b = _swa_fwd(zmain, bucket, rel_bias, sinks)
    o_c = _mem_fwd(zmain, mkv, T=T)
    merge_args = (o_raw, zmain, o_b, o_c, gl, x, gain, wbh, wbs, wbm, wout, ln1_g, ln1_b)
    h1 = _merge_fwd(*merge_args, T=T)

    d_h1, act, d_up, du2, loss, d_ln2_g, d_ln2_b = _mlp_fwd_bwd(h1, target, wup_t, wdn, ln2_g, ln2_b, T=T, FC=512)
    g_wdn = _mm_tn_resident(act, du2, tm=256, kc=KC, name="grad_w_down")
    g_wup_t = _mm_tn_resident(d_up, _bf(h1), tm=256, kc=KC, name="grad_w_up")

    (dx_part, du1, m, oa, dpa, dpb, dpc, d_hg, d_oraw, d_ob, d_oc, d_gl,
     d_gain, d_ln1_g, d_ln1_b) = _merge_bwd(d_h1, *merge_args, T=min(128, S))
    g_wout = _mm_tn_resident(m, du1, tm=256, kc=KC, name="grad_w_out")
    g_wbh = _mm_tn_resident(oa, dpa, tm=256, kc=KC, name="grad_w_branch_hg")
    g_wbs = _mm_tn_resident(o_b, dpb, tm=256, kc=KC, name="grad_w_branch_swa")
    g_wbm = _mm_tn_resident(o_c, dpc, tm=256, kc=KC, name="grad_w_branch_mem")

    d_mq, d_mk, d_mv = _mem_bwd(zmain, mkv, o_c, d_oc, T=T)
    g_wkv_t = _mm_tn_resident(jnp.concatenate([d_mk, d_mv], axis=1), mem, tm=256, kc=MEM_LEN, name="grad_w_mem_kv")
    d_sq, d_skv, d_rb, d_sink = _swa_bwd(zmain, o_b, d_ob, bucket, rel_bias, sinks)
    d_q, d_f, d_v, d_lb = _hgrn_bwd(zmain, lb_logits, states, d_oraw, T=T)

    dz = jnp.concatenate([d_q, d_f, d_v, d_hg, d_sq, d_skv, d_mq, d_gl], axis=1)
    grad_x = _mm_nn_resident(dz, win_t, dx_part, tm=T, kc=IN_COLS // 2, name="grad_x")
    g_win_t = _mm_tn_resident(dz, xb, tm=256, kc=KC, name="grad_w_in")

    small = dict(
        d_lb=d_lb, d_gain=d_gain, d_sink=d_sink[:, 0, 0].reshape(1, SWA_HEADS),
        d_rb=d_rb[:, 0, 0].reshape(SWA_HEADS, NUM_BUCKETS).T,
        d_ln1_g=d_ln1_g, d_ln1_b=d_ln1_b, d_ln2_g=d_ln2_g, d_ln2_b=d_ln2_b, loss=loss[0, 0])
    big = dict(win_t=g_win_t, wkv_t=g_wkv_t, wbh=g_wbh, wbs=g_wbs, wbm=g_wbm, wout=g_wout, wup_t=g_wup_t, wdn=g_wdn)
    return grad_x, big, small


MESH = pl.DeviceIdType.MESH
ANY = pl.BlockSpec(memory_space=pl.ANY)


def _coords():
    return lax.axis_index("x"), lax.axis_index("y"), lax.axis_index("c")


def _other_chips(x, y):
    return [(1 - x, y), (x, 1 - y), (1 - x, 1 - y)]


def _all_gather_weights(p1, p2):
    arrays = (p1, p2)
    na = len(arrays)

    def body(*refs):
        srcs, dsts = refs[:na], refs[na:2 * na]
        send_sems, recv_sems, local_sems = refs[2 * na:]
        x, y, c = _coords()
        me, sibling = (x, y, c), (x, y, 1 - c)
        chips = _other_chips(x, y)

        def slot(a, px, py, pc):
            return dsts[a].at[4 * px + 2 * py + pc]

        def copy(a, k, block, to, from_shard=False):
            return pltpu.make_async_remote_copy(
                src_ref=srcs[a] if from_shard else slot(a, *block), dst_ref=slot(a, *block),
                send_sem=send_sems.at[a * 7 + k], recv_sem=recv_sems.at[a * 7 + k],
                device_id=to, device_id_type=MESH)

        own = [pltpu.make_async_copy(srcs[a], slot(a, *me), local_sems.at[a]) for a in range(na)]
        for cp in own:
            cp.start()
        first = []
        for a in range(na):
            first.append(copy(a, 0, me, sibling, True))
            first += [copy(a, 1 + j, me, (*chip, c), True) for j, chip in enumerate(chips)]
        for cp in first:
            cp.start()
        passed = []
        for j, chip in enumerate(chips):
            for a in range(na):
                copy(a, 1 + j, (*chip, c), me).wait_recv()
                fwd = copy(a, 4 + j, (*chip, c), sibling)
                fwd.start()
                passed.append(fwd)
        for a in range(na):
            copy(a, 0, sibling, me).wait_recv()
            for j, chip in enumerate(chips):
                copy(a, 4 + j, (*chip, 1 - c), me).wait_recv()
        for cp in first + passed:
            cp.wait_send()
        for cp in own:
            cp.wait()

    return pl.pallas_call(
        body,
        in_specs=[ANY] * na,
        out_specs=[ANY] * na,
        out_shape=[jax.ShapeDtypeStruct((N_DEV,) + a.shape, a.dtype) for a in arrays],
        scratch_shapes=[pltpu.SemaphoreType.DMA((7 * na,)), pltpu.SemaphoreType.DMA((7 * na,)),
                        pltpu.SemaphoreType.DMA((na,))],
        name="all_gather_weights",
    )(*arrays)


def _exchange_with_sibling(p1, p2):
    arrays = (p1, p2)
    na = len(arrays)

    def body(*refs):
        srcs, dsts = refs[:na], refs[na:2 * na]
        send_sems, recv_sems = refs[2 * na:]
        x, y, c = _coords()
        copies = [
            pltpu.make_async_remote_copy(
                src_ref=srcs[a].at[k, 1 - c], dst_ref=dsts[a].at[k],
                send_sem=send_sems.at[a * 4 + k], recv_sem=recv_sems.at[a * 4 + k],
                device_id=(x, y, 1 - c), device_id_type=MESH)
            for a in range(na) for k in range(4)]
        for cp in copies:
            cp.start()
        for cp in copies:
            cp.wait()

    return pl.pallas_call(
        body,
        in_specs=[ANY] * na,
        out_specs=[ANY] * na,
        out_shape=[jax.ShapeDtypeStruct((4,) + a.shape[2:], a.dtype) for a in arrays],
        scratch_shapes=[pltpu.SemaphoreType.DMA((4 * na,)), pltpu.SemaphoreType.DMA((4 * na,))],
        name="reduce_scatter_sibling",
    )(*arrays)


def _exchange_with_chips(q1, q2):
    arrays = (q1, q2)
    na = len(arrays)

    def body(*refs):
        srcs, dsts = refs[:na], refs[na:2 * na]
        send_sems, recv_sems = refs[2 * na:]
        x, y, c = _coords()
        copies = [
            pltpu.make_async_remote_copy(
                src_ref=srcs[a].at[2 * chip[0] + chip[1]], dst_ref=dsts[a].at[j],
                send_sem=send_sems.at[a * 3 + j], recv_sem=recv_sems.at[a * 3 + j],
                device_id=(*chip, c), device_id_type=MESH)
            for a in range(na) for j, chip in enumerate(_other_chips(x, y))]
        for cp in copies:
            cp.start()
        for cp in copies:
            cp.wait()

    return pl.pallas_call(
        body,
        in_specs=[ANY] * na,
        out_specs=[ANY] * na,
        out_shape=[jax.ShapeDtypeStruct((3,) + a.shape[1:], a.dtype) for a in arrays],
        scratch_shapes=[pltpu.SemaphoreType.DMA((3 * na,)), pltpu.SemaphoreType.DMA((3 * na,))],
        name="reduce_scatter_chips",
    )(*arrays)


def _pair_sum(p, ra, core, *, tr):
    R = p.shape[2]

    def body(core_ref, p_ref, ra_ref, o_ref):
        del core_ref
        o_ref[...] = p_ref[0] + ra_ref[...]

    return pl.pallas_call(
        body,
        grid_spec=pltpu.PrefetchScalarGridSpec(
            num_scalar_prefetch=1, grid=(4, R // tr),
            in_specs=[pl.BlockSpec((1, 1, tr, 1024), lambda k, i, cr: (k, cr[0], i, 0)),
                      pl.BlockSpec((1, tr, 1024), lambda k, i, cr: (k, i, 0))],
            out_specs=pl.BlockSpec((1, tr, 1024), lambda k, i, cr: (k, i, 0))),
        out_shape=jax.ShapeDtypeStruct(ra.shape, F32),
        name="pair_sum",
    )(core, p, ra)


def _chip_sum(q, rb, chip, *, tr):
    R = q.shape[1]

    def body(chip_ref, q_ref, rb_ref, o_ref):
        del chip_ref
        o_ref[...] = ((q_ref[0] + rb_ref[0]) + rb_ref[1]) + rb_ref[2]

    return pl.pallas_call(
        body,
        grid_spec=pltpu.PrefetchScalarGridSpec(
            num_scalar_prefetch=1, grid=(R // tr,),
            in_specs=[pl.BlockSpec((1, tr, 1024), lambda i, cr: (cr[0], i, 0)),
                      pl.BlockSpec((3, tr, 1024), lambda i, cr: (0, i, 0))],
            out_specs=pl.BlockSpec((tr, 1024), lambda i, cr: (i, 0))),
        out_shape=jax.ShapeDtypeStruct((R, 1024), F32),
        name="chip_sum",
    )(chip, q, rb)


def _small_all_reduce(packed, lb_logits):
    def body(p_ref, lbl_ref, o_ref, gath, send_sems, recv_sems):
        x, y, c = _coords()
        mine = 4 * x + 2 * y + c
        gath[mine] = p_ref[...]
        copies = []
        for r in range(1, N_DEV):
            peer = (x ^ (r >> 2), y ^ ((r >> 1) & 1), c ^ (r & 1))
            copies.append(pltpu.make_async_remote_copy(
                src_ref=p_ref, dst_ref=gath.at[mine],
                send_sem=send_sems.at[r - 1], recv_sem=recv_sems.at[r - 1],
                device_id=peer, device_id_type=MESH))
        for cp in copies:
            cp.start()
        for r in range(1, N_DEV):
            peer_slot = 4 * (x ^ (r >> 2)) + 2 * (y ^ ((r >> 1) & 1)) + (c ^ (r & 1))
            pltpu.make_async_remote_copy(
                src_ref=p_ref, dst_ref=gath.at[peer_slot],
                send_sem=send_sems.at[r - 1], recv_sem=recv_sems.at[r - 1],
                device_id=(x, y, c), device_id_type=MESH).wait_recv()
        for cp in copies:
            cp.wait_send()
        tot = gath[0]
        for d in range(1, N_DEV):
            tot = tot + gath[d]
        o_ref[...] = tot
        lb = _lower_bound(lbl_ref)
        dl0 = o_ref[SM_LB:SM_LB + 1, :] * lb * (1.0 - lb)
        o_ref[SM_LB:SM_LB + 1, :] = dl0
        o_ref[SM_LB + 1:SM_LB + 2, :] = -dl0

    vm = pl.BlockSpec(memory_space=pltpu.VMEM)
    return pl.pallas_call(
        body,
        in_specs=[vm, vm],
        out_specs=vm,
        out_shape=jax.ShapeDtypeStruct(packed.shape, F32),
        scratch_shapes=[pltpu.VMEM((N_DEV,) + packed.shape, F32),
                        pltpu.SemaphoreType.DMA((N_DEV - 1,)), pltpu.SemaphoreType.DMA((N_DEV - 1,))],
        name="small_all_reduce",
    )(packed, lb_logits)


def _adamw(w, g, m, v, *, tr, name):
    R, C = w.shape

    def body(w_ref, g_ref, m_ref, v_ref, d_ref, nm_ref, nv_ref):
        gv = g_ref[...]
        nm = ADAM_B1 * m_ref[...] + (1.0 - ADAM_B1) * gv
        nv = ADAM_B2 * v_ref[...] + (1.0 - ADAM_B2) * jnp.square(gv)
        m_hat = nm / (1.0 - ADAM_B1 ** ADAM_STEP)
        v_hat = nv / (1.0 - ADAM_B2 ** ADAM_STEP)
        d_ref[...] = -ADAM_LR * (m_hat / (jnp.sqrt(v_hat) + ADAM_EPS) + ADAM_WD * w_ref[...])
        nm_ref[...] = nm
        nv_ref[...] = nv

    spec = pl.BlockSpec((tr, C), lambda i: (i, 0))
    return pl.pallas_call(
        body,
        grid=(R // tr,),
        in_specs=[spec] * 4,
        out_specs=[spec] * 3,
        out_shape=[jax.ShapeDtypeStruct((R, C), F32)] * 3,
        compiler_params=_cparams("parallel"),
        name=name,
    )(w, g, m, v)


def _pack_small(lb, gain, sinks, rel_bias, ln1_g, ln1_b, ln2_g, ln2_b, loss=None):
    pad = lambda a: jnp.pad(a.reshape(1, -1), ((0, 0), (0, D_MODEL - a.size)))
    rows = [lb.reshape(-1, D_MODEL)]
    if rows[0].shape[0] == 1:
        rows.append(jnp.zeros((1, D_MODEL), F32))
    rows += [gain.reshape(1, D_MODEL), pad(sinks), pad(rel_bias), ln1_g.reshape(1, D_MODEL), ln1_b.reshape(1, D_MODEL),
             ln2_g.reshape(1, D_MODEL), ln2_b.reshape(1, D_MODEL),
             pad(jnp.zeros((1,), F32) if loss is None else loss.reshape(1))]
    rows.append(jnp.zeros((SM_ROWS - SM_LOSS - 1, D_MODEL), F32))
    return jnp.concatenate(rows, axis=0)


def _unpack_small(p):
    return dict(
        lb_logits=p[SM_LB:SM_LB + 2], hg_norm_gain=p[SM_GAIN:SM_GAIN + 1], swa_sinks=p[SM_SINK:SM_SINK + 1, :SWA_HEADS],
        rel_bias=p[SM_RB, :NUM_BUCKETS * SWA_HEADS].reshape(NUM_BUCKETS, SWA_HEADS),
        ln1_g=p[SM_L1G:SM_L1G + 1], ln1_b=p[SM_L1B:SM_L1B + 1], ln2_g=p[SM_L2G:SM_L2G + 1], ln2_b=p[SM_L2B:SM_L2B + 1])


_SMALL = ("lb_logits", "hg_norm_gain", "swa_sinks", "rel_bias", "ln1_g", "ln1_b", "ln2_g", "ln2_b")
_WEIGHTS = ("w_in", "lb_logits", "hg_norm_gain", "swa_sinks", "rel_bias", "w_mem_kv", "w_branch_hg", "w_branch_swa",
            "w_branch_mem", "w_out", "ln1_g", "ln1_b", "w_up", "w_down", "ln2_g", "ln2_b")


def kernel(x, mem, w_in, lb_logits, hg_norm_gain, swa_sinks, rel_bias, w_mem_kv, w_branch_hg, w_branch_swa, w_branch_mem, w_out, ln1_g, ln1_b, w_up, w_down, ln2_g, ln2_b, loss_target, m_w_in, m_lb_logits, m_hg_norm_gain, m_swa_sinks, m_rel_bias, m_w_mem_kv, m_w_branch_hg, m_w_branch_swa, m_w_branch_mem, m_w_out, m_ln1_g, m_ln1_b, m_w_up, m_w_down, m_ln2_g, m_ln2_b, v_w_in, v_lb_logits, v_hg_norm_gain, v_swa_sinks, v_rel_bias, v_w_mem_kv, v_w_branch_hg, v_w_branch_swa, v_w_branch_mem, v_w_out, v_ln1_g, v_ln1_b, v_w_up, v_w_down, v_ln2_g, v_ln2_b):
    w = dict(w_in=w_in, lb_logits=lb_logits, hg_norm_gain=hg_norm_gain, swa_sinks=swa_sinks, rel_bias=rel_bias,
             w_mem_kv=w_mem_kv, w_branch_hg=w_branch_hg, w_branch_swa=w_branch_swa, w_branch_mem=w_branch_mem,
             w_out=w_out, ln1_g=ln1_g, ln1_b=ln1_b, w_up=w_up, w_down=w_down, ln2_g=ln2_g, ln2_b=ln2_b)
    mom = dict(w_in=m_w_in, lb_logits=m_lb_logits, hg_norm_gain=m_hg_norm_gain, swa_sinks=m_swa_sinks, rel_bias=m_rel_bias,
               w_mem_kv=m_w_mem_kv, w_branch_hg=m_w_branch_hg, w_branch_swa=m_w_branch_swa, w_branch_mem=m_w_branch_mem,
               w_out=m_w_out, ln1_g=m_ln1_g, ln1_b=m_ln1_b, w_up=m_w_up, w_down=m_w_down, ln2_g=m_ln2_g, ln2_b=m_ln2_b)
    var = dict(w_in=v_w_in, lb_logits=v_lb_logits, hg_norm_gain=v_hg_norm_gain, swa_sinks=v_swa_sinks, rel_bias=v_rel_bias,
               w_mem_kv=v_w_mem_kv, w_branch_hg=v_w_branch_hg, w_branch_swa=v_w_branch_swa, w_branch_mem=v_w_branch_mem,
               w_out=v_w_out, ln1_g=v_ln1_g, ln1_b=v_ln1_b, w_up=v_w_up, w_down=v_w_down, ln2_g=v_ln2_g, ln2_b=v_ln2_b)
    xc, yc, cc = _coords()

    p1 = _bf(w_in[0].T)
    p2 = _bf(jnp.concatenate([w_down[0], w_up[0].T, w_branch_hg[0], w_branch_swa[0], w_branch_mem[0], w_out[0],
                              w_mem_kv[0].T], axis=0))
    g1, g2 = _all_gather_weights(p1, p2)
    full = lambda lo, hi: g2[:, lo:hi].reshape(N_DEV * (hi - lo), D_MODEL)
    grad_x, big, small = _local_step(
        x[0], mem[0], loss_target[0], lb_logits, hg_norm_gain, swa_sinks, rel_bias, ln1_g, ln1_b, ln2_g, ln2_b,
        g1.reshape(IN_COLS, D_MODEL), full(R_KV, R_OTHER), full(R_BH, R_BS), full(R_BS, R_BM), full(R_BM, R_OUT),
        full(R_OUT, R_KV), full(R_UP, R_BH), full(R_DN, R_UP))

    blocks = lambda a: a.reshape(N_DEV, a.shape[0] // N_DEV, D_MODEL)
    part1 = big["win_t"].reshape(4, 2, IN_SHARD, D_MODEL)
    part2 = jnp.concatenate([blocks(big[k]) for k in ("wdn", "wup_t", "wbh", "wbs", "wbm", "wout", "wkv_t")],
                            axis=1).reshape(4, 2, R_OTHER, D_MODEL)
    ra1, ra2 = _exchange_with_sibling(part1, part2)
    core = cc.reshape(1).astype(jnp.int32)
    q1 = _pair_sum(part1, ra1, core, tr=IN_SHARD // 2)
    q2 = _pair_sum(part2, ra2, core, tr=R_OTHER // 2)
    rb1, rb2 = _exchange_with_chips(q1, q2)
    chip = (2 * xc + yc).reshape(1).astype(jnp.int32)
    gs1 = _chip_sum(q1, rb1, chip, tr=IN_SHARD // 2)
    gs2 = _chip_sum(q2, rb2, chip, tr=R_OTHER // 2)

    grads = dict(
        w_in=gs1.T, w_down=gs2[R_DN:R_UP], w_up=gs2[R_UP:R_BH].T, w_branch_hg=gs2[R_BH:R_BS],
        w_branch_swa=gs2[R_BS:R_BM], w_branch_mem=gs2[R_BM:R_OUT], w_out=gs2[R_OUT:R_KV], w_mem_kv=gs2[R_KV:R_OTHER].T)

    packed = _pack_small(small["d_lb"], small["d_gain"], small["d_sink"], small["d_rb"], small["d_ln1_g"],
                         small["d_ln1_b"], small["d_ln2_g"], small["d_ln2_b"], small["loss"])
    reduced = _small_all_reduce(packed, lb_logits)
    loss = reduced[SM_LOSS, 0]
    grads.update(_unpack_small(reduced))

    delta, new_m, new_v = {}, {}, {}
    for name in _WEIGHTS:
        if name in _SMALL:
            continue
        w2 = w[name][0]
        delta[name], new_m[name], new_v[name] = _adamw(
            w2, grads[name], mom[name][0], var[name][0], tr=w2.shape[0] // 4, name="adamw_" + name)
    sm = lambda d: _pack_small(*[d[k] for k in _SMALL])
    d_s, m_s, v_s = _adamw(sm(w), reduced, sm(mom), sm(var), tr=SM_ROWS, name="adamw_small")
    for dst, src in ((delta, d_s), (new_m, m_s), (new_v, v_s)):
        dst.update(_unpack_small(src))

    def shaped(d, name):
        return d[name].reshape(w[name].shape)

    return (loss, grad_x[None], *[shaped(grads, n) for n in _WEIGHTS], *[shaped(delta, n) for n in _WEIGHTS],
            *[shaped(new_m, n) for n in _WEIGHTS], *[shaped(new_v, n) for n in _WEIGHTS])
```

```python
import functools
import math

import jax
import jax.numpy as jnp
from jax import lax
from jax.experimental import pallas as pl
from jax.experimental.pallas import tpu as pltpu

F32 = jnp.float32
BF16 = jnp.bfloat16

D_MODEL = 1024
MEM_LEN = 256
HG_HEADS = 8
HG_DK = 128
HG_CHUNK = 64
SWA_HEADS = 16
SWA_HEAD_DIM = 64
SWA_BLOCK = 128
SWA_WINDOW = 128
MEM_HEADS = 4
MEM_HEAD_DIM = 256
NUM_BUCKETS = 32
MAX_DISTANCE = 128
D_FF = 4096
LN_EPS = 1e-5
RMS_EPS = 1e-6
ALPHA = 2.0 ** 0.25
N_DEV = 8

C_HQ, C_HF, C_HI, C_HG, C_SQ, C_SK, C_SV, C_MQ, C_GL = 0, 1024, 2048, 3072, 4096, 5120, 5248, 5376, 6400
IN_COLS = 9472
IN_SHARD = IN_COLS // N_DEV

ADAM_LR = 0.001
ADAM_B1 = 0.9
ADAM_B2 = 0.999
ADAM_EPS = 1e-08
ADAM_WD = 0.01
ADAM_STEP = 10

VMEM_LIMIT = 56 * 1024 * 1024

R_DN, R_UP, R_BH, R_BS, R_BM, R_OUT, R_KV, R_OTHER = 0, 512, 1024, 1152, 1280, 1408, 1536, 1792

SM_LB, SM_GAIN, SM_SINK, SM_RB, SM_L1G, SM_L1B, SM_L2G, SM_L2B, SM_LOSS, SM_ROWS = 0, 2, 3, 4, 5, 6, 7, 8, 9, 16


def _bf(v):
    return v.astype(BF16)


def _dot(a, b):
    return jnp.dot(a, b, preferred_element_type=F32)


def _dot_nt(a, b):
    return lax.dot_general(a, b, (((1,), (1,)), ((), ())), preferred_element_type=F32)


def _dot_tn(a, b):
    return lax.dot_general(a, b, (((0,), (0,)), ((), ())), preferred_element_type=F32)


def _sig(v):
    return 1.0 / (1.0 + jnp.exp(-v))


def _cparams(*sem):
    return pltpu.CompilerParams(dimension_semantics=sem, vmem_limit_bytes=VMEM_LIMIT)


def _const_spec(shape):
    nd = len(shape)
    return pl.BlockSpec(shape, lambda *_: (0,) * nd, pipeline_mode=pl.Buffered(1))


def _mm_nt(a, bt, *, tm, tn, out_dtype, name):
    M, K = a.shape
    N = bt.shape[0]

    def body(a_ref, b_ref, o_ref):
        o_ref[...] = _dot_nt(_bf(a_ref[...]), _bf(b_ref[...])).astype(o_ref.dtype)

    return pl.pallas_call(
        body,
        grid=(N // tn, M // tm),
        in_specs=[pl.BlockSpec((tm, K), lambda j, i: (i, 0)), pl.BlockSpec((tn, K), lambda j, i: (j, 0))],
        out_specs=pl.BlockSpec((tm, tn), lambda j, i: (i, j)),
        out_shape=jax.ShapeDtypeStruct((M, N), out_dtype),
        compiler_params=_cparams("parallel", "parallel"),
        name=name,
    )(a, bt)


def _mm_tn_resident(a, b, *, tm, kc, name):
    K, M = a.shape
    N = b.shape[1]
    nk = K // kc

    def body(a_ref, b_ref, o_ref):
        acc = jnp.zeros((tm, N), F32)
        for kk in range(nk):
            sl = pl.ds(kk * kc, kc)
            acc = acc + _dot_tn(_bf(a_ref[sl, :]), _bf(b_ref[sl, :]))
        o_ref[...] = acc

    return pl.pallas_call(
        body,
        grid=(M // tm,),
        in_specs=[pl.BlockSpec((K, tm), lambda i: (0, i)), _const_spec((K, N))],
        out_specs=pl.BlockSpec((tm, N), lambda i: (i, 0)),
        out_shape=jax.ShapeDtypeStruct((M, N), F32),
        compiler_params=_cparams("parallel"),
        name=name,
    )(a, b)


def _mm_nn_resident(a, b, add, *, tm, kc, name):
    M, K = a.shape
    N = b.shape[1]
    nk = K // kc

    def body(a_ref, b_ref, add_ref, o_ref):
        acc = add_ref[...]
        for kk in range(nk):
            sl = pl.ds(kk * kc, kc)
            acc = acc + _dot(_bf(a_ref[:, sl]), b_ref[sl, :])
        o_ref[...] = acc

    return pl.pallas_call(
        body,
        grid=(M // tm,),
        in_specs=[pl.BlockSpec((tm, K), lambda i: (i, 0)), _const_spec((K, N)), pl.BlockSpec((tm, N), lambda i: (i, 0))],
        out_specs=pl.BlockSpec((tm, N), lambda i: (i, 0)),
        out_shape=jax.ShapeDtypeStruct((M, N), F32),
        compiler_params=_cparams("parallel"),
        name=name,
    )(a, b, add)


def _lower_bound(lbl_ref):
    l0 = lbl_ref[0:1, :]
    l1 = lbl_ref[1:2, :]
    mx = jnp.maximum(l0, l1)
    e0 = jnp.exp(l0 - mx)
    e1 = jnp.exp(l1 - mx)
    return e0 / (e0 + e1)


def _tri(lower):
    r = lax.broadcasted_iota(jnp.int32, (HG_CHUNK, HG_CHUNK), 0)
    c = lax.broadcasted_iota(jnp.int32, (HG_CHUNK, HG_CHUNK), 1)
    return (r >= c) if lower else (r <= c)


def _hg_gates(fl, lb):
    sg = _sig(fl)
    f = lb + (1.0 - lb) * sg
    return sg, f, jnp.log(f), 1.0 - f


def _scan_rows(v, reverse=False):
    row = lax.broadcasted_iota(jnp.int32, v.shape, 0)
    s = 1
    while s < HG_CHUNK:
        if reverse:
            v = v + jnp.where(row < HG_CHUNK - s, pltpu.roll(v, HG_CHUNK - s, 0), 0.0)
        else:
            v = v + jnp.where(row >= s, pltpu.roll(v, s, 0), 0.0)
        s *= 2
    return v


def _hgrn_fwd(zmain, lb_logits, *, T):
    S = zmain.shape[0]
    nc = T // HG_CHUNK

    def body(q_ref, f_ref, v_ref, lbl_ref, o_ref, st_ref, state):
        @pl.when(pl.program_id(1) == 0)
        def _():
            state[...] = jnp.zeros_like(state)

        lb = _lower_bound(lbl_ref)
        tril = _tri(True)
        qis, updates, decays, intra = [], [], [], []
        for c in range(nc):
            sl = pl.ds(c * HG_CHUNK, HG_CHUNK)
            _, _, g, k = _hg_gates(f_ref[sl, :], lb)
            b = _scan_rows(g)
            bl = jnp.sum(g, axis=0, keepdims=True)
            qi = _bf(q_ref[sl, :] * jnp.exp(b))
            ki = _bf(k * jnp.exp(-b))
            ko = _bf(k * jnp.exp(bl - b))
            vb = _bf(v_ref[sl, :])
            att = jnp.where(tril, _dot_nt(qi, ki), 0.0)
            intra.append(_dot(_bf(att), vb))
            qis.append(qi)
            updates.append(_dot_tn(vb, ko))
            decays.append(jnp.exp(bl))
        st = state[...]
        for c in range(nc):
            st_ref[0, c] = st
            o_ref[pl.ds(c * HG_CHUNK, HG_CHUNK), :] = intra[c] + _dot_nt(qis[c], _bf(st))
            st = st * decays[c] + updates[c]
        state[...] = st

    col = lambda base: pl.BlockSpec((T, HG_DK), lambda h, t: (t, base + h))
    return pl.pallas_call(
        body,
        grid=(HG_HEADS, S // T),
        in_specs=[col(0), col(8), col(16), pl.BlockSpec((2, HG_DK), lambda h, t: (0, h))],
        out_specs=[
            pl.BlockSpec((T, HG_DK), lambda h, t: (t, h)),
            pl.BlockSpec((1, nc, HG_DK, HG_DK), lambda h, t: (h, t, 0, 0)),
        ],
        out_shape=[
            jax.ShapeDtypeStruct((S, D_MODEL), F32),
            jax.ShapeDtypeStruct((HG_HEADS, S // HG_CHUNK, HG_DK, HG_DK), F32),
        ],
        scratch_shapes=[pltpu.VMEM((HG_DK, HG_DK), F32)],
        compiler_params=_cparams("parallel", "arbitrary"),
        name="hgrn_fwd",
    )(zmain, zmain, zmain, lb_logits)


def _hgrn_bwd(zmain, lb_logits, states, d_o, *, T):
    S = zmain.shape[0]
    nc = T // HG_CHUNK
    nt = S // T

    def body(q_ref, f_ref, v_ref, lbl_ref, st_ref, do_ref, dq_ref, df_ref, dv_ref, dlb_ref, dstate):
        @pl.when(pl.program_id(1) == 0)
        def _():
            dstate[...] = jnp.zeros_like(dstate)
            dlb_ref[...] = jnp.zeros_like(dlb_ref)

        lb = _lower_bound(lbl_ref)
        tril = _tri(True)
        last_row = lax.broadcasted_iota(jnp.int32, (HG_CHUNK, HG_DK), 0) == HG_CHUNK - 1
        saved = []
        for c in range(nc):
            sl = pl.ds(c * HG_CHUNK, HG_CHUNK)
            sg, f, g, k = _hg_gates(f_ref[sl, :], lb)
            b = _scan_rows(g)
            bl = jnp.sum(g, axis=0, keepdims=True)
            eb = jnp.exp(b)
            enb = jnp.exp(-b)
            eo = jnp.exp(bl - b)
            q_in = q_ref[sl, :] * eb
            k_in = k * enb
            k_out = k * eo
            qi, ki, ko = _bf(q_in), _bf(k_in), _bf(k_out)
            vb = _bf(v_ref[sl, :])
            dob = do_ref[sl, :]
            att = jnp.where(tril, _dot_nt(qi, ki), 0.0)
            d_att = _bf(jnp.where(tril, _dot_nt(dob, vb), 0.0))
            d_kin = _dot_tn(d_att, qi)
            saved.append(dict(
                sg=sg, f=f, eb=eb, enb=enb, eo=eo, ebl=jnp.exp(bl), k_out=k_out, ko=ko, vb=vb, dob=dob,
                d_v=_dot_tn(_bf(att), dob), d_qin=_dot(d_att, ki), d_kin=d_kin,
                qk=(q_in, k_in), d_state=_dot_tn(dob, qi)))
        dst = dstate[...]
        dsts = [None] * nc
        for c in reversed(range(nc)):
            dsts[c] = dst
            dst = dst * saved[c]["ebl"] + saved[c]["d_state"]
        dstate[...] = dst
        dlb = jnp.zeros((1, HG_DK), F32)
        for c in range(nc):
            sl = pl.ds(c * HG_CHUNK, HG_CHUNK)
            s = saved[c]
            q_in, k_in = s["qk"]
            st = st_ref[0, c]
            dstb = _bf(dsts[c])
            d_v = s["d_v"] + _dot_nt(s["ko"], dstb)
            d_qin = s["d_qin"] + _dot(s["dob"], _bf(st))
            d_kout = _dot(s["vb"], dstb)
            d_decay = jnp.sum(dsts[c] * st, axis=0, keepdims=True)
            kk = d_kout * s["k_out"]
            d_b = d_qin * q_in - s["d_kin"] * k_in - kk
            d_bl = jnp.sum(kk, axis=0, keepdims=True) + d_decay * s["ebl"]
            d_g = _scan_rows(d_b + jnp.where(last_row, d_bl, 0.0), reverse=True)
            d_f = d_g / s["f"] - (s["d_kin"] * s["enb"] + d_kout * s["eo"])
            dq_ref[sl, :] = _bf(d_qin * s["eb"])
            df_ref[sl, :] = _bf(d_f * (1.0 - lb) * s["sg"] * (1.0 - s["sg"]))
            dv_ref[sl, :] = _bf(d_v)
            dlb = dlb + jnp.sum(d_f * (1.0 - s["sg"]), axis=0, keepdims=True)
        dlb_ref[...] += dlb

    rev = lambda base: pl.BlockSpec((T, HG_DK), lambda h, t: (nt - 1 - t, base + h))
    outc = pl.BlockSpec((T, HG_DK), lambda h, t: (nt - 1 - t, h))
    return pl.pallas_call(
        body,
        grid=(HG_HEADS, nt),
        in_specs=[
            rev(0), rev(8), rev(16),
            pl.BlockSpec((2, HG_DK), lambda h, t: (0, h)),
            pl.BlockSpec((1, nc, HG_DK, HG_DK), lambda h, t: (h, nt - 1 - t, 0, 0)),
            outc,
        ],
        out_specs=[outc, outc, outc, pl.BlockSpec((1, HG_DK), lambda h, t: (0, h))],
        out_shape=[jax.ShapeDtypeStruct((S, D_MODEL), BF16)] * 3 + [jax.ShapeDtypeStruct((1, D_MODEL), F32)],
        scratch_shapes=[pltpu.VMEM((HG_DK, HG_DK), F32)],
        compiler_params=_cparams("parallel", "arbitrary"),
        name="hgrn_bwd",
    )(zmain, zmain, zmain, lb_logits, states, d_o)


def _t5_bucket_table():
    qi = jnp.arange(SWA_BLOCK)[:, None] + SWA_BLOCK
    kj = jnp.arange(2 * SWA_BLOCK)[None, :]
    n = jnp.clip(qi - kj, 0, SWA_WINDOW - 1)
    max_exact = NUM_BUCKETS // 2
    nf = jnp.maximum(n, 1).astype(F32)
    large = max_exact + (jnp.log(nf / max_exact) / math.log(MAX_DISTANCE / max_exact)
                         * (NUM_BUCKETS - max_exact)).astype(jnp.int32)
    large = jnp.minimum(large, NUM_BUCKETS - 1)
    return jnp.where(n < max_exact, n, large).astype(jnp.int32)


def _swa_valid(n):
    qi = lax.broadcasted_iota(jnp.int32, (SWA_BLOCK, 2 * SWA_BLOCK), 0) + SWA_BLOCK
    kj = lax.broadcasted_iota(jnp.int32, (SWA_BLOCK, 2 * SWA_BLOCK), 1)
    dist = qi - kj
    return (dist >= 0) & (dist < SWA_WINDOW) & ((n > 0) | (kj >= SWA_BLOCK))


def _swa_bias_init(bias, bucket_ref, rb_ref):
    bk = bucket_ref[...]
    for h in range(SWA_HEADS):
        def sel(b, acc, h=h):
            return jnp.where(bk == b, rb_ref[b, h], acc)
        bias[h] = lax.fori_loop(0, NUM_BUCKETS, sel, jnp.zeros(bk.shape, F32))


def _lane_halves(t, kv_head):
    lane = lax.broadcasted_iota(jnp.int32, t.shape, 1)
    rolled = pltpu.roll(t, 64, 1)
    zero = jnp.zeros_like(t)
    if kv_head == 0:
        return jnp.where(lane < 64, t, zero), jnp.where(lane >= 64, rolled, zero)
    return jnp.where(lane < 64, rolled, zero), jnp.where(lane >= 64, t, zero)


def _swa_probs(s, bias_h, valid, sink):
    s = jnp.where(valid, s + bias_h, -jnp.inf)
    m = jnp.maximum(jnp.max(s, axis=-1, keepdims=True), sink)
    p = jnp.exp(s - m)
    es = jnp.exp(sink - m)
    inv = 1.0 / (jnp.sum(p, axis=-1, keepdims=True) + es)
    return p * inv, es * inv


def _swa_fwd(zmain, bucket, rel_bias, sinks):
    S = zmain.shape[0]
    nb = S // SWA_BLOCK
    scale = SWA_HEAD_DIM ** -0.5

    def body(q_ref, kvc_ref, kvp_ref, bucket_ref, rb_ref, sk_ref, o_ref, bias):
        n = pl.program_id(0)

        @pl.when(n == 0)
        def _():
            _swa_bias_init(bias, bucket_ref, rb_ref)

        valid = _swa_valid(n)
        kk = _bf(jnp.concatenate([kvp_ref[:, 0:128], kvc_ref[:, 0:128]], axis=0))
        vv = _bf(jnp.concatenate([kvp_ref[:, 128:256], kvc_ref[:, 128:256]], axis=0))
        for kvh in range(2):
            ka, kb = _lane_halves(kk, kvh)
            va, vb = _lane_halves(vv, kvh)
            qst = _bf(jnp.concatenate([q_ref[:, pl.ds((kvh * 4 + jj) * 128, 128)] for jj in range(4)], axis=0) * scale)
            probs = []
            for odd, kx in enumerate((ka, kb)):
                s = _dot_nt(qst, kx)
                parts = []
                for jj in range(4):
                    h = 2 * (kvh * 4 + jj) + odd
                    p, _ = _swa_probs(s[jj * SWA_BLOCK:(jj + 1) * SWA_BLOCK], bias[h], valid, sk_ref[0, h])
                    parts.append(_bf(p))
                probs.append(jnp.concatenate(parts, axis=0))
            ost = _dot(probs[0], va) + _dot(probs[1], vb)
            for jj in range(4):
                o_ref[:, pl.ds((kvh * 4 + jj) * 128, 128)] = ost[jj * SWA_BLOCK:(jj + 1) * SWA_BLOCK]

    smem = pl.BlockSpec(memory_space=pltpu.SMEM)
    return pl.pallas_call(
        body,
        grid=(nb,),
        in_specs=[
            pl.BlockSpec((SWA_BLOCK, 1024), lambda n: (n, C_SQ // 1024)),
            pl.BlockSpec((SWA_BLOCK, 256), lambda n: (n, C_SK // 256)),
            pl.BlockSpec((SWA_BLOCK, 256), lambda n: (jnp.maximum(n - 1, 0), C_SK // 256)),
            _const_spec((SWA_BLOCK, 2 * SWA_BLOCK)), smem, smem,
        ],
        out_specs=pl.BlockSpec((SWA_BLOCK, 1024), lambda n: (n, 0)),
        out_shape=jax.ShapeDtypeStruct((S, 1024), F32),
        scratch_shapes=[pltpu.VMEM((SWA_HEADS, SWA_BLOCK, 2 * SWA_BLOCK), F32)],
        compiler_params=_cparams("arbitrary"),
        name="swa_fwd",
    )(zmain, zmain, zmain, bucket, rel_bias, sinks)


def _swa_bwd(zmain, o_b, d_o, bucket, rel_bias, sinks):
    S = zmain.shape[0]
    nb = S // SWA_BLOCK
    scale = SWA_HEAD_DIM ** -0.5

    def body(q_ref, kvc_ref, kvp_ref, o_ref, do_ref, bucket_ref, rb_ref, sk_ref,
             dq_ref, dkv_ref, drb_ref, dsk_ref, bias, dbias, carry):
        n = pl.program_id(0)

        @pl.when(n == 0)
        def _():
            _swa_bias_init(bias, bucket_ref, rb_ref)
            dbias[...] = jnp.zeros_like(dbias)
            carry[...] = jnp.zeros_like(carry)
            dsk_ref[...] = jnp.zeros_like(dsk_ref)

        @pl.when(n < nb)
        def _():
            valid = _swa_valid(n)
            kk = _bf(jnp.concatenate([kvp_ref[:, 0:128], kvc_ref[:, 0:128]], axis=0))
            vv = _bf(jnp.concatenate([kvp_ref[:, 128:256], kvc_ref[:, 128:256]], axis=0))
            lane = lax.broadcasted_iota(jnp.int32, (2 * SWA_BLOCK, 128), 1)
            lane_q = lax.broadcasted_iota(jnp.int32, (4 * SWA_BLOCK, 128), 1)
            dk_parts, dv_parts = [], []
            for kvh in range(2):
                ka, kb = _lane_halves(kk, kvh)
                va, vb = _lane_halves(vv, kvh)
                pair_cols = [pl.ds((kvh * 4 + jj) * 128, 128) for jj in range(4)]
                qst = _bf(jnp.concatenate([q_ref[:, cl] for cl in pair_cols], axis=0) * scale)
                dost = jnp.concatenate([do_ref[:, cl] for cl in pair_cols], axis=0)
                prod = dost.astype(F32) * jnp.concatenate([o_ref[:, cl] for cl in pair_cols], axis=0)
                dq_st = jnp.zeros((4 * SWA_BLOCK, 128), F32)
                zks, zvs = [], []
                for odd, (kx, vx) in enumerate(((ka, va), (kb, vb))):
                    s = _dot_nt(qst, kx)
                    keep = (lane_q >= 64) if odd else (lane_q < 64)
                    delta = jnp.sum(jnp.where(keep, prod, 0.0), axis=-1, keepdims=True)
                    dp = _dot_nt(dost, vx)
                    p_parts, ds_parts = [], []
                    for jj in range(4):
                        h = 2 * (kvh * 4 + jj) + odd
                        rows = slice(jj * SWA_BLOCK, (jj + 1) * SWA_BLOCK)
                        p, ps = _swa_probs(s[rows], bias[h], valid, sk_ref[0, h])
                        ds = p * (dp[rows] - delta[rows])
                        dbias[h] += ds
                        dsk_ref[h] += jnp.broadcast_to(-jnp.sum(ps * delta[rows], axis=0, keepdims=True), (8, 128))
                        p_parts.append(_bf(p))
                        ds_parts.append(_bf(ds))
                    pst = jnp.concatenate(p_parts, axis=0)
                    dsst = jnp.concatenate(ds_parts, axis=0)
                    dq_st = dq_st + _dot(dsst, kx)
                    zks.append(_dot_tn(dsst, qst))
                    zvs.append(_dot_tn(pst, dost))
                for jj in range(4):
                    dq_ref[:, pair_cols[jj]] = _bf(dq_st[jj * SWA_BLOCK:(jj + 1) * SWA_BLOCK] * scale)
                zk = jnp.where(lane < 64, zks[0], zks[1])
                zv = jnp.where(lane < 64, zvs[0], zvs[1])
                dk_parts.append(zk + pltpu.roll(zk, 64, 1))
                dv_parts.append(zv + pltpu.roll(zv, 64, 1))
            dk = jnp.where(lane < 64, dk_parts[0], dk_parts[1])
            dv = jnp.where(lane < 64, dv_parts[0], dv_parts[1])
            dkv = jnp.concatenate([dk, dv], axis=1)
            dkv_ref[...] = _bf(carry[...] + dkv[0:SWA_BLOCK])
            carry[...] = dkv[SWA_BLOCK:]

        @pl.when(n == nb)
        def _():
            dkv_ref[...] = _bf(carry[...])
            bk = bucket_ref[...]

            def per_head(h, _):
                db = dbias[h]

                def per_bucket(b, _):
                    tot = jnp.sum(jnp.where(bk == b, db, 0.0), axis=1, keepdims=True)
                    tot = jnp.sum(tot, axis=0, keepdims=True)
                    drb_ref[h * NUM_BUCKETS + b] = jnp.broadcast_to(tot, (8, 128))
                    return 0

                return lax.fori_loop(0, NUM_BUCKETS, per_bucket, 0)

            lax.fori_loop(0, SWA_HEADS, per_head, 0)

    smem = pl.BlockSpec(memory_space=pltpu.SMEM)
    cur = lambda n: jnp.minimum(n, nb - 1)
    prev = lambda n: jnp.maximum(jnp.minimum(n, nb - 1) - 1, 0)
    return pl.pallas_call(
        body,
        grid=(nb + 1,),
        in_specs=[
            pl.BlockSpec((SWA_BLOCK, 1024), lambda n: (cur(n), C_SQ // 1024)),
            pl.BlockSpec((SWA_BLOCK, 256), lambda n: (cur(n), C_SK // 256)),
            pl.BlockSpec((SWA_BLOCK, 256), lambda n: (prev(n), C_SK // 256)),
            pl.BlockSpec((SWA_BLOCK, 1024), lambda n: (cur(n), 0)),
            pl.BlockSpec((SWA_BLOCK, 1024), lambda n: (cur(n), 0)),
            _const_spec((SWA_BLOCK, 2 * SWA_BLOCK)), smem, smem,
        ],
        out_specs=[
            pl.BlockSpec((SWA_BLOCK, 1024), lambda n: (cur(n), 0)),
            pl.BlockSpec((SWA_BLOCK, 256), lambda n: (jnp.maximum(n - 1, 0), 0)),
            pl.BlockSpec((SWA_HEADS * NUM_BUCKETS, 8, 128), lambda n: (0, 0, 0)),
            pl.BlockSpec((SWA_HEADS, 8, 128), lambda n: (0, 0, 0)),
        ],
        out_shape=[
            jax.ShapeDtypeStruct((S, 1024), BF16),
            jax.ShapeDtypeStruct((S, 256), BF16),
            jax.ShapeDtypeStruct((SWA_HEADS * NUM_BUCKETS, 8, 128), F32),
            jax.ShapeDtypeStruct((SWA_HEADS, 8, 128), F32),
        ],
        scratch_shapes=[
            pltpu.VMEM((SWA_HEADS, SWA_BLOCK, 2 * SWA_BLOCK), F32),
            pltpu.VMEM((SWA_HEADS, SWA_BLOCK, 2 * SWA_BLOCK), F32),
            pltpu.VMEM((SWA_BLOCK, 256), F32),
        ],
        compiler_params=_cparams("arbitrary"),
        name="swa_bwd",
    )(zmain, zmain, zmain, o_b, d_o, bucket, rel_bias, sinks)


def _mem_probs(q_ref, k_ref):
    qs = _bf(q_ref[...] * (MEM_HEAD_DIM ** -0.5))
    s = _dot_nt(qs, _bf(k_ref[...]))
    e = jnp.exp(s - jnp.max(s, axis=-1, keepdims=True))
    return qs, e / jnp.sum(e, axis=-1, keepdims=True)


def _mem_fwd(zmain, mkv, *, T):
    S = zmain.shape[0]

    def body(q_ref, k_ref, v_ref, o_ref):
        _, p = _mem_probs(q_ref, k_ref)
        o_ref[...] = _dot(_bf(p), _bf(v_ref[...]))

    return pl.pallas_call(
        body,
        grid=(MEM_HEADS, S // T),
        in_specs=[
            pl.BlockSpec((T, MEM_HEAD_DIM), lambda h, t: (t, C_MQ // MEM_HEAD_DIM + h)),
            pl.BlockSpec((MEM_LEN, MEM_HEAD_DIM), lambda h, t: (0, h)),
            pl.BlockSpec((MEM_LEN, MEM_HEAD_DIM), lambda h, t: (0, MEM_HEADS + h)),
        ],
        out_specs=pl.BlockSpec((T, MEM_HEAD_DIM), lambda h, t: (t, h)),
        out_shape=jax.ShapeDtypeStruct((S, 1024), F32),
        compiler_params=_cparams("parallel", "parallel"),
        name="mem_fwd",
    )(zmain, mkv, mkv)


def _mem_bwd(zmain, mkv, o_c, d_o, *, T):
    S = zmain.shape[0]
    scale = MEM_HEAD_DIM ** -0.5

    def body(q_ref, k_ref, v_ref, o_ref, do_ref, dq_ref, dk_ref, dv_ref):
        @pl.when(pl.program_id(1) == 0)
        def _():
            dk_ref[...] = jnp.zeros_like(dk_ref)
            dv_ref[...] = jnp.zeros_like(dv_ref)

        qs, p = _mem_probs(q_ref, k_ref)
        dob = do_ref[...]
        delta = jnp.sum(dob.astype(F32) * o_ref[...], axis=-1, keepdims=True)
        ds = _bf(p * (_dot_nt(dob, _bf(v_ref[...])) - delta))
        dq_ref[...] = _bf(_dot(ds, _bf(k_ref[...])) * scale)
        dk_ref[...] += _dot_tn(ds, qs)
        dv_ref[...] += _dot_tn(_bf(p), dob)

    tile = lambda base: pl.BlockSpec((T, MEM_HEAD_DIM), lambda h, t: (t, base + h))
    return pl.pallas_call(
        body,
        grid=(MEM_HEADS, S // T),
        in_specs=[
            tile(C_MQ // MEM_HEAD_DIM),
            pl.BlockSpec((MEM_LEN, MEM_HEAD_DIM), lambda h, t: (0, h)),
            pl.BlockSpec((MEM_LEN, MEM_HEAD_DIM), lambda h, t: (0, MEM_HEADS + h)),
            tile(0), tile(0),
        ],
        out_specs=[
            tile(0),
            pl.BlockSpec((MEM_LEN, MEM_HEAD_DIM), lambda h, t: (0, h)),
            pl.BlockSpec((MEM_LEN, MEM_HEAD_DIM), lambda h, t: (0, h)),
        ],
        out_shape=[jax.ShapeDtypeStruct((S, 1024), BF16)] + [jax.ShapeDtypeStruct((MEM_LEN, 1024), F32)] * 2,
        compiler_params=_cparams("parallel", "arbitrary"),
        name="mem_bwd",
    )(zmain, mkv, mkv, o_c, d_o)


def _layer_norm(u):
    mu = jnp.mean(u, axis=-1, keepdims=True)
    xc = u - mu
    rstd = lax.rsqrt(jnp.mean(xc * xc, axis=-1, keepdims=True) + LN_EPS)
    return xc * rstd, rstd


def _layer_norm_bwd(dy, gamma, xhat, rstd):
    dxh = dy * gamma
    return rstd * (dxh - jnp.mean(dxh, axis=-1, keepdims=True) - xhat * jnp.mean(dxh * xhat, axis=-1, keepdims=True))


def _merge_forward(oraw_ref, hg_ref, ob_ref, oc_ref, gl_ref, x_ref, gain_ref, wbh, wbs, wbm, wout):
    ys, rs = [], []
    for h in range(HG_HEADS):
        oh = oraw_ref[:, pl.ds(h * HG_DK, HG_DK)]
        r = lax.rsqrt(jnp.mean(oh * oh, axis=-1, keepdims=True) + RMS_EPS)
        ys.append(oh * r)
        rs.append(r)
    y = jnp.concatenate(ys, axis=1)
    hg = hg_ref[...]
    sg = _sig(hg)
    silu = hg * sg
    oa = _bf(y * gain_ref[...] * silu)
    pa = _dot(oa, wbh[...])
    pb = _dot(_bf(ob_ref[...]), wbs[...])
    pc = _dot(_bf(oc_ref[...]), wbm[...])
    g0 = _sig(gl_ref[:, 0:1024])
    g1 = _sig(gl_ref[:, 1024:2048])
    g2 = _sig(gl_ref[:, 2048:3072])
    m = _bf(g0 * pa + g1 * pb + g2 * pc)
    u1 = ALPHA * x_ref[...] + _dot(m, wout[...])
    xhat, rstd = _layer_norm(u1)
    return dict(y=y, rs=rs, hg=hg, sg=sg, silu=silu, oa=oa, pa=pa, pb=pb, pc=pc,
                g0=g0, g1=g1, g2=g2, m=m, xhat=xhat, rstd=rstd)


def _merge_in_specs(T):
    row = lambda w, c=0: pl.BlockSpec((T, w), lambda i: (i, c))
    vec = pl.BlockSpec((1, D_MODEL), lambda i: (0, 0))
    w = _const_spec((D_MODEL, D_MODEL))
    return [row(1024), row(1024, C_HG // 1024), row(1024), row(1024), row(3072), row(1024), vec, w, w, w, w, vec, vec]


def _merge_fwd(o_raw, zmain, o_b, o_c, gl, x, gain, wbh, wbs, wbm, wout, ln_g, ln_b, *, T):
    S = x.shape[0]

    def body(oraw_ref, hg_ref, ob_ref, oc_ref, gl_ref, x_ref, gain_ref, wbh_r, wbs_r, wbm_r, wout_r, g_ref, b_ref, h1_ref):
        f = _merge_forward(oraw_ref, hg_ref, ob_ref, oc_ref, gl_ref, x_ref, gain_ref, wbh_r, wbs_r, wbm_r, wout_r)
        h1_ref[...] = f["xhat"] * g_ref[...] + b_ref[...]

    return pl.pallas_call(
        body,
        grid=(S // T,),
        in_specs=_merge_in_specs(T),
        out_specs=pl.BlockSpec((T, D_MODEL), lambda i: (i, 0)),
        out_shape=jax.ShapeDtypeStruct((S, D_MODEL), F32),
        compiler_params=_cparams("parallel"),
        name="merge_fwd",
    )(o_raw, zmain, o_b, o_c, gl, x, gain, wbh, wbs, wbm, wout, ln_g, ln_b)


def _merge_bwd(d_h1, o_raw, zmain, o_b, o_c, gl, x, gain, wbh, wbs, wbm, wout, ln_g, ln_b, *, T):
    S = x.shape[0]

    def body(dh1_ref, oraw_ref, hg_ref, ob_ref, oc_ref, gl_ref, x_ref, gain_ref, wbh_r, wbs_r, wbm_r, wout_r, g_ref, b_ref,
             dx_ref, du1_ref, m_ref, oa_ref, dpa_ref, dpb_ref, dpc_ref, dhg_ref, doraw_ref, dob_ref, doc_ref, dgl_ref,
             dgain_ref, dg_ref, db_ref):
        del b_ref

        @pl.when(pl.program_id(0) == 0)
        def _():
            dgain_ref[...] = jnp.zeros_like(dgain_ref)
            dg_ref[...] = jnp.zeros_like(dg_ref)
            db_ref[...] = jnp.zeros_like(db_ref)

        f = _merge_forward(oraw_ref, hg_ref, ob_ref, oc_ref, gl_ref, x_ref, gain_ref, wbh_r, wbs_r, wbm_r, wout_r)
        dh1 = dh1_ref[...]
        dg_ref[...] += jnp.sum(dh1 * f["xhat"], axis=0, keepdims=True)
        db_ref[...] += jnp.sum(dh1, axis=0, keepdims=True)
        du1 = _layer_norm_bwd(dh1, g_ref[...], f["xhat"], f["rstd"])
        dx_ref[...] = ALPHA * du1
        du1b = _bf(du1)
        du1_ref[...] = du1b
        m_ref[...] = f["m"]
        oa_ref[...] = f["oa"]
        dm = _dot_nt(du1b, wout_r[...])
        for i, (g, p, dp_ref, dob_r, w_r) in enumerate((
                (f["g0"], f["pa"], dpa_ref, None, wbh_r),
                (f["g1"], f["pb"], dpb_ref, dob_ref, wbs_r),
                (f["g2"], f["pc"], dpc_ref, doc_ref, wbm_r))):
            dgl_ref[:, pl.ds(i * 1024, 1024)] = _bf(dm * p * g * (1.0 - g))
            dp = _bf(dm * g)
            dp_ref[...] = dp
            d_branch = _dot_nt(dp, w_r[...])
            if dob_r is not None:
                dob_r[...] = _bf(d_branch)
            else:
                doa = d_branch
        gain = gain_ref[...]
        t = doa * f["y"]
        dgain_ref[...] += jnp.sum(t * f["silu"], axis=0, keepdims=True)
        sg = f["sg"]
        dhg_ref[...] = _bf(t * gain * sg * (1.0 + f["hg"] * (1.0 - sg)))
        dy = doa * gain * f["silu"]
        for h in range(HG_HEADS):
            cols = slice(h * HG_DK, (h + 1) * HG_DK)
            yh = f["y"][:, cols]
            dyh = dy[:, cols]
            doraw_ref[:, pl.ds(h * HG_DK, HG_DK)] = _bf(
                f["rs"][h] * (dyh - yh * jnp.mean(dyh * yh, axis=-1, keepdims=True)))

    row = lambda w: pl.BlockSpec((T, w), lambda i: (i, 0))
    vec = pl.BlockSpec((1, D_MODEL), lambda i: (0, 0))
    bshape = jax.ShapeDtypeStruct((S, D_MODEL), BF16)
    vshape = jax.ShapeDtypeStruct((1, D_MODEL), F32)
    return pl.pallas_call(
        body,
        grid=(S // T,),
        in_specs=[row(1024)] + _merge_in_specs(T),
        out_specs=[row(1024)] * 11 + [row(3072), vec, vec, vec],
        out_shape=[jax.ShapeDtypeStruct((S, D_MODEL), F32)] + [bshape] * 10
        + [jax.ShapeDtypeStruct((S, 3072), BF16), vshape, vshape, vshape],
        compiler_params=_cparams("arbitrary"),
        name="merge_bwd",
    )(d_h1, o_raw, zmain, o_b, o_c, gl, x, gain, wbh, wbs, wbm, wout, ln_g, ln_b)


def _mlp_fwd_bwd(h1, target, wup_t, wdn, ln_g, ln_b, *, T, FC):
    S = h1.shape[0]
    nf = D_FF // FC

    def body(h1_ref, t_ref, wup_ref, wdn_ref, g_ref, b_ref, dh1_ref, a_ref, dup_ref, du2_ref, loss_ref, dg_ref, db_ref, up_scr):
        @pl.when(pl.program_id(0) == 0)
        def _():
            loss_ref[...] = jnp.zeros_like(loss_ref)
            dg_ref[...] = jnp.zeros_like(dg_ref)
            db_ref[...] = jnp.zeros_like(db_ref)

        h1v = h1_ref[...]
        h1b = _bf(h1v)
        ff = jnp.zeros((T, D_MODEL), F32)
        for j in range(nf):
            rows = pl.ds(j * FC, FC)
            up = jnp.maximum(_dot_nt(h1b, wup_ref[rows, :]), 0.0)
            up_scr[:, rows] = up
            a = _bf(up * up)
            a_ref[:, rows] = a
            ff = ff + _dot(a, wdn_ref[rows, :])
        xhat, rstd = _layer_norm(ALPHA * h1v + ff)
        gamma = g_ref[...]
        err = xhat * gamma + b_ref[...] - t_ref[...]
        loss_ref[...] += jnp.sum(jnp.sum(err * err, axis=-1, keepdims=True), axis=0, keepdims=True) * (0.5 / D_MODEL)
        dy = err * (1.0 / D_MODEL)
        dg_ref[...] += jnp.sum(dy * xhat, axis=0, keepdims=True)
        db_ref[...] += jnp.sum(dy, axis=0, keepdims=True)
        du2 = _layer_norm_bwd(dy, gamma, xhat, rstd)
        du2b = _bf(du2)
        du2_ref[...] = du2b
        dh1 = ALPHA * du2
        for j in range(nf):
            rows = pl.ds(j * FC, FC)
            dup = _bf(_dot_nt(du2b, wdn_ref[rows, :]) * (2.0 * up_scr[:, rows]))
            dup_ref[:, rows] = dup
            dh1 = dh1 + _dot(dup, wup_ref[rows, :])
        dh1_ref[...] = dh1

    row = lambda w: pl.BlockSpec((T, w), lambda i: (i, 0))
    vec = pl.BlockSpec((1, D_MODEL), lambda i: (0, 0))
    vshape = jax.ShapeDtypeStruct((1, D_MODEL), F32)
    return pl.pallas_call(
        body,
        grid=(S // T,),
        in_specs=[row(1024), row(1024), _const_spec((D_FF, D_MODEL)), _const_spec((D_FF, D_MODEL)), vec, vec],
        out_specs=[row(1024), row(D_FF), row(D_FF), row(1024), pl.BlockSpec((8, 128), lambda i: (0, 0)), vec, vec],
        out_shape=[
            jax.ShapeDtypeStruct((S, D_MODEL), F32),
            jax.ShapeDtypeStruct((S, D_FF), BF16),
            jax.ShapeDtypeStruct((S, D_FF), BF16),
            jax.ShapeDtypeStruct((S, D_MODEL), BF16),
            jax.ShapeDtypeStruct((8, 128), F32), vshape, vshape,
        ],
        scratch_shapes=[pltpu.VMEM((T, D_FF), F32)],
        compiler_params=_cparams("arbitrary"),
        name="mlp_fwd_bwd",
    )(h1, target, wup_t, wdn, ln_g, ln_b)


def _local_step(x, mem, target, lb_logits, gain, sinks, rel_bias, ln1_g, ln1_b, ln2_g, ln2_b,
                win_t, wkv_t, wbh, wbs, wbm, wout, wup_t, wdn):
    S = x.shape[0]
    T = min(256, S)
    KC = min(1024, S)
    xb = _bf(x)
    zmain = _mm_nt(x, win_t[:C_GL], tm=min(512, S), tn=3200, out_dtype=F32, name="in_proj_main")
    gl = _mm_nt(x, win_t[C_GL:], tm=min(512, S), tn=1536, out_dtype=F32, name="in_proj_gates")
    mkv = _mm_nt(mem, wkv_t, tm=MEM_LEN, tn=1024, out_dtype=F32, name="mem_kv_proj")
    bucket = _t5_bucket_table()

    o_raw, states = _hgrn_fwd(zmain, lb_logits, T=min(512, S))
    o_b = _swa_fwd(zmain, bucket, rel_bias, sinks)
    o_c = _mem_fwd(zmain, mkv, T=T)
    merge_args = (o_raw, zmain, o_b, o_c, gl, x, gain, wbh, wbs, wbm, wout, ln1_g, ln1_b)
    h1 = _merge_fwd(*merge_args, T=T)

    d_h1, act, d_up, du2, loss, d_ln2_g, d_ln2_b = _mlp_fwd_bwd(h1, target, wup_t, wdn, ln2_g, ln2_b, T=T, FC=512)
    g_wdn = _mm_tn_resident(act, du2, tm=256, kc=KC, name="grad_w_down")
    g_wup_t = _mm_tn_resident(d_up, _bf(h1), tm=256, kc=KC, name="grad_w_up")

    (dx_part, du1, m, oa, dpa, dpb, dpc, d_hg, d_oraw, d_ob, d_oc, d_gl,
     d_gain, d_ln1_g, d_ln1_b) = _merge_bwd(d_h1, *merge_args, T=T)
    g_wout = _mm_tn_resident(m, du1, tm=256, kc=KC, name="grad_w_out")
    g_wbh = _mm_tn_resident(oa, dpa, tm=256, kc=KC, name="grad_w_branch_hg")
    g_wbs = _mm_tn_resident(o_b, dpb, tm=256, kc=KC, name="grad_w_branch_swa")
    g_wbm = _mm_tn_resident(o_c, dpc, tm=256, kc=KC, name="grad_w_branch_mem")

    d_mq, d_mk, d_mv = _mem_bwd(zmain, mkv, o_c, d_oc, T=T)
    g_wkv_t = _mm_tn_resident(jnp.concatenate([d_mk, d_mv], axis=1), mem, tm=256, kc=MEM_LEN, name="grad_w_mem_kv")
    d_sq, d_skv, d_rb, d_sink = _swa_bwd(zmain, o_b, d_ob, bucket, rel_bias, sinks)
    d_q, d_f, d_v, d_lb = _hgrn_bwd(zmain, lb_logits, states, d_oraw, T=min(512, S))

    dz = jnp.concatenate([d_q, d_f, d_v, d_hg, d_sq, d_skv, d_mq, d_gl], axis=1)
    grad_x = _mm_nn_resident(dz, win_t, dx_part, tm=T, kc=IN_COLS // 2, name="grad_x")
    g_win_t = _mm_tn_resident(dz, xb, tm=256, kc=KC, name="grad_w_in")

    small = dict(
        d_lb=d_lb, d_gain=d_gain, d_sink=d_sink[:, 0, 0].reshape(1, SWA_HEADS),
        d_rb=d_rb[:, 0, 0].reshape(SWA_HEADS, NUM_BUCKETS).T,
        d_ln1_g=d_ln1_g, d_ln1_b=d_ln1_b, d_ln2_g=d_ln2_g, d_ln2_b=d_ln2_b, loss=loss[0, 0])
    big = dict(win_t=g_win_t, wkv_t=g_wkv_t, wbh=g_wbh, wbs=g_wbs, wbm=g_wbm, wout=g_wout, wup_t=g_wup_t, wdn=g_wdn)
    return grad_x, big, small


MESH = pl.DeviceIdType.MESH
ANY = pl.BlockSpec(memory_space=pl.ANY)


def _coords():
    return lax.axis_index("x"), lax.axis_index("y"), lax.axis_index("c")


def _other_chips(x, y):
    return [(1 - x, y), (x, 1 - y), (1 - x, 1 - y)]


def _all_gather_weights(p1, p2):
    arrays = (p1, p2)
    na = len(arrays)

    def body(*refs):
        srcs, dsts = refs[:na], refs[na:2 * na]
        send_sems, recv_sems, local_sems = refs[2 * na:]
        x, y, c = _coords()
        me, sibling = (x, y, c), (x, y, 1 - c)
        chips = _other_chips(x, y)

        def slot(a, px, py, pc):
            return dsts[a].at[4 * px + 2 * py + pc]

        def copy(a, k, block, to, from_shard=False):
            return pltpu.make_async_remote_copy(
                src_ref=srcs[a] if from_shard else slot(a, *block), dst_ref=slot(a, *block),
                send_sem=send_sems.at[a * 7 + k], recv_sem=recv_sems.at[a * 7 + k],
                device_id=to, device_id_type=MESH)

        own = [pltpu.make_async_copy(srcs[a], slot(a, *me), local_sems.at[a]) for a in range(na)]
        for cp in own:
            cp.start()
        first = []
        for a in range(na):
            first.append(copy(a, 0, me, sibling, True))
            first += [copy(a, 1 + j, me, (*chip, c), True) for j, chip in enumerate(chips)]
        for cp in first:
            cp.start()
        passed = []
        for j, chip in enumerate(chips):
            for a in range(na):
                copy(a, 1 + j, (*chip, c), me).wait_recv()
                fwd = copy(a, 4 + j, (*chip, c), sibling)
                fwd.start()
                passed.append(fwd)
        for a in range(na):
            copy(a, 0, sibling, me).wait_recv()
            for j, chip in enumerate(chips):
                copy(a, 4 + j, (*chip, 1 - c), me).wait_recv()
        for cp in first + passed:
            cp.wait_send()
        for cp in own:
            cp.wait()

    return pl.pallas_call(
        body,
        in_specs=[ANY] * na,
        out_specs=[ANY] * na,
        out_shape=[jax.ShapeDtypeStruct((N_DEV,) + a.shape, a.dtype) for a in arrays],
        scratch_shapes=[pltpu.SemaphoreType.DMA((7 * na,)), pltpu.SemaphoreType.DMA((7 * na,)),
                        pltpu.SemaphoreType.DMA((na,))],
        name="all_gather_weights",
    )(*arrays)


def _exchange_with_sibling(p1, p2):
    arrays = (p1, p2)
    na = len(arrays)

    def body(*refs):
        srcs, dsts = refs[:na], refs[na:2 * na]
        send_sems, recv_sems = refs[2 * na:]
        x, y, c = _coords()
        copies = [
            pltpu.make_async_remote_copy(
                src_ref=srcs[a].at[k, 1 - c], dst_ref=dsts[a].at[k],
                send_sem=send_sems.at[a * 4 + k], recv_sem=recv_sems.at[a * 4 + k],
                device_id=(x, y, 1 - c), device_id_type=MESH)
            for a in range(na) for k in range(4)]
        for cp in copies:
            cp.start()
        for cp in copies:
            cp.wait()

    return pl.pallas_call(
        body,
        in_specs=[ANY] * na,
        out_specs=[ANY] * na,
        out_shape=[jax.ShapeDtypeStruct((4,) + a.shape[2:], a.dtype) for a in arrays],
        scratch_shapes=[pltpu.SemaphoreType.DMA((4 * na,)), pltpu.SemaphoreType.DMA((4 * na,))],
        name="reduce_scatter_sibling",
    )(*arrays)


def _exchange_with_chips(q1, q2):
    arrays = (q1, q2)
    na = len(arrays)

    def body(*refs):
        srcs, dsts = refs[:na], refs[na:2 * na]
        send_sems, recv_sems = refs[2 * na:]
        x, y, c = _coords()
        copies = [
            pltpu.make_async_remote_copy(
                src_ref=srcs[a].at[2 * chip[0] + chip[1]], dst_ref=dsts[a].at[j],
                send_sem=send_sems.at[a * 3 + j], recv_sem=recv_sems.at[a * 3 + j],
                device_id=(*chip, c), device_id_type=MESH)
            for a in range(na) for j, chip in enumerate(_other_chips(x, y))]
        for cp in copies:
            cp.start()
        for cp in copies:
            cp.wait()

    return pl.pallas_call(
        body,
        in_specs=[ANY] * na,
        out_specs=[ANY] * na,
        out_shape=[jax.ShapeDtypeStruct((3,) + a.shape[1:], a.dtype) for a in arrays],
        scratch_shapes=[pltpu.SemaphoreType.DMA((3 * na,)), pltpu.SemaphoreType.DMA((3 * na,))],
        name="reduce_scatter_chips",
    )(*arrays)


def _pair_sum(p, ra, core, *, tr):
    R = p.shape[2]

    def body(core_ref, p_ref, ra_ref, o_ref):
        del core_ref
        o_ref[...] = p_ref[0] + ra_ref[...]

    return pl.pallas_call(
        body,
        grid_spec=pltpu.PrefetchScalarGridSpec(
            num_scalar_prefetch=1, grid=(4, R // tr),
            in_specs=[pl.BlockSpec((1, 1, tr, 1024), lambda k, i, cr: (k, cr[0], i, 0)),
                      pl.BlockSpec((1, tr, 1024), lambda k, i, cr: (k, i, 0))],
            out_specs=pl.BlockSpec((1, tr, 1024), lambda k, i, cr: (k, i, 0))),
        out_shape=jax.ShapeDtypeStruct(ra.shape, F32),
        name="pair_sum",
    )(core, p, ra)


def _chip_sum(q, rb, chip, *, tr):
    R = q.shape[1]

    def body(chip_ref, q_ref, rb_ref, o_ref):
        del chip_ref
        o_ref[...] = ((q_ref[0] + rb_ref[0]) + rb_ref[1]) + rb_ref[2]

    return pl.pallas_call(
        body,
        grid_spec=pltpu.PrefetchScalarGridSpec(
            num_scalar_prefetch=1, grid=(R // tr,),
            in_specs=[pl.BlockSpec((1, tr, 1024), lambda i, cr: (cr[0], i, 0)),
                      pl.BlockSpec((3, tr, 1024), lambda i, cr: (0, i, 0))],
            out_specs=pl.BlockSpec((tr, 1024), lambda i, cr: (i, 0))),
        out_shape=jax.ShapeDtypeStruct((R, 1024), F32),
        name="chip_sum",
    )(chip, q, rb)


def _small_all_reduce(packed, lb_logits):
    def body(p_ref, lbl_ref, o_ref, gath, send_sems, recv_sems):
        x, y, c = _coords()
        mine = 4 * x + 2 * y + c
        gath[mine] = p_ref[...]
        copies = []
        for r in range(1, N_DEV):
            peer = (x ^ (r >> 2), y ^ ((r >> 1) & 1), c ^ (r & 1))
            copies.append(pltpu.make_async_remote_copy(
                src_ref=p_ref, dst_ref=gath.at[mine],
                send_sem=send_sems.at[r - 1], recv_sem=recv_sems.at[r - 1],
                device_id=peer, device_id_type=MESH))
        for cp in copies:
            cp.start()
        for r in range(1, N_DEV):
            peer_slot = 4 * (x ^ (r >> 2)) + 2 * (y ^ ((r >> 1) & 1)) + (c ^ (r & 1))
            pltpu.make_async_remote_copy(
                src_ref=p_ref, dst_ref=gath.at[peer_slot],
                send_sem=send_sems.at[r - 1], recv_sem=recv_sems.at[r - 1],
                device_id=(x, y, c), device_id_type=MESH).wait_recv()
        for cp in copies:
            cp.wait_send()
        tot = gath[0]
        for d in range(1, N_DEV):
            tot = tot + gath[d]
        o_ref[...] = tot
        lb = _lower_bound(lbl_ref)
        dl0 = o_ref[SM_LB:SM_LB + 1, :] * lb * (1.0 - lb)
        o_ref[SM_LB:SM_LB + 1, :] = dl0
        o_ref[SM_LB + 1:SM_LB + 2, :] = -dl0

    vm = pl.BlockSpec(memory_space=pltpu.VMEM)
    return pl.pallas_call(
        body,
        in_specs=[vm, vm],
        out_specs=vm,
        out_shape=jax.ShapeDtypeStruct(packed.shape, F32),
        scratch_shapes=[pltpu.VMEM((N_DEV,) + packed.shape, F32),
                        pltpu.SemaphoreType.DMA((N_DEV - 1,)), pltpu.SemaphoreType.DMA((N_DEV - 1,))],
        name="small_all_reduce",
    )(packed, lb_logits)


def _adamw(w, g, m, v, *, tr, name):
    R, C = w.shape

    def body(w_ref, g_ref, m_ref, v_ref, d_ref, nm_ref, nv_ref):
        gv = g_ref[...]
        nm = ADAM_B1 * m_ref[...] + (1.0 - ADAM_B1) * gv
        nv = ADAM_B2 * v_ref[...] + (1.0 - ADAM_B2) * jnp.square(gv)
        m_hat = nm / (1.0 - ADAM_B1 ** ADAM_STEP)
        v_hat = nv / (1.0 - ADAM_B2 ** ADAM_STEP)
        d_ref[...] = -ADAM_LR * (m_hat / (jnp.sqrt(v_hat) + ADAM_EPS) + ADAM_WD * w_ref[...])
        nm_ref[...] = nm
        nv_ref[...] = nv

    spec = pl.BlockSpec((tr, C), lambda i: (i, 0))
    return pl.pallas_call(
        body,
        grid=(R // tr,),
        in_specs=[spec] * 4,
        out_specs=[spec] * 3,
        out_shape=[jax.ShapeDtypeStruct((R, C), F32)] * 3,
        compiler_params=_cparams("parallel"),
        name=name,
    )(w, g, m, v)


def _pack_small(lb, gain, sinks, rel_bias, ln1_g, ln1_b, ln2_g, ln2_b, loss=None):
    pad = lambda a: jnp.pad(a.reshape(1, -1), ((0, 0), (0, D_MODEL - a.size)))
    rows = [lb.reshape(-1, D_MODEL)]
    if rows[0].shape[0] == 1:
        rows.append(jnp.zeros((1, D_MODEL), F32))
    rows += [gain.reshape(1, D_MODEL), pad(sinks), pad(rel_bias), ln1_g.reshape(1, D_MODEL), ln1_b.reshape(1, D_MODEL),
             ln2_g.reshape(1, D_MODEL), ln2_b.reshape(1, D_MODEL),
             pad(jnp.zeros((1,), F32) if loss is None else loss.reshape(1))]
    rows.append(jnp.zeros((SM_ROWS - SM_LOSS - 1, D_MODEL), F32))
    return jnp.concatenate(rows, axis=0)


def _unpack_small(p):
    return dict(
        lb_logits=p[SM_LB:SM_LB + 2], hg_norm_gain=p[SM_GAIN:SM_GAIN + 1], swa_sinks=p[SM_SINK:SM_SINK + 1, :SWA_HEADS],
        rel_bias=p[SM_RB, :NUM_BUCKETS * SWA_HEADS].reshape(NUM_BUCKETS, SWA_HEADS),
        ln1_g=p[SM_L1G:SM_L1G + 1], ln1_b=p[SM_L1B:SM_L1B + 1], ln2_g=p[SM_L2G:SM_L2G + 1], ln2_b=p[SM_L2B:SM_L2B + 1])


_SMALL = ("lb_logits", "hg_norm_gain", "swa_sinks", "rel_bias", "ln1_g", "ln1_b", "ln2_g", "ln2_b")
_WEIGHTS = ("w_in", "lb_logits", "hg_norm_gain", "swa_sinks", "rel_bias", "w_mem_kv", "w_branch_hg", "w_branch_swa",
            "w_branch_mem", "w_out", "ln1_g", "ln1_b", "w_up", "w_down", "ln2_g", "ln2_b")


def kernel(x, mem, w_in, lb_logits, hg_norm_gain, swa_sinks, rel_bias, w_mem_kv, w_branch_hg, w_branch_swa, w_branch_mem, w_out, ln1_g, ln1_b, w_up, w_down, ln2_g, ln2_b, loss_target, m_w_in, m_lb_logits, m_hg_norm_gain, m_swa_sinks, m_rel_bias, m_w_mem_kv, m_w_branch_hg, m_w_branch_swa, m_w_branch_mem, m_w_out, m_ln1_g, m_ln1_b, m_w_up, m_w_down, m_ln2_g, m_ln2_b, v_w_in, v_lb_logits, v_hg_norm_gain, v_swa_sinks, v_rel_bias, v_w_mem_kv, v_w_branch_hg, v_w_branch_swa, v_w_branch_mem, v_w_out, v_ln1_g, v_ln1_b, v_w_up, v_w_down, v_ln2_g, v_ln2_b):
    w = dict(w_in=w_in, lb_logits=lb_logits, hg_norm_gain=hg_norm_gain, swa_sinks=swa_sinks, rel_bias=rel_bias,
             w_mem_kv=w_mem_kv, w_branch_hg=w_branch_hg, w_branch_swa=w_branch_swa, w_branch_mem=w_branch_mem,
             w_out=w_out, ln1_g=ln1_g, ln1_b=ln1_b, w_up=w_up, w_down=w_down, ln2_g=ln2_g, ln2_b=ln2_b)
    mom = dict(w_in=m_w_in, lb_logits=m_lb_logits, hg_norm_gain=m_hg_norm_gain, swa_sinks=m_swa_sinks, rel_bias=m_rel_bias,
               w_mem_kv=m_w_mem_kv, w_branch_hg=m_w_branch_hg, w_branch_swa=m_w_branch_swa, w_branch_mem=m_w_branch_mem,
               w_out=m_w_out, ln1_g=m_ln1_g, ln1_b=m_ln1_b, w_up=m_w_up, w_down=m_w_down, ln2_g=m_ln2_g, ln2_b=m_ln2_b)
    var = dict(w_in=v_w_in, lb_logits=v_lb_logits, hg_norm_gain=v_hg_norm_gain, swa_sinks=v_swa_sinks, rel_bias=v_rel_bias,
               w_mem_kv=v_w_mem_kv, w_branch_hg=v_w_branch_hg, w_branch_swa=v_w_branch_swa, w_branch_mem=v_w_branch_mem,
               w_out=v_w_out, ln1_g=v_ln1_g, ln1_b=v_ln1_b, w_up=v_w_up, w_down=v_w_down, ln2_g=v_ln2_g, ln2_b=v_ln2_b)
    xc, yc, cc = _coords()

    p1 = _bf(w_in[0].T)
    p2 = _bf(jnp.concatenate([w_down[0], w_up[0].T, w_branch_hg[0], w_branch_swa[0], w_branch_mem[0], w_out[0],
                              w_mem_kv[0].T], axis=0))
    g1, g2 = _all_gather_weights(p1, p2)
    full = lambda lo, hi: g2[:, lo:hi].reshape(N_DEV * (hi - lo), D_MODEL)
    grad_x, big, small = _local_step(
        x[0], mem[0], loss_target[0], lb_logits, hg_norm_gain, swa_sinks, rel_bias, ln1_g, ln1_b, ln2_g, ln2_b,
        g1.reshape(IN_COLS, D_MODEL), full(R_KV, R_OTHER), full(R_BH, R_BS), full(R_BS, R_BM), full(R_BM, R_OUT),
        full(R_OUT, R_KV), full(R_UP, R_BH), full(R_DN, R_UP))

    blocks = lambda a: a.reshape(N_DEV, a.shape[0] // N_DEV, D_MODEL)
    part1 = big["win_t"].reshape(4, 2, IN_SHARD, D_MODEL)
    part2 = jnp.concatenate([blocks(big[k]) for k in ("wdn", "wup_t", "wbh", "wbs", "wbm", "wout", "wkv_t")],
                            axis=1).reshape(4, 2, R_OTHER, D_MODEL)
    ra1, ra2 = _exchange_with_sibling(part1, part2)
    core = cc.reshape(1).astype(jnp.int32)
    q1 = _pair_sum(part1, ra1, core, tr=IN_SHARD // 2)
    q2 = _pair_sum(part2, ra2, core, tr=R_OTHER // 2)
    rb1, rb2 = _exchange_with_chips(q1, q2)
    chip = (2 * xc + yc).reshape(1).astype(jnp.int32)
    gs1 = _chip_sum(q1, rb1, chip, tr=IN_SHARD // 2)
    gs2 = _chip_sum(q2, rb2, chip, tr=R_OTHER // 2)

    grads = dict(
        w_in=gs1.T, w_down=gs2[R_DN:R_UP], w_up=gs2[R_UP:R_BH].T, w_branch_hg=gs2[R_BH:R_BS],
        w_branch_swa=gs2[R_BS:R_BM], w_branch_mem=gs2[R_BM:R_OUT], w_out=gs2[R_OUT:R_KV], w_mem_kv=gs2[R_KV:R_OTHER].T)

    packed = _pack_small(small["d_lb"], small["d_gain"], small["d_sink"], small["d_rb"], small["d_ln1_g"],
                         small["d_ln1_b"], small["d_ln2_g"], small["d_ln2_b"], small["loss"])
    reduced = _small_all_reduce(packed, lb_logits)
    loss = reduced[SM_LOSS, 0]
    grads.update(_unpack_small(reduced))

    delta, new_m, new_v = {}, {}, {}
    for name in _WEIGHTS:
        if name in _SMALL:
            continue
        w2 = w[name][0]
        delta[name], new_m[name], new_v[name] = _adamw(
            w2, grads[name], mom[name][0], var[name][0], tr=w2.shape[0] // 4, name="adamw_" + name)
    sm = lambda d: _pack_small(*[d[k] for k in _SMALL])
    d_s, m_s, v_s = _adamw(sm(w), reduced, sm(mom), sm(var), tr=SM_ROWS, name="adamw_small")
    for dst, src in ((delta, d_s), (new_m, m_s), (new_v, v_s)):
        dst.update(_unpack_small(src))

    def shaped(d, name):
        return d[name].reshape(w[name].shape)

    return (loss, grad_x[None], *[shaped(grads, n) for n in _WEIGHTS], *[shaped(delta, n) for n in _WEIGHTS],
            *[shaped(new_m, n) for n in _WEIGHTS], *[shaped(new_v, n) for n in _WEIGHTS])
```

```python
import functools
import math

import jax
import jax.numpy as jnp
from jax import lax
from jax.experimental import pallas as pl
from jax.experimental.pallas import tpu as pltpu

F32 = jnp.float32
BF16 = jnp.bfloat16

D_MODEL = 1024
MEM_LEN = 256
HG_HEADS = 8
HG_DK = 128
HG_CHUNK = 64
SWA_HEADS = 16
SWA_HEAD_DIM = 64
SWA_BLOCK = 128
SWA_WINDOW = 128
MEM_HEADS = 4
MEM_HEAD_DIM = 256
NUM_BUCKETS = 32
MAX_DISTANCE = 128
D_FF = 4096
LN_EPS = 1e-5
RMS_EPS = 1e-6
ALPHA = 2.0 ** 0.25
N_DEV = 8

C_HQ, C_HF, C_HI, C_HG, C_SQ, C_SK, C_SV, C_MQ, C_GL = 0, 1024, 2048, 3072, 4096, 5120, 5248, 5376, 6400
IN_COLS = 9472
IN_SHARD = IN_COLS // N_DEV

ADAM_LR = 0.001
ADAM_B1 = 0.9
ADAM_B2 = 0.999
ADAM_EPS = 1e-08
ADAM_WD = 0.01
ADAM_STEP = 10

VMEM_LIMIT = 56 * 1024 * 1024

R_DN, R_UP, R_BH, R_BS, R_BM, R_OUT, R_KV, R_OTHER = 0, 512, 1024, 1152, 1280, 1408, 1536, 1792

SM_LB, SM_GAIN, SM_SINK, SM_RB, SM_L1G, SM_L1B, SM_L2G, SM_L2B, SM_LOSS, SM_ROWS = 0, 2, 3, 4, 5, 6, 7, 8, 9, 16


def _bf(v):
    return v.astype(BF16)


def _dot(a, b):
    return jnp.dot(a, b, preferred_element_type=F32)


def _dot_nt(a, b):
    return lax.dot_general(a, b, (((1,), (1,)), ((), ())), preferred_element_type=F32)


def _dot_tn(a, b):
    return lax.dot_general(a, b, (((0,), (0,)), ((), ())), preferred_element_type=F32)


def _sig(v):
    return 1.0 / (1.0 + jnp.exp(-v))


def _cparams(*sem):
    return pltpu.CompilerParams(dimension_semantics=sem, vmem_limit_bytes=VMEM_LIMIT)


def _const_spec(shape):
    nd = len(shape)
    return pl.BlockSpec(shape, lambda *_: (0,) * nd, pipeline_mode=pl.Buffered(1))


def _dep_spec():
    return pl.BlockSpec((8, 128), lambda *_: (0, 0))


def _mm_nt(a, bt, *, tm, tn, out_dtype, name, dep=None):
    M, K = a.shape
    N = bt.shape[0]

    def body(a_ref, b_ref, *rest):
        o_ref = rest[-1]
        o_ref[...] = _dot_nt(_bf(a_ref[...]), _bf(b_ref[...])).astype(o_ref.dtype)

    deps = () if dep is None else (dep,)
    return pl.pallas_call(
        body,
        grid=(N // tn, M // tm),
        in_specs=[pl.BlockSpec((tm, K), lambda j, i: (i, 0)), pl.BlockSpec((tn, K), lambda j, i: (j, 0))]
        + [_dep_spec() for _ in deps],
        out_specs=pl.BlockSpec((tm, tn), lambda j, i: (i, j)),
        out_shape=jax.ShapeDtypeStruct((M, N), out_dtype),
        compiler_params=_cparams("parallel", "parallel"),
        name=name,
    )(a, bt, *deps)


def _mm_tn_resident(a, b, *, tm, kc, name, out_dtype=F32):
    K, M = a.shape
    N = b.shape[1]
    nk = K // kc

    def body(a_ref, b_ref, o_ref):
        acc = jnp.zeros((tm, N), F32)
        for kk in range(nk):
            sl = pl.ds(kk * kc, kc)
            acc = acc + _dot_tn(_bf(a_ref[sl, :]), _bf(b_ref[sl, :]))
        o_ref[...] = acc.astype(o_ref.dtype)

    return pl.pallas_call(
        body,
        grid=(M // tm,),
        in_specs=[pl.BlockSpec((K, tm), lambda i: (0, i)), _const_spec((K, N))],
        out_specs=pl.BlockSpec((tm, N), lambda i: (i, 0)),
        out_shape=jax.ShapeDtypeStruct((M, N), out_dtype),
        compiler_params=_cparams("parallel"),
        name=name,
    )(a, b)


def _mm_nn_resident(a, b, add, *, tm, kc, name, dep=None):
    M, K = a.shape
    N = b.shape[1]
    nk = K // kc

    def body(a_ref, b_ref, add_ref, *rest):
        o_ref = rest[-1]
        acc = add_ref[...]
        for kk in range(nk):
            sl = pl.ds(kk * kc, kc)
            acc = acc + _dot(_bf(a_ref[:, sl]), b_ref[sl, :])
        o_ref[...] = acc

    deps = () if dep is None else (dep,)
    return pl.pallas_call(
        body,
        grid=(M // tm,),
        in_specs=[pl.BlockSpec((tm, K), lambda i: (i, 0)), _const_spec((K, N)), pl.BlockSpec((tm, N), lambda i: (i, 0))]
        + [_dep_spec() for _ in deps],
        out_specs=pl.BlockSpec((tm, N), lambda i: (i, 0)),
        out_shape=jax.ShapeDtypeStruct((M, N), F32),
        compiler_params=_cparams("parallel"),
        name=name,
    )(a, b, add, *deps)


def _lower_bound(lbl_ref):
    l0 = lbl_ref[0:1, :]
    l1 = lbl_ref[1:2, :]
    mx = jnp.maximum(l0, l1)
    e0 = jnp.exp(l0 - mx)
    e1 = jnp.exp(l1 - mx)
    return e0 / (e0 + e1)


def _tri(lower):
    r = lax.broadcasted_iota(jnp.int32, (HG_CHUNK, HG_CHUNK), 0)
    c = lax.broadcasted_iota(jnp.int32, (HG_CHUNK, HG_CHUNK), 1)
    return (r >= c) if lower else (r <= c)


def _hg_gates(fl, lb):
    sg = _sig(fl)
    f = lb + (1.0 - lb) * sg
    return sg, f, jnp.log(f), 1.0 - f


def _scan_rows(v, reverse=False):
    row = lax.broadcasted_iota(jnp.int32, v.shape, 0)
    s = 1
    while s < HG_CHUNK:
        if reverse:
            v = v + jnp.where(row < HG_CHUNK - s, pltpu.roll(v, HG_CHUNK - s, 0), 0.0)
        else:
            v = v + jnp.where(row >= s, pltpu.roll(v, s, 0), 0.0)
        s *= 2
    return v


def _hgrn_fwd(zmain, lb_logits, *, T):
    S = zmain.shape[0]
    nc = T // HG_CHUNK

    def body(q_ref, f_ref, v_ref, lbl_ref, o_ref, st_ref, state):
        @pl.when(pl.program_id(1) == 0)
        def _():
            state[...] = jnp.zeros_like(state)

        lb = _lower_bound(lbl_ref)
        tril = _tri(True)
        qis, updates, decays, intra = [], [], [], []
        for c in range(nc):
            sl = pl.ds(c * HG_CHUNK, HG_CHUNK)
            _, _, g, k = _hg_gates(f_ref[sl, :], lb)
            b = _scan_rows(g)
            bl = jnp.sum(g, axis=0, keepdims=True)
            qi = _bf(q_ref[sl, :] * jnp.exp(b))
            ki = _bf(k * jnp.exp(-b))
            ko = _bf(k * jnp.exp(bl - b))
            vb = _bf(v_ref[sl, :])
            att = jnp.where(tril, _dot_nt(qi, ki), 0.0)
            intra.append(_dot(_bf(att), vb))
            qis.append(qi)
            updates.append(_dot_tn(vb, ko))
            decays.append(jnp.exp(bl))
        st = state[...]
        for c in range(nc):
            st_ref[0, c] = st
            o_ref[pl.ds(c * HG_CHUNK, HG_CHUNK), :] = intra[c] + _dot_nt(qis[c], _bf(st))
            st = st * decays[c] + updates[c]
        state[...] = st

    col = lambda base: pl.BlockSpec((T, HG_DK), lambda h, t: (t, base + h))
    return pl.pallas_call(
        body,
        grid=(HG_HEADS, S // T),
        in_specs=[col(0), col(8), col(16), pl.BlockSpec((2, HG_DK), lambda h, t: (0, h))],
        out_specs=[
            pl.BlockSpec((T, HG_DK), lambda h, t: (t, h)),
            pl.BlockSpec((1, nc, HG_DK, HG_DK), lambda h, t: (h, t, 0, 0)),
        ],
        out_shape=[
            jax.ShapeDtypeStruct((S, D_MODEL), F32),
            jax.ShapeDtypeStruct((HG_HEADS, S // HG_CHUNK, HG_DK, HG_DK), F32),
        ],
        scratch_shapes=[pltpu.VMEM((HG_DK, HG_DK), F32)],
        compiler_params=_cparams("parallel", "arbitrary"),
        name="hgrn_fwd",
    )(zmain, zmain, zmain, lb_logits)


def _hgrn_bwd(zmain, lb_logits, states, d_o, *, T):
    S = zmain.shape[0]
    nc = T // HG_CHUNK
    nt = S // T

    def body(q_ref, f_ref, v_ref, lbl_ref, st_ref, do_ref, dq_ref, df_ref, dv_ref, dlb_ref, dstate):
        @pl.when(pl.program_id(1) == 0)
        def _():
            dstate[...] = jnp.zeros_like(dstate)
            dlb_ref[...] = jnp.zeros_like(dlb_ref)

        lb = _lower_bound(lbl_ref)
        tril = _tri(True)
        last_row = lax.broadcasted_iota(jnp.int32, (HG_CHUNK, HG_DK), 0) == HG_CHUNK - 1
        saved = []
        for c in range(nc):
            sl = pl.ds(c * HG_CHUNK, HG_CHUNK)
            sg, f, g, k = _hg_gates(f_ref[sl, :], lb)
            b = _scan_rows(g)
            bl = jnp.sum(g, axis=0, keepdims=True)
            eb = jnp.exp(b)
            enb = jnp.exp(-b)
            eo = jnp.exp(bl - b)
            q_in = q_ref[sl, :] * eb
            k_in = k * enb
            k_out = k * eo
            qi, ki, ko = _bf(q_in), _bf(k_in), _bf(k_out)
            vb = _bf(v_ref[sl, :])
            dob = do_ref[sl, :]
            att = jnp.where(tril, _dot_nt(qi, ki), 0.0)
            d_att = _bf(jnp.where(tril, _dot_nt(dob, vb), 0.0))
            d_kin = _dot_tn(d_att, qi)
            saved.append(dict(
                sg=sg, f=f, eb=eb, enb=enb, eo=eo, ebl=jnp.exp(bl), k_out=k_out, ko=ko, vb=vb, dob=dob,
                d_v=_dot_tn(_bf(att), dob), d_qin=_dot(d_att, ki), d_kin=d_kin,
                qk=(q_in, k_in), d_state=_dot_tn(dob, qi)))
        dst = dstate[...]
        dsts = [None] * nc
        for c in reversed(range(nc)):
            dsts[c] = dst
            dst = dst * saved[c]["ebl"] + saved[c]["d_state"]
        dstate[...] = dst
        dlb = jnp.zeros((1, HG_DK), F32)
        for c in range(nc):
            sl = pl.ds(c * HG_CHUNK, HG_CHUNK)
            s = saved[c]
            q_in, k_in = s["qk"]
            st = st_ref[0, c]
            dstb = _bf(dsts[c])
            d_v = s["d_v"] + _dot_nt(s["ko"], dstb)
            d_qin = s["d_qin"] + _dot(s["dob"], _bf(st))
            d_kout = _dot(s["vb"], dstb)
            d_decay = jnp.sum(dsts[c] * st, axis=0, keepdims=True)
            kk = d_kout * s["k_out"]
            d_b = d_qin * q_in - s["d_kin"] * k_in - kk
            d_bl = jnp.sum(kk, axis=0, keepdims=True) + d_decay * s["ebl"]
            d_g = _scan_rows(d_b + jnp.where(last_row, d_bl, 0.0), reverse=True)
            d_f = d_g / s["f"] - (s["d_kin"] * s["enb"] + d_kout * s["eo"])
            dq_ref[sl, :] = _bf(d_qin * s["eb"])
            df_ref[sl, :] = _bf(d_f * (1.0 - lb) * s["sg"] * (1.0 - s["sg"]))
            dv_ref[sl, :] = _bf(d_v)
            dlb = dlb + jnp.sum(d_f * (1.0 - s["sg"]), axis=0, keepdims=True)
        dlb_ref[...] += dlb

    rev = lambda base: pl.BlockSpec((T, HG_DK), lambda h, t: (nt - 1 - t, base + h))
    outc = pl.BlockSpec((T, HG_DK), lambda h, t: (nt - 1 - t, h))
    return pl.pallas_call(
        body,
        grid=(HG_HEADS, nt),
        in_specs=[
            rev(0), rev(8), rev(16),
            pl.BlockSpec((2, HG_DK), lambda h, t: (0, h)),
            pl.BlockSpec((1, nc, HG_DK, HG_DK), lambda h, t: (h, nt - 1 - t, 0, 0)),
            outc,
        ],
        out_specs=[outc, outc, outc, pl.BlockSpec((1, HG_DK), lambda h, t: (0, h))],
        out_shape=[jax.ShapeDtypeStruct((S, D_MODEL), BF16)] * 3 + [jax.ShapeDtypeStruct((1, D_MODEL), F32)],
        scratch_shapes=[pltpu.VMEM((HG_DK, HG_DK), F32)],
        compiler_params=_cparams("parallel", "arbitrary"),
        name="hgrn_bwd",
    )(zmain, zmain, zmain, lb_logits, states, d_o)


def _t5_bucket_table():
    qi = jnp.arange(SWA_BLOCK)[:, None] + SWA_BLOCK
    kj = jnp.arange(2 * SWA_BLOCK)[None, :]
    n = jnp.clip(qi - kj, 0, SWA_WINDOW - 1)
    max_exact = NUM_BUCKETS // 2
    nf = jnp.maximum(n, 1).astype(F32)
    large = max_exact + (jnp.log(nf / max_exact) / math.log(MAX_DISTANCE / max_exact)
                         * (NUM_BUCKETS - max_exact)).astype(jnp.int32)
    large = jnp.minimum(large, NUM_BUCKETS - 1)
    return jnp.where(n < max_exact, n, large).astype(jnp.int32)


def _swa_valid(n):
    qi = lax.broadcasted_iota(jnp.int32, (SWA_BLOCK, 2 * SWA_BLOCK), 0) + SWA_BLOCK
    kj = lax.broadcasted_iota(jnp.int32, (SWA_BLOCK, 2 * SWA_BLOCK), 1)
    dist = qi - kj
    return (dist >= 0) & (dist < SWA_WINDOW) & ((n > 0) | (kj >= SWA_BLOCK))


def _swa_bias_init(bias, bucket_ref, rb_ref):
    bk = bucket_ref[...]
    for h in range(SWA_HEADS):
        def sel(b, acc, h=h):
            return jnp.where(bk == b, rb_ref[b, h], acc)
        bias[h] = lax.fori_loop(0, NUM_BUCKETS, sel, jnp.zeros(bk.shape, F32))


def _lane_halves(t, kv_head):
    lane = lax.broadcasted_iota(jnp.int32, t.shape, 1)
    rolled = pltpu.roll(t, 64, 1)
    zero = jnp.zeros_like(t)
    if kv_head == 0:
        return jnp.where(lane < 64, t, zero), jnp.where(lane >= 64, rolled, zero)
    return jnp.where(lane < 64, rolled, zero), jnp.where(lane >= 64, t, zero)


def _swa_probs(s, bias_h, valid, sink):
    s = jnp.where(valid, s + bias_h, -jnp.inf)
    m = jnp.maximum(jnp.max(s, axis=-1, keepdims=True), sink)
    p = jnp.exp(s - m)
    es = jnp.exp(sink - m)
    inv = 1.0 / (jnp.sum(p, axis=-1, keepdims=True) + es)
    return p * inv, es * inv


def _swa_fwd(zmain, bucket, rel_bias, sinks):
    S = zmain.shape[0]
    nb = S // SWA_BLOCK
    scale = SWA_HEAD_DIM ** -0.5

    def body(q_ref, kvc_ref, kvp_ref, bucket_ref, rb_ref, sk_ref, o_ref, bias):
        n = pl.program_id(0)

        @pl.when(n == 0)
        def _():
            _swa_bias_init(bias, bucket_ref, rb_ref)

        valid = _swa_valid(n)
        kk = _bf(jnp.concatenate([kvp_ref[:, 0:128], kvc_ref[:, 0:128]], axis=0))
        vv = _bf(jnp.concatenate([kvp_ref[:, 128:256], kvc_ref[:, 128:256]], axis=0))
        for kvh in range(2):
            ka, kb = _lane_halves(kk, kvh)
            va, vb = _lane_halves(vv, kvh)
            qst = _bf(jnp.concatenate([q_ref[:, pl.ds((kvh * 4 + jj) * 128, 128)] for jj in range(4)], axis=0) * scale)
            probs = []
            for odd, kx in enumerate((ka, kb)):
                s = _dot_nt(qst, kx)
                parts = []
                for jj in range(4):
                    h = 2 * (kvh * 4 + jj) + odd
                    p, _ = _swa_probs(s[jj * SWA_BLOCK:(jj + 1) * SWA_BLOCK], bias[h], valid, sk_ref[0, h])
                    parts.append(_bf(p))
                probs.append(jnp.concatenate(parts, axis=0))
            ost = _dot(probs[0], va) + _dot(probs[1], vb)
            for jj in range(4):
                o_ref[:, pl.ds((kvh * 4 + jj) * 128, 128)] = ost[jj * SWA_BLOCK:(jj + 1) * SWA_BLOCK]

    smem = pl.BlockSpec(memory_space=pltpu.SMEM)
    return pl.pallas_call(
        body,
        grid=(nb,),
        in_specs=[
            pl.BlockSpec((SWA_BLOCK, 1024), lambda n: (n, C_SQ // 1024)),
            pl.BlockSpec((SWA_BLOCK, 256), lambda n: (n, C_SK // 256)),
            pl.BlockSpec((SWA_BLOCK, 256), lambda n: (jnp.maximum(n - 1, 0), C_SK // 256)),
            _const_spec((SWA_BLOCK, 2 * SWA_BLOCK)), smem, smem,
        ],
        out_specs=pl.BlockSpec((SWA_BLOCK, 1024), lambda n: (n, 0)),
        out_shape=jax.ShapeDtypeStruct((S, 1024), F32),
        scratch_shapes=[pltpu.VMEM((SWA_HEADS, SWA_BLOCK, 2 * SWA_BLOCK), F32)],
        compiler_params=_cparams("arbitrary"),
        name="swa_fwd",
    )(zmain, zmain, zmain, bucket, rel_bias, sinks)


def _swa_bwd(zmain, o_b, d_o, bucket, rel_bias, sinks, dep):
    S = zmain.shape[0]
    nb = S // SWA_BLOCK
    scale = SWA_HEAD_DIM ** -0.5

    def body(q_ref, kvc_ref, kvp_ref, o_ref, do_ref, bucket_ref, rb_ref, sk_ref, dep_ref,
             dq_ref, dkv_ref, drb_ref, dsk_ref, bias, dbias, carry):
        del dep_ref
        n = pl.program_id(0)

        @pl.when(n == 0)
        def _():
            _swa_bias_init(bias, bucket_ref, rb_ref)
            dbias[...] = jnp.zeros_like(dbias)
            carry[...] = jnp.zeros_like(carry)
            dsk_ref[...] = jnp.zeros_like(dsk_ref)

        @pl.when(n < nb)
        def _():
            valid = _swa_valid(n)
            kk = _bf(jnp.concatenate([kvp_ref[:, 0:128], kvc_ref[:, 0:128]], axis=0))
            vv = _bf(jnp.concatenate([kvp_ref[:, 128:256], kvc_ref[:, 128:256]], axis=0))
            lane = lax.broadcasted_iota(jnp.int32, (2 * SWA_BLOCK, 128), 1)
            lane_q = lax.broadcasted_iota(jnp.int32, (4 * SWA_BLOCK, 128), 1)
            dk_parts, dv_parts = [], []
            for kvh in range(2):
                ka, kb = _lane_halves(kk, kvh)
                va, vb = _lane_halves(vv, kvh)
                pair_cols = [pl.ds((kvh * 4 + jj) * 128, 128) for jj in range(4)]
                qst = _bf(jnp.concatenate([q_ref[:, cl] for cl in pair_cols], axis=0) * scale)
                dost = jnp.concatenate([do_ref[:, cl] for cl in pair_cols], axis=0)
                prod = dost.astype(F32) * jnp.concatenate([o_ref[:, cl] for cl in pair_cols], axis=0)
                dq_st = jnp.zeros((4 * SWA_BLOCK, 128), F32)
                zks, zvs = [], []
                for odd, (kx, vx) in enumerate(((ka, va), (kb, vb))):
                    s = _dot_nt(qst, kx)
                    keep = (lane_q >= 64) if odd else (lane_q < 64)
                    delta = jnp.sum(jnp.where(keep, prod, 0.0), axis=-1, keepdims=True)
                    dp = _dot_nt(dost, vx)
                    p_parts, ds_parts = [], []
                    for jj in range(4):
                        h = 2 * (kvh * 4 + jj) + odd
                        rows = slice(jj * SWA_BLOCK, (jj + 1) * SWA_BLOCK)
                        p, ps = _swa_probs(s[rows], bias[h], valid, sk_ref[0, h])
                        ds = p * (dp[rows] - delta[rows])
                        dbias[h] += ds
                        dsk_ref[h] += jnp.broadcast_to(-jnp.sum(ps * delta[rows], axis=0, keepdims=True), (8, 128))
                        p_parts.append(_bf(p))
                        ds_parts.append(_bf(ds))
                    pst = jnp.concatenate(p_parts, axis=0)
                    dsst = jnp.concatenate(ds_parts, axis=0)
                    dq_st = dq_st + _dot(dsst, kx)
                    zks.append(_dot_tn(dsst, qst))
                    zvs.append(_dot_tn(pst, dost))
                for jj in range(4):
                    dq_ref[:, pair_cols[jj]] = _bf(dq_st[jj * SWA_BLOCK:(jj + 1) * SWA_BLOCK] * scale)
                zk = jnp.where(lane < 64, zks[0], zks[1])
                zv = jnp.where(lane < 64, zvs[0], zvs[1])
                dk_parts.append(zk + pltpu.roll(zk, 64, 1))
                dv_parts.append(zv + pltpu.roll(zv, 64, 1))
            dk = jnp.where(lane < 64, dk_parts[0], dk_parts[1])
            dv = jnp.where(lane < 64, dv_parts[0], dv_parts[1])
            dkv = jnp.concatenate([dk, dv], axis=1)
            dkv_ref[...] = _bf(carry[...] + dkv[0:SWA_BLOCK])
            carry[...] = dkv[SWA_BLOCK:]

        @pl.when(n == nb)
        def _():
            dkv_ref[...] = _bf(carry[...])
            bk = bucket_ref[...]

            def per_head(h, _):
                db = dbias[h]

                def per_bucket(b, _):
                    tot = jnp.sum(jnp.where(bk == b, db, 0.0), axis=1, keepdims=True)
                    tot = jnp.sum(tot, axis=0, keepdims=True)
                    drb_ref[h * NUM_BUCKETS + b] = jnp.broadcast_to(tot, (8, 128))
                    return 0

                return lax.fori_loop(0, NUM_BUCKETS, per_bucket, 0)

            lax.fori_loop(0, SWA_HEADS, per_head, 0)

    smem = pl.BlockSpec(memory_space=pltpu.SMEM)
    cur = lambda n: jnp.minimum(n, nb - 1)
    prev = lambda n: jnp.maximum(jnp.minimum(n, nb - 1) - 1, 0)
    return pl.pallas_call(
        body,
        grid=(nb + 1,),
        in_specs=[
            pl.BlockSpec((SWA_BLOCK, 1024), lambda n: (cur(n), C_SQ // 1024)),
            pl.BlockSpec((SWA_BLOCK, 256), lambda n: (cur(n), C_SK // 256)),
            pl.BlockSpec((SWA_BLOCK, 256), lambda n: (prev(n), C_SK // 256)),
            pl.BlockSpec((SWA_BLOCK, 1024), lambda n: (cur(n), 0)),
            pl.BlockSpec((SWA_BLOCK, 1024), lambda n: (cur(n), 0)),
            _const_spec((SWA_BLOCK, 2 * SWA_BLOCK)), smem, smem, _dep_spec(),
        ],
        out_specs=[
            pl.BlockSpec((SWA_BLOCK, 1024), lambda n: (cur(n), 0)),
            pl.BlockSpec((SWA_BLOCK, 256), lambda n: (jnp.maximum(n - 1, 0), 0)),
            pl.BlockSpec((SWA_HEADS * NUM_BUCKETS, 8, 128), lambda n: (0, 0, 0)),
            pl.BlockSpec((SWA_HEADS, 8, 128), lambda n: (0, 0, 0)),
        ],
        out_shape=[
            jax.ShapeDtypeStruct((S, 1024), BF16),
            jax.ShapeDtypeStruct((S, 256), BF16),
            jax.ShapeDtypeStruct((SWA_HEADS * NUM_BUCKETS, 8, 128), F32),
            jax.ShapeDtypeStruct((SWA_HEADS, 8, 128), F32),
        ],
        scratch_shapes=[
            pltpu.VMEM((SWA_HEADS, SWA_BLOCK, 2 * SWA_BLOCK), F32),
            pltpu.VMEM((SWA_HEADS, SWA_BLOCK, 2 * SWA_BLOCK), F32),
            pltpu.VMEM((SWA_BLOCK, 256), F32),
        ],
        compiler_params=_cparams("arbitrary"),
        name="swa_bwd",
    )(zmain, zmain, zmain, o_b, d_o, bucket, rel_bias, sinks, dep)


def _mem_probs(q_ref, k_ref):
    qs = _bf(q_ref[...] * (MEM_HEAD_DIM ** -0.5))
    s = _dot_nt(qs, _bf(k_ref[...]))
    e = jnp.exp(s - jnp.max(s, axis=-1, keepdims=True))
    return qs, e / jnp.sum(e, axis=-1, keepdims=True)


def _mem_fwd(zmain, mkv, *, T):
    S = zmain.shape[0]

    def body(q_ref, k_ref, v_ref, o_ref):
        _, p = _mem_probs(q_ref, k_ref)
        o_ref[...] = _dot(_bf(p), _bf(v_ref[...]))

    return pl.pallas_call(
        body,
        grid=(MEM_HEADS, S // T),
        in_specs=[
            pl.BlockSpec((T, MEM_HEAD_DIM), lambda h, t: (t, C_MQ // MEM_HEAD_DIM + h)),
            pl.BlockSpec((MEM_LEN, MEM_HEAD_DIM), lambda h, t: (0, h)),
            pl.BlockSpec((MEM_LEN, MEM_HEAD_DIM), lambda h, t: (0, MEM_HEADS + h)),
        ],
        out_specs=pl.BlockSpec((T, MEM_HEAD_DIM), lambda h, t: (t, h)),
        out_shape=jax.ShapeDtypeStruct((S, 1024), F32),
        compiler_params=_cparams("parallel", "parallel"),
        name="mem_fwd",
    )(zmain, mkv, mkv)


def _mem_bwd(zmain, mkv, o_c, d_o, *, T):
    S = zmain.shape[0]
    scale = MEM_HEAD_DIM ** -0.5

    def body(q_ref, k_ref, v_ref, o_ref, do_ref, dq_ref, dk_ref, dv_ref):
        @pl.when(pl.program_id(1) == 0)
        def _():
            dk_ref[...] = jnp.zeros_like(dk_ref)
            dv_ref[...] = jnp.zeros_like(dv_ref)

        qs, p = _mem_probs(q_ref, k_ref)
        dob = do_ref[...]
        delta = jnp.sum(dob.astype(F32) * o_ref[...], axis=-1, keepdims=True)
        ds = _bf(p * (_dot_nt(dob, _bf(v_ref[...])) - delta))
        dq_ref[...] = _bf(_dot(ds, _bf(k_ref[...])) * scale)
        dk_ref[...] += _dot_tn(ds, qs)
        dv_ref[...] += _dot_tn(_bf(p), dob)

    tile = lambda base: pl.BlockSpec((T, MEM_HEAD_DIM), lambda h, t: (t, base + h))
    return pl.pallas_call(
        body,
        grid=(MEM_HEADS, S // T),
        in_specs=[
            tile(C_MQ // MEM_HEAD_DIM),
            pl.BlockSpec((MEM_LEN, MEM_HEAD_DIM), lambda h, t: (0, h)),
            pl.BlockSpec((MEM_LEN, MEM_HEAD_DIM), lambda h, t: (0, MEM_HEADS + h)),
            tile(0), tile(0),
        ],
        out_specs=[
            tile(0),
            pl.BlockSpec((MEM_LEN, MEM_HEAD_DIM), lambda h, t: (0, h)),
            pl.BlockSpec((MEM_LEN, MEM_HEAD_DIM), lambda h, t: (0, h)),
        ],
        out_shape=[jax.ShapeDtypeStruct((S, 1024), BF16)] + [jax.ShapeDtypeStruct((MEM_LEN, 1024), F32)] * 2,
        compiler_params=_cparams("parallel", "arbitrary"),
        name="mem_bwd",
    )(zmain, mkv, mkv, o_c, d_o)


def _layer_norm(u):
    mu = jnp.mean(u, axis=-1, keepdims=True)
    xc = u - mu
    rstd = lax.rsqrt(jnp.mean(xc * xc, axis=-1, keepdims=True) + LN_EPS)
    return xc * rstd, rstd


def _layer_norm_bwd(dy, gamma, xhat, rstd):
    dxh = dy * gamma
    return rstd * (dxh - jnp.mean(dxh, axis=-1, keepdims=True) - xhat * jnp.mean(dxh * xhat, axis=-1, keepdims=True))


def _merge_forward(oraw_ref, hg_ref, ob_ref, oc_ref, gl_ref, x_ref, gain_ref, wbh, wbs, wbm, wout):
    ys, rs = [], []
    for h in range(HG_HEADS):
        oh = oraw_ref[:, pl.ds(h * HG_DK, HG_DK)]
        r = lax.rsqrt(jnp.mean(oh * oh, axis=-1, keepdims=True) + RMS_EPS)
        ys.append(oh * r)
        rs.append(r)
    y = jnp.concatenate(ys, axis=1)
    hg = hg_ref[...]
    sg = _sig(hg)
    silu = hg * sg
    oa = _bf(y * gain_ref[...] * silu)
    pa = _dot(oa, wbh[...])
    pb = _dot(_bf(ob_ref[...]), wbs[...])
    pc = _dot(_bf(oc_ref[...]), wbm[...])
    g0 = _sig(gl_ref[:, 0:1024])
    g1 = _sig(gl_ref[:, 1024:2048])
    g2 = _sig(gl_ref[:, 2048:3072])
    m = _bf(g0 * pa + g1 * pb + g2 * pc)
    u1 = ALPHA * x_ref[...] + _dot(m, wout[...])
    xhat, rstd = _layer_norm(u1)
    return dict(y=y, rs=rs, hg=hg, sg=sg, silu=silu, oa=oa, pa=pa, pb=pb, pc=pc,
                g0=g0, g1=g1, g2=g2, m=m, xhat=xhat, rstd=rstd)


def _merge_in_specs(T):
    row = lambda w, c=0: pl.BlockSpec((T, w), lambda i: (i, c))
    vec = pl.BlockSpec((1, D_MODEL), lambda i: (0, 0))
    w = _const_spec((D_MODEL, D_MODEL))
    return [row(1024), row(1024, C_HG // 1024), row(1024), row(1024), row(3072), row(1024), vec, w, w, w, w, vec, vec]


def _merge_fwd(o_raw, zmain, o_b, o_c, gl, x, gain, wbh, wbs, wbm, wout, ln_g, ln_b, *, T):
    S = x.shape[0]

    def body(oraw_ref, hg_ref, ob_ref, oc_ref, gl_ref, x_ref, gain_ref, wbh_r, wbs_r, wbm_r, wout_r, g_ref, b_ref, h1_ref):
        f = _merge_forward(oraw_ref, hg_ref, ob_ref, oc_ref, gl_ref, x_ref, gain_ref, wbh_r, wbs_r, wbm_r, wout_r)
        h1_ref[...] = f["xhat"] * g_ref[...] + b_ref[...]

    return pl.pallas_call(
        body,
        grid=(S // T,),
        in_specs=_merge_in_specs(T),
        out_specs=pl.BlockSpec((T, D_MODEL), lambda i: (i, 0)),
        out_shape=jax.ShapeDtypeStruct((S, D_MODEL), F32),
        compiler_params=_cparams("parallel"),
        name="merge_fwd",
    )(o_raw, zmain, o_b, o_c, gl, x, gain, wbh, wbs, wbm, wout, ln_g, ln_b)


def _merge_bwd(d_h1, o_raw, zmain, o_b, o_c, gl, x, gain, wbh, wbs, wbm, wout, ln_g, ln_b, *, T):
    S = x.shape[0]

    def body(dh1_ref, oraw_ref, hg_ref, ob_ref, oc_ref, gl_ref, x_ref, gain_ref, wbh_r, wbs_r, wbm_r, wout_r, g_ref, b_ref,
             dx_ref, du1_ref, m_ref, oa_ref, dpa_ref, dpb_ref, dpc_ref, dhg_ref, doraw_ref, dob_ref, doc_ref, dgl_ref,
             dgain_ref, dg_ref, db_ref):
        del b_ref

        @pl.when(pl.program_id(0) == 0)
        def _():
            dgain_ref[...] = jnp.zeros_like(dgain_ref)
            dg_ref[...] = jnp.zeros_like(dg_ref)
            db_ref[...] = jnp.zeros_like(db_ref)

        f = _merge_forward(oraw_ref, hg_ref, ob_ref, oc_ref, gl_ref, x_ref, gain_ref, wbh_r, wbs_r, wbm_r, wout_r)
        dh1 = dh1_ref[...]
        dg_ref[...] += jnp.sum(dh1 * f["xhat"], axis=0, keepdims=True)
        db_ref[...] += jnp.sum(dh1, axis=0, keepdims=True)
        du1 = _layer_norm_bwd(dh1, g_ref[...], f["xhat"], f["rstd"])
        dx_ref[...] = ALPHA * du1
        du1b = _bf(du1)
        du1_ref[...] = du1b
        m_ref[...] = f["m"]
        oa_ref[...] = f["oa"]
        dm = _dot_nt(du1b, wout_r[...])
        for i, (g, p, dp_ref, dob_r, w_r) in enumerate((
                (f["g0"], f["pa"], dpa_ref, None, wbh_r),
                (f["g1"], f["pb"], dpb_ref, dob_ref, wbs_r),
                (f["g2"], f["pc"], dpc_ref, doc_ref, wbm_r))):
            dgl_ref[:, pl.ds(i * 1024, 1024)] = _bf(dm * p * g * (1.0 - g))
            dp = _bf(dm * g)
            dp_ref[...] = dp
            d_branch = _dot_nt(dp, w_r[...])
            if dob_r is not None:
                dob_r[...] = _bf(d_branch)
            else:
                doa = d_branch
        gain = gain_ref[...]
        t = doa * f["y"]
        dgain_ref[...] += jnp.sum(t * f["silu"], axis=0, keepdims=True)
        sg = f["sg"]
        dhg_ref[...] = _bf(t * gain * sg * (1.0 + f["hg"] * (1.0 - sg)))
        dy = doa * gain * f["silu"]
        for h in range(HG_HEADS):
            cols = slice(h * HG_DK, (h + 1) * HG_DK)
            yh = f["y"][:, cols]
            dyh = dy[:, cols]
            doraw_ref[:, pl.ds(h * HG_DK, HG_DK)] = _bf(
                f["rs"][h] * (dyh - yh * jnp.mean(dyh * yh, axis=-1, keepdims=True)))

    row = lambda w: pl.BlockSpec((T, w), lambda i: (i, 0))
    vec = pl.BlockSpec((1, D_MODEL), lambda i: (0, 0))
    bshape = jax.ShapeDtypeStruct((S, D_MODEL), BF16)
    vshape = jax.ShapeDtypeStruct((1, D_MODEL), F32)
    return pl.pallas_call(
        body,
        grid=(S // T,),
        in_specs=[row(1024)] + _merge_in_specs(T),
        out_specs=[row(1024)] * 11 + [row(3072), vec, vec, vec],
        out_shape=[jax.ShapeDtypeStruct((S, D_MODEL), F32)] + [bshape] * 10
        + [jax.ShapeDtypeStruct((S, 3072), BF16), vshape, vshape, vshape],
        compiler_params=_cparams("arbitrary"),
        name="merge_bwd",
    )(d_h1, o_raw, zmain, o_b, o_c, gl, x, gain, wbh, wbs, wbm, wout, ln_g, ln_b)


def _mlp_fwd_bwd(h1, target, wup_t, wdn, ln_g, ln_b, *, T, FC):
    S = h1.shape[0]
    nf = D_FF // FC

    def body(h1_ref, t_ref, wup_ref, wdn_ref, g_ref, b_ref, dh1_ref, a_ref, dup_ref, du2_ref, loss_ref, dg_ref, db_ref, up_scr):
        @pl.when(pl.program_id(0) == 0)
        def _():
            loss_ref[...] = jnp.zeros_like(loss_ref)
            dg_ref[...] = jnp.zeros_like(dg_ref)
            db_ref[...] = jnp.zeros_like(db_ref)

        h1v = h1_ref[...]
        h1b = _bf(h1v)
        ff = jnp.zeros((T, D_MODEL), F32)
        for j in range(nf):
            rows = pl.ds(j * FC, FC)
            up = jnp.maximum(_dot_nt(h1b, wup_ref[rows, :]), 0.0)
            up_scr[:, rows] = up
            a = _bf(up * up)
            a_ref[:, rows] = a
            ff = ff + _dot(a, wdn_ref[rows, :])
        xhat, rstd = _layer_norm(ALPHA * h1v + ff)
        gamma = g_ref[...]
        err = xhat * gamma + b_ref[...] - t_ref[...]
        loss_ref[...] += jnp.sum(jnp.sum(err * err, axis=-1, keepdims=True), axis=0, keepdims=True) * (0.5 / D_MODEL)
        dy = err * (1.0 / D_MODEL)
        dg_ref[...] += jnp.sum(dy * xhat, axis=0, keepdims=True)
        db_ref[...] += jnp.sum(dy, axis=0, keepdims=True)
        du2 = _layer_norm_bwd(dy, gamma, xhat, rstd)
        du2b = _bf(du2)
        du2_ref[...] = du2b
        dh1 = ALPHA * du2
        for j in range(nf):
            rows = pl.ds(j * FC, FC)
            dup = _bf(_dot_nt(du2b, wdn_ref[rows, :]) * (2.0 * up_scr[:, rows]))
            dup_ref[:, rows] = dup
            dh1 = dh1 + _dot(dup, wup_ref[rows, :])
        dh1_ref[...] = dh1

    row = lambda w: pl.BlockSpec((T, w), lambda i: (i, 0))
    vec = pl.BlockSpec((1, D_MODEL), lambda i: (0, 0))
    vshape = jax.ShapeDtypeStruct((1, D_MODEL), F32)
    return pl.pallas_call(
        body,
        grid=(S // T,),
        in_specs=[row(1024), row(1024), _const_spec((D_FF, D_MODEL)), _const_spec((D_FF, D_MODEL)), vec, vec],
        out_specs=[row(1024), row(D_FF), row(D_FF), row(1024), pl.BlockSpec((8, 128), lambda i: (0, 0)), vec, vec],
        out_shape=[
            jax.ShapeDtypeStruct((S, D_MODEL), F32),
            jax.ShapeDtypeStruct((S, D_FF), BF16),
            jax.ShapeDtypeStruct((S, D_FF), BF16),
            jax.ShapeDtypeStruct((S, D_MODEL), BF16),
            jax.ShapeDtypeStruct((8, 128), F32), vshape, vshape,
        ],
        scratch_shapes=[pltpu.VMEM((T, D_FF), F32)],
        compiler_params=_cparams("arbitrary"),
        name="mlp_fwd_bwd",
    )(h1, target, wup_t, wdn, ln_g, ln_b)


def _local_step(x, mem, target, lb_logits, gain, sinks, rel_bias, ln1_g, ln1_b, ln2_g, ln2_b,
                win_t, dep0, other_weights, send_other_grads, send_win_grad):
    S = x.shape[0]
    T = min(256, S)
    KC = min(1024, S)
    xb = _bf(x)
    zmain = _mm_nt(x, win_t[:C_GL], tm=min(512, S), tn=3200, out_dtype=F32, name="in_proj_main", dep=dep0)
    gl = _mm_nt(x, win_t[C_GL:], tm=min(512, S), tn=1536, out_dtype=F32, name="in_proj_gates")
    bucket = _t5_bucket_table()

    o_raw, states = _hgrn_fwd(zmain, lb_logits, T=min(512, S))
    o_b = _swa_fwd(zmain, bucket, rel_bias, sinks)
    wkv_t, wbh, wbs, wbm, wout, wup_t, wdn = other_weights(o_b)
    mkv = _mm_nt(mem, wkv_t, tm=MEM_LEN, tn=1024, out_dtype=F32, name="mem_kv_proj")
    o_c = _mem_fwd(zmain, mkv, T=T)
    merge_args = (o_raw, zmain, o_b, o_c, gl, x, gain, wbh, wbs, wbm, wout, ln1_g, ln1_b)
    h1 = _merge_fwd(*merge_args, T=T)

    d_h1, act, d_up, du2, loss, d_ln2_g, d_ln2_b = _mlp_fwd_bwd(h1, target, wup_t, wdn, ln2_g, ln2_b, T=T, FC=512)
    wgrad = functools.partial(_mm_tn_resident, tm=256, out_dtype=BF16)
    g_wdn = wgrad(act, du2, kc=KC, name="grad_w_down")
    g_wup_t = wgrad(d_up, _bf(h1), kc=KC, name="grad_w_up")

    (dx_part, du1, m, oa, dpa, dpb, dpc, d_hg, d_oraw, d_ob, d_oc, d_gl,
     d_gain, d_ln1_g, d_ln1_b) = _merge_bwd(d_h1, *merge_args, T=T)
    g_wout = wgrad(m, du1, kc=KC, name="grad_w_out")
    g_wbh = wgrad(oa, dpa, kc=KC, name="grad_w_branch_hg")
    g_wbs = wgrad(o_b, dpb, kc=KC, name="grad_w_branch_swa")
    g_wbm = wgrad(o_c, dpc, kc=KC, name="grad_w_branch_mem")

    d_mq, d_mk, d_mv = _mem_bwd(zmain, mkv, o_c, d_oc, T=T)
    g_wkv_t = wgrad(jnp.concatenate([d_mk, d_mv], axis=1), mem, kc=MEM_LEN, name="grad_w_mem_kv")
    sent_others = send_other_grads(
        dict(wkv_t=g_wkv_t, wbh=g_wbh, wbs=g_wbs, wbm=g_wbm, wout=g_wout, wup_t=g_wup_t, wdn=g_wdn))
    d_sq, d_skv, d_rb, d_sink = _swa_bwd(zmain, o_b, d_ob, bucket, rel_bias, sinks, sent_others)
    d_q, d_f, d_v, d_lb = _hgrn_bwd(zmain, lb_logits, states, d_oraw, T=min(512, S))

    dz = jnp.concatenate([d_q, d_f, d_v, d_hg, d_sq, d_skv, d_mq, d_gl], axis=1)
    sent_win = send_win_grad(wgrad(dz, xb, kc=KC, name="grad_w_in"))
    grad_x = _mm_nn_resident(dz, win_t, dx_part, tm=T, kc=IN_COLS // 2, name="grad_x", dep=sent_win)

    small = dict(
        d_lb=d_lb, d_gain=d_gain, d_sink=d_sink[:, 0, 0].reshape(1, SWA_HEADS),
        d_rb=d_rb[:, 0, 0].reshape(SWA_HEADS, NUM_BUCKETS).T,
        d_ln1_g=d_ln1_g, d_ln1_b=d_ln1_b, d_ln2_g=d_ln2_g, d_ln2_b=d_ln2_b, loss=loss[0, 0])
    return grad_x, small


MESH = pl.DeviceIdType.MESH
ANY = pl.BlockSpec(memory_space=pl.ANY)


def _coords():
    return lax.axis_index("x"), lax.axis_index("y"), lax.axis_index("c")


def _other_chips(x, y):
    return [(1 - x, y), (x, 1 - y), (1 - x, 1 - y)]


def _all_gather_weights(*arrays):
    na = len(arrays)

    def body(*refs):
        srcs, dsts = refs[:na], refs[na:2 * na]
        send_sems, recv_sems, local_sems = refs[2 * na:]
        x, y, c = _coords()
        me, sibling = (x, y, c), (x, y, 1 - c)
        chips = _other_chips(x, y)

        def slot(a, px, py, pc):
            return dsts[a].at[4 * px + 2 * py + pc]

        def copy(a, k, block, to, from_shard=False):
            return pltpu.make_async_remote_copy(
                src_ref=srcs[a] if from_shard else slot(a, *block), dst_ref=slot(a, *block),
                send_sem=send_sems.at[a * 7 + k], recv_sem=recv_sems.at[a * 7 + k],
                device_id=to, device_id_type=MESH)

        own = [pltpu.make_async_copy(srcs[a], slot(a, *me), local_sems.at[a]) for a in range(na)]
        for cp in own:
            cp.start()
        first = []
        for a in range(na):
            first.append(copy(a, 0, me, sibling, True))
            first += [copy(a, 1 + j, me, (*chip, c), True) for j, chip in enumerate(chips)]
        for cp in first:
            cp.start()
        passed = []
        for j, chip in enumerate(chips):
            for a in range(na):
                copy(a, 1 + j, (*chip, c), me).wait_recv()
                fwd = copy(a, 4 + j, (*chip, c), sibling)
                fwd.start()
                passed.append(fwd)
        for a in range(na):
            copy(a, 0, sibling, me).wait_recv()
            for j, chip in enumerate(chips):
                copy(a, 4 + j, (*chip, 1 - c), me).wait_recv()
        for cp in first + passed:
            cp.wait_send()
        for cp in own:
            cp.wait()

    return pl.pallas_call(
        body,
        in_specs=[ANY] * na,
        out_specs=[ANY] * na,
        out_shape=[jax.ShapeDtypeStruct((N_DEV,) + a.shape, a.dtype) for a in arrays],
        scratch_shapes=[pltpu.SemaphoreType.DMA((7 * na,)), pltpu.SemaphoreType.DMA((7 * na,)),
                        pltpu.SemaphoreType.DMA((na,))],
        name="all_gather_weights",
    )(*arrays)


HBM = pl.BlockSpec(memory_space=pltpu.HBM)
SEM = pl.BlockSpec(memory_space=pltpu.SEMAPHORE)
_DATAFLOW = pltpu.SideEffectType.DATAFLOW_SIDE_EFFECTING


def _peer(x, y, c, r):
    return x ^ (r >> 2), y ^ ((r >> 1) & 1), c ^ (r & 1)


def _direct_copies(src_ref, land_ref, send_sems, recv_sems, gather, receiving):
    x, y, c = _coords()
    me = 4 * x + 2 * y + c
    copies = []
    for r in range(1, N_DEV):
        px, py, pc = _peer(x, y, c, r)
        peer = 4 * px + 2 * py + pc
        if gather:
            src, dst = src_ref, land_ref.at[peer if receiving else me]
        else:
            src, dst = src_ref.at[peer], land_ref.at[r - 1]
        copies.append(pltpu.make_async_remote_copy(
            src_ref=src, dst_ref=dst, send_sem=send_sems.at[r - 1], recv_sem=recv_sems.at[r - 1],
            device_id=(px, py, pc), device_id_type=MESH))
    return copies


def _direct_start(src, land, *, gather, name):
    def body(src_ref, land_ref, send_sems, recv_sems, src_thru, land_thru, token):
        del src_thru, land_thru
        for cp in _direct_copies(src_ref, land_ref, send_sems, recv_sems, gather, False):
            cp.start()
        token[...] = jnp.zeros_like(token)

    return pl.pallas_call(
        body,
        name=name,
        out_shape=(pltpu.SemaphoreType.DMA((N_DEV - 1,)), pltpu.SemaphoreType.DMA((N_DEV - 1,)),
                   pltpu.HBM(src.shape, src.dtype), pltpu.HBM(land.shape, land.dtype),
                   jax.ShapeDtypeStruct((8, 128), F32)),
        in_specs=(HBM, HBM),
        out_specs=(SEM, SEM, HBM, HBM, pl.BlockSpec(memory_space=pltpu.VMEM)),
        input_output_aliases={0: 2, 1: 3},
        compiler_params=pltpu.CompilerParams(has_side_effects=_DATAFLOW),
    )(pltpu.with_memory_space_constraint(src, pltpu.HBM), pltpu.with_memory_space_constraint(land, pltpu.HBM))


def _direct_wait(send_sems, recv_sems, src_thru, land_thru, after, *, gather, name):
    def body(src_ref, land_ref, send_sems_ref, recv_sems_ref, after_ref, src_dead, got_ref):
        del after_ref, src_dead, got_ref
        for cp in _direct_copies(src_ref, land_ref, send_sems_ref, recv_sems_ref, gather, True):
            cp.wait_send()
            cp.wait_recv()

    return pl.pallas_call(
        body,
        name=name,
        out_shape=(pltpu.HBM(src_thru.shape, src_thru.dtype), pltpu.HBM(land_thru.shape, land_thru.dtype)),
        in_specs=(HBM, HBM, SEM, SEM, ANY),
        out_specs=(HBM, HBM),
        input_output_aliases={0: 0, 1: 1},
        compiler_params=pltpu.CompilerParams(has_side_effects=_DATAFLOW),
    )(src_thru, land_thru, send_sems, recv_sems, after)


def _sum_partials(src, land, me, *, tr, name):
    R = src.shape[1]

    def body(me_ref, s_ref, l_ref, o_ref):
        del me_ref
        acc = s_ref[0].astype(F32)
        for r in range(N_DEV - 1):
            acc = acc + l_ref[r].astype(F32)
        o_ref[...] = acc

    return pl.pallas_call(
        body,
        grid_spec=pltpu.PrefetchScalarGridSpec(
            num_scalar_prefetch=1, grid=(R // tr,),
            in_specs=[pl.BlockSpec((1, tr, 1024), lambda i, mr: (mr[0], i, 0)),
                      pl.BlockSpec((N_DEV - 1, tr, 1024), lambda i, mr: (0, i, 0))],
            out_specs=pl.BlockSpec((tr, 1024), lambda i, mr: (i, 0))),
        out_shape=jax.ShapeDtypeStruct((R, 1024), F32),
        name=name,
    )(me, src, land)


def _small_all_reduce(packed, lb_logits):
    def body(p_ref, lbl_ref, o_ref, gath, send_sems, recv_sems):
        x, y, c = _coords()
        mine = 4 * x + 2 * y + c
        gath[mine] = p_ref[...]
        copies = []
        for r in range(1, N_DEV):
            peer = (x ^ (r >> 2), y ^ ((r >> 1) & 1), c ^ (r & 1))
            copies.append(pltpu.make_async_remote_copy(
                src_ref=p_ref, dst_ref=gath.at[mine],
                send_sem=send_sems.at[r - 1], recv_sem=recv_sems.at[r - 1],
                device_id=peer, device_id_type=MESH))
        for cp in copies:
            cp.start()
        for r in range(1, N_DEV):
            peer_slot = 4 * (x ^ (r >> 2)) + 2 * (y ^ ((r >> 1) & 1)) + (c ^ (r & 1))
            pltpu.make_async_remote_copy(
                src_ref=p_ref, dst_ref=gath.at[peer_slot],
                send_sem=send_sems.at[r - 1], recv_sem=recv_sems.at[r - 1],
                device_id=(x, y, c), device_id_type=MESH).wait_recv()
        for cp in copies:
            cp.wait_send()
        tot = gath[0]
        for d in range(1, N_DEV):
            tot = tot + gath[d]
        o_ref[...] = tot
        lb = _lower_bound(lbl_ref)
        dl0 = o_ref[SM_LB:SM_LB + 1, :] * lb * (1.0 - lb)
        o_ref[SM_LB:SM_LB + 1, :] = dl0
        o_ref[SM_LB + 1:SM_LB + 2, :] = -dl0

    vm = pl.BlockSpec(memory_space=pltpu.VMEM)
    return pl.pallas_call(
        body,
        in_specs=[vm, vm],
        out_specs=vm,
        out_shape=jax.ShapeDtypeStruct(packed.shape, F32),
        scratch_shapes=[pltpu.VMEM((N_DEV,) + packed.shape, F32),
                        pltpu.SemaphoreType.DMA((N_DEV - 1,)), pltpu.SemaphoreType.DMA((N_DEV - 1,))],
        name="small_all_reduce",
    )(packed, lb_logits)


def _adamw(w, g, m, v, *, tr, name):
    R, C = w.shape

    def body(w_ref, g_ref, m_ref, v_ref, d_ref, nm_ref, nv_ref):
        gv = g_ref[...]
        nm = ADAM_B1 * m_ref[...] + (1.0 - ADAM_B1) * gv
        nv = ADAM_B2 * v_ref[...] + (1.0 - ADAM_B2) * jnp.square(gv)
        m_hat = nm / (1.0 - ADAM_B1 ** ADAM_STEP)
        v_hat = nv / (1.0 - ADAM_B2 ** ADAM_STEP)
        d_ref[...] = -ADAM_LR * (m_hat / (jnp.sqrt(v_hat) + ADAM_EPS) + ADAM_WD * w_ref[...])
        nm_ref[...] = nm
        nv_ref[...] = nv

    spec = pl.BlockSpec((tr, C), lambda i: (i, 0))
    return pl.pallas_call(
        body,
        grid=(R // tr,),
        in_specs=[spec] * 4,
        out_specs=[spec] * 3,
        out_shape=[jax.ShapeDtypeStruct((R, C), F32)] * 3,
        compiler_params=_cparams("parallel"),
        name=name,
    )(w, g, m, v)


def _pack_small(lb, gain, sinks, rel_bias, ln1_g, ln1_b, ln2_g, ln2_b, loss=None):
    pad = lambda a: jnp.pad(a.reshape(1, -1), ((0, 0), (0, D_MODEL - a.size)))
    rows = [lb.reshape(-1, D_MODEL)]
    if rows[0].shape[0] == 1:
        rows.append(jnp.zeros((1, D_MODEL), F32))
    rows += [gain.reshape(1, D_MODEL), pad(sinks), pad(rel_bias), ln1_g.reshape(1, D_MODEL), ln1_b.reshape(1, D_MODEL),
             ln2_g.reshape(1, D_MODEL), ln2_b.reshape(1, D_MODEL),
             pad(jnp.zeros((1,), F32) if loss is None else loss.reshape(1))]
    rows.append(jnp.zeros((SM_ROWS - SM_LOSS - 1, D_MODEL), F32))
    return jnp.concatenate(rows, axis=0)


def _unpack_small(p):
    return dict(
        lb_logits=p[SM_LB:SM_LB + 2], hg_norm_gain=p[SM_GAIN:SM_GAIN + 1], swa_sinks=p[SM_SINK:SM_SINK + 1, :SWA_HEADS],
        rel_bias=p[SM_RB, :NUM_BUCKETS * SWA_HEADS].reshape(NUM_BUCKETS, SWA_HEADS),
        ln1_g=p[SM_L1G:SM_L1G + 1], ln1_b=p[SM_L1B:SM_L1B + 1], ln2_g=p[SM_L2G:SM_L2G + 1], ln2_b=p[SM_L2B:SM_L2B + 1])


_SMALL = ("lb_logits", "hg_norm_gain", "swa_sinks", "rel_bias", "ln1_g", "ln1_b", "ln2_g", "ln2_b")
_WEIGHTS = ("w_in", "lb_logits", "hg_norm_gain", "swa_sinks", "rel_bias", "w_mem_kv", "w_branch_hg", "w_branch_swa",
            "w_branch_mem", "w_out", "ln1_g", "ln1_b", "w_up", "w_down", "ln2_g", "ln2_b")


def kernel(x, mem, w_in, lb_logits, hg_norm_gain, swa_sinks, rel_bias, w_mem_kv, w_branch_hg, w_branch_swa, w_branch_mem, w_out, ln1_g, ln1_b, w_up, w_down, ln2_g, ln2_b, loss_target, m_w_in, m_lb_logits, m_hg_norm_gain, m_swa_sinks, m_rel_bias, m_w_mem_kv, m_w_branch_hg, m_w_branch_swa, m_w_branch_mem, m_w_out, m_ln1_g, m_ln1_b, m_w_up, m_w_down, m_ln2_g, m_ln2_b, v_w_in, v_lb_logits, v_hg_norm_gain, v_swa_sinks, v_rel_bias, v_w_mem_kv, v_w_branch_hg, v_w_branch_swa, v_w_branch_mem, v_w_out, v_ln1_g, v_ln1_b, v_w_up, v_w_down, v_ln2_g, v_ln2_b):
    w = dict(w_in=w_in, lb_logits=lb_logits, hg_norm_gain=hg_norm_gain, swa_sinks=swa_sinks, rel_bias=rel_bias,
             w_mem_kv=w_mem_kv, w_branch_hg=w_branch_hg, w_branch_swa=w_branch_swa, w_branch_mem=w_branch_mem,
             w_out=w_out, ln1_g=ln1_g, ln1_b=ln1_b, w_up=w_up, w_down=w_down, ln2_g=ln2_g, ln2_b=ln2_b)
    mom = dict(w_in=m_w_in, lb_logits=m_lb_logits, hg_norm_gain=m_hg_norm_gain, swa_sinks=m_swa_sinks, rel_bias=m_rel_bias,
               w_mem_kv=m_w_mem_kv, w_branch_hg=m_w_branch_hg, w_branch_swa=m_w_branch_swa, w_branch_mem=m_w_branch_mem,
               w_out=m_w_out, ln1_g=m_ln1_g, ln1_b=m_ln1_b, w_up=m_w_up, w_down=m_w_down, ln2_g=m_ln2_g, ln2_b=m_ln2_b)
    var = dict(w_in=v_w_in, lb_logits=v_lb_logits, hg_norm_gain=v_hg_norm_gain, swa_sinks=v_swa_sinks, rel_bias=v_rel_bias,
               w_mem_kv=v_w_mem_kv, w_branch_hg=v_w_branch_hg, w_branch_swa=v_w_branch_swa, w_branch_mem=v_w_branch_mem,
               w_out=v_w_out, ln1_g=v_ln1_g, ln1_b=v_ln1_b, w_up=v_w_up, w_down=v_w_down, ln2_g=v_ln2_g, ln2_b=v_ln2_b)
    xc, yc, cc = _coords()

    p1 = _bf(w_in[0].T)
    p2 = _bf(jnp.concatenate([w_down[0], w_up[0].T, w_branch_hg[0], w_branch_swa[0], w_branch_mem[0], w_out[0],
                              w_mem_kv[0].T], axis=0))
    me = 4 * xc + 2 * yc + cc
    (g1,) = _all_gather_weights(p1)
    land2 = lax.dynamic_update_slice(lax.empty((N_DEV, R_OTHER, D_MODEL), BF16), p2[None], (me, 0, 0))
    ag2 = _direct_start(p2, land2, gather=True, name="gather_other_weights_start")

    def other_weights(after):
        _, g2 = _direct_wait(*ag2[:4], after, gather=True, name="gather_other_weights_wait")
        full = lambda lo, hi: g2[:, lo:hi].reshape(N_DEV * (hi - lo), D_MODEL)
        return (full(R_KV, R_OTHER), full(R_BH, R_BS), full(R_BS, R_BM), full(R_BM, R_OUT), full(R_OUT, R_KV),
                full(R_UP, R_BH), full(R_DN, R_UP))

    blocks = lambda a: a.reshape(N_DEV, a.shape[0] // N_DEV, D_MODEL)
    started = {}

    def send_other_grads(g):
        part = jnp.concatenate([blocks(g[k]) for k in ("wdn", "wup_t", "wbh", "wbs", "wbm", "wout", "wkv_t")], axis=1)
        started["others"] = _direct_start(part, lax.empty((N_DEV - 1, R_OTHER, D_MODEL), BF16), gather=False,
                                          name="scatter_other_grads_start")
        return started["others"][4]

    def send_win_grad(g):
        started["win"] = _direct_start(blocks(g), lax.empty((N_DEV - 1, IN_SHARD, D_MODEL), BF16), gather=False,
                                       name="scatter_w_in_grad_start")
        return started["win"][4]

    grad_x, small = _local_step(
        x[0], mem[0], loss_target[0], lb_logits, hg_norm_gain, swa_sinks, rel_bias, ln1_g, ln1_b, ln2_g, ln2_b,
        g1.reshape(IN_COLS, D_MODEL), ag2[4], other_weights, send_other_grads, send_win_grad)

    me1 = me.reshape(1).astype(jnp.int32)
    mine2, landed2 = _direct_wait(*started["others"][:4], grad_x, gather=False, name="scatter_other_grads_wait")
    mine1, landed1 = _direct_wait(*started["win"][:4], grad_x, gather=False, name="scatter_w_in_grad_wait")
    gs2 = _sum_partials(mine2, landed2, me1, tr=R_OTHER // 2, name="sum_other_grads")
    gs1 = _sum_partials(mine1, landed1, me1, tr=IN_SHARD // 2, name="sum_w_in_grad")

    grads = dict(
        w_in=gs1.T, w_down=gs2[R_DN:R_UP], w_up=gs2[R_UP:R_BH].T, w_branch_hg=gs2[R_BH:R_BS],
        w_branch_swa=gs2[R_BS:R_BM], w_branch_mem=gs2[R_BM:R_OUT], w_out=gs2[R_OUT:R_KV], w_mem_kv=gs2[R_KV:R_OTHER].T)

    packed = _pack_small(small["d_lb"], small["d_gain"], small["d_sink"], small["d_rb"], small["d_ln1_g"],
                         small["d_ln1_b"], small["d_ln2_g"], small["d_ln2_b"], small["loss"])
    reduced = _small_all_reduce(packed, lb_logits)
    loss = reduced[SM_LOSS, 0]
    grads.update(_unpack_small(reduced))

    delta, new_m, new_v = {}, {}, {}
    for name in _WEIGHTS:
        if name in _SMALL:
            continue
        w2 = w[name][0]
        delta[name], new_m[name], new_v[name] = _adamw(
            w2, grads[name], mom[name][0], var[name][0], tr=w2.shape[0] // 4, name="adamw_" + name)
    sm = lambda d: _pack_small(*[d[k] for k in _SMALL])
    d_s, m_s, v_s = _adamw(sm(w), reduced, sm(mom), sm(var), tr=SM_ROWS, name="adamw_small")
    for dst, src in ((delta, d_s), (new_m, m_s), (new_v, v_s)):
        dst.update(_unpack_small(src))

    def shaped(d, name):
        return d[name].reshape(w[name].shape)

    return (loss, grad_x[None], *[shaped(grads, n) for n in _WEIGHTS], *[shaped(delta, n) for n in _WEIGHTS],
            *[shaped(new_m, n) for n in _WEIGHTS], *[shaped(new_v, n) for n in _WEIGHTS])
```

```python
import functools
import math

import jax
import jax.numpy as jnp
from jax import lax
from jax.experimental import pallas as pl
from jax.experimental.pallas import tpu as pltpu

F32 = jnp.float32
BF16 = jnp.bfloat16

D_MODEL = 1024
MEM_LEN = 256
HG_HEADS = 8
HG_DK = 128
HG_CHUNK = 64
SWA_HEADS = 16
SWA_HEAD_DIM = 64
SWA_BLOCK = 128
SWA_WINDOW = 128
MEM_HEADS = 4
MEM_HEAD_DIM = 256
NUM_BUCKETS = 32
MAX_DISTANCE = 128
D_FF = 4096
LN_EPS = 1e-5
RMS_EPS = 1e-6
ALPHA = 2.0 ** 0.25
N_DEV = 8

C_HQ, C_HF, C_HI, C_HG, C_SQ, C_SK, C_SV, C_MQ, C_GL = 0, 1024, 2048, 3072, 4096, 5120, 5248, 5376, 6400
IN_COLS = 9472
IN_SHARD = IN_COLS // N_DEV

ADAM_LR = 0.001
ADAM_B1 = 0.9
ADAM_B2 = 0.999
ADAM_EPS = 1e-08
ADAM_WD = 0.01
ADAM_STEP = 10

VMEM_LIMIT = 58 * 1024 * 1024

R_DN, R_UP, R_BH, R_BS, R_BM, R_OUT, R_KV, R_OTHER = 0, 512, 1024, 1152, 1280, 1408, 1536, 1792

SM_LB, SM_GAIN, SM_SINK, SM_RB, SM_L1G, SM_L1B, SM_L2G, SM_L2B, SM_LOSS, SM_ROWS = 0, 2, 3, 4, 5, 6, 7, 8, 9, 16


def _bf(v):
    return v.astype(BF16)


def _dot(a, b):
    return jnp.dot(a, b, preferred_element_type=F32)


def _dot_nt(a, b):
    return lax.dot_general(a, b, (((1,), (1,)), ((), ())), preferred_element_type=F32)


def _dot_tn(a, b):
    return lax.dot_general(a, b, (((0,), (0,)), ((), ())), preferred_element_type=F32)


def _sig(v):
    return 1.0 / (1.0 + jnp.exp(-v))


def _cparams(*sem):
    return pltpu.CompilerParams(dimension_semantics=sem, vmem_limit_bytes=VMEM_LIMIT)


def _const_spec(shape):
    nd = len(shape)
    return pl.BlockSpec(shape, lambda *_: (0,) * nd, pipeline_mode=pl.Buffered(1))


def _dep_spec():
    return pl.BlockSpec((8, 128), lambda *_: (0, 0))


def _mm_nt(a, bt, *, tm, tn, out_dtype, name, dep=None):
    M, K = a.shape
    N = bt.shape[0]

    def body(a_ref, b_ref, *rest):
        o_ref = rest[-1]
        o_ref[...] = _dot_nt(_bf(a_ref[...]), _bf(b_ref[...])).astype(o_ref.dtype)

    deps = () if dep is None else (dep,)
    return pl.pallas_call(
        body,
        grid=(N // tn, M // tm),
        in_specs=[pl.BlockSpec((tm, K), lambda j, i: (i, 0)), pl.BlockSpec((tn, K), lambda j, i: (j, 0))]
        + [_dep_spec() for _ in deps],
        out_specs=pl.BlockSpec((tm, tn), lambda j, i: (i, j)),
        out_shape=jax.ShapeDtypeStruct((M, N), out_dtype),
        compiler_params=_cparams("parallel", "parallel"),
        name=name,
    )(a, bt, *deps)


def _mm_tn_resident(a, b, *, tm, kc, name, out_dtype=F32):
    K, M = a.shape
    N = b.shape[1]
    nk = K // kc

    def body(a_ref, b_ref, o_ref):
        acc = jnp.zeros((tm, N), F32)
        for kk in range(nk):
            sl = pl.ds(kk * kc, kc)
            acc = acc + _dot_tn(_bf(a_ref[sl, :]), _bf(b_ref[sl, :]))
        o_ref[...] = acc.astype(o_ref.dtype)

    return pl.pallas_call(
        body,
        grid=(M // tm,),
        in_specs=[pl.BlockSpec((K, tm), lambda i: (0, i)), _const_spec((K, N))],
        out_specs=pl.BlockSpec((tm, N), lambda i: (i, 0)),
        out_shape=jax.ShapeDtypeStruct((M, N), out_dtype),
        compiler_params=_cparams("parallel"),
        name=name,
    )(a, b)


def _mm_nn_pieces(pieces, b, add, dep, *, tm, name):
    M = add.shape[0]
    N = b.shape[1]
    widths = [p.shape[1] for p in pieces]
    n = len(pieces)

    def body(*refs):
        b_ref, add_ref, o_ref = refs[n], refs[n + 1], refs[-1]
        acc = add_ref[...]
        off = 0
        for p_ref, w in zip(refs[:n], widths):
            acc = acc + _dot(p_ref[...], b_ref[off:off + w, :])
            off += w
        o_ref[...] = acc

    return pl.pallas_call(
        body,
        grid=(M // tm,),
        in_specs=[pl.BlockSpec((tm, w), lambda i: (i, 0)) for w in widths]
        + [_const_spec(b.shape), pl.BlockSpec((tm, N), lambda i: (i, 0)), _dep_spec()],
        out_specs=pl.BlockSpec((tm, N), lambda i: (i, 0)),
        out_shape=jax.ShapeDtypeStruct((M, N), F32),
        compiler_params=_cparams("parallel"),
        name=name,
    )(*pieces, b, add, dep)


def _lower_bound(lbl_ref):
    l0 = lbl_ref[0:1, :]
    l1 = lbl_ref[1:2, :]
    mx = jnp.maximum(l0, l1)
    e0 = jnp.exp(l0 - mx)
    e1 = jnp.exp(l1 - mx)
    return e0 / (e0 + e1)


def _tri(lower):
    r = lax.broadcasted_iota(jnp.int32, (HG_CHUNK, HG_CHUNK), 0)
    c = lax.broadcasted_iota(jnp.int32, (HG_CHUNK, HG_CHUNK), 1)
    return (r >= c) if lower else (r <= c)


def _hg_gates(fl, lb):
    sg = _sig(fl)
    f = lb + (1.0 - lb) * sg
    return sg, f, jnp.log(f), 1.0 - f


def _scan_rows(v, reverse=False):
    row = lax.broadcasted_iota(jnp.int32, v.shape, 0)
    s = 1
    while s < HG_CHUNK:
        if reverse:
            v = v + jnp.where(row < HG_CHUNK - s, pltpu.roll(v, HG_CHUNK - s, 0), 0.0)
        else:
            v = v + jnp.where(row >= s, pltpu.roll(v, s, 0), 0.0)
        s *= 2
    return v


def _hgrn_fwd(zmain, lb_logits, *, T):
    S = zmain.shape[0]
    nc = T // HG_CHUNK

    def body(q_ref, f_ref, v_ref, lbl_ref, o_ref, st_ref, state):
        @pl.when(pl.program_id(1) == 0)
        def _():
            state[...] = jnp.zeros_like(state)

        lb = _lower_bound(lbl_ref)
        tril = _tri(True)
        qis, updates, decays, intra = [], [], [], []
        for c in range(nc):
            sl = pl.ds(c * HG_CHUNK, HG_CHUNK)
            _, _, g, k = _hg_gates(f_ref[sl, :], lb)
            b = _scan_rows(g)
            bl = jnp.sum(g, axis=0, keepdims=True)
            qi = _bf(q_ref[sl, :] * jnp.exp(b))
            ki = _bf(k * jnp.exp(-b))
            ko = _bf(k * jnp.exp(bl - b))
            vb = _bf(v_ref[sl, :])
            att = jnp.where(tril, _dot_nt(qi, ki), 0.0)
            intra.append(_dot(_bf(att), vb))
            qis.append(qi)
            updates.append(_dot_tn(vb, ko))
            decays.append(jnp.exp(bl))
        st = state[...]
        for c in range(nc):
            st_ref[0, c] = st
            o_ref[pl.ds(c * HG_CHUNK, HG_CHUNK), :] = intra[c] + _dot_nt(qis[c], _bf(st))
            st = st * decays[c] + updates[c]
        state[...] = st

    col = lambda base: pl.BlockSpec((T, HG_DK), lambda h, t: (t, base + h))
    return pl.pallas_call(
        body,
        grid=(HG_HEADS, S // T),
        in_specs=[col(0), col(8), col(16), pl.BlockSpec((2, HG_DK), lambda h, t: (0, h))],
        out_specs=[
            pl.BlockSpec((T, HG_DK), lambda h, t: (t, h)),
            pl.BlockSpec((1, nc, HG_DK, HG_DK), lambda h, t: (h, t, 0, 0)),
        ],
        out_shape=[
            jax.ShapeDtypeStruct((S, D_MODEL), F32),
            jax.ShapeDtypeStruct((HG_HEADS, S // HG_CHUNK, HG_DK, HG_DK), F32),
        ],
        scratch_shapes=[pltpu.VMEM((HG_DK, HG_DK), F32)],
        compiler_params=_cparams("parallel", "arbitrary"),
        name="hgrn_fwd",
    )(zmain, zmain, zmain, lb_logits)


def _hgrn_bwd(zmain, lb_logits, states, d_o, *, T):
    S = zmain.shape[0]
    nc = T // HG_CHUNK
    nt = S // T

    def body(q_ref, f_ref, v_ref, lbl_ref, st_ref, do_ref, dz_ref, dlb_ref, dstate):
        @pl.when(pl.program_id(1) == 0)
        def _():
            dstate[...] = jnp.zeros_like(dstate)
            dlb_ref[...] = jnp.zeros_like(dlb_ref)

        lb = _lower_bound(lbl_ref)
        tril = _tri(True)
        last_row = lax.broadcasted_iota(jnp.int32, (HG_CHUNK, HG_DK), 0) == HG_CHUNK - 1
        saved = []
        for c in range(nc):
            sl = pl.ds(c * HG_CHUNK, HG_CHUNK)
            sg, f, g, k = _hg_gates(f_ref[sl, :], lb)
            b = _scan_rows(g)
            bl = jnp.sum(g, axis=0, keepdims=True)
            eb = jnp.exp(b)
            enb = jnp.exp(-b)
            eo = jnp.exp(bl - b)
            q_in = q_ref[sl, :] * eb
            k_in = k * enb
            k_out = k * eo
            qi, ki, ko = _bf(q_in), _bf(k_in), _bf(k_out)
            vb = _bf(v_ref[sl, :])
            dob = do_ref[sl, :]
            att = jnp.where(tril, _dot_nt(qi, ki), 0.0)
            d_att = _bf(jnp.where(tril, _dot_nt(dob, vb), 0.0))
            d_kin = _dot_tn(d_att, qi)
            saved.append(dict(
                sg=sg, f=f, eb=eb, enb=enb, eo=eo, ebl=jnp.exp(bl), k_out=k_out, ko=ko, vb=vb, dob=dob,
                d_v=_dot_tn(_bf(att), dob), d_qin=_dot(d_att, ki), d_kin=d_kin,
                qk=(q_in, k_in), d_state=_dot_tn(dob, qi)))
        dst = dstate[...]
        dsts = [None] * nc
        for c in reversed(range(nc)):
            dsts[c] = dst
            dst = dst * saved[c]["ebl"] + saved[c]["d_state"]
        dstate[...] = dst
        dlb = jnp.zeros((1, HG_DK), F32)
        for c in range(nc):
            sl = pl.ds(c * HG_CHUNK, HG_CHUNK)
            s = saved[c]
            q_in, k_in = s["qk"]
            st = st_ref[0, c]
            dstb = _bf(dsts[c])
            d_v = s["d_v"] + _dot_nt(s["ko"], dstb)
            d_qin = s["d_qin"] + _dot(s["dob"], _bf(st))
            d_kout = _dot(s["vb"], dstb)
            d_decay = jnp.sum(dsts[c] * st, axis=0, keepdims=True)
            kk = d_kout * s["k_out"]
            d_b = d_qin * q_in - s["d_kin"] * k_in - kk
            d_bl = jnp.sum(kk, axis=0, keepdims=True) + d_decay * s["ebl"]
            d_g = _scan_rows(d_b + jnp.where(last_row, d_bl, 0.0), reverse=True)
            d_f = d_g / s["f"] - (s["d_kin"] * s["enb"] + d_kout * s["eo"])
            dz_ref[sl, 0:HG_DK] = _bf(d_qin * s["eb"])
            dz_ref[sl, HG_DK:2 * HG_DK] = _bf(d_f * (1.0 - lb) * s["sg"] * (1.0 - s["sg"]))
            dz_ref[sl, 2 * HG_DK:3 * HG_DK] = _bf(d_v)
            dlb = dlb + jnp.sum(d_f * (1.0 - s["sg"]), axis=0, keepdims=True)
        dlb_ref[...] += dlb

    rev = lambda base: pl.BlockSpec((T, HG_DK), lambda h, t: (nt - 1 - t, base + h))
    outc = pl.BlockSpec((T, HG_DK), lambda h, t: (nt - 1 - t, h))
    return pl.pallas_call(
        body,
        grid=(HG_HEADS, nt),
        in_specs=[
            rev(0), rev(8), rev(16),
            pl.BlockSpec((2, HG_DK), lambda h, t: (0, h)),
            pl.BlockSpec((1, nc, HG_DK, HG_DK), lambda h, t: (h, nt - 1 - t, 0, 0)),
            outc,
        ],
        out_specs=[pl.BlockSpec((T, 3 * HG_DK), lambda h, t: (nt - 1 - t, h)),
                   pl.BlockSpec((1, HG_DK), lambda h, t: (0, h))],
        out_shape=[jax.ShapeDtypeStruct((S, 3 * D_MODEL), BF16), jax.ShapeDtypeStruct((1, D_MODEL), F32)],
        scratch_shapes=[pltpu.VMEM((HG_DK, HG_DK), F32)],
        compiler_params=_cparams("parallel", "arbitrary"),
        name="hgrn_bwd",
    )(zmain, zmain, zmain, lb_logits, states, d_o)


def _t5_bucket_table():
    qi = jnp.arange(SWA_BLOCK)[:, None] + SWA_BLOCK
    kj = jnp.arange(2 * SWA_BLOCK)[None, :]
    n = jnp.clip(qi - kj, 0, SWA_WINDOW - 1)
    max_exact = NUM_BUCKETS // 2
    nf = jnp.maximum(n, 1).astype(F32)
    large = max_exact + (jnp.log(nf / max_exact) / math.log(MAX_DISTANCE / max_exact)
                         * (NUM_BUCKETS - max_exact)).astype(jnp.int32)
    large = jnp.minimum(large, NUM_BUCKETS - 1)
    return jnp.where(n < max_exact, n, large).astype(jnp.int32)


def _swa_valid(n):
    qi = lax.broadcasted_iota(jnp.int32, (SWA_BLOCK, 2 * SWA_BLOCK), 0) + SWA_BLOCK
    kj = lax.broadcasted_iota(jnp.int32, (SWA_BLOCK, 2 * SWA_BLOCK), 1)
    dist = qi - kj
    return (dist >= 0) & (dist < SWA_WINDOW) & ((n > 0) | (kj >= SWA_BLOCK))


def _swa_bias_init(bias, bucket_ref, rb_ref):
    bk = bucket_ref[...]
    for h in range(SWA_HEADS):
        def sel(b, acc, h=h):
            return jnp.where(bk == b, rb_ref[b, h], acc)
        bias[h] = lax.fori_loop(0, NUM_BUCKETS, sel, jnp.zeros(bk.shape, F32))


def _lane_halves(t, kv_head):
    lane = lax.broadcasted_iota(jnp.int32, t.shape, 1)
    rolled = pltpu.roll(t, 64, 1)
    zero = jnp.zeros_like(t)
    if kv_head == 0:
        return jnp.where(lane < 64, t, zero), jnp.where(lane >= 64, rolled, zero)
    return jnp.where(lane < 64, rolled, zero), jnp.where(lane >= 64, t, zero)


def _swa_probs(s, bias_h, valid, sink):
    s = jnp.where(valid, s + bias_h, -jnp.inf)
    m = jnp.maximum(jnp.max(s, axis=-1, keepdims=True), sink)
    p = jnp.exp(s - m)
    es = jnp.exp(sink - m)
    inv = 1.0 / (jnp.sum(p, axis=-1, keepdims=True) + es)
    return p * inv, es * inv


def _swa_fwd(zmain, bucket, rel_bias, sinks):
    S = zmain.shape[0]
    nb = S // SWA_BLOCK
    scale = SWA_HEAD_DIM ** -0.5

    def body(q_ref, kvc_ref, kvp_ref, bucket_ref, rb_ref, sk_ref, o_ref, bias):
        n = pl.program_id(0)

        @pl.when(n == 0)
        def _():
            _swa_bias_init(bias, bucket_ref, rb_ref)

        valid = _swa_valid(n)
        kk = _bf(jnp.concatenate([kvp_ref[:, 0:128], kvc_ref[:, 0:128]], axis=0))
        vv = _bf(jnp.concatenate([kvp_ref[:, 128:256], kvc_ref[:, 128:256]], axis=0))
        for kvh in range(2):
            ka, kb = _lane_halves(kk, kvh)
            va, vb = _lane_halves(vv, kvh)
            qst = _bf(jnp.concatenate([q_ref[:, pl.ds((kvh * 4 + jj) * 128, 128)] for jj in range(4)], axis=0) * scale)
            probs = []
            for odd, kx in enumerate((ka, kb)):
                s = _dot_nt(qst, kx)
                parts = []
                for jj in range(4):
                    h = 2 * (kvh * 4 + jj) + odd
                    p, _ = _swa_probs(s[jj * SWA_BLOCK:(jj + 1) * SWA_BLOCK], bias[h], valid, sk_ref[0, h])
                    parts.append(_bf(p))
                probs.append(jnp.concatenate(parts, axis=0))
            ost = _dot(probs[0], va) + _dot(probs[1], vb)
            for jj in range(4):
                o_ref[:, pl.ds((kvh * 4 + jj) * 128, 128)] = ost[jj * SWA_BLOCK:(jj + 1) * SWA_BLOCK]

    smem = pl.BlockSpec(memory_space=pltpu.SMEM)
    return pl.pallas_call(
        body,
        grid=(nb,),
        in_specs=[
            pl.BlockSpec((SWA_BLOCK, 1024), lambda n: (n, C_SQ // 1024)),
            pl.BlockSpec((SWA_BLOCK, 256), lambda n: (n, C_SK // 256)),
            pl.BlockSpec((SWA_BLOCK, 256), lambda n: (jnp.maximum(n - 1, 0), C_SK // 256)),
            _const_spec((SWA_BLOCK, 2 * SWA_BLOCK)), smem, smem,
        ],
        out_specs=pl.BlockSpec((SWA_BLOCK, 1024), lambda n: (n, 0)),
        out_shape=jax.ShapeDtypeStruct((S, 1024), F32),
        scratch_shapes=[pltpu.VMEM((SWA_HEADS, SWA_BLOCK, 2 * SWA_BLOCK), F32)],
        compiler_params=_cparams("arbitrary"),
        name="swa_fwd",
    )(zmain, zmain, zmain, bucket, rel_bias, sinks)


def _swa_bwd(zmain, o_b, d_o, bucket, rel_bias, sinks, dep):
    S = zmain.shape[0]
    nb = S // SWA_BLOCK
    scale = SWA_HEAD_DIM ** -0.5

    def body(q_ref, kvc_ref, kvp_ref, o_ref, do_ref, bucket_ref, rb_ref, sk_ref, dep_ref,
             dq_ref, dkv_ref, drb_ref, dsk_ref, bias, dbias, carry):
        del dep_ref
        n = pl.program_id(0)

        @pl.when(n == 0)
        def _():
            _swa_bias_init(bias, bucket_ref, rb_ref)
            dbias[...] = jnp.zeros_like(dbias)
            carry[...] = jnp.zeros_like(carry)
            dsk_ref[...] = jnp.zeros_like(dsk_ref)

        @pl.when(n < nb)
        def _():
            valid = _swa_valid(n)
            kk = _bf(jnp.concatenate([kvp_ref[:, 0:128], kvc_ref[:, 0:128]], axis=0))
            vv = _bf(jnp.concatenate([kvp_ref[:, 128:256], kvc_ref[:, 128:256]], axis=0))
            lane = lax.broadcasted_iota(jnp.int32, (2 * SWA_BLOCK, 128), 1)
            lane_q = lax.broadcasted_iota(jnp.int32, (4 * SWA_BLOCK, 128), 1)
            dk_parts, dv_parts = [], []
            for kvh in range(2):
                ka, kb = _lane_halves(kk, kvh)
                va, vb = _lane_halves(vv, kvh)
                pair_cols = [pl.ds((kvh * 4 + jj) * 128, 128) for jj in range(4)]
                qst = _bf(jnp.concatenate([q_ref[:, cl] for cl in pair_cols], axis=0) * scale)
                dost = jnp.concatenate([do_ref[:, cl] for cl in pair_cols], axis=0)
                prod = dost.astype(F32) * jnp.concatenate([o_ref[:, cl] for cl in pair_cols], axis=0)
                dq_st = jnp.zeros((4 * SWA_BLOCK, 128), F32)
                zks, zvs = [], []
                for odd, (kx, vx) in enumerate(((ka, va), (kb, vb))):
                    s = _dot_nt(qst, kx)
                    keep = (lane_q >= 64) if odd else (lane_q < 64)
                    delta = jnp.sum(jnp.where(keep, prod, 0.0), axis=-1, keepdims=True)
                    dp = _dot_nt(dost, vx)
                    p_parts, ds_parts = [], []
                    for jj in range(4):
                        h = 2 * (kvh * 4 + jj) + odd
                        rows = slice(jj * SWA_BLOCK, (jj + 1) * SWA_BLOCK)
                        p, ps = _swa_probs(s[rows], bias[h], valid, sk_ref[0, h])
                        ds = p * (dp[rows] - delta[rows])
                        dbias[h] += ds
                        dsk_ref[h] += jnp.broadcast_to(-jnp.sum(ps * delta[rows], axis=0, keepdims=True), (8, 128))
                        p_parts.append(_bf(p))
                        ds_parts.append(_bf(ds))
                    pst = jnp.concatenate(p_parts, axis=0)
                    dsst = jnp.concatenate(ds_parts, axis=0)
                    dq_st = dq_st + _dot(dsst, kx)
                    zks.append(_dot_tn(dsst, qst))
                    zvs.append(_dot_tn(pst, dost))
                for jj in range(4):
                    dq_ref[:, pair_cols[jj]] = _bf(dq_st[jj * SWA_BLOCK:(jj + 1) * SWA_BLOCK] * scale)
                zk = jnp.where(lane < 64, zks[0], zks[1])
                zv = jnp.where(lane < 64, zvs[0], zvs[1])
                dk_parts.append(zk + pltpu.roll(zk, 64, 1))
                dv_parts.append(zv + pltpu.roll(zv, 64, 1))
            dk = jnp.where(lane < 64, dk_parts[0], dk_parts[1])
            dv = jnp.where(lane < 64, dv_parts[0], dv_parts[1])
            dkv = jnp.concatenate([dk, dv], axis=1)
            dkv_ref[...] = _bf(carry[...] + dkv[0:SWA_BLOCK])
            carry[...] = dkv[SWA_BLOCK:]

        @pl.when(n == nb)
        def _():
            dkv_ref[...] = _bf(carry[...])
            bk = bucket_ref[...]

            def per_head(h, _):
                db = dbias[h]

                def per_bucket(b, _):
                    tot = jnp.sum(jnp.where(bk == b, db, 0.0), axis=1, keepdims=True)
                    tot = jnp.sum(tot, axis=0, keepdims=True)
                    drb_ref[h * NUM_BUCKETS + b] = jnp.broadcast_to(tot, (8, 128))
                    return 0

                return lax.fori_loop(0, NUM_BUCKETS, per_bucket, 0)

            lax.fori_loop(0, SWA_HEADS, per_head, 0)

    smem = pl.BlockSpec(memory_space=pltpu.SMEM)
    cur = lambda n: jnp.minimum(n, nb - 1)
    prev = lambda n: jnp.maximum(jnp.minimum(n, nb - 1) - 1, 0)
    return pl.pallas_call(
        body,
        grid=(nb + 1,),
        in_specs=[
            pl.BlockSpec((SWA_BLOCK, 1024), lambda n: (cur(n), C_SQ // 1024)),
            pl.BlockSpec((SWA_BLOCK, 256), lambda n: (cur(n), C_SK // 256)),
            pl.BlockSpec((SWA_BLOCK, 256), lambda n: (prev(n), C_SK // 256)),
            pl.BlockSpec((SWA_BLOCK, 1024), lambda n: (cur(n), 0)),
            pl.BlockSpec((SWA_BLOCK, 1024), lambda n: (cur(n), 0)),
            _const_spec((SWA_BLOCK, 2 * SWA_BLOCK)), smem, smem, _dep_spec(),
        ],
        out_specs=[
            pl.BlockSpec((SWA_BLOCK, 1024), lambda n: (cur(n), 0)),
            pl.BlockSpec((SWA_BLOCK, 256), lambda n: (jnp.maximum(n - 1, 0), 0)),
            pl.BlockSpec((SWA_HEADS * NUM_BUCKETS, 8, 128), lambda n: (0, 0, 0)),
            pl.BlockSpec((SWA_HEADS, 8, 128), lambda n: (0, 0, 0)),
        ],
        out_shape=[
            jax.ShapeDtypeStruct((S, 1024), BF16),
            jax.ShapeDtypeStruct((S, 256), BF16),
            jax.ShapeDtypeStruct((SWA_HEADS * NUM_BUCKETS, 8, 128), F32),
            jax.ShapeDtypeStruct((SWA_HEADS, 8, 128), F32),
        ],
        scratch_shapes=[
            pltpu.VMEM((SWA_HEADS, SWA_BLOCK, 2 * SWA_BLOCK), F32),
            pltpu.VMEM((SWA_HEADS, SWA_BLOCK, 2 * SWA_BLOCK), F32),
            pltpu.VMEM((SWA_BLOCK, 256), F32),
        ],
        compiler_params=_cparams("arbitrary"),
        name="swa_bwd",
    )(zmain, zmain, zmain, o_b, d_o, bucket, rel_bias, sinks, dep)


def _mem_probs(q_ref, k):
    qs = _bf(q_ref[...] * (MEM_HEAD_DIM ** -0.5))
    s = _dot_nt(qs, k)
    e = jnp.exp(s - jnp.max(s, axis=-1, keepdims=True))
    return qs, e / jnp.sum(e, axis=-1, keepdims=True)


def _mem_q_specs(T):
    return [pl.BlockSpec((T, MEM_HEAD_DIM), lambda t, h=h: (t, C_MQ // MEM_HEAD_DIM + h)) for h in range(MEM_HEADS)]


def _mem_fwd(zmain, mkv, *, T):
    S = zmain.shape[0]

    def body(q0, q1, q2, q3, kv_ref, o_ref):
        for h, q_ref in enumerate((q0, q1, q2, q3)):
            cols = pl.ds(h * MEM_HEAD_DIM, MEM_HEAD_DIM)
            _, p = _mem_probs(q_ref, _bf(kv_ref[:, cols]))
            o_ref[:, cols] = _dot(_bf(p), _bf(kv_ref[:, pl.ds(1024 + h * MEM_HEAD_DIM, MEM_HEAD_DIM)]))

    return pl.pallas_call(
        body,
        grid=(S // T,),
        in_specs=_mem_q_specs(T) + [_const_spec((MEM_LEN, 2048))],
        out_specs=pl.BlockSpec((T, 1024), lambda t: (t, 0)),
        out_shape=jax.ShapeDtypeStruct((S, 1024), F32),
        compiler_params=_cparams("parallel"),
        name="mem_fwd",
    )(zmain, zmain, zmain, zmain, mkv)


def _mem_bwd(zmain, mkv, o_c, d_o, *, T):
    S = zmain.shape[0]
    scale = MEM_HEAD_DIM ** -0.5

    def body(q0, q1, q2, q3, kv_ref, o_ref, do_ref, dq_ref, dkv_ref):
        @pl.when(pl.program_id(0) == 0)
        def _():
            dkv_ref[...] = jnp.zeros_like(dkv_ref)

        for h, q_ref in enumerate((q0, q1, q2, q3)):
            cols = pl.ds(h * MEM_HEAD_DIM, MEM_HEAD_DIM)
            vcols = pl.ds(1024 + h * MEM_HEAD_DIM, MEM_HEAD_DIM)
            kb = _bf(kv_ref[:, cols])
            qs, p = _mem_probs(q_ref, kb)
            dob = do_ref[:, cols]
            delta = jnp.sum(dob.astype(F32) * o_ref[:, cols], axis=-1, keepdims=True)
            ds = _bf(p * (_dot_nt(dob, _bf(kv_ref[:, vcols])) - delta))
            dq_ref[:, cols] = _bf(_dot(ds, kb) * scale)
            dkv_ref[:, cols] += _dot_tn(ds, qs)
            dkv_ref[:, vcols] += _dot_tn(_bf(p), dob)

    row = pl.BlockSpec((T, 1024), lambda t: (t, 0))
    return pl.pallas_call(
        body,
        grid=(S // T,),
        in_specs=_mem_q_specs(T) + [_const_spec((MEM_LEN, 2048)), row, row],
        out_specs=[row, pl.BlockSpec((MEM_LEN, 2048), lambda t: (0, 0))],
        out_shape=[jax.ShapeDtypeStruct((S, 1024), BF16), jax.ShapeDtypeStruct((MEM_LEN, 2048), F32)],
        compiler_params=_cparams("arbitrary"),
        name="mem_bwd",
    )(zmain, zmain, zmain, zmain, mkv, o_c, d_o)


def _layer_norm(u):
    mu = jnp.mean(u, axis=-1, keepdims=True)
    xc = u - mu
    rstd = lax.rsqrt(jnp.mean(xc * xc, axis=-1, keepdims=True) + LN_EPS)
    return xc * rstd, rstd


def _layer_norm_bwd(dy, gamma, xhat, rstd):
    dxh = dy * gamma
    return rstd * (dxh - jnp.mean(dxh, axis=-1, keepdims=True) - xhat * jnp.mean(dxh * xhat, axis=-1, keepdims=True))


def _merge_forward(oraw_ref, hg_ref, ob_ref, oc_ref, gl_ref, x_ref, gain_ref, wbh, wbs, wbm, wout):
    ys, rs = [], []
    for h in range(HG_HEADS):
        oh = oraw_ref[:, pl.ds(h * HG_DK, HG_DK)]
        r = lax.rsqrt(jnp.mean(oh * oh, axis=-1, keepdims=True) + RMS_EPS)
        ys.append(oh * r)
        rs.append(r)
    y = jnp.concatenate(ys, axis=1)
    hg = hg_ref[...]
    sg = _sig(hg)
    silu = hg * sg
    oa = _bf(y * gain_ref[...] * silu)
    pa = _dot(oa, wbh[...])
    pb = _dot(_bf(ob_ref[...]), wbs[...])
    pc = _dot(_bf(oc_ref[...]), wbm[...])
    g0 = _sig(gl_ref[:, 0:1024])
    g1 = _sig(gl_ref[:, 1024:2048])
    g2 = _sig(gl_ref[:, 2048:3072])
    m = _bf(g0 * pa + g1 * pb + g2 * pc)
    u1 = ALPHA * x_ref[...] + _dot(m, wout[...])
    xhat, rstd = _layer_norm(u1)
    return dict(y=y, rs=rs, hg=hg, sg=sg, silu=silu, oa=oa, pa=pa, pb=pb, pc=pc,
                g0=g0, g1=g1, g2=g2, m=m, xhat=xhat, rstd=rstd)


def _merge_in_specs(T):
    row = lambda w, c=0: pl.BlockSpec((T, w), lambda i: (i, c))
    vec = pl.BlockSpec((1, D_MODEL), lambda i: (0, 0))
    w = _const_spec((D_MODEL, D_MODEL))
    return [row(1024), row(1024, C_HG // 1024), row(1024), row(1024), row(3072), row(1024), vec, w, w, w, w, vec, vec]


def _merge_fwd(o_raw, zmain, o_b, o_c, gl, x, gain, wbh, wbs, wbm, wout, ln_g, ln_b, *, T):
    S = x.shape[0]

    def body(oraw_ref, hg_ref, ob_ref, oc_ref, gl_ref, x_ref, gain_ref, wbh_r, wbs_r, wbm_r, wout_r, g_ref, b_ref, h1_ref):
        f = _merge_forward(oraw_ref, hg_ref, ob_ref, oc_ref, gl_ref, x_ref, gain_ref, wbh_r, wbs_r, wbm_r, wout_r)
        h1_ref[...] = f["xhat"] * g_ref[...] + b_ref[...]

    return pl.pallas_call(
        body,
        grid=(S // T,),
        in_specs=_merge_in_specs(T),
        out_specs=pl.BlockSpec((T, D_MODEL), lambda i: (i, 0)),
        out_shape=jax.ShapeDtypeStruct((S, D_MODEL), F32),
        compiler_params=_cparams("parallel"),
        name="merge_fwd",
    )(o_raw, zmain, o_b, o_c, gl, x, gain, wbh, wbs, wbm, wout, ln_g, ln_b)


def _merge_bwd(d_h1, o_raw, zmain, o_b, o_c, gl, x, gain, wbh, wbs, wbm, wout, ln_g, ln_b, *, T):
    S = x.shape[0]

    def body(dh1_ref, oraw_ref, hg_ref, ob_ref, oc_ref, gl_ref, x_ref, gain_ref, wbh_r, wbs_r, wbm_r, wout_r, g_ref, b_ref,
             dx_ref, du1_ref, m_ref, oa_ref, dpa_ref, dpb_ref, dpc_ref, doraw_ref, dob_ref, doc_ref, dz_ref,
             dgain_ref, dg_ref, db_ref):
        del b_ref

        @pl.when(pl.program_id(0) == 0)
        def _():
            dgain_ref[...] = jnp.zeros_like(dgain_ref)
            dg_ref[...] = jnp.zeros_like(dg_ref)
            db_ref[...] = jnp.zeros_like(db_ref)

        f = _merge_forward(oraw_ref, hg_ref, ob_ref, oc_ref, gl_ref, x_ref, gain_ref, wbh_r, wbs_r, wbm_r, wout_r)
        dh1 = dh1_ref[...]
        dg_ref[...] += jnp.sum(dh1 * f["xhat"], axis=0, keepdims=True)
        db_ref[...] += jnp.sum(dh1, axis=0, keepdims=True)
        du1 = _layer_norm_bwd(dh1, g_ref[...], f["xhat"], f["rstd"])
        dx_ref[...] = ALPHA * du1
        du1b = _bf(du1)
        du1_ref[...] = du1b
        m_ref[...] = f["m"]
        oa_ref[...] = f["oa"]
        dm = _dot_nt(du1b, wout_r[...])
        for i, (g, p, dp_ref, dob_r, w_r) in enumerate((
                (f["g0"], f["pa"], dpa_ref, None, wbh_r),
                (f["g1"], f["pb"], dpb_ref, dob_ref, wbs_r),
                (f["g2"], f["pc"], dpc_ref, doc_ref, wbm_r))):
            dz_ref[:, pl.ds((i + 1) * 1024, 1024)] = _bf(dm * p * g * (1.0 - g))
            dp = _bf(dm * g)
            dp_ref[...] = dp
            d_branch = _dot_nt(dp, w_r[...])
            if dob_r is not None:
                dob_r[...] = _bf(d_branch)
            else:
                doa = d_branch
        gain = gain_ref[...]
        t = doa * f["y"]
        dgain_ref[...] += jnp.sum(t * f["silu"], axis=0, keepdims=True)
        sg = f["sg"]
        dz_ref[:, 0:1024] = _bf(t * gain * sg * (1.0 + f["hg"] * (1.0 - sg)))
        dy = doa * gain * f["silu"]
        for h in range(HG_HEADS):
            cols = slice(h * HG_DK, (h + 1) * HG_DK)
            yh = f["y"][:, cols]
            dyh = dy[:, cols]
            doraw_ref[:, pl.ds(h * HG_DK, HG_DK)] = _bf(
                f["rs"][h] * (dyh - yh * jnp.mean(dyh * yh, axis=-1, keepdims=True)))

    row = lambda w: pl.BlockSpec((T, w), lambda i: (i, 0))
    vec = pl.BlockSpec((1, D_MODEL), lambda i: (0, 0))
    bshape = jax.ShapeDtypeStruct((S, D_MODEL), BF16)
    vshape = jax.ShapeDtypeStruct((1, D_MODEL), F32)
    return pl.pallas_call(
        body,
        grid=(S // T,),
        in_specs=[row(1024)] + _merge_in_specs(T),
        out_specs=[row(1024)] * 10 + [row(4096), vec, vec, vec],
        out_shape=[jax.ShapeDtypeStruct((S, D_MODEL), F32)] + [bshape] * 9
        + [jax.ShapeDtypeStruct((S, 4096), BF16), vshape, vshape, vshape],
        compiler_params=_cparams("arbitrary"),
        name="merge_bwd",
    )(d_h1, o_raw, zmain, o_b, o_c, gl, x, gain, wbh, wbs, wbm, wout, ln_g, ln_b)


def _mlp_fwd_bwd(h1, target, wup_t, wdn, ln_g, ln_b, *, T, FC):
    S = h1.shape[0]
    nf = D_FF // FC

    def body(h1_ref, t_ref, wup_ref, wdn_ref, g_ref, b_ref, dh1_ref, a_ref, dup_ref, du2_ref, loss_ref, dg_ref, db_ref, up_scr):
        @pl.when(pl.program_id(0) == 0)
        def _():
            loss_ref[...] = jnp.zeros_like(loss_ref)
            dg_ref[...] = jnp.zeros_like(dg_ref)
            db_ref[...] = jnp.zeros_like(db_ref)

        h1v = h1_ref[...]
        h1b = _bf(h1v)
        ff = jnp.zeros((T, D_MODEL), F32)
        for j in range(nf):
            rows = pl.ds(j * FC, FC)
            up = jnp.maximum(_dot_nt(h1b, wup_ref[rows, :]), 0.0)
            up_scr[:, rows] = _bf(up)
            a = _bf(up * up)
            a_ref[:, rows] = a
            ff = ff + _dot(a, wdn_ref[rows, :])
        xhat, rstd = _layer_norm(ALPHA * h1v + ff)
        gamma = g_ref[...]
        err = xhat * gamma + b_ref[...] - t_ref[...]
        loss_ref[...] += jnp.sum(jnp.sum(err * err, axis=-1, keepdims=True), axis=0, keepdims=True) * (0.5 / D_MODEL)
        dy = err * (1.0 / D_MODEL)
        dg_ref[...] += jnp.sum(dy * xhat, axis=0, keepdims=True)
        db_ref[...] += jnp.sum(dy, axis=0, keepdims=True)
        du2 = _layer_norm_bwd(dy, gamma, xhat, rstd)
        du2b = _bf(du2)
        du2_ref[...] = du2b
        dh1 = ALPHA * du2
        for j in range(nf):
            rows = pl.ds(j * FC, FC)
            dup = _bf(_dot_nt(du2b, wdn_ref[rows, :]) * (2.0 * up_scr[:, rows].astype(F32)))
            dup_ref[:, rows] = dup
            dh1 = dh1 + _dot(dup, wup_ref[rows, :])
        dh1_ref[...] = dh1

    row = lambda w: pl.BlockSpec((T, w), lambda i: (i, 0))
    vec = pl.BlockSpec((1, D_MODEL), lambda i: (0, 0))
    vshape = jax.ShapeDtypeStruct((1, D_MODEL), F32)
    return pl.pallas_call(
        body,
        grid=(S // T,),
        in_specs=[row(1024), row(1024), _const_spec((D_FF, D_MODEL)), _const_spec((D_FF, D_MODEL)), vec, vec],
        out_specs=[row(1024), row(D_FF), row(D_FF), row(1024), pl.BlockSpec((8, 128), lambda i: (0, 0)), vec, vec],
        out_shape=[
            jax.ShapeDtypeStruct((S, D_MODEL), F32),
            jax.ShapeDtypeStruct((S, D_FF), BF16),
            jax.ShapeDtypeStruct((S, D_FF), BF16),
            jax.ShapeDtypeStruct((S, D_MODEL), BF16),
            jax.ShapeDtypeStruct((8, 128), F32), vshape, vshape,
        ],
        scratch_shapes=[pltpu.VMEM((T, D_FF), BF16)],
        compiler_params=_cparams("arbitrary"),
        name="mlp_fwd_bwd",
    )(h1, target, wup_t, wdn, ln_g, ln_b)


def _local_step(x, mem, target, lb_logits, gain, sinks, rel_bias, ln1_g, ln1_b, ln2_g, ln2_b,
                win_t, dep0, other_weights, send_other_grads, send_win_grad):
    S = x.shape[0]
    T = min(256, S)
    KC = min(1024, S)
    xb = _bf(x)
    zmain = _mm_nt(x, win_t[:C_GL], tm=min(512, S), tn=3200, out_dtype=F32, name="in_proj_main", dep=dep0)
    gl = _mm_nt(x, win_t[C_GL:], tm=min(512, S), tn=1536, out_dtype=F32, name="in_proj_gates")
    bucket = _t5_bucket_table()

    o_raw, states = _hgrn_fwd(zmain, lb_logits, T=min(512, S))
    o_b = _swa_fwd(zmain, bucket, rel_bias, sinks)
    wkv_t, wbh, wbs, wbm, wout, wup_t, wdn = other_weights(o_b)
    mkv = _mm_nt(mem, wkv_t, tm=MEM_LEN, tn=1024, out_dtype=F32, name="mem_kv_proj")
    o_c = _mem_fwd(zmain, mkv, T=min(512, S))
    merge_args = (o_raw, zmain, o_b, o_c, gl, x, gain, wbh, wbs, wbm, wout, ln1_g, ln1_b)
    h1 = _merge_fwd(*merge_args, T=T)

    d_h1, act, d_up, du2, loss, d_ln2_g, d_ln2_b = _mlp_fwd_bwd(h1, target, wup_t, wdn, ln2_g, ln2_b, T=min(512, S), FC=512)
    wgrad = functools.partial(_mm_tn_resident, tm=256, out_dtype=BF16)
    g_wdn = wgrad(act, du2, kc=KC, name="grad_w_down")
    g_wup_t = wgrad(d_up, _bf(h1), kc=KC, name="grad_w_up")

    (dx_part, du1, m, oa, dpa, dpb, dpc, d_oraw, d_ob, d_oc, d_hg_gl,
     d_gain, d_ln1_g, d_ln1_b) = _merge_bwd(d_h1, *merge_args, T=T)
    g_wout = wgrad(m, du1, kc=KC, name="grad_w_out")
    g_wbh = wgrad(oa, dpa, kc=KC, name="grad_w_branch_hg")
    g_wbs = wgrad(o_b, dpb, kc=KC, name="grad_w_branch_swa")
    g_wbm = wgrad(o_c, dpc, kc=KC, name="grad_w_branch_mem")

    d_mq, d_mkv = _mem_bwd(zmain, mkv, o_c, d_oc, T=min(512, S))
    g_wkv_t = wgrad(d_mkv, mem, kc=MEM_LEN, name="grad_w_mem_kv")
    sent_others = send_other_grads(
        dict(wkv_t=g_wkv_t, wbh=g_wbh, wbs=g_wbs, wbm=g_wbm, wout=g_wout, wup_t=g_wup_t, wdn=g_wdn))
    d_sq, d_skv, d_rb, d_sink = _swa_bwd(zmain, o_b, d_ob, bucket, rel_bias, sinks, sent_others)
    d_qfv, d_lb = _hgrn_bwd(zmain, lb_logits, states, d_oraw, T=min(512, S))

    head_major = lambda a: a.reshape(3, HG_HEADS, HG_DK, D_MODEL).transpose(1, 0, 2, 3).reshape(3 * D_MODEL, D_MODEL)
    col_major = lambda a: a.reshape(HG_HEADS, 3, HG_DK, D_MODEL).transpose(1, 0, 2, 3).reshape(3 * D_MODEL, D_MODEL)
    pieces = (d_qfv, d_hg_gl, d_sq, d_skv, d_mq)
    g_qfv, g_hg_gl, g_sq, g_skv, g_mq = [
        wgrad(p, xb, kc=KC, name="grad_w_in_" + n) for p, n in zip(pieces, ("qfv", "hg_gates", "swa_q", "swa_kv", "mem_q"))]
    g_win_t = jnp.concatenate([col_major(g_qfv), g_hg_gl[:D_MODEL], g_sq, g_skv, g_mq, g_hg_gl[D_MODEL:]], axis=0)
    sent_win = send_win_grad(g_win_t)
    w_pieces = jnp.concatenate([head_major(win_t[:C_HG]), win_t[C_HG:C_SQ], win_t[C_GL:], win_t[C_SQ:C_GL]], axis=0)
    grad_x = _mm_nn_pieces(pieces, w_pieces, dx_part, sent_win, tm=T, name="grad_x")

    small = dict(
        d_lb=d_lb, d_gain=d_gain, d_sink=d_sink[:, 0, 0].reshape(1, SWA_HEADS),
        d_rb=d_rb[:, 0, 0].reshape(SWA_HEADS, NUM_BUCKETS).T,
        d_ln1_g=d_ln1_g, d_ln1_b=d_ln1_b, d_ln2_g=d_ln2_g, d_ln2_b=d_ln2_b, loss=loss[0, 0])
    return grad_x, small


MESH = pl.DeviceIdType.MESH
ANY = pl.BlockSpec(memory_space=pl.ANY)


def _coords():
    return lax.axis_index("x"), lax.axis_index("y"), lax.axis_index("c")


def _other_chips(x, y):
    return [(1 - x, y), (x, 1 - y), (1 - x, 1 - y)]


def _all_gather_weights(*arrays):
    na = len(arrays)

    def body(*refs):
        srcs, dsts = refs[:na], refs[na:2 * na]
        send_sems, recv_sems, local_sems = refs[2 * na:]
        x, y, c = _coords()
        me, sibling = (x, y, c), (x, y, 1 - c)
        chips = _other_chips(x, y)

        def slot(a, px, py, pc):
            return dsts[a].at[4 * px + 2 * py + pc]

        def copy(a, k, block, to, from_shard=False):
            return pltpu.make_async_remote_copy(
                src_ref=srcs[a] if from_shard else slot(a, *block), dst_ref=slot(a, *block),
                send_sem=send_sems.at[a * 7 + k], recv_sem=recv_sems.at[a * 7 + k],
                device_id=to, device_id_type=MESH)

        own = [pltpu.make_async_copy(srcs[a], slot(a, *me), local_sems.at[a]) for a in range(na)]
        for cp in own:
            cp.start()
        first = []
        for a in range(na):
            first.append(copy(a, 0, me, sibling, True))
            first += [copy(a, 1 + j, me, (*chip, c), True) for j, chip in enumerate(chips)]
        for cp in first:
            cp.start()
        passed = []
        for j, chip in enumerate(chips):
            for a in range(na):
                copy(a, 1 + j, (*chip, c), me).wait_recv()
                fwd = copy(a, 4 + j, (*chip, c), sibling)
                fwd.start()
                passed.append(fwd)
        for a in range(na):
            copy(a, 0, sibling, me).wait_recv()
            for j, chip in enumerate(chips):
                copy(a, 4 + j, (*chip, 1 - c), me).wait_recv()
        for cp in first + passed:
            cp.wait_send()
        for cp in own:
            cp.wait()

    return pl.pallas_call(
        body,
        in_specs=[ANY] * na,
        out_specs=[ANY] * na,
        out_shape=[jax.ShapeDtypeStruct((N_DEV,) + a.shape, a.dtype) for a in arrays],
        scratch_shapes=[pltpu.SemaphoreType.DMA((7 * na,)), pltpu.SemaphoreType.DMA((7 * na,)),
                        pltpu.SemaphoreType.DMA((na,))],
        name="all_gather_weights",
    )(*arrays)


HBM = pl.BlockSpec(memory_space=pltpu.HBM)
SEM = pl.BlockSpec(memory_space=pltpu.SEMAPHORE)
_DATAFLOW = pltpu.SideEffectType.DATAFLOW_SIDE_EFFECTING


def _peer(x, y, c, r):
    return x ^ (r >> 2), y ^ ((r >> 1) & 1), c ^ (r & 1)


def _direct_copies(src_ref, land_ref, send_sems, recv_sems, gather, receiving):
    x, y, c = _coords()
    me = 4 * x + 2 * y + c
    copies = []
    for r in range(1, N_DEV):
        px, py, pc = _peer(x, y, c, r)
        peer = 4 * px + 2 * py + pc
        if gather:
            src, dst = src_ref, land_ref.at[peer if receiving else me]
        else:
            src, dst = src_ref.at[peer], land_ref.at[r - 1]
        copies.append(pltpu.make_async_remote_copy(
            src_ref=src, dst_ref=dst, send_sem=send_sems.at[r - 1], recv_sem=recv_sems.at[r - 1],
            device_id=(px, py, pc), device_id_type=MESH))
    return copies


def _direct_start(src, land, *, gather, name):
    def body(src_ref, land_ref, send_sems, recv_sems, src_thru, land_thru, token):
        del src_thru, land_thru
        for cp in _direct_copies(src_ref, land_ref, send_sems, recv_sems, gather, False):
            cp.start()
        token[...] = jnp.zeros_like(token)

    return pl.pallas_call(
        body,
        name=name,
        out_shape=(pltpu.SemaphoreType.DMA((N_DEV - 1,)), pltpu.SemaphoreType.DMA((N_DEV - 1,)),
                   pltpu.HBM(src.shape, src.dtype), pltpu.HBM(land.shape, land.dtype),
                   jax.ShapeDtypeStruct((8, 128), F32)),
        in_specs=(HBM, HBM),
        out_specs=(SEM, SEM, HBM, HBM, pl.BlockSpec(memory_space=pltpu.VMEM)),
        input_output_aliases={0: 2, 1: 3},
        compiler_params=pltpu.CompilerParams(has_side_effects=_DATAFLOW),
    )(pltpu.with_memory_space_constraint(src, pltpu.HBM), pltpu.with_memory_space_constraint(land, pltpu.HBM))


def _direct_wait(send_sems, recv_sems, src_thru, land_thru, after, *, gather, name):
    def body(src_ref, land_ref, send_sems_ref, recv_sems_ref, after_ref, src_dead, got_ref):
        del after_ref, src_dead, got_ref
        for cp in _direct_copies(src_ref, land_ref, send_sems_ref, recv_sems_ref, gather, True):
            cp.wait_send()
            cp.wait_recv()

    return pl.pallas_call(
        body,
        name=name,
        out_shape=(pltpu.HBM(src_thru.shape, src_thru.dtype), pltpu.HBM(land_thru.shape, land_thru.dtype)),
        in_specs=(HBM, HBM, SEM, SEM, ANY),
        out_specs=(HBM, HBM),
        input_output_aliases={0: 0, 1: 1},
        compiler_params=pltpu.CompilerParams(has_side_effects=_DATAFLOW),
    )(src_thru, land_thru, send_sems, recv_sems, after)


def _sum_partials(src, land, me, *, tr, name):
    R = src.shape[1]

    def body(me_ref, s_ref, l_ref, o_ref):
        del me_ref
        acc = s_ref[0].astype(F32)
        for r in range(N_DEV - 1):
            acc = acc + l_ref[r].astype(F32)
        o_ref[...] = acc

    return pl.pallas_call(
        body,
        grid_spec=pltpu.PrefetchScalarGridSpec(
            num_scalar_prefetch=1, grid=(R // tr,),
            in_specs=[pl.BlockSpec((1, tr, 1024), lambda i, mr: (mr[0], i, 0)),
                      pl.BlockSpec((N_DEV - 1, tr, 1024), lambda i, mr: (0, i, 0))],
            out_specs=pl.BlockSpec((tr, 1024), lambda i, mr: (i, 0))),
        out_shape=jax.ShapeDtypeStruct((R, 1024), F32),
        name=name,
    )(me, src, land)


def _small_all_reduce(packed, lb_logits):
    def body(p_ref, lbl_ref, o_ref, gath, send_sems, recv_sems):
        x, y, c = _coords()
        mine = 4 * x + 2 * y + c
        gath[mine] = p_ref[...]
        copies = []
        for r in range(1, N_DEV):
            peer = (x ^ (r >> 2), y ^ ((r >> 1) & 1), c ^ (r & 1))
            copies.append(pltpu.make_async_remote_copy(
                src_ref=p_ref, dst_ref=gath.at[mine],
                send_sem=send_sems.at[r - 1], recv_sem=recv_sems.at[r - 1],
                device_id=peer, device_id_type=MESH))
        for cp in copies:
            cp.start()
        for r in range(1, N_DEV):
            peer_slot = 4 * (x ^ (r >> 2)) + 2 * (y ^ ((r >> 1) & 1)) + (c ^ (r & 1))
            pltpu.make_async_remote_copy(
                src_ref=p_ref, dst_ref=gath.at[peer_slot],
                send_sem=send_sems.at[r - 1], recv_sem=recv_sems.at[r - 1],
                device_id=(x, y, c), device_id_type=MESH).wait_recv()
        for cp in copies:
            cp.wait_send()
        tot = gath[0]
        for d in range(1, N_DEV):
            tot = tot + gath[d]
        o_ref[...] = tot
        lb = _lower_bound(lbl_ref)
        dl0 = o_ref[SM_LB:SM_LB + 1, :] * lb * (1.0 - lb)
        o_ref[SM_LB:SM_LB + 1, :] = dl0
        o_ref[SM_LB + 1:SM_LB + 2, :] = -dl0

    vm = pl.BlockSpec(memory_space=pltpu.VMEM)
    return pl.pallas_call(
        body,
        in_specs=[vm, vm],
        out_specs=vm,
        out_shape=jax.ShapeDtypeStruct(packed.shape, F32),
        scratch_shapes=[pltpu.VMEM((N_DEV,) + packed.shape, F32),
                        pltpu.SemaphoreType.DMA((N_DEV - 1,)), pltpu.SemaphoreType.DMA((N_DEV - 1,))],
        name="small_all_reduce",
    )(packed, lb_logits)


def _adamw(w, g, m, v, *, tr, name):
    R, C = w.shape

    def body(w_ref, g_ref, m_ref, v_ref, d_ref, nm_ref, nv_ref):
        gv = g_ref[...]
        nm = ADAM_B1 * m_ref[...] + (1.0 - ADAM_B1) * gv
        nv = ADAM_B2 * v_ref[...] + (1.0 - ADAM_B2) * jnp.square(gv)
        m_hat = nm / (1.0 - ADAM_B1 ** ADAM_STEP)
        v_hat = nv / (1.0 - ADAM_B2 ** ADAM_STEP)
        d_ref[...] = -ADAM_LR * (m_hat / (jnp.sqrt(v_hat) + ADAM_EPS) + ADAM_WD * w_ref[...])
        nm_ref[...] = nm
        nv_ref[...] = nv

    spec = pl.BlockSpec((tr, C), lambda i: (i, 0))
    return pl.pallas_call(
        body,
        grid=(R // tr,),
        in_specs=[spec] * 4,
        out_specs=[spec] * 3,
        out_shape=[jax.ShapeDtypeStruct((R, C), F32)] * 3,
        compiler_params=_cparams("parallel"),
        name=name,
    )(w, g, m, v)


def _pack_small(lb, gain, sinks, rel_bias, ln1_g, ln1_b, ln2_g, ln2_b, loss=None):
    pad = lambda a: jnp.pad(a.reshape(1, -1), ((0, 0), (0, D_MODEL - a.size)))
    rows = [lb.reshape(-1, D_MODEL)]
    if rows[0].shape[0] == 1:
        rows.append(jnp.zeros((1, D_MODEL), F32))
    rows += [gain.reshape(1, D_MODEL), pad(sinks), pad(rel_bias), ln1_g.reshape(1, D_MODEL), ln1_b.reshape(1, D_MODEL),
             ln2_g.reshape(1, D_MODEL), ln2_b.reshape(1, D_MODEL),
             pad(jnp.zeros((1,), F32) if loss is None else loss.reshape(1))]
    rows.append(jnp.zeros((SM_ROWS - SM_LOSS - 1, D_MODEL), F32))
    return jnp.concatenate(rows, axis=0)


def _unpack_small(p):
    return dict(
        lb_logits=p[SM_LB:SM_LB + 2], hg_norm_gain=p[SM_GAIN:SM_GAIN + 1], swa_sinks=p[SM_SINK:SM_SINK + 1, :SWA_HEADS],
        rel_bias=p[SM_RB, :NUM_BUCKETS * SWA_HEADS].reshape(NUM_BUCKETS, SWA_HEADS),
        ln1_g=p[SM_L1G:SM_L1G + 1], ln1_b=p[SM_L1B:SM_L1B + 1], ln2_g=p[SM_L2G:SM_L2G + 1], ln2_b=p[SM_L2B:SM_L2B + 1])


_SMALL = ("lb_logits", "hg_norm_gain", "swa_sinks", "rel_bias", "ln1_g", "ln1_b", "ln2_g", "ln2_b")
_WEIGHTS = ("w_in", "lb_logits", "hg_norm_gain", "swa_sinks", "rel_bias", "w_mem_kv", "w_branch_hg", "w_branch_swa",
            "w_branch_mem", "w_out", "ln1_g", "ln1_b", "w_up", "w_down", "ln2_g", "ln2_b")


def kernel(x, mem, w_in, lb_logits, hg_norm_gain, swa_sinks, rel_bias, w_mem_kv, w_branch_hg, w_branch_swa, w_branch_mem, w_out, ln1_g, ln1_b, w_up, w_down, ln2_g, ln2_b, loss_target, m_w_in, m_lb_logits, m_hg_norm_gain, m_swa_sinks, m_rel_bias, m_w_mem_kv, m_w_branch_hg, m_w_branch_swa, m_w_branch_mem, m_w_out, m_ln1_g, m_ln1_b, m_w_up, m_w_down, m_ln2_g, m_ln2_b, v_w_in, v_lb_logits, v_hg_norm_gain, v_swa_sinks, v_rel_bias, v_w_mem_kv, v_w_branch_hg, v_w_branch_swa, v_w_branch_mem, v_w_out, v_ln1_g, v_ln1_b, v_w_up, v_w_down, v_ln2_g, v_ln2_b):
    w = dict(w_in=w_in, lb_logits=lb_logits, hg_norm_gain=hg_norm_gain, swa_sinks=swa_sinks, rel_bias=rel_bias,
             w_mem_kv=w_mem_kv, w_branch_hg=w_branch_hg, w_branch_swa=w_branch_swa, w_branch_mem=w_branch_mem,
             w_out=w_out, ln1_g=ln1_g, ln1_b=ln1_b, w_up=w_up, w_down=w_down, ln2_g=ln2_g, ln2_b=ln2_b)
    mom = dict(w_in=m_w_in, lb_logits=m_lb_logits, hg_norm_gain=m_hg_norm_gain, swa_sinks=m_swa_sinks, rel_bias=m_rel_bias,
               w_mem_kv=m_w_mem_kv, w_branch_hg=m_w_branch_hg, w_branch_swa=m_w_branch_swa, w_branch_mem=m_w_branch_mem,
               w_out=m_w_out, ln1_g=m_ln1_g, ln1_b=m_ln1_b, w_up=m_w_up, w_down=m_w_down, ln2_g=m_ln2_g, ln2_b=m_ln2_b)
    var = dict(w_in=v_w_in, lb_logits=v_lb_logits, hg_norm_gain=v_hg_norm_gain, swa_sinks=v_swa_sinks, rel_bias=v_rel_bias,
               w_mem_kv=v_w_mem_kv, w_branch_hg=v_w_branch_hg, w_branch_swa=v_w_branch_swa, w_branch_mem=v_w_branch_mem,
               w_out=v_w_out, ln1_g=v_ln1_g, ln1_b=v_ln1_b, w_up=v_w_up, w_down=v_w_down, ln2_g=v_ln2_g, ln2_b=v_ln2_b)
    xc, yc, cc = _coords()

    p1 = _bf(w_in[0].T)
    p2 = _bf(jnp.concatenate([w_down[0], w_up[0].T, w_branch_hg[0], w_branch_swa[0], w_branch_mem[0], w_out[0],
                              w_mem_kv[0].T], axis=0))
    me = 4 * xc + 2 * yc + cc
    (g1,) = _all_gather_weights(p1)
    land2 = lax.dynamic_update_slice(lax.empty((N_DEV, R_OTHER, D_MODEL), BF16), p2[None], (me, 0, 0))
    ag2 = _direct_start(p2, land2, gather=True, name="gather_other_weights_start")

    def other_weights(after):
        _, g2 = _direct_wait(*ag2[:4], after, gather=True, name="gather_other_weights_wait")
        full = lambda lo, hi: g2[:, lo:hi].reshape(N_DEV * (hi - lo), D_MODEL)
        return (full(R_KV, R_OTHER), full(R_BH, R_BS), full(R_BS, R_BM), full(R_BM, R_OUT), full(R_OUT, R_KV),
                full(R_UP, R_BH), full(R_DN, R_UP))

    blocks = lambda a: a.reshape(N_DEV, a.shape[0] // N_DEV, D_MODEL)
    started = {}

    def send_other_grads(g):
        part = jnp.concatenate([blocks(g[k]) for k in ("wdn", "wup_t", "wbh", "wbs", "wbm", "wout", "wkv_t")], axis=1)
        started["others"] = _direct_start(part, lax.empty((N_DEV - 1, R_OTHER, D_MODEL), BF16), gather=False,
                                          name="scatter_other_grads_start")
        return started["others"][4]

    def send_win_grad(g):
        started["win"] = _direct_start(blocks(g), lax.empty((N_DEV - 1, IN_SHARD, D_MODEL), BF16), gather=False,
                                       name="scatter_w_in_grad_start")
        return started["win"][4]

    grad_x, small = _local_step(
        x[0], mem[0], loss_target[0], lb_logits, hg_norm_gain, swa_sinks, rel_bias, ln1_g, ln1_b, ln2_g, ln2_b,
        g1.reshape(IN_COLS, D_MODEL), ag2[4], other_weights, send_other_grads, send_win_grad)

    me1 = me.reshape(1).astype(jnp.int32)
    mine2, landed2 = _direct_wait(*started["others"][:4], grad_x, gather=False, name="scatter_other_grads_wait")
    mine1, landed1 = _direct_wait(*started["win"][:4], grad_x, gather=False, name="scatter_w_in_grad_wait")
    gs2 = _sum_partials(mine2, landed2, me1, tr=R_OTHER // 2, name="sum_other_grads")
    gs1 = _sum_partials(mine1, landed1, me1, tr=IN_SHARD // 2, name="sum_w_in_grad")

    grads = dict(
        w_in=gs1.T, w_down=gs2[R_DN:R_UP], w_up=gs2[R_UP:R_BH].T, w_branch_hg=gs2[R_BH:R_BS],
        w_branch_swa=gs2[R_BS:R_BM], w_branch_mem=gs2[R_BM:R_OUT], w_out=gs2[R_OUT:R_KV], w_mem_kv=gs2[R_KV:R_OTHER].T)

    packed = _pack_small(small["d_lb"], small["d_gain"], small["d_sink"], small["d_rb"], small["d_ln1_g"],
                         small["d_ln1_b"], small["d_ln2_g"], small["d_ln2_b"], small["loss"])
    reduced = _small_all_reduce(packed, lb_logits)
    loss = reduced[SM_LOSS, 0]
    grads.update(_unpack_small(reduced))

    delta, new_m, new_v = {}, {}, {}
    for name in _WEIGHTS:
        if name in _SMALL:
            continue
        w2 = w[name][0]
        delta[name], new_m[name], new_v[name] = _adamw(
            w2, grads[name], mom[name][0], var[name][0], tr=w2.shape[0] // 4, name="adamw_" + name)
    sm = lambda d: _pack_small(*[d[k] for k in _SMALL])
    d_s, m_s, v_s = _adamw(sm(w), reduced, sm(mom), sm(var), tr=SM_ROWS, name="adamw_small")
    for dst, src in ((delta, d_s), (new_m, m_s), (new_v, v_s)):
        dst.update(_unpack_small(src))

    def shaped(d, name):
        return d[name].reshape(w[name].shape)

    return (loss, grad_x[None], *[shaped(grads, n) for n in _WEIGHTS], *[shaped(delta, n) for n in _WEIGHTS],
            *[shaped(new_m, n) for n in _WEIGHTS], *[shaped(new_v, n) for n in _WEIGHTS])
```

```python
import functools
import math

import jax
import jax.numpy as jnp
from jax import lax
from jax.experimental import pallas as pl
from jax.experimental.pallas import tpu as pltpu

F32 = jnp.float32
BF16 = jnp.bfloat16

D_MODEL = 1024
MEM_LEN = 256
HG_HEADS = 8
HG_DK = 128
HG_CHUNK = 64
SWA_HEADS = 16
SWA_HEAD_DIM = 64
SWA_BLOCK = 128
SWA_WINDOW = 128
MEM_HEADS = 4
MEM_HEAD_DIM = 256
NUM_BUCKETS = 32
MAX_DISTANCE = 128
D_FF = 4096
LN_EPS = 1e-5
RMS_EPS = 1e-6
ALPHA = 2.0 ** 0.25
N_DEV = 8

C_HQ, C_HF, C_HI, C_HG, C_SQ, C_SK, C_SV, C_MQ, C_GL = 0, 1024, 2048, 3072, 4096, 5120, 5248, 5376, 6400
IN_COLS = 9472
IN_SHARD = IN_COLS // N_DEV

ADAM_LR = 0.001
ADAM_B1 = 0.9
ADAM_B2 = 0.999
ADAM_EPS = 1e-08
ADAM_WD = 0.01
ADAM_STEP = 10

VMEM_LIMIT = 58 * 1024 * 1024

R_DN, R_UP, R_BH, R_BS, R_BM, R_OUT, R_KV, R_OTHER = 0, 512, 1024, 1152, 1280, 1408, 1536, 1792

SM_LB, SM_GAIN, SM_SINK, SM_RB, SM_L1G, SM_L1B, SM_L2G, SM_L2B, SM_LOSS, SM_ROWS = 0, 2, 3, 4, 5, 6, 7, 8, 9, 16


def _bf(v):
    return v.astype(BF16)


def _f32(v):
    return v.astype(F32)


def _dot(a, b):
    return jnp.dot(a, b, preferred_element_type=F32)


def _dot_nt(a, b):
    return lax.dot_general(a, b, (((1,), (1,)), ((), ())), preferred_element_type=F32)


def _dot_tn(a, b):
    return lax.dot_general(a, b, (((0,), (0,)), ((), ())), preferred_element_type=F32)


def _sig(v):
    return 1.0 / (1.0 + jnp.exp(-v))


def _cparams(*sem):
    return pltpu.CompilerParams(dimension_semantics=sem, vmem_limit_bytes=VMEM_LIMIT)


def _const_spec(shape):
    nd = len(shape)
    return pl.BlockSpec(shape, lambda *_: (0,) * nd, pipeline_mode=pl.Buffered(1))


def _dep_spec():
    return pl.BlockSpec((8, 128), lambda *_: (0, 0))


def _mm_nt(a, bt, *, tm, tn, out_dtype, name, dep=None, n_cols=None):
    M, K = a.shape
    N = bt.shape[0] if n_cols is None else n_cols

    def body(a_ref, b_ref, *rest):
        o_ref = rest[-1]
        o_ref[...] = _dot_nt(_bf(a_ref[...]), _bf(b_ref[...])).astype(o_ref.dtype)

    deps = () if dep is None else (dep,)
    return pl.pallas_call(
        body,
        grid=(N // tn, M // tm),
        in_specs=[pl.BlockSpec((tm, K), lambda j, i: (i, 0)), pl.BlockSpec((tn, K), lambda j, i: (j, 0))]
        + [_dep_spec() for _ in deps],
        out_specs=pl.BlockSpec((tm, tn), lambda j, i: (i, j)),
        out_shape=jax.ShapeDtypeStruct((M, N), out_dtype),
        compiler_params=_cparams("parallel", "parallel"),
        name=name,
    )(a, bt, *deps)


def _mm_tn_resident(a, b, *, tm, kc, name, out_dtype=F32):
    K, M = a.shape
    N = b.shape[1]
    nk = K // kc

    def body(a_ref, b_ref, o_ref):
        acc = jnp.zeros((tm, N), F32)
        for kk in range(nk):
            sl = pl.ds(kk * kc, kc)
            acc = acc + _dot_tn(_bf(a_ref[sl, :]), _bf(b_ref[sl, :]))
        o_ref[...] = acc.astype(o_ref.dtype)

    return pl.pallas_call(
        body,
        grid=(M // tm,),
        in_specs=[pl.BlockSpec((K, tm), lambda i: (0, i)), _const_spec((K, N))],
        out_specs=pl.BlockSpec((tm, N), lambda i: (i, 0)),
        out_shape=jax.ShapeDtypeStruct((M, N), out_dtype),
        compiler_params=_cparams("parallel"),
        name=name,
    )(a, b)


def _grad_x(d_qfv, d_hg_gl, d_sq, d_skv, d_mq, w_qfv, win_t, add, dep, *, tm):
    M = add.shape[0]
    pieces = (d_qfv, d_hg_gl, d_sq, d_skv, d_mq)

    def body(qfv_ref, hggl_ref, sq_ref, skv_ref, mq_ref, wq_ref, w_ref, add_ref, dep_ref, o_ref):
        del dep_ref
        acc = add_ref[...] + _dot(qfv_ref[...], wq_ref[...])
        acc = acc + _dot(hggl_ref[:, 0:1024], w_ref[C_HG:C_SQ, :])
        acc = acc + _dot(hggl_ref[:, 1024:4096], w_ref[C_GL:IN_COLS, :])
        acc = acc + _dot(sq_ref[...], w_ref[C_SQ:C_SK, :])
        acc = acc + _dot(skv_ref[...], w_ref[C_SK:C_MQ, :])
        o_ref[...] = acc + _dot(mq_ref[...], w_ref[C_MQ:C_GL, :])

    return pl.pallas_call(
        body,
        grid=(M // tm,),
        in_specs=[pl.BlockSpec((tm, p.shape[1]), lambda i: (i, 0)) for p in pieces]
        + [_const_spec(w_qfv.shape), _const_spec(win_t.shape), pl.BlockSpec((tm, D_MODEL), lambda i: (i, 0)), _dep_spec()],
        out_specs=pl.BlockSpec((tm, D_MODEL), lambda i: (i, 0)),
        out_shape=jax.ShapeDtypeStruct((M, D_MODEL), F32),
        compiler_params=_cparams("parallel"),
        name="grad_x",
    )(*pieces, w_qfv, win_t, add, dep)


def _lower_bound(lbl_ref):
    l0 = lbl_ref[0:1, :]
    l1 = lbl_ref[1:2, :]
    mx = jnp.maximum(l0, l1)
    e0 = jnp.exp(l0 - mx)
    e1 = jnp.exp(l1 - mx)
    return e0 / (e0 + e1)


def _tri(lower):
    r = lax.broadcasted_iota(jnp.int32, (HG_CHUNK, HG_CHUNK), 0)
    c = lax.broadcasted_iota(jnp.int32, (HG_CHUNK, HG_CHUNK), 1)
    return (r >= c) if lower else (r <= c)


def _hg_gates(fl, lb):
    sg = _sig(fl)
    f = lb + (1.0 - lb) * sg
    return sg, f, jnp.log(f), 1.0 - f


def _scan_rows(v, reverse=False):
    row = lax.broadcasted_iota(jnp.int32, v.shape, 0)
    s = 1
    while s < HG_CHUNK:
        if reverse:
            v = v + jnp.where(row < HG_CHUNK - s, pltpu.roll(v, HG_CHUNK - s, 0), 0.0)
        else:
            v = v + jnp.where(row >= s, pltpu.roll(v, s, 0), 0.0)
        s *= 2
    return v


def _hgrn_fwd(zmain, lb_logits, *, T):
    S = zmain.shape[0]
    nc = T // HG_CHUNK

    def body(q_ref, f_ref, v_ref, lbl_ref, o_ref, st_ref, state):
        @pl.when(pl.program_id(1) == 0)
        def _():
            state[...] = jnp.zeros_like(state)

        lb = _lower_bound(lbl_ref)
        tril = _tri(True)
        qis, updates, decays, intra = [], [], [], []
        for c in range(nc):
            sl = pl.ds(c * HG_CHUNK, HG_CHUNK)
            _, _, g, k = _hg_gates(_f32(f_ref[sl, :]), lb)
            b = _scan_rows(g)
            bl = jnp.sum(g, axis=0, keepdims=True)
            qi = _bf(_f32(q_ref[sl, :]) * jnp.exp(b))
            ki = _bf(k * jnp.exp(-b))
            ko = _bf(k * jnp.exp(bl - b))
            vb = _bf(v_ref[sl, :])
            att = jnp.where(tril, _dot_nt(qi, ki), 0.0)
            intra.append(_dot(_bf(att), vb))
            qis.append(qi)
            updates.append(_dot_tn(vb, ko))
            decays.append(jnp.exp(bl))
        st = state[...]
        for c in range(nc):
            st_ref[0, c] = st
            o_ref[pl.ds(c * HG_CHUNK, HG_CHUNK), :] = intra[c] + _dot_nt(qis[c], _bf(st))
            st = st * decays[c] + updates[c]
        state[...] = st

    col = lambda base: pl.BlockSpec((T, HG_DK), lambda h, t: (t, base + h))
    return pl.pallas_call(
        body,
        grid=(HG_HEADS, S // T),
        in_specs=[col(0), col(8), col(16), pl.BlockSpec((2, HG_DK), lambda h, t: (0, h))],
        out_specs=[
            pl.BlockSpec((T, HG_DK), lambda h, t: (t, h)),
            pl.BlockSpec((1, nc, HG_DK, HG_DK), lambda h, t: (h, t, 0, 0)),
        ],
        out_shape=[
            jax.ShapeDtypeStruct((S, D_MODEL), F32),
            jax.ShapeDtypeStruct((HG_HEADS, S // HG_CHUNK, HG_DK, HG_DK), F32),
        ],
        scratch_shapes=[pltpu.VMEM((HG_DK, HG_DK), F32)],
        compiler_params=_cparams("parallel", "arbitrary"),
        name="hgrn_fwd",
    )(zmain, zmain, zmain, lb_logits)


def _hgrn_bwd(zmain, lb_logits, states, d_o, *, T):
    S = zmain.shape[0]
    nc = T // HG_CHUNK
    nt = S // T

    def body(q_ref, f_ref, v_ref, lbl_ref, st_ref, do_ref, dz_ref, dlb_ref, dstate):
        @pl.when(pl.program_id(1) == 0)
        def _():
            dstate[...] = jnp.zeros_like(dstate)
            dlb_ref[...] = jnp.zeros_like(dlb_ref)

        lb = _lower_bound(lbl_ref)
        tril = _tri(True)
        last_row = lax.broadcasted_iota(jnp.int32, (HG_CHUNK, HG_DK), 0) == HG_CHUNK - 1
        saved = []
        for c in range(nc):
            sl = pl.ds(c * HG_CHUNK, HG_CHUNK)
            sg, f, g, k = _hg_gates(_f32(f_ref[sl, :]), lb)
            b = _scan_rows(g)
            bl = jnp.sum(g, axis=0, keepdims=True)
            eb = jnp.exp(b)
            enb = jnp.exp(-b)
            eo = jnp.exp(bl - b)
            q_in = _f32(q_ref[sl, :]) * eb
            k_in = k * enb
            k_out = k * eo
            qi, ki, ko = _bf(q_in), _bf(k_in), _bf(k_out)
            vb = _bf(v_ref[sl, :])
            dob = do_ref[sl, :]
            att = jnp.where(tril, _dot_nt(qi, ki), 0.0)
            d_att = _bf(jnp.where(tril, _dot_nt(dob, vb), 0.0))
            d_kin = _dot_tn(d_att, qi)
            saved.append(dict(
                sg=sg, f=f, eb=eb, enb=enb, eo=eo, ebl=jnp.exp(bl), k_out=k_out, ko=ko, vb=vb, dob=dob,
                d_v=_dot_tn(_bf(att), dob), d_qin=_dot(d_att, ki), d_kin=d_kin,
                qk=(q_in, k_in), d_state=_dot_tn(dob, qi)))
        dst = dstate[...]
        dsts = [None] * nc
        for c in reversed(range(nc)):
            dsts[c] = dst
            dst = dst * saved[c]["ebl"] + saved[c]["d_state"]
        dstate[...] = dst
        dlb = jnp.zeros((1, HG_DK), F32)
        for c in range(nc):
            sl = pl.ds(c * HG_CHUNK, HG_CHUNK)
            s = saved[c]
            q_in, k_in = s["qk"]
            st = st_ref[0, c]
            dstb = _bf(dsts[c])
            d_v = s["d_v"] + _dot_nt(s["ko"], dstb)
            d_qin = s["d_qin"] + _dot(s["dob"], _bf(st))
            d_kout = _dot(s["vb"], dstb)
            d_decay = jnp.sum(dsts[c] * st, axis=0, keepdims=True)
            kk = d_kout * s["k_out"]
            d_b = d_qin * q_in - s["d_kin"] * k_in - kk
            d_bl = jnp.sum(kk, axis=0, keepdims=True) + d_decay * s["ebl"]
            d_g = _scan_rows(d_b + jnp.where(last_row, d_bl, 0.0), reverse=True)
            d_f = d_g / s["f"] - (s["d_kin"] * s["enb"] + d_kout * s["eo"])
            dz_ref[sl, 0:HG_DK] = _bf(d_qin * s["eb"])
            dz_ref[sl, HG_DK:2 * HG_DK] = _bf(d_f * (1.0 - lb) * s["sg"] * (1.0 - s["sg"]))
            dz_ref[sl, 2 * HG_DK:3 * HG_DK] = _bf(d_v)
            dlb = dlb + jnp.sum(d_f * (1.0 - s["sg"]), axis=0, keepdims=True)
        dlb_ref[...] += dlb

    rev = lambda base: pl.BlockSpec((T, HG_DK), lambda h, t: (nt - 1 - t, base + h))
    outc = pl.BlockSpec((T, HG_DK), lambda h, t: (nt - 1 - t, h))
    return pl.pallas_call(
        body,
        grid=(HG_HEADS, nt),
        in_specs=[
            rev(0), rev(8), rev(16),
            pl.BlockSpec((2, HG_DK), lambda h, t: (0, h)),
            pl.BlockSpec((1, nc, HG_DK, HG_DK), lambda h, t: (h, nt - 1 - t, 0, 0)),
            outc,
        ],
        out_specs=[pl.BlockSpec((T, 3 * HG_DK), lambda h, t: (nt - 1 - t, h)),
                   pl.BlockSpec((1, HG_DK), lambda h, t: (0, h))],
        out_shape=[jax.ShapeDtypeStruct((S, 3 * D_MODEL), BF16), jax.ShapeDtypeStruct((1, D_MODEL), F32)],
        scratch_shapes=[pltpu.VMEM((HG_DK, HG_DK), F32)],
        compiler_params=_cparams("parallel", "arbitrary"),
        name="hgrn_bwd",
    )(zmain, zmain, zmain, lb_logits, states, d_o)


def _t5_bucket_table():
    qi = jnp.arange(SWA_BLOCK)[:, None] + SWA_BLOCK
    kj = jnp.arange(2 * SWA_BLOCK)[None, :]
    n = jnp.clip(qi - kj, 0, SWA_WINDOW - 1)
    max_exact = NUM_BUCKETS // 2
    nf = jnp.maximum(n, 1).astype(F32)
    large = max_exact + (jnp.log(nf / max_exact) / math.log(MAX_DISTANCE / max_exact)
                         * (NUM_BUCKETS - max_exact)).astype(jnp.int32)
    large = jnp.minimum(large, NUM_BUCKETS - 1)
    return jnp.where(n < max_exact, n, large).astype(jnp.int32)


def _swa_valid(n):
    qi = lax.broadcasted_iota(jnp.int32, (SWA_BLOCK, 2 * SWA_BLOCK), 0) + SWA_BLOCK
    kj = lax.broadcasted_iota(jnp.int32, (SWA_BLOCK, 2 * SWA_BLOCK), 1)
    dist = qi - kj
    return (dist >= 0) & (dist < SWA_WINDOW) & ((n > 0) | (kj >= SWA_BLOCK))


def _swa_bias_init(bias, bucket_ref, rb_ref):
    bk = bucket_ref[...]
    for h in range(SWA_HEADS):
        def sel(b, acc, h=h):
            return jnp.where(bk == b, rb_ref[b, h], acc)
        bias[h] = lax.fori_loop(0, NUM_BUCKETS, sel, jnp.zeros(bk.shape, F32))


def _lane_halves(t, kv_head):
    lane = lax.broadcasted_iota(jnp.int32, t.shape, 1)
    rolled = pltpu.roll(t, 64, 1)
    zero = jnp.zeros_like(t)
    if kv_head == 0:
        return jnp.where(lane < 64, t, zero), jnp.where(lane >= 64, rolled, zero)
    return jnp.where(lane < 64, rolled, zero), jnp.where(lane >= 64, t, zero)


def _swa_probs(s, bias_h, valid, sink):
    s = jnp.where(valid, s + bias_h, -jnp.inf)
    m = jnp.maximum(jnp.max(s, axis=-1, keepdims=True), sink)
    p = jnp.exp(s - m)
    es = jnp.exp(sink - m)
    inv = 1.0 / (jnp.sum(p, axis=-1, keepdims=True) + es)
    return p * inv, es * inv


def _swa_fwd(zmain, bucket, rel_bias, sinks):
    S = zmain.shape[0]
    nb = S // SWA_BLOCK
    scale = SWA_HEAD_DIM ** -0.5

    def body(q_ref, kvc_ref, kvp_ref, bucket_ref, rb_ref, sk_ref, o_ref, bias):
        n = pl.program_id(0)

        @pl.when(n == 0)
        def _():
            _swa_bias_init(bias, bucket_ref, rb_ref)

        valid = _swa_valid(n)
        kk = _bf(jnp.concatenate([kvp_ref[:, 0:128], kvc_ref[:, 0:128]], axis=0))
        vv = _bf(jnp.concatenate([kvp_ref[:, 128:256], kvc_ref[:, 128:256]], axis=0))
        for kvh in range(2):
            ka, kb = _lane_halves(kk, kvh)
            va, vb = _lane_halves(vv, kvh)
            qst = _bf(jnp.concatenate([q_ref[:, pl.ds((kvh * 4 + jj) * 128, 128)] for jj in range(4)], axis=0) * scale)
            probs = []
            for odd, kx in enumerate((ka, kb)):
                s = _dot_nt(qst, kx)
                parts = []
                for jj in range(4):
                    h = 2 * (kvh * 4 + jj) + odd
                    p, _ = _swa_probs(s[jj * SWA_BLOCK:(jj + 1) * SWA_BLOCK], bias[h], valid, sk_ref[0, h])
                    parts.append(_bf(p))
                probs.append(jnp.concatenate(parts, axis=0))
            ost = _dot(probs[0], va) + _dot(probs[1], vb)
            for jj in range(4):
                o_ref[:, pl.ds((kvh * 4 + jj) * 128, 128)] = ost[jj * SWA_BLOCK:(jj + 1) * SWA_BLOCK]

    smem = pl.BlockSpec(memory_space=pltpu.SMEM)
    return pl.pallas_call(
        body,
        grid=(nb,),
        in_specs=[
            pl.BlockSpec((SWA_BLOCK, 1024), lambda n: (n, C_SQ // 1024)),
            pl.BlockSpec((SWA_BLOCK, 256), lambda n: (n, C_SK // 256)),
            pl.BlockSpec((SWA_BLOCK, 256), lambda n: (jnp.maximum(n - 1, 0), C_SK // 256)),
            _const_spec((SWA_BLOCK, 2 * SWA_BLOCK)), smem, smem,
        ],
        out_specs=pl.BlockSpec((SWA_BLOCK, 1024), lambda n: (n, 0)),
        out_shape=jax.ShapeDtypeStruct((S, 1024), F32),
        scratch_shapes=[pltpu.VMEM((SWA_HEADS, SWA_BLOCK, 2 * SWA_BLOCK), F32)],
        compiler_params=_cparams("arbitrary"),
        name="swa_fwd",
    )(zmain, zmain, zmain, bucket, rel_bias, sinks)


def _swa_bwd(zmain, o_b, d_o, bucket, rel_bias, sinks, dep):
    S = zmain.shape[0]
    nb = S // SWA_BLOCK
    scale = SWA_HEAD_DIM ** -0.5

    def body(q_ref, kvc_ref, kvp_ref, o_ref, do_ref, bucket_ref, rb_ref, sk_ref, dep_ref,
             dq_ref, dkv_ref, drb_ref, dsk_ref, bias, dbias, carry):
        del dep_ref
        n = pl.program_id(0)

        @pl.when(n == 0)
        def _():
            _swa_bias_init(bias, bucket_ref, rb_ref)
            dbias[...] = jnp.zeros_like(dbias)
            carry[...] = jnp.zeros_like(carry)
            dsk_ref[...] = jnp.zeros_like(dsk_ref)

        @pl.when(n < nb)
        def _():
            valid = _swa_valid(n)
            kk = _bf(jnp.concatenate([kvp_ref[:, 0:128], kvc_ref[:, 0:128]], axis=0))
            vv = _bf(jnp.concatenate([kvp_ref[:, 128:256], kvc_ref[:, 128:256]], axis=0))
            lane = lax.broadcasted_iota(jnp.int32, (2 * SWA_BLOCK, 128), 1)
            lane_q = lax.broadcasted_iota(jnp.int32, (4 * SWA_BLOCK, 128), 1)
            dk_parts, dv_parts = [], []
            for kvh in range(2):
                ka, kb = _lane_halves(kk, kvh)
                va, vb = _lane_halves(vv, kvh)
                pair_cols = [pl.ds((kvh * 4 + jj) * 128, 128) for jj in range(4)]
                qst = _bf(jnp.concatenate([q_ref[:, cl] for cl in pair_cols], axis=0) * scale)
                dost = jnp.concatenate([do_ref[:, cl] for cl in pair_cols], axis=0)
                prod = dost.astype(F32) * jnp.concatenate([o_ref[:, cl] for cl in pair_cols], axis=0)
                dq_st = jnp.zeros((4 * SWA_BLOCK, 128), F32)
                zks, zvs = [], []
                for odd, (kx, vx) in enumerate(((ka, va), (kb, vb))):
                    s = _dot_nt(qst, kx)
                    keep = (lane_q >= 64) if odd else (lane_q < 64)
                    delta = jnp.sum(jnp.where(keep, prod, 0.0), axis=-1, keepdims=True)
                    dp = _dot_nt(dost, vx)
                    p_parts, ds_parts = [], []
                    for jj in range(4):
                        h = 2 * (kvh * 4 + jj) + odd
                        rows = slice(jj * SWA_BLOCK, (jj + 1) * SWA_BLOCK)
                        p, ps = _swa_probs(s[rows], bias[h], valid, sk_ref[0, h])
                        ds = p * (dp[rows] - delta[rows])
                        dbias[h] += ds
                        dsk_ref[h] += jnp.broadcast_to(-jnp.sum(ps * delta[rows], axis=0, keepdims=True), (8, 128))
                        p_parts.append(_bf(p))
                        ds_parts.append(_bf(ds))
                    pst = jnp.concatenate(p_parts, axis=0)
                    dsst = jnp.concatenate(ds_parts, axis=0)
                    dq_st = dq_st + _dot(dsst, kx)
                    zks.append(_dot_tn(dsst, qst))
                    zvs.append(_dot_tn(pst, dost))
                for jj in range(4):
                    dq_ref[:, pair_cols[jj]] = _bf(dq_st[jj * SWA_BLOCK:(jj + 1) * SWA_BLOCK] * scale)
                zk = jnp.where(lane < 64, zks[0], zks[1])
                zv = jnp.where(lane < 64, zvs[0], zvs[1])
                dk_parts.append(zk + pltpu.roll(zk, 64, 1))
                dv_parts.append(zv + pltpu.roll(zv, 64, 1))
            dk = jnp.where(lane < 64, dk_parts[0], dk_parts[1])
            dv = jnp.where(lane < 64, dv_parts[0], dv_parts[1])
            dkv = jnp.concatenate([dk, dv], axis=1)
            dkv_ref[...] = _bf(carry[...] + dkv[0:SWA_BLOCK])
            carry[...] = dkv[SWA_BLOCK:]

        @pl.when(n == nb)
        def _():
            dkv_ref[...] = _bf(carry[...])
            bk = bucket_ref[...]

            def per_head(h, _):
                db = dbias[h]

                def per_bucket(b, _):
                    tot = jnp.sum(jnp.where(bk == b, db, 0.0), axis=1, keepdims=True)
                    tot = jnp.sum(tot, axis=0, keepdims=True)
                    drb_ref[h * NUM_BUCKETS + b] = jnp.broadcast_to(tot, (8, 128))
                    return 0

                return lax.fori_loop(0, NUM_BUCKETS, per_bucket, 0)

            lax.fori_loop(0, SWA_HEADS, per_head, 0)

    smem = pl.BlockSpec(memory_space=pltpu.SMEM)
    cur = lambda n: jnp.minimum(n, nb - 1)
    prev = lambda n: jnp.maximum(jnp.minimum(n, nb - 1) - 1, 0)
    return pl.pallas_call(
        body,
        grid=(nb + 1,),
        in_specs=[
            pl.BlockSpec((SWA_BLOCK, 1024), lambda n: (cur(n), C_SQ // 1024)),
            pl.BlockSpec((SWA_BLOCK, 256), lambda n: (cur(n), C_SK // 256)),
            pl.BlockSpec((SWA_BLOCK, 256), lambda n: (prev(n), C_SK // 256)),
            pl.BlockSpec((SWA_BLOCK, 1024), lambda n: (cur(n), 0)),
            pl.BlockSpec((SWA_BLOCK, 1024), lambda n: (cur(n), 0)),
            _const_spec((SWA_BLOCK, 2 * SWA_BLOCK)), smem, smem, _dep_spec(),
        ],
        out_specs=[
            pl.BlockSpec((SWA_BLOCK, 1024), lambda n: (cur(n), 0)),
            pl.BlockSpec((SWA_BLOCK, 256), lambda n: (jnp.maximum(n - 1, 0), 0)),
            pl.BlockSpec((SWA_HEADS * NUM_BUCKETS, 8, 128), lambda n: (0, 0, 0)),
            pl.BlockSpec((SWA_HEADS, 8, 128), lambda n: (0, 0, 0)),
        ],
        out_shape=[
            jax.ShapeDtypeStruct((S, 1024), BF16),
            jax.ShapeDtypeStruct((S, 256), BF16),
            jax.ShapeDtypeStruct((SWA_HEADS * NUM_BUCKETS, 8, 128), F32),
            jax.ShapeDtypeStruct((SWA_HEADS, 8, 128), F32),
        ],
        scratch_shapes=[
            pltpu.VMEM((SWA_HEADS, SWA_BLOCK, 2 * SWA_BLOCK), F32),
            pltpu.VMEM((SWA_HEADS, SWA_BLOCK, 2 * SWA_BLOCK), F32),
            pltpu.VMEM((SWA_BLOCK, 256), F32),
        ],
        compiler_params=_cparams("arbitrary"),
        name="swa_bwd",
    )(zmain, zmain, zmain, o_b, d_o, bucket, rel_bias, sinks, dep)


def _mem_probs(q_ref, k):
    qs = _bf(q_ref[...] * (MEM_HEAD_DIM ** -0.5))
    s = _dot_nt(qs, k)
    e = jnp.exp(s - jnp.max(s, axis=-1, keepdims=True))
    return qs, e / jnp.sum(e, axis=-1, keepdims=True)


def _mem_q_specs(T):
    return [pl.BlockSpec((T, MEM_HEAD_DIM), lambda t, h=h: (t, C_MQ // MEM_HEAD_DIM + h)) for h in range(MEM_HEADS)]


def _mem_kv_proj(mem, g2):
    def body(mem_ref, w_ref, o_ref):
        o_ref[...] = _dot_nt(_bf(mem_ref[...]), _rows(w_ref))

    return pl.pallas_call(
        body,
        grid=(1,),
        in_specs=[pl.BlockSpec((MEM_LEN, D_MODEL), lambda i: (0, 0)), _gathered_spec(R_KV, R_OTHER)],
        out_specs=pl.BlockSpec((MEM_LEN, 2048), lambda i: (0, 0)),
        out_shape=jax.ShapeDtypeStruct((MEM_LEN, 2048), F32),
        compiler_params=_cparams("arbitrary"),
        name="mem_kv_proj",
    )(mem, g2)


def _mem_fwd(zmain, mkv, *, T):
    S = zmain.shape[0]

    def body(q0, q1, q2, q3, kv_ref, o_ref):
        for h, q_ref in enumerate((q0, q1, q2, q3)):
            cols = pl.ds(h * MEM_HEAD_DIM, MEM_HEAD_DIM)
            _, p = _mem_probs(q_ref, _bf(kv_ref[:, cols]))
            o_ref[:, cols] = _dot(_bf(p), _bf(kv_ref[:, pl.ds(1024 + h * MEM_HEAD_DIM, MEM_HEAD_DIM)]))

    return pl.pallas_call(
        body,
        grid=(S // T,),
        in_specs=_mem_q_specs(T) + [_const_spec((MEM_LEN, 2048))],
        out_specs=pl.BlockSpec((T, 1024), lambda t: (t, 0)),
        out_shape=jax.ShapeDtypeStruct((S, 1024), F32),
        compiler_params=_cparams("parallel"),
        name="mem_fwd",
    )(zmain, zmain, zmain, zmain, mkv)


def _mem_bwd(zmain, mkv, o_c, d_o, *, T):
    S = zmain.shape[0]
    scale = MEM_HEAD_DIM ** -0.5

    def body(q0, q1, q2, q3, kv_ref, o_ref, do_ref, dq_ref, dkv_ref):
        @pl.when(pl.program_id(0) == 0)
        def _():
            dkv_ref[...] = jnp.zeros_like(dkv_ref)

        for h, q_ref in enumerate((q0, q1, q2, q3)):
            cols = pl.ds(h * MEM_HEAD_DIM, MEM_HEAD_DIM)
            vcols = pl.ds(1024 + h * MEM_HEAD_DIM, MEM_HEAD_DIM)
            kb = _bf(kv_ref[:, cols])
            qs, p = _mem_probs(q_ref, kb)
            dob = do_ref[:, cols]
            delta = jnp.sum(dob.astype(F32) * o_ref[:, cols], axis=-1, keepdims=True)
            ds = _bf(p * (_dot_nt(dob, _bf(kv_ref[:, vcols])) - delta))
            dq_ref[:, cols] = _bf(_dot(ds, kb) * scale)
            dkv_ref[:, cols] += _dot_tn(ds, qs)
            dkv_ref[:, vcols] += _dot_tn(_bf(p), dob)

    row = pl.BlockSpec((T, 1024), lambda t: (t, 0))
    return pl.pallas_call(
        body,
        grid=(S // T,),
        in_specs=_mem_q_specs(T) + [_const_spec((MEM_LEN, 2048)), row, row],
        out_specs=[row, pl.BlockSpec((MEM_LEN, 2048), lambda t: (0, 0))],
        out_shape=[jax.ShapeDtypeStruct((S, 1024), BF16), jax.ShapeDtypeStruct((MEM_LEN, 2048), F32)],
        compiler_params=_cparams("arbitrary"),
        name="mem_bwd",
    )(zmain, zmain, zmain, zmain, mkv, o_c, d_o)


def _layer_norm(u):
    mu = jnp.mean(u, axis=-1, keepdims=True)
    xc = u - mu
    rstd = lax.rsqrt(jnp.mean(xc * xc, axis=-1, keepdims=True) + LN_EPS)
    return xc * rstd, rstd


def _layer_norm_bwd(dy, gamma, xhat, rstd):
    dxh = dy * gamma
    return rstd * (dxh - jnp.mean(dxh, axis=-1, keepdims=True) - xhat * jnp.mean(dxh * xhat, axis=-1, keepdims=True))


def _merge_forward(oraw_ref, hg_ref, ob_ref, oc_ref, gl_ref, x_ref, gain_ref, wbh, wbs, wbm, wout):
    ys, rs = [], []
    for h in range(HG_HEADS):
        oh = oraw_ref[:, pl.ds(h * HG_DK, HG_DK)]
        r = lax.rsqrt(jnp.mean(oh * oh, axis=-1, keepdims=True) + RMS_EPS)
        ys.append(oh * r)
        rs.append(r)
    y = jnp.concatenate(ys, axis=1)
    hg = _f32(hg_ref[...])
    sg = _sig(hg)
    silu = hg * sg
    oa = _bf(y * gain_ref[...] * silu)
    pa = _dot(oa, _rows(wbh))
    pb = _dot(_bf(ob_ref[...]), _rows(wbs))
    pc = _dot(_bf(oc_ref[...]), _rows(wbm))
    g0 = _sig(_f32(gl_ref[:, 0:1024]))
    g1 = _sig(_f32(gl_ref[:, 1024:2048]))
    g2 = _sig(_f32(gl_ref[:, 2048:3072]))
    m = _bf(g0 * pa + g1 * pb + g2 * pc)
    u1 = ALPHA * x_ref[...] + _dot(m, _rows(wout))
    xhat, rstd = _layer_norm(u1)
    return dict(y=y, rs=rs, hg=hg, sg=sg, silu=silu, oa=oa, pa=pa, pb=pb, pc=pc,
                g0=g0, g1=g1, g2=g2, m=m, xhat=xhat, rstd=rstd)


def _gathered_spec(lo, hi):
    n = hi - lo
    return pl.BlockSpec((N_DEV, n, D_MODEL), lambda *_: (0, lo // n, 0), pipeline_mode=pl.Buffered(1))


def _rows(w_ref):
    return w_ref[...].reshape(-1, D_MODEL)


def _merge_in_specs(T):
    row = lambda w, c=0: pl.BlockSpec((T, w), lambda i: (i, c))
    vec = pl.BlockSpec((1, D_MODEL), lambda i: (0, 0))
    w = [_gathered_spec(lo, hi) for lo, hi in ((R_BH, R_BS), (R_BS, R_BM), (R_BM, R_OUT), (R_OUT, R_KV))]
    return [row(1024), row(1024, C_HG // 1024), row(1024), row(1024), row(3072), row(1024), vec, *w, vec, vec]


def _merge_fwd(o_raw, zmain, o_b, o_c, gl, x, gain, wbh, wbs, wbm, wout, ln_g, ln_b, *, T):
    S = x.shape[0]

    def body(oraw_ref, hg_ref, ob_ref, oc_ref, gl_ref, x_ref, gain_ref, wbh_r, wbs_r, wbm_r, wout_r, g_ref, b_ref, h1_ref):
        f = _merge_forward(oraw_ref, hg_ref, ob_ref, oc_ref, gl_ref, x_ref, gain_ref, wbh_r, wbs_r, wbm_r, wout_r)
        h1_ref[...] = f["xhat"] * g_ref[...] + b_ref[...]

    return pl.pallas_call(
        body,
        grid=(S // T,),
        in_specs=_merge_in_specs(T),
        out_specs=pl.BlockSpec((T, D_MODEL), lambda i: (i, 0)),
        out_shape=jax.ShapeDtypeStruct((S, D_MODEL), F32),
        compiler_params=_cparams("parallel"),
        name="merge_fwd",
    )(o_raw, zmain, o_b, o_c, gl, x, gain, wbh, wbs, wbm, wout, ln_g, ln_b)


def _merge_bwd(d_h1, o_raw, zmain, o_b, o_c, gl, x, gain, wbh, wbs, wbm, wout, ln_g, ln_b, *, T):
    S = x.shape[0]

    def body(dh1_ref, oraw_ref, hg_ref, ob_ref, oc_ref, gl_ref, x_ref, gain_ref, wbh_r, wbs_r, wbm_r, wout_r, g_ref, b_ref,
             dx_ref, du1_ref, m_ref, oa_ref, dpa_ref, dpb_ref, dpc_ref, doraw_ref, dob_ref, doc_ref, dz_ref,
             dgain_ref, dg_ref, db_ref):
        del b_ref

        @pl.when(pl.program_id(0) == 0)
        def _():
            dgain_ref[...] = jnp.zeros_like(dgain_ref)
            dg_ref[...] = jnp.zeros_like(dg_ref)
            db_ref[...] = jnp.zeros_like(db_ref)

        f = _merge_forward(oraw_ref, hg_ref, ob_ref, oc_ref, gl_ref, x_ref, gain_ref, wbh_r, wbs_r, wbm_r, wout_r)
        dh1 = dh1_ref[...]
        dg_ref[...] += jnp.sum(dh1 * f["xhat"], axis=0, keepdims=True)
        db_ref[...] += jnp.sum(dh1, axis=0, keepdims=True)
        du1 = _layer_norm_bwd(dh1, g_ref[...], f["xhat"], f["rstd"])
        dx_ref[...] = ALPHA * du1
        du1b = _bf(du1)
        du1_ref[...] = du1b
        m_ref[...] = f["m"]
        oa_ref[...] = f["oa"]
        dm = _dot_nt(du1b, _rows(wout_r))
        for i, (g, p, dp_ref, dob_r, w_r) in enumerate((
                (f["g0"], f["pa"], dpa_ref, None, wbh_r),
                (f["g1"], f["pb"], dpb_ref, dob_ref, wbs_r),
                (f["g2"], f["pc"], dpc_ref, doc_ref, wbm_r))):
            dz_ref[:, pl.ds((i + 1) * 1024, 1024)] = _bf(dm * p * g * (1.0 - g))
            dp = _bf(dm * g)
            dp_ref[...] = dp
            d_branch = _dot_nt(dp, _rows(w_r))
            if dob_r is not None:
                dob_r[...] = _bf(d_branch)
            else:
                doa = d_branch
        gain = gain_ref[...]
        t = doa * f["y"]
        dgain_ref[...] += jnp.sum(t * f["silu"], axis=0, keepdims=True)
        sg = f["sg"]
        dz_ref[:, 0:1024] = _bf(t * gain * sg * (1.0 + f["hg"] * (1.0 - sg)))
        dy = doa * gain * f["silu"]
        for h in range(HG_HEADS):
            cols = slice(h * HG_DK, (h + 1) * HG_DK)
            yh = f["y"][:, cols]
            dyh = dy[:, cols]
            doraw_ref[:, pl.ds(h * HG_DK, HG_DK)] = _bf(
                f["rs"][h] * (dyh - yh * jnp.mean(dyh * yh, axis=-1, keepdims=True)))

    row = lambda w: pl.BlockSpec((T, w), lambda i: (i, 0))
    vec = pl.BlockSpec((1, D_MODEL), lambda i: (0, 0))
    bshape = jax.ShapeDtypeStruct((S, D_MODEL), BF16)
    vshape = jax.ShapeDtypeStruct((1, D_MODEL), F32)
    return pl.pallas_call(
        body,
        grid=(S // T,),
        in_specs=[row(1024)] + _merge_in_specs(T),
        out_specs=[row(1024)] * 10 + [row(4096), vec, vec, vec],
        out_shape=[jax.ShapeDtypeStruct((S, D_MODEL), F32)] + [bshape] * 9
        + [jax.ShapeDtypeStruct((S, 4096), BF16), vshape, vshape, vshape],
        compiler_params=_cparams("arbitrary"),
        name="merge_bwd",
    )(d_h1, o_raw, zmain, o_b, o_c, gl, x, gain, wbh, wbs, wbm, wout, ln_g, ln_b)


def _mlp_fwd_bwd(h1, target, wup_t, wdn, ln_g, ln_b, *, T, FC):
    S = h1.shape[0]
    nf = D_FF // FC
    assert FC == R_BH - R_UP == R_UP - R_DN

    def body(h1_ref, t_ref, wup_ref, wdn_ref, g_ref, b_ref, dh1_ref, a_ref, dup_ref, du2_ref, loss_ref, dg_ref, db_ref, up_scr):
        @pl.when(pl.program_id(0) == 0)
        def _():
            loss_ref[...] = jnp.zeros_like(loss_ref)
            dg_ref[...] = jnp.zeros_like(dg_ref)
            db_ref[...] = jnp.zeros_like(db_ref)

        h1v = h1_ref[...]
        h1b = _bf(h1v)
        ff = jnp.zeros((T, D_MODEL), F32)
        for j in range(nf):
            rows = pl.ds(j * FC, FC)
            up = jnp.maximum(_dot_nt(h1b, wup_ref[j]), 0.0)
            up_scr[:, rows] = _bf(up)
            a = _bf(up * up)
            a_ref[:, rows] = a
            ff = ff + _dot(a, wdn_ref[j])
        xhat, rstd = _layer_norm(ALPHA * h1v + ff)
        gamma = g_ref[...]
        err = xhat * gamma + b_ref[...] - t_ref[...]
        loss_ref[...] += jnp.sum(jnp.sum(err * err, axis=-1, keepdims=True), axis=0, keepdims=True) * (0.5 / D_MODEL)
        dy = err * (1.0 / D_MODEL)
        dg_ref[...] += jnp.sum(dy * xhat, axis=0, keepdims=True)
        db_ref[...] += jnp.sum(dy, axis=0, keepdims=True)
        du2 = _layer_norm_bwd(dy, gamma, xhat, rstd)
        du2b = _bf(du2)
        du2_ref[...] = du2b
        dh1 = ALPHA * du2
        for j in range(nf):
            rows = pl.ds(j * FC, FC)
            dup = _bf(_dot_nt(du2b, wdn_ref[j]) * (2.0 * up_scr[:, rows].astype(F32)))
            dup_ref[:, rows] = dup
            dh1 = dh1 + _dot(dup, wup_ref[j])
        dh1_ref[...] = dh1

    row = lambda w: pl.BlockSpec((T, w), lambda i: (i, 0))
    vec = pl.BlockSpec((1, D_MODEL), lambda i: (0, 0))
    vshape = jax.ShapeDtypeStruct((1, D_MODEL), F32)
    return pl.pallas_call(
        body,
        grid=(S // T,),
        in_specs=[row(1024), row(1024), _gathered_spec(R_UP, R_BH), _gathered_spec(R_DN, R_UP), vec, vec],
        out_specs=[row(1024), row(D_FF), row(D_FF), row(1024), pl.BlockSpec((8, 128), lambda i: (0, 0)), vec, vec],
        out_shape=[
            jax.ShapeDtypeStruct((S, D_MODEL), F32),
            jax.ShapeDtypeStruct((S, D_FF), BF16),
            jax.ShapeDtypeStruct((S, D_FF), BF16),
            jax.ShapeDtypeStruct((S, D_MODEL), BF16),
            jax.ShapeDtypeStruct((8, 128), F32), vshape, vshape,
        ],
        scratch_shapes=[pltpu.VMEM((T, D_FF), BF16)],
        compiler_params=_cparams("arbitrary"),
        name="mlp_fwd_bwd",
    )(h1, target, wup_t, wdn, ln_g, ln_b)


def _local_step(x, mem, target, lb_logits, gain, sinks, rel_bias, ln1_g, ln1_b, ln2_g, ln2_b,
                win_t, dep0, other_weights, send_other_grads, send_win_grad):
    S = x.shape[0]
    T = min(256, S)
    KC = min(1024, S)
    xb = _bf(x)
    zmain = _mm_nt(x, win_t, n_cols=C_GL, tm=min(512, S), tn=3200, out_dtype=BF16, name="in_proj_main", dep=dep0)
    gl = _mm_nt(x, win_t[C_GL:], tm=min(512, S), tn=1536, out_dtype=BF16, name="in_proj_gates")
    bucket = _t5_bucket_table()

    o_raw, states = _hgrn_fwd(zmain, lb_logits, T=min(512, S))
    o_b = _swa_fwd(zmain, bucket, rel_bias, sinks)
    g2 = other_weights(o_b)
    mkv = _mem_kv_proj(mem, g2)
    o_c = _mem_fwd(zmain, mkv, T=min(512, S))
    merge_args = (o_raw, zmain, o_b, o_c, gl, x, gain, g2, g2, g2, g2, ln1_g, ln1_b)
    h1 = _merge_fwd(*merge_args, T=T)

    d_h1, act, d_up, du2, loss, d_ln2_g, d_ln2_b = _mlp_fwd_bwd(h1, target, g2, g2, ln2_g, ln2_b, T=min(512, S), FC=512)
    wgrad = functools.partial(_mm_tn_resident, tm=256, out_dtype=BF16)
    g_wdn = wgrad(act, du2, kc=KC, name="grad_w_down")
    g_wup_t = wgrad(d_up, _bf(h1), kc=KC, name="grad_w_up")

    (dx_part, du1, m, oa, dpa, dpb, dpc, d_oraw, d_ob, d_oc, d_hg_gl,
     d_gain, d_ln1_g, d_ln1_b) = _merge_bwd(d_h1, *merge_args, T=T)
    g_wout = wgrad(m, du1, kc=KC, name="grad_w_out")
    g_wbh = wgrad(oa, dpa, kc=KC, name="grad_w_branch_hg")
    g_wbs = wgrad(o_b, dpb, kc=KC, name="grad_w_branch_swa")
    g_wbm = wgrad(o_c, dpc, kc=KC, name="grad_w_branch_mem")

    d_mq, d_mkv = _mem_bwd(zmain, mkv, o_c, d_oc, T=min(512, S))
    g_wkv_t = wgrad(d_mkv, mem, kc=MEM_LEN, name="grad_w_mem_kv")
    sent_others = send_other_grads(
        dict(wkv_t=g_wkv_t, wbh=g_wbh, wbs=g_wbs, wbm=g_wbm, wout=g_wout, wup_t=g_wup_t, wdn=g_wdn))
    d_sq, d_skv, d_rb, d_sink = _swa_bwd(zmain, o_b, d_ob, bucket, rel_bias, sinks, sent_others)
    d_qfv, d_lb = _hgrn_bwd(zmain, lb_logits, states, d_oraw, T=min(512, S))

    head_major = lambda a: a.reshape(3, HG_HEADS, HG_DK, D_MODEL).transpose(1, 0, 2, 3).reshape(3 * D_MODEL, D_MODEL)
    col_major = lambda a: a.reshape(HG_HEADS, 3, HG_DK, D_MODEL).transpose(1, 0, 2, 3).reshape(3 * D_MODEL, D_MODEL)
    pieces = (d_qfv, d_hg_gl, d_sq, d_skv, d_mq)
    g_qfv, g_hg_gl, g_sq, g_skv, g_mq = [
        wgrad(p, xb, kc=KC, name="grad_w_in_" + n) for p, n in zip(pieces, ("qfv", "hg_gates", "swa_q", "swa_kv", "mem_q"))]
    g_win_t = jnp.concatenate([col_major(g_qfv), g_hg_gl[:D_MODEL], g_sq, g_skv, g_mq, g_hg_gl[D_MODEL:]], axis=0)
    sent_win = send_win_grad(g_win_t)
    grad_x = _grad_x(*pieces, head_major(win_t[:C_HG]), win_t, dx_part, sent_win, tm=T)

    small = dict(
        d_lb=d_lb, d_gain=d_gain, d_sink=d_sink[:, 0, 0].reshape(1, SWA_HEADS),
        d_rb=d_rb[:, 0, 0].reshape(SWA_HEADS, NUM_BUCKETS).T,
        d_ln1_g=d_ln1_g, d_ln1_b=d_ln1_b, d_ln2_g=d_ln2_g, d_ln2_b=d_ln2_b, loss=loss[0, 0])
    return grad_x, small


MESH = pl.DeviceIdType.MESH
ANY = pl.BlockSpec(memory_space=pl.ANY)


def _coords():
    return lax.axis_index("x"), lax.axis_index("y"), lax.axis_index("c")


def _other_chips(x, y):
    return [(1 - x, y), (x, 1 - y), (1 - x, 1 - y)]


def _all_gather_weights(*arrays):
    na = len(arrays)

    def body(*refs):
        srcs, dsts = refs[:na], refs[na:2 * na]
        send_sems, recv_sems, local_sems = refs[2 * na:]
        x, y, c = _coords()
        me, sibling = (x, y, c), (x, y, 1 - c)
        chips = _other_chips(x, y)

        def slot(a, px, py, pc):
            return dsts[a].at[4 * px + 2 * py + pc]

        def copy(a, k, block, to, from_shard=False):
            return pltpu.make_async_remote_copy(
                src_ref=srcs[a] if from_shard else slot(a, *block), dst_ref=slot(a, *block),
                send_sem=send_sems.at[a * 7 + k], recv_sem=recv_sems.at[a * 7 + k],
                device_id=to, device_id_type=MESH)

        own = [pltpu.make_async_copy(srcs[a], slot(a, *me), local_sems.at[a]) for a in range(na)]
        for cp in own:
            cp.start()
        first = []
        for a in range(na):
            first.append(copy(a, 0, me, sibling, True))
            first += [copy(a, 1 + j, me, (*chip, c), True) for j, chip in enumerate(chips)]
        for cp in first:
            cp.start()
        passed = []
        for j, chip in enumerate(chips):
            for a in range(na):
                copy(a, 1 + j, (*chip, c), me).wait_recv()
                fwd = copy(a, 4 + j, (*chip, c), sibling)
                fwd.start()
                passed.append(fwd)
        for a in range(na):
            copy(a, 0, sibling, me).wait_recv()
            for j, chip in enumerate(chips):
                copy(a, 4 + j, (*chip, 1 - c), me).wait_recv()
        for cp in first + passed:
            cp.wait_send()
        for cp in own:
            cp.wait()

    return pl.pallas_call(
        body,
        in_specs=[ANY] * na,
        out_specs=[ANY] * na,
        out_shape=[jax.ShapeDtypeStruct((N_DEV,) + a.shape, a.dtype) for a in arrays],
        scratch_shapes=[pltpu.SemaphoreType.DMA((7 * na,)), pltpu.SemaphoreType.DMA((7 * na,)),
                        pltpu.SemaphoreType.DMA((na,))],
        name="all_gather_weights",
    )(*arrays)


HBM = pl.BlockSpec(memory_space=pltpu.HBM)
SEM = pl.BlockSpec(memory_space=pltpu.SEMAPHORE)
_DATAFLOW = pltpu.SideEffectType.DATAFLOW_SIDE_EFFECTING


def _peer(x, y, c, r):
    return x ^ (r >> 2), y ^ ((r >> 1) & 1), c ^ (r & 1)


def _direct_copies(src_ref, land_ref, send_sems, recv_sems, gather, receiving):
    x, y, c = _coords()
    me = 4 * x + 2 * y + c
    copies = []
    for r in range(1, N_DEV):
        px, py, pc = _peer(x, y, c, r)
        peer = 4 * px + 2 * py + pc
        if gather:
            src, dst = src_ref, land_ref.at[peer if receiving else me]
        else:
            src, dst = src_ref.at[peer], land_ref.at[r - 1]
        copies.append(pltpu.make_async_remote_copy(
            src_ref=src, dst_ref=dst, send_sem=send_sems.at[r - 1], recv_sem=recv_sems.at[r - 1],
            device_id=(px, py, pc), device_id_type=MESH))
    return copies


def _direct_start(src, land, *, gather, name):
    def body(src_ref, land_ref, send_sems, recv_sems, src_thru, land_thru, token):
        del src_thru, land_thru
        for cp in _direct_copies(src_ref, land_ref, send_sems, recv_sems, gather, False):
            cp.start()
        token[...] = jnp.zeros_like(token)

    return pl.pallas_call(
        body,
        name=name,
        out_shape=(pltpu.SemaphoreType.DMA((N_DEV - 1,)), pltpu.SemaphoreType.DMA((N_DEV - 1,)),
                   pltpu.HBM(src.shape, src.dtype), pltpu.HBM(land.shape, land.dtype),
                   jax.ShapeDtypeStruct((8, 128), F32)),
        in_specs=(HBM, HBM),
        out_specs=(SEM, SEM, HBM, HBM, pl.BlockSpec(memory_space=pltpu.VMEM)),
        input_output_aliases={0: 2, 1: 3},
        compiler_params=pltpu.CompilerParams(has_side_effects=_DATAFLOW),
    )(pltpu.with_memory_space_constraint(src, pltpu.HBM), pltpu.with_memory_space_constraint(land, pltpu.HBM))


def _direct_wait(send_sems, recv_sems, src_thru, land_thru, after, *, gather, name):
    def body(src_ref, land_ref, send_sems_ref, recv_sems_ref, after_ref, src_dead, got_ref):
        del after_ref, src_dead, got_ref
        for cp in _direct_copies(src_ref, land_ref, send_sems_ref, recv_sems_ref, gather, True):
            cp.wait_send()
            cp.wait_recv()

    return pl.pallas_call(
        body,
        name=name,
        out_shape=(pltpu.HBM(src_thru.shape, src_thru.dtype), pltpu.HBM(land_thru.shape, land_thru.dtype)),
        in_specs=(HBM, HBM, SEM, SEM, ANY),
        out_specs=(HBM, HBM),
        input_output_aliases={0: 0, 1: 1},
        compiler_params=pltpu.CompilerParams(has_side_effects=_DATAFLOW),
    )(src_thru, land_thru, send_sems, recv_sems, after)


def _sum_partials(src, land, me, *, tr, name):
    R = src.shape[1]

    def body(me_ref, s_ref, l_ref, o_ref):
        del me_ref
        acc = s_ref[0].astype(F32)
        for r in range(N_DEV - 1):
            acc = acc + l_ref[r].astype(F32)
        o_ref[...] = acc

    return pl.pallas_call(
        body,
        grid_spec=pltpu.PrefetchScalarGridSpec(
            num_scalar_prefetch=1, grid=(R // tr,),
            in_specs=[pl.BlockSpec((1, tr, 1024), lambda i, mr: (mr[0], i, 0)),
                      pl.BlockSpec((N_DEV - 1, tr, 1024), lambda i, mr: (0, i, 0))],
            out_specs=pl.BlockSpec((tr, 1024), lambda i, mr: (i, 0))),
        out_shape=jax.ShapeDtypeStruct((R, 1024), F32),
        name=name,
    )(me, src, land)


def _small_all_reduce(packed, lb_logits):
    def body(p_ref, lbl_ref, o_ref, gath, send_sems, recv_sems):
        x, y, c = _coords()
        mine = 4 * x + 2 * y + c
        gath[mine] = p_ref[...]
        copies = []
        for r in range(1, N_DEV):
            peer = (x ^ (r >> 2), y ^ ((r >> 1) & 1), c ^ (r & 1))
            copies.append(pltpu.make_async_remote_copy(
                src_ref=p_ref, dst_ref=gath.at[mine],
                send_sem=send_sems.at[r - 1], recv_sem=recv_sems.at[r - 1],
                device_id=peer, device_id_type=MESH))
        for cp in copies:
            cp.start()
        for r in range(1, N_DEV):
            peer_slot = 4 * (x ^ (r >> 2)) + 2 * (y ^ ((r >> 1) & 1)) + (c ^ (r & 1))
            pltpu.make_async_remote_copy(
                src_ref=p_ref, dst_ref=gath.at[peer_slot],
                send_sem=send_sems.at[r - 1], recv_sem=recv_sems.at[r - 1],
                device_id=(x, y, c), device_id_type=MESH).wait_recv()
        for cp in copies:
            cp.wait_send()
        tot = gath[0]
        for d in range(1, N_DEV):
            tot = tot + gath[d]
        o_ref[...] = tot
        lb = _lower_bound(lbl_ref)
        dl0 = o_ref[SM_LB:SM_LB + 1, :] * lb * (1.0 - lb)
        o_ref[SM_LB:SM_LB + 1, :] = dl0
        o_ref[SM_LB + 1:SM_LB + 2, :] = -dl0

    vm = pl.BlockSpec(memory_space=pltpu.VMEM)
    return pl.pallas_call(
        body,
        in_specs=[vm, vm],
        out_specs=vm,
        out_shape=jax.ShapeDtypeStruct(packed.shape, F32),
        scratch_shapes=[pltpu.VMEM((N_DEV,) + packed.shape, F32),
                        pltpu.SemaphoreType.DMA((N_DEV - 1,)), pltpu.SemaphoreType.DMA((N_DEV - 1,))],
        name="small_all_reduce",
    )(packed, lb_logits)


def _adamw(w, g, m, v, *, tr, name):
    R, C = w.shape

    def body(w_ref, g_ref, m_ref, v_ref, d_ref, nm_ref, nv_ref):
        gv = g_ref[...]
        nm = ADAM_B1 * m_ref[...] + (1.0 - ADAM_B1) * gv
        nv = ADAM_B2 * v_ref[...] + (1.0 - ADAM_B2) * jnp.square(gv)
        m_hat = nm / (1.0 - ADAM_B1 ** ADAM_STEP)
        v_hat = nv / (1.0 - ADAM_B2 ** ADAM_STEP)
        d_ref[...] = -ADAM_LR * (m_hat / (jnp.sqrt(v_hat) + ADAM_EPS) + ADAM_WD * w_ref[...])
        nm_ref[...] = nm
        nv_ref[...] = nv

    spec = pl.BlockSpec((tr, C), lambda i: (i, 0))
    return pl.pallas_call(
        body,
        grid=(R // tr,),
        in_specs=[spec] * 4,
        out_specs=[spec] * 3,
        out_shape=[jax.ShapeDtypeStruct((R, C), F32)] * 3,
        compiler_params=_cparams("parallel"),
        name=name,
    )(w, g, m, v)


def _pack_small(lb, gain, sinks, rel_bias, ln1_g, ln1_b, ln2_g, ln2_b, loss=None):
    pad = lambda a: jnp.pad(a.reshape(1, -1), ((0, 0), (0, D_MODEL - a.size)))
    rows = [lb.reshape(-1, D_MODEL)]
    if rows[0].shape[0] == 1:
        rows.append(jnp.zeros((1, D_MODEL), F32))
    rows += [gain.reshape(1, D_MODEL), pad(sinks), pad(rel_bias), ln1_g.reshape(1, D_MODEL), ln1_b.reshape(1, D_MODEL),
             ln2_g.reshape(1, D_MODEL), ln2_b.reshape(1, D_MODEL),
             pad(jnp.zeros((1,), F32) if loss is None else loss.reshape(1))]
    rows.append(jnp.zeros((SM_ROWS - SM_LOSS - 1, D_MODEL), F32))
    return jnp.concatenate(rows, axis=0)


def _unpack_small(p):
    return dict(
        lb_logits=p[SM_LB:SM_LB + 2], hg_norm_gain=p[SM_GAIN:SM_GAIN + 1], swa_sinks=p[SM_SINK:SM_SINK + 1, :SWA_HEADS],
        rel_bias=p[SM_RB, :NUM_BUCKETS * SWA_HEADS].reshape(NUM_BUCKETS, SWA_HEADS),
        ln1_g=p[SM_L1G:SM_L1G + 1], ln1_b=p[SM_L1B:SM_L1B + 1], ln2_g=p[SM_L2G:SM_L2G + 1], ln2_b=p[SM_L2B:SM_L2B + 1])


_SMALL = ("lb_logits", "hg_norm_gain", "swa_sinks", "rel_bias", "ln1_g", "ln1_b", "ln2_g", "ln2_b")
_WEIGHTS = ("w_in", "lb_logits", "hg_norm_gain", "swa_sinks", "rel_bias", "w_mem_kv", "w_branch_hg", "w_branch_swa",
            "w_branch_mem", "w_out", "ln1_g", "ln1_b", "w_up", "w_down", "ln2_g", "ln2_b")


def kernel(x, mem, w_in, lb_logits, hg_norm_gain, swa_sinks, rel_bias, w_mem_kv, w_branch_hg, w_branch_swa, w_branch_mem, w_out, ln1_g, ln1_b, w_up, w_down, ln2_g, ln2_b, loss_target, m_w_in, m_lb_logits, m_hg_norm_gain, m_swa_sinks, m_rel_bias, m_w_mem_kv, m_w_branch_hg, m_w_branch_swa, m_w_branch_mem, m_w_out, m_ln1_g, m_ln1_b, m_w_up, m_w_down, m_ln2_g, m_ln2_b, v_w_in, v_lb_logits, v_hg_norm_gain, v_swa_sinks, v_rel_bias, v_w_mem_kv, v_w_branch_hg, v_w_branch_swa, v_w_branch_mem, v_w_out, v_ln1_g, v_ln1_b, v_w_up, v_w_down, v_ln2_g, v_ln2_b):
    w = dict(w_in=w_in, lb_logits=lb_logits, hg_norm_gain=hg_norm_gain, swa_sinks=swa_sinks, rel_bias=rel_bias,
             w_mem_kv=w_mem_kv, w_branch_hg=w_branch_hg, w_branch_swa=w_branch_swa, w_branch_mem=w_branch_mem,
             w_out=w_out, ln1_g=ln1_g, ln1_b=ln1_b, w_up=w_up, w_down=w_down, ln2_g=ln2_g, ln2_b=ln2_b)
    mom = dict(w_in=m_w_in, lb_logits=m_lb_logits, hg_norm_gain=m_hg_norm_gain, swa_sinks=m_swa_sinks, rel_bias=m_rel_bias,
               w_mem_kv=m_w_mem_kv, w_branch_hg=m_w_branch_hg, w_branch_swa=m_w_branch_swa, w_branch_mem=m_w_branch_mem,
               w_out=m_w_out, ln1_g=m_ln1_g, ln1_b=m_ln1_b, w_up=m_w_up, w_down=m_w_down, ln2_g=m_ln2_g, ln2_b=m_ln2_b)
    var = dict(w_in=v_w_in, lb_logits=v_lb_logits, hg_norm_gain=v_hg_norm_gain, swa_sinks=v_swa_sinks, rel_bias=v_rel_bias,
               w_mem_kv=v_w_mem_kv, w_branch_hg=v_w_branch_hg, w_branch_swa=v_w_branch_swa, w_branch_mem=v_w_branch_mem,
               w_out=v_w_out, ln1_g=v_ln1_g, ln1_b=v_ln1_b, w_up=v_w_up, w_down=v_w_down, ln2_g=v_ln2_g, ln2_b=v_ln2_b)
    xc, yc, cc = _coords()

    p1 = _bf(w_in[0].T)
    p2 = _bf(jnp.concatenate([w_down[0], w_up[0].T, w_branch_hg[0], w_branch_swa[0], w_branch_mem[0], w_out[0],
                              w_mem_kv[0].T], axis=0))
    me = 4 * xc + 2 * yc + cc
    (g1,) = _all_gather_weights(p1)
    land2 = lax.dynamic_update_slice(lax.empty((N_DEV, R_OTHER, D_MODEL), BF16), p2[None], (me, 0, 0))
    ag2 = _direct_start(p2, land2, gather=True, name="gather_other_weights_start")

    def other_weights(after):
        return _direct_wait(*ag2[:4], after, gather=True, name="gather_other_weights_wait")[1]

    blocks = lambda a: a.reshape(N_DEV, a.shape[0] // N_DEV, D_MODEL)
    started = {}

    def send_other_grads(g):
        part = jnp.concatenate([blocks(g[k]) for k in ("wdn", "wup_t", "wbh", "wbs", "wbm", "wout", "wkv_t")], axis=1)
        started["others"] = _direct_start(part, lax.empty((N_DEV - 1, R_OTHER, D_MODEL), BF16), gather=False,
                                          name="scatter_other_grads_start")
        return started["others"][4]

    def send_win_grad(g):
        started["win"] = _direct_start(blocks(g), lax.empty((N_DEV - 1, IN_SHARD, D_MODEL), BF16), gather=False,
                                       name="scatter_w_in_grad_start")
        return started["win"][4]

    grad_x, small = _local_step(
        x[0], mem[0], loss_target[0], lb_logits, hg_norm_gain, swa_sinks, rel_bias, ln1_g, ln1_b, ln2_g, ln2_b,
        g1.reshape(IN_COLS, D_MODEL), ag2[4], other_weights, send_other_grads, send_win_grad)

    me1 = me.reshape(1).astype(jnp.int32)
    mine2, landed2 = _direct_wait(*started["others"][:4], grad_x, gather=False, name="scatter_other_grads_wait")
    mine1, landed1 = _direct_wait(*started["win"][:4], grad_x, gather=False, name="scatter_w_in_grad_wait")
    gs2 = _sum_partials(mine2, landed2, me1, tr=R_OTHER // 2, name="sum_other_grads")
    gs1 = _sum_partials(mine1, landed1, me1, tr=IN_SHARD // 2, name="sum_w_in_grad")

    grads = dict(
        w_in=gs1.T, w_down=gs2[R_DN:R_UP], w_up=gs2[R_UP:R_BH].T, w_branch_hg=gs2[R_BH:R_BS],
        w_branch_swa=gs2[R_BS:R_BM], w_branch_mem=gs2[R_BM:R_OUT], w_out=gs2[R_OUT:R_KV], w_mem_kv=gs2[R_KV:R_OTHER].T)

    packed = _pack_small(small["d_lb"], small["d_gain"], small["d_sink"], small["d_rb"], small["d_ln1_g"],
                         small["d_ln1_b"], small["d_ln2_g"], small["d_ln2_b"], small["loss"])
    reduced = _small_all_reduce(packed, lb_logits)
    loss = reduced[SM_LOSS, 0]
    grads.update(_unpack_small(reduced))

    delta, new_m, new_v = {}, {}, {}
    for name in _WEIGHTS:
        if name in _SMALL:
            continue
        w2 = w[name][0]
        delta[name], new_m[name], new_v[name] = _adamw(
            w2, grads[name], mom[name][0], var[name][0], tr=w2.shape[0] // 4, name="adamw_" + name)
    sm = lambda d: _pack_small(*[d[k] for k in _SMALL])
    d_s, m_s, v_s = _adamw(sm(w), reduced, sm(mom), sm(var), tr=SM_ROWS, name="adamw_small")
    for dst, src in ((delta, d_s), (new_m, m_s), (new_v, v_s)):
        dst.update(_unpack_small(src))

    def shaped(d, name):
        return d[name].reshape(w[name].shape)

    return (loss, grad_x[None], *[shaped(grads, n) for n in _WEIGHTS], *[shaped(delta, n) for n in _WEIGHTS],
            *[shaped(new_m, n) for n in _WEIGHTS], *[shaped(new_v, n) for n in _WEIGHTS])
```

```python
import functools
import math

import jax
import jax.numpy as jnp
from jax import lax
from jax.experimental import pallas as pl
from jax.experimental.pallas import tpu as pltpu

F32 = jnp.float32
BF16 = jnp.bfloat16

D_MODEL = 1024
MEM_LEN = 256
HG_HEADS = 8
HG_DK = 128
HG_CHUNK = 64
SWA_HEADS = 16
SWA_HEAD_DIM = 64
SWA_BLOCK = 128
SWA_WINDOW = 128
MEM_HEADS = 4
MEM_HEAD_DIM = 256
NUM_BUCKETS = 32
MAX_DISTANCE = 128
D_FF = 4096
LN_EPS = 1e-5
RMS_EPS = 1e-6
ALPHA = 2.0 ** 0.25
N_DEV = 8

C_HQ, C_HF, C_HI, C_HG, C_SQ, C_SK, C_SV, C_MQ, C_GL = 0, 1024, 2048, 3072, 4096, 5120, 5248, 5376, 6400
IN_COLS = 9472
IN_SHARD = IN_COLS // N_DEV

ADAM_LR = 0.001
ADAM_B1 = 0.9
ADAM_B2 = 0.999
ADAM_EPS = 1e-08
ADAM_WD = 0.01
ADAM_STEP = 10

VMEM_LIMIT = 58 * 1024 * 1024

R_DN, R_UP, R_BH, R_BS, R_BM, R_OUT, R_KV, R_OTHER = 0, 512, 1024, 1152, 1280, 1408, 1536, 1792

SM_LB, SM_GAIN, SM_SINK, SM_RB, SM_L1G, SM_L1B, SM_L2G, SM_L2B, SM_LOSS, SM_ROWS = 0, 2, 3, 4, 5, 6, 7, 8, 9, 16


def _bf(v):
    return v.astype(BF16)


def _f32(v):
    return v.astype(F32)


def _dot(a, b):
    return jnp.dot(a, b, preferred_element_type=F32)


def _dot_nt(a, b):
    return lax.dot_general(a, b, (((1,), (1,)), ((), ())), preferred_element_type=F32)


def _dot_tn(a, b):
    return lax.dot_general(a, b, (((0,), (0,)), ((), ())), preferred_element_type=F32)


def _sig(v):
    return 1.0 / (1.0 + jnp.exp(-v))


def _cparams(*sem):
    return pltpu.CompilerParams(dimension_semantics=sem, vmem_limit_bytes=VMEM_LIMIT)


def _const_spec(shape):
    nd = len(shape)
    return pl.BlockSpec(shape, lambda *_: (0,) * nd, pipeline_mode=pl.Buffered(1))


def _dep_spec():
    return pl.BlockSpec((8, 128), lambda *_: (0, 0))


def _mm_nt(a, bt, *, tm, tn, out_dtype, name, dep=None, n_cols=None, also_a_bf16=False):
    M, K = a.shape
    N = bt.shape[0] if n_cols is None else n_cols
    deps = () if dep is None else (dep,)
    nd = len(deps)
    assert not also_a_bf16 or tn == N

    def body(a_ref, b_ref, *rest):
        ab = _bf(a_ref[...])
        o_ref = rest[nd]
        o_ref[...] = _dot_nt(ab, _bf(b_ref[...])).astype(o_ref.dtype)
        if also_a_bf16:
            rest[nd + 1][...] = ab

    out_specs = [pl.BlockSpec((tm, tn), lambda j, i: (i, j))]
    out_shape = [jax.ShapeDtypeStruct((M, N), out_dtype)]
    if also_a_bf16:
        out_specs.append(pl.BlockSpec((tm, K), lambda j, i: (i, 0)))
        out_shape.append(jax.ShapeDtypeStruct((M, K), BF16))
    res = pl.pallas_call(
        body,
        grid=(N // tn, M // tm),
        in_specs=[pl.BlockSpec((tm, K), lambda j, i: (i, 0)), pl.BlockSpec((tn, K), lambda j, i: (j, 0))]
        + [_dep_spec() for _ in deps],
        out_specs=out_specs,
        out_shape=out_shape,
        compiler_params=_cparams("parallel", "parallel"),
        name=name,
    )(a, bt, *deps)
    return res if also_a_bf16 else res[0]


def _mm_tn_resident(a, b, *, tm, kc, name, out_dtype):
    K, M = a.shape
    N = b.shape[1]
    nk = K // kc

    def body(a_ref, b_ref, o_ref):
        acc = jnp.zeros((tm, N), F32)
        for kk in range(nk):
            sl = pl.ds(kk * kc, kc)
            acc = acc + _dot_tn(_bf(a_ref[sl, :]), _bf(b_ref[sl, :]))
        o_ref[...] = acc.astype(o_ref.dtype)

    return pl.pallas_call(
        body,
        grid=(M // tm,),
        in_specs=[pl.BlockSpec((K, tm), lambda i: (0, i)), _const_spec((K, N))],
        out_specs=pl.BlockSpec((tm, N), lambda i: (i, 0)),
        out_shape=jax.ShapeDtypeStruct((M, N), out_dtype),
        compiler_params=_cparams("parallel"),
        name=name,
    )(a, b)


def _mm_tn(a, b, *, kc, name, out_dtype=F32):
    K, M = a.shape
    N = b.shape[1]
    if M > 1024:
        return _mm_tn_resident(a, b, tm=256, kc=min(kc, 1024), name=name, out_dtype=out_dtype)
    tm = M
    nk = K // kc

    def body(a_ref, b_ref, o_ref, acc):
        k = pl.program_id(1)
        part = _dot_tn(_bf(a_ref[...]), _bf(b_ref[...]))

        @pl.when(k == 0)
        def _():
            acc[...] = part

        @pl.when(k > 0)
        def _():
            acc[...] += part

        @pl.when(k == nk - 1)
        def _():
            o_ref[...] = acc[...].astype(o_ref.dtype)

    return pl.pallas_call(
        body,
        grid=(M // tm, nk),
        in_specs=[pl.BlockSpec((kc, tm), lambda i, k: (k, i)), pl.BlockSpec((kc, N), lambda i, k: (k, 0))],
        out_specs=pl.BlockSpec((tm, N), lambda i, k: (i, 0)),
        out_shape=jax.ShapeDtypeStruct((M, N), out_dtype),
        scratch_shapes=[pltpu.VMEM((tm, N), F32)],
        compiler_params=_cparams("parallel", "arbitrary"),
        name=name,
    )(a, b)


def _grad_x(d_qfv, d_hg_gl, d_sq, d_skv, d_mq, w_qfv, win_t, add, dep, *, tm):
    M = add.shape[0]
    pieces = (d_qfv, d_hg_gl, d_sq, d_skv, d_mq)

    def body(qfv_ref, hggl_ref, sq_ref, skv_ref, mq_ref, wq_ref, w_ref, add_ref, dep_ref, o_ref):
        del dep_ref
        acc = add_ref[...] + _dot(qfv_ref[...], wq_ref[...])
        acc = acc + _dot(hggl_ref[:, 0:1024], w_ref[C_HG:C_SQ, :])
        acc = acc + _dot(hggl_ref[:, 1024:4096], w_ref[C_GL:IN_COLS, :])
        acc = acc + _dot(sq_ref[...], w_ref[C_SQ:C_SK, :])
        acc = acc + _dot(skv_ref[...], w_ref[C_SK:C_MQ, :])
        o_ref[...] = acc + _dot(mq_ref[...], w_ref[C_MQ:C_GL, :])

    return pl.pallas_call(
        body,
        grid=(M // tm,),
        in_specs=[pl.BlockSpec((tm, p.shape[1]), lambda i: (i, 0)) for p in pieces]
        + [_const_spec(w_qfv.shape), _const_spec(win_t.shape), pl.BlockSpec((tm, D_MODEL), lambda i: (i, 0)), _dep_spec()],
        out_specs=pl.BlockSpec((tm, D_MODEL), lambda i: (i, 0)),
        out_shape=jax.ShapeDtypeStruct((M, D_MODEL), F32),
        compiler_params=_cparams("parallel"),
        name="grad_x",
    )(*pieces, w_qfv, win_t, add, dep)


def _lower_bound(lbl_ref):
    l0 = lbl_ref[0:1, :]
    l1 = lbl_ref[1:2, :]
    mx = jnp.maximum(l0, l1)
    e0 = jnp.exp(l0 - mx)
    e1 = jnp.exp(l1 - mx)
    return e0 / (e0 + e1)


def _tri(lower):
    r = lax.broadcasted_iota(jnp.int32, (HG_CHUNK, HG_CHUNK), 0)
    c = lax.broadcasted_iota(jnp.int32, (HG_CHUNK, HG_CHUNK), 1)
    return (r >= c) if lower else (r <= c)


def _hg_gates(fl, lb):
    sg = _sig(fl)
    f = lb + (1.0 - lb) * sg
    return sg, f, jnp.log(f), 1.0 - f


def _scan_rows(v, reverse=False):
    row = lax.broadcasted_iota(jnp.int32, v.shape, 0)
    s = 1
    while s < HG_CHUNK:
        if reverse:
            v = v + jnp.where(row < HG_CHUNK - s, pltpu.roll(v, HG_CHUNK - s, 0), 0.0)
        else:
            v = v + jnp.where(row >= s, pltpu.roll(v, s, 0), 0.0)
        s *= 2
    return v


def _hgrn_fwd(zmain, lb_logits, *, T):
    S = zmain.shape[0]
    nc = T // HG_CHUNK

    def body(q_ref, f_ref, v_ref, lbl_ref, o_ref, st_ref, state):
        @pl.when(pl.program_id(1) == 0)
        def _():
            state[...] = jnp.zeros_like(state)

        lb = _lower_bound(lbl_ref)
        tril = _tri(True)
        qis, updates, decays, intra = [], [], [], []
        for c in range(nc):
            sl = pl.ds(c * HG_CHUNK, HG_CHUNK)
            _, _, g, k = _hg_gates(_f32(f_ref[sl, :]), lb)
            b = _scan_rows(g)
            bl = jnp.sum(g, axis=0, keepdims=True)
            qi = _bf(_f32(q_ref[sl, :]) * jnp.exp(b))
            ki = _bf(k * jnp.exp(-b))
            ko = _bf(k * jnp.exp(bl - b))
            vb = _bf(v_ref[sl, :])
            att = jnp.where(tril, _dot_nt(qi, ki), 0.0)
            intra.append(_dot(_bf(att), vb))
            qis.append(qi)
            updates.append(_dot_tn(vb, ko))
            decays.append(jnp.exp(bl))
        st = state[...]
        for c in range(nc):
            st_ref[0, c] = st
            o_ref[pl.ds(c * HG_CHUNK, HG_CHUNK), :] = intra[c] + _dot_nt(qis[c], _bf(st))
            st = st * decays[c] + updates[c]
        state[...] = st

    col = lambda base: pl.BlockSpec((T, HG_DK), lambda h, t: (t, base + h))
    return pl.pallas_call(
        body,
        grid=(HG_HEADS, S // T),
        in_specs=[col(0), col(8), col(16), pl.BlockSpec((2, HG_DK), lambda h, t: (0, h))],
        out_specs=[
            pl.BlockSpec((T, HG_DK), lambda h, t: (t, h)),
            pl.BlockSpec((1, nc, HG_DK, HG_DK), lambda h, t: (h, t, 0, 0)),
        ],
        out_shape=[
            jax.ShapeDtypeStruct((S, D_MODEL), F32),
            jax.ShapeDtypeStruct((HG_HEADS, S // HG_CHUNK, HG_DK, HG_DK), F32),
        ],
        scratch_shapes=[pltpu.VMEM((HG_DK, HG_DK), F32)],
        compiler_params=_cparams("parallel", "arbitrary"),
        name="hgrn_fwd",
    )(zmain, zmain, zmain, lb_logits)


def _hgrn_bwd(zmain, lb_logits, states, d_o, *, T):
    S = zmain.shape[0]
    nc = T // HG_CHUNK
    nt = S // T

    def body(q_ref, f_ref, v_ref, lbl_ref, st_ref, do_ref, dz_ref, dlb_ref, dstate):
        @pl.when(pl.program_id(1) == 0)
        def _():
            dstate[...] = jnp.zeros_like(dstate)
            dlb_ref[...] = jnp.zeros_like(dlb_ref)

        lb = _lower_bound(lbl_ref)
        tril = _tri(True)
        last_row = lax.broadcasted_iota(jnp.int32, (HG_CHUNK, HG_DK), 0) == HG_CHUNK - 1
        saved = []
        for c in range(nc):
            sl = pl.ds(c * HG_CHUNK, HG_CHUNK)
            sg, f, g, k = _hg_gates(_f32(f_ref[sl, :]), lb)
            b = _scan_rows(g)
            bl = jnp.sum(g, axis=0, keepdims=True)
            eb = jnp.exp(b)
            enb = jnp.exp(-b)
            eo = jnp.exp(bl - b)
            q_in = _f32(q_ref[sl, :]) * eb
            k_in = k * enb
            k_out = k * eo
            qi, ki, ko = _bf(q_in), _bf(k_in), _bf(k_out)
            vb = _bf(v_ref[sl, :])
            dob = do_ref[sl, :]
            att = jnp.where(tril, _dot_nt(qi, ki), 0.0)
            d_att = _bf(jnp.where(tril, _dot_nt(dob, vb), 0.0))
            d_kin = _dot_tn(d_att, qi)
            saved.append(dict(
                sg=sg, f=f, eb=eb, enb=enb, eo=eo, ebl=jnp.exp(bl), k_out=k_out, ko=ko, vb=vb, dob=dob,
                d_v=_dot_tn(_bf(att), dob), d_qin=_dot(d_att, ki), d_kin=d_kin,
                qk=(q_in, k_in), d_state=_dot_tn(dob, qi)))
        dst = dstate[...]
        dsts = [None] * nc
        for c in reversed(range(nc)):
            dsts[c] = dst
            dst = dst * saved[c]["ebl"] + saved[c]["d_state"]
        dstate[...] = dst
        dlb = jnp.zeros((1, HG_DK), F32)
        for c in range(nc):
            sl = pl.ds(c * HG_CHUNK, HG_CHUNK)
            s = saved[c]
            q_in, k_in = s["qk"]
            st = st_ref[0, c]
            dstb = _bf(dsts[c])
            d_v = s["d_v"] + _dot_nt(s["ko"], dstb)
            d_qin = s["d_qin"] + _dot(s["dob"], _bf(st))
            d_kout = _dot(s["vb"], dstb)
            d_decay = jnp.sum(dsts[c] * st, axis=0, keepdims=True)
            kk = d_kout * s["k_out"]
            d_b = d_qin * q_in - s["d_kin"] * k_in - kk
            d_bl = jnp.sum(kk, axis=0, keepdims=True) + d_decay * s["ebl"]
            d_g = _scan_rows(d_b + jnp.where(last_row, d_bl, 0.0), reverse=True)
            d_f = d_g / s["f"] - (s["d_kin"] * s["enb"] + d_kout * s["eo"])
            dz_ref[sl, 0:HG_DK] = _bf(d_qin * s["eb"])
            dz_ref[sl, HG_DK:2 * HG_DK] = _bf(d_f * (1.0 - lb) * s["sg"] * (1.0 - s["sg"]))
            dz_ref[sl, 2 * HG_DK:3 * HG_DK] = _bf(d_v)
            dlb = dlb + jnp.sum(d_f * (1.0 - s["sg"]), axis=0, keepdims=True)
        dlb_ref[...] += dlb

    rev = lambda base: pl.BlockSpec((T, HG_DK), lambda h, t: (nt - 1 - t, base + h))
    outc = pl.BlockSpec((T, HG_DK), lambda h, t: (nt - 1 - t, h))
    return pl.pallas_call(
        body,
        grid=(HG_HEADS, nt),
        in_specs=[
            rev(0), rev(8), rev(16),
            pl.BlockSpec((2, HG_DK), lambda h, t: (0, h)),
            pl.BlockSpec((1, nc, HG_DK, HG_DK), lambda h, t: (h, nt - 1 - t, 0, 0)),
            outc,
        ],
        out_specs=[pl.BlockSpec((T, 3 * HG_DK), lambda h, t: (nt - 1 - t, h)),
                   pl.BlockSpec((1, HG_DK), lambda h, t: (0, h))],
        out_shape=[jax.ShapeDtypeStruct((S, 3 * D_MODEL), BF16), jax.ShapeDtypeStruct((1, D_MODEL), F32)],
        scratch_shapes=[pltpu.VMEM((HG_DK, HG_DK), F32)],
        compiler_params=_cparams("parallel", "arbitrary"),
        name="hgrn_bwd",
    )(zmain, zmain, zmain, lb_logits, states, d_o)


def _t5_bucket_table():
    qi = jnp.arange(SWA_BLOCK)[:, None] + SWA_BLOCK
    kj = jnp.arange(2 * SWA_BLOCK)[None, :]
    n = jnp.clip(qi - kj, 0, SWA_WINDOW - 1)
    max_exact = NUM_BUCKETS // 2
    nf = jnp.maximum(n, 1).astype(F32)
    large = max_exact + (jnp.log(nf / max_exact) / math.log(MAX_DISTANCE / max_exact)
                         * (NUM_BUCKETS - max_exact)).astype(jnp.int32)
    large = jnp.minimum(large, NUM_BUCKETS - 1)
    return jnp.where(n < max_exact, n, large).astype(jnp.int32)


def _swa_valid(n):
    qi = lax.broadcasted_iota(jnp.int32, (SWA_BLOCK, 2 * SWA_BLOCK), 0) + SWA_BLOCK
    kj = lax.broadcasted_iota(jnp.int32, (SWA_BLOCK, 2 * SWA_BLOCK), 1)
    dist = qi - kj
    return (dist >= 0) & (dist < SWA_WINDOW) & ((n > 0) | (kj >= SWA_BLOCK))


def _swa_bias_init(bias, bucket_ref, rb_ref):
    bk = bucket_ref[...]
    for h in range(SWA_HEADS):
        def sel(b, acc, h=h):
            return jnp.where(bk == b, rb_ref[b, h], acc)
        bias[h] = lax.fori_loop(0, NUM_BUCKETS, sel, jnp.zeros(bk.shape, F32))


def _lane_halves(t, kv_head):
    lane = lax.broadcasted_iota(jnp.int32, t.shape, 1)
    rolled = pltpu.roll(t, 64, 1)
    zero = jnp.zeros_like(t)
    if kv_head == 0:
        return jnp.where(lane < 64, t, zero), jnp.where(lane >= 64, rolled, zero)
    return jnp.where(lane < 64, rolled, zero), jnp.where(lane >= 64, t, zero)


def _swa_probs(s, bias_h, valid, sink):
    s = jnp.where(valid, s + bias_h, -jnp.inf)
    m = jnp.maximum(jnp.max(s, axis=-1, keepdims=True), sink)
    p = jnp.exp(s - m)
    es = jnp.exp(sink - m)
    inv = 1.0 / (jnp.sum(p, axis=-1, keepdims=True) + es)
    return p * inv, es * inv


def _swa_fwd(zmain, bucket, rel_bias, sinks):
    S = zmain.shape[0]
    nb = S // SWA_BLOCK
    scale = SWA_HEAD_DIM ** -0.5

    def body(q_ref, kvc_ref, kvp_ref, bucket_ref, rb_ref, sk_ref, o_ref, bias):
        n = pl.program_id(0)

        @pl.when(n == 0)
        def _():
            _swa_bias_init(bias, bucket_ref, rb_ref)

        valid = _swa_valid(n)
        kk = _bf(jnp.concatenate([kvp_ref[:, 0:128], kvc_ref[:, 0:128]], axis=0))
        vv = _bf(jnp.concatenate([kvp_ref[:, 128:256], kvc_ref[:, 128:256]], axis=0))
        for kvh in range(2):
            ka, kb = _lane_halves(kk, kvh)
            va, vb = _lane_halves(vv, kvh)
            qst = _bf(jnp.concatenate([q_ref[:, pl.ds((kvh * 4 + jj) * 128, 128)] for jj in range(4)], axis=0) * scale)
            probs = []
            for odd, kx in enumerate((ka, kb)):
                s = _dot_nt(qst, kx)
                parts = []
                for jj in range(4):
                    h = 2 * (kvh * 4 + jj) + odd
                    p, _ = _swa_probs(s[jj * SWA_BLOCK:(jj + 1) * SWA_BLOCK], bias[h], valid, sk_ref[0, h])
                    parts.append(_bf(p))
                probs.append(jnp.concatenate(parts, axis=0))
            ost = _dot(probs[0], va) + _dot(probs[1], vb)
            for jj in range(4):
                o_ref[:, pl.ds((kvh * 4 + jj) * 128, 128)] = ost[jj * SWA_BLOCK:(jj + 1) * SWA_BLOCK]

    smem = pl.BlockSpec(memory_space=pltpu.SMEM)
    return pl.pallas_call(
        body,
        grid=(nb,),
        in_specs=[
            pl.BlockSpec((SWA_BLOCK, 1024), lambda n: (n, C_SQ // 1024)),
            pl.BlockSpec((SWA_BLOCK, 256), lambda n: (n, C_SK // 256)),
            pl.BlockSpec((SWA_BLOCK, 256), lambda n: (jnp.maximum(n - 1, 0), C_SK // 256)),
            _const_spec((SWA_BLOCK, 2 * SWA_BLOCK)), smem, smem,
        ],
        out_specs=pl.BlockSpec((SWA_BLOCK, 1024), lambda n: (n, 0)),
        out_shape=jax.ShapeDtypeStruct((S, 1024), F32),
        scratch_shapes=[pltpu.VMEM((SWA_HEADS, SWA_BLOCK, 2 * SWA_BLOCK), F32)],
        compiler_params=_cparams("arbitrary"),
        name="swa_fwd",
    )(zmain, zmain, zmain, bucket, rel_bias, sinks)


def _swa_bwd(zmain, o_b, d_o, bucket, rel_bias, sinks, dep):
    S = zmain.shape[0]
    nb = S // SWA_BLOCK
    scale = SWA_HEAD_DIM ** -0.5

    def body(q_ref, kvc_ref, kvp_ref, o_ref, do_ref, bucket_ref, rb_ref, sk_ref, dep_ref,
             dq_ref, dkv_ref, drb_ref, dsk_ref, bias, dbias, carry):
        del dep_ref
        n = pl.program_id(0)

        @pl.when(n == 0)
        def _():
            _swa_bias_init(bias, bucket_ref, rb_ref)
            dbias[...] = jnp.zeros_like(dbias)
            carry[...] = jnp.zeros_like(carry)
            dsk_ref[...] = jnp.zeros_like(dsk_ref)

        @pl.when(n < nb)
        def _():
            valid = _swa_valid(n)
            kk = _bf(jnp.concatenate([kvp_ref[:, 0:128], kvc_ref[:, 0:128]], axis=0))
            vv = _bf(jnp.concatenate([kvp_ref[:, 128:256], kvc_ref[:, 128:256]], axis=0))
            lane = lax.broadcasted_iota(jnp.int32, (2 * SWA_BLOCK, 128), 1)
            lane_q = lax.broadcasted_iota(jnp.int32, (4 * SWA_BLOCK, 128), 1)
            dk_parts, dv_parts = [], []
            for kvh in range(2):
                ka, kb = _lane_halves(kk, kvh)
                va, vb = _lane_halves(vv, kvh)
                pair_cols = [pl.ds((kvh * 4 + jj) * 128, 128) for jj in range(4)]
                qst = _bf(jnp.concatenate([q_ref[:, cl] for cl in pair_cols], axis=0) * scale)
                dost = jnp.concatenate([do_ref[:, cl] for cl in pair_cols], axis=0)
                prod = dost.astype(F32) * jnp.concatenate([o_ref[:, cl] for cl in pair_cols], axis=0)
                dq_st = jnp.zeros((4 * SWA_BLOCK, 128), F32)
                zks, zvs = [], []
                for odd, (kx, vx) in enumerate(((ka, va), (kb, vb))):
                    s = _dot_nt(qst, kx)
                    keep = (lane_q >= 64) if odd else (lane_q < 64)
                    delta = jnp.sum(jnp.where(keep, prod, 0.0), axis=-1, keepdims=True)
                    dp = _dot_nt(dost, vx)
                    p_parts, ds_parts = [], []
                    for jj in range(4):
                        h = 2 * (kvh * 4 + jj) + odd
                        rows = slice(jj * SWA_BLOCK, (jj + 1) * SWA_BLOCK)
                        p, ps = _swa_probs(s[rows], bias[h], valid, sk_ref[0, h])
                        ds = p * (dp[rows] - delta[rows])
                        dbias[h] += ds
                        dsk_ref[h] += jnp.broadcast_to(-jnp.sum(ps * delta[rows], axis=0, keepdims=True), (8, 128))
                        p_parts.append(_bf(p))
                        ds_parts.append(_bf(ds))
                    pst = jnp.concatenate(p_parts, axis=0)
                    dsst = jnp.concatenate(ds_parts, axis=0)
                    dq_st = dq_st + _dot(dsst, kx)
                    zks.append(_dot_tn(dsst, qst))
                    zvs.append(_dot_tn(pst, dost))
                for jj in range(4):
                    dq_ref[:, pair_cols[jj]] = _bf(dq_st[jj * SWA_BLOCK:(jj + 1) * SWA_BLOCK] * scale)
                zk = jnp.where(lane < 64, zks[0], zks[1])
                zv = jnp.where(lane < 64, zvs[0], zvs[1])
                dk_parts.append(zk + pltpu.roll(zk, 64, 1))
                dv_parts.append(zv + pltpu.roll(zv, 64, 1))
            dk = jnp.where(lane < 64, dk_parts[0], dk_parts[1])
            dv = jnp.where(lane < 64, dv_parts[0], dv_parts[1])
            dkv = jnp.concatenate([dk, dv], axis=1)
            dkv_ref[...] = _bf(carry[...] + dkv[0:SWA_BLOCK])
            carry[...] = dkv[SWA_BLOCK:]

        @pl.when(n == nb)
        def _():
            dkv_ref[...] = _bf(carry[...])
            bk = bucket_ref[...]

            def per_head(h, _):
                db = dbias[h]

                def per_bucket(b, _):
                    tot = jnp.sum(jnp.where(bk == b, db, 0.0), axis=1, keepdims=True)
                    tot = jnp.sum(tot, axis=0, keepdims=True)
                    drb_ref[h * NUM_BUCKETS + b] = jnp.broadcast_to(tot, (8, 128))
                    return 0

                return lax.fori_loop(0, NUM_BUCKETS, per_bucket, 0)

            lax.fori_loop(0, SWA_HEADS, per_head, 0)

    smem = pl.BlockSpec(memory_space=pltpu.SMEM)
    cur = lambda n: jnp.minimum(n, nb - 1)
    prev = lambda n: jnp.maximum(jnp.minimum(n, nb - 1) - 1, 0)
    return pl.pallas_call(
        body,
        grid=(nb + 1,),
        in_specs=[
            pl.BlockSpec((SWA_BLOCK, 1024), lambda n: (cur(n), C_SQ // 1024)),
            pl.BlockSpec((SWA_BLOCK, 256), lambda n: (cur(n), C_SK // 256)),
            pl.BlockSpec((SWA_BLOCK, 256), lambda n: (prev(n), C_SK // 256)),
            pl.BlockSpec((SWA_BLOCK, 1024), lambda n: (cur(n), 0)),
            pl.BlockSpec((SWA_BLOCK, 1024), lambda n: (cur(n), 0)),
            _const_spec((SWA_BLOCK, 2 * SWA_BLOCK)), smem, smem, _dep_spec(),
        ],
        out_specs=[
            pl.BlockSpec((SWA_BLOCK, 1024), lambda n: (cur(n), 0)),
            pl.BlockSpec((SWA_BLOCK, 256), lambda n: (jnp.maximum(n - 1, 0), 0)),
            pl.BlockSpec((SWA_HEADS * NUM_BUCKETS, 8, 128), lambda n: (0, 0, 0)),
            pl.BlockSpec((SWA_HEADS, 8, 128), lambda n: (0, 0, 0)),
        ],
        out_shape=[
            jax.ShapeDtypeStruct((S, 1024), BF16),
            jax.ShapeDtypeStruct((S, 256), BF16),
            jax.ShapeDtypeStruct((SWA_HEADS * NUM_BUCKETS, 8, 128), F32),
            jax.ShapeDtypeStruct((SWA_HEADS, 8, 128), F32),
        ],
        scratch_shapes=[
            pltpu.VMEM((SWA_HEADS, SWA_BLOCK, 2 * SWA_BLOCK), F32),
            pltpu.VMEM((SWA_HEADS, SWA_BLOCK, 2 * SWA_BLOCK), F32),
            pltpu.VMEM((SWA_BLOCK, 256), F32),
        ],
        compiler_params=_cparams("arbitrary"),
        name="swa_bwd",
    )(zmain, zmain, zmain, o_b, d_o, bucket, rel_bias, sinks, dep)


def _mem_probs(q_ref, k):
    qs = _bf(q_ref[...] * (MEM_HEAD_DIM ** -0.5))
    s = _dot_nt(qs, k)
    e = jnp.exp(s - jnp.max(s, axis=-1, keepdims=True))
    return qs, e / jnp.sum(e, axis=-1, keepdims=True)


def _mem_q_specs(T):
    return [pl.BlockSpec((T, MEM_HEAD_DIM), lambda t, h=h: (t, C_MQ // MEM_HEAD_DIM + h)) for h in range(MEM_HEADS)]


def _mem_kv_proj(mem, g2):
    def body(mem_ref, w_ref, o_ref):
        o_ref[...] = _dot_nt(_bf(mem_ref[...]), _rows(w_ref))

    return pl.pallas_call(
        body,
        grid=(1,),
        in_specs=[pl.BlockSpec((MEM_LEN, D_MODEL), lambda i: (0, 0)), _gathered_spec(R_KV, R_OTHER)],
        out_specs=pl.BlockSpec((MEM_LEN, 2048), lambda i: (0, 0)),
        out_shape=jax.ShapeDtypeStruct((MEM_LEN, 2048), F32),
        compiler_params=_cparams("arbitrary"),
        name="mem_kv_proj",
    )(mem, g2)


def _mem_fwd(zmain, mkv, *, T):
    S = zmain.shape[0]

    def body(q0, q1, q2, q3, kv_ref, o_ref):
        for h, q_ref in enumerate((q0, q1, q2, q3)):
            cols = pl.ds(h * MEM_HEAD_DIM, MEM_HEAD_DIM)
            _, p = _mem_probs(q_ref, _bf(kv_ref[:, cols]))
            o_ref[:, cols] = _dot(_bf(p), _bf(kv_ref[:, pl.ds(1024 + h * MEM_HEAD_DIM, MEM_HEAD_DIM)]))

    return pl.pallas_call(
        body,
        grid=(S // T,),
        in_specs=_mem_q_specs(T) + [_const_spec((MEM_LEN, 2048))],
        out_specs=pl.BlockSpec((T, 1024), lambda t: (t, 0)),
        out_shape=jax.ShapeDtypeStruct((S, 1024), F32),
        compiler_params=_cparams("parallel"),
        name="mem_fwd",
    )(zmain, zmain, zmain, zmain, mkv)


def _mem_bwd(zmain, mkv, o_c, d_o, *, T):
    S = zmain.shape[0]
    scale = MEM_HEAD_DIM ** -0.5

    def body(q0, q1, q2, q3, kv_ref, o_ref, do_ref, dq_ref, dkv_ref):
        @pl.when(pl.program_id(0) == 0)
        def _():
            dkv_ref[...] = jnp.zeros_like(dkv_ref)

        for h, q_ref in enumerate((q0, q1, q2, q3)):
            cols = pl.ds(h * MEM_HEAD_DIM, MEM_HEAD_DIM)
            vcols = pl.ds(1024 + h * MEM_HEAD_DIM, MEM_HEAD_DIM)
            kb = _bf(kv_ref[:, cols])
            qs, p = _mem_probs(q_ref, kb)
            dob = do_ref[:, cols]
            delta = jnp.sum(dob.astype(F32) * o_ref[:, cols], axis=-1, keepdims=True)
            ds = _bf(p * (_dot_nt(dob, _bf(kv_ref[:, vcols])) - delta))
            dq_ref[:, cols] = _bf(_dot(ds, kb) * scale)
            dkv_ref[:, cols] += _dot_tn(ds, qs)
            dkv_ref[:, vcols] += _dot_tn(_bf(p), dob)

    row = pl.BlockSpec((T, 1024), lambda t: (t, 0))
    return pl.pallas_call(
        body,
        grid=(S // T,),
        in_specs=_mem_q_specs(T) + [_const_spec((MEM_LEN, 2048)), row, row],
        out_specs=[row, pl.BlockSpec((MEM_LEN, 2048), lambda t: (0, 0))],
        out_shape=[jax.ShapeDtypeStruct((S, 1024), BF16), jax.ShapeDtypeStruct((MEM_LEN, 2048), F32)],
        compiler_params=_cparams("arbitrary"),
        name="mem_bwd",
    )(zmain, zmain, zmain, zmain, mkv, o_c, d_o)


def _layer_norm(u):
    mu = jnp.mean(u, axis=-1, keepdims=True)
    xc = u - mu
    rstd = lax.rsqrt(jnp.mean(xc * xc, axis=-1, keepdims=True) + LN_EPS)
    return xc * rstd, rstd


def _layer_norm_bwd(dy, gamma, xhat, rstd):
    dxh = dy * gamma
    return rstd * (dxh - jnp.mean(dxh, axis=-1, keepdims=True) - xhat * jnp.mean(dxh * xhat, axis=-1, keepdims=True))


def _merge_forward(oraw_ref, hg_ref, ob_ref, oc_ref, gl_ref, x_ref, gain_ref, wbh, wbs, wbm, wout):
    ys, rs = [], []
    for h in range(HG_HEADS):
        oh = oraw_ref[:, pl.ds(h * HG_DK, HG_DK)]
        r = lax.rsqrt(jnp.mean(oh * oh, axis=-1, keepdims=True) + RMS_EPS)
        ys.append(oh * r)
        rs.append(r)
    y = jnp.concatenate(ys, axis=1)
    hg = _f32(hg_ref[...])
    sg = _sig(hg)
    silu = hg * sg
    oa = _bf(y * gain_ref[...] * silu)
    pa = _dot(oa, _rows(wbh))
    pb = _dot(_bf(ob_ref[...]), _rows(wbs))
    pc = _dot(_bf(oc_ref[...]), _rows(wbm))
    g0 = _sig(_f32(gl_ref[:, 0:1024]))
    g1 = _sig(_f32(gl_ref[:, 1024:2048]))
    g2 = _sig(_f32(gl_ref[:, 2048:3072]))
    m = _bf(g0 * pa + g1 * pb + g2 * pc)
    u1 = ALPHA * x_ref[...] + _dot(m, _rows(wout))
    xhat, rstd = _layer_norm(u1)
    return dict(y=y, rs=rs, hg=hg, sg=sg, silu=silu, oa=oa, pa=pa, pb=pb, pc=pc,
                g0=g0, g1=g1, g2=g2, m=m, xhat=xhat, rstd=rstd)


def _gathered_spec(lo, hi):
    n = hi - lo
    return pl.BlockSpec((N_DEV, n, D_MODEL), lambda *_: (0, lo // n, 0), pipeline_mode=pl.Buffered(1))


def _rows(w_ref):
    return w_ref[...].reshape(-1, D_MODEL)


def _merge_in_specs(T):
    row = lambda w, c=0: pl.BlockSpec((T, w), lambda i: (i, c))
    vec = pl.BlockSpec((1, D_MODEL), lambda i: (0, 0))
    w = [_gathered_spec(lo, hi) for lo, hi in ((R_BH, R_BS), (R_BS, R_BM), (R_BM, R_OUT), (R_OUT, R_KV))]
    return [row(1024), row(1024, C_HG // 1024), row(1024), row(1024), row(3072), row(1024), vec, *w, vec, vec]


def _merge_fwd(o_raw, zmain, o_b, o_c, gl, x, gain, wbh, wbs, wbm, wout, ln_g, ln_b, *, T):
    S = x.shape[0]

    def body(oraw_ref, hg_ref, ob_ref, oc_ref, gl_ref, x_ref, gain_ref, wbh_r, wbs_r, wbm_r, wout_r, g_ref, b_ref,
             h1_ref, h1b_ref):
        f = _merge_forward(oraw_ref, hg_ref, ob_ref, oc_ref, gl_ref, x_ref, gain_ref, wbh_r, wbs_r, wbm_r, wout_r)
        h1 = f["xhat"] * g_ref[...] + b_ref[...]
        h1_ref[...] = h1
        h1b_ref[...] = _bf(h1)

    row = pl.BlockSpec((T, D_MODEL), lambda i: (i, 0))
    return pl.pallas_call(
        body,
        grid=(S // T,),
        in_specs=_merge_in_specs(T),
        out_specs=[row, row],
        out_shape=[jax.ShapeDtypeStruct((S, D_MODEL), F32), jax.ShapeDtypeStruct((S, D_MODEL), BF16)],
        compiler_params=_cparams("parallel"),
        name="merge_fwd",
    )(o_raw, zmain, o_b, o_c, gl, x, gain, wbh, wbs, wbm, wout, ln_g, ln_b)


def _merge_bwd(d_h1, o_raw, zmain, o_b, o_c, gl, x, gain, wbh, wbs, wbm, wout, ln_g, ln_b, *, T):
    S = x.shape[0]

    def body(dh1_ref, oraw_ref, hg_ref, ob_ref, oc_ref, gl_ref, x_ref, gain_ref, wbh_r, wbs_r, wbm_r, wout_r, g_ref, b_ref,
             dx_ref, du1_ref, m_ref, oa_ref, dpa_ref, dpb_ref, dpc_ref, doraw_ref, dob_ref, doc_ref, dz_ref,
             dgain_ref, dg_ref, db_ref):
        del b_ref

        @pl.when(pl.program_id(0) == 0)
        def _():
            dgain_ref[...] = jnp.zeros_like(dgain_ref)
            dg_ref[...] = jnp.zeros_like(dg_ref)
            db_ref[...] = jnp.zeros_like(db_ref)

        f = _merge_forward(oraw_ref, hg_ref, ob_ref, oc_ref, gl_ref, x_ref, gain_ref, wbh_r, wbs_r, wbm_r, wout_r)
        dh1 = dh1_ref[...]
        dg_ref[...] += jnp.sum(dh1 * f["xhat"], axis=0, keepdims=True)
        db_ref[...] += jnp.sum(dh1, axis=0, keepdims=True)
        du1 = _layer_norm_bwd(dh1, g_ref[...], f["xhat"], f["rstd"])
        dx_ref[...] = ALPHA * du1
        du1b = _bf(du1)
        du1_ref[...] = du1b
        m_ref[...] = f["m"]
        oa_ref[...] = f["oa"]
        dm = _dot_nt(du1b, _rows(wout_r))
        for i, (g, p, dp_ref, dob_r, w_r) in enumerate((
                (f["g0"], f["pa"], dpa_ref, None, wbh_r),
                (f["g1"], f["pb"], dpb_ref, dob_ref, wbs_r),
                (f["g2"], f["pc"], dpc_ref, doc_ref, wbm_r))):
            dz_ref[:, pl.ds((i + 1) * 1024, 1024)] = _bf(dm * p * g * (1.0 - g))
            dp = _bf(dm * g)
            dp_ref[...] = dp
            d_branch = _dot_nt(dp, _rows(w_r))
            if dob_r is not None:
                dob_r[...] = _bf(d_branch)
            else:
                doa = d_branch
        gain = gain_ref[...]
        t = doa * f["y"]
        dgain_ref[...] += jnp.sum(t * f["silu"], axis=0, keepdims=True)
        sg = f["sg"]
        dz_ref[:, 0:1024] = _bf(t * gain * sg * (1.0 + f["hg"] * (1.0 - sg)))
        dy = doa * gain * f["silu"]
        for h in range(HG_HEADS):
            cols = slice(h * HG_DK, (h + 1) * HG_DK)
            yh = f["y"][:, cols]
            dyh = dy[:, cols]
            doraw_ref[:, pl.ds(h * HG_DK, HG_DK)] = _bf(
                f["rs"][h] * (dyh - yh * jnp.mean(dyh * yh, axis=-1, keepdims=True)))

    row = lambda w: pl.BlockSpec((T, w), lambda i: (i, 0))
    vec = pl.BlockSpec((1, D_MODEL), lambda i: (0, 0))
    bshape = jax.ShapeDtypeStruct((S, D_MODEL), BF16)
    vshape = jax.ShapeDtypeStruct((1, D_MODEL), F32)
    return pl.pallas_call(
        body,
        grid=(S // T,),
        in_specs=[row(1024)] + _merge_in_specs(T),
        out_specs=[row(1024)] * 10 + [row(4096), vec, vec, vec],
        out_shape=[jax.ShapeDtypeStruct((S, D_MODEL), F32)] + [bshape] * 9
        + [jax.ShapeDtypeStruct((S, 4096), BF16), vshape, vshape, vshape],
        compiler_params=_cparams("arbitrary"),
        name="merge_bwd",
    )(d_h1, o_raw, zmain, o_b, o_c, gl, x, gain, wbh, wbs, wbm, wout, ln_g, ln_b)


def _mlp_fwd_bwd(h1, target, wup_t, wdn, ln_g, ln_b, *, T, FC):
    S = h1.shape[0]
    nf = D_FF // FC
    assert FC == R_BH - R_UP == R_UP - R_DN

    def body(h1_ref, t_ref, wup_ref, wdn_ref, g_ref, b_ref, dh1_ref, a_ref, dup_ref, du2_ref, loss_ref, dg_ref, db_ref, up_scr):
        @pl.when(pl.program_id(0) == 0)
        def _():
            loss_ref[...] = jnp.zeros_like(loss_ref)
            dg_ref[...] = jnp.zeros_like(dg_ref)
            db_ref[...] = jnp.zeros_like(db_ref)

        h1v = h1_ref[...]
        h1b = _bf(h1v)
        ff = jnp.zeros((T, D_MODEL), F32)
        for j in range(nf):
            rows = pl.ds(j * FC, FC)
            up = jnp.maximum(_dot_nt(h1b, wup_ref[j]), 0.0)
            up_scr[:, rows] = _bf(up)
            a = _bf(up * up)
            a_ref[:, rows] = a
            ff = ff + _dot(a, wdn_ref[j])
        xhat, rstd = _layer_norm(ALPHA * h1v + ff)
        gamma = g_ref[...]
        err = xhat * gamma + b_ref[...] - t_ref[...]
        loss_ref[...] += jnp.sum(jnp.sum(err * err, axis=-1, keepdims=True), axis=0, keepdims=True) * (0.5 / D_MODEL)
        dy = err * (1.0 / D_MODEL)
        dg_ref[...] += jnp.sum(dy * xhat, axis=0, keepdims=True)
        db_ref[...] += jnp.sum(dy, axis=0, keepdims=True)
        du2 = _layer_norm_bwd(dy, gamma, xhat, rstd)
        du2b = _bf(du2)
        du2_ref[...] = du2b
        dh1 = ALPHA * du2
        for j in range(nf):
            rows = pl.ds(j * FC, FC)
            dup = _bf(_dot_nt(du2b, wdn_ref[j]) * (2.0 * up_scr[:, rows].astype(F32)))
            dup_ref[:, rows] = dup
            dh1 = dh1 + _dot(dup, wup_ref[j])
        dh1_ref[...] = dh1

    row = lambda w: pl.BlockSpec((T, w), lambda i: (i, 0))
    vec = pl.BlockSpec((1, D_MODEL), lambda i: (0, 0))
    vshape = jax.ShapeDtypeStruct((1, D_MODEL), F32)
    return pl.pallas_call(
        body,
        grid=(S // T,),
        in_specs=[row(1024), row(1024), _gathered_spec(R_UP, R_BH), _gathered_spec(R_DN, R_UP), vec, vec],
        out_specs=[row(1024), row(D_FF), row(D_FF), row(1024), pl.BlockSpec((8, 128), lambda i: (0, 0)), vec, vec],
        out_shape=[
            jax.ShapeDtypeStruct((S, D_MODEL), F32),
            jax.ShapeDtypeStruct((S, D_FF), BF16),
            jax.ShapeDtypeStruct((S, D_FF), BF16),
            jax.ShapeDtypeStruct((S, D_MODEL), BF16),
            jax.ShapeDtypeStruct((8, 128), F32), vshape, vshape,
        ],
        scratch_shapes=[pltpu.VMEM((T, D_FF), BF16)],
        compiler_params=_cparams("arbitrary"),
        name="mlp_fwd_bwd",
    )(h1, target, wup_t, wdn, ln_g, ln_b)


def _local_step(x, mem, target, lb_logits, gain, sinks, rel_bias, ln1_g, ln1_b, ln2_g, ln2_b,
                win_t, dep0, other_weights, send_other_grads, send_win_grad):
    S = x.shape[0]
    T = min(256, S)
    KC = min(2048, S)
    zmain, xb = _mm_nt(x, win_t, n_cols=C_GL, tm=min(512, S), tn=C_GL, out_dtype=BF16, name="in_proj_main", dep=dep0,
                       also_a_bf16=True)
    gl = _mm_nt(x, win_t[C_GL:], tm=min(512, S), tn=1536, out_dtype=BF16, name="in_proj_gates")
    bucket = _t5_bucket_table()

    o_raw, states = _hgrn_fwd(zmain, lb_logits, T=min(1024, S))
    o_b = _swa_fwd(zmain, bucket, rel_bias, sinks)
    g2 = other_weights(o_b)
    mkv = _mem_kv_proj(mem, g2)
    o_c = _mem_fwd(zmain, mkv, T=min(512, S))
    merge_args = (o_raw, zmain, o_b, o_c, gl, x, gain, g2, g2, g2, g2, ln1_g, ln1_b)
    h1, h1b = _merge_fwd(*merge_args, T=T)

    d_h1, act, d_up, du2, loss, d_ln2_g, d_ln2_b = _mlp_fwd_bwd(h1, target, g2, g2, ln2_g, ln2_b, T=min(512, S), FC=512)
    wgrad = functools.partial(_mm_tn, out_dtype=BF16)
    g_wdn = wgrad(act, du2, kc=KC, name="grad_w_down")
    g_wup_t = wgrad(d_up, h1b, kc=KC, name="grad_w_up")

    (dx_part, du1, m, oa, dpa, dpb, dpc, d_oraw, d_ob, d_oc, d_hg_gl,
     d_gain, d_ln1_g, d_ln1_b) = _merge_bwd(d_h1, *merge_args, T=T)
    g_wout = wgrad(m, du1, kc=KC, name="grad_w_out")
    g_wbh = wgrad(oa, dpa, kc=KC, name="grad_w_branch_hg")
    g_wbs = wgrad(o_b, dpb, kc=KC, name="grad_w_branch_swa")
    g_wbm = wgrad(o_c, dpc, kc=KC, name="grad_w_branch_mem")

    d_mq, d_mkv = _mem_bwd(zmain, mkv, o_c, d_oc, T=min(512, S))
    g_wkv_t = wgrad(d_mkv, mem, kc=MEM_LEN, name="grad_w_mem_kv")
    sent_others = send_other_grads(
        dict(wkv_t=g_wkv_t, wbh=g_wbh, wbs=g_wbs, wbm=g_wbm, wout=g_wout, wup_t=g_wup_t, wdn=g_wdn))
    d_sq, d_skv, d_rb, d_sink = _swa_bwd(zmain, o_b, d_ob, bucket, rel_bias, sinks, sent_others)
    d_qfv, d_lb = _hgrn_bwd(zmain, lb_logits, states, d_oraw, T=min(1024, S))

    head_major = lambda a: a.reshape(3, HG_HEADS, HG_DK, D_MODEL).transpose(1, 0, 2, 3).reshape(3 * D_MODEL, D_MODEL)
    col_major = lambda a: a.reshape(HG_HEADS, 3, HG_DK, D_MODEL).transpose(1, 0, 2, 3).reshape(3 * D_MODEL, D_MODEL)
    pieces = (d_qfv, d_hg_gl, d_sq, d_skv, d_mq)
    g_qfv, g_hg_gl, g_sq, g_skv, g_mq = [
        wgrad(p, xb, kc=KC, name="grad_w_in_" + n) for p, n in zip(pieces, ("qfv", "hg_gates", "swa_q", "swa_kv", "mem_q"))]
    g_win_t = jnp.concatenate([col_major(g_qfv), g_hg_gl[:D_MODEL], g_sq, g_skv, g_mq, g_hg_gl[D_MODEL:]], axis=0)
    sent_win = send_win_grad(g_win_t)
    grad_x = _grad_x(*pieces, head_major(win_t[:C_HG]), win_t, dx_part, sent_win, tm=T)

    small = dict(
        d_lb=d_lb, d_gain=d_gain, d_sink=d_sink[:, 0, 0].reshape(1, SWA_HEADS),
        d_rb=d_rb[:, 0, 0].reshape(SWA_HEADS, NUM_BUCKETS).T,
        d_ln1_g=d_ln1_g, d_ln1_b=d_ln1_b, d_ln2_g=d_ln2_g, d_ln2_b=d_ln2_b, loss=loss[0, 0])
    return grad_x, small


MESH = pl.DeviceIdType.MESH
ANY = pl.BlockSpec(memory_space=pl.ANY)


def _coords():
    return lax.axis_index("x"), lax.axis_index("y"), lax.axis_index("c")


def _other_chips(x, y):
    return [(1 - x, y), (x, 1 - y), (1 - x, 1 - y)]


def _all_gather_weights(*arrays):
    na = len(arrays)

    def body(*refs):
        srcs, dsts = refs[:na], refs[na:2 * na]
        send_sems, recv_sems, local_sems = refs[2 * na:]
        x, y, c = _coords()
        me, sibling = (x, y, c), (x, y, 1 - c)
        chips = _other_chips(x, y)

        def slot(a, px, py, pc):
            return dsts[a].at[4 * px + 2 * py + pc]

        def copy(a, k, block, to, from_shard=False):
            return pltpu.make_async_remote_copy(
                src_ref=srcs[a] if from_shard else slot(a, *block), dst_ref=slot(a, *block),
                send_sem=send_sems.at[a * 7 + k], recv_sem=recv_sems.at[a * 7 + k],
                device_id=to, device_id_type=MESH)

        own = [pltpu.make_async_copy(srcs[a], slot(a, *me), local_sems.at[a]) for a in range(na)]
        for cp in own:
            cp.start()
        first = []
        for a in range(na):
            first.append(copy(a, 0, me, sibling, True))
            first += [copy(a, 1 + j, me, (*chip, c), True) for j, chip in enumerate(chips)]
        for cp in first:
            cp.start()
        passed = []
        for j, chip in enumerate(chips):
            for a in range(na):
                copy(a, 1 + j, (*chip, c), me).wait_recv()
                fwd = copy(a, 4 + j, (*chip, c), sibling)
                fwd.start()
                passed.append(fwd)
        for a in range(na):
            copy(a, 0, sibling, me).wait_recv()
            for j, chip in enumerate(chips):
                copy(a, 4 + j, (*chip, 1 - c), me).wait_recv()
        for cp in first + passed:
            cp.wait_send()
        for cp in own:
            cp.wait()

    return pl.pallas_call(
        body,
        in_specs=[ANY] * na,
        out_specs=[ANY] * na,
        out_shape=[jax.ShapeDtypeStruct((N_DEV,) + a.shape, a.dtype) for a in arrays],
        scratch_shapes=[pltpu.SemaphoreType.DMA((7 * na,)), pltpu.SemaphoreType.DMA((7 * na,)),
                        pltpu.SemaphoreType.DMA((na,))],
        name="all_gather_weights",
    )(*arrays)


HBM = pl.BlockSpec(memory_space=pltpu.HBM)
SEM = pl.BlockSpec(memory_space=pltpu.SEMAPHORE)
_DATAFLOW = pltpu.SideEffectType.DATAFLOW_SIDE_EFFECTING


def _peer(x, y, c, r):
    return x ^ (r >> 2), y ^ ((r >> 1) & 1), c ^ (r & 1)


def _direct_copies(src_ref, land_ref, send_sems, recv_sems, gather, receiving):
    x, y, c = _coords()
    me = 4 * x + 2 * y + c
    copies = []
    for r in range(1, N_DEV):
        px, py, pc = _peer(x, y, c, r)
        peer = 4 * px + 2 * py + pc
        if gather:
            src, dst = src_ref, land_ref.at[peer if receiving else me]
        else:
            src, dst = src_ref.at[peer], land_ref.at[r - 1]
        copies.append(pltpu.make_async_remote_copy(
            src_ref=src, dst_ref=dst, send_sem=send_sems.at[r - 1], recv_sem=recv_sems.at[r - 1],
            device_id=(px, py, pc), device_id_type=MESH))
    return copies


def _direct_start(src, land, *, gather, name):
    def body(src_ref, land_ref, send_sems, recv_sems, src_thru, land_thru, token):
        del src_thru, land_thru
        for cp in _direct_copies(src_ref, land_ref, send_sems, recv_sems, gather, False):
            cp.start()
        token[...] = jnp.zeros_like(token)

    return pl.pallas_call(
        body,
        name=name,
        out_shape=(pltpu.SemaphoreType.DMA((N_DEV - 1,)), pltpu.SemaphoreType.DMA((N_DEV - 1,)),
                   pltpu.HBM(src.shape, src.dtype), pltpu.HBM(land.shape, land.dtype),
                   jax.ShapeDtypeStruct((8, 128), F32)),
        in_specs=(HBM, HBM),
        out_specs=(SEM, SEM, HBM, HBM, pl.BlockSpec(memory_space=pltpu.VMEM)),
        input_output_aliases={0: 2, 1: 3},
        compiler_params=pltpu.CompilerParams(has_side_effects=_DATAFLOW),
    )(pltpu.with_memory_space_constraint(src, pltpu.HBM), pltpu.with_memory_space_constraint(land, pltpu.HBM))


def _direct_wait(send_sems, recv_sems, src_thru, land_thru, after, *, gather, name):
    def body(src_ref, land_ref, send_sems_ref, recv_sems_ref, after_ref, src_dead, got_ref):
        del after_ref, src_dead, got_ref
        for cp in _direct_copies(src_ref, land_ref, send_sems_ref, recv_sems_ref, gather, True):
            cp.wait_send()
            cp.wait_recv()

    return pl.pallas_call(
        body,
        name=name,
        out_shape=(pltpu.HBM(src_thru.shape, src_thru.dtype), pltpu.HBM(land_thru.shape, land_thru.dtype)),
        in_specs=(HBM, HBM, SEM, SEM, ANY),
        out_specs=(HBM, HBM),
        input_output_aliases={0: 0, 1: 1},
        compiler_params=pltpu.CompilerParams(has_side_effects=_DATAFLOW),
    )(src_thru, land_thru, send_sems, recv_sems, after)


def _sum_partials(src, land, me, *, tr, name):
    R = src.shape[1]

    def body(me_ref, s_ref, l_ref, o_ref):
        del me_ref
        acc = s_ref[0].astype(F32)
        for r in range(N_DEV - 1):
            acc = acc + l_ref[r].astype(F32)
        o_ref[...] = acc

    return pl.pallas_call(
        body,
        grid_spec=pltpu.PrefetchScalarGridSpec(
            num_scalar_prefetch=1, grid=(R // tr,),
            in_specs=[pl.BlockSpec((1, tr, 1024), lambda i, mr: (mr[0], i, 0)),
                      pl.BlockSpec((N_DEV - 1, tr, 1024), lambda i, mr: (0, i, 0))],
            out_specs=pl.BlockSpec((tr, 1024), lambda i, mr: (i, 0))),
        out_shape=jax.ShapeDtypeStruct((R, 1024), F32),
        name=name,
    )(me, src, land)


def _small_all_reduce(packed, lb_logits):
    def body(p_ref, lbl_ref, o_ref, gath, send_sems, recv_sems):
        x, y, c = _coords()
        mine = 4 * x + 2 * y + c
        gath[mine] = p_ref[...]
        copies = []
        for r in range(1, N_DEV):
            peer = (x ^ (r >> 2), y ^ ((r >> 1) & 1), c ^ (r & 1))
            copies.append(pltpu.make_async_remote_copy(
                src_ref=p_ref, dst_ref=gath.at[mine],
                send_sem=send_sems.at[r - 1], recv_sem=recv_sems.at[r - 1],
                device_id=peer, device_id_type=MESH))
        for cp in copies:
            cp.start()
        for r in range(1, N_DEV):
            peer_slot = 4 * (x ^ (r >> 2)) + 2 * (y ^ ((r >> 1) & 1)) + (c ^ (r & 1))
            pltpu.make_async_remote_copy(
                src_ref=p_ref, dst_ref=gath.at[peer_slot],
                send_sem=send_sems.at[r - 1], recv_sem=recv_sems.at[r - 1],
                device_id=(x, y, c), device_id_type=MESH).wait_recv()
        for cp in copies:
            cp.wait_send()
        tot = gath[0]
        for d in range(1, N_DEV):
            tot = tot + gath[d]
        o_ref[...] = tot
        lb = _lower_bound(lbl_ref)
        dl0 = o_ref[SM_LB:SM_LB + 1, :] * lb * (1.0 - lb)
        o_ref[SM_LB:SM_LB + 1, :] = dl0
        o_ref[SM_LB + 1:SM_LB + 2, :] = -dl0

    vm = pl.BlockSpec(memory_space=pltpu.VMEM)
    return pl.pallas_call(
        body,
        in_specs=[vm, vm],
        out_specs=vm,
        out_shape=jax.ShapeDtypeStruct(packed.shape, F32),
        scratch_shapes=[pltpu.VMEM((N_DEV,) + packed.shape, F32),
                        pltpu.SemaphoreType.DMA((N_DEV - 1,)), pltpu.SemaphoreType.DMA((N_DEV - 1,))],
        name="small_all_reduce",
    )(packed, lb_logits)


def _adamw(w, g, m, v, *, tr, name):
    R, C = w.shape

    def body(w_ref, g_ref, m_ref, v_ref, d_ref, nm_ref, nv_ref):
        gv = g_ref[...]
        nm = ADAM_B1 * m_ref[...] + (1.0 - ADAM_B1) * gv
        nv = ADAM_B2 * v_ref[...] + (1.0 - ADAM_B2) * jnp.square(gv)
        m_hat = nm / (1.0 - ADAM_B1 ** ADAM_STEP)
        v_hat = nv / (1.0 - ADAM_B2 ** ADAM_STEP)
        d_ref[...] = -ADAM_LR * (m_hat / (jnp.sqrt(v_hat) + ADAM_EPS) + ADAM_WD * w_ref[...])
        nm_ref[...] = nm
        nv_ref[...] = nv

    spec = pl.BlockSpec((tr, C), lambda i: (i, 0))
    return pl.pallas_call(
        body,
        grid=(R // tr,),
        in_specs=[spec] * 4,
        out_specs=[spec] * 3,
        out_shape=[jax.ShapeDtypeStruct((R, C), F32)] * 3,
        compiler_params=_cparams("parallel"),
        name=name,
    )(w, g, m, v)


def _pack_small(lb, gain, sinks, rel_bias, ln1_g, ln1_b, ln2_g, ln2_b, loss=None):
    pad = lambda a: jnp.pad(a.reshape(1, -1), ((0, 0), (0, D_MODEL - a.size)))
    rows = [lb.reshape(-1, D_MODEL)]
    if rows[0].shape[0] == 1:
        rows.append(jnp.zeros((1, D_MODEL), F32))
    rows += [gain.reshape(1, D_MODEL), pad(sinks), pad(rel_bias), ln1_g.reshape(1, D_MODEL), ln1_b.reshape(1, D_MODEL),
             ln2_g.reshape(1, D_MODEL), ln2_b.reshape(1, D_MODEL),
             pad(jnp.zeros((1,), F32) if loss is None else loss.reshape(1))]
    rows.append(jnp.zeros((SM_ROWS - SM_LOSS - 1, D_MODEL), F32))
    return jnp.concatenate(rows, axis=0)


def _unpack_small(p):
    return dict(
        lb_logits=p[SM_LB:SM_LB + 2], hg_norm_gain=p[SM_GAIN:SM_GAIN + 1], swa_sinks=p[SM_SINK:SM_SINK + 1, :SWA_HEADS],
        rel_bias=p[SM_RB, :NUM_BUCKETS * SWA_HEADS].reshape(NUM_BUCKETS, SWA_HEADS),
        ln1_g=p[SM_L1G:SM_L1G + 1], ln1_b=p[SM_L1B:SM_L1B + 1], ln2_g=p[SM_L2G:SM_L2G + 1], ln2_b=p[SM_L2B:SM_L2B + 1])


_SMALL = ("lb_logits", "hg_norm_gain", "swa_sinks", "rel_bias", "ln1_g", "ln1_b", "ln2_g", "ln2_b")
_WEIGHTS = ("w_in", "lb_logits", "hg_norm_gain", "swa_sinks", "rel_bias", "w_mem_kv", "w_branch_hg", "w_branch_swa",
            "w_branch_mem", "w_out", "ln1_g", "ln1_b", "w_up", "w_down", "ln2_g", "ln2_b")


def kernel(x, mem, w_in, lb_logits, hg_norm_gain, swa_sinks, rel_bias, w_mem_kv, w_branch_hg, w_branch_swa, w_branch_mem, w_out, ln1_g, ln1_b, w_up, w_down, ln2_g, ln2_b, loss_target, m_w_in, m_lb_logits, m_hg_norm_gain, m_swa_sinks, m_rel_bias, m_w_mem_kv, m_w_branch_hg, m_w_branch_swa, m_w_branch_mem, m_w_out, m_ln1_g, m_ln1_b, m_w_up, m_w_down, m_ln2_g, m_ln2_b, v_w_in, v_lb_logits, v_hg_norm_gain, v_swa_sinks, v_rel_bias, v_w_mem_kv, v_w_branch_hg, v_w_branch_swa, v_w_branch_mem, v_w_out, v_ln1_g, v_ln1_b, v_w_up, v_w_down, v_ln2_g, v_ln2_b):
    w = dict(w_in=w_in, lb_logits=lb_logits, hg_norm_gain=hg_norm_gain, swa_sinks=swa_sinks, rel_bias=rel_bias,
             w_mem_kv=w_mem_kv, w_branch_hg=w_branch_hg, w_branch_swa=w_branch_swa, w_branch_mem=w_branch_mem,
             w_out=w_out, ln1_g=ln1_g, ln1_b=ln1_b, w_up=w_up, w_down=w_down, ln2_g=ln2_g, ln2_b=ln2_b)
    mom = dict(w_in=m_w_in, lb_logits=m_lb_logits, hg_norm_gain=m_hg_norm_gain, swa_sinks=m_swa_sinks, rel_bias=m_rel_bias,
               w_mem_kv=m_w_mem_kv, w_branch_hg=m_w_branch_hg, w_branch_swa=m_w_branch_swa, w_branch_mem=m_w_branch_mem,
               w_out=m_w_out, ln1_g=m_ln1_g, ln1_b=m_ln1_b, w_up=m_w_up, w_down=m_w_down, ln2_g=m_ln2_g, ln2_b=m_ln2_b)
    var = dict(w_in=v_w_in, lb_logits=v_lb_logits, hg_norm_gain=v_hg_norm_gain, swa_sinks=v_swa_sinks, rel_bias=v_rel_bias,
               w_mem_kv=v_w_mem_kv, w_branch_hg=v_w_branch_hg, w_branch_swa=v_w_branch_swa, w_branch_mem=v_w_branch_mem,
               w_out=v_w_out, ln1_g=v_ln1_g, ln1_b=v_ln1_b, w_up=v_w_up, w_down=v_w_down, ln2_g=v_ln2_g, ln2_b=v_ln2_b)
    xc, yc, cc = _coords()

    p1 = _bf(w_in[0].T)
    p2 = _bf(jnp.concatenate([w_down[0], w_up[0].T, w_branch_hg[0], w_branch_swa[0], w_branch_mem[0], w_out[0],
                              w_mem_kv[0].T], axis=0))
    me = 4 * xc + 2 * yc + cc
    (g1,) = _all_gather_weights(p1)
    land2 = lax.dynamic_update_slice(lax.empty((N_DEV, R_OTHER, D_MODEL), BF16), p2[None], (me, 0, 0))
    ag2 = _direct_start(p2, land2, gather=True, name="gather_other_weights_start")

    def other_weights(after):
        return _direct_wait(*ag2[:4], after, gather=True, name="gather_other_weights_wait")[1]

    blocks = lambda a: a.reshape(N_DEV, a.shape[0] // N_DEV, D_MODEL)
    started = {}

    def send_other_grads(g):
        part = jnp.concatenate([blocks(g[k]) for k in ("wdn", "wup_t", "wbh", "wbs", "wbm", "wout", "wkv_t")], axis=1)
        started["others"] = _direct_start(part, lax.empty((N_DEV - 1, R_OTHER, D_MODEL), BF16), gather=False,
                                          name="scatter_other_grads_start")
        return started["others"][4]

    me1 = me.reshape(1).astype(jnp.int32)
    grads, delta, new_m, new_v = {}, {}, {}, {}

    def adamw(name):
        w2 = w[name][0]
        delta[name], new_m[name], new_v[name] = _adamw(
            w2, grads[name], mom[name][0], var[name][0], tr=w2.shape[0] // 4, name="adamw_" + name)

    def send_win_grad(g):
        started["win"] = _direct_start(blocks(g), lax.empty((N_DEV - 1, IN_SHARD, D_MODEL), BF16), gather=False,
                                       name="scatter_w_in_grad_start")
        mine2, landed2 = _direct_wait(*started["others"][:4], started["win"][4], gather=False,
                                      name="scatter_other_grads_wait")
        gs2 = _sum_partials(mine2, landed2, me1, tr=R_OTHER // 2, name="sum_other_grads")
        grads.update(
            w_down=gs2[R_DN:R_UP], w_up=gs2[R_UP:R_BH].T, w_branch_hg=gs2[R_BH:R_BS], w_branch_swa=gs2[R_BS:R_BM],
            w_branch_mem=gs2[R_BM:R_OUT], w_out=gs2[R_OUT:R_KV], w_mem_kv=gs2[R_KV:R_OTHER].T)
        for name in ("w_mem_kv", "w_branch_hg", "w_branch_swa", "w_branch_mem", "w_out", "w_up", "w_down"):
            adamw(name)
        return new_v["w_down"]

    grad_x, small = _local_step(
        x[0], mem[0], loss_target[0], lb_logits, hg_norm_gain, swa_sinks, rel_bias, ln1_g, ln1_b, ln2_g, ln2_b,
        g1.reshape(IN_COLS, D_MODEL), ag2[4], other_weights, send_other_grads, send_win_grad)

    mine1, landed1 = _direct_wait(*started["win"][:4], grad_x, gather=False, name="scatter_w_in_grad_wait")
    grads["w_in"] = _sum_partials(mine1, landed1, me1, tr=IN_SHARD // 2, name="sum_w_in_grad").T
    adamw("w_in")

    packed = _pack_small(small["d_lb"], small["d_gain"], small["d_sink"], small["d_rb"], small["d_ln1_g"],
                         small["d_ln1_b"], small["d_ln2_g"], small["d_ln2_b"], small["loss"])
    reduced = _small_all_reduce(packed, lb_logits)
    loss = reduced[SM_LOSS, 0]
    grads.update(_unpack_small(reduced))

    sm = lambda d: _pack_small(*[d[k] for k in _SMALL])
    d_s, m_s, v_s = _adamw(sm(w), reduced, sm(mom), sm(var), tr=SM_ROWS, name="adamw_small")
    for dst, src in ((delta, d_s), (new_m, m_s), (new_v, v_s)):
        dst.update(_unpack_small(src))

    def shaped(d, name):
        return d[name].reshape(w[name].shape)

    return (loss, grad_x[None], *[shaped(grads, n) for n in _WEIGHTS], *[shaped(delta, n) for n in _WEIGHTS],
            *[shaped(new_m, n) for n in _WEIGHTS], *[shaped(new_v, n) for n in _WEIGHTS])
```

```python
import functools
import math

import jax
import jax.numpy as jnp
from jax import lax
from jax.experimental import pallas as pl
from jax.experimental.pallas import tpu as pltpu

F32 = jnp.float32
BF16 = jnp.bfloat16

D_MODEL = 1024
MEM_LEN = 256
HG_HEADS = 8
HG_DK = 128
HG_CHUNK = 64
SWA_HEADS = 16
SWA_HEAD_DIM = 64
SWA_BLOCK = 128
SWA_WINDOW = 128
MEM_HEADS = 4
MEM_HEAD_DIM = 256
NUM_BUCKETS = 32
MAX_DISTANCE = 128
D_FF = 4096
LN_EPS = 1e-5
RMS_EPS = 1e-6
ALPHA = 2.0 ** 0.25
N_DEV = 8

C_HQ, C_HF, C_HI, C_HG, C_SQ, C_SK, C_SV, C_MQ, C_GL = 0, 1024, 2048, 3072, 4096, 5120, 5248, 5376, 6400
IN_COLS = 9472
IN_SHARD = IN_COLS // N_DEV

ADAM_LR = 0.001
ADAM_B1 = 0.9
ADAM_B2 = 0.999
ADAM_EPS = 1e-08
ADAM_WD = 0.01
ADAM_STEP = 10

VMEM_LIMIT = 58 * 1024 * 1024

R_DN, R_UP, R_BH, R_BS, R_BM, R_OUT, R_KV, R_OTHER = 0, 512, 1024, 1152, 1280, 1408, 1536, 1792

SM_LB, SM_GAIN, SM_SINK, SM_RB, SM_L1G, SM_L1B, SM_L2G, SM_L2B, SM_LOSS, SM_ROWS = 0, 2, 3, 4, 5, 6, 7, 8, 9, 16


def _bf(v):
    return v.astype(BF16)


def _f32(v):
    return v.astype(F32)


def _dot(a, b):
    return jnp.dot(a, b, preferred_element_type=F32)


def _dot_nt(a, b):
    return lax.dot_general(a, b, (((1,), (1,)), ((), ())), preferred_element_type=F32)


def _dot_tn(a, b):
    return lax.dot_general(a, b, (((0,), (0,)), ((), ())), preferred_element_type=F32)


def _sig(v):
    return 1.0 / (1.0 + jnp.exp(-v))


def _cparams(*sem):
    return pltpu.CompilerParams(dimension_semantics=sem, vmem_limit_bytes=VMEM_LIMIT)


def _const_spec(shape):
    nd = len(shape)
    return pl.BlockSpec(shape, lambda *_: (0,) * nd, pipeline_mode=pl.Buffered(1))


def _dep_spec():
    return pl.BlockSpec((8, 128), lambda *_: (0, 0))


def _mm_nt(a, bt, *, tm, tn, out_dtype, name, dep=None, n_cols=None, also_a_bf16=False):
    M, K = a.shape
    N = bt.shape[0] if n_cols is None else n_cols
    deps = () if dep is None else (dep,)
    nd = len(deps)
    assert not also_a_bf16 or tn == N

    def body(a_ref, b_ref, *rest):
        ab = _bf(a_ref[...])
        o_ref = rest[nd]
        o_ref[...] = _dot_nt(ab, _bf(b_ref[...])).astype(o_ref.dtype)
        if also_a_bf16:
            rest[nd + 1][...] = ab

    out_specs = [pl.BlockSpec((tm, tn), lambda j, i: (i, j))]
    out_shape = [jax.ShapeDtypeStruct((M, N), out_dtype)]
    if also_a_bf16:
        out_specs.append(pl.BlockSpec((tm, K), lambda j, i: (i, 0)))
        out_shape.append(jax.ShapeDtypeStruct((M, K), BF16))
    res = pl.pallas_call(
        body,
        grid=(N // tn, M // tm),
        in_specs=[pl.BlockSpec((tm, K), lambda j, i: (i, 0)), pl.BlockSpec((tn, K), lambda j, i: (j, 0))]
        + [_dep_spec() for _ in deps],
        out_specs=out_specs,
        out_shape=out_shape,
        compiler_params=_cparams("parallel", "parallel"),
        name=name,
    )(a, bt, *deps)
    return res if also_a_bf16 else res[0]


def _mm_tn_resident(a, b, *, tm, kc, name, out_dtype):
    K, M = a.shape
    N = b.shape[1]
    nk = K // kc

    def body(a_ref, b_ref, o_ref):
        acc = jnp.zeros((tm, N), F32)
        for kk in range(nk):
            sl = pl.ds(kk * kc, kc)
            acc = acc + _dot_tn(_bf(a_ref[sl, :]), _bf(b_ref[sl, :]))
        o_ref[...] = acc.astype(o_ref.dtype)

    return pl.pallas_call(
        body,
        grid=(M // tm,),
        in_specs=[pl.BlockSpec((K, tm), lambda i: (0, i)), _const_spec((K, N))],
        out_specs=pl.BlockSpec((tm, N), lambda i: (i, 0)),
        out_shape=jax.ShapeDtypeStruct((M, N), out_dtype),
        compiler_params=_cparams("parallel"),
        name=name,
    )(a, b)


def _mm_tn(a, b, *, kc, name, out_dtype=F32):
    K, M = a.shape
    N = b.shape[1]
    if M > 1024:
        return _mm_tn_resident(a, b, tm=256, kc=min(kc, 1024), name=name, out_dtype=out_dtype)
    tm = M
    nk = K // kc

    def body(a_ref, b_ref, o_ref, acc):
        k = pl.program_id(1)
        part = _dot_tn(_bf(a_ref[...]), _bf(b_ref[...]))

        @pl.when(k == 0)
        def _():
            acc[...] = part

        @pl.when(k > 0)
        def _():
            acc[...] += part

        @pl.when(k == nk - 1)
        def _():
            o_ref[...] = acc[...].astype(o_ref.dtype)

    return pl.pallas_call(
        body,
        grid=(M // tm, nk),
        in_specs=[pl.BlockSpec((kc, tm), lambda i, k: (k, i)), pl.BlockSpec((kc, N), lambda i, k: (k, 0))],
        out_specs=pl.BlockSpec((tm, N), lambda i, k: (i, 0)),
        out_shape=jax.ShapeDtypeStruct((M, N), out_dtype),
        scratch_shapes=[pltpu.VMEM((tm, N), F32)],
        compiler_params=_cparams("parallel", "arbitrary"),
        name=name,
    )(a, b)


def _grad_x(d_qfv, d_hg_gl, d_sq, d_skv, d_mq, w_qfv, win_t, add, dep, *, tm):
    M = add.shape[0]
    pieces = (d_qfv, d_hg_gl, d_sq, d_skv, d_mq)

    def body(qfv_ref, hggl_ref, sq_ref, skv_ref, mq_ref, wq_ref, w_ref, add_ref, dep_ref, o_ref):
        del dep_ref
        acc = add_ref[...] + _dot(qfv_ref[...], wq_ref[...])
        acc = acc + _dot(hggl_ref[:, 0:1024], w_ref[C_HG:C_SQ, :])
        acc = acc + _dot(hggl_ref[:, 1024:4096], w_ref[C_GL:IN_COLS, :])
        acc = acc + _dot(sq_ref[...], w_ref[C_SQ:C_SK, :])
        acc = acc + _dot(skv_ref[...], w_ref[C_SK:C_MQ, :])
        o_ref[...] = acc + _dot(mq_ref[...], w_ref[C_MQ:C_GL, :])

    return pl.pallas_call(
        body,
        grid=(M // tm,),
        in_specs=[pl.BlockSpec((tm, p.shape[1]), lambda i: (i, 0)) for p in pieces]
        + [_const_spec(w_qfv.shape), _const_spec(win_t.shape), pl.BlockSpec((tm, D_MODEL), lambda i: (i, 0)), _dep_spec()],
        out_specs=pl.BlockSpec((tm, D_MODEL), lambda i: (i, 0)),
        out_shape=jax.ShapeDtypeStruct((M, D_MODEL), F32),
        compiler_params=_cparams("parallel"),
        name="grad_x",
    )(*pieces, w_qfv, win_t, add, dep)


def _lower_bound(lbl_ref):
    l0 = lbl_ref[0:1, :]
    l1 = lbl_ref[1:2, :]
    mx = jnp.maximum(l0, l1)
    e0 = jnp.exp(l0 - mx)
    e1 = jnp.exp(l1 - mx)
    return e0 / (e0 + e1)


def _tri(lower):
    r = lax.broadcasted_iota(jnp.int32, (HG_CHUNK, HG_CHUNK), 0)
    c = lax.broadcasted_iota(jnp.int32, (HG_CHUNK, HG_CHUNK), 1)
    return (r >= c) if lower else (r <= c)


def _hg_gates(fl, lb):
    sg = _sig(fl)
    f = lb + (1.0 - lb) * sg
    return sg, f, jnp.log(f), 1.0 - f


def _scan_rows(v, reverse=False):
    row = lax.broadcasted_iota(jnp.int32, v.shape, 0)
    s = 1
    while s < HG_CHUNK:
        if reverse:
            v = v + jnp.where(row < HG_CHUNK - s, pltpu.roll(v, HG_CHUNK - s, 0), 0.0)
        else:
            v = v + jnp.where(row >= s, pltpu.roll(v, s, 0), 0.0)
        s *= 2
    return v


def _hgrn_fwd(zmain, lb_logits, *, T):
    S = zmain.shape[0]
    nc = T // HG_CHUNK

    def body(q_ref, f_ref, v_ref, lbl_ref, o_ref, st_ref, state):
        @pl.when(pl.program_id(1) == 0)
        def _():
            state[...] = jnp.zeros_like(state)

        lb = _lower_bound(lbl_ref)
        tril = _tri(True)
        qis, updates, decays, intra = [], [], [], []
        for c in range(nc):
            sl = pl.ds(c * HG_CHUNK, HG_CHUNK)
            _, _, g, k = _hg_gates(_f32(f_ref[sl, :]), lb)
            b = _scan_rows(g)
            bl = jnp.sum(g, axis=0, keepdims=True)
            qi = _bf(_f32(q_ref[sl, :]) * jnp.exp(b))
            ki = _bf(k * jnp.exp(-b))
            ko = _bf(k * jnp.exp(bl - b))
            vb = _bf(v_ref[sl, :])
            att = jnp.where(tril, _dot_nt(qi, ki), 0.0)
            intra.append(_dot(_bf(att), vb))
            qis.append(qi)
            updates.append(_dot_tn(vb, ko))
            decays.append(jnp.exp(bl))
        st = state[...]
        for c in range(nc):
            st_ref[0, c] = st
            o_ref[pl.ds(c * HG_CHUNK, HG_CHUNK), :] = intra[c] + _dot_nt(qis[c], _bf(st))
            st = st * decays[c] + updates[c]
        state[...] = st

    col = lambda base: pl.BlockSpec((T, HG_DK), lambda h, t: (t, base + h))
    return pl.pallas_call(
        body,
        grid=(HG_HEADS, S // T),
        in_specs=[col(0), col(8), col(16), pl.BlockSpec((2, HG_DK), lambda h, t: (0, h))],
        out_specs=[
            pl.BlockSpec((T, HG_DK), lambda h, t: (t, h)),
            pl.BlockSpec((1, nc, HG_DK, HG_DK), lambda h, t: (h, t, 0, 0)),
        ],
        out_shape=[
            jax.ShapeDtypeStruct((S, D_MODEL), F32),
            jax.ShapeDtypeStruct((HG_HEADS, S // HG_CHUNK, HG_DK, HG_DK), F32),
        ],
        scratch_shapes=[pltpu.VMEM((HG_DK, HG_DK), F32)],
        compiler_params=_cparams("parallel", "arbitrary"),
        name="hgrn_fwd",
    )(zmain, zmain, zmain, lb_logits)


def _hgrn_bwd(zmain, lb_logits, states, d_o, *, T):
    S = zmain.shape[0]
    nc = T // HG_CHUNK
    nt = S // T

    def body(q_ref, f_ref, v_ref, lbl_ref, st_ref, do_ref, dz_ref, dlb_ref, dstate):
        @pl.when(pl.program_id(1) == 0)
        def _():
            dstate[...] = jnp.zeros_like(dstate)
            dlb_ref[...] = jnp.zeros_like(dlb_ref)

        lb = _lower_bound(lbl_ref)
        tril = _tri(True)
        last_row = lax.broadcasted_iota(jnp.int32, (HG_CHUNK, HG_DK), 0) == HG_CHUNK - 1
        saved = []
        for c in range(nc):
            sl = pl.ds(c * HG_CHUNK, HG_CHUNK)
            sg, f, g, k = _hg_gates(_f32(f_ref[sl, :]), lb)
            b = _scan_rows(g)
            bl = jnp.sum(g, axis=0, keepdims=True)
            eb = jnp.exp(b)
            enb = jnp.exp(-b)
            eo = jnp.exp(bl - b)
            q_in = _f32(q_ref[sl, :]) * eb
            k_in = k * enb
            k_out = k * eo
            qi, ki, ko = _bf(q_in), _bf(k_in), _bf(k_out)
            vb = _bf(v_ref[sl, :])
            dob = do_ref[sl, :]
            att = jnp.where(tril, _dot_nt(qi, ki), 0.0)
            d_att = _bf(jnp.where(tril, _dot_nt(dob, vb), 0.0))
            d_kin = _dot_tn(d_att, qi)
            saved.append(dict(
                sg=sg, f=f, eb=eb, enb=enb, eo=eo, ebl=jnp.exp(bl), k_out=k_out, ko=ko, vb=vb, dob=dob,
                d_v=_dot_tn(_bf(att), dob), d_qin=_dot(d_att, ki), d_kin=d_kin,
                qk=(q_in, k_in), d_state=_dot_tn(dob, qi)))
        dst = dstate[...]
        dsts = [None] * nc
        for c in reversed(range(nc)):
            dsts[c] = dst
            dst = dst * saved[c]["ebl"] + saved[c]["d_state"]
        dstate[...] = dst
        dlb = jnp.zeros((1, HG_DK), F32)
        for c in range(nc):
            sl = pl.ds(c * HG_CHUNK, HG_CHUNK)
            s = saved[c]
            q_in, k_in = s["qk"]
            st = st_ref[0, c]
            dstb = _bf(dsts[c])
            d_v = s["d_v"] + _dot_nt(s["ko"], dstb)
            d_qin = s["d_qin"] + _dot(s["dob"], _bf(st))
            d_kout = _dot(s["vb"], dstb)
            d_decay = jnp.sum(dsts[c] * st, axis=0, keepdims=True)
            kk = d_kout * s["k_out"]
            d_b = d_qin * q_in - s["d_kin"] * k_in - kk
            d_bl = jnp.sum(kk, axis=0, keepdims=True) + d_decay * s["ebl"]
            d_g = _scan_rows(d_b + jnp.where(last_row, d_bl, 0.0), reverse=True)
            d_f = d_g / s["f"] - (s["d_kin"] * s["enb"] + d_kout * s["eo"])
            dz_ref[sl, 0:HG_DK] = _bf(d_qin * s["eb"])
            dz_ref[sl, HG_DK:2 * HG_DK] = _bf(d_f * (1.0 - lb) * s["sg"] * (1.0 - s["sg"]))
            dz_ref[sl, 2 * HG_DK:3 * HG_DK] = _bf(d_v)
            dlb = dlb + jnp.sum(d_f * (1.0 - s["sg"]), axis=0, keepdims=True)
        dlb_ref[...] += dlb

    rev = lambda base: pl.BlockSpec((T, HG_DK), lambda h, t: (nt - 1 - t, base + h))
    outc = pl.BlockSpec((T, HG_DK), lambda h, t: (nt - 1 - t, h))
    return pl.pallas_call(
        body,
        grid=(HG_HEADS, nt),
        in_specs=[
            rev(0), rev(8), rev(16),
            pl.BlockSpec((2, HG_DK), lambda h, t: (0, h)),
            pl.BlockSpec((1, nc, HG_DK, HG_DK), lambda h, t: (h, nt - 1 - t, 0, 0)),
            outc,
        ],
        out_specs=[pl.BlockSpec((T, 3 * HG_DK), lambda h, t: (nt - 1 - t, h)),
                   pl.BlockSpec((1, HG_DK), lambda h, t: (0, h))],
        out_shape=[jax.ShapeDtypeStruct((S, 3 * D_MODEL), BF16), jax.ShapeDtypeStruct((1, D_MODEL), F32)],
        scratch_shapes=[pltpu.VMEM((HG_DK, HG_DK), F32)],
        compiler_params=_cparams("parallel", "arbitrary"),
        name="hgrn_bwd",
    )(zmain, zmain, zmain, lb_logits, states, d_o)


def _t5_bucket_table():
    qi = jnp.arange(SWA_BLOCK)[:, None] + SWA_BLOCK
    kj = jnp.arange(2 * SWA_BLOCK)[None, :]
    n = jnp.clip(qi - kj, 0, SWA_WINDOW - 1)
    max_exact = NUM_BUCKETS // 2
    nf = jnp.maximum(n, 1).astype(F32)
    large = max_exact + (jnp.log(nf / max_exact) / math.log(MAX_DISTANCE / max_exact)
                         * (NUM_BUCKETS - max_exact)).astype(jnp.int32)
    large = jnp.minimum(large, NUM_BUCKETS - 1)
    return jnp.where(n < max_exact, n, large).astype(jnp.int32)


SWA_ROWS = 32


def _swa_bias_init(bias, bucket_ref, rb_ref):
    bk = bucket_ref[...]
    qi = lax.broadcasted_iota(jnp.int32, bk.shape, 0) + SWA_BLOCK
    kj = lax.broadcasted_iota(jnp.int32, bk.shape, 1)
    band = (qi - kj >= 0) & (qi - kj < SWA_WINDOW)
    for h in range(SWA_HEADS):
        def sel(b, acc, h=h):
            return jnp.where(bk == b, rb_ref[b, h], acc)
        t = lax.fori_loop(0, NUM_BUCKETS, sel, jnp.zeros(bk.shape, F32))
        bias[1, h] = jnp.where(band, t, -jnp.inf)
        bias[0, h] = jnp.where(band & (kj >= SWA_BLOCK), t, -jnp.inf)


def _lane_halves(t, kv_head):
    lane = lax.broadcasted_iota(jnp.int32, t.shape, 1)
    rolled = pltpu.roll(t, 64, 1)
    zero = jnp.zeros_like(t)
    if kv_head == 0:
        return jnp.where(lane < 64, t, zero), jnp.where(lane >= 64, rolled, zero)
    return jnp.where(lane < 64, rolled, zero), jnp.where(lane >= 64, t, zero)


def _swa_zero_key0(t):
    return jnp.where(lax.broadcasted_iota(jnp.int32, t.shape, 0) == 0, jnp.zeros_like(t), t)


def _swa_probs(s, masked_bias, sink):
    s = s + masked_bias
    m = jnp.maximum(jnp.max(s, axis=-1, keepdims=True), sink)
    p = jnp.exp(s - m)
    es = jnp.exp(sink - m)
    inv = 1.0 / (jnp.sum(p, axis=-1, keepdims=True) + es)
    return p * inv, es * inv


def _swa_fwd(zmain, bucket, rel_bias, sinks):
    S = zmain.shape[0]
    nb = S // SWA_BLOCK
    scale = SWA_HEAD_DIM ** -0.5

    def body(q_ref, kvc_ref, kvp_ref, bucket_ref, rb_ref, sk_ref, o_ref, p_ref, bias):
        n = pl.program_id(0)

        @pl.when(n == 0)
        def _():
            _swa_bias_init(bias, bucket_ref, rb_ref)

        later = jnp.minimum(n, 1)
        kk = _bf(jnp.concatenate([kvp_ref[:, 0:128], kvc_ref[:, 0:128]], axis=0))
        vv = _swa_zero_key0(_bf(jnp.concatenate([kvp_ref[:, 128:256], kvc_ref[:, 128:256]], axis=0)))
        first_col = lax.broadcasted_iota(jnp.int32, (SWA_ROWS, 2 * SWA_BLOCK), 1) == 0
        for kvh in range(2):
            ka, kb = _lane_halves(kk, kvh)
            va, vb = _lane_halves(vv, kvh)
            qst = _bf(jnp.concatenate([q_ref[:, pl.ds((kvh * 4 + jj) * 128, 128)] for jj in range(4)], axis=0) * scale)
            probs = []
            for odd, kx in enumerate((ka, kb)):
                s = _dot_nt(qst, kx)
                parts = []
                for jj in range(4):
                    h = 2 * (kvh * 4 + jj) + odd
                    for r0 in range(0, SWA_BLOCK, SWA_ROWS):
                        p, ps = _swa_probs(s[jj * SWA_BLOCK + r0:jj * SWA_BLOCK + r0 + SWA_ROWS],
                                           bias[later, h, pl.ds(r0, SWA_ROWS), :], sk_ref[0, h])
                        part = _bf(jnp.where(first_col, ps, p))
                        p_ref[pl.ds(r0, SWA_ROWS), pl.ds(h * 2 * SWA_BLOCK, 2 * SWA_BLOCK)] = part
                        parts.append(part)
                probs.append(jnp.concatenate(parts, axis=0))
            ost = _dot(probs[0], va) + _dot(probs[1], vb)
            for jj in range(4):
                o_ref[:, pl.ds((kvh * 4 + jj) * 128, 128)] = ost[jj * SWA_BLOCK:(jj + 1) * SWA_BLOCK]

    smem = pl.BlockSpec(memory_space=pltpu.SMEM)
    return pl.pallas_call(
        body,
        grid=(nb,),
        in_specs=[
            pl.BlockSpec((SWA_BLOCK, 1024), lambda n: (n, C_SQ // 1024)),
            pl.BlockSpec((SWA_BLOCK, 256), lambda n: (n, C_SK // 256)),
            pl.BlockSpec((SWA_BLOCK, 256), lambda n: (jnp.maximum(n - 1, 0), C_SK // 256)),
            _const_spec((SWA_BLOCK, 2 * SWA_BLOCK)), smem, smem,
        ],
        out_specs=[pl.BlockSpec((SWA_BLOCK, 1024), lambda n: (n, 0)),
                   pl.BlockSpec((SWA_BLOCK, SWA_HEADS * 2 * SWA_BLOCK), lambda n: (n, 0))],
        out_shape=[jax.ShapeDtypeStruct((S, 1024), F32),
                   jax.ShapeDtypeStruct((S, SWA_HEADS * 2 * SWA_BLOCK), BF16)],
        scratch_shapes=[pltpu.VMEM((2, SWA_HEADS, SWA_BLOCK, 2 * SWA_BLOCK), F32)],
        compiler_params=_cparams("arbitrary"),
        name="swa_fwd",
    )(zmain, zmain, zmain, bucket, rel_bias, sinks)


def _swa_bwd(zmain, o_b, probs, d_o, bucket, dep):
    S = zmain.shape[0]
    nb = S // SWA_BLOCK
    scale = SWA_HEAD_DIM ** -0.5

    def body(q_ref, kvc_ref, kvp_ref, o_ref, p_ref, do_ref, bucket_ref, dep_ref,
             dq_ref, dkv_ref, drb_ref, dsk_ref, dbias, carry):
        del dep_ref
        n = pl.program_id(0)

        @pl.when(n == 0)
        def _():
            dbias[...] = jnp.zeros_like(dbias)
            carry[...] = jnp.zeros_like(carry)

        @pl.when(n < nb)
        def _():
            kk = _swa_zero_key0(_bf(jnp.concatenate([kvp_ref[:, 0:128], kvc_ref[:, 0:128]], axis=0)))
            vv = _swa_zero_key0(_bf(jnp.concatenate([kvp_ref[:, 128:256], kvc_ref[:, 128:256]], axis=0)))
            lane = lax.broadcasted_iota(jnp.int32, (2 * SWA_BLOCK, 128), 1)
            lane_q = lax.broadcasted_iota(jnp.int32, (4 * SWA_BLOCK, 128), 1)
            dk_parts, dv_parts = [], []
            for kvh in range(2):
                ka, kb = _lane_halves(kk, kvh)
                va, vb = _lane_halves(vv, kvh)
                pair_cols = [pl.ds((kvh * 4 + jj) * 128, 128) for jj in range(4)]
                qst = _bf(jnp.concatenate([q_ref[:, cl] for cl in pair_cols], axis=0) * scale)
                dost = jnp.concatenate([do_ref[:, cl] for cl in pair_cols], axis=0)
                prod = dost.astype(F32) * jnp.concatenate([o_ref[:, cl] for cl in pair_cols], axis=0)
                dq_st = jnp.zeros((4 * SWA_BLOCK, 128), F32)
                zks, zvs = [], []
                for odd, (kx, vx) in enumerate(((ka, va), (kb, vb))):
                    keep = (lane_q >= 64) if odd else (lane_q < 64)
                    delta = jnp.sum(jnp.where(keep, prod, 0.0), axis=-1, keepdims=True)
                    dp = _dot_nt(dost, vx)
                    p_parts, ds_parts = [], []
                    for jj in range(4):
                        h = 2 * (kvh * 4 + jj) + odd
                        rows = slice(jj * SWA_BLOCK, (jj + 1) * SWA_BLOCK)
                        p = p_ref[:, pl.ds(h * 2 * SWA_BLOCK, 2 * SWA_BLOCK)]
                        ds = _f32(p) * (dp[rows] - delta[rows])
                        dbias[h] += ds
                        p_parts.append(p)
                        ds_parts.append(_bf(ds))
                    pst = jnp.concatenate(p_parts, axis=0)
                    dsst = jnp.concatenate(ds_parts, axis=0)
                    dq_st = dq_st + _dot(dsst, kx)
                    zks.append(_dot_tn(dsst, qst))
                    zvs.append(_dot_tn(pst, dost))
                for jj in range(4):
                    dq_ref[:, pair_cols[jj]] = _bf(dq_st[jj * SWA_BLOCK:(jj + 1) * SWA_BLOCK] * scale)
                zk = jnp.where(lane < 64, zks[0], zks[1])
                zv = jnp.where(lane < 64, zvs[0], zvs[1])
                dk_parts.append(zk + pltpu.roll(zk, 64, 1))
                dv_parts.append(zv + pltpu.roll(zv, 64, 1))
            dk = jnp.where(lane < 64, dk_parts[0], dk_parts[1])
            dv = jnp.where(lane < 64, dv_parts[0], dv_parts[1])
            dkv = _swa_zero_key0(jnp.concatenate([dk, dv], axis=1))
            dkv_ref[...] = _bf(carry[...] + dkv[0:SWA_BLOCK])
            carry[...] = dkv[SWA_BLOCK:]

        @pl.when(n == nb)
        def _():
            dkv_ref[...] = _bf(carry[...])
            first_col = lax.broadcasted_iota(jnp.int32, (SWA_BLOCK, 2 * SWA_BLOCK), 1) == 0
            bk = jnp.where(first_col, -1, bucket_ref[...])

            def total(v):
                return jnp.broadcast_to(jnp.sum(jnp.sum(v, axis=1, keepdims=True), axis=0, keepdims=True), (8, 128))

            def per_head(h, _):
                db = dbias[h]
                dsk_ref[h] = total(jnp.where(first_col, db, 0.0))

                def per_bucket(b, _):
                    drb_ref[h * NUM_BUCKETS + b] = total(jnp.where(bk == b, db, 0.0))
                    return 0

                return lax.fori_loop(0, NUM_BUCKETS, per_bucket, 0)

            lax.fori_loop(0, SWA_HEADS, per_head, 0)

    cur = lambda n: jnp.minimum(n, nb - 1)
    prev = lambda n: jnp.maximum(jnp.minimum(n, nb - 1) - 1, 0)
    return pl.pallas_call(
        body,
        grid=(nb + 1,),
        in_specs=[
            pl.BlockSpec((SWA_BLOCK, 1024), lambda n: (cur(n), C_SQ // 1024)),
            pl.BlockSpec((SWA_BLOCK, 256), lambda n: (cur(n), C_SK // 256)),
            pl.BlockSpec((SWA_BLOCK, 256), lambda n: (prev(n), C_SK // 256)),
            pl.BlockSpec((SWA_BLOCK, 1024), lambda n: (cur(n), 0)),
            pl.BlockSpec((SWA_BLOCK, SWA_HEADS * 2 * SWA_BLOCK), lambda n: (cur(n), 0)),
            pl.BlockSpec((SWA_BLOCK, 1024), lambda n: (cur(n), 0)),
            _const_spec((SWA_BLOCK, 2 * SWA_BLOCK)), _dep_spec(),
        ],
        out_specs=[
            pl.BlockSpec((SWA_BLOCK, 1024), lambda n: (cur(n), 0)),
            pl.BlockSpec((SWA_BLOCK, 256), lambda n: (jnp.maximum(n - 1, 0), 0)),
            pl.BlockSpec((SWA_HEADS * NUM_BUCKETS, 8, 128), lambda n: (0, 0, 0)),
            pl.BlockSpec((SWA_HEADS, 8, 128), lambda n: (0, 0, 0)),
        ],
        out_shape=[
            jax.ShapeDtypeStruct((S, 1024), BF16),
            jax.ShapeDtypeStruct((S, 256), BF16),
            jax.ShapeDtypeStruct((SWA_HEADS * NUM_BUCKETS, 8, 128), F32),
            jax.ShapeDtypeStruct((SWA_HEADS, 8, 128), F32),
        ],
        scratch_shapes=[
            pltpu.VMEM((SWA_HEADS, SWA_BLOCK, 2 * SWA_BLOCK), F32),
            pltpu.VMEM((SWA_BLOCK, 256), F32),
        ],
        compiler_params=_cparams("arbitrary"),
        name="swa_bwd",
    )(zmain, zmain, zmain, o_b, probs, d_o, bucket, dep)


def _mem_probs(q_ref, k):
    qs = _bf(q_ref[...] * (MEM_HEAD_DIM ** -0.5))
    s = _dot_nt(qs, k)
    e = jnp.exp(s - jnp.max(s, axis=-1, keepdims=True))
    return qs, e / jnp.sum(e, axis=-1, keepdims=True)


def _mem_q_specs(T):
    return [pl.BlockSpec((T, MEM_HEAD_DIM), lambda t, h=h: (t, C_MQ // MEM_HEAD_DIM + h)) for h in range(MEM_HEADS)]


def _mem_kv_proj(mem, g2):
    def body(mem_ref, w_ref, o_ref):
        o_ref[...] = _dot_nt(_bf(mem_ref[...]), _rows(w_ref))

    return pl.pallas_call(
        body,
        grid=(1,),
        in_specs=[pl.BlockSpec((MEM_LEN, D_MODEL), lambda i: (0, 0)), _gathered_spec(R_KV, R_OTHER)],
        out_specs=pl.BlockSpec((MEM_LEN, 2048), lambda i: (0, 0)),
        out_shape=jax.ShapeDtypeStruct((MEM_LEN, 2048), F32),
        compiler_params=_cparams("arbitrary"),
        name="mem_kv_proj",
    )(mem, g2)


def _mem_fwd(zmain, mkv, *, T):
    S = zmain.shape[0]

    def body(q0, q1, q2, q3, kv_ref, o_ref):
        for h, q_ref in enumerate((q0, q1, q2, q3)):
            cols = pl.ds(h * MEM_HEAD_DIM, MEM_HEAD_DIM)
            _, p = _mem_probs(q_ref, _bf(kv_ref[:, cols]))
            o_ref[:, cols] = _dot(_bf(p), _bf(kv_ref[:, pl.ds(1024 + h * MEM_HEAD_DIM, MEM_HEAD_DIM)]))

    return pl.pallas_call(
        body,
        grid=(S // T,),
        in_specs=_mem_q_specs(T) + [_const_spec((MEM_LEN, 2048))],
        out_specs=pl.BlockSpec((T, 1024), lambda t: (t, 0)),
        out_shape=jax.ShapeDtypeStruct((S, 1024), F32),
        compiler_params=_cparams("parallel"),
        name="mem_fwd",
    )(zmain, zmain, zmain, zmain, mkv)


def _mem_bwd(zmain, mkv, o_c, d_o, *, T):
    S = zmain.shape[0]
    scale = MEM_HEAD_DIM ** -0.5

    def body(q0, q1, q2, q3, kv_ref, o_ref, do_ref, dq_ref, dkv_ref):
        @pl.when(pl.program_id(0) == 0)
        def _():
            dkv_ref[...] = jnp.zeros_like(dkv_ref)

        for h, q_ref in enumerate((q0, q1, q2, q3)):
            cols = pl.ds(h * MEM_HEAD_DIM, MEM_HEAD_DIM)
            vcols = pl.ds(1024 + h * MEM_HEAD_DIM, MEM_HEAD_DIM)
            kb = _bf(kv_ref[:, cols])
            qs, p = _mem_probs(q_ref, kb)
            dob = do_ref[:, cols]
            delta = jnp.sum(dob.astype(F32) * o_ref[:, cols], axis=-1, keepdims=True)
            ds = _bf(p * (_dot_nt(dob, _bf(kv_ref[:, vcols])) - delta))
            dq_ref[:, cols] = _bf(_dot(ds, kb) * scale)
            dkv_ref[:, cols] += _dot_tn(ds, qs)
            dkv_ref[:, vcols] += _dot_tn(_bf(p), dob)

    row = pl.BlockSpec((T, 1024), lambda t: (t, 0))
    return pl.pallas_call(
        body,
        grid=(S // T,),
        in_specs=_mem_q_specs(T) + [_const_spec((MEM_LEN, 2048)), row, row],
        out_specs=[row, pl.BlockSpec((MEM_LEN, 2048), lambda t: (0, 0))],
        out_shape=[jax.ShapeDtypeStruct((S, 1024), BF16), jax.ShapeDtypeStruct((MEM_LEN, 2048), F32)],
        compiler_params=_cparams("arbitrary"),
        name="mem_bwd",
    )(zmain, zmain, zmain, zmain, mkv, o_c, d_o)


def _layer_norm(u):
    mu = jnp.mean(u, axis=-1, keepdims=True)
    xc = u - mu
    rstd = lax.rsqrt(jnp.mean(xc * xc, axis=-1, keepdims=True) + LN_EPS)
    return xc * rstd, rstd


def _layer_norm_bwd(dy, gamma, xhat, rstd):
    dxh = dy * gamma
    return rstd * (dxh - jnp.mean(dxh, axis=-1, keepdims=True) - xhat * jnp.mean(dxh * xhat, axis=-1, keepdims=True))


def _merge_forward(oraw_ref, hg_ref, ob_ref, oc_ref, gl_ref, x_ref, gain_ref, wbh, wbs, wbm, wout):
    ys, rs = [], []
    for h in range(HG_HEADS):
        oh = oraw_ref[:, pl.ds(h * HG_DK, HG_DK)]
        r = lax.rsqrt(jnp.mean(oh * oh, axis=-1, keepdims=True) + RMS_EPS)
        ys.append(oh * r)
        rs.append(r)
    y = jnp.concatenate(ys, axis=1)
    hg = _f32(hg_ref[...])
    sg = _sig(hg)
    silu = hg * sg
    oa = _bf(y * gain_ref[...] * silu)
    pa = _dot(oa, _rows(wbh))
    pb = _dot(_bf(ob_ref[...]), _rows(wbs))
    pc = _dot(_bf(oc_ref[...]), _rows(wbm))
    g0 = _sig(_f32(gl_ref[:, 0:1024]))
    g1 = _sig(_f32(gl_ref[:, 1024:2048]))
    g2 = _sig(_f32(gl_ref[:, 2048:3072]))
    m = _bf(g0 * pa + g1 * pb + g2 * pc)
    u1 = ALPHA * x_ref[...] + _dot(m, _rows(wout))
    xhat, rstd = _layer_norm(u1)
    return dict(y=y, rs=rs, hg=hg, sg=sg, silu=silu, oa=oa, pa=pa, pb=pb, pc=pc,
                g0=g0, g1=g1, g2=g2, m=m, xhat=xhat, rstd=rstd)


def _gathered_spec(lo, hi):
    n = hi - lo
    return pl.BlockSpec((N_DEV, n, D_MODEL), lambda *_: (0, lo // n, 0), pipeline_mode=pl.Buffered(1))


def _rows(w_ref):
    return w_ref[...].reshape(-1, D_MODEL)


def _merge_in_specs(T):
    row = lambda w, c=0: pl.BlockSpec((T, w), lambda i: (i, c))
    vec = pl.BlockSpec((1, D_MODEL), lambda i: (0, 0))
    w = [_gathered_spec(lo, hi) for lo, hi in ((R_BH, R_BS), (R_BS, R_BM), (R_BM, R_OUT), (R_OUT, R_KV))]
    return [row(1024), row(1024, C_HG // 1024), row(1024), row(1024), row(3072), row(1024), vec, *w, vec, vec]


def _merge_fwd(o_raw, zmain, o_b, o_c, gl, x, gain, wbh, wbs, wbm, wout, ln_g, ln_b, *, T):
    S = x.shape[0]

    def body(oraw_ref, hg_ref, ob_ref, oc_ref, gl_ref, x_ref, gain_ref, wbh_r, wbs_r, wbm_r, wout_r, g_ref, b_ref,
             h1_ref, h1b_ref):
        f = _merge_forward(oraw_ref, hg_ref, ob_ref, oc_ref, gl_ref, x_ref, gain_ref, wbh_r, wbs_r, wbm_r, wout_r)
        h1 = f["xhat"] * g_ref[...] + b_ref[...]
        h1_ref[...] = h1
        h1b_ref[...] = _bf(h1)

    row = pl.BlockSpec((T, D_MODEL), lambda i: (i, 0))
    return pl.pallas_call(
        body,
        grid=(S // T,),
        in_specs=_merge_in_specs(T),
        out_specs=[row, row],
        out_shape=[jax.ShapeDtypeStruct((S, D_MODEL), F32), jax.ShapeDtypeStruct((S, D_MODEL), BF16)],
        compiler_params=_cparams("parallel"),
        name="merge_fwd",
    )(o_raw, zmain, o_b, o_c, gl, x, gain, wbh, wbs, wbm, wout, ln_g, ln_b)


def _merge_bwd(d_h1, o_raw, zmain, o_b, o_c, gl, x, gain, wbh, wbs, wbm, wout, ln_g, ln_b, *, T):
    S = x.shape[0]

    def body(dh1_ref, oraw_ref, hg_ref, ob_ref, oc_ref, gl_ref, x_ref, gain_ref, wbh_r, wbs_r, wbm_r, wout_r, g_ref, b_ref,
             dx_ref, du1_ref, m_ref, oa_ref, dpa_ref, dpb_ref, dpc_ref, doraw_ref, dob_ref, doc_ref, dz_ref,
             dgain_ref, dg_ref, db_ref):
        del b_ref

        @pl.when(pl.program_id(0) == 0)
        def _():
            dgain_ref[...] = jnp.zeros_like(dgain_ref)
            dg_ref[...] = jnp.zeros_like(dg_ref)
            db_ref[...] = jnp.zeros_like(db_ref)

        f = _merge_forward(oraw_ref, hg_ref, ob_ref, oc_ref, gl_ref, x_ref, gain_ref, wbh_r, wbs_r, wbm_r, wout_r)
        dh1 = dh1_ref[...]
        dg_ref[...] += jnp.sum(dh1 * f["xhat"], axis=0, keepdims=True)
        db_ref[...] += jnp.sum(dh1, axis=0, keepdims=True)
        du1 = _layer_norm_bwd(dh1, g_ref[...], f["xhat"], f["rstd"])
        dx_ref[...] = ALPHA * du1
        du1b = _bf(du1)
        du1_ref[...] = du1b
        m_ref[...] = f["m"]
        oa_ref[...] = f["oa"]
        dm = _dot_nt(du1b, _rows(wout_r))
        for i, (g, p, dp_ref, dob_r, w_r) in enumerate((
                (f["g0"], f["pa"], dpa_ref, None, wbh_r),
                (f["g1"], f["pb"], dpb_ref, dob_ref, wbs_r),
                (f["g2"], f["pc"], dpc_ref, doc_ref, wbm_r))):
            dz_ref[:, pl.ds((i + 1) * 1024, 1024)] = _bf(dm * p * g * (1.0 - g))
            dp = _bf(dm * g)
            dp_ref[...] = dp
            d_branch = _dot_nt(dp, _rows(w_r))
            if dob_r is not None:
                dob_r[...] = _bf(d_branch)
            else:
                doa = d_branch
        gain = gain_ref[...]
        t = doa * f["y"]
        dgain_ref[...] += jnp.sum(t * f["silu"], axis=0, keepdims=True)
        sg = f["sg"]
        dz_ref[:, 0:1024] = _bf(t * gain * sg * (1.0 + f["hg"] * (1.0 - sg)))
        dy = doa * gain * f["silu"]
        for h in range(HG_HEADS):
            cols = slice(h * HG_DK, (h + 1) * HG_DK)
            yh = f["y"][:, cols]
            dyh = dy[:, cols]
            doraw_ref[:, pl.ds(h * HG_DK, HG_DK)] = _bf(
                f["rs"][h] * (dyh - yh * jnp.mean(dyh * yh, axis=-1, keepdims=True)))

    row = lambda w: pl.BlockSpec((T, w), lambda i: (i, 0))
    vec = pl.BlockSpec((1, D_MODEL), lambda i: (0, 0))
    bshape = jax.ShapeDtypeStruct((S, D_MODEL), BF16)
    vshape = jax.ShapeDtypeStruct((1, D_MODEL), F32)
    return pl.pallas_call(
        body,
        grid=(S // T,),
        in_specs=[row(1024)] + _merge_in_specs(T),
        out_specs=[row(1024)] * 10 + [row(4096), vec, vec, vec],
        out_shape=[jax.ShapeDtypeStruct((S, D_MODEL), F32)] + [bshape] * 9
        + [jax.ShapeDtypeStruct((S, 4096), BF16), vshape, vshape, vshape],
        compiler_params=_cparams("arbitrary"),
        name="merge_bwd",
    )(d_h1, o_raw, zmain, o_b, o_c, gl, x, gain, wbh, wbs, wbm, wout, ln_g, ln_b)


def _mlp_fwd_bwd(h1, target, wup_t, wdn, ln_g, ln_b, *, T, FC):
    S = h1.shape[0]
    nf = D_FF // FC
    assert FC == R_BH - R_UP == R_UP - R_DN

    def body(h1_ref, t_ref, wup_ref, wdn_ref, g_ref, b_ref, dh1_ref, a_ref, dup_ref, du2_ref, loss_ref, dg_ref, db_ref, up_scr):
        @pl.when(pl.program_id(0) == 0)
        def _():
            loss_ref[...] = jnp.zeros_like(loss_ref)
            dg_ref[...] = jnp.zeros_like(dg_ref)
            db_ref[...] = jnp.zeros_like(db_ref)

        h1v = h1_ref[...]
        h1b = _bf(h1v)
        ff = jnp.zeros((T, D_MODEL), F32)
        for j in range(nf):
            rows = pl.ds(j * FC, FC)
            up = jnp.maximum(_dot_nt(h1b, wup_ref[j]), 0.0)
            up_scr[:, rows] = _bf(up)
            a = _bf(up * up)
            a_ref[:, rows] = a
            ff = ff + _dot(a, wdn_ref[j])
        xhat, rstd = _layer_norm(ALPHA * h1v + ff)
        gamma = g_ref[...]
        err = xhat * gamma + b_ref[...] - t_ref[...]
        loss_ref[...] += jnp.sum(jnp.sum(err * err, axis=-1, keepdims=True), axis=0, keepdims=True) * (0.5 / D_MODEL)
        dy = err * (1.0 / D_MODEL)
        dg_ref[...] += jnp.sum(dy * xhat, axis=0, keepdims=True)
        db_ref[...] += jnp.sum(dy, axis=0, keepdims=True)
        du2 = _layer_norm_bwd(dy, gamma, xhat, rstd)
        du2b = _bf(du2)
        du2_ref[...] = du2b
        dh1 = ALPHA * du2
        for j in range(nf):
            rows = pl.ds(j * FC, FC)
            dup = _bf(_dot_nt(du2b, wdn_ref[j]) * (2.0 * up_scr[:, rows].astype(F32)))
            dup_ref[:, rows] = dup
            dh1 = dh1 + _dot(dup, wup_ref[j])
        dh1_ref[...] = dh1

    row = lambda w: pl.BlockSpec((T, w), lambda i: (i, 0))
    vec = pl.BlockSpec((1, D_MODEL), lambda i: (0, 0))
    vshape = jax.ShapeDtypeStruct((1, D_MODEL), F32)
    return pl.pallas_call(
        body,
        grid=(S // T,),
        in_specs=[row(1024), row(1024), _gathered_spec(R_UP, R_BH), _gathered_spec(R_DN, R_UP), vec, vec],
        out_specs=[row(1024), row(D_FF), row(D_FF), row(1024), pl.BlockSpec((8, 128), lambda i: (0, 0)), vec, vec],
        out_shape=[
            jax.ShapeDtypeStruct((S, D_MODEL), F32),
            jax.ShapeDtypeStruct((S, D_FF), BF16),
            jax.ShapeDtypeStruct((S, D_FF), BF16),
            jax.ShapeDtypeStruct((S, D_MODEL), BF16),
            jax.ShapeDtypeStruct((8, 128), F32), vshape, vshape,
        ],
        scratch_shapes=[pltpu.VMEM((T, D_FF), BF16)],
        compiler_params=_cparams("arbitrary"),
        name="mlp_fwd_bwd",
    )(h1, target, wup_t, wdn, ln_g, ln_b)


def _local_step(x, mem, target, lb_logits, gain, sinks, rel_bias, ln1_g, ln1_b, ln2_g, ln2_b,
                win_t, dep0, other_weights, send_other_grads, send_win_grad):
    S = x.shape[0]
    T = min(256, S)
    KC = min(2048, S)
    zmain, xb = _mm_nt(x, win_t, n_cols=C_GL, tm=min(512, S), tn=C_GL, out_dtype=BF16, name="in_proj_main", dep=dep0,
                       also_a_bf16=True)
    gl = _mm_nt(x, win_t[C_GL:], tm=min(512, S), tn=1536, out_dtype=BF16, name="in_proj_gates")
    bucket = _t5_bucket_table()

    o_raw, states = _hgrn_fwd(zmain, lb_logits, T=min(1024, S))
    o_b, swa_probs = _swa_fwd(zmain, bucket, rel_bias, sinks)
    g2 = other_weights(o_b)
    mkv = _mem_kv_proj(mem, g2)
    o_c = _mem_fwd(zmain, mkv, T=min(512, S))
    merge_args = (o_raw, zmain, o_b, o_c, gl, x, gain, g2, g2, g2, g2, ln1_g, ln1_b)
    h1, h1b = _merge_fwd(*merge_args, T=T)

    d_h1, act, d_up, du2, loss, d_ln2_g, d_ln2_b = _mlp_fwd_bwd(h1, target, g2, g2, ln2_g, ln2_b, T=min(512, S), FC=512)
    wgrad = functools.partial(_mm_tn, out_dtype=BF16)
    g_wdn = wgrad(act, du2, kc=KC, name="grad_w_down")
    g_wup_t = wgrad(d_up, h1b, kc=KC, name="grad_w_up")

    (dx_part, du1, m, oa, dpa, dpb, dpc, d_oraw, d_ob, d_oc, d_hg_gl,
     d_gain, d_ln1_g, d_ln1_b) = _merge_bwd(d_h1, *merge_args, T=T)
    g_wout = wgrad(m, du1, kc=KC, name="grad_w_out")
    g_wbh = wgrad(oa, dpa, kc=KC, name="grad_w_branch_hg")
    g_wbs = wgrad(o_b, dpb, kc=KC, name="grad_w_branch_swa")
    g_wbm = wgrad(o_c, dpc, kc=KC, name="grad_w_branch_mem")

    d_mq, d_mkv = _mem_bwd(zmain, mkv, o_c, d_oc, T=min(512, S))
    g_wkv_t = wgrad(d_mkv, mem, kc=MEM_LEN, name="grad_w_mem_kv")
    sent_others = send_other_grads(
        dict(wkv_t=g_wkv_t, wbh=g_wbh, wbs=g_wbs, wbm=g_wbm, wout=g_wout, wup_t=g_wup_t, wdn=g_wdn))
    d_sq, d_skv, d_rb, d_sink = _swa_bwd(zmain, o_b, swa_probs, d_ob, bucket, sent_others)
    d_qfv, d_lb = _hgrn_bwd(zmain, lb_logits, states, d_oraw, T=min(1024, S))

    head_major = lambda a: a.reshape(3, HG_HEADS, HG_DK, D_MODEL).transpose(1, 0, 2, 3).reshape(3 * D_MODEL, D_MODEL)
    col_major = lambda a: a.reshape(HG_HEADS, 3, HG_DK, D_MODEL).transpose(1, 0, 2, 3).reshape(3 * D_MODEL, D_MODEL)
    pieces = (d_qfv, d_hg_gl, d_sq, d_skv, d_mq)
    g_qfv, g_hg_gl, g_sq, g_skv, g_mq = [
        wgrad(p, xb, kc=KC, name="grad_w_in_" + n) for p, n in zip(pieces, ("qfv", "hg_gates", "swa_q", "swa_kv", "mem_q"))]
    g_win_t = jnp.concatenate([col_major(g_qfv), g_hg_gl[:D_MODEL], g_sq, g_skv, g_mq, g_hg_gl[D_MODEL:]], axis=0)
    sent_win = send_win_grad(g_win_t)
    grad_x = _grad_x(*pieces, head_major(win_t[:C_HG]), win_t, dx_part, sent_win, tm=T)

    small = dict(
        d_lb=d_lb, d_gain=d_gain, d_sink=d_sink[:, 0, 0].reshape(1, SWA_HEADS),
        d_rb=d_rb[:, 0, 0].reshape(SWA_HEADS, NUM_BUCKETS).T,
        d_ln1_g=d_ln1_g, d_ln1_b=d_ln1_b, d_ln2_g=d_ln2_g, d_ln2_b=d_ln2_b, loss=loss[0, 0])
    return grad_x, small


MESH = pl.DeviceIdType.MESH
ANY = pl.BlockSpec(memory_space=pl.ANY)


def _coords():
    return lax.axis_index("x"), lax.axis_index("y"), lax.axis_index("c")


def _other_chips(x, y):
    return [(1 - x, y), (x, 1 - y), (1 - x, 1 - y)]


def _all_gather_weights(*arrays):
    na = len(arrays)

    def body(*refs):
        srcs, dsts = refs[:na], refs[na:2 * na]
        send_sems, recv_sems, local_sems = refs[2 * na:]
        x, y, c = _coords()
        me, sibling = (x, y, c), (x, y, 1 - c)
        chips = _other_chips(x, y)

        def slot(a, px, py, pc):
            return dsts[a].at[4 * px + 2 * py + pc]

        def copy(a, k, block, to, from_shard=False):
            return pltpu.make_async_remote_copy(
                src_ref=srcs[a] if from_shard else slot(a, *block), dst_ref=slot(a, *block),
                send_sem=send_sems.at[a * 7 + k], recv_sem=recv_sems.at[a * 7 + k],
                device_id=to, device_id_type=MESH)

        own = [pltpu.make_async_copy(srcs[a], slot(a, *me), local_sems.at[a]) for a in range(na)]
        for cp in own:
            cp.start()
        first = []
        for a in range(na):
            first.append(copy(a, 0, me, sibling, True))
            first += [copy(a, 1 + j, me, (*chip, c), True) for j, chip in enumerate(chips)]
        for cp in first:
            cp.start()
        passed = []
        for j, chip in enumerate(chips):
            for a in range(na):
                copy(a, 1 + j, (*chip, c), me).wait_recv()
                fwd = copy(a, 4 + j, (*chip, c), sibling)
                fwd.start()
                passed.append(fwd)
        for a in range(na):
            copy(a, 0, sibling, me).wait_recv()
            for j, chip in enumerate(chips):
                copy(a, 4 + j, (*chip, 1 - c), me).wait_recv()
        for cp in first + passed:
            cp.wait_send()
        for cp in own:
            cp.wait()

    return pl.pallas_call(
        body,
        in_specs=[ANY] * na,
        out_specs=[ANY] * na,
        out_shape=[jax.ShapeDtypeStruct((N_DEV,) + a.shape, a.dtype) for a in arrays],
        scratch_shapes=[pltpu.SemaphoreType.DMA((7 * na,)), pltpu.SemaphoreType.DMA((7 * na,)),
                        pltpu.SemaphoreType.DMA((na,))],
        name="all_gather_weights",
    )(*arrays)


HBM = pl.BlockSpec(memory_space=pltpu.HBM)
SEM = pl.BlockSpec(memory_space=pltpu.SEMAPHORE)
_DATAFLOW = pltpu.SideEffectType.DATAFLOW_SIDE_EFFECTING


def _peer(x, y, c, r):
    return x ^ (r >> 2), y ^ ((r >> 1) & 1), c ^ (r & 1)


def _direct_copies(src_ref, land_ref, send_sems, recv_sems, gather, receiving):
    x, y, c = _coords()
    me = 4 * x + 2 * y + c
    copies = []
    for r in range(1, N_DEV):
        px, py, pc = _peer(x, y, c, r)
        peer = 4 * px + 2 * py + pc
        if gather:
            src, dst = src_ref, land_ref.at[peer if receiving else me]
        else:
            src, dst = src_ref.at[peer], land_ref.at[r - 1]
        copies.append(pltpu.make_async_remote_copy(
            src_ref=src, dst_ref=dst, send_sem=send_sems.at[r - 1], recv_sem=recv_sems.at[r - 1],
            device_id=(px, py, pc), device_id_type=MESH))
    return copies


def _direct_start(src, land, *, gather, name):
    def body(src_ref, land_ref, send_sems, recv_sems, src_thru, land_thru, token):
        del src_thru, land_thru
        for cp in _direct_copies(src_ref, land_ref, send_sems, recv_sems, gather, False):
            cp.start()
        token[...] = jnp.zeros_like(token)

    return pl.pallas_call(
        body,
        name=name,
        out_shape=(pltpu.SemaphoreType.DMA((N_DEV - 1,)), pltpu.SemaphoreType.DMA((N_DEV - 1,)),
                   pltpu.HBM(src.shape, src.dtype), pltpu.HBM(land.shape, land.dtype),
                   jax.ShapeDtypeStruct((8, 128), F32)),
        in_specs=(HBM, HBM),
        out_specs=(SEM, SEM, HBM, HBM, pl.BlockSpec(memory_space=pltpu.VMEM)),
        input_output_aliases={0: 2, 1: 3},
        compiler_params=pltpu.CompilerParams(has_side_effects=_DATAFLOW),
    )(pltpu.with_memory_space_constraint(src, pltpu.HBM), pltpu.with_memory_space_constraint(land, pltpu.HBM))


def _direct_wait(send_sems, recv_sems, src_thru, land_thru, after, *, gather, name):
    def body(src_ref, land_ref, send_sems_ref, recv_sems_ref, after_ref, src_dead, got_ref):
        del after_ref, src_dead, got_ref
        for cp in _direct_copies(src_ref, land_ref, send_sems_ref, recv_sems_ref, gather, True):
            cp.wait_send()
            cp.wait_recv()

    return pl.pallas_call(
        body,
        name=name,
        out_shape=(pltpu.HBM(src_thru.shape, src_thru.dtype), pltpu.HBM(land_thru.shape, land_thru.dtype)),
        in_specs=(HBM, HBM, SEM, SEM, ANY),
        out_specs=(HBM, HBM),
        input_output_aliases={0: 0, 1: 1},
        compiler_params=pltpu.CompilerParams(has_side_effects=_DATAFLOW),
    )(src_thru, land_thru, send_sems, recv_sems, after)


def _sum_partials(src, land, me, *, tr, name):
    R = src.shape[1]

    def body(me_ref, s_ref, l_ref, o_ref):
        del me_ref
        acc = s_ref[0].astype(F32)
        for r in range(N_DEV - 1):
            acc = acc + l_ref[r].astype(F32)
        o_ref[...] = acc

    return pl.pallas_call(
        body,
        grid_spec=pltpu.PrefetchScalarGridSpec(
            num_scalar_prefetch=1, grid=(R // tr,),
            in_specs=[pl.BlockSpec((1, tr, 1024), lambda i, mr: (mr[0], i, 0)),
                      pl.BlockSpec((N_DEV - 1, tr, 1024), lambda i, mr: (0, i, 0))],
            out_specs=pl.BlockSpec((tr, 1024), lambda i, mr: (i, 0))),
        out_shape=jax.ShapeDtypeStruct((R, 1024), F32),
        name=name,
    )(me, src, land)


def _small_all_reduce(packed, lb_logits):
    def body(p_ref, lbl_ref, o_ref, gath, send_sems, recv_sems):
        x, y, c = _coords()
        mine = 4 * x + 2 * y + c
        gath[mine] = p_ref[...]
        copies = []
        for r in range(1, N_DEV):
            peer = (x ^ (r >> 2), y ^ ((r >> 1) & 1), c ^ (r & 1))
            copies.append(pltpu.make_async_remote_copy(
                src_ref=p_ref, dst_ref=gath.at[mine],
                send_sem=send_sems.at[r - 1], recv_sem=recv_sems.at[r - 1],
                device_id=peer, device_id_type=MESH))
        for cp in copies:
            cp.start()
        for r in range(1, N_DEV):
            peer_slot = 4 * (x ^ (r >> 2)) + 2 * (y ^ ((r >> 1) & 1)) + (c ^ (r & 1))
            pltpu.make_async_remote_copy(
                src_ref=p_ref, dst_ref=gath.at[peer_slot],
                send_sem=send_sems.at[r - 1], recv_sem=recv_sems.at[r - 1],
                device_id=(x, y, c), device_id_type=MESH).wait_recv()
        for cp in copies:
            cp.wait_send()
        tot = gath[0]
        for d in range(1, N_DEV):
            tot = tot + gath[d]
        o_ref[...] = tot
        lb = _lower_bound(lbl_ref)
        dl0 = o_ref[SM_LB:SM_LB + 1, :] * lb * (1.0 - lb)
        o_ref[SM_LB:SM_LB + 1, :] = dl0
        o_ref[SM_LB + 1:SM_LB + 2, :] = -dl0

    vm = pl.BlockSpec(memory_space=pltpu.VMEM)
    return pl.pallas_call(
        body,
        in_specs=[vm, vm],
        out_specs=vm,
        out_shape=jax.ShapeDtypeStruct(packed.shape, F32),
        scratch_shapes=[pltpu.VMEM((N_DEV,) + packed.shape, F32),
                        pltpu.SemaphoreType.DMA((N_DEV - 1,)), pltpu.SemaphoreType.DMA((N_DEV - 1,))],
        name="small_all_reduce",
    )(packed, lb_logits)


def _adamw(w, g, m, v, *, tr, name):
    R, C = w.shape

    def body(w_ref, g_ref, m_ref, v_ref, d_ref, nm_ref, nv_ref):
        gv = g_ref[...]
        nm = ADAM_B1 * m_ref[...] + (1.0 - ADAM_B1) * gv
        nv = ADAM_B2 * v_ref[...] + (1.0 - ADAM_B2) * jnp.square(gv)
        m_hat = nm / (1.0 - ADAM_B1 ** ADAM_STEP)
        v_hat = nv / (1.0 - ADAM_B2 ** ADAM_STEP)
        d_ref[...] = -ADAM_LR * (m_hat / (jnp.sqrt(v_hat) + ADAM_EPS) + ADAM_WD * w_ref[...])
        nm_ref[...] = nm
        nv_ref[...] = nv

    spec = pl.BlockSpec((tr, C), lambda i: (i, 0))
    return pl.pallas_call(
        body,
        grid=(R // tr,),
        in_specs=[spec] * 4,
        out_specs=[spec] * 3,
        out_shape=[jax.ShapeDtypeStruct((R, C), F32)] * 3,
        compiler_params=_cparams("parallel"),
        name=name,
    )(w, g, m, v)


def _pack_small(lb, gain, sinks, rel_bias, ln1_g, ln1_b, ln2_g, ln2_b, loss=None):
    pad = lambda a: jnp.pad(a.reshape(1, -1), ((0, 0), (0, D_MODEL - a.size)))
    rows = [lb.reshape(-1, D_MODEL)]
    if rows[0].shape[0] == 1:
        rows.append(jnp.zeros((1, D_MODEL), F32))
    rows += [gain.reshape(1, D_MODEL), pad(sinks), pad(rel_bias), ln1_g.reshape(1, D_MODEL), ln1_b.reshape(1, D_MODEL),
             ln2_g.reshape(1, D_MODEL), ln2_b.reshape(1, D_MODEL),
             pad(jnp.zeros((1,), F32) if loss is None else loss.reshape(1))]
    rows.append(jnp.zeros((SM_ROWS - SM_LOSS - 1, D_MODEL), F32))
    return jnp.concatenate(rows, axis=0)


def _unpack_small(p):
    return dict(
        lb_logits=p[SM_LB:SM_LB + 2], hg_norm_gain=p[SM_GAIN:SM_GAIN + 1], swa_sinks=p[SM_SINK:SM_SINK + 1, :SWA_HEADS],
        rel_bias=p[SM_RB, :NUM_BUCKETS * SWA_HEADS].reshape(NUM_BUCKETS, SWA_HEADS),
        ln1_g=p[SM_L1G:SM_L1G + 1], ln1_b=p[SM_L1B:SM_L1B + 1], ln2_g=p[SM_L2G:SM_L2G + 1], ln2_b=p[SM_L2B:SM_L2B + 1])


_SMALL = ("lb_logits", "hg_norm_gain", "swa_sinks", "rel_bias", "ln1_g", "ln1_b", "ln2_g", "ln2_b")
_WEIGHTS = ("w_in", "lb_logits", "hg_norm_gain", "swa_sinks", "rel_bias", "w_mem_kv", "w_branch_hg", "w_branch_swa",
            "w_branch_mem", "w_out", "ln1_g", "ln1_b", "w_up", "w_down", "ln2_g", "ln2_b")


def kernel(x, mem, w_in, lb_logits, hg_norm_gain, swa_sinks, rel_bias, w_mem_kv, w_branch_hg, w_branch_swa, w_branch_mem, w_out, ln1_g, ln1_b, w_up, w_down, ln2_g, ln2_b, loss_target, m_w_in, m_lb_logits, m_hg_norm_gain, m_swa_sinks, m_rel_bias, m_w_mem_kv, m_w_branch_hg, m_w_branch_swa, m_w_branch_mem, m_w_out, m_ln1_g, m_ln1_b, m_w_up, m_w_down, m_ln2_g, m_ln2_b, v_w_in, v_lb_logits, v_hg_norm_gain, v_swa_sinks, v_rel_bias, v_w_mem_kv, v_w_branch_hg, v_w_branch_swa, v_w_branch_mem, v_w_out, v_ln1_g, v_ln1_b, v_w_up, v_w_down, v_ln2_g, v_ln2_b):
    w = dict(w_in=w_in, lb_logits=lb_logits, hg_norm_gain=hg_norm_gain, swa_sinks=swa_sinks, rel_bias=rel_bias,
             w_mem_kv=w_mem_kv, w_branch_hg=w_branch_hg, w_branch_swa=w_branch_swa, w_branch_mem=w_branch_mem,
             w_out=w_out, ln1_g=ln1_g, ln1_b=ln1_b, w_up=w_up, w_down=w_down, ln2_g=ln2_g, ln2_b=ln2_b)
    mom = dict(w_in=m_w_in, lb_logits=m_lb_logits, hg_norm_gain=m_hg_norm_gain, swa_sinks=m_swa_sinks, rel_bias=m_rel_bias,
               w_mem_kv=m_w_mem_kv, w_branch_hg=m_w_branch_hg, w_branch_swa=m_w_branch_swa, w_branch_mem=m_w_branch_mem,
               w_out=m_w_out, ln1_g=m_ln1_g, ln1_b=m_ln1_b, w_up=m_w_up, w_down=m_w_down, ln2_g=m_ln2_g, ln2_b=m_ln2_b)
    var = dict(w_in=v_w_in, lb_logits=v_lb_logits, hg_norm_gain=v_hg_norm_gain, swa_sinks=v_swa_sinks, rel_bias=v_rel_bias,
               w_mem_kv=v_w_mem_kv, w_branch_hg=v_w_branch_hg, w_branch_swa=v_w_branch_swa, w_branch_mem=v_w_branch_mem,
               w_out=v_w_out, ln1_g=v_ln1_g, ln1_b=v_ln1_b, w_up=v_w_up, w_down=v_w_down, ln2_g=v_ln2_g, ln2_b=v_ln2_b)
    xc, yc, cc = _coords()

    p1 = _bf(w_in[0].T)
    p2 = _bf(jnp.concatenate([w_down[0], w_up[0].T, w_branch_hg[0], w_branch_swa[0], w_branch_mem[0], w_out[0],
                              w_mem_kv[0].T], axis=0))
    me = 4 * xc + 2 * yc + cc
    (g1,) = _all_gather_weights(p1)
    land2 = lax.dynamic_update_slice(lax.empty((N_DEV, R_OTHER, D_MODEL), BF16), p2[None], (me, 0, 0))
    ag2 = _direct_start(p2, land2, gather=True, name="gather_other_weights_start")

    def other_weights(after):
        return _direct_wait(*ag2[:4], after, gather=True, name="gather_other_weights_wait")[1]

    blocks = lambda a: a.reshape(N_DEV, a.shape[0] // N_DEV, D_MODEL)
    started = {}

    def send_other_grads(g):
        part = jnp.concatenate([blocks(g[k]) for k in ("wdn", "wup_t", "wbh", "wbs", "wbm", "wout", "wkv_t")], axis=1)
        started["others"] = _direct_start(part, lax.empty((N_DEV - 1, R_OTHER, D_MODEL), BF16), gather=False,
                                          name="scatter_other_grads_start")
        return started["others"][4]

    me1 = me.reshape(1).astype(jnp.int32)
    grads, delta, new_m, new_v = {}, {}, {}, {}

    def adamw(name):
        w2 = w[name][0]
        delta[name], new_m[name], new_v[name] = _adamw(
            w2, grads[name], mom[name][0], var[name][0], tr=w2.shape[0] // 4, name="adamw_" + name)

    def send_win_grad(g):
        started["win"] = _direct_start(blocks(g), lax.empty((N_DEV - 1, IN_SHARD, D_MODEL), BF16), gather=False,
                                       name="scatter_w_in_grad_start")
        mine2, landed2 = _direct_wait(*started["others"][:4], started["win"][4], gather=False,
                                      name="scatter_other_grads_wait")
        gs2 = _sum_partials(mine2, landed2, me1, tr=R_OTHER // 2, name="sum_other_grads")
        grads.update(
            w_down=gs2[R_DN:R_UP], w_up=gs2[R_UP:R_BH].T, w_branch_hg=gs2[R_BH:R_BS], w_branch_swa=gs2[R_BS:R_BM],
            w_branch_mem=gs2[R_BM:R_OUT], w_out=gs2[R_OUT:R_KV], w_mem_kv=gs2[R_KV:R_OTHER].T)
        for name in ("w_mem_kv", "w_branch_hg", "w_branch_swa", "w_branch_mem", "w_out", "w_up", "w_down"):
            adamw(name)
        return new_v["w_down"]

    grad_x, small = _local_step(
        x[0], mem[0], loss_target[0], lb_logits, hg_norm_gain, swa_sinks, rel_bias, ln1_g, ln1_b, ln2_g, ln2_b,
        g1.reshape(IN_COLS, D_MODEL), ag2[4], other_weights, send_other_grads, send_win_grad)

    mine1, landed1 = _direct_wait(*started["win"][:4], grad_x, gather=False, name="scatter_w_in_grad_wait")
    grads["w_in"] = _sum_partials(mine1, landed1, me1, tr=IN_SHARD // 2, name="sum_w_in_grad").T
    adamw("w_in")

    packed = _pack_small(small["d_lb"], small["d_gain"], small["d_sink"], small["d_rb"], small["d_ln1_g"],
                         small["d_ln1_b"], small["d_ln2_g"], small["d_ln2_b"], small["loss"])
    reduced = _small_all_reduce(packed, lb_logits)
    loss = reduced[SM_LOSS, 0]
    grads.update(_unpack_small(reduced))

    sm = lambda d: _pack_small(*[d[k] for k in _SMALL])
    d_s, m_s, v_s = _adamw(sm(w), reduced, sm(mom), sm(var), tr=SM_ROWS, name="adamw_small")
    for dst, src in ((delta, d_s), (new_m, m_s), (new_v, v_s)):
        dst.update(_unpack_small(src))

    def shaped(d, name):
        return d[name].reshape(w[name].shape)

    return (loss, grad_x[None], *[shaped(grads, n) for n in _WEIGHTS], *[shaped(delta, n) for n in _WEIGHTS],
            *[shaped(new_m, n) for n in _WEIGHTS], *[shaped(new_v, n) for n in _WEIGHTS])
```

```python
import functools
import math

import jax
import jax.numpy as jnp
from jax import lax
from jax.experimental import pallas as pl
from jax.experimental.pallas import tpu as pltpu

F32 = jnp.float32
BF16 = jnp.bfloat16

D_MODEL = 1024
MEM_LEN = 256
HG_HEADS = 8
HG_DK = 128
HG_CHUNK = 64
SWA_HEADS = 16
SWA_HEAD_DIM = 64
SWA_BLOCK = 128
SWA_WINDOW = 128
MEM_HEADS = 4
MEM_HEAD_DIM = 256
NUM_BUCKETS = 32
MAX_DISTANCE = 128
D_FF = 4096
LN_EPS = 1e-5
RMS_EPS = 1e-6
ALPHA = 2.0 ** 0.25
N_DEV = 8

C_HQ, C_HF, C_HI, C_HG, C_SQ, C_SK, C_SV, C_MQ, C_GL = 0, 1024, 2048, 3072, 4096, 5120, 5248, 5376, 6400
IN_COLS = 9472
IN_SHARD = IN_COLS // N_DEV

ADAM_LR = 0.001
ADAM_B1 = 0.9
ADAM_B2 = 0.999
ADAM_EPS = 1e-08
ADAM_WD = 0.01
ADAM_STEP = 10

VMEM_LIMIT = 58 * 1024 * 1024

R_DN, R_UP, R_BH, R_BS, R_BM, R_OUT, R_KV, R_OTHER = 0, 512, 1024, 1152, 1280, 1408, 1536, 1792

SM_LB, SM_GAIN, SM_SINK, SM_RB, SM_L1G, SM_L1B, SM_L2G, SM_L2B, SM_LOSS, SM_ROWS = 0, 2, 3, 4, 5, 6, 7, 8, 9, 16


def _bf(v):
    return v.astype(BF16)


def _f32(v):
    return v.astype(F32)


def _dot(a, b):
    return jnp.dot(a, b, preferred_element_type=F32)


def _dot_nt(a, b):
    return lax.dot_general(a, b, (((1,), (1,)), ((), ())), preferred_element_type=F32)


def _dot_tn(a, b):
    return lax.dot_general(a, b, (((0,), (0,)), ((), ())), preferred_element_type=F32)


def _sig(v):
    return 1.0 / (1.0 + jnp.exp(-v))


def _cparams(*sem):
    return pltpu.CompilerParams(dimension_semantics=sem, vmem_limit_bytes=VMEM_LIMIT)


def _const_spec(shape):
    nd = len(shape)
    return pl.BlockSpec(shape, lambda *_: (0,) * nd, pipeline_mode=pl.Buffered(1))


def _dep_spec():
    return pl.BlockSpec((8, 128), lambda *_: (0, 0))


def _mm_nt(a, bt, *, tm, tn, out_dtype, name, dep=None, n_cols=None, also_a_bf16=False):
    M, K = a.shape
    N = bt.shape[0] if n_cols is None else n_cols
    deps = () if dep is None else (dep,)
    nd = len(deps)
    assert not also_a_bf16 or tn == N

    def body(a_ref, b_ref, *rest):
        ab = _bf(a_ref[...])
        o_ref = rest[nd]
        o_ref[...] = _dot_nt(ab, _bf(b_ref[...])).astype(o_ref.dtype)
        if also_a_bf16:
            rest[nd + 1][...] = ab

    out_specs = [pl.BlockSpec((tm, tn), lambda j, i: (i, j))]
    out_shape = [jax.ShapeDtypeStruct((M, N), out_dtype)]
    if also_a_bf16:
        out_specs.append(pl.BlockSpec((tm, K), lambda j, i: (i, 0)))
        out_shape.append(jax.ShapeDtypeStruct((M, K), BF16))
    res = pl.pallas_call(
        body,
        grid=(N // tn, M // tm),
        in_specs=[pl.BlockSpec((tm, K), lambda j, i: (i, 0)), pl.BlockSpec((tn, K), lambda j, i: (j, 0))]
        + [_dep_spec() for _ in deps],
        out_specs=out_specs,
        out_shape=out_shape,
        compiler_params=_cparams("parallel", "parallel"),
        name=name,
    )(a, bt, *deps)
    return res if also_a_bf16 else res[0]


def _mm_tn_resident(a, b, *, tm, kc, name, out_dtype):
    K, M = a.shape
    N = b.shape[1]
    nk = K // kc

    def body(a_ref, b_ref, o_ref):
        acc = jnp.zeros((tm, N), F32)
        for kk in range(nk):
            sl = pl.ds(kk * kc, kc)
            acc = acc + _dot_tn(_bf(a_ref[sl, :]), _bf(b_ref[sl, :]))
        o_ref[...] = acc.astype(o_ref.dtype)

    return pl.pallas_call(
        body,
        grid=(M // tm,),
        in_specs=[pl.BlockSpec((K, tm), lambda i: (0, i)), _const_spec((K, N))],
        out_specs=pl.BlockSpec((tm, N), lambda i: (i, 0)),
        out_shape=jax.ShapeDtypeStruct((M, N), out_dtype),
        compiler_params=_cparams("parallel"),
        name=name,
    )(a, b)


def _mm_tn(a, b, *, kc, name, out_dtype=F32):
    K, M = a.shape
    N = b.shape[1]
    if M > 1024:
        return _mm_tn_resident(a, b, tm=256, kc=min(kc, 1024), name=name, out_dtype=out_dtype)
    tm = M
    nk = K // kc

    def body(a_ref, b_ref, o_ref, acc):
        k = pl.program_id(1)
        part = _dot_tn(_bf(a_ref[...]), _bf(b_ref[...]))

        @pl.when(k == 0)
        def _():
            acc[...] = part

        @pl.when(k > 0)
        def _():
            acc[...] += part

        @pl.when(k == nk - 1)
        def _():
            o_ref[...] = acc[...].astype(o_ref.dtype)

    return pl.pallas_call(
        body,
        grid=(M // tm, nk),
        in_specs=[pl.BlockSpec((kc, tm), lambda i, k: (k, i)), pl.BlockSpec((kc, N), lambda i, k: (k, 0))],
        out_specs=pl.BlockSpec((tm, N), lambda i, k: (i, 0)),
        out_shape=jax.ShapeDtypeStruct((M, N), out_dtype),
        scratch_shapes=[pltpu.VMEM((tm, N), F32)],
        compiler_params=_cparams("parallel", "arbitrary"),
        name=name,
    )(a, b)


def _grad_x(d_qfv, d_hg_gl, d_sq, d_skv, d_mq, w_qfv, win_t, add, dep, *, tm):
    M = add.shape[0]
    pieces = (d_qfv, d_hg_gl, d_sq, d_skv, d_mq)

    def body(qfv_ref, hggl_ref, sq_ref, skv_ref, mq_ref, wq_ref, w_ref, add_ref, dep_ref, o_ref):
        del dep_ref
        acc = add_ref[...] + _dot(qfv_ref[...], wq_ref[...])
        acc = acc + _dot(hggl_ref[:, 0:1024], w_ref[C_HG:C_SQ, :])
        acc = acc + _dot(hggl_ref[:, 1024:4096], w_ref[C_GL:IN_COLS, :])
        acc = acc + _dot(sq_ref[...], w_ref[C_SQ:C_SK, :])
        acc = acc + _dot(skv_ref[...], w_ref[C_SK:C_MQ, :])
        o_ref[...] = acc + _dot(mq_ref[...], w_ref[C_MQ:C_GL, :])

    return pl.pallas_call(
        body,
        grid=(M // tm,),
        in_specs=[pl.BlockSpec((tm, p.shape[1]), lambda i: (i, 0)) for p in pieces]
        + [_const_spec(w_qfv.shape), _const_spec(win_t.shape), pl.BlockSpec((tm, D_MODEL), lambda i: (i, 0)), _dep_spec()],
        out_specs=pl.BlockSpec((tm, D_MODEL), lambda i: (i, 0)),
        out_shape=jax.ShapeDtypeStruct((M, D_MODEL), F32),
        compiler_params=_cparams("parallel"),
        name="grad_x",
    )(*pieces, w_qfv, win_t, add, dep)


def _lower_bound(lbl_ref):
    l0 = lbl_ref[0:1, :]
    l1 = lbl_ref[1:2, :]
    mx = jnp.maximum(l0, l1)
    e0 = jnp.exp(l0 - mx)
    e1 = jnp.exp(l1 - mx)
    return e0 / (e0 + e1)


def _tri(lower):
    r = lax.broadcasted_iota(jnp.int32, (HG_CHUNK, HG_CHUNK), 0)
    c = lax.broadcasted_iota(jnp.int32, (HG_CHUNK, HG_CHUNK), 1)
    return (r >= c) if lower else (r <= c)


def _hg_gates(fl, lb):
    sg = _sig(fl)
    f = lb + (1.0 - lb) * sg
    return sg, f, jnp.log(f), 1.0 - f


def _scan_rows(v, reverse=False):
    row = lax.broadcasted_iota(jnp.int32, v.shape, 0)
    s = 1
    while s < HG_CHUNK:
        if reverse:
            v = v + jnp.where(row < HG_CHUNK - s, pltpu.roll(v, HG_CHUNK - s, 0), 0.0)
        else:
            v = v + jnp.where(row >= s, pltpu.roll(v, s, 0), 0.0)
        s *= 2
    return v


def _hgrn_fwd(zmain, lb_logits, *, T):
    S = zmain.shape[0]
    nc = T // HG_CHUNK

    def body(q_ref, f_ref, v_ref, lbl_ref, o_ref, st_ref, state):
        @pl.when(pl.program_id(1) == 0)
        def _():
            state[...] = jnp.zeros_like(state)

        lb = _lower_bound(lbl_ref)
        tril = _tri(True)
        qis, updates, decays, intra = [], [], [], []
        for c in range(nc):
            sl = pl.ds(c * HG_CHUNK, HG_CHUNK)
            _, _, g, k = _hg_gates(_f32(f_ref[sl, :]), lb)
            b = _scan_rows(g)
            bl = jnp.sum(g, axis=0, keepdims=True)
            qi = _bf(_f32(q_ref[sl, :]) * jnp.exp(b))
            ki = _bf(k * jnp.exp(-b))
            ko = _bf(k * jnp.exp(bl - b))
            vb = _bf(v_ref[sl, :])
            att = jnp.where(tril, _dot_nt(qi, ki), 0.0)
            intra.append(_dot(_bf(att), vb))
            qis.append(qi)
            updates.append(_dot_tn(vb, ko))
            decays.append(jnp.exp(bl))
        st = state[...]
        for c in range(nc):
            st_ref[0, c] = st
            o_ref[pl.ds(c * HG_CHUNK, HG_CHUNK), :] = intra[c] + _dot_nt(qis[c], _bf(st))
            st = st * decays[c] + updates[c]
        state[...] = st

    col = lambda base: pl.BlockSpec((T, HG_DK), lambda h, t: (t, base + h))
    return pl.pallas_call(
        body,
        grid=(HG_HEADS, S // T),
        in_specs=[col(0), col(8), col(16), pl.BlockSpec((2, HG_DK), lambda h, t: (0, h))],
        out_specs=[
            pl.BlockSpec((T, HG_DK), lambda h, t: (t, h)),
            pl.BlockSpec((1, nc, HG_DK, HG_DK), lambda h, t: (h, t, 0, 0)),
        ],
        out_shape=[
            jax.ShapeDtypeStruct((S, D_MODEL), F32),
            jax.ShapeDtypeStruct((HG_HEADS, S // HG_CHUNK, HG_DK, HG_DK), F32),
        ],
        scratch_shapes=[pltpu.VMEM((HG_DK, HG_DK), F32)],
        compiler_params=_cparams("parallel", "arbitrary"),
        name="hgrn_fwd",
    )(zmain, zmain, zmain, lb_logits)


def _hgrn_bwd(zmain, lb_logits, states, d_o, *, T):
    S = zmain.shape[0]
    nc = T // HG_CHUNK
    nt = S // T

    def body(q_ref, f_ref, v_ref, lbl_ref, st_ref, do_ref, dz_ref, dlb_ref, dstate):
        @pl.when(pl.program_id(1) == 0)
        def _():
            dstate[...] = jnp.zeros_like(dstate)
            dlb_ref[...] = jnp.zeros_like(dlb_ref)

        lb = _lower_bound(lbl_ref)
        tril = _tri(True)
        last_row = lax.broadcasted_iota(jnp.int32, (HG_CHUNK, HG_DK), 0) == HG_CHUNK - 1
        saved = []
        for c in range(nc):
            sl = pl.ds(c * HG_CHUNK, HG_CHUNK)
            sg, f, g, k = _hg_gates(_f32(f_ref[sl, :]), lb)
            b = _scan_rows(g)
            bl = jnp.sum(g, axis=0, keepdims=True)
            eb = jnp.exp(b)
            enb = jnp.exp(-b)
            eo = jnp.exp(bl - b)
            q_in = _f32(q_ref[sl, :]) * eb
            k_in = k * enb
            k_out = k * eo
            qi, ki, ko = _bf(q_in), _bf(k_in), _bf(k_out)
            vb = _bf(v_ref[sl, :])
            dob = do_ref[sl, :]
            att = jnp.where(tril, _dot_nt(qi, ki), 0.0)
            d_att = _bf(jnp.where(tril, _dot_nt(dob, vb), 0.0))
            d_kin = _dot_tn(d_att, qi)
            saved.append(dict(
                sg=sg, f=f, eb=eb, enb=enb, eo=eo, ebl=jnp.exp(bl), k_out=k_out, ko=ko, vb=vb, dob=dob,
                d_v=_dot_tn(_bf(att), dob), d_qin=_dot(d_att, ki), d_kin=d_kin,
                qk=(q_in, k_in), d_state=_dot_tn(dob, qi)))
        dst = dstate[...]
        dsts = [None] * nc
        for c in reversed(range(nc)):
            dsts[c] = dst
            dst = dst * saved[c]["ebl"] + saved[c]["d_state"]
        dstate[...] = dst
        dlb = jnp.zeros((1, HG_DK), F32)
        for c in range(nc):
            sl = pl.ds(c * HG_CHUNK, HG_CHUNK)
            s = saved[c]
            q_in, k_in = s["qk"]
            st = st_ref[0, c]
            dstb = _bf(dsts[c])
            d_v = s["d_v"] + _dot_nt(s["ko"], dstb)
            d_qin = s["d_qin"] + _dot(s["dob"], _bf(st))
            d_kout = _dot(s["vb"], dstb)
            d_decay = jnp.sum(dsts[c] * st, axis=0, keepdims=True)
            kk = d_kout * s["k_out"]
            d_b = d_qin * q_in - s["d_kin"] * k_in - kk
            d_bl = jnp.sum(kk, axis=0, keepdims=True) + d_decay * s["ebl"]
            d_g = _scan_rows(d_b + jnp.where(last_row, d_bl, 0.0), reverse=True)
            d_f = d_g / s["f"] - (s["d_kin"] * s["enb"] + d_kout * s["eo"])
            dz_ref[sl, 0:HG_DK] = _bf(d_qin * s["eb"])
            dz_ref[sl, HG_DK:2 * HG_DK] = _bf(d_f * (1.0 - lb) * s["sg"] * (1.0 - s["sg"]))
            dz_ref[sl, 2 * HG_DK:3 * HG_DK] = _bf(d_v)
            dlb = dlb + jnp.sum(d_f * (1.0 - s["sg"]), axis=0, keepdims=True)
        dlb_ref[...] += dlb

    rev = lambda base: pl.BlockSpec((T, HG_DK), lambda h, t: (nt - 1 - t, base + h))
    outc = pl.BlockSpec((T, HG_DK), lambda h, t: (nt - 1 - t, h))
    return pl.pallas_call(
        body,
        grid=(HG_HEADS, nt),
        in_specs=[
            rev(0), rev(8), rev(16),
            pl.BlockSpec((2, HG_DK), lambda h, t: (0, h)),
            pl.BlockSpec((1, nc, HG_DK, HG_DK), lambda h, t: (h, nt - 1 - t, 0, 0)),
            outc,
        ],
        out_specs=[pl.BlockSpec((T, 3 * HG_DK), lambda h, t: (nt - 1 - t, h)),
                   pl.BlockSpec((1, HG_DK), lambda h, t: (0, h))],
        out_shape=[jax.ShapeDtypeStruct((S, 3 * D_MODEL), BF16), jax.ShapeDtypeStruct((1, D_MODEL), F32)],
        scratch_shapes=[pltpu.VMEM((HG_DK, HG_DK), F32)],
        compiler_params=_cparams("parallel", "arbitrary"),
        name="hgrn_bwd",
    )(zmain, zmain, zmain, lb_logits, states, d_o)


def _t5_bucket_table():
    qi = jnp.arange(SWA_BLOCK)[:, None] + SWA_BLOCK
    kj = jnp.arange(2 * SWA_BLOCK)[None, :]
    n = jnp.clip(qi - kj, 0, SWA_WINDOW - 1)
    max_exact = NUM_BUCKETS // 2
    nf = jnp.maximum(n, 1).astype(F32)
    large = max_exact + (jnp.log(nf / max_exact) / math.log(MAX_DISTANCE / max_exact)
                         * (NUM_BUCKETS - max_exact)).astype(jnp.int32)
    large = jnp.minimum(large, NUM_BUCKETS - 1)
    return jnp.where(n < max_exact, n, large).astype(jnp.int32)


SWA_ROWS = 32


def _swa_bias_init(bias, bucket_ref, rb_ref):
    bk = bucket_ref[...]
    qi = lax.broadcasted_iota(jnp.int32, bk.shape, 0) + SWA_BLOCK
    kj = lax.broadcasted_iota(jnp.int32, bk.shape, 1)
    band = (qi - kj >= 0) & (qi - kj < SWA_WINDOW)
    for h in range(SWA_HEADS):
        def sel(b, acc, h=h):
            return jnp.where(bk == b, rb_ref[b, h], acc)
        t = lax.fori_loop(0, NUM_BUCKETS, sel, jnp.zeros(bk.shape, F32))
        bias[1, h] = jnp.where(band, t, -jnp.inf)
        bias[0, h] = jnp.where(band & (kj >= SWA_BLOCK), t, -jnp.inf)


def _lane_halves(t, kv_head):
    lane = lax.broadcasted_iota(jnp.int32, t.shape, 1)
    rolled = pltpu.roll(t, 64, 1)
    zero = jnp.zeros_like(t)
    if kv_head == 0:
        return jnp.where(lane < 64, t, zero), jnp.where(lane >= 64, rolled, zero)
    return jnp.where(lane < 64, rolled, zero), jnp.where(lane >= 64, t, zero)


def _swa_zero_key0(t):
    return jnp.where(lax.broadcasted_iota(jnp.int32, t.shape, 0) == 0, jnp.zeros_like(t), t)


def _swa_probs(s, masked_bias, sink):
    s = s + masked_bias
    m = jnp.maximum(jnp.max(s, axis=-1, keepdims=True), sink)
    p = jnp.exp(s - m)
    es = jnp.exp(sink - m)
    inv = 1.0 / (jnp.sum(p, axis=-1, keepdims=True) + es)
    return p * inv, es * inv


def _swa_fwd(zmain, bucket, rel_bias, sinks):
    S = zmain.shape[0]
    nb = S // SWA_BLOCK
    scale = SWA_HEAD_DIM ** -0.5

    def body(q_ref, kvc_ref, kvp_ref, bucket_ref, rb_ref, sk_ref, o_ref, p_ref, bias):
        n = pl.program_id(0)

        @pl.when(n == 0)
        def _():
            _swa_bias_init(bias, bucket_ref, rb_ref)

        later = jnp.minimum(n, 1)
        kk = _bf(jnp.concatenate([kvp_ref[:, 0:128], kvc_ref[:, 0:128]], axis=0))
        vv = _swa_zero_key0(_bf(jnp.concatenate([kvp_ref[:, 128:256], kvc_ref[:, 128:256]], axis=0)))
        first_col = lax.broadcasted_iota(jnp.int32, (SWA_ROWS, 2 * SWA_BLOCK), 1) == 0
        for kvh in range(2):
            ka, kb = _lane_halves(kk, kvh)
            va, vb = _lane_halves(vv, kvh)
            qst = _bf(jnp.concatenate([q_ref[:, pl.ds((kvh * 4 + jj) * 128, 128)] for jj in range(4)], axis=0) * scale)
            probs = []
            for odd, kx in enumerate((ka, kb)):
                s = _dot_nt(qst, kx)
                parts = []
                for jj in range(4):
                    h = 2 * (kvh * 4 + jj) + odd
                    for r0 in range(0, SWA_BLOCK, SWA_ROWS):
                        p, ps = _swa_probs(s[jj * SWA_BLOCK + r0:jj * SWA_BLOCK + r0 + SWA_ROWS],
                                           bias[later, h, pl.ds(r0, SWA_ROWS), :], sk_ref[0, h])
                        part = _bf(jnp.where(first_col, ps, p))
                        p_ref[pl.ds(r0, SWA_ROWS), pl.ds(h * 2 * SWA_BLOCK, 2 * SWA_BLOCK)] = part
                        parts.append(part)
                probs.append(jnp.concatenate(parts, axis=0))
            ost = _dot(probs[0], va) + _dot(probs[1], vb)
            for jj in range(4):
                o_ref[:, pl.ds((kvh * 4 + jj) * 128, 128)] = ost[jj * SWA_BLOCK:(jj + 1) * SWA_BLOCK]

    smem = pl.BlockSpec(memory_space=pltpu.SMEM)
    return pl.pallas_call(
        body,
        grid=(nb,),
        in_specs=[
            pl.BlockSpec((SWA_BLOCK, 1024), lambda n: (n, C_SQ // 1024)),
            pl.BlockSpec((SWA_BLOCK, 256), lambda n: (n, C_SK // 256)),
            pl.BlockSpec((SWA_BLOCK, 256), lambda n: (jnp.maximum(n - 1, 0), C_SK // 256)),
            _const_spec((SWA_BLOCK, 2 * SWA_BLOCK)), smem, smem,
        ],
        out_specs=[pl.BlockSpec((SWA_BLOCK, 1024), lambda n: (n, 0)),
                   pl.BlockSpec((SWA_BLOCK, SWA_HEADS * 2 * SWA_BLOCK), lambda n: (n, 0))],
        out_shape=[jax.ShapeDtypeStruct((S, 1024), F32),
                   jax.ShapeDtypeStruct((S, SWA_HEADS * 2 * SWA_BLOCK), BF16)],
        scratch_shapes=[pltpu.VMEM((2, SWA_HEADS, SWA_BLOCK, 2 * SWA_BLOCK), F32)],
        compiler_params=_cparams("arbitrary"),
        name="swa_fwd",
    )(zmain, zmain, zmain, bucket, rel_bias, sinks)


def _swa_bwd(zmain, o_b, probs, d_o, bucket, dep):
    S = zmain.shape[0]
    nb = S // SWA_BLOCK
    scale = SWA_HEAD_DIM ** -0.5

    def body(q_ref, kvc_ref, kvp_ref, o_ref, p_ref, do_ref, bucket_ref, dep_ref,
             dq_ref, dkv_ref, drb_ref, dsk_ref, dbias, carry):
        del dep_ref
        n = pl.program_id(0)

        @pl.when(n == 0)
        def _():
            dbias[...] = jnp.zeros_like(dbias)
            carry[...] = jnp.zeros_like(carry)

        @pl.when(n < nb)
        def _():
            kk = _swa_zero_key0(_bf(jnp.concatenate([kvp_ref[:, 0:128], kvc_ref[:, 0:128]], axis=0)))
            vv = _swa_zero_key0(_bf(jnp.concatenate([kvp_ref[:, 128:256], kvc_ref[:, 128:256]], axis=0)))
            lane = lax.broadcasted_iota(jnp.int32, (2 * SWA_BLOCK, 128), 1)
            lane_q = lax.broadcasted_iota(jnp.int32, (4 * SWA_BLOCK, 128), 1)
            dk_parts, dv_parts = [], []
            for kvh in range(2):
                ka, kb = _lane_halves(kk, kvh)
                va, vb = _lane_halves(vv, kvh)
                pair_cols = [pl.ds((kvh * 4 + jj) * 128, 128) for jj in range(4)]
                qst = _bf(jnp.concatenate([q_ref[:, cl] for cl in pair_cols], axis=0) * scale)
                dost = jnp.concatenate([do_ref[:, cl] for cl in pair_cols], axis=0)
                prod = dost.astype(F32) * jnp.concatenate([o_ref[:, cl] for cl in pair_cols], axis=0)
                dq_st = jnp.zeros((4 * SWA_BLOCK, 128), F32)
                zks, zvs = [], []
                for odd, (kx, vx) in enumerate(((ka, va), (kb, vb))):
                    keep = (lane_q >= 64) if odd else (lane_q < 64)
                    delta = jnp.sum(jnp.where(keep, prod, 0.0), axis=-1, keepdims=True)
                    dp = _dot_nt(dost, vx)
                    p_parts, ds_parts = [], []
                    for jj in range(4):
                        h = 2 * (kvh * 4 + jj) + odd
                        rows = slice(jj * SWA_BLOCK, (jj + 1) * SWA_BLOCK)
                        p = p_ref[:, pl.ds(h * 2 * SWA_BLOCK, 2 * SWA_BLOCK)]
                        ds = _f32(p) * (dp[rows] - delta[rows])
                        dbias[h] += ds
                        p_parts.append(p)
                        ds_parts.append(_bf(ds))
                    pst = jnp.concatenate(p_parts, axis=0)
                    dsst = jnp.concatenate(ds_parts, axis=0)
                    dq_st = dq_st + _dot(dsst, kx)
                    zks.append(_dot_tn(dsst, qst))
                    zvs.append(_dot_tn(pst, dost))
                for jj in range(4):
                    dq_ref[:, pair_cols[jj]] = _bf(dq_st[jj * SWA_BLOCK:(jj + 1) * SWA_BLOCK] * scale)
                zk = jnp.where(lane < 64, zks[0], zks[1])
                zv = jnp.where(lane < 64, zvs[0], zvs[1])
                dk_parts.append(zk + pltpu.roll(zk, 64, 1))
                dv_parts.append(zv + pltpu.roll(zv, 64, 1))
            dk = jnp.where(lane < 64, dk_parts[0], dk_parts[1])
            dv = jnp.where(lane < 64, dv_parts[0], dv_parts[1])
            dkv = _swa_zero_key0(jnp.concatenate([dk, dv], axis=1))
            dkv_ref[...] = _bf(carry[...] + dkv[0:SWA_BLOCK])
            carry[...] = dkv[SWA_BLOCK:]

        @pl.when(n == nb)
        def _():
            dkv_ref[...] = _bf(carry[...])
            first_col = lax.broadcasted_iota(jnp.int32, (SWA_BLOCK, 2 * SWA_BLOCK), 1) == 0
            bk = jnp.where(first_col, -1, bucket_ref[...])

            def total(v):
                return jnp.broadcast_to(jnp.sum(jnp.sum(v, axis=1, keepdims=True), axis=0, keepdims=True), (8, 128))

            def per_head(h, _):
                db = dbias[h]
                dsk_ref[h] = total(jnp.where(first_col, db, 0.0))

                def per_bucket(b, _):
                    drb_ref[h * NUM_BUCKETS + b] = total(jnp.where(bk == b, db, 0.0))
                    return 0

                return lax.fori_loop(0, NUM_BUCKETS, per_bucket, 0)

            lax.fori_loop(0, SWA_HEADS, per_head, 0)

    cur = lambda n: jnp.minimum(n, nb - 1)
    prev = lambda n: jnp.maximum(jnp.minimum(n, nb - 1) - 1, 0)
    return pl.pallas_call(
        body,
        grid=(nb + 1,),
        in_specs=[
            pl.BlockSpec((SWA_BLOCK, 1024), lambda n: (cur(n), C_SQ // 1024)),
            pl.BlockSpec((SWA_BLOCK, 256), lambda n: (cur(n), C_SK // 256)),
            pl.BlockSpec((SWA_BLOCK, 256), lambda n: (prev(n), C_SK // 256)),
            pl.BlockSpec((SWA_BLOCK, 1024), lambda n: (cur(n), 0)),
            pl.BlockSpec((SWA_BLOCK, SWA_HEADS * 2 * SWA_BLOCK), lambda n: (cur(n), 0)),
            pl.BlockSpec((SWA_BLOCK, 1024), lambda n: (cur(n), 0)),
            _const_spec((SWA_BLOCK, 2 * SWA_BLOCK)), _dep_spec(),
        ],
        out_specs=[
            pl.BlockSpec((SWA_BLOCK, 1024), lambda n: (cur(n), 0)),
            pl.BlockSpec((SWA_BLOCK, 256), lambda n: (jnp.maximum(n - 1, 0), 0)),
            pl.BlockSpec((SWA_HEADS * NUM_BUCKETS, 8, 128), lambda n: (0, 0, 0)),
            pl.BlockSpec((SWA_HEADS, 8, 128), lambda n: (0, 0, 0)),
        ],
        out_shape=[
            jax.ShapeDtypeStruct((S, 1024), BF16),
            jax.ShapeDtypeStruct((S, 256), BF16),
            jax.ShapeDtypeStruct((SWA_HEADS * NUM_BUCKETS, 8, 128), F32),
            jax.ShapeDtypeStruct((SWA_HEADS, 8, 128), F32),
        ],
        scratch_shapes=[
            pltpu.VMEM((SWA_HEADS, SWA_BLOCK, 2 * SWA_BLOCK), F32),
            pltpu.VMEM((SWA_BLOCK, 256), F32),
        ],
        compiler_params=_cparams("arbitrary"),
        name="swa_bwd",
    )(zmain, zmain, zmain, o_b, probs, d_o, bucket, dep)


def _mem_probs(q_ref, k):
    qs = _bf(q_ref[...] * (MEM_HEAD_DIM ** -0.5))
    s = _dot_nt(qs, k)
    e = jnp.exp(s - jnp.max(s, axis=-1, keepdims=True))
    return qs, e / jnp.sum(e, axis=-1, keepdims=True)


def _mem_q_specs(T):
    return [pl.BlockSpec((T, MEM_HEAD_DIM), lambda t, h=h: (t, C_MQ // MEM_HEAD_DIM + h)) for h in range(MEM_HEADS)]


def _mem_kv_proj(mem, g2):
    def body(mem_ref, w_ref, o_ref):
        o_ref[...] = _dot_nt(_bf(mem_ref[...]), _rows(w_ref))

    return pl.pallas_call(
        body,
        grid=(1,),
        in_specs=[pl.BlockSpec((MEM_LEN, D_MODEL), lambda i: (0, 0)), _gathered_spec(R_KV, R_OTHER)],
        out_specs=pl.BlockSpec((MEM_LEN, 2048), lambda i: (0, 0)),
        out_shape=jax.ShapeDtypeStruct((MEM_LEN, 2048), F32),
        compiler_params=_cparams("arbitrary"),
        name="mem_kv_proj",
    )(mem, g2)


def _mem_fwd(zmain, mkv, *, T):
    S = zmain.shape[0]

    def body(q0, q1, q2, q3, kv_ref, o_ref):
        for h, q_ref in enumerate((q0, q1, q2, q3)):
            cols = pl.ds(h * MEM_HEAD_DIM, MEM_HEAD_DIM)
            _, p = _mem_probs(q_ref, _bf(kv_ref[:, cols]))
            o_ref[:, cols] = _dot(_bf(p), _bf(kv_ref[:, pl.ds(1024 + h * MEM_HEAD_DIM, MEM_HEAD_DIM)]))

    return pl.pallas_call(
        body,
        grid=(S // T,),
        in_specs=_mem_q_specs(T) + [_const_spec((MEM_LEN, 2048))],
        out_specs=pl.BlockSpec((T, 1024), lambda t: (t, 0)),
        out_shape=jax.ShapeDtypeStruct((S, 1024), F32),
        compiler_params=_cparams("parallel"),
        name="mem_fwd",
    )(zmain, zmain, zmain, zmain, mkv)


def _mem_bwd(zmain, mkv, o_c, d_o, *, T):
    S = zmain.shape[0]
    scale = MEM_HEAD_DIM ** -0.5

    def body(q0, q1, q2, q3, kv_ref, o_ref, do_ref, dq_ref, dkv_ref):
        @pl.when(pl.program_id(0) == 0)
        def _():
            dkv_ref[...] = jnp.zeros_like(dkv_ref)

        for h, q_ref in enumerate((q0, q1, q2, q3)):
            cols = pl.ds(h * MEM_HEAD_DIM, MEM_HEAD_DIM)
            vcols = pl.ds(1024 + h * MEM_HEAD_DIM, MEM_HEAD_DIM)
            kb = _bf(kv_ref[:, cols])
            qs, p = _mem_probs(q_ref, kb)
            dob = do_ref[:, cols]
            delta = jnp.sum(dob.astype(F32) * o_ref[:, cols], axis=-1, keepdims=True)
            ds = _bf(p * (_dot_nt(dob, _bf(kv_ref[:, vcols])) - delta))
            dq_ref[:, cols] = _bf(_dot(ds, kb) * scale)
            dkv_ref[:, cols] += _dot_tn(ds, qs)
            dkv_ref[:, vcols] += _dot_tn(_bf(p), dob)

    row = pl.BlockSpec((T, 1024), lambda t: (t, 0))
    return pl.pallas_call(
        body,
        grid=(S // T,),
        in_specs=_mem_q_specs(T) + [_const_spec((MEM_LEN, 2048)), row, row],
        out_specs=[row, pl.BlockSpec((MEM_LEN, 2048), lambda t: (0, 0))],
        out_shape=[jax.ShapeDtypeStruct((S, 1024), BF16), jax.ShapeDtypeStruct((MEM_LEN, 2048), F32)],
        compiler_params=_cparams("arbitrary"),
        name="mem_bwd",
    )(zmain, zmain, zmain, zmain, mkv, o_c, d_o)


def _layer_norm(u):
    mu = jnp.mean(u, axis=-1, keepdims=True)
    xc = u - mu
    rstd = lax.rsqrt(jnp.mean(xc * xc, axis=-1, keepdims=True) + LN_EPS)
    return xc * rstd, rstd


def _layer_norm_bwd(dy, gamma, xhat, rstd):
    dxh = dy * gamma
    return rstd * (dxh - jnp.mean(dxh, axis=-1, keepdims=True) - xhat * jnp.mean(dxh * xhat, axis=-1, keepdims=True))


def _merge_forward(oraw_ref, hg_ref, ob_ref, oc_ref, gl_ref, x_ref, gain_ref, wbh, wbs, wbm, wout):
    ys, rs = [], []
    for h in range(HG_HEADS):
        oh = oraw_ref[:, pl.ds(h * HG_DK, HG_DK)]
        r = lax.rsqrt(jnp.mean(oh * oh, axis=-1, keepdims=True) + RMS_EPS)
        ys.append(oh * r)
        rs.append(r)
    y = jnp.concatenate(ys, axis=1)
    hg = _f32(hg_ref[...])
    sg = _sig(hg)
    silu = hg * sg
    oa = _bf(y * gain_ref[...] * silu)
    pa = _dot(oa, _rows(wbh))
    pb = _dot(_bf(ob_ref[...]), _rows(wbs))
    pc = _dot(_bf(oc_ref[...]), _rows(wbm))
    g0 = _sig(_f32(gl_ref[:, 0:1024]))
    g1 = _sig(_f32(gl_ref[:, 1024:2048]))
    g2 = _sig(_f32(gl_ref[:, 2048:3072]))
    m = _bf(g0 * pa + g1 * pb + g2 * pc)
    u1 = ALPHA * x_ref[...] + _dot(m, _rows(wout))
    xhat, rstd = _layer_norm(u1)
    return dict(y=y, rs=rs, hg=hg, sg=sg, silu=silu, oa=oa, pa=pa, pb=pb, pc=pc,
                g0=g0, g1=g1, g2=g2, m=m, xhat=xhat, rstd=rstd)


def _gathered_spec(lo, hi):
    n = hi - lo
    return pl.BlockSpec((N_DEV, n, D_MODEL), lambda *_: (0, lo // n, 0), pipeline_mode=pl.Buffered(1))


def _rows(w_ref):
    return w_ref[...].reshape(-1, D_MODEL)


def _merge_in_specs(T):
    row = lambda w, c=0: pl.BlockSpec((T, w), lambda i: (i, c))
    vec = pl.BlockSpec((1, D_MODEL), lambda i: (0, 0))
    w = [_gathered_spec(lo, hi) for lo, hi in ((R_BH, R_BS), (R_BS, R_BM), (R_BM, R_OUT), (R_OUT, R_KV))]
    return [row(1024), row(1024, C_HG // 1024), row(1024), row(1024), row(3072), row(1024), vec, *w, vec, vec]


def _merge_fwd(o_raw, zmain, o_b, o_c, gl, x, gain, wbh, wbs, wbm, wout, ln_g, ln_b, *, T):
    S = x.shape[0]

    def body(oraw_ref, hg_ref, ob_ref, oc_ref, gl_ref, x_ref, gain_ref, wbh_r, wbs_r, wbm_r, wout_r, g_ref, b_ref,
             h1_ref, h1b_ref):
        f = _merge_forward(oraw_ref, hg_ref, ob_ref, oc_ref, gl_ref, x_ref, gain_ref, wbh_r, wbs_r, wbm_r, wout_r)
        h1 = f["xhat"] * g_ref[...] + b_ref[...]
        h1_ref[...] = h1
        h1b_ref[...] = _bf(h1)

    row = pl.BlockSpec((T, D_MODEL), lambda i: (i, 0))
    return pl.pallas_call(
        body,
        grid=(S // T,),
        in_specs=_merge_in_specs(T),
        out_specs=[row, row],
        out_shape=[jax.ShapeDtypeStruct((S, D_MODEL), F32), jax.ShapeDtypeStruct((S, D_MODEL), BF16)],
        compiler_params=_cparams("parallel"),
        name="merge_fwd",
    )(o_raw, zmain, o_b, o_c, gl, x, gain, wbh, wbs, wbm, wout, ln_g, ln_b)


def _merge_bwd(d_h1, o_raw, zmain, o_b, o_c, gl, x, gain, wbh, wbs, wbm, wout, ln_g, ln_b, *, T):
    S = x.shape[0]

    def body(dh1_ref, oraw_ref, hg_ref, ob_ref, oc_ref, gl_ref, x_ref, gain_ref, wbh_r, wbs_r, wbm_r, wout_r, g_ref, b_ref,
             dx_ref, du1_ref, m_ref, oa_ref, dpa_ref, dpb_ref, dpc_ref, doraw_ref, dob_ref, doc_ref, dz_ref,
             dgain_ref, dg_ref, db_ref):
        del b_ref

        @pl.when(pl.program_id(0) == 0)
        def _():
            dgain_ref[...] = jnp.zeros_like(dgain_ref)
            dg_ref[...] = jnp.zeros_like(dg_ref)
            db_ref[...] = jnp.zeros_like(db_ref)

        f = _merge_forward(oraw_ref, hg_ref, ob_ref, oc_ref, gl_ref, x_ref, gain_ref, wbh_r, wbs_r, wbm_r, wout_r)
        dh1 = dh1_ref[...]
        dg_ref[...] += jnp.sum(dh1 * f["xhat"], axis=0, keepdims=True)
        db_ref[...] += jnp.sum(dh1, axis=0, keepdims=True)
        du1 = _layer_norm_bwd(dh1, g_ref[...], f["xhat"], f["rstd"])
        dx_ref[...] = ALPHA * du1
        du1b = _bf(du1)
        du1_ref[...] = du1b
        m_ref[...] = f["m"]
        oa_ref[...] = f["oa"]
        dm = _dot_nt(du1b, _rows(wout_r))
        for i, (g, p, dp_ref, dob_r, w_r) in enumerate((
                (f["g0"], f["pa"], dpa_ref, None, wbh_r),
                (f["g1"], f["pb"], dpb_ref, dob_ref, wbs_r),
                (f["g2"], f["pc"], dpc_ref, doc_ref, wbm_r))):
            dz_ref[:, pl.ds((i + 1) * 1024, 1024)] = _bf(dm * p * g * (1.0 - g))
            dp = _bf(dm * g)
            dp_ref[...] = dp
            d_branch = _dot_nt(dp, _rows(w_r))
            if dob_r is not None:
                dob_r[...] = _bf(d_branch)
            else:
                doa = d_branch
        gain = gain_ref[...]
        t = doa * f["y"]
        dgain_ref[...] += jnp.sum(t * f["silu"], axis=0, keepdims=True)
        sg = f["sg"]
        dz_ref[:, 0:1024] = _bf(t * gain * sg * (1.0 + f["hg"] * (1.0 - sg)))
        dy = doa * gain * f["silu"]
        for h in range(HG_HEADS):
            cols = slice(h * HG_DK, (h + 1) * HG_DK)
            yh = f["y"][:, cols]
            dyh = dy[:, cols]
            doraw_ref[:, pl.ds(h * HG_DK, HG_DK)] = _bf(
                f["rs"][h] * (dyh - yh * jnp.mean(dyh * yh, axis=-1, keepdims=True)))

    row = lambda w: pl.BlockSpec((T, w), lambda i: (i, 0))
    vec = pl.BlockSpec((1, D_MODEL), lambda i: (0, 0))
    bshape = jax.ShapeDtypeStruct((S, D_MODEL), BF16)
    vshape = jax.ShapeDtypeStruct((1, D_MODEL), F32)
    return pl.pallas_call(
        body,
        grid=(S // T,),
        in_specs=[row(1024)] + _merge_in_specs(T),
        out_specs=[row(1024)] * 10 + [row(4096), vec, vec, vec],
        out_shape=[jax.ShapeDtypeStruct((S, D_MODEL), F32)] + [bshape] * 9
        + [jax.ShapeDtypeStruct((S, 4096), BF16), vshape, vshape, vshape],
        compiler_params=_cparams("arbitrary"),
        name="merge_bwd",
    )(d_h1, o_raw, zmain, o_b, o_c, gl, x, gain, wbh, wbs, wbm, wout, ln_g, ln_b)


def _mlp_fwd_bwd(h1, target, wup_t, wdn, ln_g, ln_b, *, T, FC):
    S = h1.shape[0]
    nf = D_FF // FC
    assert FC == R_BH - R_UP == R_UP - R_DN

    def body(h1_ref, t_ref, wup_ref, wdn_ref, g_ref, b_ref, dh1_ref, a_ref, dup_ref, du2_ref, loss_ref, dg_ref, db_ref, up_scr):
        @pl.when(pl.program_id(0) == 0)
        def _():
            loss_ref[...] = jnp.zeros_like(loss_ref)
            dg_ref[...] = jnp.zeros_like(dg_ref)
            db_ref[...] = jnp.zeros_like(db_ref)

        h1v = h1_ref[...]
        h1b = _bf(h1v)
        ff = jnp.zeros((T, D_MODEL), F32)
        for j in range(nf):
            rows = pl.ds(j * FC, FC)
            up = jnp.maximum(_dot_nt(h1b, wup_ref[j]), 0.0)
            up_scr[:, rows] = _bf(up)
            a = _bf(up * up)
            a_ref[:, rows] = a
            ff = ff + _dot(a, wdn_ref[j])
        xhat, rstd = _layer_norm(ALPHA * h1v + ff)
        gamma = g_ref[...]
        err = xhat * gamma + b_ref[...] - t_ref[...]
        loss_ref[...] += jnp.sum(jnp.sum(err * err, axis=-1, keepdims=True), axis=0, keepdims=True) * (0.5 / D_MODEL)
        dy = err * (1.0 / D_MODEL)
        dg_ref[...] += jnp.sum(dy * xhat, axis=0, keepdims=True)
        db_ref[...] += jnp.sum(dy, axis=0, keepdims=True)
        du2 = _layer_norm_bwd(dy, gamma, xhat, rstd)
        du2b = _bf(du2)
        du2_ref[...] = du2b
        dh1 = ALPHA * du2
        for j in range(nf):
            rows = pl.ds(j * FC, FC)
            dup = _bf(_dot_nt(du2b, wdn_ref[j]) * (2.0 * up_scr[:, rows].astype(F32)))
            dup_ref[:, rows] = dup
            dh1 = dh1 + _dot(dup, wup_ref[j])
        dh1_ref[...] = dh1

    row = lambda w: pl.BlockSpec((T, w), lambda i: (i, 0))
    vec = pl.BlockSpec((1, D_MODEL), lambda i: (0, 0))
    vshape = jax.ShapeDtypeStruct((1, D_MODEL), F32)
    return pl.pallas_call(
        body,
        grid=(S // T,),
        in_specs=[row(1024), row(1024), _gathered_spec(R_UP, R_BH), _gathered_spec(R_DN, R_UP), vec, vec],
        out_specs=[row(1024), row(D_FF), row(D_FF), row(1024), pl.BlockSpec((8, 128), lambda i: (0, 0)), vec, vec],
        out_shape=[
            jax.ShapeDtypeStruct((S, D_MODEL), F32),
            jax.ShapeDtypeStruct((S, D_FF), BF16),
            jax.ShapeDtypeStruct((S, D_FF), BF16),
            jax.ShapeDtypeStruct((S, D_MODEL), BF16),
            jax.ShapeDtypeStruct((8, 128), F32), vshape, vshape,
        ],
        scratch_shapes=[pltpu.VMEM((T, D_FF), BF16)],
        compiler_params=_cparams("arbitrary"),
        name="mlp_fwd_bwd",
    )(h1, target, wup_t, wdn, ln_g, ln_b)


def _local_step(x, mem, target, lb_logits, gain, sinks, rel_bias, ln1_g, ln1_b, ln2_g, ln2_b,
                win_t, dep0, other_weights, send_other_grads, send_small_grads, send_win_grad):
    S = x.shape[0]
    T = min(256, S)
    KC = min(2048, S)
    zmain, xb = _mm_nt(x, win_t, n_cols=C_GL, tm=min(512, S), tn=C_GL, out_dtype=BF16, name="in_proj_main", dep=dep0,
                       also_a_bf16=True)
    gl = _mm_nt(x, win_t[C_GL:], tm=min(512, S), tn=1536, out_dtype=BF16, name="in_proj_gates")
    bucket = _t5_bucket_table()

    o_raw, states = _hgrn_fwd(zmain, lb_logits, T=min(1024, S))
    o_b, swa_probs = _swa_fwd(zmain, bucket, rel_bias, sinks)
    g2 = other_weights(o_b)
    mkv = _mem_kv_proj(mem, g2)
    o_c = _mem_fwd(zmain, mkv, T=min(512, S))
    merge_args = (o_raw, zmain, o_b, o_c, gl, x, gain, g2, g2, g2, g2, ln1_g, ln1_b)
    h1, h1b = _merge_fwd(*merge_args, T=T)

    d_h1, act, d_up, du2, loss, d_ln2_g, d_ln2_b = _mlp_fwd_bwd(h1, target, g2, g2, ln2_g, ln2_b, T=min(512, S), FC=512)
    wgrad = functools.partial(_mm_tn, out_dtype=BF16)
    g_wdn = wgrad(act, du2, kc=KC, name="grad_w_down")
    g_wup_t = wgrad(d_up, h1b, kc=KC, name="grad_w_up")

    (dx_part, du1, m, oa, dpa, dpb, dpc, d_oraw, d_ob, d_oc, d_hg_gl,
     d_gain, d_ln1_g, d_ln1_b) = _merge_bwd(d_h1, *merge_args, T=T)
    g_wout = wgrad(m, du1, kc=KC, name="grad_w_out")
    g_wbh = wgrad(oa, dpa, kc=KC, name="grad_w_branch_hg")
    g_wbs = wgrad(o_b, dpb, kc=KC, name="grad_w_branch_swa")
    g_wbm = wgrad(o_c, dpc, kc=KC, name="grad_w_branch_mem")

    d_mq, d_mkv = _mem_bwd(zmain, mkv, o_c, d_oc, T=min(512, S))
    g_wkv_t = wgrad(d_mkv, mem, kc=MEM_LEN, name="grad_w_mem_kv")
    sent_others = send_other_grads(
        dict(wkv_t=g_wkv_t, wbh=g_wbh, wbs=g_wbs, wbm=g_wbm, wout=g_wout, wup_t=g_wup_t, wdn=g_wdn))
    d_sq, d_skv, d_rb, d_sink = _swa_bwd(zmain, o_b, swa_probs, d_ob, bucket, sent_others)
    d_qfv, d_lb = _hgrn_bwd(zmain, lb_logits, states, d_oraw, T=min(1024, S))
    sent_small = send_small_grads(dict(
        d_lb=d_lb, d_gain=d_gain, d_sink=d_sink[:, 0, 0].reshape(1, SWA_HEADS),
        d_rb=d_rb[:, 0, 0].reshape(SWA_HEADS, NUM_BUCKETS).T,
        d_ln1_g=d_ln1_g, d_ln1_b=d_ln1_b, d_ln2_g=d_ln2_g, d_ln2_b=d_ln2_b, loss=loss[0, 0]))

    head_major = lambda a: a.reshape(3, HG_HEADS, HG_DK, D_MODEL).transpose(1, 0, 2, 3).reshape(3 * D_MODEL, D_MODEL)
    col_major = lambda a: a.reshape(HG_HEADS, 3, HG_DK, D_MODEL).transpose(1, 0, 2, 3).reshape(3 * D_MODEL, D_MODEL)
    pieces = (d_qfv, d_hg_gl, d_sq, d_skv, d_mq)
    g_qfv, g_hg_gl, g_sq, g_skv, g_mq = [
        wgrad(p, xb, kc=KC, name="grad_w_in_" + n) for p, n in zip(pieces, ("qfv", "hg_gates", "swa_q", "swa_kv", "mem_q"))]
    g_win_t = jnp.concatenate([col_major(g_qfv), g_hg_gl[:D_MODEL], g_sq, g_skv, g_mq, g_hg_gl[D_MODEL:]], axis=0)
    sent_win = send_win_grad(g_win_t, sent_small)
    return _grad_x(*pieces, head_major(win_t[:C_HG]), win_t, dx_part, sent_win, tm=T)


MESH = pl.DeviceIdType.MESH
ANY = pl.BlockSpec(memory_space=pl.ANY)


def _coords():
    return lax.axis_index("x"), lax.axis_index("y"), lax.axis_index("c")


def _other_chips(x, y):
    return [(1 - x, y), (x, 1 - y), (1 - x, 1 - y)]


def _all_gather_weights(*arrays):
    na = len(arrays)

    def body(*refs):
        srcs, dsts = refs[:na], refs[na:2 * na]
        send_sems, recv_sems, local_sems = refs[2 * na:]
        x, y, c = _coords()
        me, sibling = (x, y, c), (x, y, 1 - c)
        chips = _other_chips(x, y)

        def slot(a, px, py, pc):
            return dsts[a].at[4 * px + 2 * py + pc]

        def copy(a, k, block, to, from_shard=False):
            return pltpu.make_async_remote_copy(
                src_ref=srcs[a] if from_shard else slot(a, *block), dst_ref=slot(a, *block),
                send_sem=send_sems.at[a * 7 + k], recv_sem=recv_sems.at[a * 7 + k],
                device_id=to, device_id_type=MESH)

        own = [pltpu.make_async_copy(srcs[a], slot(a, *me), local_sems.at[a]) for a in range(na)]
        for cp in own:
            cp.start()
        first = []
        for a in range(na):
            first.append(copy(a, 0, me, sibling, True))
            first += [copy(a, 1 + j, me, (*chip, c), True) for j, chip in enumerate(chips)]
        for cp in first:
            cp.start()
        passed = []
        for j, chip in enumerate(chips):
            for a in range(na):
                copy(a, 1 + j, (*chip, c), me).wait_recv()
                fwd = copy(a, 4 + j, (*chip, c), sibling)
                fwd.start()
                passed.append(fwd)
        for a in range(na):
            copy(a, 0, sibling, me).wait_recv()
            for j, chip in enumerate(chips):
                copy(a, 4 + j, (*chip, 1 - c), me).wait_recv()
        for cp in first + passed:
            cp.wait_send()
        for cp in own:
            cp.wait()

    return pl.pallas_call(
        body,
        in_specs=[ANY] * na,
        out_specs=[ANY] * na,
        out_shape=[jax.ShapeDtypeStruct((N_DEV,) + a.shape, a.dtype) for a in arrays],
        scratch_shapes=[pltpu.SemaphoreType.DMA((7 * na,)), pltpu.SemaphoreType.DMA((7 * na,)),
                        pltpu.SemaphoreType.DMA((na,))],
        name="all_gather_weights",
    )(*arrays)


HBM = pl.BlockSpec(memory_space=pltpu.HBM)
SEM = pl.BlockSpec(memory_space=pltpu.SEMAPHORE)
_DATAFLOW = pltpu.SideEffectType.DATAFLOW_SIDE_EFFECTING


def _peer(x, y, c, r):
    return x ^ (r >> 2), y ^ ((r >> 1) & 1), c ^ (r & 1)


def _direct_copies(src_ref, land_ref, send_sems, recv_sems, gather, receiving):
    x, y, c = _coords()
    me = 4 * x + 2 * y + c
    copies = []
    for r in range(1, N_DEV):
        px, py, pc = _peer(x, y, c, r)
        peer = 4 * px + 2 * py + pc
        if gather:
            src, dst = src_ref, land_ref.at[peer if receiving else me]
        else:
            src, dst = src_ref.at[peer], land_ref.at[r - 1]
        copies.append(pltpu.make_async_remote_copy(
            src_ref=src, dst_ref=dst, send_sem=send_sems.at[r - 1], recv_sem=recv_sems.at[r - 1],
            device_id=(px, py, pc), device_id_type=MESH))
    return copies


def _direct_start(src, land, *, gather, name, after=None):
    def body(src_ref, land_ref, *rest):
        send_sems, recv_sems, token = rest[-5], rest[-4], rest[-1]
        for cp in _direct_copies(src_ref, land_ref, send_sems, recv_sems, gather, False):
            cp.start()
        token[...] = jnp.zeros_like(token)

    afters = () if after is None else (after,)
    return pl.pallas_call(
        body,
        name=name,
        out_shape=(pltpu.SemaphoreType.DMA((N_DEV - 1,)), pltpu.SemaphoreType.DMA((N_DEV - 1,)),
                   pltpu.HBM(src.shape, src.dtype), pltpu.HBM(land.shape, land.dtype),
                   jax.ShapeDtypeStruct((8, 128), F32)),
        in_specs=(HBM, HBM) + tuple(ANY for _ in afters),
        out_specs=(SEM, SEM, HBM, HBM, pl.BlockSpec(memory_space=pltpu.VMEM)),
        input_output_aliases={0: 2, 1: 3},
        compiler_params=pltpu.CompilerParams(has_side_effects=_DATAFLOW),
    )(pltpu.with_memory_space_constraint(src, pltpu.HBM), pltpu.with_memory_space_constraint(land, pltpu.HBM), *afters)


def _direct_wait(send_sems, recv_sems, src_thru, land_thru, after, *, gather, name):
    def body(src_ref, land_ref, send_sems_ref, recv_sems_ref, after_ref, src_dead, got_ref):
        del after_ref, src_dead, got_ref
        for cp in _direct_copies(src_ref, land_ref, send_sems_ref, recv_sems_ref, gather, True):
            cp.wait_send()
            cp.wait_recv()

    return pl.pallas_call(
        body,
        name=name,
        out_shape=(pltpu.HBM(src_thru.shape, src_thru.dtype), pltpu.HBM(land_thru.shape, land_thru.dtype)),
        in_specs=(HBM, HBM, SEM, SEM, ANY),
        out_specs=(HBM, HBM),
        input_output_aliases={0: 0, 1: 1},
        compiler_params=pltpu.CompilerParams(has_side_effects=_DATAFLOW),
    )(src_thru, land_thru, send_sems, recv_sems, after)


def _sum_partials(src, land, me, *, tr, name):
    R = src.shape[1]

    def body(me_ref, s_ref, l_ref, o_ref):
        del me_ref
        acc = s_ref[0].astype(F32)
        for r in range(N_DEV - 1):
            acc = acc + l_ref[r].astype(F32)
        o_ref[...] = acc

    return pl.pallas_call(
        body,
        grid_spec=pltpu.PrefetchScalarGridSpec(
            num_scalar_prefetch=1, grid=(R // tr,),
            in_specs=[pl.BlockSpec((1, tr, 1024), lambda i, mr: (mr[0], i, 0)),
                      pl.BlockSpec((N_DEV - 1, tr, 1024), lambda i, mr: (0, i, 0))],
            out_specs=pl.BlockSpec((tr, 1024), lambda i, mr: (i, 0))),
        out_shape=jax.ShapeDtypeStruct((R, 1024), F32),
        name=name,
    )(me, src, land)


def _small_sum(gathered, lb_logits):
    def body(g_ref, lbl_ref, o_ref):
        tot = g_ref[0]
        for d in range(1, N_DEV):
            tot = tot + g_ref[d]
        o_ref[...] = tot
        lb = _lower_bound(lbl_ref)
        dl0 = o_ref[SM_LB:SM_LB + 1, :] * lb * (1.0 - lb)
        o_ref[SM_LB:SM_LB + 1, :] = dl0
        o_ref[SM_LB + 1:SM_LB + 2, :] = -dl0

    vm = pl.BlockSpec(memory_space=pltpu.VMEM)
    return pl.pallas_call(
        body,
        in_specs=[vm, vm],
        out_specs=vm,
        out_shape=jax.ShapeDtypeStruct(gathered.shape[1:], F32),
        name="small_sum",
    )(gathered, lb_logits)


def _adamw(w, g, m, v, *, tr, name):
    R, C = w.shape

    def body(w_ref, g_ref, m_ref, v_ref, d_ref, nm_ref, nv_ref):
        gv = g_ref[...]
        nm = ADAM_B1 * m_ref[...] + (1.0 - ADAM_B1) * gv
        nv = ADAM_B2 * v_ref[...] + (1.0 - ADAM_B2) * jnp.square(gv)
        m_hat = nm / (1.0 - ADAM_B1 ** ADAM_STEP)
        v_hat = nv / (1.0 - ADAM_B2 ** ADAM_STEP)
        d_ref[...] = -ADAM_LR * (m_hat / (jnp.sqrt(v_hat) + ADAM_EPS) + ADAM_WD * w_ref[...])
        nm_ref[...] = nm
        nv_ref[...] = nv

    spec = pl.BlockSpec((tr, C), lambda i: (i, 0))
    return pl.pallas_call(
        body,
        grid=(R // tr,),
        in_specs=[spec] * 4,
        out_specs=[spec] * 3,
        out_shape=[jax.ShapeDtypeStruct((R, C), F32)] * 3,
        compiler_params=_cparams("parallel"),
        name=name,
    )(w, g, m, v)


def _pack_small(lb, gain, sinks, rel_bias, ln1_g, ln1_b, ln2_g, ln2_b, loss=None):
    pad = lambda a: jnp.pad(a.reshape(1, -1), ((0, 0), (0, D_MODEL - a.size)))
    rows = [lb.reshape(-1, D_MODEL)]
    if rows[0].shape[0] == 1:
        rows.append(jnp.zeros((1, D_MODEL), F32))
    rows += [gain.reshape(1, D_MODEL), pad(sinks), pad(rel_bias), ln1_g.reshape(1, D_MODEL), ln1_b.reshape(1, D_MODEL),
             ln2_g.reshape(1, D_MODEL), ln2_b.reshape(1, D_MODEL),
             pad(jnp.zeros((1,), F32) if loss is None else loss.reshape(1))]
    rows.append(jnp.zeros((SM_ROWS - SM_LOSS - 1, D_MODEL), F32))
    return jnp.concatenate(rows, axis=0)


def _unpack_small(p):
    return dict(
        lb_logits=p[SM_LB:SM_LB + 2], hg_norm_gain=p[SM_GAIN:SM_GAIN + 1], swa_sinks=p[SM_SINK:SM_SINK + 1, :SWA_HEADS],
        rel_bias=p[SM_RB, :NUM_BUCKETS * SWA_HEADS].reshape(NUM_BUCKETS, SWA_HEADS),
        ln1_g=p[SM_L1G:SM_L1G + 1], ln1_b=p[SM_L1B:SM_L1B + 1], ln2_g=p[SM_L2G:SM_L2G + 1], ln2_b=p[SM_L2B:SM_L2B + 1])


_SMALL = ("lb_logits", "hg_norm_gain", "swa_sinks", "rel_bias", "ln1_g", "ln1_b", "ln2_g", "ln2_b")
_WEIGHTS = ("w_in", "lb_logits", "hg_norm_gain", "swa_sinks", "rel_bias", "w_mem_kv", "w_branch_hg", "w_branch_swa",
            "w_branch_mem", "w_out", "ln1_g", "ln1_b", "w_up", "w_down", "ln2_g", "ln2_b")


def kernel(x, mem, w_in, lb_logits, hg_norm_gain, swa_sinks, rel_bias, w_mem_kv, w_branch_hg, w_branch_swa, w_branch_mem, w_out, ln1_g, ln1_b, w_up, w_down, ln2_g, ln2_b, loss_target, m_w_in, m_lb_logits, m_hg_norm_gain, m_swa_sinks, m_rel_bias, m_w_mem_kv, m_w_branch_hg, m_w_branch_swa, m_w_branch_mem, m_w_out, m_ln1_g, m_ln1_b, m_w_up, m_w_down, m_ln2_g, m_ln2_b, v_w_in, v_lb_logits, v_hg_norm_gain, v_swa_sinks, v_rel_bias, v_w_mem_kv, v_w_branch_hg, v_w_branch_swa, v_w_branch_mem, v_w_out, v_ln1_g, v_ln1_b, v_w_up, v_w_down, v_ln2_g, v_ln2_b):
    w = dict(w_in=w_in, lb_logits=lb_logits, hg_norm_gain=hg_norm_gain, swa_sinks=swa_sinks, rel_bias=rel_bias,
             w_mem_kv=w_mem_kv, w_branch_hg=w_branch_hg, w_branch_swa=w_branch_swa, w_branch_mem=w_branch_mem,
             w_out=w_out, ln1_g=ln1_g, ln1_b=ln1_b, w_up=w_up, w_down=w_down, ln2_g=ln2_g, ln2_b=ln2_b)
    mom = dict(w_in=m_w_in, lb_logits=m_lb_logits, hg_norm_gain=m_hg_norm_gain, swa_sinks=m_swa_sinks, rel_bias=m_rel_bias,
               w_mem_kv=m_w_mem_kv, w_branch_hg=m_w_branch_hg, w_branch_swa=m_w_branch_swa, w_branch_mem=m_w_branch_mem,
               w_out=m_w_out, ln1_g=m_ln1_g, ln1_b=m_ln1_b, w_up=m_w_up, w_down=m_w_down, ln2_g=m_ln2_g, ln2_b=m_ln2_b)
    var = dict(w_in=v_w_in, lb_logits=v_lb_logits, hg_norm_gain=v_hg_norm_gain, swa_sinks=v_swa_sinks, rel_bias=v_rel_bias,
               w_mem_kv=v_w_mem_kv, w_branch_hg=v_w_branch_hg, w_branch_swa=v_w_branch_swa, w_branch_mem=v_w_branch_mem,
               w_out=v_w_out, ln1_g=v_ln1_g, ln1_b=v_ln1_b, w_up=v_w_up, w_down=v_w_down, ln2_g=v_ln2_g, ln2_b=v_ln2_b)
    xc, yc, cc = _coords()

    p1 = _bf(w_in[0].T)
    p2 = _bf(jnp.concatenate([w_down[0], w_up[0].T, w_branch_hg[0], w_branch_swa[0], w_branch_mem[0], w_out[0],
                              w_mem_kv[0].T], axis=0))
    me = 4 * xc + 2 * yc + cc
    (g1,) = _all_gather_weights(p1)
    land2 = lax.dynamic_update_slice(lax.empty((N_DEV, R_OTHER, D_MODEL), BF16), p2[None], (me, 0, 0))
    ag2 = _direct_start(p2, land2, gather=True, name="gather_other_weights_start")

    def other_weights(after):
        return _direct_wait(*ag2[:4], after, gather=True, name="gather_other_weights_wait")[1]

    blocks = lambda a: a.reshape(N_DEV, a.shape[0] // N_DEV, D_MODEL)
    started = {}

    def send_other_grads(g):
        part = jnp.concatenate([blocks(g[k]) for k in ("wdn", "wup_t", "wbh", "wbs", "wbm", "wout", "wkv_t")], axis=1)
        started["others"] = _direct_start(part, lax.empty((N_DEV - 1, R_OTHER, D_MODEL), BF16), gather=False,
                                          name="scatter_other_grads_start")
        return started["others"][4]

    me1 = me.reshape(1).astype(jnp.int32)
    grads, delta, new_m, new_v = {}, {}, {}, {}

    def adamw(name):
        w2 = w[name][0]
        delta[name], new_m[name], new_v[name] = _adamw(
            w2, grads[name], mom[name][0], var[name][0], tr=w2.shape[0] // 4, name="adamw_" + name)

    def send_small_grads(small):
        packed = _pack_small(small["d_lb"], small["d_gain"], small["d_sink"], small["d_rb"], small["d_ln1_g"],
                             small["d_ln1_b"], small["d_ln2_g"], small["d_ln2_b"], small["loss"])
        land = lax.dynamic_update_slice(lax.empty((N_DEV, SM_ROWS, D_MODEL), F32), packed[None], (me, 0, 0))
        started["small"] = _direct_start(packed, land, gather=True, name="gather_small_grads_start")
        return started["small"][4]

    def send_win_grad(g, after):
        started["win"] = _direct_start(blocks(g), lax.empty((N_DEV - 1, IN_SHARD, D_MODEL), BF16), gather=False,
                                       name="scatter_w_in_grad_start", after=after)
        mine2, landed2 = _direct_wait(*started["others"][:4], started["win"][4], gather=False,
                                      name="scatter_other_grads_wait")
        gs2 = _sum_partials(mine2, landed2, me1, tr=R_OTHER // 2, name="sum_other_grads")
        grads.update(
            w_down=gs2[R_DN:R_UP], w_up=gs2[R_UP:R_BH].T, w_branch_hg=gs2[R_BH:R_BS], w_branch_swa=gs2[R_BS:R_BM],
            w_branch_mem=gs2[R_BM:R_OUT], w_out=gs2[R_OUT:R_KV], w_mem_kv=gs2[R_KV:R_OTHER].T)
        for name in ("w_mem_kv", "w_branch_hg", "w_branch_swa", "w_branch_mem", "w_out", "w_up", "w_down"):
            adamw(name)
        return new_v["w_down"]

    grad_x = _local_step(
        x[0], mem[0], loss_target[0], lb_logits, hg_norm_gain, swa_sinks, rel_bias, ln1_g, ln1_b, ln2_g, ln2_b,
        g1.reshape(IN_COLS, D_MODEL), ag2[4], other_weights, send_other_grads, send_small_grads, send_win_grad)

    mine1, landed1 = _direct_wait(*started["win"][:4], grad_x, gather=False, name="scatter_w_in_grad_wait")
    grads["w_in"] = _sum_partials(mine1, landed1, me1, tr=IN_SHARD // 2, name="sum_w_in_grad").T
    adamw("w_in")

    _, gathered = _direct_wait(*started["small"][:4], grad_x, gather=True, name="gather_small_grads_wait")
    reduced = _small_sum(gathered, lb_logits)
    loss = reduced[SM_LOSS, 0]
    grads.update(_unpack_small(reduced))

    sm = lambda d: _pack_small(*[d[k] for k in _SMALL])
    d_s, m_s, v_s = _adamw(sm(w), reduced, sm(mom), sm(var), tr=SM_ROWS, name="adamw_small")
    for dst, src in ((delta, d_s), (new_m, m_s), (new_v, v_s)):
        dst.update(_unpack_small(src))

    def shaped(d, name):
        return d[name].reshape(w[name].shape)

    return (loss, grad_x[None], *[shaped(grads, n) for n in _WEIGHTS], *[shaped(delta, n) for n in _WEIGHTS],
            *[shaped(new_m, n) for n in _WEIGHTS], *[shaped(new_v, n) for n in _WEIGHTS])
```

```python
import functools
import math

import jax
import jax.numpy as jnp
from jax import lax
from jax.experimental import pallas as pl
from jax.experimental.pallas import tpu as pltpu

F32 = jnp.float32
BF16 = jnp.bfloat16

D_MODEL = 1024
MEM_LEN = 256
HG_HEADS = 8
HG_DK = 128
HG_CHUNK = 64
SWA_HEADS = 16
SWA_HEAD_DIM = 64
SWA_BLOCK = 128
SWA_WINDOW = 128
MEM_HEADS = 4
MEM_HEAD_DIM = 256
NUM_BUCKETS = 32
MAX_DISTANCE = 128
D_FF = 4096
LN_EPS = 1e-5
RMS_EPS = 1e-6
ALPHA = 2.0 ** 0.25
N_DEV = 8

C_HQ, C_HF, C_HI, C_HG, C_SQ, C_SK, C_SV, C_MQ, C_GL = 0, 1024, 2048, 3072, 4096, 5120, 5248, 5376, 6400
IN_COLS = 9472
IN_SHARD = IN_COLS // N_DEV

ADAM_LR = 0.001
ADAM_B1 = 0.9
ADAM_B2 = 0.999
ADAM_EPS = 1e-08
ADAM_WD = 0.01
ADAM_STEP = 10

VMEM_LIMIT = 58 * 1024 * 1024

R_DN, R_UP, R_BH, R_BS, R_BM, R_OUT, R_KV, R_OTHER = 0, 512, 1024, 1152, 1280, 1408, 1536, 1792

SM_LB, SM_GAIN, SM_SINK, SM_L1G, SM_L1B, SM_L2G, SM_L2B, SM_LOSS, SM_RB, SM_ROWS = 0, 2, 3, 4, 5, 6, 7, 8, 16, 48


def _bf(v):
    return v.astype(BF16)


def _f32(v):
    return v.astype(F32)


def _dot(a, b):
    return jnp.dot(a, b, preferred_element_type=F32)


def _dot_nt(a, b):
    return lax.dot_general(a, b, (((1,), (1,)), ((), ())), preferred_element_type=F32)


def _dot_tn(a, b):
    return lax.dot_general(a, b, (((0,), (0,)), ((), ())), preferred_element_type=F32)


def _sig(v):
    return 1.0 / (1.0 + jnp.exp(-v))


def _cparams(*sem):
    return pltpu.CompilerParams(dimension_semantics=sem, vmem_limit_bytes=VMEM_LIMIT)


def _const_spec(shape):
    nd = len(shape)
    return pl.BlockSpec(shape, lambda *_: (0,) * nd, pipeline_mode=pl.Buffered(1))


def _dep_spec():
    return pl.BlockSpec((8, 128), lambda *_: (0, 0))


def _mm_nt(a, bt, *, tm, tn, out_dtype, name, dep=None, n_cols=None, also_a_bf16=False):
    M, K = a.shape
    N = bt.shape[0] if n_cols is None else n_cols
    deps = () if dep is None else (dep,)
    nd = len(deps)
    assert not also_a_bf16 or tn == N

    def body(a_ref, b_ref, *rest):
        ab = _bf(a_ref[...])
        o_ref = rest[nd]
        o_ref[...] = _dot_nt(ab, _bf(b_ref[...])).astype(o_ref.dtype)
        if also_a_bf16:
            rest[nd + 1][...] = ab

    out_specs = [pl.BlockSpec((tm, tn), lambda j, i: (i, j))]
    out_shape = [jax.ShapeDtypeStruct((M, N), out_dtype)]
    if also_a_bf16:
        out_specs.append(pl.BlockSpec((tm, K), lambda j, i: (i, 0)))
        out_shape.append(jax.ShapeDtypeStruct((M, K), BF16))
    res = pl.pallas_call(
        body,
        grid=(N // tn, M // tm),
        in_specs=[pl.BlockSpec((tm, K), lambda j, i: (i, 0)), pl.BlockSpec((tn, K), lambda j, i: (j, 0))]
        + [_dep_spec() for _ in deps],
        out_specs=out_specs,
        out_shape=out_shape,
        compiler_params=_cparams("parallel", "parallel"),
        name=name,
    )(a, bt, *deps)
    return res if also_a_bf16 else res[0]


def _mm_tn_resident(a, b, *, tm, kc, name, out_dtype):
    K, M = a.shape
    N = b.shape[1]
    nk = K // kc

    def body(a_ref, b_ref, o_ref):
        acc = jnp.zeros((tm, N), F32)
        for kk in range(nk):
            sl = pl.ds(kk * kc, kc)
            acc = acc + _dot_tn(_bf(a_ref[sl, :]), _bf(b_ref[sl, :]))
        o_ref[...] = acc.astype(o_ref.dtype)

    return pl.pallas_call(
        body,
        grid=(M // tm,),
        in_specs=[pl.BlockSpec((K, tm), lambda i: (0, i)), _const_spec((K, N))],
        out_specs=pl.BlockSpec((tm, N), lambda i: (i, 0)),
        out_shape=jax.ShapeDtypeStruct((M, N), out_dtype),
        compiler_params=_cparams("parallel"),
        name=name,
    )(a, b)


def _mm_tn(a, b, *, kc, name, out_dtype=F32):
    K, M = a.shape
    N = b.shape[1]
    if M > 1024:
        return _mm_tn_resident(a, b, tm=256, kc=min(kc, 1024), name=name, out_dtype=out_dtype)
    tm = M
    nk = K // kc

    def body(a_ref, b_ref, o_ref, acc):
        k = pl.program_id(1)
        part = _dot_tn(_bf(a_ref[...]), _bf(b_ref[...]))

        @pl.when(k == 0)
        def _():
            acc[...] = part

        @pl.when(k > 0)
        def _():
            acc[...] += part

        @pl.when(k == nk - 1)
        def _():
            o_ref[...] = acc[...].astype(o_ref.dtype)

    return pl.pallas_call(
        body,
        grid=(M // tm, nk),
        in_specs=[pl.BlockSpec((kc, tm), lambda i, k: (k, i)), pl.BlockSpec((kc, N), lambda i, k: (k, 0))],
        out_specs=pl.BlockSpec((tm, N), lambda i, k: (i, 0)),
        out_shape=jax.ShapeDtypeStruct((M, N), out_dtype),
        scratch_shapes=[pltpu.VMEM((tm, N), F32)],
        compiler_params=_cparams("parallel", "arbitrary"),
        name=name,
    )(a, b)


def _grad_x(d_qfv, d_hg_gl, d_sq, d_skv, d_mq, w_qfv, win_t, add, deps, *, tm):
    M = add.shape[0]
    pieces = (d_qfv, d_hg_gl, d_sq, d_skv, d_mq)

    def body(qfv_ref, hggl_ref, sq_ref, skv_ref, mq_ref, wq_ref, w_ref, add_ref, *rest):
        o_ref = rest[-1]
        acc = add_ref[...] + _dot(qfv_ref[...], wq_ref[...])
        acc = acc + _dot(hggl_ref[:, 0:1024], w_ref[C_HG:C_SQ, :])
        acc = acc + _dot(hggl_ref[:, 1024:4096], w_ref[C_GL:IN_COLS, :])
        acc = acc + _dot(sq_ref[...], w_ref[C_SQ:C_SK, :])
        acc = acc + _dot(skv_ref[...], w_ref[C_SK:C_MQ, :])
        o_ref[...] = acc + _dot(mq_ref[...], w_ref[C_MQ:C_GL, :])

    return pl.pallas_call(
        body,
        grid=(M // tm,),
        in_specs=[pl.BlockSpec((tm, p.shape[1]), lambda i: (i, 0)) for p in pieces]
        + [_const_spec(w_qfv.shape), _const_spec(win_t.shape), pl.BlockSpec((tm, D_MODEL), lambda i: (i, 0))]
        + [_dep_spec() for _ in deps],
        out_specs=pl.BlockSpec((tm, D_MODEL), lambda i: (i, 0)),
        out_shape=jax.ShapeDtypeStruct((M, D_MODEL), F32),
        compiler_params=_cparams("parallel"),
        name="grad_x",
    )(*pieces, w_qfv, win_t, add, *deps)


def _lower_bound(lbl_ref):
    l0 = lbl_ref[0:1, :]
    l1 = lbl_ref[1:2, :]
    mx = jnp.maximum(l0, l1)
    e0 = jnp.exp(l0 - mx)
    e1 = jnp.exp(l1 - mx)
    return e0 / (e0 + e1)


def _tri(lower):
    r = lax.broadcasted_iota(jnp.int32, (HG_CHUNK, HG_CHUNK), 0)
    c = lax.broadcasted_iota(jnp.int32, (HG_CHUNK, HG_CHUNK), 1)
    return (r >= c) if lower else (r <= c)


def _hg_gates(fl, lb):
    sg = _sig(fl)
    f = lb + (1.0 - lb) * sg
    return sg, f, jnp.log(f), 1.0 - f


def _scan_rows(v, reverse=False):
    row = lax.broadcasted_iota(jnp.int32, v.shape, 0)
    s = 1
    while s < HG_CHUNK:
        if reverse:
            v = v + jnp.where(row < HG_CHUNK - s, pltpu.roll(v, HG_CHUNK - s, 0), 0.0)
        else:
            v = v + jnp.where(row >= s, pltpu.roll(v, s, 0), 0.0)
        s *= 2
    return v


def _hgrn_fwd(zmain, lb_logits, *, T):
    S = zmain.shape[0]
    nc = T // HG_CHUNK

    def body(q_ref, f_ref, v_ref, lbl_ref, o_ref, st_ref, state):
        @pl.when(pl.program_id(1) == 0)
        def _():
            state[...] = jnp.zeros_like(state)

        lb = _lower_bound(lbl_ref)
        tril = _tri(True)
        qis, updates, decays, intra = [], [], [], []
        for c in range(nc):
            sl = pl.ds(c * HG_CHUNK, HG_CHUNK)
            _, _, g, k = _hg_gates(_f32(f_ref[sl, :]), lb)
            b = _scan_rows(g)
            bl = jnp.sum(g, axis=0, keepdims=True)
            qi = _bf(_f32(q_ref[sl, :]) * jnp.exp(b))
            ki = _bf(k * jnp.exp(-b))
            ko = _bf(k * jnp.exp(bl - b))
            vb = _bf(v_ref[sl, :])
            att = jnp.where(tril, _dot_nt(qi, ki), 0.0)
            intra.append(_dot(_bf(att), vb))
            qis.append(qi)
            updates.append(_dot_tn(vb, ko))
            decays.append(jnp.exp(bl))
        st = state[...]
        for c in range(nc):
            st_ref[0, c] = st
            o_ref[pl.ds(c * HG_CHUNK, HG_CHUNK), :] = intra[c] + _dot_nt(qis[c], _bf(st))
            st = st * decays[c] + updates[c]
        state[...] = st

    col = lambda base: pl.BlockSpec((T, HG_DK), lambda h, t: (t, base + h))
    return pl.pallas_call(
        body,
        grid=(HG_HEADS, S // T),
        in_specs=[col(0), col(8), col(16), pl.BlockSpec((2, HG_DK), lambda h, t: (0, h))],
        out_specs=[
            pl.BlockSpec((T, HG_DK), lambda h, t: (t, h)),
            pl.BlockSpec((1, nc, HG_DK, HG_DK), lambda h, t: (h, t, 0, 0)),
        ],
        out_shape=[
            jax.ShapeDtypeStruct((S, D_MODEL), F32),
            jax.ShapeDtypeStruct((HG_HEADS, S // HG_CHUNK, HG_DK, HG_DK), F32),
        ],
        scratch_shapes=[pltpu.VMEM((HG_DK, HG_DK), F32)],
        compiler_params=_cparams("parallel", "arbitrary"),
        name="hgrn_fwd",
    )(zmain, zmain, zmain, lb_logits)


def _hgrn_bwd(zmain, lb_logits, states, d_o, *, T):
    S = zmain.shape[0]
    nc = T // HG_CHUNK
    nt = S // T

    def body(q_ref, f_ref, v_ref, lbl_ref, st_ref, do_ref, dz_ref, dlb_ref, dstate):
        @pl.when(pl.program_id(1) == 0)
        def _():
            dstate[...] = jnp.zeros_like(dstate)
            dlb_ref[...] = jnp.zeros_like(dlb_ref)

        lb = _lower_bound(lbl_ref)
        tril = _tri(True)
        last_row = lax.broadcasted_iota(jnp.int32, (HG_CHUNK, HG_DK), 0) == HG_CHUNK - 1
        saved = []
        for c in range(nc):
            sl = pl.ds(c * HG_CHUNK, HG_CHUNK)
            sg, f, g, k = _hg_gates(_f32(f_ref[sl, :]), lb)
            b = _scan_rows(g)
            bl = jnp.sum(g, axis=0, keepdims=True)
            eb = jnp.exp(b)
            enb = jnp.exp(-b)
            eo = jnp.exp(bl - b)
            q_in = _f32(q_ref[sl, :]) * eb
            k_in = k * enb
            k_out = k * eo
            qi, ki, ko = _bf(q_in), _bf(k_in), _bf(k_out)
            vb = _bf(v_ref[sl, :])
            dob = do_ref[sl, :]
            att = jnp.where(tril, _dot_nt(qi, ki), 0.0)
            d_att = _bf(jnp.where(tril, _dot_nt(dob, vb), 0.0))
            d_kin = _dot_tn(d_att, qi)
            saved.append(dict(
                sg=sg, f=f, eb=eb, enb=enb, eo=eo, ebl=jnp.exp(bl), k_out=k_out, ko=ko, vb=vb, dob=dob,
                d_v=_dot_tn(_bf(att), dob), d_qin=_dot(d_att, ki), d_kin=d_kin,
                qk=(q_in, k_in), d_state=_dot_tn(dob, qi)))
        dst = dstate[...]
        dsts = [None] * nc
        for c in reversed(range(nc)):
            dsts[c] = dst
            dst = dst * saved[c]["ebl"] + saved[c]["d_state"]
        dstate[...] = dst
        dlb = jnp.zeros((1, HG_DK), F32)
        for c in range(nc):
            sl = pl.ds(c * HG_CHUNK, HG_CHUNK)
            s = saved[c]
            q_in, k_in = s["qk"]
            st = st_ref[0, c]
            dstb = _bf(dsts[c])
            d_v = s["d_v"] + _dot_nt(s["ko"], dstb)
            d_qin = s["d_qin"] + _dot(s["dob"], _bf(st))
            d_kout = _dot(s["vb"], dstb)
            d_decay = jnp.sum(dsts[c] * st, axis=0, keepdims=True)
            kk = d_kout * s["k_out"]
            d_b = d_qin * q_in - s["d_kin"] * k_in - kk
            d_bl = jnp.sum(kk, axis=0, keepdims=True) + d_decay * s["ebl"]
            d_g = _scan_rows(d_b + jnp.where(last_row, d_bl, 0.0), reverse=True)
            d_f = d_g / s["f"] - (s["d_kin"] * s["enb"] + d_kout * s["eo"])
            dz_ref[sl, 0:HG_DK] = _bf(d_qin * s["eb"])
            dz_ref[sl, HG_DK:2 * HG_DK] = _bf(d_f * (1.0 - lb) * s["sg"] * (1.0 - s["sg"]))
            dz_ref[sl, 2 * HG_DK:3 * HG_DK] = _bf(d_v)
            dlb = dlb + jnp.sum(d_f * (1.0 - s["sg"]), axis=0, keepdims=True)
        dlb_ref[...] += dlb

    rev = lambda base: pl.BlockSpec((T, HG_DK), lambda h, t: (nt - 1 - t, base + h))
    outc = pl.BlockSpec((T, HG_DK), lambda h, t: (nt - 1 - t, h))
    return pl.pallas_call(
        body,
        grid=(HG_HEADS, nt),
        in_specs=[
            rev(0), rev(8), rev(16),
            pl.BlockSpec((2, HG_DK), lambda h, t: (0, h)),
            pl.BlockSpec((1, nc, HG_DK, HG_DK), lambda h, t: (h, nt - 1 - t, 0, 0)),
            outc,
        ],
        out_specs=[pl.BlockSpec((T, 3 * HG_DK), lambda h, t: (nt - 1 - t, h)),
                   pl.BlockSpec((1, HG_DK), lambda h, t: (0, h))],
        out_shape=[jax.ShapeDtypeStruct((S, 3 * D_MODEL), BF16), jax.ShapeDtypeStruct((1, D_MODEL), F32)],
        scratch_shapes=[pltpu.VMEM((HG_DK, HG_DK), F32)],
        compiler_params=_cparams("parallel", "arbitrary"),
        name="hgrn_bwd",
    )(zmain, zmain, zmain, lb_logits, states, d_o)


def _t5_bucket_table():
    qi = jnp.arange(SWA_BLOCK)[:, None] + SWA_BLOCK
    kj = jnp.arange(2 * SWA_BLOCK)[None, :]
    n = jnp.clip(qi - kj, 0, SWA_WINDOW - 1)
    max_exact = NUM_BUCKETS // 2
    nf = jnp.maximum(n, 1).astype(F32)
    large = max_exact + (jnp.log(nf / max_exact) / math.log(MAX_DISTANCE / max_exact)
                         * (NUM_BUCKETS - max_exact)).astype(jnp.int32)
    large = jnp.minimum(large, NUM_BUCKETS - 1)
    return jnp.where(n < max_exact, n, large).astype(jnp.int32)


SWA_ROWS = 32


def _swa_bias_init(bias, bucket_ref, rb_ref):
    bk = bucket_ref[...]
    qi = lax.broadcasted_iota(jnp.int32, bk.shape, 0) + SWA_BLOCK
    kj = lax.broadcasted_iota(jnp.int32, bk.shape, 1)
    band = (qi - kj >= 0) & (qi - kj < SWA_WINDOW)
    for h in range(SWA_HEADS):
        def sel(b, acc, h=h):
            return jnp.where(bk == b, rb_ref[b, h], acc)
        t = lax.fori_loop(0, NUM_BUCKETS, sel, jnp.zeros(bk.shape, F32))
        bias[1, h] = jnp.where(band, t, -jnp.inf)
        bias[0, h] = jnp.where(band & (kj >= SWA_BLOCK), t, -jnp.inf)


def _lane_halves(t, kv_head):
    lane = lax.broadcasted_iota(jnp.int32, t.shape, 1)
    rolled = pltpu.roll(t, 64, 1)
    zero = jnp.zeros_like(t)
    if kv_head == 0:
        return jnp.where(lane < 64, t, zero), jnp.where(lane >= 64, rolled, zero)
    return jnp.where(lane < 64, rolled, zero), jnp.where(lane >= 64, t, zero)


def _swa_zero_key0(t):
    return jnp.where(lax.broadcasted_iota(jnp.int32, t.shape, 0) == 0, jnp.zeros_like(t), t)


def _swa_probs(s, masked_bias, sink):
    s = s + masked_bias
    m = jnp.maximum(jnp.max(s, axis=-1, keepdims=True), sink)
    p = jnp.exp(s - m)
    es = jnp.exp(sink - m)
    inv = 1.0 / (jnp.sum(p, axis=-1, keepdims=True) + es)
    return p * inv, es * inv


def _swa_fwd(zmain, bucket, rel_bias, sinks):
    S = zmain.shape[0]
    nb = S // SWA_BLOCK
    scale = SWA_HEAD_DIM ** -0.5

    def body(q_ref, kvc_ref, kvp_ref, bucket_ref, rb_ref, sk_ref, o_ref, p_ref, bias):
        n = pl.program_id(0)

        @pl.when(n == 0)
        def _():
            _swa_bias_init(bias, bucket_ref, rb_ref)

        later = jnp.minimum(n, 1)
        kk = _bf(jnp.concatenate([kvp_ref[:, 0:128], kvc_ref[:, 0:128]], axis=0))
        vv = _swa_zero_key0(_bf(jnp.concatenate([kvp_ref[:, 128:256], kvc_ref[:, 128:256]], axis=0)))
        first_col = lax.broadcasted_iota(jnp.int32, (SWA_ROWS, 2 * SWA_BLOCK), 1) == 0
        for kvh in range(2):
            ka, kb = _lane_halves(kk, kvh)
            va, vb = _lane_halves(vv, kvh)
            qst = _bf(jnp.concatenate([q_ref[:, pl.ds((kvh * 4 + jj) * 128, 128)] for jj in range(4)], axis=0) * scale)
            probs = []
            for odd, kx in enumerate((ka, kb)):
                s = _dot_nt(qst, kx)
                parts = []
                for jj in range(4):
                    h = 2 * (kvh * 4 + jj) + odd
                    for r0 in range(0, SWA_BLOCK, SWA_ROWS):
                        p, ps = _swa_probs(s[jj * SWA_BLOCK + r0:jj * SWA_BLOCK + r0 + SWA_ROWS],
                                           bias[later, h, pl.ds(r0, SWA_ROWS), :], sk_ref[0, h])
                        part = _bf(jnp.where(first_col, ps, p))
                        p_ref[pl.ds(r0, SWA_ROWS), pl.ds(h * 2 * SWA_BLOCK, 2 * SWA_BLOCK)] = part
                        parts.append(part)
                probs.append(jnp.concatenate(parts, axis=0))
            ost = _dot(probs[0], va) + _dot(probs[1], vb)
            for jj in range(4):
                o_ref[:, pl.ds((kvh * 4 + jj) * 128, 128)] = ost[jj * SWA_BLOCK:(jj + 1) * SWA_BLOCK]

    smem = pl.BlockSpec(memory_space=pltpu.SMEM)
    return pl.pallas_call(
        body,
        grid=(nb,),
        in_specs=[
            pl.BlockSpec((SWA_BLOCK, 1024), lambda n: (n, C_SQ // 1024)),
            pl.BlockSpec((SWA_BLOCK, 256), lambda n: (n, C_SK // 256)),
            pl.BlockSpec((SWA_BLOCK, 256), lambda n: (jnp.maximum(n - 1, 0), C_SK // 256)),
            _const_spec((SWA_BLOCK, 2 * SWA_BLOCK)), smem, smem,
        ],
        out_specs=[pl.BlockSpec((SWA_BLOCK, 1024), lambda n: (n, 0)),
                   pl.BlockSpec((SWA_BLOCK, SWA_HEADS * 2 * SWA_BLOCK), lambda n: (n, 0))],
        out_shape=[jax.ShapeDtypeStruct((S, 1024), F32),
                   jax.ShapeDtypeStruct((S, SWA_HEADS * 2 * SWA_BLOCK), BF16)],
        scratch_shapes=[pltpu.VMEM((2, SWA_HEADS, SWA_BLOCK, 2 * SWA_BLOCK), F32)],
        compiler_params=_cparams("arbitrary"),
        name="swa_fwd",
    )(zmain, zmain, zmain, bucket, rel_bias, sinks)


def _swa_bwd(zmain, o_b, probs, d_o, bucket, dep):
    S = zmain.shape[0]
    nb = S // SWA_BLOCK
    scale = SWA_HEAD_DIM ** -0.5

    def body(q_ref, kvc_ref, kvp_ref, o_ref, p_ref, do_ref, bucket_ref, dep_ref,
             dq_ref, dkv_ref, drb_ref, dsk_ref, dbias, carry):
        del dep_ref
        n = pl.program_id(0)

        @pl.when(n == 0)
        def _():
            dbias[...] = jnp.zeros_like(dbias)
            carry[...] = jnp.zeros_like(carry)

        @pl.when(n < nb)
        def _():
            kk = _swa_zero_key0(_bf(jnp.concatenate([kvp_ref[:, 0:128], kvc_ref[:, 0:128]], axis=0)))
            vv = _swa_zero_key0(_bf(jnp.concatenate([kvp_ref[:, 128:256], kvc_ref[:, 128:256]], axis=0)))
            lane = lax.broadcasted_iota(jnp.int32, (2 * SWA_BLOCK, 128), 1)
            lane_q = lax.broadcasted_iota(jnp.int32, (4 * SWA_BLOCK, 128), 1)
            dk_parts, dv_parts = [], []
            for kvh in range(2):
                ka, kb = _lane_halves(kk, kvh)
                va, vb = _lane_halves(vv, kvh)
                pair_cols = [pl.ds((kvh * 4 + jj) * 128, 128) for jj in range(4)]
                qst = _bf(jnp.concatenate([q_ref[:, cl] for cl in pair_cols], axis=0) * scale)
                dost = jnp.concatenate([do_ref[:, cl] for cl in pair_cols], axis=0)
                prod = dost.astype(F32) * jnp.concatenate([o_ref[:, cl] for cl in pair_cols], axis=0)
                dq_st = jnp.zeros((4 * SWA_BLOCK, 128), F32)
                zks, zvs = [], []
                for odd, (kx, vx) in enumerate(((ka, va), (kb, vb))):
                    keep = (lane_q >= 64) if odd else (lane_q < 64)
                    delta = jnp.sum(jnp.where(keep, prod, 0.0), axis=-1, keepdims=True)
                    dp = _dot_nt(dost, vx)
                    p_parts, ds_parts = [], []
                    for jj in range(4):
                        h = 2 * (kvh * 4 + jj) + odd
                        rows = slice(jj * SWA_BLOCK, (jj + 1) * SWA_BLOCK)
                        p = p_ref[:, pl.ds(h * 2 * SWA_BLOCK, 2 * SWA_BLOCK)]
                        ds = _f32(p) * (dp[rows] - delta[rows])
                        dbias[h] += ds
                        p_parts.append(p)
                        ds_parts.append(_bf(ds))
                    pst = jnp.concatenate(p_parts, axis=0)
                    dsst = jnp.concatenate(ds_parts, axis=0)
                    dq_st = dq_st + _dot(dsst, kx)
                    zks.append(_dot_tn(dsst, qst))
                    zvs.append(_dot_tn(pst, dost))
                for jj in range(4):
                    dq_ref[:, pair_cols[jj]] = _bf(dq_st[jj * SWA_BLOCK:(jj + 1) * SWA_BLOCK] * scale)
                zk = jnp.where(lane < 64, zks[0], zks[1])
                zv = jnp.where(lane < 64, zvs[0], zvs[1])
                dk_parts.append(zk + pltpu.roll(zk, 64, 1))
                dv_parts.append(zv + pltpu.roll(zv, 64, 1))
            dk = jnp.where(lane < 64, dk_parts[0], dk_parts[1])
            dv = jnp.where(lane < 64, dv_parts[0], dv_parts[1])
            dkv = _swa_zero_key0(jnp.concatenate([dk, dv], axis=1))
            dkv_ref[...] = _bf(carry[...] + dkv[0:SWA_BLOCK])
            carry[...] = dkv[SWA_BLOCK:]

        @pl.when(n == nb)
        def _():
            dkv_ref[...] = _bf(carry[...])
            first_col = lax.broadcasted_iota(jnp.int32, (SWA_BLOCK, 2 * SWA_BLOCK), 1) == 0
            bk = jnp.where(first_col, -1, bucket_ref[...])

            row = lax.broadcasted_iota(jnp.int32, (NUM_BUCKETS, 128), 0)
            lane = lax.broadcasted_iota(jnp.int32, (NUM_BUCKETS, 128), 1)

            def total(v):
                return jnp.sum(jnp.sum(v, axis=1, keepdims=True), axis=0, keepdims=True)

            def per_head(h, acc):
                db = dbias[h]
                d_rb, d_sk = acc
                d_sk = d_sk + jnp.where((row == 0) & (lane == h), total(jnp.where(first_col, db, 0.0)), 0.0)

                def per_bucket(b, d_rb):
                    return d_rb + jnp.where((row == b) & (lane == h), total(jnp.where(bk == b, db, 0.0)), 0.0)

                return lax.fori_loop(0, NUM_BUCKETS, per_bucket, d_rb), d_sk

            zero = jnp.zeros((NUM_BUCKETS, 128), F32)
            d_rb, d_sk = lax.fori_loop(0, SWA_HEADS, per_head, (zero, zero))
            drb_ref[...] = d_rb
            dsk_ref[...] = d_sk[0:8]

    cur = lambda n: jnp.minimum(n, nb - 1)
    prev = lambda n: jnp.maximum(jnp.minimum(n, nb - 1) - 1, 0)
    return pl.pallas_call(
        body,
        grid=(nb + 1,),
        in_specs=[
            pl.BlockSpec((SWA_BLOCK, 1024), lambda n: (cur(n), C_SQ // 1024)),
            pl.BlockSpec((SWA_BLOCK, 256), lambda n: (cur(n), C_SK // 256)),
            pl.BlockSpec((SWA_BLOCK, 256), lambda n: (prev(n), C_SK // 256)),
            pl.BlockSpec((SWA_BLOCK, 1024), lambda n: (cur(n), 0)),
            pl.BlockSpec((SWA_BLOCK, SWA_HEADS * 2 * SWA_BLOCK), lambda n: (cur(n), 0)),
            pl.BlockSpec((SWA_BLOCK, 1024), lambda n: (cur(n), 0)),
            _const_spec((SWA_BLOCK, 2 * SWA_BLOCK)), _dep_spec(),
        ],
        out_specs=[
            pl.BlockSpec((SWA_BLOCK, 1024), lambda n: (cur(n), 0)),
            pl.BlockSpec((SWA_BLOCK, 256), lambda n: (jnp.maximum(n - 1, 0), 0)),
            pl.BlockSpec((NUM_BUCKETS, 128), lambda n: (0, 0)),
            pl.BlockSpec((8, 128), lambda n: (0, 0)),
        ],
        out_shape=[
            jax.ShapeDtypeStruct((S, 1024), BF16),
            jax.ShapeDtypeStruct((S, 256), BF16),
            jax.ShapeDtypeStruct((NUM_BUCKETS, 128), F32),
            jax.ShapeDtypeStruct((8, 128), F32),
        ],
        scratch_shapes=[
            pltpu.VMEM((SWA_HEADS, SWA_BLOCK, 2 * SWA_BLOCK), F32),
            pltpu.VMEM((SWA_BLOCK, 256), F32),
        ],
        compiler_params=_cparams("arbitrary"),
        name="swa_bwd",
    )(zmain, zmain, zmain, o_b, probs, d_o, bucket, dep)


def _mem_probs(q_ref, k):
    qs = _bf(q_ref[...] * (MEM_HEAD_DIM ** -0.5))
    s = _dot_nt(qs, k)
    e = jnp.exp(s - jnp.max(s, axis=-1, keepdims=True))
    return qs, e / jnp.sum(e, axis=-1, keepdims=True)


def _mem_q_specs(T):
    return [pl.BlockSpec((T, MEM_HEAD_DIM), lambda t, h=h: (t, C_MQ // MEM_HEAD_DIM + h)) for h in range(MEM_HEADS)]


def _mem_kv_proj(mem, g2):
    def body(mem_ref, w_ref, o_ref):
        o_ref[...] = _dot_nt(_bf(mem_ref[...]), _rows(w_ref))

    return pl.pallas_call(
        body,
        grid=(1,),
        in_specs=[pl.BlockSpec((MEM_LEN, D_MODEL), lambda i: (0, 0)), _gathered_spec(R_KV, R_OTHER)],
        out_specs=pl.BlockSpec((MEM_LEN, 2048), lambda i: (0, 0)),
        out_shape=jax.ShapeDtypeStruct((MEM_LEN, 2048), F32),
        compiler_params=_cparams("arbitrary"),
        name="mem_kv_proj",
    )(mem, g2)


def _mem_fwd(zmain, mkv, *, T):
    S = zmain.shape[0]

    def body(q0, q1, q2, q3, kv_ref, o_ref):
        for h, q_ref in enumerate((q0, q1, q2, q3)):
            cols = pl.ds(h * MEM_HEAD_DIM, MEM_HEAD_DIM)
            _, p = _mem_probs(q_ref, _bf(kv_ref[:, cols]))
            o_ref[:, cols] = _dot(_bf(p), _bf(kv_ref[:, pl.ds(1024 + h * MEM_HEAD_DIM, MEM_HEAD_DIM)]))

    return pl.pallas_call(
        body,
        grid=(S // T,),
        in_specs=_mem_q_specs(T) + [_const_spec((MEM_LEN, 2048))],
        out_specs=pl.BlockSpec((T, 1024), lambda t: (t, 0)),
        out_shape=jax.ShapeDtypeStruct((S, 1024), F32),
        compiler_params=_cparams("parallel"),
        name="mem_fwd",
    )(zmain, zmain, zmain, zmain, mkv)


def _mem_bwd(zmain, mkv, o_c, d_o, *, T):
    S = zmain.shape[0]
    scale = MEM_HEAD_DIM ** -0.5

    def body(q0, q1, q2, q3, kv_ref, o_ref, do_ref, dq_ref, dkv_ref):
        @pl.when(pl.program_id(0) == 0)
        def _():
            dkv_ref[...] = jnp.zeros_like(dkv_ref)

        for h, q_ref in enumerate((q0, q1, q2, q3)):
            cols = pl.ds(h * MEM_HEAD_DIM, MEM_HEAD_DIM)
            vcols = pl.ds(1024 + h * MEM_HEAD_DIM, MEM_HEAD_DIM)
            kb = _bf(kv_ref[:, cols])
            qs, p = _mem_probs(q_ref, kb)
            dob = do_ref[:, cols]
            delta = jnp.sum(dob.astype(F32) * o_ref[:, cols], axis=-1, keepdims=True)
            ds = _bf(p * (_dot_nt(dob, _bf(kv_ref[:, vcols])) - delta))
            dq_ref[:, cols] = _bf(_dot(ds, kb) * scale)
            dkv_ref[:, cols] += _dot_tn(ds, qs)
            dkv_ref[:, vcols] += _dot_tn(_bf(p), dob)

    row = pl.BlockSpec((T, 1024), lambda t: (t, 0))
    return pl.pallas_call(
        body,
        grid=(S // T,),
        in_specs=_mem_q_specs(T) + [_const_spec((MEM_LEN, 2048)), row, row],
        out_specs=[row, pl.BlockSpec((MEM_LEN, 2048), lambda t: (0, 0))],
        out_shape=[jax.ShapeDtypeStruct((S, 1024), BF16), jax.ShapeDtypeStruct((MEM_LEN, 2048), F32)],
        compiler_params=_cparams("arbitrary"),
        name="mem_bwd",
    )(zmain, zmain, zmain, zmain, mkv, o_c, d_o)


def _layer_norm(u):
    mu = jnp.mean(u, axis=-1, keepdims=True)
    xc = u - mu
    rstd = lax.rsqrt(jnp.mean(xc * xc, axis=-1, keepdims=True) + LN_EPS)
    return xc * rstd, rstd


def _layer_norm_bwd(dy, gamma, xhat, rstd):
    dxh = dy * gamma
    return rstd * (dxh - jnp.mean(dxh, axis=-1, keepdims=True) - xhat * jnp.mean(dxh * xhat, axis=-1, keepdims=True))


def _merge_forward(oraw_ref, hg_ref, ob_ref, oc_ref, gl_ref, x_ref, gain_ref, wbh, wbs, wbm, wout):
    ys, rs = [], []
    for h in range(HG_HEADS):
        oh = oraw_ref[:, pl.ds(h * HG_DK, HG_DK)]
        r = lax.rsqrt(jnp.mean(oh * oh, axis=-1, keepdims=True) + RMS_EPS)
        ys.append(oh * r)
        rs.append(r)
    y = jnp.concatenate(ys, axis=1)
    hg = _f32(hg_ref[...])
    sg = _sig(hg)
    silu = hg * sg
    oa = _bf(y * gain_ref[...] * silu)
    pa = _dot(oa, _rows(wbh))
    pb = _dot(_bf(ob_ref[...]), _rows(wbs))
    pc = _dot(_bf(oc_ref[...]), _rows(wbm))
    g0 = _sig(_f32(gl_ref[:, 0:1024]))
    g1 = _sig(_f32(gl_ref[:, 1024:2048]))
    g2 = _sig(_f32(gl_ref[:, 2048:3072]))
    m = _bf(g0 * pa + g1 * pb + g2 * pc)
    u1 = ALPHA * x_ref[...] + _dot(m, _rows(wout))
    xhat, rstd = _layer_norm(u1)
    return dict(y=y, rs=rs, hg=hg, sg=sg, silu=silu, oa=oa, pa=pa, pb=pb, pc=pc,
                g0=g0, g1=g1, g2=g2, m=m, xhat=xhat, rstd=rstd)


def _gathered_spec(lo, hi):
    n = hi - lo
    return pl.BlockSpec((N_DEV, n, D_MODEL), lambda *_: (0, lo // n, 0), pipeline_mode=pl.Buffered(1))


def _rows(w_ref):
    return w_ref[...].reshape(-1, D_MODEL)


def _merge_in_specs(T):
    row = lambda w, c=0: pl.BlockSpec((T, w), lambda i: (i, c))
    vec = pl.BlockSpec((1, D_MODEL), lambda i: (0, 0))
    w = [_gathered_spec(lo, hi) for lo, hi in ((R_BH, R_BS), (R_BS, R_BM), (R_BM, R_OUT), (R_OUT, R_KV))]
    return [row(1024), row(1024, C_HG // 1024), row(1024), row(1024), row(3072), row(1024), vec, *w, vec, vec]


def _merge_fwd(o_raw, zmain, o_b, o_c, gl, x, gain, wbh, wbs, wbm, wout, ln_g, ln_b, *, T):
    S = x.shape[0]

    def body(oraw_ref, hg_ref, ob_ref, oc_ref, gl_ref, x_ref, gain_ref, wbh_r, wbs_r, wbm_r, wout_r, g_ref, b_ref,
             h1_ref, h1b_ref):
        f = _merge_forward(oraw_ref, hg_ref, ob_ref, oc_ref, gl_ref, x_ref, gain_ref, wbh_r, wbs_r, wbm_r, wout_r)
        h1 = f["xhat"] * g_ref[...] + b_ref[...]
        h1_ref[...] = h1
        h1b_ref[...] = _bf(h1)

    row = pl.BlockSpec((T, D_MODEL), lambda i: (i, 0))
    return pl.pallas_call(
        body,
        grid=(S // T,),
        in_specs=_merge_in_specs(T),
        out_specs=[row, row],
        out_shape=[jax.ShapeDtypeStruct((S, D_MODEL), F32), jax.ShapeDtypeStruct((S, D_MODEL), BF16)],
        compiler_params=_cparams("parallel"),
        name="merge_fwd",
    )(o_raw, zmain, o_b, o_c, gl, x, gain, wbh, wbs, wbm, wout, ln_g, ln_b)


def _merge_bwd(d_h1, o_raw, zmain, o_b, o_c, gl, x, gain, wbh, wbs, wbm, wout, ln_g, ln_b, *, T):
    S = x.shape[0]

    def body(dh1_ref, oraw_ref, hg_ref, ob_ref, oc_ref, gl_ref, x_ref, gain_ref, wbh_r, wbs_r, wbm_r, wout_r, g_ref, b_ref,
             dx_ref, du1_ref, m_ref, oa_ref, dpa_ref, dpb_ref, dpc_ref, doraw_ref, dob_ref, doc_ref, dz_ref,
             dgain_ref, dg_ref, db_ref):
        del b_ref

        @pl.when(pl.program_id(0) == 0)
        def _():
            dgain_ref[...] = jnp.zeros_like(dgain_ref)
            dg_ref[...] = jnp.zeros_like(dg_ref)
            db_ref[...] = jnp.zeros_like(db_ref)

        f = _merge_forward(oraw_ref, hg_ref, ob_ref, oc_ref, gl_ref, x_ref, gain_ref, wbh_r, wbs_r, wbm_r, wout_r)
        dh1 = dh1_ref[...]
        dg_ref[...] += jnp.sum(dh1 * f["xhat"], axis=0, keepdims=True)
        db_ref[...] += jnp.sum(dh1, axis=0, keepdims=True)
        du1 = _layer_norm_bwd(dh1, g_ref[...], f["xhat"], f["rstd"])
        dx_ref[...] = ALPHA * du1
        du1b = _bf(du1)
        du1_ref[...] = du1b
        m_ref[...] = f["m"]
        oa_ref[...] = f["oa"]
        dm = _dot_nt(du1b, _rows(wout_r))
        for i, (g, p, dp_ref, dob_r, w_r) in enumerate((
                (f["g0"], f["pa"], dpa_ref, None, wbh_r),
                (f["g1"], f["pb"], dpb_ref, dob_ref, wbs_r),
                (f["g2"], f["pc"], dpc_ref, doc_ref, wbm_r))):
            dz_ref[:, pl.ds((i + 1) * 1024, 1024)] = _bf(dm * p * g * (1.0 - g))
            dp = _bf(dm * g)
            dp_ref[...] = dp
            d_branch = _dot_nt(dp, _rows(w_r))
            if dob_r is not None:
                dob_r[...] = _bf(d_branch)
            else:
                doa = d_branch
        gain = gain_ref[...]
        t = doa * f["y"]
        dgain_ref[...] += jnp.sum(t * f["silu"], axis=0, keepdims=True)
        sg = f["sg"]
        dz_ref[:, 0:1024] = _bf(t * gain * sg * (1.0 + f["hg"] * (1.0 - sg)))
        dy = doa * gain * f["silu"]
        for h in range(HG_HEADS):
            cols = slice(h * HG_DK, (h + 1) * HG_DK)
            yh = f["y"][:, cols]
            dyh = dy[:, cols]
            doraw_ref[:, pl.ds(h * HG_DK, HG_DK)] = _bf(
                f["rs"][h] * (dyh - yh * jnp.mean(dyh * yh, axis=-1, keepdims=True)))

    row = lambda w: pl.BlockSpec((T, w), lambda i: (i, 0))
    vec = pl.BlockSpec((1, D_MODEL), lambda i: (0, 0))
    bshape = jax.ShapeDtypeStruct((S, D_MODEL), BF16)
    vshape = jax.ShapeDtypeStruct((1, D_MODEL), F32)
    return pl.pallas_call(
        body,
        grid=(S // T,),
        in_specs=[row(1024)] + _merge_in_specs(T),
        out_specs=[row(1024)] * 10 + [row(4096), vec, vec, vec],
        out_shape=[jax.ShapeDtypeStruct((S, D_MODEL), F32)] + [bshape] * 9
        + [jax.ShapeDtypeStruct((S, 4096), BF16), vshape, vshape, vshape],
        compiler_params=_cparams("arbitrary"),
        name="merge_bwd",
    )(d_h1, o_raw, zmain, o_b, o_c, gl, x, gain, wbh, wbs, wbm, wout, ln_g, ln_b)


def _mlp_fwd_bwd(h1, target, wup_t, wdn, ln_g, ln_b, *, T, FC):
    S = h1.shape[0]
    nf = D_FF // FC
    assert FC == R_BH - R_UP == R_UP - R_DN

    def body(h1_ref, t_ref, wup_ref, wdn_ref, g_ref, b_ref, dh1_ref, a_ref, dup_ref, du2_ref, loss_ref, dg_ref, db_ref, up_scr):
        @pl.when(pl.program_id(0) == 0)
        def _():
            loss_ref[...] = jnp.zeros_like(loss_ref)
            dg_ref[...] = jnp.zeros_like(dg_ref)
            db_ref[...] = jnp.zeros_like(db_ref)

        h1v = h1_ref[...]
        h1b = _bf(h1v)
        ff = jnp.zeros((T, D_MODEL), F32)
        for j in range(nf):
            rows = pl.ds(j * FC, FC)
            up = jnp.maximum(_dot_nt(h1b, wup_ref[j]), 0.0)
            up_scr[:, rows] = _bf(up)
            a = _bf(up * up)
            a_ref[:, rows] = a
            ff = ff + _dot(a, wdn_ref[j])
        xhat, rstd = _layer_norm(ALPHA * h1v + ff)
        gamma = g_ref[...]
        err = xhat * gamma + b_ref[...] - t_ref[...]
        loss_ref[...] += jnp.sum(jnp.sum(err * err, axis=-1, keepdims=True), axis=0, keepdims=True) * (0.5 / D_MODEL)
        dy = err * (1.0 / D_MODEL)
        dg_ref[...] += jnp.sum(dy * xhat, axis=0, keepdims=True)
        db_ref[...] += jnp.sum(dy, axis=0, keepdims=True)
        du2 = _layer_norm_bwd(dy, gamma, xhat, rstd)
        du2b = _bf(du2)
        du2_ref[...] = du2b
        dh1 = ALPHA * du2
        for j in range(nf):
            rows = pl.ds(j * FC, FC)
            dup = _bf(_dot_nt(du2b, wdn_ref[j]) * (2.0 * up_scr[:, rows].astype(F32)))
            dup_ref[:, rows] = dup
            dh1 = dh1 + _dot(dup, wup_ref[j])
        dh1_ref[...] = dh1

    row = lambda w: pl.BlockSpec((T, w), lambda i: (i, 0))
    vec = pl.BlockSpec((1, D_MODEL), lambda i: (0, 0))
    vshape = jax.ShapeDtypeStruct((1, D_MODEL), F32)
    return pl.pallas_call(
        body,
        grid=(S // T,),
        in_specs=[row(1024), row(1024), _gathered_spec(R_UP, R_BH), _gathered_spec(R_DN, R_UP), vec, vec],
        out_specs=[row(1024), row(D_FF), row(D_FF), row(1024), pl.BlockSpec((8, 128), lambda i: (0, 0)), vec, vec],
        out_shape=[
            jax.ShapeDtypeStruct((S, D_MODEL), F32),
            jax.ShapeDtypeStruct((S, D_FF), BF16),
            jax.ShapeDtypeStruct((S, D_FF), BF16),
            jax.ShapeDtypeStruct((S, D_MODEL), BF16),
            jax.ShapeDtypeStruct((8, 128), F32), vshape, vshape,
        ],
        scratch_shapes=[pltpu.VMEM((T, D_FF), BF16)],
        compiler_params=_cparams("arbitrary"),
        name="mlp_fwd_bwd",
    )(h1, target, wup_t, wdn, ln_g, ln_b)


def _local_step(x, mem, target, lb_logits, gain, sinks, rel_bias, ln1_g, ln1_b, ln2_g, ln2_b,
                win_t, dep0, other_weights, send_other_grads, send_small_grads, send_win_grad):
    S = x.shape[0]
    T = min(256, S)
    KC = min(2048, S)
    zmain, xb = _mm_nt(x, win_t, n_cols=C_GL, tm=min(512, S), tn=C_GL, out_dtype=BF16, name="in_proj_main", dep=dep0,
                       also_a_bf16=True)
    gl = _mm_nt(x, win_t[C_GL:], tm=min(512, S), tn=1536, out_dtype=BF16, name="in_proj_gates")
    bucket = _t5_bucket_table()

    o_raw, states = _hgrn_fwd(zmain, lb_logits, T=min(1024, S))
    o_b, swa_probs = _swa_fwd(zmain, bucket, rel_bias, sinks)
    g2 = other_weights(o_b)
    mkv = _mem_kv_proj(mem, g2)
    o_c = _mem_fwd(zmain, mkv, T=min(512, S))
    merge_args = (o_raw, zmain, o_b, o_c, gl, x, gain, g2, g2, g2, g2, ln1_g, ln1_b)
    h1, h1b = _merge_fwd(*merge_args, T=T)

    d_h1, act, d_up, du2, loss, d_ln2_g, d_ln2_b = _mlp_fwd_bwd(h1, target, g2, g2, ln2_g, ln2_b, T=min(512, S), FC=512)
    wgrad = functools.partial(_mm_tn, out_dtype=BF16)
    g_wdn = wgrad(act, du2, kc=KC, name="grad_w_down")
    g_wup_t = wgrad(d_up, h1b, kc=KC, name="grad_w_up")

    (dx_part, du1, m, oa, dpa, dpb, dpc, d_oraw, d_ob, d_oc, d_hg_gl,
     d_gain, d_ln1_g, d_ln1_b) = _merge_bwd(d_h1, *merge_args, T=T)
    g_wout = wgrad(m, du1, kc=KC, name="grad_w_out")
    g_wbh = wgrad(oa, dpa, kc=KC, name="grad_w_branch_hg")
    g_wbs = wgrad(o_b, dpb, kc=KC, name="grad_w_branch_swa")
    g_wbm = wgrad(o_c, dpc, kc=KC, name="grad_w_branch_mem")

    d_mq, d_mkv = _mem_bwd(zmain, mkv, o_c, d_oc, T=min(512, S))
    g_wkv_t = wgrad(d_mkv, mem, kc=MEM_LEN, name="grad_w_mem_kv")
    sent_others = send_other_grads(
        dict(wkv_t=g_wkv_t, wbh=g_wbh, wbs=g_wbs, wbm=g_wbm, wout=g_wout, wup_t=g_wup_t, wdn=g_wdn))
    d_sq, d_skv, d_rb, d_sink = _swa_bwd(zmain, o_b, swa_probs, d_ob, bucket, sent_others)
    d_qfv, d_lb = _hgrn_bwd(zmain, lb_logits, states, d_oraw, T=min(1024, S))
    sent_small = send_small_grads(_pack_small_grads(d_lb, d_gain, d_sink, d_rb, d_ln1_g, d_ln1_b, d_ln2_g, d_ln2_b, loss))

    head_major = lambda a: a.reshape(3, HG_HEADS, HG_DK, D_MODEL).transpose(1, 0, 2, 3).reshape(3 * D_MODEL, D_MODEL)
    col_major = lambda a: a.reshape(HG_HEADS, 3, HG_DK, D_MODEL).transpose(1, 0, 2, 3).reshape(3 * D_MODEL, D_MODEL)
    pieces = (d_qfv, d_hg_gl, d_sq, d_skv, d_mq)
    g_qfv, g_hg_gl, g_sq, g_skv, g_mq = [
        wgrad(p, xb, kc=KC, name="grad_w_in_" + n) for p, n in zip(pieces, ("qfv", "hg_gates", "swa_q", "swa_kv", "mem_q"))]
    g_win_t = jnp.concatenate([col_major(g_qfv), g_hg_gl[:D_MODEL], g_sq, g_skv, g_mq, g_hg_gl[D_MODEL:]], axis=0)
    sent_win = send_win_grad(g_win_t, sent_small)
    return _grad_x(*pieces, head_major(win_t[:C_HG]), win_t, dx_part, sent_win, tm=T)


MESH = pl.DeviceIdType.MESH
ANY = pl.BlockSpec(memory_space=pl.ANY)


def _coords():
    return lax.axis_index("x"), lax.axis_index("y"), lax.axis_index("c")


def _other_chips(x, y):
    return [(1 - x, y), (x, 1 - y), (1 - x, 1 - y)]


def _all_gather_weights(*arrays):
    na = len(arrays)

    def body(*refs):
        srcs, dsts = refs[:na], refs[na:2 * na]
        send_sems, recv_sems, local_sems = refs[2 * na:]
        x, y, c = _coords()
        me, sibling = (x, y, c), (x, y, 1 - c)
        chips = _other_chips(x, y)

        def slot(a, px, py, pc):
            return dsts[a].at[4 * px + 2 * py + pc]

        def copy(a, k, block, to, from_shard=False):
            return pltpu.make_async_remote_copy(
                src_ref=srcs[a] if from_shard else slot(a, *block), dst_ref=slot(a, *block),
                send_sem=send_sems.at[a * 7 + k], recv_sem=recv_sems.at[a * 7 + k],
                device_id=to, device_id_type=MESH)

        own = [pltpu.make_async_copy(srcs[a], slot(a, *me), local_sems.at[a]) for a in range(na)]
        for cp in own:
            cp.start()
        first = []
        for a in range(na):
            first.append(copy(a, 0, me, sibling, True))
            first += [copy(a, 1 + j, me, (*chip, c), True) for j, chip in enumerate(chips)]
        for cp in first:
            cp.start()
        passed = []
        for j, chip in enumerate(chips):
            for a in range(na):
                copy(a, 1 + j, (*chip, c), me).wait_recv()
                fwd = copy(a, 4 + j, (*chip, c), sibling)
                fwd.start()
                passed.append(fwd)
        for a in range(na):
            copy(a, 0, sibling, me).wait_recv()
            for j, chip in enumerate(chips):
                copy(a, 4 + j, (*chip, 1 - c), me).wait_recv()
        for cp in first + passed:
            cp.wait_send()
        for cp in own:
            cp.wait()

    return pl.pallas_call(
        body,
        in_specs=[ANY] * na,
        out_specs=[ANY] * na,
        out_shape=[jax.ShapeDtypeStruct((N_DEV,) + a.shape, a.dtype) for a in arrays],
        scratch_shapes=[pltpu.SemaphoreType.DMA((7 * na,)), pltpu.SemaphoreType.DMA((7 * na,)),
                        pltpu.SemaphoreType.DMA((na,))],
        name="all_gather_weights",
    )(*arrays)


HBM = pl.BlockSpec(memory_space=pltpu.HBM)
SEM = pl.BlockSpec(memory_space=pltpu.SEMAPHORE)
_DATAFLOW = pltpu.SideEffectType.DATAFLOW_SIDE_EFFECTING


def _peer(x, y, c, r):
    return x ^ (r >> 2), y ^ ((r >> 1) & 1), c ^ (r & 1)


def _direct_copies(src_ref, land_ref, send_sems, recv_sems, gather, receiving):
    x, y, c = _coords()
    me = 4 * x + 2 * y + c
    copies = []
    for r in range(1, N_DEV):
        px, py, pc = _peer(x, y, c, r)
        peer = 4 * px + 2 * py + pc
        if gather:
            src, dst = src_ref, land_ref.at[peer if receiving else me]
        else:
            src, dst = src_ref.at[peer], land_ref.at[r - 1]
        copies.append(pltpu.make_async_remote_copy(
            src_ref=src, dst_ref=dst, send_sem=send_sems.at[r - 1], recv_sem=recv_sems.at[r - 1],
            device_id=(px, py, pc), device_id_type=MESH))
    return copies


def _direct_start(src, land, *, gather, name, after=None):
    def body(src_ref, land_ref, *rest):
        send_sems, recv_sems, token = rest[-5], rest[-4], rest[-1]
        for cp in _direct_copies(src_ref, land_ref, send_sems, recv_sems, gather, False):
            cp.start()
        token[...] = jnp.zeros_like(token)

    afters = () if after is None else (after,)
    return pl.pallas_call(
        body,
        name=name,
        out_shape=(pltpu.SemaphoreType.DMA((N_DEV - 1,)), pltpu.SemaphoreType.DMA((N_DEV - 1,)),
                   pltpu.HBM(src.shape, src.dtype), pltpu.HBM(land.shape, land.dtype),
                   jax.ShapeDtypeStruct((8, 128), F32)),
        in_specs=(HBM, HBM) + tuple(ANY for _ in afters),
        out_specs=(SEM, SEM, HBM, HBM, pl.BlockSpec(memory_space=pltpu.VMEM)),
        input_output_aliases={0: 2, 1: 3},
        compiler_params=pltpu.CompilerParams(has_side_effects=_DATAFLOW),
    )(pltpu.with_memory_space_constraint(src, pltpu.HBM), pltpu.with_memory_space_constraint(land, pltpu.HBM), *afters)


def _direct_wait(send_sems, recv_sems, src_thru, land_thru, after, *, gather, name):
    def body(src_ref, land_ref, send_sems_ref, recv_sems_ref, after_ref, src_dead, got_ref):
        del after_ref, src_dead, got_ref
        for cp in _direct_copies(src_ref, land_ref, send_sems_ref, recv_sems_ref, gather, True):
            cp.wait_send()
            cp.wait_recv()

    return pl.pallas_call(
        body,
        name=name,
        out_shape=(pltpu.HBM(src_thru.shape, src_thru.dtype), pltpu.HBM(land_thru.shape, land_thru.dtype)),
        in_specs=(HBM, HBM, SEM, SEM, ANY),
        out_specs=(HBM, HBM),
        input_output_aliases={0: 0, 1: 1},
        compiler_params=pltpu.CompilerParams(has_side_effects=_DATAFLOW),
    )(src_thru, land_thru, send_sems, recv_sems, after)


def _sum_partials(src, land, me, *, tr, name):
    R = src.shape[1]

    def body(me_ref, s_ref, l_ref, o_ref):
        del me_ref
        acc = s_ref[0].astype(F32)
        for r in range(N_DEV - 1):
            acc = acc + l_ref[r].astype(F32)
        o_ref[...] = acc

    return pl.pallas_call(
        body,
        grid_spec=pltpu.PrefetchScalarGridSpec(
            num_scalar_prefetch=1, grid=(R // tr,),
            in_specs=[pl.BlockSpec((1, tr, 1024), lambda i, mr: (mr[0], i, 0)),
                      pl.BlockSpec((N_DEV - 1, tr, 1024), lambda i, mr: (0, i, 0))],
            out_specs=pl.BlockSpec((tr, 1024), lambda i, mr: (i, 0))),
        out_shape=jax.ShapeDtypeStruct((R, 1024), F32),
        name=name,
    )(me, src, land)


_SMALL = ("lb_logits", "hg_norm_gain", "swa_sinks", "rel_bias", "ln1_g", "ln1_b", "ln2_g", "ln2_b")


def _pack_small_grads(d_lb, d_gain, d_sink, d_rb, d_ln1_g, d_ln1_b, d_ln2_g, d_ln2_b, loss):
    def body(lb_ref, gain_ref, sink_ref, rb_ref, l1g_ref, l1b_ref, l2g_ref, l2b_ref, loss_ref, o_ref):
        o_ref[...] = jnp.zeros_like(o_ref)
        for row, ref in ((SM_LB, lb_ref), (SM_GAIN, gain_ref), (SM_L1G, l1g_ref), (SM_L1B, l1b_ref),
                         (SM_L2G, l2g_ref), (SM_L2B, l2b_ref)):
            o_ref[row:row + 1, :] = ref[...]
        o_ref[SM_SINK:SM_SINK + 1, 0:128] = sink_ref[0:1, :]
        o_ref[SM_LOSS:SM_LOSS + 1, 0:128] = loss_ref[0:1, :]
        o_ref[SM_RB:SM_RB + NUM_BUCKETS, 0:128] = rb_ref[...]

    vm = pl.BlockSpec(memory_space=pltpu.VMEM)
    return pl.pallas_call(
        body,
        in_specs=[vm] * 9,
        out_specs=vm,
        out_shape=jax.ShapeDtypeStruct((SM_ROWS, D_MODEL), F32),
        name="pack_small_grads",
    )(d_lb, d_gain, d_sink, d_rb, d_ln1_g, d_ln1_b, d_ln2_g, d_ln2_b, loss)


def _small_finish(gathered, w, m, v):
    n = len(_SMALL)

    def body(*refs):
        g_ref = refs[0]
        w_refs, m_refs, v_refs = refs[1:1 + n], refs[1 + n:1 + 2 * n], refs[1 + 2 * n:1 + 3 * n]
        outs = refs[1 + 3 * n:]
        loss_ref, tot = outs[0], outs[-1]
        g_out, d_out, m_out, v_out = (outs[1 + k * n:1 + (k + 1) * n] for k in range(4))
        acc = g_ref[0]
        for d in range(1, N_DEV):
            acc = acc + g_ref[d]
        tot[...] = acc
        loss_ref[...] = tot[SM_LOSS:SM_LOSS + 1, 0:1]
        lb = _lower_bound(w_refs[0])
        dl0 = tot[SM_LB:SM_LB + 1, :] * lb * (1.0 - lb)
        grads = (jnp.concatenate([dl0, -dl0], axis=0), tot[SM_GAIN:SM_GAIN + 1, :],
                 tot[SM_SINK:SM_SINK + 1, 0:SWA_HEADS], tot[SM_RB:SM_RB + NUM_BUCKETS, 0:SWA_HEADS],
                 tot[SM_L1G:SM_L1G + 1, :], tot[SM_L1B:SM_L1B + 1, :], tot[SM_L2G:SM_L2G + 1, :], tot[SM_L2B:SM_L2B + 1, :])
        for k, g in enumerate(grads):
            g_out[k][...] = g
            d_out[k][...], m_out[k][...], v_out[k][...] = _adam_step(w_refs[k][...], g, m_refs[k][...], v_refs[k][...])

    vm = pl.BlockSpec(memory_space=pltpu.VMEM)
    shapes = [jax.ShapeDtypeStruct(w[k].shape, F32) for k in _SMALL]
    res = pl.pallas_call(
        body,
        in_specs=[vm] * (1 + 3 * n),
        out_specs=[vm] * (1 + 4 * n),
        out_shape=[jax.ShapeDtypeStruct((1, 1), F32)] + shapes * 4,
        scratch_shapes=[pltpu.VMEM((SM_ROWS, D_MODEL), F32)],
        name="small_finish",
    )(gathered, *[w[k] for k in _SMALL], *[m[k] for k in _SMALL], *[v[k] for k in _SMALL])
    parts = [dict(zip(_SMALL, res[1 + k * n:1 + (k + 1) * n])) for k in range(4)]
    return (res[0], *parts)


def _adam_step(w, g, m, v):
    nm = ADAM_B1 * m + (1.0 - ADAM_B1) * g
    nv = ADAM_B2 * v + (1.0 - ADAM_B2) * jnp.square(g)
    m_hat = nm / (1.0 - ADAM_B1 ** ADAM_STEP)
    v_hat = nv / (1.0 - ADAM_B2 ** ADAM_STEP)
    return -ADAM_LR * (m_hat / (jnp.sqrt(v_hat) + ADAM_EPS) + ADAM_WD * w), nm, nv


def _adamw(w, g, m, v, *, tr, name):
    R, C = w.shape

    def body(w_ref, g_ref, m_ref, v_ref, d_ref, nm_ref, nv_ref):
        d_ref[...], nm_ref[...], nv_ref[...] = _adam_step(w_ref[...], g_ref[...], m_ref[...], v_ref[...])

    spec = pl.BlockSpec((tr, C), lambda i: (i, 0))
    return pl.pallas_call(
        body,
        grid=(R // tr,),
        in_specs=[spec] * 4,
        out_specs=[spec] * 3,
        out_shape=[jax.ShapeDtypeStruct((R, C), F32)] * 3,
        compiler_params=_cparams("parallel"),
        name=name,
    )(w, g, m, v)


_WEIGHTS = ("w_in", "lb_logits", "hg_norm_gain", "swa_sinks", "rel_bias", "w_mem_kv", "w_branch_hg", "w_branch_swa",
            "w_branch_mem", "w_out", "ln1_g", "ln1_b", "w_up", "w_down", "ln2_g", "ln2_b")


def kernel(x, mem, w_in, lb_logits, hg_norm_gain, swa_sinks, rel_bias, w_mem_kv, w_branch_hg, w_branch_swa, w_branch_mem, w_out, ln1_g, ln1_b, w_up, w_down, ln2_g, ln2_b, loss_target, m_w_in, m_lb_logits, m_hg_norm_gain, m_swa_sinks, m_rel_bias, m_w_mem_kv, m_w_branch_hg, m_w_branch_swa, m_w_branch_mem, m_w_out, m_ln1_g, m_ln1_b, m_w_up, m_w_down, m_ln2_g, m_ln2_b, v_w_in, v_lb_logits, v_hg_norm_gain, v_swa_sinks, v_rel_bias, v_w_mem_kv, v_w_branch_hg, v_w_branch_swa, v_w_branch_mem, v_w_out, v_ln1_g, v_ln1_b, v_w_up, v_w_down, v_ln2_g, v_ln2_b):
    w = dict(w_in=w_in, lb_logits=lb_logits, hg_norm_gain=hg_norm_gain, swa_sinks=swa_sinks, rel_bias=rel_bias,
             w_mem_kv=w_mem_kv, w_branch_hg=w_branch_hg, w_branch_swa=w_branch_swa, w_branch_mem=w_branch_mem,
             w_out=w_out, ln1_g=ln1_g, ln1_b=ln1_b, w_up=w_up, w_down=w_down, ln2_g=ln2_g, ln2_b=ln2_b)
    mom = dict(w_in=m_w_in, lb_logits=m_lb_logits, hg_norm_gain=m_hg_norm_gain, swa_sinks=m_swa_sinks, rel_bias=m_rel_bias,
               w_mem_kv=m_w_mem_kv, w_branch_hg=m_w_branch_hg, w_branch_swa=m_w_branch_swa, w_branch_mem=m_w_branch_mem,
               w_out=m_w_out, ln1_g=m_ln1_g, ln1_b=m_ln1_b, w_up=m_w_up, w_down=m_w_down, ln2_g=m_ln2_g, ln2_b=m_ln2_b)
    var = dict(w_in=v_w_in, lb_logits=v_lb_logits, hg_norm_gain=v_hg_norm_gain, swa_sinks=v_swa_sinks, rel_bias=v_rel_bias,
               w_mem_kv=v_w_mem_kv, w_branch_hg=v_w_branch_hg, w_branch_swa=v_w_branch_swa, w_branch_mem=v_w_branch_mem,
               w_out=v_w_out, ln1_g=v_ln1_g, ln1_b=v_ln1_b, w_up=v_w_up, w_down=v_w_down, ln2_g=v_ln2_g, ln2_b=v_ln2_b)
    xc, yc, cc = _coords()

    p1 = _bf(w_in[0].T)
    p2 = _bf(jnp.concatenate([w_down[0], w_up[0].T, w_branch_hg[0], w_branch_swa[0], w_branch_mem[0], w_out[0],
                              w_mem_kv[0].T], axis=0))
    me = 4 * xc + 2 * yc + cc
    (g1,) = _all_gather_weights(p1)
    land2 = lax.dynamic_update_slice(lax.empty((N_DEV, R_OTHER, D_MODEL), BF16), p2[None], (me, 0, 0))
    ag2 = _direct_start(p2, land2, gather=True, name="gather_other_weights_start")

    def other_weights(after):
        return _direct_wait(*ag2[:4], after, gather=True, name="gather_other_weights_wait")[1]

    blocks = lambda a: a.reshape(N_DEV, a.shape[0] // N_DEV, D_MODEL)
    started = {}

    def send_other_grads(g):
        part = jnp.concatenate([blocks(g[k]) for k in ("wdn", "wup_t", "wbh", "wbs", "wbm", "wout", "wkv_t")], axis=1)
        started["others"] = _direct_start(part, lax.empty((N_DEV - 1, R_OTHER, D_MODEL), BF16), gather=False,
                                          name="scatter_other_grads_start")
        return started["others"][4]

    me1 = me.reshape(1).astype(jnp.int32)
    grads, delta, new_m, new_v = {}, {}, {}, {}

    def adamw(name):
        w2 = w[name][0]
        delta[name], new_m[name], new_v[name] = _adamw(
            w2, grads[name], mom[name][0], var[name][0], tr=w2.shape[0] // 4, name="adamw_" + name)

    def send_small_grads(packed):
        land = lax.dynamic_update_slice(lax.empty((N_DEV, SM_ROWS, D_MODEL), F32), packed[None], (me, 0, 0))
        started["small"] = _direct_start(packed, land, gather=True, name="gather_small_grads_start")
        return started["small"][4]

    def send_win_grad(g, after):
        started["win"] = _direct_start(blocks(g), lax.empty((N_DEV - 1, IN_SHARD, D_MODEL), BF16), gather=False,
                                       name="scatter_w_in_grad_start", after=after)
        mine2, landed2 = _direct_wait(*started["others"][:4], started["win"][4], gather=False,
                                      name="scatter_other_grads_wait")
        gs2 = _sum_partials(mine2, landed2, me1, tr=R_OTHER // 2, name="sum_other_grads")
        grads.update(
            w_down=gs2[R_DN:R_UP], w_up=gs2[R_UP:R_BH].T, w_branch_hg=gs2[R_BH:R_BS], w_branch_swa=gs2[R_BS:R_BM],
            w_branch_mem=gs2[R_BM:R_OUT], w_out=gs2[R_OUT:R_KV], w_mem_kv=gs2[R_KV:R_OTHER].T)
        for name in ("w_mem_kv", "w_branch_hg", "w_branch_swa", "w_branch_mem", "w_out", "w_up", "w_down"):
            adamw(name)
        return tuple(new_v[name] for name in new_v)

    grad_x = _local_step(
        x[0], mem[0], loss_target[0], lb_logits, hg_norm_gain, swa_sinks, rel_bias, ln1_g, ln1_b, ln2_g, ln2_b,
        g1.reshape(IN_COLS, D_MODEL), ag2[4], other_weights, send_other_grads, send_small_grads, send_win_grad)

    mine1, landed1 = _direct_wait(*started["win"][:4], grad_x, gather=False, name="scatter_w_in_grad_wait")
    grads["w_in"] = _sum_partials(mine1, landed1, me1, tr=IN_SHARD // 2, name="sum_w_in_grad").T
    adamw("w_in")

    _, gathered = _direct_wait(*started["small"][:4], grad_x, gather=True, name="gather_small_grads_wait")
    loss, g_s, d_s, m_s, v_s = _small_finish(gathered, w, mom, var)
    for dst, src in ((grads, g_s), (delta, d_s), (new_m, m_s), (new_v, v_s)):
        dst.update(src)

    def shaped(d, name):
        return d[name].reshape(w[name].shape)

    return (loss.reshape(()), grad_x[None], *[shaped(grads, n) for n in _WEIGHTS], *[shaped(delta, n) for n in _WEIGHTS],
            *[shaped(new_m, n) for n in _WEIGHTS], *[shaped(new_v, n) for n in _WEIGHTS])
```

```python
import functools
import math

import jax
import jax.numpy as jnp
from jax import lax
from jax.experimental import pallas as pl
from jax.experimental.pallas import tpu as pltpu

F32 = jnp.float32
BF16 = jnp.bfloat16

D_MODEL = 1024
MEM_LEN = 256
HG_HEADS = 8
HG_DK = 128
HG_CHUNK = 64
SWA_HEADS = 16
SWA_HEAD_DIM = 64
SWA_BLOCK = 128
SWA_WINDOW = 128
MEM_HEADS = 4
MEM_HEAD_DIM = 256
NUM_BUCKETS = 32
MAX_DISTANCE = 128
D_FF = 4096
LN_EPS = 1e-5
RMS_EPS = 1e-6
ALPHA = 2.0 ** 0.25
N_DEV = 8

C_HQ, C_HF, C_HI, C_HG, C_SQ, C_SK, C_SV, C_MQ, C_GL = 0, 1024, 2048, 3072, 4096, 5120, 5248, 5376, 6400
IN_COLS = 9472
IN_SHARD = IN_COLS // N_DEV

ADAM_LR = 0.001
ADAM_B1 = 0.9
ADAM_B2 = 0.999
ADAM_EPS = 1e-08
ADAM_WD = 0.01
ADAM_STEP = 10

VMEM_LIMIT = 58 * 1024 * 1024

R_DN, R_UP, R_BH, R_BS, R_BM, R_OUT, R_KV, R_OTHER = 0, 512, 1024, 1152, 1280, 1408, 1536, 1792

SM_LB, SM_GAIN, SM_SINK, SM_L1G, SM_L1B, SM_L2G, SM_L2B, SM_LOSS, SM_RB, SM_ROWS = 0, 2, 3, 4, 5, 6, 7, 8, 16, 48


def _bf(v):
    return v.astype(BF16)


def _f32(v):
    return v.astype(F32)


def _dot(a, b):
    return jnp.dot(a, b, preferred_element_type=F32)


def _dot_nt(a, b):
    return lax.dot_general(a, b, (((1,), (1,)), ((), ())), preferred_element_type=F32)


def _dot_tn(a, b):
    return lax.dot_general(a, b, (((0,), (0,)), ((), ())), preferred_element_type=F32)


def _sig(v):
    return 1.0 / (1.0 + jnp.exp(-v))


def _cparams(*sem):
    return pltpu.CompilerParams(dimension_semantics=sem, vmem_limit_bytes=VMEM_LIMIT)


def _const_spec(shape):
    nd = len(shape)
    return pl.BlockSpec(shape, lambda *_: (0,) * nd, pipeline_mode=pl.Buffered(1))


def _dep_spec():
    return pl.BlockSpec((8, 128), lambda *_: (0, 0))


def _mm_nt(a, bt, *, tm, tn, out_dtype, name, dep=None, n_cols=None, also_a_bf16=False):
    M, K = a.shape
    N = bt.shape[0] if n_cols is None else n_cols
    deps = () if dep is None else (dep,)
    nd = len(deps)
    assert not also_a_bf16 or tn == N

    def body(a_ref, b_ref, *rest):
        ab = _bf(a_ref[...])
        o_ref = rest[nd]
        o_ref[...] = _dot_nt(ab, _bf(b_ref[...])).astype(o_ref.dtype)
        if also_a_bf16:
            rest[nd + 1][...] = ab

    out_specs = [pl.BlockSpec((tm, tn), lambda j, i: (i, j))]
    out_shape = [jax.ShapeDtypeStruct((M, N), out_dtype)]
    if also_a_bf16:
        out_specs.append(pl.BlockSpec((tm, K), lambda j, i: (i, 0)))
        out_shape.append(jax.ShapeDtypeStruct((M, K), BF16))
    res = pl.pallas_call(
        body,
        grid=(N // tn, M // tm),
        in_specs=[pl.BlockSpec((tm, K), lambda j, i: (i, 0)), pl.BlockSpec((tn, K), lambda j, i: (j, 0))]
        + [_dep_spec() for _ in deps],
        out_specs=out_specs,
        out_shape=out_shape,
        compiler_params=_cparams("parallel", "parallel"),
        name=name,
    )(a, bt, *deps)
    return res if also_a_bf16 else res[0]


def _mm_tn_resident(a, b, *, tm, kc, name, out_dtype):
    K, M = a.shape
    N = b.shape[1]
    nk = K // kc

    def body(a_ref, b_ref, o_ref):
        acc = jnp.zeros((tm, N), F32)
        for kk in range(nk):
            sl = pl.ds(kk * kc, kc)
            acc = acc + _dot_tn(_bf(a_ref[sl, :]), _bf(b_ref[sl, :]))
        o_ref[...] = acc.astype(o_ref.dtype)

    return pl.pallas_call(
        body,
        grid=(M // tm,),
        in_specs=[pl.BlockSpec((K, tm), lambda i: (0, i)), _const_spec((K, N))],
        out_specs=pl.BlockSpec((tm, N), lambda i: (i, 0)),
        out_shape=jax.ShapeDtypeStruct((M, N), out_dtype),
        compiler_params=_cparams("parallel"),
        name=name,
    )(a, b)


def _mm_tn(a, b, *, kc, name, out_dtype=F32):
    K, M = a.shape
    N = b.shape[1]
    if M > 1024:
        return _mm_tn_resident(a, b, tm=256, kc=min(kc, 1024), name=name, out_dtype=out_dtype)
    tm = M
    nk = K // kc

    def body(a_ref, b_ref, o_ref, acc):
        k = pl.program_id(1)
        part = _dot_tn(_bf(a_ref[...]), _bf(b_ref[...]))

        @pl.when(k == 0)
        def _():
            acc[...] = part

        @pl.when(k > 0)
        def _():
            acc[...] += part

        @pl.when(k == nk - 1)
        def _():
            o_ref[...] = acc[...].astype(o_ref.dtype)

    return pl.pallas_call(
        body,
        grid=(M // tm, nk),
        in_specs=[pl.BlockSpec((kc, tm), lambda i, k: (k, i)), pl.BlockSpec((kc, N), lambda i, k: (k, 0))],
        out_specs=pl.BlockSpec((tm, N), lambda i, k: (i, 0)),
        out_shape=jax.ShapeDtypeStruct((M, N), out_dtype),
        scratch_shapes=[pltpu.VMEM((tm, N), F32)],
        compiler_params=_cparams("parallel", "arbitrary"),
        name=name,
    )(a, b)


def _grad_x(d_qfv, d_hg_gl, d_sq, d_skv, d_mq, w_qfv, win_t, add, deps, *, tm):
    M = add.shape[0]
    pieces = (d_qfv, d_hg_gl, d_sq, d_skv, d_mq)

    def body(qfv_ref, hggl_ref, sq_ref, skv_ref, mq_ref, wq_ref, w_ref, add_ref, *rest):
        o_ref = rest[-1]
        acc = add_ref[...] + _dot(qfv_ref[...], wq_ref[...])
        acc = acc + _dot(hggl_ref[:, 0:1024], w_ref[C_HG:C_SQ, :])
        acc = acc + _dot(hggl_ref[:, 1024:4096], w_ref[C_GL:IN_COLS, :])
        acc = acc + _dot(sq_ref[...], w_ref[C_SQ:C_SK, :])
        acc = acc + _dot(skv_ref[...], w_ref[C_SK:C_MQ, :])
        o_ref[...] = acc + _dot(mq_ref[...], w_ref[C_MQ:C_GL, :])

    return pl.pallas_call(
        body,
        grid=(M // tm,),
        in_specs=[pl.BlockSpec((tm, p.shape[1]), lambda i: (i, 0)) for p in pieces]
        + [_const_spec(w_qfv.shape), _const_spec(win_t.shape), pl.BlockSpec((tm, D_MODEL), lambda i: (i, 0))]
        + [_dep_spec() for _ in deps],
        out_specs=pl.BlockSpec((tm, D_MODEL), lambda i: (i, 0)),
        out_shape=jax.ShapeDtypeStruct((M, D_MODEL), F32),
        compiler_params=_cparams("parallel"),
        name="grad_x",
    )(*pieces, w_qfv, win_t, add, *deps)


def _lower_bound(lbl_ref):
    l0 = lbl_ref[0:1, :]
    l1 = lbl_ref[1:2, :]
    mx = jnp.maximum(l0, l1)
    e0 = jnp.exp(l0 - mx)
    e1 = jnp.exp(l1 - mx)
    return e0 / (e0 + e1)


def _tri(lower):
    r = lax.broadcasted_iota(jnp.int32, (HG_CHUNK, HG_CHUNK), 0)
    c = lax.broadcasted_iota(jnp.int32, (HG_CHUNK, HG_CHUNK), 1)
    return (r >= c) if lower else (r <= c)


def _hg_gates(fl, lb):
    sg = _sig(fl)
    f = lb + (1.0 - lb) * sg
    return sg, f, jnp.log(f), 1.0 - f


def _scan_rows(v, reverse=False):
    row = lax.broadcasted_iota(jnp.int32, v.shape, 0)
    s = 1
    while s < HG_CHUNK:
        if reverse:
            v = v + jnp.where(row < HG_CHUNK - s, pltpu.roll(v, HG_CHUNK - s, 0), 0.0)
        else:
            v = v + jnp.where(row >= s, pltpu.roll(v, s, 0), 0.0)
        s *= 2
    return v


def _hgrn_fwd(zmain, lb_logits, *, T):
    S = zmain.shape[0]
    nc = T // HG_CHUNK

    def body(q_ref, f_ref, v_ref, lbl_ref, o_ref, st_ref, state):
        @pl.when(pl.program_id(1) == 0)
        def _():
            state[...] = jnp.zeros_like(state)

        lb = _lower_bound(lbl_ref)
        tril = _tri(True)
        qis, updates, decays, intra = [], [], [], []
        for c in range(nc):
            sl = pl.ds(c * HG_CHUNK, HG_CHUNK)
            _, _, g, k = _hg_gates(_f32(f_ref[sl, :]), lb)
            b = _scan_rows(g)
            bl = jnp.sum(g, axis=0, keepdims=True)
            qi = _bf(_f32(q_ref[sl, :]) * jnp.exp(b))
            ki = _bf(k * jnp.exp(-b))
            ko = _bf(k * jnp.exp(bl - b))
            vb = _bf(v_ref[sl, :])
            att = jnp.where(tril, _dot_nt(qi, ki), 0.0)
            intra.append(_dot(_bf(att), vb))
            qis.append(qi)
            updates.append(_dot_tn(vb, ko))
            decays.append(jnp.exp(bl))
        st = state[...]
        for c in range(nc):
            st_ref[0, c] = st
            o_ref[pl.ds(c * HG_CHUNK, HG_CHUNK), :] = intra[c] + _dot_nt(qis[c], _bf(st))
            st = st * decays[c] + updates[c]
        state[...] = st

    col = lambda base: pl.BlockSpec((T, HG_DK), lambda h, t: (t, base + h))
    return pl.pallas_call(
        body,
        grid=(HG_HEADS, S // T),
        in_specs=[col(0), col(8), col(16), pl.BlockSpec((2, HG_DK), lambda h, t: (0, h))],
        out_specs=[
            pl.BlockSpec((T, HG_DK), lambda h, t: (t, h)),
            pl.BlockSpec((1, nc, HG_DK, HG_DK), lambda h, t: (h, t, 0, 0)),
        ],
        out_shape=[
            jax.ShapeDtypeStruct((S, D_MODEL), F32),
            jax.ShapeDtypeStruct((HG_HEADS, S // HG_CHUNK, HG_DK, HG_DK), F32),
        ],
        scratch_shapes=[pltpu.VMEM((HG_DK, HG_DK), F32)],
        compiler_params=_cparams("parallel", "arbitrary"),
        name="hgrn_fwd",
    )(zmain, zmain, zmain, lb_logits)


def _hgrn_bwd(zmain, lb_logits, states, d_o, *, T):
    S = zmain.shape[0]
    nc = T // HG_CHUNK
    nt = S // T

    def body(q_ref, f_ref, v_ref, lbl_ref, st_ref, do_ref, dz_ref, dlb_ref, dstate):
        @pl.when(pl.program_id(1) == 0)
        def _():
            dstate[...] = jnp.zeros_like(dstate)
            dlb_ref[...] = jnp.zeros_like(dlb_ref)

        lb = _lower_bound(lbl_ref)
        tril = _tri(True)
        last_row = lax.broadcasted_iota(jnp.int32, (HG_CHUNK, HG_DK), 0) == HG_CHUNK - 1
        saved = []
        for c in range(nc):
            sl = pl.ds(c * HG_CHUNK, HG_CHUNK)
            sg, f, g, k = _hg_gates(_f32(f_ref[sl, :]), lb)
            b = _scan_rows(g)
            bl = jnp.sum(g, axis=0, keepdims=True)
            eb = jnp.exp(b)
            enb = jnp.exp(-b)
            eo = jnp.exp(bl - b)
            q_in = _f32(q_ref[sl, :]) * eb
            k_in = k * enb
            k_out = k * eo
            qi, ki, ko = _bf(q_in), _bf(k_in), _bf(k_out)
            vb = _bf(v_ref[sl, :])
            dob = do_ref[sl, :]
            att = jnp.where(tril, _dot_nt(qi, ki), 0.0)
            d_att = _bf(jnp.where(tril, _dot_nt(dob, vb), 0.0))
            d_kin = _dot_tn(d_att, qi)
            saved.append(dict(
                sg=sg, f=f, eb=eb, enb=enb, eo=eo, ebl=jnp.exp(bl), k_out=k_out, ko=ko, vb=vb, dob=dob,
                d_v=_dot_tn(_bf(att), dob), d_qin=_dot(d_att, ki), d_kin=d_kin,
                qk=(q_in, k_in), d_state=_dot_tn(dob, qi)))
        dst = dstate[...]
        dsts = [None] * nc
        for c in reversed(range(nc)):
            dsts[c] = dst
            dst = dst * saved[c]["ebl"] + saved[c]["d_state"]
        dstate[...] = dst
        dlb = jnp.zeros((1, HG_DK), F32)
        for c in range(nc):
            sl = pl.ds(c * HG_CHUNK, HG_CHUNK)
            s = saved[c]
            q_in, k_in = s["qk"]
            st = st_ref[0, c]
            dstb = _bf(dsts[c])
            d_v = s["d_v"] + _dot_nt(s["ko"], dstb)
            d_qin = s["d_qin"] + _dot(s["dob"], _bf(st))
            d_kout = _dot(s["vb"], dstb)
            d_decay = jnp.sum(dsts[c] * st, axis=0, keepdims=True)
            kk = d_kout * s["k_out"]
            d_b = d_qin * q_in - s["d_kin"] * k_in - kk
            d_bl = jnp.sum(kk, axis=0, keepdims=True) + d_decay * s["ebl"]
            d_g = _scan_rows(d_b + jnp.where(last_row, d_bl, 0.0), reverse=True)
            d_f = d_g / s["f"] - (s["d_kin"] * s["enb"] + d_kout * s["eo"])
            dz_ref[sl, 0:HG_DK] = _bf(d_qin * s["eb"])
            dz_ref[sl, HG_DK:2 * HG_DK] = _bf(d_f * (1.0 - lb) * s["sg"] * (1.0 - s["sg"]))
            dz_ref[sl, 2 * HG_DK:3 * HG_DK] = _bf(d_v)
            dlb = dlb + jnp.sum(d_f * (1.0 - s["sg"]), axis=0, keepdims=True)
        dlb_ref[...] += dlb

    rev = lambda base: pl.BlockSpec((T, HG_DK), lambda h, t: (nt - 1 - t, base + h))
    outc = pl.BlockSpec((T, HG_DK), lambda h, t: (nt - 1 - t, h))
    return pl.pallas_call(
        body,
        grid=(HG_HEADS, nt),
        in_specs=[
            rev(0), rev(8), rev(16),
            pl.BlockSpec((2, HG_DK), lambda h, t: (0, h)),
            pl.BlockSpec((1, nc, HG_DK, HG_DK), lambda h, t: (h, nt - 1 - t, 0, 0)),
            outc,
        ],
        out_specs=[pl.BlockSpec((T, 3 * HG_DK), lambda h, t: (nt - 1 - t, h)),
                   pl.BlockSpec((1, HG_DK), lambda h, t: (0, h))],
        out_shape=[jax.ShapeDtypeStruct((S, 3 * D_MODEL), BF16), jax.ShapeDtypeStruct((1, D_MODEL), F32)],
        scratch_shapes=[pltpu.VMEM((HG_DK, HG_DK), F32)],
        compiler_params=_cparams("parallel", "arbitrary"),
        name="hgrn_bwd",
    )(zmain, zmain, zmain, lb_logits, states, d_o)


def _t5_bucket_table():
    qi = jnp.arange(SWA_BLOCK)[:, None] + SWA_BLOCK
    kj = jnp.arange(2 * SWA_BLOCK)[None, :]
    n = jnp.clip(qi - kj, 0, SWA_WINDOW - 1)
    max_exact = NUM_BUCKETS // 2
    nf = jnp.maximum(n, 1).astype(F32)
    large = max_exact + (jnp.log(nf / max_exact) / math.log(MAX_DISTANCE / max_exact)
                         * (NUM_BUCKETS - max_exact)).astype(jnp.int32)
    large = jnp.minimum(large, NUM_BUCKETS - 1)
    return jnp.where(n < max_exact, n, large).astype(jnp.int32)


SWA_ROWS = 32


def _swa_bias_init(bias, bucket_ref, rb_ref):
    bk = bucket_ref[...]
    qi = lax.broadcasted_iota(jnp.int32, bk.shape, 0) + SWA_BLOCK
    kj = lax.broadcasted_iota(jnp.int32, bk.shape, 1)
    band = (qi - kj >= 0) & (qi - kj < SWA_WINDOW)
    for h in range(SWA_HEADS):
        def sel(b, acc, h=h):
            return jnp.where(bk == b, rb_ref[b, h], acc)
        t = lax.fori_loop(0, NUM_BUCKETS, sel, jnp.zeros(bk.shape, F32))
        bias[1, h] = jnp.where(band, t, -jnp.inf)
        bias[0, h] = jnp.where(band & (kj >= SWA_BLOCK), t, -jnp.inf)


def _lane_halves(t, kv_head):
    lane = lax.broadcasted_iota(jnp.int32, t.shape, 1)
    rolled = pltpu.roll(t, 64, 1)
    zero = jnp.zeros_like(t)
    if kv_head == 0:
        return jnp.where(lane < 64, t, zero), jnp.where(lane >= 64, rolled, zero)
    return jnp.where(lane < 64, rolled, zero), jnp.where(lane >= 64, t, zero)


def _swa_zero_key0(t):
    return jnp.where(lax.broadcasted_iota(jnp.int32, t.shape, 0) == 0, jnp.zeros_like(t), t)


def _swa_probs(s, masked_bias, sink):
    s = s + masked_bias
    m = jnp.maximum(jnp.max(s, axis=-1, keepdims=True), sink)
    p = jnp.exp(s - m)
    es = jnp.exp(sink - m)
    inv = 1.0 / (jnp.sum(p, axis=-1, keepdims=True) + es)
    return p * inv, es * inv


def _swa_fwd(zmain, bucket, rel_bias, sinks):
    S = zmain.shape[0]
    nb = S // SWA_BLOCK
    scale = SWA_HEAD_DIM ** -0.5

    def body(q_ref, kvc_ref, kvp_ref, bucket_ref, rb_ref, sk_ref, o_ref, p_ref, bias):
        n = pl.program_id(0)

        @pl.when(n == 0)
        def _():
            _swa_bias_init(bias, bucket_ref, rb_ref)

        later = jnp.minimum(n, 1)
        kk = _bf(jnp.concatenate([kvp_ref[:, 0:128], kvc_ref[:, 0:128]], axis=0))
        vv = _swa_zero_key0(_bf(jnp.concatenate([kvp_ref[:, 128:256], kvc_ref[:, 128:256]], axis=0)))
        first_col = lax.broadcasted_iota(jnp.int32, (SWA_ROWS, 2 * SWA_BLOCK), 1) == 0
        for kvh in range(2):
            ka, kb = _lane_halves(kk, kvh)
            va, vb = _lane_halves(vv, kvh)
            qst = _bf(jnp.concatenate([q_ref[:, pl.ds((kvh * 4 + jj) * 128, 128)] for jj in range(4)], axis=0) * scale)
            probs = []
            for odd, kx in enumerate((ka, kb)):
                s = _dot_nt(qst, kx)
                parts = []
                for jj in range(4):
                    h = 2 * (kvh * 4 + jj) + odd
                    for r0 in range(0, SWA_BLOCK, SWA_ROWS):
                        p, ps = _swa_probs(s[jj * SWA_BLOCK + r0:jj * SWA_BLOCK + r0 + SWA_ROWS],
                                           bias[later, h, pl.ds(r0, SWA_ROWS), :], sk_ref[0, h])
                        part = _bf(jnp.where(first_col, ps, p))
                        p_ref[pl.ds(r0, SWA_ROWS), pl.ds(h * 2 * SWA_BLOCK, 2 * SWA_BLOCK)] = part
                        parts.append(part)
                probs.append(jnp.concatenate(parts, axis=0))
            ost = _dot(probs[0], va) + _dot(probs[1], vb)
            for jj in range(4):
                o_ref[:, pl.ds((kvh * 4 + jj) * 128, 128)] = ost[jj * SWA_BLOCK:(jj + 1) * SWA_BLOCK]

    smem = pl.BlockSpec(memory_space=pltpu.SMEM)
    return pl.pallas_call(
        body,
        grid=(nb,),
        in_specs=[
            pl.BlockSpec((SWA_BLOCK, 1024), lambda n: (n, C_SQ // 1024)),
            pl.BlockSpec((SWA_BLOCK, 256), lambda n: (n, C_SK // 256)),
            pl.BlockSpec((SWA_BLOCK, 256), lambda n: (jnp.maximum(n - 1, 0), C_SK // 256)),
            _const_spec((SWA_BLOCK, 2 * SWA_BLOCK)), smem, smem,
        ],
        out_specs=[pl.BlockSpec((SWA_BLOCK, 1024), lambda n: (n, 0)),
                   pl.BlockSpec((SWA_BLOCK, SWA_HEADS * 2 * SWA_BLOCK), lambda n: (n, 0))],
        out_shape=[jax.ShapeDtypeStruct((S, 1024), F32),
                   jax.ShapeDtypeStruct((S, SWA_HEADS * 2 * SWA_BLOCK), BF16)],
        scratch_shapes=[pltpu.VMEM((2, SWA_HEADS, SWA_BLOCK, 2 * SWA_BLOCK), F32)],
        compiler_params=_cparams("arbitrary"),
        name="swa_fwd",
    )(zmain, zmain, zmain, bucket, rel_bias, sinks)


def _swa_bwd(zmain, o_b, probs, d_o, bucket, dep):
    S = zmain.shape[0]
    nb = S // SWA_BLOCK
    scale = SWA_HEAD_DIM ** -0.5

    def body(q_ref, kvc_ref, kvp_ref, o_ref, p_ref, do_ref, bucket_ref, dep_ref,
             dq_ref, dkv_ref, drb_ref, dsk_ref, dbias, carry):
        del dep_ref
        n = pl.program_id(0)

        @pl.when(n == 0)
        def _():
            dbias[...] = jnp.zeros_like(dbias)
            carry[...] = jnp.zeros_like(carry)

        @pl.when(n < nb)
        def _():
            kk = _swa_zero_key0(_bf(jnp.concatenate([kvp_ref[:, 0:128], kvc_ref[:, 0:128]], axis=0)))
            vv = _swa_zero_key0(_bf(jnp.concatenate([kvp_ref[:, 128:256], kvc_ref[:, 128:256]], axis=0)))
            lane = lax.broadcasted_iota(jnp.int32, (2 * SWA_BLOCK, 128), 1)
            lane_q = lax.broadcasted_iota(jnp.int32, (4 * SWA_BLOCK, 128), 1)
            dk_parts, dv_parts = [], []
            for kvh in range(2):
                ka, kb = _lane_halves(kk, kvh)
                va, vb = _lane_halves(vv, kvh)
                pair_cols = [pl.ds((kvh * 4 + jj) * 128, 128) for jj in range(4)]
                qst = _bf(jnp.concatenate([q_ref[:, cl] for cl in pair_cols], axis=0) * scale)
                dost = jnp.concatenate([do_ref[:, cl] for cl in pair_cols], axis=0)
                prod = dost.astype(F32) * jnp.concatenate([o_ref[:, cl] for cl in pair_cols], axis=0)
                dq_st = jnp.zeros((4 * SWA_BLOCK, 128), F32)
                zks, zvs = [], []
                for odd, (kx, vx) in enumerate(((ka, va), (kb, vb))):
                    keep = (lane_q >= 64) if odd else (lane_q < 64)
                    delta = jnp.sum(jnp.where(keep, prod, 0.0), axis=-1, keepdims=True)
                    dp = _dot_nt(dost, vx)
                    p_parts, ds_parts = [], []
                    for jj in range(4):
                        h = 2 * (kvh * 4 + jj) + odd
                        rows = slice(jj * SWA_BLOCK, (jj + 1) * SWA_BLOCK)
                        p = p_ref[:, pl.ds(h * 2 * SWA_BLOCK, 2 * SWA_BLOCK)]
                        ds = _f32(p) * (dp[rows] - delta[rows])
                        dbias[h] += ds
                        p_parts.append(p)
                        ds_parts.append(_bf(ds))
                    pst = jnp.concatenate(p_parts, axis=0)
                    dsst = jnp.concatenate(ds_parts, axis=0)
                    dq_st = dq_st + _dot(dsst, kx)
                    zks.append(_dot_tn(dsst, qst))
                    zvs.append(_dot_tn(pst, dost))
                for jj in range(4):
                    dq_ref[:, pair_cols[jj]] = _bf(dq_st[jj * SWA_BLOCK:(jj + 1) * SWA_BLOCK] * scale)
                zk = jnp.where(lane < 64, zks[0], zks[1])
                zv = jnp.where(lane < 64, zvs[0], zvs[1])
                dk_parts.append(zk + pltpu.roll(zk, 64, 1))
                dv_parts.append(zv + pltpu.roll(zv, 64, 1))
            dk = jnp.where(lane < 64, dk_parts[0], dk_parts[1])
            dv = jnp.where(lane < 64, dv_parts[0], dv_parts[1])
            dkv = _swa_zero_key0(jnp.concatenate([dk, dv], axis=1))
            dkv_ref[...] = _bf(carry[...] + dkv[0:SWA_BLOCK])
            carry[...] = dkv[SWA_BLOCK:]

        @pl.when(n == nb)
        def _():
            dkv_ref[...] = _bf(carry[...])
            first_col = lax.broadcasted_iota(jnp.int32, (SWA_BLOCK, 2 * SWA_BLOCK), 1) == 0
            bk = jnp.where(first_col, -1, bucket_ref[...])

            row = lax.broadcasted_iota(jnp.int32, (NUM_BUCKETS, 128), 0)
            lane = lax.broadcasted_iota(jnp.int32, (NUM_BUCKETS, 128), 1)

            def total(v):
                return jnp.sum(jnp.sum(v, axis=1, keepdims=True), axis=0, keepdims=True)

            def per_head(h, acc):
                db = dbias[h]
                d_rb, d_sk = acc
                d_sk = d_sk + jnp.where((row == 0) & (lane == h), total(jnp.where(first_col, db, 0.0)), 0.0)

                def per_bucket(b, d_rb):
                    return d_rb + jnp.where((row == b) & (lane == h), total(jnp.where(bk == b, db, 0.0)), 0.0)

                return lax.fori_loop(0, NUM_BUCKETS, per_bucket, d_rb), d_sk

            zero = jnp.zeros((NUM_BUCKETS, 128), F32)
            d_rb, d_sk = lax.fori_loop(0, SWA_HEADS, per_head, (zero, zero))
            drb_ref[...] = d_rb
            dsk_ref[...] = d_sk[0:8]

    cur = lambda n: jnp.minimum(n, nb - 1)
    prev = lambda n: jnp.maximum(jnp.minimum(n, nb - 1) - 1, 0)
    return pl.pallas_call(
        body,
        grid=(nb + 1,),
        in_specs=[
            pl.BlockSpec((SWA_BLOCK, 1024), lambda n: (cur(n), C_SQ // 1024)),
            pl.BlockSpec((SWA_BLOCK, 256), lambda n: (cur(n), C_SK // 256)),
            pl.BlockSpec((SWA_BLOCK, 256), lambda n: (prev(n), C_SK // 256)),
            pl.BlockSpec((SWA_BLOCK, 1024), lambda n: (cur(n), 0)),
            pl.BlockSpec((SWA_BLOCK, SWA_HEADS * 2 * SWA_BLOCK), lambda n: (cur(n), 0)),
            pl.BlockSpec((SWA_BLOCK, 1024), lambda n: (cur(n), 0)),
            _const_spec((SWA_BLOCK, 2 * SWA_BLOCK)), _dep_spec(),
        ],
        out_specs=[
            pl.BlockSpec((SWA_BLOCK, 1024), lambda n: (cur(n), 0)),
            pl.BlockSpec((SWA_BLOCK, 256), lambda n: (jnp.maximum(n - 1, 0), 0)),
            pl.BlockSpec((NUM_BUCKETS, 128), lambda n: (0, 0)),
            pl.BlockSpec((8, 128), lambda n: (0, 0)),
        ],
        out_shape=[
            jax.ShapeDtypeStruct((S, 1024), BF16),
            jax.ShapeDtypeStruct((S, 256), BF16),
            jax.ShapeDtypeStruct((NUM_BUCKETS, 128), F32),
            jax.ShapeDtypeStruct((8, 128), F32),
        ],
        scratch_shapes=[
            pltpu.VMEM((SWA_HEADS, SWA_BLOCK, 2 * SWA_BLOCK), F32),
            pltpu.VMEM((SWA_BLOCK, 256), F32),
        ],
        compiler_params=_cparams("arbitrary"),
        name="swa_bwd",
    )(zmain, zmain, zmain, o_b, probs, d_o, bucket, dep)


def _mem_probs(q_ref, k):
    qs = _bf(q_ref[...] * (MEM_HEAD_DIM ** -0.5))
    s = _dot_nt(qs, k)
    e = jnp.exp(s - jnp.max(s, axis=-1, keepdims=True))
    return qs, e / jnp.sum(e, axis=-1, keepdims=True)


def _mem_q_specs(T):
    return [pl.BlockSpec((T, MEM_HEAD_DIM), lambda t, h=h: (t, C_MQ // MEM_HEAD_DIM + h)) for h in range(MEM_HEADS)]


def _mem_kv_proj(mem, g2):
    def body(mem_ref, w_ref, o_ref):
        o_ref[...] = _dot_nt(_bf(mem_ref[...]), _rows(w_ref))

    return pl.pallas_call(
        body,
        grid=(1,),
        in_specs=[pl.BlockSpec((MEM_LEN, D_MODEL), lambda i: (0, 0)), _gathered_spec(R_KV, R_OTHER)],
        out_specs=pl.BlockSpec((MEM_LEN, 2048), lambda i: (0, 0)),
        out_shape=jax.ShapeDtypeStruct((MEM_LEN, 2048), F32),
        compiler_params=_cparams("arbitrary"),
        name="mem_kv_proj",
    )(mem, g2)


def _mem_fwd(zmain, mkv, *, T):
    S = zmain.shape[0]

    def body(q0, q1, q2, q3, kv_ref, o_ref, p_ref):
        for h, q_ref in enumerate((q0, q1, q2, q3)):
            cols = pl.ds(h * MEM_HEAD_DIM, MEM_HEAD_DIM)
            _, p = _mem_probs(q_ref, _bf(kv_ref[:, cols]))
            pb = _bf(p)
            p_ref[:, cols] = pb
            o_ref[:, cols] = _dot(pb, _bf(kv_ref[:, pl.ds(1024 + h * MEM_HEAD_DIM, MEM_HEAD_DIM)]))

    row = pl.BlockSpec((T, 1024), lambda t: (t, 0))
    return pl.pallas_call(
        body,
        grid=(S // T,),
        in_specs=_mem_q_specs(T) + [_const_spec((MEM_LEN, 2048))],
        out_specs=[row, row],
        out_shape=[jax.ShapeDtypeStruct((S, 1024), F32), jax.ShapeDtypeStruct((S, 1024), BF16)],
        compiler_params=_cparams("parallel"),
        name="mem_fwd",
    )(zmain, zmain, zmain, zmain, mkv)


def _mem_bwd(zmain, mkv, o_c, probs, d_o, *, T):
    S = zmain.shape[0]
    scale = MEM_HEAD_DIM ** -0.5

    def body(q0, q1, q2, q3, kv_ref, o_ref, p_ref, do_ref, dq_ref, dkv_ref):
        @pl.when(pl.program_id(0) == 0)
        def _():
            dkv_ref[...] = jnp.zeros_like(dkv_ref)

        for h, q_ref in enumerate((q0, q1, q2, q3)):
            cols = pl.ds(h * MEM_HEAD_DIM, MEM_HEAD_DIM)
            vcols = pl.ds(1024 + h * MEM_HEAD_DIM, MEM_HEAD_DIM)
            kb = _bf(kv_ref[:, cols])
            qs = _bf(q_ref[...] * scale)
            pb = p_ref[:, cols]
            dob = do_ref[:, cols]
            delta = jnp.sum(dob.astype(F32) * o_ref[:, cols], axis=-1, keepdims=True)
            ds = _bf(_f32(pb) * (_dot_nt(dob, _bf(kv_ref[:, vcols])) - delta))
            dq_ref[:, cols] = _bf(_dot(ds, kb) * scale)
            dkv_ref[:, cols] += _dot_tn(ds, qs)
            dkv_ref[:, vcols] += _dot_tn(pb, dob)

    row = pl.BlockSpec((T, 1024), lambda t: (t, 0))
    return pl.pallas_call(
        body,
        grid=(S // T,),
        in_specs=_mem_q_specs(T) + [_const_spec((MEM_LEN, 2048)), row, row, row],
        out_specs=[row, pl.BlockSpec((MEM_LEN, 2048), lambda t: (0, 0))],
        out_shape=[jax.ShapeDtypeStruct((S, 1024), BF16), jax.ShapeDtypeStruct((MEM_LEN, 2048), F32)],
        compiler_params=_cparams("arbitrary"),
        name="mem_bwd",
    )(zmain, zmain, zmain, zmain, mkv, o_c, probs, d_o)


def _layer_norm(u):
    mu = jnp.mean(u, axis=-1, keepdims=True)
    xc = u - mu
    rstd = lax.rsqrt(jnp.mean(xc * xc, axis=-1, keepdims=True) + LN_EPS)
    return xc * rstd, rstd


def _layer_norm_bwd(dy, gamma, xhat, rstd):
    dxh = dy * gamma
    return rstd * (dxh - jnp.mean(dxh, axis=-1, keepdims=True) - xhat * jnp.mean(dxh * xhat, axis=-1, keepdims=True))


def _merge_forward(oraw_ref, hg_ref, ob_ref, oc_ref, gl_ref, x_ref, gain_ref, wbh, wbs, wbm, wout):
    ys, rs = [], []
    for h in range(HG_HEADS):
        oh = oraw_ref[:, pl.ds(h * HG_DK, HG_DK)]
        r = lax.rsqrt(jnp.mean(oh * oh, axis=-1, keepdims=True) + RMS_EPS)
        ys.append(oh * r)
        rs.append(r)
    y = jnp.concatenate(ys, axis=1)
    hg = _f32(hg_ref[...])
    sg = _sig(hg)
    silu = hg * sg
    oa = _bf(y * gain_ref[...] * silu)
    pa = _dot(oa, _rows(wbh))
    pb = _dot(_bf(ob_ref[...]), _rows(wbs))
    pc = _dot(_bf(oc_ref[...]), _rows(wbm))
    g0 = _sig(_f32(gl_ref[:, 0:1024]))
    g1 = _sig(_f32(gl_ref[:, 1024:2048]))
    g2 = _sig(_f32(gl_ref[:, 2048:3072]))
    m = _bf(g0 * pa + g1 * pb + g2 * pc)
    u1 = ALPHA * x_ref[...] + _dot(m, _rows(wout))
    xhat, rstd = _layer_norm(u1)
    return dict(y=y, rs=rs, hg=hg, sg=sg, silu=silu, oa=oa, pa=pa, pb=pb, pc=pc,
                g0=g0, g1=g1, g2=g2, m=m, xhat=xhat, rstd=rstd)


def _gathered_spec(lo, hi):
    n = hi - lo
    return pl.BlockSpec((N_DEV, n, D_MODEL), lambda *_: (0, lo // n, 0), pipeline_mode=pl.Buffered(1))


def _rows(w_ref):
    return w_ref[...].reshape(-1, D_MODEL)


def _merge_in_specs(T):
    row = lambda w, c=0: pl.BlockSpec((T, w), lambda i: (i, c))
    vec = pl.BlockSpec((1, D_MODEL), lambda i: (0, 0))
    w = [_gathered_spec(lo, hi) for lo, hi in ((R_BH, R_BS), (R_BS, R_BM), (R_BM, R_OUT), (R_OUT, R_KV))]
    return [row(1024), row(1024, C_HG // 1024), row(1024), row(1024), row(3072), row(1024), vec, *w, vec, vec]


def _merge_fwd(o_raw, zmain, o_b, o_c, gl, x, gain, wbh, wbs, wbm, wout, ln_g, ln_b, *, T):
    S = x.shape[0]

    def body(oraw_ref, hg_ref, ob_ref, oc_ref, gl_ref, x_ref, gain_ref, wbh_r, wbs_r, wbm_r, wout_r, g_ref, b_ref,
             h1_ref, h1b_ref):
        f = _merge_forward(oraw_ref, hg_ref, ob_ref, oc_ref, gl_ref, x_ref, gain_ref, wbh_r, wbs_r, wbm_r, wout_r)
        h1 = f["xhat"] * g_ref[...] + b_ref[...]
        h1_ref[...] = h1
        h1b_ref[...] = _bf(h1)

    row = pl.BlockSpec((T, D_MODEL), lambda i: (i, 0))
    return pl.pallas_call(
        body,
        grid=(S // T,),
        in_specs=_merge_in_specs(T),
        out_specs=[row, row],
        out_shape=[jax.ShapeDtypeStruct((S, D_MODEL), F32), jax.ShapeDtypeStruct((S, D_MODEL), BF16)],
        compiler_params=_cparams("parallel"),
        name="merge_fwd",
    )(o_raw, zmain, o_b, o_c, gl, x, gain, wbh, wbs, wbm, wout, ln_g, ln_b)


def _merge_bwd(d_h1, o_raw, zmain, o_b, o_c, gl, x, gain, wbh, wbs, wbm, wout, ln_g, ln_b, *, T):
    S = x.shape[0]

    def body(dh1_ref, oraw_ref, hg_ref, ob_ref, oc_ref, gl_ref, x_ref, gain_ref, wbh_r, wbs_r, wbm_r, wout_r, g_ref, b_ref,
             dx_ref, du1_ref, m_ref, oa_ref, dpa_ref, dpb_ref, dpc_ref, doraw_ref, dob_ref, doc_ref, dz_ref,
             dgain_ref, dg_ref, db_ref):
        del b_ref

        @pl.when(pl.program_id(0) == 0)
        def _():
            dgain_ref[...] = jnp.zeros_like(dgain_ref)
            dg_ref[...] = jnp.zeros_like(dg_ref)
            db_ref[...] = jnp.zeros_like(db_ref)

        f = _merge_forward(oraw_ref, hg_ref, ob_ref, oc_ref, gl_ref, x_ref, gain_ref, wbh_r, wbs_r, wbm_r, wout_r)
        dh1 = dh1_ref[...]
        dg_ref[...] += jnp.sum(dh1 * f["xhat"], axis=0, keepdims=True)
        db_ref[...] += jnp.sum(dh1, axis=0, keepdims=True)
        du1 = _layer_norm_bwd(dh1, g_ref[...], f["xhat"], f["rstd"])
        dx_ref[...] = ALPHA * du1
        du1b = _bf(du1)
        du1_ref[...] = du1b
        m_ref[...] = f["m"]
        oa_ref[...] = f["oa"]
        dm = _dot_nt(du1b, _rows(wout_r))
        for i, (g, p, dp_ref, dob_r, w_r) in enumerate((
                (f["g0"], f["pa"], dpa_ref, None, wbh_r),
                (f["g1"], f["pb"], dpb_ref, dob_ref, wbs_r),
                (f["g2"], f["pc"], dpc_ref, doc_ref, wbm_r))):
            dz_ref[:, pl.ds((i + 1) * 1024, 1024)] = _bf(dm * p * g * (1.0 - g))
            dp = _bf(dm * g)
            dp_ref[...] = dp
            d_branch = _dot_nt(dp, _rows(w_r))
            if dob_r is not None:
                dob_r[...] = _bf(d_branch)
            else:
                doa = d_branch
        gain = gain_ref[...]
        t = doa * f["y"]
        dgain_ref[...] += jnp.sum(t * f["silu"], axis=0, keepdims=True)
        sg = f["sg"]
        dz_ref[:, 0:1024] = _bf(t * gain * sg * (1.0 + f["hg"] * (1.0 - sg)))
        dy = doa * gain * f["silu"]
        for h in range(HG_HEADS):
            cols = slice(h * HG_DK, (h + 1) * HG_DK)
            yh = f["y"][:, cols]
            dyh = dy[:, cols]
            doraw_ref[:, pl.ds(h * HG_DK, HG_DK)] = _bf(
                f["rs"][h] * (dyh - yh * jnp.mean(dyh * yh, axis=-1, keepdims=True)))

    row = lambda w: pl.BlockSpec((T, w), lambda i: (i, 0))
    vec = pl.BlockSpec((1, D_MODEL), lambda i: (0, 0))
    bshape = jax.ShapeDtypeStruct((S, D_MODEL), BF16)
    vshape = jax.ShapeDtypeStruct((1, D_MODEL), F32)
    return pl.pallas_call(
        body,
        grid=(S // T,),
        in_specs=[row(1024)] + _merge_in_specs(T),
        out_specs=[row(1024)] * 10 + [row(4096), vec, vec, vec],
        out_shape=[jax.ShapeDtypeStruct((S, D_MODEL), F32)] + [bshape] * 9
        + [jax.ShapeDtypeStruct((S, 4096), BF16), vshape, vshape, vshape],
        compiler_params=_cparams("arbitrary"),
        name="merge_bwd",
    )(d_h1, o_raw, zmain, o_b, o_c, gl, x, gain, wbh, wbs, wbm, wout, ln_g, ln_b)


def _mlp_fwd_bwd(h1, target, wup_t, wdn, ln_g, ln_b, *, T, FC):
    S = h1.shape[0]
    nf = D_FF // FC
    assert FC == R_BH - R_UP == R_UP - R_DN

    def body(h1_ref, t_ref, wup_ref, wdn_ref, g_ref, b_ref, dh1_ref, a_ref, dup_ref, du2_ref, loss_ref, dg_ref, db_ref, up_scr):
        @pl.when(pl.program_id(0) == 0)
        def _():
            loss_ref[...] = jnp.zeros_like(loss_ref)
            dg_ref[...] = jnp.zeros_like(dg_ref)
            db_ref[...] = jnp.zeros_like(db_ref)

        h1v = h1_ref[...]
        h1b = _bf(h1v)
        ff = jnp.zeros((T, D_MODEL), F32)
        for j in range(nf):
            rows = pl.ds(j * FC, FC)
            up = jnp.maximum(_dot_nt(h1b, wup_ref[j]), 0.0)
            up_scr[:, rows] = _bf(up)
            a = _bf(up * up)
            a_ref[:, rows] = a
            ff = ff + _dot(a, wdn_ref[j])
        xhat, rstd = _layer_norm(ALPHA * h1v + ff)
        gamma = g_ref[...]
        err = xhat * gamma + b_ref[...] - t_ref[...]
        loss_ref[...] += jnp.sum(jnp.sum(err * err, axis=-1, keepdims=True), axis=0, keepdims=True) * (0.5 / D_MODEL)
        dy = err * (1.0 / D_MODEL)
        dg_ref[...] += jnp.sum(dy * xhat, axis=0, keepdims=True)
        db_ref[...] += jnp.sum(dy, axis=0, keepdims=True)
        du2 = _layer_norm_bwd(dy, gamma, xhat, rstd)
        du2b = _bf(du2)
        du2_ref[...] = du2b
        dh1 = ALPHA * du2
        for j in range(nf):
            rows = pl.ds(j * FC, FC)
            dup = _bf(_dot_nt(du2b, wdn_ref[j]) * (2.0 * up_scr[:, rows].astype(F32)))
            dup_ref[:, rows] = dup
            dh1 = dh1 + _dot(dup, wup_ref[j])
        dh1_ref[...] = dh1

    row = lambda w: pl.BlockSpec((T, w), lambda i: (i, 0))
    vec = pl.BlockSpec((1, D_MODEL), lambda i: (0, 0))
    vshape = jax.ShapeDtypeStruct((1, D_MODEL), F32)
    return pl.pallas_call(
        body,
        grid=(S // T,),
        in_specs=[row(1024), row(1024), _gathered_spec(R_UP, R_BH), _gathered_spec(R_DN, R_UP), vec, vec],
        out_specs=[row(1024), row(D_FF), row(D_FF), row(1024), pl.BlockSpec((8, 128), lambda i: (0, 0)), vec, vec],
        out_shape=[
            jax.ShapeDtypeStruct((S, D_MODEL), F32),
            jax.ShapeDtypeStruct((S, D_FF), BF16),
            jax.ShapeDtypeStruct((S, D_FF), BF16),
            jax.ShapeDtypeStruct((S, D_MODEL), BF16),
            jax.ShapeDtypeStruct((8, 128), F32), vshape, vshape,
        ],
        scratch_shapes=[pltpu.VMEM((T, D_FF), BF16)],
        compiler_params=_cparams("arbitrary"),
        name="mlp_fwd_bwd",
    )(h1, target, wup_t, wdn, ln_g, ln_b)


def _local_step(x, mem, target, lb_logits, gain, sinks, rel_bias, ln1_g, ln1_b, ln2_g, ln2_b,
                win_t, dep0, other_weights, send_other_grads, send_small_grads, send_win_grad):
    S = x.shape[0]
    T = min(256, S)
    KC = min(2048, S)
    zmain, xb = _mm_nt(x, win_t, n_cols=C_GL, tm=min(512, S), tn=C_GL, out_dtype=BF16, name="in_proj_main", dep=dep0,
                       also_a_bf16=True)
    gl = _mm_nt(x, win_t[C_GL:], tm=min(512, S), tn=1536, out_dtype=BF16, name="in_proj_gates")
    bucket = _t5_bucket_table()

    o_raw, states = _hgrn_fwd(zmain, lb_logits, T=min(1024, S))
    o_b, swa_probs = _swa_fwd(zmain, bucket, rel_bias, sinks)
    g2 = other_weights(o_b)
    mkv = _mem_kv_proj(mem, g2)
    o_c, mem_probs = _mem_fwd(zmain, mkv, T=min(512, S))
    merge_args = (o_raw, zmain, o_b, o_c, gl, x, gain, g2, g2, g2, g2, ln1_g, ln1_b)
    h1, h1b = _merge_fwd(*merge_args, T=T)

    d_h1, act, d_up, du2, loss, d_ln2_g, d_ln2_b = _mlp_fwd_bwd(h1, target, g2, g2, ln2_g, ln2_b, T=min(512, S), FC=512)
    wgrad = functools.partial(_mm_tn, out_dtype=BF16)
    g_wdn = wgrad(act, du2, kc=KC, name="grad_w_down")
    g_wup_t = wgrad(d_up, h1b, kc=KC, name="grad_w_up")

    (dx_part, du1, m, oa, dpa, dpb, dpc, d_oraw, d_ob, d_oc, d_hg_gl,
     d_gain, d_ln1_g, d_ln1_b) = _merge_bwd(d_h1, *merge_args, T=T)
    g_wout = wgrad(m, du1, kc=KC, name="grad_w_out")
    g_wbh = wgrad(oa, dpa, kc=KC, name="grad_w_branch_hg")
    g_wbs = wgrad(o_b, dpb, kc=KC, name="grad_w_branch_swa")
    g_wbm = wgrad(o_c, dpc, kc=KC, name="grad_w_branch_mem")

    d_mq, d_mkv = _mem_bwd(zmain, mkv, o_c, mem_probs, d_oc, T=min(512, S))
    g_wkv_t = wgrad(d_mkv, mem, kc=MEM_LEN, name="grad_w_mem_kv")
    sent_others = send_other_grads(
        dict(wkv_t=g_wkv_t, wbh=g_wbh, wbs=g_wbs, wbm=g_wbm, wout=g_wout, wup_t=g_wup_t, wdn=g_wdn))
    d_sq, d_skv, d_rb, d_sink = _swa_bwd(zmain, o_b, swa_probs, d_ob, bucket, sent_others)
    d_qfv, d_lb = _hgrn_bwd(zmain, lb_logits, states, d_oraw, T=min(1024, S))
    sent_small = send_small_grads(_pack_small_grads(d_lb, d_gain, d_sink, d_rb, d_ln1_g, d_ln1_b, d_ln2_g, d_ln2_b, loss))

    head_major = lambda a: a.reshape(3, HG_HEADS, HG_DK, D_MODEL).transpose(1, 0, 2, 3).reshape(3 * D_MODEL, D_MODEL)
    col_major = lambda a: a.reshape(HG_HEADS, 3, HG_DK, D_MODEL).transpose(1, 0, 2, 3).reshape(3 * D_MODEL, D_MODEL)
    pieces = (d_qfv, d_hg_gl, d_sq, d_skv, d_mq)
    g_qfv, g_hg_gl, g_sq, g_skv, g_mq = [
        wgrad(p, xb, kc=KC, name="grad_w_in_" + n) for p, n in zip(pieces, ("qfv", "hg_gates", "swa_q", "swa_kv", "mem_q"))]
    g_win_t = jnp.concatenate([col_major(g_qfv), g_hg_gl[:D_MODEL], g_sq, g_skv, g_mq, g_hg_gl[D_MODEL:]], axis=0)
    sent_win = send_win_grad(g_win_t, sent_small)
    return _grad_x(*pieces, head_major(win_t[:C_HG]), win_t, dx_part, sent_win, tm=T)


MESH = pl.DeviceIdType.MESH
ANY = pl.BlockSpec(memory_space=pl.ANY)


def _coords():
    return lax.axis_index("x"), lax.axis_index("y"), lax.axis_index("c")


def _other_chips(x, y):
    return [(1 - x, y), (x, 1 - y), (1 - x, 1 - y)]


def _all_gather_weights(*arrays):
    na = len(arrays)

    def body(*refs):
        srcs, dsts = refs[:na], refs[na:2 * na]
        send_sems, recv_sems, local_sems = refs[2 * na:]
        x, y, c = _coords()
        me, sibling = (x, y, c), (x, y, 1 - c)
        chips = _other_chips(x, y)

        def slot(a, px, py, pc):
            return dsts[a].at[4 * px + 2 * py + pc]

        def copy(a, k, block, to, from_shard=False):
            return pltpu.make_async_remote_copy(
                src_ref=srcs[a] if from_shard else slot(a, *block), dst_ref=slot(a, *block),
                send_sem=send_sems.at[a * 7 + k], recv_sem=recv_sems.at[a * 7 + k],
                device_id=to, device_id_type=MESH)

        own = [pltpu.make_async_copy(srcs[a], slot(a, *me), local_sems.at[a]) for a in range(na)]
        for cp in own:
            cp.start()
        first = []
        for a in range(na):
            first.append(copy(a, 0, me, sibling, True))
            first += [copy(a, 1 + j, me, (*chip, c), True) for j, chip in enumerate(chips)]
        for cp in first:
            cp.start()
        passed = []
        for j, chip in enumerate(chips):
            for a in range(na):
                copy(a, 1 + j, (*chip, c), me).wait_recv()
                fwd = copy(a, 4 + j, (*chip, c), sibling)
                fwd.start()
                passed.append(fwd)
        for a in range(na):
            copy(a, 0, sibling, me).wait_recv()
            for j, chip in enumerate(chips):
                copy(a, 4 + j, (*chip, 1 - c), me).wait_recv()
        for cp in first + passed:
            cp.wait_send()
        for cp in own:
            cp.wait()

    return pl.pallas_call(
        body,
        in_specs=[ANY] * na,
        out_specs=[ANY] * na,
        out_shape=[jax.ShapeDtypeStruct((N_DEV,) + a.shape, a.dtype) for a in arrays],
        scratch_shapes=[pltpu.SemaphoreType.DMA((7 * na,)), pltpu.SemaphoreType.DMA((7 * na,)),
                        pltpu.SemaphoreType.DMA((na,))],
        name="all_gather_weights",
    )(*arrays)


HBM = pl.BlockSpec(memory_space=pltpu.HBM)
SEM = pl.BlockSpec(memory_space=pltpu.SEMAPHORE)
_DATAFLOW = pltpu.SideEffectType.DATAFLOW_SIDE_EFFECTING


def _peer(x, y, c, r):
    return x ^ (r >> 2), y ^ ((r >> 1) & 1), c ^ (r & 1)


def _direct_copies(src_ref, land_ref, send_sems, recv_sems, gather, receiving):
    x, y, c = _coords()
    me = 4 * x + 2 * y + c
    copies = []
    for r in range(1, N_DEV):
        px, py, pc = _peer(x, y, c, r)
        peer = 4 * px + 2 * py + pc
        if gather:
            src, dst = src_ref, land_ref.at[peer if receiving else me]
        else:
            src, dst = src_ref.at[peer], land_ref.at[r - 1]
        copies.append(pltpu.make_async_remote_copy(
            src_ref=src, dst_ref=dst, send_sem=send_sems.at[r - 1], recv_sem=recv_sems.at[r - 1],
            device_id=(px, py, pc), device_id_type=MESH))
    return copies


def _direct_start(src, land, *, gather, name, after=None):
    def body(src_ref, land_ref, *rest):
        send_sems, recv_sems, token = rest[-5], rest[-4], rest[-1]
        for cp in _direct_copies(src_ref, land_ref, send_sems, recv_sems, gather, False):
            cp.start()
        token[...] = jnp.zeros_like(token)

    afters = () if after is None else (after,)
    return pl.pallas_call(
        body,
        name=name,
        out_shape=(pltpu.SemaphoreType.DMA((N_DEV - 1,)), pltpu.SemaphoreType.DMA((N_DEV - 1,)),
                   pltpu.HBM(src.shape, src.dtype), pltpu.HBM(land.shape, land.dtype),
                   jax.ShapeDtypeStruct((8, 128), F32)),
        in_specs=(HBM, HBM) + tuple(ANY for _ in afters),
        out_specs=(SEM, SEM, HBM, HBM, pl.BlockSpec(memory_space=pltpu.VMEM)),
        input_output_aliases={0: 2, 1: 3},
        compiler_params=pltpu.CompilerParams(has_side_effects=_DATAFLOW),
    )(pltpu.with_memory_space_constraint(src, pltpu.HBM), pltpu.with_memory_space_constraint(land, pltpu.HBM), *afters)


def _direct_wait(send_sems, recv_sems, src_thru, land_thru, after, *, gather, name):
    def body(src_ref, land_ref, send_sems_ref, recv_sems_ref, after_ref, src_dead, got_ref):
        del after_ref, src_dead, got_ref
        for cp in _direct_copies(src_ref, land_ref, send_sems_ref, recv_sems_ref, gather, True):
            cp.wait_send()
            cp.wait_recv()

    return pl.pallas_call(
        body,
        name=name,
        out_shape=(pltpu.HBM(src_thru.shape, src_thru.dtype), pltpu.HBM(land_thru.shape, land_thru.dtype)),
        in_specs=(HBM, HBM, SEM, SEM, ANY),
        out_specs=(HBM, HBM),
        input_output_aliases={0: 0, 1: 1},
        compiler_params=pltpu.CompilerParams(has_side_effects=_DATAFLOW),
    )(src_thru, land_thru, send_sems, recv_sems, after)


def _sum_partials(src, land, me, *, tr, name):
    R = src.shape[1]

    def body(me_ref, s_ref, l_ref, o_ref):
        del me_ref
        acc = s_ref[0].astype(F32)
        for r in range(N_DEV - 1):
            acc = acc + l_ref[r].astype(F32)
        o_ref[...] = acc

    return pl.pallas_call(
        body,
        grid_spec=pltpu.PrefetchScalarGridSpec(
            num_scalar_prefetch=1, grid=(R // tr,),
            in_specs=[pl.BlockSpec((1, tr, 1024), lambda i, mr: (mr[0], i, 0)),
                      pl.BlockSpec((N_DEV - 1, tr, 1024), lambda i, mr: (0, i, 0))],
            out_specs=pl.BlockSpec((tr, 1024), lambda i, mr: (i, 0))),
        out_shape=jax.ShapeDtypeStruct((R, 1024), F32),
        name=name,
    )(me, src, land)


_SMALL = ("lb_logits", "hg_norm_gain", "swa_sinks", "rel_bias", "ln1_g", "ln1_b", "ln2_g", "ln2_b")


def _pack_small_grads(d_lb, d_gain, d_sink, d_rb, d_ln1_g, d_ln1_b, d_ln2_g, d_ln2_b, loss):
    def body(lb_ref, gain_ref, sink_ref, rb_ref, l1g_ref, l1b_ref, l2g_ref, l2b_ref, loss_ref, o_ref):
        o_ref[...] = jnp.zeros_like(o_ref)
        for row, ref in ((SM_LB, lb_ref), (SM_GAIN, gain_ref), (SM_L1G, l1g_ref), (SM_L1B, l1b_ref),
                         (SM_L2G, l2g_ref), (SM_L2B, l2b_ref)):
            o_ref[row:row + 1, :] = ref[...]
        o_ref[SM_SINK:SM_SINK + 1, 0:128] = sink_ref[0:1, :]
        o_ref[SM_LOSS:SM_LOSS + 1, 0:128] = loss_ref[0:1, :]
        o_ref[SM_RB:SM_RB + NUM_BUCKETS, 0:128] = rb_ref[...]

    vm = pl.BlockSpec(memory_space=pltpu.VMEM)
    return pl.pallas_call(
        body,
        in_specs=[vm] * 9,
        out_specs=vm,
        out_shape=jax.ShapeDtypeStruct((SM_ROWS, D_MODEL), F32),
        name="pack_small_grads",
    )(d_lb, d_gain, d_sink, d_rb, d_ln1_g, d_ln1_b, d_ln2_g, d_ln2_b, loss)


def _small_finish(gathered, w, m, v):
    n = len(_SMALL)

    def body(*refs):
        g_ref = refs[0]
        w_refs, m_refs, v_refs = refs[1:1 + n], refs[1 + n:1 + 2 * n], refs[1 + 2 * n:1 + 3 * n]
        outs = refs[1 + 3 * n:]
        loss_ref, tot = outs[0], outs[-1]
        g_out, d_out, m_out, v_out = (outs[1 + k * n:1 + (k + 1) * n] for k in range(4))
        acc = g_ref[0]
        for d in range(1, N_DEV):
            acc = acc + g_ref[d]
        tot[...] = acc
        loss_ref[...] = tot[SM_LOSS:SM_LOSS + 1, 0:1]
        lb = _lower_bound(w_refs[0])
        dl0 = tot[SM_LB:SM_LB + 1, :] * lb * (1.0 - lb)
        grads = (jnp.concatenate([dl0, -dl0], axis=0), tot[SM_GAIN:SM_GAIN + 1, :],
                 tot[SM_SINK:SM_SINK + 1, 0:SWA_HEADS], tot[SM_RB:SM_RB + NUM_BUCKETS, 0:SWA_HEADS],
                 tot[SM_L1G:SM_L1G + 1, :], tot[SM_L1B:SM_L1B + 1, :], tot[SM_L2G:SM_L2G + 1, :], tot[SM_L2B:SM_L2B + 1, :])
        for k, g in enumerate(grads):
            g_out[k][...] = g
            d_out[k][...], m_out[k][...], v_out[k][...] = _adam_step(w_refs[k][...], g, m_refs[k][...], v_refs[k][...])

    vm = pl.BlockSpec(memory_space=pltpu.VMEM)
    shapes = [jax.ShapeDtypeStruct(w[k].shape, F32) for k in _SMALL]
    res = pl.pallas_call(
        body,
        in_specs=[vm] * (1 + 3 * n),
        out_specs=[vm] * (1 + 4 * n),
        out_shape=[jax.ShapeDtypeStruct((1, 1), F32)] + shapes * 4,
        scratch_shapes=[pltpu.VMEM((SM_ROWS, D_MODEL), F32)],
        name="small_finish",
    )(gathered, *[w[k] for k in _SMALL], *[m[k] for k in _SMALL], *[v[k] for k in _SMALL])
    parts = [dict(zip(_SMALL, res[1 + k * n:1 + (k + 1) * n])) for k in range(4)]
    return (res[0], *parts)


def _adam_step(w, g, m, v):
    nm = ADAM_B1 * m + (1.0 - ADAM_B1) * g
    nv = ADAM_B2 * v + (1.0 - ADAM_B2) * jnp.square(g)
    m_hat = nm / (1.0 - ADAM_B1 ** ADAM_STEP)
    v_hat = nv / (1.0 - ADAM_B2 ** ADAM_STEP)
    return -ADAM_LR * (m_hat / (jnp.sqrt(v_hat) + ADAM_EPS) + ADAM_WD * w), nm, nv


def _adamw(w, g, m, v, *, tr, name):
    R, C = w.shape

    def body(w_ref, g_ref, m_ref, v_ref, d_ref, nm_ref, nv_ref):
        d_ref[...], nm_ref[...], nv_ref[...] = _adam_step(w_ref[...], g_ref[...], m_ref[...], v_ref[...])

    spec = pl.BlockSpec((tr, C), lambda i: (i, 0))
    return pl.pallas_call(
        body,
        grid=(R // tr,),
        in_specs=[spec] * 4,
        out_specs=[spec] * 3,
        out_shape=[jax.ShapeDtypeStruct((R, C), F32)] * 3,
        compiler_params=_cparams("parallel"),
        name=name,
    )(w, g, m, v)


_WEIGHTS = ("w_in", "lb_logits", "hg_norm_gain", "swa_sinks", "rel_bias", "w_mem_kv", "w_branch_hg", "w_branch_swa",
            "w_branch_mem", "w_out", "ln1_g", "ln1_b", "w_up", "w_down", "ln2_g", "ln2_b")


def kernel(x, mem, w_in, lb_logits, hg_norm_gain, swa_sinks, rel_bias, w_mem_kv, w_branch_hg, w_branch_swa, w_branch_mem, w_out, ln1_g, ln1_b, w_up, w_down, ln2_g, ln2_b, loss_target, m_w_in, m_lb_logits, m_hg_norm_gain, m_swa_sinks, m_rel_bias, m_w_mem_kv, m_w_branch_hg, m_w_branch_swa, m_w_branch_mem, m_w_out, m_ln1_g, m_ln1_b, m_w_up, m_w_down, m_ln2_g, m_ln2_b, v_w_in, v_lb_logits, v_hg_norm_gain, v_swa_sinks, v_rel_bias, v_w_mem_kv, v_w_branch_hg, v_w_branch_swa, v_w_branch_mem, v_w_out, v_ln1_g, v_ln1_b, v_w_up, v_w_down, v_ln2_g, v_ln2_b):
    w = dict(w_in=w_in, lb_logits=lb_logits, hg_norm_gain=hg_norm_gain, swa_sinks=swa_sinks, rel_bias=rel_bias,
             w_mem_kv=w_mem_kv, w_branch_hg=w_branch_hg, w_branch_swa=w_branch_swa, w_branch_mem=w_branch_mem,
             w_out=w_out, ln1_g=ln1_g, ln1_b=ln1_b, w_up=w_up, w_down=w_down, ln2_g=ln2_g, ln2_b=ln2_b)
    mom = dict(w_in=m_w_in, lb_logits=m_lb_logits, hg_norm_gain=m_hg_norm_gain, swa_sinks=m_swa_sinks, rel_bias=m_rel_bias,
               w_mem_kv=m_w_mem_kv, w_branch_hg=m_w_branch_hg, w_branch_swa=m_w_branch_swa, w_branch_mem=m_w_branch_mem,
               w_out=m_w_out, ln1_g=m_ln1_g, ln1_b=m_ln1_b, w_up=m_w_up, w_down=m_w_down, ln2_g=m_ln2_g, ln2_b=m_ln2_b)
    var = dict(w_in=v_w_in, lb_logits=v_lb_logits, hg_norm_gain=v_hg_norm_gain, swa_sinks=v_swa_sinks, rel_bias=v_rel_bias,
               w_mem_kv=v_w_mem_kv, w_branch_hg=v_w_branch_hg, w_branch_swa=v_w_branch_swa, w_branch_mem=v_w_branch_mem,
               w_out=v_w_out, ln1_g=v_ln1_g, ln1_b=v_ln1_b, w_up=v_w_up, w_down=v_w_down, ln2_g=v_ln2_g, ln2_b=v_ln2_b)
    xc, yc, cc = _coords()

    p1 = _bf(w_in[0].T)
    p2 = _bf(jnp.concatenate([w_down[0], w_up[0].T, w_branch_hg[0], w_branch_swa[0], w_branch_mem[0], w_out[0],
                              w_mem_kv[0].T], axis=0))
    me = 4 * xc + 2 * yc + cc
    (g1,) = _all_gather_weights(p1)
    land2 = lax.dynamic_update_slice(lax.empty((N_DEV, R_OTHER, D_MODEL), BF16), p2[None], (me, 0, 0))
    ag2 = _direct_start(p2, land2, gather=True, name="gather_other_weights_start")

    def other_weights(after):
        return _direct_wait(*ag2[:4], after, gather=True, name="gather_other_weights_wait")[1]

    blocks = lambda a: a.reshape(N_DEV, a.shape[0] // N_DEV, D_MODEL)
    started = {}

    def send_other_grads(g):
        part = jnp.concatenate([blocks(g[k]) for k in ("wdn", "wup_t", "wbh", "wbs", "wbm", "wout", "wkv_t")], axis=1)
        started["others"] = _direct_start(part, lax.empty((N_DEV - 1, R_OTHER, D_MODEL), BF16), gather=False,
                                          name="scatter_other_grads_start")
        return started["others"][4]

    me1 = me.reshape(1).astype(jnp.int32)
    grads, delta, new_m, new_v = {}, {}, {}, {}

    def adamw(name):
        w2 = w[name][0]
        delta[name], new_m[name], new_v[name] = _adamw(
            w2, grads[name], mom[name][0], var[name][0], tr=w2.shape[0] // 4, name="adamw_" + name)

    def send_small_grads(packed):
        land = lax.dynamic_update_slice(lax.empty((N_DEV, SM_ROWS, D_MODEL), F32), packed[None], (me, 0, 0))
        started["small"] = _direct_start(packed, land, gather=True, name="gather_small_grads_start")
        return started["small"][4]

    def send_win_grad(g, after):
        started["win"] = _direct_start(blocks(g), lax.empty((N_DEV - 1, IN_SHARD, D_MODEL), BF16), gather=False,
                                       name="scatter_w_in_grad_start", after=after)
        mine2, landed2 = _direct_wait(*started["others"][:4], started["win"][4], gather=False,
                                      name="scatter_other_grads_wait")
        gs2 = _sum_partials(mine2, landed2, me1, tr=R_OTHER // 2, name="sum_other_grads")
        grads.update(
            w_down=gs2[R_DN:R_UP], w_up=gs2[R_UP:R_BH].T, w_branch_hg=gs2[R_BH:R_BS], w_branch_swa=gs2[R_BS:R_BM],
            w_branch_mem=gs2[R_BM:R_OUT], w_out=gs2[R_OUT:R_KV], w_mem_kv=gs2[R_KV:R_OTHER].T)
        for name in ("w_mem_kv", "w_branch_hg", "w_branch_swa", "w_branch_mem", "w_out", "w_up", "w_down"):
            adamw(name)
        return tuple(new_v[name] for name in new_v)

    grad_x = _local_step(
        x[0], mem[0], loss_target[0], lb_logits, hg_norm_gain, swa_sinks, rel_bias, ln1_g, ln1_b, ln2_g, ln2_b,
        g1.reshape(IN_COLS, D_MODEL), ag2[4], other_weights, send_other_grads, send_small_grads, send_win_grad)

    mine1, landed1 = _direct_wait(*started["win"][:4], grad_x, gather=False, name="scatter_w_in_grad_wait")
    g_win_t = _sum_partials(mine1, landed1, me1, tr=IN_SHARD // 2, name="sum_w_in_grad")
    d_t, m_t, v_t = _adamw(w_in[0].T, g_win_t, m_w_in[0].T, v_w_in[0].T, tr=IN_SHARD // 4, name="adamw_w_in")
    grads["w_in"], delta["w_in"], new_m["w_in"], new_v["w_in"] = g_win_t.T, d_t.T, m_t.T, v_t.T

    _, gathered = _direct_wait(*started["small"][:4], grad_x, gather=True, name="gather_small_grads_wait")
    loss, g_s, d_s, m_s, v_s = _small_finish(gathered, w, mom, var)
    for dst, src in ((grads, g_s), (delta, d_s), (new_m, m_s), (new_v, v_s)):
        dst.update(src)

    def shaped(d, name):
        return d[name].reshape(w[name].shape)

    return (loss.reshape(()), grad_x[None], *[shaped(grads, n) for n in _WEIGHTS], *[shaped(delta, n) for n in _WEIGHTS],
            *[shaped(new_m, n) for n in _WEIGHTS], *[shaped(new_v, n) for n in _WEIGHTS])
```

```python
import functools
import math

import jax
import jax.numpy as jnp
from jax import lax
from jax.experimental import pallas as pl
from jax.experimental.pallas import tpu as pltpu

F32 = jnp.float32
BF16 = jnp.bfloat16

D_MODEL = 1024
MEM_LEN = 256
HG_HEADS = 8
HG_DK = 128
HG_CHUNK = 64
SWA_HEADS = 16
SWA_HEAD_DIM = 64
SWA_BLOCK = 128
SWA_WINDOW = 128
MEM_HEADS = 4
MEM_HEAD_DIM = 256
NUM_BUCKETS = 32
MAX_DISTANCE = 128
D_FF = 4096
LN_EPS = 1e-5
RMS_EPS = 1e-6
ALPHA = 2.0 ** 0.25
N_DEV = 8

C_HQ, C_HF, C_HI, C_HG, C_SQ, C_SK, C_SV, C_MQ, C_GL = 0, 1024, 2048, 3072, 4096, 5120, 5248, 5376, 6400
IN_COLS = 9472
IN_SHARD = IN_COLS // N_DEV

ADAM_LR = 0.001
ADAM_B1 = 0.9
ADAM_B2 = 0.999
ADAM_EPS = 1e-08
ADAM_WD = 0.01
ADAM_STEP = 10

VMEM_LIMIT = 58 * 1024 * 1024

R_DN, R_UP, R_BH, R_BS, R_BM, R_OUT, R_KV, R_OTHER = 0, 512, 1024, 1152, 1280, 1408, 1536, 1792

SM_LB, SM_GAIN, SM_SINK, SM_L1G, SM_L1B, SM_L2G, SM_L2B, SM_LOSS, SM_RB, SM_ROWS = 0, 2, 3, 4, 5, 6, 7, 8, 16, 48


def _bf(v):
    return v.astype(BF16)


def _f32(v):
    return v.astype(F32)


def _dot(a, b):
    return jnp.dot(a, b, preferred_element_type=F32)


def _dot_nt(a, b):
    return lax.dot_general(a, b, (((1,), (1,)), ((), ())), preferred_element_type=F32)


def _dot_tn(a, b):
    return lax.dot_general(a, b, (((0,), (0,)), ((), ())), preferred_element_type=F32)


def _sig(v):
    return 1.0 / (1.0 + jnp.exp(-v))


def _cparams(*sem):
    return pltpu.CompilerParams(dimension_semantics=sem, vmem_limit_bytes=VMEM_LIMIT)


def _const_spec(shape):
    nd = len(shape)
    return pl.BlockSpec(shape, lambda *_: (0,) * nd, pipeline_mode=pl.Buffered(1))


def _dep_spec():
    return pl.BlockSpec((8, 128), lambda *_: (0, 0))


def _in_proj(x, win_t, dep, *, tm):
    S = x.shape[0]

    def body(x_ref, w_ref, dep_ref, z_ref, gl_ref, xb_ref):
        del dep_ref
        xb = _bf(x_ref[...])
        xb_ref[...] = xb
        for c0 in range(0, C_GL, 1280):
            z_ref[:, c0:c0 + 1280] = _bf(_dot_nt(xb, w_ref[c0:c0 + 1280, :]))
        for c0 in range(0, IN_COLS - C_GL, 1024):
            gl_ref[:, c0:c0 + 1024] = _bf(_dot_nt(xb, w_ref[C_GL + c0:C_GL + c0 + 1024, :]))

    row = lambda w: pl.BlockSpec((tm, w), lambda i: (i, 0))
    return pl.pallas_call(
        body,
        grid=(S // tm,),
        in_specs=[row(D_MODEL), _const_spec(win_t.shape), _dep_spec()],
        out_specs=[row(C_GL), row(IN_COLS - C_GL), row(D_MODEL)],
        out_shape=[jax.ShapeDtypeStruct((S, C_GL), BF16), jax.ShapeDtypeStruct((S, IN_COLS - C_GL), BF16),
                   jax.ShapeDtypeStruct((S, D_MODEL), BF16)],
        compiler_params=_cparams("parallel"),
        name="in_proj",
    )(x, win_t, dep)


def _mm_tn_resident(a, b, *, tm, kc, name, out_dtype):
    K, M = a.shape
    N = b.shape[1]
    nk = K // kc

    def body(a_ref, b_ref, o_ref):
        acc = jnp.zeros((tm, N), F32)
        for kk in range(nk):
            sl = pl.ds(kk * kc, kc)
            acc = acc + _dot_tn(_bf(a_ref[sl, :]), _bf(b_ref[sl, :]))
        o_ref[...] = acc.astype(o_ref.dtype)

    return pl.pallas_call(
        body,
        grid=(M // tm,),
        in_specs=[pl.BlockSpec((K, tm), lambda i: (0, i)), _const_spec((K, N))],
        out_specs=pl.BlockSpec((tm, N), lambda i: (i, 0)),
        out_shape=jax.ShapeDtypeStruct((M, N), out_dtype),
        compiler_params=_cparams("parallel"),
        name=name,
    )(a, b)


def _mm_tn(a, b, *, kc, name, out_dtype=F32):
    K, M = a.shape
    N = b.shape[1]
    if M > 1024:
        return _mm_tn_resident(a, b, tm=256, kc=min(kc, 1024), name=name, out_dtype=out_dtype)
    tm = M
    nk = K // kc

    def body(a_ref, b_ref, o_ref, acc):
        k = pl.program_id(1)
        part = _dot_tn(_bf(a_ref[...]), _bf(b_ref[...]))

        @pl.when(k == 0)
        def _():
            acc[...] = part

        @pl.when(k > 0)
        def _():
            acc[...] += part

        @pl.when(k == nk - 1)
        def _():
            o_ref[...] = acc[...].astype(o_ref.dtype)

    return pl.pallas_call(
        body,
        grid=(M // tm, nk),
        in_specs=[pl.BlockSpec((kc, tm), lambda i, k: (k, i)), pl.BlockSpec((kc, N), lambda i, k: (k, 0))],
        out_specs=pl.BlockSpec((tm, N), lambda i, k: (i, 0)),
        out_shape=jax.ShapeDtypeStruct((M, N), out_dtype),
        scratch_shapes=[pltpu.VMEM((tm, N), F32)],
        compiler_params=_cparams("parallel", "arbitrary"),
        name=name,
    )(a, b)


def _grad_x(d_qfv, d_hg_gl, d_sq, d_skv, d_mq, w_qfv, win_t, add, deps, *, tm):
    M = add.shape[0]
    pieces = (d_qfv, d_hg_gl, d_sq, d_skv, d_mq)

    def body(qfv_ref, hggl_ref, sq_ref, skv_ref, mq_ref, wq_ref, w_ref, add_ref, *rest):
        o_ref = rest[-1]
        acc = add_ref[...] + _dot(qfv_ref[...], wq_ref[...])
        acc = acc + _dot(hggl_ref[:, 0:1024], w_ref[C_HG:C_SQ, :])
        acc = acc + _dot(hggl_ref[:, 1024:4096], w_ref[C_GL:IN_COLS, :])
        acc = acc + _dot(sq_ref[...], w_ref[C_SQ:C_SK, :])
        acc = acc + _dot(skv_ref[...], w_ref[C_SK:C_MQ, :])
        o_ref[...] = acc + _dot(mq_ref[...], w_ref[C_MQ:C_GL, :])

    return pl.pallas_call(
        body,
        grid=(M // tm,),
        in_specs=[pl.BlockSpec((tm, p.shape[1]), lambda i: (i, 0)) for p in pieces]
        + [_const_spec(w_qfv.shape), _const_spec(win_t.shape), pl.BlockSpec((tm, D_MODEL), lambda i: (i, 0))]
        + [_dep_spec() for _ in deps],
        out_specs=pl.BlockSpec((tm, D_MODEL), lambda i: (i, 0)),
        out_shape=jax.ShapeDtypeStruct((M, D_MODEL), F32),
        compiler_params=_cparams("parallel"),
        name="grad_x",
    )(*pieces, w_qfv, win_t, add, *deps)


def _lower_bound(lbl_ref):
    l0 = lbl_ref[0:1, :]
    l1 = lbl_ref[1:2, :]
    mx = jnp.maximum(l0, l1)
    e0 = jnp.exp(l0 - mx)
    e1 = jnp.exp(l1 - mx)
    return e0 / (e0 + e1)


def _tri(lower):
    r = lax.broadcasted_iota(jnp.int32, (HG_CHUNK, HG_CHUNK), 0)
    c = lax.broadcasted_iota(jnp.int32, (HG_CHUNK, HG_CHUNK), 1)
    return (r >= c) if lower else (r <= c)


def _hg_gates(fl, lb):
    sg = _sig(fl)
    f = lb + (1.0 - lb) * sg
    return sg, f, jnp.log(f), 1.0 - f


def _scan_rows(v, reverse=False):
    row = lax.broadcasted_iota(jnp.int32, v.shape, 0)
    s = 1
    while s < HG_CHUNK:
        if reverse:
            v = v + jnp.where(row < HG_CHUNK - s, pltpu.roll(v, HG_CHUNK - s, 0), 0.0)
        else:
            v = v + jnp.where(row >= s, pltpu.roll(v, s, 0), 0.0)
        s *= 2
    return v


def _hgrn_fwd(zmain, lb_logits, *, T):
    S = zmain.shape[0]
    nc = T // HG_CHUNK

    def body(q_ref, f_ref, v_ref, lbl_ref, o_ref, st_ref, state):
        @pl.when(pl.program_id(1) == 0)
        def _():
            state[...] = jnp.zeros_like(state)

        lb = _lower_bound(lbl_ref)
        tril = _tri(True)
        qis, updates, decays, intra = [], [], [], []
        for c in range(nc):
            sl = pl.ds(c * HG_CHUNK, HG_CHUNK)
            _, _, g, k = _hg_gates(_f32(f_ref[sl, :]), lb)
            b = _scan_rows(g)
            bl = jnp.sum(g, axis=0, keepdims=True)
            qi = _bf(_f32(q_ref[sl, :]) * jnp.exp(b))
            ki = _bf(k * jnp.exp(-b))
            ko = _bf(k * jnp.exp(bl - b))
            vb = _bf(v_ref[sl, :])
            att = jnp.where(tril, _dot_nt(qi, ki), 0.0)
            intra.append(_dot(_bf(att), vb))
            qis.append(qi)
            updates.append(_dot_tn(vb, ko))
            decays.append(jnp.exp(bl))
        st = state[...]
        for c in range(nc):
            st_ref[0, c] = st
            o_ref[pl.ds(c * HG_CHUNK, HG_CHUNK), :] = intra[c] + _dot_nt(qis[c], _bf(st))
            st = st * decays[c] + updates[c]
        state[...] = st

    col = lambda base: pl.BlockSpec((T, HG_DK), lambda h, t: (t, base + h))
    return pl.pallas_call(
        body,
        grid=(HG_HEADS, S // T),
        in_specs=[col(0), col(8), col(16), pl.BlockSpec((2, HG_DK), lambda h, t: (0, h))],
        out_specs=[
            pl.BlockSpec((T, HG_DK), lambda h, t: (t, h)),
            pl.BlockSpec((1, nc, HG_DK, HG_DK), lambda h, t: (h, t, 0, 0)),
        ],
        out_shape=[
            jax.ShapeDtypeStruct((S, D_MODEL), F32),
            jax.ShapeDtypeStruct((HG_HEADS, S // HG_CHUNK, HG_DK, HG_DK), F32),
        ],
        scratch_shapes=[pltpu.VMEM((HG_DK, HG_DK), F32)],
        compiler_params=_cparams("parallel", "arbitrary"),
        name="hgrn_fwd",
    )(zmain, zmain, zmain, lb_logits)


def _hgrn_bwd(zmain, lb_logits, states, d_o, *, T):
    S = zmain.shape[0]
    nc = T // HG_CHUNK
    nt = S // T

    def body(q_ref, f_ref, v_ref, lbl_ref, st_ref, do_ref, dz_ref, dlb_ref, dstate):
        @pl.when(pl.program_id(1) == 0)
        def _():
            dstate[...] = jnp.zeros_like(dstate)
            dlb_ref[...] = jnp.zeros_like(dlb_ref)

        lb = _lower_bound(lbl_ref)
        tril = _tri(True)
        last_row = lax.broadcasted_iota(jnp.int32, (HG_CHUNK, HG_DK), 0) == HG_CHUNK - 1
        saved = []
        for c in range(nc):
            sl = pl.ds(c * HG_CHUNK, HG_CHUNK)
            sg, f, g, k = _hg_gates(_f32(f_ref[sl, :]), lb)
            b = _scan_rows(g)
            bl = jnp.sum(g, axis=0, keepdims=True)
            eb = jnp.exp(b)
            enb = jnp.exp(-b)
            eo = jnp.exp(bl - b)
            q_in = _f32(q_ref[sl, :]) * eb
            k_in = k * enb
            k_out = k * eo
            qi, ki, ko = _bf(q_in), _bf(k_in), _bf(k_out)
            vb = _bf(v_ref[sl, :])
            dob = do_ref[sl, :]
            att = jnp.where(tril, _dot_nt(qi, ki), 0.0)
            d_att = _bf(jnp.where(tril, _dot_nt(dob, vb), 0.0))
            d_kin = _dot_tn(d_att, qi)
            saved.append(dict(
                sg=sg, f=f, eb=eb, enb=enb, eo=eo, ebl=jnp.exp(bl), k_out=k_out, ko=ko, vb=vb, dob=dob,
                d_v=_dot_tn(_bf(att), dob), d_qin=_dot(d_att, ki), d_kin=d_kin,
                qk=(q_in, k_in), d_state=_dot_tn(dob, qi)))
        dst = dstate[...]
        dsts = [None] * nc
        for c in reversed(range(nc)):
            dsts[c] = dst
            dst = dst * saved[c]["ebl"] + saved[c]["d_state"]
        dstate[...] = dst
        dlb = jnp.zeros((1, HG_DK), F32)
        for c in range(nc):
            sl = pl.ds(c * HG_CHUNK, HG_CHUNK)
            s = saved[c]
            q_in, k_in = s["qk"]
            st = st_ref[0, c]
            dstb = _bf(dsts[c])
            d_v = s["d_v"] + _dot_nt(s["ko"], dstb)
            d_qin = s["d_qin"] + _dot(s["dob"], _bf(st))
            d_kout = _dot(s["vb"], dstb)
            d_decay = jnp.sum(dsts[c] * st, axis=0, keepdims=True)
            kk = d_kout * s["k_out"]
            d_b = d_qin * q_in - s["d_kin"] * k_in - kk
            d_bl = jnp.sum(kk, axis=0, keepdims=True) + d_decay * s["ebl"]
            d_g = _scan_rows(d_b + jnp.where(last_row, d_bl, 0.0), reverse=True)
            d_f = d_g / s["f"] - (s["d_kin"] * s["enb"] + d_kout * s["eo"])
            dz_ref[sl, 0:HG_DK] = _bf(d_qin * s["eb"])
            dz_ref[sl, HG_DK:2 * HG_DK] = _bf(d_f * (1.0 - lb) * s["sg"] * (1.0 - s["sg"]))
            dz_ref[sl, 2 * HG_DK:3 * HG_DK] = _bf(d_v)
            dlb = dlb + jnp.sum(d_f * (1.0 - s["sg"]), axis=0, keepdims=True)
        dlb_ref[...] += dlb

    rev = lambda base: pl.BlockSpec((T, HG_DK), lambda h, t: (nt - 1 - t, base + h))
    outc = pl.BlockSpec((T, HG_DK), lambda h, t: (nt - 1 - t, h))
    return pl.pallas_call(
        body,
        grid=(HG_HEADS, nt),
        in_specs=[
            rev(0), rev(8), rev(16),
            pl.BlockSpec((2, HG_DK), lambda h, t: (0, h)),
            pl.BlockSpec((1, nc, HG_DK, HG_DK), lambda h, t: (h, nt - 1 - t, 0, 0)),
            outc,
        ],
        out_specs=[pl.BlockSpec((T, 3 * HG_DK), lambda h, t: (nt - 1 - t, h)),
                   pl.BlockSpec((1, HG_DK), lambda h, t: (0, h))],
        out_shape=[jax.ShapeDtypeStruct((S, 3 * D_MODEL), BF16), jax.ShapeDtypeStruct((1, D_MODEL), F32)],
        scratch_shapes=[pltpu.VMEM((HG_DK, HG_DK), F32)],
        compiler_params=_cparams("parallel", "arbitrary"),
        name="hgrn_bwd",
    )(zmain, zmain, zmain, lb_logits, states, d_o)


def _t5_bucket_table():
    qi = jnp.arange(SWA_BLOCK)[:, None] + SWA_BLOCK
    kj = jnp.arange(2 * SWA_BLOCK)[None, :]
    n = jnp.clip(qi - kj, 0, SWA_WINDOW - 1)
    max_exact = NUM_BUCKETS // 2
    nf = jnp.maximum(n, 1).astype(F32)
    large = max_exact + (jnp.log(nf / max_exact) / math.log(MAX_DISTANCE / max_exact)
                         * (NUM_BUCKETS - max_exact)).astype(jnp.int32)
    large = jnp.minimum(large, NUM_BUCKETS - 1)
    return jnp.where(n < max_exact, n, large).astype(jnp.int32)


SWA_ROWS = 32


def _swa_bias_init(bias, bucket_ref, rb_ref):
    bk = bucket_ref[...]
    qi = lax.broadcasted_iota(jnp.int32, bk.shape, 0) + SWA_BLOCK
    kj = lax.broadcasted_iota(jnp.int32, bk.shape, 1)
    band = (qi - kj >= 0) & (qi - kj < SWA_WINDOW)
    for h in range(SWA_HEADS):
        def sel(b, acc, h=h):
            return jnp.where(bk == b, rb_ref[b, h], acc)
        t = lax.fori_loop(0, NUM_BUCKETS, sel, jnp.zeros(bk.shape, F32))
        bias[1, h] = jnp.where(band, t, -jnp.inf)
        bias[0, h] = jnp.where(band & (kj >= SWA_BLOCK), t, -jnp.inf)


def _lane_halves(t, kv_head):
    lane = lax.broadcasted_iota(jnp.int32, t.shape, 1)
    rolled = pltpu.roll(t, 64, 1)
    zero = jnp.zeros_like(t)
    if kv_head == 0:
        return jnp.where(lane < 64, t, zero), jnp.where(lane >= 64, rolled, zero)
    return jnp.where(lane < 64, rolled, zero), jnp.where(lane >= 64, t, zero)


def _swa_zero_key0(t):
    return jnp.where(lax.broadcasted_iota(jnp.int32, t.shape, 0) == 0, jnp.zeros_like(t), t)


def _swa_probs(s, masked_bias, sink):
    s = s + masked_bias
    m = jnp.maximum(jnp.max(s, axis=-1, keepdims=True), sink)
    p = jnp.exp(s - m)
    es = jnp.exp(sink - m)
    inv = 1.0 / (jnp.sum(p, axis=-1, keepdims=True) + es)
    return p * inv, es * inv


def _swa_fwd(zmain, bucket, rel_bias, sinks):
    S = zmain.shape[0]
    nb = S // SWA_BLOCK
    scale = SWA_HEAD_DIM ** -0.5

    def body(q_ref, kvc_ref, kvp_ref, bucket_ref, rb_ref, sk_ref, o_ref, p_ref, bias):
        n = pl.program_id(0)

        @pl.when(n == 0)
        def _():
            _swa_bias_init(bias, bucket_ref, rb_ref)

        later = jnp.minimum(n, 1)
        kk = _bf(jnp.concatenate([kvp_ref[:, 0:128], kvc_ref[:, 0:128]], axis=0))
        vv = _swa_zero_key0(_bf(jnp.concatenate([kvp_ref[:, 128:256], kvc_ref[:, 128:256]], axis=0)))
        first_col = lax.broadcasted_iota(jnp.int32, (SWA_ROWS, 2 * SWA_BLOCK), 1) == 0
        for kvh in range(2):
            ka, kb = _lane_halves(kk, kvh)
            va, vb = _lane_halves(vv, kvh)
            qst = _bf(jnp.concatenate([q_ref[:, pl.ds((kvh * 4 + jj) * 128, 128)] for jj in range(4)], axis=0) * scale)
            probs = []
            for odd, kx in enumerate((ka, kb)):
                s = _dot_nt(qst, kx)
                parts = []
                for jj in range(4):
                    h = 2 * (kvh * 4 + jj) + odd
                    for r0 in range(0, SWA_BLOCK, SWA_ROWS):
                        p, ps = _swa_probs(s[jj * SWA_BLOCK + r0:jj * SWA_BLOCK + r0 + SWA_ROWS],
                                           bias[later, h, pl.ds(r0, SWA_ROWS), :], sk_ref[0, h])
                        part = _bf(jnp.where(first_col, ps, p))
                        p_ref[pl.ds(r0, SWA_ROWS), pl.ds(h * 2 * SWA_BLOCK, 2 * SWA_BLOCK)] = part
                        parts.append(part)
                probs.append(jnp.concatenate(parts, axis=0))
            ost = _dot(probs[0], va) + _dot(probs[1], vb)
            for jj in range(4):
                o_ref[:, pl.ds((kvh * 4 + jj) * 128, 128)] = ost[jj * SWA_BLOCK:(jj + 1) * SWA_BLOCK]

    smem = pl.BlockSpec(memory_space=pltpu.SMEM)
    return pl.pallas_call(
        body,
        grid=(nb,),
        in_specs=[
            pl.BlockSpec((SWA_BLOCK, 1024), lambda n: (n, C_SQ // 1024)),
            pl.BlockSpec((SWA_BLOCK, 256), lambda n: (n, C_SK // 256)),
            pl.BlockSpec((SWA_BLOCK, 256), lambda n: (jnp.maximum(n - 1, 0), C_SK // 256)),
            _const_spec((SWA_BLOCK, 2 * SWA_BLOCK)), smem, smem,
        ],
        out_specs=[pl.BlockSpec((SWA_BLOCK, 1024), lambda n: (n, 0)),
                   pl.BlockSpec((SWA_BLOCK, SWA_HEADS * 2 * SWA_BLOCK), lambda n: (n, 0))],
        out_shape=[jax.ShapeDtypeStruct((S, 1024), F32),
                   jax.ShapeDtypeStruct((S, SWA_HEADS * 2 * SWA_BLOCK), BF16)],
        scratch_shapes=[pltpu.VMEM((2, SWA_HEADS, SWA_BLOCK, 2 * SWA_BLOCK), F32)],
        compiler_params=_cparams("arbitrary"),
        name="swa_fwd",
    )(zmain, zmain, zmain, bucket, rel_bias, sinks)


def _swa_bwd(zmain, o_b, probs, d_o, bucket, dep):
    S = zmain.shape[0]
    nb = S // SWA_BLOCK
    scale = SWA_HEAD_DIM ** -0.5

    def body(q_ref, kvc_ref, kvp_ref, o_ref, p_ref, do_ref, bucket_ref, dep_ref,
             dq_ref, dkv_ref, drb_ref, dsk_ref, dbias, carry):
        del dep_ref
        n = pl.program_id(0)

        @pl.when(n == 0)
        def _():
            dbias[...] = jnp.zeros_like(dbias)
            carry[...] = jnp.zeros_like(carry)

        @pl.when(n < nb)
        def _():
            kk = _swa_zero_key0(_bf(jnp.concatenate([kvp_ref[:, 0:128], kvc_ref[:, 0:128]], axis=0)))
            vv = _swa_zero_key0(_bf(jnp.concatenate([kvp_ref[:, 128:256], kvc_ref[:, 128:256]], axis=0)))
            lane = lax.broadcasted_iota(jnp.int32, (2 * SWA_BLOCK, 128), 1)
            lane_q = lax.broadcasted_iota(jnp.int32, (4 * SWA_BLOCK, 128), 1)
            dk_parts, dv_parts = [], []
            for kvh in range(2):
                ka, kb = _lane_halves(kk, kvh)
                va, vb = _lane_halves(vv, kvh)
                pair_cols = [pl.ds((kvh * 4 + jj) * 128, 128) for jj in range(4)]
                qst = _bf(jnp.concatenate([q_ref[:, cl] for cl in pair_cols], axis=0) * scale)
                dost = jnp.concatenate([do_ref[:, cl] for cl in pair_cols], axis=0)
                prod = dost.astype(F32) * jnp.concatenate([o_ref[:, cl] for cl in pair_cols], axis=0)
                dq_st = jnp.zeros((4 * SWA_BLOCK, 128), F32)
                zks, zvs = [], []
                for odd, (kx, vx) in enumerate(((ka, va), (kb, vb))):
                    keep = (lane_q >= 64) if odd else (lane_q < 64)
                    delta = jnp.sum(jnp.where(keep, prod, 0.0), axis=-1, keepdims=True)
                    dp = _dot_nt(dost, vx)
                    p_parts, ds_parts = [], []
                    for jj in range(4):
                        h = 2 * (kvh * 4 + jj) + odd
                        rows = slice(jj * SWA_BLOCK, (jj + 1) * SWA_BLOCK)
                        p = p_ref[:, pl.ds(h * 2 * SWA_BLOCK, 2 * SWA_BLOCK)]
                        ds = _f32(p) * (dp[rows] - delta[rows])
                        dbias[h] += ds
                        p_parts.append(p)
                        ds_parts.append(_bf(ds))
                    pst = jnp.concatenate(p_parts, axis=0)
                    dsst = jnp.concatenate(ds_parts, axis=0)
                    dq_st = dq_st + _dot(dsst, kx)
                    zks.append(_dot_tn(dsst, qst))
                    zvs.append(_dot_tn(pst, dost))
                for jj in range(4):
                    dq_ref[:, pair_cols[jj]] = _bf(dq_st[jj * SWA_BLOCK:(jj + 1) * SWA_BLOCK] * scale)
                zk = jnp.where(lane < 64, zks[0], zks[1])
                zv = jnp.where(lane < 64, zvs[0], zvs[1])
                dk_parts.append(zk + pltpu.roll(zk, 64, 1))
                dv_parts.append(zv + pltpu.roll(zv, 64, 1))
            dk = jnp.where(lane < 64, dk_parts[0], dk_parts[1])
            dv = jnp.where(lane < 64, dv_parts[0], dv_parts[1])
            dkv = _swa_zero_key0(jnp.concatenate([dk, dv], axis=1))
            dkv_ref[...] = _bf(carry[...] + dkv[0:SWA_BLOCK])
            carry[...] = dkv[SWA_BLOCK:]

        @pl.when(n == nb)
        def _():
            dkv_ref[...] = _bf(carry[...])
            first_col = lax.broadcasted_iota(jnp.int32, (SWA_BLOCK, 2 * SWA_BLOCK), 1) == 0
            bk = jnp.where(first_col, -1, bucket_ref[...])

            row = lax.broadcasted_iota(jnp.int32, (NUM_BUCKETS, 128), 0)
            lane = lax.broadcasted_iota(jnp.int32, (NUM_BUCKETS, 128), 1)

            def total(v):
                return jnp.sum(jnp.sum(v, axis=1, keepdims=True), axis=0, keepdims=True)

            def per_head(h, acc):
                db = dbias[h]
                d_rb, d_sk = acc
                d_sk = d_sk + jnp.where((row == 0) & (lane == h), total(jnp.where(first_col, db, 0.0)), 0.0)

                def per_bucket(b, d_rb):
                    return d_rb + jnp.where((row == b) & (lane == h), total(jnp.where(bk == b, db, 0.0)), 0.0)

                return lax.fori_loop(0, NUM_BUCKETS, per_bucket, d_rb), d_sk

            zero = jnp.zeros((NUM_BUCKETS, 128), F32)
            d_rb, d_sk = lax.fori_loop(0, SWA_HEADS, per_head, (zero, zero))
            drb_ref[...] = d_rb
            dsk_ref[...] = d_sk[0:8]

    cur = lambda n: jnp.minimum(n, nb - 1)
    prev = lambda n: jnp.maximum(jnp.minimum(n, nb - 1) - 1, 0)
    return pl.pallas_call(
        body,
        grid=(nb + 1,),
        in_specs=[
            pl.BlockSpec((SWA_BLOCK, 1024), lambda n: (cur(n), C_SQ // 1024)),
            pl.BlockSpec((SWA_BLOCK, 256), lambda n: (cur(n), C_SK // 256)),
            pl.BlockSpec((SWA_BLOCK, 256), lambda n: (prev(n), C_SK // 256)),
            pl.BlockSpec((SWA_BLOCK, 1024), lambda n: (cur(n), 0)),
            pl.BlockSpec((SWA_BLOCK, SWA_HEADS * 2 * SWA_BLOCK), lambda n: (cur(n), 0)),
            pl.BlockSpec((SWA_BLOCK, 1024), lambda n: (cur(n), 0)),
            _const_spec((SWA_BLOCK, 2 * SWA_BLOCK)), _dep_spec(),
        ],
        out_specs=[
            pl.BlockSpec((SWA_BLOCK, 1024), lambda n: (cur(n), 0)),
            pl.BlockSpec((SWA_BLOCK, 256), lambda n: (jnp.maximum(n - 1, 0), 0)),
            pl.BlockSpec((NUM_BUCKETS, 128), lambda n: (0, 0)),
            pl.BlockSpec((8, 128), lambda n: (0, 0)),
        ],
        out_shape=[
            jax.ShapeDtypeStruct((S, 1024), BF16),
            jax.ShapeDtypeStruct((S, 256), BF16),
            jax.ShapeDtypeStruct((NUM_BUCKETS, 128), F32),
            jax.ShapeDtypeStruct((8, 128), F32),
        ],
        scratch_shapes=[
            pltpu.VMEM((SWA_HEADS, SWA_BLOCK, 2 * SWA_BLOCK), F32),
            pltpu.VMEM((SWA_BLOCK, 256), F32),
        ],
        compiler_params=_cparams("arbitrary"),
        name="swa_bwd",
    )(zmain, zmain, zmain, o_b, probs, d_o, bucket, dep)


def _mem_probs(q_ref, k):
    qs = _bf(q_ref[...] * (MEM_HEAD_DIM ** -0.5))
    s = _dot_nt(qs, k)
    e = jnp.exp(s - jnp.max(s, axis=-1, keepdims=True))
    return qs, e / jnp.sum(e, axis=-1, keepdims=True)


def _mem_q_specs(T):
    return [pl.BlockSpec((T, MEM_HEAD_DIM), lambda t, h=h: (t, C_MQ // MEM_HEAD_DIM + h)) for h in range(MEM_HEADS)]


def _mem_kv_proj(mem, g2):
    def body(mem_ref, w_ref, o_ref):
        o_ref[...] = _dot_nt(_bf(mem_ref[...]), _rows(w_ref))

    return pl.pallas_call(
        body,
        grid=(1,),
        in_specs=[pl.BlockSpec((MEM_LEN, D_MODEL), lambda i: (0, 0)), _gathered_spec(R_KV, R_OTHER)],
        out_specs=pl.BlockSpec((MEM_LEN, 2048), lambda i: (0, 0)),
        out_shape=jax.ShapeDtypeStruct((MEM_LEN, 2048), F32),
        compiler_params=_cparams("arbitrary"),
        name="mem_kv_proj",
    )(mem, g2)


def _mem_fwd(zmain, mkv, *, T):
    S = zmain.shape[0]

    def body(q0, q1, q2, q3, kv_ref, o_ref, p_ref):
        for h, q_ref in enumerate((q0, q1, q2, q3)):
            cols = pl.ds(h * MEM_HEAD_DIM, MEM_HEAD_DIM)
            _, p = _mem_probs(q_ref, _bf(kv_ref[:, cols]))
            pb = _bf(p)
            p_ref[:, cols] = pb
            o_ref[:, cols] = _dot(pb, _bf(kv_ref[:, pl.ds(1024 + h * MEM_HEAD_DIM, MEM_HEAD_DIM)]))

    row = pl.BlockSpec((T, 1024), lambda t: (t, 0))
    return pl.pallas_call(
        body,
        grid=(S // T,),
        in_specs=_mem_q_specs(T) + [_const_spec((MEM_LEN, 2048))],
        out_specs=[row, row],
        out_shape=[jax.ShapeDtypeStruct((S, 1024), F32), jax.ShapeDtypeStruct((S, 1024), BF16)],
        compiler_params=_cparams("parallel"),
        name="mem_fwd",
    )(zmain, zmain, zmain, zmain, mkv)


def _mem_bwd(zmain, mkv, o_c, probs, d_o, *, T):
    S = zmain.shape[0]
    scale = MEM_HEAD_DIM ** -0.5

    def body(q0, q1, q2, q3, kv_ref, o_ref, p_ref, do_ref, dq_ref, dkv_ref):
        @pl.when(pl.program_id(0) == 0)
        def _():
            dkv_ref[...] = jnp.zeros_like(dkv_ref)

        for h, q_ref in enumerate((q0, q1, q2, q3)):
            cols = pl.ds(h * MEM_HEAD_DIM, MEM_HEAD_DIM)
            vcols = pl.ds(1024 + h * MEM_HEAD_DIM, MEM_HEAD_DIM)
            kb = _bf(kv_ref[:, cols])
            qs = _bf(q_ref[...] * scale)
            pb = p_ref[:, cols]
            dob = do_ref[:, cols]
            delta = jnp.sum(dob.astype(F32) * o_ref[:, cols], axis=-1, keepdims=True)
            ds = _bf(_f32(pb) * (_dot_nt(dob, _bf(kv_ref[:, vcols])) - delta))
            dq_ref[:, cols] = _bf(_dot(ds, kb) * scale)
            dkv_ref[:, cols] += _dot_tn(ds, qs)
            dkv_ref[:, vcols] += _dot_tn(pb, dob)

    row = pl.BlockSpec((T, 1024), lambda t: (t, 0))
    return pl.pallas_call(
        body,
        grid=(S // T,),
        in_specs=_mem_q_specs(T) + [_const_spec((MEM_LEN, 2048)), row, row, row],
        out_specs=[row, pl.BlockSpec((MEM_LEN, 2048), lambda t: (0, 0))],
        out_shape=[jax.ShapeDtypeStruct((S, 1024), BF16), jax.ShapeDtypeStruct((MEM_LEN, 2048), F32)],
        compiler_params=_cparams("arbitrary"),
        name="mem_bwd",
    )(zmain, zmain, zmain, zmain, mkv, o_c, probs, d_o)


def _layer_norm(u):
    mu = jnp.mean(u, axis=-1, keepdims=True)
    xc = u - mu
    rstd = lax.rsqrt(jnp.mean(xc * xc, axis=-1, keepdims=True) + LN_EPS)
    return xc * rstd, rstd


def _layer_norm_bwd(dy, gamma, xhat, rstd):
    dxh = dy * gamma
    return rstd * (dxh - jnp.mean(dxh, axis=-1, keepdims=True) - xhat * jnp.mean(dxh * xhat, axis=-1, keepdims=True))


def _merge_forward(oraw_ref, hg_ref, ob_ref, oc_ref, gl_ref, x_ref, gain_ref, wbh, wbs, wbm, wout):
    ys, rs = [], []
    for h in range(HG_HEADS):
        oh = oraw_ref[:, pl.ds(h * HG_DK, HG_DK)]
        r = lax.rsqrt(jnp.mean(oh * oh, axis=-1, keepdims=True) + RMS_EPS)
        ys.append(oh * r)
        rs.append(r)
    y = jnp.concatenate(ys, axis=1)
    hg = _f32(hg_ref[...])
    sg = _sig(hg)
    silu = hg * sg
    oa = _bf(y * gain_ref[...] * silu)
    pa = _dot(oa, _rows(wbh))
    pb = _dot(_bf(ob_ref[...]), _rows(wbs))
    pc = _dot(_bf(oc_ref[...]), _rows(wbm))
    g0 = _sig(_f32(gl_ref[:, 0:1024]))
    g1 = _sig(_f32(gl_ref[:, 1024:2048]))
    g2 = _sig(_f32(gl_ref[:, 2048:3072]))
    m = _bf(g0 * pa + g1 * pb + g2 * pc)
    u1 = ALPHA * x_ref[...] + _dot(m, _rows(wout))
    xhat, rstd = _layer_norm(u1)
    return dict(y=y, rs=rs, hg=hg, sg=sg, silu=silu, oa=oa, pa=pa, pb=pb, pc=pc,
                g0=g0, g1=g1, g2=g2, m=m, xhat=xhat, rstd=rstd)


def _gathered_spec(lo, hi):
    n = hi - lo
    return pl.BlockSpec((N_DEV, n, D_MODEL), lambda *_: (0, lo // n, 0), pipeline_mode=pl.Buffered(1))


def _rows(w_ref):
    return w_ref[...].reshape(-1, D_MODEL)


def _merge_in_specs(T):
    row = lambda w, c=0: pl.BlockSpec((T, w), lambda i: (i, c))
    vec = pl.BlockSpec((1, D_MODEL), lambda i: (0, 0))
    w = [_gathered_spec(lo, hi) for lo, hi in ((R_BH, R_BS), (R_BS, R_BM), (R_BM, R_OUT), (R_OUT, R_KV))]
    return [row(1024), row(1024, C_HG // 1024), row(1024), row(1024), row(3072), row(1024), vec, *w, vec, vec]


def _merge_fwd(o_raw, zmain, o_b, o_c, gl, x, gain, wbh, wbs, wbm, wout, ln_g, ln_b, *, T):
    S = x.shape[0]

    def body(oraw_ref, hg_ref, ob_ref, oc_ref, gl_ref, x_ref, gain_ref, wbh_r, wbs_r, wbm_r, wout_r, g_ref, b_ref,
             h1_ref, h1b_ref):
        f = _merge_forward(oraw_ref, hg_ref, ob_ref, oc_ref, gl_ref, x_ref, gain_ref, wbh_r, wbs_r, wbm_r, wout_r)
        h1 = f["xhat"] * g_ref[...] + b_ref[...]
        h1_ref[...] = h1
        h1b_ref[...] = _bf(h1)

    row = pl.BlockSpec((T, D_MODEL), lambda i: (i, 0))
    return pl.pallas_call(
        body,
        grid=(S // T,),
        in_specs=_merge_in_specs(T),
        out_specs=[row, row],
        out_shape=[jax.ShapeDtypeStruct((S, D_MODEL), F32), jax.ShapeDtypeStruct((S, D_MODEL), BF16)],
        compiler_params=_cparams("parallel"),
        name="merge_fwd",
    )(o_raw, zmain, o_b, o_c, gl, x, gain, wbh, wbs, wbm, wout, ln_g, ln_b)


def _merge_bwd(d_h1, o_raw, zmain, o_b, o_c, gl, x, gain, wbh, wbs, wbm, wout, ln_g, ln_b, *, T):
    S = x.shape[0]

    def body(dh1_ref, oraw_ref, hg_ref, ob_ref, oc_ref, gl_ref, x_ref, gain_ref, wbh_r, wbs_r, wbm_r, wout_r, g_ref, b_ref,
             dx_ref, du1_ref, m_ref, oa_ref, dpa_ref, dpb_ref, dpc_ref, doraw_ref, dob_ref, doc_ref, dz_ref,
             dgain_ref, dg_ref, db_ref):
        del b_ref

        @pl.when(pl.program_id(0) == 0)
        def _():
            dgain_ref[...] = jnp.zeros_like(dgain_ref)
            dg_ref[...] = jnp.zeros_like(dg_ref)
            db_ref[...] = jnp.zeros_like(db_ref)

        f = _merge_forward(oraw_ref, hg_ref, ob_ref, oc_ref, gl_ref, x_ref, gain_ref, wbh_r, wbs_r, wbm_r, wout_r)
        dh1 = dh1_ref[...]
        dg_ref[...] += jnp.sum(dh1 * f["xhat"], axis=0, keepdims=True)
        db_ref[...] += jnp.sum(dh1, axis=0, keepdims=True)
        du1 = _layer_norm_bwd(dh1, g_ref[...], f["xhat"], f["rstd"])
        dx_ref[...] = ALPHA * du1
        du1b = _bf(du1)
        du1_ref[...] = du1b
        m_ref[...] = f["m"]
        oa_ref[...] = f["oa"]
        dm = _dot_nt(du1b, _rows(wout_r))
        for i, (g, p, dp_ref, dob_r, w_r) in enumerate((
                (f["g0"], f["pa"], dpa_ref, None, wbh_r),
                (f["g1"], f["pb"], dpb_ref, dob_ref, wbs_r),
                (f["g2"], f["pc"], dpc_ref, doc_ref, wbm_r))):
            dz_ref[:, pl.ds((i + 1) * 1024, 1024)] = _bf(dm * p * g * (1.0 - g))
            dp = _bf(dm * g)
            dp_ref[...] = dp
            d_branch = _dot_nt(dp, _rows(w_r))
            if dob_r is not None:
                dob_r[...] = _bf(d_branch)
            else:
                doa = d_branch
        gain = gain_ref[...]
        t = doa * f["y"]
        dgain_ref[...] += jnp.sum(t * f["silu"], axis=0, keepdims=True)
        sg = f["sg"]
        dz_ref[:, 0:1024] = _bf(t * gain * sg * (1.0 + f["hg"] * (1.0 - sg)))
        dy = doa * gain * f["silu"]
        for h in range(HG_HEADS):
            cols = slice(h * HG_DK, (h + 1) * HG_DK)
            yh = f["y"][:, cols]
            dyh = dy[:, cols]
            doraw_ref[:, pl.ds(h * HG_DK, HG_DK)] = _bf(
                f["rs"][h] * (dyh - yh * jnp.mean(dyh * yh, axis=-1, keepdims=True)))

    row = lambda w: pl.BlockSpec((T, w), lambda i: (i, 0))
    vec = pl.BlockSpec((1, D_MODEL), lambda i: (0, 0))
    bshape = jax.ShapeDtypeStruct((S, D_MODEL), BF16)
    vshape = jax.ShapeDtypeStruct((1, D_MODEL), F32)
    return pl.pallas_call(
        body,
        grid=(S // T,),
        in_specs=[row(1024)] + _merge_in_specs(T),
        out_specs=[row(1024)] * 10 + [row(4096), vec, vec, vec],
        out_shape=[jax.ShapeDtypeStruct((S, D_MODEL), F32)] + [bshape] * 9
        + [jax.ShapeDtypeStruct((S, 4096), BF16), vshape, vshape, vshape],
        compiler_params=_cparams("arbitrary"),
        name="merge_bwd",
    )(d_h1, o_raw, zmain, o_b, o_c, gl, x, gain, wbh, wbs, wbm, wout, ln_g, ln_b)


def _mlp_fwd_bwd(h1, target, wup_t, wdn, ln_g, ln_b, *, T, FC):
    S = h1.shape[0]
    nf = D_FF // FC
    assert FC == R_BH - R_UP == R_UP - R_DN

    def body(h1_ref, t_ref, wup_ref, wdn_ref, g_ref, b_ref, dh1_ref, a_ref, dup_ref, du2_ref, loss_ref, dg_ref, db_ref, up_scr):
        @pl.when(pl.program_id(0) == 0)
        def _():
            loss_ref[...] = jnp.zeros_like(loss_ref)
            dg_ref[...] = jnp.zeros_like(dg_ref)
            db_ref[...] = jnp.zeros_like(db_ref)

        h1v = h1_ref[...]
        h1b = _bf(h1v)
        ff = jnp.zeros((T, D_MODEL), F32)
        for j in range(nf):
            rows = pl.ds(j * FC, FC)
            up = jnp.maximum(_dot_nt(h1b, wup_ref[j]), 0.0)
            up_scr[:, rows] = _bf(up)
            a = _bf(up * up)
            a_ref[:, rows] = a
            ff = ff + _dot(a, wdn_ref[j])
        xhat, rstd = _layer_norm(ALPHA * h1v + ff)
        gamma = g_ref[...]
        err = xhat * gamma + b_ref[...] - t_ref[...]
        loss_ref[...] += jnp.sum(jnp.sum(err * err, axis=-1, keepdims=True), axis=0, keepdims=True) * (0.5 / D_MODEL)
        dy = err * (1.0 / D_MODEL)
        dg_ref[...] += jnp.sum(dy * xhat, axis=0, keepdims=True)
        db_ref[...] += jnp.sum(dy, axis=0, keepdims=True)
        du2 = _layer_norm_bwd(dy, gamma, xhat, rstd)
        du2b = _bf(du2)
        du2_ref[...] = du2b
        dh1 = ALPHA * du2
        for j in range(nf):
            rows = pl.ds(j * FC, FC)
            dup = _bf(_dot_nt(du2b, wdn_ref[j]) * (2.0 * up_scr[:, rows].astype(F32)))
            dup_ref[:, rows] = dup
            dh1 = dh1 + _dot(dup, wup_ref[j])
        dh1_ref[...] = dh1

    row = lambda w: pl.BlockSpec((T, w), lambda i: (i, 0))
    vec = pl.BlockSpec((1, D_MODEL), lambda i: (0, 0))
    vshape = jax.ShapeDtypeStruct((1, D_MODEL), F32)
    return pl.pallas_call(
        body,
        grid=(S // T,),
        in_specs=[row(1024), row(1024), _gathered_spec(R_UP, R_BH), _gathered_spec(R_DN, R_UP), vec, vec],
        out_specs=[row(1024), row(D_FF), row(D_FF), row(1024), pl.BlockSpec((8, 128), lambda i: (0, 0)), vec, vec],
        out_shape=[
            jax.ShapeDtypeStruct((S, D_MODEL), F32),
            jax.ShapeDtypeStruct((S, D_FF), BF16),
            jax.ShapeDtypeStruct((S, D_FF), BF16),
            jax.ShapeDtypeStruct((S, D_MODEL), BF16),
            jax.ShapeDtypeStruct((8, 128), F32), vshape, vshape,
        ],
        scratch_shapes=[pltpu.VMEM((T, D_FF), BF16)],
        compiler_params=_cparams("arbitrary"),
        name="mlp_fwd_bwd",
    )(h1, target, wup_t, wdn, ln_g, ln_b)


def _local_step(x, mem, target, lb_logits, gain, sinks, rel_bias, ln1_g, ln1_b, ln2_g, ln2_b,
                win_t, dep0, other_weights, send_other_grads, send_small_grads, send_win_grad):
    S = x.shape[0]
    T = min(256, S)
    KC = min(2048, S)
    zmain, gl, xb = _in_proj(x, win_t, dep0, tm=min(512, S))
    bucket = _t5_bucket_table()

    o_raw, states = _hgrn_fwd(zmain, lb_logits, T=min(1024, S))
    o_b, swa_probs = _swa_fwd(zmain, bucket, rel_bias, sinks)
    g2 = other_weights(o_b)
    mkv = _mem_kv_proj(mem, g2)
    o_c, mem_probs = _mem_fwd(zmain, mkv, T=min(512, S))
    merge_args = (o_raw, zmain, o_b, o_c, gl, x, gain, g2, g2, g2, g2, ln1_g, ln1_b)
    h1, h1b = _merge_fwd(*merge_args, T=min(512, S))

    d_h1, act, d_up, du2, loss, d_ln2_g, d_ln2_b = _mlp_fwd_bwd(h1, target, g2, g2, ln2_g, ln2_b, T=min(512, S), FC=512)
    wgrad = functools.partial(_mm_tn, out_dtype=BF16)
    g_wdn = wgrad(act, du2, kc=KC, name="grad_w_down")
    g_wup_t = wgrad(d_up, h1b, kc=KC, name="grad_w_up")

    (dx_part, du1, m, oa, dpa, dpb, dpc, d_oraw, d_ob, d_oc, d_hg_gl,
     d_gain, d_ln1_g, d_ln1_b) = _merge_bwd(d_h1, *merge_args, T=T)
    g_wout = wgrad(m, du1, kc=KC, name="grad_w_out")
    g_wbh = wgrad(oa, dpa, kc=KC, name="grad_w_branch_hg")
    g_wbs = wgrad(o_b, dpb, kc=KC, name="grad_w_branch_swa")
    g_wbm = wgrad(o_c, dpc, kc=KC, name="grad_w_branch_mem")

    d_mq, d_mkv = _mem_bwd(zmain, mkv, o_c, mem_probs, d_oc, T=min(512, S))
    g_wkv_t = wgrad(d_mkv, mem, kc=MEM_LEN, name="grad_w_mem_kv")
    sent_others = send_other_grads(
        dict(wkv_t=g_wkv_t, wbh=g_wbh, wbs=g_wbs, wbm=g_wbm, wout=g_wout, wup_t=g_wup_t, wdn=g_wdn))
    d_sq, d_skv, d_rb, d_sink = _swa_bwd(zmain, o_b, swa_probs, d_ob, bucket, sent_others)
    d_qfv, d_lb = _hgrn_bwd(zmain, lb_logits, states, d_oraw, T=min(1024, S))
    sent_small = send_small_grads(_pack_small_grads(d_lb, d_gain, d_sink, d_rb, d_ln1_g, d_ln1_b, d_ln2_g, d_ln2_b, loss))

    head_major = lambda a: a.reshape(3, HG_HEADS, HG_DK, D_MODEL).transpose(1, 0, 2, 3).reshape(3 * D_MODEL, D_MODEL)
    col_major = lambda a: a.reshape(HG_HEADS, 3, HG_DK, D_MODEL).transpose(1, 0, 2, 3).reshape(3 * D_MODEL, D_MODEL)
    pieces = (d_qfv, d_hg_gl, d_sq, d_skv, d_mq)
    g_qfv, g_hg_gl, g_sq, g_skv, g_mq = [
        wgrad(p, xb, kc=KC, name="grad_w_in_" + n) for p, n in zip(pieces, ("qfv", "hg_gates", "swa_q", "swa_kv", "mem_q"))]
    g_win_t = jnp.concatenate([col_major(g_qfv), g_hg_gl[:D_MODEL], g_sq, g_skv, g_mq, g_hg_gl[D_MODEL:]], axis=0)
    sent_win = send_win_grad(g_win_t, sent_small)
    return _grad_x(*pieces, head_major(win_t[:C_HG]), win_t, dx_part, sent_win, tm=T)


MESH = pl.DeviceIdType.MESH
ANY = pl.BlockSpec(memory_space=pl.ANY)


def _coords():
    return lax.axis_index("x"), lax.axis_index("y"), lax.axis_index("c")


def _other_chips(x, y):
    return [(1 - x, y), (x, 1 - y), (1 - x, 1 - y)]


def _all_gather_weights(*arrays):
    na = len(arrays)

    def body(*refs):
        srcs, dsts = refs[:na], refs[na:2 * na]
        send_sems, recv_sems, local_sems = refs[2 * na:]
        x, y, c = _coords()
        me, sibling = (x, y, c), (x, y, 1 - c)
        chips = _other_chips(x, y)

        def slot(a, px, py, pc):
            return dsts[a].at[4 * px + 2 * py + pc]

        def copy(a, k, block, to, from_shard=False):
            return pltpu.make_async_remote_copy(
                src_ref=srcs[a] if from_shard else slot(a, *block), dst_ref=slot(a, *block),
                send_sem=send_sems.at[a * 7 + k], recv_sem=recv_sems.at[a * 7 + k],
                device_id=to, device_id_type=MESH)

        own = [pltpu.make_async_copy(srcs[a], slot(a, *me), local_sems.at[a]) for a in range(na)]
        for cp in own:
            cp.start()
        first = []
        for a in range(na):
            first.append(copy(a, 0, me, sibling, True))
            first += [copy(a, 1 + j, me, (*chip, c), True) for j, chip in enumerate(chips)]
        for cp in first:
            cp.start()
        passed = []
        for j, chip in enumerate(chips):
            for a in range(na):
                copy(a, 1 + j, (*chip, c), me).wait_recv()
                fwd = copy(a, 4 + j, (*chip, c), sibling)
                fwd.start()
                passed.append(fwd)
        for a in range(na):
            copy(a, 0, sibling, me).wait_recv()
            for j, chip in enumerate(chips):
                copy(a, 4 + j, (*chip, 1 - c), me).wait_recv()
        for cp in first + passed:
            cp.wait_send()
        for cp in own:
            cp.wait()

    return pl.pallas_call(
        body,
        in_specs=[ANY] * na,
        out_specs=[ANY] * na,
        out_shape=[jax.ShapeDtypeStruct((N_DEV,) + a.shape, a.dtype) for a in arrays],
        scratch_shapes=[pltpu.SemaphoreType.DMA((7 * na,)), pltpu.SemaphoreType.DMA((7 * na,)),
                        pltpu.SemaphoreType.DMA((na,))],
        name="all_gather_weights",
    )(*arrays)


HBM = pl.BlockSpec(memory_space=pltpu.HBM)
SEM = pl.BlockSpec(memory_space=pltpu.SEMAPHORE)
_DATAFLOW = pltpu.SideEffectType.DATAFLOW_SIDE_EFFECTING


def _peer(x, y, c, r):
    return x ^ (r >> 2), y ^ ((r >> 1) & 1), c ^ (r & 1)


def _direct_copies(src_ref, land_ref, send_sems, recv_sems, gather, receiving):
    x, y, c = _coords()
    me = 4 * x + 2 * y + c
    copies = []
    for r in range(1, N_DEV):
        px, py, pc = _peer(x, y, c, r)
        peer = 4 * px + 2 * py + pc
        if gather:
            src, dst = src_ref, land_ref.at[peer if receiving else me]
        else:
            src, dst = src_ref.at[peer], land_ref.at[r - 1]
        copies.append(pltpu.make_async_remote_copy(
            src_ref=src, dst_ref=dst, send_sem=send_sems.at[r - 1], recv_sem=recv_sems.at[r - 1],
            device_id=(px, py, pc), device_id_type=MESH))
    return copies


def _direct_start(src, land, *, gather, name, after=None):
    def body(src_ref, land_ref, *rest):
        send_sems, recv_sems, token = rest[-5], rest[-4], rest[-1]
        for cp in _direct_copies(src_ref, land_ref, send_sems, recv_sems, gather, False):
            cp.start()
        token[...] = jnp.zeros_like(token)

    afters = () if after is None else (after,)
    return pl.pallas_call(
        body,
        name=name,
        out_shape=(pltpu.SemaphoreType.DMA((N_DEV - 1,)), pltpu.SemaphoreType.DMA((N_DEV - 1,)),
                   pltpu.HBM(src.shape, src.dtype), pltpu.HBM(land.shape, land.dtype),
                   jax.ShapeDtypeStruct((8, 128), F32)),
        in_specs=(HBM, HBM) + tuple(ANY for _ in afters),
        out_specs=(SEM, SEM, HBM, HBM, pl.BlockSpec(memory_space=pltpu.VMEM)),
        input_output_aliases={0: 2, 1: 3},
        compiler_params=pltpu.CompilerParams(has_side_effects=_DATAFLOW),
    )(pltpu.with_memory_space_constraint(src, pltpu.HBM), pltpu.with_memory_space_constraint(land, pltpu.HBM), *afters)


def _direct_wait(send_sems, recv_sems, src_thru, land_thru, after, *, gather, name):
    def body(src_ref, land_ref, send_sems_ref, recv_sems_ref, after_ref, src_dead, got_ref):
        del after_ref, src_dead, got_ref
        for cp in _direct_copies(src_ref, land_ref, send_sems_ref, recv_sems_ref, gather, True):
            cp.wait_send()
            cp.wait_recv()

    return pl.pallas_call(
        body,
        name=name,
        out_shape=(pltpu.HBM(src_thru.shape, src_thru.dtype), pltpu.HBM(land_thru.shape, land_thru.dtype)),
        in_specs=(HBM, HBM, SEM, SEM, ANY),
        out_specs=(HBM, HBM),
        input_output_aliases={0: 0, 1: 1},
        compiler_params=pltpu.CompilerParams(has_side_effects=_DATAFLOW),
    )(src_thru, land_thru, send_sems, recv_sems, after)


def _sum_partials(src, land, me, *, tr, name):
    R = src.shape[1]

    def body(me_ref, s_ref, l_ref, o_ref):
        del me_ref
        acc = s_ref[0].astype(F32)
        for r in range(N_DEV - 1):
            acc = acc + l_ref[r].astype(F32)
        o_ref[...] = acc

    return pl.pallas_call(
        body,
        grid_spec=pltpu.PrefetchScalarGridSpec(
            num_scalar_prefetch=1, grid=(R // tr,),
            in_specs=[pl.BlockSpec((1, tr, 1024), lambda i, mr: (mr[0], i, 0)),
                      pl.BlockSpec((N_DEV - 1, tr, 1024), lambda i, mr: (0, i, 0))],
            out_specs=pl.BlockSpec((tr, 1024), lambda i, mr: (i, 0))),
        out_shape=jax.ShapeDtypeStruct((R, 1024), F32),
        name=name,
    )(me, src, land)


_SMALL = ("lb_logits", "hg_norm_gain", "swa_sinks", "rel_bias", "ln1_g", "ln1_b", "ln2_g", "ln2_b")


def _pack_small_grads(d_lb, d_gain, d_sink, d_rb, d_ln1_g, d_ln1_b, d_ln2_g, d_ln2_b, loss):
    def body(lb_ref, gain_ref, sink_ref, rb_ref, l1g_ref, l1b_ref, l2g_ref, l2b_ref, loss_ref, o_ref):
        o_ref[...] = jnp.zeros_like(o_ref)
        for row, ref in ((SM_LB, lb_ref), (SM_GAIN, gain_ref), (SM_L1G, l1g_ref), (SM_L1B, l1b_ref),
                         (SM_L2G, l2g_ref), (SM_L2B, l2b_ref)):
            o_ref[row:row + 1, :] = ref[...]
        o_ref[SM_SINK:SM_SINK + 1, 0:128] = sink_ref[0:1, :]
        o_ref[SM_LOSS:SM_LOSS + 1, 0:128] = loss_ref[0:1, :]
        o_ref[SM_RB:SM_RB + NUM_BUCKETS, 0:128] = rb_ref[...]

    vm = pl.BlockSpec(memory_space=pltpu.VMEM)
    return pl.pallas_call(
        body,
        in_specs=[vm] * 9,
        out_specs=vm,
        out_shape=jax.ShapeDtypeStruct((SM_ROWS, D_MODEL), F32),
        name="pack_small_grads",
    )(d_lb, d_gain, d_sink, d_rb, d_ln1_g, d_ln1_b, d_ln2_g, d_ln2_b, loss)


def _small_finish(gathered, w, m, v):
    n = len(_SMALL)

    def body(*refs):
        g_ref = refs[0]
        w_refs, m_refs, v_refs = refs[1:1 + n], refs[1 + n:1 + 2 * n], refs[1 + 2 * n:1 + 3 * n]
        outs = refs[1 + 3 * n:]
        loss_ref, tot = outs[0], outs[-1]
        g_out, d_out, m_out, v_out = (outs[1 + k * n:1 + (k + 1) * n] for k in range(4))
        acc = g_ref[0]
        for d in range(1, N_DEV):
            acc = acc + g_ref[d]
        tot[...] = acc
        loss_ref[...] = tot[SM_LOSS:SM_LOSS + 1, 0:1]
        lb = _lower_bound(w_refs[0])
        dl0 = tot[SM_LB:SM_LB + 1, :] * lb * (1.0 - lb)
        grads = (jnp.concatenate([dl0, -dl0], axis=0), tot[SM_GAIN:SM_GAIN + 1, :],
                 tot[SM_SINK:SM_SINK + 1, 0:SWA_HEADS], tot[SM_RB:SM_RB + NUM_BUCKETS, 0:SWA_HEADS],
                 tot[SM_L1G:SM_L1G + 1, :], tot[SM_L1B:SM_L1B + 1, :], tot[SM_L2G:SM_L2G + 1, :], tot[SM_L2B:SM_L2B + 1, :])
        for k, g in enumerate(grads):
            g_out[k][...] = g
            d_out[k][...], m_out[k][...], v_out[k][...] = _adam_step(w_refs[k][...], g, m_refs[k][...], v_refs[k][...])

    vm = pl.BlockSpec(memory_space=pltpu.VMEM)
    shapes = [jax.ShapeDtypeStruct(w[k].shape, F32) for k in _SMALL]
    res = pl.pallas_call(
        body,
        in_specs=[vm] * (1 + 3 * n),
        out_specs=[vm] * (1 + 4 * n),
        out_shape=[jax.ShapeDtypeStruct((1, 1), F32)] + shapes * 4,
        scratch_shapes=[pltpu.VMEM((SM_ROWS, D_MODEL), F32)],
        name="small_finish",
    )(gathered, *[w[k] for k in _SMALL], *[m[k] for k in _SMALL], *[v[k] for k in _SMALL])
    parts = [dict(zip(_SMALL, res[1 + k * n:1 + (k + 1) * n])) for k in range(4)]
    return (res[0], *parts)


def _adam_step(w, g, m, v):
    nm = ADAM_B1 * m + (1.0 - ADAM_B1) * g
    nv = ADAM_B2 * v + (1.0 - ADAM_B2) * jnp.square(g)
    m_hat = nm / (1.0 - ADAM_B1 ** ADAM_STEP)
    v_hat = nv / (1.0 - ADAM_B2 ** ADAM_STEP)
    return -ADAM_LR * (m_hat / (jnp.sqrt(v_hat) + ADAM_EPS) + ADAM_WD * w), nm, nv


def _adamw(w, g, m, v, *, tr, name):
    R, C = w.shape

    def body(w_ref, g_ref, m_ref, v_ref, d_ref, nm_ref, nv_ref):
        d_ref[...], nm_ref[...], nv_ref[...] = _adam_step(w_ref[...], g_ref[...], m_ref[...], v_ref[...])

    spec = pl.BlockSpec((tr, C), lambda i: (i, 0))
    return pl.pallas_call(
        body,
        grid=(R // tr,),
        in_specs=[spec] * 4,
        out_specs=[spec] * 3,
        out_shape=[jax.ShapeDtypeStruct((R, C), F32)] * 3,
        compiler_params=_cparams("parallel"),
        name=name,
    )(w, g, m, v)


_WEIGHTS = ("w_in", "lb_logits", "hg_norm_gain", "swa_sinks", "rel_bias", "w_mem_kv", "w_branch_hg", "w_branch_swa",
            "w_branch_mem", "w_out", "ln1_g", "ln1_b", "w_up", "w_down", "ln2_g", "ln2_b")


def kernel(x, mem, w_in, lb_logits, hg_norm_gain, swa_sinks, rel_bias, w_mem_kv, w_branch_hg, w_branch_swa, w_branch_mem, w_out, ln1_g, ln1_b, w_up, w_down, ln2_g, ln2_b, loss_target, m_w_in, m_lb_logits, m_hg_norm_gain, m_swa_sinks, m_rel_bias, m_w_mem_kv, m_w_branch_hg, m_w_branch_swa, m_w_branch_mem, m_w_out, m_ln1_g, m_ln1_b, m_w_up, m_w_down, m_ln2_g, m_ln2_b, v_w_in, v_lb_logits, v_hg_norm_gain, v_swa_sinks, v_rel_bias, v_w_mem_kv, v_w_branch_hg, v_w_branch_swa, v_w_branch_mem, v_w_out, v_ln1_g, v_ln1_b, v_w_up, v_w_down, v_ln2_g, v_ln2_b):
    w = dict(w_in=w_in, lb_logits=lb_logits, hg_norm_gain=hg_norm_gain, swa_sinks=swa_sinks, rel_bias=rel_bias,
             w_mem_kv=w_mem_kv, w_branch_hg=w_branch_hg, w_branch_swa=w_branch_swa, w_branch_mem=w_branch_mem,
             w_out=w_out, ln1_g=ln1_g, ln1_b=ln1_b, w_up=w_up, w_down=w_down, ln2_g=ln2_g, ln2_b=ln2_b)
    mom = dict(w_in=m_w_in, lb_logits=m_lb_logits, hg_norm_gain=m_hg_norm_gain, swa_sinks=m_swa_sinks, rel_bias=m_rel_bias,
               w_mem_kv=m_w_mem_kv, w_branch_hg=m_w_branch_hg, w_branch_swa=m_w_branch_swa, w_branch_mem=m_w_branch_mem,
               w_out=m_w_out, ln1_g=m_ln1_g, ln1_b=m_ln1_b, w_up=m_w_up, w_down=m_w_down, ln2_g=m_ln2_g, ln2_b=m_ln2_b)
    var = dict(w_in=v_w_in, lb_logits=v_lb_logits, hg_norm_gain=v_hg_norm_gain, swa_sinks=v_swa_sinks, rel_bias=v_rel_bias,
               w_mem_kv=v_w_mem_kv, w_branch_hg=v_w_branch_hg, w_branch_swa=v_w_branch_swa, w_branch_mem=v_w_branch_mem,
               w_out=v_w_out, ln1_g=v_ln1_g, ln1_b=v_ln1_b, w_up=v_w_up, w_down=v_w_down, ln2_g=v_ln2_g, ln2_b=v_ln2_b)
    xc, yc, cc = _coords()

    p1 = _bf(w_in[0].T)
    p2 = _bf(jnp.concatenate([w_down[0], w_up[0].T, w_branch_hg[0], w_branch_swa[0], w_branch_mem[0], w_out[0],
                              w_mem_kv[0].T], axis=0))
    me = 4 * xc + 2 * yc + cc
    (g1,) = _all_gather_weights(p1)
    land2 = lax.dynamic_update_slice(lax.empty((N_DEV, R_OTHER, D_MODEL), BF16), p2[None], (me, 0, 0))
    ag2 = _direct_start(p2, land2, gather=True, name="gather_other_weights_start")

    def other_weights(after):
        return _direct_wait(*ag2[:4], after, gather=True, name="gather_other_weights_wait")[1]

    blocks = lambda a: a.reshape(N_DEV, a.shape[0] // N_DEV, D_MODEL)
    started = {}

    def send_other_grads(g):
        part = jnp.concatenate([blocks(g[k]) for k in ("wdn", "wup_t", "wbh", "wbs", "wbm", "wout", "wkv_t")], axis=1)
        started["others"] = _direct_start(part, lax.empty((N_DEV - 1, R_OTHER, D_MODEL), BF16), gather=False,
                                          name="scatter_other_grads_start")
        return started["others"][4]

    me1 = me.reshape(1).astype(jnp.int32)
    grads, delta, new_m, new_v = {}, {}, {}, {}

    def adamw(name):
        w2 = w[name][0]
        delta[name], new_m[name], new_v[name] = _adamw(
            w2, grads[name], mom[name][0], var[name][0], tr=w2.shape[0] // 4, name="adamw_" + name)

    def send_small_grads(packed):
        land = lax.dynamic_update_slice(lax.empty((N_DEV, SM_ROWS, D_MODEL), F32), packed[None], (me, 0, 0))
        started["small"] = _direct_start(packed, land, gather=True, name="gather_small_grads_start")
        return started["small"][4]

    def send_win_grad(g, after):
        started["win"] = _direct_start(blocks(g), lax.empty((N_DEV - 1, IN_SHARD, D_MODEL), BF16), gather=False,
                                       name="scatter_w_in_grad_start", after=after)
        mine2, landed2 = _direct_wait(*started["others"][:4], started["win"][4], gather=False,
                                      name="scatter_other_grads_wait")
        gs2 = _sum_partials(mine2, landed2, me1, tr=R_OTHER // 2, name="sum_other_grads")
        grads.update(
            w_down=gs2[R_DN:R_UP], w_up=gs2[R_UP:R_BH].T, w_branch_hg=gs2[R_BH:R_BS], w_branch_swa=gs2[R_BS:R_BM],
            w_branch_mem=gs2[R_BM:R_OUT], w_out=gs2[R_OUT:R_KV], w_mem_kv=gs2[R_KV:R_OTHER].T)
        for name in ("w_mem_kv", "w_branch_hg", "w_branch_swa", "w_branch_mem", "w_out", "w_up", "w_down"):
            adamw(name)
        return tuple(new_v[name] for name in new_v)

    grad_x = _local_step(
        x[0], mem[0], loss_target[0], lb_logits, hg_norm_gain, swa_sinks, rel_bias, ln1_g, ln1_b, ln2_g, ln2_b,
        g1.reshape(IN_COLS, D_MODEL), ag2[4], other_weights, send_other_grads, send_small_grads, send_win_grad)

    mine1, landed1 = _direct_wait(*started["win"][:4], grad_x, gather=False, name="scatter_w_in_grad_wait")
    g_win_t = _sum_partials(mine1, landed1, me1, tr=IN_SHARD // 2, name="sum_w_in_grad")
    d_t, m_t, v_t = _adamw(w_in[0].T, g_win_t, m_w_in[0].T, v_w_in[0].T, tr=IN_SHARD // 4, name="adamw_w_in")
    grads["w_in"], delta["w_in"], new_m["w_in"], new_v["w_in"] = g_win_t.T, d_t.T, m_t.T, v_t.T

    _, gathered = _direct_wait(*started["small"][:4], grad_x, gather=True, name="gather_small_grads_wait")
    loss, g_s, d_s, m_s, v_s = _small_finish(gathered, w, mom, var)
    for dst, src in ((grads, g_s), (delta, d_s), (new_m, m_s), (new_v, v_s)):
        dst.update(src)

    def shaped(d, name):
        return d[name].reshape(w[name].shape)

    return (loss.reshape(()), grad_x[None], *[shaped(grads, n) for n in _WEIGHTS], *[shaped(delta, n) for n in _WEIGHTS],
            *[shaped(new_m, n) for n in _WEIGHTS], *[shaped(new_v, n) for n in _WEIGHTS])
```

```python
import functools
import math

import jax
import jax.numpy as jnp
from jax import lax
from jax.experimental import pallas as pl
from jax.experimental.pallas import tpu as pltpu

F32 = jnp.float32
BF16 = jnp.bfloat16

D_MODEL = 1024
MEM_LEN = 256
HG_HEADS = 8
HG_DK = 128
HG_CHUNK = 64
SWA_HEADS = 16
SWA_HEAD_DIM = 64
SWA_BLOCK = 128
SWA_WINDOW = 128
MEM_HEADS = 4
MEM_HEAD_DIM = 256
NUM_BUCKETS = 32
MAX_DISTANCE = 128
D_FF = 4096
LN_EPS = 1e-5
RMS_EPS = 1e-6
ALPHA = 2.0 ** 0.25
N_DEV = 8

C_HQ, C_HF, C_HI, C_HG, C_SQ, C_SK, C_SV, C_MQ, C_GL = 0, 1024, 2048, 3072, 4096, 5120, 5248, 5376, 6400
IN_COLS = 9472
IN_SHARD = IN_COLS // N_DEV

ADAM_LR = 0.001
ADAM_B1 = 0.9
ADAM_B2 = 0.999
ADAM_EPS = 1e-08
ADAM_WD = 0.01
ADAM_STEP = 10

VMEM_LIMIT = 58 * 1024 * 1024

R_DN, R_UP, R_BH, R_BS, R_BM, R_OUT, R_KV, R_OTHER = 0, 512, 1024, 1152, 1280, 1408, 1536, 1792

SM_LB, SM_GAIN, SM_SINK, SM_L1G, SM_L1B, SM_L2G, SM_L2B, SM_LOSS, SM_RB, SM_ROWS = 0, 2, 3, 4, 5, 6, 7, 8, 16, 48


def _bf(v):
    return v.astype(BF16)


def _f32(v):
    return v.astype(F32)


def _dot(a, b):
    return jnp.dot(a, b, preferred_element_type=F32)


def _dot_nt(a, b):
    return lax.dot_general(a, b, (((1,), (1,)), ((), ())), preferred_element_type=F32)


def _dot_tn(a, b):
    return lax.dot_general(a, b, (((0,), (0,)), ((), ())), preferred_element_type=F32)


def _sig(v):
    return 0.5 * jnp.tanh(0.5 * v) + 0.5


def _cparams(*sem):
    return pltpu.CompilerParams(dimension_semantics=sem, vmem_limit_bytes=VMEM_LIMIT)


def _const_spec(shape):
    nd = len(shape)
    return pl.BlockSpec(shape, lambda *_: (0,) * nd, pipeline_mode=pl.Buffered(1))


def _dep_spec():
    return pl.BlockSpec((8, 128), lambda *_: (0, 0))


def _in_proj(x, win_t, dep, *, tm):
    S = x.shape[0]

    def body(x_ref, w_ref, dep_ref, z_ref, gl_ref, xb_ref):
        del dep_ref
        xb = _bf(x_ref[...])
        xb_ref[...] = xb
        for c0 in range(0, C_GL, 1280):
            z_ref[:, c0:c0 + 1280] = _bf(_dot_nt(xb, w_ref[c0:c0 + 1280, :]))
        for c0 in range(0, IN_COLS - C_GL, 1024):
            gl_ref[:, c0:c0 + 1024] = _bf(_dot_nt(xb, w_ref[C_GL + c0:C_GL + c0 + 1024, :]))

    row = lambda w: pl.BlockSpec((tm, w), lambda i: (i, 0))
    return pl.pallas_call(
        body,
        grid=(S // tm,),
        in_specs=[row(D_MODEL), _const_spec(win_t.shape), _dep_spec()],
        out_specs=[row(C_GL), row(IN_COLS - C_GL), row(D_MODEL)],
        out_shape=[jax.ShapeDtypeStruct((S, C_GL), BF16), jax.ShapeDtypeStruct((S, IN_COLS - C_GL), BF16),
                   jax.ShapeDtypeStruct((S, D_MODEL), BF16)],
        compiler_params=_cparams("parallel"),
        name="in_proj",
    )(x, win_t, dep)


def _mm_tn_resident(a, b, *, tm, kc, name, out_dtype):
    K, M = a.shape
    N = b.shape[1]
    nk = K // kc

    def body(a_ref, b_ref, o_ref):
        acc = jnp.zeros((tm, N), F32)
        for kk in range(nk):
            sl = pl.ds(kk * kc, kc)
            acc = acc + _dot_tn(_bf(a_ref[sl, :]), _bf(b_ref[sl, :]))
        o_ref[...] = acc.astype(o_ref.dtype)

    return pl.pallas_call(
        body,
        grid=(M // tm,),
        in_specs=[pl.BlockSpec((K, tm), lambda i: (0, i)), _const_spec((K, N))],
        out_specs=pl.BlockSpec((tm, N), lambda i: (i, 0)),
        out_shape=jax.ShapeDtypeStruct((M, N), out_dtype),
        compiler_params=_cparams("parallel"),
        name=name,
    )(a, b)


def _mm_tn(a, b, *, kc, name, out_dtype=F32):
    K, M = a.shape
    N = b.shape[1]
    if M > 1024:
        return _mm_tn_resident(a, b, tm=256, kc=min(kc, 1024), name=name, out_dtype=out_dtype)
    tm = M
    nk = K // kc

    def body(a_ref, b_ref, o_ref, acc):
        k = pl.program_id(1)
        part = _dot_tn(_bf(a_ref[...]), _bf(b_ref[...]))

        @pl.when(k == 0)
        def _():
            acc[...] = part

        @pl.when(k > 0)
        def _():
            acc[...] += part

        @pl.when(k == nk - 1)
        def _():
            o_ref[...] = acc[...].astype(o_ref.dtype)

    return pl.pallas_call(
        body,
        grid=(M // tm, nk),
        in_specs=[pl.BlockSpec((kc, tm), lambda i, k: (k, i)), pl.BlockSpec((kc, N), lambda i, k: (k, 0))],
        out_specs=pl.BlockSpec((tm, N), lambda i, k: (i, 0)),
        out_shape=jax.ShapeDtypeStruct((M, N), out_dtype),
        scratch_shapes=[pltpu.VMEM((tm, N), F32)],
        compiler_params=_cparams("parallel", "arbitrary"),
        name=name,
    )(a, b)


def _grad_x(d_qfv, d_hg_gl, d_sq, d_skv, d_mq, w_qfv, win_t, add, deps, *, tm):
    M = add.shape[0]
    pieces = (d_qfv, d_hg_gl, d_sq, d_skv, d_mq)

    def body(qfv_ref, hggl_ref, sq_ref, skv_ref, mq_ref, wq_ref, w_ref, add_ref, *rest):
        o_ref = rest[-1]
        acc = add_ref[...] + _dot(qfv_ref[...], wq_ref[...])
        acc = acc + _dot(hggl_ref[:, 0:1024], w_ref[C_HG:C_SQ, :])
        acc = acc + _dot(hggl_ref[:, 1024:4096], w_ref[C_GL:IN_COLS, :])
        acc = acc + _dot(sq_ref[...], w_ref[C_SQ:C_SK, :])
        acc = acc + _dot(skv_ref[...], w_ref[C_SK:C_MQ, :])
        o_ref[...] = acc + _dot(mq_ref[...], w_ref[C_MQ:C_GL, :])

    return pl.pallas_call(
        body,
        grid=(M // tm,),
        in_specs=[pl.BlockSpec((tm, p.shape[1]), lambda i: (i, 0)) for p in pieces]
        + [_const_spec(w_qfv.shape), _const_spec(win_t.shape), pl.BlockSpec((tm, D_MODEL), lambda i: (i, 0))]
        + [_dep_spec() for _ in deps],
        out_specs=pl.BlockSpec((tm, D_MODEL), lambda i: (i, 0)),
        out_shape=jax.ShapeDtypeStruct((M, D_MODEL), F32),
        compiler_params=_cparams("parallel"),
        name="grad_x",
    )(*pieces, w_qfv, win_t, add, *deps)


def _lower_bound(lbl_ref):
    l0 = lbl_ref[0:1, :]
    l1 = lbl_ref[1:2, :]
    mx = jnp.maximum(l0, l1)
    e0 = jnp.exp(l0 - mx)
    e1 = jnp.exp(l1 - mx)
    return e0 / (e0 + e1)


def _tri(lower):
    r = lax.broadcasted_iota(jnp.int32, (HG_CHUNK, HG_CHUNK), 0)
    c = lax.broadcasted_iota(jnp.int32, (HG_CHUNK, HG_CHUNK), 1)
    return (r >= c) if lower else (r <= c)


def _hg_gates(fl, lb):
    sg = _sig(fl)
    f = lb + (1.0 - lb) * sg
    return sg, f, jnp.log(f), 1.0 - f


def _scan_rows(v, reverse=False):
    row = lax.broadcasted_iota(jnp.int32, v.shape, 0)
    s = 1
    while s < HG_CHUNK:
        if reverse:
            v = v + jnp.where(row < HG_CHUNK - s, pltpu.roll(v, HG_CHUNK - s, 0), 0.0)
        else:
            v = v + jnp.where(row >= s, pltpu.roll(v, s, 0), 0.0)
        s *= 2
    return v


def _hgrn_fwd(zmain, lb_logits, *, T):
    S = zmain.shape[0]
    nc = T // HG_CHUNK

    def body(q_ref, f_ref, v_ref, lbl_ref, o_ref, st_ref, state):
        @pl.when(pl.program_id(1) == 0)
        def _():
            state[...] = jnp.zeros_like(state)

        lb = _lower_bound(lbl_ref)
        tril = _tri(True)
        qis, updates, decays, intra = [], [], [], []
        for c in range(nc):
            sl = pl.ds(c * HG_CHUNK, HG_CHUNK)
            _, _, g, k = _hg_gates(_f32(f_ref[sl, :]), lb)
            b = _scan_rows(g)
            bl = jnp.sum(g, axis=0, keepdims=True)
            qi = _bf(_f32(q_ref[sl, :]) * jnp.exp(b))
            ki = _bf(k * jnp.exp(-b))
            ko = _bf(k * jnp.exp(bl - b))
            vb = _bf(v_ref[sl, :])
            att = jnp.where(tril, _dot_nt(qi, ki), 0.0)
            intra.append(_dot(_bf(att), vb))
            qis.append(qi)
            updates.append(_dot_tn(vb, ko))
            decays.append(jnp.exp(bl))
        st = state[...]
        for c in range(nc):
            st_ref[0, c] = st
            o_ref[pl.ds(c * HG_CHUNK, HG_CHUNK), :] = intra[c] + _dot_nt(qis[c], _bf(st))
            st = st * decays[c] + updates[c]
        state[...] = st

    col = lambda base: pl.BlockSpec((T, HG_DK), lambda h, t: (t, base + h))
    return pl.pallas_call(
        body,
        grid=(HG_HEADS, S // T),
        in_specs=[col(0), col(8), col(16), pl.BlockSpec((2, HG_DK), lambda h, t: (0, h))],
        out_specs=[
            pl.BlockSpec((T, HG_DK), lambda h, t: (t, h)),
            pl.BlockSpec((1, nc, HG_DK, HG_DK), lambda h, t: (h, t, 0, 0)),
        ],
        out_shape=[
            jax.ShapeDtypeStruct((S, D_MODEL), F32),
            jax.ShapeDtypeStruct((HG_HEADS, S // HG_CHUNK, HG_DK, HG_DK), F32),
        ],
        scratch_shapes=[pltpu.VMEM((HG_DK, HG_DK), F32)],
        compiler_params=_cparams("parallel", "arbitrary"),
        name="hgrn_fwd",
    )(zmain, zmain, zmain, lb_logits)


def _hgrn_bwd(zmain, lb_logits, states, d_o, *, T):
    S = zmain.shape[0]
    nc = T // HG_CHUNK
    nt = S // T

    def body(q_ref, f_ref, v_ref, lbl_ref, st_ref, do_ref, dz_ref, dlb_ref, dstate):
        @pl.when(pl.program_id(1) == 0)
        def _():
            dstate[...] = jnp.zeros_like(dstate)
            dlb_ref[...] = jnp.zeros_like(dlb_ref)

        lb = _lower_bound(lbl_ref)
        tril = _tri(True)
        last_row = lax.broadcasted_iota(jnp.int32, (HG_CHUNK, HG_DK), 0) == HG_CHUNK - 1
        saved = []
        for c in range(nc):
            sl = pl.ds(c * HG_CHUNK, HG_CHUNK)
            sg, f, g, k = _hg_gates(_f32(f_ref[sl, :]), lb)
            b = _scan_rows(g)
            bl = jnp.sum(g, axis=0, keepdims=True)
            eb = jnp.exp(b)
            enb = jnp.exp(-b)
            eo = jnp.exp(bl - b)
            q_in = _f32(q_ref[sl, :]) * eb
            k_in = k * enb
            k_out = k * eo
            qi, ki, ko = _bf(q_in), _bf(k_in), _bf(k_out)
            vb = _bf(v_ref[sl, :])
            dob = do_ref[sl, :]
            att = jnp.where(tril, _dot_nt(qi, ki), 0.0)
            d_att = _bf(jnp.where(tril, _dot_nt(dob, vb), 0.0))
            d_kin = _dot_tn(d_att, qi)
            saved.append(dict(
                sg=sg, f=f, eb=eb, enb=enb, eo=eo, ebl=jnp.exp(bl), k_out=k_out, ko=ko, vb=vb, dob=dob,
                d_v=_dot_tn(_bf(att), dob), d_qin=_dot(d_att, ki), d_kin=d_kin,
                qk=(q_in, k_in), d_state=_dot_tn(dob, qi)))
        dst = dstate[...]
        dsts = [None] * nc
        for c in reversed(range(nc)):
            dsts[c] = dst
            dst = dst * saved[c]["ebl"] + saved[c]["d_state"]
        dstate[...] = dst
        dlb = jnp.zeros((1, HG_DK), F32)
        for c in range(nc):
            sl = pl.ds(c * HG_CHUNK, HG_CHUNK)
            s = saved[c]
            q_in, k_in = s["qk"]
            st = st_ref[0, c]
            dstb = _bf(dsts[c])
            d_v = s["d_v"] + _dot_nt(s["ko"], dstb)
            d_qin = s["d_qin"] + _dot(s["dob"], _bf(st))
            d_kout = _dot(s["vb"], dstb)
            d_decay = jnp.sum(dsts[c] * st, axis=0, keepdims=True)
            kk = d_kout * s["k_out"]
            d_b = d_qin * q_in - s["d_kin"] * k_in - kk
            d_bl = jnp.sum(kk, axis=0, keepdims=True) + d_decay * s["ebl"]
            d_g = _scan_rows(d_b + jnp.where(last_row, d_bl, 0.0), reverse=True)
            d_f = d_g / s["f"] - (s["d_kin"] * s["enb"] + d_kout * s["eo"])
            dz_ref[sl, 0:HG_DK] = _bf(d_qin * s["eb"])
            dz_ref[sl, HG_DK:2 * HG_DK] = _bf(d_f * (1.0 - lb) * s["sg"] * (1.0 - s["sg"]))
            dz_ref[sl, 2 * HG_DK:3 * HG_DK] = _bf(d_v)
            dlb = dlb + jnp.sum(d_f * (1.0 - s["sg"]), axis=0, keepdims=True)
        dlb_ref[...] += dlb

    rev = lambda base: pl.BlockSpec((T, HG_DK), lambda h, t: (nt - 1 - t, base + h))
    outc = pl.BlockSpec((T, HG_DK), lambda h, t: (nt - 1 - t, h))
    return pl.pallas_call(
        body,
        grid=(HG_HEADS, nt),
        in_specs=[
            rev(0), rev(8), rev(16),
            pl.BlockSpec((2, HG_DK), lambda h, t: (0, h)),
            pl.BlockSpec((1, nc, HG_DK, HG_DK), lambda h, t: (h, nt - 1 - t, 0, 0)),
            outc,
        ],
        out_specs=[pl.BlockSpec((T, 3 * HG_DK), lambda h, t: (nt - 1 - t, h)),
                   pl.BlockSpec((1, HG_DK), lambda h, t: (0, h))],
        out_shape=[jax.ShapeDtypeStruct((S, 3 * D_MODEL), BF16), jax.ShapeDtypeStruct((1, D_MODEL), F32)],
        scratch_shapes=[pltpu.VMEM((HG_DK, HG_DK), F32)],
        compiler_params=_cparams("parallel", "arbitrary"),
        name="hgrn_bwd",
    )(zmain, zmain, zmain, lb_logits, states, d_o)


def _t5_bucket_table():
    qi = jnp.arange(SWA_BLOCK)[:, None] + SWA_BLOCK
    kj = jnp.arange(2 * SWA_BLOCK)[None, :]
    n = jnp.clip(qi - kj, 0, SWA_WINDOW - 1)
    max_exact = NUM_BUCKETS // 2
    nf = jnp.maximum(n, 1).astype(F32)
    large = max_exact + (jnp.log(nf / max_exact) / math.log(MAX_DISTANCE / max_exact)
                         * (NUM_BUCKETS - max_exact)).astype(jnp.int32)
    large = jnp.minimum(large, NUM_BUCKETS - 1)
    return jnp.where(n < max_exact, n, large).astype(jnp.int32)


SWA_ROWS = 32


def _swa_bias_init(bias, bucket_ref, rb_ref):
    bk = bucket_ref[...]
    qi = lax.broadcasted_iota(jnp.int32, bk.shape, 0) + SWA_BLOCK
    kj = lax.broadcasted_iota(jnp.int32, bk.shape, 1)
    band = (qi - kj >= 0) & (qi - kj < SWA_WINDOW)
    for h in range(SWA_HEADS):
        def sel(b, acc, h=h):
            return jnp.where(bk == b, rb_ref[b, h], acc)
        t = lax.fori_loop(0, NUM_BUCKETS, sel, jnp.zeros(bk.shape, F32))
        bias[1, h] = jnp.where(band, t, -jnp.inf)
        bias[0, h] = jnp.where(band & (kj >= SWA_BLOCK), t, -jnp.inf)


def _lane_halves(t, kv_head):
    lane = lax.broadcasted_iota(jnp.int32, t.shape, 1)
    rolled = pltpu.roll(t, 64, 1)
    zero = jnp.zeros_like(t)
    if kv_head == 0:
        return jnp.where(lane < 64, t, zero), jnp.where(lane >= 64, rolled, zero)
    return jnp.where(lane < 64, rolled, zero), jnp.where(lane >= 64, t, zero)


def _swa_zero_key0(t):
    return jnp.where(lax.broadcasted_iota(jnp.int32, t.shape, 0) == 0, jnp.zeros_like(t), t)


def _swa_probs(s, masked_bias, sink):
    s = s + masked_bias
    m = jnp.maximum(jnp.max(s, axis=-1, keepdims=True), sink)
    p = jnp.exp(s - m)
    es = jnp.exp(sink - m)
    inv = 1.0 / (jnp.sum(p, axis=-1, keepdims=True) + es)
    return p * inv, es * inv


def _swa_fwd(zmain, bucket, rel_bias, sinks):
    S = zmain.shape[0]
    nb = S // SWA_BLOCK
    scale = SWA_HEAD_DIM ** -0.5

    def body(q_ref, kvc_ref, kvp_ref, bucket_ref, rb_ref, sk_ref, o_ref, p_ref, bias):
        n = pl.program_id(0)

        @pl.when(n == 0)
        def _():
            _swa_bias_init(bias, bucket_ref, rb_ref)

        later = jnp.minimum(n, 1)
        kk = _bf(jnp.concatenate([kvp_ref[:, 0:128], kvc_ref[:, 0:128]], axis=0))
        vv = _swa_zero_key0(_bf(jnp.concatenate([kvp_ref[:, 128:256], kvc_ref[:, 128:256]], axis=0)))
        first_col = lax.broadcasted_iota(jnp.int32, (SWA_ROWS, 2 * SWA_BLOCK), 1) == 0
        for kvh in range(2):
            ka, kb = _lane_halves(kk, kvh)
            va, vb = _lane_halves(vv, kvh)
            qst = _bf(jnp.concatenate([q_ref[:, pl.ds((kvh * 4 + jj) * 128, 128)] for jj in range(4)], axis=0) * scale)
            probs = []
            for odd, kx in enumerate((ka, kb)):
                s = _dot_nt(qst, kx)
                parts = []
                for jj in range(4):
                    h = 2 * (kvh * 4 + jj) + odd
                    for r0 in range(0, SWA_BLOCK, SWA_ROWS):
                        p, ps = _swa_probs(s[jj * SWA_BLOCK + r0:jj * SWA_BLOCK + r0 + SWA_ROWS],
                                           bias[later, h, pl.ds(r0, SWA_ROWS), :], sk_ref[0, h])
                        part = _bf(jnp.where(first_col, ps, p))
                        p_ref[pl.ds(r0, SWA_ROWS), pl.ds(h * 2 * SWA_BLOCK, 2 * SWA_BLOCK)] = part
                        parts.append(part)
                probs.append(jnp.concatenate(parts, axis=0))
            ost = _dot(probs[0], va) + _dot(probs[1], vb)
            for jj in range(4):
                o_ref[:, pl.ds((kvh * 4 + jj) * 128, 128)] = ost[jj * SWA_BLOCK:(jj + 1) * SWA_BLOCK]

    smem = pl.BlockSpec(memory_space=pltpu.SMEM)
    return pl.pallas_call(
        body,
        grid=(nb,),
        in_specs=[
            pl.BlockSpec((SWA_BLOCK, 1024), lambda n: (n, C_SQ // 1024)),
            pl.BlockSpec((SWA_BLOCK, 256), lambda n: (n, C_SK // 256)),
            pl.BlockSpec((SWA_BLOCK, 256), lambda n: (jnp.maximum(n - 1, 0), C_SK // 256)),
            _const_spec((SWA_BLOCK, 2 * SWA_BLOCK)), smem, smem,
        ],
        out_specs=[pl.BlockSpec((SWA_BLOCK, 1024), lambda n: (n, 0)),
                   pl.BlockSpec((SWA_BLOCK, SWA_HEADS * 2 * SWA_BLOCK), lambda n: (n, 0))],
        out_shape=[jax.ShapeDtypeStruct((S, 1024), F32),
                   jax.ShapeDtypeStruct((S, SWA_HEADS * 2 * SWA_BLOCK), BF16)],
        scratch_shapes=[pltpu.VMEM((2, SWA_HEADS, SWA_BLOCK, 2 * SWA_BLOCK), F32)],
        compiler_params=_cparams("arbitrary"),
        name="swa_fwd",
    )(zmain, zmain, zmain, bucket, rel_bias, sinks)


def _swa_bwd(zmain, o_b, probs, d_o, bucket, dep):
    S = zmain.shape[0]
    nb = S // SWA_BLOCK
    scale = SWA_HEAD_DIM ** -0.5

    def body(q_ref, kvc_ref, kvp_ref, o_ref, p_ref, do_ref, bucket_ref, dep_ref,
             dq_ref, dkv_ref, drb_ref, dsk_ref, dbias, carry):
        del dep_ref
        n = pl.program_id(0)

        @pl.when(n == 0)
        def _():
            dbias[...] = jnp.zeros_like(dbias)
            carry[...] = jnp.zeros_like(carry)

        @pl.when(n < nb)
        def _():
            kk = _swa_zero_key0(_bf(jnp.concatenate([kvp_ref[:, 0:128], kvc_ref[:, 0:128]], axis=0)))
            vv = _swa_zero_key0(_bf(jnp.concatenate([kvp_ref[:, 128:256], kvc_ref[:, 128:256]], axis=0)))
            lane = lax.broadcasted_iota(jnp.int32, (2 * SWA_BLOCK, 128), 1)
            lane_q = lax.broadcasted_iota(jnp.int32, (4 * SWA_BLOCK, 128), 1)
            dk_parts, dv_parts = [], []
            for kvh in range(2):
                ka, kb = _lane_halves(kk, kvh)
                va, vb = _lane_halves(vv, kvh)
                pair_cols = [pl.ds((kvh * 4 + jj) * 128, 128) for jj in range(4)]
                qst = _bf(jnp.concatenate([q_ref[:, cl] for cl in pair_cols], axis=0) * scale)
                dost = jnp.concatenate([do_ref[:, cl] for cl in pair_cols], axis=0)
                prod = dost.astype(F32) * jnp.concatenate([o_ref[:, cl] for cl in pair_cols], axis=0)
                dq_st = jnp.zeros((4 * SWA_BLOCK, 128), F32)
                zks, zvs = [], []
                for odd, (kx, vx) in enumerate(((ka, va), (kb, vb))):
                    keep = (lane_q >= 64) if odd else (lane_q < 64)
                    delta = jnp.sum(jnp.where(keep, prod, 0.0), axis=-1, keepdims=True)
                    dp = _dot_nt(dost, vx)
                    p_parts, ds_parts = [], []
                    for jj in range(4):
                        h = 2 * (kvh * 4 + jj) + odd
                        rows = slice(jj * SWA_BLOCK, (jj + 1) * SWA_BLOCK)
                        p = p_ref[:, pl.ds(h * 2 * SWA_BLOCK, 2 * SWA_BLOCK)]
                        ds = _f32(p) * (dp[rows] - delta[rows])
                        dbias[h] += ds
                        p_parts.append(p)
                        ds_parts.append(_bf(ds))
                    pst = jnp.concatenate(p_parts, axis=0)
                    dsst = jnp.concatenate(ds_parts, axis=0)
                    dq_st = dq_st + _dot(dsst, kx)
                    zks.append(_dot_tn(dsst, qst))
                    zvs.append(_dot_tn(pst, dost))
                for jj in range(4):
                    dq_ref[:, pair_cols[jj]] = _bf(dq_st[jj * SWA_BLOCK:(jj + 1) * SWA_BLOCK] * scale)
                zk = jnp.where(lane < 64, zks[0], zks[1])
                zv = jnp.where(lane < 64, zvs[0], zvs[1])
                dk_parts.append(zk + pltpu.roll(zk, 64, 1))
                dv_parts.append(zv + pltpu.roll(zv, 64, 1))
            dk = jnp.where(lane < 64, dk_parts[0], dk_parts[1])
            dv = jnp.where(lane < 64, dv_parts[0], dv_parts[1])
            dkv = _swa_zero_key0(jnp.concatenate([dk, dv], axis=1))
            dkv_ref[...] = _bf(carry[...] + dkv[0:SWA_BLOCK])
            carry[...] = dkv[SWA_BLOCK:]

        @pl.when(n == nb)
        def _():
            dkv_ref[...] = _bf(carry[...])
            first_col = lax.broadcasted_iota(jnp.int32, (SWA_BLOCK, 2 * SWA_BLOCK), 1) == 0
            bk = jnp.where(first_col, -1, bucket_ref[...])

            row = lax.broadcasted_iota(jnp.int32, (NUM_BUCKETS, 128), 0)
            lane = lax.broadcasted_iota(jnp.int32, (NUM_BUCKETS, 128), 1)

            def total(v):
                return jnp.sum(jnp.sum(v, axis=1, keepdims=True), axis=0, keepdims=True)

            def per_head(h, acc):
                db = dbias[h]
                d_rb, d_sk = acc
                d_sk = d_sk + jnp.where((row == 0) & (lane == h), total(jnp.where(first_col, db, 0.0)), 0.0)

                def per_bucket(b, d_rb):
                    return d_rb + jnp.where((row == b) & (lane == h), total(jnp.where(bk == b, db, 0.0)), 0.0)

                return lax.fori_loop(0, NUM_BUCKETS, per_bucket, d_rb), d_sk

            zero = jnp.zeros((NUM_BUCKETS, 128), F32)
            d_rb, d_sk = lax.fori_loop(0, SWA_HEADS, per_head, (zero, zero))
            drb_ref[...] = d_rb
            dsk_ref[...] = d_sk[0:8]

    cur = lambda n: jnp.minimum(n, nb - 1)
    prev = lambda n: jnp.maximum(jnp.minimum(n, nb - 1) - 1, 0)
    return pl.pallas_call(
        body,
        grid=(nb + 1,),
        in_specs=[
            pl.BlockSpec((SWA_BLOCK, 1024), lambda n: (cur(n), C_SQ // 1024)),
            pl.BlockSpec((SWA_BLOCK, 256), lambda n: (cur(n), C_SK // 256)),
            pl.BlockSpec((SWA_BLOCK, 256), lambda n: (prev(n), C_SK // 256)),
            pl.BlockSpec((SWA_BLOCK, 1024), lambda n: (cur(n), 0)),
            pl.BlockSpec((SWA_BLOCK, SWA_HEADS * 2 * SWA_BLOCK), lambda n: (cur(n), 0)),
            pl.BlockSpec((SWA_BLOCK, 1024), lambda n: (cur(n), 0)),
            _const_spec((SWA_BLOCK, 2 * SWA_BLOCK)), _dep_spec(),
        ],
        out_specs=[
            pl.BlockSpec((SWA_BLOCK, 1024), lambda n: (cur(n), 0)),
            pl.BlockSpec((SWA_BLOCK, 256), lambda n: (jnp.maximum(n - 1, 0), 0)),
            pl.BlockSpec((NUM_BUCKETS, 128), lambda n: (0, 0)),
            pl.BlockSpec((8, 128), lambda n: (0, 0)),
        ],
        out_shape=[
            jax.ShapeDtypeStruct((S, 1024), BF16),
            jax.ShapeDtypeStruct((S, 256), BF16),
            jax.ShapeDtypeStruct((NUM_BUCKETS, 128), F32),
            jax.ShapeDtypeStruct((8, 128), F32),
        ],
        scratch_shapes=[
            pltpu.VMEM((SWA_HEADS, SWA_BLOCK, 2 * SWA_BLOCK), F32),
            pltpu.VMEM((SWA_BLOCK, 256), F32),
        ],
        compiler_params=_cparams("arbitrary"),
        name="swa_bwd",
    )(zmain, zmain, zmain, o_b, probs, d_o, bucket, dep)


def _mem_probs(q_ref, k):
    qs = _bf(q_ref[...] * (MEM_HEAD_DIM ** -0.5))
    s = _dot_nt(qs, k)
    e = jnp.exp(s - jnp.max(s, axis=-1, keepdims=True))
    return qs, e * (1.0 / jnp.sum(e, axis=-1, keepdims=True))


def _mem_q_specs(T):
    return [pl.BlockSpec((T, MEM_HEAD_DIM), lambda t, h=h: (t, C_MQ // MEM_HEAD_DIM + h)) for h in range(MEM_HEADS)]


def _mem_kv_proj(mem, g2):
    def body(mem_ref, w_ref, o_ref):
        o_ref[...] = _dot_nt(_bf(mem_ref[...]), _rows(w_ref))

    return pl.pallas_call(
        body,
        grid=(1,),
        in_specs=[pl.BlockSpec((MEM_LEN, D_MODEL), lambda i: (0, 0)), _gathered_spec(R_KV, R_OTHER)],
        out_specs=pl.BlockSpec((MEM_LEN, 2048), lambda i: (0, 0)),
        out_shape=jax.ShapeDtypeStruct((MEM_LEN, 2048), F32),
        compiler_params=_cparams("arbitrary"),
        name="mem_kv_proj",
    )(mem, g2)


def _mem_fwd(zmain, mkv, *, T):
    S = zmain.shape[0]

    def body(q0, q1, q2, q3, kv_ref, o_ref, p_ref):
        for h, q_ref in enumerate((q0, q1, q2, q3)):
            cols = pl.ds(h * MEM_HEAD_DIM, MEM_HEAD_DIM)
            _, p = _mem_probs(q_ref, _bf(kv_ref[:, cols]))
            pb = _bf(p)
            p_ref[:, cols] = pb
            o_ref[:, cols] = _dot(pb, _bf(kv_ref[:, pl.ds(1024 + h * MEM_HEAD_DIM, MEM_HEAD_DIM)]))

    row = pl.BlockSpec((T, 1024), lambda t: (t, 0))
    return pl.pallas_call(
        body,
        grid=(S // T,),
        in_specs=_mem_q_specs(T) + [_const_spec((MEM_LEN, 2048))],
        out_specs=[row, row],
        out_shape=[jax.ShapeDtypeStruct((S, 1024), F32), jax.ShapeDtypeStruct((S, 1024), BF16)],
        compiler_params=_cparams("parallel"),
        name="mem_fwd",
    )(zmain, zmain, zmain, zmain, mkv)


def _mem_bwd(zmain, mkv, o_c, probs, d_o, *, T):
    S = zmain.shape[0]
    scale = MEM_HEAD_DIM ** -0.5

    def body(q0, q1, q2, q3, kv_ref, o_ref, p_ref, do_ref, dq_ref, dkv_ref):
        @pl.when(pl.program_id(0) == 0)
        def _():
            dkv_ref[...] = jnp.zeros_like(dkv_ref)

        for h, q_ref in enumerate((q0, q1, q2, q3)):
            cols = pl.ds(h * MEM_HEAD_DIM, MEM_HEAD_DIM)
            vcols = pl.ds(1024 + h * MEM_HEAD_DIM, MEM_HEAD_DIM)
            kb = _bf(kv_ref[:, cols])
            qs = _bf(q_ref[...] * scale)
            pb = p_ref[:, cols]
            dob = do_ref[:, cols]
            delta = jnp.sum(dob.astype(F32) * o_ref[:, cols], axis=-1, keepdims=True)
            ds = _bf(_f32(pb) * (_dot_nt(dob, _bf(kv_ref[:, vcols])) - delta))
            dq_ref[:, cols] = _bf(_dot(ds, kb) * scale)
            dkv_ref[:, cols] += _dot_tn(ds, qs)
            dkv_ref[:, vcols] += _dot_tn(pb, dob)

    row = pl.BlockSpec((T, 1024), lambda t: (t, 0))
    return pl.pallas_call(
        body,
        grid=(S // T,),
        in_specs=_mem_q_specs(T) + [_const_spec((MEM_LEN, 2048)), row, row, row],
        out_specs=[row, pl.BlockSpec((MEM_LEN, 2048), lambda t: (0, 0))],
        out_shape=[jax.ShapeDtypeStruct((S, 1024), BF16), jax.ShapeDtypeStruct((MEM_LEN, 2048), F32)],
        compiler_params=_cparams("arbitrary"),
        name="mem_bwd",
    )(zmain, zmain, zmain, zmain, mkv, o_c, probs, d_o)


def _layer_norm(u):
    mu = jnp.mean(u, axis=-1, keepdims=True)
    xc = u - mu
    rstd = lax.rsqrt(jnp.mean(xc * xc, axis=-1, keepdims=True) + LN_EPS)
    return xc * rstd, rstd


def _layer_norm_bwd(dy, gamma, xhat, rstd):
    dxh = dy * gamma
    return rstd * (dxh - jnp.mean(dxh, axis=-1, keepdims=True) - xhat * jnp.mean(dxh * xhat, axis=-1, keepdims=True))


def _merge_forward(oraw_ref, hg_ref, ob_ref, oc_ref, gl_ref, x_ref, gain_ref, wbh, wbs, wbm, wout):
    ys, rs = [], []
    for h in range(HG_HEADS):
        oh = oraw_ref[:, pl.ds(h * HG_DK, HG_DK)]
        r = lax.rsqrt(jnp.mean(oh * oh, axis=-1, keepdims=True) + RMS_EPS)
        ys.append(oh * r)
        rs.append(r)
    y = jnp.concatenate(ys, axis=1)
    hg = _f32(hg_ref[...])
    sg = _sig(hg)
    silu = hg * sg
    oa = _bf(y * gain_ref[...] * silu)
    pa = _dot(oa, _rows(wbh))
    pb = _dot(_bf(ob_ref[...]), _rows(wbs))
    pc = _dot(_bf(oc_ref[...]), _rows(wbm))
    g0 = _sig(_f32(gl_ref[:, 0:1024]))
    g1 = _sig(_f32(gl_ref[:, 1024:2048]))
    g2 = _sig(_f32(gl_ref[:, 2048:3072]))
    m = _bf(g0 * pa + g1 * pb + g2 * pc)
    u1 = ALPHA * x_ref[...] + _dot(m, _rows(wout))
    xhat, rstd = _layer_norm(u1)
    return dict(y=y, rs=rs, hg=hg, sg=sg, silu=silu, oa=oa, pa=pa, pb=pb, pc=pc,
                g0=g0, g1=g1, g2=g2, m=m, xhat=xhat, rstd=rstd)


def _gathered_spec(lo, hi):
    n = hi - lo
    return pl.BlockSpec((N_DEV, n, D_MODEL), lambda *_: (0, lo // n, 0), pipeline_mode=pl.Buffered(1))


def _rows(w_ref):
    return w_ref[...].reshape(-1, D_MODEL)


def _merge_in_specs(T):
    row = lambda w, c=0: pl.BlockSpec((T, w), lambda i: (i, c))
    vec = pl.BlockSpec((1, D_MODEL), lambda i: (0, 0))
    w = [_gathered_spec(lo, hi) for lo, hi in ((R_BH, R_BS), (R_BS, R_BM), (R_BM, R_OUT), (R_OUT, R_KV))]
    return [row(1024), row(1024, C_HG // 1024), row(1024), row(1024), row(3072), row(1024), vec, *w, vec, vec]


def _merge_fwd(o_raw, zmain, o_b, o_c, gl, x, gain, wbh, wbs, wbm, wout, ln_g, ln_b, *, T):
    S = x.shape[0]

    def body(oraw_ref, hg_ref, ob_ref, oc_ref, gl_ref, x_ref, gain_ref, wbh_r, wbs_r, wbm_r, wout_r, g_ref, b_ref,
             h1_ref, h1b_ref):
        f = _merge_forward(oraw_ref, hg_ref, ob_ref, oc_ref, gl_ref, x_ref, gain_ref, wbh_r, wbs_r, wbm_r, wout_r)
        h1 = f["xhat"] * g_ref[...] + b_ref[...]
        h1_ref[...] = h1
        h1b_ref[...] = _bf(h1)

    row = pl.BlockSpec((T, D_MODEL), lambda i: (i, 0))
    return pl.pallas_call(
        body,
        grid=(S // T,),
        in_specs=_merge_in_specs(T),
        out_specs=[row, row],
        out_shape=[jax.ShapeDtypeStruct((S, D_MODEL), F32), jax.ShapeDtypeStruct((S, D_MODEL), BF16)],
        compiler_params=_cparams("parallel"),
        name="merge_fwd",
    )(o_raw, zmain, o_b, o_c, gl, x, gain, wbh, wbs, wbm, wout, ln_g, ln_b)


def _merge_bwd(d_h1, o_raw, zmain, o_b, o_c, gl, x, gain, wbh, wbs, wbm, wout, ln_g, ln_b, *, T):
    S = x.shape[0]

    def body(dh1_ref, oraw_ref, hg_ref, ob_ref, oc_ref, gl_ref, x_ref, gain_ref, wbh_r, wbs_r, wbm_r, wout_r, g_ref, b_ref,
             dx_ref, du1_ref, m_ref, oa_ref, dpa_ref, dpb_ref, dpc_ref, doraw_ref, dob_ref, doc_ref, dz_ref,
             dgain_ref, dg_ref, db_ref):
        del b_ref

        @pl.when(pl.program_id(0) == 0)
        def _():
            dgain_ref[...] = jnp.zeros_like(dgain_ref)
            dg_ref[...] = jnp.zeros_like(dg_ref)
            db_ref[...] = jnp.zeros_like(db_ref)

        f = _merge_forward(oraw_ref, hg_ref, ob_ref, oc_ref, gl_ref, x_ref, gain_ref, wbh_r, wbs_r, wbm_r, wout_r)
        dh1 = dh1_ref[...]
        dg_ref[...] += jnp.sum(dh1 * f["xhat"], axis=0, keepdims=True)
        db_ref[...] += jnp.sum(dh1, axis=0, keepdims=True)
        du1 = _layer_norm_bwd(dh1, g_ref[...], f["xhat"], f["rstd"])
        dx_ref[...] = ALPHA * du1
        du1b = _bf(du1)
        du1_ref[...] = du1b
        m_ref[...] = f["m"]
        oa_ref[...] = f["oa"]
        dm = _dot_nt(du1b, _rows(wout_r))
        for i, (g, p, dp_ref, dob_r, w_r) in enumerate((
                (f["g0"], f["pa"], dpa_ref, None, wbh_r),
                (f["g1"], f["pb"], dpb_ref, dob_ref, wbs_r),
                (f["g2"], f["pc"], dpc_ref, doc_ref, wbm_r))):
            dz_ref[:, pl.ds((i + 1) * 1024, 1024)] = _bf(dm * p * g * (1.0 - g))
            dp = _bf(dm * g)
            dp_ref[...] = dp
            d_branch = _dot_nt(dp, _rows(w_r))
            if dob_r is not None:
                dob_r[...] = _bf(d_branch)
            else:
                doa = d_branch
        gain = gain_ref[...]
        t = doa * f["y"]
        dgain_ref[...] += jnp.sum(t * f["silu"], axis=0, keepdims=True)
        sg = f["sg"]
        dz_ref[:, 0:1024] = _bf(t * gain * sg * (1.0 + f["hg"] * (1.0 - sg)))
        dy = doa * gain * f["silu"]
        for h in range(HG_HEADS):
            cols = slice(h * HG_DK, (h + 1) * HG_DK)
            yh = f["y"][:, cols]
            dyh = dy[:, cols]
            doraw_ref[:, pl.ds(h * HG_DK, HG_DK)] = _bf(
                f["rs"][h] * (dyh - yh * jnp.mean(dyh * yh, axis=-1, keepdims=True)))

    row = lambda w: pl.BlockSpec((T, w), lambda i: (i, 0))
    vec = pl.BlockSpec((1, D_MODEL), lambda i: (0, 0))
    bshape = jax.ShapeDtypeStruct((S, D_MODEL), BF16)
    vshape = jax.ShapeDtypeStruct((1, D_MODEL), F32)
    return pl.pallas_call(
        body,
        grid=(S // T,),
        in_specs=[row(1024)] + _merge_in_specs(T),
        out_specs=[row(1024)] * 10 + [row(4096), vec, vec, vec],
        out_shape=[jax.ShapeDtypeStruct((S, D_MODEL), F32)] + [bshape] * 9
        + [jax.ShapeDtypeStruct((S, 4096), BF16), vshape, vshape, vshape],
        compiler_params=_cparams("arbitrary"),
        name="merge_bwd",
    )(d_h1, o_raw, zmain, o_b, o_c, gl, x, gain, wbh, wbs, wbm, wout, ln_g, ln_b)


def _mlp_fwd_bwd(h1, target, wup_t, wdn, ln_g, ln_b, *, T, FC):
    S = h1.shape[0]
    nf = D_FF // FC
    assert FC == R_BH - R_UP == R_UP - R_DN

    def body(h1_ref, t_ref, wup_ref, wdn_ref, g_ref, b_ref, dh1_ref, a_ref, dup_ref, du2_ref, loss_ref, dg_ref, db_ref, up_scr):
        @pl.when(pl.program_id(0) == 0)
        def _():
            loss_ref[...] = jnp.zeros_like(loss_ref)
            dg_ref[...] = jnp.zeros_like(dg_ref)
            db_ref[...] = jnp.zeros_like(db_ref)

        h1v = h1_ref[...]
        h1b = _bf(h1v)
        ff = jnp.zeros((T, D_MODEL), F32)
        for j in range(nf):
            rows = pl.ds(j * FC, FC)
            up = jnp.maximum(_dot_nt(h1b, wup_ref[j]), 0.0)
            up_scr[:, rows] = _bf(up)
            a = _bf(up * up)
            a_ref[:, rows] = a
            ff = ff + _dot(a, wdn_ref[j])
        xhat, rstd = _layer_norm(ALPHA * h1v + ff)
        gamma = g_ref[...]
        err = xhat * gamma + b_ref[...] - t_ref[...]
        loss_ref[...] += jnp.sum(jnp.sum(err * err, axis=-1, keepdims=True), axis=0, keepdims=True) * (0.5 / D_MODEL)
        dy = err * (1.0 / D_MODEL)
        dg_ref[...] += jnp.sum(dy * xhat, axis=0, keepdims=True)
        db_ref[...] += jnp.sum(dy, axis=0, keepdims=True)
        du2 = _layer_norm_bwd(dy, gamma, xhat, rstd)
        du2b = _bf(du2)
        du2_ref[...] = du2b
        dh1 = ALPHA * du2
        for j in range(nf):
            rows = pl.ds(j * FC, FC)
            dup = _bf(_dot_nt(du2b, wdn_ref[j]) * (2.0 * up_scr[:, rows].astype(F32)))
            dup_ref[:, rows] = dup
            dh1 = dh1 + _dot(dup, wup_ref[j])
        dh1_ref[...] = dh1

    row = lambda w: pl.BlockSpec((T, w), lambda i: (i, 0))
    vec = pl.BlockSpec((1, D_MODEL), lambda i: (0, 0))
    vshape = jax.ShapeDtypeStruct((1, D_MODEL), F32)
    return pl.pallas_call(
        body,
        grid=(S // T,),
        in_specs=[row(1024), row(1024), _gathered_spec(R_UP, R_BH), _gathered_spec(R_DN, R_UP), vec, vec],
        out_specs=[row(1024), row(D_FF), row(D_FF), row(1024), pl.BlockSpec((8, 128), lambda i: (0, 0)), vec, vec],
        out_shape=[
            jax.ShapeDtypeStruct((S, D_MODEL), F32),
            jax.ShapeDtypeStruct((S, D_FF), BF16),
            jax.ShapeDtypeStruct((S, D_FF), BF16),
            jax.ShapeDtypeStruct((S, D_MODEL), BF16),
            jax.ShapeDtypeStruct((8, 128), F32), vshape, vshape,
        ],
        scratch_shapes=[pltpu.VMEM((T, D_FF), BF16)],
        compiler_params=_cparams("arbitrary"),
        name="mlp_fwd_bwd",
    )(h1, target, wup_t, wdn, ln_g, ln_b)


def _local_step(x, mem, target, lb_logits, gain, sinks, rel_bias, ln1_g, ln1_b, ln2_g, ln2_b,
                win_t, dep0, other_weights, send_other_grads, send_small_grads, send_win_grad):
    S = x.shape[0]
    T = min(256, S)
    KC = min(2048, S)
    zmain, gl, xb = _in_proj(x, win_t, dep0, tm=min(512, S))
    bucket = _t5_bucket_table()

    o_raw, states = _hgrn_fwd(zmain, lb_logits, T=min(1024, S))
    o_b, swa_probs = _swa_fwd(zmain, bucket, rel_bias, sinks)
    g2 = other_weights((o_b, o_raw))
    mkv = _mem_kv_proj(mem, g2)
    o_c, mem_probs = _mem_fwd(zmain, mkv, T=min(512, S))
    merge_args = (o_raw, zmain, o_b, o_c, gl, x, gain, g2, g2, g2, g2, ln1_g, ln1_b)
    h1, h1b = _merge_fwd(*merge_args, T=min(512, S))

    d_h1, act, d_up, du2, loss, d_ln2_g, d_ln2_b = _mlp_fwd_bwd(h1, target, g2, g2, ln2_g, ln2_b, T=min(512, S), FC=512)
    wgrad = functools.partial(_mm_tn, out_dtype=BF16)
    g_wdn = wgrad(act, du2, kc=KC, name="grad_w_down")
    g_wup_t = wgrad(d_up, h1b, kc=KC, name="grad_w_up")

    (dx_part, du1, m, oa, dpa, dpb, dpc, d_oraw, d_ob, d_oc, d_hg_gl,
     d_gain, d_ln1_g, d_ln1_b) = _merge_bwd(d_h1, *merge_args, T=T)
    g_wout = wgrad(m, du1, kc=KC, name="grad_w_out")
    g_wbh = wgrad(oa, dpa, kc=KC, name="grad_w_branch_hg")
    g_wbs = wgrad(o_b, dpb, kc=KC, name="grad_w_branch_swa")
    g_wbm = wgrad(o_c, dpc, kc=KC, name="grad_w_branch_mem")

    d_mq, d_mkv = _mem_bwd(zmain, mkv, o_c, mem_probs, d_oc, T=min(512, S))
    g_wkv_t = wgrad(d_mkv, mem, kc=MEM_LEN, name="grad_w_mem_kv")
    sent_others = send_other_grads(
        dict(wkv_t=g_wkv_t, wbh=g_wbh, wbs=g_wbs, wbm=g_wbm, wout=g_wout, wup_t=g_wup_t, wdn=g_wdn))
    d_sq, d_skv, d_rb, d_sink = _swa_bwd(zmain, o_b, swa_probs, d_ob, bucket, sent_others)
    d_qfv, d_lb = _hgrn_bwd(zmain, lb_logits, states, d_oraw, T=min(1024, S))
    sent_small = send_small_grads(_pack_small_grads(d_lb, d_gain, d_sink, d_rb, d_ln1_g, d_ln1_b, d_ln2_g, d_ln2_b, loss))

    head_major = lambda a: a.reshape(3, HG_HEADS, HG_DK, D_MODEL).transpose(1, 0, 2, 3).reshape(3 * D_MODEL, D_MODEL)
    col_major = lambda a: a.reshape(HG_HEADS, 3, HG_DK, D_MODEL).transpose(1, 0, 2, 3).reshape(3 * D_MODEL, D_MODEL)
    pieces = (d_qfv, d_hg_gl, d_sq, d_skv, d_mq)
    g_qfv, g_hg_gl, g_sq, g_skv, g_mq = [
        wgrad(p, xb, kc=KC, name="grad_w_in_" + n) for p, n in zip(pieces, ("qfv", "hg_gates", "swa_q", "swa_kv", "mem_q"))]
    g_win_t = jnp.concatenate([col_major(g_qfv), g_hg_gl[:D_MODEL], g_sq, g_skv, g_mq, g_hg_gl[D_MODEL:]], axis=0)
    sent_win = send_win_grad(g_win_t, sent_small)
    return _grad_x(*pieces, head_major(win_t[:C_HG]), win_t, dx_part, sent_win, tm=T)


MESH = pl.DeviceIdType.MESH
ANY = pl.BlockSpec(memory_space=pl.ANY)


def _coords():
    return lax.axis_index("x"), lax.axis_index("y"), lax.axis_index("c")


def _other_chips(x, y):
    return [(1 - x, y), (x, 1 - y), (1 - x, 1 - y)]


def _all_gather_weights(*arrays):
    na = len(arrays)

    def body(*refs):
        srcs, dsts = refs[:na], refs[na:2 * na]
        send_sems, recv_sems, local_sems = refs[2 * na:]
        x, y, c = _coords()
        me, sibling = (x, y, c), (x, y, 1 - c)
        chips = _other_chips(x, y)

        def slot(a, px, py, pc):
            return dsts[a].at[4 * px + 2 * py + pc]

        def copy(a, k, block, to, from_shard=False):
            return pltpu.make_async_remote_copy(
                src_ref=srcs[a] if from_shard else slot(a, *block), dst_ref=slot(a, *block),
                send_sem=send_sems.at[a * 7 + k], recv_sem=recv_sems.at[a * 7 + k],
                device_id=to, device_id_type=MESH)

        own = [pltpu.make_async_copy(srcs[a], slot(a, *me), local_sems.at[a]) for a in range(na)]
        for cp in own:
            cp.start()
        first = []
        for a in range(na):
            first.append(copy(a, 0, me, sibling, True))
            first += [copy(a, 1 + j, me, (*chip, c), True) for j, chip in enumerate(chips)]
        for cp in first:
            cp.start()
        passed = []
        for j, chip in enumerate(chips):
            for a in range(na):
                copy(a, 1 + j, (*chip, c), me).wait_recv()
                fwd = copy(a, 4 + j, (*chip, c), sibling)
                fwd.start()
                passed.append(fwd)
        for a in range(na):
            copy(a, 0, sibling, me).wait_recv()
            for j, chip in enumerate(chips):
                copy(a, 4 + j, (*chip, 1 - c), me).wait_recv()
        for cp in first + passed:
            cp.wait_send()
        for cp in own:
            cp.wait()

    return pl.pallas_call(
        body,
        in_specs=[ANY] * na,
        out_specs=[ANY] * na,
        out_shape=[jax.ShapeDtypeStruct((N_DEV,) + a.shape, a.dtype) for a in arrays],
        scratch_shapes=[pltpu.SemaphoreType.DMA((7 * na,)), pltpu.SemaphoreType.DMA((7 * na,)),
                        pltpu.SemaphoreType.DMA((na,))],
        name="all_gather_weights",
    )(*arrays)


HBM = pl.BlockSpec(memory_space=pltpu.HBM)
SEM = pl.BlockSpec(memory_space=pltpu.SEMAPHORE)
_DATAFLOW = pltpu.SideEffectType.DATAFLOW_SIDE_EFFECTING


def _peer(x, y, c, r):
    return x ^ (r >> 2), y ^ ((r >> 1) & 1), c ^ (r & 1)


def _direct_copies(src_ref, land_ref, send_sems, recv_sems, gather, receiving):
    x, y, c = _coords()
    me = 4 * x + 2 * y + c
    copies = []
    for r in range(1, N_DEV):
        px, py, pc = _peer(x, y, c, r)
        peer = 4 * px + 2 * py + pc
        if gather:
            src, dst = src_ref, land_ref.at[peer if receiving else me]
        else:
            src, dst = src_ref.at[peer], land_ref.at[r - 1]
        copies.append(pltpu.make_async_remote_copy(
            src_ref=src, dst_ref=dst, send_sem=send_sems.at[r - 1], recv_sem=recv_sems.at[r - 1],
            device_id=(px, py, pc), device_id_type=MESH))
    return copies


def _direct_start(src, land, *, gather, name, after=None):
    def body(src_ref, land_ref, *rest):
        send_sems, recv_sems, token = rest[-5], rest[-4], rest[-1]
        for cp in _direct_copies(src_ref, land_ref, send_sems, recv_sems, gather, False):
            cp.start()
        token[...] = jnp.zeros_like(token)

    afters = () if after is None else (after,)
    return pl.pallas_call(
        body,
        name=name,
        out_shape=(pltpu.SemaphoreType.DMA((N_DEV - 1,)), pltpu.SemaphoreType.DMA((N_DEV - 1,)),
                   pltpu.HBM(src.shape, src.dtype), pltpu.HBM(land.shape, land.dtype),
                   jax.ShapeDtypeStruct((8, 128), F32)),
        in_specs=(HBM, HBM) + tuple(ANY for _ in afters),
        out_specs=(SEM, SEM, HBM, HBM, pl.BlockSpec(memory_space=pltpu.VMEM)),
        input_output_aliases={0: 2, 1: 3},
        compiler_params=pltpu.CompilerParams(has_side_effects=_DATAFLOW),
    )(pltpu.with_memory_space_constraint(src, pltpu.HBM), pltpu.with_memory_space_constraint(land, pltpu.HBM), *afters)


def _direct_wait(send_sems, recv_sems, src_thru, land_thru, after, *, gather, name):
    afters = after if isinstance(after, tuple) else (after,)

    def body(src_ref, land_ref, send_sems_ref, recv_sems_ref, *rest):
        del rest
        for cp in _direct_copies(src_ref, land_ref, send_sems_ref, recv_sems_ref, gather, True):
            cp.wait_send()
            cp.wait_recv()

    return pl.pallas_call(
        body,
        name=name,
        out_shape=(pltpu.HBM(src_thru.shape, src_thru.dtype), pltpu.HBM(land_thru.shape, land_thru.dtype)),
        in_specs=(HBM, HBM, SEM, SEM) + tuple(ANY for _ in afters),
        out_specs=(HBM, HBM),
        input_output_aliases={0: 0, 1: 1},
        compiler_params=pltpu.CompilerParams(has_side_effects=_DATAFLOW),
    )(src_thru, land_thru, send_sems, recv_sems, *afters)


def _sum_partials(src, land, me, *, tr, name):
    R = src.shape[1]

    def body(me_ref, s_ref, l_ref, o_ref):
        del me_ref
        acc = s_ref[0].astype(F32)
        for r in range(N_DEV - 1):
            acc = acc + l_ref[r].astype(F32)
        o_ref[...] = acc

    return pl.pallas_call(
        body,
        grid_spec=pltpu.PrefetchScalarGridSpec(
            num_scalar_prefetch=1, grid=(R // tr,),
            in_specs=[pl.BlockSpec((1, tr, 1024), lambda i, mr: (mr[0], i, 0)),
                      pl.BlockSpec((N_DEV - 1, tr, 1024), lambda i, mr: (0, i, 0))],
            out_specs=pl.BlockSpec((tr, 1024), lambda i, mr: (i, 0))),
        out_shape=jax.ShapeDtypeStruct((R, 1024), F32),
        name=name,
    )(me, src, land)


_SMALL = ("lb_logits", "hg_norm_gain", "swa_sinks", "rel_bias", "ln1_g", "ln1_b", "ln2_g", "ln2_b")


def _pack_small_grads(d_lb, d_gain, d_sink, d_rb, d_ln1_g, d_ln1_b, d_ln2_g, d_ln2_b, loss):
    def body(lb_ref, gain_ref, sink_ref, rb_ref, l1g_ref, l1b_ref, l2g_ref, l2b_ref, loss_ref, o_ref):
        o_ref[...] = jnp.zeros_like(o_ref)
        for row, ref in ((SM_LB, lb_ref), (SM_GAIN, gain_ref), (SM_L1G, l1g_ref), (SM_L1B, l1b_ref),
                         (SM_L2G, l2g_ref), (SM_L2B, l2b_ref)):
            o_ref[row:row + 1, :] = ref[...]
        o_ref[SM_SINK:SM_SINK + 1, 0:128] = sink_ref[0:1, :]
        o_ref[SM_LOSS:SM_LOSS + 1, 0:128] = loss_ref[0:1, :]
        o_ref[SM_RB:SM_RB + NUM_BUCKETS, 0:128] = rb_ref[...]

    vm = pl.BlockSpec(memory_space=pltpu.VMEM)
    return pl.pallas_call(
        body,
        in_specs=[vm] * 9,
        out_specs=vm,
        out_shape=jax.ShapeDtypeStruct((SM_ROWS, D_MODEL), F32),
        name="pack_small_grads",
    )(d_lb, d_gain, d_sink, d_rb, d_ln1_g, d_ln1_b, d_ln2_g, d_ln2_b, loss)


def _small_finish(gathered, w, m, v):
    n = len(_SMALL)

    def body(*refs):
        g_ref = refs[0]
        w_refs, m_refs, v_refs = refs[1:1 + n], refs[1 + n:1 + 2 * n], refs[1 + 2 * n:1 + 3 * n]
        outs = refs[1 + 3 * n:]
        loss_ref, tot = outs[0], outs[-1]
        g_out, d_out, m_out, v_out = (outs[1 + k * n:1 + (k + 1) * n] for k in range(4))
        acc = g_ref[0]
        for d in range(1, N_DEV):
            acc = acc + g_ref[d]
        tot[...] = acc
        loss_ref[...] = tot[SM_LOSS:SM_LOSS + 1, 0:1]
        lb = _lower_bound(w_refs[0])
        dl0 = tot[SM_LB:SM_LB + 1, :] * lb * (1.0 - lb)
        grads = (jnp.concatenate([dl0, -dl0], axis=0), tot[SM_GAIN:SM_GAIN + 1, :],
                 tot[SM_SINK:SM_SINK + 1, 0:SWA_HEADS], tot[SM_RB:SM_RB + NUM_BUCKETS, 0:SWA_HEADS],
                 tot[SM_L1G:SM_L1G + 1, :], tot[SM_L1B:SM_L1B + 1, :], tot[SM_L2G:SM_L2G + 1, :], tot[SM_L2B:SM_L2B + 1, :])
        for k, g in enumerate(grads):
            g_out[k][...] = g
            d_out[k][...], m_out[k][...], v_out[k][...] = _adam_step(w_refs[k][...], g, m_refs[k][...], v_refs[k][...])

    vm = pl.BlockSpec(memory_space=pltpu.VMEM)
    shapes = [jax.ShapeDtypeStruct(w[k].shape, F32) for k in _SMALL]
    res = pl.pallas_call(
        body,
        in_specs=[vm] * (1 + 3 * n),
        out_specs=[vm] * (1 + 4 * n),
        out_shape=[jax.ShapeDtypeStruct((1, 1), F32)] + shapes * 4,
        scratch_shapes=[pltpu.VMEM((SM_ROWS, D_MODEL), F32)],
        name="small_finish",
    )(gathered, *[w[k] for k in _SMALL], *[m[k] for k in _SMALL], *[v[k] for k in _SMALL])
    parts = [dict(zip(_SMALL, res[1 + k * n:1 + (k + 1) * n])) for k in range(4)]
    return (res[0], *parts)


def _adam_step(w, g, m, v):
    nm = ADAM_B1 * m + (1.0 - ADAM_B1) * g
    nv = ADAM_B2 * v + (1.0 - ADAM_B2) * jnp.square(g)
    m_hat = nm / (1.0 - ADAM_B1 ** ADAM_STEP)
    v_hat = nv / (1.0 - ADAM_B2 ** ADAM_STEP)
    return -ADAM_LR * (m_hat / (jnp.sqrt(v_hat) + ADAM_EPS) + ADAM_WD * w), nm, nv


def _adamw(w, g, m, v, *, tr, name):
    R, C = w.shape

    def body(w_ref, g_ref, m_ref, v_ref, d_ref, nm_ref, nv_ref):
        d_ref[...], nm_ref[...], nv_ref[...] = _adam_step(w_ref[...], g_ref[...], m_ref[...], v_ref[...])

    spec = pl.BlockSpec((tr, C), lambda i: (i, 0))
    return pl.pallas_call(
        body,
        grid=(R // tr,),
        in_specs=[spec] * 4,
        out_specs=[spec] * 3,
        out_shape=[jax.ShapeDtypeStruct((R, C), F32)] * 3,
        compiler_params=_cparams("parallel"),
        name=name,
    )(w, g, m, v)


_WEIGHTS = ("w_in", "lb_logits", "hg_norm_gain", "swa_sinks", "rel_bias", "w_mem_kv", "w_branch_hg", "w_branch_swa",
            "w_branch_mem", "w_out", "ln1_g", "ln1_b", "w_up", "w_down", "ln2_g", "ln2_b")


def kernel(x, mem, w_in, lb_logits, hg_norm_gain, swa_sinks, rel_bias, w_mem_kv, w_branch_hg, w_branch_swa, w_branch_mem, w_out, ln1_g, ln1_b, w_up, w_down, ln2_g, ln2_b, loss_target, m_w_in, m_lb_logits, m_hg_norm_gain, m_swa_sinks, m_rel_bias, m_w_mem_kv, m_w_branch_hg, m_w_branch_swa, m_w_branch_mem, m_w_out, m_ln1_g, m_ln1_b, m_w_up, m_w_down, m_ln2_g, m_ln2_b, v_w_in, v_lb_logits, v_hg_norm_gain, v_swa_sinks, v_rel_bias, v_w_mem_kv, v_w_branch_hg, v_w_branch_swa, v_w_branch_mem, v_w_out, v_ln1_g, v_ln1_b, v_w_up, v_w_down, v_ln2_g, v_ln2_b):
    w = dict(w_in=w_in, lb_logits=lb_logits, hg_norm_gain=hg_norm_gain, swa_sinks=swa_sinks, rel_bias=rel_bias,
             w_mem_kv=w_mem_kv, w_branch_hg=w_branch_hg, w_branch_swa=w_branch_swa, w_branch_mem=w_branch_mem,
             w_out=w_out, ln1_g=ln1_g, ln1_b=ln1_b, w_up=w_up, w_down=w_down, ln2_g=ln2_g, ln2_b=ln2_b)
    mom = dict(w_in=m_w_in, lb_logits=m_lb_logits, hg_norm_gain=m_hg_norm_gain, swa_sinks=m_swa_sinks, rel_bias=m_rel_bias,
               w_mem_kv=m_w_mem_kv, w_branch_hg=m_w_branch_hg, w_branch_swa=m_w_branch_swa, w_branch_mem=m_w_branch_mem,
               w_out=m_w_out, ln1_g=m_ln1_g, ln1_b=m_ln1_b, w_up=m_w_up, w_down=m_w_down, ln2_g=m_ln2_g, ln2_b=m_ln2_b)
    var = dict(w_in=v_w_in, lb_logits=v_lb_logits, hg_norm_gain=v_hg_norm_gain, swa_sinks=v_swa_sinks, rel_bias=v_rel_bias,
               w_mem_kv=v_w_mem_kv, w_branch_hg=v_w_branch_hg, w_branch_swa=v_w_branch_swa, w_branch_mem=v_w_branch_mem,
               w_out=v_w_out, ln1_g=v_ln1_g, ln1_b=v_ln1_b, w_up=v_w_up, w_down=v_w_down, ln2_g=v_ln2_g, ln2_b=v_ln2_b)
    xc, yc, cc = _coords()

    p1 = _bf(w_in[0].T)
    p2 = _bf(jnp.concatenate([w_down[0], w_up[0].T, w_branch_hg[0], w_branch_swa[0], w_branch_mem[0], w_out[0],
                              w_mem_kv[0].T], axis=0))
    me = 4 * xc + 2 * yc + cc
    (g1,) = _all_gather_weights(p1)
    land2 = lax.dynamic_update_slice(lax.empty((N_DEV, R_OTHER, D_MODEL), BF16), p2[None], (me, 0, 0))
    ag2 = _direct_start(p2, land2, gather=True, name="gather_other_weights_start")

    def other_weights(after):
        return _direct_wait(*ag2[:4], after, gather=True, name="gather_other_weights_wait")[1]

    blocks = lambda a: a.reshape(N_DEV, a.shape[0] // N_DEV, D_MODEL)
    started = {}

    def send_other_grads(g):
        part = jnp.concatenate([blocks(g[k]) for k in ("wdn", "wup_t", "wbh", "wbs", "wbm", "wout", "wkv_t")], axis=1)
        started["others"] = _direct_start(part, lax.empty((N_DEV - 1, R_OTHER, D_MODEL), BF16), gather=False,
                                          name="scatter_other_grads_start")
        return started["others"][4]

    me1 = me.reshape(1).astype(jnp.int32)
    grads, delta, new_m, new_v = {}, {}, {}, {}

    def adamw(name):
        w2 = w[name][0]
        delta[name], new_m[name], new_v[name] = _adamw(
            w2, grads[name], mom[name][0], var[name][0], tr=w2.shape[0] // 4, name="adamw_" + name)

    def send_small_grads(packed):
        land = lax.dynamic_update_slice(lax.empty((N_DEV, SM_ROWS, D_MODEL), F32), packed[None], (me, 0, 0))
        started["small"] = _direct_start(packed, land, gather=True, name="gather_small_grads_start")
        return started["small"][4]

    def send_win_grad(g, after):
        started["win"] = _direct_start(blocks(g), lax.empty((N_DEV - 1, IN_SHARD, D_MODEL), BF16), gather=False,
                                       name="scatter_w_in_grad_start", after=after)
        mine2, landed2 = _direct_wait(*started["others"][:4], started["win"][4], gather=False,
                                      name="scatter_other_grads_wait")
        gs2 = _sum_partials(mine2, landed2, me1, tr=R_OTHER // 2, name="sum_other_grads")
        grads.update(
            w_down=gs2[R_DN:R_UP], w_up=gs2[R_UP:R_BH].T, w_branch_hg=gs2[R_BH:R_BS], w_branch_swa=gs2[R_BS:R_BM],
            w_branch_mem=gs2[R_BM:R_OUT], w_out=gs2[R_OUT:R_KV], w_mem_kv=gs2[R_KV:R_OTHER].T)
        for name in ("w_mem_kv", "w_branch_hg", "w_branch_swa", "w_branch_mem", "w_out", "w_up", "w_down"):
            adamw(name)
        return tuple(new_v[name] for name in new_v)

    grad_x = _local_step(
        x[0], mem[0], loss_target[0], lb_logits, hg_norm_gain, swa_sinks, rel_bias, ln1_g, ln1_b, ln2_g, ln2_b,
        g1.reshape(IN_COLS, D_MODEL), ag2[4], other_weights, send_other_grads, send_small_grads, send_win_grad)

    mine1, landed1 = _direct_wait(*started["win"][:4], grad_x, gather=False, name="scatter_w_in_grad_wait")
    g_win_t = _sum_partials(mine1, landed1, me1, tr=IN_SHARD // 2, name="sum_w_in_grad")
    d_t, m_t, v_t = _adamw(w_in[0].T, g_win_t, m_w_in[0].T, v_w_in[0].T, tr=IN_SHARD // 4, name="adamw_w_in")
    grads["w_in"], delta["w_in"], new_m["w_in"], new_v["w_in"] = g_win_t.T, d_t.T, m_t.T, v_t.T

    _, gathered = _direct_wait(*started["small"][:4], grad_x, gather=True, name="gather_small_grads_wait")
    loss, g_s, d_s, m_s, v_s = _small_finish(gathered, w, mom, var)
    for dst, src in ((grads, g_s), (delta, d_s), (new_m, m_s), (new_v, v_s)):
        dst.update(src)

    def shaped(d, name):
        return d[name].reshape(w[name].shape)

    return (loss.reshape(()), grad_x[None], *[shaped(grads, n) for n in _WEIGHTS], *[shaped(delta, n) for n in _WEIGHTS],
            *[shaped(new_m, n) for n in _WEIGHTS], *[shaped(new_v, n) for n in _WEIGHTS])
```

```python
import functools
import math

import jax
import jax.numpy as jnp
from jax import lax
from jax.experimental import pallas as pl
from jax.experimental.pallas import tpu as pltpu

F32 = jnp.float32
BF16 = jnp.bfloat16

D_MODEL = 1024
MEM_LEN = 256
HG_HEADS = 8
HG_DK = 128
HG_CHUNK = 64
SWA_HEADS = 16
SWA_HEAD_DIM = 64
SWA_BLOCK = 128
SWA_WINDOW = 128
MEM_HEADS = 4
MEM_HEAD_DIM = 256
NUM_BUCKETS = 32
MAX_DISTANCE = 128
D_FF = 4096
LN_EPS = 1e-5
RMS_EPS = 1e-6
ALPHA = 2.0 ** 0.25
N_DEV = 8

C_HQ, C_HF, C_HI, C_HG, C_SQ, C_SK, C_SV, C_MQ, C_GL = 0, 1024, 2048, 3072, 4096, 5120, 5248, 5376, 6400
IN_COLS = 9472
IN_SHARD = IN_COLS // N_DEV

ADAM_LR = 0.001
ADAM_B1 = 0.9
ADAM_B2 = 0.999
ADAM_EPS = 1e-08
ADAM_WD = 0.01
ADAM_STEP = 10

VMEM_LIMIT = 58 * 1024 * 1024

R_DN, R_UP, R_BH, R_BS, R_BM, R_OUT, R_KV, R_OTHER = 0, 512, 1024, 1152, 1280, 1408, 1536, 1792

SM_LB, SM_GAIN, SM_SINK, SM_L1G, SM_L1B, SM_L2G, SM_L2B, SM_LOSS, SM_RB, SM_ROWS = 0, 2, 3, 4, 5, 6, 7, 8, 16, 48


def _bf(v):
    return v.astype(BF16)


def _f32(v):
    return v.astype(F32)


def _dot(a, b):
    return jnp.dot(a, b, preferred_element_type=F32)


def _dot_nt(a, b):
    return lax.dot_general(a, b, (((1,), (1,)), ((), ())), preferred_element_type=F32)


def _dot_tn(a, b):
    return lax.dot_general(a, b, (((0,), (0,)), ((), ())), preferred_element_type=F32)


def _sig(v):
    return 0.5 * jnp.tanh(0.5 * v) + 0.5


def _cparams(*sem):
    return pltpu.CompilerParams(dimension_semantics=sem, vmem_limit_bytes=VMEM_LIMIT)


def _const_spec(shape):
    nd = len(shape)
    return pl.BlockSpec(shape, lambda *_: (0,) * nd, pipeline_mode=pl.Buffered(1))


def _dep_spec():
    return pl.BlockSpec((8, 128), lambda *_: (0, 0))


def _in_proj(x, win_t, dep, *, tm):
    S = x.shape[0]

    def body(x_ref, w_ref, dep_ref, z_ref, gl_ref, xb_ref):
        del dep_ref
        xb = _bf(x_ref[...])
        xb_ref[...] = xb
        for c0 in range(0, C_GL, 1280):
            z_ref[:, c0:c0 + 1280] = _bf(_dot_nt(xb, w_ref[c0:c0 + 1280, :]))
        for c0 in range(0, IN_COLS - C_GL, 1024):
            gl_ref[:, c0:c0 + 1024] = _bf(_dot_nt(xb, w_ref[C_GL + c0:C_GL + c0 + 1024, :]))

    row = lambda w: pl.BlockSpec((tm, w), lambda i: (i, 0))
    return pl.pallas_call(
        body,
        grid=(S // tm,),
        in_specs=[row(D_MODEL), _const_spec(win_t.shape), _dep_spec()],
        out_specs=[row(C_GL), row(IN_COLS - C_GL), row(D_MODEL)],
        out_shape=[jax.ShapeDtypeStruct((S, C_GL), BF16), jax.ShapeDtypeStruct((S, IN_COLS - C_GL), BF16),
                   jax.ShapeDtypeStruct((S, D_MODEL), BF16)],
        compiler_params=_cparams("parallel"),
        name="in_proj",
    )(x, win_t, dep)


def _mm_tn_resident(a, b, *, tm, kc, name, out_dtype):
    K, M = a.shape
    N = b.shape[1]
    nk = K // kc

    def body(a_ref, b_ref, o_ref):
        acc = jnp.zeros((tm, N), F32)
        for kk in range(nk):
            sl = pl.ds(kk * kc, kc)
            acc = acc + _dot_tn(_bf(a_ref[sl, :]), _bf(b_ref[sl, :]))
        o_ref[...] = acc.astype(o_ref.dtype)

    return pl.pallas_call(
        body,
        grid=(M // tm,),
        in_specs=[pl.BlockSpec((K, tm), lambda i: (0, i)), _const_spec((K, N))],
        out_specs=pl.BlockSpec((tm, N), lambda i: (i, 0)),
        out_shape=jax.ShapeDtypeStruct((M, N), out_dtype),
        compiler_params=_cparams("parallel"),
        name=name,
    )(a, b)


def _mm_tn(a, b, *, kc, name, out_dtype=F32):
    K, M = a.shape
    N = b.shape[1]
    if M > 1024:
        return _mm_tn_resident(a, b, tm=256, kc=min(kc, 1024), name=name, out_dtype=out_dtype)
    tm = M
    nk = K // kc

    def body(a_ref, b_ref, o_ref, acc):
        k = pl.program_id(1)
        part = _dot_tn(_bf(a_ref[...]), _bf(b_ref[...]))

        @pl.when(k == 0)
        def _():
            acc[...] = part

        @pl.when(k > 0)
        def _():
            acc[...] += part

        @pl.when(k == nk - 1)
        def _():
            o_ref[...] = acc[...].astype(o_ref.dtype)

    return pl.pallas_call(
        body,
        grid=(M // tm, nk),
        in_specs=[pl.BlockSpec((kc, tm), lambda i, k: (k, i)), pl.BlockSpec((kc, N), lambda i, k: (k, 0))],
        out_specs=pl.BlockSpec((tm, N), lambda i, k: (i, 0)),
        out_shape=jax.ShapeDtypeStruct((M, N), out_dtype),
        scratch_shapes=[pltpu.VMEM((tm, N), F32)],
        compiler_params=_cparams("parallel", "arbitrary"),
        name=name,
    )(a, b)


def _grad_x(d_qfv, d_hg_gl, d_sq, d_skv, d_mq, w_qfv, win_t, add, deps, *, tm):
    M = add.shape[0]
    pieces = (d_qfv, d_hg_gl, d_sq, d_skv, d_mq)

    def body(qfv_ref, hggl_ref, sq_ref, skv_ref, mq_ref, wq_ref, w_ref, add_ref, *rest):
        o_ref = rest[-1]
        acc = add_ref[...] + _dot(qfv_ref[...], wq_ref[...])
        acc = acc + _dot(hggl_ref[:, 0:1024], w_ref[C_HG:C_SQ, :])
        acc = acc + _dot(hggl_ref[:, 1024:4096], w_ref[C_GL:IN_COLS, :])
        acc = acc + _dot(sq_ref[...], w_ref[C_SQ:C_SK, :])
        acc = acc + _dot(skv_ref[...], w_ref[C_SK:C_MQ, :])
        o_ref[...] = acc + _dot(mq_ref[...], w_ref[C_MQ:C_GL, :])

    return pl.pallas_call(
        body,
        grid=(M // tm,),
        in_specs=[pl.BlockSpec((tm, p.shape[1]), lambda i: (i, 0)) for p in pieces]
        + [_const_spec(w_qfv.shape), _const_spec(win_t.shape), pl.BlockSpec((tm, D_MODEL), lambda i: (i, 0))]
        + [_dep_spec() for _ in deps],
        out_specs=pl.BlockSpec((tm, D_MODEL), lambda i: (i, 0)),
        out_shape=jax.ShapeDtypeStruct((M, D_MODEL), F32),
        compiler_params=_cparams("parallel"),
        name="grad_x",
    )(*pieces, w_qfv, win_t, add, *deps)


def _lower_bound(lbl_ref):
    l0 = lbl_ref[0:1, :]
    l1 = lbl_ref[1:2, :]
    mx = jnp.maximum(l0, l1)
    e0 = jnp.exp(l0 - mx)
    e1 = jnp.exp(l1 - mx)
    return e0 / (e0 + e1)


def _tri(lower):
    r = lax.broadcasted_iota(jnp.int32, (HG_CHUNK, HG_CHUNK), 0)
    c = lax.broadcasted_iota(jnp.int32, (HG_CHUNK, HG_CHUNK), 1)
    return (r >= c) if lower else (r <= c)


def _hg_gates(fl, lb):
    sg = _sig(fl)
    f = lb + (1.0 - lb) * sg
    return sg, f, jnp.log(f), 1.0 - f


def _scan_rows(v, reverse=False):
    row = lax.broadcasted_iota(jnp.int32, v.shape, 0)
    s = 1
    while s < HG_CHUNK:
        if reverse:
            v = v + jnp.where(row < HG_CHUNK - s, pltpu.roll(v, HG_CHUNK - s, 0), 0.0)
        else:
            v = v + jnp.where(row >= s, pltpu.roll(v, s, 0), 0.0)
        s *= 2
    return v


def _hgrn_fwd(zmain, lb_logits, *, T):
    S = zmain.shape[0]
    nc = T // HG_CHUNK

    def body(q_ref, f_ref, v_ref, lbl_ref, o_ref, st_ref, state):
        @pl.when(pl.program_id(1) == 0)
        def _():
            state[...] = jnp.zeros_like(state)

        lb = _lower_bound(lbl_ref)
        tril = _tri(True)
        qis, updates, decays, intra = [], [], [], []
        for c in range(nc):
            sl = pl.ds(c * HG_CHUNK, HG_CHUNK)
            _, _, g, k = _hg_gates(_f32(f_ref[sl, :]), lb)
            b = _scan_rows(g)
            bl = jnp.sum(g, axis=0, keepdims=True)
            qi = _bf(_f32(q_ref[sl, :]) * jnp.exp(b))
            ki = _bf(k * jnp.exp(-b))
            ko = _bf(k * jnp.exp(bl - b))
            vb = _bf(v_ref[sl, :])
            att = jnp.where(tril, _dot_nt(qi, ki), 0.0)
            intra.append(_dot(_bf(att), vb))
            qis.append(qi)
            updates.append(_dot_tn(vb, ko))
            decays.append(jnp.exp(bl))
        st = state[...]
        for c in range(nc):
            st_ref[0, c] = st
            o_ref[pl.ds(c * HG_CHUNK, HG_CHUNK), :] = intra[c] + _dot_nt(qis[c], _bf(st))
            st = st * decays[c] + updates[c]
        state[...] = st

    col = lambda base: pl.BlockSpec((T, HG_DK), lambda h, t: (t, base + h))
    return pl.pallas_call(
        body,
        grid=(HG_HEADS, S // T),
        in_specs=[col(0), col(8), col(16), pl.BlockSpec((2, HG_DK), lambda h, t: (0, h))],
        out_specs=[
            pl.BlockSpec((T, HG_DK), lambda h, t: (t, h)),
            pl.BlockSpec((1, nc, HG_DK, HG_DK), lambda h, t: (h, t, 0, 0)),
        ],
        out_shape=[
            jax.ShapeDtypeStruct((S, D_MODEL), F32),
            jax.ShapeDtypeStruct((HG_HEADS, S // HG_CHUNK, HG_DK, HG_DK), F32),
        ],
        scratch_shapes=[pltpu.VMEM((HG_DK, HG_DK), F32)],
        compiler_params=_cparams("parallel", "arbitrary"),
        name="hgrn_fwd",
    )(zmain, zmain, zmain, lb_logits)


def _hgrn_bwd(zmain, lb_logits, states, d_o, *, T):
    S = zmain.shape[0]
    nc = T // HG_CHUNK
    nt = S // T

    def body(q_ref, f_ref, v_ref, lbl_ref, st_ref, do_ref, dz_ref, dlb_ref, dstate):
        @pl.when(pl.program_id(1) == 0)
        def _():
            dstate[...] = jnp.zeros_like(dstate)
            dlb_ref[...] = jnp.zeros_like(dlb_ref)

        lb = _lower_bound(lbl_ref)
        tril = _tri(True)
        last_row = lax.broadcasted_iota(jnp.int32, (HG_CHUNK, HG_DK), 0) == HG_CHUNK - 1
        saved = []
        for c in range(nc):
            sl = pl.ds(c * HG_CHUNK, HG_CHUNK)
            sg, f, g, k = _hg_gates(_f32(f_ref[sl, :]), lb)
            b = _scan_rows(g)
            bl = jnp.sum(g, axis=0, keepdims=True)
            eb = jnp.exp(b)
            enb = jnp.exp(-b)
            eo = jnp.exp(bl - b)
            q_in = _f32(q_ref[sl, :]) * eb
            k_in = k * enb
            k_out = k * eo
            qi, ki, ko = _bf(q_in), _bf(k_in), _bf(k_out)
            vb = _bf(v_ref[sl, :])
            dob = do_ref[sl, :]
            att = jnp.where(tril, _dot_nt(qi, ki), 0.0)
            d_att = _bf(jnp.where(tril, _dot_nt(dob, vb), 0.0))
            d_kin = _dot_tn(d_att, qi)
            saved.append(dict(
                sg=sg, f=f, eb=eb, enb=enb, eo=eo, ebl=jnp.exp(bl), k_out=k_out, ko=ko, vb=vb, dob=dob,
                d_v=_dot_tn(_bf(att), dob), d_qin=_dot(d_att, ki), d_kin=d_kin,
                qk=(q_in, k_in), d_state=_dot_tn(dob, qi)))
        dst = dstate[...]
        dsts = [None] * nc
        for c in reversed(range(nc)):
            dsts[c] = dst
            dst = dst * saved[c]["ebl"] + saved[c]["d_state"]
        dstate[...] = dst
        dlb = jnp.zeros((1, HG_DK), F32)
        for c in range(nc):
            sl = pl.ds(c * HG_CHUNK, HG_CHUNK)
            s = saved[c]
            q_in, k_in = s["qk"]
            st = st_ref[0, c]
            dstb = _bf(dsts[c])
            d_v = s["d_v"] + _dot_nt(s["ko"], dstb)
            d_qin = s["d_qin"] + _dot(s["dob"], _bf(st))
            d_kout = _dot(s["vb"], dstb)
            d_decay = jnp.sum(dsts[c] * st, axis=0, keepdims=True)
            kk = d_kout * s["k_out"]
            d_b = d_qin * q_in - s["d_kin"] * k_in - kk
            d_bl = jnp.sum(kk, axis=0, keepdims=True) + d_decay * s["ebl"]
            d_g = _scan_rows(d_b + jnp.where(last_row, d_bl, 0.0), reverse=True)
            d_f = d_g / s["f"] - (s["d_kin"] * s["enb"] + d_kout * s["eo"])
            dz_ref[sl, 0:HG_DK] = _bf(d_qin * s["eb"])
            dz_ref[sl, HG_DK:2 * HG_DK] = _bf(d_f * (1.0 - lb) * s["sg"] * (1.0 - s["sg"]))
            dz_ref[sl, 2 * HG_DK:3 * HG_DK] = _bf(d_v)
            dlb = dlb + jnp.sum(d_f * (1.0 - s["sg"]), axis=0, keepdims=True)
        dlb_ref[...] += dlb

    rev = lambda base: pl.BlockSpec((T, HG_DK), lambda h, t: (nt - 1 - t, base + h))
    outc = pl.BlockSpec((T, HG_DK), lambda h, t: (nt - 1 - t, h))
    return pl.pallas_call(
        body,
        grid=(HG_HEADS, nt),
        in_specs=[
            rev(0), rev(8), rev(16),
            pl.BlockSpec((2, HG_DK), lambda h, t: (0, h)),
            pl.BlockSpec((1, nc, HG_DK, HG_DK), lambda h, t: (h, nt - 1 - t, 0, 0)),
            outc,
        ],
        out_specs=[pl.BlockSpec((T, 3 * HG_DK), lambda h, t: (nt - 1 - t, h)),
                   pl.BlockSpec((1, HG_DK), lambda h, t: (0, h))],
        out_shape=[jax.ShapeDtypeStruct((S, 3 * D_MODEL), BF16), jax.ShapeDtypeStruct((1, D_MODEL), F32)],
        scratch_shapes=[pltpu.VMEM((HG_DK, HG_DK), F32)],
        compiler_params=_cparams("parallel", "arbitrary"),
        name="hgrn_bwd",
    )(zmain, zmain, zmain, lb_logits, states, d_o)


def _t5_bucket_table():
    qi = jnp.arange(SWA_BLOCK)[:, None] + SWA_BLOCK
    kj = jnp.arange(2 * SWA_BLOCK)[None, :]
    n = jnp.clip(qi - kj, 0, SWA_WINDOW - 1)
    max_exact = NUM_BUCKETS // 2
    nf = jnp.maximum(n, 1).astype(F32)
    large = max_exact + (jnp.log(nf / max_exact) / math.log(MAX_DISTANCE / max_exact)
                         * (NUM_BUCKETS - max_exact)).astype(jnp.int32)
    large = jnp.minimum(large, NUM_BUCKETS - 1)
    return jnp.where(n < max_exact, n, large).astype(jnp.int32)


SWA_ROWS = 32


def _swa_bias_init(bias, bucket_ref, rb_ref):
    bk = bucket_ref[...]
    qi = lax.broadcasted_iota(jnp.int32, bk.shape, 0) + SWA_BLOCK
    kj = lax.broadcasted_iota(jnp.int32, bk.shape, 1)
    band = (qi - kj >= 0) & (qi - kj < SWA_WINDOW)
    for h in range(SWA_HEADS):
        def sel(b, acc, h=h):
            return jnp.where(bk == b, rb_ref[b, h], acc)
        t = lax.fori_loop(0, NUM_BUCKETS, sel, jnp.zeros(bk.shape, F32))
        bias[1, h] = jnp.where(band, t, -jnp.inf)
        bias[0, h] = jnp.where(band & (kj >= SWA_BLOCK), t, -jnp.inf)


def _lane_halves(t, kv_head):
    lane = lax.broadcasted_iota(jnp.int32, t.shape, 1)
    rolled = pltpu.roll(t, 64, 1)
    zero = jnp.zeros_like(t)
    if kv_head == 0:
        return jnp.where(lane < 64, t, zero), jnp.where(lane >= 64, rolled, zero)
    return jnp.where(lane < 64, rolled, zero), jnp.where(lane >= 64, t, zero)


def _swa_zero_key0(t):
    return jnp.where(lax.broadcasted_iota(jnp.int32, t.shape, 0) == 0, jnp.zeros_like(t), t)


def _swa_probs(s, masked_bias, sink):
    s = s + masked_bias
    m = jnp.maximum(jnp.max(s, axis=-1, keepdims=True), sink)
    p = jnp.exp(s - m)
    es = jnp.exp(sink - m)
    inv = 1.0 / (jnp.sum(p, axis=-1, keepdims=True) + es)
    return p * inv, es * inv


def _swa_fwd(zmain, bucket, rel_bias, sinks):
    S = zmain.shape[0]
    nb = S // SWA_BLOCK
    scale = SWA_HEAD_DIM ** -0.5

    def body(q_ref, kvc_ref, kvp_ref, bucket_ref, rb_ref, sk_ref, o_ref, p_ref, bias):
        n = pl.program_id(0)

        @pl.when(n == 0)
        def _():
            _swa_bias_init(bias, bucket_ref, rb_ref)

        later = jnp.minimum(n, 1)
        kk = _bf(jnp.concatenate([kvp_ref[:, 0:128], kvc_ref[:, 0:128]], axis=0))
        vv = _swa_zero_key0(_bf(jnp.concatenate([kvp_ref[:, 128:256], kvc_ref[:, 128:256]], axis=0)))
        first_col = lax.broadcasted_iota(jnp.int32, (SWA_ROWS, 2 * SWA_BLOCK), 1) == 0
        scores, values = {}, {}
        for kvh in range(2):
            qst = _bf(jnp.concatenate([q_ref[:, pl.ds((kvh * 4 + jj) * 128, 128)] for jj in range(4)], axis=0) * scale)
            values[kvh] = _lane_halves(vv, kvh)
            for odd, kx in enumerate(_lane_halves(kk, kvh)):
                scores[kvh, odd] = _dot_nt(qst, kx)
        probs = {}
        for (kvh, odd), s in scores.items():
            parts = []
            for jj in range(4):
                h = 2 * (kvh * 4 + jj) + odd
                for r0 in range(0, SWA_BLOCK, SWA_ROWS):
                    p, ps = _swa_probs(s[jj * SWA_BLOCK + r0:jj * SWA_BLOCK + r0 + SWA_ROWS],
                                       bias[later, h, pl.ds(r0, SWA_ROWS), :], sk_ref[0, h])
                    part = _bf(jnp.where(first_col, ps, p))
                    p_ref[pl.ds(r0, SWA_ROWS), pl.ds(h * 2 * SWA_BLOCK, 2 * SWA_BLOCK)] = part
                    parts.append(part)
            probs[kvh, odd] = jnp.concatenate(parts, axis=0)
        for kvh in range(2):
            ost = _dot(probs[kvh, 0], values[kvh][0]) + _dot(probs[kvh, 1], values[kvh][1])
            for jj in range(4):
                o_ref[:, pl.ds((kvh * 4 + jj) * 128, 128)] = ost[jj * SWA_BLOCK:(jj + 1) * SWA_BLOCK]

    smem = pl.BlockSpec(memory_space=pltpu.SMEM)
    return pl.pallas_call(
        body,
        grid=(nb,),
        in_specs=[
            pl.BlockSpec((SWA_BLOCK, 1024), lambda n: (n, C_SQ // 1024)),
            pl.BlockSpec((SWA_BLOCK, 256), lambda n: (n, C_SK // 256)),
            pl.BlockSpec((SWA_BLOCK, 256), lambda n: (jnp.maximum(n - 1, 0), C_SK // 256)),
            _const_spec((SWA_BLOCK, 2 * SWA_BLOCK)), smem, smem,
        ],
        out_specs=[pl.BlockSpec((SWA_BLOCK, 1024), lambda n: (n, 0)),
                   pl.BlockSpec((SWA_BLOCK, SWA_HEADS * 2 * SWA_BLOCK), lambda n: (n, 0))],
        out_shape=[jax.ShapeDtypeStruct((S, 1024), F32),
                   jax.ShapeDtypeStruct((S, SWA_HEADS * 2 * SWA_BLOCK), BF16)],
        scratch_shapes=[pltpu.VMEM((2, SWA_HEADS, SWA_BLOCK, 2 * SWA_BLOCK), F32)],
        compiler_params=_cparams("arbitrary"),
        name="swa_fwd",
    )(zmain, zmain, zmain, bucket, rel_bias, sinks)


def _swa_bwd(zmain, o_b, probs, d_o, bucket, dep):
    S = zmain.shape[0]
    nb = S // SWA_BLOCK
    scale = SWA_HEAD_DIM ** -0.5

    def body(q_ref, kvc_ref, kvp_ref, o_ref, p_ref, do_ref, bucket_ref, dep_ref,
             dq_ref, dkv_ref, drb_ref, dsk_ref, dbias, carry):
        del dep_ref
        n = pl.program_id(0)

        @pl.when(n == 0)
        def _():
            dbias[...] = jnp.zeros_like(dbias)
            carry[...] = jnp.zeros_like(carry)

        @pl.when(n < nb)
        def _():
            kk = _swa_zero_key0(_bf(jnp.concatenate([kvp_ref[:, 0:128], kvc_ref[:, 0:128]], axis=0)))
            vv = _swa_zero_key0(_bf(jnp.concatenate([kvp_ref[:, 128:256], kvc_ref[:, 128:256]], axis=0)))
            lane = lax.broadcasted_iota(jnp.int32, (2 * SWA_BLOCK, 128), 1)
            lane_q = lax.broadcasted_iota(jnp.int32, (4 * SWA_BLOCK, 128), 1)
            pair_cols = {kvh: [pl.ds((kvh * 4 + jj) * 128, 128) for jj in range(4)] for kvh in range(2)}
            qst, dost, ks, d_p, delta = {}, {}, {}, {}, {}
            for kvh in range(2):
                qst[kvh] = _bf(jnp.concatenate([q_ref[:, cl] for cl in pair_cols[kvh]], axis=0) * scale)
                dost[kvh] = jnp.concatenate([do_ref[:, cl] for cl in pair_cols[kvh]], axis=0)
                prod = dost[kvh].astype(F32) * jnp.concatenate([o_ref[:, cl] for cl in pair_cols[kvh]], axis=0)
                ks[kvh] = _lane_halves(kk, kvh)
                for odd, vx in enumerate(_lane_halves(vv, kvh)):
                    keep = (lane_q >= 64) if odd else (lane_q < 64)
                    delta[kvh, odd] = jnp.sum(jnp.where(keep, prod, 0.0), axis=-1, keepdims=True)
                    d_p[kvh, odd] = _dot_nt(dost[kvh], vx)
            pst, dsst = {}, {}
            for (kvh, odd), dp in d_p.items():
                p_parts, ds_parts = [], []
                for jj in range(4):
                    h = 2 * (kvh * 4 + jj) + odd
                    rows = slice(jj * SWA_BLOCK, (jj + 1) * SWA_BLOCK)
                    p = p_ref[:, pl.ds(h * 2 * SWA_BLOCK, 2 * SWA_BLOCK)]
                    ds = _f32(p) * (dp[rows] - delta[kvh, odd][rows])
                    dbias[h] += ds
                    p_parts.append(p)
                    ds_parts.append(_bf(ds))
                pst[kvh, odd] = jnp.concatenate(p_parts, axis=0)
                dsst[kvh, odd] = jnp.concatenate(ds_parts, axis=0)
            dk_parts, dv_parts = [], []
            for kvh in range(2):
                dq_st = _dot(dsst[kvh, 0], ks[kvh][0]) + _dot(dsst[kvh, 1], ks[kvh][1])
                for jj in range(4):
                    dq_ref[:, pair_cols[kvh][jj]] = _bf(dq_st[jj * SWA_BLOCK:(jj + 1) * SWA_BLOCK] * scale)
                zk = jnp.where(lane < 64, _dot_tn(dsst[kvh, 0], qst[kvh]), _dot_tn(dsst[kvh, 1], qst[kvh]))
                zv = jnp.where(lane < 64, _dot_tn(pst[kvh, 0], dost[kvh]), _dot_tn(pst[kvh, 1], dost[kvh]))
                dk_parts.append(zk + pltpu.roll(zk, 64, 1))
                dv_parts.append(zv + pltpu.roll(zv, 64, 1))
            dk = jnp.where(lane < 64, dk_parts[0], dk_parts[1])
            dv = jnp.where(lane < 64, dv_parts[0], dv_parts[1])
            dkv = _swa_zero_key0(jnp.concatenate([dk, dv], axis=1))
            dkv_ref[...] = _bf(carry[...] + dkv[0:SWA_BLOCK])
            carry[...] = dkv[SWA_BLOCK:]

        @pl.when(n == nb)
        def _():
            dkv_ref[...] = _bf(carry[...])
            first_col = lax.broadcasted_iota(jnp.int32, (SWA_BLOCK, 2 * SWA_BLOCK), 1) == 0
            bk = jnp.where(first_col, -1, bucket_ref[...])

            row = lax.broadcasted_iota(jnp.int32, (NUM_BUCKETS, 128), 0)
            lane = lax.broadcasted_iota(jnp.int32, (NUM_BUCKETS, 128), 1)

            def total(v):
                return jnp.sum(jnp.sum(v, axis=1, keepdims=True), axis=0, keepdims=True)

            def per_head(h, acc):
                db = dbias[h]
                d_rb, d_sk = acc
                d_sk = d_sk + jnp.where((row == 0) & (lane == h), total(jnp.where(first_col, db, 0.0)), 0.0)

                def per_bucket(b, d_rb):
                    return d_rb + jnp.where((row == b) & (lane == h), total(jnp.where(bk == b, db, 0.0)), 0.0)

                return lax.fori_loop(0, NUM_BUCKETS, per_bucket, d_rb), d_sk

            zero = jnp.zeros((NUM_BUCKETS, 128), F32)
            d_rb, d_sk = lax.fori_loop(0, SWA_HEADS, per_head, (zero, zero))
            drb_ref[...] = d_rb
            dsk_ref[...] = d_sk[0:8]

    cur = lambda n: jnp.minimum(n, nb - 1)
    prev = lambda n: jnp.maximum(jnp.minimum(n, nb - 1) - 1, 0)
    return pl.pallas_call(
        body,
        grid=(nb + 1,),
        in_specs=[
            pl.BlockSpec((SWA_BLOCK, 1024), lambda n: (cur(n), C_SQ // 1024)),
            pl.BlockSpec((SWA_BLOCK, 256), lambda n: (cur(n), C_SK // 256)),
            pl.BlockSpec((SWA_BLOCK, 256), lambda n: (prev(n), C_SK // 256)),
            pl.BlockSpec((SWA_BLOCK, 1024), lambda n: (cur(n), 0)),
            pl.BlockSpec((SWA_BLOCK, SWA_HEADS * 2 * SWA_BLOCK), lambda n: (cur(n), 0)),
            pl.BlockSpec((SWA_BLOCK, 1024), lambda n: (cur(n), 0)),
            _const_spec((SWA_BLOCK, 2 * SWA_BLOCK)), _dep_spec(),
        ],
        out_specs=[
            pl.BlockSpec((SWA_BLOCK, 1024), lambda n: (cur(n), 0)),
            pl.BlockSpec((SWA_BLOCK, 256), lambda n: (jnp.maximum(n - 1, 0), 0)),
            pl.BlockSpec((NUM_BUCKETS, 128), lambda n: (0, 0)),
            pl.BlockSpec((8, 128), lambda n: (0, 0)),
        ],
        out_shape=[
            jax.ShapeDtypeStruct((S, 1024), BF16),
            jax.ShapeDtypeStruct((S, 256), BF16),
            jax.ShapeDtypeStruct((NUM_BUCKETS, 128), F32),
            jax.ShapeDtypeStruct((8, 128), F32),
        ],
        scratch_shapes=[
            pltpu.VMEM((SWA_HEADS, SWA_BLOCK, 2 * SWA_BLOCK), F32),
            pltpu.VMEM((SWA_BLOCK, 256), F32),
        ],
        compiler_params=_cparams("arbitrary"),
        name="swa_bwd",
    )(zmain, zmain, zmain, o_b, probs, d_o, bucket, dep)


def _mem_probs(q_ref, k):
    qs = _bf(q_ref[...] * (MEM_HEAD_DIM ** -0.5))
    s = _dot_nt(qs, k)
    e = jnp.exp(s - jnp.max(s, axis=-1, keepdims=True))
    return qs, e * (1.0 / jnp.sum(e, axis=-1, keepdims=True))


def _mem_q_specs(T):
    return [pl.BlockSpec((T, MEM_HEAD_DIM), lambda t, h=h: (t, C_MQ // MEM_HEAD_DIM + h)) for h in range(MEM_HEADS)]


def _mem_kv_proj(mem, g2):
    def body(mem_ref, w_ref, o_ref):
        o_ref[...] = _dot_nt(_bf(mem_ref[...]), _rows(w_ref))

    return pl.pallas_call(
        body,
        grid=(1,),
        in_specs=[pl.BlockSpec((MEM_LEN, D_MODEL), lambda i: (0, 0)), _gathered_spec(R_KV, R_OTHER)],
        out_specs=pl.BlockSpec((MEM_LEN, 2048), lambda i: (0, 0)),
        out_shape=jax.ShapeDtypeStruct((MEM_LEN, 2048), F32),
        compiler_params=_cparams("arbitrary"),
        name="mem_kv_proj",
    )(mem, g2)


def _mem_fwd(zmain, mkv, *, T):
    S = zmain.shape[0]

    def body(q0, q1, q2, q3, kv_ref, o_ref, p_ref):
        for h, q_ref in enumerate((q0, q1, q2, q3)):
            cols = pl.ds(h * MEM_HEAD_DIM, MEM_HEAD_DIM)
            _, p = _mem_probs(q_ref, _bf(kv_ref[:, cols]))
            pb = _bf(p)
            p_ref[:, cols] = pb
            o_ref[:, cols] = _dot(pb, _bf(kv_ref[:, pl.ds(1024 + h * MEM_HEAD_DIM, MEM_HEAD_DIM)]))

    row = pl.BlockSpec((T, 1024), lambda t: (t, 0))
    return pl.pallas_call(
        body,
        grid=(S // T,),
        in_specs=_mem_q_specs(T) + [_const_spec((MEM_LEN, 2048))],
        out_specs=[row, row],
        out_shape=[jax.ShapeDtypeStruct((S, 1024), F32), jax.ShapeDtypeStruct((S, 1024), BF16)],
        compiler_params=_cparams("parallel"),
        name="mem_fwd",
    )(zmain, zmain, zmain, zmain, mkv)


def _mem_bwd(zmain, mkv, o_c, probs, d_o, *, T):
    S = zmain.shape[0]
    scale = MEM_HEAD_DIM ** -0.5

    def body(q0, q1, q2, q3, kv_ref, o_ref, p_ref, do_ref, dq_ref, dkv_ref):
        @pl.when(pl.program_id(0) == 0)
        def _():
            dkv_ref[...] = jnp.zeros_like(dkv_ref)

        for h, q_ref in enumerate((q0, q1, q2, q3)):
            cols = pl.ds(h * MEM_HEAD_DIM, MEM_HEAD_DIM)
            vcols = pl.ds(1024 + h * MEM_HEAD_DIM, MEM_HEAD_DIM)
            kb = _bf(kv_ref[:, cols])
            qs = _bf(q_ref[...] * scale)
            pb = p_ref[:, cols]
            dob = do_ref[:, cols]
            delta = jnp.sum(dob.astype(F32) * o_ref[:, cols], axis=-1, keepdims=True)
            ds = _bf(_f32(pb) * (_dot_nt(dob, _bf(kv_ref[:, vcols])) - delta))
            dq_ref[:, cols] = _bf(_dot(ds, kb) * scale)
            dkv_ref[:, cols] += _dot_tn(ds, qs)
            dkv_ref[:, vcols] += _dot_tn(pb, dob)

    row = pl.BlockSpec((T, 1024), lambda t: (t, 0))
    return pl.pallas_call(
        body,
        grid=(S // T,),
        in_specs=_mem_q_specs(T) + [_const_spec((MEM_LEN, 2048)), row, row, row],
        out_specs=[row, pl.BlockSpec((MEM_LEN, 2048), lambda t: (0, 0))],
        out_shape=[jax.ShapeDtypeStruct((S, 1024), BF16), jax.ShapeDtypeStruct((MEM_LEN, 2048), F32)],
        compiler_params=_cparams("arbitrary"),
        name="mem_bwd",
    )(zmain, zmain, zmain, zmain, mkv, o_c, probs, d_o)


def _layer_norm(u):
    mu = jnp.mean(u, axis=-1, keepdims=True)
    xc = u - mu
    rstd = lax.rsqrt(jnp.mean(xc * xc, axis=-1, keepdims=True) + LN_EPS)
    return xc * rstd, rstd


def _layer_norm_bwd(dy, gamma, xhat, rstd):
    dxh = dy * gamma
    return rstd * (dxh - jnp.mean(dxh, axis=-1, keepdims=True) - xhat * jnp.mean(dxh * xhat, axis=-1, keepdims=True))


def _merge_forward(oraw_ref, hg_ref, ob_ref, oc_ref, gl_ref, x_ref, gain_ref, wbh, wbs, wbm, wout):
    ys, rs = [], []
    for h in range(HG_HEADS):
        oh = oraw_ref[:, pl.ds(h * HG_DK, HG_DK)]
        r = lax.rsqrt(jnp.mean(oh * oh, axis=-1, keepdims=True) + RMS_EPS)
        ys.append(oh * r)
        rs.append(r)
    y = jnp.concatenate(ys, axis=1)
    hg = _f32(hg_ref[...])
    sg = _sig(hg)
    silu = hg * sg
    oa = _bf(y * gain_ref[...] * silu)
    pa = _dot(oa, _rows(wbh))
    pb = _dot(_bf(ob_ref[...]), _rows(wbs))
    pc = _dot(_bf(oc_ref[...]), _rows(wbm))
    g0 = _sig(_f32(gl_ref[:, 0:1024]))
    g1 = _sig(_f32(gl_ref[:, 1024:2048]))
    g2 = _sig(_f32(gl_ref[:, 2048:3072]))
    m = _bf(g0 * pa + g1 * pb + g2 * pc)
    u1 = ALPHA * x_ref[...] + _dot(m, _rows(wout))
    xhat, rstd = _layer_norm(u1)
    return dict(y=y, rs=rs, hg=hg, sg=sg, silu=silu, oa=oa, pa=pa, pb=pb, pc=pc,
                g0=g0, g1=g1, g2=g2, m=m, xhat=xhat, rstd=rstd)


def _gathered_spec(lo, hi):
    n = hi - lo
    return pl.BlockSpec((N_DEV, n, D_MODEL), lambda *_: (0, lo // n, 0), pipeline_mode=pl.Buffered(1))


def _rows(w_ref):
    return w_ref[...].reshape(-1, D_MODEL)


def _merge_in_specs(T):
    row = lambda w, c=0: pl.BlockSpec((T, w), lambda i: (i, c))
    vec = pl.BlockSpec((1, D_MODEL), lambda i: (0, 0))
    w = [_gathered_spec(lo, hi) for lo, hi in ((R_BH, R_BS), (R_BS, R_BM), (R_BM, R_OUT), (R_OUT, R_KV))]
    return [row(1024), row(1024, C_HG // 1024), row(1024), row(1024), row(3072), row(1024), vec, *w, vec, vec]


def _merge_fwd(o_raw, zmain, o_b, o_c, gl, x, gain, wbh, wbs, wbm, wout, ln_g, ln_b, *, T):
    S = x.shape[0]

    def body(oraw_ref, hg_ref, ob_ref, oc_ref, gl_ref, x_ref, gain_ref, wbh_r, wbs_r, wbm_r, wout_r, g_ref, b_ref,
             h1_ref, h1b_ref):
        f = _merge_forward(oraw_ref, hg_ref, ob_ref, oc_ref, gl_ref, x_ref, gain_ref, wbh_r, wbs_r, wbm_r, wout_r)
        h1 = f["xhat"] * g_ref[...] + b_ref[...]
        h1_ref[...] = h1
        h1b_ref[...] = _bf(h1)

    row = pl.BlockSpec((T, D_MODEL), lambda i: (i, 0))
    return pl.pallas_call(
        body,
        grid=(S // T,),
        in_specs=_merge_in_specs(T),
        out_specs=[row, row],
        out_shape=[jax.ShapeDtypeStruct((S, D_MODEL), F32), jax.ShapeDtypeStruct((S, D_MODEL), BF16)],
        compiler_params=_cparams("parallel"),
        name="merge_fwd",
    )(o_raw, zmain, o_b, o_c, gl, x, gain, wbh, wbs, wbm, wout, ln_g, ln_b)


def _merge_bwd(d_h1, o_raw, zmain, o_b, o_c, gl, x, gain, wbh, wbs, wbm, wout, ln_g, ln_b, *, T):
    S = x.shape[0]

    def body(dh1_ref, oraw_ref, hg_ref, ob_ref, oc_ref, gl_ref, x_ref, gain_ref, wbh_r, wbs_r, wbm_r, wout_r, g_ref, b_ref,
             dx_ref, du1_ref, m_ref, oa_ref, dpa_ref, dpb_ref, dpc_ref, doraw_ref, dob_ref, doc_ref, dz_ref,
             dgain_ref, dg_ref, db_ref):
        del b_ref

        @pl.when(pl.program_id(0) == 0)
        def _():
            dgain_ref[...] = jnp.zeros_like(dgain_ref)
            dg_ref[...] = jnp.zeros_like(dg_ref)
            db_ref[...] = jnp.zeros_like(db_ref)

        f = _merge_forward(oraw_ref, hg_ref, ob_ref, oc_ref, gl_ref, x_ref, gain_ref, wbh_r, wbs_r, wbm_r, wout_r)
        dh1 = dh1_ref[...]
        dg_ref[...] += jnp.sum(dh1 * f["xhat"], axis=0, keepdims=True)
        db_ref[...] += jnp.sum(dh1, axis=0, keepdims=True)
        du1 = _layer_norm_bwd(dh1, g_ref[...], f["xhat"], f["rstd"])
        dx_ref[...] = ALPHA * du1
        du1b = _bf(du1)
        du1_ref[...] = du1b
        m_ref[...] = f["m"]
        oa_ref[...] = f["oa"]
        dm = _dot_nt(du1b, _rows(wout_r))
        for i, (g, p, dp_ref, dob_r, w_r) in enumerate((
                (f["g0"], f["pa"], dpa_ref, None, wbh_r),
                (f["g1"], f["pb"], dpb_ref, dob_ref, wbs_r),
                (f["g2"], f["pc"], dpc_ref, doc_ref, wbm_r))):
            dz_ref[:, pl.ds((i + 1) * 1024, 1024)] = _bf(dm * p * g * (1.0 - g))
            dp = _bf(dm * g)
            dp_ref[...] = dp
            d_branch = _dot_nt(dp, _rows(w_r))
            if dob_r is not None:
                dob_r[...] = _bf(d_branch)
            else:
                doa = d_branch
        gain = gain_ref[...]
        t = doa * f["y"]
        dgain_ref[...] += jnp.sum(t * f["silu"], axis=0, keepdims=True)
        sg = f["sg"]
        dz_ref[:, 0:1024] = _bf(t * gain * sg * (1.0 + f["hg"] * (1.0 - sg)))
        dy = doa * gain * f["silu"]
        for h in range(HG_HEADS):
            cols = slice(h * HG_DK, (h + 1) * HG_DK)
            yh = f["y"][:, cols]
            dyh = dy[:, cols]
            doraw_ref[:, pl.ds(h * HG_DK, HG_DK)] = _bf(
                f["rs"][h] * (dyh - yh * jnp.mean(dyh * yh, axis=-1, keepdims=True)))

    row = lambda w: pl.BlockSpec((T, w), lambda i: (i, 0))
    vec = pl.BlockSpec((1, D_MODEL), lambda i: (0, 0))
    bshape = jax.ShapeDtypeStruct((S, D_MODEL), BF16)
    vshape = jax.ShapeDtypeStruct((1, D_MODEL), F32)
    return pl.pallas_call(
        body,
        grid=(S // T,),
        in_specs=[row(1024)] + _merge_in_specs(T),
        out_specs=[row(1024)] * 10 + [row(4096), vec, vec, vec],
        out_shape=[jax.ShapeDtypeStruct((S, D_MODEL), F32)] + [bshape] * 9
        + [jax.ShapeDtypeStruct((S, 4096), BF16), vshape, vshape, vshape],
        compiler_params=_cparams("arbitrary"),
        name="merge_bwd",
    )(d_h1, o_raw, zmain, o_b, o_c, gl, x, gain, wbh, wbs, wbm, wout, ln_g, ln_b)


def _mlp_fwd_bwd(h1, target, wup_t, wdn, ln_g, ln_b, *, T, FC):
    S = h1.shape[0]
    nf = D_FF // FC
    assert FC == R_BH - R_UP == R_UP - R_DN

    def body(h1_ref, t_ref, wup_ref, wdn_ref, g_ref, b_ref, dh1_ref, a_ref, dup_ref, du2_ref, loss_ref, dg_ref, db_ref, up_scr):
        @pl.when(pl.program_id(0) == 0)
        def _():
            loss_ref[...] = jnp.zeros_like(loss_ref)
            dg_ref[...] = jnp.zeros_like(dg_ref)
            db_ref[...] = jnp.zeros_like(db_ref)

        h1v = h1_ref[...]
        h1b = _bf(h1v)
        ff = jnp.zeros((T, D_MODEL), F32)
        for j in range(nf):
            rows = pl.ds(j * FC, FC)
            up = jnp.maximum(_dot_nt(h1b, wup_ref[j]), 0.0)
            up_scr[:, rows] = _bf(up)
            a = _bf(up * up)
            a_ref[:, rows] = a
            ff = ff + _dot(a, wdn_ref[j])
        xhat, rstd = _layer_norm(ALPHA * h1v + ff)
        gamma = g_ref[...]
        err = xhat * gamma + b_ref[...] - t_ref[...]
        loss_ref[...] += jnp.sum(jnp.sum(err * err, axis=-1, keepdims=True), axis=0, keepdims=True) * (0.5 / D_MODEL)
        dy = err * (1.0 / D_MODEL)
        dg_ref[...] += jnp.sum(dy * xhat, axis=0, keepdims=True)
        db_ref[...] += jnp.sum(dy, axis=0, keepdims=True)
        du2 = _layer_norm_bwd(dy, gamma, xhat, rstd)
        du2b = _bf(du2)
        du2_ref[...] = du2b
        dh1 = ALPHA * du2
        for j in range(nf):
            rows = pl.ds(j * FC, FC)
            dup = _bf(_dot_nt(du2b, wdn_ref[j]) * (2.0 * up_scr[:, rows].astype(F32)))
            dup_ref[:, rows] = dup
            dh1 = dh1 + _dot(dup, wup_ref[j])
        dh1_ref[...] = dh1

    row = lambda w: pl.BlockSpec((T, w), lambda i: (i, 0))
    vec = pl.BlockSpec((1, D_MODEL), lambda i: (0, 0))
    vshape = jax.ShapeDtypeStruct((1, D_MODEL), F32)
    return pl.pallas_call(
        body,
        grid=(S // T,),
        in_specs=[row(1024), row(1024), _gathered_spec(R_UP, R_BH), _gathered_spec(R_DN, R_UP), vec, vec],
        out_specs=[row(1024), row(D_FF), row(D_FF), row(1024), pl.BlockSpec((8, 128), lambda i: (0, 0)), vec, vec],
        out_shape=[
            jax.ShapeDtypeStruct((S, D_MODEL), F32),
            jax.ShapeDtypeStruct((S, D_FF), BF16),
            jax.ShapeDtypeStruct((S, D_FF), BF16),
            jax.ShapeDtypeStruct((S, D_MODEL), BF16),
            jax.ShapeDtypeStruct((8, 128), F32), vshape, vshape,
        ],
        scratch_shapes=[pltpu.VMEM((T, D_FF), BF16)],
        compiler_params=_cparams("arbitrary"),
        name="mlp_fwd_bwd",
    )(h1, target, wup_t, wdn, ln_g, ln_b)


def _local_step(x, mem, target, lb_logits, gain, sinks, rel_bias, ln1_g, ln1_b, ln2_g, ln2_b,
                win_t, dep0, other_weights, send_other_grads, send_small_grads, send_win_grad):
    S = x.shape[0]
    T = min(256, S)
    KC = min(2048, S)
    zmain, gl, xb = _in_proj(x, win_t, dep0, tm=min(512, S))
    bucket = _t5_bucket_table()

    o_raw, states = _hgrn_fwd(zmain, lb_logits, T=min(2048, S))
    o_b, swa_probs = _swa_fwd(zmain, bucket, rel_bias, sinks)
    g2 = other_weights((o_b, o_raw))
    mkv = _mem_kv_proj(mem, g2)
    o_c, mem_probs = _mem_fwd(zmain, mkv, T=min(1024, S))
    merge_args = (o_raw, zmain, o_b, o_c, gl, x, gain, g2, g2, g2, g2, ln1_g, ln1_b)
    h1, h1b = _merge_fwd(*merge_args, T=min(512, S))

    d_h1, act, d_up, du2, loss, d_ln2_g, d_ln2_b = _mlp_fwd_bwd(h1, target, g2, g2, ln2_g, ln2_b, T=min(512, S), FC=512)
    wgrad = functools.partial(_mm_tn, out_dtype=BF16)
    g_wdn = wgrad(act, du2, kc=KC, name="grad_w_down")
    g_wup_t = wgrad(d_up, h1b, kc=KC, name="grad_w_up")

    (dx_part, du1, m, oa, dpa, dpb, dpc, d_oraw, d_ob, d_oc, d_hg_gl,
     d_gain, d_ln1_g, d_ln1_b) = _merge_bwd(d_h1, *merge_args, T=T)
    g_wout = wgrad(m, du1, kc=KC, name="grad_w_out")
    g_wbh = wgrad(oa, dpa, kc=KC, name="grad_w_branch_hg")
    g_wbs = wgrad(o_b, dpb, kc=KC, name="grad_w_branch_swa")
    g_wbm = wgrad(o_c, dpc, kc=KC, name="grad_w_branch_mem")

    d_mq, d_mkv = _mem_bwd(zmain, mkv, o_c, mem_probs, d_oc, T=min(1024, S))
    g_wkv_t = wgrad(d_mkv, mem, kc=MEM_LEN, name="grad_w_mem_kv")
    sent_others = send_other_grads(
        dict(wkv_t=g_wkv_t, wbh=g_wbh, wbs=g_wbs, wbm=g_wbm, wout=g_wout, wup_t=g_wup_t, wdn=g_wdn))
    d_sq, d_skv, d_rb, d_sink = _swa_bwd(zmain, o_b, swa_probs, d_ob, bucket, sent_others)
    d_qfv, d_lb = _hgrn_bwd(zmain, lb_logits, states, d_oraw, T=min(2048, S))
    sent_small = send_small_grads(_pack_small_grads(d_lb, d_gain, d_sink, d_rb, d_ln1_g, d_ln1_b, d_ln2_g, d_ln2_b, loss))

    head_major = lambda a: a.reshape(3, HG_HEADS, HG_DK, D_MODEL).transpose(1, 0, 2, 3).reshape(3 * D_MODEL, D_MODEL)
    col_major = lambda a: a.reshape(HG_HEADS, 3, HG_DK, D_MODEL).transpose(1, 0, 2, 3).reshape(3 * D_MODEL, D_MODEL)
    pieces = (d_qfv, d_hg_gl, d_sq, d_skv, d_mq)
    g_qfv, g_hg_gl, g_sq, g_skv, g_mq = [
        wgrad(p, xb, kc=KC, name="grad_w_in_" + n) for p, n in zip(pieces, ("qfv", "hg_gates", "swa_q", "swa_kv", "mem_q"))]
    g_win_t = jnp.concatenate([col_major(g_qfv), g_hg_gl[:D_MODEL], g_sq, g_skv, g_mq, g_hg_gl[D_MODEL:]], axis=0)
    sent_win = send_win_grad(g_win_t, sent_small)
    return _grad_x(*pieces, head_major(win_t[:C_HG]), win_t, dx_part, sent_win, tm=T)


MESH = pl.DeviceIdType.MESH
ANY = pl.BlockSpec(memory_space=pl.ANY)


def _coords():
    return lax.axis_index("x"), lax.axis_index("y"), lax.axis_index("c")


def _other_chips(x, y):
    return [(1 - x, y), (x, 1 - y), (1 - x, 1 - y)]


def _all_gather_weights(*arrays):
    na = len(arrays)

    def body(*refs):
        srcs, dsts = refs[:na], refs[na:2 * na]
        send_sems, recv_sems, local_sems = refs[2 * na:]
        x, y, c = _coords()
        me, sibling = (x, y, c), (x, y, 1 - c)
        chips = _other_chips(x, y)

        def slot(a, px, py, pc):
            return dsts[a].at[4 * px + 2 * py + pc]

        def copy(a, k, block, to, from_shard=False):
            return pltpu.make_async_remote_copy(
                src_ref=srcs[a] if from_shard else slot(a, *block), dst_ref=slot(a, *block),
                send_sem=send_sems.at[a * 7 + k], recv_sem=recv_sems.at[a * 7 + k],
                device_id=to, device_id_type=MESH)

        own = [pltpu.make_async_copy(srcs[a], slot(a, *me), local_sems.at[a]) for a in range(na)]
        for cp in own:
            cp.start()
        first = []
        for a in range(na):
            first.append(copy(a, 0, me, sibling, True))
            first += [copy(a, 1 + j, me, (*chip, c), True) for j, chip in enumerate(chips)]
        for cp in first:
            cp.start()
        passed = []
        for j, chip in enumerate(chips):
            for a in range(na):
                copy(a, 1 + j, (*chip, c), me).wait_recv()
                fwd = copy(a, 4 + j, (*chip, c), sibling)
                fwd.start()
                passed.append(fwd)
        for a in range(na):
            copy(a, 0, sibling, me).wait_recv()
            for j, chip in enumerate(chips):
                copy(a, 4 + j, (*chip, 1 - c), me).wait_recv()
        for cp in first + passed:
            cp.wait_send()
        for cp in own:
            cp.wait()

    return pl.pallas_call(
        body,
        in_specs=[ANY] * na,
        out_specs=[ANY] * na,
        out_shape=[jax.ShapeDtypeStruct((N_DEV,) + a.shape, a.dtype) for a in arrays],
        scratch_shapes=[pltpu.SemaphoreType.DMA((7 * na,)), pltpu.SemaphoreType.DMA((7 * na,)),
                        pltpu.SemaphoreType.DMA((na,))],
        name="all_gather_weights",
    )(*arrays)


HBM = pl.BlockSpec(memory_space=pltpu.HBM)
SEM = pl.BlockSpec(memory_space=pltpu.SEMAPHORE)
_DATAFLOW = pltpu.SideEffectType.DATAFLOW_SIDE_EFFECTING


def _peer(x, y, c, r):
    return x ^ (r >> 2), y ^ ((r >> 1) & 1), c ^ (r & 1)


def _direct_copies(src_ref, land_ref, send_sems, recv_sems, gather, receiving):
    x, y, c = _coords()
    me = 4 * x + 2 * y + c
    copies = []
    for r in range(1, N_DEV):
        px, py, pc = _peer(x, y, c, r)
        peer = 4 * px + 2 * py + pc
        if gather:
            src, dst = src_ref, land_ref.at[peer if receiving else me]
        else:
            src, dst = src_ref.at[peer], land_ref.at[r - 1]
        copies.append(pltpu.make_async_remote_copy(
            src_ref=src, dst_ref=dst, send_sem=send_sems.at[r - 1], recv_sem=recv_sems.at[r - 1],
            device_id=(px, py, pc), device_id_type=MESH))
    return copies


def _direct_start(src, land, *, gather, name, after=None):
    def body(src_ref, land_ref, *rest):
        send_sems, recv_sems, token = rest[-5], rest[-4], rest[-1]
        for cp in _direct_copies(src_ref, land_ref, send_sems, recv_sems, gather, False):
            cp.start()
        token[...] = jnp.zeros_like(token)

    afters = () if after is None else (after,)
    return pl.pallas_call(
        body,
        name=name,
        out_shape=(pltpu.SemaphoreType.DMA((N_DEV - 1,)), pltpu.SemaphoreType.DMA((N_DEV - 1,)),
                   pltpu.HBM(src.shape, src.dtype), pltpu.HBM(land.shape, land.dtype),
                   jax.ShapeDtypeStruct((8, 128), F32)),
        in_specs=(HBM, HBM) + tuple(ANY for _ in afters),
        out_specs=(SEM, SEM, HBM, HBM, pl.BlockSpec(memory_space=pltpu.VMEM)),
        input_output_aliases={0: 2, 1: 3},
        compiler_params=pltpu.CompilerParams(has_side_effects=_DATAFLOW),
    )(pltpu.with_memory_space_constraint(src, pltpu.HBM), pltpu.with_memory_space_constraint(land, pltpu.HBM), *afters)


def _direct_wait(send_sems, recv_sems, src_thru, land_thru, after, *, gather, name):
    afters = after if isinstance(after, tuple) else (after,)

    def body(src_ref, land_ref, send_sems_ref, recv_sems_ref, *rest):
        del rest
        for cp in _direct_copies(src_ref, land_ref, send_sems_ref, recv_sems_ref, gather, True):
            cp.wait_send()
            cp.wait_recv()

    return pl.pallas_call(
        body,
        name=name,
        out_shape=(pltpu.HBM(src_thru.shape, src_thru.dtype), pltpu.HBM(land_thru.shape, land_thru.dtype)),
        in_specs=(HBM, HBM, SEM, SEM) + tuple(ANY for _ in afters),
        out_specs=(HBM, HBM),
        input_output_aliases={0: 0, 1: 1},
        compiler_params=pltpu.CompilerParams(has_side_effects=_DATAFLOW),
    )(src_thru, land_thru, send_sems, recv_sems, *afters)


def _sum_partials(src, land, me, *, tr, name):
    R = src.shape[1]

    def body(me_ref, s_ref, l_ref, o_ref):
        del me_ref
        acc = s_ref[0].astype(F32)
        for r in range(N_DEV - 1):
            acc = acc + l_ref[r].astype(F32)
        o_ref[...] = acc

    return pl.pallas_call(
        body,
        grid_spec=pltpu.PrefetchScalarGridSpec(
            num_scalar_prefetch=1, grid=(R // tr,),
            in_specs=[pl.BlockSpec((1, tr, 1024), lambda i, mr: (mr[0], i, 0)),
                      pl.BlockSpec((N_DEV - 1, tr, 1024), lambda i, mr: (0, i, 0))],
            out_specs=pl.BlockSpec((tr, 1024), lambda i, mr: (i, 0))),
        out_shape=jax.ShapeDtypeStruct((R, 1024), F32),
        name=name,
    )(me, src, land)


_SMALL = ("lb_logits", "hg_norm_gain", "swa_sinks", "rel_bias", "ln1_g", "ln1_b", "ln2_g", "ln2_b")


def _pack_small_grads(d_lb, d_gain, d_sink, d_rb, d_ln1_g, d_ln1_b, d_ln2_g, d_ln2_b, loss):
    def body(lb_ref, gain_ref, sink_ref, rb_ref, l1g_ref, l1b_ref, l2g_ref, l2b_ref, loss_ref, o_ref):
        o_ref[...] = jnp.zeros_like(o_ref)
        for row, ref in ((SM_LB, lb_ref), (SM_GAIN, gain_ref), (SM_L1G, l1g_ref), (SM_L1B, l1b_ref),
                         (SM_L2G, l2g_ref), (SM_L2B, l2b_ref)):
            o_ref[row:row + 1, :] = ref[...]
        o_ref[SM_SINK:SM_SINK + 1, 0:128] = sink_ref[0:1, :]
        o_ref[SM_LOSS:SM_LOSS + 1, 0:128] = loss_ref[0:1, :]
        o_ref[SM_RB:SM_RB + NUM_BUCKETS, 0:128] = rb_ref[...]

    vm = pl.BlockSpec(memory_space=pltpu.VMEM)
    return pl.pallas_call(
        body,
        in_specs=[vm] * 9,
        out_specs=vm,
        out_shape=jax.ShapeDtypeStruct((SM_ROWS, D_MODEL), F32),
        name="pack_small_grads",
    )(d_lb, d_gain, d_sink, d_rb, d_ln1_g, d_ln1_b, d_ln2_g, d_ln2_b, loss)


def _small_finish(gathered, w, m, v):
    n = len(_SMALL)

    def body(*refs):
        g_ref = refs[0]
        w_refs, m_refs, v_refs = refs[1:1 + n], refs[1 + n:1 + 2 * n], refs[1 + 2 * n:1 + 3 * n]
        outs = refs[1 + 3 * n:]
        loss_ref, tot = outs[0], outs[-1]
        g_out, d_out, m_out, v_out = (outs[1 + k * n:1 + (k + 1) * n] for k in range(4))
        acc = g_ref[0]
        for d in range(1, N_DEV):
            acc = acc + g_ref[d]
        tot[...] = acc
        loss_ref[...] = tot[SM_LOSS:SM_LOSS + 1, 0:1]
        lb = _lower_bound(w_refs[0])
        dl0 = tot[SM_LB:SM_LB + 1, :] * lb * (1.0 - lb)
        grads = (jnp.concatenate([dl0, -dl0], axis=0), tot[SM_GAIN:SM_GAIN + 1, :],
                 tot[SM_SINK:SM_SINK + 1, 0:SWA_HEADS], tot[SM_RB:SM_RB + NUM_BUCKETS, 0:SWA_HEADS],
                 tot[SM_L1G:SM_L1G + 1, :], tot[SM_L1B:SM_L1B + 1, :], tot[SM_L2G:SM_L2G + 1, :], tot[SM_L2B:SM_L2B + 1, :])
        for k, g in enumerate(grads):
            g_out[k][...] = g
            d_out[k][...], m_out[k][...], v_out[k][...] = _adam_step(w_refs[k][...], g, m_refs[k][...], v_refs[k][...])

    vm = pl.BlockSpec(memory_space=pltpu.VMEM)
    shapes = [jax.ShapeDtypeStruct(w[k].shape, F32) for k in _SMALL]
    res = pl.pallas_call(
        body,
        in_specs=[vm] * (1 + 3 * n),
        out_specs=[vm] * (1 + 4 * n),
        out_shape=[jax.ShapeDtypeStruct((1, 1), F32)] + shapes * 4,
        scratch_shapes=[pltpu.VMEM((SM_ROWS, D_MODEL), F32)],
        name="small_finish",
    )(gathered, *[w[k] for k in _SMALL], *[m[k] for k in _SMALL], *[v[k] for k in _SMALL])
    parts = [dict(zip(_SMALL, res[1 + k * n:1 + (k + 1) * n])) for k in range(4)]
    return (res[0], *parts)


def _adam_step(w, g, m, v):
    nm = ADAM_B1 * m + (1.0 - ADAM_B1) * g
    nv = ADAM_B2 * v + (1.0 - ADAM_B2) * jnp.square(g)
    m_hat = nm / (1.0 - ADAM_B1 ** ADAM_STEP)
    v_hat = nv / (1.0 - ADAM_B2 ** ADAM_STEP)
    return -ADAM_LR * (m_hat / (jnp.sqrt(v_hat) + ADAM_EPS) + ADAM_WD * w), nm, nv


def _adamw(w, g, m, v, *, tr, name):
    R, C = w.shape

    def body(w_ref, g_ref, m_ref, v_ref, d_ref, nm_ref, nv_ref):
        d_ref[...], nm_ref[...], nv_ref[...] = _adam_step(w_ref[...], g_ref[...], m_ref[...], v_ref[...])

    spec = pl.BlockSpec((tr, C), lambda i: (i, 0))
    return pl.pallas_call(
        body,
        grid=(R // tr,),
        in_specs=[spec] * 4,
        out_specs=[spec] * 3,
        out_shape=[jax.ShapeDtypeStruct((R, C), F32)] * 3,
        compiler_params=_cparams("parallel"),
        name=name,
    )(w, g, m, v)


_WEIGHTS = ("w_in", "lb_logits", "hg_norm_gain", "swa_sinks", "rel_bias", "w_mem_kv", "w_branch_hg", "w_branch_swa",
            "w_branch_mem", "w_out", "ln1_g", "ln1_b", "w_up", "w_down", "ln2_g", "ln2_b")


def kernel(x, mem, w_in, lb_logits, hg_norm_gain, swa_sinks, rel_bias, w_mem_kv, w_branch_hg, w_branch_swa, w_branch_mem, w_out, ln1_g, ln1_b, w_up, w_down, ln2_g, ln2_b, loss_target, m_w_in, m_lb_logits, m_hg_norm_gain, m_swa_sinks, m_rel_bias, m_w_mem_kv, m_w_branch_hg, m_w_branch_swa, m_w_branch_mem, m_w_out, m_ln1_g, m_ln1_b, m_w_up, m_w_down, m_ln2_g, m_ln2_b, v_w_in, v_lb_logits, v_hg_norm_gain, v_swa_sinks, v_rel_bias, v_w_mem_kv, v_w_branch_hg, v_w_branch_swa, v_w_branch_mem, v_w_out, v_ln1_g, v_ln1_b, v_w_up, v_w_down, v_ln2_g, v_ln2_b):
    w = dict(w_in=w_in, lb_logits=lb_logits, hg_norm_gain=hg_norm_gain, swa_sinks=swa_sinks, rel_bias=rel_bias,
             w_mem_kv=w_mem_kv, w_branch_hg=w_branch_hg, w_branch_swa=w_branch_swa, w_branch_mem=w_branch_mem,
             w_out=w_out, ln1_g=ln1_g, ln1_b=ln1_b, w_up=w_up, w_down=w_down, ln2_g=ln2_g, ln2_b=ln2_b)
    mom = dict(w_in=m_w_in, lb_logits=m_lb_logits, hg_norm_gain=m_hg_norm_gain, swa_sinks=m_swa_sinks, rel_bias=m_rel_bias,
               w_mem_kv=m_w_mem_kv, w_branch_hg=m_w_branch_hg, w_branch_swa=m_w_branch_swa, w_branch_mem=m_w_branch_mem,
               w_out=m_w_out, ln1_g=m_ln1_g, ln1_b=m_ln1_b, w_up=m_w_up, w_down=m_w_down, ln2_g=m_ln2_g, ln2_b=m_ln2_b)
    var = dict(w_in=v_w_in, lb_logits=v_lb_logits, hg_norm_gain=v_hg_norm_gain, swa_sinks=v_swa_sinks, rel_bias=v_rel_bias,
               w_mem_kv=v_w_mem_kv, w_branch_hg=v_w_branch_hg, w_branch_swa=v_w_branch_swa, w_branch_mem=v_w_branch_mem,
               w_out=v_w_out, ln1_g=v_ln1_g, ln1_b=v_ln1_b, w_up=v_w_up, w_down=v_w_down, ln2_g=v_ln2_g, ln2_b=v_ln2_b)
    xc, yc, cc = _coords()

    p1 = _bf(w_in[0].T)
    p2 = _bf(jnp.concatenate([w_down[0], w_up[0].T, w_branch_hg[0], w_branch_swa[0], w_branch_mem[0], w_out[0],
                              w_mem_kv[0].T], axis=0))
    me = 4 * xc + 2 * yc + cc
    (g1,) = _all_gather_weights(p1)
    land2 = lax.dynamic_update_slice(lax.empty((N_DEV, R_OTHER, D_MODEL), BF16), p2[None], (me, 0, 0))
    ag2 = _direct_start(p2, land2, gather=True, name="gather_other_weights_start")

    def other_weights(after):
        return _direct_wait(*ag2[:4], after, gather=True, name="gather_other_weights_wait")[1]

    blocks = lambda a: a.reshape(N_DEV, a.shape[0] // N_DEV, D_MODEL)
    started = {}

    def send_other_grads(g):
        part = jnp.concatenate([blocks(g[k]) for k in ("wdn", "wup_t", "wbh", "wbs", "wbm", "wout", "wkv_t")], axis=1)
        started["others"] = _direct_start(part, lax.empty((N_DEV - 1, R_OTHER, D_MODEL), BF16), gather=False,
                                          name="scatter_other_grads_start")
        return started["others"][4]

    me1 = me.reshape(1).astype(jnp.int32)
    grads, delta, new_m, new_v = {}, {}, {}, {}

    def adamw(name):
        w2 = w[name][0]
        delta[name], new_m[name], new_v[name] = _adamw(
            w2, grads[name], mom[name][0], var[name][0], tr=w2.shape[0] // 4, name="adamw_" + name)

    def send_small_grads(packed):
        land = lax.dynamic_update_slice(lax.empty((N_DEV, SM_ROWS, D_MODEL), F32), packed[None], (me, 0, 0))
        started["small"] = _direct_start(packed, land, gather=True, name="gather_small_grads_start")
        return started["small"][4]

    def send_win_grad(g, after):
        started["win"] = _direct_start(blocks(g), lax.empty((N_DEV - 1, IN_SHARD, D_MODEL), BF16), gather=False,
                                       name="scatter_w_in_grad_start", after=after)
        mine2, landed2 = _direct_wait(*started["others"][:4], started["win"][4], gather=False,
                                      name="scatter_other_grads_wait")
        gs2 = _sum_partials(mine2, landed2, me1, tr=R_OTHER // 2, name="sum_other_grads")
        grads.update(
            w_down=gs2[R_DN:R_UP], w_up=gs2[R_UP:R_BH].T, w_branch_hg=gs2[R_BH:R_BS], w_branch_swa=gs2[R_BS:R_BM],
            w_branch_mem=gs2[R_BM:R_OUT], w_out=gs2[R_OUT:R_KV], w_mem_kv=gs2[R_KV:R_OTHER].T)
        for name in ("w_mem_kv", "w_branch_hg", "w_branch_swa", "w_branch_mem", "w_out", "w_up", "w_down"):
            adamw(name)
        return tuple(new_v[name] for name in new_v)

    grad_x = _local_step(
        x[0], mem[0], loss_target[0], lb_logits, hg_norm_gain, swa_sinks, rel_bias, ln1_g, ln1_b, ln2_g, ln2_b,
        g1.reshape(IN_COLS, D_MODEL), ag2[4], other_weights, send_other_grads, send_small_grads, send_win_grad)

    mine1, landed1 = _direct_wait(*started["win"][:4], grad_x, gather=False, name="scatter_w_in_grad_wait")
    g_win_t = _sum_partials(mine1, landed1, me1, tr=IN_SHARD // 2, name="sum_w_in_grad")
    d_t, m_t, v_t = _adamw(w_in[0].T, g_win_t, m_w_in[0].T, v_w_in[0].T, tr=IN_SHARD // 4, name="adamw_w_in")
    grads["w_in"], delta["w_in"], new_m["w_in"], new_v["w_in"] = g_win_t.T, d_t.T, m_t.T, v_t.T

    _, gathered = _direct_wait(*started["small"][:4], grad_x, gather=True, name="gather_small_grads_wait")
    loss, g_s, d_s, m_s, v_s = _small_finish(gathered, w, mom, var)
    for dst, src in ((grads, g_s), (delta, d_s), (new_m, m_s), (new_v, v_s)):
        dst.update(src)

    def shaped(d, name):
        return d[name].reshape(w[name].shape)

    return (loss.reshape(()), grad_x[None], *[shaped(grads, n) for n in _WEIGHTS], *[shaped(delta, n) for n in _WEIGHTS],
            *[shaped(new_m, n) for n in _WEIGHTS], *[shaped(new_v, n) for n in _WEIGHTS])
```

```python
import functools
import math

import jax
import jax.numpy as jnp
from jax import lax
from jax.experimental import pallas as pl
from jax.experimental.pallas import tpu as pltpu

F32 = jnp.float32
BF16 = jnp.bfloat16

D_MODEL = 1024
MEM_LEN = 256
HG_HEADS = 8
HG_DK = 128
HG_CHUNK = 64
SWA_HEADS = 16
SWA_HEAD_DIM = 64
SWA_BLOCK = 128
SWA_WINDOW = 128
MEM_HEADS = 4
MEM_HEAD_DIM = 256
NUM_BUCKETS = 32
MAX_DISTANCE = 128
D_FF = 4096
LN_EPS = 1e-5
RMS_EPS = 1e-6
ALPHA = 2.0 ** 0.25
N_DEV = 8

C_HQ, C_HF, C_HI, C_HG, C_SQ, C_SK, C_SV, C_MQ, C_GL = 0, 1024, 2048, 3072, 4096, 5120, 5248, 5376, 6400
IN_COLS = 9472
IN_SHARD = IN_COLS // N_DEV

ADAM_LR = 0.001
ADAM_B1 = 0.9
ADAM_B2 = 0.999
ADAM_EPS = 1e-08
ADAM_WD = 0.01
ADAM_STEP = 10

VMEM_LIMIT = 58 * 1024 * 1024

R_DN, R_UP, R_BH, R_BS, R_BM, R_OUT, R_KV, R_OTHER = 0, 512, 1024, 1152, 1280, 1408, 1536, 1792

SM_LB, SM_GAIN, SM_SINK, SM_L1G, SM_L1B, SM_L2G, SM_L2B, SM_LOSS, SM_RB, SM_ROWS = 0, 2, 3, 4, 5, 6, 7, 8, 16, 48


def _bf(v):
    return v.astype(BF16)


def _f32(v):
    return v.astype(F32)


def _dot(a, b):
    return jnp.dot(a, b, preferred_element_type=F32)


def _dot_nt(a, b):
    return lax.dot_general(a, b, (((1,), (1,)), ((), ())), preferred_element_type=F32)


def _dot_tn(a, b):
    return lax.dot_general(a, b, (((0,), (0,)), ((), ())), preferred_element_type=F32)


def _sig(v):
    return 0.5 * jnp.tanh(0.5 * v) + 0.5


def _cparams(*sem):
    return pltpu.CompilerParams(dimension_semantics=sem, vmem_limit_bytes=VMEM_LIMIT)


def _const_spec(shape):
    nd = len(shape)
    return pl.BlockSpec(shape, lambda *_: (0,) * nd, pipeline_mode=pl.Buffered(1))


def _dep_spec():
    return pl.BlockSpec((8, 128), lambda *_: (0, 0))


def _in_proj(x, win_t, dep, *, tm):
    S = x.shape[0]

    def body(x_ref, w_ref, dep_ref, z_ref, gl_ref, xb_ref):
        del dep_ref
        xb = _bf(x_ref[...])
        xb_ref[...] = xb
        for c0 in range(0, C_GL, 1280):
            z_ref[:, c0:c0 + 1280] = _bf(_dot_nt(xb, w_ref[c0:c0 + 1280, :]))
        for c0 in range(0, IN_COLS - C_GL, 1024):
            gl_ref[:, c0:c0 + 1024] = _bf(_dot_nt(xb, w_ref[C_GL + c0:C_GL + c0 + 1024, :]))

    row = lambda w: pl.BlockSpec((tm, w), lambda i: (i, 0))
    return pl.pallas_call(
        body,
        grid=(S // tm,),
        in_specs=[row(D_MODEL), _const_spec(win_t.shape), _dep_spec()],
        out_specs=[row(C_GL), row(IN_COLS - C_GL), row(D_MODEL)],
        out_shape=[jax.ShapeDtypeStruct((S, C_GL), BF16), jax.ShapeDtypeStruct((S, IN_COLS - C_GL), BF16),
                   jax.ShapeDtypeStruct((S, D_MODEL), BF16)],
        compiler_params=_cparams("parallel"),
        name="in_proj",
    )(x, win_t, dep)


def _mm_tn_resident(a, b, *, tm, kc, name, out_dtype):
    K, M = a.shape
    N = b.shape[1]
    nk = K // kc

    def body(a_ref, b_ref, o_ref):
        acc = jnp.zeros((tm, N), F32)
        for kk in range(nk):
            sl = pl.ds(kk * kc, kc)
            acc = acc + _dot_tn(_bf(a_ref[sl, :]), _bf(b_ref[sl, :]))
        o_ref[...] = acc.astype(o_ref.dtype)

    return pl.pallas_call(
        body,
        grid=(M // tm,),
        in_specs=[pl.BlockSpec((K, tm), lambda i: (0, i)), _const_spec((K, N))],
        out_specs=pl.BlockSpec((tm, N), lambda i: (i, 0)),
        out_shape=jax.ShapeDtypeStruct((M, N), out_dtype),
        compiler_params=_cparams("parallel"),
        name=name,
    )(a, b)


def _mm_tn(a, b, *, kc, name, out_dtype=F32):
    K, M = a.shape
    N = b.shape[1]
    if M > 1024:
        return _mm_tn_resident(a, b, tm=256, kc=min(kc, 1024), name=name, out_dtype=out_dtype)
    tm = M
    nk = K // kc

    def body(a_ref, b_ref, o_ref, acc):
        k = pl.program_id(1)
        part = _dot_tn(_bf(a_ref[...]), _bf(b_ref[...]))

        @pl.when(k == 0)
        def _():
            acc[...] = part

        @pl.when(k > 0)
        def _():
            acc[...] += part

        @pl.when(k == nk - 1)
        def _():
            o_ref[...] = acc[...].astype(o_ref.dtype)

    return pl.pallas_call(
        body,
        grid=(M // tm, nk),
        in_specs=[pl.BlockSpec((kc, tm), lambda i, k: (k, i)), pl.BlockSpec((kc, N), lambda i, k: (k, 0))],
        out_specs=pl.BlockSpec((tm, N), lambda i, k: (i, 0)),
        out_shape=jax.ShapeDtypeStruct((M, N), out_dtype),
        scratch_shapes=[pltpu.VMEM((tm, N), F32)],
        compiler_params=_cparams("parallel", "arbitrary"),
        name=name,
    )(a, b)


def _grad_x(d_qfv, d_hg_gl, d_sq, d_skv, d_mq, w_qfv, win_t, add, deps, *, tm):
    M = add.shape[0]
    pieces = (d_qfv, d_hg_gl, d_sq, d_skv, d_mq)

    def body(qfv_ref, hggl_ref, sq_ref, skv_ref, mq_ref, wq_ref, w_ref, add_ref, *rest):
        o_ref = rest[-1]
        acc = add_ref[...] + _dot(qfv_ref[...], wq_ref[...])
        acc = acc + _dot(hggl_ref[:, 0:1024], w_ref[C_HG:C_SQ, :])
        acc = acc + _dot(hggl_ref[:, 1024:4096], w_ref[C_GL:IN_COLS, :])
        acc = acc + _dot(sq_ref[...], w_ref[C_SQ:C_SK, :])
        acc = acc + _dot(skv_ref[...], w_ref[C_SK:C_MQ, :])
        o_ref[...] = acc + _dot(mq_ref[...], w_ref[C_MQ:C_GL, :])

    return pl.pallas_call(
        body,
        grid=(M // tm,),
        in_specs=[pl.BlockSpec((tm, p.shape[1]), lambda i: (i, 0)) for p in pieces]
        + [_const_spec(w_qfv.shape), _const_spec(win_t.shape), pl.BlockSpec((tm, D_MODEL), lambda i: (i, 0))]
        + [_dep_spec() for _ in deps],
        out_specs=pl.BlockSpec((tm, D_MODEL), lambda i: (i, 0)),
        out_shape=jax.ShapeDtypeStruct((M, D_MODEL), F32),
        compiler_params=_cparams("parallel"),
        name="grad_x",
    )(*pieces, w_qfv, win_t, add, *deps)


def _lower_bound(lbl_ref):
    l0 = lbl_ref[0:1, :]
    l1 = lbl_ref[1:2, :]
    mx = jnp.maximum(l0, l1)
    e0 = jnp.exp(l0 - mx)
    e1 = jnp.exp(l1 - mx)
    return e0 / (e0 + e1)


def _tri(lower):
    r = lax.broadcasted_iota(jnp.int32, (HG_CHUNK, HG_CHUNK), 0)
    c = lax.broadcasted_iota(jnp.int32, (HG_CHUNK, HG_CHUNK), 1)
    return (r >= c) if lower else (r <= c)


def _hg_gates(fl, lb):
    sg = _sig(fl)
    f = lb + (1.0 - lb) * sg
    return sg, f, jnp.log(f), 1.0 - f


def _scan_rows(v, reverse=False):
    row = lax.broadcasted_iota(jnp.int32, v.shape, 0)
    s = 1
    while s < HG_CHUNK:
        if reverse:
            v = v + jnp.where(row < HG_CHUNK - s, pltpu.roll(v, HG_CHUNK - s, 0), 0.0)
        else:
            v = v + jnp.where(row >= s, pltpu.roll(v, s, 0), 0.0)
        s *= 2
    return v


def _hgrn_fwd(zmain, lb_logits, *, T):
    S = zmain.shape[0]
    nc = T // HG_CHUNK

    def body(q_ref, f_ref, v_ref, lbl_ref, o_ref, st_ref, state):
        @pl.when(pl.program_id(1) == 0)
        def _():
            state[...] = jnp.zeros_like(state)

        lb = _lower_bound(lbl_ref)
        tril = _tri(True)
        qis, updates, decays, intra = [], [], [], []
        for c in range(nc):
            sl = pl.ds(c * HG_CHUNK, HG_CHUNK)
            _, _, g, k = _hg_gates(_f32(f_ref[sl, :]), lb)
            b = _scan_rows(g)
            bl = jnp.sum(g, axis=0, keepdims=True)
            qi = _bf(_f32(q_ref[sl, :]) * jnp.exp(b))
            ki = _bf(k * jnp.exp(-b))
            ko = _bf(k * jnp.exp(bl - b))
            vb = _bf(v_ref[sl, :])
            att = jnp.where(tril, _dot_nt(qi, ki), 0.0)
            intra.append(_dot(_bf(att), vb))
            qis.append(qi)
            updates.append(_dot_tn(vb, ko))
            decays.append(jnp.exp(bl))
        st = state[...]
        for c in range(nc):
            st_ref[0, c] = st
            o_ref[pl.ds(c * HG_CHUNK, HG_CHUNK), :] = intra[c] + _dot_nt(qis[c], _bf(st))
            st = st * decays[c] + updates[c]
        state[...] = st

    col = lambda base: pl.BlockSpec((T, HG_DK), lambda h, t: (t, base + h))
    return pl.pallas_call(
        body,
        grid=(HG_HEADS, S // T),
        in_specs=[col(0), col(8), col(16), pl.BlockSpec((2, HG_DK), lambda h, t: (0, h))],
        out_specs=[
            pl.BlockSpec((T, HG_DK), lambda h, t: (t, h)),
            pl.BlockSpec((1, nc, HG_DK, HG_DK), lambda h, t: (h, t, 0, 0)),
        ],
        out_shape=[
            jax.ShapeDtypeStruct((S, D_MODEL), F32),
            jax.ShapeDtypeStruct((HG_HEADS, S // HG_CHUNK, HG_DK, HG_DK), F32),
        ],
        scratch_shapes=[pltpu.VMEM((HG_DK, HG_DK), F32)],
        compiler_params=_cparams("parallel", "arbitrary"),
        name="hgrn_fwd",
    )(zmain, zmain, zmain, lb_logits)


def _hgrn_bwd(zmain, lb_logits, states, d_o, *, T):
    S = zmain.shape[0]
    nc = T // HG_CHUNK
    nt = S // T

    def body(q_ref, f_ref, v_ref, lbl_ref, st_ref, do_ref, dz_ref, dlb_ref, dstate):
        @pl.when(pl.program_id(1) == 0)
        def _():
            dstate[...] = jnp.zeros_like(dstate)
            dlb_ref[...] = jnp.zeros_like(dlb_ref)

        lb = _lower_bound(lbl_ref)
        tril = _tri(True)
        last_row = lax.broadcasted_iota(jnp.int32, (HG_CHUNK, HG_DK), 0) == HG_CHUNK - 1
        saved = []
        for c in range(nc):
            sl = pl.ds(c * HG_CHUNK, HG_CHUNK)
            sg, f, g, k = _hg_gates(_f32(f_ref[sl, :]), lb)
            b = _scan_rows(g)
            bl = jnp.sum(g, axis=0, keepdims=True)
            eb = jnp.exp(b)
            enb = jnp.exp(-b)
            eo = jnp.exp(bl - b)
            q_in = _f32(q_ref[sl, :]) * eb
            k_in = k * enb
            k_out = k * eo
            qi, ki, ko = _bf(q_in), _bf(k_in), _bf(k_out)
            vb = _bf(v_ref[sl, :])
            dob = do_ref[sl, :]
            att = jnp.where(tril, _dot_nt(qi, ki), 0.0)
            d_att = _bf(jnp.where(tril, _dot_nt(dob, vb), 0.0))
            d_kin = _dot_tn(d_att, qi)
            saved.append(dict(
                sg=sg, f=f, eb=eb, enb=enb, eo=eo, ebl=jnp.exp(bl), k_out=k_out, ko=ko, vb=vb, dob=dob,
                d_v=_dot_tn(_bf(att), dob), d_qin=_dot(d_att, ki), d_kin=d_kin,
                qk=(q_in, k_in), d_state=_dot_tn(dob, qi)))
        dst = dstate[...]
        dsts = [None] * nc
        for c in reversed(range(nc)):
            dsts[c] = dst
            dst = dst * saved[c]["ebl"] + saved[c]["d_state"]
        dstate[...] = dst
        dlb = jnp.zeros((1, HG_DK), F32)
        for c in range(nc):
            sl = pl.ds(c * HG_CHUNK, HG_CHUNK)
            s = saved[c]
            q_in, k_in = s["qk"]
            st = st_ref[0, c]
            dstb = _bf(dsts[c])
            d_v = s["d_v"] + _dot_nt(s["ko"], dstb)
            d_qin = s["d_qin"] + _dot(s["dob"], _bf(st))
            d_kout = _dot(s["vb"], dstb)
            d_decay = jnp.sum(dsts[c] * st, axis=0, keepdims=True)
            kk = d_kout * s["k_out"]
            d_b = d_qin * q_in - s["d_kin"] * k_in - kk
            d_bl = jnp.sum(kk, axis=0, keepdims=True) + d_decay * s["ebl"]
            d_g = _scan_rows(d_b + jnp.where(last_row, d_bl, 0.0), reverse=True)
            d_f = d_g / s["f"] - (s["d_kin"] * s["enb"] + d_kout * s["eo"])
            dz_ref[sl, 0:HG_DK] = _bf(d_qin * s["eb"])
            dz_ref[sl, HG_DK:2 * HG_DK] = _bf(d_f * (1.0 - lb) * s["sg"] * (1.0 - s["sg"]))
            dz_ref[sl, 2 * HG_DK:3 * HG_DK] = _bf(d_v)
            dlb = dlb + jnp.sum(d_f * (1.0 - s["sg"]), axis=0, keepdims=True)
        dlb_ref[...] += dlb

    rev = lambda base: pl.BlockSpec((T, HG_DK), lambda h, t: (nt - 1 - t, base + h))
    outc = pl.BlockSpec((T, HG_DK), lambda h, t: (nt - 1 - t, h))
    return pl.pallas_call(
        body,
        grid=(HG_HEADS, nt),
        in_specs=[
            rev(0), rev(8), rev(16),
            pl.BlockSpec((2, HG_DK), lambda h, t: (0, h)),
            pl.BlockSpec((1, nc, HG_DK, HG_DK), lambda h, t: (h, nt - 1 - t, 0, 0)),
            outc,
        ],
        out_specs=[pl.BlockSpec((T, 3 * HG_DK), lambda h, t: (nt - 1 - t, h)),
                   pl.BlockSpec((1, HG_DK), lambda h, t: (0, h))],
        out_shape=[jax.ShapeDtypeStruct((S, 3 * D_MODEL), BF16), jax.ShapeDtypeStruct((1, D_MODEL), F32)],
        scratch_shapes=[pltpu.VMEM((HG_DK, HG_DK), F32)],
        compiler_params=_cparams("parallel", "arbitrary"),
        name="hgrn_bwd",
    )(zmain, zmain, zmain, lb_logits, states, d_o)


def _t5_bucket_table():
    qi = jnp.arange(SWA_BLOCK)[:, None] + SWA_BLOCK
    kj = jnp.arange(2 * SWA_BLOCK)[None, :]
    n = jnp.clip(qi - kj, 0, SWA_WINDOW - 1)
    max_exact = NUM_BUCKETS // 2
    nf = jnp.maximum(n, 1).astype(F32)
    large = max_exact + (jnp.log(nf / max_exact) / math.log(MAX_DISTANCE / max_exact)
                         * (NUM_BUCKETS - max_exact)).astype(jnp.int32)
    large = jnp.minimum(large, NUM_BUCKETS - 1)
    return jnp.where(n < max_exact, n, large).astype(jnp.int32)


SWA_ROWS = 32
MERGE_GROUPS = 2


def _swa_bias_init(bias, bucket_ref, rb_ref):
    bk = bucket_ref[...]
    qi = lax.broadcasted_iota(jnp.int32, bk.shape, 0) + SWA_BLOCK
    kj = lax.broadcasted_iota(jnp.int32, bk.shape, 1)
    band = (qi - kj >= 0) & (qi - kj < SWA_WINDOW)
    for h in range(SWA_HEADS):
        def sel(b, acc, h=h):
            return jnp.where(bk == b, rb_ref[b, h], acc)
        t = lax.fori_loop(0, NUM_BUCKETS, sel, jnp.zeros(bk.shape, F32))
        bias[1, h] = jnp.where(band, t, -jnp.inf)
        bias[0, h] = jnp.where(band & (kj >= SWA_BLOCK), t, -jnp.inf)


def _lane_halves(t, kv_head):
    lane = lax.broadcasted_iota(jnp.int32, t.shape, 1)
    rolled = pltpu.roll(t, 64, 1)
    zero = jnp.zeros_like(t)
    if kv_head == 0:
        return jnp.where(lane < 64, t, zero), jnp.where(lane >= 64, rolled, zero)
    return jnp.where(lane < 64, rolled, zero), jnp.where(lane >= 64, t, zero)


def _swa_zero_key0(t):
    return jnp.where(lax.broadcasted_iota(jnp.int32, t.shape, 0) == 0, jnp.zeros_like(t), t)


def _swa_probs(s, masked_bias, sink):
    s = s + masked_bias
    m = jnp.maximum(jnp.max(s, axis=-1, keepdims=True), sink)
    p = jnp.exp(s - m)
    es = jnp.exp(sink - m)
    inv = 1.0 / (jnp.sum(p, axis=-1, keepdims=True) + es)
    return p * inv, es * inv


def _swa_fwd(zmain, bucket, rel_bias, sinks):
    S = zmain.shape[0]
    nb = S // SWA_BLOCK
    scale = SWA_HEAD_DIM ** -0.5

    def body(q_ref, kvc_ref, kvp_ref, bucket_ref, rb_ref, sk_ref, o_ref, p_ref, bias):
        n = pl.program_id(0)

        @pl.when(n == 0)
        def _():
            _swa_bias_init(bias, bucket_ref, rb_ref)

        later = jnp.minimum(n, 1)
        kk = _bf(jnp.concatenate([kvp_ref[:, 0:128], kvc_ref[:, 0:128]], axis=0))
        vv = _swa_zero_key0(_bf(jnp.concatenate([kvp_ref[:, 128:256], kvc_ref[:, 128:256]], axis=0)))
        first_col = lax.broadcasted_iota(jnp.int32, (SWA_ROWS, 2 * SWA_BLOCK), 1) == 0
        scores, values = {}, {}
        for kvh in range(2):
            qst = _bf(jnp.concatenate([q_ref[:, pl.ds((kvh * 4 + jj) * 128, 128)] for jj in range(4)], axis=0) * scale)
            values[kvh] = _lane_halves(vv, kvh)
            for odd, kx in enumerate(_lane_halves(kk, kvh)):
                scores[kvh, odd] = _dot_nt(qst, kx)
        probs = {}
        for (kvh, odd), s in scores.items():
            parts = []
            for jj in range(4):
                h = 2 * (kvh * 4 + jj) + odd
                for r0 in range(0, SWA_BLOCK, SWA_ROWS):
                    p, ps = _swa_probs(s[jj * SWA_BLOCK + r0:jj * SWA_BLOCK + r0 + SWA_ROWS],
                                       bias[later, h, pl.ds(r0, SWA_ROWS), :], sk_ref[0, h])
                    part = _bf(jnp.where(first_col, ps, p))
                    p_ref[pl.ds(r0, SWA_ROWS), pl.ds(h * 2 * SWA_BLOCK, 2 * SWA_BLOCK)] = part
                    parts.append(part)
            probs[kvh, odd] = jnp.concatenate(parts, axis=0)
        for kvh in range(2):
            ost = _dot(probs[kvh, 0], values[kvh][0]) + _dot(probs[kvh, 1], values[kvh][1])
            for jj in range(4):
                o_ref[:, pl.ds((kvh * 4 + jj) * 128, 128)] = ost[jj * SWA_BLOCK:(jj + 1) * SWA_BLOCK]

    smem = pl.BlockSpec(memory_space=pltpu.SMEM)
    return pl.pallas_call(
        body,
        grid=(nb,),
        in_specs=[
            pl.BlockSpec((SWA_BLOCK, 1024), lambda n: (n, C_SQ // 1024)),
            pl.BlockSpec((SWA_BLOCK, 256), lambda n: (n, C_SK // 256)),
            pl.BlockSpec((SWA_BLOCK, 256), lambda n: (jnp.maximum(n - 1, 0), C_SK // 256)),
            _const_spec((SWA_BLOCK, 2 * SWA_BLOCK)), smem, smem,
        ],
        out_specs=[pl.BlockSpec((SWA_BLOCK, 1024), lambda n: (n, 0)),
                   pl.BlockSpec((SWA_BLOCK, SWA_HEADS * 2 * SWA_BLOCK), lambda n: (n, 0))],
        out_shape=[jax.ShapeDtypeStruct((S, 1024), F32),
                   jax.ShapeDtypeStruct((S, SWA_HEADS * 2 * SWA_BLOCK), BF16)],
        scratch_shapes=[pltpu.VMEM((2, SWA_HEADS, SWA_BLOCK, 2 * SWA_BLOCK), F32)],
        compiler_params=_cparams("arbitrary"),
        name="swa_fwd",
    )(zmain, zmain, zmain, bucket, rel_bias, sinks)


def _swa_bwd(zmain, o_b, probs, d_o, bucket, dep):
    S = zmain.shape[0]
    nb = S // SWA_BLOCK
    scale = SWA_HEAD_DIM ** -0.5

    def body(q_ref, kvc_ref, kvp_ref, o_ref, p_ref, do_ref, bucket_ref, dep_ref,
             dq_ref, dkv_ref, drb_ref, dsk_ref, dbias, carry):
        del dep_ref
        n = pl.program_id(0)

        @pl.when(n == 0)
        def _():
            dbias[...] = jnp.zeros_like(dbias)
            carry[...] = jnp.zeros_like(carry)

        @pl.when(n < nb)
        def _():
            kk = _swa_zero_key0(_bf(jnp.concatenate([kvp_ref[:, 0:128], kvc_ref[:, 0:128]], axis=0)))
            vv = _swa_zero_key0(_bf(jnp.concatenate([kvp_ref[:, 128:256], kvc_ref[:, 128:256]], axis=0)))
            lane = lax.broadcasted_iota(jnp.int32, (2 * SWA_BLOCK, 128), 1)
            lane_q = lax.broadcasted_iota(jnp.int32, (4 * SWA_BLOCK, 128), 1)
            pair_cols = {kvh: [pl.ds((kvh * 4 + jj) * 128, 128) for jj in range(4)] for kvh in range(2)}
            qst, dost, ks, d_p, delta = {}, {}, {}, {}, {}
            for kvh in range(2):
                qst[kvh] = _bf(jnp.concatenate([q_ref[:, cl] for cl in pair_cols[kvh]], axis=0) * scale)
                dost[kvh] = jnp.concatenate([do_ref[:, cl] for cl in pair_cols[kvh]], axis=0)
                prod = dost[kvh].astype(F32) * jnp.concatenate([o_ref[:, cl] for cl in pair_cols[kvh]], axis=0)
                ks[kvh] = _lane_halves(kk, kvh)
                for odd, vx in enumerate(_lane_halves(vv, kvh)):
                    keep = (lane_q >= 64) if odd else (lane_q < 64)
                    delta[kvh, odd] = jnp.sum(jnp.where(keep, prod, 0.0), axis=-1, keepdims=True)
                    d_p[kvh, odd] = _dot_nt(dost[kvh], vx)
            pst, dsst = {}, {}
            for (kvh, odd), dp in d_p.items():
                p_parts, ds_parts = [], []
                for jj in range(4):
                    h = 2 * (kvh * 4 + jj) + odd
                    rows = slice(jj * SWA_BLOCK, (jj + 1) * SWA_BLOCK)
                    p = p_ref[:, pl.ds(h * 2 * SWA_BLOCK, 2 * SWA_BLOCK)]
                    ds = _f32(p) * (dp[rows] - delta[kvh, odd][rows])
                    dbias[h] += ds
                    p_parts.append(p)
                    ds_parts.append(_bf(ds))
                pst[kvh, odd] = jnp.concatenate(p_parts, axis=0)
                dsst[kvh, odd] = jnp.concatenate(ds_parts, axis=0)
            dk_parts, dv_parts = [], []
            for kvh in range(2):
                dq_st = _dot(dsst[kvh, 0], ks[kvh][0]) + _dot(dsst[kvh, 1], ks[kvh][1])
                for jj in range(4):
                    dq_ref[:, pair_cols[kvh][jj]] = _bf(dq_st[jj * SWA_BLOCK:(jj + 1) * SWA_BLOCK] * scale)
                zk = jnp.where(lane < 64, _dot_tn(dsst[kvh, 0], qst[kvh]), _dot_tn(dsst[kvh, 1], qst[kvh]))
                zv = jnp.where(lane < 64, _dot_tn(pst[kvh, 0], dost[kvh]), _dot_tn(pst[kvh, 1], dost[kvh]))
                dk_parts.append(zk + pltpu.roll(zk, 64, 1))
                dv_parts.append(zv + pltpu.roll(zv, 64, 1))
            dk = jnp.where(lane < 64, dk_parts[0], dk_parts[1])
            dv = jnp.where(lane < 64, dv_parts[0], dv_parts[1])
            dkv = _swa_zero_key0(jnp.concatenate([dk, dv], axis=1))
            dkv_ref[...] = _bf(carry[...] + dkv[0:SWA_BLOCK])
            carry[...] = dkv[SWA_BLOCK:]

        @pl.when(n == nb)
        def _():
            dkv_ref[...] = _bf(carry[...])
            first_col = lax.broadcasted_iota(jnp.int32, (SWA_BLOCK, 2 * SWA_BLOCK), 1) == 0
            bk = jnp.where(first_col, -1, bucket_ref[...])

            row = lax.broadcasted_iota(jnp.int32, (NUM_BUCKETS, 128), 0)
            lane = lax.broadcasted_iota(jnp.int32, (NUM_BUCKETS, 128), 1)

            def total(v):
                return jnp.sum(jnp.sum(v, axis=1, keepdims=True), axis=0, keepdims=True)

            def per_head(h, acc):
                db = dbias[h]
                d_rb, d_sk = acc
                d_sk = d_sk + jnp.where((row == 0) & (lane == h), total(jnp.where(first_col, db, 0.0)), 0.0)

                def per_bucket(b, d_rb):
                    return d_rb + jnp.where((row == b) & (lane == h), total(jnp.where(bk == b, db, 0.0)), 0.0)

                return lax.fori_loop(0, NUM_BUCKETS, per_bucket, d_rb), d_sk

            zero = jnp.zeros((NUM_BUCKETS, 128), F32)
            d_rb, d_sk = lax.fori_loop(0, SWA_HEADS, per_head, (zero, zero))
            drb_ref[...] = d_rb
            dsk_ref[...] = d_sk[0:8]

    cur = lambda n: jnp.minimum(n, nb - 1)
    prev = lambda n: jnp.maximum(jnp.minimum(n, nb - 1) - 1, 0)
    return pl.pallas_call(
        body,
        grid=(nb + 1,),
        in_specs=[
            pl.BlockSpec((SWA_BLOCK, 1024), lambda n: (cur(n), C_SQ // 1024)),
            pl.BlockSpec((SWA_BLOCK, 256), lambda n: (cur(n), C_SK // 256)),
            pl.BlockSpec((SWA_BLOCK, 256), lambda n: (prev(n), C_SK // 256)),
            pl.BlockSpec((SWA_BLOCK, 1024), lambda n: (cur(n), 0)),
            pl.BlockSpec((SWA_BLOCK, SWA_HEADS * 2 * SWA_BLOCK), lambda n: (cur(n), 0)),
            pl.BlockSpec((SWA_BLOCK, 1024), lambda n: (cur(n), 0)),
            _const_spec((SWA_BLOCK, 2 * SWA_BLOCK)), _dep_spec(),
        ],
        out_specs=[
            pl.BlockSpec((SWA_BLOCK, 1024), lambda n: (cur(n), 0)),
            pl.BlockSpec((SWA_BLOCK, 256), lambda n: (jnp.maximum(n - 1, 0), 0)),
            pl.BlockSpec((NUM_BUCKETS, 128), lambda n: (0, 0)),
            pl.BlockSpec((8, 128), lambda n: (0, 0)),
        ],
        out_shape=[
            jax.ShapeDtypeStruct((S, 1024), BF16),
            jax.ShapeDtypeStruct((S, 256), BF16),
            jax.ShapeDtypeStruct((NUM_BUCKETS, 128), F32),
            jax.ShapeDtypeStruct((8, 128), F32),
        ],
        scratch_shapes=[
            pltpu.VMEM((SWA_HEADS, SWA_BLOCK, 2 * SWA_BLOCK), F32),
            pltpu.VMEM((SWA_BLOCK, 256), F32),
        ],
        compiler_params=_cparams("arbitrary"),
        name="swa_bwd",
    )(zmain, zmain, zmain, o_b, probs, d_o, bucket, dep)


def _mem_q_specs(T):
    return [pl.BlockSpec((T, MEM_HEAD_DIM), lambda t, h=h: (t, C_MQ // MEM_HEAD_DIM + h)) for h in range(MEM_HEADS)]


def _mem_kv_proj(mem, g2):
    def body(mem_ref, w_ref, o_ref):
        o_ref[...] = _dot_nt(_bf(mem_ref[...]), _rows(w_ref))

    return pl.pallas_call(
        body,
        grid=(1,),
        in_specs=[pl.BlockSpec((MEM_LEN, D_MODEL), lambda i: (0, 0)), _gathered_spec(R_KV, R_OTHER)],
        out_specs=pl.BlockSpec((MEM_LEN, 2048), lambda i: (0, 0)),
        out_shape=jax.ShapeDtypeStruct((MEM_LEN, 2048), F32),
        compiler_params=_cparams("arbitrary"),
        name="mem_kv_proj",
    )(mem, g2)


def _mem_fwd(zmain, mkv, *, T):
    S = zmain.shape[0]

    def body(q0, q1, q2, q3, kv_ref, o_ref, p_ref):
        heads = [pl.ds(h * MEM_HEAD_DIM, MEM_HEAD_DIM) for h in range(MEM_HEADS)]
        scores = [_dot_nt(_bf(q_ref[...] * (MEM_HEAD_DIM ** -0.5)), _bf(kv_ref[:, cols]))
                  for q_ref, cols in zip((q0, q1, q2, q3), heads)]
        probs = []
        for s, cols in zip(scores, heads):
            e = jnp.exp(s - jnp.max(s, axis=-1, keepdims=True))
            pb = _bf(e * (1.0 / jnp.sum(e, axis=-1, keepdims=True)))
            p_ref[:, cols] = pb
            probs.append(pb)
        for h, (pb, cols) in enumerate(zip(probs, heads)):
            o_ref[:, cols] = _dot(pb, _bf(kv_ref[:, pl.ds(1024 + h * MEM_HEAD_DIM, MEM_HEAD_DIM)]))

    row = pl.BlockSpec((T, 1024), lambda t: (t, 0))
    return pl.pallas_call(
        body,
        grid=(S // T,),
        in_specs=_mem_q_specs(T) + [_const_spec((MEM_LEN, 2048))],
        out_specs=[row, row],
        out_shape=[jax.ShapeDtypeStruct((S, 1024), F32), jax.ShapeDtypeStruct((S, 1024), BF16)],
        compiler_params=_cparams("parallel"),
        name="mem_fwd",
    )(zmain, zmain, zmain, zmain, mkv)


def _mem_bwd(zmain, mkv, o_c, probs, d_o, *, T):
    S = zmain.shape[0]
    scale = MEM_HEAD_DIM ** -0.5

    def body(q0, q1, q2, q3, kv_ref, o_ref, p_ref, do_ref, dq_ref, dkv_ref):
        @pl.when(pl.program_id(0) == 0)
        def _():
            dkv_ref[...] = jnp.zeros_like(dkv_ref)

        heads = [(pl.ds(h * MEM_HEAD_DIM, MEM_HEAD_DIM), pl.ds(1024 + h * MEM_HEAD_DIM, MEM_HEAD_DIM))
                 for h in range(MEM_HEADS)]
        d_p = [_dot_nt(do_ref[:, cols], _bf(kv_ref[:, vcols])) for cols, vcols in heads]
        d_s = []
        for dp, (cols, _) in zip(d_p, heads):
            delta = jnp.sum(do_ref[:, cols].astype(F32) * o_ref[:, cols], axis=-1, keepdims=True)
            d_s.append(_bf(_f32(p_ref[:, cols]) * (dp - delta)))
        for ds, q_ref, (cols, vcols) in zip(d_s, (q0, q1, q2, q3), heads):
            dq_ref[:, cols] = _bf(_dot(ds, _bf(kv_ref[:, cols])) * scale)
            dkv_ref[:, cols] += _dot_tn(ds, _bf(q_ref[...] * scale))
            dkv_ref[:, vcols] += _dot_tn(p_ref[:, cols], do_ref[:, cols])

    row = pl.BlockSpec((T, 1024), lambda t: (t, 0))
    return pl.pallas_call(
        body,
        grid=(S // T,),
        in_specs=_mem_q_specs(T) + [_const_spec((MEM_LEN, 2048)), row, row, row],
        out_specs=[row, pl.BlockSpec((MEM_LEN, 2048), lambda t: (0, 0))],
        out_shape=[jax.ShapeDtypeStruct((S, 1024), BF16), jax.ShapeDtypeStruct((MEM_LEN, 2048), F32)],
        compiler_params=_cparams("arbitrary"),
        name="mem_bwd",
    )(zmain, zmain, zmain, zmain, mkv, o_c, probs, d_o)


def _layer_norm(u):
    mu = jnp.mean(u, axis=-1, keepdims=True)
    xc = u - mu
    rstd = lax.rsqrt(jnp.mean(xc * xc, axis=-1, keepdims=True) + LN_EPS)
    return xc * rstd, rstd


def _layer_norm_bwd(dy, gamma, xhat, rstd):
    dxh = dy * gamma
    return rstd * (dxh - jnp.mean(dxh, axis=-1, keepdims=True) - xhat * jnp.mean(dxh * xhat, axis=-1, keepdims=True))


def _merge_stages(rows, oraw_ref, hg_ref, ob_ref, oc_ref, gl_ref, x_ref, gain_ref, wbh, wbs, wbm, wout, g_ref, b_ref,
                  fwd_out=None, bwd=None):
    ys, rs = [], []
    for h in range(HG_HEADS):
        oh = oraw_ref[rows, pl.ds(h * HG_DK, HG_DK)]
        r = lax.rsqrt(jnp.mean(oh * oh, axis=-1, keepdims=True) + RMS_EPS)
        ys.append(oh * r)
        rs.append(r)
    y = jnp.concatenate(ys, axis=1)
    hg = _f32(hg_ref[rows, :])
    sg = _sig(hg)
    silu = hg * sg
    gain = gain_ref[...]
    oa = _bf(y * gain * silu)
    pa = _dot(oa, _rows(wbh))
    pb = _dot(_bf(ob_ref[rows, :]), _rows(wbs))
    pc = _dot(_bf(oc_ref[rows, :]), _rows(wbm))
    yield
    gates = [_sig(_f32(gl_ref[rows, pl.ds(i * 1024, 1024)])) for i in range(3)]
    m = _bf(gates[0] * pa + gates[1] * pb + gates[2] * pc)
    mix = _dot(m, _rows(wout))
    yield
    xhat, rstd = _layer_norm(ALPHA * x_ref[rows, :] + mix)
    if bwd is None:
        h1 = xhat * g_ref[...] + b_ref[...]
        fwd_out[0][rows, :] = h1
        fwd_out[1][rows, :] = _bf(h1)
        return
    (dh1_ref, dx_ref, du1_ref, m_ref, oa_ref, dpa_ref, dpb_ref, dpc_ref, doraw_ref, dob_ref, doc_ref, dz_ref,
     dgain_ref, dg_ref, db_ref) = bwd
    dh1 = dh1_ref[rows, :]
    dg_ref[...] += jnp.sum(dh1 * xhat, axis=0, keepdims=True)
    db_ref[...] += jnp.sum(dh1, axis=0, keepdims=True)
    du1 = _layer_norm_bwd(dh1, g_ref[...], xhat, rstd)
    dx_ref[rows, :] = ALPHA * du1
    du1b = _bf(du1)
    du1_ref[rows, :] = du1b
    m_ref[rows, :] = m
    oa_ref[rows, :] = oa
    dm = _dot_nt(du1b, _rows(wout))
    yield
    d_branches = []
    for i, (g, p, dp_ref, w_r) in enumerate(zip(gates, (pa, pb, pc), (dpa_ref, dpb_ref, dpc_ref), (wbh, wbs, wbm))):
        dz_ref[rows, pl.ds((i + 1) * 1024, 1024)] = _bf(dm * p * g * (1.0 - g))
        dp = _bf(dm * g)
        dp_ref[rows, :] = dp
        d_branches.append(_dot_nt(dp, _rows(w_r)))
    yield
    doa, d_ob, d_oc = d_branches
    dob_ref[rows, :] = _bf(d_ob)
    doc_ref[rows, :] = _bf(d_oc)
    t = doa * y
    dgain_ref[...] += jnp.sum(t * silu, axis=0, keepdims=True)
    dz_ref[rows, 0:1024] = _bf(t * gain * sg * (1.0 + hg * (1.0 - sg)))
    dy = doa * gain * silu
    for h in range(HG_HEADS):
        cols = slice(h * HG_DK, (h + 1) * HG_DK)
        yh = y[:, cols]
        dyh = dy[:, cols]
        doraw_ref[rows, pl.ds(h * HG_DK, HG_DK)] = _bf(rs[h] * (dyh - yh * jnp.mean(dyh * yh, axis=-1, keepdims=True)))


def _interleave(chains):
    live = list(chains)
    while live:
        still = []
        for c in live:
            try:
                next(c)
                still.append(c)
            except StopIteration:
                pass
        live = still


def _gathered_spec(lo, hi):
    n = hi - lo
    return pl.BlockSpec((N_DEV, n, D_MODEL), lambda *_: (0, lo // n, 0), pipeline_mode=pl.Buffered(1))


def _rows(w_ref):
    return w_ref[...].reshape(-1, D_MODEL)


def _merge_in_specs(T):
    row = lambda w, c=0: pl.BlockSpec((T, w), lambda i: (i, c))
    vec = pl.BlockSpec((1, D_MODEL), lambda i: (0, 0))
    w = [_gathered_spec(lo, hi) for lo, hi in ((R_BH, R_BS), (R_BS, R_BM), (R_BM, R_OUT), (R_OUT, R_KV))]
    return [row(1024), row(1024, C_HG // 1024), row(1024), row(1024), row(3072), row(1024), vec, *w, vec, vec]


def _merge_fwd(o_raw, zmain, o_b, o_c, gl, x, gain, wbh, wbs, wbm, wout, ln_g, ln_b, *, T):
    S = x.shape[0]

    def body(*refs):
        ins, outs = refs[:13], refs[13:]
        _interleave(_merge_stages(pl.ds(r0, T // MERGE_GROUPS), *ins, fwd_out=outs)
                    for r0 in range(0, T, T // MERGE_GROUPS))

    row = pl.BlockSpec((T, D_MODEL), lambda i: (i, 0))
    return pl.pallas_call(
        body,
        grid=(S // T,),
        in_specs=_merge_in_specs(T),
        out_specs=[row, row],
        out_shape=[jax.ShapeDtypeStruct((S, D_MODEL), F32), jax.ShapeDtypeStruct((S, D_MODEL), BF16)],
        compiler_params=_cparams("parallel"),
        name="merge_fwd",
    )(o_raw, zmain, o_b, o_c, gl, x, gain, wbh, wbs, wbm, wout, ln_g, ln_b)


def _merge_bwd(d_h1, o_raw, zmain, o_b, o_c, gl, x, gain, wbh, wbs, wbm, wout, ln_g, ln_b, *, T):
    S = x.shape[0]

    def body(dh1_ref, oraw_ref, hg_ref, ob_ref, oc_ref, gl_ref, x_ref, gain_ref, wbh_r, wbs_r, wbm_r, wout_r, g_ref, b_ref,
             dx_ref, du1_ref, m_ref, oa_ref, dpa_ref, dpb_ref, dpc_ref, doraw_ref, dob_ref, doc_ref, dz_ref,
             dgain_ref, dg_ref, db_ref):
        del b_ref

        @pl.when(pl.program_id(0) == 0)
        def _():
            dgain_ref[...] = jnp.zeros_like(dgain_ref)
            dg_ref[...] = jnp.zeros_like(dg_ref)
            db_ref[...] = jnp.zeros_like(db_ref)

        ins = (oraw_ref, hg_ref, ob_ref, oc_ref, gl_ref, x_ref, gain_ref, wbh_r, wbs_r, wbm_r, wout_r, g_ref, None)
        bwd = (dh1_ref, dx_ref, du1_ref, m_ref, oa_ref, dpa_ref, dpb_ref, dpc_ref, doraw_ref, dob_ref, doc_ref, dz_ref,
               dgain_ref, dg_ref, db_ref)
        _interleave([_merge_stages(pl.ds(0, T), *ins, bwd=bwd)])

    row = lambda w: pl.BlockSpec((T, w), lambda i: (i, 0))
    vec = pl.BlockSpec((1, D_MODEL), lambda i: (0, 0))
    bshape = jax.ShapeDtypeStruct((S, D_MODEL), BF16)
    vshape = jax.ShapeDtypeStruct((1, D_MODEL), F32)
    return pl.pallas_call(
        body,
        grid=(S // T,),
        in_specs=[row(1024)] + _merge_in_specs(T),
        out_specs=[row(1024)] * 10 + [row(4096), vec, vec, vec],
        out_shape=[jax.ShapeDtypeStruct((S, D_MODEL), F32)] + [bshape] * 9
        + [jax.ShapeDtypeStruct((S, 4096), BF16), vshape, vshape, vshape],
        compiler_params=_cparams("arbitrary"),
        name="merge_bwd",
    )(d_h1, o_raw, zmain, o_b, o_c, gl, x, gain, wbh, wbs, wbm, wout, ln_g, ln_b)


def _mlp_fwd_bwd(h1, target, wup_t, wdn, ln_g, ln_b, *, T, FC):
    S = h1.shape[0]
    nf = D_FF // FC
    assert FC == R_BH - R_UP == R_UP - R_DN

    def body(h1_ref, t_ref, wup_ref, wdn_ref, g_ref, b_ref, dh1_ref, a_ref, dup_ref, du2_ref, loss_ref, dg_ref, db_ref, up_scr):
        @pl.when(pl.program_id(0) == 0)
        def _():
            loss_ref[...] = jnp.zeros_like(loss_ref)
            dg_ref[...] = jnp.zeros_like(dg_ref)
            db_ref[...] = jnp.zeros_like(db_ref)

        h1v = h1_ref[...]
        h1b = _bf(h1v)
        ff = jnp.zeros((T, D_MODEL), F32)
        for j in range(nf):
            rows = pl.ds(j * FC, FC)
            up = jnp.maximum(_dot_nt(h1b, wup_ref[j]), 0.0)
            up_scr[:, rows] = _bf(up)
            a = _bf(up * up)
            a_ref[:, rows] = a
            ff = ff + _dot(a, wdn_ref[j])
        xhat, rstd = _layer_norm(ALPHA * h1v + ff)
        gamma = g_ref[...]
        err = xhat * gamma + b_ref[...] - t_ref[...]
        loss_ref[...] += jnp.sum(jnp.sum(err * err, axis=-1, keepdims=True), axis=0, keepdims=True) * (0.5 / D_MODEL)
        dy = err * (1.0 / D_MODEL)
        dg_ref[...] += jnp.sum(dy * xhat, axis=0, keepdims=True)
        db_ref[...] += jnp.sum(dy, axis=0, keepdims=True)
        du2 = _layer_norm_bwd(dy, gamma, xhat, rstd)
        du2b = _bf(du2)
        du2_ref[...] = du2b
        dh1 = ALPHA * du2
        for j in range(nf):
            rows = pl.ds(j * FC, FC)
            dup = _bf(_dot_nt(du2b, wdn_ref[j]) * (2.0 * up_scr[:, rows].astype(F32)))
            dup_ref[:, rows] = dup
            dh1 = dh1 + _dot(dup, wup_ref[j])
        dh1_ref[...] = dh1

    row = lambda w: pl.BlockSpec((T, w), lambda i: (i, 0))
    vec = pl.BlockSpec((1, D_MODEL), lambda i: (0, 0))
    vshape = jax.ShapeDtypeStruct((1, D_MODEL), F32)
    return pl.pallas_call(
        body,
        grid=(S // T,),
        in_specs=[row(1024), row(1024), _gathered_spec(R_UP, R_BH), _gathered_spec(R_DN, R_UP), vec, vec],
        out_specs=[row(1024), row(D_FF), row(D_FF), row(1024), pl.BlockSpec((8, 128), lambda i: (0, 0)), vec, vec],
        out_shape=[
            jax.ShapeDtypeStruct((S, D_MODEL), F32),
            jax.ShapeDtypeStruct((S, D_FF), BF16),
            jax.ShapeDtypeStruct((S, D_FF), BF16),
            jax.ShapeDtypeStruct((S, D_MODEL), BF16),
            jax.ShapeDtypeStruct((8, 128), F32), vshape, vshape,
        ],
        scratch_shapes=[pltpu.VMEM((T, D_FF), BF16)],
        compiler_params=_cparams("arbitrary"),
        name="mlp_fwd_bwd",
    )(h1, target, wup_t, wdn, ln_g, ln_b)


def _local_step(x, mem, target, lb_logits, gain, sinks, rel_bias, ln1_g, ln1_b, ln2_g, ln2_b,
                win_t, dep0, other_weights, send_other_grads, send_small_grads, send_win_grad):
    S = x.shape[0]
    T = min(256, S)
    KC = min(2048, S)
    zmain, gl, xb = _in_proj(x, win_t, dep0, tm=min(512, S))
    bucket = _t5_bucket_table()

    o_raw, states = _hgrn_fwd(zmain, lb_logits, T=min(2048, S))
    o_b, swa_probs = _swa_fwd(zmain, bucket, rel_bias, sinks)
    g2 = other_weights((o_b, o_raw))
    mkv = _mem_kv_proj(mem, g2)
    o_c, mem_probs = _mem_fwd(zmain, mkv, T=min(1024, S))
    merge_args = (o_raw, zmain, o_b, o_c, gl, x, gain, g2, g2, g2, g2, ln1_g, ln1_b)
    h1, h1b = _merge_fwd(*merge_args, T=min(512, S))

    d_h1, act, d_up, du2, loss, d_ln2_g, d_ln2_b = _mlp_fwd_bwd(h1, target, g2, g2, ln2_g, ln2_b, T=min(512, S), FC=512)
    wgrad = functools.partial(_mm_tn, out_dtype=BF16)
    g_wdn = wgrad(act, du2, kc=KC, name="grad_w_down")
    g_wup_t = wgrad(d_up, h1b, kc=KC, name="grad_w_up")

    (dx_part, du1, m, oa, dpa, dpb, dpc, d_oraw, d_ob, d_oc, d_hg_gl,
     d_gain, d_ln1_g, d_ln1_b) = _merge_bwd(d_h1, *merge_args, T=T)
    g_wout = wgrad(m, du1, kc=KC, name="grad_w_out")
    g_wbh = wgrad(oa, dpa, kc=KC, name="grad_w_branch_hg")
    g_wbs = wgrad(o_b, dpb, kc=KC, name="grad_w_branch_swa")
    g_wbm = wgrad(o_c, dpc, kc=KC, name="grad_w_branch_mem")

    d_mq, d_mkv = _mem_bwd(zmain, mkv, o_c, mem_probs, d_oc, T=min(1024, S))
    g_wkv_t = wgrad(d_mkv, mem, kc=MEM_LEN, name="grad_w_mem_kv")
    sent_others = send_other_grads(
        dict(wkv_t=g_wkv_t, wbh=g_wbh, wbs=g_wbs, wbm=g_wbm, wout=g_wout, wup_t=g_wup_t, wdn=g_wdn))
    d_sq, d_skv, d_rb, d_sink = _swa_bwd(zmain, o_b, swa_probs, d_ob, bucket, sent_others)
    d_qfv, d_lb = _hgrn_bwd(zmain, lb_logits, states, d_oraw, T=min(2048, S))
    sent_small = send_small_grads(_pack_small_grads(d_lb, d_gain, d_sink, d_rb, d_ln1_g, d_ln1_b, d_ln2_g, d_ln2_b, loss))

    head_major = lambda a: a.reshape(3, HG_HEADS, HG_DK, D_MODEL).transpose(1, 0, 2, 3).reshape(3 * D_MODEL, D_MODEL)
    col_major = lambda a: a.reshape(HG_HEADS, 3, HG_DK, D_MODEL).transpose(1, 0, 2, 3).reshape(3 * D_MODEL, D_MODEL)
    pieces = (d_qfv, d_hg_gl, d_sq, d_skv, d_mq)
    g_qfv, g_hg_gl, g_sq, g_skv, g_mq = [
        wgrad(p, xb, kc=KC, name="grad_w_in_" + n) for p, n in zip(pieces, ("qfv", "hg_gates", "swa_q", "swa_kv", "mem_q"))]
    g_win_t = jnp.concatenate([col_major(g_qfv), g_hg_gl[:D_MODEL], g_sq, g_skv, g_mq, g_hg_gl[D_MODEL:]], axis=0)
    sent_win = send_win_grad(g_win_t, sent_small)
    return _grad_x(*pieces, head_major(win_t[:C_HG]), win_t, dx_part, sent_win, tm=T)


MESH = pl.DeviceIdType.MESH
ANY = pl.BlockSpec(memory_space=pl.ANY)


def _coords():
    return lax.axis_index("x"), lax.axis_index("y"), lax.axis_index("c")


def _other_chips(x, y):
    return [(1 - x, y), (x, 1 - y), (1 - x, 1 - y)]


def _all_gather_weights(*arrays):
    na = len(arrays)

    def body(*refs):
        srcs, dsts = refs[:na], refs[na:2 * na]
        send_sems, recv_sems, local_sems = refs[2 * na:]
        x, y, c = _coords()
        me, sibling = (x, y, c), (x, y, 1 - c)
        chips = _other_chips(x, y)

        def slot(a, px, py, pc):
            return dsts[a].at[4 * px + 2 * py + pc]

        def copy(a, k, block, to, from_shard=False):
            return pltpu.make_async_remote_copy(
                src_ref=srcs[a] if from_shard else slot(a, *block), dst_ref=slot(a, *block),
                send_sem=send_sems.at[a * 7 + k], recv_sem=recv_sems.at[a * 7 + k],
                device_id=to, device_id_type=MESH)

        own = [pltpu.make_async_copy(srcs[a], slot(a, *me), local_sems.at[a]) for a in range(na)]
        for cp in own:
            cp.start()
        first = []
        for a in range(na):
            first.append(copy(a, 0, me, sibling, True))
            first += [copy(a, 1 + j, me, (*chip, c), True) for j, chip in enumerate(chips)]
        for cp in first:
            cp.start()
        passed = []
        for j, chip in enumerate(chips):
            for a in range(na):
                copy(a, 1 + j, (*chip, c), me).wait_recv()
                fwd = copy(a, 4 + j, (*chip, c), sibling)
                fwd.start()
                passed.append(fwd)
        for a in range(na):
            copy(a, 0, sibling, me).wait_recv()
            for j, chip in enumerate(chips):
                copy(a, 4 + j, (*chip, 1 - c), me).wait_recv()
        for cp in first + passed:
            cp.wait_send()
        for cp in own:
            cp.wait()

    return pl.pallas_call(
        body,
        in_specs=[ANY] * na,
        out_specs=[ANY] * na,
        out_shape=[jax.ShapeDtypeStruct((N_DEV,) + a.shape, a.dtype) for a in arrays],
        scratch_shapes=[pltpu.SemaphoreType.DMA((7 * na,)), pltpu.SemaphoreType.DMA((7 * na,)),
                        pltpu.SemaphoreType.DMA((na,))],
        name="all_gather_weights",
    )(*arrays)


HBM = pl.BlockSpec(memory_space=pltpu.HBM)
SEM = pl.BlockSpec(memory_space=pltpu.SEMAPHORE)
_DATAFLOW = pltpu.SideEffectType.DATAFLOW_SIDE_EFFECTING


def _peer(x, y, c, r):
    return x ^ (r >> 2), y ^ ((r >> 1) & 1), c ^ (r & 1)


def _direct_copies(src_ref, land_ref, send_sems, recv_sems, gather, receiving):
    x, y, c = _coords()
    me = 4 * x + 2 * y + c
    copies = []
    for r in range(1, N_DEV):
        px, py, pc = _peer(x, y, c, r)
        peer = 4 * px + 2 * py + pc
        if gather:
            src, dst = src_ref, land_ref.at[peer if receiving else me]
        else:
            src, dst = src_ref.at[peer], land_ref.at[r - 1]
        copies.append(pltpu.make_async_remote_copy(
            src_ref=src, dst_ref=dst, send_sem=send_sems.at[r - 1], recv_sem=recv_sems.at[r - 1],
            device_id=(px, py, pc), device_id_type=MESH))
    return copies


def _direct_start(src, land, *, gather, name, after=None):
    def body(src_ref, land_ref, *rest):
        send_sems, recv_sems, token = rest[-5], rest[-4], rest[-1]
        for cp in _direct_copies(src_ref, land_ref, send_sems, recv_sems, gather, False):
            cp.start()
        token[...] = jnp.zeros_like(token)

    afters = () if after is None else (after,)
    return pl.pallas_call(
        body,
        name=name,
        out_shape=(pltpu.SemaphoreType.DMA((N_DEV - 1,)), pltpu.SemaphoreType.DMA((N_DEV - 1,)),
                   pltpu.HBM(src.shape, src.dtype), pltpu.HBM(land.shape, land.dtype),
                   jax.ShapeDtypeStruct((8, 128), F32)),
        in_specs=(HBM, HBM) + tuple(ANY for _ in afters),
        out_specs=(SEM, SEM, HBM, HBM, pl.BlockSpec(memory_space=pltpu.VMEM)),
        input_output_aliases={0: 2, 1: 3},
        compiler_params=pltpu.CompilerParams(has_side_effects=_DATAFLOW),
    )(pltpu.with_memory_space_constraint(src, pltpu.HBM), pltpu.with_memory_space_constraint(land, pltpu.HBM), *afters)


def _direct_wait(send_sems, recv_sems, src_thru, land_thru, after, *, gather, name):
    afters = after if isinstance(after, tuple) else (after,)

    def body(src_ref, land_ref, send_sems_ref, recv_sems_ref, *rest):
        del rest
        for cp in _direct_copies(src_ref, land_ref, send_sems_ref, recv_sems_ref, gather, True):
            cp.wait_send()
            cp.wait_recv()

    return pl.pallas_call(
        body,
        name=name,
        out_shape=(pltpu.HBM(src_thru.shape, src_thru.dtype), pltpu.HBM(land_thru.shape, land_thru.dtype)),
        in_specs=(HBM, HBM, SEM, SEM) + tuple(ANY for _ in afters),
        out_specs=(HBM, HBM),
        input_output_aliases={0: 0, 1: 1},
        compiler_params=pltpu.CompilerParams(has_side_effects=_DATAFLOW),
    )(src_thru, land_thru, send_sems, recv_sems, *afters)


def _sum_partials(src, land, me, *, tr, name):
    R = src.shape[1]

    def body(me_ref, s_ref, l_ref, o_ref):
        del me_ref
        acc = s_ref[0].astype(F32)
        for r in range(N_DEV - 1):
            acc = acc + l_ref[r].astype(F32)
        o_ref[...] = acc

    return pl.pallas_call(
        body,
        grid_spec=pltpu.PrefetchScalarGridSpec(
            num_scalar_prefetch=1, grid=(R // tr,),
            in_specs=[pl.BlockSpec((1, tr, 1024), lambda i, mr: (mr[0], i, 0)),
                      pl.BlockSpec((N_DEV - 1, tr, 1024), lambda i, mr: (0, i, 0))],
            out_specs=pl.BlockSpec((tr, 1024), lambda i, mr: (i, 0))),
        out_shape=jax.ShapeDtypeStruct((R, 1024), F32),
        name=name,
    )(me, src, land)


_SMALL = ("lb_logits", "hg_norm_gain", "swa_sinks", "rel_bias", "ln1_g", "ln1_b", "ln2_g", "ln2_b")


def _pack_small_grads(d_lb, d_gain, d_sink, d_rb, d_ln1_g, d_ln1_b, d_ln2_g, d_ln2_b, loss):
    def body(lb_ref, gain_ref, sink_ref, rb_ref, l1g_ref, l1b_ref, l2g_ref, l2b_ref, loss_ref, o_ref):
        o_ref[...] = jnp.zeros_like(o_ref)
        for row, ref in ((SM_LB, lb_ref), (SM_GAIN, gain_ref), (SM_L1G, l1g_ref), (SM_L1B, l1b_ref),
                         (SM_L2G, l2g_ref), (SM_L2B, l2b_ref)):
            o_ref[row:row + 1, :] = ref[...]
        o_ref[SM_SINK:SM_SINK + 1, 0:128] = sink_ref[0:1, :]
        o_ref[SM_LOSS:SM_LOSS + 1, 0:128] = loss_ref[0:1, :]
        o_ref[SM_RB:SM_RB + NUM_BUCKETS, 0:128] = rb_ref[...]

    vm = pl.BlockSpec(memory_space=pltpu.VMEM)
    return pl.pallas_call(
        body,
        in_specs=[vm] * 9,
        out_specs=vm,
        out_shape=jax.ShapeDtypeStruct((SM_ROWS, D_MODEL), F32),
        name="pack_small_grads",
    )(d_lb, d_gain, d_sink, d_rb, d_ln1_g, d_ln1_b, d_ln2_g, d_ln2_b, loss)


def _small_finish(gathered, w, m, v):
    n = len(_SMALL)

    def body(*refs):
        g_ref = refs[0]
        w_refs, m_refs, v_refs = refs[1:1 + n], refs[1 + n:1 + 2 * n], refs[1 + 2 * n:1 + 3 * n]
        outs = refs[1 + 3 * n:]
        loss_ref, tot = outs[0], outs[-1]
        g_out, d_out, m_out, v_out = (outs[1 + k * n:1 + (k + 1) * n] for k in range(4))
        acc = g_ref[0]
        for d in range(1, N_DEV):
            acc = acc + g_ref[d]
        tot[...] = acc
        loss_ref[...] = tot[SM_LOSS:SM_LOSS + 1, 0:1]
        lb = _lower_bound(w_refs[0])
        dl0 = tot[SM_LB:SM_LB + 1, :] * lb * (1.0 - lb)
        grads = (jnp.concatenate([dl0, -dl0], axis=0), tot[SM_GAIN:SM_GAIN + 1, :],
                 tot[SM_SINK:SM_SINK + 1, 0:SWA_HEADS], tot[SM_RB:SM_RB + NUM_BUCKETS, 0:SWA_HEADS],
                 tot[SM_L1G:SM_L1G + 1, :], tot[SM_L1B:SM_L1B + 1, :], tot[SM_L2G:SM_L2G + 1, :], tot[SM_L2B:SM_L2B + 1, :])
        for k, g in enumerate(grads):
            g_out[k][...] = g
            d_out[k][...], m_out[k][...], v_out[k][...] = _adam_step(w_refs[k][...], g, m_refs[k][...], v_refs[k][...])

    vm = pl.BlockSpec(memory_space=pltpu.VMEM)
    shapes = [jax.ShapeDtypeStruct(w[k].shape, F32) for k in _SMALL]
    res = pl.pallas_call(
        body,
        in_specs=[vm] * (1 + 3 * n),
        out_specs=[vm] * (1 + 4 * n),
        out_shape=[jax.ShapeDtypeStruct((1, 1), F32)] + shapes * 4,
        scratch_shapes=[pltpu.VMEM((SM_ROWS, D_MODEL), F32)],
        name="small_finish",
    )(gathered, *[w[k] for k in _SMALL], *[m[k] for k in _SMALL], *[v[k] for k in _SMALL])
    parts = [dict(zip(_SMALL, res[1 + k * n:1 + (k + 1) * n])) for k in range(4)]
    return (res[0], *parts)


def _adam_step(w, g, m, v):
    nm = ADAM_B1 * m + (1.0 - ADAM_B1) * g
    nv = ADAM_B2 * v + (1.0 - ADAM_B2) * jnp.square(g)
    m_hat = nm / (1.0 - ADAM_B1 ** ADAM_STEP)
    v_hat = nv / (1.0 - ADAM_B2 ** ADAM_STEP)
    return -ADAM_LR * (m_hat / (jnp.sqrt(v_hat) + ADAM_EPS) + ADAM_WD * w), nm, nv


def _adamw(w, g, m, v, *, tr, name):
    R, C = w.shape

    def body(w_ref, g_ref, m_ref, v_ref, d_ref, nm_ref, nv_ref):
        d_ref[...], nm_ref[...], nv_ref[...] = _adam_step(w_ref[...], g_ref[...], m_ref[...], v_ref[...])

    spec = pl.BlockSpec((tr, C), lambda i: (i, 0))
    return pl.pallas_call(
        body,
        grid=(R // tr,),
        in_specs=[spec] * 4,
        out_specs=[spec] * 3,
        out_shape=[jax.ShapeDtypeStruct((R, C), F32)] * 3,
        compiler_params=_cparams("parallel"),
        name=name,
    )(w, g, m, v)


_WEIGHTS = ("w_in", "lb_logits", "hg_norm_gain", "swa_sinks", "rel_bias", "w_mem_kv", "w_branch_hg", "w_branch_swa",
            "w_branch_mem", "w_out", "ln1_g", "ln1_b", "w_up", "w_down", "ln2_g", "ln2_b")


def kernel(x, mem, w_in, lb_logits, hg_norm_gain, swa_sinks, rel_bias, w_mem_kv, w_branch_hg, w_branch_swa, w_branch_mem, w_out, ln1_g, ln1_b, w_up, w_down, ln2_g, ln2_b, loss_target, m_w_in, m_lb_logits, m_hg_norm_gain, m_swa_sinks, m_rel_bias, m_w_mem_kv, m_w_branch_hg, m_w_branch_swa, m_w_branch_mem, m_w_out, m_ln1_g, m_ln1_b, m_w_up, m_w_down, m_ln2_g, m_ln2_b, v_w_in, v_lb_logits, v_hg_norm_gain, v_swa_sinks, v_rel_bias, v_w_mem_kv, v_w_branch_hg, v_w_branch_swa, v_w_branch_mem, v_w_out, v_ln1_g, v_ln1_b, v_w_up, v_w_down, v_ln2_g, v_ln2_b):
    w = dict(w_in=w_in, lb_logits=lb_logits, hg_norm_gain=hg_norm_gain, swa_sinks=swa_sinks, rel_bias=rel_bias,
             w_mem_kv=w_mem_kv, w_branch_hg=w_branch_hg, w_branch_swa=w_branch_swa, w_branch_mem=w_branch_mem,
             w_out=w_out, ln1_g=ln1_g, ln1_b=ln1_b, w_up=w_up, w_down=w_down, ln2_g=ln2_g, ln2_b=ln2_b)
    mom = dict(w_in=m_w_in, lb_logits=m_lb_logits, hg_norm_gain=m_hg_norm_gain, swa_sinks=m_swa_sinks, rel_bias=m_rel_bias,
               w_mem_kv=m_w_mem_kv, w_branch_hg=m_w_branch_hg, w_branch_swa=m_w_branch_swa, w_branch_mem=m_w_branch_mem,
               w_out=m_w_out, ln1_g=m_ln1_g, ln1_b=m_ln1_b, w_up=m_w_up, w_down=m_w_down, ln2_g=m_ln2_g, ln2_b=m_ln2_b)
    var = dict(w_in=v_w_in, lb_logits=v_lb_logits, hg_norm_gain=v_hg_norm_gain, swa_sinks=v_swa_sinks, rel_bias=v_rel_bias,
               w_mem_kv=v_w_mem_kv, w_branch_hg=v_w_branch_hg, w_branch_swa=v_w_branch_swa, w_branch_mem=v_w_branch_mem,
               w_out=v_w_out, ln1_g=v_ln1_g, ln1_b=v_ln1_b, w_up=v_w_up, w_down=v_w_down, ln2_g=v_ln2_g, ln2_b=v_ln2_b)
    xc, yc, cc = _coords()

    p1 = _bf(w_in[0].T)
    p2 = _bf(jnp.concatenate([w_down[0], w_up[0].T, w_branch_hg[0], w_branch_swa[0], w_branch_mem[0], w_out[0],
                              w_mem_kv[0].T], axis=0))
    me = 4 * xc + 2 * yc + cc
    (g1,) = _all_gather_weights(p1)
    land2 = lax.dynamic_update_slice(lax.empty((N_DEV, R_OTHER, D_MODEL), BF16), p2[None], (me, 0, 0))
    ag2 = _direct_start(p2, land2, gather=True, name="gather_other_weights_start")

    def other_weights(after):
        return _direct_wait(*ag2[:4], after, gather=True, name="gather_other_weights_wait")[1]

    blocks = lambda a: a.reshape(N_DEV, a.shape[0] // N_DEV, D_MODEL)
    started = {}

    def send_other_grads(g):
        part = jnp.concatenate([blocks(g[k]) for k in ("wdn", "wup_t", "wbh", "wbs", "wbm", "wout", "wkv_t")], axis=1)
        started["others"] = _direct_start(part, lax.empty((N_DEV - 1, R_OTHER, D_MODEL), BF16), gather=False,
                                          name="scatter_other_grads_start")
        return started["others"][4]

    me1 = me.reshape(1).astype(jnp.int32)
    grads, delta, new_m, new_v = {}, {}, {}, {}

    def adamw(name):
        w2 = w[name][0]
        delta[name], new_m[name], new_v[name] = _adamw(
            w2, grads[name], mom[name][0], var[name][0], tr=w2.shape[0] // 4, name="adamw_" + name)

    def send_small_grads(packed):
        land = lax.dynamic_update_slice(lax.empty((N_DEV, SM_ROWS, D_MODEL), F32), packed[None], (me, 0, 0))
        started["small"] = _direct_start(packed, land, gather=True, name="gather_small_grads_start")
        return started["small"][4]

    def send_win_grad(g, after):
        started["win"] = _direct_start(blocks(g), lax.empty((N_DEV - 1, IN_SHARD, D_MODEL), BF16), gather=False,
                                       name="scatter_w_in_grad_start", after=after)
        mine2, landed2 = _direct_wait(*started["others"][:4], started["win"][4], gather=False,
                                      name="scatter_other_grads_wait")
        gs2 = _sum_partials(mine2, landed2, me1, tr=R_OTHER // 2, name="sum_other_grads")
        grads.update(
            w_down=gs2[R_DN:R_UP], w_up=gs2[R_UP:R_BH].T, w_branch_hg=gs2[R_BH:R_BS], w_branch_swa=gs2[R_BS:R_BM],
            w_branch_mem=gs2[R_BM:R_OUT], w_out=gs2[R_OUT:R_KV], w_mem_kv=gs2[R_KV:R_OTHER].T)
        for name in ("w_mem_kv", "w_branch_hg", "w_branch_swa", "w_branch_mem", "w_out", "w_up", "w_down"):
            adamw(name)
        return tuple(new_v[name] for name in new_v)

    grad_x = _local_step(
        x[0], mem[0], loss_target[0], lb_logits, hg_norm_gain, swa_sinks, rel_bias, ln1_g, ln1_b, ln2_g, ln2_b,
        g1.reshape(IN_COLS, D_MODEL), ag2[4], other_weights, send_other_grads, send_small_grads, send_win_grad)

    mine1, landed1 = _direct_wait(*started["win"][:4], grad_x, gather=False, name="scatter_w_in_grad_wait")
    g_win_t = _sum_partials(mine1, landed1, me1, tr=IN_SHARD // 2, name="sum_w_in_grad")
    d_t, m_t, v_t = _adamw(w_in[0].T, g_win_t, m_w_in[0].T, v_w_in[0].T, tr=IN_SHARD // 4, name="adamw_w_in")
    grads["w_in"], delta["w_in"], new_m["w_in"], new_v["w_in"] = g_win_t.T, d_t.T, m_t.T, v_t.T

    _, gathered = _direct_wait(*started["small"][:4], grad_x, gather=True, name="gather_small_grads_wait")
    loss, g_s, d_s, m_s, v_s = _small_finish(gathered, w, mom, var)
    for dst, src in ((grads, g_s), (delta, d_s), (new_m, m_s), (new_v, v_s)):
        dst.update(src)

    def shaped(d, name):
        return d[name].reshape(w[name].shape)

    return (loss.reshape(()), grad_x[None], *[shaped(grads, n) for n in _WEIGHTS], *[shaped(delta, n) for n in _WEIGHTS],
            *[shaped(new_m, n) for n in _WEIGHTS], *[shaped(new_v, n) for n in _WEIGHTS])
```

```python
import functools
import math

import jax
import jax.numpy as jnp
from jax import lax
from jax.experimental import pallas as pl
from jax.experimental.pallas import tpu as pltpu

F32 = jnp.float32
BF16 = jnp.bfloat16

D_MODEL = 1024
MEM_LEN = 256
HG_HEADS = 8
HG_DK = 128
HG_CHUNK = 64
SWA_HEADS = 16
SWA_HEAD_DIM = 64
SWA_BLOCK = 128
SWA_WINDOW = 128
MEM_HEADS = 4
MEM_HEAD_DIM = 256
NUM_BUCKETS = 32
MAX_DISTANCE = 128
D_FF = 4096
LN_EPS = 1e-5
RMS_EPS = 1e-6
ALPHA = 2.0 ** 0.25
N_DEV = 8

C_HQ, C_HF, C_HI, C_HG, C_SQ, C_SK, C_SV, C_MQ, C_GL = 0, 1024, 2048, 3072, 4096, 5120, 5248, 5376, 6400
IN_COLS = 9472
IN_SHARD = IN_COLS // N_DEV
Z_HG, Z_SQ, Z_SK, Z_MQ, Z_REST = 0, C_SQ - C_HG, C_SK - C_HG, C_MQ - C_HG, C_GL - C_HG

ADAM_LR = 0.001
ADAM_B1 = 0.9
ADAM_B2 = 0.999
ADAM_EPS = 1e-08
ADAM_WD = 0.01
ADAM_STEP = 10

VMEM_LIMIT = 58 * 1024 * 1024

R_DN, R_UP, R_BH, R_BS, R_BM, R_OUT, R_KV, R_OTHER = 0, 512, 1024, 1152, 1280, 1408, 1536, 1792

SM_LB, SM_GAIN, SM_SINK, SM_L1G, SM_L1B, SM_L2G, SM_L2B, SM_LOSS, SM_RB, SM_ROWS = 0, 2, 3, 4, 5, 6, 7, 8, 16, 48


def _bf(v):
    return v.astype(BF16)


def _f32(v):
    return v.astype(F32)


def _dot(a, b):
    return jnp.dot(a, b, preferred_element_type=F32)


def _dot_nt(a, b):
    return lax.dot_general(a, b, (((1,), (1,)), ((), ())), preferred_element_type=F32)


def _dot_tn(a, b):
    return lax.dot_general(a, b, (((0,), (0,)), ((), ())), preferred_element_type=F32)


def _sig(v):
    return 0.5 * jnp.tanh(0.5 * v) + 0.5


def _cparams(*sem):
    return pltpu.CompilerParams(dimension_semantics=sem, vmem_limit_bytes=VMEM_LIMIT)


def _const_spec(shape):
    nd = len(shape)
    return pl.BlockSpec(shape, lambda *_: (0,) * nd, pipeline_mode=pl.Buffered(1))


def _dep_spec():
    return pl.BlockSpec((8, 128), lambda *_: (0, 0))


def _in_proj(x, win_t, dep, *, tm):
    S = x.shape[0]

    def body(x_ref, w_ref, dep_ref, qfv_ref, z_ref, gl_ref, xb_ref):
        del dep_ref
        xb = _bf(x_ref[...])
        xb_ref[...] = xb
        for c0 in range(0, C_HG, 1024):
            qfv_ref[:, c0:c0 + 1024] = _dot_nt(xb, w_ref[c0:c0 + 1024, :])
        for c0 in range(0, Z_REST, Z_REST // 2):
            z_ref[:, c0:c0 + Z_REST // 2] = _bf(_dot_nt(xb, w_ref[C_HG + c0:C_HG + c0 + Z_REST // 2, :]))
        for c0 in range(0, IN_COLS - C_GL, 1024):
            gl_ref[:, c0:c0 + 1024] = _bf(_dot_nt(xb, w_ref[C_GL + c0:C_GL + c0 + 1024, :]))

    row = lambda w: pl.BlockSpec((tm, w), lambda i: (i, 0))
    return pl.pallas_call(
        body,
        grid=(S // tm,),
        in_specs=[row(D_MODEL), _const_spec(win_t.shape), _dep_spec()],
        out_specs=[row(C_HG), row(Z_REST), row(IN_COLS - C_GL), row(D_MODEL)],
        out_shape=[jax.ShapeDtypeStruct((S, C_HG), F32), jax.ShapeDtypeStruct((S, Z_REST), BF16),
                   jax.ShapeDtypeStruct((S, IN_COLS - C_GL), BF16), jax.ShapeDtypeStruct((S, D_MODEL), BF16)],
        compiler_params=_cparams("parallel"),
        name="in_proj",
    )(x, win_t, dep)


def _placement(into, tm, N, M, out_dtype):
    if into is None:
        return (lambda i: (i, 0)), (tm, N), jax.ShapeDtypeStruct((M, N), out_dtype), (), {}
    dest, block, index = into
    assert math.prod(block) == tm * N and dest.dtype == out_dtype
    return index, block, jax.ShapeDtypeStruct(dest.shape, dest.dtype), (dest,), {2: 0}


def _mm_tn_resident(a, b, *, tm, kc, name, out_dtype, into=None):
    K, M = a.shape
    N = b.shape[1]
    nk = K // kc
    index, block, out_shape, extra, aliases = _placement(into, tm, N, M, out_dtype)

    def body(a_ref, b_ref, *rest):
        o_ref = rest[-1]
        acc = jnp.zeros((tm, N), F32)
        for kk in range(nk):
            sl = pl.ds(kk * kc, kc)
            acc = acc + _dot_tn(_bf(a_ref[sl, :]), _bf(b_ref[sl, :]))
        o_ref[...] = acc.astype(o_ref.dtype).reshape(block)

    return pl.pallas_call(
        body,
        grid=(M // tm,),
        in_specs=[pl.BlockSpec((K, tm), lambda i: (0, i)), _const_spec((K, N))] + [ANY for _ in extra],
        out_specs=pl.BlockSpec(block, index),
        out_shape=out_shape,
        input_output_aliases=aliases,
        compiler_params=_cparams("parallel"),
        name=name,
    )(a, b, *extra)


def _mm_tn(a, b, *, kc, name, out_dtype=F32, into=None, tm=None):
    K, M = a.shape
    N = b.shape[1]
    if M > 1024 or tm is not None:
        return _mm_tn_resident(a, b, tm=tm or 256, kc=min(kc, 1024), name=name, out_dtype=out_dtype, into=into)
    tm = M
    nk = K // kc
    index, block, out_shape, extra, aliases = _placement(into, tm, N, M, out_dtype)

    def body(a_ref, b_ref, *rest):
        o_ref, acc = rest[-2], rest[-1]
        k = pl.program_id(1)
        part = _dot_tn(_bf(a_ref[...]), _bf(b_ref[...]))

        @pl.when(k == 0)
        def _():
            acc[...] = part

        @pl.when(k > 0)
        def _():
            acc[...] += part

        @pl.when(k == nk - 1)
        def _():
            o_ref[...] = acc[...].astype(o_ref.dtype).reshape(block)

    return pl.pallas_call(
        body,
        grid=(M // tm, nk),
        in_specs=[pl.BlockSpec((kc, tm), lambda i, k: (k, i)), pl.BlockSpec((kc, N), lambda i, k: (k, 0))]
        + [ANY for _ in extra],
        out_specs=pl.BlockSpec(block, lambda i, k: index(i)),
        out_shape=out_shape,
        input_output_aliases=aliases,
        scratch_shapes=[pltpu.VMEM((tm, N), F32)],
        compiler_params=_cparams("parallel", "arbitrary"),
        name=name,
    )(a, b, *extra)


def _grad_x(d_qfv, d_hg_gl, d_sq, d_skv, d_mq, w_qfv, win_t, add, deps, *, tm):
    M = add.shape[0]
    pieces = (d_qfv, d_hg_gl, d_sq, d_skv, d_mq)

    def body(qfv_ref, hggl_ref, sq_ref, skv_ref, mq_ref, wq_ref, w_ref, add_ref, *rest):
        o_ref = rest[-1]
        acc = add_ref[...] + _dot(qfv_ref[...], wq_ref[...])
        acc = acc + _dot(hggl_ref[:, 0:1024], w_ref[C_HG:C_SQ, :])
        acc = acc + _dot(hggl_ref[:, 1024:4096], w_ref[C_GL:IN_COLS, :])
        acc = acc + _dot(sq_ref[...], w_ref[C_SQ:C_SK, :])
        acc = acc + _dot(skv_ref[...], w_ref[C_SK:C_MQ, :])
        o_ref[...] = acc + _dot(mq_ref[...], w_ref[C_MQ:C_GL, :])

    return pl.pallas_call(
        body,
        grid=(M // tm,),
        in_specs=[pl.BlockSpec((tm, p.shape[1]), lambda i: (i, 0)) for p in pieces]
        + [_const_spec(w_qfv.shape), _const_spec(win_t.shape), pl.BlockSpec((tm, D_MODEL), lambda i: (i, 0))]
        + [_dep_spec() for _ in deps],
        out_specs=pl.BlockSpec((tm, D_MODEL), lambda i: (i, 0)),
        out_shape=jax.ShapeDtypeStruct((M, D_MODEL), F32),
        compiler_params=_cparams("parallel"),
        name="grad_x",
    )(*pieces, w_qfv, win_t, add, *deps)


def _lower_bound(lbl_ref):
    l0 = lbl_ref[0:1, :]
    l1 = lbl_ref[1:2, :]
    mx = jnp.maximum(l0, l1)
    e0 = jnp.exp(l0 - mx)
    e1 = jnp.exp(l1 - mx)
    return e0 / (e0 + e1)


def _tri(lower):
    r = lax.broadcasted_iota(jnp.int32, (HG_CHUNK, HG_CHUNK), 0)
    c = lax.broadcasted_iota(jnp.int32, (HG_CHUNK, HG_CHUNK), 1)
    return (r >= c) if lower else (r <= c)


def _hg_gates(fl, lb):
    sg = _sig(fl)
    f = lb + (1.0 - lb) * sg
    return sg, f, jnp.log(f), 1.0 - f


def _scan_rows(v, reverse=False):
    row = lax.broadcasted_iota(jnp.int32, v.shape, 0)
    s = 1
    while s < HG_CHUNK:
        if reverse:
            v = v + jnp.where(row < HG_CHUNK - s, pltpu.roll(v, HG_CHUNK - s, 0), 0.0)
        else:
            v = v + jnp.where(row >= s, pltpu.roll(v, s, 0), 0.0)
        s *= 2
    return v


def _hgrn_fwd(zmain, lb_logits, *, T):
    S = zmain.shape[0]
    nc = T // HG_CHUNK

    def body(q_ref, f_ref, v_ref, lbl_ref, o_ref, st_ref, state):
        @pl.when(pl.program_id(1) == 0)
        def _():
            state[...] = jnp.zeros_like(state)

        lb = _lower_bound(lbl_ref)
        tril = _tri(True)
        qis, updates, decays, intra = [], [], [], []
        for c in range(nc):
            sl = pl.ds(c * HG_CHUNK, HG_CHUNK)
            _, _, g, k = _hg_gates(_f32(f_ref[sl, :]), lb)
            b = _scan_rows(g)
            bl = jnp.sum(g, axis=0, keepdims=True)
            qi = _bf(_f32(q_ref[sl, :]) * jnp.exp(b))
            ki = _bf(k * jnp.exp(-b))
            ko = _bf(k * jnp.exp(bl - b))
            vb = _bf(v_ref[sl, :])
            att = jnp.where(tril, _dot_nt(qi, ki), 0.0)
            intra.append(_dot(_bf(att), vb))
            qis.append(qi)
            updates.append(_dot_tn(vb, ko))
            decays.append(jnp.exp(bl))
        st = state[...]
        for c in range(nc):
            st_ref[0, c] = st
            o_ref[pl.ds(c * HG_CHUNK, HG_CHUNK), :] = intra[c] + _dot_nt(qis[c], _bf(st))
            st = st * decays[c] + updates[c]
        state[...] = st

    col = lambda base: pl.BlockSpec((T, HG_DK), lambda h, t: (t, base + h))
    return pl.pallas_call(
        body,
        grid=(HG_HEADS, S // T),
        in_specs=[col(0), col(8), col(16), pl.BlockSpec((2, HG_DK), lambda h, t: (0, h))],
        out_specs=[
            pl.BlockSpec((T, HG_DK), lambda h, t: (t, h)),
            pl.BlockSpec((1, nc, HG_DK, HG_DK), lambda h, t: (h, t, 0, 0)),
        ],
        out_shape=[
            jax.ShapeDtypeStruct((S, D_MODEL), F32),
            jax.ShapeDtypeStruct((HG_HEADS, S // HG_CHUNK, HG_DK, HG_DK), F32),
        ],
        scratch_shapes=[pltpu.VMEM((HG_DK, HG_DK), F32)],
        compiler_params=_cparams("parallel", "arbitrary"),
        name="hgrn_fwd",
    )(zmain, zmain, zmain, lb_logits)


def _hgrn_bwd(zmain, lb_logits, states, d_o, *, T):
    S = zmain.shape[0]
    nc = T // HG_CHUNK
    nt = S // T

    def body(q_ref, f_ref, v_ref, lbl_ref, st_ref, do_ref, dz_ref, dlb_ref, dstate):
        @pl.when(pl.program_id(1) == 0)
        def _():
            dstate[...] = jnp.zeros_like(dstate)
            dlb_ref[...] = jnp.zeros_like(dlb_ref)

        lb = _lower_bound(lbl_ref)
        tril = _tri(True)
        last_row = lax.broadcasted_iota(jnp.int32, (HG_CHUNK, HG_DK), 0) == HG_CHUNK - 1
        saved = []
        for c in range(nc):
            sl = pl.ds(c * HG_CHUNK, HG_CHUNK)
            sg, f, g, k = _hg_gates(_f32(f_ref[sl, :]), lb)
            b = _scan_rows(g)
            bl = jnp.sum(g, axis=0, keepdims=True)
            eb = jnp.exp(b)
            enb = jnp.exp(-b)
            eo = jnp.exp(bl - b)
            q_in = _f32(q_ref[sl, :]) * eb
            k_in = k * enb
            k_out = k * eo
            qi, ki, ko = _bf(q_in), _bf(k_in), _bf(k_out)
            vb = _bf(v_ref[sl, :])
            dob = do_ref[sl, :]
            att = jnp.where(tril, _dot_nt(qi, ki), 0.0)
            d_att = _bf(jnp.where(tril, _dot_nt(dob, vb), 0.0))
            d_kin = _dot_tn(d_att, qi)
            saved.append(dict(
                sg=sg, f=f, eb=eb, enb=enb, eo=eo, ebl=jnp.exp(bl), k_out=k_out, ko=ko, vb=vb, dob=dob,
                d_v=_dot_tn(_bf(att), dob), d_qin=_dot(d_att, ki), d_kin=d_kin,
                qk=(q_in, k_in), d_state=_dot_tn(dob, qi)))
        dst = dstate[...]
        dsts = [None] * nc
        for c in reversed(range(nc)):
            dsts[c] = dst
            dst = dst * saved[c]["ebl"] + saved[c]["d_state"]
        dstate[...] = dst
        dlb = jnp.zeros((1, HG_DK), F32)
        for c in range(nc):
            sl = pl.ds(c * HG_CHUNK, HG_CHUNK)
            s = saved[c]
            q_in, k_in = s["qk"]
            st = st_ref[0, c]
            dstb = _bf(dsts[c])
            d_v = s["d_v"] + _dot_nt(s["ko"], dstb)
            d_qin = s["d_qin"] + _dot(s["dob"], _bf(st))
            d_kout = _dot(s["vb"], dstb)
            d_decay = jnp.sum(dsts[c] * st, axis=0, keepdims=True)
            kk = d_kout * s["k_out"]
            d_b = d_qin * q_in - s["d_kin"] * k_in - kk
            d_bl = jnp.sum(kk, axis=0, keepdims=True) + d_decay * s["ebl"]
            d_g = _scan_rows(d_b + jnp.where(last_row, d_bl, 0.0), reverse=True)
            d_f = d_g / s["f"] - (s["d_kin"] * s["enb"] + d_kout * s["eo"])
            dz_ref[sl, 0:HG_DK] = _bf(d_qin * s["eb"])
            dz_ref[sl, HG_DK:2 * HG_DK] = _bf(d_f * (1.0 - lb) * s["sg"] * (1.0 - s["sg"]))
            dz_ref[sl, 2 * HG_DK:3 * HG_DK] = _bf(d_v)
            dlb = dlb + jnp.sum(d_f * (1.0 - s["sg"]), axis=0, keepdims=True)
        dlb_ref[...] += dlb

    rev = lambda base: pl.BlockSpec((T, HG_DK), lambda h, t: (nt - 1 - t, base + h))
    outc = pl.BlockSpec((T, HG_DK), lambda h, t: (nt - 1 - t, h))
    return pl.pallas_call(
        body,
        grid=(HG_HEADS, nt),
        in_specs=[
            rev(0), rev(8), rev(16),
            pl.BlockSpec((2, HG_DK), lambda h, t: (0, h)),
            pl.BlockSpec((1, nc, HG_DK, HG_DK), lambda h, t: (h, nt - 1 - t, 0, 0)),
            outc,
        ],
        out_specs=[pl.BlockSpec((T, 3 * HG_DK), lambda h, t: (nt - 1 - t, h)),
                   pl.BlockSpec((1, HG_DK), lambda h, t: (0, h))],
        out_shape=[jax.ShapeDtypeStruct((S, 3 * D_MODEL), BF16), jax.ShapeDtypeStruct((1, D_MODEL), F32)],
        scratch_shapes=[pltpu.VMEM((HG_DK, HG_DK), F32)],
        compiler_params=_cparams("parallel", "arbitrary"),
        name="hgrn_bwd",
    )(zmain, zmain, zmain, lb_logits, states, d_o)


def _t5_bucket_table():
    qi = jnp.arange(SWA_BLOCK)[:, None] + SWA_BLOCK
    kj = jnp.arange(2 * SWA_BLOCK)[None, :]
    n = jnp.clip(qi - kj, 0, SWA_WINDOW - 1)
    max_exact = NUM_BUCKETS // 2
    nf = jnp.maximum(n, 1).astype(F32)
    large = max_exact + (jnp.log(nf / max_exact) / math.log(MAX_DISTANCE / max_exact)
                         * (NUM_BUCKETS - max_exact)).astype(jnp.int32)
    large = jnp.minimum(large, NUM_BUCKETS - 1)
    return jnp.where(n < max_exact, n, large).astype(jnp.int32)


SWA_ROWS = 32
MERGE_GROUPS = 2


def _swa_bias_init(bias, bucket_ref, rb_ref):
    bk = bucket_ref[...]
    qi = lax.broadcasted_iota(jnp.int32, bk.shape, 0) + SWA_BLOCK
    kj = lax.broadcasted_iota(jnp.int32, bk.shape, 1)
    band = (qi - kj >= 0) & (qi - kj < SWA_WINDOW)
    for h in range(SWA_HEADS):
        def sel(b, acc, h=h):
            return jnp.where(bk == b, rb_ref[b, h], acc)
        t = lax.fori_loop(0, NUM_BUCKETS, sel, jnp.zeros(bk.shape, F32))
        bias[1, h] = jnp.where(band, t, -jnp.inf)
        bias[0, h] = jnp.where(band & (kj >= SWA_BLOCK), t, -jnp.inf)


def _lane_halves(t, kv_head):
    lane = lax.broadcasted_iota(jnp.int32, t.shape, 1)
    rolled = pltpu.roll(t, 64, 1)
    zero = jnp.zeros_like(t)
    if kv_head == 0:
        return jnp.where(lane < 64, t, zero), jnp.where(lane >= 64, rolled, zero)
    return jnp.where(lane < 64, rolled, zero), jnp.where(lane >= 64, t, zero)


def _swa_zero_key0(t):
    return jnp.where(lax.broadcasted_iota(jnp.int32, t.shape, 0) == 0, jnp.zeros_like(t), t)


def _swa_probs(s, masked_bias, sink):
    s = s + masked_bias
    m = jnp.maximum(jnp.max(s, axis=-1, keepdims=True), sink)
    p = jnp.exp(s - m)
    es = jnp.exp(sink - m)
    inv = 1.0 / (jnp.sum(p, axis=-1, keepdims=True) + es)
    return p * inv, es * inv


def _swa_fwd(zmain, bucket, rel_bias, sinks):
    S = zmain.shape[0]
    nb = S // SWA_BLOCK
    scale = SWA_HEAD_DIM ** -0.5

    def body(q_ref, kvc_ref, kvp_ref, bucket_ref, rb_ref, sk_ref, o_ref, p_ref, bias):
        n = pl.program_id(0)

        @pl.when(n == 0)
        def _():
            _swa_bias_init(bias, bucket_ref, rb_ref)

        later = jnp.minimum(n, 1)
        kk = _bf(jnp.concatenate([kvp_ref[:, 0:128], kvc_ref[:, 0:128]], axis=0))
        vv = _swa_zero_key0(_bf(jnp.concatenate([kvp_ref[:, 128:256], kvc_ref[:, 128:256]], axis=0)))
        first_col = lax.broadcasted_iota(jnp.int32, (SWA_ROWS, 2 * SWA_BLOCK), 1) == 0
        scores, values = {}, {}
        for kvh in range(2):
            qst = _bf(jnp.concatenate([q_ref[:, pl.ds((kvh * 4 + jj) * 128, 128)] for jj in range(4)], axis=0) * scale)
            values[kvh] = _lane_halves(vv, kvh)
            for odd, kx in enumerate(_lane_halves(kk, kvh)):
                scores[kvh, odd] = _dot_nt(qst, kx)
        probs = {}
        for (kvh, odd), s in scores.items():
            parts = []
            for jj in range(4):
                h = 2 * (kvh * 4 + jj) + odd
                for r0 in range(0, SWA_BLOCK, SWA_ROWS):
                    p, ps = _swa_probs(s[jj * SWA_BLOCK + r0:jj * SWA_BLOCK + r0 + SWA_ROWS],
                                       bias[later, h, pl.ds(r0, SWA_ROWS), :], sk_ref[0, h])
                    part = _bf(jnp.where(first_col, ps, p))
                    p_ref[pl.ds(r0, SWA_ROWS), pl.ds(h * 2 * SWA_BLOCK, 2 * SWA_BLOCK)] = part
                    parts.append(part)
            probs[kvh, odd] = jnp.concatenate(parts, axis=0)
        for kvh in range(2):
            ost = _dot(probs[kvh, 0], values[kvh][0]) + _dot(probs[kvh, 1], values[kvh][1])
            for jj in range(4):
                o_ref[:, pl.ds((kvh * 4 + jj) * 128, 128)] = ost[jj * SWA_BLOCK:(jj + 1) * SWA_BLOCK]

    smem = pl.BlockSpec(memory_space=pltpu.SMEM)
    return pl.pallas_call(
        body,
        grid=(nb,),
        in_specs=[
            pl.BlockSpec((SWA_BLOCK, 1024), lambda n: (n, Z_SQ // 1024)),
            pl.BlockSpec((SWA_BLOCK, 256), lambda n: (n, Z_SK // 256)),
            pl.BlockSpec((SWA_BLOCK, 256), lambda n: (jnp.maximum(n - 1, 0), Z_SK // 256)),
            _const_spec((SWA_BLOCK, 2 * SWA_BLOCK)), smem, smem,
        ],
        out_specs=[pl.BlockSpec((SWA_BLOCK, 1024), lambda n: (n, 0)),
                   pl.BlockSpec((SWA_BLOCK, SWA_HEADS * 2 * SWA_BLOCK), lambda n: (n, 0))],
        out_shape=[jax.ShapeDtypeStruct((S, 1024), F32),
                   jax.ShapeDtypeStruct((S, SWA_HEADS * 2 * SWA_BLOCK), BF16)],
        scratch_shapes=[pltpu.VMEM((2, SWA_HEADS, SWA_BLOCK, 2 * SWA_BLOCK), F32)],
        compiler_params=_cparams("arbitrary"),
        name="swa_fwd",
    )(zmain, zmain, zmain, bucket, rel_bias, sinks)


def _swa_bwd(zmain, o_b, probs, d_o, bucket, dep):
    S = zmain.shape[0]
    nb = S // SWA_BLOCK
    scale = SWA_HEAD_DIM ** -0.5

    def body(q_ref, kvc_ref, kvp_ref, o_ref, p_ref, do_ref, bucket_ref, dep_ref,
             dq_ref, dkv_ref, drb_ref, dsk_ref, dbias, carry):
        del dep_ref
        n = pl.program_id(0)

        @pl.when(n == 0)
        def _():
            dbias[...] = jnp.zeros_like(dbias)
            carry[...] = jnp.zeros_like(carry)

        @pl.when(n < nb)
        def _():
            kk = _swa_zero_key0(_bf(jnp.concatenate([kvp_ref[:, 0:128], kvc_ref[:, 0:128]], axis=0)))
            vv = _swa_zero_key0(_bf(jnp.concatenate([kvp_ref[:, 128:256], kvc_ref[:, 128:256]], axis=0)))
            lane = lax.broadcasted_iota(jnp.int32, (2 * SWA_BLOCK, 128), 1)
            lane_q = lax.broadcasted_iota(jnp.int32, (4 * SWA_BLOCK, 128), 1)
            pair_cols = {kvh: [pl.ds((kvh * 4 + jj) * 128, 128) for jj in range(4)] for kvh in range(2)}
            qst, dost, ks, d_p, delta = {}, {}, {}, {}, {}
            for kvh in range(2):
                qst[kvh] = _bf(jnp.concatenate([q_ref[:, cl] for cl in pair_cols[kvh]], axis=0) * scale)
                dost[kvh] = jnp.concatenate([do_ref[:, cl] for cl in pair_cols[kvh]], axis=0)
                prod = dost[kvh].astype(F32) * jnp.concatenate([o_ref[:, cl] for cl in pair_cols[kvh]], axis=0)
                ks[kvh] = _lane_halves(kk, kvh)
                for odd, vx in enumerate(_lane_halves(vv, kvh)):
                    keep = (lane_q >= 64) if odd else (lane_q < 64)
                    delta[kvh, odd] = jnp.sum(jnp.where(keep, prod, 0.0), axis=-1, keepdims=True)
                    d_p[kvh, odd] = _dot_nt(dost[kvh], vx)
            pst, dsst = {}, {}
            for (kvh, odd), dp in d_p.items():
                p_parts, ds_parts = [], []
                for jj in range(4):
                    h = 2 * (kvh * 4 + jj) + odd
                    rows = slice(jj * SWA_BLOCK, (jj + 1) * SWA_BLOCK)
                    p = p_ref[:, pl.ds(h * 2 * SWA_BLOCK, 2 * SWA_BLOCK)]
                    ds = _f32(p) * (dp[rows] - delta[kvh, odd][rows])
                    dbias[h] += ds
                    p_parts.append(p)
                    ds_parts.append(_bf(ds))
                pst[kvh, odd] = jnp.concatenate(p_parts, axis=0)
                dsst[kvh, odd] = jnp.concatenate(ds_parts, axis=0)
            dk_parts, dv_parts = [], []
            for kvh in range(2):
                dq_st = _dot(dsst[kvh, 0], ks[kvh][0]) + _dot(dsst[kvh, 1], ks[kvh][1])
                for jj in range(4):
                    dq_ref[:, pair_cols[kvh][jj]] = _bf(dq_st[jj * SWA_BLOCK:(jj + 1) * SWA_BLOCK] * scale)
                zk = jnp.where(lane < 64, _dot_tn(dsst[kvh, 0], qst[kvh]), _dot_tn(dsst[kvh, 1], qst[kvh]))
                zv = jnp.where(lane < 64, _dot_tn(pst[kvh, 0], dost[kvh]), _dot_tn(pst[kvh, 1], dost[kvh]))
                dk_parts.append(zk + pltpu.roll(zk, 64, 1))
                dv_parts.append(zv + pltpu.roll(zv, 64, 1))
            dk = jnp.where(lane < 64, dk_parts[0], dk_parts[1])
            dv = jnp.where(lane < 64, dv_parts[0], dv_parts[1])
            dkv = _swa_zero_key0(jnp.concatenate([dk, dv], axis=1))
            dkv_ref[...] = _bf(carry[...] + dkv[0:SWA_BLOCK])
            carry[...] = dkv[SWA_BLOCK:]

        @pl.when(n == nb)
        def _():
            dkv_ref[...] = _bf(carry[...])
            first_col = lax.broadcasted_iota(jnp.int32, (SWA_BLOCK, 2 * SWA_BLOCK), 1) == 0
            bk = jnp.where(first_col, -1, bucket_ref[...])

            row = lax.broadcasted_iota(jnp.int32, (NUM_BUCKETS, 128), 0)
            lane = lax.broadcasted_iota(jnp.int32, (NUM_BUCKETS, 128), 1)

            def total(v):
                return jnp.sum(jnp.sum(v, axis=1, keepdims=True), axis=0, keepdims=True)

            def per_head(h, acc):
                db = dbias[h]
                d_rb, d_sk = acc
                d_sk = d_sk + jnp.where((row == 0) & (lane == h), total(jnp.where(first_col, db, 0.0)), 0.0)

                def per_bucket(b, d_rb):
                    return d_rb + jnp.where((row == b) & (lane == h), total(jnp.where(bk == b, db, 0.0)), 0.0)

                return lax.fori_loop(0, NUM_BUCKETS, per_bucket, d_rb), d_sk

            zero = jnp.zeros((NUM_BUCKETS, 128), F32)
            d_rb, d_sk = lax.fori_loop(0, SWA_HEADS, per_head, (zero, zero))
            drb_ref[...] = d_rb
            dsk_ref[...] = d_sk[0:8]

    cur = lambda n: jnp.minimum(n, nb - 1)
    prev = lambda n: jnp.maximum(jnp.minimum(n, nb - 1) - 1, 0)
    return pl.pallas_call(
        body,
        grid=(nb + 1,),
        in_specs=[
            pl.BlockSpec((SWA_BLOCK, 1024), lambda n: (cur(n), Z_SQ // 1024)),
            pl.BlockSpec((SWA_BLOCK, 256), lambda n: (cur(n), Z_SK // 256)),
            pl.BlockSpec((SWA_BLOCK, 256), lambda n: (prev(n), Z_SK // 256)),
            pl.BlockSpec((SWA_BLOCK, 1024), lambda n: (cur(n), 0)),
            pl.BlockSpec((SWA_BLOCK, SWA_HEADS * 2 * SWA_BLOCK), lambda n: (cur(n), 0)),
            pl.BlockSpec((SWA_BLOCK, 1024), lambda n: (cur(n), 0)),
            _const_spec((SWA_BLOCK, 2 * SWA_BLOCK)), _dep_spec(),
        ],
        out_specs=[
            pl.BlockSpec((SWA_BLOCK, 1024), lambda n: (cur(n), 0)),
            pl.BlockSpec((SWA_BLOCK, 256), lambda n: (jnp.maximum(n - 1, 0), 0)),
            pl.BlockSpec((NUM_BUCKETS, 128), lambda n: (0, 0)),
            pl.BlockSpec((8, 128), lambda n: (0, 0)),
        ],
        out_shape=[
            jax.ShapeDtypeStruct((S, 1024), BF16),
            jax.ShapeDtypeStruct((S, 256), BF16),
            jax.ShapeDtypeStruct((NUM_BUCKETS, 128), F32),
            jax.ShapeDtypeStruct((8, 128), F32),
        ],
        scratch_shapes=[
            pltpu.VMEM((SWA_HEADS, SWA_BLOCK, 2 * SWA_BLOCK), F32),
            pltpu.VMEM((SWA_BLOCK, 256), F32),
        ],
        compiler_params=_cparams("arbitrary"),
        name="swa_bwd",
    )(zmain, zmain, zmain, o_b, probs, d_o, bucket, dep)


def _mem_q_specs(T):
    return [pl.BlockSpec((T, MEM_HEAD_DIM), lambda t, h=h: (t, Z_MQ // MEM_HEAD_DIM + h)) for h in range(MEM_HEADS)]


def _mem_kv_proj(mem, g2):
    def body(mem_ref, w_ref, o_ref):
        o_ref[...] = _dot_nt(_bf(mem_ref[...]), _rows(w_ref))

    return pl.pallas_call(
        body,
        grid=(1,),
        in_specs=[pl.BlockSpec((MEM_LEN, D_MODEL), lambda i: (0, 0)), _gathered_spec(R_KV, R_OTHER)],
        out_specs=pl.BlockSpec((MEM_LEN, 2048), lambda i: (0, 0)),
        out_shape=jax.ShapeDtypeStruct((MEM_LEN, 2048), F32),
        compiler_params=_cparams("arbitrary"),
        name="mem_kv_proj",
    )(mem, g2)


def _mem_fwd(zmain, mkv, *, T):
    S = zmain.shape[0]

    def body(q0, q1, q2, q3, kv_ref, o_ref, p_ref):
        heads = [pl.ds(h * MEM_HEAD_DIM, MEM_HEAD_DIM) for h in range(MEM_HEADS)]
        scores = [_dot_nt(_bf(q_ref[...] * (MEM_HEAD_DIM ** -0.5)), _bf(kv_ref[:, cols]))
                  for q_ref, cols in zip((q0, q1, q2, q3), heads)]
        probs = []
        for s, cols in zip(scores, heads):
            e = jnp.exp(s - jnp.max(s, axis=-1, keepdims=True))
            pb = _bf(e * (1.0 / jnp.sum(e, axis=-1, keepdims=True)))
            p_ref[:, cols] = pb
            probs.append(pb)
        for h, (pb, cols) in enumerate(zip(probs, heads)):
            o_ref[:, cols] = _dot(pb, _bf(kv_ref[:, pl.ds(1024 + h * MEM_HEAD_DIM, MEM_HEAD_DIM)]))

    row = pl.BlockSpec((T, 1024), lambda t: (t, 0))
    return pl.pallas_call(
        body,
        grid=(S // T,),
        in_specs=_mem_q_specs(T) + [_const_spec((MEM_LEN, 2048))],
        out_specs=[row, row],
        out_shape=[jax.ShapeDtypeStruct((S, 1024), F32), jax.ShapeDtypeStruct((S, 1024), BF16)],
        compiler_params=_cparams("parallel"),
        name="mem_fwd",
    )(zmain, zmain, zmain, zmain, mkv)


def _mem_bwd(zmain, mkv, o_c, probs, d_o, *, T):
    S = zmain.shape[0]
    scale = MEM_HEAD_DIM ** -0.5

    def body(q0, q1, q2, q3, kv_ref, o_ref, p_ref, do_ref, dq_ref, dkv_ref):
        @pl.when(pl.program_id(0) == 0)
        def _():
            dkv_ref[...] = jnp.zeros_like(dkv_ref)

        heads = [(pl.ds(h * MEM_HEAD_DIM, MEM_HEAD_DIM), pl.ds(1024 + h * MEM_HEAD_DIM, MEM_HEAD_DIM))
                 for h in range(MEM_HEADS)]
        d_p = [_dot_nt(do_ref[:, cols], _bf(kv_ref[:, vcols])) for cols, vcols in heads]
        d_s = []
        for dp, (cols, _) in zip(d_p, heads):
            delta = jnp.sum(do_ref[:, cols].astype(F32) * o_ref[:, cols], axis=-1, keepdims=True)
            d_s.append(_bf(_f32(p_ref[:, cols]) * (dp - delta)))
        for ds, q_ref, (cols, vcols) in zip(d_s, (q0, q1, q2, q3), heads):
            dq_ref[:, cols] = _bf(_dot(ds, _bf(kv_ref[:, cols])) * scale)
            dkv_ref[:, cols] += _dot_tn(ds, _bf(q_ref[...] * scale))
            dkv_ref[:, vcols] += _dot_tn(p_ref[:, cols], do_ref[:, cols])

    row = pl.BlockSpec((T, 1024), lambda t: (t, 0))
    return pl.pallas_call(
        body,
        grid=(S // T,),
        in_specs=_mem_q_specs(T) + [_const_spec((MEM_LEN, 2048)), row, row, row],
        out_specs=[row, pl.BlockSpec((MEM_LEN, 2048), lambda t: (0, 0))],
        out_shape=[jax.ShapeDtypeStruct((S, 1024), BF16), jax.ShapeDtypeStruct((MEM_LEN, 2048), F32)],
        compiler_params=_cparams("arbitrary"),
        name="mem_bwd",
    )(zmain, zmain, zmain, zmain, mkv, o_c, probs, d_o)


def _layer_norm(u):
    mu = jnp.mean(u, axis=-1, keepdims=True)
    xc = u - mu
    rstd = lax.rsqrt(jnp.mean(xc * xc, axis=-1, keepdims=True) + LN_EPS)
    return xc * rstd, rstd


def _layer_norm_bwd(dy, gamma, xhat, rstd):
    dxh = dy * gamma
    return rstd * (dxh - jnp.mean(dxh, axis=-1, keepdims=True) - xhat * jnp.mean(dxh * xhat, axis=-1, keepdims=True))


def _merge_stages(rows, oraw_ref, hg_ref, ob_ref, oc_ref, gl_ref, x_ref, gain_ref, wbh, wbs, wbm, wout, g_ref, b_ref,
                  fwd_out=None, bwd=None):
    ys, rs = [], []
    for h in range(HG_HEADS):
        oh = oraw_ref[rows, pl.ds(h * HG_DK, HG_DK)]
        r = lax.rsqrt(jnp.mean(oh * oh, axis=-1, keepdims=True) + RMS_EPS)
        ys.append(oh * r)
        rs.append(r)
    y = jnp.concatenate(ys, axis=1)
    hg = _f32(hg_ref[rows, :])
    sg = _sig(hg)
    silu = hg * sg
    gain = gain_ref[...]
    oa = _bf(y * gain * silu)
    pa = _dot(oa, _rows(wbh))
    pb = _dot(_bf(ob_ref[rows, :]), _rows(wbs))
    pc = _dot(_bf(oc_ref[rows, :]), _rows(wbm))
    yield
    gates = [_sig(_f32(gl_ref[rows, pl.ds(i * 1024, 1024)])) for i in range(3)]
    m = _bf(gates[0] * pa + gates[1] * pb + gates[2] * pc)
    mix = _dot(m, _rows(wout))
    yield
    xhat, rstd = _layer_norm(ALPHA * x_ref[rows, :] + mix)
    if bwd is None:
        h1 = xhat * g_ref[...] + b_ref[...]
        fwd_out[0][rows, :] = h1
        fwd_out[1][rows, :] = _bf(h1)
        return
    (dh1_ref, dx_ref, du1_ref, m_ref, oa_ref, dpa_ref, dpb_ref, dpc_ref, doraw_ref, dob_ref, doc_ref, dz_ref,
     dgain_ref, dg_ref, db_ref) = bwd
    dh1 = dh1_ref[rows, :]
    dg_ref[...] += jnp.sum(dh1 * xhat, axis=0, keepdims=True)
    db_ref[...] += jnp.sum(dh1, axis=0, keepdims=True)
    du1 = _layer_norm_bwd(dh1, g_ref[...], xhat, rstd)
    dx_ref[rows, :] = ALPHA * du1
    du1b = _bf(du1)
    du1_ref[rows, :] = du1b
    m_ref[rows, :] = m
    oa_ref[rows, :] = oa
    dm = _dot_nt(du1b, _rows(wout))
    yield
    d_branches = []
    for i, (g, p, dp_ref, w_r) in enumerate(zip(gates, (pa, pb, pc), (dpa_ref, dpb_ref, dpc_ref), (wbh, wbs, wbm))):
        dz_ref[rows, pl.ds((i + 1) * 1024, 1024)] = _bf(dm * p * g * (1.0 - g))
        dp = _bf(dm * g)
        dp_ref[rows, :] = dp
        d_branches.append(_dot_nt(dp, _rows(w_r)))
    yield
    doa, d_ob, d_oc = d_branches
    dob_ref[rows, :] = _bf(d_ob)
    doc_ref[rows, :] = _bf(d_oc)
    t = doa * y
    dgain_ref[...] += jnp.sum(t * silu, axis=0, keepdims=True)
    dz_ref[rows, 0:1024] = _bf(t * gain * sg * (1.0 + hg * (1.0 - sg)))
    dy = doa * gain * silu
    for h in range(HG_HEADS):
        cols = slice(h * HG_DK, (h + 1) * HG_DK)
        yh = y[:, cols]
        dyh = dy[:, cols]
        doraw_ref[rows, pl.ds(h * HG_DK, HG_DK)] = _bf(rs[h] * (dyh - yh * jnp.mean(dyh * yh, axis=-1, keepdims=True)))


def _interleave(chains):
    live = list(chains)
    while live:
        still = []
        for c in live:
            try:
                next(c)
                still.append(c)
            except StopIteration:
                pass
        live = still


def _gathered_spec(lo, hi):
    n = hi - lo
    return pl.BlockSpec((N_DEV, n, D_MODEL), lambda *_: (0, lo // n, 0), pipeline_mode=pl.Buffered(1))


def _rows(w_ref):
    return w_ref[...].reshape(-1, D_MODEL)


def _merge_in_specs(T):
    row = lambda w, c=0: pl.BlockSpec((T, w), lambda i: (i, c))
    vec = pl.BlockSpec((1, D_MODEL), lambda i: (0, 0))
    w = [_gathered_spec(lo, hi) for lo, hi in ((R_BH, R_BS), (R_BS, R_BM), (R_BM, R_OUT), (R_OUT, R_KV))]
    return [row(1024), row(1024, Z_HG // 1024), row(1024), row(1024), row(3072), row(1024), vec, *w, vec, vec]


def _merge_fwd(o_raw, zmain, o_b, o_c, gl, x, gain, wbh, wbs, wbm, wout, ln_g, ln_b, *, T):
    S = x.shape[0]

    def body(*refs):
        ins, outs = refs[:13], refs[13:]
        _interleave(_merge_stages(pl.ds(r0, T // MERGE_GROUPS), *ins, fwd_out=outs)
                    for r0 in range(0, T, T // MERGE_GROUPS))

    row = pl.BlockSpec((T, D_MODEL), lambda i: (i, 0))
    return pl.pallas_call(
        body,
        grid=(S // T,),
        in_specs=_merge_in_specs(T),
        out_specs=[row, row],
        out_shape=[jax.ShapeDtypeStruct((S, D_MODEL), F32), jax.ShapeDtypeStruct((S, D_MODEL), BF16)],
        compiler_params=_cparams("parallel"),
        name="merge_fwd",
    )(o_raw, zmain, o_b, o_c, gl, x, gain, wbh, wbs, wbm, wout, ln_g, ln_b)


def _merge_bwd(d_h1, o_raw, zmain, o_b, o_c, gl, x, gain, wbh, wbs, wbm, wout, ln_g, ln_b, *, T):
    S = x.shape[0]

    def body(dh1_ref, oraw_ref, hg_ref, ob_ref, oc_ref, gl_ref, x_ref, gain_ref, wbh_r, wbs_r, wbm_r, wout_r, g_ref, b_ref,
             dx_ref, du1_ref, m_ref, oa_ref, dpa_ref, dpb_ref, dpc_ref, doraw_ref, dob_ref, doc_ref, dz_ref,
             dgain_ref, dg_ref, db_ref):
        del b_ref

        @pl.when(pl.program_id(0) == 0)
        def _():
            dgain_ref[...] = jnp.zeros_like(dgain_ref)
            dg_ref[...] = jnp.zeros_like(dg_ref)
            db_ref[...] = jnp.zeros_like(db_ref)

        ins = (oraw_ref, hg_ref, ob_ref, oc_ref, gl_ref, x_ref, gain_ref, wbh_r, wbs_r, wbm_r, wout_r, g_ref, None)
        bwd = (dh1_ref, dx_ref, du1_ref, m_ref, oa_ref, dpa_ref, dpb_ref, dpc_ref, doraw_ref, dob_ref, doc_ref, dz_ref,
               dgain_ref, dg_ref, db_ref)
        _interleave([_merge_stages(pl.ds(0, T), *ins, bwd=bwd)])

    row = lambda w: pl.BlockSpec((T, w), lambda i: (i, 0))
    vec = pl.BlockSpec((1, D_MODEL), lambda i: (0, 0))
    bshape = jax.ShapeDtypeStruct((S, D_MODEL), BF16)
    vshape = jax.ShapeDtypeStruct((1, D_MODEL), F32)
    return pl.pallas_call(
        body,
        grid=(S // T,),
        in_specs=[row(1024)] + _merge_in_specs(T),
        out_specs=[row(1024)] * 10 + [row(4096), vec, vec, vec],
        out_shape=[jax.ShapeDtypeStruct((S, D_MODEL), F32)] + [bshape] * 9
        + [jax.ShapeDtypeStruct((S, 4096), BF16), vshape, vshape, vshape],
        compiler_params=_cparams("arbitrary"),
        name="merge_bwd",
    )(d_h1, o_raw, zmain, o_b, o_c, gl, x, gain, wbh, wbs, wbm, wout, ln_g, ln_b)


def _mlp_fwd_bwd(h1, target, wup_t, wdn, ln_g, ln_b, *, T, FC):
    S = h1.shape[0]
    nf = D_FF // FC
    assert FC == R_BH - R_UP == R_UP - R_DN

    def body(h1_ref, t_ref, wup_ref, wdn_ref, g_ref, b_ref, dh1_ref, a_ref, dup_ref, du2_ref, loss_ref, dg_ref, db_ref, up_scr):
        @pl.when(pl.program_id(0) == 0)
        def _():
            loss_ref[...] = jnp.zeros_like(loss_ref)
            dg_ref[...] = jnp.zeros_like(dg_ref)
            db_ref[...] = jnp.zeros_like(db_ref)

        h1v = h1_ref[...]
        h1b = _bf(h1v)
        ff = jnp.zeros((T, D_MODEL), F32)
        for j in range(nf):
            rows = pl.ds(j * FC, FC)
            up = jnp.maximum(_dot_nt(h1b, wup_ref[j]), 0.0)
            up_scr[:, rows] = _bf(up)
            a = _bf(up * up)
            a_ref[:, rows] = a
            ff = ff + _dot(a, wdn_ref[j])
        xhat, rstd = _layer_norm(ALPHA * h1v + ff)
        gamma = g_ref[...]
        err = xhat * gamma + b_ref[...] - t_ref[...]
        loss_ref[...] += jnp.sum(jnp.sum(err * err, axis=-1, keepdims=True), axis=0, keepdims=True) * (0.5 / D_MODEL)
        dy = err * (1.0 / D_MODEL)
        dg_ref[...] += jnp.sum(dy * xhat, axis=0, keepdims=True)
        db_ref[...] += jnp.sum(dy, axis=0, keepdims=True)
        du2 = _layer_norm_bwd(dy, gamma, xhat, rstd)
        du2b = _bf(du2)
        du2_ref[...] = du2b
        dh1 = ALPHA * du2
        for j in range(nf):
            rows = pl.ds(j * FC, FC)
            dup = _bf(_dot_nt(du2b, wdn_ref[j]) * (2.0 * up_scr[:, rows].astype(F32)))
            dup_ref[:, rows] = dup
            dh1 = dh1 + _dot(dup, wup_ref[j])
        dh1_ref[...] = dh1

    row = lambda w: pl.BlockSpec((T, w), lambda i: (i, 0))
    vec = pl.BlockSpec((1, D_MODEL), lambda i: (0, 0))
    vshape = jax.ShapeDtypeStruct((1, D_MODEL), F32)
    return pl.pallas_call(
        body,
        grid=(S // T,),
        in_specs=[row(1024), row(1024), _gathered_spec(R_UP, R_BH), _gathered_spec(R_DN, R_UP), vec, vec],
        out_specs=[row(1024), row(D_FF), row(D_FF), row(1024), pl.BlockSpec((8, 128), lambda i: (0, 0)), vec, vec],
        out_shape=[
            jax.ShapeDtypeStruct((S, D_MODEL), F32),
            jax.ShapeDtypeStruct((S, D_FF), BF16),
            jax.ShapeDtypeStruct((S, D_FF), BF16),
            jax.ShapeDtypeStruct((S, D_MODEL), BF16),
            jax.ShapeDtypeStruct((8, 128), F32), vshape, vshape,
        ],
        scratch_shapes=[pltpu.VMEM((T, D_FF), BF16)],
        compiler_params=_cparams("arbitrary"),
        name="mlp_fwd_bwd",
    )(h1, target, wup_t, wdn, ln_g, ln_b)


def _local_step(x, mem, target, lb_logits, gain, sinks, rel_bias, ln1_g, ln1_b, ln2_g, ln2_b,
                win_t, dep0, other_weights, send_other_grads, send_small_grads, send_win_grad):
    S = x.shape[0]
    T = min(256, S)
    KC = min(2048, S)
    z_qfv, zmain, gl, xb = _in_proj(x, win_t, dep0, tm=min(512, S))
    bucket = _t5_bucket_table()

    o_raw, states = _hgrn_fwd(z_qfv, lb_logits, T=min(2048, S))
    o_b, swa_probs = _swa_fwd(zmain, bucket, rel_bias, sinks)
    g2 = other_weights((o_b, o_raw))
    mkv = _mem_kv_proj(mem, g2)
    o_c, mem_probs = _mem_fwd(zmain, mkv, T=min(1024, S))
    merge_args = (o_raw, zmain, o_b, o_c, gl, x, gain, g2, g2, g2, g2, ln1_g, ln1_b)
    h1, h1b = _merge_fwd(*merge_args, T=min(512, S))

    d_h1, act, d_up, du2, loss, d_ln2_g, d_ln2_b = _mlp_fwd_bwd(h1, target, g2, g2, ln2_g, ln2_b, T=min(512, S), FC=512)
    wgrad = functools.partial(_mm_tn, out_dtype=BF16)
    halves = lambda r0: (lambda i: (i // 2, r0 // 256 + i % 2, 0))
    whole = lambda r0: (lambda i: (0, r0 // 128, 0))
    og = lax.empty((N_DEV, R_OTHER, D_MODEL), BF16)
    og = wgrad(act, du2, kc=KC, name="grad_w_down", into=(og, (1, 256, D_MODEL), halves(R_DN)))
    og = wgrad(d_up, h1b, kc=KC, name="grad_w_up", into=(og, (1, 256, D_MODEL), halves(R_UP)))

    (dx_part, du1, m, oa, dpa, dpb, dpc, d_oraw, d_ob, d_oc, d_hg_gl,
     d_gain, d_ln1_g, d_ln1_b) = _merge_bwd(d_h1, *merge_args, T=T)
    for a_op, b_op, r0, nm in ((m, du1, R_OUT, "out"), (oa, dpa, R_BH, "branch_hg"), (o_b, dpb, R_BS, "branch_swa"),
                               (o_c, dpc, R_BM, "branch_mem")):
        og = wgrad(a_op, b_op, kc=KC, name="grad_w_" + nm, into=(og, (N_DEV, 128, D_MODEL), whole(r0)))

    d_mq, d_mkv = _mem_bwd(zmain, mkv, o_c, mem_probs, d_oc, T=min(1024, S))
    og = wgrad(d_mkv, mem, kc=MEM_LEN, name="grad_w_mem_kv",
               into=(og, (1, 256, D_MODEL), lambda i: (i, R_KV // 256, 0)))
    sent_others = send_other_grads(og)
    d_sq, d_skv, d_rb, d_sink = _swa_bwd(zmain, o_b, swa_probs, d_ob, bucket, sent_others)
    d_qfv, d_lb = _hgrn_bwd(z_qfv, lb_logits, states, d_oraw, T=min(2048, S))
    sent_small = send_small_grads(_pack_small_grads(d_lb, d_gain, d_sink, d_rb, d_ln1_g, d_ln1_b, d_ln2_g, d_ln2_b, loss))

    head_major = lambda a: a.reshape(3, HG_HEADS, HG_DK, D_MODEL).transpose(1, 0, 2, 3).reshape(3 * D_MODEL, D_MODEL)
    pieces = (d_qfv, d_hg_gl, d_sq, d_skv, d_mq)
    placed = (
        ("qfv", d_qfv, 128, lambda i: ((i % 3) * HG_HEADS + i // 3, 0)),
        ("hg_gates", d_hg_gl, 256, lambda i: (jnp.where(i < 4, C_HG // 256 + i, C_GL // 256 + i - 4), 0)),
        ("swa_q", d_sq, None, lambda i: (C_SQ // 1024, 0)),
        ("swa_kv", d_skv, None, lambda i: (C_SK // 256, 0)),
        ("mem_q", d_mq, 256, lambda i: (C_MQ // 256 + i, 0)),
    )
    g_win_t = lax.empty((IN_COLS, D_MODEL), BF16)
    for nm, piece, tile, index in placed:
        g_win_t = wgrad(piece, xb, kc=KC, name="grad_w_in_" + nm, tm=tile,
                        into=(g_win_t, (tile or piece.shape[1], D_MODEL), index))
    sent_win = send_win_grad(g_win_t, sent_small)
    return _grad_x(*pieces, head_major(win_t[:C_HG]), win_t, dx_part, sent_win, tm=T)


MESH = pl.DeviceIdType.MESH
ANY = pl.BlockSpec(memory_space=pl.ANY)


def _coords():
    return lax.axis_index("x"), lax.axis_index("y"), lax.axis_index("c")


def _other_chips(x, y):
    return [(1 - x, y), (x, 1 - y), (1 - x, 1 - y)]


def _all_gather_weights(*arrays):
    na = len(arrays)

    def body(*refs):
        srcs, dsts = refs[:na], refs[na:2 * na]
        send_sems, recv_sems, local_sems = refs[2 * na:]
        x, y, c = _coords()
        me, sibling = (x, y, c), (x, y, 1 - c)
        chips = _other_chips(x, y)

        def slot(a, px, py, pc):
            return dsts[a].at[4 * px + 2 * py + pc]

        def copy(a, k, block, to, from_shard=False):
            return pltpu.make_async_remote_copy(
                src_ref=srcs[a] if from_shard else slot(a, *block), dst_ref=slot(a, *block),
                send_sem=send_sems.at[a * 7 + k], recv_sem=recv_sems.at[a * 7 + k],
                device_id=to, device_id_type=MESH)

        own = [pltpu.make_async_copy(srcs[a], slot(a, *me), local_sems.at[a]) for a in range(na)]
        for cp in own:
            cp.start()
        first = []
        for a in range(na):
            first.append(copy(a, 0, me, sibling, True))
            first += [copy(a, 1 + j, me, (*chip, c), True) for j, chip in enumerate(chips)]
        for cp in first:
            cp.start()
        passed = []
        for j, chip in enumerate(chips):
            for a in range(na):
                copy(a, 1 + j, (*chip, c), me).wait_recv()
                fwd = copy(a, 4 + j, (*chip, c), sibling)
                fwd.start()
                passed.append(fwd)
        for a in range(na):
            copy(a, 0, sibling, me).wait_recv()
            for j, chip in enumerate(chips):
                copy(a, 4 + j, (*chip, 1 - c), me).wait_recv()
        for cp in first + passed:
            cp.wait_send()
        for cp in own:
            cp.wait()

    return pl.pallas_call(
        body,
        in_specs=[ANY] * na,
        out_specs=[ANY] * na,
        out_shape=[jax.ShapeDtypeStruct((N_DEV,) + a.shape, a.dtype) for a in arrays],
        scratch_shapes=[pltpu.SemaphoreType.DMA((7 * na,)), pltpu.SemaphoreType.DMA((7 * na,)),
                        pltpu.SemaphoreType.DMA((na,))],
        name="all_gather_weights",
    )(*arrays)


HBM = pl.BlockSpec(memory_space=pltpu.HBM)
SEM = pl.BlockSpec(memory_space=pltpu.SEMAPHORE)
_DATAFLOW = pltpu.SideEffectType.DATAFLOW_SIDE_EFFECTING


def _peer(x, y, c, r):
    return x ^ (r >> 2), y ^ ((r >> 1) & 1), c ^ (r & 1)


def _direct_copies(src_ref, land_ref, send_sems, recv_sems, gather, receiving):
    x, y, c = _coords()
    me = 4 * x + 2 * y + c
    copies = []
    for r in range(1, N_DEV):
        px, py, pc = _peer(x, y, c, r)
        peer = 4 * px + 2 * py + pc
        if gather:
            src, dst = src_ref, land_ref.at[peer if receiving else me]
        else:
            src, dst = src_ref.at[peer], land_ref.at[r - 1]
        copies.append(pltpu.make_async_remote_copy(
            src_ref=src, dst_ref=dst, send_sem=send_sems.at[r - 1], recv_sem=recv_sems.at[r - 1],
            device_id=(px, py, pc), device_id_type=MESH))
    return copies


def _direct_start(src, land, *, gather, name, after=None):
    def body(src_ref, land_ref, *rest):
        send_sems, recv_sems, token = rest[-5], rest[-4], rest[-1]
        for cp in _direct_copies(src_ref, land_ref, send_sems, recv_sems, gather, False):
            cp.start()
        token[...] = jnp.zeros_like(token)

    afters = () if after is None else (after,)
    return pl.pallas_call(
        body,
        name=name,
        out_shape=(pltpu.SemaphoreType.DMA((N_DEV - 1,)), pltpu.SemaphoreType.DMA((N_DEV - 1,)),
                   pltpu.HBM(src.shape, src.dtype), pltpu.HBM(land.shape, land.dtype),
                   jax.ShapeDtypeStruct((8, 128), F32)),
        in_specs=(HBM, HBM) + tuple(ANY for _ in afters),
        out_specs=(SEM, SEM, HBM, HBM, pl.BlockSpec(memory_space=pltpu.VMEM)),
        input_output_aliases={0: 2, 1: 3},
        compiler_params=pltpu.CompilerParams(has_side_effects=_DATAFLOW),
    )(pltpu.with_memory_space_constraint(src, pltpu.HBM), pltpu.with_memory_space_constraint(land, pltpu.HBM), *afters)


def _direct_wait(send_sems, recv_sems, src_thru, land_thru, after, *, gather, name):
    afters = after if isinstance(after, tuple) else (after,)

    def body(src_ref, land_ref, send_sems_ref, recv_sems_ref, *rest):
        del rest
        for cp in _direct_copies(src_ref, land_ref, send_sems_ref, recv_sems_ref, gather, True):
            cp.wait_send()
            cp.wait_recv()

    return pl.pallas_call(
        body,
        name=name,
        out_shape=(pltpu.HBM(src_thru.shape, src_thru.dtype), pltpu.HBM(land_thru.shape, land_thru.dtype)),
        in_specs=(HBM, HBM, SEM, SEM) + tuple(ANY for _ in afters),
        out_specs=(HBM, HBM),
        input_output_aliases={0: 0, 1: 1},
        compiler_params=pltpu.CompilerParams(has_side_effects=_DATAFLOW),
    )(src_thru, land_thru, send_sems, recv_sems, *afters)


def _sum_partials(src, land, me, *, tr, name):
    R = src.shape[1]

    def body(me_ref, s_ref, l_ref, o_ref):
        del me_ref
        acc = s_ref[0].astype(F32)
        for r in range(N_DEV - 1):
            acc = acc + l_ref[r].astype(F32)
        o_ref[...] = acc

    return pl.pallas_call(
        body,
        grid_spec=pltpu.PrefetchScalarGridSpec(
            num_scalar_prefetch=1, grid=(R // tr,),
            in_specs=[pl.BlockSpec((1, tr, 1024), lambda i, mr: (mr[0], i, 0)),
                      pl.BlockSpec((N_DEV - 1, tr, 1024), lambda i, mr: (0, i, 0))],
            out_specs=pl.BlockSpec((tr, 1024), lambda i, mr: (i, 0))),
        out_shape=jax.ShapeDtypeStruct((R, 1024), F32),
        name=name,
    )(me, src, land)


_SMALL = ("lb_logits", "hg_norm_gain", "swa_sinks", "rel_bias", "ln1_g", "ln1_b", "ln2_g", "ln2_b")


def _pack_small_grads(d_lb, d_gain, d_sink, d_rb, d_ln1_g, d_ln1_b, d_ln2_g, d_ln2_b, loss):
    def body(lb_ref, gain_ref, sink_ref, rb_ref, l1g_ref, l1b_ref, l2g_ref, l2b_ref, loss_ref, o_ref):
        o_ref[...] = jnp.zeros_like(o_ref)
        for row, ref in ((SM_LB, lb_ref), (SM_GAIN, gain_ref), (SM_L1G, l1g_ref), (SM_L1B, l1b_ref),
                         (SM_L2G, l2g_ref), (SM_L2B, l2b_ref)):
            o_ref[row:row + 1, :] = ref[...]
        o_ref[SM_SINK:SM_SINK + 1, 0:128] = sink_ref[0:1, :]
        o_ref[SM_LOSS:SM_LOSS + 1, 0:128] = loss_ref[0:1, :]
        o_ref[SM_RB:SM_RB + NUM_BUCKETS, 0:128] = rb_ref[...]

    vm = pl.BlockSpec(memory_space=pltpu.VMEM)
    return pl.pallas_call(
        body,
        in_specs=[vm] * 9,
        out_specs=vm,
        out_shape=jax.ShapeDtypeStruct((SM_ROWS, D_MODEL), F32),
        name="pack_small_grads",
    )(d_lb, d_gain, d_sink, d_rb, d_ln1_g, d_ln1_b, d_ln2_g, d_ln2_b, loss)


def _small_finish(gathered, w, m, v):
    n = len(_SMALL)

    def body(*refs):
        g_ref = refs[0]
        w_refs, m_refs, v_refs = refs[1:1 + n], refs[1 + n:1 + 2 * n], refs[1 + 2 * n:1 + 3 * n]
        outs = refs[1 + 3 * n:]
        loss_ref, tot = outs[0], outs[-1]
        g_out, d_out, m_out, v_out = (outs[1 + k * n:1 + (k + 1) * n] for k in range(4))
        acc = g_ref[0]
        for d in range(1, N_DEV):
            acc = acc + g_ref[d]
        tot[...] = acc
        loss_ref[...] = tot[SM_LOSS:SM_LOSS + 1, 0:1]
        lb = _lower_bound(w_refs[0])
        dl0 = tot[SM_LB:SM_LB + 1, :] * lb * (1.0 - lb)
        grads = (jnp.concatenate([dl0, -dl0], axis=0), tot[SM_GAIN:SM_GAIN + 1, :],
                 tot[SM_SINK:SM_SINK + 1, 0:SWA_HEADS], tot[SM_RB:SM_RB + NUM_BUCKETS, 0:SWA_HEADS],
                 tot[SM_L1G:SM_L1G + 1, :], tot[SM_L1B:SM_L1B + 1, :], tot[SM_L2G:SM_L2G + 1, :], tot[SM_L2B:SM_L2B + 1, :])
        for k, g in enumerate(grads):
            g_out[k][...] = g
            d_out[k][...], m_out[k][...], v_out[k][...] = _adam_step(w_refs[k][...], g, m_refs[k][...], v_refs[k][...])

    vm = pl.BlockSpec(memory_space=pltpu.VMEM)
    shapes = [jax.ShapeDtypeStruct(w[k].shape, F32) for k in _SMALL]
    res = pl.pallas_call(
        body,
        in_specs=[vm] * (1 + 3 * n),
        out_specs=[vm] * (1 + 4 * n),
        out_shape=[jax.ShapeDtypeStruct((1, 1), F32)] + shapes * 4,
        scratch_shapes=[pltpu.VMEM((SM_ROWS, D_MODEL), F32)],
        name="small_finish",
    )(gathered, *[w[k] for k in _SMALL], *[m[k] for k in _SMALL], *[v[k] for k in _SMALL])
    parts = [dict(zip(_SMALL, res[1 + k * n:1 + (k + 1) * n])) for k in range(4)]
    return (res[0], *parts)


def _adam_step(w, g, m, v):
    nm = ADAM_B1 * m + (1.0 - ADAM_B1) * g
    nv = ADAM_B2 * v + (1.0 - ADAM_B2) * jnp.square(g)
    m_hat = nm / (1.0 - ADAM_B1 ** ADAM_STEP)
    v_hat = nv / (1.0 - ADAM_B2 ** ADAM_STEP)
    return -ADAM_LR * (m_hat / (jnp.sqrt(v_hat) + ADAM_EPS) + ADAM_WD * w), nm, nv


def _adamw(w, g, m, v, *, tr, name):
    R, C = w.shape

    def body(w_ref, g_ref, m_ref, v_ref, d_ref, nm_ref, nv_ref):
        d_ref[...], nm_ref[...], nv_ref[...] = _adam_step(w_ref[...], g_ref[...], m_ref[...], v_ref[...])

    spec = pl.BlockSpec((tr, C), lambda i: (i, 0))
    return pl.pallas_call(
        body,
        grid=(R // tr,),
        in_specs=[spec] * 4,
        out_specs=[spec] * 3,
        out_shape=[jax.ShapeDtypeStruct((R, C), F32)] * 3,
        compiler_params=_cparams("parallel"),
        name=name,
    )(w, g, m, v)


_WEIGHTS = ("w_in", "lb_logits", "hg_norm_gain", "swa_sinks", "rel_bias", "w_mem_kv", "w_branch_hg", "w_branch_swa",
            "w_branch_mem", "w_out", "ln1_g", "ln1_b", "w_up", "w_down", "ln2_g", "ln2_b")


def kernel(x, mem, w_in, lb_logits, hg_norm_gain, swa_sinks, rel_bias, w_mem_kv, w_branch_hg, w_branch_swa, w_branch_mem, w_out, ln1_g, ln1_b, w_up, w_down, ln2_g, ln2_b, loss_target, m_w_in, m_lb_logits, m_hg_norm_gain, m_swa_sinks, m_rel_bias, m_w_mem_kv, m_w_branch_hg, m_w_branch_swa, m_w_branch_mem, m_w_out, m_ln1_g, m_ln1_b, m_w_up, m_w_down, m_ln2_g, m_ln2_b, v_w_in, v_lb_logits, v_hg_norm_gain, v_swa_sinks, v_rel_bias, v_w_mem_kv, v_w_branch_hg, v_w_branch_swa, v_w_branch_mem, v_w_out, v_ln1_g, v_ln1_b, v_w_up, v_w_down, v_ln2_g, v_ln2_b):
    w = dict(w_in=w_in, lb_logits=lb_logits, hg_norm_gain=hg_norm_gain, swa_sinks=swa_sinks, rel_bias=rel_bias,
             w_mem_kv=w_mem_kv, w_branch_hg=w_branch_hg, w_branch_swa=w_branch_swa, w_branch_mem=w_branch_mem,
             w_out=w_out, ln1_g=ln1_g, ln1_b=ln1_b, w_up=w_up, w_down=w_down, ln2_g=ln2_g, ln2_b=ln2_b)
    mom = dict(w_in=m_w_in, lb_logits=m_lb_logits, hg_norm_gain=m_hg_norm_gain, swa_sinks=m_swa_sinks, rel_bias=m_rel_bias,
               w_mem_kv=m_w_mem_kv, w_branch_hg=m_w_branch_hg, w_branch_swa=m_w_branch_swa, w_branch_mem=m_w_branch_mem,
               w_out=m_w_out, ln1_g=m_ln1_g, ln1_b=m_ln1_b, w_up=m_w_up, w_down=m_w_down, ln2_g=m_ln2_g, ln2_b=m_ln2_b)
    var = dict(w_in=v_w_in, lb_logits=v_lb_logits, hg_norm_gain=v_hg_norm_gain, swa_sinks=v_swa_sinks, rel_bias=v_rel_bias,
               w_mem_kv=v_w_mem_kv, w_branch_hg=v_w_branch_hg, w_branch_swa=v_w_branch_swa, w_branch_mem=v_w_branch_mem,
               w_out=v_w_out, ln1_g=v_ln1_g, ln1_b=v_ln1_b, w_up=v_w_up, w_down=v_w_down, ln2_g=v_ln2_g, ln2_b=v_ln2_b)
    xc, yc, cc = _coords()

    p1 = _bf(w_in[0].T)
    p2 = _bf(jnp.concatenate([w_down[0], w_up[0].T, w_branch_hg[0], w_branch_swa[0], w_branch_mem[0], w_out[0],
                              w_mem_kv[0].T], axis=0))
    me = 4 * xc + 2 * yc + cc
    (g1,) = _all_gather_weights(p1)
    land2 = lax.dynamic_update_slice(lax.empty((N_DEV, R_OTHER, D_MODEL), BF16), p2[None], (me, 0, 0))
    ag2 = _direct_start(p2, land2, gather=True, name="gather_other_weights_start")

    def other_weights(after):
        return _direct_wait(*ag2[:4], after, gather=True, name="gather_other_weights_wait")[1]

    blocks = lambda a: a.reshape(N_DEV, a.shape[0] // N_DEV, D_MODEL)
    started = {}

    def send_other_grads(part):
        started["others"] = _direct_start(part, lax.empty((N_DEV - 1, R_OTHER, D_MODEL), BF16), gather=False,
                                          name="scatter_other_grads_start")
        return started["others"][4]

    me1 = me.reshape(1).astype(jnp.int32)
    grads, delta, new_m, new_v = {}, {}, {}, {}

    def adamw(name):
        w2 = w[name][0]
        delta[name], new_m[name], new_v[name] = _adamw(
            w2, grads[name], mom[name][0], var[name][0], tr=w2.shape[0] // 4, name="adamw_" + name)

    def send_small_grads(packed):
        land = lax.dynamic_update_slice(lax.empty((N_DEV, SM_ROWS, D_MODEL), F32), packed[None], (me, 0, 0))
        started["small"] = _direct_start(packed, land, gather=True, name="gather_small_grads_start")
        return started["small"][4]

    def send_win_grad(g, after):
        started["win"] = _direct_start(blocks(g), lax.empty((N_DEV - 1, IN_SHARD, D_MODEL), BF16), gather=False,
                                       name="scatter_w_in_grad_start", after=after)
        mine2, landed2 = _direct_wait(*started["others"][:4], started["win"][4], gather=False,
                                      name="scatter_other_grads_wait")
        gs2 = _sum_partials(mine2, landed2, me1, tr=R_OTHER // 2, name="sum_other_grads")
        grads.update(
            w_down=gs2[R_DN:R_UP], w_up=gs2[R_UP:R_BH].T, w_branch_hg=gs2[R_BH:R_BS], w_branch_swa=gs2[R_BS:R_BM],
            w_branch_mem=gs2[R_BM:R_OUT], w_out=gs2[R_OUT:R_KV], w_mem_kv=gs2[R_KV:R_OTHER].T)
        for name in ("w_mem_kv", "w_branch_hg", "w_branch_swa", "w_branch_mem", "w_out", "w_up", "w_down"):
            adamw(name)
        return tuple(new_v[name] for name in new_v)

    grad_x = _local_step(
        x[0], mem[0], loss_target[0], lb_logits, hg_norm_gain, swa_sinks, rel_bias, ln1_g, ln1_b, ln2_g, ln2_b,
        g1.reshape(IN_COLS, D_MODEL), ag2[4], other_weights, send_other_grads, send_small_grads, send_win_grad)

    mine1, landed1 = _direct_wait(*started["win"][:4], grad_x, gather=False, name="scatter_w_in_grad_wait")
    g_win_t = _sum_partials(mine1, landed1, me1, tr=IN_SHARD // 2, name="sum_w_in_grad")
    d_t, m_t, v_t = _adamw(w_in[0].T, g_win_t, m_w_in[0].T, v_w_in[0].T, tr=IN_SHARD // 4, name="adamw_w_in")
    grads["w_in"], delta["w_in"], new_m["w_in"], new_v["w_in"] = g_win_t.T, d_t.T, m_t.T, v_t.T

    _, gathered = _direct_wait(*started["small"][:4], grad_x, gather=True, name="gather_small_grads_wait")
    loss, g_s, d_s, m_s, v_s = _small_finish(gathered, w, mom, var)
    for dst, src in ((grads, g_s), (delta, d_s), (new_m, m_s), (new_v, v_s)):
        dst.update(src)

    def shaped(d, name):
        return d[name].reshape(w[name].shape)

    return (loss.reshape(()), grad_x[None], *[shaped(grads, n) for n in _WEIGHTS], *[shaped(delta, n) for n in _WEIGHTS],
            *[shaped(new_m, n) for n in _WEIGHTS], *[shaped(new_v, n) for n in _WEIGHTS])
```

```python
import functools
import math

import jax
import jax.numpy as jnp
from jax import lax
from jax.experimental import pallas as pl
from jax.experimental.pallas import tpu as pltpu

F32 = jnp.float32
BF16 = jnp.bfloat16

D_MODEL = 1024
MEM_LEN = 256
HG_HEADS = 8
HG_DK = 128
HG_CHUNK = 64
SWA_HEADS = 16
SWA_HEAD_DIM = 64
SWA_BLOCK = 128
SWA_WINDOW = 128
MEM_HEADS = 4
MEM_HEAD_DIM = 256
NUM_BUCKETS = 32
MAX_DISTANCE = 128
D_FF = 4096
LN_EPS = 1e-5
RMS_EPS = 1e-6
ALPHA = 2.0 ** 0.25
N_DEV = 8

C_HQ, C_HF, C_HI, C_HG, C_SQ, C_SK, C_SV, C_MQ, C_GL = 0, 1024, 2048, 3072, 4096, 5120, 5248, 5376, 6400
IN_COLS = 9472
IN_SHARD = IN_COLS // N_DEV
Z_HG, Z_SQ, Z_SK, Z_MQ, Z_REST = 0, C_SQ - C_HG, C_SK - C_HG, C_MQ - C_HG, C_GL - C_HG

ADAM_LR = 0.001
ADAM_B1 = 0.9
ADAM_B2 = 0.999
ADAM_EPS = 1e-08
ADAM_WD = 0.01
ADAM_STEP = 10

VMEM_LIMIT = 58 * 1024 * 1024

R_DN, R_UP, R_BH, R_BS, R_BM, R_OUT, R_KV, R_OTHER = 0, 512, 1024, 1152, 1280, 1408, 1536, 1792

SM_LB, SM_GAIN, SM_SINK, SM_L1G, SM_L1B, SM_L2G, SM_L2B, SM_LOSS, SM_RB, SM_ROWS = 0, 2, 3, 4, 5, 6, 7, 8, 16, 48


def _bf(v):
    return v.astype(BF16)


def _f32(v):
    return v.astype(F32)


def _dot(a, b):
    return jnp.dot(a, b, preferred_element_type=F32)


def _dot_nt(a, b):
    return lax.dot_general(a, b, (((1,), (1,)), ((), ())), preferred_element_type=F32)


def _dot_tn(a, b):
    return lax.dot_general(a, b, (((0,), (0,)), ((), ())), preferred_element_type=F32)


def _sig(v):
    return 0.5 * jnp.tanh(0.5 * v) + 0.5


def _cparams(*sem):
    return pltpu.CompilerParams(dimension_semantics=sem, vmem_limit_bytes=VMEM_LIMIT)


def _const_spec(shape):
    nd = len(shape)
    return pl.BlockSpec(shape, lambda *_: (0,) * nd, pipeline_mode=pl.Buffered(1))


def _dep_spec():
    return pl.BlockSpec((8, 128), lambda *_: (0, 0))


def _in_proj(x, win_t, dep, *, tm):
    S = x.shape[0]

    def body(x_ref, w_ref, dep_ref, qfv_ref, z_ref, gl_ref, xb_ref):
        del dep_ref
        xb = _bf(x_ref[...])
        xb_ref[...] = xb
        for c0 in range(0, C_HG, 1024):
            qfv_ref[:, c0:c0 + 1024] = _dot_nt(xb, w_ref[c0:c0 + 1024, :])
        for c0 in range(0, Z_REST, Z_REST // 2):
            z_ref[:, c0:c0 + Z_REST // 2] = _bf(_dot_nt(xb, w_ref[C_HG + c0:C_HG + c0 + Z_REST // 2, :]))
        for c0 in range(0, IN_COLS - C_GL, 1024):
            gl_ref[:, c0:c0 + 1024] = _bf(_dot_nt(xb, w_ref[C_GL + c0:C_GL + c0 + 1024, :]))

    row = lambda w: pl.BlockSpec((tm, w), lambda i: (i, 0))
    return pl.pallas_call(
        body,
        grid=(S // tm,),
        in_specs=[row(D_MODEL), _const_spec(win_t.shape), _dep_spec()],
        out_specs=[row(C_HG), row(Z_REST), row(IN_COLS - C_GL), row(D_MODEL)],
        out_shape=[jax.ShapeDtypeStruct((S, C_HG), F32), jax.ShapeDtypeStruct((S, Z_REST), BF16),
                   jax.ShapeDtypeStruct((S, IN_COLS - C_GL), BF16), jax.ShapeDtypeStruct((S, D_MODEL), BF16)],
        compiler_params=_cparams("parallel"),
        name="in_proj",
    )(x, win_t, dep)


def _placement(into, tm, N, M, out_dtype):
    if into is None:
        return (lambda i: (i, 0)), (tm, N), jax.ShapeDtypeStruct((M, N), out_dtype), (), {}
    dest, block, index = into
    assert math.prod(block) == tm * N and dest.dtype == out_dtype
    return index, block, jax.ShapeDtypeStruct(dest.shape, dest.dtype), (dest,), {2: 0}


def _mm_tn_resident(a, b, *, tm, kc, name, out_dtype, into=None):
    K, M = a.shape
    N = b.shape[1]
    nk = K // kc
    index, block, out_shape, extra, aliases = _placement(into, tm, N, M, out_dtype)

    def body(a_ref, b_ref, *rest):
        o_ref = rest[-1]
        acc = jnp.zeros((tm, N), F32)
        for kk in range(nk):
            sl = pl.ds(kk * kc, kc)
            acc = acc + _dot_tn(_bf(a_ref[sl, :]), _bf(b_ref[sl, :]))
        o_ref[...] = acc.astype(o_ref.dtype).reshape(block)

    return pl.pallas_call(
        body,
        grid=(M // tm,),
        in_specs=[pl.BlockSpec((K, tm), lambda i: (0, i)), _const_spec((K, N))] + [ANY for _ in extra],
        out_specs=pl.BlockSpec(block, index),
        out_shape=out_shape,
        input_output_aliases=aliases,
        compiler_params=_cparams("parallel"),
        name=name,
    )(a, b, *extra)


def _mm_tn(a, b, *, kc, name, out_dtype=F32, into=None, tm=None):
    K, M = a.shape
    N = b.shape[1]
    if M > 1024 or tm is not None:
        return _mm_tn_resident(a, b, tm=tm or 256, kc=min(kc, 1024), name=name, out_dtype=out_dtype, into=into)
    tm = M
    nk = K // kc
    index, block, out_shape, extra, aliases = _placement(into, tm, N, M, out_dtype)

    def body(a_ref, b_ref, *rest):
        o_ref, acc = rest[-2], rest[-1]
        k = pl.program_id(1)
        part = _dot_tn(_bf(a_ref[...]), _bf(b_ref[...]))

        @pl.when(k == 0)
        def _():
            acc[...] = part

        @pl.when(k > 0)
        def _():
            acc[...] += part

        @pl.when(k == nk - 1)
        def _():
            o_ref[...] = acc[...].astype(o_ref.dtype).reshape(block)

    return pl.pallas_call(
        body,
        grid=(M // tm, nk),
        in_specs=[pl.BlockSpec((kc, tm), lambda i, k: (k, i)), pl.BlockSpec((kc, N), lambda i, k: (k, 0))]
        + [ANY for _ in extra],
        out_specs=pl.BlockSpec(block, lambda i, k: index(i)),
        out_shape=out_shape,
        input_output_aliases=aliases,
        scratch_shapes=[pltpu.VMEM((tm, N), F32)],
        compiler_params=_cparams("parallel", "arbitrary"),
        name=name,
    )(a, b, *extra)


def _grad_x(d_qfv, d_hg_gl, d_sq, d_skv, d_mq, w_qfv, win_t, add, deps, *, tm):
    M = add.shape[0]
    pieces = (d_qfv, d_hg_gl, d_sq, d_skv, d_mq)

    def body(qfv_ref, hggl_ref, sq_ref, skv_ref, mq_ref, wq_ref, w_ref, add_ref, *rest):
        o_ref = rest[-1]
        acc = add_ref[...] + _dot(qfv_ref[...], wq_ref[...])
        acc = acc + _dot(hggl_ref[:, 0:1024], w_ref[C_HG:C_SQ, :])
        acc = acc + _dot(hggl_ref[:, 1024:4096], w_ref[C_GL:IN_COLS, :])
        acc = acc + _dot(sq_ref[...], w_ref[C_SQ:C_SK, :])
        acc = acc + _dot(skv_ref[...], w_ref[C_SK:C_MQ, :])
        o_ref[...] = acc + _dot(mq_ref[...], w_ref[C_MQ:C_GL, :])

    return pl.pallas_call(
        body,
        grid=(M // tm,),
        in_specs=[pl.BlockSpec((tm, p.shape[1]), lambda i: (i, 0)) for p in pieces]
        + [_const_spec(w_qfv.shape), _const_spec(win_t.shape), pl.BlockSpec((tm, D_MODEL), lambda i: (i, 0))]
        + [_dep_spec() for _ in deps],
        out_specs=pl.BlockSpec((tm, D_MODEL), lambda i: (i, 0)),
        out_shape=jax.ShapeDtypeStruct((M, D_MODEL), F32),
        compiler_params=_cparams("parallel"),
        name="grad_x",
    )(*pieces, w_qfv, win_t, add, *deps)


def _lower_bound(lbl_ref):
    l0 = lbl_ref[0:1, :]
    l1 = lbl_ref[1:2, :]
    mx = jnp.maximum(l0, l1)
    e0 = jnp.exp(l0 - mx)
    e1 = jnp.exp(l1 - mx)
    return e0 / (e0 + e1)


def _tri(lower):
    r = lax.broadcasted_iota(jnp.int32, (HG_CHUNK, HG_CHUNK), 0)
    c = lax.broadcasted_iota(jnp.int32, (HG_CHUNK, HG_CHUNK), 1)
    return (r >= c) if lower else (r <= c)


def _hg_gates(fl, lb):
    sg = _sig(fl)
    f = lb + (1.0 - lb) * sg
    return sg, f, jnp.log(f), 1.0 - f


def _scan_rows(v, reverse=False):
    row = lax.broadcasted_iota(jnp.int32, v.shape, 0)
    s = 1
    while s < HG_CHUNK:
        if reverse:
            v = v + jnp.where(row < HG_CHUNK - s, pltpu.roll(v, HG_CHUNK - s, 0), 0.0)
        else:
            v = v + jnp.where(row >= s, pltpu.roll(v, s, 0), 0.0)
        s *= 2
    return v


def _hgrn_fwd(zmain, lb_logits, *, T):
    S = zmain.shape[0]
    nc = T // HG_CHUNK

    def body(q_ref, f_ref, v_ref, lbl_ref, o_ref, st_ref, state):
        @pl.when(pl.program_id(1) == 0)
        def _():
            state[...] = jnp.zeros_like(state)

        lb = _lower_bound(lbl_ref)
        tril = _tri(True)
        qis, updates, decays, intra = [], [], [], []
        for c in range(nc):
            sl = pl.ds(c * HG_CHUNK, HG_CHUNK)
            _, _, g, k = _hg_gates(_f32(f_ref[sl, :]), lb)
            b = _scan_rows(g)
            bl = jnp.sum(g, axis=0, keepdims=True)
            qi = _bf(_f32(q_ref[sl, :]) * jnp.exp(b))
            ki = _bf(k * jnp.exp(-b))
            ko = _bf(k * jnp.exp(bl - b))
            vb = _bf(v_ref[sl, :])
            att = jnp.where(tril, _dot_nt(qi, ki), 0.0)
            intra.append(_dot(_bf(att), vb))
            qis.append(qi)
            updates.append(_dot_tn(vb, ko))
            decays.append(jnp.exp(bl))
        st = state[...]
        for c in range(nc):
            st_ref[0, c] = st
            o_ref[pl.ds(c * HG_CHUNK, HG_CHUNK), :] = intra[c] + _dot_nt(qis[c], _bf(st))
            st = st * decays[c] + updates[c]
        state[...] = st

    col = lambda base: pl.BlockSpec((T, HG_DK), lambda h, t: (t, base + h))
    return pl.pallas_call(
        body,
        grid=(HG_HEADS, S // T),
        in_specs=[col(0), col(8), col(16), pl.BlockSpec((2, HG_DK), lambda h, t: (0, h))],
        out_specs=[
            pl.BlockSpec((T, HG_DK), lambda h, t: (t, h)),
            pl.BlockSpec((1, nc, HG_DK, HG_DK), lambda h, t: (h, t, 0, 0)),
        ],
        out_shape=[
            jax.ShapeDtypeStruct((S, D_MODEL), F32),
            jax.ShapeDtypeStruct((HG_HEADS, S // HG_CHUNK, HG_DK, HG_DK), F32),
        ],
        scratch_shapes=[pltpu.VMEM((HG_DK, HG_DK), F32)],
        compiler_params=_cparams("parallel", "arbitrary"),
        name="hgrn_fwd",
    )(zmain, zmain, zmain, lb_logits)


def _hgrn_bwd(zmain, lb_logits, states, d_o, *, T):
    S = zmain.shape[0]
    nc = T // HG_CHUNK
    nt = S // T

    def body(q_ref, f_ref, v_ref, lbl_ref, st_ref, do_ref, dz_ref, dlb_ref, dstate):
        @pl.when(pl.program_id(1) == 0)
        def _():
            dstate[...] = jnp.zeros_like(dstate)
            dlb_ref[...] = jnp.zeros_like(dlb_ref)

        lb = _lower_bound(lbl_ref)
        tril = _tri(True)
        last_row = lax.broadcasted_iota(jnp.int32, (HG_CHUNK, HG_DK), 0) == HG_CHUNK - 1
        saved = []
        for c in range(nc):
            sl = pl.ds(c * HG_CHUNK, HG_CHUNK)
            sg, f, g, k = _hg_gates(_f32(f_ref[sl, :]), lb)
            b = _scan_rows(g)
            bl = jnp.sum(g, axis=0, keepdims=True)
            eb = jnp.exp(b)
            enb = jnp.exp(-b)
            eo = jnp.exp(bl - b)
            q_in = _f32(q_ref[sl, :]) * eb
            k_in = k * enb
            k_out = k * eo
            qi, ki, ko = _bf(q_in), _bf(k_in), _bf(k_out)
            vb = _bf(v_ref[sl, :])
            dob = do_ref[sl, :]
            att = jnp.where(tril, _dot_nt(qi, ki), 0.0)
            d_att = _bf(jnp.where(tril, _dot_nt(dob, vb), 0.0))
            d_kin = _dot_tn(d_att, qi)
            saved.append(dict(
                sg=sg, f=f, eb=eb, enb=enb, eo=eo, ebl=jnp.exp(bl), k_out=k_out, ko=ko, vb=vb, dob=dob,
                d_v=_dot_tn(_bf(att), dob), d_qin=_dot(d_att, ki), d_kin=d_kin,
                qk=(q_in, k_in), d_state=_dot_tn(dob, qi)))
        dst = dstate[...]
        dsts = [None] * nc
        for c in reversed(range(nc)):
            dsts[c] = dst
            dst = dst * saved[c]["ebl"] + saved[c]["d_state"]
        dstate[...] = dst
        dlb = jnp.zeros((1, HG_DK), F32)
        for c in range(nc):
            sl = pl.ds(c * HG_CHUNK, HG_CHUNK)
            s = saved[c]
            q_in, k_in = s["qk"]
            st = st_ref[0, c]
            dstb = _bf(dsts[c])
            d_v = s["d_v"] + _dot_nt(s["ko"], dstb)
            d_qin = s["d_qin"] + _dot(s["dob"], _bf(st))
            d_kout = _dot(s["vb"], dstb)
            d_decay = jnp.sum(dsts[c] * st, axis=0, keepdims=True)
            kk = d_kout * s["k_out"]
            d_b = d_qin * q_in - s["d_kin"] * k_in - kk
            d_bl = jnp.sum(kk, axis=0, keepdims=True) + d_decay * s["ebl"]
            d_g = _scan_rows(d_b + jnp.where(last_row, d_bl, 0.0), reverse=True)
            d_f = d_g / s["f"] - (s["d_kin"] * s["enb"] + d_kout * s["eo"])
            dz_ref[sl, 0:HG_DK] = _bf(d_qin * s["eb"])
            dz_ref[sl, HG_DK:2 * HG_DK] = _bf(d_f * (1.0 - lb) * s["sg"] * (1.0 - s["sg"]))
            dz_ref[sl, 2 * HG_DK:3 * HG_DK] = _bf(d_v)
            dlb = dlb + jnp.sum(d_f * (1.0 - s["sg"]), axis=0, keepdims=True)
        dlb_ref[...] += dlb

    rev = lambda base: pl.BlockSpec((T, HG_DK), lambda h, t: (nt - 1 - t, base + h))
    outc = pl.BlockSpec((T, HG_DK), lambda h, t: (nt - 1 - t, h))
    return pl.pallas_call(
        body,
        grid=(HG_HEADS, nt),
        in_specs=[
            rev(0), rev(8), rev(16),
            pl.BlockSpec((2, HG_DK), lambda h, t: (0, h)),
            pl.BlockSpec((1, nc, HG_DK, HG_DK), lambda h, t: (h, nt - 1 - t, 0, 0)),
            outc,
        ],
        out_specs=[pl.BlockSpec((T, 3 * HG_DK), lambda h, t: (nt - 1 - t, h)),
                   pl.BlockSpec((1, HG_DK), lambda h, t: (0, h))],
        out_shape=[jax.ShapeDtypeStruct((S, 3 * D_MODEL), BF16), jax.ShapeDtypeStruct((1, D_MODEL), F32)],
        scratch_shapes=[pltpu.VMEM((HG_DK, HG_DK), F32)],
        compiler_params=_cparams("parallel", "arbitrary"),
        name="hgrn_bwd",
    )(zmain, zmain, zmain, lb_logits, states, d_o)


def _t5_bucket_table():
    qi = jnp.arange(SWA_BLOCK)[:, None] + SWA_BLOCK
    kj = jnp.arange(2 * SWA_BLOCK)[None, :]
    n = jnp.clip(qi - kj, 0, SWA_WINDOW - 1)
    max_exact = NUM_BUCKETS // 2
    nf = jnp.maximum(n, 1).astype(F32)
    large = max_exact + (jnp.log(nf / max_exact) / math.log(MAX_DISTANCE / max_exact)
                         * (NUM_BUCKETS - max_exact)).astype(jnp.int32)
    large = jnp.minimum(large, NUM_BUCKETS - 1)
    return jnp.where(n < max_exact, n, large).astype(jnp.int32)


SWA_ROWS = 32
MERGE_GROUPS = 2


def _swa_bias_init(bias, bucket_ref, rb_ref):
    bk = bucket_ref[...]
    qi = lax.broadcasted_iota(jnp.int32, bk.shape, 0) + SWA_BLOCK
    kj = lax.broadcasted_iota(jnp.int32, bk.shape, 1)
    band = (qi - kj >= 0) & (qi - kj < SWA_WINDOW)
    for h in range(SWA_HEADS):
        def sel(b, acc, h=h):
            return jnp.where(bk == b, rb_ref[b, h], acc)
        t = lax.fori_loop(0, NUM_BUCKETS, sel, jnp.zeros(bk.shape, F32))
        bias[1, h] = jnp.where(band, t, -jnp.inf)
        bias[0, h] = jnp.where(band & (kj >= SWA_BLOCK), t, -jnp.inf)


def _lane_halves(t, kv_head):
    lane = lax.broadcasted_iota(jnp.int32, t.shape, 1)
    rolled = pltpu.roll(t, 64, 1)
    zero = jnp.zeros_like(t)
    if kv_head == 0:
        return jnp.where(lane < 64, t, zero), jnp.where(lane >= 64, rolled, zero)
    return jnp.where(lane < 64, rolled, zero), jnp.where(lane >= 64, t, zero)


def _swa_zero_key0(t):
    return jnp.where(lax.broadcasted_iota(jnp.int32, t.shape, 0) == 0, jnp.zeros_like(t), t)


def _swa_probs(s, masked_bias, sink):
    s = s + masked_bias
    m = jnp.maximum(jnp.max(s, axis=-1, keepdims=True), sink)
    p = jnp.exp(s - m)
    es = jnp.exp(sink - m)
    inv = 1.0 / (jnp.sum(p, axis=-1, keepdims=True) + es)
    return p * inv, es * inv


def _swa_fwd(zmain, bucket, rel_bias, sinks):
    S = zmain.shape[0]
    nb = S // SWA_BLOCK
    scale = SWA_HEAD_DIM ** -0.5

    def body(q_ref, kvc_ref, kvp_ref, bucket_ref, rb_ref, sk_ref, o_ref, p_ref, bias):
        n = pl.program_id(0)

        @pl.when(n == 0)
        def _():
            _swa_bias_init(bias, bucket_ref, rb_ref)

        later = jnp.minimum(n, 1)
        kk = _bf(jnp.concatenate([kvp_ref[:, 0:128], kvc_ref[:, 0:128]], axis=0))
        vv = _swa_zero_key0(_bf(jnp.concatenate([kvp_ref[:, 128:256], kvc_ref[:, 128:256]], axis=0)))
        first_col = lax.broadcasted_iota(jnp.int32, (SWA_ROWS, 2 * SWA_BLOCK), 1) == 0
        scores, values = {}, {}
        for kvh in range(2):
            qst = _bf(jnp.concatenate([q_ref[:, pl.ds((kvh * 4 + jj) * 128, 128)] for jj in range(4)], axis=0) * scale)
            values[kvh] = _lane_halves(vv, kvh)
            for odd, kx in enumerate(_lane_halves(kk, kvh)):
                scores[kvh, odd] = _dot_nt(qst, kx)
        probs = {}
        for (kvh, odd), s in scores.items():
            parts = []
            for jj in range(4):
                h = 2 * (kvh * 4 + jj) + odd
                for r0 in range(0, SWA_BLOCK, SWA_ROWS):
                    p, ps = _swa_probs(s[jj * SWA_BLOCK + r0:jj * SWA_BLOCK + r0 + SWA_ROWS],
                                       bias[later, h, pl.ds(r0, SWA_ROWS), :], sk_ref[0, h])
                    part = _bf(jnp.where(first_col, ps, p))
                    p_ref[pl.ds(r0, SWA_ROWS), pl.ds(h * 2 * SWA_BLOCK, 2 * SWA_BLOCK)] = part
                    parts.append(part)
            probs[kvh, odd] = jnp.concatenate(parts, axis=0)
        for kvh in range(2):
            ost = _dot(probs[kvh, 0], values[kvh][0]) + _dot(probs[kvh, 1], values[kvh][1])
            for jj in range(4):
                o_ref[:, pl.ds((kvh * 4 + jj) * 128, 128)] = ost[jj * SWA_BLOCK:(jj + 1) * SWA_BLOCK]

    smem = pl.BlockSpec(memory_space=pltpu.SMEM)
    return pl.pallas_call(
        body,
        grid=(nb,),
        in_specs=[
            pl.BlockSpec((SWA_BLOCK, 1024), lambda n: (n, Z_SQ // 1024)),
            pl.BlockSpec((SWA_BLOCK, 256), lambda n: (n, Z_SK // 256)),
            pl.BlockSpec((SWA_BLOCK, 256), lambda n: (jnp.maximum(n - 1, 0), Z_SK // 256)),
            _const_spec((SWA_BLOCK, 2 * SWA_BLOCK)), smem, smem,
        ],
        out_specs=[pl.BlockSpec((SWA_BLOCK, 1024), lambda n: (n, 0)),
                   pl.BlockSpec((SWA_BLOCK, SWA_HEADS * 2 * SWA_BLOCK), lambda n: (n, 0))],
        out_shape=[jax.ShapeDtypeStruct((S, 1024), F32),
                   jax.ShapeDtypeStruct((S, SWA_HEADS * 2 * SWA_BLOCK), BF16)],
        scratch_shapes=[pltpu.VMEM((2, SWA_HEADS, SWA_BLOCK, 2 * SWA_BLOCK), F32)],
        compiler_params=_cparams("arbitrary"),
        name="swa_fwd",
    )(zmain, zmain, zmain, bucket, rel_bias, sinks)


def _swa_bwd(zmain, o_b, probs, d_o, bucket, dep):
    S = zmain.shape[0]
    nb = S // SWA_BLOCK
    scale = SWA_HEAD_DIM ** -0.5

    def body(q_ref, kvc_ref, kvp_ref, o_ref, p_ref, do_ref, bucket_ref, dep_ref,
             dq_ref, dkv_ref, drb_ref, dsk_ref, dbias, carry):
        del dep_ref
        n = pl.program_id(0)

        @pl.when(n == 0)
        def _():
            dbias[...] = jnp.zeros_like(dbias)
            carry[...] = jnp.zeros_like(carry)

        @pl.when(n < nb)
        def _():
            kk = _swa_zero_key0(_bf(jnp.concatenate([kvp_ref[:, 0:128], kvc_ref[:, 0:128]], axis=0)))
            vv = _swa_zero_key0(_bf(jnp.concatenate([kvp_ref[:, 128:256], kvc_ref[:, 128:256]], axis=0)))
            lane = lax.broadcasted_iota(jnp.int32, (2 * SWA_BLOCK, 128), 1)
            lane_q = lax.broadcasted_iota(jnp.int32, (4 * SWA_BLOCK, 128), 1)
            pair_cols = {kvh: [pl.ds((kvh * 4 + jj) * 128, 128) for jj in range(4)] for kvh in range(2)}
            qst, dost, ks, d_p, delta = {}, {}, {}, {}, {}
            for kvh in range(2):
                qst[kvh] = _bf(jnp.concatenate([q_ref[:, cl] for cl in pair_cols[kvh]], axis=0) * scale)
                dost[kvh] = jnp.concatenate([do_ref[:, cl] for cl in pair_cols[kvh]], axis=0)
                prod = dost[kvh].astype(F32) * jnp.concatenate([o_ref[:, cl] for cl in pair_cols[kvh]], axis=0)
                ks[kvh] = _lane_halves(kk, kvh)
                for odd, vx in enumerate(_lane_halves(vv, kvh)):
                    keep = (lane_q >= 64) if odd else (lane_q < 64)
                    delta[kvh, odd] = jnp.sum(jnp.where(keep, prod, 0.0), axis=-1, keepdims=True)
                    d_p[kvh, odd] = _dot_nt(dost[kvh], vx)
            pst, dsst = {}, {}
            for (kvh, odd), dp in d_p.items():
                p_parts, ds_parts = [], []
                for jj in range(4):
                    h = 2 * (kvh * 4 + jj) + odd
                    rows = slice(jj * SWA_BLOCK, (jj + 1) * SWA_BLOCK)
                    p = p_ref[:, pl.ds(h * 2 * SWA_BLOCK, 2 * SWA_BLOCK)]
                    ds = _f32(p) * (dp[rows] - delta[kvh, odd][rows])
                    dbias[h] += ds
                    p_parts.append(p)
                    ds_parts.append(_bf(ds))
                pst[kvh, odd] = jnp.concatenate(p_parts, axis=0)
                dsst[kvh, odd] = jnp.concatenate(ds_parts, axis=0)
            dk_parts, dv_parts = [], []
            for kvh in range(2):
                dq_st = _dot(dsst[kvh, 0], ks[kvh][0]) + _dot(dsst[kvh, 1], ks[kvh][1])
                for jj in range(4):
                    dq_ref[:, pair_cols[kvh][jj]] = _bf(dq_st[jj * SWA_BLOCK:(jj + 1) * SWA_BLOCK] * scale)
                zk = jnp.where(lane < 64, _dot_tn(dsst[kvh, 0], qst[kvh]), _dot_tn(dsst[kvh, 1], qst[kvh]))
                zv = jnp.where(lane < 64, _dot_tn(pst[kvh, 0], dost[kvh]), _dot_tn(pst[kvh, 1], dost[kvh]))
                dk_parts.append(zk + pltpu.roll(zk, 64, 1))
                dv_parts.append(zv + pltpu.roll(zv, 64, 1))
            dk = jnp.where(lane < 64, dk_parts[0], dk_parts[1])
            dv = jnp.where(lane < 64, dv_parts[0], dv_parts[1])
            dkv = _swa_zero_key0(jnp.concatenate([dk, dv], axis=1))
            dkv_ref[...] = _bf(carry[...] + dkv[0:SWA_BLOCK])
            carry[...] = dkv[SWA_BLOCK:]

        @pl.when(n == nb)
        def _():
            dkv_ref[...] = _bf(carry[...])
            first_col = lax.broadcasted_iota(jnp.int32, (SWA_BLOCK, 2 * SWA_BLOCK), 1) == 0
            bk = jnp.where(first_col, -1, bucket_ref[...])

            row = lax.broadcasted_iota(jnp.int32, (NUM_BUCKETS, 128), 0)
            lane = lax.broadcasted_iota(jnp.int32, (NUM_BUCKETS, 128), 1)

            def total(v):
                return jnp.sum(jnp.sum(v, axis=1, keepdims=True), axis=0, keepdims=True)

            def per_head(h, acc):
                db = dbias[h]
                d_rb, d_sk = acc
                d_sk = d_sk + jnp.where((row == 0) & (lane == h), total(jnp.where(first_col, db, 0.0)), 0.0)

                def per_bucket(b, d_rb):
                    return d_rb + jnp.where((row == b) & (lane == h), total(jnp.where(bk == b, db, 0.0)), 0.0)

                return lax.fori_loop(0, NUM_BUCKETS, per_bucket, d_rb), d_sk

            zero = jnp.zeros((NUM_BUCKETS, 128), F32)
            d_rb, d_sk = lax.fori_loop(0, SWA_HEADS, per_head, (zero, zero))
            drb_ref[...] = d_rb
            dsk_ref[...] = d_sk[0:8]

    cur = lambda n: jnp.minimum(n, nb - 1)
    prev = lambda n: jnp.maximum(jnp.minimum(n, nb - 1) - 1, 0)
    return pl.pallas_call(
        body,
        grid=(nb + 1,),
        in_specs=[
            pl.BlockSpec((SWA_BLOCK, 1024), lambda n: (cur(n), Z_SQ // 1024)),
            pl.BlockSpec((SWA_BLOCK, 256), lambda n: (cur(n), Z_SK // 256)),
            pl.BlockSpec((SWA_BLOCK, 256), lambda n: (prev(n), Z_SK // 256)),
            pl.BlockSpec((SWA_BLOCK, 1024), lambda n: (cur(n), 0)),
            pl.BlockSpec((SWA_BLOCK, SWA_HEADS * 2 * SWA_BLOCK), lambda n: (cur(n), 0)),
            pl.BlockSpec((SWA_BLOCK, 1024), lambda n: (cur(n), 0)),
            _const_spec((SWA_BLOCK, 2 * SWA_BLOCK)), _dep_spec(),
        ],
        out_specs=[
            pl.BlockSpec((SWA_BLOCK, 1024), lambda n: (cur(n), 0)),
            pl.BlockSpec((SWA_BLOCK, 256), lambda n: (jnp.maximum(n - 1, 0), 0)),
            pl.BlockSpec((NUM_BUCKETS, 128), lambda n: (0, 0)),
            pl.BlockSpec((8, 128), lambda n: (0, 0)),
        ],
        out_shape=[
            jax.ShapeDtypeStruct((S, 1024), BF16),
            jax.ShapeDtypeStruct((S, 256), BF16),
            jax.ShapeDtypeStruct((NUM_BUCKETS, 128), F32),
            jax.ShapeDtypeStruct((8, 128), F32),
        ],
        scratch_shapes=[
            pltpu.VMEM((SWA_HEADS, SWA_BLOCK, 2 * SWA_BLOCK), F32),
            pltpu.VMEM((SWA_BLOCK, 256), F32),
        ],
        compiler_params=_cparams("arbitrary"),
        name="swa_bwd",
    )(zmain, zmain, zmain, o_b, probs, d_o, bucket, dep)


def _mem_q_specs(T):
    return [pl.BlockSpec((T, MEM_HEAD_DIM), lambda t, h=h: (t, Z_MQ // MEM_HEAD_DIM + h)) for h in range(MEM_HEADS)]


def _mem_kv_proj(mem, g2):
    def body(mem_ref, w_ref, o_ref):
        o_ref[...] = _dot_nt(_bf(mem_ref[...]), _rows(w_ref))

    return pl.pallas_call(
        body,
        grid=(1,),
        in_specs=[pl.BlockSpec((MEM_LEN, D_MODEL), lambda i: (0, 0)), _gathered_spec(R_KV, R_OTHER)],
        out_specs=pl.BlockSpec((MEM_LEN, 2048), lambda i: (0, 0)),
        out_shape=jax.ShapeDtypeStruct((MEM_LEN, 2048), F32),
        compiler_params=_cparams("arbitrary"),
        name="mem_kv_proj",
    )(mem, g2)


def _mem_fwd(zmain, mkv, *, T):
    S = zmain.shape[0]

    def body(q0, q1, q2, q3, kv_ref, o_ref, p_ref):
        heads = [pl.ds(h * MEM_HEAD_DIM, MEM_HEAD_DIM) for h in range(MEM_HEADS)]
        scores = [_dot_nt(_bf(q_ref[...] * (MEM_HEAD_DIM ** -0.5)), _bf(kv_ref[:, cols]))
                  for q_ref, cols in zip((q0, q1, q2, q3), heads)]
        probs = []
        for s, cols in zip(scores, heads):
            e = jnp.exp(s - jnp.max(s, axis=-1, keepdims=True))
            pb = _bf(e * (1.0 / jnp.sum(e, axis=-1, keepdims=True)))
            p_ref[:, cols] = pb
            probs.append(pb)
        for h, (pb, cols) in enumerate(zip(probs, heads)):
            o_ref[:, cols] = _dot(pb, _bf(kv_ref[:, pl.ds(1024 + h * MEM_HEAD_DIM, MEM_HEAD_DIM)]))

    row = pl.BlockSpec((T, 1024), lambda t: (t, 0))
    return pl.pallas_call(
        body,
        grid=(S // T,),
        in_specs=_mem_q_specs(T) + [_const_spec((MEM_LEN, 2048))],
        out_specs=[row, row],
        out_shape=[jax.ShapeDtypeStruct((S, 1024), F32), jax.ShapeDtypeStruct((S, 1024), BF16)],
        compiler_params=_cparams("parallel"),
        name="mem_fwd",
    )(zmain, zmain, zmain, zmain, mkv)


def _mem_bwd(zmain, mkv, o_c, probs, d_o, *, T):
    S = zmain.shape[0]
    scale = MEM_HEAD_DIM ** -0.5

    def body(q0, q1, q2, q3, kv_ref, o_ref, p_ref, do_ref, dq_ref, dkv_ref):
        @pl.when(pl.program_id(0) == 0)
        def _():
            dkv_ref[...] = jnp.zeros_like(dkv_ref)

        heads = [(pl.ds(h * MEM_HEAD_DIM, MEM_HEAD_DIM), pl.ds(1024 + h * MEM_HEAD_DIM, MEM_HEAD_DIM))
                 for h in range(MEM_HEADS)]
        d_p = [_dot_nt(do_ref[:, cols], _bf(kv_ref[:, vcols])) for cols, vcols in heads]
        d_s = []
        for dp, (cols, _) in zip(d_p, heads):
            delta = jnp.sum(do_ref[:, cols].astype(F32) * o_ref[:, cols], axis=-1, keepdims=True)
            d_s.append(_bf(_f32(p_ref[:, cols]) * (dp - delta)))
        for ds, q_ref, (cols, vcols) in zip(d_s, (q0, q1, q2, q3), heads):
            dq_ref[:, cols] = _bf(_dot(ds, _bf(kv_ref[:, cols])) * scale)
            dkv_ref[:, cols] += _dot_tn(ds, _bf(q_ref[...] * scale))
            dkv_ref[:, vcols] += _dot_tn(p_ref[:, cols], do_ref[:, cols])

    row = pl.BlockSpec((T, 1024), lambda t: (t, 0))
    return pl.pallas_call(
        body,
        grid=(S // T,),
        in_specs=_mem_q_specs(T) + [_const_spec((MEM_LEN, 2048)), row, row, row],
        out_specs=[row, pl.BlockSpec((MEM_LEN, 2048), lambda t: (0, 0))],
        out_shape=[jax.ShapeDtypeStruct((S, 1024), BF16), jax.ShapeDtypeStruct((MEM_LEN, 2048), F32)],
        compiler_params=_cparams("arbitrary"),
        name="mem_bwd",
    )(zmain, zmain, zmain, zmain, mkv, o_c, probs, d_o)


def _layer_norm(u):
    mu = jnp.mean(u, axis=-1, keepdims=True)
    xc = u - mu
    rstd = lax.rsqrt(jnp.mean(xc * xc, axis=-1, keepdims=True) + LN_EPS)
    return xc * rstd, rstd


def _layer_norm_bwd(dy, gamma, xhat, rstd):
    dxh = dy * gamma
    return rstd * (dxh - jnp.mean(dxh, axis=-1, keepdims=True) - xhat * jnp.mean(dxh * xhat, axis=-1, keepdims=True))


def _merge_stages(rows, oraw_ref, hg_ref, ob_ref, oc_ref, gl_ref, x_ref, gain_ref, wbh, wbs, wbm, wout, g_ref, b_ref,
                  fwd_out=None, bwd=None):
    ys, rs = [], []
    for h in range(HG_HEADS):
        oh = oraw_ref[rows, pl.ds(h * HG_DK, HG_DK)]
        r = lax.rsqrt(jnp.mean(oh * oh, axis=-1, keepdims=True) + RMS_EPS)
        ys.append(oh * r)
        rs.append(r)
    y = jnp.concatenate(ys, axis=1)
    hg = _f32(hg_ref[rows, :])
    sg = _sig(hg)
    silu = hg * sg
    gain = gain_ref[...]
    oa = _bf(y * gain * silu)
    pa = _dot(oa, _rows(wbh))
    pb = _dot(_bf(ob_ref[rows, :]), _rows(wbs))
    pc = _dot(_bf(oc_ref[rows, :]), _rows(wbm))
    yield
    gates = [_sig(_f32(gl_ref[rows, pl.ds(i * 1024, 1024)])) for i in range(3)]
    m = _bf(gates[0] * pa + gates[1] * pb + gates[2] * pc)
    mix = _dot(m, _rows(wout))
    yield
    xhat, rstd = _layer_norm(ALPHA * x_ref[rows, :] + mix)
    if bwd is None:
        h1 = xhat * g_ref[...] + b_ref[...]
        fwd_out[0][rows, :] = h1
        fwd_out[1][rows, :] = _bf(h1)
        return
    (dh1_ref, dx_ref, du1_ref, m_ref, oa_ref, dpa_ref, dpb_ref, dpc_ref, doraw_ref, dob_ref, doc_ref, dz_ref,
     dgain_ref, dg_ref, db_ref) = bwd
    dh1 = dh1_ref[rows, :]
    dg_ref[...] += jnp.sum(dh1 * xhat, axis=0, keepdims=True)
    db_ref[...] += jnp.sum(dh1, axis=0, keepdims=True)
    du1 = _layer_norm_bwd(dh1, g_ref[...], xhat, rstd)
    dx_ref[rows, :] = ALPHA * du1
    du1b = _bf(du1)
    du1_ref[rows, :] = du1b
    m_ref[rows, :] = m
    oa_ref[rows, :] = oa
    dm = _dot_nt(du1b, _rows(wout))
    yield
    d_branches = []
    for i, (g, p, dp_ref, w_r) in enumerate(zip(gates, (pa, pb, pc), (dpa_ref, dpb_ref, dpc_ref), (wbh, wbs, wbm))):
        dz_ref[rows, pl.ds((i + 1) * 1024, 1024)] = _bf(dm * p * g * (1.0 - g))
        dp = _bf(dm * g)
        dp_ref[rows, :] = dp
        d_branches.append(_dot_nt(dp, _rows(w_r)))
    yield
    doa, d_ob, d_oc = d_branches
    dob_ref[rows, :] = _bf(d_ob)
    doc_ref[rows, :] = _bf(d_oc)
    t = doa * y
    dgain_ref[...] += jnp.sum(t * silu, axis=0, keepdims=True)
    dz_ref[rows, 0:1024] = _bf(t * gain * sg * (1.0 + hg * (1.0 - sg)))
    dy = doa * gain * silu
    for h in range(HG_HEADS):
        cols = slice(h * HG_DK, (h + 1) * HG_DK)
        yh = y[:, cols]
        dyh = dy[:, cols]
        doraw_ref[rows, pl.ds(h * HG_DK, HG_DK)] = _bf(rs[h] * (dyh - yh * jnp.mean(dyh * yh, axis=-1, keepdims=True)))


def _interleave(chains):
    live = list(chains)
    while live:
        still = []
        for c in live:
            try:
                next(c)
                still.append(c)
            except StopIteration:
                pass
        live = still


def _gathered_spec(lo, hi):
    n = hi - lo
    return pl.BlockSpec((N_DEV, n, D_MODEL), lambda *_: (0, lo // n, 0), pipeline_mode=pl.Buffered(1))


def _rows(w_ref):
    return w_ref[...].reshape(-1, D_MODEL)


def _merge_in_specs(T):
    row = lambda w, c=0: pl.BlockSpec((T, w), lambda i: (i, c))
    vec = pl.BlockSpec((1, D_MODEL), lambda i: (0, 0))
    w = [_gathered_spec(lo, hi) for lo, hi in ((R_BH, R_BS), (R_BS, R_BM), (R_BM, R_OUT), (R_OUT, R_KV))]
    return [row(1024), row(1024, Z_HG // 1024), row(1024), row(1024), row(3072), row(1024), vec, *w, vec, vec]


def _merge_fwd(o_raw, zmain, o_b, o_c, gl, x, gain, wbh, wbs, wbm, wout, ln_g, ln_b, *, T):
    S = x.shape[0]

    def body(*refs):
        ins, outs = refs[:13], refs[13:]
        _interleave(_merge_stages(pl.ds(r0, T // MERGE_GROUPS), *ins, fwd_out=outs)
                    for r0 in range(0, T, T // MERGE_GROUPS))

    row = pl.BlockSpec((T, D_MODEL), lambda i: (i, 0))
    return pl.pallas_call(
        body,
        grid=(S // T,),
        in_specs=_merge_in_specs(T),
        out_specs=[row, row],
        out_shape=[jax.ShapeDtypeStruct((S, D_MODEL), F32), jax.ShapeDtypeStruct((S, D_MODEL), BF16)],
        compiler_params=_cparams("parallel"),
        name="merge_fwd",
    )(o_raw, zmain, o_b, o_c, gl, x, gain, wbh, wbs, wbm, wout, ln_g, ln_b)


def _merge_bwd(d_h1, o_raw, zmain, o_b, o_c, gl, x, gain, wbh, wbs, wbm, wout, ln_g, ln_b, *, T):
    S = x.shape[0]

    def body(dh1_ref, oraw_ref, hg_ref, ob_ref, oc_ref, gl_ref, x_ref, gain_ref, wbh_r, wbs_r, wbm_r, wout_r, g_ref, b_ref,
             dx_ref, du1_ref, m_ref, oa_ref, dpa_ref, dpb_ref, dpc_ref, doraw_ref, dob_ref, doc_ref, dz_ref,
             dgain_ref, dg_ref, db_ref):
        del b_ref

        @pl.when(pl.program_id(0) == 0)
        def _():
            dgain_ref[...] = jnp.zeros_like(dgain_ref)
            dg_ref[...] = jnp.zeros_like(dg_ref)
            db_ref[...] = jnp.zeros_like(db_ref)

        ins = (oraw_ref, hg_ref, ob_ref, oc_ref, gl_ref, x_ref, gain_ref, wbh_r, wbs_r, wbm_r, wout_r, g_ref, None)
        bwd = (dh1_ref, dx_ref, du1_ref, m_ref, oa_ref, dpa_ref, dpb_ref, dpc_ref, doraw_ref, dob_ref, doc_ref, dz_ref,
               dgain_ref, dg_ref, db_ref)
        _interleave([_merge_stages(pl.ds(0, T), *ins, bwd=bwd)])

    row = lambda w: pl.BlockSpec((T, w), lambda i: (i, 0))
    vec = pl.BlockSpec((1, D_MODEL), lambda i: (0, 0))
    bshape = jax.ShapeDtypeStruct((S, D_MODEL), BF16)
    vshape = jax.ShapeDtypeStruct((1, D_MODEL), F32)
    return pl.pallas_call(
        body,
        grid=(S // T,),
        in_specs=[row(1024)] + _merge_in_specs(T),
        out_specs=[row(1024)] * 10 + [row(4096), vec, vec, vec],
        out_shape=[jax.ShapeDtypeStruct((S, D_MODEL), F32)] + [bshape] * 9
        + [jax.ShapeDtypeStruct((S, 4096), BF16), vshape, vshape, vshape],
        compiler_params=_cparams("arbitrary"),
        name="merge_bwd",
    )(d_h1, o_raw, zmain, o_b, o_c, gl, x, gain, wbh, wbs, wbm, wout, ln_g, ln_b)


def _mlp_fwd_bwd(h1, target, wup_t, wdn, ln_g, ln_b, *, T, FC):
    S = h1.shape[0]
    nf = D_FF // FC
    assert FC == R_BH - R_UP == R_UP - R_DN

    def body(h1_ref, t_ref, wup_ref, wdn_ref, g_ref, b_ref, dh1_ref, a_ref, dup_ref, du2_ref, loss_ref, dg_ref, db_ref, up_scr):
        @pl.when(pl.program_id(0) == 0)
        def _():
            loss_ref[...] = jnp.zeros_like(loss_ref)
            dg_ref[...] = jnp.zeros_like(dg_ref)
            db_ref[...] = jnp.zeros_like(db_ref)

        h1v = h1_ref[...]
        h1b = _bf(h1v)
        ff = jnp.zeros((T, D_MODEL), F32)
        for j in range(nf):
            rows = pl.ds(j * FC, FC)
            up = jnp.maximum(_dot_nt(h1b, wup_ref[j]), 0.0)
            up_scr[:, rows] = _bf(up)
            a = _bf(up * up)
            a_ref[:, rows] = a
            ff = ff + _dot(a, wdn_ref[j])
        xhat, rstd = _layer_norm(ALPHA * h1v + ff)
        gamma = g_ref[...]
        err = xhat * gamma + b_ref[...] - t_ref[...]
        loss_ref[...] += jnp.sum(jnp.sum(err * err, axis=-1, keepdims=True), axis=0, keepdims=True) * (0.5 / D_MODEL)
        dy = err * (1.0 / D_MODEL)
        dg_ref[...] += jnp.sum(dy * xhat, axis=0, keepdims=True)
        db_ref[...] += jnp.sum(dy, axis=0, keepdims=True)
        du2 = _layer_norm_bwd(dy, gamma, xhat, rstd)
        du2b = _bf(du2)
        du2_ref[...] = du2b
        dh1 = ALPHA * du2
        for j in range(nf):
            rows = pl.ds(j * FC, FC)
            dup = _bf(_dot_nt(du2b, wdn_ref[j]) * (2.0 * up_scr[:, rows].astype(F32)))
            dup_ref[:, rows] = dup
            dh1 = dh1 + _dot(dup, wup_ref[j])
        dh1_ref[...] = dh1

    row = lambda w: pl.BlockSpec((T, w), lambda i: (i, 0))
    vec = pl.BlockSpec((1, D_MODEL), lambda i: (0, 0))
    vshape = jax.ShapeDtypeStruct((1, D_MODEL), F32)
    return pl.pallas_call(
        body,
        grid=(S // T,),
        in_specs=[row(1024), row(1024), _gathered_spec(R_UP, R_BH), _gathered_spec(R_DN, R_UP), vec, vec],
        out_specs=[row(1024), row(D_FF), row(D_FF), row(1024), pl.BlockSpec((8, 128), lambda i: (0, 0)), vec, vec],
        out_shape=[
            jax.ShapeDtypeStruct((S, D_MODEL), F32),
            jax.ShapeDtypeStruct((S, D_FF), BF16),
            jax.ShapeDtypeStruct((S, D_FF), BF16),
            jax.ShapeDtypeStruct((S, D_MODEL), BF16),
            jax.ShapeDtypeStruct((8, 128), F32), vshape, vshape,
        ],
        scratch_shapes=[pltpu.VMEM((T, D_FF), BF16)],
        compiler_params=_cparams("arbitrary"),
        name="mlp_fwd_bwd",
    )(h1, target, wup_t, wdn, ln_g, ln_b)


def _local_step(x, mem, target, lb_logits, gain, sinks, rel_bias, ln1_g, ln1_b, ln2_g, ln2_b,
                win_t, dep0, other_weights, send_other_grads, send_small_grads, send_win_grad):
    S = x.shape[0]
    T = min(256, S)
    KC = min(2048, S)
    z_qfv, zmain, gl, xb = _in_proj(x, win_t, dep0, tm=min(512, S))
    bucket = _t5_bucket_table()

    o_raw, states = _hgrn_fwd(z_qfv, lb_logits, T=min(2048, S))
    o_b, swa_probs = _swa_fwd(zmain, bucket, rel_bias, sinks)
    g2 = other_weights((o_b, o_raw))
    mkv = _mem_kv_proj(mem, g2)
    o_c, mem_probs = _mem_fwd(zmain, mkv, T=min(1024, S))
    merge_args = (o_raw, zmain, o_b, o_c, gl, x, gain, g2, g2, g2, g2, ln1_g, ln1_b)
    h1, h1b = _merge_fwd(*merge_args, T=min(512, S))

    d_h1, act, d_up, du2, loss, d_ln2_g, d_ln2_b = _mlp_fwd_bwd(h1, target, g2, g2, ln2_g, ln2_b, T=min(512, S), FC=512)
    wgrad = functools.partial(_mm_tn, out_dtype=BF16)
    halves = lambda r0: (lambda i: (i // 2, r0 // 256 + i % 2, 0))
    whole = lambda r0: (lambda i: (0, r0 // 128, 0))
    og = lax.empty((N_DEV, R_OTHER, D_MODEL), BF16)
    og = wgrad(act, du2, kc=KC, name="grad_w_down", into=(og, (1, 256, D_MODEL), halves(R_DN)))
    og = wgrad(d_up, h1b, kc=KC, name="grad_w_up", into=(og, (1, 256, D_MODEL), halves(R_UP)))

    (dx_part, du1, m, oa, dpa, dpb, dpc, d_oraw, d_ob, d_oc, d_hg_gl,
     d_gain, d_ln1_g, d_ln1_b) = _merge_bwd(d_h1, *merge_args, T=T)
    for a_op, b_op, r0, nm in ((m, du1, R_OUT, "out"), (oa, dpa, R_BH, "branch_hg"), (o_b, dpb, R_BS, "branch_swa"),
                               (o_c, dpc, R_BM, "branch_mem")):
        og = wgrad(a_op, b_op, kc=KC, name="grad_w_" + nm, into=(og, (N_DEV, 128, D_MODEL), whole(r0)))

    d_mq, d_mkv = _mem_bwd(zmain, mkv, o_c, mem_probs, d_oc, T=min(1024, S))
    og = wgrad(d_mkv, mem, kc=MEM_LEN, name="grad_w_mem_kv",
               into=(og, (1, 256, D_MODEL), lambda i: (i, R_KV // 256, 0)))
    sent_others = send_other_grads(og)
    d_sq, d_skv, d_rb, d_sink = _swa_bwd(zmain, o_b, swa_probs, d_ob, bucket, sent_others)
    d_qfv, d_lb = _hgrn_bwd(z_qfv, lb_logits, states, d_oraw, T=min(2048, S))
    sent_small = send_small_grads(_pack_small_grads(d_lb, d_gain, d_sink, d_rb, d_ln1_g, d_ln1_b, d_ln2_g, d_ln2_b, loss))

    head_major = lambda a: a.reshape(3, HG_HEADS, HG_DK, D_MODEL).transpose(1, 0, 2, 3).reshape(3 * D_MODEL, D_MODEL)
    pieces = (d_qfv, d_hg_gl, d_sq, d_skv, d_mq)
    placed = (
        ("qfv", d_qfv, 128, lambda i: ((i % 3) * HG_HEADS + i // 3, 0)),
        ("hg_gates", d_hg_gl, 256, lambda i: (jnp.where(i < 4, C_HG // 256 + i, C_GL // 256 + i - 4), 0)),
        ("swa_q", d_sq, None, lambda i: (C_SQ // 1024, 0)),
        ("swa_kv", d_skv, None, lambda i: (C_SK // 256, 0)),
        ("mem_q", d_mq, 256, lambda i: (C_MQ // 256 + i, 0)),
    )
    g_win_t = lax.empty((IN_COLS, D_MODEL), BF16)
    for nm, piece, tile, index in placed:
        g_win_t = wgrad(piece, xb, kc=KC, name="grad_w_in_" + nm, tm=tile,
                        into=(g_win_t, (tile or piece.shape[1], D_MODEL), index))
    sent_win = send_win_grad(g_win_t, sent_small)
    return _grad_x(*pieces, head_major(win_t[:C_HG]), win_t, dx_part, sent_win, tm=T)


MESH = pl.DeviceIdType.MESH
ANY = pl.BlockSpec(memory_space=pl.ANY)


def _coords():
    return lax.axis_index("x"), lax.axis_index("y"), lax.axis_index("c")


def _other_chips(x, y):
    return [(1 - x, y), (x, 1 - y), (1 - x, 1 - y)]


def _all_gather_weights(*arrays):
    na = len(arrays)

    def body(*refs):
        srcs, dsts = refs[:na], refs[na:2 * na]
        send_sems, recv_sems, local_sems = refs[2 * na:]
        x, y, c = _coords()
        me, sibling = (x, y, c), (x, y, 1 - c)
        chips = _other_chips(x, y)

        def slot(a, px, py, pc):
            return dsts[a].at[4 * px + 2 * py + pc]

        def copy(a, k, block, to, from_shard=False):
            return pltpu.make_async_remote_copy(
                src_ref=srcs[a] if from_shard else slot(a, *block), dst_ref=slot(a, *block),
                send_sem=send_sems.at[a * 7 + k], recv_sem=recv_sems.at[a * 7 + k],
                device_id=to, device_id_type=MESH)

        own = [pltpu.make_async_copy(srcs[a], slot(a, *me), local_sems.at[a]) for a in range(na)]
        for cp in own:
            cp.start()
        first = []
        for a in range(na):
            first.append(copy(a, 0, me, sibling, True))
            first += [copy(a, 1 + j, me, (*chip, c), True) for j, chip in enumerate(chips)]
        for cp in first:
            cp.start()
        passed = []
        for j, chip in enumerate(chips):
            for a in range(na):
                copy(a, 1 + j, (*chip, c), me).wait_recv()
                fwd = copy(a, 4 + j, (*chip, c), sibling)
                fwd.start()
                passed.append(fwd)
        for a in range(na):
            copy(a, 0, sibling, me).wait_recv()
            for j, chip in enumerate(chips):
                copy(a, 4 + j, (*chip, 1 - c), me).wait_recv()
        for cp in first + passed:
            cp.wait_send()
        for cp in own:
            cp.wait()

    return pl.pallas_call(
        body,
        in_specs=[ANY] * na,
        out_specs=[ANY] * na,
        out_shape=[jax.ShapeDtypeStruct((N_DEV,) + a.shape, a.dtype) for a in arrays],
        scratch_shapes=[pltpu.SemaphoreType.DMA((7 * na,)), pltpu.SemaphoreType.DMA((7 * na,)),
                        pltpu.SemaphoreType.DMA((na,))],
        name="all_gather_weights",
    )(*arrays)


HBM = pl.BlockSpec(memory_space=pltpu.HBM)
SEM = pl.BlockSpec(memory_space=pltpu.SEMAPHORE)
_DATAFLOW = pltpu.SideEffectType.DATAFLOW_SIDE_EFFECTING


def _peer(x, y, c, r):
    return x ^ (r >> 2), y ^ ((r >> 1) & 1), c ^ (r & 1)


def _direct_copies(src_ref, land_ref, send_sems, recv_sems, gather, receiving):
    x, y, c = _coords()
    me = 4 * x + 2 * y + c
    copies = []
    for r in range(1, N_DEV):
        px, py, pc = _peer(x, y, c, r)
        peer = 4 * px + 2 * py + pc
        if gather:
            src, dst = src_ref, land_ref.at[peer if receiving else me]
        else:
            src, dst = src_ref.at[peer], land_ref.at[r - 1]
        copies.append(pltpu.make_async_remote_copy(
            src_ref=src, dst_ref=dst, send_sem=send_sems.at[r - 1], recv_sem=recv_sems.at[r - 1],
            device_id=(px, py, pc), device_id_type=MESH))
    return copies


def _direct_start(src, land, *, gather, name, after=None):
    def body(src_ref, land_ref, *rest):
        send_sems, recv_sems, token = rest[-5], rest[-4], rest[-1]
        for cp in _direct_copies(src_ref, land_ref, send_sems, recv_sems, gather, False):
            cp.start()
        token[...] = jnp.zeros_like(token)

    afters = () if after is None else (after,)
    return pl.pallas_call(
        body,
        name=name,
        out_shape=(pltpu.SemaphoreType.DMA((N_DEV - 1,)), pltpu.SemaphoreType.DMA((N_DEV - 1,)),
                   pltpu.HBM(src.shape, src.dtype), pltpu.HBM(land.shape, land.dtype),
                   jax.ShapeDtypeStruct((8, 128), F32)),
        in_specs=(HBM, HBM) + tuple(ANY for _ in afters),
        out_specs=(SEM, SEM, HBM, HBM, pl.BlockSpec(memory_space=pltpu.VMEM)),
        input_output_aliases={0: 2, 1: 3},
        compiler_params=pltpu.CompilerParams(has_side_effects=_DATAFLOW),
    )(pltpu.with_memory_space_constraint(src, pltpu.HBM), pltpu.with_memory_space_constraint(land, pltpu.HBM), *afters)


def _direct_wait(send_sems, recv_sems, src_thru, land_thru, after, *, gather, name):
    afters = after if isinstance(after, tuple) else (after,)

    def body(src_ref, land_ref, send_sems_ref, recv_sems_ref, *rest):
        del rest
        for cp in _direct_copies(src_ref, land_ref, send_sems_ref, recv_sems_ref, gather, True):
            cp.wait_send()
            cp.wait_recv()

    return pl.pallas_call(
        body,
        name=name,
        out_shape=(pltpu.HBM(src_thru.shape, src_thru.dtype), pltpu.HBM(land_thru.shape, land_thru.dtype)),
        in_specs=(HBM, HBM, SEM, SEM) + tuple(ANY for _ in afters),
        out_specs=(HBM, HBM),
        input_output_aliases={0: 0, 1: 1},
        compiler_params=pltpu.CompilerParams(has_side_effects=_DATAFLOW),
    )(src_thru, land_thru, send_sems, recv_sems, *afters)


def _sum_partials(src, land, me, *, tr, name, wmv=None):
    R = src.shape[1]
    extra = () if wmv is None else tuple(wmv)

    def body(me_ref, s_ref, l_ref, *rest):
        del me_ref
        acc = s_ref[0].astype(F32)
        for r in range(N_DEV - 1):
            acc = acc + l_ref[r].astype(F32)
        rest[len(extra)][...] = acc
        if extra:
            w_ref, m_ref, v_ref, _, d_ref, nm_ref, nv_ref = rest
            d_ref[...], nm_ref[...], nv_ref[...] = _adam_step(w_ref[...], acc, m_ref[...], v_ref[...])

    row = pl.BlockSpec((tr, 1024), lambda i, mr: (i, 0))
    n_out = 4 if extra else 1
    res = pl.pallas_call(
        body,
        grid_spec=pltpu.PrefetchScalarGridSpec(
            num_scalar_prefetch=1, grid=(R // tr,),
            in_specs=[pl.BlockSpec((1, tr, 1024), lambda i, mr: (mr[0], i, 0)),
                      pl.BlockSpec((N_DEV - 1, tr, 1024), lambda i, mr: (0, i, 0))] + [row for _ in extra],
            out_specs=[row] * n_out),
        out_shape=[jax.ShapeDtypeStruct((R, 1024), F32)] * n_out,
        name=name,
    )(me, src, land, *extra)
    return res if extra else res[0]


_SMALL = ("lb_logits", "hg_norm_gain", "swa_sinks", "rel_bias", "ln1_g", "ln1_b", "ln2_g", "ln2_b")


def _pack_small_grads(d_lb, d_gain, d_sink, d_rb, d_ln1_g, d_ln1_b, d_ln2_g, d_ln2_b, loss):
    def body(lb_ref, gain_ref, sink_ref, rb_ref, l1g_ref, l1b_ref, l2g_ref, l2b_ref, loss_ref, o_ref):
        o_ref[...] = jnp.zeros_like(o_ref)
        for row, ref in ((SM_LB, lb_ref), (SM_GAIN, gain_ref), (SM_L1G, l1g_ref), (SM_L1B, l1b_ref),
                         (SM_L2G, l2g_ref), (SM_L2B, l2b_ref)):
            o_ref[row:row + 1, :] = ref[...]
        o_ref[SM_SINK:SM_SINK + 1, 0:128] = sink_ref[0:1, :]
        o_ref[SM_LOSS:SM_LOSS + 1, 0:128] = loss_ref[0:1, :]
        o_ref[SM_RB:SM_RB + NUM_BUCKETS, 0:128] = rb_ref[...]

    vm = pl.BlockSpec(memory_space=pltpu.VMEM)
    return pl.pallas_call(
        body,
        in_specs=[vm] * 9,
        out_specs=vm,
        out_shape=jax.ShapeDtypeStruct((SM_ROWS, D_MODEL), F32),
        name="pack_small_grads",
    )(d_lb, d_gain, d_sink, d_rb, d_ln1_g, d_ln1_b, d_ln2_g, d_ln2_b, loss)


def _small_finish(gathered, w, m, v):
    n = len(_SMALL)

    def body(*refs):
        g_ref = refs[0]
        w_refs, m_refs, v_refs = refs[1:1 + n], refs[1 + n:1 + 2 * n], refs[1 + 2 * n:1 + 3 * n]
        outs = refs[1 + 3 * n:]
        loss_ref, tot = outs[0], outs[-1]
        g_out, d_out, m_out, v_out = (outs[1 + k * n:1 + (k + 1) * n] for k in range(4))
        acc = g_ref[0]
        for d in range(1, N_DEV):
            acc = acc + g_ref[d]
        tot[...] = acc
        loss_ref[...] = tot[SM_LOSS:SM_LOSS + 1, 0:1]
        lb = _lower_bound(w_refs[0])
        dl0 = tot[SM_LB:SM_LB + 1, :] * lb * (1.0 - lb)
        grads = (jnp.concatenate([dl0, -dl0], axis=0), tot[SM_GAIN:SM_GAIN + 1, :],
                 tot[SM_SINK:SM_SINK + 1, 0:SWA_HEADS], tot[SM_RB:SM_RB + NUM_BUCKETS, 0:SWA_HEADS],
                 tot[SM_L1G:SM_L1G + 1, :], tot[SM_L1B:SM_L1B + 1, :], tot[SM_L2G:SM_L2G + 1, :], tot[SM_L2B:SM_L2B + 1, :])
        for k, g in enumerate(grads):
            g_out[k][...] = g
            d_out[k][...], m_out[k][...], v_out[k][...] = _adam_step(w_refs[k][...], g, m_refs[k][...], v_refs[k][...])

    vm = pl.BlockSpec(memory_space=pltpu.VMEM)
    shapes = [jax.ShapeDtypeStruct(w[k].shape, F32) for k in _SMALL]
    res = pl.pallas_call(
        body,
        in_specs=[vm] * (1 + 3 * n),
        out_specs=[vm] * (1 + 4 * n),
        out_shape=[jax.ShapeDtypeStruct((1, 1), F32)] + shapes * 4,
        scratch_shapes=[pltpu.VMEM((SM_ROWS, D_MODEL), F32)],
        name="small_finish",
    )(gathered, *[w[k] for k in _SMALL], *[m[k] for k in _SMALL], *[v[k] for k in _SMALL])
    parts = [dict(zip(_SMALL, res[1 + k * n:1 + (k + 1) * n])) for k in range(4)]
    return (res[0], *parts)


def _adam_step(w, g, m, v):
    nm = ADAM_B1 * m + (1.0 - ADAM_B1) * g
    nv = ADAM_B2 * v + (1.0 - ADAM_B2) * jnp.square(g)
    m_hat = nm / (1.0 - ADAM_B1 ** ADAM_STEP)
    v_hat = nv / (1.0 - ADAM_B2 ** ADAM_STEP)
    return -ADAM_LR * (m_hat / (jnp.sqrt(v_hat) + ADAM_EPS) + ADAM_WD * w), nm, nv


def _adamw(w, g, m, v, *, tr, name):
    R, C = w.shape

    def body(w_ref, g_ref, m_ref, v_ref, d_ref, nm_ref, nv_ref):
        d_ref[...], nm_ref[...], nv_ref[...] = _adam_step(w_ref[...], g_ref[...], m_ref[...], v_ref[...])

    spec = pl.BlockSpec((tr, C), lambda i: (i, 0))
    return pl.pallas_call(
        body,
        grid=(R // tr,),
        in_specs=[spec] * 4,
        out_specs=[spec] * 3,
        out_shape=[jax.ShapeDtypeStruct((R, C), F32)] * 3,
        compiler_params=_cparams("parallel"),
        name=name,
    )(w, g, m, v)


_WEIGHTS = ("w_in", "lb_logits", "hg_norm_gain", "swa_sinks", "rel_bias", "w_mem_kv", "w_branch_hg", "w_branch_swa",
            "w_branch_mem", "w_out", "ln1_g", "ln1_b", "w_up", "w_down", "ln2_g", "ln2_b")


def kernel(x, mem, w_in, lb_logits, hg_norm_gain, swa_sinks, rel_bias, w_mem_kv, w_branch_hg, w_branch_swa, w_branch_mem, w_out, ln1_g, ln1_b, w_up, w_down, ln2_g, ln2_b, loss_target, m_w_in, m_lb_logits, m_hg_norm_gain, m_swa_sinks, m_rel_bias, m_w_mem_kv, m_w_branch_hg, m_w_branch_swa, m_w_branch_mem, m_w_out, m_ln1_g, m_ln1_b, m_w_up, m_w_down, m_ln2_g, m_ln2_b, v_w_in, v_lb_logits, v_hg_norm_gain, v_swa_sinks, v_rel_bias, v_w_mem_kv, v_w_branch_hg, v_w_branch_swa, v_w_branch_mem, v_w_out, v_ln1_g, v_ln1_b, v_w_up, v_w_down, v_ln2_g, v_ln2_b):
    w = dict(w_in=w_in, lb_logits=lb_logits, hg_norm_gain=hg_norm_gain, swa_sinks=swa_sinks, rel_bias=rel_bias,
             w_mem_kv=w_mem_kv, w_branch_hg=w_branch_hg, w_branch_swa=w_branch_swa, w_branch_mem=w_branch_mem,
             w_out=w_out, ln1_g=ln1_g, ln1_b=ln1_b, w_up=w_up, w_down=w_down, ln2_g=ln2_g, ln2_b=ln2_b)
    mom = dict(w_in=m_w_in, lb_logits=m_lb_logits, hg_norm_gain=m_hg_norm_gain, swa_sinks=m_swa_sinks, rel_bias=m_rel_bias,
               w_mem_kv=m_w_mem_kv, w_branch_hg=m_w_branch_hg, w_branch_swa=m_w_branch_swa, w_branch_mem=m_w_branch_mem,
               w_out=m_w_out, ln1_g=m_ln1_g, ln1_b=m_ln1_b, w_up=m_w_up, w_down=m_w_down, ln2_g=m_ln2_g, ln2_b=m_ln2_b)
    var = dict(w_in=v_w_in, lb_logits=v_lb_logits, hg_norm_gain=v_hg_norm_gain, swa_sinks=v_swa_sinks, rel_bias=v_rel_bias,
               w_mem_kv=v_w_mem_kv, w_branch_hg=v_w_branch_hg, w_branch_swa=v_w_branch_swa, w_branch_mem=v_w_branch_mem,
               w_out=v_w_out, ln1_g=v_ln1_g, ln1_b=v_ln1_b, w_up=v_w_up, w_down=v_w_down, ln2_g=v_ln2_g, ln2_b=v_ln2_b)
    xc, yc, cc = _coords()

    p1 = _bf(w_in[0].T)
    p2 = _bf(jnp.concatenate([w_down[0], w_up[0].T, w_branch_hg[0], w_branch_swa[0], w_branch_mem[0], w_out[0],
                              w_mem_kv[0].T], axis=0))
    me = 4 * xc + 2 * yc + cc
    (g1,) = _all_gather_weights(p1)
    land2 = lax.dynamic_update_slice(lax.empty((N_DEV, R_OTHER, D_MODEL), BF16), p2[None], (me, 0, 0))
    ag2 = _direct_start(p2, land2, gather=True, name="gather_other_weights_start")

    def other_weights(after):
        return _direct_wait(*ag2[:4], after, gather=True, name="gather_other_weights_wait")[1]

    blocks = lambda a: a.reshape(N_DEV, a.shape[0] // N_DEV, D_MODEL)
    started = {}

    def send_other_grads(part):
        started["others"] = _direct_start(part, lax.empty((N_DEV - 1, R_OTHER, D_MODEL), BF16), gather=False,
                                          name="scatter_other_grads_start")
        return started["others"][4]

    me1 = me.reshape(1).astype(jnp.int32)
    grads, delta, new_m, new_v = {}, {}, {}, {}

    def adamw(name):
        w2 = w[name][0]
        delta[name], new_m[name], new_v[name] = _adamw(
            w2, grads[name], mom[name][0], var[name][0], tr=w2.shape[0] // 4, name="adamw_" + name)

    def send_small_grads(packed):
        land = lax.dynamic_update_slice(lax.empty((N_DEV, SM_ROWS, D_MODEL), F32), packed[None], (me, 0, 0))
        started["small"] = _direct_start(packed, land, gather=True, name="gather_small_grads_start")
        return started["small"][4]

    def send_win_grad(g, after):
        started["win"] = _direct_start(blocks(g), lax.empty((N_DEV - 1, IN_SHARD, D_MODEL), BF16), gather=False,
                                       name="scatter_w_in_grad_start", after=after)
        mine2, landed2 = _direct_wait(*started["others"][:4], started["win"][4], gather=False,
                                      name="scatter_other_grads_wait")
        gs2 = _sum_partials(mine2, landed2, me1, tr=R_OTHER // 2, name="sum_other_grads")
        grads.update(
            w_down=gs2[R_DN:R_UP], w_up=gs2[R_UP:R_BH].T, w_branch_hg=gs2[R_BH:R_BS], w_branch_swa=gs2[R_BS:R_BM],
            w_branch_mem=gs2[R_BM:R_OUT], w_out=gs2[R_OUT:R_KV], w_mem_kv=gs2[R_KV:R_OTHER].T)
        for name in ("w_mem_kv", "w_branch_hg", "w_branch_swa", "w_branch_mem", "w_out", "w_up", "w_down"):
            adamw(name)
        return tuple(new_v[name] for name in new_v)

    grad_x = _local_step(
        x[0], mem[0], loss_target[0], lb_logits, hg_norm_gain, swa_sinks, rel_bias, ln1_g, ln1_b, ln2_g, ln2_b,
        g1.reshape(IN_COLS, D_MODEL), ag2[4], other_weights, send_other_grads, send_small_grads, send_win_grad)

    mine1, landed1 = _direct_wait(*started["win"][:4], grad_x, gather=False, name="scatter_w_in_grad_wait")
    g_win_t, d_t, m_t, v_t = _sum_partials(mine1, landed1, me1, tr=IN_SHARD // 2, name="sum_adamw_w_in",
                                           wmv=(w_in[0].T, m_w_in[0].T, v_w_in[0].T))
    grads["w_in"], delta["w_in"], new_m["w_in"], new_v["w_in"] = g_win_t.T, d_t.T, m_t.T, v_t.T

    _, gathered = _direct_wait(*started["small"][:4], grad_x, gather=True, name="gather_small_grads_wait")
    loss, g_s, d_s, m_s, v_s = _small_finish(gathered, w, mom, var)
    for dst, src in ((grads, g_s), (delta, d_s), (new_m, m_s), (new_v, v_s)):
        dst.update(src)

    def shaped(d, name):
        return d[name].reshape(w[name].shape)

    return (loss.reshape(()), grad_x[None], *[shaped(grads, n) for n in _WEIGHTS], *[shaped(delta, n) for n in _WEIGHTS],
            *[shaped(new_m, n) for n in _WEIGHTS], *[shaped(new_v, n) for n in _WEIGHTS])
```

```python
import functools
import math

import jax
import jax.numpy as jnp
from jax import lax
from jax.experimental import pallas as pl
from jax.experimental.pallas import tpu as pltpu

F32 = jnp.float32
BF16 = jnp.bfloat16

D_MODEL = 1024
MEM_LEN = 256
HG_HEADS = 8
HG_DK = 128
HG_CHUNK = 64
SWA_HEADS = 16
SWA_HEAD_DIM = 64
SWA_BLOCK = 128
SWA_WINDOW = 128
MEM_HEADS = 4
MEM_HEAD_DIM = 256
NUM_BUCKETS = 32
MAX_DISTANCE = 128
D_FF = 4096
LN_EPS = 1e-5
RMS_EPS = 1e-6
ALPHA = 2.0 ** 0.25
N_DEV = 8

C_HQ, C_HF, C_HI, C_HG, C_SQ, C_SK, C_SV, C_MQ, C_GL = 0, 1024, 2048, 3072, 4096, 5120, 5248, 5376, 6400
IN_COLS = 9472
IN_SHARD = IN_COLS // N_DEV
Z_HG, Z_SQ, Z_SK, Z_MQ, Z_REST = 0, C_SQ - C_HG, C_SK - C_HG, C_MQ - C_HG, C_GL - C_HG

ADAM_LR = 0.001
ADAM_B1 = 0.9
ADAM_B2 = 0.999
ADAM_EPS = 1e-08
ADAM_WD = 0.01
ADAM_STEP = 10

VMEM_LIMIT = 58 * 1024 * 1024

R_DN, R_UP, R_BH, R_BS, R_BM, R_OUT, R_KV, R_OTHER = 0, 512, 1024, 1152, 1280, 1408, 1536, 1792

SM_LB, SM_GAIN, SM_SINK, SM_L1G, SM_L1B, SM_L2G, SM_L2B, SM_LOSS, SM_RB, SM_ROWS = 0, 2, 3, 4, 5, 6, 7, 8, 16, 48


def _bf(v):
    return v.astype(BF16)


def _f32(v):
    return v.astype(F32)


def _dot(a, b):
    return jnp.dot(a, b, preferred_element_type=F32)


def _dot_nt(a, b):
    return lax.dot_general(a, b, (((1,), (1,)), ((), ())), preferred_element_type=F32)


def _dot_tn(a, b):
    return lax.dot_general(a, b, (((0,), (0,)), ((), ())), preferred_element_type=F32)


def _sig(v):
    return 0.5 * jnp.tanh(0.5 * v) + 0.5


def _cparams(*sem):
    return pltpu.CompilerParams(dimension_semantics=sem, vmem_limit_bytes=VMEM_LIMIT)


def _const_spec(shape):
    nd = len(shape)
    return pl.BlockSpec(shape, lambda *_: (0,) * nd, pipeline_mode=pl.Buffered(1))


def _dep_spec():
    return pl.BlockSpec((8, 128), lambda *_: (0, 0))


def _in_proj(x, win_t, dep, *, tm):
    S = x.shape[0]

    def body(x_ref, w_ref, dep_ref, qfv_ref, z_ref, gl_ref, xb_ref):
        del dep_ref
        xb = _bf(x_ref[...])
        xb_ref[...] = xb
        for c0 in range(0, C_HG, 1024):
            qfv_ref[:, c0:c0 + 1024] = _dot_nt(xb, w_ref[c0:c0 + 1024, :])
        for c0 in range(0, Z_REST, Z_REST // 2):
            z_ref[:, c0:c0 + Z_REST // 2] = _bf(_dot_nt(xb, w_ref[C_HG + c0:C_HG + c0 + Z_REST // 2, :]))
        for c0 in range(0, IN_COLS - C_GL, 1024):
            gl_ref[:, c0:c0 + 1024] = _bf(_dot_nt(xb, w_ref[C_GL + c0:C_GL + c0 + 1024, :]))

    row = lambda w: pl.BlockSpec((tm, w), lambda i: (i, 0))
    return pl.pallas_call(
        body,
        grid=(S // tm,),
        in_specs=[row(D_MODEL), _const_spec(win_t.shape), _dep_spec()],
        out_specs=[row(C_HG), row(Z_REST), row(IN_COLS - C_GL), row(D_MODEL)],
        out_shape=[jax.ShapeDtypeStruct((S, C_HG), F32), jax.ShapeDtypeStruct((S, Z_REST), BF16),
                   jax.ShapeDtypeStruct((S, IN_COLS - C_GL), BF16), jax.ShapeDtypeStruct((S, D_MODEL), BF16)],
        compiler_params=_cparams("parallel"),
        name="in_proj",
    )(x, win_t, dep)


def _placement(into, tm, N, M, out_dtype):
    if into is None:
        return (lambda i: (i, 0)), (tm, N), jax.ShapeDtypeStruct((M, N), out_dtype), (), {}
    dest, block, index = into
    assert math.prod(block) == tm * N and dest.dtype == out_dtype
    return index, block, jax.ShapeDtypeStruct(dest.shape, dest.dtype), (dest,), {2: 0}


def _mm_tn_resident(a, b, *, tm, kc, name, out_dtype, into=None):
    K, M = a.shape
    N = b.shape[1]
    nk = K // kc
    index, block, out_shape, extra, aliases = _placement(into, tm, N, M, out_dtype)

    def body(a_ref, b_ref, *rest):
        o_ref = rest[-1]
        acc = jnp.zeros((tm, N), F32)
        for kk in range(nk):
            sl = pl.ds(kk * kc, kc)
            acc = acc + _dot_tn(_bf(a_ref[sl, :]), _bf(b_ref[sl, :]))
        o_ref[...] = acc.astype(o_ref.dtype).reshape(block)

    return pl.pallas_call(
        body,
        grid=(M // tm,),
        in_specs=[pl.BlockSpec((K, tm), lambda i: (0, i)), _const_spec((K, N))] + [ANY for _ in extra],
        out_specs=pl.BlockSpec(block, index),
        out_shape=out_shape,
        input_output_aliases=aliases,
        compiler_params=_cparams("parallel"),
        name=name,
    )(a, b, *extra)


def _mm_tn(a, b, *, kc, name, out_dtype=F32, into=None, tm=None):
    K, M = a.shape
    N = b.shape[1]
    if M > 1024 or tm is not None:
        return _mm_tn_resident(a, b, tm=tm or 256, kc=min(kc, 1024), name=name, out_dtype=out_dtype, into=into)
    tm = M
    nk = K // kc
    index, block, out_shape, extra, aliases = _placement(into, tm, N, M, out_dtype)

    def body(a_ref, b_ref, *rest):
        o_ref, acc = rest[-2], rest[-1]
        k = pl.program_id(1)
        part = _dot_tn(_bf(a_ref[...]), _bf(b_ref[...]))

        @pl.when(k == 0)
        def _():
            acc[...] = part

        @pl.when(k > 0)
        def _():
            acc[...] += part

        @pl.when(k == nk - 1)
        def _():
            o_ref[...] = acc[...].astype(o_ref.dtype).reshape(block)

    return pl.pallas_call(
        body,
        grid=(M // tm, nk),
        in_specs=[pl.BlockSpec((kc, tm), lambda i, k: (k, i)), pl.BlockSpec((kc, N), lambda i, k: (k, 0))]
        + [ANY for _ in extra],
        out_specs=pl.BlockSpec(block, lambda i, k: index(i)),
        out_shape=out_shape,
        input_output_aliases=aliases,
        scratch_shapes=[pltpu.VMEM((tm, N), F32)],
        compiler_params=_cparams("parallel", "arbitrary"),
        name=name,
    )(a, b, *extra)


def _grad_x(d_qfv, d_hg_gl, d_sq, d_skv, d_mq, w_qfv, win_t, add, deps, *, tm):
    M = add.shape[0]
    pieces = (d_qfv, d_hg_gl, d_sq, d_skv, d_mq)

    def body(qfv_ref, hggl_ref, sq_ref, skv_ref, mq_ref, wq_ref, w_ref, add_ref, *rest):
        o_ref = rest[-1]
        acc = add_ref[...] + _dot(qfv_ref[...], wq_ref[...])
        acc = acc + _dot(hggl_ref[:, 0:1024], w_ref[C_HG:C_SQ, :])
        acc = acc + _dot(hggl_ref[:, 1024:4096], w_ref[C_GL:IN_COLS, :])
        acc = acc + _dot(sq_ref[...], w_ref[C_SQ:C_SK, :])
        acc = acc + _dot(skv_ref[...], w_ref[C_SK:C_MQ, :])
        o_ref[...] = acc + _dot(mq_ref[...], w_ref[C_MQ:C_GL, :])

    return pl.pallas_call(
        body,
        grid=(M // tm,),
        in_specs=[pl.BlockSpec((tm, p.shape[1]), lambda i: (i, 0)) for p in pieces]
        + [_const_spec(w_qfv.shape), _const_spec(win_t.shape), pl.BlockSpec((tm, D_MODEL), lambda i: (i, 0))]
        + [_dep_spec() for _ in deps],
        out_specs=pl.BlockSpec((tm, D_MODEL), lambda i: (i, 0)),
        out_shape=jax.ShapeDtypeStruct((M, D_MODEL), F32),
        compiler_params=_cparams("parallel"),
        name="grad_x",
    )(*pieces, w_qfv, win_t, add, *deps)


def _lower_bound(lbl_ref):
    l0 = lbl_ref[0:1, :]
    l1 = lbl_ref[1:2, :]
    mx = jnp.maximum(l0, l1)
    e0 = jnp.exp(l0 - mx)
    e1 = jnp.exp(l1 - mx)
    return e0 / (e0 + e1)


def _tri(lower):
    r = lax.broadcasted_iota(jnp.int32, (HG_CHUNK, HG_CHUNK), 0)
    c = lax.broadcasted_iota(jnp.int32, (HG_CHUNK, HG_CHUNK), 1)
    return (r >= c) if lower else (r <= c)


def _hg_gates(fl, lb):
    sg = _sig(fl)
    f = lb + (1.0 - lb) * sg
    return sg, f, jnp.log(f), 1.0 - f


def _scan_rows(v, reverse=False):
    row = lax.broadcasted_iota(jnp.int32, v.shape, 0)
    s = 1
    while s < HG_CHUNK:
        if reverse:
            v = v + jnp.where(row < HG_CHUNK - s, pltpu.roll(v, HG_CHUNK - s, 0), 0.0)
        else:
            v = v + jnp.where(row >= s, pltpu.roll(v, s, 0), 0.0)
        s *= 2
    return v


def _hgrn_fwd(zmain, lb_logits, *, T):
    S = zmain.shape[0]
    nc = T // HG_CHUNK

    def body(q_ref, f_ref, v_ref, lbl_ref, o_ref, st_ref, state):
        @pl.when(pl.program_id(1) == 0)
        def _():
            state[...] = jnp.zeros_like(state)

        lb = _lower_bound(lbl_ref)
        tril = _tri(True)
        qis, updates, decays, intra = [], [], [], []
        for c in range(nc):
            sl = pl.ds(c * HG_CHUNK, HG_CHUNK)
            _, _, g, k = _hg_gates(_f32(f_ref[sl, :]), lb)
            b = _scan_rows(g)
            bl = jnp.sum(g, axis=0, keepdims=True)
            qi = _bf(_f32(q_ref[sl, :]) * jnp.exp(b))
            ki = _bf(k * jnp.exp(-b))
            ko = _bf(k * jnp.exp(bl - b))
            vb = _bf(v_ref[sl, :])
            att = jnp.where(tril, _dot_nt(qi, ki), 0.0)
            intra.append(_dot(_bf(att), vb))
            qis.append(qi)
            updates.append(_dot_tn(vb, ko))
            decays.append(jnp.exp(bl))
        st = state[...]
        for c in range(nc):
            st_ref[0, c] = st
            o_ref[pl.ds(c * HG_CHUNK, HG_CHUNK), :] = intra[c] + _dot_nt(qis[c], _bf(st))
            st = st * decays[c] + updates[c]
        state[...] = st

    col = lambda base: pl.BlockSpec((T, HG_DK), lambda h, t: (t, base + h))
    return pl.pallas_call(
        body,
        grid=(HG_HEADS, S // T),
        in_specs=[col(0), col(8), col(16), pl.BlockSpec((2, HG_DK), lambda h, t: (0, h))],
        out_specs=[
            pl.BlockSpec((T, HG_DK), lambda h, t: (t, h)),
            pl.BlockSpec((1, nc, HG_DK, HG_DK), lambda h, t: (h, t, 0, 0)),
        ],
        out_shape=[
            jax.ShapeDtypeStruct((S, D_MODEL), F32),
            jax.ShapeDtypeStruct((HG_HEADS, S // HG_CHUNK, HG_DK, HG_DK), F32),
        ],
        scratch_shapes=[pltpu.VMEM((HG_DK, HG_DK), F32)],
        compiler_params=_cparams("parallel", "arbitrary"),
        name="hgrn_fwd",
    )(zmain, zmain, zmain, lb_logits)


def _hgrn_bwd(zmain, lb_logits, states, d_o, *, T):
    S = zmain.shape[0]
    nc = T // HG_CHUNK
    nt = S // T

    def body(q_ref, f_ref, v_ref, lbl_ref, st_ref, do_ref, dz_ref, dlb_ref, dstate):
        @pl.when(pl.program_id(1) == 0)
        def _():
            dstate[...] = jnp.zeros_like(dstate)
            dlb_ref[...] = jnp.zeros_like(dlb_ref)

        lb = _lower_bound(lbl_ref)
        tril = _tri(True)
        last_row = lax.broadcasted_iota(jnp.int32, (HG_CHUNK, HG_DK), 0) == HG_CHUNK - 1
        saved = []
        for c in range(nc):
            sl = pl.ds(c * HG_CHUNK, HG_CHUNK)
            sg, f, g, k = _hg_gates(_f32(f_ref[sl, :]), lb)
            b = _scan_rows(g)
            bl = jnp.sum(g, axis=0, keepdims=True)
            eb = jnp.exp(b)
            enb = jnp.exp(-b)
            eo = jnp.exp(bl - b)
            q_in = _f32(q_ref[sl, :]) * eb
            k_in = k * enb
            k_out = k * eo
            qi, ki, ko = _bf(q_in), _bf(k_in), _bf(k_out)
            vb = _bf(v_ref[sl, :])
            dob = do_ref[sl, :]
            att = jnp.where(tril, _dot_nt(qi, ki), 0.0)
            d_att = _bf(jnp.where(tril, _dot_nt(dob, vb), 0.0))
            d_kin = _dot_tn(d_att, qi)
            saved.append(dict(
                sg=sg, f=f, eb=eb, enb=enb, eo=eo, ebl=jnp.exp(bl), k_out=k_out, ko=ko, vb=vb, dob=dob,
                d_v=_dot_tn(_bf(att), dob), d_qin=_dot(d_att, ki), d_kin=d_kin,
                qk=(q_in, k_in), d_state=_dot_tn(dob, qi)))
        dst = dstate[...]
        dsts = [None] * nc
        for c in reversed(range(nc)):
            dsts[c] = dst
            dst = dst * saved[c]["ebl"] + saved[c]["d_state"]
        dstate[...] = dst
        dlb = jnp.zeros((1, HG_DK), F32)
        for c in range(nc):
            sl = pl.ds(c * HG_CHUNK, HG_CHUNK)
            s = saved[c]
            q_in, k_in = s["qk"]
            st = st_ref[0, c]
            dstb = _bf(dsts[c])
            d_v = s["d_v"] + _dot_nt(s["ko"], dstb)
            d_qin = s["d_qin"] + _dot(s["dob"], _bf(st))
            d_kout = _dot(s["vb"], dstb)
            d_decay = jnp.sum(dsts[c] * st, axis=0, keepdims=True)
            kk = d_kout * s["k_out"]
            d_b = d_qin * q_in - s["d_kin"] * k_in - kk
            d_bl = jnp.sum(kk, axis=0, keepdims=True) + d_decay * s["ebl"]
            d_g = _scan_rows(d_b + jnp.where(last_row, d_bl, 0.0), reverse=True)
            d_f = d_g / s["f"] - (s["d_kin"] * s["enb"] + d_kout * s["eo"])
            dz_ref[sl, 0:HG_DK] = _bf(d_qin * s["eb"])
            dz_ref[sl, HG_DK:2 * HG_DK] = _bf(d_f * (1.0 - lb) * s["sg"] * (1.0 - s["sg"]))
            dz_ref[sl, 2 * HG_DK:3 * HG_DK] = _bf(d_v)
            dlb = dlb + jnp.sum(d_f * (1.0 - s["sg"]), axis=0, keepdims=True)
        dlb_ref[...] += dlb

    rev = lambda base: pl.BlockSpec((T, HG_DK), lambda h, t: (nt - 1 - t, base + h))
    outc = pl.BlockSpec((T, HG_DK), lambda h, t: (nt - 1 - t, h))
    return pl.pallas_call(
        body,
        grid=(HG_HEADS, nt),
        in_specs=[
            rev(0), rev(8), rev(16),
            pl.BlockSpec((2, HG_DK), lambda h, t: (0, h)),
            pl.BlockSpec((1, nc, HG_DK, HG_DK), lambda h, t: (h, nt - 1 - t, 0, 0)),
            outc,
        ],
        out_specs=[pl.BlockSpec((T, 3 * HG_DK), lambda h, t: (nt - 1 - t, h)),
                   pl.BlockSpec((1, HG_DK), lambda h, t: (0, h))],
        out_shape=[jax.ShapeDtypeStruct((S, 3 * D_MODEL), BF16), jax.ShapeDtypeStruct((1, D_MODEL), F32)],
        scratch_shapes=[pltpu.VMEM((HG_DK, HG_DK), F32)],
        compiler_params=_cparams("parallel", "arbitrary"),
        name="hgrn_bwd",
    )(zmain, zmain, zmain, lb_logits, states, d_o)


def _t5_bucket_table():
    qi = jnp.arange(SWA_BLOCK)[:, None] + SWA_BLOCK
    kj = jnp.arange(2 * SWA_BLOCK)[None, :]
    n = jnp.clip(qi - kj, 0, SWA_WINDOW - 1)
    max_exact = NUM_BUCKETS // 2
    nf = jnp.maximum(n, 1).astype(F32)
    large = max_exact + (jnp.log(nf / max_exact) / math.log(MAX_DISTANCE / max_exact)
                         * (NUM_BUCKETS - max_exact)).astype(jnp.int32)
    large = jnp.minimum(large, NUM_BUCKETS - 1)
    return jnp.where(n < max_exact, n, large).astype(jnp.int32)


SWA_ROWS = 32
MERGE_GROUPS = 2


def _swa_bias_init(bias, bucket_ref, rb_ref):
    bk = bucket_ref[...]
    qi = lax.broadcasted_iota(jnp.int32, bk.shape, 0) + SWA_BLOCK
    kj = lax.broadcasted_iota(jnp.int32, bk.shape, 1)
    band = (qi - kj >= 0) & (qi - kj < SWA_WINDOW)
    for h in range(SWA_HEADS):
        def sel(b, acc, h=h):
            return jnp.where(bk == b, rb_ref[b, h], acc)
        t = lax.fori_loop(0, NUM_BUCKETS, sel, jnp.zeros(bk.shape, F32))
        bias[1, h] = jnp.where(band, t, -jnp.inf)
        bias[0, h] = jnp.where(band & (kj >= SWA_BLOCK), t, -jnp.inf)


def _lane_halves(t, kv_head):
    lane = lax.broadcasted_iota(jnp.int32, t.shape, 1)
    rolled = pltpu.roll(t, 64, 1)
    zero = jnp.zeros_like(t)
    if kv_head == 0:
        return jnp.where(lane < 64, t, zero), jnp.where(lane >= 64, rolled, zero)
    return jnp.where(lane < 64, rolled, zero), jnp.where(lane >= 64, t, zero)


def _swa_zero_key0(t):
    return jnp.where(lax.broadcasted_iota(jnp.int32, t.shape, 0) == 0, jnp.zeros_like(t), t)


def _swa_probs(s, masked_bias, sink):
    s = s + masked_bias
    m = jnp.maximum(jnp.max(s, axis=-1, keepdims=True), sink)
    p = jnp.exp(s - m)
    es = jnp.exp(sink - m)
    inv = 1.0 / (jnp.sum(p, axis=-1, keepdims=True) + es)
    return p * inv, es * inv


def _swa_fwd(zmain, bucket, rel_bias, sinks):
    S = zmain.shape[0]
    nb = S // SWA_BLOCK
    scale = SWA_HEAD_DIM ** -0.5

    def body(q_ref, kvc_ref, kvp_ref, bucket_ref, rb_ref, sk_ref, o_ref, p_ref, bias):
        n = pl.program_id(0)

        @pl.when(n == 0)
        def _():
            _swa_bias_init(bias, bucket_ref, rb_ref)

        later = jnp.minimum(n, 1)
        kk = _bf(jnp.concatenate([kvp_ref[:, 0:128], kvc_ref[:, 0:128]], axis=0))
        vv = _swa_zero_key0(_bf(jnp.concatenate([kvp_ref[:, 128:256], kvc_ref[:, 128:256]], axis=0)))
        first_col = lax.broadcasted_iota(jnp.int32, (SWA_ROWS, 2 * SWA_BLOCK), 1) == 0
        scores, values = {}, {}
        for kvh in range(2):
            qst = _bf(jnp.concatenate([q_ref[:, pl.ds((kvh * 4 + jj) * 128, 128)] for jj in range(4)], axis=0) * scale)
            values[kvh] = _lane_halves(vv, kvh)
            for odd, kx in enumerate(_lane_halves(kk, kvh)):
                scores[kvh, odd] = _dot_nt(qst, kx)
        probs = {}
        for (kvh, odd), s in scores.items():
            parts = []
            for jj in range(4):
                h = 2 * (kvh * 4 + jj) + odd
                for r0 in range(0, SWA_BLOCK, SWA_ROWS):
                    p, ps = _swa_probs(s[jj * SWA_BLOCK + r0:jj * SWA_BLOCK + r0 + SWA_ROWS],
                                       bias[later, h, pl.ds(r0, SWA_ROWS), :], sk_ref[0, h])
                    part = _bf(jnp.where(first_col, ps, p))
                    p_ref[pl.ds(r0, SWA_ROWS), pl.ds(h * 2 * SWA_BLOCK, 2 * SWA_BLOCK)] = part
                    parts.append(part)
            probs[kvh, odd] = jnp.concatenate(parts, axis=0)
        for kvh in range(2):
            ost = _dot(probs[kvh, 0], values[kvh][0]) + _dot(probs[kvh, 1], values[kvh][1])
            for jj in range(4):
                o_ref[:, pl.ds((kvh * 4 + jj) * 128, 128)] = ost[jj * SWA_BLOCK:(jj + 1) * SWA_BLOCK]

    smem = pl.BlockSpec(memory_space=pltpu.SMEM)
    return pl.pallas_call(
        body,
        grid=(nb,),
        in_specs=[
            pl.BlockSpec((SWA_BLOCK, 1024), lambda n: (n, Z_SQ // 1024)),
            pl.BlockSpec((SWA_BLOCK, 256), lambda n: (n, Z_SK // 256)),
            pl.BlockSpec((SWA_BLOCK, 256), lambda n: (jnp.maximum(n - 1, 0), Z_SK // 256)),
            _const_spec((SWA_BLOCK, 2 * SWA_BLOCK)), smem, smem,
        ],
        out_specs=[pl.BlockSpec((SWA_BLOCK, 1024), lambda n: (n, 0)),
                   pl.BlockSpec((SWA_BLOCK, SWA_HEADS * 2 * SWA_BLOCK), lambda n: (n, 0))],
        out_shape=[jax.ShapeDtypeStruct((S, 1024), F32),
                   jax.ShapeDtypeStruct((S, SWA_HEADS * 2 * SWA_BLOCK), BF16)],
        scratch_shapes=[pltpu.VMEM((2, SWA_HEADS, SWA_BLOCK, 2 * SWA_BLOCK), F32)],
        compiler_params=_cparams("arbitrary"),
        name="swa_fwd",
    )(zmain, zmain, zmain, bucket, rel_bias, sinks)


def _swa_bwd(zmain, o_b, probs, d_o, bucket, dep):
    S = zmain.shape[0]
    nb = S // SWA_BLOCK
    scale = SWA_HEAD_DIM ** -0.5

    def body(q_ref, kvc_ref, kvp_ref, o_ref, p_ref, do_ref, bucket_ref, dep_ref,
             dq_ref, dkv_ref, drb_ref, dsk_ref, dbias, carry):
        del dep_ref
        n = pl.program_id(0)

        @pl.when(n == 0)
        def _():
            dbias[...] = jnp.zeros_like(dbias)
            carry[...] = jnp.zeros_like(carry)

        @pl.when(n < nb)
        def _():
            kk = _swa_zero_key0(_bf(jnp.concatenate([kvp_ref[:, 0:128], kvc_ref[:, 0:128]], axis=0)))
            vv = _swa_zero_key0(_bf(jnp.concatenate([kvp_ref[:, 128:256], kvc_ref[:, 128:256]], axis=0)))
            lane = lax.broadcasted_iota(jnp.int32, (2 * SWA_BLOCK, 128), 1)
            lane_q = lax.broadcasted_iota(jnp.int32, (4 * SWA_BLOCK, 128), 1)
            pair_cols = {kvh: [pl.ds((kvh * 4 + jj) * 128, 128) for jj in range(4)] for kvh in range(2)}
            qst, dost, ks, d_p, delta = {}, {}, {}, {}, {}
            for kvh in range(2):
                qst[kvh] = _bf(jnp.concatenate([q_ref[:, cl] for cl in pair_cols[kvh]], axis=0) * scale)
                dost[kvh] = jnp.concatenate([do_ref[:, cl] for cl in pair_cols[kvh]], axis=0)
                prod = dost[kvh].astype(F32) * jnp.concatenate([o_ref[:, cl] for cl in pair_cols[kvh]], axis=0)
                ks[kvh] = _lane_halves(kk, kvh)
                for odd, vx in enumerate(_lane_halves(vv, kvh)):
                    keep = (lane_q >= 64) if odd else (lane_q < 64)
                    delta[kvh, odd] = jnp.sum(jnp.where(keep, prod, 0.0), axis=-1, keepdims=True)
                    d_p[kvh, odd] = _dot_nt(dost[kvh], vx)
            pst, dsst = {}, {}
            for (kvh, odd), dp in d_p.items():
                p_parts, ds_parts = [], []
                for jj in range(4):
                    h = 2 * (kvh * 4 + jj) + odd
                    rows = slice(jj * SWA_BLOCK, (jj + 1) * SWA_BLOCK)
                    p = p_ref[:, pl.ds(h * 2 * SWA_BLOCK, 2 * SWA_BLOCK)]
                    ds = _f32(p) * (dp[rows] - delta[kvh, odd][rows])
                    dbias[h] += ds
                    p_parts.append(p)
                    ds_parts.append(_bf(ds))
                pst[kvh, odd] = jnp.concatenate(p_parts, axis=0)
                dsst[kvh, odd] = jnp.concatenate(ds_parts, axis=0)
            dk_parts, dv_parts = [], []
            for kvh in range(2):
                dq_st = _dot(dsst[kvh, 0], ks[kvh][0]) + _dot(dsst[kvh, 1], ks[kvh][1])
                for jj in range(4):
                    dq_ref[:, pair_cols[kvh][jj]] = _bf(dq_st[jj * SWA_BLOCK:(jj + 1) * SWA_BLOCK] * scale)
                zk = jnp.where(lane < 64, _dot_tn(dsst[kvh, 0], qst[kvh]), _dot_tn(dsst[kvh, 1], qst[kvh]))
                zv = jnp.where(lane < 64, _dot_tn(pst[kvh, 0], dost[kvh]), _dot_tn(pst[kvh, 1], dost[kvh]))
                dk_parts.append(zk + pltpu.roll(zk, 64, 1))
                dv_parts.append(zv + pltpu.roll(zv, 64, 1))
            dk = jnp.where(lane < 64, dk_parts[0], dk_parts[1])
            dv = jnp.where(lane < 64, dv_parts[0], dv_parts[1])
            dkv = _swa_zero_key0(jnp.concatenate([dk, dv], axis=1))
            dkv_ref[...] = _bf(carry[...] + dkv[0:SWA_BLOCK])
            carry[...] = dkv[SWA_BLOCK:]

        @pl.when(n == nb)
        def _():
            dkv_ref[...] = _bf(carry[...])
            first_col = lax.broadcasted_iota(jnp.int32, (SWA_BLOCK, 2 * SWA_BLOCK), 1) == 0
            bk = jnp.where(first_col, -1, bucket_ref[...])

            row = lax.broadcasted_iota(jnp.int32, (NUM_BUCKETS, 128), 0)
            lane = lax.broadcasted_iota(jnp.int32, (NUM_BUCKETS, 128), 1)

            def total(v):
                return jnp.sum(jnp.sum(v, axis=1, keepdims=True), axis=0, keepdims=True)

            def per_head(h, acc):
                db = dbias[h]
                d_rb, d_sk = acc
                d_sk = d_sk + jnp.where((row == 0) & (lane == h), total(jnp.where(first_col, db, 0.0)), 0.0)

                def per_bucket(b, d_rb):
                    return d_rb + jnp.where((row == b) & (lane == h), total(jnp.where(bk == b, db, 0.0)), 0.0)

                return lax.fori_loop(0, NUM_BUCKETS, per_bucket, d_rb), d_sk

            zero = jnp.zeros((NUM_BUCKETS, 128), F32)
            d_rb, d_sk = lax.fori_loop(0, SWA_HEADS, per_head, (zero, zero))
            drb_ref[...] = d_rb
            dsk_ref[...] = d_sk[0:8]

    cur = lambda n: jnp.minimum(n, nb - 1)
    prev = lambda n: jnp.maximum(jnp.minimum(n, nb - 1) - 1, 0)
    return pl.pallas_call(
        body,
        grid=(nb + 1,),
        in_specs=[
            pl.BlockSpec((SWA_BLOCK, 1024), lambda n: (cur(n), Z_SQ // 1024)),
            pl.BlockSpec((SWA_BLOCK, 256), lambda n: (cur(n), Z_SK // 256)),
            pl.BlockSpec((SWA_BLOCK, 256), lambda n: (prev(n), Z_SK // 256)),
            pl.BlockSpec((SWA_BLOCK, 1024), lambda n: (cur(n), 0)),
            pl.BlockSpec((SWA_BLOCK, SWA_HEADS * 2 * SWA_BLOCK), lambda n: (cur(n), 0)),
            pl.BlockSpec((SWA_BLOCK, 1024), lambda n: (cur(n), 0)),
            _const_spec((SWA_BLOCK, 2 * SWA_BLOCK)), _dep_spec(),
        ],
        out_specs=[
            pl.BlockSpec((SWA_BLOCK, 1024), lambda n: (cur(n), 0)),
            pl.BlockSpec((SWA_BLOCK, 256), lambda n: (jnp.maximum(n - 1, 0), 0)),
            pl.BlockSpec((NUM_BUCKETS, 128), lambda n: (0, 0)),
            pl.BlockSpec((8, 128), lambda n: (0, 0)),
        ],
        out_shape=[
            jax.ShapeDtypeStruct((S, 1024), BF16),
            jax.ShapeDtypeStruct((S, 256), BF16),
            jax.ShapeDtypeStruct((NUM_BUCKETS, 128), F32),
            jax.ShapeDtypeStruct((8, 128), F32),
        ],
        scratch_shapes=[
            pltpu.VMEM((SWA_HEADS, SWA_BLOCK, 2 * SWA_BLOCK), F32),
            pltpu.VMEM((SWA_BLOCK, 256), F32),
        ],
        compiler_params=_cparams("arbitrary"),
        name="swa_bwd",
    )(zmain, zmain, zmain, o_b, probs, d_o, bucket, dep)


def _mem_q_specs(T):
    return [pl.BlockSpec((T, MEM_HEAD_DIM), lambda t, h=h: (t, Z_MQ // MEM_HEAD_DIM + h)) for h in range(MEM_HEADS)]


def _mem_kv_proj(mem, g2):
    def body(mem_ref, w_ref, o_ref):
        o_ref[...] = _dot_nt(_bf(mem_ref[...]), _rows(w_ref))

    return pl.pallas_call(
        body,
        grid=(1,),
        in_specs=[pl.BlockSpec((MEM_LEN, D_MODEL), lambda i: (0, 0)), _gathered_spec(R_KV, R_OTHER)],
        out_specs=pl.BlockSpec((MEM_LEN, 2048), lambda i: (0, 0)),
        out_shape=jax.ShapeDtypeStruct((MEM_LEN, 2048), F32),
        compiler_params=_cparams("arbitrary"),
        name="mem_kv_proj",
    )(mem, g2)


def _mem_fwd(zmain, mkv, *, T):
    S = zmain.shape[0]

    def body(q0, q1, q2, q3, kv_ref, o_ref, p_ref):
        heads = [pl.ds(h * MEM_HEAD_DIM, MEM_HEAD_DIM) for h in range(MEM_HEADS)]
        scores = [_dot_nt(_bf(q_ref[...] * (MEM_HEAD_DIM ** -0.5)), _bf(kv_ref[:, cols]))
                  for q_ref, cols in zip((q0, q1, q2, q3), heads)]
        probs = []
        for s, cols in zip(scores, heads):
            e = jnp.exp(s - jnp.max(s, axis=-1, keepdims=True))
            pb = _bf(e * (1.0 / jnp.sum(e, axis=-1, keepdims=True)))
            p_ref[:, cols] = pb
            probs.append(pb)
        for h, (pb, cols) in enumerate(zip(probs, heads)):
            o_ref[:, cols] = _dot(pb, _bf(kv_ref[:, pl.ds(1024 + h * MEM_HEAD_DIM, MEM_HEAD_DIM)]))

    row = pl.BlockSpec((T, 1024), lambda t: (t, 0))
    return pl.pallas_call(
        body,
        grid=(S // T,),
        in_specs=_mem_q_specs(T) + [_const_spec((MEM_LEN, 2048))],
        out_specs=[row, row],
        out_shape=[jax.ShapeDtypeStruct((S, 1024), F32), jax.ShapeDtypeStruct((S, 1024), BF16)],
        compiler_params=_cparams("parallel"),
        name="mem_fwd",
    )(zmain, zmain, zmain, zmain, mkv)


def _mem_bwd(zmain, mkv, o_c, probs, d_o, *, T):
    S = zmain.shape[0]
    scale = MEM_HEAD_DIM ** -0.5

    def body(q0, q1, q2, q3, kv_ref, o_ref, p_ref, do_ref, dq_ref, dkv_ref):
        @pl.when(pl.program_id(0) == 0)
        def _():
            dkv_ref[...] = jnp.zeros_like(dkv_ref)

        heads = [(pl.ds(h * MEM_HEAD_DIM, MEM_HEAD_DIM), pl.ds(1024 + h * MEM_HEAD_DIM, MEM_HEAD_DIM))
                 for h in range(MEM_HEADS)]
        d_p = [_dot_nt(do_ref[:, cols], _bf(kv_ref[:, vcols])) for cols, vcols in heads]
        d_s = []
        for dp, (cols, _) in zip(d_p, heads):
            delta = jnp.sum(do_ref[:, cols].astype(F32) * o_ref[:, cols], axis=-1, keepdims=True)
            d_s.append(_bf(_f32(p_ref[:, cols]) * (dp - delta)))
        for ds, q_ref, (cols, vcols) in zip(d_s, (q0, q1, q2, q3), heads):
            dq_ref[:, cols] = _bf(_dot(ds, _bf(kv_ref[:, cols])) * scale)
            dkv_ref[:, cols] += _dot_tn(ds, _bf(q_ref[...] * scale))
            dkv_ref[:, vcols] += _dot_tn(p_ref[:, cols], do_ref[:, cols])

    row = pl.BlockSpec((T, 1024), lambda t: (t, 0))
    return pl.pallas_call(
        body,
        grid=(S // T,),
        in_specs=_mem_q_specs(T) + [_const_spec((MEM_LEN, 2048)), row, row, row],
        out_specs=[row, pl.BlockSpec((MEM_LEN, 2048), lambda t: (0, 0))],
        out_shape=[jax.ShapeDtypeStruct((S, 1024), BF16), jax.ShapeDtypeStruct((MEM_LEN, 2048), F32)],
        compiler_params=_cparams("arbitrary"),
        name="mem_bwd",
    )(zmain, zmain, zmain, zmain, mkv, o_c, probs, d_o)


def _layer_norm(u):
    mu = jnp.mean(u, axis=-1, keepdims=True)
    xc = u - mu
    rstd = lax.rsqrt(jnp.mean(xc * xc, axis=-1, keepdims=True) + LN_EPS)
    return xc * rstd, rstd


def _layer_norm_bwd(dy, gamma, xhat, rstd):
    dxh = dy * gamma
    return rstd * (dxh - jnp.mean(dxh, axis=-1, keepdims=True) - xhat * jnp.mean(dxh * xhat, axis=-1, keepdims=True))


def _merge_stages(rows, oraw_ref, hg_ref, ob_ref, oc_ref, gl_ref, x_ref, gain_ref, wbh, wbs, wbm, wout, g_ref, b_ref,
                  fwd_out=None, bwd=None, saved=None):
    ys, rs = [], []
    for h in range(HG_HEADS):
        oh = oraw_ref[rows, pl.ds(h * HG_DK, HG_DK)]
        r = lax.rsqrt(jnp.mean(oh * oh, axis=-1, keepdims=True) + RMS_EPS)
        ys.append(oh * r)
        rs.append(r)
    y = jnp.concatenate(ys, axis=1)
    hg = _f32(hg_ref[rows, :])
    sg = _sig(hg)
    silu = hg * sg
    gain = gain_ref[...]
    gates = [_sig(_f32(gl_ref[rows, pl.ds(i * 1024, 1024)])) for i in range(3)]
    if saved is None:
        oa = _bf(y * gain * silu)
        pa = _dot(oa, _rows(wbh))
        pb = _dot(_bf(ob_ref[rows, :]), _rows(wbs))
        pc = _dot(_bf(oc_ref[rows, :]), _rows(wbm))
        yield
        m = _bf(gates[0] * pa + gates[1] * pb + gates[2] * pc)
    else:
        pa, pb, pc = (_f32(r[rows, :]) for r in saved[:3])
        m = saved[3][rows, :]
    mix = _dot(m, _rows(wout))
    yield
    xhat, rstd = _layer_norm(ALPHA * x_ref[rows, :] + mix)
    if bwd is None:
        h1 = xhat * g_ref[...] + b_ref[...]
        fwd_out[0][rows, :] = h1
        fwd_out[1][rows, :] = _bf(h1)
        for ref, val in zip(fwd_out[2:], (_bf(pa), _bf(pb), _bf(pc), m, oa)):
            ref[rows, :] = val
        return
    (dh1_ref, dx_ref, du1_ref, dpa_ref, dpb_ref, dpc_ref, doraw_ref, dob_ref, doc_ref, dz_ref,
     dgain_ref, dg_ref, db_ref) = bwd
    dh1 = dh1_ref[rows, :]
    dg_ref[...] += jnp.sum(dh1 * xhat, axis=0, keepdims=True)
    db_ref[...] += jnp.sum(dh1, axis=0, keepdims=True)
    du1 = _layer_norm_bwd(dh1, g_ref[...], xhat, rstd)
    dx_ref[rows, :] = ALPHA * du1
    du1b = _bf(du1)
    du1_ref[rows, :] = du1b
    dm = _dot_nt(du1b, _rows(wout))
    yield
    d_branches = []
    for i, (g, p, dp_ref, w_r) in enumerate(zip(gates, (pa, pb, pc), (dpa_ref, dpb_ref, dpc_ref), (wbh, wbs, wbm))):
        dz_ref[rows, pl.ds((i + 1) * 1024, 1024)] = _bf(dm * p * g * (1.0 - g))
        dp = _bf(dm * g)
        dp_ref[rows, :] = dp
        d_branches.append(_dot_nt(dp, _rows(w_r)))
    yield
    doa, d_ob, d_oc = d_branches
    dob_ref[rows, :] = _bf(d_ob)
    doc_ref[rows, :] = _bf(d_oc)
    t = doa * y
    dgain_ref[...] += jnp.sum(t * silu, axis=0, keepdims=True)
    dz_ref[rows, 0:1024] = _bf(t * gain * sg * (1.0 + hg * (1.0 - sg)))
    dy = doa * gain * silu
    for h in range(HG_HEADS):
        cols = slice(h * HG_DK, (h + 1) * HG_DK)
        yh = y[:, cols]
        dyh = dy[:, cols]
        doraw_ref[rows, pl.ds(h * HG_DK, HG_DK)] = _bf(rs[h] * (dyh - yh * jnp.mean(dyh * yh, axis=-1, keepdims=True)))


def _interleave(chains):
    live = list(chains)
    while live:
        still = []
        for c in live:
            try:
                next(c)
                still.append(c)
            except StopIteration:
                pass
        live = still


def _gathered_spec(lo, hi):
    n = hi - lo
    return pl.BlockSpec((N_DEV, n, D_MODEL), lambda *_: (0, lo // n, 0), pipeline_mode=pl.Buffered(1))


def _rows(w_ref):
    return w_ref[...].reshape(-1, D_MODEL)


def _merge_in_specs(T):
    row = lambda w, c=0: pl.BlockSpec((T, w), lambda i: (i, c))
    vec = pl.BlockSpec((1, D_MODEL), lambda i: (0, 0))
    w = [_gathered_spec(lo, hi) for lo, hi in ((R_BH, R_BS), (R_BS, R_BM), (R_BM, R_OUT), (R_OUT, R_KV))]
    return [row(1024), row(1024, Z_HG // 1024), row(1024), row(1024), row(3072), row(1024), vec, *w, vec, vec]


def _merge_fwd(o_raw, zmain, o_b, o_c, gl, x, gain, wbh, wbs, wbm, wout, ln_g, ln_b, *, T):
    S = x.shape[0]

    def body(*refs):
        ins, outs = refs[:13], refs[13:]
        _interleave(_merge_stages(pl.ds(r0, T // MERGE_GROUPS), *ins, fwd_out=outs)
                    for r0 in range(0, T, T // MERGE_GROUPS))

    row = pl.BlockSpec((T, D_MODEL), lambda i: (i, 0))
    return pl.pallas_call(
        body,
        grid=(S // T,),
        in_specs=_merge_in_specs(T),
        out_specs=[row] * 7,
        out_shape=[jax.ShapeDtypeStruct((S, D_MODEL), F32)] + [jax.ShapeDtypeStruct((S, D_MODEL), BF16)] * 6,
        compiler_params=_cparams("parallel"),
        name="merge_fwd",
    )(o_raw, zmain, o_b, o_c, gl, x, gain, wbh, wbs, wbm, wout, ln_g, ln_b)


def _merge_bwd(d_h1, pa, pb, pc, m, o_raw, zmain, gl, x, gain, wbh, wbs, wbm, wout, ln_g, *, T):
    S = x.shape[0]

    def body(dh1_ref, pa_ref, pb_ref, pc_ref, m_ref, oraw_ref, hg_ref, gl_ref, x_ref, gain_ref, wbh_r, wbs_r, wbm_r, wout_r,
             g_ref, dx_ref, du1_ref, dpa_ref, dpb_ref, dpc_ref, doraw_ref, dob_ref, doc_ref, dz_ref,
             dgain_ref, dg_ref, db_ref):
        @pl.when(pl.program_id(0) == 0)
        def _():
            dgain_ref[...] = jnp.zeros_like(dgain_ref)
            dg_ref[...] = jnp.zeros_like(dg_ref)
            db_ref[...] = jnp.zeros_like(db_ref)

        ins = (oraw_ref, hg_ref, None, None, gl_ref, x_ref, gain_ref, wbh_r, wbs_r, wbm_r, wout_r, g_ref, None)
        bwd = (dh1_ref, dx_ref, du1_ref, dpa_ref, dpb_ref, dpc_ref, doraw_ref, dob_ref, doc_ref, dz_ref,
               dgain_ref, dg_ref, db_ref)
        _interleave([_merge_stages(pl.ds(0, T), *ins, bwd=bwd, saved=(pa_ref, pb_ref, pc_ref, m_ref))])

    row = lambda w, c=0: pl.BlockSpec((T, w), lambda i: (i, c))
    vec = pl.BlockSpec((1, D_MODEL), lambda i: (0, 0))
    w = [_gathered_spec(lo, hi) for lo, hi in ((R_BH, R_BS), (R_BS, R_BM), (R_BM, R_OUT), (R_OUT, R_KV))]
    bshape = jax.ShapeDtypeStruct((S, D_MODEL), BF16)
    vshape = jax.ShapeDtypeStruct((1, D_MODEL), F32)
    return pl.pallas_call(
        body,
        grid=(S // T,),
        in_specs=[row(1024)] * 6 + [row(1024, Z_HG // 1024), row(3072), row(1024), vec, *w, vec],
        out_specs=[row(1024)] * 8 + [row(4096), vec, vec, vec],
        out_shape=[jax.ShapeDtypeStruct((S, D_MODEL), F32)] + [bshape] * 7
        + [jax.ShapeDtypeStruct((S, 4096), BF16), vshape, vshape, vshape],
        compiler_params=_cparams("arbitrary"),
        name="merge_bwd",
    )(d_h1, pa, pb, pc, m, o_raw, zmain, gl, x, gain, wbh, wbs, wbm, wout, ln_g)


def _mlp_fwd_bwd(h1, target, wup_t, wdn, ln_g, ln_b, *, T, FC):
    S = h1.shape[0]
    nf = D_FF // FC
    assert FC == R_BH - R_UP == R_UP - R_DN

    def body(h1_ref, t_ref, wup_ref, wdn_ref, g_ref, b_ref, dh1_ref, a_ref, dup_ref, du2_ref, loss_ref, dg_ref, db_ref, up_scr):
        @pl.when(pl.program_id(0) == 0)
        def _():
            loss_ref[...] = jnp.zeros_like(loss_ref)
            dg_ref[...] = jnp.zeros_like(dg_ref)
            db_ref[...] = jnp.zeros_like(db_ref)

        h1v = h1_ref[...]
        h1b = _bf(h1v)
        ff = jnp.zeros((T, D_MODEL), F32)
        for j in range(nf):
            rows = pl.ds(j * FC, FC)
            up = jnp.maximum(_dot_nt(h1b, wup_ref[j]), 0.0)
            up_scr[:, rows] = _bf(up)
            a = _bf(up * up)
            a_ref[:, rows] = a
            ff = ff + _dot(a, wdn_ref[j])
        xhat, rstd = _layer_norm(ALPHA * h1v + ff)
        gamma = g_ref[...]
        err = xhat * gamma + b_ref[...] - t_ref[...]
        loss_ref[...] += jnp.sum(jnp.sum(err * err, axis=-1, keepdims=True), axis=0, keepdims=True) * (0.5 / D_MODEL)
        dy = err * (1.0 / D_MODEL)
        dg_ref[...] += jnp.sum(dy * xhat, axis=0, keepdims=True)
        db_ref[...] += jnp.sum(dy, axis=0, keepdims=True)
        du2 = _layer_norm_bwd(dy, gamma, xhat, rstd)
        du2b = _bf(du2)
        du2_ref[...] = du2b
        dh1 = ALPHA * du2
        for j in range(nf):
            rows = pl.ds(j * FC, FC)
            dup = _bf(_dot_nt(du2b, wdn_ref[j]) * (2.0 * up_scr[:, rows].astype(F32)))
            dup_ref[:, rows] = dup
            dh1 = dh1 + _dot(dup, wup_ref[j])
        dh1_ref[...] = dh1

    row = lambda w: pl.BlockSpec((T, w), lambda i: (i, 0))
    vec = pl.BlockSpec((1, D_MODEL), lambda i: (0, 0))
    vshape = jax.ShapeDtypeStruct((1, D_MODEL), F32)
    return pl.pallas_call(
        body,
        grid=(S // T,),
        in_specs=[row(1024), row(1024), _gathered_spec(R_UP, R_BH), _gathered_spec(R_DN, R_UP), vec, vec],
        out_specs=[row(1024), row(D_FF), row(D_FF), row(1024), pl.BlockSpec((8, 128), lambda i: (0, 0)), vec, vec],
        out_shape=[
            jax.ShapeDtypeStruct((S, D_MODEL), F32),
            jax.ShapeDtypeStruct((S, D_FF), BF16),
            jax.ShapeDtypeStruct((S, D_FF), BF16),
            jax.ShapeDtypeStruct((S, D_MODEL), BF16),
            jax.ShapeDtypeStruct((8, 128), F32), vshape, vshape,
        ],
        scratch_shapes=[pltpu.VMEM((T, D_FF), BF16)],
        compiler_params=_cparams("arbitrary"),
        name="mlp_fwd_bwd",
    )(h1, target, wup_t, wdn, ln_g, ln_b)


def _local_step(x, mem, target, lb_logits, gain, sinks, rel_bias, ln1_g, ln1_b, ln2_g, ln2_b,
                win_t, dep0, other_weights, send_other_grads, send_small_grads, send_win_grad):
    S = x.shape[0]
    T = min(256, S)
    KC = min(2048, S)
    z_qfv, zmain, gl, xb = _in_proj(x, win_t, dep0, tm=min(512, S))
    bucket = _t5_bucket_table()

    o_raw, states = _hgrn_fwd(z_qfv, lb_logits, T=min(2048, S))
    o_b, swa_probs = _swa_fwd(zmain, bucket, rel_bias, sinks)
    g2 = other_weights((o_b, o_raw))
    mkv = _mem_kv_proj(mem, g2)
    o_c, mem_probs = _mem_fwd(zmain, mkv, T=min(1024, S))
    h1, h1b, pa, pb, pc, m, oa = _merge_fwd(o_raw, zmain, o_b, o_c, gl, x, gain, g2, g2, g2, g2, ln1_g, ln1_b,
                                                T=min(512, S))

    d_h1, act, d_up, du2, loss, d_ln2_g, d_ln2_b = _mlp_fwd_bwd(h1, target, g2, g2, ln2_g, ln2_b, T=min(512, S), FC=512)
    wgrad = functools.partial(_mm_tn, out_dtype=BF16)
    halves = lambda r0: (lambda i: (i // 2, r0 // 256 + i % 2, 0))
    whole = lambda r0: (lambda i: (0, r0 // 128, 0))
    og = lax.empty((N_DEV, R_OTHER, D_MODEL), BF16)
    og = wgrad(act, du2, kc=KC, name="grad_w_down", into=(og, (1, 256, D_MODEL), halves(R_DN)))
    og = wgrad(d_up, h1b, kc=KC, name="grad_w_up", into=(og, (1, 256, D_MODEL), halves(R_UP)))

    (dx_part, du1, dpa, dpb, dpc, d_oraw, d_ob, d_oc, d_hg_gl, d_gain, d_ln1_g, d_ln1_b) = _merge_bwd(
        d_h1, pa, pb, pc, m, o_raw, zmain, gl, x, gain, g2, g2, g2, g2, ln1_g, T=T)
    for a_op, b_op, r0, nm in ((m, du1, R_OUT, "out"), (oa, dpa, R_BH, "branch_hg"), (o_b, dpb, R_BS, "branch_swa"),
                               (o_c, dpc, R_BM, "branch_mem")):
        og = wgrad(a_op, b_op, kc=KC, name="grad_w_" + nm, into=(og, (N_DEV, 128, D_MODEL), whole(r0)))

    d_mq, d_mkv = _mem_bwd(zmain, mkv, o_c, mem_probs, d_oc, T=min(1024, S))
    og = wgrad(d_mkv, mem, kc=MEM_LEN, name="grad_w_mem_kv",
               into=(og, (1, 256, D_MODEL), lambda i: (i, R_KV // 256, 0)))
    sent_others = send_other_grads(og)
    d_sq, d_skv, d_rb, d_sink = _swa_bwd(zmain, o_b, swa_probs, d_ob, bucket, sent_others)
    d_qfv, d_lb = _hgrn_bwd(z_qfv, lb_logits, states, d_oraw, T=min(2048, S))
    sent_small = send_small_grads(_pack_small_grads(d_lb, d_gain, d_sink, d_rb, d_ln1_g, d_ln1_b, d_ln2_g, d_ln2_b, loss))

    head_major = lambda a: a.reshape(3, HG_HEADS, HG_DK, D_MODEL).transpose(1, 0, 2, 3).reshape(3 * D_MODEL, D_MODEL)
    pieces = (d_qfv, d_hg_gl, d_sq, d_skv, d_mq)
    placed = (
        ("qfv", d_qfv, 128, lambda i: ((i % 3) * HG_HEADS + i // 3, 0)),
        ("hg_gates", d_hg_gl, 256, lambda i: (jnp.where(i < 4, C_HG // 256 + i, C_GL // 256 + i - 4), 0)),
        ("swa_q", d_sq, None, lambda i: (C_SQ // 1024, 0)),
        ("swa_kv", d_skv, None, lambda i: (C_SK // 256, 0)),
        ("mem_q", d_mq, 256, lambda i: (C_MQ // 256 + i, 0)),
    )
    g_win_t = lax.empty((IN_COLS, D_MODEL), BF16)
    for nm, piece, tile, index in placed:
        g_win_t = wgrad(piece, xb, kc=KC, name="grad_w_in_" + nm, tm=tile,
                        into=(g_win_t, (tile or piece.shape[1], D_MODEL), index))
    sent_win = send_win_grad(g_win_t, sent_small)
    return _grad_x(*pieces, head_major(win_t[:C_HG]), win_t, dx_part, sent_win, tm=T)


MESH = pl.DeviceIdType.MESH
ANY = pl.BlockSpec(memory_space=pl.ANY)


def _coords():
    return lax.axis_index("x"), lax.axis_index("y"), lax.axis_index("c")


def _other_chips(x, y):
    return [(1 - x, y), (x, 1 - y), (1 - x, 1 - y)]


def _all_gather_weights(*arrays):
    na = len(arrays)

    def body(*refs):
        srcs, dsts = refs[:na], refs[na:2 * na]
        send_sems, recv_sems, local_sems = refs[2 * na:]
        x, y, c = _coords()
        me, sibling = (x, y, c), (x, y, 1 - c)
        chips = _other_chips(x, y)

        def slot(a, px, py, pc):
            return dsts[a].at[4 * px + 2 * py + pc]

        def copy(a, k, block, to, from_shard=False):
            return pltpu.make_async_remote_copy(
                src_ref=srcs[a] if from_shard else slot(a, *block), dst_ref=slot(a, *block),
                send_sem=send_sems.at[a * 7 + k], recv_sem=recv_sems.at[a * 7 + k],
                device_id=to, device_id_type=MESH)

        own = [pltpu.make_async_copy(srcs[a], slot(a, *me), local_sems.at[a]) for a in range(na)]
        for cp in own:
            cp.start()
        first = []
        for a in range(na):
            first.append(copy(a, 0, me, sibling, True))
            first += [copy(a, 1 + j, me, (*chip, c), True) for j, chip in enumerate(chips)]
        for cp in first:
            cp.start()
        passed = []
        for j, chip in enumerate(chips):
            for a in range(na):
                copy(a, 1 + j, (*chip, c), me).wait_recv()
                fwd = copy(a, 4 + j, (*chip, c), sibling)
                fwd.start()
                passed.append(fwd)
        for a in range(na):
            copy(a, 0, sibling, me).wait_recv()
            for j, chip in enumerate(chips):
                copy(a, 4 + j, (*chip, 1 - c), me).wait_recv()
        for cp in first + passed:
            cp.wait_send()
        for cp in own:
            cp.wait()

    return pl.pallas_call(
        body,
        in_specs=[ANY] * na,
        out_specs=[ANY] * na,
        out_shape=[jax.ShapeDtypeStruct((N_DEV,) + a.shape, a.dtype) for a in arrays],
        scratch_shapes=[pltpu.SemaphoreType.DMA((7 * na,)), pltpu.SemaphoreType.DMA((7 * na,)),
                        pltpu.SemaphoreType.DMA((na,))],
        name="all_gather_weights",
    )(*arrays)


HBM = pl.BlockSpec(memory_space=pltpu.HBM)
SEM = pl.BlockSpec(memory_space=pltpu.SEMAPHORE)
_DATAFLOW = pltpu.SideEffectType.DATAFLOW_SIDE_EFFECTING


def _peer(x, y, c, r):
    return x ^ (r >> 2), y ^ ((r >> 1) & 1), c ^ (r & 1)


def _direct_copies(src_ref, land_ref, send_sems, recv_sems, gather, receiving):
    x, y, c = _coords()
    me = 4 * x + 2 * y + c
    copies = []
    for r in range(1, N_DEV):
        px, py, pc = _peer(x, y, c, r)
        peer = 4 * px + 2 * py + pc
        if gather:
            src, dst = src_ref, land_ref.at[peer if receiving else me]
        else:
            src, dst = src_ref.at[peer], land_ref.at[r - 1]
        copies.append(pltpu.make_async_remote_copy(
            src_ref=src, dst_ref=dst, send_sem=send_sems.at[r - 1], recv_sem=recv_sems.at[r - 1],
            device_id=(px, py, pc), device_id_type=MESH))
    return copies


def _direct_start(src, land, *, gather, name, after=None):
    def body(src_ref, land_ref, *rest):
        send_sems, recv_sems, token = rest[-5], rest[-4], rest[-1]
        for cp in _direct_copies(src_ref, land_ref, send_sems, recv_sems, gather, False):
            cp.start()
        token[...] = jnp.zeros_like(token)

    afters = () if after is None else (after,)
    return pl.pallas_call(
        body,
        name=name,
        out_shape=(pltpu.SemaphoreType.DMA((N_DEV - 1,)), pltpu.SemaphoreType.DMA((N_DEV - 1,)),
                   pltpu.HBM(src.shape, src.dtype), pltpu.HBM(land.shape, land.dtype),
                   jax.ShapeDtypeStruct((8, 128), F32)),
        in_specs=(HBM, HBM) + tuple(ANY for _ in afters),
        out_specs=(SEM, SEM, HBM, HBM, pl.BlockSpec(memory_space=pltpu.VMEM)),
        input_output_aliases={0: 2, 1: 3},
        compiler_params=pltpu.CompilerParams(has_side_effects=_DATAFLOW),
    )(pltpu.with_memory_space_constraint(src, pltpu.HBM), pltpu.with_memory_space_constraint(land, pltpu.HBM), *afters)


def _direct_wait(send_sems, recv_sems, src_thru, land_thru, after, *, gather, name):
    afters = after if isinstance(after, tuple) else (after,)

    def body(src_ref, land_ref, send_sems_ref, recv_sems_ref, *rest):
        del rest
        for cp in _direct_copies(src_ref, land_ref, send_sems_ref, recv_sems_ref, gather, True):
            cp.wait_send()
            cp.wait_recv()

    return pl.pallas_call(
        body,
        name=name,
        out_shape=(pltpu.HBM(src_thru.shape, src_thru.dtype), pltpu.HBM(land_thru.shape, land_thru.dtype)),
        in_specs=(HBM, HBM, SEM, SEM) + tuple(ANY for _ in afters),
        out_specs=(HBM, HBM),
        input_output_aliases={0: 0, 1: 1},
        compiler_params=pltpu.CompilerParams(has_side_effects=_DATAFLOW),
    )(src_thru, land_thru, send_sems, recv_sems, *afters)


def _sum_partials(src, land, me, *, tr, name, wmv=None):
    R = src.shape[1]
    extra = () if wmv is None else tuple(wmv)

    def body(me_ref, s_ref, l_ref, *rest):
        del me_ref
        acc = s_ref[0].astype(F32)
        for r in range(N_DEV - 1):
            acc = acc + l_ref[r].astype(F32)
        rest[len(extra)][...] = acc
        if extra:
            w_ref, m_ref, v_ref, _, d_ref, nm_ref, nv_ref = rest
            d_ref[...], nm_ref[...], nv_ref[...] = _adam_step(w_ref[...], acc, m_ref[...], v_ref[...])

    row = pl.BlockSpec((tr, 1024), lambda i, mr: (i, 0))
    n_out = 4 if extra else 1
    res = pl.pallas_call(
        body,
        grid_spec=pltpu.PrefetchScalarGridSpec(
            num_scalar_prefetch=1, grid=(R // tr,),
            in_specs=[pl.BlockSpec((1, tr, 1024), lambda i, mr: (mr[0], i, 0)),
                      pl.BlockSpec((N_DEV - 1, tr, 1024), lambda i, mr: (0, i, 0))] + [row for _ in extra],
            out_specs=[row] * n_out),
        out_shape=[jax.ShapeDtypeStruct((R, 1024), F32)] * n_out,
        name=name,
    )(me, src, land, *extra)
    return res if extra else res[0]


_SMALL = ("lb_logits", "hg_norm_gain", "swa_sinks", "rel_bias", "ln1_g", "ln1_b", "ln2_g", "ln2_b")


def _pack_small_grads(d_lb, d_gain, d_sink, d_rb, d_ln1_g, d_ln1_b, d_ln2_g, d_ln2_b, loss):
    def body(lb_ref, gain_ref, sink_ref, rb_ref, l1g_ref, l1b_ref, l2g_ref, l2b_ref, loss_ref, o_ref):
        o_ref[...] = jnp.zeros_like(o_ref)
        for row, ref in ((SM_LB, lb_ref), (SM_GAIN, gain_ref), (SM_L1G, l1g_ref), (SM_L1B, l1b_ref),
                         (SM_L2G, l2g_ref), (SM_L2B, l2b_ref)):
            o_ref[row:row + 1, :] = ref[...]
        o_ref[SM_SINK:SM_SINK + 1, 0:128] = sink_ref[0:1, :]
        o_ref[SM_LOSS:SM_LOSS + 1, 0:128] = loss_ref[0:1, :]
        o_ref[SM_RB:SM_RB + NUM_BUCKETS, 0:128] = rb_ref[...]

    vm = pl.BlockSpec(memory_space=pltpu.VMEM)
    return pl.pallas_call(
        body,
        in_specs=[vm] * 9,
        out_specs=vm,
        out_shape=jax.ShapeDtypeStruct((SM_ROWS, D_MODEL), F32),
        name="pack_small_grads",
    )(d_lb, d_gain, d_sink, d_rb, d_ln1_g, d_ln1_b, d_ln2_g, d_ln2_b, loss)


def _small_finish(gathered, w, m, v):
    n = len(_SMALL)

    def body(*refs):
        g_ref = refs[0]
        w_refs, m_refs, v_refs = refs[1:1 + n], refs[1 + n:1 + 2 * n], refs[1 + 2 * n:1 + 3 * n]
        outs = refs[1 + 3 * n:]
        loss_ref, tot = outs[0], outs[-1]
        g_out, d_out, m_out, v_out = (outs[1 + k * n:1 + (k + 1) * n] for k in range(4))
        acc = g_ref[0]
        for d in range(1, N_DEV):
            acc = acc + g_ref[d]
        tot[...] = acc
        loss_ref[...] = tot[SM_LOSS:SM_LOSS + 1, 0:1]
        lb = _lower_bound(w_refs[0])
        dl0 = tot[SM_LB:SM_LB + 1, :] * lb * (1.0 - lb)
        grads = (jnp.concatenate([dl0, -dl0], axis=0), tot[SM_GAIN:SM_GAIN + 1, :],
                 tot[SM_SINK:SM_SINK + 1, 0:SWA_HEADS], tot[SM_RB:SM_RB + NUM_BUCKETS, 0:SWA_HEADS],
                 tot[SM_L1G:SM_L1G + 1, :], tot[SM_L1B:SM_L1B + 1, :], tot[SM_L2G:SM_L2G + 1, :], tot[SM_L2B:SM_L2B + 1, :])
        for k, g in enumerate(grads):
            g_out[k][...] = g
            d_out[k][...], m_out[k][...], v_out[k][...] = _adam_step(w_refs[k][...], g, m_refs[k][...], v_refs[k][...])

    vm = pl.BlockSpec(memory_space=pltpu.VMEM)
    shapes = [jax.ShapeDtypeStruct(w[k].shape, F32) for k in _SMALL]
    res = pl.pallas_call(
        body,
        in_specs=[vm] * (1 + 3 * n),
        out_specs=[vm] * (1 + 4 * n),
        out_shape=[jax.ShapeDtypeStruct((1, 1), F32)] + shapes * 4,
        scratch_shapes=[pltpu.VMEM((SM_ROWS, D_MODEL), F32)],
        name="small_finish",
    )(gathered, *[w[k] for k in _SMALL], *[m[k] for k in _SMALL], *[v[k] for k in _SMALL])
    parts = [dict(zip(_SMALL, res[1 + k * n:1 + (k + 1) * n])) for k in range(4)]
    return (res[0], *parts)


def _adam_step(w, g, m, v):
    nm = ADAM_B1 * m + (1.0 - ADAM_B1) * g
    nv = ADAM_B2 * v + (1.0 - ADAM_B2) * jnp.square(g)
    m_hat = nm / (1.0 - ADAM_B1 ** ADAM_STEP)
    v_hat = nv / (1.0 - ADAM_B2 ** ADAM_STEP)
    return -ADAM_LR * (m_hat / (jnp.sqrt(v_hat) + ADAM_EPS) + ADAM_WD * w), nm, nv


def _adamw(w, g, m, v, *, tr, name):
    R, C = w.shape

    def body(w_ref, g_ref, m_ref, v_ref, d_ref, nm_ref, nv_ref):
        d_ref[...], nm_ref[...], nv_ref[...] = _adam_step(w_ref[...], g_ref[...], m_ref[...], v_ref[...])

    spec = pl.BlockSpec((tr, C), lambda i: (i, 0))
    return pl.pallas_call(
        body,
        grid=(R // tr,),
        in_specs=[spec] * 4,
        out_specs=[spec] * 3,
        out_shape=[jax.ShapeDtypeStruct((R, C), F32)] * 3,
        compiler_params=_cparams("parallel"),
        name=name,
    )(w, g, m, v)


_WEIGHTS = ("w_in", "lb_logits", "hg_norm_gain", "swa_sinks", "rel_bias", "w_mem_kv", "w_branch_hg", "w_branch_swa",
            "w_branch_mem", "w_out", "ln1_g", "ln1_b", "w_up", "w_down", "ln2_g", "ln2_b")


def kernel(x, mem, w_in, lb_logits, hg_norm_gain, swa_sinks, rel_bias, w_mem_kv, w_branch_hg, w_branch_swa, w_branch_mem, w_out, ln1_g, ln1_b, w_up, w_down, ln2_g, ln2_b, loss_target, m_w_in, m_lb_logits, m_hg_norm_gain, m_swa_sinks, m_rel_bias, m_w_mem_kv, m_w_branch_hg, m_w_branch_swa, m_w_branch_mem, m_w_out, m_ln1_g, m_ln1_b, m_w_up, m_w_down, m_ln2_g, m_ln2_b, v_w_in, v_lb_logits, v_hg_norm_gain, v_swa_sinks, v_rel_bias, v_w_mem_kv, v_w_branch_hg, v_w_branch_swa, v_w_branch_mem, v_w_out, v_ln1_g, v_ln1_b, v_w_up, v_w_down, v_ln2_g, v_ln2_b):
    w = dict(w_in=w_in, lb_logits=lb_logits, hg_norm_gain=hg_norm_gain, swa_sinks=swa_sinks, rel_bias=rel_bias,
             w_mem_kv=w_mem_kv, w_branch_hg=w_branch_hg, w_branch_swa=w_branch_swa, w_branch_mem=w_branch_mem,
             w_out=w_out, ln1_g=ln1_g, ln1_b=ln1_b, w_up=w_up, w_down=w_down, ln2_g=ln2_g, ln2_b=ln2_b)
    mom = dict(w_in=m_w_in, lb_logits=m_lb_logits, hg_norm_gain=m_hg_norm_gain, swa_sinks=m_swa_sinks, rel_bias=m_rel_bias,
               w_mem_kv=m_w_mem_kv, w_branch_hg=m_w_branch_hg, w_branch_swa=m_w_branch_swa, w_branch_mem=m_w_branch_mem,
               w_out=m_w_out, ln1_g=m_ln1_g, ln1_b=m_ln1_b, w_up=m_w_up, w_down=m_w_down, ln2_g=m_ln2_g, ln2_b=m_ln2_b)
    var = dict(w_in=v_w_in, lb_logits=v_lb_logits, hg_norm_gain=v_hg_norm_gain, swa_sinks=v_swa_sinks, rel_bias=v_rel_bias,
               w_mem_kv=v_w_mem_kv, w_branch_hg=v_w_branch_hg, w_branch_swa=v_w_branch_swa, w_branch_mem=v_w_branch_mem,
               w_out=v_w_out, ln1_g=v_ln1_g, ln1_b=v_ln1_b, w_up=v_w_up, w_down=v_w_down, ln2_g=v_ln2_g, ln2_b=v_ln2_b)
    xc, yc, cc = _coords()

    p1 = _bf(w_in[0].T)
    p2 = _bf(jnp.concatenate([w_down[0], w_up[0].T, w_branch_hg[0], w_branch_swa[0], w_branch_mem[0], w_out[0],
                              w_mem_kv[0].T], axis=0))
    me = 4 * xc + 2 * yc + cc
    (g1,) = _all_gather_weights(p1)
    land2 = lax.dynamic_update_slice(lax.empty((N_DEV, R_OTHER, D_MODEL), BF16), p2[None], (me, 0, 0))
    ag2 = _direct_start(p2, land2, gather=True, name="gather_other_weights_start")

    def other_weights(after):
        return _direct_wait(*ag2[:4], after, gather=True, name="gather_other_weights_wait")[1]

    blocks = lambda a: a.reshape(N_DEV, a.shape[0] // N_DEV, D_MODEL)
    started = {}

    def send_other_grads(part):
        started["others"] = _direct_start(part, lax.empty((N_DEV - 1, R_OTHER, D_MODEL), BF16), gather=False,
                                          name="scatter_other_grads_start")
        return started["others"][4]

    me1 = me.reshape(1).astype(jnp.int32)
    grads, delta, new_m, new_v = {}, {}, {}, {}

    def adamw(name):
        w2 = w[name][0]
        delta[name], new_m[name], new_v[name] = _adamw(
            w2, grads[name], mom[name][0], var[name][0], tr=w2.shape[0] // 4, name="adamw_" + name)

    def send_small_grads(packed):
        land = lax.dynamic_update_slice(lax.empty((N_DEV, SM_ROWS, D_MODEL), F32), packed[None], (me, 0, 0))
        started["small"] = _direct_start(packed, land, gather=True, name="gather_small_grads_start")
        return started["small"][4]

    def send_win_grad(g, after):
        started["win"] = _direct_start(blocks(g), lax.empty((N_DEV - 1, IN_SHARD, D_MODEL), BF16), gather=False,
                                       name="scatter_w_in_grad_start", after=after)
        mine2, landed2 = _direct_wait(*started["others"][:4], started["win"][4], gather=False,
                                      name="scatter_other_grads_wait")
        gs2 = _sum_partials(mine2, landed2, me1, tr=R_OTHER // 2, name="sum_other_grads")
        grads.update(
            w_down=gs2[R_DN:R_UP], w_up=gs2[R_UP:R_BH].T, w_branch_hg=gs2[R_BH:R_BS], w_branch_swa=gs2[R_BS:R_BM],
            w_branch_mem=gs2[R_BM:R_OUT], w_out=gs2[R_OUT:R_KV], w_mem_kv=gs2[R_KV:R_OTHER].T)
        for name in ("w_mem_kv", "w_branch_hg", "w_branch_swa", "w_branch_mem", "w_out", "w_up", "w_down"):
            adamw(name)
        return tuple(new_v[name] for name in new_v)

    grad_x = _local_step(
        x[0], mem[0], loss_target[0], lb_logits, hg_norm_gain, swa_sinks, rel_bias, ln1_g, ln1_b, ln2_g, ln2_b,
        g1.reshape(IN_COLS, D_MODEL), ag2[4], other_weights, send_other_grads, send_small_grads, send_win_grad)

    mine1, landed1 = _direct_wait(*started["win"][:4], grad_x, gather=False, name="scatter_w_in_grad_wait")
    g_win_t, d_t, m_t, v_t = _sum_partials(mine1, landed1, me1, tr=IN_SHARD // 2, name="sum_adamw_w_in",
                                           wmv=(w_in[0].T, m_w_in[0].T, v_w_in[0].T))
    grads["w_in"], delta["w_in"], new_m["w_in"], new_v["w_in"] = g_win_t.T, d_t.T, m_t.T, v_t.T

    _, gathered = _direct_wait(*started["small"][:4], grad_x, gather=True, name="gather_small_grads_wait")
    loss, g_s, d_s, m_s, v_s = _small_finish(gathered, w, mom, var)
    for dst, src in ((grads, g_s), (delta, d_s), (new_m, m_s), (new_v, v_s)):
        dst.update(src)

    def shaped(d, name):
        return d[name].reshape(w[name].shape)

    return (loss.reshape(()), grad_x[None], *[shaped(grads, n) for n in _WEIGHTS], *[shaped(delta, n) for n in _WEIGHTS],
            *[shaped(new_m, n) for n in _WEIGHTS], *[shaped(new_v, n) for n in _WEIGHTS])
```

```python
import functools
import math

import jax
import jax.numpy as jnp
from jax import lax
from jax.experimental import pallas as pl
from jax.experimental.pallas import tpu as pltpu

F32 = jnp.float32
BF16 = jnp.bfloat16

D_MODEL = 1024
MEM_LEN = 256
HG_HEADS = 8
HG_DK = 128
HG_CHUNK = 64
SWA_HEADS = 16
SWA_HEAD_DIM = 64
SWA_BLOCK = 128
SWA_WINDOW = 128
MEM_HEADS = 4
MEM_HEAD_DIM = 256
NUM_BUCKETS = 32
MAX_DISTANCE = 128
D_FF = 4096
LN_EPS = 1e-5
RMS_EPS = 1e-6
ALPHA = 2.0 ** 0.25
N_DEV = 8

C_HQ, C_HF, C_HI, C_HG, C_SQ, C_SK, C_SV, C_MQ, C_GL = 0, 1024, 2048, 3072, 4096, 5120, 5248, 5376, 6400
IN_COLS = 9472
IN_SHARD = IN_COLS // N_DEV
Z_HG, Z_SQ, Z_SK, Z_MQ, Z_REST = 0, C_SQ - C_HG, C_SK - C_HG, C_MQ - C_HG, C_GL - C_HG

ADAM_LR = 0.001
ADAM_B1 = 0.9
ADAM_B2 = 0.999
ADAM_EPS = 1e-08
ADAM_WD = 0.01
ADAM_STEP = 10

VMEM_LIMIT = 58 * 1024 * 1024

R_DN, R_UP, R_BH, R_BS, R_BM, R_OUT, R_KV, R_OTHER = 0, 512, 1024, 1152, 1280, 1408, 1536, 1792

SM_LB, SM_GAIN, SM_SINK, SM_L1G, SM_L1B, SM_L2G, SM_L2B, SM_LOSS, SM_RB, SM_ROWS = 0, 2, 3, 4, 5, 6, 7, 8, 16, 48


def _bf(v):
    return v.astype(BF16)


def _f32(v):
    return v.astype(F32)


def _dot(a, b):
    return jnp.dot(a, b, preferred_element_type=F32)


def _dot_nt(a, b):
    return lax.dot_general(a, b, (((1,), (1,)), ((), ())), preferred_element_type=F32)


def _dot_tn(a, b):
    return lax.dot_general(a, b, (((0,), (0,)), ((), ())), preferred_element_type=F32)


def _sig(v):
    return 0.5 * jnp.tanh(0.5 * v) + 0.5


def _cparams(*sem):
    return pltpu.CompilerParams(dimension_semantics=sem, vmem_limit_bytes=VMEM_LIMIT)


def _const_spec(shape):
    nd = len(shape)
    return pl.BlockSpec(shape, lambda *_: (0,) * nd, pipeline_mode=pl.Buffered(1))


def _dep_spec():
    return pl.BlockSpec((8, 128), lambda *_: (0, 0))


def _in_proj(x, win_t, dep, *, tm):
    S = x.shape[0]

    def body(x_ref, w_ref, dep_ref, qfv_ref, z_ref, gl_ref, xb_ref):
        del dep_ref
        xb = _bf(x_ref[...])
        xb_ref[...] = xb
        for c0 in range(0, C_HG, 1024):
            qfv_ref[:, c0:c0 + 1024] = _dot_nt(xb, w_ref[c0:c0 + 1024, :])
        for c0 in range(0, Z_REST, Z_REST // 2):
            z_ref[:, c0:c0 + Z_REST // 2] = _bf(_dot_nt(xb, w_ref[C_HG + c0:C_HG + c0 + Z_REST // 2, :]))
        for c0 in range(0, IN_COLS - C_GL, 1024):
            gl_ref[:, c0:c0 + 1024] = _bf(_dot_nt(xb, w_ref[C_GL + c0:C_GL + c0 + 1024, :]))

    row = lambda w: pl.BlockSpec((tm, w), lambda i: (i, 0))
    return pl.pallas_call(
        body,
        grid=(S // tm,),
        in_specs=[row(D_MODEL), _const_spec(win_t.shape), _dep_spec()],
        out_specs=[row(C_HG), row(Z_REST), row(IN_COLS - C_GL), row(D_MODEL)],
        out_shape=[jax.ShapeDtypeStruct((S, C_HG), F32), jax.ShapeDtypeStruct((S, Z_REST), BF16),
                   jax.ShapeDtypeStruct((S, IN_COLS - C_GL), BF16), jax.ShapeDtypeStruct((S, D_MODEL), BF16)],
        compiler_params=_cparams("parallel"),
        name="in_proj",
    )(x, win_t, dep)


def _placement(into, tm, N, M, out_dtype):
    if into is None:
        return (lambda i: (i, 0)), (tm, N), jax.ShapeDtypeStruct((M, N), out_dtype), (), {}
    dest, block, index = into
    assert math.prod(block) == tm * N and dest.dtype == out_dtype
    return index, block, jax.ShapeDtypeStruct(dest.shape, dest.dtype), (dest,), {2: 0}


def _mm_tn_resident(a, b, *, tm, kc, name, out_dtype, into=None):
    K, M = a.shape
    N = b.shape[1]
    nk = K // kc
    index, block, out_shape, extra, aliases = _placement(into, tm, N, M, out_dtype)

    def body(a_ref, b_ref, *rest):
        o_ref = rest[-1]
        acc = jnp.zeros((tm, N), F32)
        for kk in range(nk):
            sl = pl.ds(kk * kc, kc)
            acc = acc + _dot_tn(_bf(a_ref[sl, :]), _bf(b_ref[sl, :]))
        o_ref[...] = acc.astype(o_ref.dtype).reshape(block)

    return pl.pallas_call(
        body,
        grid=(M // tm,),
        in_specs=[pl.BlockSpec((K, tm), lambda i: (0, i)), _const_spec((K, N))] + [ANY for _ in extra],
        out_specs=pl.BlockSpec(block, index),
        out_shape=out_shape,
        input_output_aliases=aliases,
        compiler_params=_cparams("parallel"),
        name=name,
    )(a, b, *extra)


def _mm_tn(a, b, *, kc, name, out_dtype=F32, into=None, tm=None):
    K, M = a.shape
    N = b.shape[1]
    if M > 1024 or tm is not None:
        return _mm_tn_resident(a, b, tm=tm or 256, kc=min(kc, 1024), name=name, out_dtype=out_dtype, into=into)
    tm = M
    nk = K // kc
    index, block, out_shape, extra, aliases = _placement(into, tm, N, M, out_dtype)

    def body(a_ref, b_ref, *rest):
        o_ref, acc = rest[-2], rest[-1]
        k = pl.program_id(1)
        part = _dot_tn(_bf(a_ref[...]), _bf(b_ref[...]))

        @pl.when(k == 0)
        def _():
            acc[...] = part

        @pl.when(k > 0)
        def _():
            acc[...] += part

        @pl.when(k == nk - 1)
        def _():
            o_ref[...] = acc[...].astype(o_ref.dtype).reshape(block)

    return pl.pallas_call(
        body,
        grid=(M // tm, nk),
        in_specs=[pl.BlockSpec((kc, tm), lambda i, k: (k, i)), pl.BlockSpec((kc, N), lambda i, k: (k, 0))]
        + [ANY for _ in extra],
        out_specs=pl.BlockSpec(block, lambda i, k: index(i)),
        out_shape=out_shape,
        input_output_aliases=aliases,
        scratch_shapes=[pltpu.VMEM((tm, N), F32)],
        compiler_params=_cparams("parallel", "arbitrary"),
        name=name,
    )(a, b, *extra)


def _grad_x(d_qfv, d_hg_gl, d_sq, d_skv, d_mq, w_qfv, win_t, add, deps, *, tm):
    M = add.shape[0]
    pieces = (d_qfv, d_hg_gl, d_sq, d_skv, d_mq)

    def body(qfv_ref, hggl_ref, sq_ref, skv_ref, mq_ref, wq_ref, w_ref, add_ref, *rest):
        o_ref = rest[-1]
        acc = add_ref[...] + _dot(qfv_ref[...], wq_ref[...])
        acc = acc + _dot(hggl_ref[:, 0:1024], w_ref[C_HG:C_SQ, :])
        acc = acc + _dot(hggl_ref[:, 1024:4096], w_ref[C_GL:IN_COLS, :])
        acc = acc + _dot(sq_ref[...], w_ref[C_SQ:C_SK, :])
        acc = acc + _dot(skv_ref[...], w_ref[C_SK:C_MQ, :])
        o_ref[...] = acc + _dot(mq_ref[...], w_ref[C_MQ:C_GL, :])

    return pl.pallas_call(
        body,
        grid=(M // tm,),
        in_specs=[pl.BlockSpec((tm, p.shape[1]), lambda i: (i, 0)) for p in pieces]
        + [_const_spec(w_qfv.shape), _const_spec(win_t.shape), pl.BlockSpec((tm, D_MODEL), lambda i: (i, 0))]
        + [_dep_spec() for _ in deps],
        out_specs=pl.BlockSpec((tm, D_MODEL), lambda i: (i, 0)),
        out_shape=jax.ShapeDtypeStruct((M, D_MODEL), F32),
        compiler_params=_cparams("parallel"),
        name="grad_x",
    )(*pieces, w_qfv, win_t, add, *deps)


def _lower_bound(lbl_ref):
    l0 = lbl_ref[0:1, :]
    l1 = lbl_ref[1:2, :]
    mx = jnp.maximum(l0, l1)
    e0 = jnp.exp(l0 - mx)
    e1 = jnp.exp(l1 - mx)
    return e0 / (e0 + e1)


def _tri(lower):
    r = lax.broadcasted_iota(jnp.int32, (HG_CHUNK, HG_CHUNK), 0)
    c = lax.broadcasted_iota(jnp.int32, (HG_CHUNK, HG_CHUNK), 1)
    return (r >= c) if lower else (r <= c)


def _hg_gates(fl, lb):
    sg = _sig(fl)
    f = lb + (1.0 - lb) * sg
    return sg, f, jnp.log(f), 1.0 - f


def _scan_rows(v, reverse=False):
    row = lax.broadcasted_iota(jnp.int32, v.shape, 0)
    s = 1
    while s < HG_CHUNK:
        if reverse:
            v = v + jnp.where(row < HG_CHUNK - s, pltpu.roll(v, HG_CHUNK - s, 0), 0.0)
        else:
            v = v + jnp.where(row >= s, pltpu.roll(v, s, 0), 0.0)
        s *= 2
    return v


def _hgrn_fwd(zmain, lb_logits, *, T):
    S = zmain.shape[0]
    nc = T // HG_CHUNK

    def body(q_ref, f_ref, v_ref, lbl_ref, o_ref, st_ref, sg_ref, b_ref, state):
        @pl.when(pl.program_id(1) == 0)
        def _():
            state[...] = jnp.zeros_like(state)

        lb = _lower_bound(lbl_ref)
        tril = _tri(True)
        qis, updates, decays, intra = [], [], [], []
        for c in range(nc):
            sl = pl.ds(c * HG_CHUNK, HG_CHUNK)
            sg, _, g, k = _hg_gates(_f32(f_ref[sl, :]), lb)
            b = _scan_rows(g)
            sg_ref[sl, :] = sg
            b_ref[sl, :] = b
            bl = jnp.sum(g, axis=0, keepdims=True)
            qi = _bf(_f32(q_ref[sl, :]) * jnp.exp(b))
            ki = _bf(k * jnp.exp(-b))
            ko = _bf(k * jnp.exp(bl - b))
            vb = _bf(v_ref[sl, :])
            att = jnp.where(tril, _dot_nt(qi, ki), 0.0)
            intra.append(_dot(_bf(att), vb))
            qis.append(qi)
            updates.append(_dot_tn(vb, ko))
            decays.append(jnp.exp(bl))
        st = state[...]
        for c in range(nc):
            st_ref[0, c] = st
            o_ref[pl.ds(c * HG_CHUNK, HG_CHUNK), :] = intra[c] + _dot_nt(qis[c], _bf(st))
            st = st * decays[c] + updates[c]
        state[...] = st

    col = lambda base: pl.BlockSpec((T, HG_DK), lambda h, t: (t, base + h))
    return pl.pallas_call(
        body,
        grid=(HG_HEADS, S // T),
        in_specs=[col(0), col(8), col(16), pl.BlockSpec((2, HG_DK), lambda h, t: (0, h))],
        out_specs=[
            pl.BlockSpec((T, HG_DK), lambda h, t: (t, h)),
            pl.BlockSpec((1, nc, HG_DK, HG_DK), lambda h, t: (h, t, 0, 0)),
            pl.BlockSpec((T, HG_DK), lambda h, t: (t, h)),
            pl.BlockSpec((T, HG_DK), lambda h, t: (t, h)),
        ],
        out_shape=[
            jax.ShapeDtypeStruct((S, D_MODEL), F32),
            jax.ShapeDtypeStruct((HG_HEADS, S // HG_CHUNK, HG_DK, HG_DK), F32),
            jax.ShapeDtypeStruct((S, D_MODEL), F32),
            jax.ShapeDtypeStruct((S, D_MODEL), F32),
        ],
        scratch_shapes=[pltpu.VMEM((HG_DK, HG_DK), F32)],
        compiler_params=_cparams("parallel", "arbitrary"),
        name="hgrn_fwd",
    )(zmain, zmain, zmain, lb_logits)


def _hgrn_bwd(zmain, sig_f, cum_b, lb_logits, states, d_o, *, T):
    S = zmain.shape[0]
    nc = T // HG_CHUNK
    nt = S // T

    def body(q_ref, sg_ref, b_ref, v_ref, lbl_ref, st_ref, do_ref, dz_ref, dlb_ref, dstate):
        @pl.when(pl.program_id(1) == 0)
        def _():
            dstate[...] = jnp.zeros_like(dstate)
            dlb_ref[...] = jnp.zeros_like(dlb_ref)

        lb = _lower_bound(lbl_ref)
        tril = _tri(True)
        last_row = lax.broadcasted_iota(jnp.int32, (HG_CHUNK, HG_DK), 0) == HG_CHUNK - 1
        saved = []
        for c in range(nc):
            sl = pl.ds(c * HG_CHUNK, HG_CHUNK)
            sg = sg_ref[sl, :]
            f = lb + (1.0 - lb) * sg
            k = 1.0 - f
            b = b_ref[sl, :]
            bl = b_ref[pl.ds((c + 1) * HG_CHUNK - 1, 1), :]
            eb = jnp.exp(b)
            enb = jnp.exp(-b)
            eo = jnp.exp(bl - b)
            q_in = _f32(q_ref[sl, :]) * eb
            k_in = k * enb
            k_out = k * eo
            qi, ki, ko = _bf(q_in), _bf(k_in), _bf(k_out)
            vb = _bf(v_ref[sl, :])
            dob = do_ref[sl, :]
            att = jnp.where(tril, _dot_nt(qi, ki), 0.0)
            d_att = _bf(jnp.where(tril, _dot_nt(dob, vb), 0.0))
            d_kin = _dot_tn(d_att, qi)
            saved.append(dict(
                sg=sg, f=f, eb=eb, enb=enb, eo=eo, ebl=jnp.exp(bl), k_out=k_out, ko=ko, vb=vb, dob=dob,
                d_v=_dot_tn(_bf(att), dob), d_qin=_dot(d_att, ki), d_kin=d_kin,
                qk=(q_in, k_in), d_state=_dot_tn(dob, qi)))
        dst = dstate[...]
        dsts = [None] * nc
        for c in reversed(range(nc)):
            dsts[c] = dst
            dst = dst * saved[c]["ebl"] + saved[c]["d_state"]
        dstate[...] = dst
        dlb = jnp.zeros((1, HG_DK), F32)
        for c in range(nc):
            sl = pl.ds(c * HG_CHUNK, HG_CHUNK)
            s = saved[c]
            q_in, k_in = s["qk"]
            st = st_ref[0, c]
            dstb = _bf(dsts[c])
            d_v = s["d_v"] + _dot_nt(s["ko"], dstb)
            d_qin = s["d_qin"] + _dot(s["dob"], _bf(st))
            d_kout = _dot(s["vb"], dstb)
            d_decay = jnp.sum(dsts[c] * st, axis=0, keepdims=True)
            kk = d_kout * s["k_out"]
            d_b = d_qin * q_in - s["d_kin"] * k_in - kk
            d_bl = jnp.sum(kk, axis=0, keepdims=True) + d_decay * s["ebl"]
            d_g = _scan_rows(d_b + jnp.where(last_row, d_bl, 0.0), reverse=True)
            d_f = d_g / s["f"] - (s["d_kin"] * s["enb"] + d_kout * s["eo"])
            dz_ref[sl, 0:HG_DK] = _bf(d_qin * s["eb"])
            dz_ref[sl, HG_DK:2 * HG_DK] = _bf(d_f * (1.0 - lb) * s["sg"] * (1.0 - s["sg"]))
            dz_ref[sl, 2 * HG_DK:3 * HG_DK] = _bf(d_v)
            dlb = dlb + jnp.sum(d_f * (1.0 - s["sg"]), axis=0, keepdims=True)
        dlb_ref[...] += dlb

    rev = lambda base: pl.BlockSpec((T, HG_DK), lambda h, t: (nt - 1 - t, base + h))
    outc = pl.BlockSpec((T, HG_DK), lambda h, t: (nt - 1 - t, h))
    return pl.pallas_call(
        body,
        grid=(HG_HEADS, nt),
        in_specs=[
            rev(0), outc, outc, rev(16),
            pl.BlockSpec((2, HG_DK), lambda h, t: (0, h)),
            pl.BlockSpec((1, nc, HG_DK, HG_DK), lambda h, t: (h, nt - 1 - t, 0, 0)),
            outc,
        ],
        out_specs=[pl.BlockSpec((T, 3 * HG_DK), lambda h, t: (nt - 1 - t, h)),
                   pl.BlockSpec((1, HG_DK), lambda h, t: (0, h))],
        out_shape=[jax.ShapeDtypeStruct((S, 3 * D_MODEL), BF16), jax.ShapeDtypeStruct((1, D_MODEL), F32)],
        scratch_shapes=[pltpu.VMEM((HG_DK, HG_DK), F32)],
        compiler_params=_cparams("parallel", "arbitrary"),
        name="hgrn_bwd",
    )(zmain, sig_f, cum_b, zmain, lb_logits, states, d_o)


def _t5_bucket_table():
    qi = jnp.arange(SWA_BLOCK)[:, None] + SWA_BLOCK
    kj = jnp.arange(2 * SWA_BLOCK)[None, :]
    n = jnp.clip(qi - kj, 0, SWA_WINDOW - 1)
    max_exact = NUM_BUCKETS // 2
    nf = jnp.maximum(n, 1).astype(F32)
    large = max_exact + (jnp.log(nf / max_exact) / math.log(MAX_DISTANCE / max_exact)
                         * (NUM_BUCKETS - max_exact)).astype(jnp.int32)
    large = jnp.minimum(large, NUM_BUCKETS - 1)
    return jnp.where(n < max_exact, n, large).astype(jnp.int32)


SWA_ROWS = 32
MERGE_GROUPS = 2


def _swa_bias_init(bias, bucket_ref, rb_ref):
    bk = bucket_ref[...]
    qi = lax.broadcasted_iota(jnp.int32, bk.shape, 0) + SWA_BLOCK
    kj = lax.broadcasted_iota(jnp.int32, bk.shape, 1)
    band = (qi - kj >= 0) & (qi - kj < SWA_WINDOW)
    for h in range(SWA_HEADS):
        def sel(b, acc, h=h):
            return jnp.where(bk == b, rb_ref[b, h], acc)
        t = lax.fori_loop(0, NUM_BUCKETS, sel, jnp.zeros(bk.shape, F32))
        bias[1, h] = jnp.where(band, t, -jnp.inf)
        bias[0, h] = jnp.where(band & (kj >= SWA_BLOCK), t, -jnp.inf)


def _lane_halves(t, kv_head):
    lane = lax.broadcasted_iota(jnp.int32, t.shape, 1)
    rolled = pltpu.roll(t, 64, 1)
    zero = jnp.zeros_like(t)
    if kv_head == 0:
        return jnp.where(lane < 64, t, zero), jnp.where(lane >= 64, rolled, zero)
    return jnp.where(lane < 64, rolled, zero), jnp.where(lane >= 64, t, zero)


def _swa_zero_key0(t):
    return jnp.where(lax.broadcasted_iota(jnp.int32, t.shape, 0) == 0, jnp.zeros_like(t), t)


def _swa_probs(s, masked_bias, sink):
    s = s + masked_bias
    m = jnp.maximum(jnp.max(s, axis=-1, keepdims=True), sink)
    p = jnp.exp(s - m)
    es = jnp.exp(sink - m)
    inv = 1.0 / (jnp.sum(p, axis=-1, keepdims=True) + es)
    return p * inv, es * inv


def _swa_fwd(zmain, bucket, rel_bias, sinks):
    S = zmain.shape[0]
    nb = S // SWA_BLOCK
    scale = SWA_HEAD_DIM ** -0.5

    def body(q_ref, kvc_ref, kvp_ref, bucket_ref, rb_ref, sk_ref, o_ref, p_ref, bias):
        n = pl.program_id(0)

        @pl.when(n == 0)
        def _():
            _swa_bias_init(bias, bucket_ref, rb_ref)

        later = jnp.minimum(n, 1)
        kk = _bf(jnp.concatenate([kvp_ref[:, 0:128], kvc_ref[:, 0:128]], axis=0))
        vv = _swa_zero_key0(_bf(jnp.concatenate([kvp_ref[:, 128:256], kvc_ref[:, 128:256]], axis=0)))
        first_col = lax.broadcasted_iota(jnp.int32, (SWA_ROWS, 2 * SWA_BLOCK), 1) == 0
        scores, values = {}, {}
        for kvh in range(2):
            qst = _bf(jnp.concatenate([q_ref[:, pl.ds((kvh * 4 + jj) * 128, 128)] for jj in range(4)], axis=0) * scale)
            values[kvh] = _lane_halves(vv, kvh)
            for odd, kx in enumerate(_lane_halves(kk, kvh)):
                scores[kvh, odd] = _dot_nt(qst, kx)
        probs = {}
        for (kvh, odd), s in scores.items():
            parts = []
            for jj in range(4):
                h = 2 * (kvh * 4 + jj) + odd
                for r0 in range(0, SWA_BLOCK, SWA_ROWS):
                    p, ps = _swa_probs(s[jj * SWA_BLOCK + r0:jj * SWA_BLOCK + r0 + SWA_ROWS],
                                       bias[later, h, pl.ds(r0, SWA_ROWS), :], sk_ref[0, h])
                    part = _bf(jnp.where(first_col, ps, p))
                    p_ref[pl.ds(r0, SWA_ROWS), pl.ds(h * 2 * SWA_BLOCK, 2 * SWA_BLOCK)] = part
                    parts.append(part)
            probs[kvh, odd] = jnp.concatenate(parts, axis=0)
        for kvh in range(2):
            ost = _dot(probs[kvh, 0], values[kvh][0]) + _dot(probs[kvh, 1], values[kvh][1])
            for jj in range(4):
                o_ref[:, pl.ds((kvh * 4 + jj) * 128, 128)] = ost[jj * SWA_BLOCK:(jj + 1) * SWA_BLOCK]

    smem = pl.BlockSpec(memory_space=pltpu.SMEM)
    return pl.pallas_call(
        body,
        grid=(nb,),
        in_specs=[
            pl.BlockSpec((SWA_BLOCK, 1024), lambda n: (n, Z_SQ // 1024)),
            pl.BlockSpec((SWA_BLOCK, 256), lambda n: (n, Z_SK // 256)),
            pl.BlockSpec((SWA_BLOCK, 256), lambda n: (jnp.maximum(n - 1, 0), Z_SK // 256)),
            _const_spec((SWA_BLOCK, 2 * SWA_BLOCK)), smem, smem,
        ],
        out_specs=[pl.BlockSpec((SWA_BLOCK, 1024), lambda n: (n, 0)),
                   pl.BlockSpec((SWA_BLOCK, SWA_HEADS * 2 * SWA_BLOCK), lambda n: (n, 0))],
        out_shape=[jax.ShapeDtypeStruct((S, 1024), F32),
                   jax.ShapeDtypeStruct((S, SWA_HEADS * 2 * SWA_BLOCK), BF16)],
        scratch_shapes=[pltpu.VMEM((2, SWA_HEADS, SWA_BLOCK, 2 * SWA_BLOCK), F32)],
        compiler_params=_cparams("arbitrary"),
        name="swa_fwd",
    )(zmain, zmain, zmain, bucket, rel_bias, sinks)


def _swa_bwd(zmain, o_b, probs, d_o, bucket, dep):
    S = zmain.shape[0]
    nb = S // SWA_BLOCK
    scale = SWA_HEAD_DIM ** -0.5

    def body(q_ref, kvc_ref, kvp_ref, o_ref, p_ref, do_ref, bucket_ref, dep_ref,
             dq_ref, dkv_ref, drb_ref, dsk_ref, dbias, carry):
        del dep_ref
        n = pl.program_id(0)

        @pl.when(n == 0)
        def _():
            dbias[...] = jnp.zeros_like(dbias)
            carry[...] = jnp.zeros_like(carry)

        @pl.when(n < nb)
        def _():
            kk = _swa_zero_key0(_bf(jnp.concatenate([kvp_ref[:, 0:128], kvc_ref[:, 0:128]], axis=0)))
            vv = _swa_zero_key0(_bf(jnp.concatenate([kvp_ref[:, 128:256], kvc_ref[:, 128:256]], axis=0)))
            lane = lax.broadcasted_iota(jnp.int32, (2 * SWA_BLOCK, 128), 1)
            lane_q = lax.broadcasted_iota(jnp.int32, (4 * SWA_BLOCK, 128), 1)
            pair_cols = {kvh: [pl.ds((kvh * 4 + jj) * 128, 128) for jj in range(4)] for kvh in range(2)}
            qst, dost, ks, d_p, delta = {}, {}, {}, {}, {}
            for kvh in range(2):
                qst[kvh] = _bf(jnp.concatenate([q_ref[:, cl] for cl in pair_cols[kvh]], axis=0) * scale)
                dost[kvh] = jnp.concatenate([do_ref[:, cl] for cl in pair_cols[kvh]], axis=0)
                prod = dost[kvh].astype(F32) * jnp.concatenate([o_ref[:, cl] for cl in pair_cols[kvh]], axis=0)
                ks[kvh] = _lane_halves(kk, kvh)
                for odd, vx in enumerate(_lane_halves(vv, kvh)):
                    keep = (lane_q >= 64) if odd else (lane_q < 64)
                    delta[kvh, odd] = jnp.sum(jnp.where(keep, prod, 0.0), axis=-1, keepdims=True)
                    d_p[kvh, odd] = _dot_nt(dost[kvh], vx)
            pst, dsst = {}, {}
            for (kvh, odd), dp in d_p.items():
                p_parts, ds_parts = [], []
                for jj in range(4):
                    h = 2 * (kvh * 4 + jj) + odd
                    rows = slice(jj * SWA_BLOCK, (jj + 1) * SWA_BLOCK)
                    p = p_ref[:, pl.ds(h * 2 * SWA_BLOCK, 2 * SWA_BLOCK)]
                    ds = _f32(p) * (dp[rows] - delta[kvh, odd][rows])
                    dbias[h] += ds
                    p_parts.append(p)
                    ds_parts.append(_bf(ds))
                pst[kvh, odd] = jnp.concatenate(p_parts, axis=0)
                dsst[kvh, odd] = jnp.concatenate(ds_parts, axis=0)
            dk_parts, dv_parts = [], []
            for kvh in range(2):
                dq_st = _dot(dsst[kvh, 0], ks[kvh][0]) + _dot(dsst[kvh, 1], ks[kvh][1])
                for jj in range(4):
                    dq_ref[:, pair_cols[kvh][jj]] = _bf(dq_st[jj * SWA_BLOCK:(jj + 1) * SWA_BLOCK] * scale)
                zk = jnp.where(lane < 64, _dot_tn(dsst[kvh, 0], qst[kvh]), _dot_tn(dsst[kvh, 1], qst[kvh]))
                zv = jnp.where(lane < 64, _dot_tn(pst[kvh, 0], dost[kvh]), _dot_tn(pst[kvh, 1], dost[kvh]))
                dk_parts.append(zk + pltpu.roll(zk, 64, 1))
                dv_parts.append(zv + pltpu.roll(zv, 64, 1))
            dk = jnp.where(lane < 64, dk_parts[0], dk_parts[1])
            dv = jnp.where(lane < 64, dv_parts[0], dv_parts[1])
            dkv = _swa_zero_key0(jnp.concatenate([dk, dv], axis=1))
            dkv_ref[...] = _bf(carry[...] + dkv[0:SWA_BLOCK])
            carry[...] = dkv[SWA_BLOCK:]

        @pl.when(n == nb)
        def _():
            dkv_ref[...] = _bf(carry[...])
            first_col = lax.broadcasted_iota(jnp.int32, (SWA_BLOCK, 2 * SWA_BLOCK), 1) == 0
            bk = jnp.where(first_col, -1, bucket_ref[...])

            row = lax.broadcasted_iota(jnp.int32, (NUM_BUCKETS, 128), 0)
            lane = lax.broadcasted_iota(jnp.int32, (NUM_BUCKETS, 128), 1)

            def total(v):
                return jnp.sum(jnp.sum(v, axis=1, keepdims=True), axis=0, keepdims=True)

            def per_head(h, acc):
                db = dbias[h]
                d_rb, d_sk = acc
                d_sk = d_sk + jnp.where((row == 0) & (lane == h), total(jnp.where(first_col, db, 0.0)), 0.0)

                def per_bucket(b, d_rb):
                    return d_rb + jnp.where((row == b) & (lane == h), total(jnp.where(bk == b, db, 0.0)), 0.0)

                return lax.fori_loop(0, NUM_BUCKETS, per_bucket, d_rb), d_sk

            zero = jnp.zeros((NUM_BUCKETS, 128), F32)
            d_rb, d_sk = lax.fori_loop(0, SWA_HEADS, per_head, (zero, zero))
            drb_ref[...] = d_rb
            dsk_ref[...] = d_sk[0:8]

    cur = lambda n: jnp.minimum(n, nb - 1)
    prev = lambda n: jnp.maximum(jnp.minimum(n, nb - 1) - 1, 0)
    return pl.pallas_call(
        body,
        grid=(nb + 1,),
        in_specs=[
            pl.BlockSpec((SWA_BLOCK, 1024), lambda n: (cur(n), Z_SQ // 1024)),
            pl.BlockSpec((SWA_BLOCK, 256), lambda n: (cur(n), Z_SK // 256)),
            pl.BlockSpec((SWA_BLOCK, 256), lambda n: (prev(n), Z_SK // 256)),
            pl.BlockSpec((SWA_BLOCK, 1024), lambda n: (cur(n), 0)),
            pl.BlockSpec((SWA_BLOCK, SWA_HEADS * 2 * SWA_BLOCK), lambda n: (cur(n), 0)),
            pl.BlockSpec((SWA_BLOCK, 1024), lambda n: (cur(n), 0)),
            _const_spec((SWA_BLOCK, 2 * SWA_BLOCK)), _dep_spec(),
        ],
        out_specs=[
            pl.BlockSpec((SWA_BLOCK, 1024), lambda n: (cur(n), 0)),
            pl.BlockSpec((SWA_BLOCK, 256), lambda n: (jnp.maximum(n - 1, 0), 0)),
            pl.BlockSpec((NUM_BUCKETS, 128), lambda n: (0, 0)),
            pl.BlockSpec((8, 128), lambda n: (0, 0)),
        ],
        out_shape=[
            jax.ShapeDtypeStruct((S, 1024), BF16),
            jax.ShapeDtypeStruct((S, 256), BF16),
            jax.ShapeDtypeStruct((NUM_BUCKETS, 128), F32),
            jax.ShapeDtypeStruct((8, 128), F32),
        ],
        scratch_shapes=[
            pltpu.VMEM((SWA_HEADS, SWA_BLOCK, 2 * SWA_BLOCK), F32),
            pltpu.VMEM((SWA_BLOCK, 256), F32),
        ],
        compiler_params=_cparams("arbitrary"),
        name="swa_bwd",
    )(zmain, zmain, zmain, o_b, probs, d_o, bucket, dep)


def _mem_q_specs(T):
    return [pl.BlockSpec((T, MEM_HEAD_DIM), lambda t, h=h: (t, Z_MQ // MEM_HEAD_DIM + h)) for h in range(MEM_HEADS)]


def _mem_kv_proj(mem, g2):
    def body(mem_ref, w_ref, o_ref):
        o_ref[...] = _dot_nt(_bf(mem_ref[...]), _rows(w_ref))

    return pl.pallas_call(
        body,
        grid=(1,),
        in_specs=[pl.BlockSpec((MEM_LEN, D_MODEL), lambda i: (0, 0)), _gathered_spec(R_KV, R_OTHER)],
        out_specs=pl.BlockSpec((MEM_LEN, 2048), lambda i: (0, 0)),
        out_shape=jax.ShapeDtypeStruct((MEM_LEN, 2048), F32),
        compiler_params=_cparams("arbitrary"),
        name="mem_kv_proj",
    )(mem, g2)


def _mem_fwd(zmain, mkv, *, T):
    S = zmain.shape[0]

    def body(q0, q1, q2, q3, kv_ref, o_ref, p_ref):
        heads = [pl.ds(h * MEM_HEAD_DIM, MEM_HEAD_DIM) for h in range(MEM_HEADS)]
        scores = [_dot_nt(_bf(q_ref[...] * (MEM_HEAD_DIM ** -0.5)), _bf(kv_ref[:, cols]))
                  for q_ref, cols in zip((q0, q1, q2, q3), heads)]
        probs = []
        for s, cols in zip(scores, heads):
            e = jnp.exp(s - jnp.max(s, axis=-1, keepdims=True))
            pb = _bf(e * (1.0 / jnp.sum(e, axis=-1, keepdims=True)))
            p_ref[:, cols] = pb
            probs.append(pb)
        for h, (pb, cols) in enumerate(zip(probs, heads)):
            o_ref[:, cols] = _dot(pb, _bf(kv_ref[:, pl.ds(1024 + h * MEM_HEAD_DIM, MEM_HEAD_DIM)]))

    row = pl.BlockSpec((T, 1024), lambda t: (t, 0))
    return pl.pallas_call(
        body,
        grid=(S // T,),
        in_specs=_mem_q_specs(T) + [_const_spec((MEM_LEN, 2048))],
        out_specs=[row, row],
        out_shape=[jax.ShapeDtypeStruct((S, 1024), F32), jax.ShapeDtypeStruct((S, 1024), BF16)],
        compiler_params=_cparams("parallel"),
        name="mem_fwd",
    )(zmain, zmain, zmain, zmain, mkv)


def _mem_bwd(zmain, mkv, o_c, probs, d_o, *, T):
    S = zmain.shape[0]
    scale = MEM_HEAD_DIM ** -0.5

    def body(q0, q1, q2, q3, kv_ref, o_ref, p_ref, do_ref, dq_ref, dkv_ref):
        @pl.when(pl.program_id(0) == 0)
        def _():
            dkv_ref[...] = jnp.zeros_like(dkv_ref)

        heads = [(pl.ds(h * MEM_HEAD_DIM, MEM_HEAD_DIM), pl.ds(1024 + h * MEM_HEAD_DIM, MEM_HEAD_DIM))
                 for h in range(MEM_HEADS)]
        d_p = [_dot_nt(do_ref[:, cols], _bf(kv_ref[:, vcols])) for cols, vcols in heads]
        d_s = []
        for dp, (cols, _) in zip(d_p, heads):
            delta = jnp.sum(do_ref[:, cols].astype(F32) * o_ref[:, cols], axis=-1, keepdims=True)
            d_s.append(_bf(_f32(p_ref[:, cols]) * (dp - delta)))
        for ds, q_ref, (cols, vcols) in zip(d_s, (q0, q1, q2, q3), heads):
            dq_ref[:, cols] = _bf(_dot(ds, _bf(kv_ref[:, cols])) * scale)
            dkv_ref[:, cols] += _dot_tn(ds, _bf(q_ref[...] * scale))
            dkv_ref[:, vcols] += _dot_tn(p_ref[:, cols], do_ref[:, cols])

    row = pl.BlockSpec((T, 1024), lambda t: (t, 0))
    return pl.pallas_call(
        body,
        grid=(S // T,),
        in_specs=_mem_q_specs(T) + [_const_spec((MEM_LEN, 2048)), row, row, row],
        out_specs=[row, pl.BlockSpec((MEM_LEN, 2048), lambda t: (0, 0))],
        out_shape=[jax.ShapeDtypeStruct((S, 1024), BF16), jax.ShapeDtypeStruct((MEM_LEN, 2048), F32)],
        compiler_params=_cparams("arbitrary"),
        name="mem_bwd",
    )(zmain, zmain, zmain, zmain, mkv, o_c, probs, d_o)


def _layer_norm(u):
    mu = jnp.mean(u, axis=-1, keepdims=True)
    xc = u - mu
    rstd = lax.rsqrt(jnp.mean(xc * xc, axis=-1, keepdims=True) + LN_EPS)
    return xc * rstd, rstd


def _layer_norm_bwd(dy, gamma, xhat, rstd):
    dxh = dy * gamma
    return rstd * (dxh - jnp.mean(dxh, axis=-1, keepdims=True) - xhat * jnp.mean(dxh * xhat, axis=-1, keepdims=True))


def _merge_stages(rows, oraw_ref, hg_ref, ob_ref, oc_ref, gl_ref, x_ref, gain_ref, wbh, wbs, wbm, wout, g_ref, b_ref,
                  fwd_out=None, bwd=None, saved=None):
    ys, rs = [], []
    for h in range(HG_HEADS):
        oh = oraw_ref[rows, pl.ds(h * HG_DK, HG_DK)]
        r = lax.rsqrt(jnp.mean(oh * oh, axis=-1, keepdims=True) + RMS_EPS)
        ys.append(oh * r)
        rs.append(r)
    y = jnp.concatenate(ys, axis=1)
    hg = _f32(hg_ref[rows, :])
    sg = _sig(hg)
    silu = hg * sg
    gain = gain_ref[...]
    gates = [_sig(_f32(gl_ref[rows, pl.ds(i * 1024, 1024)])) for i in range(3)]
    if saved is None:
        oa = _bf(y * gain * silu)
        pa = _dot(oa, _rows(wbh))
        pb = _dot(_bf(ob_ref[rows, :]), _rows(wbs))
        pc = _dot(_bf(oc_ref[rows, :]), _rows(wbm))
        yield
        m = _bf(gates[0] * pa + gates[1] * pb + gates[2] * pc)
    else:
        pa, pb, pc = (_f32(r[rows, :]) for r in saved[:3])
        m = saved[3][rows, :]
    mix = _dot(m, _rows(wout))
    yield
    xhat, rstd = _layer_norm(ALPHA * x_ref[rows, :] + mix)
    if bwd is None:
        h1 = xhat * g_ref[...] + b_ref[...]
        fwd_out[0][rows, :] = h1
        fwd_out[1][rows, :] = _bf(h1)
        for ref, val in zip(fwd_out[2:], (_bf(pa), _bf(pb), _bf(pc), m, oa)):
            ref[rows, :] = val
        return
    (dh1_ref, dx_ref, du1_ref, dpa_ref, dpb_ref, dpc_ref, doraw_ref, dob_ref, doc_ref, dz_ref,
     dgain_ref, dg_ref, db_ref) = bwd
    dh1 = dh1_ref[rows, :]
    dg_ref[...] += jnp.sum(dh1 * xhat, axis=0, keepdims=True)
    db_ref[...] += jnp.sum(dh1, axis=0, keepdims=True)
    du1 = _layer_norm_bwd(dh1, g_ref[...], xhat, rstd)
    dx_ref[rows, :] = ALPHA * du1
    du1b = _bf(du1)
    du1_ref[rows, :] = du1b
    dm = _dot_nt(du1b, _rows(wout))
    yield
    d_branches = []
    for i, (g, p, dp_ref, w_r) in enumerate(zip(gates, (pa, pb, pc), (dpa_ref, dpb_ref, dpc_ref), (wbh, wbs, wbm))):
        dz_ref[rows, pl.ds((i + 1) * 1024, 1024)] = _bf(dm * p * g * (1.0 - g))
        dp = _bf(dm * g)
        dp_ref[rows, :] = dp
        d_branches.append(_dot_nt(dp, _rows(w_r)))
    yield
    doa, d_ob, d_oc = d_branches
    dob_ref[rows, :] = _bf(d_ob)
    doc_ref[rows, :] = _bf(d_oc)
    t = doa * y
    dgain_ref[...] += jnp.sum(t * silu, axis=0, keepdims=True)
    dz_ref[rows, 0:1024] = _bf(t * gain * sg * (1.0 + hg * (1.0 - sg)))
    dy = doa * gain * silu
    for h in range(HG_HEADS):
        cols = slice(h * HG_DK, (h + 1) * HG_DK)
        yh = y[:, cols]
        dyh = dy[:, cols]
        doraw_ref[rows, pl.ds(h * HG_DK, HG_DK)] = _bf(rs[h] * (dyh - yh * jnp.mean(dyh * yh, axis=-1, keepdims=True)))


def _interleave(chains):
    live = list(chains)
    while live:
        still = []
        for c in live:
            try:
                next(c)
                still.append(c)
            except StopIteration:
                pass
        live = still


def _gathered_spec(lo, hi):
    n = hi - lo
    return pl.BlockSpec((N_DEV, n, D_MODEL), lambda *_: (0, lo // n, 0), pipeline_mode=pl.Buffered(1))


def _rows(w_ref):
    return w_ref[...].reshape(-1, D_MODEL)


def _merge_in_specs(T):
    row = lambda w, c=0: pl.BlockSpec((T, w), lambda i: (i, c))
    vec = pl.BlockSpec((1, D_MODEL), lambda i: (0, 0))
    w = [_gathered_spec(lo, hi) for lo, hi in ((R_BH, R_BS), (R_BS, R_BM), (R_BM, R_OUT), (R_OUT, R_KV))]
    return [row(1024), row(1024, Z_HG // 1024), row(1024), row(1024), row(3072), row(1024), vec, *w, vec, vec]


def _merge_fwd(o_raw, zmain, o_b, o_c, gl, x, gain, wbh, wbs, wbm, wout, ln_g, ln_b, *, T):
    S = x.shape[0]

    def body(*refs):
        ins, outs = refs[:13], refs[13:]
        _interleave(_merge_stages(pl.ds(r0, T // MERGE_GROUPS), *ins, fwd_out=outs)
                    for r0 in range(0, T, T // MERGE_GROUPS))

    row = pl.BlockSpec((T, D_MODEL), lambda i: (i, 0))
    return pl.pallas_call(
        body,
        grid=(S // T,),
        in_specs=_merge_in_specs(T),
        out_specs=[row] * 7,
        out_shape=[jax.ShapeDtypeStruct((S, D_MODEL), F32)] + [jax.ShapeDtypeStruct((S, D_MODEL), BF16)] * 6,
        compiler_params=_cparams("parallel"),
        name="merge_fwd",
    )(o_raw, zmain, o_b, o_c, gl, x, gain, wbh, wbs, wbm, wout, ln_g, ln_b)


def _merge_bwd(d_h1, pa, pb, pc, m, o_raw, zmain, gl, x, gain, wbh, wbs, wbm, wout, ln_g, *, T):
    S = x.shape[0]

    def body(dh1_ref, pa_ref, pb_ref, pc_ref, m_ref, oraw_ref, hg_ref, gl_ref, x_ref, gain_ref, wbh_r, wbs_r, wbm_r, wout_r,
             g_ref, dx_ref, du1_ref, dpa_ref, dpb_ref, dpc_ref, doraw_ref, dob_ref, doc_ref, dz_ref,
             dgain_ref, dg_ref, db_ref):
        @pl.when(pl.program_id(0) == 0)
        def _():
            dgain_ref[...] = jnp.zeros_like(dgain_ref)
            dg_ref[...] = jnp.zeros_like(dg_ref)
            db_ref[...] = jnp.zeros_like(db_ref)

        ins = (oraw_ref, hg_ref, None, None, gl_ref, x_ref, gain_ref, wbh_r, wbs_r, wbm_r, wout_r, g_ref, None)
        bwd = (dh1_ref, dx_ref, du1_ref, dpa_ref, dpb_ref, dpc_ref, doraw_ref, dob_ref, doc_ref, dz_ref,
               dgain_ref, dg_ref, db_ref)
        _interleave([_merge_stages(pl.ds(0, T), *ins, bwd=bwd, saved=(pa_ref, pb_ref, pc_ref, m_ref))])

    row = lambda w, c=0: pl.BlockSpec((T, w), lambda i: (i, c))
    vec = pl.BlockSpec((1, D_MODEL), lambda i: (0, 0))
    w = [_gathered_spec(lo, hi) for lo, hi in ((R_BH, R_BS), (R_BS, R_BM), (R_BM, R_OUT), (R_OUT, R_KV))]
    bshape = jax.ShapeDtypeStruct((S, D_MODEL), BF16)
    vshape = jax.ShapeDtypeStruct((1, D_MODEL), F32)
    return pl.pallas_call(
        body,
        grid=(S // T,),
        in_specs=[row(1024)] * 6 + [row(1024, Z_HG // 1024), row(3072), row(1024), vec, *w, vec],
        out_specs=[row(1024)] * 8 + [row(4096), vec, vec, vec],
        out_shape=[jax.ShapeDtypeStruct((S, D_MODEL), F32)] + [bshape] * 7
        + [jax.ShapeDtypeStruct((S, 4096), BF16), vshape, vshape, vshape],
        compiler_params=_cparams("arbitrary"),
        name="merge_bwd",
    )(d_h1, pa, pb, pc, m, o_raw, zmain, gl, x, gain, wbh, wbs, wbm, wout, ln_g)


def _mlp_fwd_bwd(h1, target, wup_t, wdn, ln_g, ln_b, *, T, FC):
    S = h1.shape[0]
    nf = D_FF // FC
    assert FC == R_BH - R_UP == R_UP - R_DN

    def body(h1_ref, t_ref, wup_ref, wdn_ref, g_ref, b_ref, dh1_ref, a_ref, dup_ref, du2_ref, loss_ref, dg_ref, db_ref, up_scr):
        @pl.when(pl.program_id(0) == 0)
        def _():
            loss_ref[...] = jnp.zeros_like(loss_ref)
            dg_ref[...] = jnp.zeros_like(dg_ref)
            db_ref[...] = jnp.zeros_like(db_ref)

        h1v = h1_ref[...]
        h1b = _bf(h1v)
        ff = jnp.zeros((T, D_MODEL), F32)
        for j in range(nf):
            rows = pl.ds(j * FC, FC)
            up = jnp.maximum(_dot_nt(h1b, wup_ref[j]), 0.0)
            up_scr[:, rows] = _bf(up)
            a = _bf(up * up)
            a_ref[:, rows] = a
            ff = ff + _dot(a, wdn_ref[j])
        xhat, rstd = _layer_norm(ALPHA * h1v + ff)
        gamma = g_ref[...]
        err = xhat * gamma + b_ref[...] - t_ref[...]
        loss_ref[...] += jnp.sum(jnp.sum(err * err, axis=-1, keepdims=True), axis=0, keepdims=True) * (0.5 / D_MODEL)
        dy = err * (1.0 / D_MODEL)
        dg_ref[...] += jnp.sum(dy * xhat, axis=0, keepdims=True)
        db_ref[...] += jnp.sum(dy, axis=0, keepdims=True)
        du2 = _layer_norm_bwd(dy, gamma, xhat, rstd)
        du2b = _bf(du2)
        du2_ref[...] = du2b
        dh1 = ALPHA * du2
        for j in range(nf):
            rows = pl.ds(j * FC, FC)
            dup = _bf(_dot_nt(du2b, wdn_ref[j]) * (2.0 * up_scr[:, rows].astype(F32)))
            dup_ref[:, rows] = dup
            dh1 = dh1 + _dot(dup, wup_ref[j])
        dh1_ref[...] = dh1

    row = lambda w: pl.BlockSpec((T, w), lambda i: (i, 0))
    vec = pl.BlockSpec((1, D_MODEL), lambda i: (0, 0))
    vshape = jax.ShapeDtypeStruct((1, D_MODEL), F32)
    return pl.pallas_call(
        body,
        grid=(S // T,),
        in_specs=[row(1024), row(1024), _gathered_spec(R_UP, R_BH), _gathered_spec(R_DN, R_UP), vec, vec],
        out_specs=[row(1024), row(D_FF), row(D_FF), row(1024), pl.BlockSpec((8, 128), lambda i: (0, 0)), vec, vec],
        out_shape=[
            jax.ShapeDtypeStruct((S, D_MODEL), F32),
            jax.ShapeDtypeStruct((S, D_FF), BF16),
            jax.ShapeDtypeStruct((S, D_FF), BF16),
            jax.ShapeDtypeStruct((S, D_MODEL), BF16),
            jax.ShapeDtypeStruct((8, 128), F32), vshape, vshape,
        ],
        scratch_shapes=[pltpu.VMEM((T, D_FF), BF16)],
        compiler_params=_cparams("arbitrary"),
        name="mlp_fwd_bwd",
    )(h1, target, wup_t, wdn, ln_g, ln_b)


def _local_step(x, mem, target, lb_logits, gain, sinks, rel_bias, ln1_g, ln1_b, ln2_g, ln2_b,
                win_t, dep0, other_weights, send_other_grads, send_small_grads, send_win_grad):
    S = x.shape[0]
    T = min(256, S)
    KC = min(2048, S)
    z_qfv, zmain, gl, xb = _in_proj(x, win_t, dep0, tm=min(512, S))
    bucket = _t5_bucket_table()

    o_raw, states, sig_f, cum_b = _hgrn_fwd(z_qfv, lb_logits, T=min(2048, S))
    o_b, swa_probs = _swa_fwd(zmain, bucket, rel_bias, sinks)
    g2 = other_weights((o_b, o_raw))
    mkv = _mem_kv_proj(mem, g2)
    o_c, mem_probs = _mem_fwd(zmain, mkv, T=min(1024, S))
    h1, h1b, pa, pb, pc, m, oa = _merge_fwd(o_raw, zmain, o_b, o_c, gl, x, gain, g2, g2, g2, g2, ln1_g, ln1_b,
                                                T=min(512, S))

    d_h1, act, d_up, du2, loss, d_ln2_g, d_ln2_b = _mlp_fwd_bwd(h1, target, g2, g2, ln2_g, ln2_b, T=min(512, S), FC=512)
    wgrad = functools.partial(_mm_tn, out_dtype=BF16)
    halves = lambda r0: (lambda i: (i // 2, r0 // 256 + i % 2, 0))
    whole = lambda r0: (lambda i: (0, r0 // 128, 0))
    og = lax.empty((N_DEV, R_OTHER, D_MODEL), BF16)
    og = wgrad(act, du2, kc=KC, name="grad_w_down", into=(og, (1, 256, D_MODEL), halves(R_DN)))
    og = wgrad(d_up, h1b, kc=KC, name="grad_w_up", into=(og, (1, 256, D_MODEL), halves(R_UP)))

    (dx_part, du1, dpa, dpb, dpc, d_oraw, d_ob, d_oc, d_hg_gl, d_gain, d_ln1_g, d_ln1_b) = _merge_bwd(
        d_h1, pa, pb, pc, m, o_raw, zmain, gl, x, gain, g2, g2, g2, g2, ln1_g, T=T)
    for a_op, b_op, r0, nm in ((m, du1, R_OUT, "out"), (oa, dpa, R_BH, "branch_hg"), (o_b, dpb, R_BS, "branch_swa"),
                               (o_c, dpc, R_BM, "branch_mem")):
        og = wgrad(a_op, b_op, kc=KC, name="grad_w_" + nm, into=(og, (N_DEV, 128, D_MODEL), whole(r0)))

    d_mq, d_mkv = _mem_bwd(zmain, mkv, o_c, mem_probs, d_oc, T=min(1024, S))
    og = wgrad(d_mkv, mem, kc=MEM_LEN, name="grad_w_mem_kv",
               into=(og, (1, 256, D_MODEL), lambda i: (i, R_KV // 256, 0)))
    sent_others = send_other_grads(og)
    d_sq, d_skv, d_rb, d_sink = _swa_bwd(zmain, o_b, swa_probs, d_ob, bucket, sent_others)
    d_qfv, d_lb = _hgrn_bwd(z_qfv, sig_f, cum_b, lb_logits, states, d_oraw, T=min(2048, S))
    sent_small = send_small_grads(_pack_small_grads(d_lb, d_gain, d_sink, d_rb, d_ln1_g, d_ln1_b, d_ln2_g, d_ln2_b, loss))

    head_major = lambda a: a.reshape(3, HG_HEADS, HG_DK, D_MODEL).transpose(1, 0, 2, 3).reshape(3 * D_MODEL, D_MODEL)
    pieces = (d_qfv, d_hg_gl, d_sq, d_skv, d_mq)
    placed = (
        ("qfv", d_qfv, 128, lambda i: ((i % 3) * HG_HEADS + i // 3, 0)),
        ("hg_gates", d_hg_gl, 256, lambda i: (jnp.where(i < 4, C_HG // 256 + i, C_GL // 256 + i - 4), 0)),
        ("swa_q", d_sq, None, lambda i: (C_SQ // 1024, 0)),
        ("swa_kv", d_skv, None, lambda i: (C_SK // 256, 0)),
        ("mem_q", d_mq, 256, lambda i: (C_MQ // 256 + i, 0)),
    )
    g_win_t = lax.empty((IN_COLS, D_MODEL), BF16)
    for nm, piece, tile, index in placed:
        g_win_t = wgrad(piece, xb, kc=KC, name="grad_w_in_" + nm, tm=tile,
                        into=(g_win_t, (tile or piece.shape[1], D_MODEL), index))
    sent_win = send_win_grad(g_win_t, sent_small)
    return _grad_x(*pieces, head_major(win_t[:C_HG]), win_t, dx_part, sent_win, tm=T)


MESH = pl.DeviceIdType.MESH
ANY = pl.BlockSpec(memory_space=pl.ANY)


def _coords():
    return lax.axis_index("x"), lax.axis_index("y"), lax.axis_index("c")


def _other_chips(x, y):
    return [(1 - x, y), (x, 1 - y), (1 - x, 1 - y)]


def _all_gather_weights(*arrays):
    na = len(arrays)

    def body(*refs):
        srcs, dsts = refs[:na], refs[na:2 * na]
        send_sems, recv_sems, local_sems = refs[2 * na:]
        x, y, c = _coords()
        me, sibling = (x, y, c), (x, y, 1 - c)
        chips = _other_chips(x, y)

        def slot(a, px, py, pc):
            return dsts[a].at[4 * px + 2 * py + pc]

        def copy(a, k, block, to, from_shard=False):
            return pltpu.make_async_remote_copy(
                src_ref=srcs[a] if from_shard else slot(a, *block), dst_ref=slot(a, *block),
                send_sem=send_sems.at[a * 7 + k], recv_sem=recv_sems.at[a * 7 + k],
                device_id=to, device_id_type=MESH)

        own = [pltpu.make_async_copy(srcs[a], slot(a, *me), local_sems.at[a]) for a in range(na)]
        for cp in own:
            cp.start()
        first = []
        for a in range(na):
            first.append(copy(a, 0, me, sibling, True))
            first += [copy(a, 1 + j, me, (*chip, c), True) for j, chip in enumerate(chips)]
        for cp in first:
            cp.start()
        passed = []
        for j, chip in enumerate(chips):
            for a in range(na):
                copy(a, 1 + j, (*chip, c), me).wait_recv()
                fwd = copy(a, 4 + j, (*chip, c), sibling)
                fwd.start()
                passed.append(fwd)
        for a in range(na):
            copy(a, 0, sibling, me).wait_recv()
            for j, chip in enumerate(chips):
                copy(a, 4 + j, (*chip, 1 - c), me).wait_recv()
        for cp in first + passed:
            cp.wait_send()
        for cp in own:
            cp.wait()

    return pl.pallas_call(
        body,
        in_specs=[ANY] * na,
        out_specs=[ANY] * na,
        out_shape=[jax.ShapeDtypeStruct((N_DEV,) + a.shape, a.dtype) for a in arrays],
        scratch_shapes=[pltpu.SemaphoreType.DMA((7 * na,)), pltpu.SemaphoreType.DMA((7 * na,)),
                        pltpu.SemaphoreType.DMA((na,))],
        name="all_gather_weights",
    )(*arrays)


HBM = pl.BlockSpec(memory_space=pltpu.HBM)
SEM = pl.BlockSpec(memory_space=pltpu.SEMAPHORE)
_DATAFLOW = pltpu.SideEffectType.DATAFLOW_SIDE_EFFECTING


def _peer(x, y, c, r):
    return x ^ (r >> 2), y ^ ((r >> 1) & 1), c ^ (r & 1)


def _direct_copies(src_ref, land_ref, send_sems, recv_sems, gather, receiving):
    x, y, c = _coords()
    me = 4 * x + 2 * y + c
    copies = []
    for r in range(1, N_DEV):
        px, py, pc = _peer(x, y, c, r)
        peer = 4 * px + 2 * py + pc
        if gather:
            src, dst = src_ref, land_ref.at[peer if receiving else me]
        else:
            src, dst = src_ref.at[peer], land_ref.at[r - 1]
        copies.append(pltpu.make_async_remote_copy(
            src_ref=src, dst_ref=dst, send_sem=send_sems.at[r - 1], recv_sem=recv_sems.at[r - 1],
            device_id=(px, py, pc), device_id_type=MESH))
    return copies


def _direct_start(src, land, *, gather, name, after=None):
    def body(src_ref, land_ref, *rest):
        send_sems, recv_sems, token = rest[-5], rest[-4], rest[-1]
        for cp in _direct_copies(src_ref, land_ref, send_sems, recv_sems, gather, False):
            cp.start()
        token[...] = jnp.zeros_like(token)

    afters = () if after is None else (after,)
    return pl.pallas_call(
        body,
        name=name,
        out_shape=(pltpu.SemaphoreType.DMA((N_DEV - 1,)), pltpu.SemaphoreType.DMA((N_DEV - 1,)),
                   pltpu.HBM(src.shape, src.dtype), pltpu.HBM(land.shape, land.dtype),
                   jax.ShapeDtypeStruct((8, 128), F32)),
        in_specs=(HBM, HBM) + tuple(ANY for _ in afters),
        out_specs=(SEM, SEM, HBM, HBM, pl.BlockSpec(memory_space=pltpu.VMEM)),
        input_output_aliases={0: 2, 1: 3},
        compiler_params=pltpu.CompilerParams(has_side_effects=_DATAFLOW),
    )(pltpu.with_memory_space_constraint(src, pltpu.HBM), pltpu.with_memory_space_constraint(land, pltpu.HBM), *afters)


def _direct_wait(send_sems, recv_sems, src_thru, land_thru, after, *, gather, name):
    afters = after if isinstance(after, tuple) else (after,)

    def body(src_ref, land_ref, send_sems_ref, recv_sems_ref, *rest):
        del rest
        for cp in _direct_copies(src_ref, land_ref, send_sems_ref, recv_sems_ref, gather, True):
            cp.wait_send()
            cp.wait_recv()

    return pl.pallas_call(
        body,
        name=name,
        out_shape=(pltpu.HBM(src_thru.shape, src_thru.dtype), pltpu.HBM(land_thru.shape, land_thru.dtype)),
        in_specs=(HBM, HBM, SEM, SEM) + tuple(ANY for _ in afters),
        out_specs=(HBM, HBM),
        input_output_aliases={0: 0, 1: 1},
        compiler_params=pltpu.CompilerParams(has_side_effects=_DATAFLOW),
    )(src_thru, land_thru, send_sems, recv_sems, *afters)


def _sum_partials(src, land, me, *, tr, name, wmv=None):
    R = src.shape[1]
    extra = () if wmv is None else tuple(wmv)

    def body(me_ref, s_ref, l_ref, *rest):
        del me_ref
        acc = s_ref[0].astype(F32)
        for r in range(N_DEV - 1):
            acc = acc + l_ref[r].astype(F32)
        rest[len(extra)][...] = acc
        if extra:
            w_ref, m_ref, v_ref, _, d_ref, nm_ref, nv_ref = rest
            d_ref[...], nm_ref[...], nv_ref[...] = _adam_step(w_ref[...], acc, m_ref[...], v_ref[...])

    row = pl.BlockSpec((tr, 1024), lambda i, mr: (i, 0))
    n_out = 4 if extra else 1
    res = pl.pallas_call(
        body,
        grid_spec=pltpu.PrefetchScalarGridSpec(
            num_scalar_prefetch=1, grid=(R // tr,),
            in_specs=[pl.BlockSpec((1, tr, 1024), lambda i, mr: (mr[0], i, 0)),
                      pl.BlockSpec((N_DEV - 1, tr, 1024), lambda i, mr: (0, i, 0))] + [row for _ in extra],
            out_specs=[row] * n_out),
        out_shape=[jax.ShapeDtypeStruct((R, 1024), F32)] * n_out,
        name=name,
    )(me, src, land, *extra)
    return res if extra else res[0]


_SMALL = ("lb_logits", "hg_norm_gain", "swa_sinks", "rel_bias", "ln1_g", "ln1_b", "ln2_g", "ln2_b")


def _pack_small_grads(d_lb, d_gain, d_sink, d_rb, d_ln1_g, d_ln1_b, d_ln2_g, d_ln2_b, loss):
    def body(lb_ref, gain_ref, sink_ref, rb_ref, l1g_ref, l1b_ref, l2g_ref, l2b_ref, loss_ref, o_ref):
        o_ref[...] = jnp.zeros_like(o_ref)
        for row, ref in ((SM_LB, lb_ref), (SM_GAIN, gain_ref), (SM_L1G, l1g_ref), (SM_L1B, l1b_ref),
                         (SM_L2G, l2g_ref), (SM_L2B, l2b_ref)):
            o_ref[row:row + 1, :] = ref[...]
        o_ref[SM_SINK:SM_SINK + 1, 0:128] = sink_ref[0:1, :]
        o_ref[SM_LOSS:SM_LOSS + 1, 0:128] = loss_ref[0:1, :]
        o_ref[SM_RB:SM_RB + NUM_BUCKETS, 0:128] = rb_ref[...]

    vm = pl.BlockSpec(memory_space=pltpu.VMEM)
    return pl.pallas_call(
        body,
        in_specs=[vm] * 9,
        out_specs=vm,
        out_shape=jax.ShapeDtypeStruct((SM_ROWS, D_MODEL), F32),
        name="pack_small_grads",
    )(d_lb, d_gain, d_sink, d_rb, d_ln1_g, d_ln1_b, d_ln2_g, d_ln2_b, loss)


def _small_finish(gathered, w, m, v):
    n = len(_SMALL)

    def body(*refs):
        g_ref = refs[0]
        w_refs, m_refs, v_refs = refs[1:1 + n], refs[1 + n:1 + 2 * n], refs[1 + 2 * n:1 + 3 * n]
        outs = refs[1 + 3 * n:]
        loss_ref, tot = outs[0], outs[-1]
        g_out, d_out, m_out, v_out = (outs[1 + k * n:1 + (k + 1) * n] for k in range(4))
        acc = g_ref[0]
        for d in range(1, N_DEV):
            acc = acc + g_ref[d]
        tot[...] = acc
        loss_ref[...] = tot[SM_LOSS:SM_LOSS + 1, 0:1]
        lb = _lower_bound(w_refs[0])
        dl0 = tot[SM_LB:SM_LB + 1, :] * lb * (1.0 - lb)
        grads = (jnp.concatenate([dl0, -dl0], axis=0), tot[SM_GAIN:SM_GAIN + 1, :],
                 tot[SM_SINK:SM_SINK + 1, 0:SWA_HEADS], tot[SM_RB:SM_RB + NUM_BUCKETS, 0:SWA_HEADS],
                 tot[SM_L1G:SM_L1G + 1, :], tot[SM_L1B:SM_L1B + 1, :], tot[SM_L2G:SM_L2G + 1, :], tot[SM_L2B:SM_L2B + 1, :])
        for k, g in enumerate(grads):
            g_out[k][...] = g
            d_out[k][...], m_out[k][...], v_out[k][...] = _adam_step(w_refs[k][...], g, m_refs[k][...], v_refs[k][...])

    vm = pl.BlockSpec(memory_space=pltpu.VMEM)
    shapes = [jax.ShapeDtypeStruct(w[k].shape, F32) for k in _SMALL]
    res = pl.pallas_call(
        body,
        in_specs=[vm] * (1 + 3 * n),
        out_specs=[vm] * (1 + 4 * n),
        out_shape=[jax.ShapeDtypeStruct((1, 1), F32)] + shapes * 4,
        scratch_shapes=[pltpu.VMEM((SM_ROWS, D_MODEL), F32)],
        name="small_finish",
    )(gathered, *[w[k] for k in _SMALL], *[m[k] for k in _SMALL], *[v[k] for k in _SMALL])
    parts = [dict(zip(_SMALL, res[1 + k * n:1 + (k + 1) * n])) for k in range(4)]
    return (res[0], *parts)


def _adam_step(w, g, m, v):
    nm = ADAM_B1 * m + (1.0 - ADAM_B1) * g
    nv = ADAM_B2 * v + (1.0 - ADAM_B2) * jnp.square(g)
    m_hat = nm / (1.0 - ADAM_B1 ** ADAM_STEP)
    v_hat = nv / (1.0 - ADAM_B2 ** ADAM_STEP)
    return -ADAM_LR * (m_hat / (jnp.sqrt(v_hat) + ADAM_EPS) + ADAM_WD * w), nm, nv


def _adamw(w, g, m, v, *, tr, name):
    R, C = w.shape

    def body(w_ref, g_ref, m_ref, v_ref, d_ref, nm_ref, nv_ref):
        d_ref[...], nm_ref[...], nv_ref[...] = _adam_step(w_ref[...], g_ref[...], m_ref[...], v_ref[...])

    spec = pl.BlockSpec((tr, C), lambda i: (i, 0))
    return pl.pallas_call(
        body,
        grid=(R // tr,),
        in_specs=[spec] * 4,
        out_specs=[spec] * 3,
        out_shape=[jax.ShapeDtypeStruct((R, C), F32)] * 3,
        compiler_params=_cparams("parallel"),
        name=name,
    )(w, g, m, v)


_WEIGHTS = ("w_in", "lb_logits", "hg_norm_gain", "swa_sinks", "rel_bias", "w_mem_kv", "w_branch_hg", "w_branch_swa",
            "w_branch_mem", "w_out", "ln1_g", "ln1_b", "w_up", "w_down", "ln2_g", "ln2_b")


def kernel(x, mem, w_in, lb_logits, hg_norm_gain, swa_sinks, rel_bias, w_mem_kv, w_branch_hg, w_branch_swa, w_branch_mem, w_out, ln1_g, ln1_b, w_up, w_down, ln2_g, ln2_b, loss_target, m_w_in, m_lb_logits, m_hg_norm_gain, m_swa_sinks, m_rel_bias, m_w_mem_kv, m_w_branch_hg, m_w_branch_swa, m_w_branch_mem, m_w_out, m_ln1_g, m_ln1_b, m_w_up, m_w_down, m_ln2_g, m_ln2_b, v_w_in, v_lb_logits, v_hg_norm_gain, v_swa_sinks, v_rel_bias, v_w_mem_kv, v_w_branch_hg, v_w_branch_swa, v_w_branch_mem, v_w_out, v_ln1_g, v_ln1_b, v_w_up, v_w_down, v_ln2_g, v_ln2_b):
    w = dict(w_in=w_in, lb_logits=lb_logits, hg_norm_gain=hg_norm_gain, swa_sinks=swa_sinks, rel_bias=rel_bias,
             w_mem_kv=w_mem_kv, w_branch_hg=w_branch_hg, w_branch_swa=w_branch_swa, w_branch_mem=w_branch_mem,
             w_out=w_out, ln1_g=ln1_g, ln1_b=ln1_b, w_up=w_up, w_down=w_down, ln2_g=ln2_g, ln2_b=ln2_b)
    mom = dict(w_in=m_w_in, lb_logits=m_lb_logits, hg_norm_gain=m_hg_norm_gain, swa_sinks=m_swa_sinks, rel_bias=m_rel_bias,
               w_mem_kv=m_w_mem_kv, w_branch_hg=m_w_branch_hg, w_branch_swa=m_w_branch_swa, w_branch_mem=m_w_branch_mem,
               w_out=m_w_out, ln1_g=m_ln1_g, ln1_b=m_ln1_b, w_up=m_w_up, w_down=m_w_down, ln2_g=m_ln2_g, ln2_b=m_ln2_b)
    var = dict(w_in=v_w_in, lb_logits=v_lb_logits, hg_norm_gain=v_hg_norm_gain, swa_sinks=v_swa_sinks, rel_bias=v_rel_bias,
               w_mem_kv=v_w_mem_kv, w_branch_hg=v_w_branch_hg, w_branch_swa=v_w_branch_swa, w_branch_mem=v_w_branch_mem,
               w_out=v_w_out, ln1_g=v_ln1_g, ln1_b=v_ln1_b, w_up=v_w_up, w_down=v_w_down, ln2_g=v_ln2_g, ln2_b=v_ln2_b)
    xc, yc, cc = _coords()

    p1 = _bf(w_in[0].T)
    p2 = _bf(jnp.concatenate([w_down[0], w_up[0].T, w_branch_hg[0], w_branch_swa[0], w_branch_mem[0], w_out[0],
                              w_mem_kv[0].T], axis=0))
    me = 4 * xc + 2 * yc + cc
    (g1,) = _all_gather_weights(p1)
    land2 = lax.dynamic_update_slice(lax.empty((N_DEV, R_OTHER, D_MODEL), BF16), p2[None], (me, 0, 0))
    ag2 = _direct_start(p2, land2, gather=True, name="gather_other_weights_start")

    def other_weights(after):
        return _direct_wait(*ag2[:4], after, gather=True, name="gather_other_weights_wait")[1]

    blocks = lambda a: a.reshape(N_DEV, a.shape[0] // N_DEV, D_MODEL)
    started = {}

    def send_other_grads(part):
        started["others"] = _direct_start(part, lax.empty((N_DEV - 1, R_OTHER, D_MODEL), BF16), gather=False,
                                          name="scatter_other_grads_start")
        return started["others"][4]

    me1 = me.reshape(1).astype(jnp.int32)
    grads, delta, new_m, new_v = {}, {}, {}, {}

    def adamw(name):
        w2 = w[name][0]
        delta[name], new_m[name], new_v[name] = _adamw(
            w2, grads[name], mom[name][0], var[name][0], tr=w2.shape[0] // 4, name="adamw_" + name)

    def send_small_grads(packed):
        land = lax.dynamic_update_slice(lax.empty((N_DEV, SM_ROWS, D_MODEL), F32), packed[None], (me, 0, 0))
        started["small"] = _direct_start(packed, land, gather=True, name="gather_small_grads_start")
        return started["small"][4]

    def send_win_grad(g, after):
        started["win"] = _direct_start(blocks(g), lax.empty((N_DEV - 1, IN_SHARD, D_MODEL), BF16), gather=False,
                                       name="scatter_w_in_grad_start", after=after)
        mine2, landed2 = _direct_wait(*started["others"][:4], started["win"][4], gather=False,
                                      name="scatter_other_grads_wait")
        gs2 = _sum_partials(mine2, landed2, me1, tr=R_OTHER // 2, name="sum_other_grads")
        grads.update(
            w_down=gs2[R_DN:R_UP], w_up=gs2[R_UP:R_BH].T, w_branch_hg=gs2[R_BH:R_BS], w_branch_swa=gs2[R_BS:R_BM],
            w_branch_mem=gs2[R_BM:R_OUT], w_out=gs2[R_OUT:R_KV], w_mem_kv=gs2[R_KV:R_OTHER].T)
        for name in ("w_mem_kv", "w_branch_hg", "w_branch_swa", "w_branch_mem", "w_out", "w_up", "w_down"):
            adamw(name)
        return tuple(new_v[name] for name in new_v)

    grad_x = _local_step(
        x[0], mem[0], loss_target[0], lb_logits, hg_norm_gain, swa_sinks, rel_bias, ln1_g, ln1_b, ln2_g, ln2_b,
        g1.reshape(IN_COLS, D_MODEL), ag2[4], other_weights, send_other_grads, send_small_grads, send_win_grad)

    mine1, landed1 = _direct_wait(*started["win"][:4], grad_x, gather=False, name="scatter_w_in_grad_wait")
    g_win_t, d_t, m_t, v_t = _sum_partials(mine1, landed1, me1, tr=IN_SHARD // 2, name="sum_adamw_w_in",
                                           wmv=(w_in[0].T, m_w_in[0].T, v_w_in[0].T))
    grads["w_in"], delta["w_in"], new_m["w_in"], new_v["w_in"] = g_win_t.T, d_t.T, m_t.T, v_t.T

    _, gathered = _direct_wait(*started["small"][:4], grad_x, gather=True, name="gather_small_grads_wait")
    loss, g_s, d_s, m_s, v_s = _small_finish(gathered, w, mom, var)
    for dst, src in ((grads, g_s), (delta, d_s), (new_m, m_s), (new_v, v_s)):
        dst.update(src)

    def shaped(d, name):
        return d[name].reshape(w[name].shape)

    return (loss.reshape(()), grad_x[None], *[shaped(grads, n) for n in _WEIGHTS], *[shaped(delta, n) for n in _WEIGHTS],
            *[shaped(new_m, n) for n in _WEIGHTS], *[shaped(new_v, n) for n in _WEIGHTS])
```

```python
import functools
import math

import jax
import jax.numpy as jnp
from jax import lax
from jax.experimental import pallas as pl
from jax.experimental.pallas import tpu as pltpu

F32 = jnp.float32
BF16 = jnp.bfloat16

D_MODEL = 1024
MEM_LEN = 256
HG_HEADS = 8
HG_DK = 128
HG_CHUNK = 64
SWA_HEADS = 16
SWA_HEAD_DIM = 64
SWA_BLOCK = 128
SWA_WINDOW = 128
MEM_HEADS = 4
MEM_HEAD_DIM = 256
NUM_BUCKETS = 32
MAX_DISTANCE = 128
D_FF = 4096
LN_EPS = 1e-5
RMS_EPS = 1e-6
ALPHA = 2.0 ** 0.25
N_DEV = 8

C_HQ, C_HF, C_HI, C_HG, C_SQ, C_SK, C_SV, C_MQ, C_GL = 0, 1024, 2048, 3072, 4096, 5120, 5248, 5376, 6400
IN_COLS = 9472
IN_SHARD = IN_COLS // N_DEV
Z_HG, Z_SQ, Z_SK, Z_MQ, Z_REST = 0, C_SQ - C_HG, C_SK - C_HG, C_MQ - C_HG, C_GL - C_HG

ADAM_LR = 0.001
ADAM_B1 = 0.9
ADAM_B2 = 0.999
ADAM_EPS = 1e-08
ADAM_WD = 0.01
ADAM_STEP = 10

VMEM_LIMIT = 58 * 1024 * 1024

R_DN, R_UP, R_BH, R_BS, R_BM, R_OUT, R_KV, R_OTHER = 0, 512, 1024, 1152, 1280, 1408, 1536, 1792

SM_LB, SM_GAIN, SM_SINK, SM_L1G, SM_L1B, SM_L2G, SM_L2B, SM_LOSS, SM_RB, SM_ROWS = 0, 2, 3, 4, 5, 6, 7, 8, 16, 48


def _bf(v):
    return v.astype(BF16)


def _f32(v):
    return v.astype(F32)


def _dot(a, b):
    return jnp.dot(a, b, preferred_element_type=F32)


def _dot_nt(a, b):
    return lax.dot_general(a, b, (((1,), (1,)), ((), ())), preferred_element_type=F32)


def _dot_tn(a, b):
    return lax.dot_general(a, b, (((0,), (0,)), ((), ())), preferred_element_type=F32)


def _sig(v):
    return 0.5 * jnp.tanh(0.5 * v) + 0.5


def _cparams(*sem):
    return pltpu.CompilerParams(dimension_semantics=sem, vmem_limit_bytes=VMEM_LIMIT)


def _const_spec(shape):
    nd = len(shape)
    return pl.BlockSpec(shape, lambda *_: (0,) * nd, pipeline_mode=pl.Buffered(1))


def _dep_spec():
    return pl.BlockSpec((8, 128), lambda *_: (0, 0))


def _in_proj(x, win_t, dep, *, tm):
    S = x.shape[0]

    def body(x_ref, w_ref, dep_ref, qfv_ref, z_ref, gl_ref, xb_ref):
        del dep_ref
        xb = _bf(x_ref[...])
        xb_ref[...] = xb
        for c0 in range(0, C_HG, 1024):
            qfv_ref[:, c0:c0 + 1024] = _dot_nt(xb, w_ref[c0:c0 + 1024, :])
        for c0 in range(0, Z_REST, Z_REST // 2):
            z_ref[:, c0:c0 + Z_REST // 2] = _bf(_dot_nt(xb, w_ref[C_HG + c0:C_HG + c0 + Z_REST // 2, :]))
        for c0 in range(0, IN_COLS - C_GL, 1024):
            gl_ref[:, c0:c0 + 1024] = _bf(_dot_nt(xb, w_ref[C_GL + c0:C_GL + c0 + 1024, :]))

    row = lambda w: pl.BlockSpec((tm, w), lambda i: (i, 0))
    return pl.pallas_call(
        body,
        grid=(S // tm,),
        in_specs=[row(D_MODEL), _const_spec(win_t.shape), _dep_spec()],
        out_specs=[row(C_HG), row(Z_REST), row(IN_COLS - C_GL), row(D_MODEL)],
        out_shape=[jax.ShapeDtypeStruct((S, C_HG), F32), jax.ShapeDtypeStruct((S, Z_REST), BF16),
                   jax.ShapeDtypeStruct((S, IN_COLS - C_GL), BF16), jax.ShapeDtypeStruct((S, D_MODEL), BF16)],
        compiler_params=_cparams("parallel"),
        name="in_proj",
    )(x, win_t, dep)


def _placement(into, tm, N, M, out_dtype):
    if into is None:
        return (lambda i: (i, 0)), (tm, N), jax.ShapeDtypeStruct((M, N), out_dtype), (), {}
    dest, block, index = into
    assert math.prod(block) == tm * N and dest.dtype == out_dtype
    return index, block, jax.ShapeDtypeStruct(dest.shape, dest.dtype), (dest,), {2: 0}


def _mm_tn_resident(a, b, *, tm, kc, name, out_dtype, into=None):
    K, M = a.shape
    N = b.shape[1]
    nk = K // kc
    index, block, out_shape, extra, aliases = _placement(into, tm, N, M, out_dtype)

    def body(a_ref, b_ref, *rest):
        o_ref = rest[-1]
        acc = jnp.zeros((tm, N), F32)
        for kk in range(nk):
            sl = pl.ds(kk * kc, kc)
            acc = acc + _dot_tn(_bf(a_ref[sl, :]), _bf(b_ref[sl, :]))
        o_ref[...] = acc.astype(o_ref.dtype).reshape(block)

    return pl.pallas_call(
        body,
        grid=(M // tm,),
        in_specs=[pl.BlockSpec((K, tm), lambda i: (0, i)), _const_spec((K, N))] + [ANY for _ in extra],
        out_specs=pl.BlockSpec(block, index),
        out_shape=out_shape,
        input_output_aliases=aliases,
        compiler_params=_cparams("parallel"),
        name=name,
    )(a, b, *extra)


def _mm_tn(a, b, *, kc, name, out_dtype=F32, into=None, tm=None):
    K, M = a.shape
    N = b.shape[1]
    if M > 1024 or tm is not None:
        return _mm_tn_resident(a, b, tm=tm or 256, kc=min(kc, 1024), name=name, out_dtype=out_dtype, into=into)
    tm = M
    nk = K // kc
    index, block, out_shape, extra, aliases = _placement(into, tm, N, M, out_dtype)

    def body(a_ref, b_ref, *rest):
        o_ref, acc = rest[-2], rest[-1]
        k = pl.program_id(1)
        part = _dot_tn(_bf(a_ref[...]), _bf(b_ref[...]))

        @pl.when(k == 0)
        def _():
            acc[...] = part

        @pl.when(k > 0)
        def _():
            acc[...] += part

        @pl.when(k == nk - 1)
        def _():
            o_ref[...] = acc[...].astype(o_ref.dtype).reshape(block)

    return pl.pallas_call(
        body,
        grid=(M // tm, nk),
        in_specs=[pl.BlockSpec((kc, tm), lambda i, k: (k, i)), pl.BlockSpec((kc, N), lambda i, k: (k, 0))]
        + [ANY for _ in extra],
        out_specs=pl.BlockSpec(block, lambda i, k: index(i)),
        out_shape=out_shape,
        input_output_aliases=aliases,
        scratch_shapes=[pltpu.VMEM((tm, N), F32)],
        compiler_params=_cparams("parallel", "arbitrary"),
        name=name,
    )(a, b, *extra)


def _grad_x(d_qfv, d_hg_gl, d_sq, d_skv, d_mq, w_qfv, win_t, add, deps, *, tm):
    M = add.shape[0]
    pieces = (d_qfv, d_hg_gl, d_sq, d_skv, d_mq)

    def body(qfv_ref, hggl_ref, sq_ref, skv_ref, mq_ref, wq_ref, w_ref, add_ref, *rest):
        o_ref = rest[-1]
        acc = add_ref[...] + _dot(qfv_ref[...], wq_ref[...])
        acc = acc + _dot(hggl_ref[:, 0:1024], w_ref[C_HG:C_SQ, :])
        acc = acc + _dot(hggl_ref[:, 1024:4096], w_ref[C_GL:IN_COLS, :])
        acc = acc + _dot(sq_ref[...], w_ref[C_SQ:C_SK, :])
        acc = acc + _dot(skv_ref[...], w_ref[C_SK:C_MQ, :])
        o_ref[...] = acc + _dot(mq_ref[...], w_ref[C_MQ:C_GL, :])

    return pl.pallas_call(
        body,
        grid=(M // tm,),
        in_specs=[pl.BlockSpec((tm, p.shape[1]), lambda i: (i, 0)) for p in pieces]
        + [_const_spec(w_qfv.shape), _const_spec(win_t.shape), pl.BlockSpec((tm, D_MODEL), lambda i: (i, 0))]
        + [_dep_spec() for _ in deps],
        out_specs=pl.BlockSpec((tm, D_MODEL), lambda i: (i, 0)),
        out_shape=jax.ShapeDtypeStruct((M, D_MODEL), F32),
        compiler_params=_cparams("parallel"),
        name="grad_x",
    )(*pieces, w_qfv, win_t, add, *deps)


def _lower_bound(lbl_ref):
    l0 = lbl_ref[0:1, :]
    l1 = lbl_ref[1:2, :]
    mx = jnp.maximum(l0, l1)
    e0 = jnp.exp(l0 - mx)
    e1 = jnp.exp(l1 - mx)
    return e0 / (e0 + e1)


def _tri(lower):
    r = lax.broadcasted_iota(jnp.int32, (HG_CHUNK, HG_CHUNK), 0)
    c = lax.broadcasted_iota(jnp.int32, (HG_CHUNK, HG_CHUNK), 1)
    return (r >= c) if lower else (r <= c)


def _hg_gates(fl, lb):
    sg = _sig(fl)
    f = lb + (1.0 - lb) * sg
    return sg, f, jnp.log(f), 1.0 - f


def _scan_rows(v, reverse=False):
    row = lax.broadcasted_iota(jnp.int32, v.shape, 0)
    s = 1
    while s < HG_CHUNK:
        if reverse:
            v = v + jnp.where(row < HG_CHUNK - s, pltpu.roll(v, HG_CHUNK - s, 0), 0.0)
        else:
            v = v + jnp.where(row >= s, pltpu.roll(v, s, 0), 0.0)
        s *= 2
    return v


def _hgrn_fwd(zmain, lb_logits, *, T):
    S = zmain.shape[0]
    nc = T // HG_CHUNK

    def body(q_ref, f_ref, v_ref, lbl_ref, o_ref, st_ref, state):
        @pl.when(pl.program_id(1) == 0)
        def _():
            state[...] = jnp.zeros_like(state)

        lb = _lower_bound(lbl_ref)
        tril = _tri(True)
        qis, updates, decays, intra = [], [], [], []
        for c in range(nc):
            sl = pl.ds(c * HG_CHUNK, HG_CHUNK)
            _, _, g, k = _hg_gates(_f32(f_ref[sl, :]), lb)
            b = _scan_rows(g)
            bl = jnp.sum(g, axis=0, keepdims=True)
            qi = _bf(_f32(q_ref[sl, :]) * jnp.exp(b))
            ki = _bf(k * jnp.exp(-b))
            ko = _bf(k * jnp.exp(bl - b))
            vb = _bf(v_ref[sl, :])
            att = jnp.where(tril, _dot_nt(qi, ki), 0.0)
            intra.append(_dot(_bf(att), vb))
            qis.append(qi)
            updates.append(_dot_tn(vb, ko))
            decays.append(jnp.exp(bl))
        st = state[...]
        for c in range(nc):
            st_ref[0, c] = st
            o_ref[pl.ds(c * HG_CHUNK, HG_CHUNK), :] = intra[c] + _dot_nt(qis[c], _bf(st))
            st = st * decays[c] + updates[c]
        state[...] = st

    col = lambda base: pl.BlockSpec((T, HG_DK), lambda h, t: (t, base + h))
    return pl.pallas_call(
        body,
        grid=(HG_HEADS, S // T),
        in_specs=[col(0), col(8), col(16), pl.BlockSpec((2, HG_DK), lambda h, t: (0, h))],
        out_specs=[
            pl.BlockSpec((T, HG_DK), lambda h, t: (t, h)),
            pl.BlockSpec((1, nc, HG_DK, HG_DK), lambda h, t: (h, t, 0, 0)),
        ],
        out_shape=[
            jax.ShapeDtypeStruct((S, D_MODEL), F32),
            jax.ShapeDtypeStruct((HG_HEADS, S // HG_CHUNK, HG_DK, HG_DK), F32),
        ],
        scratch_shapes=[pltpu.VMEM((HG_DK, HG_DK), F32)],
        compiler_params=_cparams("parallel", "arbitrary"),
        name="hgrn_fwd",
    )(zmain, zmain, zmain, lb_logits)


def _hgrn_bwd(zmain, lb_logits, states, d_o, *, T):
    S = zmain.shape[0]
    nc = T // HG_CHUNK
    nt = S // T

    def body(q_ref, f_ref, v_ref, lbl_ref, st_ref, do_ref, dz_ref, dlb_ref, dstate):
        @pl.when(pl.program_id(1) == 0)
        def _():
            dstate[...] = jnp.zeros_like(dstate)
            dlb_ref[...] = jnp.zeros_like(dlb_ref)

        lb = _lower_bound(lbl_ref)
        tril = _tri(True)
        last_row = lax.broadcasted_iota(jnp.int32, (HG_CHUNK, HG_DK), 0) == HG_CHUNK - 1
        saved = []
        for c in range(nc):
            sl = pl.ds(c * HG_CHUNK, HG_CHUNK)
            sg, f, g, k = _hg_gates(_f32(f_ref[sl, :]), lb)
            b = _scan_rows(g)
            bl = jnp.sum(g, axis=0, keepdims=True)
            eb = jnp.exp(b)
            enb = jnp.exp(-b)
            eo = jnp.exp(bl - b)
            q_in = _f32(q_ref[sl, :]) * eb
            k_in = k * enb
            k_out = k * eo
            qi, ki, ko = _bf(q_in), _bf(k_in), _bf(k_out)
            vb = _bf(v_ref[sl, :])
            dob = do_ref[sl, :]
            att = jnp.where(tril, _dot_nt(qi, ki), 0.0)
            d_att = _bf(jnp.where(tril, _dot_nt(dob, vb), 0.0))
            d_kin = _dot_tn(d_att, qi)
            saved.append(dict(
                sg=sg, f=f, eb=eb, enb=enb, eo=eo, ebl=jnp.exp(bl), k_out=k_out, ko=ko, vb=vb, dob=dob,
                d_v=_dot_tn(_bf(att), dob), d_qin=_dot(d_att, ki), d_kin=d_kin,
                qk=(q_in, k_in), d_state=_dot_tn(dob, qi)))
        dst = dstate[...]
        dsts = [None] * nc
        for c in reversed(range(nc)):
            dsts[c] = dst
            dst = dst * saved[c]["ebl"] + saved[c]["d_state"]
        dstate[...] = dst
        dlb = jnp.zeros((1, HG_DK), F32)
        for c in range(nc):
            sl = pl.ds(c * HG_CHUNK, HG_CHUNK)
            s = saved[c]
            q_in, k_in = s["qk"]
            st = st_ref[0, c]
            dstb = _bf(dsts[c])
            d_v = s["d_v"] + _dot_nt(s["ko"], dstb)
            d_qin = s["d_qin"] + _dot(s["dob"], _bf(st))
            d_kout = _dot(s["vb"], dstb)
            d_decay = jnp.sum(dsts[c] * st, axis=0, keepdims=True)
            kk = d_kout * s["k_out"]
            d_b = d_qin * q_in - s["d_kin"] * k_in - kk
            d_bl = jnp.sum(kk, axis=0, keepdims=True) + d_decay * s["ebl"]
            d_g = _scan_rows(d_b + jnp.where(last_row, d_bl, 0.0), reverse=True)
            d_f = d_g / s["f"] - (s["d_kin"] * s["enb"] + d_kout * s["eo"])
            dz_ref[sl, 0:HG_DK] = _bf(d_qin * s["eb"])
            dz_ref[sl, HG_DK:2 * HG_DK] = _bf(d_f * (1.0 - lb) * s["sg"] * (1.0 - s["sg"]))
            dz_ref[sl, 2 * HG_DK:3 * HG_DK] = _bf(d_v)
            dlb = dlb + jnp.sum(d_f * (1.0 - s["sg"]), axis=0, keepdims=True)
        dlb_ref[...] += dlb

    rev = lambda base: pl.BlockSpec((T, HG_DK), lambda h, t: (nt - 1 - t, base + h))
    outc = pl.BlockSpec((T, HG_DK), lambda h, t: (nt - 1 - t, h))
    return pl.pallas_call(
        body,
        grid=(HG_HEADS, nt),
        in_specs=[
            rev(0), rev(8), rev(16),
            pl.BlockSpec((2, HG_DK), lambda h, t: (0, h)),
            pl.BlockSpec((1, nc, HG_DK, HG_DK), lambda h, t: (h, nt - 1 - t, 0, 0)),
            outc,
        ],
        out_specs=[pl.BlockSpec((T, 3 * HG_DK), lambda h, t: (nt - 1 - t, h)),
                   pl.BlockSpec((1, HG_DK), lambda h, t: (0, h))],
        out_shape=[jax.ShapeDtypeStruct((S, 3 * D_MODEL), BF16), jax.ShapeDtypeStruct((1, D_MODEL), F32)],
        scratch_shapes=[pltpu.VMEM((HG_DK, HG_DK), F32)],
        compiler_params=_cparams("parallel", "arbitrary"),
        name="hgrn_bwd",
    )(zmain, zmain, zmain, lb_logits, states, d_o)


def _t5_bucket_table():
    qi = jnp.arange(SWA_BLOCK)[:, None] + SWA_BLOCK
    kj = jnp.arange(2 * SWA_BLOCK)[None, :]
    n = jnp.clip(qi - kj, 0, SWA_WINDOW - 1)
    max_exact = NUM_BUCKETS // 2
    nf = jnp.maximum(n, 1).astype(F32)
    large = max_exact + (jnp.log(nf / max_exact) / math.log(MAX_DISTANCE / max_exact)
                         * (NUM_BUCKETS - max_exact)).astype(jnp.int32)
    large = jnp.minimum(large, NUM_BUCKETS - 1)
    return jnp.where(n < max_exact, n, large).astype(jnp.int32)


SWA_ROWS = 32
MERGE_GROUPS = 2


def _swa_bias_init(bias, bucket_ref, rb_ref):
    bk = bucket_ref[...]
    qi = lax.broadcasted_iota(jnp.int32, bk.shape, 0) + SWA_BLOCK
    kj = lax.broadcasted_iota(jnp.int32, bk.shape, 1)
    band = (qi - kj >= 0) & (qi - kj < SWA_WINDOW)
    for h in range(SWA_HEADS):
        def sel(b, acc, h=h):
            return jnp.where(bk == b, rb_ref[b, h], acc)
        t = lax.fori_loop(0, NUM_BUCKETS, sel, jnp.zeros(bk.shape, F32))
        bias[1, h] = jnp.where(band, t, -jnp.inf)
        bias[0, h] = jnp.where(band & (kj >= SWA_BLOCK), t, -jnp.inf)


def _lane_halves(t, kv_head):
    lane = lax.broadcasted_iota(jnp.int32, t.shape, 1)
    rolled = pltpu.roll(t, 64, 1)
    zero = jnp.zeros_like(t)
    if kv_head == 0:
        return jnp.where(lane < 64, t, zero), jnp.where(lane >= 64, rolled, zero)
    return jnp.where(lane < 64, rolled, zero), jnp.where(lane >= 64, t, zero)


def _swa_zero_key0(t):
    return jnp.where(lax.broadcasted_iota(jnp.int32, t.shape, 0) == 0, jnp.zeros_like(t), t)


def _swa_probs(s, masked_bias, sink):
    s = s + masked_bias
    m = jnp.maximum(jnp.max(s, axis=-1, keepdims=True), sink)
    p = jnp.exp(s - m)
    es = jnp.exp(sink - m)
    inv = 1.0 / (jnp.sum(p, axis=-1, keepdims=True) + es)
    return p * inv, es * inv


def _swa_fwd(zmain, bucket, rel_bias, sinks):
    S = zmain.shape[0]
    nb = S // SWA_BLOCK
    scale = SWA_HEAD_DIM ** -0.5

    def body(q_ref, kvc_ref, kvp_ref, bucket_ref, rb_ref, sk_ref, o_ref, p_ref, bias):
        n = pl.program_id(0)

        @pl.when(n == 0)
        def _():
            _swa_bias_init(bias, bucket_ref, rb_ref)

        later = jnp.minimum(n, 1)
        kk = _bf(jnp.concatenate([kvp_ref[:, 0:128], kvc_ref[:, 0:128]], axis=0))
        vv = _swa_zero_key0(_bf(jnp.concatenate([kvp_ref[:, 128:256], kvc_ref[:, 128:256]], axis=0)))
        first_col = lax.broadcasted_iota(jnp.int32, (SWA_ROWS, 2 * SWA_BLOCK), 1) == 0
        scores, values = {}, {}
        for kvh in range(2):
            qst = _bf(jnp.concatenate([q_ref[:, pl.ds((kvh * 4 + jj) * 128, 128)] for jj in range(4)], axis=0) * scale)
            values[kvh] = _lane_halves(vv, kvh)
            for odd, kx in enumerate(_lane_halves(kk, kvh)):
                scores[kvh, odd] = _dot_nt(qst, kx)
        probs = {}
        for (kvh, odd), s in scores.items():
            parts = []
            for jj in range(4):
                h = 2 * (kvh * 4 + jj) + odd
                for r0 in range(0, SWA_BLOCK, SWA_ROWS):
                    p, ps = _swa_probs(s[jj * SWA_BLOCK + r0:jj * SWA_BLOCK + r0 + SWA_ROWS],
                                       bias[later, h, pl.ds(r0, SWA_ROWS), :], sk_ref[0, h])
                    part = _bf(jnp.where(first_col, ps, p))
                    p_ref[pl.ds(r0, SWA_ROWS), pl.ds(h * 2 * SWA_BLOCK, 2 * SWA_BLOCK)] = part
                    parts.append(part)
            probs[kvh, odd] = jnp.concatenate(parts, axis=0)
        for kvh in range(2):
            ost = _dot(probs[kvh, 0], values[kvh][0]) + _dot(probs[kvh, 1], values[kvh][1])
            for jj in range(4):
                o_ref[:, pl.ds((kvh * 4 + jj) * 128, 128)] = ost[jj * SWA_BLOCK:(jj + 1) * SWA_BLOCK]

    smem = pl.BlockSpec(memory_space=pltpu.SMEM)
    return pl.pallas_call(
        body,
        grid=(nb,),
        in_specs=[
            pl.BlockSpec((SWA_BLOCK, 1024), lambda n: (n, Z_SQ // 1024)),
            pl.BlockSpec((SWA_BLOCK, 256), lambda n: (n, Z_SK // 256)),
            pl.BlockSpec((SWA_BLOCK, 256), lambda n: (jnp.maximum(n - 1, 0), Z_SK // 256)),
            _const_spec((SWA_BLOCK, 2 * SWA_BLOCK)), smem, smem,
        ],
        out_specs=[pl.BlockSpec((SWA_BLOCK, 1024), lambda n: (n, 0)),
                   pl.BlockSpec((SWA_BLOCK, SWA_HEADS * 2 * SWA_BLOCK), lambda n: (n, 0))],
        out_shape=[jax.ShapeDtypeStruct((S, 1024), F32),
                   jax.ShapeDtypeStruct((S, SWA_HEADS * 2 * SWA_BLOCK), BF16)],
        scratch_shapes=[pltpu.VMEM((2, SWA_HEADS, SWA_BLOCK, 2 * SWA_BLOCK), F32)],
        compiler_params=_cparams("arbitrary"),
        name="swa_fwd",
    )(zmain, zmain, zmain, bucket, rel_bias, sinks)


def _swa_bwd(zmain, o_b, probs, d_o, bucket, dep):
    S = zmain.shape[0]
    nb = S // SWA_BLOCK
    scale = SWA_HEAD_DIM ** -0.5

    def body(q_ref, kvc_ref, kvp_ref, o_ref, p_ref, do_ref, bucket_ref, dep_ref,
             dq_ref, dkv_ref, drb_ref, dsk_ref, dbias, carry):
        del dep_ref
        n = pl.program_id(0)

        @pl.when(n == 0)
        def _():
            dbias[...] = jnp.zeros_like(dbias)
            carry[...] = jnp.zeros_like(carry)

        @pl.when(n < nb)
        def _():
            kk = _swa_zero_key0(_bf(jnp.concatenate([kvp_ref[:, 0:128], kvc_ref[:, 0:128]], axis=0)))
            vv = _swa_zero_key0(_bf(jnp.concatenate([kvp_ref[:, 128:256], kvc_ref[:, 128:256]], axis=0)))
            lane = lax.broadcasted_iota(jnp.int32, (2 * SWA_BLOCK, 128), 1)
            lane_q = lax.broadcasted_iota(jnp.int32, (4 * SWA_BLOCK, 128), 1)
            pair_cols = {kvh: [pl.ds((kvh * 4 + jj) * 128, 128) for jj in range(4)] for kvh in range(2)}
            qst, dost, ks, d_p, delta = {}, {}, {}, {}, {}
            for kvh in range(2):
                qst[kvh] = _bf(jnp.concatenate([q_ref[:, cl] for cl in pair_cols[kvh]], axis=0) * scale)
                dost[kvh] = jnp.concatenate([do_ref[:, cl] for cl in pair_cols[kvh]], axis=0)
                prod = dost[kvh].astype(F32) * jnp.concatenate([o_ref[:, cl] for cl in pair_cols[kvh]], axis=0)
                ks[kvh] = _lane_halves(kk, kvh)
                for odd, vx in enumerate(_lane_halves(vv, kvh)):
                    keep = (lane_q >= 64) if odd else (lane_q < 64)
                    delta[kvh, odd] = jnp.sum(jnp.where(keep, prod, 0.0), axis=-1, keepdims=True)
                    d_p[kvh, odd] = _dot_nt(dost[kvh], vx)
            pst, dsst = {}, {}
            for (kvh, odd), dp in d_p.items():
                p_parts, ds_parts = [], []
                for jj in range(4):
                    h = 2 * (kvh * 4 + jj) + odd
                    rows = slice(jj * SWA_BLOCK, (jj + 1) * SWA_BLOCK)
                    p = p_ref[:, pl.ds(h * 2 * SWA_BLOCK, 2 * SWA_BLOCK)]
                    ds = _f32(p) * (dp[rows] - delta[kvh, odd][rows])
                    dbias[h] += ds
                    p_parts.append(p)
                    ds_parts.append(_bf(ds))
                pst[kvh, odd] = jnp.concatenate(p_parts, axis=0)
                dsst[kvh, odd] = jnp.concatenate(ds_parts, axis=0)
            dk_parts, dv_parts = [], []
            for kvh in range(2):
                dq_st = _dot(dsst[kvh, 0], ks[kvh][0]) + _dot(dsst[kvh, 1], ks[kvh][1])
                for jj in range(4):
                    dq_ref[:, pair_cols[kvh][jj]] = _bf(dq_st[jj * SWA_BLOCK:(jj + 1) * SWA_BLOCK] * scale)
                zk = jnp.where(lane < 64, _dot_tn(dsst[kvh, 0], qst[kvh]), _dot_tn(dsst[kvh, 1], qst[kvh]))
                zv = jnp.where(lane < 64, _dot_tn(pst[kvh, 0], dost[kvh]), _dot_tn(pst[kvh, 1], dost[kvh]))
                dk_parts.append(zk + pltpu.roll(zk, 64, 1))
                dv_parts.append(zv + pltpu.roll(zv, 64, 1))
            dk = jnp.where(lane < 64, dk_parts[0], dk_parts[1])
            dv = jnp.where(lane < 64, dv_parts[0], dv_parts[1])
            dkv = _swa_zero_key0(jnp.concatenate([dk, dv], axis=1))
            dkv_ref[...] = _bf(carry[...] + dkv[0:SWA_BLOCK])
            carry[...] = dkv[SWA_BLOCK:]

        @pl.when(n == nb)
        def _():
            dkv_ref[...] = _bf(carry[...])
            first_col = lax.broadcasted_iota(jnp.int32, (SWA_BLOCK, 2 * SWA_BLOCK), 1) == 0
            bk = jnp.where(first_col, -1, bucket_ref[...])

            row = lax.broadcasted_iota(jnp.int32, (NUM_BUCKETS, 128), 0)
            lane = lax.broadcasted_iota(jnp.int32, (NUM_BUCKETS, 128), 1)

            def total(v):
                return jnp.sum(jnp.sum(v, axis=1, keepdims=True), axis=0, keepdims=True)

            def per_head(h, acc):
                db = dbias[h]
                d_rb, d_sk = acc
                d_sk = d_sk + jnp.where((row == 0) & (lane == h), total(jnp.where(first_col, db, 0.0)), 0.0)

                def per_bucket(b, d_rb):
                    return d_rb + jnp.where((row == b) & (lane == h), total(jnp.where(bk == b, db, 0.0)), 0.0)

                return lax.fori_loop(0, NUM_BUCKETS, per_bucket, d_rb), d_sk

            zero = jnp.zeros((NUM_BUCKETS, 128), F32)
            d_rb, d_sk = lax.fori_loop(0, SWA_HEADS, per_head, (zero, zero))
            drb_ref[...] = d_rb
            dsk_ref[...] = d_sk[0:8]

    cur = lambda n: jnp.minimum(n, nb - 1)
    prev = lambda n: jnp.maximum(jnp.minimum(n, nb - 1) - 1, 0)
    return pl.pallas_call(
        body,
        grid=(nb + 1,),
        in_specs=[
            pl.BlockSpec((SWA_BLOCK, 1024), lambda n: (cur(n), Z_SQ // 1024)),
            pl.BlockSpec((SWA_BLOCK, 256), lambda n: (cur(n), Z_SK // 256)),
            pl.BlockSpec((SWA_BLOCK, 256), lambda n: (prev(n), Z_SK // 256)),
            pl.BlockSpec((SWA_BLOCK, 1024), lambda n: (cur(n), 0)),
            pl.BlockSpec((SWA_BLOCK, SWA_HEADS * 2 * SWA_BLOCK), lambda n: (cur(n), 0)),
            pl.BlockSpec((SWA_BLOCK, 1024), lambda n: (cur(n), 0)),
            _const_spec((SWA_BLOCK, 2 * SWA_BLOCK)), _dep_spec(),
        ],
        out_specs=[
            pl.BlockSpec((SWA_BLOCK, 1024), lambda n: (cur(n), 0)),
            pl.BlockSpec((SWA_BLOCK, 256), lambda n: (jnp.maximum(n - 1, 0), 0)),
            pl.BlockSpec((NUM_BUCKETS, 128), lambda n: (0, 0)),
            pl.BlockSpec((8, 128), lambda n: (0, 0)),
        ],
        out_shape=[
            jax.ShapeDtypeStruct((S, 1024), BF16),
            jax.ShapeDtypeStruct((S, 256), BF16),
            jax.ShapeDtypeStruct((NUM_BUCKETS, 128), F32),
            jax.ShapeDtypeStruct((8, 128), F32),
        ],
        scratch_shapes=[
            pltpu.VMEM((SWA_HEADS, SWA_BLOCK, 2 * SWA_BLOCK), F32),
            pltpu.VMEM((SWA_BLOCK, 256), F32),
        ],
        compiler_params=_cparams("arbitrary"),
        name="swa_bwd",
    )(zmain, zmain, zmain, o_b, probs, d_o, bucket, dep)


def _mem_q_specs(T):
    return [pl.BlockSpec((T, MEM_HEAD_DIM), lambda t, h=h: (t, Z_MQ // MEM_HEAD_DIM + h)) for h in range(MEM_HEADS)]


def _mem_kv_proj(mem, g2):
    def body(mem_ref, w_ref, o_ref):
        o_ref[...] = _dot_nt(_bf(mem_ref[...]), _rows(w_ref))

    return pl.pallas_call(
        body,
        grid=(1,),
        in_specs=[pl.BlockSpec((MEM_LEN, D_MODEL), lambda i: (0, 0)), _gathered_spec(R_KV, R_OTHER)],
        out_specs=pl.BlockSpec((MEM_LEN, 2048), lambda i: (0, 0)),
        out_shape=jax.ShapeDtypeStruct((MEM_LEN, 2048), F32),
        compiler_params=_cparams("arbitrary"),
        name="mem_kv_proj",
    )(mem, g2)


def _mem_fwd(zmain, mkv, *, T):
    S = zmain.shape[0]

    def body(q0, q1, q2, q3, kv_ref, o_ref, p_ref):
        heads = [pl.ds(h * MEM_HEAD_DIM, MEM_HEAD_DIM) for h in range(MEM_HEADS)]
        scores = [_dot_nt(_bf(q_ref[...] * (MEM_HEAD_DIM ** -0.5)), _bf(kv_ref[:, cols]))
                  for q_ref, cols in zip((q0, q1, q2, q3), heads)]
        probs = []
        for s, cols in zip(scores, heads):
            e = jnp.exp(s - jnp.max(s, axis=-1, keepdims=True))
            pb = _bf(e * (1.0 / jnp.sum(e, axis=-1, keepdims=True)))
            p_ref[:, cols] = pb
            probs.append(pb)
        for h, (pb, cols) in enumerate(zip(probs, heads)):
            o_ref[:, cols] = _dot(pb, _bf(kv_ref[:, pl.ds(1024 + h * MEM_HEAD_DIM, MEM_HEAD_DIM)]))

    row = pl.BlockSpec((T, 1024), lambda t: (t, 0))
    return pl.pallas_call(
        body,
        grid=(S // T,),
        in_specs=_mem_q_specs(T) + [_const_spec((MEM_LEN, 2048))],
        out_specs=[row, row],
        out_shape=[jax.ShapeDtypeStruct((S, 1024), F32), jax.ShapeDtypeStruct((S, 1024), BF16)],
        compiler_params=_cparams("parallel"),
        name="mem_fwd",
    )(zmain, zmain, zmain, zmain, mkv)


def _mem_bwd(zmain, mkv, o_c, probs, d_o, *, T):
    S = zmain.shape[0]
    scale = MEM_HEAD_DIM ** -0.5

    def body(q0, q1, q2, q3, kv_ref, o_ref, p_ref, do_ref, dq_ref, dkv_ref):
        @pl.when(pl.program_id(0) == 0)
        def _():
            dkv_ref[...] = jnp.zeros_like(dkv_ref)

        heads = [(pl.ds(h * MEM_HEAD_DIM, MEM_HEAD_DIM), pl.ds(1024 + h * MEM_HEAD_DIM, MEM_HEAD_DIM))
                 for h in range(MEM_HEADS)]
        d_p = [_dot_nt(do_ref[:, cols], _bf(kv_ref[:, vcols])) for cols, vcols in heads]
        d_s = []
        for dp, (cols, _) in zip(d_p, heads):
            delta = jnp.sum(do_ref[:, cols].astype(F32) * o_ref[:, cols], axis=-1, keepdims=True)
            d_s.append(_bf(_f32(p_ref[:, cols]) * (dp - delta)))
        for ds, q_ref, (cols, vcols) in zip(d_s, (q0, q1, q2, q3), heads):
            dq_ref[:, cols] = _bf(_dot(ds, _bf(kv_ref[:, cols])) * scale)
            dkv_ref[:, cols] += _dot_tn(ds, _bf(q_ref[...] * scale))
            dkv_ref[:, vcols] += _dot_tn(p_ref[:, cols], do_ref[:, cols])

    row = pl.BlockSpec((T, 1024), lambda t: (t, 0))
    return pl.pallas_call(
        body,
        grid=(S // T,),
        in_specs=_mem_q_specs(T) + [_const_spec((MEM_LEN, 2048)), row, row, row],
        out_specs=[row, pl.BlockSpec((MEM_LEN, 2048), lambda t: (0, 0))],
        out_shape=[jax.ShapeDtypeStruct((S, 1024), BF16), jax.ShapeDtypeStruct((MEM_LEN, 2048), F32)],
        compiler_params=_cparams("arbitrary"),
        name="mem_bwd",
    )(zmain, zmain, zmain, zmain, mkv, o_c, probs, d_o)


def _layer_norm(u):
    mu = jnp.mean(u, axis=-1, keepdims=True)
    xc = u - mu
    rstd = lax.rsqrt(jnp.mean(xc * xc, axis=-1, keepdims=True) + LN_EPS)
    return xc * rstd, rstd


def _layer_norm_bwd(dy, gamma, xhat, rstd):
    dxh = dy * gamma
    return rstd * (dxh - jnp.mean(dxh, axis=-1, keepdims=True) - xhat * jnp.mean(dxh * xhat, axis=-1, keepdims=True))


def _merge_stages(rows, oraw_ref, hg_ref, ob_ref, oc_ref, gl_ref, x_ref, gain_ref, wbh, wbs, wbm, wout, g_ref, b_ref,
                  fwd_out=None, bwd=None, saved=None):
    ys, rs = [], []
    for h in range(HG_HEADS):
        oh = oraw_ref[rows, pl.ds(h * HG_DK, HG_DK)]
        r = lax.rsqrt(jnp.mean(oh * oh, axis=-1, keepdims=True) + RMS_EPS)
        ys.append(oh * r)
        rs.append(r)
    y = jnp.concatenate(ys, axis=1)
    hg = _f32(hg_ref[rows, :])
    sg = _sig(hg)
    silu = hg * sg
    gain = gain_ref[...]
    gates = [_sig(_f32(gl_ref[rows, pl.ds(i * 1024, 1024)])) for i in range(3)]
    if saved is None:
        oa = _bf(y * gain * silu)
        pa = _dot(oa, _rows(wbh))
        pb = _dot(_bf(ob_ref[rows, :]), _rows(wbs))
        pc = _dot(_bf(oc_ref[rows, :]), _rows(wbm))
        yield
        m = _bf(gates[0] * pa + gates[1] * pb + gates[2] * pc)
    else:
        pa, pb, pc = (_f32(r[rows, :]) for r in saved[:3])
        m = saved[3][rows, :]
    mix = _dot(m, _rows(wout))
    yield
    xhat, rstd = _layer_norm(ALPHA * x_ref[rows, :] + mix)
    if bwd is None:
        h1 = xhat * g_ref[...] + b_ref[...]
        fwd_out[0][rows, :] = h1
        fwd_out[1][rows, :] = _bf(h1)
        for ref, val in zip(fwd_out[2:], (_bf(pa), _bf(pb), _bf(pc), m, oa)):
            ref[rows, :] = val
        return
    (dh1_ref, dx_ref, du1_ref, dpa_ref, dpb_ref, dpc_ref, doraw_ref, dob_ref, doc_ref, dz_ref,
     dgain_ref, dg_ref, db_ref) = bwd
    dh1 = dh1_ref[rows, :]
    dg_ref[...] += jnp.sum(dh1 * xhat, axis=0, keepdims=True)
    db_ref[...] += jnp.sum(dh1, axis=0, keepdims=True)
    du1 = _layer_norm_bwd(dh1, g_ref[...], xhat, rstd)
    dx_ref[rows, :] = ALPHA * du1
    du1b = _bf(du1)
    du1_ref[rows, :] = du1b
    dm = _dot_nt(du1b, _rows(wout))
    yield
    d_branches = []
    for i, (g, p, dp_ref, w_r) in enumerate(zip(gates, (pa, pb, pc), (dpa_ref, dpb_ref, dpc_ref), (wbh, wbs, wbm))):
        dz_ref[rows, pl.ds((i + 1) * 1024, 1024)] = _bf(dm * p * g * (1.0 - g))
        dp = _bf(dm * g)
        dp_ref[rows, :] = dp
        d_branches.append(_dot_nt(dp, _rows(w_r)))
    yield
    doa, d_ob, d_oc = d_branches
    dob_ref[rows, :] = _bf(d_ob)
    doc_ref[rows, :] = _bf(d_oc)
    t = doa * y
    dgain_ref[...] += jnp.sum(t * silu, axis=0, keepdims=True)
    dz_ref[rows, 0:1024] = _bf(t * gain * sg * (1.0 + hg * (1.0 - sg)))
    dy = doa * gain * silu
    for h in range(HG_HEADS):
        cols = slice(h * HG_DK, (h + 1) * HG_DK)
        yh = y[:, cols]
        dyh = dy[:, cols]
        doraw_ref[rows, pl.ds(h * HG_DK, HG_DK)] = _bf(rs[h] * (dyh - yh * jnp.mean(dyh * yh, axis=-1, keepdims=True)))


def _interleave(chains):
    live = list(chains)
    while live:
        still = []
        for c in live:
            try:
                next(c)
                still.append(c)
            except StopIteration:
                pass
        live = still


def _gathered_spec(lo, hi):
    n = hi - lo
    return pl.BlockSpec((N_DEV, n, D_MODEL), lambda *_: (0, lo // n, 0), pipeline_mode=pl.Buffered(1))


def _rows(w_ref):
    return w_ref[...].reshape(-1, D_MODEL)


def _merge_in_specs(T):
    row = lambda w, c=0: pl.BlockSpec((T, w), lambda i: (i, c))
    vec = pl.BlockSpec((1, D_MODEL), lambda i: (0, 0))
    w = [_gathered_spec(lo, hi) for lo, hi in ((R_BH, R_BS), (R_BS, R_BM), (R_BM, R_OUT), (R_OUT, R_KV))]
    return [row(1024), row(1024, Z_HG // 1024), row(1024), row(1024), row(3072), row(1024), vec, *w, vec, vec]


def _merge_fwd(o_raw, zmain, o_b, o_c, gl, x, gain, wbh, wbs, wbm, wout, ln_g, ln_b, *, T):
    S = x.shape[0]

    def body(*refs):
        ins, outs = refs[:13], refs[13:]
        _interleave(_merge_stages(pl.ds(r0, T // MERGE_GROUPS), *ins, fwd_out=outs)
                    for r0 in range(0, T, T // MERGE_GROUPS))

    row = pl.BlockSpec((T, D_MODEL), lambda i: (i, 0))
    return pl.pallas_call(
        body,
        grid=(S // T,),
        in_specs=_merge_in_specs(T),
        out_specs=[row] * 7,
        out_shape=[jax.ShapeDtypeStruct((S, D_MODEL), F32)] + [jax.ShapeDtypeStruct((S, D_MODEL), BF16)] * 6,
        compiler_params=_cparams("parallel"),
        name="merge_fwd",
    )(o_raw, zmain, o_b, o_c, gl, x, gain, wbh, wbs, wbm, wout, ln_g, ln_b)


def _merge_bwd(d_h1, pa, pb, pc, m, o_raw, zmain, gl, x, gain, wbh, wbs, wbm, wout, ln_g, *, T):
    S = x.shape[0]

    def body(dh1_ref, pa_ref, pb_ref, pc_ref, m_ref, oraw_ref, hg_ref, gl_ref, x_ref, gain_ref, wbh_r, wbs_r, wbm_r, wout_r,
             g_ref, dx_ref, du1_ref, dpa_ref, dpb_ref, dpc_ref, doraw_ref, dob_ref, doc_ref, dz_ref,
             dgain_ref, dg_ref, db_ref):
        @pl.when(pl.program_id(0) == 0)
        def _():
            dgain_ref[...] = jnp.zeros_like(dgain_ref)
            dg_ref[...] = jnp.zeros_like(dg_ref)
            db_ref[...] = jnp.zeros_like(db_ref)

        ins = (oraw_ref, hg_ref, None, None, gl_ref, x_ref, gain_ref, wbh_r, wbs_r, wbm_r, wout_r, g_ref, None)
        bwd = (dh1_ref, dx_ref, du1_ref, dpa_ref, dpb_ref, dpc_ref, doraw_ref, dob_ref, doc_ref, dz_ref,
               dgain_ref, dg_ref, db_ref)
        _interleave([_merge_stages(pl.ds(0, T), *ins, bwd=bwd, saved=(pa_ref, pb_ref, pc_ref, m_ref))])

    row = lambda w, c=0: pl.BlockSpec((T, w), lambda i: (i, c))
    vec = pl.BlockSpec((1, D_MODEL), lambda i: (0, 0))
    w = [_gathered_spec(lo, hi) for lo, hi in ((R_BH, R_BS), (R_BS, R_BM), (R_BM, R_OUT), (R_OUT, R_KV))]
    bshape = jax.ShapeDtypeStruct((S, D_MODEL), BF16)
    vshape = jax.ShapeDtypeStruct((1, D_MODEL), F32)
    return pl.pallas_call(
        body,
        grid=(S // T,),
        in_specs=[row(1024)] * 6 + [row(1024, Z_HG // 1024), row(3072), row(1024), vec, *w, vec],
        out_specs=[row(1024)] * 8 + [row(4096), vec, vec, vec],
        out_shape=[jax.ShapeDtypeStruct((S, D_MODEL), F32)] + [bshape] * 7
        + [jax.ShapeDtypeStruct((S, 4096), BF16), vshape, vshape, vshape],
        compiler_params=_cparams("arbitrary"),
        name="merge_bwd",
    )(d_h1, pa, pb, pc, m, o_raw, zmain, gl, x, gain, wbh, wbs, wbm, wout, ln_g)


def _mlp_fwd_bwd(h1, target, wup_t, wdn, ln_g, ln_b, *, T, FC):
    S = h1.shape[0]
    nf = D_FF // FC
    assert FC == R_BH - R_UP == R_UP - R_DN

    def body(h1_ref, t_ref, wup_ref, wdn_ref, g_ref, b_ref, dh1_ref, a_ref, dup_ref, du2_ref, loss_ref, dg_ref, db_ref, up_scr):
        @pl.when(pl.program_id(0) == 0)
        def _():
            loss_ref[...] = jnp.zeros_like(loss_ref)
            dg_ref[...] = jnp.zeros_like(dg_ref)
            db_ref[...] = jnp.zeros_like(db_ref)

        h1v = h1_ref[...]
        h1b = _bf(h1v)
        ff = jnp.zeros((T, D_MODEL), F32)
        for j in range(nf):
            rows = pl.ds(j * FC, FC)
            up = jnp.maximum(_dot_nt(h1b, wup_ref[j]), 0.0)
            up_scr[:, rows] = _bf(up)
            a = _bf(up * up)
            a_ref[:, rows] = a
            ff = ff + _dot(a, wdn_ref[j])
        xhat, rstd = _layer_norm(ALPHA * h1v + ff)
        gamma = g_ref[...]
        err = xhat * gamma + b_ref[...] - t_ref[...]
        loss_ref[...] += jnp.sum(jnp.sum(err * err, axis=-1, keepdims=True), axis=0, keepdims=True) * (0.5 / D_MODEL)
        dy = err * (1.0 / D_MODEL)
        dg_ref[...] += jnp.sum(dy * xhat, axis=0, keepdims=True)
        db_ref[...] += jnp.sum(dy, axis=0, keepdims=True)
        du2 = _layer_norm_bwd(dy, gamma, xhat, rstd)
        du2b = _bf(du2)
        du2_ref[...] = du2b
        dh1 = ALPHA * du2
        for j in range(nf):
            rows = pl.ds(j * FC, FC)
            dup = _bf(_dot_nt(du2b, wdn_ref[j]) * (2.0 * up_scr[:, rows].astype(F32)))
            dup_ref[:, rows] = dup
            dh1 = dh1 + _dot(dup, wup_ref[j])
        dh1_ref[...] = dh1

    row = lambda w: pl.BlockSpec((T, w), lambda i: (i, 0))
    vec = pl.BlockSpec((1, D_MODEL), lambda i: (0, 0))
    vshape = jax.ShapeDtypeStruct((1, D_MODEL), F32)
    return pl.pallas_call(
        body,
        grid=(S // T,),
        in_specs=[row(1024), row(1024), _gathered_spec(R_UP, R_BH), _gathered_spec(R_DN, R_UP), vec, vec],
        out_specs=[row(1024), row(D_FF), row(D_FF), row(1024), pl.BlockSpec((8, 128), lambda i: (0, 0)), vec, vec],
        out_shape=[
            jax.ShapeDtypeStruct((S, D_MODEL), F32),
            jax.ShapeDtypeStruct((S, D_FF), BF16),
            jax.ShapeDtypeStruct((S, D_FF), BF16),
            jax.ShapeDtypeStruct((S, D_MODEL), BF16),
            jax.ShapeDtypeStruct((8, 128), F32), vshape, vshape,
        ],
        scratch_shapes=[pltpu.VMEM((T, D_FF), BF16)],
        compiler_params=_cparams("arbitrary"),
        name="mlp_fwd_bwd",
    )(h1, target, wup_t, wdn, ln_g, ln_b)


def _local_step(x, mem, target, lb_logits, gain, sinks, rel_bias, ln1_g, ln1_b, ln2_g, ln2_b,
                win_t, dep0, other_weights, send_other_grads, send_small_grads, send_win_grad):
    S = x.shape[0]
    T = min(256, S)
    KC = min(2048, S)
    z_qfv, zmain, gl, xb = _in_proj(x, win_t, dep0, tm=min(512, S))
    bucket = _t5_bucket_table()

    o_raw, states = _hgrn_fwd(z_qfv, lb_logits, T=min(2048, S))
    o_b, swa_probs = _swa_fwd(zmain, bucket, rel_bias, sinks)
    g2 = other_weights((o_b, o_raw))
    mkv = _mem_kv_proj(mem, g2)
    o_c, mem_probs = _mem_fwd(zmain, mkv, T=min(1024, S))
    h1, h1b, pa, pb, pc, m, oa = _merge_fwd(o_raw, zmain, o_b, o_c, gl, x, gain, g2, g2, g2, g2, ln1_g, ln1_b,
                                                T=min(512, S))

    d_h1, act, d_up, du2, loss, d_ln2_g, d_ln2_b = _mlp_fwd_bwd(h1, target, g2, g2, ln2_g, ln2_b, T=min(512, S), FC=512)
    wgrad = functools.partial(_mm_tn, out_dtype=BF16)
    halves = lambda r0: (lambda i: (i // 2, r0 // 256 + i % 2, 0))
    whole = lambda r0: (lambda i: (0, r0 // 128, 0))
    og = lax.empty((N_DEV, R_OTHER, D_MODEL), BF16)
    og = wgrad(act, du2, kc=KC, name="grad_w_down", into=(og, (1, 256, D_MODEL), halves(R_DN)))
    og = wgrad(d_up, h1b, kc=KC, name="grad_w_up", into=(og, (1, 256, D_MODEL), halves(R_UP)))

    (dx_part, du1, dpa, dpb, dpc, d_oraw, d_ob, d_oc, d_hg_gl, d_gain, d_ln1_g, d_ln1_b) = _merge_bwd(
        d_h1, pa, pb, pc, m, o_raw, zmain, gl, x, gain, g2, g2, g2, g2, ln1_g, T=T)
    for a_op, b_op, r0, nm in ((m, du1, R_OUT, "out"), (oa, dpa, R_BH, "branch_hg"), (o_b, dpb, R_BS, "branch_swa"),
                               (o_c, dpc, R_BM, "branch_mem")):
        og = wgrad(a_op, b_op, kc=KC, name="grad_w_" + nm, into=(og, (N_DEV, 128, D_MODEL), whole(r0)))

    d_mq, d_mkv = _mem_bwd(zmain, mkv, o_c, mem_probs, d_oc, T=min(1024, S))
    og = wgrad(d_mkv, mem, kc=MEM_LEN, name="grad_w_mem_kv",
               into=(og, (1, 256, D_MODEL), lambda i: (i, R_KV // 256, 0)))
    sent_others = send_other_grads(og)
    d_sq, d_skv, d_rb, d_sink = _swa_bwd(zmain, o_b, swa_probs, d_ob, bucket, sent_others)
    d_qfv, d_lb = _hgrn_bwd(z_qfv, lb_logits, states, d_oraw, T=min(2048, S))
    sent_small = send_small_grads(_pack_small_grads(d_lb, d_gain, d_sink, d_rb, d_ln1_g, d_ln1_b, d_ln2_g, d_ln2_b, loss))

    head_major = lambda a: a.reshape(3, HG_HEADS, HG_DK, D_MODEL).transpose(1, 0, 2, 3).reshape(3 * D_MODEL, D_MODEL)
    pieces = (d_qfv, d_hg_gl, d_sq, d_skv, d_mq)
    placed = (
        ("qfv", d_qfv, 128, lambda i: ((i % 3) * HG_HEADS + i // 3, 0)),
        ("hg_gates", d_hg_gl, 256, lambda i: (jnp.where(i < 4, C_HG // 256 + i, C_GL // 256 + i - 4), 0)),
        ("swa_q", d_sq, None, lambda i: (C_SQ // 1024, 0)),
        ("swa_kv", d_skv, None, lambda i: (C_SK // 256, 0)),
        ("mem_q", d_mq, 256, lambda i: (C_MQ // 256 + i, 0)),
    )
    g_win_t = lax.empty((IN_COLS, D_MODEL), BF16)
    for nm, piece, tile, index in placed:
        g_win_t = wgrad(piece, xb, kc=KC, name="grad_w_in_" + nm, tm=tile,
                        into=(g_win_t, (tile or piece.shape[1], D_MODEL), index))
    sent_win = send_win_grad(g_win_t, sent_small)
    return _grad_x(*pieces, head_major(win_t[:C_HG]), win_t, dx_part, sent_win, tm=T)


MESH = pl.DeviceIdType.MESH
ANY = pl.BlockSpec(memory_space=pl.ANY)


def _coords():
    return lax.axis_index("x"), lax.axis_index("y"), lax.axis_index("c")


def _other_chips(x, y):
    return [(1 - x, y), (x, 1 - y), (1 - x, 1 - y)]


def _all_gather_weights(*arrays):
    na = len(arrays)

    def body(*refs):
        srcs, dsts = refs[:na], refs[na:2 * na]
        send_sems, recv_sems, local_sems = refs[2 * na:]
        x, y, c = _coords()
        me, sibling = (x, y, c), (x, y, 1 - c)
        chips = _other_chips(x, y)

        def slot(a, px, py, pc):
            return dsts[a].at[4 * px + 2 * py + pc]

        def copy(a, k, block, to, from_shard=False):
            return pltpu.make_async_remote_copy(
                src_ref=srcs[a] if from_shard else slot(a, *block), dst_ref=slot(a, *block),
                send_sem=send_sems.at[a * 7 + k], recv_sem=recv_sems.at[a * 7 + k],
                device_id=to, device_id_type=MESH)

        own = [pltpu.make_async_copy(srcs[a], slot(a, *me), local_sems.at[a]) for a in range(na)]
        for cp in own:
            cp.start()
        first = []
        for a in range(na):
            first.append(copy(a, 0, me, sibling, True))
            first += [copy(a, 1 + j, me, (*chip, c), True) for j, chip in enumerate(chips)]
        for cp in first:
            cp.start()
        passed = []
        for j, chip in enumerate(chips):
            for a in range(na):
                copy(a, 1 + j, (*chip, c), me).wait_recv()
                fwd = copy(a, 4 + j, (*chip, c), sibling)
                fwd.start()
                passed.append(fwd)
        for a in range(na):
            copy(a, 0, sibling, me).wait_recv()
            for j, chip in enumerate(chips):
                copy(a, 4 + j, (*chip, 1 - c), me).wait_recv()
        for cp in first + passed:
            cp.wait_send()
        for cp in own:
            cp.wait()

    return pl.pallas_call(
        body,
        in_specs=[ANY] * na,
        out_specs=[ANY] * na,
        out_shape=[jax.ShapeDtypeStruct((N_DEV,) + a.shape, a.dtype) for a in arrays],
        scratch_shapes=[pltpu.SemaphoreType.DMA((7 * na,)), pltpu.SemaphoreType.DMA((7 * na,)),
                        pltpu.SemaphoreType.DMA((na,))],
        name="all_gather_weights",
    )(*arrays)


HBM = pl.BlockSpec(memory_space=pltpu.HBM)
SEM = pl.BlockSpec(memory_space=pltpu.SEMAPHORE)
_DATAFLOW = pltpu.SideEffectType.DATAFLOW_SIDE_EFFECTING


def _peer(x, y, c, r):
    return x ^ (r >> 2), y ^ ((r >> 1) & 1), c ^ (r & 1)


def _direct_copies(src_ref, land_ref, send_sems, recv_sems, gather, receiving):
    x, y, c = _coords()
    me = 4 * x + 2 * y + c
    copies = []
    for r in range(1, N_DEV):
        px, py, pc = _peer(x, y, c, r)
        peer = 4 * px + 2 * py + pc
        if gather:
            src, dst = src_ref, land_ref.at[peer if receiving else me]
        else:
            src, dst = src_ref.at[peer], land_ref.at[r - 1]
        copies.append(pltpu.make_async_remote_copy(
            src_ref=src, dst_ref=dst, send_sem=send_sems.at[r - 1], recv_sem=recv_sems.at[r - 1],
            device_id=(px, py, pc), device_id_type=MESH))
    return copies


def _direct_start(src, land, *, gather, name, after=None):
    def body(src_ref, land_ref, *rest):
        send_sems, recv_sems, token = rest[-5], rest[-4], rest[-1]
        for cp in _direct_copies(src_ref, land_ref, send_sems, recv_sems, gather, False):
            cp.start()
        token[...] = jnp.zeros_like(token)

    afters = () if after is None else (after,)
    return pl.pallas_call(
        body,
        name=name,
        out_shape=(pltpu.SemaphoreType.DMA((N_DEV - 1,)), pltpu.SemaphoreType.DMA((N_DEV - 1,)),
                   pltpu.HBM(src.shape, src.dtype), pltpu.HBM(land.shape, land.dtype),
                   jax.ShapeDtypeStruct((8, 128), F32)),
        in_specs=(HBM, HBM) + tuple(ANY for _ in afters),
        out_specs=(SEM, SEM, HBM, HBM, pl.BlockSpec(memory_space=pltpu.VMEM)),
        input_output_aliases={0: 2, 1: 3},
        compiler_params=pltpu.CompilerParams(has_side_effects=_DATAFLOW),
    )(pltpu.with_memory_space_constraint(src, pltpu.HBM), pltpu.with_memory_space_constraint(land, pltpu.HBM), *afters)


def _direct_wait(send_sems, recv_sems, src_thru, land_thru, after, *, gather, name):
    afters = after if isinstance(after, tuple) else (after,)

    def body(src_ref, land_ref, send_sems_ref, recv_sems_ref, *rest):
        del rest
        for cp in _direct_copies(src_ref, land_ref, send_sems_ref, recv_sems_ref, gather, True):
            cp.wait_send()
            cp.wait_recv()

    return pl.pallas_call(
        body,
        name=name,
        out_shape=(pltpu.HBM(src_thru.shape, src_thru.dtype), pltpu.HBM(land_thru.shape, land_thru.dtype)),
        in_specs=(HBM, HBM, SEM, SEM) + tuple(ANY for _ in afters),
        out_specs=(HBM, HBM),
        input_output_aliases={0: 0, 1: 1},
        compiler_params=pltpu.CompilerParams(has_side_effects=_DATAFLOW),
    )(src_thru, land_thru, send_sems, recv_sems, *afters)


def _sum_partials(src, land, me, *, tr, name, wmv=None):
    R = src.shape[1]
    extra = () if wmv is None else tuple(wmv)

    def body(me_ref, s_ref, l_ref, *rest):
        del me_ref
        acc = s_ref[0].astype(F32)
        for r in range(N_DEV - 1):
            acc = acc + l_ref[r].astype(F32)
        rest[len(extra)][...] = acc
        if extra:
            w_ref, m_ref, v_ref, _, d_ref, nm_ref, nv_ref = rest
            d_ref[...], nm_ref[...], nv_ref[...] = _adam_step(w_ref[...], acc, m_ref[...], v_ref[...])

    row = pl.BlockSpec((tr, 1024), lambda i, mr: (i, 0))
    n_out = 4 if extra else 1
    res = pl.pallas_call(
        body,
        grid_spec=pltpu.PrefetchScalarGridSpec(
            num_scalar_prefetch=1, grid=(R // tr,),
            in_specs=[pl.BlockSpec((1, tr, 1024), lambda i, mr: (mr[0], i, 0)),
                      pl.BlockSpec((N_DEV - 1, tr, 1024), lambda i, mr: (0, i, 0))] + [row for _ in extra],
            out_specs=[row] * n_out),
        out_shape=[jax.ShapeDtypeStruct((R, 1024), F32)] * n_out,
        name=name,
    )(me, src, land, *extra)
    return res if extra else res[0]


_SMALL = ("lb_logits", "hg_norm_gain", "swa_sinks", "rel_bias", "ln1_g", "ln1_b", "ln2_g", "ln2_b")


def _pack_small_grads(d_lb, d_gain, d_sink, d_rb, d_ln1_g, d_ln1_b, d_ln2_g, d_ln2_b, loss):
    def body(lb_ref, gain_ref, sink_ref, rb_ref, l1g_ref, l1b_ref, l2g_ref, l2b_ref, loss_ref, o_ref):
        o_ref[...] = jnp.zeros_like(o_ref)
        for row, ref in ((SM_LB, lb_ref), (SM_GAIN, gain_ref), (SM_L1G, l1g_ref), (SM_L1B, l1b_ref),
                         (SM_L2G, l2g_ref), (SM_L2B, l2b_ref)):
            o_ref[row:row + 1, :] = ref[...]
        o_ref[SM_SINK:SM_SINK + 1, 0:128] = sink_ref[0:1, :]
        o_ref[SM_LOSS:SM_LOSS + 1, 0:128] = loss_ref[0:1, :]
        o_ref[SM_RB:SM_RB + NUM_BUCKETS, 0:128] = rb_ref[...]

    vm = pl.BlockSpec(memory_space=pltpu.VMEM)
    return pl.pallas_call(
        body,
        in_specs=[vm] * 9,
        out_specs=vm,
        out_shape=jax.ShapeDtypeStruct((SM_ROWS, D_MODEL), F32),
        name="pack_small_grads",
    )(d_lb, d_gain, d_sink, d_rb, d_ln1_g, d_ln1_b, d_ln2_g, d_ln2_b, loss)


def _small_finish(gathered, w, m, v):
    n = len(_SMALL)

    def body(*refs):
        g_ref = refs[0]
        w_refs, m_refs, v_refs = refs[1:1 + n], refs[1 + n:1 + 2 * n], refs[1 + 2 * n:1 + 3 * n]
        outs = refs[1 + 3 * n:]
        loss_ref, tot = outs[0], outs[-1]
        g_out, d_out, m_out, v_out = (outs[1 + k * n:1 + (k + 1) * n] for k in range(4))
        acc = g_ref[0]
        for d in range(1, N_DEV):
            acc = acc + g_ref[d]
        tot[...] = acc
        loss_ref[...] = tot[SM_LOSS:SM_LOSS + 1, 0:1]
        lb = _lower_bound(w_refs[0])
        dl0 = tot[SM_LB:SM_LB + 1, :] * lb * (1.0 - lb)
        grads = (jnp.concatenate([dl0, -dl0], axis=0), tot[SM_GAIN:SM_GAIN + 1, :],
                 tot[SM_SINK:SM_SINK + 1, 0:SWA_HEADS], tot[SM_RB:SM_RB + NUM_BUCKETS, 0:SWA_HEADS],
                 tot[SM_L1G:SM_L1G + 1, :], tot[SM_L1B:SM_L1B + 1, :], tot[SM_L2G:SM_L2G + 1, :], tot[SM_L2B:SM_L2B + 1, :])
        for k, g in enumerate(grads):
            g_out[k][...] = g
            d_out[k][...], m_out[k][...], v_out[k][...] = _adam_step(w_refs[k][...], g, m_refs[k][...], v_refs[k][...])

    vm = pl.BlockSpec(memory_space=pltpu.VMEM)
    shapes = [jax.ShapeDtypeStruct(w[k].shape, F32) for k in _SMALL]
    res = pl.pallas_call(
        body,
        in_specs=[vm] * (1 + 3 * n),
        out_specs=[vm] * (1 + 4 * n),
        out_shape=[jax.ShapeDtypeStruct((1, 1), F32)] + shapes * 4,
        scratch_shapes=[pltpu.VMEM((SM_ROWS, D_MODEL), F32)],
        name="small_finish",
    )(gathered, *[w[k] for k in _SMALL], *[m[k] for k in _SMALL], *[v[k] for k in _SMALL])
    parts = [dict(zip(_SMALL, res[1 + k * n:1 + (k + 1) * n])) for k in range(4)]
    return (res[0], *parts)


def _adam_step(w, g, m, v):
    nm = ADAM_B1 * m + (1.0 - ADAM_B1) * g
    nv = ADAM_B2 * v + (1.0 - ADAM_B2) * jnp.square(g)
    m_hat = nm / (1.0 - ADAM_B1 ** ADAM_STEP)
    v_hat = nv / (1.0 - ADAM_B2 ** ADAM_STEP)
    return -ADAM_LR * (m_hat / (jnp.sqrt(v_hat) + ADAM_EPS) + ADAM_WD * w), nm, nv


def _adamw_group(ws, ms, vs, grads=None, packed=None, name="adamw_group"):
    n = len(ws)
    g_in = list(grads) if packed is None else [packed[0]]

    def body(*refs):
        g_refs = refs[:len(g_in)]
        w_refs, m_refs, v_refs = (refs[len(g_in) + k * n:len(g_in) + (k + 1) * n] for k in range(3))
        outs = refs[len(g_in) + 3 * n:]
        for k in range(n):
            if packed is None:
                g = g_refs[k][...]
            else:
                r0, rows = packed[1][k]
                g = g_refs[0][r0:r0 + rows, :]
            outs[k][...] = g
            outs[n + k][...], outs[2 * n + k][...], outs[3 * n + k][...] = _adam_step(
                w_refs[k][...], g, m_refs[k][...], v_refs[k][...])

    vm = pl.BlockSpec(memory_space=pltpu.VMEM)
    shapes = [jax.ShapeDtypeStruct(a.shape, F32) for a in ws]
    res = pl.pallas_call(
        body,
        in_specs=[vm] * (len(g_in) + 3 * n),
        out_specs=[vm] * (4 * n),
        out_shape=shapes * 4,
        compiler_params=pltpu.CompilerParams(vmem_limit_bytes=VMEM_LIMIT),
        name=name,
    )(*g_in, *ws, *ms, *vs)
    return [res[k * n:(k + 1) * n] for k in range(4)]


_WEIGHTS = ("w_in", "lb_logits", "hg_norm_gain", "swa_sinks", "rel_bias", "w_mem_kv", "w_branch_hg", "w_branch_swa",
            "w_branch_mem", "w_out", "ln1_g", "ln1_b", "w_up", "w_down", "ln2_g", "ln2_b")


def kernel(x, mem, w_in, lb_logits, hg_norm_gain, swa_sinks, rel_bias, w_mem_kv, w_branch_hg, w_branch_swa, w_branch_mem, w_out, ln1_g, ln1_b, w_up, w_down, ln2_g, ln2_b, loss_target, m_w_in, m_lb_logits, m_hg_norm_gain, m_swa_sinks, m_rel_bias, m_w_mem_kv, m_w_branch_hg, m_w_branch_swa, m_w_branch_mem, m_w_out, m_ln1_g, m_ln1_b, m_w_up, m_w_down, m_ln2_g, m_ln2_b, v_w_in, v_lb_logits, v_hg_norm_gain, v_swa_sinks, v_rel_bias, v_w_mem_kv, v_w_branch_hg, v_w_branch_swa, v_w_branch_mem, v_w_out, v_ln1_g, v_ln1_b, v_w_up, v_w_down, v_ln2_g, v_ln2_b):
    w = dict(w_in=w_in, lb_logits=lb_logits, hg_norm_gain=hg_norm_gain, swa_sinks=swa_sinks, rel_bias=rel_bias,
             w_mem_kv=w_mem_kv, w_branch_hg=w_branch_hg, w_branch_swa=w_branch_swa, w_branch_mem=w_branch_mem,
             w_out=w_out, ln1_g=ln1_g, ln1_b=ln1_b, w_up=w_up, w_down=w_down, ln2_g=ln2_g, ln2_b=ln2_b)
    mom = dict(w_in=m_w_in, lb_logits=m_lb_logits, hg_norm_gain=m_hg_norm_gain, swa_sinks=m_swa_sinks, rel_bias=m_rel_bias,
               w_mem_kv=m_w_mem_kv, w_branch_hg=m_w_branch_hg, w_branch_swa=m_w_branch_swa, w_branch_mem=m_w_branch_mem,
               w_out=m_w_out, ln1_g=m_ln1_g, ln1_b=m_ln1_b, w_up=m_w_up, w_down=m_w_down, ln2_g=m_ln2_g, ln2_b=m_ln2_b)
    var = dict(w_in=v_w_in, lb_logits=v_lb_logits, hg_norm_gain=v_hg_norm_gain, swa_sinks=v_swa_sinks, rel_bias=v_rel_bias,
               w_mem_kv=v_w_mem_kv, w_branch_hg=v_w_branch_hg, w_branch_swa=v_w_branch_swa, w_branch_mem=v_w_branch_mem,
               w_out=v_w_out, ln1_g=v_ln1_g, ln1_b=v_ln1_b, w_up=v_w_up, w_down=v_w_down, ln2_g=v_ln2_g, ln2_b=v_ln2_b)
    xc, yc, cc = _coords()

    p1 = _bf(w_in[0].T)
    p2 = _bf(jnp.concatenate([w_down[0], w_up[0].T, w_branch_hg[0], w_branch_swa[0], w_branch_mem[0], w_out[0],
                              w_mem_kv[0].T], axis=0))
    me = 4 * xc + 2 * yc + cc
    (g1,) = _all_gather_weights(p1)
    land2 = lax.dynamic_update_slice(lax.empty((N_DEV, R_OTHER, D_MODEL), BF16), p2[None], (me, 0, 0))
    ag2 = _direct_start(p2, land2, gather=True, name="gather_other_weights_start")

    def other_weights(after):
        return _direct_wait(*ag2[:4], after, gather=True, name="gather_other_weights_wait")[1]

    blocks = lambda a: a.reshape(N_DEV, a.shape[0] // N_DEV, D_MODEL)
    started = {}

    def send_other_grads(part):
        started["others"] = _direct_start(part, lax.empty((N_DEV - 1, R_OTHER, D_MODEL), BF16), gather=False,
                                          name="scatter_other_grads_start")
        return started["others"][4]

    me1 = me.reshape(1).astype(jnp.int32)
    grads, delta, new_m, new_v = {}, {}, {}, {}

    def send_small_grads(packed):
        land = lax.dynamic_update_slice(lax.empty((N_DEV, SM_ROWS, D_MODEL), F32), packed[None], (me, 0, 0))
        started["small"] = _direct_start(packed, land, gather=True, name="gather_small_grads_start")
        return started["small"][4]

    def send_win_grad(g, after):
        started["win"] = _direct_start(blocks(g), lax.empty((N_DEV - 1, IN_SHARD, D_MODEL), BF16), gather=False,
                                       name="scatter_w_in_grad_start", after=after)
        mine2, landed2 = _direct_wait(*started["others"][:4], started["win"][4], gather=False,
                                      name="scatter_other_grads_wait")
        gs2 = _sum_partials(mine2, landed2, me1, tr=R_OTHER // 2, name="sum_other_grads")
        rowwise = (("w_down", R_DN, R_UP), ("w_branch_hg", R_BH, R_BS), ("w_branch_swa", R_BS, R_BM),
                   ("w_branch_mem", R_BM, R_OUT), ("w_out", R_OUT, R_KV))
        colwise = (("w_up", R_UP, R_BH), ("w_mem_kv", R_KV, R_OTHER))
        for names, kw in (([n for n, _, _ in rowwise], dict(packed=(gs2, [(lo, hi - lo) for _, lo, hi in rowwise]))),
                          ([n for n, _, _ in colwise], dict(grads=[gs2[lo:hi].T for _, lo, hi in colwise]))):
            res = _adamw_group([w[n][0] for n in names], [mom[n][0] for n in names], [var[n][0] for n in names],
                               name="adamw_" + "_".join(n[2:] for n in names), **kw)
            for dst, vals in zip((grads, delta, new_m, new_v), res):
                dst.update(zip(names, vals))
        return (new_v["w_down"], new_v["w_up"])

    grad_x = _local_step(
        x[0], mem[0], loss_target[0], lb_logits, hg_norm_gain, swa_sinks, rel_bias, ln1_g, ln1_b, ln2_g, ln2_b,
        g1.reshape(IN_COLS, D_MODEL), ag2[4], other_weights, send_other_grads, send_small_grads, send_win_grad)

    mine1, landed1 = _direct_wait(*started["win"][:4], grad_x, gather=False, name="scatter_w_in_grad_wait")
    g_win_t, d_t, m_t, v_t = _sum_partials(mine1, landed1, me1, tr=IN_SHARD // 2, name="sum_adamw_w_in",
                                           wmv=(w_in[0].T, m_w_in[0].T, v_w_in[0].T))
    grads["w_in"], delta["w_in"], new_m["w_in"], new_v["w_in"] = g_win_t.T, d_t.T, m_t.T, v_t.T

    _, gathered = _direct_wait(*started["small"][:4], grad_x, gather=True, name="gather_small_grads_wait")
    loss, g_s, d_s, m_s, v_s = _small_finish(gathered, w, mom, var)
    for dst, src in ((grads, g_s), (delta, d_s), (new_m, m_s), (new_v, v_s)):
        dst.update(src)

    def shaped(d, name):
        return d[name].reshape(w[name].shape)

    return (loss.reshape(()), grad_x[None], *[shaped(grads, n) for n in _WEIGHTS], *[shaped(delta, n) for n in _WEIGHTS],
            *[shaped(new_m, n) for n in _WEIGHTS], *[shaped(new_v, n) for n in _WEIGHTS])
```

```python
import functools
import math

import jax
import jax.numpy as jnp
from jax import lax
from jax.experimental import pallas as pl
from jax.experimental.pallas import tpu as pltpu

F32 = jnp.float32
BF16 = jnp.bfloat16

D_MODEL = 1024
MEM_LEN = 256
HG_HEADS = 8
HG_DK = 128
HG_CHUNK = 64
SWA_HEADS = 16
SWA_HEAD_DIM = 64
SWA_BLOCK = 128
SWA_WINDOW = 128
MEM_HEADS = 4
MEM_HEAD_DIM = 256
NUM_BUCKETS = 32
MAX_DISTANCE = 128
D_FF = 4096
LN_EPS = 1e-5
RMS_EPS = 1e-6
ALPHA = 2.0 ** 0.25
N_DEV = 8

C_HQ, C_HF, C_HI, C_HG, C_SQ, C_SK, C_SV, C_MQ, C_GL = 0, 1024, 2048, 3072, 4096, 5120, 5248, 5376, 6400
IN_COLS = 9472
IN_SHARD = IN_COLS // N_DEV
Z_HG, Z_SQ, Z_SK, Z_MQ, Z_REST = 0, C_SQ - C_HG, C_SK - C_HG, C_MQ - C_HG, C_GL - C_HG

ADAM_LR = 0.001
ADAM_B1 = 0.9
ADAM_B2 = 0.999
ADAM_EPS = 1e-08
ADAM_WD = 0.01
ADAM_STEP = 10

VMEM_LIMIT = 58 * 1024 * 1024

R_DN, R_UP, R_BH, R_BS, R_BM, R_OUT, R_KV, R_OTHER = 0, 512, 1024, 1152, 1280, 1408, 1536, 1792

SM_LB, SM_GAIN, SM_SINK, SM_L1G, SM_L1B, SM_L2G, SM_L2B, SM_LOSS, SM_RB, SM_ROWS = 0, 2, 3, 4, 5, 6, 7, 8, 16, 48


def _bf(v):
    return v.astype(BF16)


def _f32(v):
    return v.astype(F32)


def _dot(a, b):
    return jnp.dot(a, b, preferred_element_type=F32)


def _dot_nt(a, b):
    return lax.dot_general(a, b, (((1,), (1,)), ((), ())), preferred_element_type=F32)


def _dot_tn(a, b):
    return lax.dot_general(a, b, (((0,), (0,)), ((), ())), preferred_element_type=F32)


def _sig(v):
    return 0.5 * jnp.tanh(0.5 * v) + 0.5


def _cparams(*sem):
    return pltpu.CompilerParams(dimension_semantics=sem, vmem_limit_bytes=VMEM_LIMIT)


def _const_spec(shape):
    nd = len(shape)
    return pl.BlockSpec(shape, lambda *_: (0,) * nd, pipeline_mode=pl.Buffered(1))


def _dep_spec():
    return pl.BlockSpec((8, 128), lambda *_: (0, 0))


def _in_proj(x, win_t, dep, *, tm):
    S = x.shape[0]

    def body(x_ref, w_ref, dep_ref, qfv_ref, z_ref, gl_ref, xb_ref):
        del dep_ref
        xb = _bf(x_ref[...])
        xb_ref[...] = xb
        for c0 in range(0, C_HG, 1024):
            qfv_ref[:, c0:c0 + 1024] = _dot_nt(xb, w_ref[c0:c0 + 1024, :])
        for c0 in range(0, Z_REST, Z_REST // 2):
            z_ref[:, c0:c0 + Z_REST // 2] = _bf(_dot_nt(xb, w_ref[C_HG + c0:C_HG + c0 + Z_REST // 2, :]))
        for c0 in range(0, IN_COLS - C_GL, 1024):
            gl_ref[:, c0:c0 + 1024] = _bf(_dot_nt(xb, w_ref[C_GL + c0:C_GL + c0 + 1024, :]))

    row = lambda w: pl.BlockSpec((tm, w), lambda i: (i, 0))
    return pl.pallas_call(
        body,
        grid=(S // tm,),
        in_specs=[row(D_MODEL), _const_spec(win_t.shape), _dep_spec()],
        out_specs=[row(C_HG), row(Z_REST), row(IN_COLS - C_GL), row(D_MODEL)],
        out_shape=[jax.ShapeDtypeStruct((S, C_HG), F32), jax.ShapeDtypeStruct((S, Z_REST), BF16),
                   jax.ShapeDtypeStruct((S, IN_COLS - C_GL), BF16), jax.ShapeDtypeStruct((S, D_MODEL), BF16)],
        compiler_params=_cparams("parallel"),
        name="in_proj",
    )(x, win_t, dep)


def _placement(into, tm, N, M, out_dtype):
    if into is None:
        return (lambda i: (i, 0)), (tm, N), jax.ShapeDtypeStruct((M, N), out_dtype), (), {}
    dest, block, index = into
    assert math.prod(block) == tm * N and dest.dtype == out_dtype
    return index, block, jax.ShapeDtypeStruct(dest.shape, dest.dtype), (dest,), {2: 0}


def _mm_tn_resident(a, b, *, tm, kc, name, out_dtype, into=None):
    K, M = a.shape
    N = b.shape[1]
    nk = K // kc
    index, block, out_shape, extra, aliases = _placement(into, tm, N, M, out_dtype)

    def body(a_ref, b_ref, *rest):
        o_ref = rest[-1]
        acc = jnp.zeros((tm, N), F32)
        for kk in range(nk):
            sl = pl.ds(kk * kc, kc)
            acc = acc + _dot_tn(_bf(a_ref[sl, :]), _bf(b_ref[sl, :]))
        o_ref[...] = acc.astype(o_ref.dtype).reshape(block)

    return pl.pallas_call(
        body,
        grid=(M // tm,),
        in_specs=[pl.BlockSpec((K, tm), lambda i: (0, i)), _const_spec((K, N))] + [ANY for _ in extra],
        out_specs=pl.BlockSpec(block, index),
        out_shape=out_shape,
        input_output_aliases=aliases,
        compiler_params=_cparams("parallel"),
        name=name,
    )(a, b, *extra)


def _mm_tn(a, b, *, kc, name, out_dtype=F32, into=None, tm=None):
    K, M = a.shape
    N = b.shape[1]
    if M > 1024 or tm is not None:
        return _mm_tn_resident(a, b, tm=tm or 256, kc=min(kc, 1024), name=name, out_dtype=out_dtype, into=into)
    tm = M
    if a.dtype == BF16 and b.dtype == BF16 and K % (2 * kc) == 0:
        kc = 2 * kc
    nk = K // kc
    index, block, out_shape, extra, aliases = _placement(into, tm, N, M, out_dtype)

    def body(a_ref, b_ref, *rest):
        o_ref, acc = rest[-2], rest[-1]
        k = pl.program_id(1)
        part = _dot_tn(_bf(a_ref[...]), _bf(b_ref[...]))

        @pl.when(k == 0)
        def _():
            acc[...] = part

        @pl.when(k > 0)
        def _():
            acc[...] += part

        @pl.when(k == nk - 1)
        def _():
            o_ref[...] = acc[...].astype(o_ref.dtype).reshape(block)

    return pl.pallas_call(
        body,
        grid=(M // tm, nk),
        in_specs=[pl.BlockSpec((kc, tm), lambda i, k: (k, i)), pl.BlockSpec((kc, N), lambda i, k: (k, 0))]
        + [ANY for _ in extra],
        out_specs=pl.BlockSpec(block, lambda i, k: index(i)),
        out_shape=out_shape,
        input_output_aliases=aliases,
        scratch_shapes=[pltpu.VMEM((tm, N), F32)],
        compiler_params=_cparams("parallel", "arbitrary"),
        name=name,
    )(a, b, *extra)


def _grad_x(d_qfv, d_hg_gl, d_sq, d_skv, d_mq, w_qfv, win_t, add, deps, *, tm):
    M = add.shape[0]
    pieces = (d_qfv, d_hg_gl, d_sq, d_skv, d_mq)

    def body(qfv_ref, hggl_ref, sq_ref, skv_ref, mq_ref, wq_ref, w_ref, add_ref, *rest):
        o_ref = rest[-1]
        acc = add_ref[...] + _dot(qfv_ref[...], wq_ref[...])
        acc = acc + _dot(hggl_ref[:, 0:1024], w_ref[C_HG:C_SQ, :])
        acc = acc + _dot(hggl_ref[:, 1024:4096], w_ref[C_GL:IN_COLS, :])
        acc = acc + _dot(sq_ref[...], w_ref[C_SQ:C_SK, :])
        acc = acc + _dot(skv_ref[...], w_ref[C_SK:C_MQ, :])
        o_ref[...] = acc + _dot(mq_ref[...], w_ref[C_MQ:C_GL, :])

    return pl.pallas_call(
        body,
        grid=(M // tm,),
        in_specs=[pl.BlockSpec((tm, p.shape[1]), lambda i: (i, 0)) for p in pieces]
        + [_const_spec(w_qfv.shape), _const_spec(win_t.shape), pl.BlockSpec((tm, D_MODEL), lambda i: (i, 0))]
        + [_dep_spec() for _ in deps],
        out_specs=pl.BlockSpec((tm, D_MODEL), lambda i: (i, 0)),
        out_shape=jax.ShapeDtypeStruct((M, D_MODEL), F32),
        compiler_params=_cparams("parallel"),
        name="grad_x",
    )(*pieces, w_qfv, win_t, add, *deps)


def _lower_bound(lbl_ref):
    l0 = lbl_ref[0:1, :]
    l1 = lbl_ref[1:2, :]
    mx = jnp.maximum(l0, l1)
    e0 = jnp.exp(l0 - mx)
    e1 = jnp.exp(l1 - mx)
    return e0 / (e0 + e1)


def _tri(lower):
    r = lax.broadcasted_iota(jnp.int32, (HG_CHUNK, HG_CHUNK), 0)
    c = lax.broadcasted_iota(jnp.int32, (HG_CHUNK, HG_CHUNK), 1)
    return (r >= c) if lower else (r <= c)


def _hg_gates(fl, lb):
    sg = _sig(fl)
    f = lb + (1.0 - lb) * sg
    return sg, f, jnp.log(f), 1.0 - f


def _scan_rows(v, reverse=False):
    row = lax.broadcasted_iota(jnp.int32, v.shape, 0)
    s = 1
    while s < HG_CHUNK:
        if reverse:
            v = v + jnp.where(row < HG_CHUNK - s, pltpu.roll(v, HG_CHUNK - s, 0), 0.0)
        else:
            v = v + jnp.where(row >= s, pltpu.roll(v, s, 0), 0.0)
        s *= 2
    return v


def _hgrn_fwd(zmain, lb_logits, *, T):
    S = zmain.shape[0]
    nc = T // HG_CHUNK

    def body(q_ref, f_ref, v_ref, lbl_ref, o_ref, st_ref, state):
        @pl.when(pl.program_id(1) == 0)
        def _():
            state[...] = jnp.zeros_like(state)

        lb = _lower_bound(lbl_ref)
        tril = _tri(True)
        qis, updates, decays, intra = [], [], [], []
        for c in range(nc):
            sl = pl.ds(c * HG_CHUNK, HG_CHUNK)
            _, _, g, k = _hg_gates(_f32(f_ref[sl, :]), lb)
            b = _scan_rows(g)
            bl = jnp.sum(g, axis=0, keepdims=True)
            qi = _bf(_f32(q_ref[sl, :]) * jnp.exp(b))
            ki = _bf(k * jnp.exp(-b))
            ko = _bf(k * jnp.exp(bl - b))
            vb = _bf(v_ref[sl, :])
            att = jnp.where(tril, _dot_nt(qi, ki), 0.0)
            intra.append(_dot(_bf(att), vb))
            qis.append(qi)
            updates.append(_dot_tn(vb, ko))
            decays.append(jnp.exp(bl))
        st = state[...]
        for c in range(nc):
            st_ref[0, c] = st
            o_ref[pl.ds(c * HG_CHUNK, HG_CHUNK), :] = intra[c] + _dot_nt(qis[c], _bf(st))
            st = st * decays[c] + updates[c]
        state[...] = st

    col = lambda base: pl.BlockSpec((T, HG_DK), lambda h, t: (t, base + h))
    return pl.pallas_call(
        body,
        grid=(HG_HEADS, S // T),
        in_specs=[col(0), col(8), col(16), pl.BlockSpec((2, HG_DK), lambda h, t: (0, h))],
        out_specs=[
            pl.BlockSpec((T, HG_DK), lambda h, t: (t, h)),
            pl.BlockSpec((1, nc, HG_DK, HG_DK), lambda h, t: (h, t, 0, 0)),
        ],
        out_shape=[
            jax.ShapeDtypeStruct((S, D_MODEL), F32),
            jax.ShapeDtypeStruct((HG_HEADS, S // HG_CHUNK, HG_DK, HG_DK), F32),
        ],
        scratch_shapes=[pltpu.VMEM((HG_DK, HG_DK), F32)],
        compiler_params=_cparams("parallel", "arbitrary"),
        name="hgrn_fwd",
    )(zmain, zmain, zmain, lb_logits)


def _hgrn_bwd(zmain, lb_logits, states, d_o, *, T):
    S = zmain.shape[0]
    nc = T // HG_CHUNK
    nt = S // T

    def body(q_ref, f_ref, v_ref, lbl_ref, st_ref, do_ref, dz_ref, dlb_ref, dstate):
        @pl.when(pl.program_id(1) == 0)
        def _():
            dstate[...] = jnp.zeros_like(dstate)
            dlb_ref[...] = jnp.zeros_like(dlb_ref)

        lb = _lower_bound(lbl_ref)
        tril = _tri(True)
        last_row = lax.broadcasted_iota(jnp.int32, (HG_CHUNK, HG_DK), 0) == HG_CHUNK - 1
        saved = []
        for c in range(nc):
            sl = pl.ds(c * HG_CHUNK, HG_CHUNK)
            sg, f, g, k = _hg_gates(_f32(f_ref[sl, :]), lb)
            b = _scan_rows(g)
            bl = jnp.sum(g, axis=0, keepdims=True)
            eb = jnp.exp(b)
            enb = jnp.exp(-b)
            eo = jnp.exp(bl - b)
            q_in = _f32(q_ref[sl, :]) * eb
            k_in = k * enb
            k_out = k * eo
            qi, ki, ko = _bf(q_in), _bf(k_in), _bf(k_out)
            vb = _bf(v_ref[sl, :])
            dob = do_ref[sl, :]
            att = jnp.where(tril, _dot_nt(qi, ki), 0.0)
            d_att = _bf(jnp.where(tril, _dot_nt(dob, vb), 0.0))
            d_kin = _dot_tn(d_att, qi)
            saved.append(dict(
                sg=sg, f=f, eb=eb, enb=enb, eo=eo, ebl=jnp.exp(bl), k_out=k_out, ko=ko, vb=vb, dob=dob,
                d_v=_dot_tn(_bf(att), dob), d_qin=_dot(d_att, ki), d_kin=d_kin,
                qk=(q_in, k_in), d_state=_dot_tn(dob, qi)))
        dst = dstate[...]
        dsts = [None] * nc
        for c in reversed(range(nc)):
            dsts[c] = dst
            dst = dst * saved[c]["ebl"] + saved[c]["d_state"]
        dstate[...] = dst
        dlb = jnp.zeros((1, HG_DK), F32)
        for c in range(nc):
            sl = pl.ds(c * HG_CHUNK, HG_CHUNK)
            s = saved[c]
            q_in, k_in = s["qk"]
            st = st_ref[0, c]
            dstb = _bf(dsts[c])
            d_v = s["d_v"] + _dot_nt(s["ko"], dstb)
            d_qin = s["d_qin"] + _dot(s["dob"], _bf(st))
            d_kout = _dot(s["vb"], dstb)
            d_decay = jnp.sum(dsts[c] * st, axis=0, keepdims=True)
            kk = d_kout * s["k_out"]
            d_b = d_qin * q_in - s["d_kin"] * k_in - kk
            d_bl = jnp.sum(kk, axis=0, keepdims=True) + d_decay * s["ebl"]
            d_g = _scan_rows(d_b + jnp.where(last_row, d_bl, 0.0), reverse=True)
            d_f = d_g / s["f"] - (s["d_kin"] * s["enb"] + d_kout * s["eo"])
            dz_ref[sl, 0:HG_DK] = _bf(d_qin * s["eb"])
            dz_ref[sl, HG_DK:2 * HG_DK] = _bf(d_f * (1.0 - lb) * s["sg"] * (1.0 - s["sg"]))
            dz_ref[sl, 2 * HG_DK:3 * HG_DK] = _bf(d_v)
            dlb = dlb + jnp.sum(d_f * (1.0 - s["sg"]), axis=0, keepdims=True)
        dlb_ref[...] += dlb

    rev = lambda base: pl.BlockSpec((T, HG_DK), lambda h, t: (nt - 1 - t, base + h))
    outc = pl.BlockSpec((T, HG_DK), lambda h, t: (nt - 1 - t, h))
    return pl.pallas_call(
        body,
        grid=(HG_HEADS, nt),
        in_specs=[
            rev(0), rev(8), rev(16),
            pl.BlockSpec((2, HG_DK), lambda h, t: (0, h)),
            pl.BlockSpec((1, nc, HG_DK, HG_DK), lambda h, t: (h, nt - 1 - t, 0, 0)),
            outc,
        ],
        out_specs=[pl.BlockSpec((T, 3 * HG_DK), lambda h, t: (nt - 1 - t, h)),
                   pl.BlockSpec((1, HG_DK), lambda h, t: (0, h))],
        out_shape=[jax.ShapeDtypeStruct((S, 3 * D_MODEL), BF16), jax.ShapeDtypeStruct((1, D_MODEL), F32)],
        scratch_shapes=[pltpu.VMEM((HG_DK, HG_DK), F32)],
        compiler_params=_cparams("parallel", "arbitrary"),
        name="hgrn_bwd",
    )(zmain, zmain, zmain, lb_logits, states, d_o)


def _t5_bucket_table():
    qi = jnp.arange(SWA_BLOCK)[:, None] + SWA_BLOCK
    kj = jnp.arange(2 * SWA_BLOCK)[None, :]
    n = jnp.clip(qi - kj, 0, SWA_WINDOW - 1)
    max_exact = NUM_BUCKETS // 2
    nf = jnp.maximum(n, 1).astype(F32)
    large = max_exact + (jnp.log(nf / max_exact) / math.log(MAX_DISTANCE / max_exact)
                         * (NUM_BUCKETS - max_exact)).astype(jnp.int32)
    large = jnp.minimum(large, NUM_BUCKETS - 1)
    return jnp.where(n < max_exact, n, large).astype(jnp.int32)


SWA_ROWS = 32
MERGE_GROUPS = 1


def _swa_bias_init(bias, bucket_ref, rb_ref):
    bk = bucket_ref[...]
    qi = lax.broadcasted_iota(jnp.int32, bk.shape, 0) + SWA_BLOCK
    kj = lax.broadcasted_iota(jnp.int32, bk.shape, 1)
    band = (qi - kj >= 0) & (qi - kj < SWA_WINDOW)
    for h in range(SWA_HEADS):
        def sel(b, acc, h=h):
            return jnp.where(bk == b, rb_ref[b, h], acc)
        t = lax.fori_loop(0, NUM_BUCKETS, sel, jnp.zeros(bk.shape, F32))
        bias[1, h] = jnp.where(band, t, -jnp.inf)
        bias[0, h] = jnp.where(band & (kj >= SWA_BLOCK), t, -jnp.inf)


def _lane_halves(t, kv_head):
    lane = lax.broadcasted_iota(jnp.int32, t.shape, 1)
    rolled = pltpu.roll(t, 64, 1)
    zero = jnp.zeros_like(t)
    if kv_head == 0:
        return jnp.where(lane < 64, t, zero), jnp.where(lane >= 64, rolled, zero)
    return jnp.where(lane < 64, rolled, zero), jnp.where(lane >= 64, t, zero)


def _swa_zero_key0(t):
    return jnp.where(lax.broadcasted_iota(jnp.int32, t.shape, 0) == 0, jnp.zeros_like(t), t)


def _swa_probs(s, masked_bias, sink):
    s = s + masked_bias
    m = jnp.maximum(jnp.max(s, axis=-1, keepdims=True), sink)
    p = jnp.exp(s - m)
    es = jnp.exp(sink - m)
    inv = 1.0 / (jnp.sum(p, axis=-1, keepdims=True) + es)
    return p * inv, es * inv


def _swa_fwd(zmain, bucket, rel_bias, sinks):
    S = zmain.shape[0]
    nb = S // SWA_BLOCK
    scale = SWA_HEAD_DIM ** -0.5

    def body(q_ref, kvc_ref, kvp_ref, bucket_ref, rb_ref, sk_ref, o_ref, p_ref, bias):
        n = pl.program_id(0)

        @pl.when(n == 0)
        def _():
            _swa_bias_init(bias, bucket_ref, rb_ref)

        later = jnp.minimum(n, 1)
        kk = _bf(jnp.concatenate([kvp_ref[:, 0:128], kvc_ref[:, 0:128]], axis=0))
        vv = _swa_zero_key0(_bf(jnp.concatenate([kvp_ref[:, 128:256], kvc_ref[:, 128:256]], axis=0)))
        first_col = lax.broadcasted_iota(jnp.int32, (SWA_ROWS, 2 * SWA_BLOCK), 1) == 0
        scores, values = {}, {}
        for kvh in range(2):
            qst = _bf(jnp.concatenate([q_ref[:, pl.ds((kvh * 4 + jj) * 128, 128)] for jj in range(4)], axis=0) * scale)
            values[kvh] = _lane_halves(vv, kvh)
            for odd, kx in enumerate(_lane_halves(kk, kvh)):
                scores[kvh, odd] = _dot_nt(qst, kx)
        probs = {}
        for (kvh, odd), s in scores.items():
            parts = []
            for jj in range(4):
                h = 2 * (kvh * 4 + jj) + odd
                for r0 in range(0, SWA_BLOCK, SWA_ROWS):
                    p, ps = _swa_probs(s[jj * SWA_BLOCK + r0:jj * SWA_BLOCK + r0 + SWA_ROWS],
                                       bias[later, h, pl.ds(r0, SWA_ROWS), :], sk_ref[0, h])
                    part = _bf(jnp.where(first_col, ps, p))
                    p_ref[pl.ds(r0, SWA_ROWS), pl.ds(h * 2 * SWA_BLOCK, 2 * SWA_BLOCK)] = part
                    parts.append(part)
            probs[kvh, odd] = jnp.concatenate(parts, axis=0)
        for kvh in range(2):
            ost = _dot(probs[kvh, 0], values[kvh][0]) + _dot(probs[kvh, 1], values[kvh][1])
            for jj in range(4):
                o_ref[:, pl.ds((kvh * 4 + jj) * 128, 128)] = ost[jj * SWA_BLOCK:(jj + 1) * SWA_BLOCK]

    smem = pl.BlockSpec(memory_space=pltpu.SMEM)
    return pl.pallas_call(
        body,
        grid=(nb,),
        in_specs=[
            pl.BlockSpec((SWA_BLOCK, 1024), lambda n: (n, Z_SQ // 1024)),
            pl.BlockSpec((SWA_BLOCK, 256), lambda n: (n, Z_SK // 256)),
            pl.BlockSpec((SWA_BLOCK, 256), lambda n: (jnp.maximum(n - 1, 0), Z_SK // 256)),
            _const_spec((SWA_BLOCK, 2 * SWA_BLOCK)), smem, smem,
        ],
        out_specs=[pl.BlockSpec((SWA_BLOCK, 1024), lambda n: (n, 0)),
                   pl.BlockSpec((SWA_BLOCK, SWA_HEADS * 2 * SWA_BLOCK), lambda n: (n, 0))],
        out_shape=[jax.ShapeDtypeStruct((S, 1024), F32),
                   jax.ShapeDtypeStruct((S, SWA_HEADS * 2 * SWA_BLOCK), BF16)],
        scratch_shapes=[pltpu.VMEM((2, SWA_HEADS, SWA_BLOCK, 2 * SWA_BLOCK), F32)],
        compiler_params=_cparams("arbitrary"),
        name="swa_fwd",
    )(zmain, zmain, zmain, bucket, rel_bias, sinks)


def _swa_bwd(zmain, o_b, probs, d_o, bucket, dep):
    S = zmain.shape[0]
    nb = S // SWA_BLOCK
    scale = SWA_HEAD_DIM ** -0.5

    def body(q_ref, kvc_ref, kvp_ref, o_ref, p_ref, do_ref, bucket_ref, dep_ref,
             dq_ref, dkv_ref, drb_ref, dsk_ref, dbias, carry):
        del dep_ref
        n = pl.program_id(0)

        @pl.when(n == 0)
        def _():
            dbias[...] = jnp.zeros_like(dbias)
            carry[...] = jnp.zeros_like(carry)

        @pl.when(n < nb)
        def _():
            kk = _swa_zero_key0(_bf(jnp.concatenate([kvp_ref[:, 0:128], kvc_ref[:, 0:128]], axis=0)))
            vv = _swa_zero_key0(_bf(jnp.concatenate([kvp_ref[:, 128:256], kvc_ref[:, 128:256]], axis=0)))
            lane = lax.broadcasted_iota(jnp.int32, (2 * SWA_BLOCK, 128), 1)
            lane_q = lax.broadcasted_iota(jnp.int32, (4 * SWA_BLOCK, 128), 1)
            pair_cols = {kvh: [pl.ds((kvh * 4 + jj) * 128, 128) for jj in range(4)] for kvh in range(2)}
            qst, dost, ks, d_p, delta = {}, {}, {}, {}, {}
            for kvh in range(2):
                qst[kvh] = _bf(jnp.concatenate([q_ref[:, cl] for cl in pair_cols[kvh]], axis=0) * scale)
                dost[kvh] = jnp.concatenate([do_ref[:, cl] for cl in pair_cols[kvh]], axis=0)
                prod = dost[kvh].astype(F32) * jnp.concatenate([o_ref[:, cl] for cl in pair_cols[kvh]], axis=0)
                ks[kvh] = _lane_halves(kk, kvh)
                for odd, vx in enumerate(_lane_halves(vv, kvh)):
                    keep = (lane_q >= 64) if odd else (lane_q < 64)
                    delta[kvh, odd] = jnp.sum(jnp.where(keep, prod, 0.0), axis=-1, keepdims=True)
                    d_p[kvh, odd] = _dot_nt(dost[kvh], vx)
            pst, dsst = {}, {}
            for (kvh, odd), dp in d_p.items():
                p_parts, ds_parts = [], []
                for jj in range(4):
                    h = 2 * (kvh * 4 + jj) + odd
                    rows = slice(jj * SWA_BLOCK, (jj + 1) * SWA_BLOCK)
                    p = p_ref[:, pl.ds(h * 2 * SWA_BLOCK, 2 * SWA_BLOCK)]
                    ds = _f32(p) * (dp[rows] - delta[kvh, odd][rows])
                    dbias[h] += ds
                    p_parts.append(p)
                    ds_parts.append(_bf(ds))
                pst[kvh, odd] = jnp.concatenate(p_parts, axis=0)
                dsst[kvh, odd] = jnp.concatenate(ds_parts, axis=0)
            dk_parts, dv_parts = [], []
            for kvh in range(2):
                dq_st = _dot(dsst[kvh, 0], ks[kvh][0]) + _dot(dsst[kvh, 1], ks[kvh][1])
                for jj in range(4):
                    dq_ref[:, pair_cols[kvh][jj]] = _bf(dq_st[jj * SWA_BLOCK:(jj + 1) * SWA_BLOCK] * scale)
                zk = jnp.where(lane < 64, _dot_tn(dsst[kvh, 0], qst[kvh]), _dot_tn(dsst[kvh, 1], qst[kvh]))
                zv = jnp.where(lane < 64, _dot_tn(pst[kvh, 0], dost[kvh]), _dot_tn(pst[kvh, 1], dost[kvh]))
                dk_parts.append(zk + pltpu.roll(zk, 64, 1))
                dv_parts.append(zv + pltpu.roll(zv, 64, 1))
            dk = jnp.where(lane < 64, dk_parts[0], dk_parts[1])
            dv = jnp.where(lane < 64, dv_parts[0], dv_parts[1])
            dkv = _swa_zero_key0(jnp.concatenate([dk, dv], axis=1))
            dkv_ref[...] = _bf(carry[...] + dkv[0:SWA_BLOCK])
            carry[...] = dkv[SWA_BLOCK:]

        @pl.when(n == nb)
        def _():
            dkv_ref[...] = _bf(carry[...])
            first_col = lax.broadcasted_iota(jnp.int32, (SWA_BLOCK, 2 * SWA_BLOCK), 1) == 0
            bk = jnp.where(first_col, -1, bucket_ref[...])

            row = lax.broadcasted_iota(jnp.int32, (NUM_BUCKETS, 128), 0)
            lane = lax.broadcasted_iota(jnp.int32, (NUM_BUCKETS, 128), 1)

            def total(v):
                return jnp.sum(jnp.sum(v, axis=1, keepdims=True), axis=0, keepdims=True)

            def per_head(h, acc):
                db = dbias[h]
                d_rb, d_sk = acc
                d_sk = d_sk + jnp.where((row == 0) & (lane == h), total(jnp.where(first_col, db, 0.0)), 0.0)

                def per_bucket(b, d_rb):
                    return d_rb + jnp.where((row == b) & (lane == h), total(jnp.where(bk == b, db, 0.0)), 0.0)

                return lax.fori_loop(0, NUM_BUCKETS, per_bucket, d_rb), d_sk

            zero = jnp.zeros((NUM_BUCKETS, 128), F32)
            d_rb, d_sk = lax.fori_loop(0, SWA_HEADS, per_head, (zero, zero))
            drb_ref[...] = d_rb
            dsk_ref[...] = d_sk[0:8]

    cur = lambda n: jnp.minimum(n, nb - 1)
    prev = lambda n: jnp.maximum(jnp.minimum(n, nb - 1) - 1, 0)
    return pl.pallas_call(
        body,
        grid=(nb + 1,),
        in_specs=[
            pl.BlockSpec((SWA_BLOCK, 1024), lambda n: (cur(n), Z_SQ // 1024)),
            pl.BlockSpec((SWA_BLOCK, 256), lambda n: (cur(n), Z_SK // 256)),
            pl.BlockSpec((SWA_BLOCK, 256), lambda n: (prev(n), Z_SK // 256)),
            pl.BlockSpec((SWA_BLOCK, 1024), lambda n: (cur(n), 0)),
            pl.BlockSpec((SWA_BLOCK, SWA_HEADS * 2 * SWA_BLOCK), lambda n: (cur(n), 0)),
            pl.BlockSpec((SWA_BLOCK, 1024), lambda n: (cur(n), 0)),
            _const_spec((SWA_BLOCK, 2 * SWA_BLOCK)), _dep_spec(),
        ],
        out_specs=[
            pl.BlockSpec((SWA_BLOCK, 1024), lambda n: (cur(n), 0)),
            pl.BlockSpec((SWA_BLOCK, 256), lambda n: (jnp.maximum(n - 1, 0), 0)),
            pl.BlockSpec((NUM_BUCKETS, 128), lambda n: (0, 0)),
            pl.BlockSpec((8, 128), lambda n: (0, 0)),
        ],
        out_shape=[
            jax.ShapeDtypeStruct((S, 1024), BF16),
            jax.ShapeDtypeStruct((S, 256), BF16),
            jax.ShapeDtypeStruct((NUM_BUCKETS, 128), F32),
            jax.ShapeDtypeStruct((8, 128), F32),
        ],
        scratch_shapes=[
            pltpu.VMEM((SWA_HEADS, SWA_BLOCK, 2 * SWA_BLOCK), F32),
            pltpu.VMEM((SWA_BLOCK, 256), F32),
        ],
        compiler_params=_cparams("arbitrary"),
        name="swa_bwd",
    )(zmain, zmain, zmain, o_b, probs, d_o, bucket, dep)


def _mem_q_specs(T):
    return [pl.BlockSpec((T, MEM_HEAD_DIM), lambda t, h=h: (t, Z_MQ // MEM_HEAD_DIM + h)) for h in range(MEM_HEADS)]


def _mem_kv_proj(mem, g2):
    def body(mem_ref, w_ref, o_ref):
        o_ref[...] = _dot_nt(_bf(mem_ref[...]), _rows(w_ref))

    return pl.pallas_call(
        body,
        grid=(1,),
        in_specs=[pl.BlockSpec((MEM_LEN, D_MODEL), lambda i: (0, 0)), _gathered_spec(R_KV, R_OTHER)],
        out_specs=pl.BlockSpec((MEM_LEN, 2048), lambda i: (0, 0)),
        out_shape=jax.ShapeDtypeStruct((MEM_LEN, 2048), F32),
        compiler_params=_cparams("arbitrary"),
        name="mem_kv_proj",
    )(mem, g2)


def _mem_fwd(zmain, mkv, *, T):
    S = zmain.shape[0]

    def body(q0, q1, q2, q3, kv_ref, o_ref, p_ref):
        heads = [pl.ds(h * MEM_HEAD_DIM, MEM_HEAD_DIM) for h in range(MEM_HEADS)]
        scores = [_dot_nt(_bf(q_ref[...] * (MEM_HEAD_DIM ** -0.5)), _bf(kv_ref[:, cols]))
                  for q_ref, cols in zip((q0, q1, q2, q3), heads)]
        probs = []
        for s, cols in zip(scores, heads):
            e = jnp.exp(s - jnp.max(s, axis=-1, keepdims=True))
            pb = _bf(e * (1.0 / jnp.sum(e, axis=-1, keepdims=True)))
            p_ref[:, cols] = pb
            probs.append(pb)
        for h, (pb, cols) in enumerate(zip(probs, heads)):
            o_ref[:, cols] = _dot(pb, _bf(kv_ref[:, pl.ds(1024 + h * MEM_HEAD_DIM, MEM_HEAD_DIM)]))

    row = pl.BlockSpec((T, 1024), lambda t: (t, 0))
    return pl.pallas_call(
        body,
        grid=(S // T,),
        in_specs=_mem_q_specs(T) + [_const_spec((MEM_LEN, 2048))],
        out_specs=[row, row],
        out_shape=[jax.ShapeDtypeStruct((S, 1024), F32), jax.ShapeDtypeStruct((S, 1024), BF16)],
        compiler_params=_cparams("parallel"),
        name="mem_fwd",
    )(zmain, zmain, zmain, zmain, mkv)


def _mem_bwd(zmain, mkv, o_c, probs, d_o, *, T):
    S = zmain.shape[0]
    scale = MEM_HEAD_DIM ** -0.5

    def body(q0, q1, q2, q3, kv_ref, o_ref, p_ref, do_ref, dq_ref, dkv_ref):
        @pl.when(pl.program_id(0) == 0)
        def _():
            dkv_ref[...] = jnp.zeros_like(dkv_ref)

        heads = [(pl.ds(h * MEM_HEAD_DIM, MEM_HEAD_DIM), pl.ds(1024 + h * MEM_HEAD_DIM, MEM_HEAD_DIM))
                 for h in range(MEM_HEADS)]
        d_p = [_dot_nt(do_ref[:, cols], _bf(kv_ref[:, vcols])) for cols, vcols in heads]
        d_s = []
        for dp, (cols, _) in zip(d_p, heads):
            delta = jnp.sum(do_ref[:, cols].astype(F32) * o_ref[:, cols], axis=-1, keepdims=True)
            d_s.append(_bf(_f32(p_ref[:, cols]) * (dp - delta)))
        for ds, q_ref, (cols, vcols) in zip(d_s, (q0, q1, q2, q3), heads):
            dq_ref[:, cols] = _bf(_dot(ds, _bf(kv_ref[:, cols])) * scale)
            dkv_ref[:, cols] += _dot_tn(ds, _bf(q_ref[...] * scale))
            dkv_ref[:, vcols] += _dot_tn(p_ref[:, cols], do_ref[:, cols])

    row = pl.BlockSpec((T, 1024), lambda t: (t, 0))
    return pl.pallas_call(
        body,
        grid=(S // T,),
        in_specs=_mem_q_specs(T) + [_const_spec((MEM_LEN, 2048)), row, row, row],
        out_specs=[row, pl.BlockSpec((MEM_LEN, 2048), lambda t: (0, 0))],
        out_shape=[jax.ShapeDtypeStruct((S, 1024), BF16), jax.ShapeDtypeStruct((MEM_LEN, 2048), F32)],
        compiler_params=_cparams("arbitrary"),
        name="mem_bwd",
    )(zmain, zmain, zmain, zmain, mkv, o_c, probs, d_o)


def _layer_norm(u):
    mu = jnp.mean(u, axis=-1, keepdims=True)
    xc = u - mu
    rstd = lax.rsqrt(jnp.mean(xc * xc, axis=-1, keepdims=True) + LN_EPS)
    return xc * rstd, rstd


def _layer_norm_bwd(dy, gamma, xhat, rstd):
    dxh = dy * gamma
    return rstd * (dxh - jnp.mean(dxh, axis=-1, keepdims=True) - xhat * jnp.mean(dxh * xhat, axis=-1, keepdims=True))


def _merge_stages(rows, oraw_ref, hg_ref, ob_ref, oc_ref, gl_ref, x_ref, gain_ref, wbh, wbs, wbm, wout, g_ref, b_ref,
                  fwd_out=None, bwd=None, saved=None):
    ys, rs = [], []
    for h in range(HG_HEADS):
        oh = oraw_ref[rows, pl.ds(h * HG_DK, HG_DK)]
        r = lax.rsqrt(jnp.mean(oh * oh, axis=-1, keepdims=True) + RMS_EPS)
        ys.append(oh * r)
        rs.append(r)
    y = jnp.concatenate(ys, axis=1)
    hg = _f32(hg_ref[rows, :])
    sg = _sig(hg)
    silu = hg * sg
    gain = gain_ref[...]
    gates = [_sig(_f32(gl_ref[rows, pl.ds(i * 1024, 1024)])) for i in range(3)]
    if saved is None:
        oa = _bf(y * gain * silu)
        pa = _dot(oa, _rows(wbh))
        pb = _dot(_bf(ob_ref[rows, :]), _rows(wbs))
        pc = _dot(_bf(oc_ref[rows, :]), _rows(wbm))
        yield
        m = _bf(gates[0] * pa + gates[1] * pb + gates[2] * pc)
    else:
        pa, pb, pc = (_f32(r[rows, :]) for r in saved[:3])
        m = saved[3][rows, :]
    mix = _dot(m, _rows(wout))
    yield
    xhat, rstd = _layer_norm(ALPHA * x_ref[rows, :] + mix)
    if bwd is None:
        h1 = xhat * g_ref[...] + b_ref[...]
        fwd_out[0][rows, :] = h1
        fwd_out[1][rows, :] = _bf(h1)
        for ref, val in zip(fwd_out[2:], (_bf(pa), _bf(pb), _bf(pc), m, oa)):
            ref[rows, :] = val
        return
    (dh1_ref, dx_ref, du1_ref, dpa_ref, dpb_ref, dpc_ref, doraw_ref, dob_ref, doc_ref, dz_ref,
     dgain_ref, dg_ref, db_ref) = bwd
    dh1 = dh1_ref[rows, :]
    dg_ref[...] += jnp.sum(dh1 * xhat, axis=0, keepdims=True)
    db_ref[...] += jnp.sum(dh1, axis=0, keepdims=True)
    du1 = _layer_norm_bwd(dh1, g_ref[...], xhat, rstd)
    dx_ref[rows, :] = ALPHA * du1
    du1b = _bf(du1)
    du1_ref[rows, :] = du1b
    dm = _dot_nt(du1b, _rows(wout))
    yield
    d_branches = []
    for i, (g, p, dp_ref, w_r) in enumerate(zip(gates, (pa, pb, pc), (dpa_ref, dpb_ref, dpc_ref), (wbh, wbs, wbm))):
        dz_ref[rows, pl.ds((i + 1) * 1024, 1024)] = _bf(dm * p * g * (1.0 - g))
        dp = _bf(dm * g)
        dp_ref[rows, :] = dp
        d_branches.append(_dot_nt(dp, _rows(w_r)))
    yield
    doa, d_ob, d_oc = d_branches
    dob_ref[rows, :] = _bf(d_ob)
    doc_ref[rows, :] = _bf(d_oc)
    t = doa * y
    dgain_ref[...] += jnp.sum(t * silu, axis=0, keepdims=True)
    dz_ref[rows, 0:1024] = _bf(t * gain * sg * (1.0 + hg * (1.0 - sg)))
    dy = doa * gain * silu
    for h in range(HG_HEADS):
        cols = slice(h * HG_DK, (h + 1) * HG_DK)
        yh = y[:, cols]
        dyh = dy[:, cols]
        doraw_ref[rows, pl.ds(h * HG_DK, HG_DK)] = _bf(rs[h] * (dyh - yh * jnp.mean(dyh * yh, axis=-1, keepdims=True)))


def _interleave(chains):
    live = list(chains)
    while live:
        still = []
        for c in live:
            try:
                next(c)
                still.append(c)
            except StopIteration:
                pass
        live = still


def _gathered_spec(lo, hi):
    n = hi - lo
    return pl.BlockSpec((N_DEV, n, D_MODEL), lambda *_: (0, lo // n, 0), pipeline_mode=pl.Buffered(1))


def _rows(w_ref):
    return w_ref[...].reshape(-1, D_MODEL)


def _merge_in_specs(T):
    row = lambda w, c=0: pl.BlockSpec((T, w), lambda i: (i, c))
    vec = pl.BlockSpec((1, D_MODEL), lambda i: (0, 0))
    w = [_gathered_spec(lo, hi) for lo, hi in ((R_BH, R_BS), (R_BS, R_BM), (R_BM, R_OUT), (R_OUT, R_KV))]
    return [row(1024), row(1024, Z_HG // 1024), row(1024), row(1024), row(3072), row(1024), vec, *w, vec, vec]


def _merge_fwd(o_raw, zmain, o_b, o_c, gl, x, gain, wbh, wbs, wbm, wout, ln_g, ln_b, *, T):
    S = x.shape[0]

    def body(*refs):
        ins, outs = refs[:13], refs[13:]
        _interleave(_merge_stages(pl.ds(r0, T // MERGE_GROUPS), *ins, fwd_out=outs)
                    for r0 in range(0, T, T // MERGE_GROUPS))

    row = pl.BlockSpec((T, D_MODEL), lambda i: (i, 0))
    return pl.pallas_call(
        body,
        grid=(S // T,),
        in_specs=_merge_in_specs(T),
        out_specs=[row] * 7,
        out_shape=[jax.ShapeDtypeStruct((S, D_MODEL), F32)] + [jax.ShapeDtypeStruct((S, D_MODEL), BF16)] * 6,
        compiler_params=_cparams("parallel"),
        name="merge_fwd",
    )(o_raw, zmain, o_b, o_c, gl, x, gain, wbh, wbs, wbm, wout, ln_g, ln_b)


def _merge_bwd(d_h1, pa, pb, pc, m, o_raw, zmain, gl, x, gain, wbh, wbs, wbm, wout, ln_g, *, T):
    S = x.shape[0]

    def body(dh1_ref, pa_ref, pb_ref, pc_ref, m_ref, oraw_ref, hg_ref, gl_ref, x_ref, gain_ref, wbh_r, wbs_r, wbm_r, wout_r,
             g_ref, dx_ref, du1_ref, dpa_ref, dpb_ref, dpc_ref, doraw_ref, dob_ref, doc_ref, dz_ref,
             dgain_ref, dg_ref, db_ref):
        @pl.when(pl.program_id(0) == 0)
        def _():
            dgain_ref[...] = jnp.zeros_like(dgain_ref)
            dg_ref[...] = jnp.zeros_like(dg_ref)
            db_ref[...] = jnp.zeros_like(db_ref)

        ins = (oraw_ref, hg_ref, None, None, gl_ref, x_ref, gain_ref, wbh_r, wbs_r, wbm_r, wout_r, g_ref, None)
        bwd = (dh1_ref, dx_ref, du1_ref, dpa_ref, dpb_ref, dpc_ref, doraw_ref, dob_ref, doc_ref, dz_ref,
               dgain_ref, dg_ref, db_ref)
        _interleave([_merge_stages(pl.ds(0, T), *ins, bwd=bwd, saved=(pa_ref, pb_ref, pc_ref, m_ref))])

    row = lambda w, c=0: pl.BlockSpec((T, w), lambda i: (i, c))
    vec = pl.BlockSpec((1, D_MODEL), lambda i: (0, 0))
    w = [_gathered_spec(lo, hi) for lo, hi in ((R_BH, R_BS), (R_BS, R_BM), (R_BM, R_OUT), (R_OUT, R_KV))]
    bshape = jax.ShapeDtypeStruct((S, D_MODEL), BF16)
    vshape = jax.ShapeDtypeStruct((1, D_MODEL), F32)
    return pl.pallas_call(
        body,
        grid=(S // T,),
        in_specs=[row(1024)] * 6 + [row(1024, Z_HG // 1024), row(3072), row(1024), vec, *w, vec],
        out_specs=[row(1024)] * 8 + [row(4096), vec, vec, vec],
        out_shape=[jax.ShapeDtypeStruct((S, D_MODEL), F32)] + [bshape] * 7
        + [jax.ShapeDtypeStruct((S, 4096), BF16), vshape, vshape, vshape],
        compiler_params=_cparams("arbitrary"),
        name="merge_bwd",
    )(d_h1, pa, pb, pc, m, o_raw, zmain, gl, x, gain, wbh, wbs, wbm, wout, ln_g)


def _mlp_fwd_bwd(h1, target, wup_t, wdn, ln_g, ln_b, *, T, FC):
    S = h1.shape[0]
    nf = D_FF // FC
    assert FC == R_BH - R_UP == R_UP - R_DN

    def body(h1_ref, t_ref, wup_ref, wdn_ref, g_ref, b_ref, dh1_ref, a_ref, dup_ref, du2_ref, loss_ref, dg_ref, db_ref, up_scr):
        @pl.when(pl.program_id(0) == 0)
        def _():
            loss_ref[...] = jnp.zeros_like(loss_ref)
            dg_ref[...] = jnp.zeros_like(dg_ref)
            db_ref[...] = jnp.zeros_like(db_ref)

        h1v = h1_ref[...]
        h1b = _bf(h1v)
        ff = jnp.zeros((T, D_MODEL), F32)
        for j in range(nf):
            rows = pl.ds(j * FC, FC)
            up = jnp.maximum(_dot_nt(h1b, wup_ref[j]), 0.0)
            up_scr[:, rows] = _bf(up)
            a = _bf(up * up)
            a_ref[:, rows] = a
            ff = ff + _dot(a, wdn_ref[j])
        xhat, rstd = _layer_norm(ALPHA * h1v + ff)
        gamma = g_ref[...]
        err = xhat * gamma + b_ref[...] - t_ref[...]
        loss_ref[...] += jnp.sum(jnp.sum(err * err, axis=-1, keepdims=True), axis=0, keepdims=True) * (0.5 / D_MODEL)
        dy = err * (1.0 / D_MODEL)
        dg_ref[...] += jnp.sum(dy * xhat, axis=0, keepdims=True)
        db_ref[...] += jnp.sum(dy, axis=0, keepdims=True)
        du2 = _layer_norm_bwd(dy, gamma, xhat, rstd)
        du2b = _bf(du2)
        du2_ref[...] = du2b
        dh1 = ALPHA * du2
        for j in range(nf):
            rows = pl.ds(j * FC, FC)
            dup = _bf(_dot_nt(du2b, wdn_ref[j]) * (2.0 * up_scr[:, rows].astype(F32)))
            dup_ref[:, rows] = dup
            dh1 = dh1 + _dot(dup, wup_ref[j])
        dh1_ref[...] = dh1

    row = lambda w: pl.BlockSpec((T, w), lambda i: (i, 0))
    vec = pl.BlockSpec((1, D_MODEL), lambda i: (0, 0))
    vshape = jax.ShapeDtypeStruct((1, D_MODEL), F32)
    return pl.pallas_call(
        body,
        grid=(S // T,),
        in_specs=[row(1024), row(1024), _gathered_spec(R_UP, R_BH), _gathered_spec(R_DN, R_UP), vec, vec],
        out_specs=[row(1024), row(D_FF), row(D_FF), row(1024), pl.BlockSpec((8, 128), lambda i: (0, 0)), vec, vec],
        out_shape=[
            jax.ShapeDtypeStruct((S, D_MODEL), F32),
            jax.ShapeDtypeStruct((S, D_FF), BF16),
            jax.ShapeDtypeStruct((S, D_FF), BF16),
            jax.ShapeDtypeStruct((S, D_MODEL), BF16),
            jax.ShapeDtypeStruct((8, 128), F32), vshape, vshape,
        ],
        scratch_shapes=[pltpu.VMEM((T, D_FF), BF16)],
        compiler_params=_cparams("arbitrary"),
        name="mlp_fwd_bwd",
    )(h1, target, wup_t, wdn, ln_g, ln_b)


def _local_step(x, mem, target, lb_logits, gain, sinks, rel_bias, ln1_g, ln1_b, ln2_g, ln2_b,
                win_t, dep0, other_weights, send_other_grads, send_small_grads, send_win_grad):
    S = x.shape[0]
    T = min(256, S)
    KC = min(2048, S)
    z_qfv, zmain, gl, xb = _in_proj(x, win_t, dep0, tm=min(512, S))
    bucket = _t5_bucket_table()

    o_raw, states = _hgrn_fwd(z_qfv, lb_logits, T=min(2048, S))
    o_b, swa_probs = _swa_fwd(zmain, bucket, rel_bias, sinks)
    g2 = other_weights((o_b, o_raw))
    mkv = _mem_kv_proj(mem, g2)
    o_c, mem_probs = _mem_fwd(zmain, mkv, T=min(1024, S))
    h1, h1b, pa, pb, pc, m, oa = _merge_fwd(o_raw, zmain, o_b, o_c, gl, x, gain, g2, g2, g2, g2, ln1_g, ln1_b,
                                                T=min(512, S))

    d_h1, act, d_up, du2, loss, d_ln2_g, d_ln2_b = _mlp_fwd_bwd(h1, target, g2, g2, ln2_g, ln2_b, T=min(512, S), FC=512)
    wgrad = functools.partial(_mm_tn, out_dtype=BF16)
    halves = lambda r0: (lambda i: (i // 2, r0 // 256 + i % 2, 0))
    whole = lambda r0: (lambda i: (0, r0 // 128, 0))
    og = lax.empty((N_DEV, R_OTHER, D_MODEL), BF16)
    og = wgrad(act, du2, kc=KC, name="grad_w_down", into=(og, (1, 256, D_MODEL), halves(R_DN)))
    og = wgrad(d_up, h1b, kc=KC, name="grad_w_up", into=(og, (1, 256, D_MODEL), halves(R_UP)))

    (dx_part, du1, dpa, dpb, dpc, d_oraw, d_ob, d_oc, d_hg_gl, d_gain, d_ln1_g, d_ln1_b) = _merge_bwd(
        d_h1, pa, pb, pc, m, o_raw, zmain, gl, x, gain, g2, g2, g2, g2, ln1_g, T=T)
    for a_op, b_op, r0, nm in ((m, du1, R_OUT, "out"), (oa, dpa, R_BH, "branch_hg"), (o_b, dpb, R_BS, "branch_swa"),
                               (o_c, dpc, R_BM, "branch_mem")):
        og = wgrad(a_op, b_op, kc=KC, name="grad_w_" + nm, into=(og, (N_DEV, 128, D_MODEL), whole(r0)))

    d_mq, d_mkv = _mem_bwd(zmain, mkv, o_c, mem_probs, d_oc, T=min(1024, S))
    og = wgrad(d_mkv, mem, kc=MEM_LEN, name="grad_w_mem_kv",
               into=(og, (1, 256, D_MODEL), lambda i: (i, R_KV // 256, 0)))
    sent_others = send_other_grads(og)
    d_sq, d_skv, d_rb, d_sink = _swa_bwd(zmain, o_b, swa_probs, d_ob, bucket, sent_others)
    d_qfv, d_lb = _hgrn_bwd(z_qfv, lb_logits, states, d_oraw, T=min(2048, S))
    sent_small = send_small_grads(_pack_small_grads(d_lb, d_gain, d_sink, d_rb, d_ln1_g, d_ln1_b, d_ln2_g, d_ln2_b, loss))

    head_major = lambda a: a.reshape(3, HG_HEADS, HG_DK, D_MODEL).transpose(1, 0, 2, 3).reshape(3 * D_MODEL, D_MODEL)
    pieces = (d_qfv, d_hg_gl, d_sq, d_skv, d_mq)
    placed = (
        ("qfv", d_qfv, 128, lambda i: ((i % 3) * HG_HEADS + i // 3, 0)),
        ("hg_gates", d_hg_gl, 256, lambda i: (jnp.where(i < 4, C_HG // 256 + i, C_GL // 256 + i - 4), 0)),
        ("swa_q", d_sq, None, lambda i: (C_SQ // 1024, 0)),
        ("swa_kv", d_skv, None, lambda i: (C_SK // 256, 0)),
        ("mem_q", d_mq, 256, lambda i: (C_MQ // 256 + i, 0)),
    )
    g_win_t = lax.empty((IN_COLS, D_MODEL), BF16)
    for nm, piece, tile, index in placed:
        g_win_t = wgrad(piece, xb, kc=KC, name="grad_w_in_" + nm, tm=tile,
                        into=(g_win_t, (tile or piece.shape[1], D_MODEL), index))
    sent_win = send_win_grad(g_win_t, sent_small)
    return _grad_x(*pieces, head_major(win_t[:C_HG]), win_t, dx_part, sent_win, tm=T)


MESH = pl.DeviceIdType.MESH
ANY = pl.BlockSpec(memory_space=pl.ANY)


def _coords():
    return lax.axis_index("x"), lax.axis_index("y"), lax.axis_index("c")


def _other_chips(x, y):
    return [(1 - x, y), (x, 1 - y), (1 - x, 1 - y)]


def _all_gather_weights(*arrays):
    na = len(arrays)

    def body(*refs):
        srcs, dsts = refs[:na], refs[na:2 * na]
        send_sems, recv_sems, local_sems = refs[2 * na:]
        x, y, c = _coords()
        me, sibling = (x, y, c), (x, y, 1 - c)
        chips = _other_chips(x, y)

        def slot(a, px, py, pc):
            return dsts[a].at[4 * px + 2 * py + pc]

        def copy(a, k, block, to, from_shard=False):
            return pltpu.make_async_remote_copy(
                src_ref=srcs[a] if from_shard else slot(a, *block), dst_ref=slot(a, *block),
                send_sem=send_sems.at[a * 7 + k], recv_sem=recv_sems.at[a * 7 + k],
                device_id=to, device_id_type=MESH)

        own = [pltpu.make_async_copy(srcs[a], slot(a, *me), local_sems.at[a]) for a in range(na)]
        for cp in own:
            cp.start()
        first = []
        for a in range(na):
            first.append(copy(a, 0, me, sibling, True))
            first += [copy(a, 1 + j, me, (*chip, c), True) for j, chip in enumerate(chips)]
        for cp in first:
            cp.start()
        passed = []
        for j, chip in enumerate(chips):
            for a in range(na):
                copy(a, 1 + j, (*chip, c), me).wait_recv()
                fwd = copy(a, 4 + j, (*chip, c), sibling)
                fwd.start()
                passed.append(fwd)
        for a in range(na):
            copy(a, 0, sibling, me).wait_recv()
            for j, chip in enumerate(chips):
                copy(a, 4 + j, (*chip, 1 - c), me).wait_recv()
        for cp in first + passed:
            cp.wait_send()
        for cp in own:
            cp.wait()

    return pl.pallas_call(
        body,
        in_specs=[ANY] * na,
        out_specs=[ANY] * na,
        out_shape=[jax.ShapeDtypeStruct((N_DEV,) + a.shape, a.dtype) for a in arrays],
        scratch_shapes=[pltpu.SemaphoreType.DMA((7 * na,)), pltpu.SemaphoreType.DMA((7 * na,)),
                        pltpu.SemaphoreType.DMA((na,))],
        name="all_gather_weights",
    )(*arrays)


HBM = pl.BlockSpec(memory_space=pltpu.HBM)
SEM = pl.BlockSpec(memory_space=pltpu.SEMAPHORE)
_DATAFLOW = pltpu.SideEffectType.DATAFLOW_SIDE_EFFECTING


def _peer(x, y, c, r):
    return x ^ (r >> 2), y ^ ((r >> 1) & 1), c ^ (r & 1)


def _direct_copies(src_ref, land_ref, send_sems, recv_sems, gather, receiving):
    x, y, c = _coords()
    me = 4 * x + 2 * y + c
    copies = []
    for r in range(1, N_DEV):
        px, py, pc = _peer(x, y, c, r)
        peer = 4 * px + 2 * py + pc
        if gather:
            src, dst = src_ref, land_ref.at[peer if receiving else me]
        else:
            src, dst = src_ref.at[peer], land_ref.at[r - 1]
        copies.append(pltpu.make_async_remote_copy(
            src_ref=src, dst_ref=dst, send_sem=send_sems.at[r - 1], recv_sem=recv_sems.at[r - 1],
            device_id=(px, py, pc), device_id_type=MESH))
    return copies


def _direct_start(src, land, *, gather, name, after=None):
    def body(src_ref, land_ref, *rest):
        send_sems, recv_sems, token = rest[-5], rest[-4], rest[-1]
        for cp in _direct_copies(src_ref, land_ref, send_sems, recv_sems, gather, False):
            cp.start()
        token[...] = jnp.zeros_like(token)

    afters = () if after is None else (after,)
    return pl.pallas_call(
        body,
        name=name,
        out_shape=(pltpu.SemaphoreType.DMA((N_DEV - 1,)), pltpu.SemaphoreType.DMA((N_DEV - 1,)),
                   pltpu.HBM(src.shape, src.dtype), pltpu.HBM(land.shape, land.dtype),
                   jax.ShapeDtypeStruct((8, 128), F32)),
        in_specs=(HBM, HBM) + tuple(ANY for _ in afters),
        out_specs=(SEM, SEM, HBM, HBM, pl.BlockSpec(memory_space=pltpu.VMEM)),
        input_output_aliases={0: 2, 1: 3},
        compiler_params=pltpu.CompilerParams(has_side_effects=_DATAFLOW),
    )(pltpu.with_memory_space_constraint(src, pltpu.HBM), pltpu.with_memory_space_constraint(land, pltpu.HBM), *afters)


def _direct_wait(send_sems, recv_sems, src_thru, land_thru, after, *, gather, name):
    afters = after if isinstance(after, tuple) else (after,)

    def body(src_ref, land_ref, send_sems_ref, recv_sems_ref, *rest):
        del rest
        for cp in _direct_copies(src_ref, land_ref, send_sems_ref, recv_sems_ref, gather, True):
            cp.wait_send()
            cp.wait_recv()

    return pl.pallas_call(
        body,
        name=name,
        out_shape=(pltpu.HBM(src_thru.shape, src_thru.dtype), pltpu.HBM(land_thru.shape, land_thru.dtype)),
        in_specs=(HBM, HBM, SEM, SEM) + tuple(ANY for _ in afters),
        out_specs=(HBM, HBM),
        input_output_aliases={0: 0, 1: 1},
        compiler_params=pltpu.CompilerParams(has_side_effects=_DATAFLOW),
    )(src_thru, land_thru, send_sems, recv_sems, *afters)


def _sum_partials(src, land, me, *, tr, name, wmv=None):
    R = src.shape[1]
    extra = () if wmv is None else tuple(wmv)

    def body(me_ref, s_ref, l_ref, *rest):
        del me_ref
        acc = s_ref[0].astype(F32)
        for r in range(N_DEV - 1):
            acc = acc + l_ref[r].astype(F32)
        rest[len(extra)][...] = acc
        if extra:
            w_ref, m_ref, v_ref, _, d_ref, nm_ref, nv_ref = rest
            d_ref[...], nm_ref[...], nv_ref[...] = _adam_step(w_ref[...], acc, m_ref[...], v_ref[...])

    row = pl.BlockSpec((tr, 1024), lambda i, mr: (i, 0))
    n_out = 4 if extra else 1
    res = pl.pallas_call(
        body,
        grid_spec=pltpu.PrefetchScalarGridSpec(
            num_scalar_prefetch=1, grid=(R // tr,),
            in_specs=[pl.BlockSpec((1, tr, 1024), lambda i, mr: (mr[0], i, 0)),
                      pl.BlockSpec((N_DEV - 1, tr, 1024), lambda i, mr: (0, i, 0))] + [row for _ in extra],
            out_specs=[row] * n_out),
        out_shape=[jax.ShapeDtypeStruct((R, 1024), F32)] * n_out,
        name=name,
    )(me, src, land, *extra)
    return res if extra else res[0]


_SMALL = ("lb_logits", "hg_norm_gain", "swa_sinks", "rel_bias", "ln1_g", "ln1_b", "ln2_g", "ln2_b")


def _pack_small_grads(d_lb, d_gain, d_sink, d_rb, d_ln1_g, d_ln1_b, d_ln2_g, d_ln2_b, loss):
    def body(lb_ref, gain_ref, sink_ref, rb_ref, l1g_ref, l1b_ref, l2g_ref, l2b_ref, loss_ref, o_ref):
        o_ref[...] = jnp.zeros_like(o_ref)
        for row, ref in ((SM_LB, lb_ref), (SM_GAIN, gain_ref), (SM_L1G, l1g_ref), (SM_L1B, l1b_ref),
                         (SM_L2G, l2g_ref), (SM_L2B, l2b_ref)):
            o_ref[row:row + 1, :] = ref[...]
        o_ref[SM_SINK:SM_SINK + 1, 0:128] = sink_ref[0:1, :]
        o_ref[SM_LOSS:SM_LOSS + 1, 0:128] = loss_ref[0:1, :]
        o_ref[SM_RB:SM_RB + NUM_BUCKETS, 0:128] = rb_ref[...]

    vm = pl.BlockSpec(memory_space=pltpu.VMEM)
    return pl.pallas_call(
        body,
        in_specs=[vm] * 9,
        out_specs=vm,
        out_shape=jax.ShapeDtypeStruct((SM_ROWS, D_MODEL), F32),
        name="pack_small_grads",
    )(d_lb, d_gain, d_sink, d_rb, d_ln1_g, d_ln1_b, d_ln2_g, d_ln2_b, loss)


def _small_finish(gathered, w, m, v):
    n = len(_SMALL)

    def body(*refs):
        g_ref = refs[0]
        w_refs, m_refs, v_refs = refs[1:1 + n], refs[1 + n:1 + 2 * n], refs[1 + 2 * n:1 + 3 * n]
        outs = refs[1 + 3 * n:]
        loss_ref, tot = outs[0], outs[-1]
        g_out, d_out, m_out, v_out = (outs[1 + k * n:1 + (k + 1) * n] for k in range(4))
        acc = g_ref[0]
        for d in range(1, N_DEV):
            acc = acc + g_ref[d]
        tot[...] = acc
        loss_ref[...] = tot[SM_LOSS:SM_LOSS + 1, 0:1]
        lb = _lower_bound(w_refs[0])
        dl0 = tot[SM_LB:SM_LB + 1, :] * lb * (1.0 - lb)
        grads = (jnp.concatenate([dl0, -dl0], axis=0), tot[SM_GAIN:SM_GAIN + 1, :],
                 tot[SM_SINK:SM_SINK + 1, 0:SWA_HEADS], tot[SM_RB:SM_RB + NUM_BUCKETS, 0:SWA_HEADS],
                 tot[SM_L1G:SM_L1G + 1, :], tot[SM_L1B:SM_L1B + 1, :], tot[SM_L2G:SM_L2G + 1, :], tot[SM_L2B:SM_L2B + 1, :])
        for k, g in enumerate(grads):
            g_out[k][...] = g
            d_out[k][...], m_out[k][...], v_out[k][...] = _adam_step(w_refs[k][...], g, m_refs[k][...], v_refs[k][...])

    vm = pl.BlockSpec(memory_space=pltpu.VMEM)
    shapes = [jax.ShapeDtypeStruct(w[k].shape, F32) for k in _SMALL]
    res = pl.pallas_call(
        body,
        in_specs=[vm] * (1 + 3 * n),
        out_specs=[vm] * (1 + 4 * n),
        out_shape=[jax.ShapeDtypeStruct((1, 1), F32)] + shapes * 4,
        scratch_shapes=[pltpu.VMEM((SM_ROWS, D_MODEL), F32)],
        name="small_finish",
    )(gathered, *[w[k] for k in _SMALL], *[m[k] for k in _SMALL], *[v[k] for k in _SMALL])
    parts = [dict(zip(_SMALL, res[1 + k * n:1 + (k + 1) * n])) for k in range(4)]
    return (res[0], *parts)


def _adam_step(w, g, m, v):
    nm = ADAM_B1 * m + (1.0 - ADAM_B1) * g
    nv = ADAM_B2 * v + (1.0 - ADAM_B2) * jnp.square(g)
    m_hat = nm / (1.0 - ADAM_B1 ** ADAM_STEP)
    v_hat = nv / (1.0 - ADAM_B2 ** ADAM_STEP)
    return -ADAM_LR * (m_hat / (jnp.sqrt(v_hat) + ADAM_EPS) + ADAM_WD * w), nm, nv


def _adamw_group(ws, ms, vs, grads=None, packed=None, name="adamw_group"):
    n = len(ws)
    g_in = list(grads) if packed is None else [packed[0]]

    def body(*refs):
        g_refs = refs[:len(g_in)]
        w_refs, m_refs, v_refs = (refs[len(g_in) + k * n:len(g_in) + (k + 1) * n] for k in range(3))
        outs = refs[len(g_in) + 3 * n:]
        for k in range(n):
            if packed is None:
                g = g_refs[k][...]
            else:
                r0, rows = packed[1][k]
                g = g_refs[0][r0:r0 + rows, :]
            outs[k][...] = g
            outs[n + k][...], outs[2 * n + k][...], outs[3 * n + k][...] = _adam_step(
                w_refs[k][...], g, m_refs[k][...], v_refs[k][...])

    vm = pl.BlockSpec(memory_space=pltpu.VMEM)
    shapes = [jax.ShapeDtypeStruct(a.shape, F32) for a in ws]
    res = pl.pallas_call(
        body,
        in_specs=[vm] * (len(g_in) + 3 * n),
        out_specs=[vm] * (4 * n),
        out_shape=shapes * 4,
        compiler_params=pltpu.CompilerParams(vmem_limit_bytes=VMEM_LIMIT),
        name=name,
    )(*g_in, *ws, *ms, *vs)
    return [res[k * n:(k + 1) * n] for k in range(4)]


_WEIGHTS = ("w_in", "lb_logits", "hg_norm_gain", "swa_sinks", "rel_bias", "w_mem_kv", "w_branch_hg", "w_branch_swa",
            "w_branch_mem", "w_out", "ln1_g", "ln1_b", "w_up", "w_down", "ln2_g", "ln2_b")


def kernel(x, mem, w_in, lb_logits, hg_norm_gain, swa_sinks, rel_bias, w_mem_kv, w_branch_hg, w_branch_swa, w_branch_mem, w_out, ln1_g, ln1_b, w_up, w_down, ln2_g, ln2_b, loss_target, m_w_in, m_lb_logits, m_hg_norm_gain, m_swa_sinks, m_rel_bias, m_w_mem_kv, m_w_branch_hg, m_w_branch_swa, m_w_branch_mem, m_w_out, m_ln1_g, m_ln1_b, m_w_up, m_w_down, m_ln2_g, m_ln2_b, v_w_in, v_lb_logits, v_hg_norm_gain, v_swa_sinks, v_rel_bias, v_w_mem_kv, v_w_branch_hg, v_w_branch_swa, v_w_branch_mem, v_w_out, v_ln1_g, v_ln1_b, v_w_up, v_w_down, v_ln2_g, v_ln2_b):
    w = dict(w_in=w_in, lb_logits=lb_logits, hg_norm_gain=hg_norm_gain, swa_sinks=swa_sinks, rel_bias=rel_bias,
             w_mem_kv=w_mem_kv, w_branch_hg=w_branch_hg, w_branch_swa=w_branch_swa, w_branch_mem=w_branch_mem,
             w_out=w_out, ln1_g=ln1_g, ln1_b=ln1_b, w_up=w_up, w_down=w_down, ln2_g=ln2_g, ln2_b=ln2_b)
    mom = dict(w_in=m_w_in, lb_logits=m_lb_logits, hg_norm_gain=m_hg_norm_gain, swa_sinks=m_swa_sinks, rel_bias=m_rel_bias,
               w_mem_kv=m_w_mem_kv, w_branch_hg=m_w_branch_hg, w_branch_swa=m_w_branch_swa, w_branch_mem=m_w_branch_mem,
               w_out=m_w_out, ln1_g=m_ln1_g, ln1_b=m_ln1_b, w_up=m_w_up, w_down=m_w_down, ln2_g=m_ln2_g, ln2_b=m_ln2_b)
    var = dict(w_in=v_w_in, lb_logits=v_lb_logits, hg_norm_gain=v_hg_norm_gain, swa_sinks=v_swa_sinks, rel_bias=v_rel_bias,
               w_mem_kv=v_w_mem_kv, w_branch_hg=v_w_branch_hg, w_branch_swa=v_w_branch_swa, w_branch_mem=v_w_branch_mem,
               w_out=v_w_out, ln1_g=v_ln1_g, ln1_b=v_ln1_b, w_up=v_w_up, w_down=v_w_down, ln2_g=v_ln2_g, ln2_b=v_ln2_b)
    xc, yc, cc = _coords()

    p1 = _bf(w_in[0].T)
    p2 = _bf(jnp.concatenate([w_down[0], w_up[0].T, w_branch_hg[0], w_branch_swa[0], w_branch_mem[0], w_out[0],
                              w_mem_kv[0].T], axis=0))
    me = 4 * xc + 2 * yc + cc
    (g1,) = _all_gather_weights(p1)
    land2 = lax.dynamic_update_slice(lax.empty((N_DEV, R_OTHER, D_MODEL), BF16), p2[None], (me, 0, 0))
    ag2 = _direct_start(p2, land2, gather=True, name="gather_other_weights_start")

    def other_weights(after):
        return _direct_wait(*ag2[:4], after, gather=True, name="gather_other_weights_wait")[1]

    blocks = lambda a: a.reshape(N_DEV, a.shape[0] // N_DEV, D_MODEL)
    started = {}

    def send_other_grads(part):
        started["others"] = _direct_start(part, lax.empty((N_DEV - 1, R_OTHER, D_MODEL), BF16), gather=False,
                                          name="scatter_other_grads_start")
        return started["others"][4]

    me1 = me.reshape(1).astype(jnp.int32)
    grads, delta, new_m, new_v = {}, {}, {}, {}

    def send_small_grads(packed):
        land = lax.dynamic_update_slice(lax.empty((N_DEV, SM_ROWS, D_MODEL), F32), packed[None], (me, 0, 0))
        started["small"] = _direct_start(packed, land, gather=True, name="gather_small_grads_start")
        return started["small"][4]

    def send_win_grad(g, after):
        started["win"] = _direct_start(blocks(g), lax.empty((N_DEV - 1, IN_SHARD, D_MODEL), BF16), gather=False,
                                       name="scatter_w_in_grad_start", after=after)
        mine2, landed2 = _direct_wait(*started["others"][:4], started["win"][4], gather=False,
                                      name="scatter_other_grads_wait")
        gs2 = _sum_partials(mine2, landed2, me1, tr=R_OTHER // 2, name="sum_other_grads")
        rowwise = (("w_down", R_DN, R_UP), ("w_branch_hg", R_BH, R_BS), ("w_branch_swa", R_BS, R_BM),
                   ("w_branch_mem", R_BM, R_OUT), ("w_out", R_OUT, R_KV))
        colwise = (("w_up", R_UP, R_BH), ("w_mem_kv", R_KV, R_OTHER))
        for names, kw in (([n for n, _, _ in rowwise], dict(packed=(gs2, [(lo, hi - lo) for _, lo, hi in rowwise]))),
                          ([n for n, _, _ in colwise], dict(grads=[gs2[lo:hi].T for _, lo, hi in colwise]))):
            res = _adamw_group([w[n][0] for n in names], [mom[n][0] for n in names], [var[n][0] for n in names],
                               name="adamw_" + "_".join(n[2:] for n in names), **kw)
            for dst, vals in zip((grads, delta, new_m, new_v), res):
                dst.update(zip(names, vals))
        return (new_v["w_down"], new_v["w_up"])

    grad_x = _local_step(
        x[0], mem[0], loss_target[0], lb_logits, hg_norm_gain, swa_sinks, rel_bias, ln1_g, ln1_b, ln2_g, ln2_b,
        g1.reshape(IN_COLS, D_MODEL), ag2[4], other_weights, send_other_grads, send_small_grads, send_win_grad)

    mine1, landed1 = _direct_wait(*started["win"][:4], grad_x, gather=False, name="scatter_w_in_grad_wait")
    g_win_t, d_t, m_t, v_t = _sum_partials(mine1, landed1, me1, tr=IN_SHARD // 2, name="sum_adamw_w_in",
                                           wmv=(w_in[0].T, m_w_in[0].T, v_w_in[0].T))
    grads["w_in"], delta["w_in"], new_m["w_in"], new_v["w_in"] = g_win_t.T, d_t.T, m_t.T, v_t.T

    _, gathered = _direct_wait(*started["small"][:4], grad_x, gather=True, name="gather_small_grads_wait")
    loss, g_s, d_s, m_s, v_s = _small_finish(gathered, w, mom, var)
    for dst, src in ((grads, g_s), (delta, d_s), (new_m, m_s), (new_v, v_s)):
        dst.update(src)

    def shaped(d, name):
        return d[name].reshape(w[name].shape)

    return (loss.reshape(()), grad_x[None], *[shaped(grads, n) for n in _WEIGHTS], *[shaped(delta, n) for n in _WEIGHTS],
            *[shaped(new_m, n) for n in _WEIGHTS], *[shaped(new_v, n) for n in _WEIGHTS])
```

```python
import functools
import math

import jax
import jax.numpy as jnp
from jax import lax
from jax.experimental import pallas as pl
from jax.experimental.pallas import tpu as pltpu

F32 = jnp.float32
BF16 = jnp.bfloat16

D_MODEL = 1024
MEM_LEN = 256
HG_HEADS = 8
HG_DK = 128
HG_CHUNK = 64
SWA_HEADS = 16
SWA_HEAD_DIM = 64
SWA_BLOCK = 128
SWA_WINDOW = 128
MEM_HEADS = 4
MEM_HEAD_DIM = 256
NUM_BUCKETS = 32
MAX_DISTANCE = 128
D_FF = 4096
LN_EPS = 1e-5
RMS_EPS = 1e-6
ALPHA = 2.0 ** 0.25
N_DEV = 8

C_HQ, C_HF, C_HI, C_HG, C_SQ, C_SK, C_SV, C_MQ, C_GL = 0, 1024, 2048, 3072, 4096, 5120, 5248, 5376, 6400
IN_COLS = 9472
IN_SHARD = IN_COLS // N_DEV
Z_HG, Z_SQ, Z_SK, Z_MQ, Z_REST = 0, C_SQ - C_HG, C_SK - C_HG, C_MQ - C_HG, C_GL - C_HG

ADAM_LR = 0.001
ADAM_B1 = 0.9
ADAM_B2 = 0.999
ADAM_EPS = 1e-08
ADAM_WD = 0.01
ADAM_STEP = 10

VMEM_LIMIT = 58 * 1024 * 1024

R_DN, R_UP, R_BH, R_BS, R_BM, R_OUT, R_KV, R_OTHER = 0, 512, 1024, 1152, 1280, 1408, 1536, 1792

SM_LB, SM_GAIN, SM_SINK, SM_L1G, SM_L1B, SM_L2G, SM_L2B, SM_LOSS, SM_RB, SM_ROWS = 0, 2, 3, 4, 5, 6, 7, 8, 16, 48


def _bf(v):
    return v.astype(BF16)


def _f32(v):
    return v.astype(F32)


def _dot(a, b):
    return jnp.dot(a, b, preferred_element_type=F32)


def _dot_nt(a, b):
    return lax.dot_general(a, b, (((1,), (1,)), ((), ())), preferred_element_type=F32)


def _dot_tn(a, b):
    return lax.dot_general(a, b, (((0,), (0,)), ((), ())), preferred_element_type=F32)


def _sig(v):
    return 0.5 * jnp.tanh(0.5 * v) + 0.5


def _cparams(*sem):
    return pltpu.CompilerParams(dimension_semantics=sem, vmem_limit_bytes=VMEM_LIMIT)


def _const_spec(shape):
    nd = len(shape)
    return pl.BlockSpec(shape, lambda *_: (0,) * nd, pipeline_mode=pl.Buffered(1))


def _dep_spec():
    return pl.BlockSpec((8, 128), lambda *_: (0, 0))


def _in_proj(x, win_t, dep, *, tm):
    S = x.shape[0]

    def body(x_ref, w_ref, dep_ref, qfv_ref, z_ref, gl_ref, xb_ref):
        del dep_ref
        xb = _bf(x_ref[...])
        xb_ref[...] = xb
        for c0 in range(0, C_HG, 1024):
            qfv_ref[:, c0:c0 + 1024] = _dot_nt(xb, w_ref[c0:c0 + 1024, :])
        for c0 in range(0, Z_REST, Z_REST // 2):
            z_ref[:, c0:c0 + Z_REST // 2] = _bf(_dot_nt(xb, w_ref[C_HG + c0:C_HG + c0 + Z_REST // 2, :]))
        for c0 in range(0, IN_COLS - C_GL, 1024):
            gl_ref[:, c0:c0 + 1024] = _bf(_dot_nt(xb, w_ref[C_GL + c0:C_GL + c0 + 1024, :]))

    row = lambda w: pl.BlockSpec((tm, w), lambda i: (i, 0))
    return pl.pallas_call(
        body,
        grid=(S // tm,),
        in_specs=[row(D_MODEL), _const_spec(win_t.shape), _dep_spec()],
        out_specs=[row(C_HG), row(Z_REST), row(IN_COLS - C_GL), row(D_MODEL)],
        out_shape=[jax.ShapeDtypeStruct((S, C_HG), F32), jax.ShapeDtypeStruct((S, Z_REST), BF16),
                   jax.ShapeDtypeStruct((S, IN_COLS - C_GL), BF16), jax.ShapeDtypeStruct((S, D_MODEL), BF16)],
        compiler_params=_cparams("parallel"),
        name="in_proj",
    )(x, win_t, dep)


def _placement(into, tm, N, M, out_dtype):
    if into is None:
        return (lambda i: (i, 0)), (tm, N), jax.ShapeDtypeStruct((M, N), out_dtype), (), {}
    dest, block, index = into
    assert math.prod(block) == tm * N and dest.dtype == out_dtype
    return index, block, jax.ShapeDtypeStruct(dest.shape, dest.dtype), (dest,), {2: 0}


def _mm_tn_resident(a, b, *, tm, kc, name, out_dtype, into=None):
    K, M = a.shape
    N = b.shape[1]
    nk = K // kc
    index, block, out_shape, extra, aliases = _placement(into, tm, N, M, out_dtype)

    def body(a_ref, b_ref, *rest):
        o_ref = rest[-1]
        acc = jnp.zeros((tm, N), F32)
        for kk in range(nk):
            sl = pl.ds(kk * kc, kc)
            acc = acc + _dot_tn(_bf(a_ref[sl, :]), _bf(b_ref[sl, :]))
        o_ref[...] = acc.astype(o_ref.dtype).reshape(block)

    return pl.pallas_call(
        body,
        grid=(M // tm,),
        in_specs=[pl.BlockSpec((K, tm), lambda i: (0, i)), _const_spec((K, N))] + [ANY for _ in extra],
        out_specs=pl.BlockSpec(block, index),
        out_shape=out_shape,
        input_output_aliases=aliases,
        compiler_params=_cparams("parallel"),
        name=name,
    )(a, b, *extra)


def _mm_tn(a, b, *, kc, name, out_dtype=F32, into=None, tm=None):
    K, M = a.shape
    N = b.shape[1]
    if M > 1024 or tm is not None:
        return _mm_tn_resident(a, b, tm=tm or 256, kc=min(kc, 1024), name=name, out_dtype=out_dtype, into=into)
    tm = M
    if a.dtype == BF16 and b.dtype == BF16 and K % (2 * kc) == 0:
        kc = 2 * kc
    nk = K // kc
    index, block, out_shape, extra, aliases = _placement(into, tm, N, M, out_dtype)

    def body(a_ref, b_ref, *rest):
        o_ref, acc = rest[-2], rest[-1]
        k = pl.program_id(1)
        part = _dot_tn(_bf(a_ref[...]), _bf(b_ref[...]))

        @pl.when(k == 0)
        def _():
            acc[...] = part

        @pl.when(k > 0)
        def _():
            acc[...] += part

        @pl.when(k == nk - 1)
        def _():
            o_ref[...] = acc[...].astype(o_ref.dtype).reshape(block)

    return pl.pallas_call(
        body,
        grid=(M // tm, nk),
        in_specs=[pl.BlockSpec((kc, tm), lambda i, k: (k, i)), pl.BlockSpec((kc, N), lambda i, k: (k, 0))]
        + [ANY for _ in extra],
        out_specs=pl.BlockSpec(block, lambda i, k: index(i)),
        out_shape=out_shape,
        input_output_aliases=aliases,
        scratch_shapes=[pltpu.VMEM((tm, N), F32)],
        compiler_params=_cparams("parallel", "arbitrary"),
        name=name,
    )(a, b, *extra)


def _grad_x(d_qfv, d_hg_gl, d_sq, d_skv, d_mq, w_qfv, win_t, add, deps, *, tm):
    M = add.shape[0]
    pieces = (d_qfv, d_hg_gl, d_sq, d_skv, d_mq)

    def body(qfv_ref, hggl_ref, sq_ref, skv_ref, mq_ref, wq_ref, w_ref, add_ref, *rest):
        o_ref = rest[-1]
        acc = add_ref[...] + _dot(qfv_ref[...], wq_ref[...])
        acc = acc + _dot(hggl_ref[:, 0:1024], w_ref[C_HG:C_SQ, :])
        acc = acc + _dot(hggl_ref[:, 1024:4096], w_ref[C_GL:IN_COLS, :])
        acc = acc + _dot(sq_ref[...], w_ref[C_SQ:C_SK, :])
        acc = acc + _dot(skv_ref[...], w_ref[C_SK:C_MQ, :])
        o_ref[...] = acc + _dot(mq_ref[...], w_ref[C_MQ:C_GL, :])

    return pl.pallas_call(
        body,
        grid=(M // tm,),
        in_specs=[pl.BlockSpec((tm, p.shape[1]), lambda i: (i, 0)) for p in pieces]
        + [_const_spec(w_qfv.shape), _const_spec(win_t.shape), pl.BlockSpec((tm, D_MODEL), lambda i: (i, 0))]
        + [_dep_spec() for _ in deps],
        out_specs=pl.BlockSpec((tm, D_MODEL), lambda i: (i, 0)),
        out_shape=jax.ShapeDtypeStruct((M, D_MODEL), F32),
        compiler_params=_cparams("parallel"),
        name="grad_x",
    )(*pieces, w_qfv, win_t, add, *deps)


def _lower_bound(lbl_ref):
    l0 = lbl_ref[0:1, :]
    l1 = lbl_ref[1:2, :]
    mx = jnp.maximum(l0, l1)
    e0 = jnp.exp(l0 - mx)
    e1 = jnp.exp(l1 - mx)
    return e0 / (e0 + e1)


def _tri(lower):
    r = lax.broadcasted_iota(jnp.int32, (HG_CHUNK, HG_CHUNK), 0)
    c = lax.broadcasted_iota(jnp.int32, (HG_CHUNK, HG_CHUNK), 1)
    return (r >= c) if lower else (r <= c)


def _hg_gates(fl, lb):
    sg = _sig(fl)
    f = lb + (1.0 - lb) * sg
    return sg, f, jnp.log(f), 1.0 - f


def _scan_rows(v, reverse=False):
    row = lax.broadcasted_iota(jnp.int32, v.shape, 0)
    s = 1
    while s < HG_CHUNK:
        if reverse:
            v = v + jnp.where(row < HG_CHUNK - s, pltpu.roll(v, HG_CHUNK - s, 0), 0.0)
        else:
            v = v + jnp.where(row >= s, pltpu.roll(v, s, 0), 0.0)
        s *= 2
    return v


def _hgrn_fwd(zmain, lb_logits, *, T):
    S = zmain.shape[0]
    nc = T // HG_CHUNK

    def body(q_ref, f_ref, v_ref, lbl_ref, o_ref, st_ref, state):
        @pl.when(pl.program_id(1) == 0)
        def _():
            state[...] = jnp.zeros_like(state)

        lb = _lower_bound(lbl_ref)
        tril = _tri(True)
        qis, updates, decays, intra = [], [], [], []
        for c in range(nc):
            sl = pl.ds(c * HG_CHUNK, HG_CHUNK)
            _, _, g, k = _hg_gates(_f32(f_ref[sl, :]), lb)
            b = _scan_rows(g)
            bl = jnp.sum(g, axis=0, keepdims=True)
            qi = _bf(_f32(q_ref[sl, :]) * jnp.exp(b))
            ki = _bf(k * jnp.exp(-b))
            ko = _bf(k * jnp.exp(bl - b))
            vb = _bf(v_ref[sl, :])
            att = jnp.where(tril, _dot_nt(qi, ki), 0.0)
            intra.append(_dot(_bf(att), vb))
            qis.append(qi)
            updates.append(_dot_tn(vb, ko))
            decays.append(jnp.exp(bl))
        st = state[...]
        for c in range(nc):
            st_ref[0, c] = st
            o_ref[pl.ds(c * HG_CHUNK, HG_CHUNK), :] = intra[c] + _dot_nt(qis[c], _bf(st))
            st = st * decays[c] + updates[c]
        state[...] = st

    col = lambda base: pl.BlockSpec((T, HG_DK), lambda h, t: (t, base + h))
    return pl.pallas_call(
        body,
        grid=(HG_HEADS, S // T),
        in_specs=[col(0), col(8), col(16), pl.BlockSpec((2, HG_DK), lambda h, t: (0, h))],
        out_specs=[
            pl.BlockSpec((T, HG_DK), lambda h, t: (t, h)),
            pl.BlockSpec((1, nc, HG_DK, HG_DK), lambda h, t: (h, t, 0, 0)),
        ],
        out_shape=[
            jax.ShapeDtypeStruct((S, D_MODEL), F32),
            jax.ShapeDtypeStruct((HG_HEADS, S // HG_CHUNK, HG_DK, HG_DK), F32),
        ],
        scratch_shapes=[pltpu.VMEM((HG_DK, HG_DK), F32)],
        compiler_params=_cparams("parallel", "arbitrary"),
        name="hgrn_fwd",
    )(zmain, zmain, zmain, lb_logits)


def _hgrn_bwd(zmain, lb_logits, states, d_o, *, T):
    S = zmain.shape[0]
    nc = T // HG_CHUNK
    nt = S // T

    def body(q_ref, f_ref, v_ref, lbl_ref, st_ref, do_ref, dz_ref, dlb_ref, dstate):
        @pl.when(pl.program_id(1) == 0)
        def _():
            dstate[...] = jnp.zeros_like(dstate)
            dlb_ref[...] = jnp.zeros_like(dlb_ref)

        lb = _lower_bound(lbl_ref)
        tril = _tri(True)
        last_row = lax.broadcasted_iota(jnp.int32, (HG_CHUNK, HG_DK), 0) == HG_CHUNK - 1
        saved = []
        for c in range(nc):
            sl = pl.ds(c * HG_CHUNK, HG_CHUNK)
            sg, f, g, k = _hg_gates(_f32(f_ref[sl, :]), lb)
            b = _scan_rows(g)
            bl = jnp.sum(g, axis=0, keepdims=True)
            eb = jnp.exp(b)
            enb = jnp.exp(-b)
            eo = jnp.exp(bl - b)
            q_in = _f32(q_ref[sl, :]) * eb
            k_in = k * enb
            k_out = k * eo
            qi, ki, ko = _bf(q_in), _bf(k_in), _bf(k_out)
            vb = _bf(v_ref[sl, :])
            dob = do_ref[sl, :]
            att = jnp.where(tril, _dot_nt(qi, ki), 0.0)
            d_att = _bf(jnp.where(tril, _dot_nt(dob, vb), 0.0))
            d_kin = _dot_tn(d_att, qi)
            saved.append(dict(
                sg=sg, f=f, eb=eb, enb=enb, eo=eo, ebl=jnp.exp(bl), k_out=k_out, ko=ko, vb=vb, dob=dob,
                d_v=_dot_tn(_bf(att), dob), d_qin=_dot(d_att, ki), d_kin=d_kin,
                qk=(q_in, k_in), d_state=_dot_tn(dob, qi)))
        dst = dstate[...]
        dsts = [None] * nc
        for c in reversed(range(nc)):
            dsts[c] = dst
            dst = dst * saved[c]["ebl"] + saved[c]["d_state"]
        dstate[...] = dst
        dlb = jnp.zeros((1, HG_DK), F32)
        for c in range(nc):
            sl = pl.ds(c * HG_CHUNK, HG_CHUNK)
            s = saved[c]
            q_in, k_in = s["qk"]
            st = st_ref[0, c]
            dstb = _bf(dsts[c])
            d_v = s["d_v"] + _dot_nt(s["ko"], dstb)
            d_qin = s["d_qin"] + _dot(s["dob"], _bf(st))
            d_kout = _dot(s["vb"], dstb)
            d_decay = jnp.sum(dsts[c] * st, axis=0, keepdims=True)
            kk = d_kout * s["k_out"]
            d_b = d_qin * q_in - s["d_kin"] * k_in - kk
            d_bl = jnp.sum(kk, axis=0, keepdims=True) + d_decay * s["ebl"]
            d_g = _scan_rows(d_b + jnp.where(last_row, d_bl, 0.0), reverse=True)
            d_f = d_g / s["f"] - (s["d_kin"] * s["enb"] + d_kout * s["eo"])
            dz_ref[sl, 0:HG_DK] = _bf(d_qin * s["eb"])
            dz_ref[sl, HG_DK:2 * HG_DK] = _bf(d_f * (1.0 - lb) * s["sg"] * (1.0 - s["sg"]))
            dz_ref[sl, 2 * HG_DK:3 * HG_DK] = _bf(d_v)
            dlb = dlb + jnp.sum(d_f * (1.0 - s["sg"]), axis=0, keepdims=True)
        dlb_ref[...] += dlb

    rev = lambda base: pl.BlockSpec((T, HG_DK), lambda h, t: (nt - 1 - t, base + h))
    outc = pl.BlockSpec((T, HG_DK), lambda h, t: (nt - 1 - t, h))
    return pl.pallas_call(
        body,
        grid=(HG_HEADS, nt),
        in_specs=[
            rev(0), rev(8), rev(16),
            pl.BlockSpec((2, HG_DK), lambda h, t: (0, h)),
            pl.BlockSpec((1, nc, HG_DK, HG_DK), lambda h, t: (h, nt - 1 - t, 0, 0)),
            outc,
        ],
        out_specs=[pl.BlockSpec((T, 3 * HG_DK), lambda h, t: (nt - 1 - t, h)),
                   pl.BlockSpec((1, HG_DK), lambda h, t: (0, h))],
        out_shape=[jax.ShapeDtypeStruct((S, 3 * D_MODEL), BF16), jax.ShapeDtypeStruct((1, D_MODEL), F32)],
        scratch_shapes=[pltpu.VMEM((HG_DK, HG_DK), F32)],
        compiler_params=_cparams("parallel", "arbitrary"),
        name="hgrn_bwd",
    )(zmain, zmain, zmain, lb_logits, states, d_o)


def _t5_bucket_table():
    qi = jnp.arange(SWA_BLOCK)[:, None] + SWA_BLOCK
    kj = jnp.arange(2 * SWA_BLOCK)[None, :]
    n = jnp.clip(qi - kj, 0, SWA_WINDOW - 1)
    max_exact = NUM_BUCKETS // 2
    nf = jnp.maximum(n, 1).astype(F32)
    large = max_exact + (jnp.log(nf / max_exact) / math.log(MAX_DISTANCE / max_exact)
                         * (NUM_BUCKETS - max_exact)).astype(jnp.int32)
    large = jnp.minimum(large, NUM_BUCKETS - 1)
    return jnp.where(n < max_exact, n, large).astype(jnp.int32)


SWA_ROWS = 32
MERGE_GROUPS = 1


def _swa_bias_init(bias, bucket_ref, rb_ref):
    bk = bucket_ref[...]
    qi = lax.broadcasted_iota(jnp.int32, bk.shape, 0) + SWA_BLOCK
    kj = lax.broadcasted_iota(jnp.int32, bk.shape, 1)
    band = (qi - kj >= 0) & (qi - kj < SWA_WINDOW)
    for h in range(SWA_HEADS):
        def sel(b, acc, h=h):
            return jnp.where(bk == b, rb_ref[b, h], acc)
        t = lax.fori_loop(0, NUM_BUCKETS, sel, jnp.zeros(bk.shape, F32))
        bias[1, h] = jnp.where(band, t, -jnp.inf)
        bias[0, h] = jnp.where(band & (kj >= SWA_BLOCK), t, -jnp.inf)


def _lane_halves(t, kv_head):
    lane = lax.broadcasted_iota(jnp.int32, t.shape, 1)
    rolled = pltpu.roll(t, 64, 1)
    zero = jnp.zeros_like(t)
    if kv_head == 0:
        return jnp.where(lane < 64, t, zero), jnp.where(lane >= 64, rolled, zero)
    return jnp.where(lane < 64, rolled, zero), jnp.where(lane >= 64, t, zero)


def _swa_zero_key0(t):
    return jnp.where(lax.broadcasted_iota(jnp.int32, t.shape, 0) == 0, jnp.zeros_like(t), t)


def _swa_probs(s, masked_bias, sink):
    s = s + masked_bias
    m = jnp.maximum(jnp.max(s, axis=-1, keepdims=True), sink)
    p = jnp.exp(s - m)
    es = jnp.exp(sink - m)
    inv = 1.0 / (jnp.sum(p, axis=-1, keepdims=True) + es)
    return p * inv, es * inv


def _swa_fwd(zmain, bucket, rel_bias, sinks):
    S = zmain.shape[0]
    nb = S // SWA_BLOCK
    scale = SWA_HEAD_DIM ** -0.5

    def body(q_ref, kvc_ref, kvp_ref, bucket_ref, rb_ref, sk_ref, o_ref, p_ref, bias):
        n = pl.program_id(0)

        @pl.when(n == 0)
        def _():
            _swa_bias_init(bias, bucket_ref, rb_ref)

        later = jnp.minimum(n, 1)
        kk = _bf(jnp.concatenate([kvp_ref[:, 0:128], kvc_ref[:, 0:128]], axis=0))
        vv = _swa_zero_key0(_bf(jnp.concatenate([kvp_ref[:, 128:256], kvc_ref[:, 128:256]], axis=0)))
        first_col = lax.broadcasted_iota(jnp.int32, (SWA_ROWS, 2 * SWA_BLOCK), 1) == 0
        scores, values = {}, {}
        for kvh in range(2):
            qst = _bf(jnp.concatenate([q_ref[:, pl.ds((kvh * 4 + jj) * 128, 128)] for jj in range(4)], axis=0) * scale)
            values[kvh] = _lane_halves(vv, kvh)
            for odd, kx in enumerate(_lane_halves(kk, kvh)):
                scores[kvh, odd] = _dot_nt(qst, kx)
        probs = {}
        for (kvh, odd), s in scores.items():
            parts = []
            for jj in range(4):
                h = 2 * (kvh * 4 + jj) + odd
                for r0 in range(0, SWA_BLOCK, SWA_ROWS):
                    p, ps = _swa_probs(s[jj * SWA_BLOCK + r0:jj * SWA_BLOCK + r0 + SWA_ROWS],
                                       bias[later, h, pl.ds(r0, SWA_ROWS), :], sk_ref[0, h])
                    part = _bf(jnp.where(first_col, ps, p))
                    p_ref[pl.ds(r0, SWA_ROWS), pl.ds(h * 2 * SWA_BLOCK, 2 * SWA_BLOCK)] = part
                    parts.append(part)
            probs[kvh, odd] = jnp.concatenate(parts, axis=0)
        for kvh in range(2):
            ost = _dot(probs[kvh, 0], values[kvh][0]) + _dot(probs[kvh, 1], values[kvh][1])
            for jj in range(4):
                o_ref[:, pl.ds((kvh * 4 + jj) * 128, 128)] = ost[jj * SWA_BLOCK:(jj + 1) * SWA_BLOCK]

    smem = pl.BlockSpec(memory_space=pltpu.SMEM)
    return pl.pallas_call(
        body,
        grid=(nb,),
        in_specs=[
            pl.BlockSpec((SWA_BLOCK, 1024), lambda n: (n, Z_SQ // 1024)),
            pl.BlockSpec((SWA_BLOCK, 256), lambda n: (n, Z_SK // 256)),
            pl.BlockSpec((SWA_BLOCK, 256), lambda n: (jnp.maximum(n - 1, 0), Z_SK // 256)),
            _const_spec((SWA_BLOCK, 2 * SWA_BLOCK)), smem, smem,
        ],
        out_specs=[pl.BlockSpec((SWA_BLOCK, 1024), lambda n: (n, 0)),
                   pl.BlockSpec((SWA_BLOCK, SWA_HEADS * 2 * SWA_BLOCK), lambda n: (n, 0))],
        out_shape=[jax.ShapeDtypeStruct((S, 1024), F32),
                   jax.ShapeDtypeStruct((S, SWA_HEADS * 2 * SWA_BLOCK), BF16)],
        scratch_shapes=[pltpu.VMEM((2, SWA_HEADS, SWA_BLOCK, 2 * SWA_BLOCK), F32)],
        compiler_params=_cparams("arbitrary"),
        name="swa_fwd",
    )(zmain, zmain, zmain, bucket, rel_bias, sinks)


def _swa_bwd(zmain, o_b, probs, d_o, bucket, dep):
    S = zmain.shape[0]
    nb = S // SWA_BLOCK
    scale = SWA_HEAD_DIM ** -0.5

    def body(q_ref, kvc_ref, kvp_ref, o_ref, p_ref, do_ref, bucket_ref, dep_ref,
             dq_ref, dkv_ref, drb_ref, dsk_ref, dbias, carry):
        del dep_ref
        n = pl.program_id(0)

        @pl.when(n == 0)
        def _():
            dbias[...] = jnp.zeros_like(dbias)
            carry[...] = jnp.zeros_like(carry)

        @pl.when(n < nb)
        def _():
            kk = _swa_zero_key0(_bf(jnp.concatenate([kvp_ref[:, 0:128], kvc_ref[:, 0:128]], axis=0)))
            vv = _swa_zero_key0(_bf(jnp.concatenate([kvp_ref[:, 128:256], kvc_ref[:, 128:256]], axis=0)))
            lane = lax.broadcasted_iota(jnp.int32, (2 * SWA_BLOCK, 128), 1)
            lane_q = lax.broadcasted_iota(jnp.int32, (4 * SWA_BLOCK, 128), 1)
            pair_cols = {kvh: [pl.ds((kvh * 4 + jj) * 128, 128) for jj in range(4)] for kvh in range(2)}
            qst, dost, ks, d_p, delta = {}, {}, {}, {}, {}
            for kvh in range(2):
                qst[kvh] = _bf(jnp.concatenate([q_ref[:, cl] for cl in pair_cols[kvh]], axis=0) * scale)
                dost[kvh] = jnp.concatenate([do_ref[:, cl] for cl in pair_cols[kvh]], axis=0)
                prod = dost[kvh].astype(F32) * jnp.concatenate([o_ref[:, cl] for cl in pair_cols[kvh]], axis=0)
                ks[kvh] = _lane_halves(kk, kvh)
                for odd, vx in enumerate(_lane_halves(vv, kvh)):
                    keep = (lane_q >= 64) if odd else (lane_q < 64)
                    delta[kvh, odd] = jnp.sum(jnp.where(keep, prod, 0.0), axis=-1, keepdims=True)
                    d_p[kvh, odd] = _dot_nt(dost[kvh], vx)
            pst, dsst = {}, {}
            for (kvh, odd), dp in d_p.items():
                p_parts, ds_parts = [], []
                for jj in range(4):
                    h = 2 * (kvh * 4 + jj) + odd
                    rows = slice(jj * SWA_BLOCK, (jj + 1) * SWA_BLOCK)
                    p = p_ref[:, pl.ds(h * 2 * SWA_BLOCK, 2 * SWA_BLOCK)]
                    ds = _f32(p) * (dp[rows] - delta[kvh, odd][rows])
                    dbias[h] += ds
                    p_parts.append(p)
                    ds_parts.append(_bf(ds))
                pst[kvh, odd] = jnp.concatenate(p_parts, axis=0)
                dsst[kvh, odd] = jnp.concatenate(ds_parts, axis=0)
            dk_parts, dv_parts = [], []
            for kvh in range(2):
                dq_st = _dot(dsst[kvh, 0], ks[kvh][0]) + _dot(dsst[kvh, 1], ks[kvh][1])
                for jj in range(4):
                    dq_ref[:, pair_cols[kvh][jj]] = _bf(dq_st[jj * SWA_BLOCK:(jj + 1) * SWA_BLOCK] * scale)
                zk = jnp.where(lane < 64, _dot_tn(dsst[kvh, 0], qst[kvh]), _dot_tn(dsst[kvh, 1], qst[kvh]))
                zv = jnp.where(lane < 64, _dot_tn(pst[kvh, 0], dost[kvh]), _dot_tn(pst[kvh, 1], dost[kvh]))
                dk_parts.append(zk + pltpu.roll(zk, 64, 1))
                dv_parts.append(zv + pltpu.roll(zv, 64, 1))
            dk = jnp.where(lane < 64, dk_parts[0], dk_parts[1])
            dv = jnp.where(lane < 64, dv_parts[0], dv_parts[1])
            dkv = _swa_zero_key0(jnp.concatenate([dk, dv], axis=1))
            dkv_ref[...] = _bf(carry[...] + dkv[0:SWA_BLOCK])
            carry[...] = dkv[SWA_BLOCK:]

        @pl.when(n == nb)
        def _():
            dkv_ref[...] = _bf(carry[...])
            first_col = lax.broadcasted_iota(jnp.int32, (SWA_BLOCK, 2 * SWA_BLOCK), 1) == 0
            bk = jnp.where(first_col, -1, bucket_ref[...])

            row = lax.broadcasted_iota(jnp.int32, (NUM_BUCKETS, 128), 0)
            lane = lax.broadcasted_iota(jnp.int32, (NUM_BUCKETS, 128), 1)

            def total(v):
                return jnp.sum(jnp.sum(v, axis=1, keepdims=True), axis=0, keepdims=True)

            def per_head(h, acc):
                db = dbias[h]
                d_rb, d_sk = acc
                d_sk = d_sk + jnp.where((row == 0) & (lane == h), total(jnp.where(first_col, db, 0.0)), 0.0)

                def per_bucket(b, d_rb):
                    return d_rb + jnp.where((row == b) & (lane == h), total(jnp.where(bk == b, db, 0.0)), 0.0)

                return lax.fori_loop(0, NUM_BUCKETS, per_bucket, d_rb), d_sk

            zero = jnp.zeros((NUM_BUCKETS, 128), F32)
            d_rb, d_sk = lax.fori_loop(0, SWA_HEADS, per_head, (zero, zero))
            drb_ref[...] = d_rb
            dsk_ref[...] = d_sk[0:8]

    cur = lambda n: jnp.minimum(n, nb - 1)
    prev = lambda n: jnp.maximum(jnp.minimum(n, nb - 1) - 1, 0)
    return pl.pallas_call(
        body,
        grid=(nb + 1,),
        in_specs=[
            pl.BlockSpec((SWA_BLOCK, 1024), lambda n: (cur(n), Z_SQ // 1024)),
            pl.BlockSpec((SWA_BLOCK, 256), lambda n: (cur(n), Z_SK // 256)),
            pl.BlockSpec((SWA_BLOCK, 256), lambda n: (prev(n), Z_SK // 256)),
            pl.BlockSpec((SWA_BLOCK, 1024), lambda n: (cur(n), 0)),
            pl.BlockSpec((SWA_BLOCK, SWA_HEADS * 2 * SWA_BLOCK), lambda n: (cur(n), 0)),
            pl.BlockSpec((SWA_BLOCK, 1024), lambda n: (cur(n), 0)),
            _const_spec((SWA_BLOCK, 2 * SWA_BLOCK)), _dep_spec(),
        ],
        out_specs=[
            pl.BlockSpec((SWA_BLOCK, 1024), lambda n: (cur(n), 0)),
            pl.BlockSpec((SWA_BLOCK, 256), lambda n: (jnp.maximum(n - 1, 0), 0)),
            pl.BlockSpec((NUM_BUCKETS, 128), lambda n: (0, 0)),
            pl.BlockSpec((8, 128), lambda n: (0, 0)),
        ],
        out_shape=[
            jax.ShapeDtypeStruct((S, 1024), BF16),
            jax.ShapeDtypeStruct((S, 256), BF16),
            jax.ShapeDtypeStruct((NUM_BUCKETS, 128), F32),
            jax.ShapeDtypeStruct((8, 128), F32),
        ],
        scratch_shapes=[
            pltpu.VMEM((SWA_HEADS, SWA_BLOCK, 2 * SWA_BLOCK), F32),
            pltpu.VMEM((SWA_BLOCK, 256), F32),
        ],
        compiler_params=_cparams("arbitrary"),
        name="swa_bwd",
    )(zmain, zmain, zmain, o_b, probs, d_o, bucket, dep)


def _mem_q_specs(T):
    return [pl.BlockSpec((T, MEM_HEAD_DIM), lambda t, h=h: (t, Z_MQ // MEM_HEAD_DIM + h)) for h in range(MEM_HEADS)]


def _mem_kv_proj(mem, g2):
    def body(mem_ref, w_ref, o_ref):
        o_ref[...] = _dot_nt(_bf(mem_ref[...]), _rows(w_ref))

    return pl.pallas_call(
        body,
        grid=(1,),
        in_specs=[pl.BlockSpec((MEM_LEN, D_MODEL), lambda i: (0, 0)), _gathered_spec(R_KV, R_OTHER)],
        out_specs=pl.BlockSpec((MEM_LEN, 2048), lambda i: (0, 0)),
        out_shape=jax.ShapeDtypeStruct((MEM_LEN, 2048), F32),
        compiler_params=_cparams("arbitrary"),
        name="mem_kv_proj",
    )(mem, g2)


def _mem_fwd(zmain, mkv, *, T):
    S = zmain.shape[0]

    def body(q0, q1, q2, q3, kv_ref, o_ref, p_ref):
        heads = [pl.ds(h * MEM_HEAD_DIM, MEM_HEAD_DIM) for h in range(MEM_HEADS)]
        scores = [_dot_nt(_bf(q_ref[...] * (MEM_HEAD_DIM ** -0.5)), _bf(kv_ref[:, cols]))
                  for q_ref, cols in zip((q0, q1, q2, q3), heads)]
        probs = []
        for s, cols in zip(scores, heads):
            e = jnp.exp(s - jnp.max(s, axis=-1, keepdims=True))
            pb = _bf(e * (1.0 / jnp.sum(e, axis=-1, keepdims=True)))
            p_ref[:, cols] = pb
            probs.append(pb)
        for h, (pb, cols) in enumerate(zip(probs, heads)):
            o_ref[:, cols] = _dot(pb, _bf(kv_ref[:, pl.ds(1024 + h * MEM_HEAD_DIM, MEM_HEAD_DIM)]))

    row = pl.BlockSpec((T, 1024), lambda t: (t, 0))
    return pl.pallas_call(
        body,
        grid=(S // T,),
        in_specs=_mem_q_specs(T) + [_const_spec((MEM_LEN, 2048))],
        out_specs=[row, row],
        out_shape=[jax.ShapeDtypeStruct((S, 1024), F32), jax.ShapeDtypeStruct((S, 1024), BF16)],
        compiler_params=_cparams("parallel"),
        name="mem_fwd",
    )(zmain, zmain, zmain, zmain, mkv)


def _mem_bwd(zmain, mkv, o_c, probs, d_o, *, T):
    S = zmain.shape[0]
    scale = MEM_HEAD_DIM ** -0.5

    def body(q0, q1, q2, q3, kv_ref, o_ref, p_ref, do_ref, dq_ref, dkv_ref):
        @pl.when(pl.program_id(0) == 0)
        def _():
            dkv_ref[...] = jnp.zeros_like(dkv_ref)

        heads = [(pl.ds(h * MEM_HEAD_DIM, MEM_HEAD_DIM), pl.ds(1024 + h * MEM_HEAD_DIM, MEM_HEAD_DIM))
                 for h in range(MEM_HEADS)]
        d_p = [_dot_nt(do_ref[:, cols], _bf(kv_ref[:, vcols])) for cols, vcols in heads]
        d_s = []
        for dp, (cols, _) in zip(d_p, heads):
            delta = jnp.sum(do_ref[:, cols].astype(F32) * o_ref[:, cols], axis=-1, keepdims=True)
            d_s.append(_bf(_f32(p_ref[:, cols]) * (dp - delta)))
        for ds, q_ref, (cols, vcols) in zip(d_s, (q0, q1, q2, q3), heads):
            dq_ref[:, cols] = _bf(_dot(ds, _bf(kv_ref[:, cols])) * scale)
            dkv_ref[:, cols] += _dot_tn(ds, _bf(q_ref[...] * scale))
            dkv_ref[:, vcols] += _dot_tn(p_ref[:, cols], do_ref[:, cols])

    row = pl.BlockSpec((T, 1024), lambda t: (t, 0))
    return pl.pallas_call(
        body,
        grid=(S // T,),
        in_specs=_mem_q_specs(T) + [_const_spec((MEM_LEN, 2048)), row, row, row],
        out_specs=[row, pl.BlockSpec((MEM_LEN, 2048), lambda t: (0, 0))],
        out_shape=[jax.ShapeDtypeStruct((S, 1024), BF16), jax.ShapeDtypeStruct((MEM_LEN, 2048), F32)],
        compiler_params=_cparams("arbitrary"),
        name="mem_bwd",
    )(zmain, zmain, zmain, zmain, mkv, o_c, probs, d_o)


def _layer_norm(u):
    mu = jnp.mean(u, axis=-1, keepdims=True)
    xc = u - mu
    rstd = lax.rsqrt(jnp.mean(xc * xc, axis=-1, keepdims=True) + LN_EPS)
    return xc * rstd, rstd


def _layer_norm_bwd(dy, gamma, xhat, rstd):
    dxh = dy * gamma
    return rstd * (dxh - jnp.mean(dxh, axis=-1, keepdims=True) - xhat * jnp.mean(dxh * xhat, axis=-1, keepdims=True))


def _merge_stages(rows, oraw_ref, hg_ref, ob_ref, oc_ref, gl_ref, x_ref, gain_ref, wbh, wbs, wbm, wout, g_ref, b_ref,
                  fwd_out=None, bwd=None, saved=None):
    ys, rs = [], []
    for h in range(HG_HEADS):
        oh = oraw_ref[rows, pl.ds(h * HG_DK, HG_DK)]
        r = lax.rsqrt(jnp.mean(oh * oh, axis=-1, keepdims=True) + RMS_EPS)
        ys.append(oh * r)
        rs.append(r)
    y = jnp.concatenate(ys, axis=1)
    hg = _f32(hg_ref[rows, :])
    sg = _sig(hg)
    silu = hg * sg
    gain = gain_ref[...]
    gates = [_sig(_f32(gl_ref[rows, pl.ds(i * 1024, 1024)])) for i in range(3)]
    if saved is None:
        oa = _bf(y * gain * silu)
        pa = _dot(oa, _rows(wbh))
        pb = _dot(_bf(ob_ref[rows, :]), _rows(wbs))
        pc = _dot(_bf(oc_ref[rows, :]), _rows(wbm))
        yield
        m = _bf(gates[0] * pa + gates[1] * pb + gates[2] * pc)
    else:
        pa, pb, pc = (_f32(r[rows, :]) for r in saved[:3])
        m = saved[3][rows, :]
    mix = _dot(m, _rows(wout))
    yield
    xhat, rstd = _layer_norm(ALPHA * x_ref[rows, :] + mix)
    if bwd is None:
        h1 = xhat * g_ref[...] + b_ref[...]
        fwd_out[0][rows, :] = h1
        fwd_out[1][rows, :] = _bf(h1)
        for ref, val in zip(fwd_out[2:], (_bf(pa), _bf(pb), _bf(pc), m, oa)):
            ref[rows, :] = val
        return
    (dh1_ref, dx_ref, du1_ref, dpa_ref, dpb_ref, dpc_ref, doraw_ref, dob_ref, doc_ref, dz_ref,
     dgain_ref, dg_ref, db_ref) = bwd
    dh1 = dh1_ref[rows, :]
    dg_ref[...] += jnp.sum(dh1 * xhat, axis=0, keepdims=True)
    db_ref[...] += jnp.sum(dh1, axis=0, keepdims=True)
    du1 = _layer_norm_bwd(dh1, g_ref[...], xhat, rstd)
    dx_ref[rows, :] = ALPHA * du1
    du1b = _bf(du1)
    du1_ref[rows, :] = du1b
    dm = _dot_nt(du1b, _rows(wout))
    yield
    d_branches = []
    for i, (g, p, dp_ref, w_r) in enumerate(zip(gates, (pa, pb, pc), (dpa_ref, dpb_ref, dpc_ref), (wbh, wbs, wbm))):
        dz_ref[rows, pl.ds((i + 1) * 1024, 1024)] = _bf(dm * p * g * (1.0 - g))
        dp = _bf(dm * g)
        dp_ref[rows, :] = dp
        d_branches.append(_dot_nt(dp, _rows(w_r)))
    yield
    doa, d_ob, d_oc = d_branches
    dob_ref[rows, :] = _bf(d_ob)
    doc_ref[rows, :] = _bf(d_oc)
    t = doa * y
    dgain_ref[...] += jnp.sum(t * silu, axis=0, keepdims=True)
    dz_ref[rows, 0:1024] = _bf(t * gain * sg * (1.0 + hg * (1.0 - sg)))
    dy = doa * gain * silu
    for h in range(HG_HEADS):
        cols = slice(h * HG_DK, (h + 1) * HG_DK)
        yh = y[:, cols]
        dyh = dy[:, cols]
        doraw_ref[rows, pl.ds(h * HG_DK, HG_DK)] = _bf(rs[h] * (dyh - yh * jnp.mean(dyh * yh, axis=-1, keepdims=True)))


def _interleave(chains):
    live = list(chains)
    while live:
        still = []
        for c in live:
            try:
                next(c)
                still.append(c)
            except StopIteration:
                pass
        live = still


def _gathered_spec(lo, hi):
    n = hi - lo
    return pl.BlockSpec((N_DEV, n, D_MODEL), lambda *_: (0, lo // n, 0), pipeline_mode=pl.Buffered(1))


def _rows(w_ref):
    return w_ref[...].reshape(-1, D_MODEL)


def _merge_in_specs(T):
    row = lambda w, c=0: pl.BlockSpec((T, w), lambda i: (i, c))
    vec = pl.BlockSpec((1, D_MODEL), lambda i: (0, 0))
    w = [_gathered_spec(lo, hi) for lo, hi in ((R_BH, R_BS), (R_BS, R_BM), (R_BM, R_OUT), (R_OUT, R_KV))]
    return [row(1024), row(1024, Z_HG // 1024), row(1024), row(1024), row(3072), row(1024), vec, *w, vec, vec]


def _merge_fwd(o_raw, zmain, o_b, o_c, gl, x, gain, wbh, wbs, wbm, wout, ln_g, ln_b, *, T):
    S = x.shape[0]

    def body(*refs):
        ins, outs = refs[:13], refs[13:]
        _interleave(_merge_stages(pl.ds(r0, T // MERGE_GROUPS), *ins, fwd_out=outs)
                    for r0 in range(0, T, T // MERGE_GROUPS))

    row = pl.BlockSpec((T, D_MODEL), lambda i: (i, 0))
    return pl.pallas_call(
        body,
        grid=(S // T,),
        in_specs=_merge_in_specs(T),
        out_specs=[row] * 7,
        out_shape=[jax.ShapeDtypeStruct((S, D_MODEL), F32)] + [jax.ShapeDtypeStruct((S, D_MODEL), BF16)] * 6,
        compiler_params=_cparams("parallel"),
        name="merge_fwd",
    )(o_raw, zmain, o_b, o_c, gl, x, gain, wbh, wbs, wbm, wout, ln_g, ln_b)


def _merge_bwd(d_h1, pa, pb, pc, m, o_raw, zmain, gl, x, gain, wbh, wbs, wbm, wout, ln_g, *, T):
    S = x.shape[0]

    def body(dh1_ref, pa_ref, pb_ref, pc_ref, m_ref, oraw_ref, hg_ref, gl_ref, x_ref, gain_ref, wbh_r, wbs_r, wbm_r, wout_r,
             g_ref, dx_ref, du1_ref, dpa_ref, dpb_ref, dpc_ref, doraw_ref, dob_ref, doc_ref, dz_ref,
             dgain_ref, dg_ref, db_ref):
        @pl.when(pl.program_id(0) == 0)
        def _():
            dgain_ref[...] = jnp.zeros_like(dgain_ref)
            dg_ref[...] = jnp.zeros_like(dg_ref)
            db_ref[...] = jnp.zeros_like(db_ref)

        ins = (oraw_ref, hg_ref, None, None, gl_ref, x_ref, gain_ref, wbh_r, wbs_r, wbm_r, wout_r, g_ref, None)
        bwd = (dh1_ref, dx_ref, du1_ref, dpa_ref, dpb_ref, dpc_ref, doraw_ref, dob_ref, doc_ref, dz_ref,
               dgain_ref, dg_ref, db_ref)
        _interleave([_merge_stages(pl.ds(0, T), *ins, bwd=bwd, saved=(pa_ref, pb_ref, pc_ref, m_ref))])

    row = lambda w, c=0: pl.BlockSpec((T, w), lambda i: (i, c))
    vec = pl.BlockSpec((1, D_MODEL), lambda i: (0, 0))
    w = [_gathered_spec(lo, hi) for lo, hi in ((R_BH, R_BS), (R_BS, R_BM), (R_BM, R_OUT), (R_OUT, R_KV))]
    bshape = jax.ShapeDtypeStruct((S, D_MODEL), BF16)
    vshape = jax.ShapeDtypeStruct((1, D_MODEL), F32)
    return pl.pallas_call(
        body,
        grid=(S // T,),
        in_specs=[row(1024)] * 6 + [row(1024, Z_HG // 1024), row(3072), row(1024), vec, *w, vec],
        out_specs=[row(1024)] * 8 + [row(4096), vec, vec, vec],
        out_shape=[jax.ShapeDtypeStruct((S, D_MODEL), F32)] + [bshape] * 7
        + [jax.ShapeDtypeStruct((S, 4096), BF16), vshape, vshape, vshape],
        compiler_params=_cparams("arbitrary"),
        name="merge_bwd",
    )(d_h1, pa, pb, pc, m, o_raw, zmain, gl, x, gain, wbh, wbs, wbm, wout, ln_g)


def _mlp_fwd_bwd(h1, target, wup_t, wdn, ln_g, ln_b, *, T, FC):
    S = h1.shape[0]
    nf = D_FF // FC
    assert FC == R_BH - R_UP == R_UP - R_DN

    def body(h1_ref, t_ref, wup_ref, wdn_ref, g_ref, b_ref, dh1_ref, a_ref, dup_ref, du2_ref, loss_ref, dg_ref, db_ref, up_scr):
        @pl.when(pl.program_id(0) == 0)
        def _():
            loss_ref[...] = jnp.zeros_like(loss_ref)
            dg_ref[...] = jnp.zeros_like(dg_ref)
            db_ref[...] = jnp.zeros_like(db_ref)

        h1v = h1_ref[...]
        h1b = _bf(h1v)
        ff = jnp.zeros((T, D_MODEL), F32)
        for j in range(nf):
            rows = pl.ds(j * FC, FC)
            up = jnp.maximum(_dot_nt(h1b, wup_ref[j]), 0.0)
            up_scr[:, rows] = _bf(up)
            a = _bf(up * up)
            a_ref[:, rows] = a
            ff = ff + _dot(a, wdn_ref[j])
        xhat, rstd = _layer_norm(ALPHA * h1v + ff)
        gamma = g_ref[...]
        err = xhat * gamma + b_ref[...] - t_ref[...]
        loss_ref[...] += jnp.sum(jnp.sum(err * err, axis=-1, keepdims=True), axis=0, keepdims=True) * (0.5 / D_MODEL)
        dy = err * (1.0 / D_MODEL)
        dg_ref[...] += jnp.sum(dy * xhat, axis=0, keepdims=True)
        db_ref[...] += jnp.sum(dy, axis=0, keepdims=True)
        du2 = _layer_norm_bwd(dy, gamma, xhat, rstd)
        du2b = _bf(du2)
        du2_ref[...] = du2b
        dh1 = ALPHA * du2
        for j in range(nf):
            rows = pl.ds(j * FC, FC)
            dup = _bf(_dot_nt(du2b, wdn_ref[j]) * (2.0 * up_scr[:, rows].astype(F32)))
            dup_ref[:, rows] = dup
            dh1 = dh1 + _dot(dup, wup_ref[j])
        dh1_ref[...] = dh1

    row = lambda w: pl.BlockSpec((T, w), lambda i: (i, 0))
    vec = pl.BlockSpec((1, D_MODEL), lambda i: (0, 0))
    vshape = jax.ShapeDtypeStruct((1, D_MODEL), F32)
    return pl.pallas_call(
        body,
        grid=(S // T,),
        in_specs=[row(1024), row(1024), _gathered_spec(R_UP, R_BH), _gathered_spec(R_DN, R_UP), vec, vec],
        out_specs=[row(1024), row(D_FF), row(D_FF), row(1024), pl.BlockSpec((8, 128), lambda i: (0, 0)), vec, vec],
        out_shape=[
            jax.ShapeDtypeStruct((S, D_MODEL), F32),
            jax.ShapeDtypeStruct((S, D_FF), BF16),
            jax.ShapeDtypeStruct((S, D_FF), BF16),
            jax.ShapeDtypeStruct((S, D_MODEL), BF16),
            jax.ShapeDtypeStruct((8, 128), F32), vshape, vshape,
        ],
        scratch_shapes=[pltpu.VMEM((T, D_FF), BF16)],
        compiler_params=_cparams("arbitrary"),
        name="mlp_fwd_bwd",
    )(h1, target, wup_t, wdn, ln_g, ln_b)


def _local_step(x, mem, target, lb_logits, gain, sinks, rel_bias, ln1_g, ln1_b, ln2_g, ln2_b,
                win_t, dep0, other_weights, send_other_grads, send_small_grads, send_win_grad):
    S = x.shape[0]
    T = min(256, S)
    KC = min(2048, S)
    z_qfv, zmain, gl, xb = _in_proj(x, win_t, dep0, tm=min(512, S))
    bucket = _t5_bucket_table()

    o_raw, states = _hgrn_fwd(z_qfv, lb_logits, T=min(2048, S))
    o_b, swa_probs = _swa_fwd(zmain, bucket, rel_bias, sinks)
    g2 = other_weights((o_b, o_raw))
    mkv = _mem_kv_proj(mem, g2)
    o_c, mem_probs = _mem_fwd(zmain, mkv, T=min(1024, S))
    h1, h1b, pa, pb, pc, m, oa = _merge_fwd(o_raw, zmain, o_b, o_c, gl, x, gain, g2, g2, g2, g2, ln1_g, ln1_b,
                                                T=min(512, S))

    d_h1, act, d_up, du2, loss, d_ln2_g, d_ln2_b = _mlp_fwd_bwd(h1, target, g2, g2, ln2_g, ln2_b, T=min(512, S), FC=512)
    wgrad = functools.partial(_mm_tn, out_dtype=BF16)
    halves = lambda r0: (lambda i: (i // 2, r0 // 256 + i % 2, 0))
    whole = lambda r0: (lambda i: (0, r0 // 128, 0))
    og = lax.empty((N_DEV, R_OTHER, D_MODEL), BF16)
    og = wgrad(act, du2, kc=KC, name="grad_w_down", into=(og, (1, 256, D_MODEL), halves(R_DN)))
    og = wgrad(d_up, h1b, kc=KC, name="grad_w_up", into=(og, (1, 256, D_MODEL), halves(R_UP)))

    (dx_part, du1, dpa, dpb, dpc, d_oraw, d_ob, d_oc, d_hg_gl, d_gain, d_ln1_g, d_ln1_b) = _merge_bwd(
        d_h1, pa, pb, pc, m, o_raw, zmain, gl, x, gain, g2, g2, g2, g2, ln1_g, T=T)
    for a_op, b_op, r0, nm in ((m, du1, R_OUT, "out"), (oa, dpa, R_BH, "branch_hg"), (o_b, dpb, R_BS, "branch_swa"),
                               (o_c, dpc, R_BM, "branch_mem")):
        og = wgrad(a_op, b_op, kc=KC, name="grad_w_" + nm, into=(og, (N_DEV, 128, D_MODEL), whole(r0)))

    d_mq, d_mkv = _mem_bwd(zmain, mkv, o_c, mem_probs, d_oc, T=min(1024, S))
    og = wgrad(d_mkv, mem, kc=MEM_LEN, name="grad_w_mem_kv",
               into=(og, (1, 256, D_MODEL), lambda i: (i, R_KV // 256, 0)))
    sent_others = send_other_grads(og)
    d_sq, d_skv, d_rb, d_sink = _swa_bwd(zmain, o_b, swa_probs, d_ob, bucket, sent_others)
    d_qfv, d_lb = _hgrn_bwd(z_qfv, lb_logits, states, d_oraw, T=min(2048, S))
    sent_small = send_small_grads(_pack_small_grads(d_lb, d_gain, d_sink, d_rb, d_ln1_g, d_ln1_b, d_ln2_g, d_ln2_b, loss))

    head_major = lambda a: a.reshape(3, HG_HEADS, HG_DK, D_MODEL).transpose(1, 0, 2, 3).reshape(3 * D_MODEL, D_MODEL)
    pieces = (d_qfv, d_hg_gl, d_sq, d_skv, d_mq)
    placed = (
        ("qfv", d_qfv, 128, lambda i: ((i % 3) * HG_HEADS + i // 3, 0)),
        ("hg_gates", d_hg_gl, 256, lambda i: (jnp.where(i < 4, C_HG // 256 + i, C_GL // 256 + i - 4), 0)),
        ("swa_q", d_sq, None, lambda i: (C_SQ // 1024, 0)),
        ("swa_kv", d_skv, None, lambda i: (C_SK // 256, 0)),
        ("mem_q", d_mq, 256, lambda i: (C_MQ // 256 + i, 0)),
    )
    g_win_t = lax.empty((IN_COLS, D_MODEL), BF16)
    for nm, piece, tile, index in placed:
        g_win_t = wgrad(piece, xb, kc=KC, name="grad_w_in_" + nm, tm=tile,
                        into=(g_win_t, (tile or piece.shape[1], D_MODEL), index))
    sent_win = send_win_grad(g_win_t, sent_small)
    return _grad_x(*pieces, head_major(win_t[:C_HG]), win_t, dx_part, sent_win, tm=T)


MESH = pl.DeviceIdType.MESH
ANY = pl.BlockSpec(memory_space=pl.ANY)


def _coords():
    return lax.axis_index("x"), lax.axis_index("y"), lax.axis_index("c")


def _other_chips(x, y):
    return [(1 - x, y), (x, 1 - y), (1 - x, 1 - y)]


def _all_gather_weights(shard, pack):
    arrays = (shard,)
    na = len(arrays)
    packed_rows = sum(a.shape[1] if t else a.shape[0] for a, t in pack)

    def body(*refs):
        srcs, pack_refs = refs[:na], refs[na:na + len(pack)]
        dsts, packed_ref = refs[na + len(pack):2 * na + len(pack)], refs[2 * na + len(pack)]
        send_sems, recv_sems, local_sems = refs[2 * na + len(pack) + 1:]
        x, y, c = _coords()
        me, sibling = (x, y, c), (x, y, 1 - c)
        chips = _other_chips(x, y)

        def slot(a, px, py, pc):
            return dsts[a].at[4 * px + 2 * py + pc]

        def copy(a, k, block, to, from_shard=False):
            return pltpu.make_async_remote_copy(
                src_ref=srcs[a] if from_shard else slot(a, *block), dst_ref=slot(a, *block),
                send_sem=send_sems.at[a * 7 + k], recv_sem=recv_sems.at[a * 7 + k],
                device_id=to, device_id_type=MESH)

        own = [pltpu.make_async_copy(srcs[a], slot(a, *me), local_sems.at[a]) for a in range(na)]
        for cp in own:
            cp.start()
        first = []
        for a in range(na):
            first.append(copy(a, 0, me, sibling, True))
            first += [copy(a, 1 + j, me, (*chip, c), True) for j, chip in enumerate(chips)]
        for cp in first:
            cp.start()
        row = 0
        for (a, transposed), p_ref in zip(pack, pack_refs):
            v = p_ref[...].T if transposed else p_ref[...]
            packed_ref[row:row + v.shape[0], :] = _bf(v)
            row += v.shape[0]
        passed = []
        for j, chip in enumerate(chips):
            for a in range(na):
                copy(a, 1 + j, (*chip, c), me).wait_recv()
                fwd = copy(a, 4 + j, (*chip, c), sibling)
                fwd.start()
                passed.append(fwd)
        for a in range(na):
            copy(a, 0, sibling, me).wait_recv()
            for j, chip in enumerate(chips):
                copy(a, 4 + j, (*chip, 1 - c), me).wait_recv()
        for cp in first + passed:
            cp.wait_send()
        for cp in own:
            cp.wait()

    vm = pl.BlockSpec(memory_space=pltpu.VMEM)
    return pl.pallas_call(
        body,
        in_specs=[ANY] * na + [vm] * len(pack),
        out_specs=[ANY] * na + [vm],
        out_shape=[jax.ShapeDtypeStruct((N_DEV,) + a.shape, a.dtype) for a in arrays]
        + [jax.ShapeDtypeStruct((packed_rows, D_MODEL), BF16)],
        scratch_shapes=[pltpu.SemaphoreType.DMA((7 * na,)), pltpu.SemaphoreType.DMA((7 * na,)),
                        pltpu.SemaphoreType.DMA((na,))],
        compiler_params=pltpu.CompilerParams(vmem_limit_bytes=VMEM_LIMIT),
        name="all_gather_weights",
    )(*arrays, *[a for a, _ in pack])


HBM = pl.BlockSpec(memory_space=pltpu.HBM)
SEM = pl.BlockSpec(memory_space=pltpu.SEMAPHORE)
_DATAFLOW = pltpu.SideEffectType.DATAFLOW_SIDE_EFFECTING


def _peer(x, y, c, r):
    return x ^ (r >> 2), y ^ ((r >> 1) & 1), c ^ (r & 1)


def _direct_copies(src_ref, land_ref, send_sems, recv_sems, gather, receiving):
    x, y, c = _coords()
    me = 4 * x + 2 * y + c
    copies = []
    for r in range(1, N_DEV):
        px, py, pc = _peer(x, y, c, r)
        peer = 4 * px + 2 * py + pc
        if gather:
            src, dst = src_ref, land_ref.at[peer if receiving else me]
        else:
            src, dst = src_ref.at[peer], land_ref.at[r - 1]
        copies.append(pltpu.make_async_remote_copy(
            src_ref=src, dst_ref=dst, send_sem=send_sems.at[r - 1], recv_sem=recv_sems.at[r - 1],
            device_id=(px, py, pc), device_id_type=MESH))
    return copies


def _direct_start(src, land, *, gather, name, after=None):
    def body(src_ref, land_ref, *rest):
        send_sems, recv_sems, token = rest[-5], rest[-4], rest[-1]
        for cp in _direct_copies(src_ref, land_ref, send_sems, recv_sems, gather, False):
            cp.start()
        token[...] = jnp.zeros_like(token)

    afters = () if after is None else (after,)
    return pl.pallas_call(
        body,
        name=name,
        out_shape=(pltpu.SemaphoreType.DMA((N_DEV - 1,)), pltpu.SemaphoreType.DMA((N_DEV - 1,)),
                   pltpu.HBM(src.shape, src.dtype), pltpu.HBM(land.shape, land.dtype),
                   jax.ShapeDtypeStruct((8, 128), F32)),
        in_specs=(HBM, HBM) + tuple(ANY for _ in afters),
        out_specs=(SEM, SEM, HBM, HBM, pl.BlockSpec(memory_space=pltpu.VMEM)),
        input_output_aliases={0: 2, 1: 3},
        compiler_params=pltpu.CompilerParams(has_side_effects=_DATAFLOW),
    )(pltpu.with_memory_space_constraint(src, pltpu.HBM), pltpu.with_memory_space_constraint(land, pltpu.HBM), *afters)


def _direct_wait(send_sems, recv_sems, src_thru, land_thru, after, *, gather, name):
    afters = after if isinstance(after, tuple) else (after,)

    def body(src_ref, land_ref, send_sems_ref, recv_sems_ref, *rest):
        del rest
        for cp in _direct_copies(src_ref, land_ref, send_sems_ref, recv_sems_ref, gather, True):
            cp.wait_send()
            cp.wait_recv()

    return pl.pallas_call(
        body,
        name=name,
        out_shape=(pltpu.HBM(src_thru.shape, src_thru.dtype), pltpu.HBM(land_thru.shape, land_thru.dtype)),
        in_specs=(HBM, HBM, SEM, SEM) + tuple(ANY for _ in afters),
        out_specs=(HBM, HBM),
        input_output_aliases={0: 0, 1: 1},
        compiler_params=pltpu.CompilerParams(has_side_effects=_DATAFLOW),
    )(src_thru, land_thru, send_sems, recv_sems, *afters)


def _sum_partials(src, land, me, *, tr, name, wmv=None):
    R = src.shape[1]
    extra = () if wmv is None else tuple(wmv)

    def body(me_ref, s_ref, l_ref, *rest):
        del me_ref
        acc = s_ref[0].astype(F32)
        for r in range(N_DEV - 1):
            acc = acc + l_ref[r].astype(F32)
        rest[len(extra)][...] = acc
        if extra:
            w_ref, m_ref, v_ref, _, d_ref, nm_ref, nv_ref = rest
            d_ref[...], nm_ref[...], nv_ref[...] = _adam_step(w_ref[...], acc, m_ref[...], v_ref[...])

    row = pl.BlockSpec((tr, 1024), lambda i, mr: (i, 0))
    n_out = 4 if extra else 1
    res = pl.pallas_call(
        body,
        grid_spec=pltpu.PrefetchScalarGridSpec(
            num_scalar_prefetch=1, grid=(R // tr,),
            in_specs=[pl.BlockSpec((1, tr, 1024), lambda i, mr: (mr[0], i, 0)),
                      pl.BlockSpec((N_DEV - 1, tr, 1024), lambda i, mr: (0, i, 0))] + [row for _ in extra],
            out_specs=[row] * n_out),
        out_shape=[jax.ShapeDtypeStruct((R, 1024), F32)] * n_out,
        name=name,
    )(me, src, land, *extra)
    return res if extra else res[0]


_SMALL = ("lb_logits", "hg_norm_gain", "swa_sinks", "rel_bias", "ln1_g", "ln1_b", "ln2_g", "ln2_b")


def _pack_small_grads(d_lb, d_gain, d_sink, d_rb, d_ln1_g, d_ln1_b, d_ln2_g, d_ln2_b, loss):
    def body(lb_ref, gain_ref, sink_ref, rb_ref, l1g_ref, l1b_ref, l2g_ref, l2b_ref, loss_ref, o_ref):
        o_ref[...] = jnp.zeros_like(o_ref)
        for row, ref in ((SM_LB, lb_ref), (SM_GAIN, gain_ref), (SM_L1G, l1g_ref), (SM_L1B, l1b_ref),
                         (SM_L2G, l2g_ref), (SM_L2B, l2b_ref)):
            o_ref[row:row + 1, :] = ref[...]
        o_ref[SM_SINK:SM_SINK + 1, 0:128] = sink_ref[0:1, :]
        o_ref[SM_LOSS:SM_LOSS + 1, 0:128] = loss_ref[0:1, :]
        o_ref[SM_RB:SM_RB + NUM_BUCKETS, 0:128] = rb_ref[...]

    vm = pl.BlockSpec(memory_space=pltpu.VMEM)
    return pl.pallas_call(
        body,
        in_specs=[vm] * 9,
        out_specs=vm,
        out_shape=jax.ShapeDtypeStruct((SM_ROWS, D_MODEL), F32),
        name="pack_small_grads",
    )(d_lb, d_gain, d_sink, d_rb, d_ln1_g, d_ln1_b, d_ln2_g, d_ln2_b, loss)


def _small_finish(gathered, w, m, v):
    n = len(_SMALL)

    def body(*refs):
        g_ref = refs[0]
        w_refs, m_refs, v_refs = refs[1:1 + n], refs[1 + n:1 + 2 * n], refs[1 + 2 * n:1 + 3 * n]
        outs = refs[1 + 3 * n:]
        loss_ref, tot = outs[0], outs[-1]
        g_out, d_out, m_out, v_out = (outs[1 + k * n:1 + (k + 1) * n] for k in range(4))
        acc = g_ref[0]
        for d in range(1, N_DEV):
            acc = acc + g_ref[d]
        tot[...] = acc
        loss_ref[...] = tot[SM_LOSS:SM_LOSS + 1, 0:1]
        lb = _lower_bound(w_refs[0])
        dl0 = tot[SM_LB:SM_LB + 1, :] * lb * (1.0 - lb)
        grads = (jnp.concatenate([dl0, -dl0], axis=0), tot[SM_GAIN:SM_GAIN + 1, :],
                 tot[SM_SINK:SM_SINK + 1, 0:SWA_HEADS], tot[SM_RB:SM_RB + NUM_BUCKETS, 0:SWA_HEADS],
                 tot[SM_L1G:SM_L1G + 1, :], tot[SM_L1B:SM_L1B + 1, :], tot[SM_L2G:SM_L2G + 1, :], tot[SM_L2B:SM_L2B + 1, :])
        for k, g in enumerate(grads):
            g_out[k][...] = g
            d_out[k][...], m_out[k][...], v_out[k][...] = _adam_step(w_refs[k][...], g, m_refs[k][...], v_refs[k][...])

    vm = pl.BlockSpec(memory_space=pltpu.VMEM)
    shapes = [jax.ShapeDtypeStruct(w[k].shape, F32) for k in _SMALL]
    res = pl.pallas_call(
        body,
        in_specs=[vm] * (1 + 3 * n),
        out_specs=[vm] * (1 + 4 * n),
        out_shape=[jax.ShapeDtypeStruct((1, 1), F32)] + shapes * 4,
        scratch_shapes=[pltpu.VMEM((SM_ROWS, D_MODEL), F32)],
        name="small_finish",
    )(gathered, *[w[k] for k in _SMALL], *[m[k] for k in _SMALL], *[v[k] for k in _SMALL])
    parts = [dict(zip(_SMALL, res[1 + k * n:1 + (k + 1) * n])) for k in range(4)]
    return (res[0], *parts)


def _adam_step(w, g, m, v):
    nm = ADAM_B1 * m + (1.0 - ADAM_B1) * g
    nv = ADAM_B2 * v + (1.0 - ADAM_B2) * jnp.square(g)
    m_hat = nm / (1.0 - ADAM_B1 ** ADAM_STEP)
    v_hat = nv / (1.0 - ADAM_B2 ** ADAM_STEP)
    return -ADAM_LR * (m_hat / (jnp.sqrt(v_hat) + ADAM_EPS) + ADAM_WD * w), nm, nv


def _adamw_group(ws, ms, vs, grads=None, packed=None, name="adamw_group"):
    n = len(ws)
    g_in = list(grads) if packed is None else [packed[0]]

    def body(*refs):
        g_refs = refs[:len(g_in)]
        w_refs, m_refs, v_refs = (refs[len(g_in) + k * n:len(g_in) + (k + 1) * n] for k in range(3))
        outs = refs[len(g_in) + 3 * n:]
        for k in range(n):
            if packed is None:
                g = g_refs[k][...]
            else:
                r0, rows = packed[1][k]
                g = g_refs[0][r0:r0 + rows, :]
            outs[k][...] = g
            outs[n + k][...], outs[2 * n + k][...], outs[3 * n + k][...] = _adam_step(
                w_refs[k][...], g, m_refs[k][...], v_refs[k][...])

    vm = pl.BlockSpec(memory_space=pltpu.VMEM)
    shapes = [jax.ShapeDtypeStruct(a.shape, F32) for a in ws]
    res = pl.pallas_call(
        body,
        in_specs=[vm] * (len(g_in) + 3 * n),
        out_specs=[vm] * (4 * n),
        out_shape=shapes * 4,
        compiler_params=pltpu.CompilerParams(vmem_limit_bytes=VMEM_LIMIT),
        name=name,
    )(*g_in, *ws, *ms, *vs)
    return [res[k * n:(k + 1) * n] for k in range(4)]


_WEIGHTS = ("w_in", "lb_logits", "hg_norm_gain", "swa_sinks", "rel_bias", "w_mem_kv", "w_branch_hg", "w_branch_swa",
            "w_branch_mem", "w_out", "ln1_g", "ln1_b", "w_up", "w_down", "ln2_g", "ln2_b")


def kernel(x, mem, w_in, lb_logits, hg_norm_gain, swa_sinks, rel_bias, w_mem_kv, w_branch_hg, w_branch_swa, w_branch_mem, w_out, ln1_g, ln1_b, w_up, w_down, ln2_g, ln2_b, loss_target, m_w_in, m_lb_logits, m_hg_norm_gain, m_swa_sinks, m_rel_bias, m_w_mem_kv, m_w_branch_hg, m_w_branch_swa, m_w_branch_mem, m_w_out, m_ln1_g, m_ln1_b, m_w_up, m_w_down, m_ln2_g, m_ln2_b, v_w_in, v_lb_logits, v_hg_norm_gain, v_swa_sinks, v_rel_bias, v_w_mem_kv, v_w_branch_hg, v_w_branch_swa, v_w_branch_mem, v_w_out, v_ln1_g, v_ln1_b, v_w_up, v_w_down, v_ln2_g, v_ln2_b):
    w = dict(w_in=w_in, lb_logits=lb_logits, hg_norm_gain=hg_norm_gain, swa_sinks=swa_sinks, rel_bias=rel_bias,
             w_mem_kv=w_mem_kv, w_branch_hg=w_branch_hg, w_branch_swa=w_branch_swa, w_branch_mem=w_branch_mem,
             w_out=w_out, ln1_g=ln1_g, ln1_b=ln1_b, w_up=w_up, w_down=w_down, ln2_g=ln2_g, ln2_b=ln2_b)
    mom = dict(w_in=m_w_in, lb_logits=m_lb_logits, hg_norm_gain=m_hg_norm_gain, swa_sinks=m_swa_sinks, rel_bias=m_rel_bias,
               w_mem_kv=m_w_mem_kv, w_branch_hg=m_w_branch_hg, w_branch_swa=m_w_branch_swa, w_branch_mem=m_w_branch_mem,
               w_out=m_w_out, ln1_g=m_ln1_g, ln1_b=m_ln1_b, w_up=m_w_up, w_down=m_w_down, ln2_g=m_ln2_g, ln2_b=m_ln2_b)
    var = dict(w_in=v_w_in, lb_logits=v_lb_logits, hg_norm_gain=v_hg_norm_gain, swa_sinks=v_swa_sinks, rel_bias=v_rel_bias,
               w_mem_kv=v_w_mem_kv, w_branch_hg=v_w_branch_hg, w_branch_swa=v_w_branch_swa, w_branch_mem=v_w_branch_mem,
               w_out=v_w_out, ln1_g=v_ln1_g, ln1_b=v_ln1_b, w_up=v_w_up, w_down=v_w_down, ln2_g=v_ln2_g, ln2_b=v_ln2_b)
    xc, yc, cc = _coords()

    p1 = _bf(w_in[0].T)
    me = 4 * xc + 2 * yc + cc
    g1, p2 = _all_gather_weights(p1, [(w_down[0], False), (w_up[0], True), (w_branch_hg[0], False), (w_branch_swa[0], False),
                                      (w_branch_mem[0], False), (w_out[0], False), (w_mem_kv[0], True)])
    land2 = lax.dynamic_update_slice(lax.empty((N_DEV, R_OTHER, D_MODEL), BF16), p2[None], (me, 0, 0))
    ag2 = _direct_start(p2, land2, gather=True, name="gather_other_weights_start")

    def other_weights(after):
        return _direct_wait(*ag2[:4], after, gather=True, name="gather_other_weights_wait")[1]

    blocks = lambda a: a.reshape(N_DEV, a.shape[0] // N_DEV, D_MODEL)
    started = {}

    def send_other_grads(part):
        started["others"] = _direct_start(part, lax.empty((N_DEV - 1, R_OTHER, D_MODEL), BF16), gather=False,
                                          name="scatter_other_grads_start")
        return started["others"][4]

    me1 = me.reshape(1).astype(jnp.int32)
    grads, delta, new_m, new_v = {}, {}, {}, {}

    def send_small_grads(packed):
        land = lax.dynamic_update_slice(lax.empty((N_DEV, SM_ROWS, D_MODEL), F32), packed[None], (me, 0, 0))
        started["small"] = _direct_start(packed, land, gather=True, name="gather_small_grads_start")
        return started["small"][4]

    def send_win_grad(g, after):
        started["win"] = _direct_start(blocks(g), lax.empty((N_DEV - 1, IN_SHARD, D_MODEL), BF16), gather=False,
                                       name="scatter_w_in_grad_start", after=after)
        mine2, landed2 = _direct_wait(*started["others"][:4], started["win"][4], gather=False,
                                      name="scatter_other_grads_wait")
        gs2 = _sum_partials(mine2, landed2, me1, tr=R_OTHER // 2, name="sum_other_grads")
        rowwise = (("w_down", R_DN, R_UP), ("w_branch_hg", R_BH, R_BS), ("w_branch_swa", R_BS, R_BM),
                   ("w_branch_mem", R_BM, R_OUT), ("w_out", R_OUT, R_KV))
        colwise = (("w_up", R_UP, R_BH), ("w_mem_kv", R_KV, R_OTHER))
        for names, kw in (([n for n, _, _ in rowwise], dict(packed=(gs2, [(lo, hi - lo) for _, lo, hi in rowwise]))),
                          ([n for n, _, _ in colwise], dict(grads=[gs2[lo:hi].T for _, lo, hi in colwise]))):
            res = _adamw_group([w[n][0] for n in names], [mom[n][0] for n in names], [var[n][0] for n in names],
                               name="adamw_" + "_".join(n[2:] for n in names), **kw)
            for dst, vals in zip((grads, delta, new_m, new_v), res):
                dst.update(zip(names, vals))
        return (new_v["w_down"], new_v["w_up"])

    grad_x = _local_step(
        x[0], mem[0], loss_target[0], lb_logits, hg_norm_gain, swa_sinks, rel_bias, ln1_g, ln1_b, ln2_g, ln2_b,
        g1.reshape(IN_COLS, D_MODEL), ag2[4], other_weights, send_other_grads, send_small_grads, send_win_grad)

    mine1, landed1 = _direct_wait(*started["win"][:4], grad_x, gather=False, name="scatter_w_in_grad_wait")
    g_win_t, d_t, m_t, v_t = _sum_partials(mine1, landed1, me1, tr=IN_SHARD // 2, name="sum_adamw_w_in",
                                           wmv=(w_in[0].T, m_w_in[0].T, v_w_in[0].T))
    grads["w_in"], delta["w_in"], new_m["w_in"], new_v["w_in"] = g_win_t.T, d_t.T, m_t.T, v_t.T

    _, gathered = _direct_wait(*started["small"][:4], grad_x, gather=True, name="gather_small_grads_wait")
    loss, g_s, d_s, m_s, v_s = _small_finish(gathered, w, mom, var)
    for dst, src in ((grads, g_s), (delta, d_s), (new_m, m_s), (new_v, v_s)):
        dst.update(src)

    def shaped(d, name):
        return d[name].reshape(w[name].shape)

    return (loss.reshape(()), grad_x[None], *[shaped(grads, n) for n in _WEIGHTS], *[shaped(delta, n) for n in _WEIGHTS],
            *[shaped(new_m, n) for n in _WEIGHTS], *[shaped(new_v, n) for n in _WEIGHTS])
```

```python
import functools
import math

import jax
import jax.numpy as jnp
from jax import lax
from jax.experimental import pallas as pl
from jax.experimental.pallas import tpu as pltpu

F32 = jnp.float32
BF16 = jnp.bfloat16

D_MODEL = 1024
MEM_LEN = 256
HG_HEADS = 8
HG_DK = 128
HG_CHUNK = 64
SWA_HEADS = 16
SWA_HEAD_DIM = 64
SWA_BLOCK = 128
SWA_WINDOW = 128
MEM_HEADS = 4
MEM_HEAD_DIM = 256
NUM_BUCKETS = 32
MAX_DISTANCE = 128
D_FF = 4096
LN_EPS = 1e-5
RMS_EPS = 1e-6
ALPHA = 2.0 ** 0.25
N_DEV = 8

C_HQ, C_HF, C_HI, C_HG, C_SQ, C_SK, C_SV, C_MQ, C_GL = 0, 1024, 2048, 3072, 4096, 5120, 5248, 5376, 6400
IN_COLS = 9472
IN_SHARD = IN_COLS // N_DEV
Z_HG, Z_SQ, Z_SK, Z_MQ, Z_REST = 0, C_SQ - C_HG, C_SK - C_HG, C_MQ - C_HG, C_GL - C_HG

ADAM_LR = 0.001
ADAM_B1 = 0.9
ADAM_B2 = 0.999
ADAM_EPS = 1e-08
ADAM_WD = 0.01
ADAM_STEP = 10

VMEM_LIMIT = 58 * 1024 * 1024

R_DN, R_UP, R_BH, R_BS, R_BM, R_OUT, R_KV, R_OTHER = 0, 512, 1024, 1152, 1280, 1408, 1536, 1792

SM_LB, SM_GAIN, SM_SINK, SM_L1G, SM_L1B, SM_L2G, SM_L2B, SM_LOSS, SM_RB, SM_ROWS = 0, 2, 3, 4, 5, 6, 7, 8, 16, 48


def _bf(v):
    return v.astype(BF16)


def _f32(v):
    return v.astype(F32)


def _dot(a, b):
    return jnp.dot(a, b, preferred_element_type=F32)


def _dot_nt(a, b):
    return lax.dot_general(a, b, (((1,), (1,)), ((), ())), preferred_element_type=F32)


def _dot_tn(a, b):
    return lax.dot_general(a, b, (((0,), (0,)), ((), ())), preferred_element_type=F32)


def _sig(v):
    return 0.5 * jnp.tanh(0.5 * v) + 0.5


def _cparams(*sem):
    return pltpu.CompilerParams(dimension_semantics=sem, vmem_limit_bytes=VMEM_LIMIT)


def _const_spec(shape):
    nd = len(shape)
    return pl.BlockSpec(shape, lambda *_: (0,) * nd, pipeline_mode=pl.Buffered(1))


def _dep_spec():
    return pl.BlockSpec((8, 128), lambda *_: (0, 0))


def _in_proj(x, win_t, dep, *, tm):
    S = x.shape[0]

    def body(x_ref, w_ref, dep_ref, qfv_ref, z_ref, gl_ref, xb_ref):
        del dep_ref
        xb = _bf(x_ref[...])
        xb_ref[...] = xb
        for c0 in range(0, C_HG, 1024):
            qfv_ref[:, c0:c0 + 1024] = _dot_nt(xb, w_ref[c0:c0 + 1024, :])
        for c0 in range(0, Z_REST, Z_REST // 2):
            z_ref[:, c0:c0 + Z_REST // 2] = _bf(_dot_nt(xb, w_ref[C_HG + c0:C_HG + c0 + Z_REST // 2, :]))
        for c0 in range(0, IN_COLS - C_GL, 1024):
            gl_ref[:, c0:c0 + 1024] = _bf(_dot_nt(xb, w_ref[C_GL + c0:C_GL + c0 + 1024, :]))

    row = lambda w: pl.BlockSpec((tm, w), lambda i: (i, 0))
    return pl.pallas_call(
        body,
        grid=(S // tm,),
        in_specs=[row(D_MODEL), _const_spec(win_t.shape), _dep_spec()],
        out_specs=[row(C_HG), row(Z_REST), row(IN_COLS - C_GL), row(D_MODEL)],
        out_shape=[jax.ShapeDtypeStruct((S, C_HG), F32), jax.ShapeDtypeStruct((S, Z_REST), BF16),
                   jax.ShapeDtypeStruct((S, IN_COLS - C_GL), BF16), jax.ShapeDtypeStruct((S, D_MODEL), BF16)],
        compiler_params=_cparams("parallel"),
        name="in_proj",
    )(x, win_t, dep)


def _placement(into, tm, N, M, out_dtype):
    if into is None:
        return (lambda i: (i, 0)), (tm, N), jax.ShapeDtypeStruct((M, N), out_dtype), (), {}
    dest, block, index = into
    assert math.prod(block) == tm * N and dest.dtype == out_dtype
    return index, block, jax.ShapeDtypeStruct(dest.shape, dest.dtype), (dest,), {2: 0}


def _mm_tn_resident(a, b, *, tm, kc, name, out_dtype, into=None):
    K, M = a.shape
    N = b.shape[1]
    nk = K // kc
    index, block, out_shape, extra, aliases = _placement(into, tm, N, M, out_dtype)

    def body(a_ref, b_ref, *rest):
        o_ref = rest[-1]
        acc = jnp.zeros((tm, N), F32)
        for kk in range(nk):
            sl = pl.ds(kk * kc, kc)
            acc = acc + _dot_tn(_bf(a_ref[sl, :]), _bf(b_ref[sl, :]))
        o_ref[...] = acc.astype(o_ref.dtype).reshape(block)

    return pl.pallas_call(
        body,
        grid=(M // tm,),
        in_specs=[pl.BlockSpec((K, tm), lambda i: (0, i)), _const_spec((K, N))] + [ANY for _ in extra],
        out_specs=pl.BlockSpec(block, index),
        out_shape=out_shape,
        input_output_aliases=aliases,
        compiler_params=_cparams("parallel"),
        name=name,
    )(a, b, *extra)


def _mm_tn(a, b, *, kc, name, out_dtype=F32, into=None, tm=None):
    K, M = a.shape
    N = b.shape[1]
    if M > 1024 or tm is not None:
        return _mm_tn_resident(a, b, tm=tm or 256, kc=min(kc, 1024), name=name, out_dtype=out_dtype, into=into)
    tm = M
    if a.dtype == BF16 and b.dtype == BF16 and K % (2 * kc) == 0:
        kc = 2 * kc
    nk = K // kc
    index, block, out_shape, extra, aliases = _placement(into, tm, N, M, out_dtype)

    def body(a_ref, b_ref, *rest):
        o_ref, acc = rest[-2], rest[-1]
        k = pl.program_id(1)
        part = _dot_tn(_bf(a_ref[...]), _bf(b_ref[...]))

        @pl.when(k == 0)
        def _():
            acc[...] = part

        @pl.when(k > 0)
        def _():
            acc[...] += part

        @pl.when(k == nk - 1)
        def _():
            o_ref[...] = acc[...].astype(o_ref.dtype).reshape(block)

    return pl.pallas_call(
        body,
        grid=(M // tm, nk),
        in_specs=[pl.BlockSpec((kc, tm), lambda i, k: (k, i)), pl.BlockSpec((kc, N), lambda i, k: (k, 0))]
        + [ANY for _ in extra],
        out_specs=pl.BlockSpec(block, lambda i, k: index(i)),
        out_shape=out_shape,
        input_output_aliases=aliases,
        scratch_shapes=[pltpu.VMEM((tm, N), F32)],
        compiler_params=_cparams("parallel", "arbitrary"),
        name=name,
    )(a, b, *extra)


def _grad_x(d_qfv, d_hg_gl, d_sq, d_skv, d_mq, win_t, add, deps, *, tm):
    M = add.shape[0]
    pieces = (d_qfv, d_hg_gl, d_sq, d_skv, d_mq)

    def body(qfv_ref, hggl_ref, sq_ref, skv_ref, mq_ref, w_ref, add_ref, *rest):
        o_ref, wq_ref = rest[-2], rest[-1]

        @pl.when(pl.program_id(0) == 0)
        def _():
            for h in range(HG_HEADS):
                for p in range(3):
                    wq_ref[(h * 3 + p) * HG_DK:(h * 3 + p + 1) * HG_DK, :] = (
                        w_ref[p * D_MODEL + h * HG_DK:p * D_MODEL + (h + 1) * HG_DK, :])

        acc = add_ref[...] + _dot(qfv_ref[...], wq_ref[...])
        acc = acc + _dot(hggl_ref[:, 0:1024], w_ref[C_HG:C_SQ, :])
        acc = acc + _dot(hggl_ref[:, 1024:4096], w_ref[C_GL:IN_COLS, :])
        acc = acc + _dot(sq_ref[...], w_ref[C_SQ:C_SK, :])
        acc = acc + _dot(skv_ref[...], w_ref[C_SK:C_MQ, :])
        o_ref[...] = acc + _dot(mq_ref[...], w_ref[C_MQ:C_GL, :])

    return pl.pallas_call(
        body,
        grid=(M // tm,),
        in_specs=[pl.BlockSpec((tm, p.shape[1]), lambda i: (i, 0)) for p in pieces]
        + [_const_spec(win_t.shape), pl.BlockSpec((tm, D_MODEL), lambda i: (i, 0))]
        + [_dep_spec() for _ in deps],
        out_specs=pl.BlockSpec((tm, D_MODEL), lambda i: (i, 0)),
        out_shape=jax.ShapeDtypeStruct((M, D_MODEL), F32),
        scratch_shapes=[pltpu.VMEM((C_HG, D_MODEL), win_t.dtype)],
        compiler_params=_cparams("arbitrary"),
        name="grad_x",
    )(*pieces, win_t, add, *deps)


def _lower_bound(lbl_ref):
    l0 = lbl_ref[0:1, :]
    l1 = lbl_ref[1:2, :]
    mx = jnp.maximum(l0, l1)
    e0 = jnp.exp(l0 - mx)
    e1 = jnp.exp(l1 - mx)
    return e0 / (e0 + e1)


def _tri(lower):
    r = lax.broadcasted_iota(jnp.int32, (HG_CHUNK, HG_CHUNK), 0)
    c = lax.broadcasted_iota(jnp.int32, (HG_CHUNK, HG_CHUNK), 1)
    return (r >= c) if lower else (r <= c)


def _hg_gates(fl, lb):
    sg = _sig(fl)
    f = lb + (1.0 - lb) * sg
    return sg, f, jnp.log(f), 1.0 - f


def _scan_rows(v, reverse=False):
    row = lax.broadcasted_iota(jnp.int32, v.shape, 0)
    s = 1
    while s < HG_CHUNK:
        if reverse:
            v = v + jnp.where(row < HG_CHUNK - s, pltpu.roll(v, HG_CHUNK - s, 0), 0.0)
        else:
            v = v + jnp.where(row >= s, pltpu.roll(v, s, 0), 0.0)
        s *= 2
    return v


def _hgrn_fwd(zmain, lb_logits, *, T):
    S = zmain.shape[0]
    nc = T // HG_CHUNK

    def body(q_ref, f_ref, v_ref, lbl_ref, o_ref, st_ref, state):
        @pl.when(pl.program_id(1) == 0)
        def _():
            state[...] = jnp.zeros_like(state)

        lb = _lower_bound(lbl_ref)
        tril = _tri(True)
        qis, updates, decays, intra = [], [], [], []
        for c in range(nc):
            sl = pl.ds(c * HG_CHUNK, HG_CHUNK)
            _, _, g, k = _hg_gates(_f32(f_ref[sl, :]), lb)
            b = _scan_rows(g)
            bl = jnp.sum(g, axis=0, keepdims=True)
            qi = _bf(_f32(q_ref[sl, :]) * jnp.exp(b))
            ki = _bf(k * jnp.exp(-b))
            ko = _bf(k * jnp.exp(bl - b))
            vb = _bf(v_ref[sl, :])
            att = jnp.where(tril, _dot_nt(qi, ki), 0.0)
            intra.append(_dot(_bf(att), vb))
            qis.append(qi)
            updates.append(_dot_tn(vb, ko))
            decays.append(jnp.exp(bl))
        st = state[...]
        for c in range(nc):
            st_ref[0, c] = st
            o_ref[pl.ds(c * HG_CHUNK, HG_CHUNK), :] = intra[c] + _dot_nt(qis[c], _bf(st))
            st = st * decays[c] + updates[c]
        state[...] = st

    col = lambda base: pl.BlockSpec((T, HG_DK), lambda h, t: (t, base + h))
    return pl.pallas_call(
        body,
        grid=(HG_HEADS, S // T),
        in_specs=[col(0), col(8), col(16), pl.BlockSpec((2, HG_DK), lambda h, t: (0, h))],
        out_specs=[
            pl.BlockSpec((T, HG_DK), lambda h, t: (t, h)),
            pl.BlockSpec((1, nc, HG_DK, HG_DK), lambda h, t: (h, t, 0, 0)),
        ],
        out_shape=[
            jax.ShapeDtypeStruct((S, D_MODEL), F32),
            jax.ShapeDtypeStruct((HG_HEADS, S // HG_CHUNK, HG_DK, HG_DK), F32),
        ],
        scratch_shapes=[pltpu.VMEM((HG_DK, HG_DK), F32)],
        compiler_params=_cparams("parallel", "arbitrary"),
        name="hgrn_fwd",
    )(zmain, zmain, zmain, lb_logits)


def _hgrn_bwd(zmain, lb_logits, states, d_o, *, T):
    S = zmain.shape[0]
    nc = T // HG_CHUNK
    nt = S // T

    def body(q_ref, f_ref, v_ref, lbl_ref, st_ref, do_ref, dz_ref, dlb_ref, dstate):
        @pl.when(pl.program_id(1) == 0)
        def _():
            dstate[...] = jnp.zeros_like(dstate)
            dlb_ref[...] = jnp.zeros_like(dlb_ref)

        lb = _lower_bound(lbl_ref)
        tril = _tri(True)
        last_row = lax.broadcasted_iota(jnp.int32, (HG_CHUNK, HG_DK), 0) == HG_CHUNK - 1
        saved = []
        for c in range(nc):
            sl = pl.ds(c * HG_CHUNK, HG_CHUNK)
            sg, f, g, k = _hg_gates(_f32(f_ref[sl, :]), lb)
            b = _scan_rows(g)
            bl = jnp.sum(g, axis=0, keepdims=True)
            eb = jnp.exp(b)
            enb = jnp.exp(-b)
            eo = jnp.exp(bl - b)
            q_in = _f32(q_ref[sl, :]) * eb
            k_in = k * enb
            k_out = k * eo
            qi, ki, ko = _bf(q_in), _bf(k_in), _bf(k_out)
            vb = _bf(v_ref[sl, :])
            dob = do_ref[sl, :]
            att = jnp.where(tril, _dot_nt(qi, ki), 0.0)
            d_att = _bf(jnp.where(tril, _dot_nt(dob, vb), 0.0))
            d_kin = _dot_tn(d_att, qi)
            saved.append(dict(
                sg=sg, f=f, eb=eb, enb=enb, eo=eo, ebl=jnp.exp(bl), k_out=k_out, ko=ko, vb=vb, dob=dob,
                d_v=_dot_tn(_bf(att), dob), d_qin=_dot(d_att, ki), d_kin=d_kin,
                qk=(q_in, k_in), d_state=_dot_tn(dob, qi)))
        dst = dstate[...]
        dsts = [None] * nc
        for c in reversed(range(nc)):
            dsts[c] = dst
            dst = dst * saved[c]["ebl"] + saved[c]["d_state"]
        dstate[...] = dst
        dlb = jnp.zeros((1, HG_DK), F32)
        for c in range(nc):
            sl = pl.ds(c * HG_CHUNK, HG_CHUNK)
            s = saved[c]
            q_in, k_in = s["qk"]
            st = st_ref[0, c]
            dstb = _bf(dsts[c])
            d_v = s["d_v"] + _dot_nt(s["ko"], dstb)
            d_qin = s["d_qin"] + _dot(s["dob"], _bf(st))
            d_kout = _dot(s["vb"], dstb)
            d_decay = jnp.sum(dsts[c] * st, axis=0, keepdims=True)
            kk = d_kout * s["k_out"]
            d_b = d_qin * q_in - s["d_kin"] * k_in - kk
            d_bl = jnp.sum(kk, axis=0, keepdims=True) + d_decay * s["ebl"]
            d_g = _scan_rows(d_b + jnp.where(last_row, d_bl, 0.0), reverse=True)
            d_f = d_g / s["f"] - (s["d_kin"] * s["enb"] + d_kout * s["eo"])
            dz_ref[sl, 0:HG_DK] = _bf(d_qin * s["eb"])
            dz_ref[sl, HG_DK:2 * HG_DK] = _bf(d_f * (1.0 - lb) * s["sg"] * (1.0 - s["sg"]))
            dz_ref[sl, 2 * HG_DK:3 * HG_DK] = _bf(d_v)
            dlb = dlb + jnp.sum(d_f * (1.0 - s["sg"]), axis=0, keepdims=True)
        dlb_ref[...] += dlb

    rev = lambda base: pl.BlockSpec((T, HG_DK), lambda h, t: (nt - 1 - t, base + h))
    outc = pl.BlockSpec((T, HG_DK), lambda h, t: (nt - 1 - t, h))
    return pl.pallas_call(
        body,
        grid=(HG_HEADS, nt),
        in_specs=[
            rev(0), rev(8), rev(16),
            pl.BlockSpec((2, HG_DK), lambda h, t: (0, h)),
            pl.BlockSpec((1, nc, HG_DK, HG_DK), lambda h, t: (h, nt - 1 - t, 0, 0)),
            outc,
        ],
        out_specs=[pl.BlockSpec((T, 3 * HG_DK), lambda h, t: (nt - 1 - t, h)),
                   pl.BlockSpec((1, HG_DK), lambda h, t: (0, h))],
        out_shape=[jax.ShapeDtypeStruct((S, 3 * D_MODEL), BF16), jax.ShapeDtypeStruct((1, D_MODEL), F32)],
        scratch_shapes=[pltpu.VMEM((HG_DK, HG_DK), F32)],
        compiler_params=_cparams("parallel", "arbitrary"),
        name="hgrn_bwd",
    )(zmain, zmain, zmain, lb_logits, states, d_o)


def _t5_bucket_table():
    qi = jnp.arange(SWA_BLOCK)[:, None] + SWA_BLOCK
    kj = jnp.arange(2 * SWA_BLOCK)[None, :]
    n = jnp.clip(qi - kj, 0, SWA_WINDOW - 1)
    max_exact = NUM_BUCKETS // 2
    nf = jnp.maximum(n, 1).astype(F32)
    large = max_exact + (jnp.log(nf / max_exact) / math.log(MAX_DISTANCE / max_exact)
                         * (NUM_BUCKETS - max_exact)).astype(jnp.int32)
    large = jnp.minimum(large, NUM_BUCKETS - 1)
    return jnp.where(n < max_exact, n, large).astype(jnp.int32)


SWA_ROWS = 32
MERGE_GROUPS = 1


def _swa_bias_init(bias, bucket_ref, rb_ref):
    bk = bucket_ref[...]
    qi = lax.broadcasted_iota(jnp.int32, bk.shape, 0) + SWA_BLOCK
    kj = lax.broadcasted_iota(jnp.int32, bk.shape, 1)
    band = (qi - kj >= 0) & (qi - kj < SWA_WINDOW)
    for h in range(SWA_HEADS):
        def sel(b, acc, h=h):
            return jnp.where(bk == b, rb_ref[b, h], acc)
        t = lax.fori_loop(0, NUM_BUCKETS, sel, jnp.zeros(bk.shape, F32))
        bias[1, h] = jnp.where(band, t, -jnp.inf)
        bias[0, h] = jnp.where(band & (kj >= SWA_BLOCK), t, -jnp.inf)


def _lane_halves(t, kv_head):
    lane = lax.broadcasted_iota(jnp.int32, t.shape, 1)
    rolled = pltpu.roll(t, 64, 1)
    zero = jnp.zeros_like(t)
    if kv_head == 0:
        return jnp.where(lane < 64, t, zero), jnp.where(lane >= 64, rolled, zero)
    return jnp.where(lane < 64, rolled, zero), jnp.where(lane >= 64, t, zero)


def _swa_zero_key0(t):
    return jnp.where(lax.broadcasted_iota(jnp.int32, t.shape, 0) == 0, jnp.zeros_like(t), t)


def _swa_probs(s, masked_bias, sink):
    s = s + masked_bias
    m = jnp.maximum(jnp.max(s, axis=-1, keepdims=True), sink)
    p = jnp.exp(s - m)
    es = jnp.exp(sink - m)
    inv = 1.0 / (jnp.sum(p, axis=-1, keepdims=True) + es)
    return p * inv, es * inv


def _swa_fwd(zmain, bucket, rel_bias, sinks):
    S = zmain.shape[0]
    nb = S // SWA_BLOCK
    scale = SWA_HEAD_DIM ** -0.5

    def body(q_ref, kvc_ref, kvp_ref, bucket_ref, rb_ref, sk_ref, o_ref, p_ref, bias):
        n = pl.program_id(0)

        @pl.when(n == 0)
        def _():
            _swa_bias_init(bias, bucket_ref, rb_ref)

        later = jnp.minimum(n, 1)
        kk = _bf(jnp.concatenate([kvp_ref[:, 0:128], kvc_ref[:, 0:128]], axis=0))
        vv = _swa_zero_key0(_bf(jnp.concatenate([kvp_ref[:, 128:256], kvc_ref[:, 128:256]], axis=0)))
        first_col = lax.broadcasted_iota(jnp.int32, (SWA_ROWS, 2 * SWA_BLOCK), 1) == 0
        scores, values = {}, {}
        for kvh in range(2):
            qst = _bf(jnp.concatenate([q_ref[:, pl.ds((kvh * 4 + jj) * 128, 128)] for jj in range(4)], axis=0) * scale)
            values[kvh] = _lane_halves(vv, kvh)
            for odd, kx in enumerate(_lane_halves(kk, kvh)):
                scores[kvh, odd] = _dot_nt(qst, kx)
        probs = {}
        for (kvh, odd), s in scores.items():
            parts = []
            for jj in range(4):
                h = 2 * (kvh * 4 + jj) + odd
                for r0 in range(0, SWA_BLOCK, SWA_ROWS):
                    p, ps = _swa_probs(s[jj * SWA_BLOCK + r0:jj * SWA_BLOCK + r0 + SWA_ROWS],
                                       bias[later, h, pl.ds(r0, SWA_ROWS), :], sk_ref[0, h])
                    part = _bf(jnp.where(first_col, ps, p))
                    p_ref[pl.ds(r0, SWA_ROWS), pl.ds(h * 2 * SWA_BLOCK, 2 * SWA_BLOCK)] = part
                    parts.append(part)
            probs[kvh, odd] = jnp.concatenate(parts, axis=0)
        for kvh in range(2):
            ost = _dot(probs[kvh, 0], values[kvh][0]) + _dot(probs[kvh, 1], values[kvh][1])
            for jj in range(4):
                o_ref[:, pl.ds((kvh * 4 + jj) * 128, 128)] = ost[jj * SWA_BLOCK:(jj + 1) * SWA_BLOCK]

    smem = pl.BlockSpec(memory_space=pltpu.SMEM)
    return pl.pallas_call(
        body,
        grid=(nb,),
        in_specs=[
            pl.BlockSpec((SWA_BLOCK, 1024), lambda n: (n, Z_SQ // 1024)),
            pl.BlockSpec((SWA_BLOCK, 256), lambda n: (n, Z_SK // 256)),
            pl.BlockSpec((SWA_BLOCK, 256), lambda n: (jnp.maximum(n - 1, 0), Z_SK // 256)),
            _const_spec((SWA_BLOCK, 2 * SWA_BLOCK)), smem, smem,
        ],
        out_specs=[pl.BlockSpec((SWA_BLOCK, 1024), lambda n: (n, 0)),
                   pl.BlockSpec((SWA_BLOCK, SWA_HEADS * 2 * SWA_BLOCK), lambda n: (n, 0))],
        out_shape=[jax.ShapeDtypeStruct((S, 1024), F32),
                   jax.ShapeDtypeStruct((S, SWA_HEADS * 2 * SWA_BLOCK), BF16)],
        scratch_shapes=[pltpu.VMEM((2, SWA_HEADS, SWA_BLOCK, 2 * SWA_BLOCK), F32)],
        compiler_params=_cparams("arbitrary"),
        name="swa_fwd",
    )(zmain, zmain, zmain, bucket, rel_bias, sinks)


def _swa_bwd(zmain, o_b, probs, d_o, bucket, dep):
    S = zmain.shape[0]
    nb = S // SWA_BLOCK
    scale = SWA_HEAD_DIM ** -0.5

    def body(q_ref, kvc_ref, kvp_ref, o_ref, p_ref, do_ref, bucket_ref, dep_ref,
             dq_ref, dkv_ref, drb_ref, dsk_ref, dbias, carry):
        del dep_ref
        n = pl.program_id(0)

        @pl.when(n == 0)
        def _():
            dbias[...] = jnp.zeros_like(dbias)
            carry[...] = jnp.zeros_like(carry)

        @pl.when(n < nb)
        def _():
            kk = _swa_zero_key0(_bf(jnp.concatenate([kvp_ref[:, 0:128], kvc_ref[:, 0:128]], axis=0)))
            vv = _swa_zero_key0(_bf(jnp.concatenate([kvp_ref[:, 128:256], kvc_ref[:, 128:256]], axis=0)))
            lane = lax.broadcasted_iota(jnp.int32, (2 * SWA_BLOCK, 128), 1)
            lane_q = lax.broadcasted_iota(jnp.int32, (4 * SWA_BLOCK, 128), 1)
            pair_cols = {kvh: [pl.ds((kvh * 4 + jj) * 128, 128) for jj in range(4)] for kvh in range(2)}
            qst, dost, ks, d_p, delta = {}, {}, {}, {}, {}
            for kvh in range(2):
                qst[kvh] = _bf(jnp.concatenate([q_ref[:, cl] for cl in pair_cols[kvh]], axis=0) * scale)
                dost[kvh] = jnp.concatenate([do_ref[:, cl] for cl in pair_cols[kvh]], axis=0)
                prod = dost[kvh].astype(F32) * jnp.concatenate([o_ref[:, cl] for cl in pair_cols[kvh]], axis=0)
                ks[kvh] = _lane_halves(kk, kvh)
                for odd, vx in enumerate(_lane_halves(vv, kvh)):
                    keep = (lane_q >= 64) if odd else (lane_q < 64)
                    delta[kvh, odd] = jnp.sum(jnp.where(keep, prod, 0.0), axis=-1, keepdims=True)
                    d_p[kvh, odd] = _dot_nt(dost[kvh], vx)
            pst, dsst = {}, {}
            for (kvh, odd), dp in d_p.items():
                p_parts, ds_parts = [], []
                for jj in range(4):
                    h = 2 * (kvh * 4 + jj) + odd
                    rows = slice(jj * SWA_BLOCK, (jj + 1) * SWA_BLOCK)
                    p = p_ref[:, pl.ds(h * 2 * SWA_BLOCK, 2 * SWA_BLOCK)]
                    ds = _f32(p) * (dp[rows] - delta[kvh, odd][rows])
                    dbias[h] += ds
                    p_parts.append(p)
                    ds_parts.append(_bf(ds))
                pst[kvh, odd] = jnp.concatenate(p_parts, axis=0)
                dsst[kvh, odd] = jnp.concatenate(ds_parts, axis=0)
            dk_parts, dv_parts = [], []
            for kvh in range(2):
                dq_st = _dot(dsst[kvh, 0], ks[kvh][0]) + _dot(dsst[kvh, 1], ks[kvh][1])
                for jj in range(4):
                    dq_ref[:, pair_cols[kvh][jj]] = _bf(dq_st[jj * SWA_BLOCK:(jj + 1) * SWA_BLOCK] * scale)
                zk = jnp.where(lane < 64, _dot_tn(dsst[kvh, 0], qst[kvh]), _dot_tn(dsst[kvh, 1], qst[kvh]))
                zv = jnp.where(lane < 64, _dot_tn(pst[kvh, 0], dost[kvh]), _dot_tn(pst[kvh, 1], dost[kvh]))
                dk_parts.append(zk + pltpu.roll(zk, 64, 1))
                dv_parts.append(zv + pltpu.roll(zv, 64, 1))
            dk = jnp.where(lane < 64, dk_parts[0], dk_parts[1])
            dv = jnp.where(lane < 64, dv_parts[0], dv_parts[1])
            dkv = _swa_zero_key0(jnp.concatenate([dk, dv], axis=1))
            dkv_ref[...] = _bf(carry[...] + dkv[0:SWA_BLOCK])
            carry[...] = dkv[SWA_BLOCK:]

        @pl.when(n == nb)
        def _():
            dkv_ref[...] = _bf(carry[...])
            first_col = lax.broadcasted_iota(jnp.int32, (SWA_BLOCK, 2 * SWA_BLOCK), 1) == 0
            bk = jnp.where(first_col, -1, bucket_ref[...])

            row = lax.broadcasted_iota(jnp.int32, (NUM_BUCKETS, 128), 0)
            lane = lax.broadcasted_iota(jnp.int32, (NUM_BUCKETS, 128), 1)

            def total(v):
                return jnp.sum(jnp.sum(v, axis=1, keepdims=True), axis=0, keepdims=True)

            def per_head(h, acc):
                db = dbias[h]
                d_rb, d_sk = acc
                d_sk = d_sk + jnp.where((row == 0) & (lane == h), total(jnp.where(first_col, db, 0.0)), 0.0)

                def per_bucket(b, d_rb):
                    return d_rb + jnp.where((row == b) & (lane == h), total(jnp.where(bk == b, db, 0.0)), 0.0)

                return lax.fori_loop(0, NUM_BUCKETS, per_bucket, d_rb), d_sk

            zero = jnp.zeros((NUM_BUCKETS, 128), F32)
            d_rb, d_sk = lax.fori_loop(0, SWA_HEADS, per_head, (zero, zero))
            drb_ref[...] = d_rb
            dsk_ref[...] = d_sk[0:8]

    cur = lambda n: jnp.minimum(n, nb - 1)
    prev = lambda n: jnp.maximum(jnp.minimum(n, nb - 1) - 1, 0)
    return pl.pallas_call(
        body,
        grid=(nb + 1,),
        in_specs=[
            pl.BlockSpec((SWA_BLOCK, 1024), lambda n: (cur(n), Z_SQ // 1024)),
            pl.BlockSpec((SWA_BLOCK, 256), lambda n: (cur(n), Z_SK // 256)),
            pl.BlockSpec((SWA_BLOCK, 256), lambda n: (prev(n), Z_SK // 256)),
            pl.BlockSpec((SWA_BLOCK, 1024), lambda n: (cur(n), 0)),
            pl.BlockSpec((SWA_BLOCK, SWA_HEADS * 2 * SWA_BLOCK), lambda n: (cur(n), 0)),
            pl.BlockSpec((SWA_BLOCK, 1024), lambda n: (cur(n), 0)),
            _const_spec((SWA_BLOCK, 2 * SWA_BLOCK)), _dep_spec(),
        ],
        out_specs=[
            pl.BlockSpec((SWA_BLOCK, 1024), lambda n: (cur(n), 0)),
            pl.BlockSpec((SWA_BLOCK, 256), lambda n: (jnp.maximum(n - 1, 0), 0)),
            pl.BlockSpec((NUM_BUCKETS, 128), lambda n: (0, 0)),
            pl.BlockSpec((8, 128), lambda n: (0, 0)),
        ],
        out_shape=[
            jax.ShapeDtypeStruct((S, 1024), BF16),
            jax.ShapeDtypeStruct((S, 256), BF16),
            jax.ShapeDtypeStruct((NUM_BUCKETS, 128), F32),
            jax.ShapeDtypeStruct((8, 128), F32),
        ],
        scratch_shapes=[
            pltpu.VMEM((SWA_HEADS, SWA_BLOCK, 2 * SWA_BLOCK), F32),
            pltpu.VMEM((SWA_BLOCK, 256), F32),
        ],
        compiler_params=_cparams("arbitrary"),
        name="swa_bwd",
    )(zmain, zmain, zmain, o_b, probs, d_o, bucket, dep)


def _mem_q_specs(T):
    return [pl.BlockSpec((T, MEM_HEAD_DIM), lambda t, h=h: (t, Z_MQ // MEM_HEAD_DIM + h)) for h in range(MEM_HEADS)]


def _mem_kv_proj(mem, g2):
    def body(mem_ref, w_ref, o_ref):
        o_ref[...] = _dot_nt(_bf(mem_ref[...]), _rows(w_ref))

    return pl.pallas_call(
        body,
        grid=(1,),
        in_specs=[pl.BlockSpec((MEM_LEN, D_MODEL), lambda i: (0, 0)), _gathered_spec(R_KV, R_OTHER)],
        out_specs=pl.BlockSpec((MEM_LEN, 2048), lambda i: (0, 0)),
        out_shape=jax.ShapeDtypeStruct((MEM_LEN, 2048), F32),
        compiler_params=_cparams("arbitrary"),
        name="mem_kv_proj",
    )(mem, g2)


def _mem_fwd(zmain, mkv, *, T):
    S = zmain.shape[0]

    def body(q0, q1, q2, q3, kv_ref, o_ref, p_ref):
        heads = [pl.ds(h * MEM_HEAD_DIM, MEM_HEAD_DIM) for h in range(MEM_HEADS)]
        scores = [_dot_nt(_bf(q_ref[...] * (MEM_HEAD_DIM ** -0.5)), _bf(kv_ref[:, cols]))
                  for q_ref, cols in zip((q0, q1, q2, q3), heads)]
        probs = []
        for s, cols in zip(scores, heads):
            e = jnp.exp(s - jnp.max(s, axis=-1, keepdims=True))
            pb = _bf(e * (1.0 / jnp.sum(e, axis=-1, keepdims=True)))
            p_ref[:, cols] = pb
            probs.append(pb)
        for h, (pb, cols) in enumerate(zip(probs, heads)):
            o_ref[:, cols] = _dot(pb, _bf(kv_ref[:, pl.ds(1024 + h * MEM_HEAD_DIM, MEM_HEAD_DIM)]))

    row = pl.BlockSpec((T, 1024), lambda t: (t, 0))
    return pl.pallas_call(
        body,
        grid=(S // T,),
        in_specs=_mem_q_specs(T) + [_const_spec((MEM_LEN, 2048))],
        out_specs=[row, row],
        out_shape=[jax.ShapeDtypeStruct((S, 1024), F32), jax.ShapeDtypeStruct((S, 1024), BF16)],
        compiler_params=_cparams("parallel"),
        name="mem_fwd",
    )(zmain, zmain, zmain, zmain, mkv)


def _mem_bwd(zmain, mkv, o_c, probs, d_o, *, T):
    S = zmain.shape[0]
    scale = MEM_HEAD_DIM ** -0.5

    def body(q0, q1, q2, q3, kv_ref, o_ref, p_ref, do_ref, dq_ref, dkv_ref):
        @pl.when(pl.program_id(0) == 0)
        def _():
            dkv_ref[...] = jnp.zeros_like(dkv_ref)

        heads = [(pl.ds(h * MEM_HEAD_DIM, MEM_HEAD_DIM), pl.ds(1024 + h * MEM_HEAD_DIM, MEM_HEAD_DIM))
                 for h in range(MEM_HEADS)]
        d_p = [_dot_nt(do_ref[:, cols], _bf(kv_ref[:, vcols])) for cols, vcols in heads]
        d_s = []
        for dp, (cols, _) in zip(d_p, heads):
            delta = jnp.sum(do_ref[:, cols].astype(F32) * o_ref[:, cols], axis=-1, keepdims=True)
            d_s.append(_bf(_f32(p_ref[:, cols]) * (dp - delta)))
        for ds, q_ref, (cols, vcols) in zip(d_s, (q0, q1, q2, q3), heads):
            dq_ref[:, cols] = _bf(_dot(ds, _bf(kv_ref[:, cols])) * scale)
            dkv_ref[:, cols] += _dot_tn(ds, _bf(q_ref[...] * scale))
            dkv_ref[:, vcols] += _dot_tn(p_ref[:, cols], do_ref[:, cols])

    row = pl.BlockSpec((T, 1024), lambda t: (t, 0))
    return pl.pallas_call(
        body,
        grid=(S // T,),
        in_specs=_mem_q_specs(T) + [_const_spec((MEM_LEN, 2048)), row, row, row],
        out_specs=[row, pl.BlockSpec((MEM_LEN, 2048), lambda t: (0, 0))],
        out_shape=[jax.ShapeDtypeStruct((S, 1024), BF16), jax.ShapeDtypeStruct((MEM_LEN, 2048), F32)],
        compiler_params=_cparams("arbitrary"),
        name="mem_bwd",
    )(zmain, zmain, zmain, zmain, mkv, o_c, probs, d_o)


def _layer_norm(u):
    mu = jnp.mean(u, axis=-1, keepdims=True)
    xc = u - mu
    rstd = lax.rsqrt(jnp.mean(xc * xc, axis=-1, keepdims=True) + LN_EPS)
    return xc * rstd, rstd


def _layer_norm_bwd(dy, gamma, xhat, rstd):
    dxh = dy * gamma
    return rstd * (dxh - jnp.mean(dxh, axis=-1, keepdims=True) - xhat * jnp.mean(dxh * xhat, axis=-1, keepdims=True))


def _merge_stages(rows, oraw_ref, hg_ref, ob_ref, oc_ref, gl_ref, x_ref, gain_ref, wbh, wbs, wbm, wout, g_ref, b_ref,
                  fwd_out=None, bwd=None, saved=None):
    ys, rs = [], []
    for h in range(HG_HEADS):
        oh = oraw_ref[rows, pl.ds(h * HG_DK, HG_DK)]
        r = lax.rsqrt(jnp.mean(oh * oh, axis=-1, keepdims=True) + RMS_EPS)
        ys.append(oh * r)
        rs.append(r)
    y = jnp.concatenate(ys, axis=1)
    hg = _f32(hg_ref[rows, :])
    sg = _sig(hg)
    silu = hg * sg
    gain = gain_ref[...]
    gates = [_sig(_f32(gl_ref[rows, pl.ds(i * 1024, 1024)])) for i in range(3)]
    if saved is None:
        oa = _bf(y * gain * silu)
        pa = _dot(oa, _rows(wbh))
        pb = _dot(_bf(ob_ref[rows, :]), _rows(wbs))
        pc = _dot(_bf(oc_ref[rows, :]), _rows(wbm))
        yield
        m = _bf(gates[0] * pa + gates[1] * pb + gates[2] * pc)
    else:
        pa, pb, pc = (_f32(r[rows, :]) for r in saved[:3])
        m = saved[3][rows, :]
    mix = _dot(m, _rows(wout))
    yield
    xhat, rstd = _layer_norm(ALPHA * x_ref[rows, :] + mix)
    if bwd is None:
        h1 = xhat * g_ref[...] + b_ref[...]
        fwd_out[0][rows, :] = h1
        fwd_out[1][rows, :] = _bf(h1)
        for ref, val in zip(fwd_out[2:], (_bf(pa), _bf(pb), _bf(pc), m, oa)):
            ref[rows, :] = val
        return
    (dh1_ref, dx_ref, du1_ref, dpa_ref, dpb_ref, dpc_ref, doraw_ref, dob_ref, doc_ref, dz_ref,
     dgain_ref, dg_ref, db_ref) = bwd
    dh1 = dh1_ref[rows, :]
    dg_ref[...] += jnp.sum(dh1 * xhat, axis=0, keepdims=True)
    db_ref[...] += jnp.sum(dh1, axis=0, keepdims=True)
    du1 = _layer_norm_bwd(dh1, g_ref[...], xhat, rstd)
    dx_ref[rows, :] = ALPHA * du1
    du1b = _bf(du1)
    du1_ref[rows, :] = du1b
    dm = _dot_nt(du1b, _rows(wout))
    yield
    d_branches = []
    for i, (g, p, dp_ref, w_r) in enumerate(zip(gates, (pa, pb, pc), (dpa_ref, dpb_ref, dpc_ref), (wbh, wbs, wbm))):
        dz_ref[rows, pl.ds((i + 1) * 1024, 1024)] = _bf(dm * p * g * (1.0 - g))
        dp = _bf(dm * g)
        dp_ref[rows, :] = dp
        d_branches.append(_dot_nt(dp, _rows(w_r)))
    yield
    doa, d_ob, d_oc = d_branches
    dob_ref[rows, :] = _bf(d_ob)
    doc_ref[rows, :] = _bf(d_oc)
    t = doa * y
    dgain_ref[...] += jnp.sum(t * silu, axis=0, keepdims=True)
    dz_ref[rows, 0:1024] = _bf(t * gain * sg * (1.0 + hg * (1.0 - sg)))
    dy = doa * gain * silu
    for h in range(HG_HEADS):
        cols = slice(h * HG_DK, (h + 1) * HG_DK)
        yh = y[:, cols]
        dyh = dy[:, cols]
        doraw_ref[rows, pl.ds(h * HG_DK, HG_DK)] = _bf(rs[h] * (dyh - yh * jnp.mean(dyh * yh, axis=-1, keepdims=True)))


def _interleave(chains):
    live = list(chains)
    while live:
        still = []
        for c in live:
            try:
                next(c)
                still.append(c)
            except StopIteration:
                pass
        live = still


def _gathered_spec(lo, hi):
    n = hi - lo
    return pl.BlockSpec((N_DEV, n, D_MODEL), lambda *_: (0, lo // n, 0), pipeline_mode=pl.Buffered(1))


def _rows(w_ref):
    return w_ref[...].reshape(-1, D_MODEL)


def _merge_in_specs(T):
    row = lambda w, c=0: pl.BlockSpec((T, w), lambda i: (i, c))
    vec = pl.BlockSpec((1, D_MODEL), lambda i: (0, 0))
    w = [_gathered_spec(lo, hi) for lo, hi in ((R_BH, R_BS), (R_BS, R_BM), (R_BM, R_OUT), (R_OUT, R_KV))]
    return [row(1024), row(1024, Z_HG // 1024), row(1024), row(1024), row(3072), row(1024), vec, *w, vec, vec]


def _merge_fwd(o_raw, zmain, o_b, o_c, gl, x, gain, wbh, wbs, wbm, wout, ln_g, ln_b, *, T):
    S = x.shape[0]

    def body(*refs):
        ins, outs = refs[:13], refs[13:]
        _interleave(_merge_stages(pl.ds(r0, T // MERGE_GROUPS), *ins, fwd_out=outs)
                    for r0 in range(0, T, T // MERGE_GROUPS))

    row = pl.BlockSpec((T, D_MODEL), lambda i: (i, 0))
    return pl.pallas_call(
        body,
        grid=(S // T,),
        in_specs=_merge_in_specs(T),
        out_specs=[row] * 7,
        out_shape=[jax.ShapeDtypeStruct((S, D_MODEL), F32)] + [jax.ShapeDtypeStruct((S, D_MODEL), BF16)] * 6,
        compiler_params=_cparams("parallel"),
        name="merge_fwd",
    )(o_raw, zmain, o_b, o_c, gl, x, gain, wbh, wbs, wbm, wout, ln_g, ln_b)


def _merge_bwd(d_h1, pa, pb, pc, m, o_raw, zmain, gl, x, gain, wbh, wbs, wbm, wout, ln_g, *, T):
    S = x.shape[0]

    def body(dh1_ref, pa_ref, pb_ref, pc_ref, m_ref, oraw_ref, hg_ref, gl_ref, x_ref, gain_ref, wbh_r, wbs_r, wbm_r, wout_r,
             g_ref, dx_ref, du1_ref, dpa_ref, dpb_ref, dpc_ref, doraw_ref, dob_ref, doc_ref, dz_ref,
             dgain_ref, dg_ref, db_ref):
        @pl.when(pl.program_id(0) == 0)
        def _():
            dgain_ref[...] = jnp.zeros_like(dgain_ref)
            dg_ref[...] = jnp.zeros_like(dg_ref)
            db_ref[...] = jnp.zeros_like(db_ref)

        ins = (oraw_ref, hg_ref, None, None, gl_ref, x_ref, gain_ref, wbh_r, wbs_r, wbm_r, wout_r, g_ref, None)
        bwd = (dh1_ref, dx_ref, du1_ref, dpa_ref, dpb_ref, dpc_ref, doraw_ref, dob_ref, doc_ref, dz_ref,
               dgain_ref, dg_ref, db_ref)
        _interleave([_merge_stages(pl.ds(0, T), *ins, bwd=bwd, saved=(pa_ref, pb_ref, pc_ref, m_ref))])

    row = lambda w, c=0: pl.BlockSpec((T, w), lambda i: (i, c))
    vec = pl.BlockSpec((1, D_MODEL), lambda i: (0, 0))
    w = [_gathered_spec(lo, hi) for lo, hi in ((R_BH, R_BS), (R_BS, R_BM), (R_BM, R_OUT), (R_OUT, R_KV))]
    bshape = jax.ShapeDtypeStruct((S, D_MODEL), BF16)
    vshape = jax.ShapeDtypeStruct((1, D_MODEL), F32)
    return pl.pallas_call(
        body,
        grid=(S // T,),
        in_specs=[row(1024)] * 6 + [row(1024, Z_HG // 1024), row(3072), row(1024), vec, *w, vec],
        out_specs=[row(1024)] * 8 + [row(4096), vec, vec, vec],
        out_shape=[jax.ShapeDtypeStruct((S, D_MODEL), F32)] + [bshape] * 7
        + [jax.ShapeDtypeStruct((S, 4096), BF16), vshape, vshape, vshape],
        compiler_params=_cparams("arbitrary"),
        name="merge_bwd",
    )(d_h1, pa, pb, pc, m, o_raw, zmain, gl, x, gain, wbh, wbs, wbm, wout, ln_g)


def _mlp_fwd_bwd(h1, target, wup_t, wdn, ln_g, ln_b, *, T, FC):
    S = h1.shape[0]
    nf = D_FF // FC
    assert FC == R_BH - R_UP == R_UP - R_DN

    def body(h1_ref, t_ref, wup_ref, wdn_ref, g_ref, b_ref, dh1_ref, a_ref, dup_ref, du2_ref, loss_ref, dg_ref, db_ref, up_scr):
        @pl.when(pl.program_id(0) == 0)
        def _():
            loss_ref[...] = jnp.zeros_like(loss_ref)
            dg_ref[...] = jnp.zeros_like(dg_ref)
            db_ref[...] = jnp.zeros_like(db_ref)

        h1v = h1_ref[...]
        h1b = _bf(h1v)
        ff = jnp.zeros((T, D_MODEL), F32)
        for j in range(nf):
            rows = pl.ds(j * FC, FC)
            up = jnp.maximum(_dot_nt(h1b, wup_ref[j]), 0.0)
            up_scr[:, rows] = _bf(up)
            a = _bf(up * up)
            a_ref[:, rows] = a
            ff = ff + _dot(a, wdn_ref[j])
        xhat, rstd = _layer_norm(ALPHA * h1v + ff)
        gamma = g_ref[...]
        err = xhat * gamma + b_ref[...] - t_ref[...]
        loss_ref[...] += jnp.sum(jnp.sum(err * err, axis=-1, keepdims=True), axis=0, keepdims=True) * (0.5 / D_MODEL)
        dy = err * (1.0 / D_MODEL)
        dg_ref[...] += jnp.sum(dy * xhat, axis=0, keepdims=True)
        db_ref[...] += jnp.sum(dy, axis=0, keepdims=True)
        du2 = _layer_norm_bwd(dy, gamma, xhat, rstd)
        du2b = _bf(du2)
        du2_ref[...] = du2b
        dh1 = ALPHA * du2
        for j in range(nf):
            rows = pl.ds(j * FC, FC)
            dup = _bf(_dot_nt(du2b, wdn_ref[j]) * (2.0 * up_scr[:, rows].astype(F32)))
            dup_ref[:, rows] = dup
            dh1 = dh1 + _dot(dup, wup_ref[j])
        dh1_ref[...] = dh1

    row = lambda w: pl.BlockSpec((T, w), lambda i: (i, 0))
    vec = pl.BlockSpec((1, D_MODEL), lambda i: (0, 0))
    vshape = jax.ShapeDtypeStruct((1, D_MODEL), F32)
    return pl.pallas_call(
        body,
        grid=(S // T,),
        in_specs=[row(1024), row(1024), _gathered_spec(R_UP, R_BH), _gathered_spec(R_DN, R_UP), vec, vec],
        out_specs=[row(1024), row(D_FF), row(D_FF), row(1024), pl.BlockSpec((8, 128), lambda i: (0, 0)), vec, vec],
        out_shape=[
            jax.ShapeDtypeStruct((S, D_MODEL), F32),
            jax.ShapeDtypeStruct((S, D_FF), BF16),
            jax.ShapeDtypeStruct((S, D_FF), BF16),
            jax.ShapeDtypeStruct((S, D_MODEL), BF16),
            jax.ShapeDtypeStruct((8, 128), F32), vshape, vshape,
        ],
        scratch_shapes=[pltpu.VMEM((T, D_FF), BF16)],
        compiler_params=_cparams("arbitrary"),
        name="mlp_fwd_bwd",
    )(h1, target, wup_t, wdn, ln_g, ln_b)


def _local_step(x, mem, target, lb_logits, gain, sinks, rel_bias, ln1_g, ln1_b, ln2_g, ln2_b,
                win_t, dep0, other_weights, send_other_grads, send_small_grads, send_win_grad):
    S = x.shape[0]
    T = min(256, S)
    KC = min(2048, S)
    z_qfv, zmain, gl, xb = _in_proj(x, win_t, dep0, tm=min(512, S))
    bucket = _t5_bucket_table()

    o_raw, states = _hgrn_fwd(z_qfv, lb_logits, T=min(2048, S))
    o_b, swa_probs = _swa_fwd(zmain, bucket, rel_bias, sinks)
    g2 = other_weights((o_b, o_raw))
    mkv = _mem_kv_proj(mem, g2)
    o_c, mem_probs = _mem_fwd(zmain, mkv, T=min(1024, S))
    h1, h1b, pa, pb, pc, m, oa = _merge_fwd(o_raw, zmain, o_b, o_c, gl, x, gain, g2, g2, g2, g2, ln1_g, ln1_b,
                                                T=min(512, S))

    d_h1, act, d_up, du2, loss, d_ln2_g, d_ln2_b = _mlp_fwd_bwd(h1, target, g2, g2, ln2_g, ln2_b, T=min(512, S), FC=512)
    wgrad = functools.partial(_mm_tn, out_dtype=BF16)
    halves = lambda r0: (lambda i: (i // 2, r0 // 256 + i % 2, 0))
    whole = lambda r0: (lambda i: (0, r0 // 128, 0))
    og = lax.empty((N_DEV, R_OTHER, D_MODEL), BF16)
    og = wgrad(act, du2, kc=KC, name="grad_w_down", into=(og, (1, 256, D_MODEL), halves(R_DN)))
    og = wgrad(d_up, h1b, kc=KC, name="grad_w_up", into=(og, (1, 256, D_MODEL), halves(R_UP)))

    (dx_part, du1, dpa, dpb, dpc, d_oraw, d_ob, d_oc, d_hg_gl, d_gain, d_ln1_g, d_ln1_b) = _merge_bwd(
        d_h1, pa, pb, pc, m, o_raw, zmain, gl, x, gain, g2, g2, g2, g2, ln1_g, T=T)
    for a_op, b_op, r0, nm in ((m, du1, R_OUT, "out"), (oa, dpa, R_BH, "branch_hg"), (o_b, dpb, R_BS, "branch_swa"),
                               (o_c, dpc, R_BM, "branch_mem")):
        og = wgrad(a_op, b_op, kc=KC, name="grad_w_" + nm, into=(og, (N_DEV, 128, D_MODEL), whole(r0)))

    d_mq, d_mkv = _mem_bwd(zmain, mkv, o_c, mem_probs, d_oc, T=min(1024, S))
    og = wgrad(d_mkv, mem, kc=MEM_LEN, name="grad_w_mem_kv",
               into=(og, (1, 256, D_MODEL), lambda i: (i, R_KV // 256, 0)))
    sent_others = send_other_grads(og)
    d_sq, d_skv, d_rb, d_sink = _swa_bwd(zmain, o_b, swa_probs, d_ob, bucket, sent_others)
    d_qfv, d_lb = _hgrn_bwd(z_qfv, lb_logits, states, d_oraw, T=min(2048, S))
    sent_small = send_small_grads(_pack_small_grads(d_lb, d_gain, d_sink, d_rb, d_ln1_g, d_ln1_b, d_ln2_g, d_ln2_b, loss))

    pieces = (d_qfv, d_hg_gl, d_sq, d_skv, d_mq)
    placed = (
        ("qfv", d_qfv, 128, lambda i: ((i % 3) * HG_HEADS + i // 3, 0)),
        ("hg_gates", d_hg_gl, 256, lambda i: (jnp.where(i < 4, C_HG // 256 + i, C_GL // 256 + i - 4), 0)),
        ("swa_q", d_sq, None, lambda i: (C_SQ // 1024, 0)),
        ("swa_kv", d_skv, None, lambda i: (C_SK // 256, 0)),
        ("mem_q", d_mq, 256, lambda i: (C_MQ // 256 + i, 0)),
    )
    g_win_t = lax.empty((IN_COLS, D_MODEL), BF16)
    for nm, piece, tile, index in placed:
        g_win_t = wgrad(piece, xb, kc=KC, name="grad_w_in_" + nm, tm=tile,
                        into=(g_win_t, (tile or piece.shape[1], D_MODEL), index))
    sent_win = send_win_grad(g_win_t, sent_small)
    return _grad_x(*pieces, win_t, dx_part, sent_win, tm=T)


MESH = pl.DeviceIdType.MESH
ANY = pl.BlockSpec(memory_space=pl.ANY)


def _coords():
    return lax.axis_index("x"), lax.axis_index("y"), lax.axis_index("c")


def _other_chips(x, y):
    return [(1 - x, y), (x, 1 - y), (1 - x, 1 - y)]


def _all_gather_weights(shard, pack):
    arrays = (shard,)
    na = len(arrays)
    packed_rows = sum(a.shape[1] if t else a.shape[0] for a, t in pack)

    def body(*refs):
        srcs, pack_refs = refs[:na], refs[na:na + len(pack)]
        dsts, packed_ref = refs[na + len(pack):2 * na + len(pack)], refs[2 * na + len(pack)]
        send_sems, recv_sems, local_sems = refs[2 * na + len(pack) + 1:]
        x, y, c = _coords()
        me, sibling = (x, y, c), (x, y, 1 - c)
        chips = _other_chips(x, y)

        def slot(a, px, py, pc):
            return dsts[a].at[4 * px + 2 * py + pc]

        def copy(a, k, block, to, from_shard=False):
            return pltpu.make_async_remote_copy(
                src_ref=srcs[a] if from_shard else slot(a, *block), dst_ref=slot(a, *block),
                send_sem=send_sems.at[a * 7 + k], recv_sem=recv_sems.at[a * 7 + k],
                device_id=to, device_id_type=MESH)

        own = [pltpu.make_async_copy(srcs[a], slot(a, *me), local_sems.at[a]) for a in range(na)]
        for cp in own:
            cp.start()
        first = []
        for a in range(na):
            first.append(copy(a, 0, me, sibling, True))
            first += [copy(a, 1 + j, me, (*chip, c), True) for j, chip in enumerate(chips)]
        for cp in first:
            cp.start()
        row = 0
        for (a, transposed), p_ref in zip(pack, pack_refs):
            v = p_ref[...].T if transposed else p_ref[...]
            packed_ref[row:row + v.shape[0], :] = _bf(v)
            row += v.shape[0]
        passed = []
        for j, chip in enumerate(chips):
            for a in range(na):
                copy(a, 1 + j, (*chip, c), me).wait_recv()
                fwd = copy(a, 4 + j, (*chip, c), sibling)
                fwd.start()
                passed.append(fwd)
        for a in range(na):
            copy(a, 0, sibling, me).wait_recv()
            for j, chip in enumerate(chips):
                copy(a, 4 + j, (*chip, 1 - c), me).wait_recv()
        for cp in first + passed:
            cp.wait_send()
        for cp in own:
            cp.wait()

    vm = pl.BlockSpec(memory_space=pltpu.VMEM)
    return pl.pallas_call(
        body,
        in_specs=[ANY] * na + [vm] * len(pack),
        out_specs=[ANY] * na + [vm],
        out_shape=[jax.ShapeDtypeStruct((N_DEV,) + a.shape, a.dtype) for a in arrays]
        + [jax.ShapeDtypeStruct((packed_rows, D_MODEL), BF16)],
        scratch_shapes=[pltpu.SemaphoreType.DMA((7 * na,)), pltpu.SemaphoreType.DMA((7 * na,)),
                        pltpu.SemaphoreType.DMA((na,))],
        compiler_params=pltpu.CompilerParams(vmem_limit_bytes=VMEM_LIMIT),
        name="all_gather_weights",
    )(*arrays, *[a for a, _ in pack])


HBM = pl.BlockSpec(memory_space=pltpu.HBM)
SEM = pl.BlockSpec(memory_space=pltpu.SEMAPHORE)
_DATAFLOW = pltpu.SideEffectType.DATAFLOW_SIDE_EFFECTING


def _peer(x, y, c, r):
    return x ^ (r >> 2), y ^ ((r >> 1) & 1), c ^ (r & 1)


def _direct_copies(src_ref, land_ref, send_sems, recv_sems, gather, receiving):
    x, y, c = _coords()
    me = 4 * x + 2 * y + c
    copies = []
    for r in range(1, N_DEV):
        px, py, pc = _peer(x, y, c, r)
        peer = 4 * px + 2 * py + pc
        if gather:
            src, dst = src_ref, land_ref.at[peer if receiving else me]
        else:
            src, dst = src_ref.at[peer], land_ref.at[r - 1]
        copies.append(pltpu.make_async_remote_copy(
            src_ref=src, dst_ref=dst, send_sem=send_sems.at[r - 1], recv_sem=recv_sems.at[r - 1],
            device_id=(px, py, pc), device_id_type=MESH))
    return copies


def _direct_start(src, land, *, gather, name, after=None):
    def body(src_ref, land_ref, *rest):
        send_sems, recv_sems, token = rest[-5], rest[-4], rest[-1]
        for cp in _direct_copies(src_ref, land_ref, send_sems, recv_sems, gather, False):
            cp.start()
        token[...] = jnp.zeros_like(token)

    afters = () if after is None else (after,)
    return pl.pallas_call(
        body,
        name=name,
        out_shape=(pltpu.SemaphoreType.DMA((N_DEV - 1,)), pltpu.SemaphoreType.DMA((N_DEV - 1,)),
                   pltpu.HBM(src.shape, src.dtype), pltpu.HBM(land.shape, land.dtype),
                   jax.ShapeDtypeStruct((8, 128), F32)),
        in_specs=(HBM, HBM) + tuple(ANY for _ in afters),
        out_specs=(SEM, SEM, HBM, HBM, pl.BlockSpec(memory_space=pltpu.VMEM)),
        input_output_aliases={0: 2, 1: 3},
        compiler_params=pltpu.CompilerParams(has_side_effects=_DATAFLOW),
    )(pltpu.with_memory_space_constraint(src, pltpu.HBM), pltpu.with_memory_space_constraint(land, pltpu.HBM), *afters)


def _direct_wait(send_sems, recv_sems, src_thru, land_thru, after, *, gather, name):
    afters = after if isinstance(after, tuple) else (after,)

    def body(src_ref, land_ref, send_sems_ref, recv_sems_ref, *rest):
        del rest
        for cp in _direct_copies(src_ref, land_ref, send_sems_ref, recv_sems_ref, gather, True):
            cp.wait_send()
            cp.wait_recv()

    return pl.pallas_call(
        body,
        name=name,
        out_shape=(pltpu.HBM(src_thru.shape, src_thru.dtype), pltpu.HBM(land_thru.shape, land_thru.dtype)),
        in_specs=(HBM, HBM, SEM, SEM) + tuple(ANY for _ in afters),
        out_specs=(HBM, HBM),
        input_output_aliases={0: 0, 1: 1},
        compiler_params=pltpu.CompilerParams(has_side_effects=_DATAFLOW),
    )(src_thru, land_thru, send_sems, recv_sems, *afters)


def _sum_partials(src, land, me, *, tr, name, wmv=None):
    R = src.shape[1]
    extra = () if wmv is None else tuple(wmv)

    def body(me_ref, s_ref, l_ref, *rest):
        del me_ref
        acc = s_ref[0].astype(F32)
        for r in range(N_DEV - 1):
            acc = acc + l_ref[r].astype(F32)
        rest[len(extra)][...] = acc
        if extra:
            w_ref, m_ref, v_ref, _, d_ref, nm_ref, nv_ref = rest
            d_ref[...], nm_ref[...], nv_ref[...] = _adam_step(w_ref[...], acc, m_ref[...], v_ref[...])

    row = pl.BlockSpec((tr, 1024), lambda i, mr: (i, 0))
    n_out = 4 if extra else 1
    res = pl.pallas_call(
        body,
        grid_spec=pltpu.PrefetchScalarGridSpec(
            num_scalar_prefetch=1, grid=(R // tr,),
            in_specs=[pl.BlockSpec((1, tr, 1024), lambda i, mr: (mr[0], i, 0)),
                      pl.BlockSpec((N_DEV - 1, tr, 1024), lambda i, mr: (0, i, 0))] + [row for _ in extra],
            out_specs=[row] * n_out),
        out_shape=[jax.ShapeDtypeStruct((R, 1024), F32)] * n_out,
        name=name,
    )(me, src, land, *extra)
    return res if extra else res[0]


_SMALL = ("lb_logits", "hg_norm_gain", "swa_sinks", "rel_bias", "ln1_g", "ln1_b", "ln2_g", "ln2_b")


def _pack_small_grads(d_lb, d_gain, d_sink, d_rb, d_ln1_g, d_ln1_b, d_ln2_g, d_ln2_b, loss):
    def body(lb_ref, gain_ref, sink_ref, rb_ref, l1g_ref, l1b_ref, l2g_ref, l2b_ref, loss_ref, o_ref):
        o_ref[...] = jnp.zeros_like(o_ref)
        for row, ref in ((SM_LB, lb_ref), (SM_GAIN, gain_ref), (SM_L1G, l1g_ref), (SM_L1B, l1b_ref),
                         (SM_L2G, l2g_ref), (SM_L2B, l2b_ref)):
            o_ref[row:row + 1, :] = ref[...]
        o_ref[SM_SINK:SM_SINK + 1, 0:128] = sink_ref[0:1, :]
        o_ref[SM_LOSS:SM_LOSS + 1, 0:128] = loss_ref[0:1, :]
        o_ref[SM_RB:SM_RB + NUM_BUCKETS, 0:128] = rb_ref[...]

    vm = pl.BlockSpec(memory_space=pltpu.VMEM)
    return pl.pallas_call(
        body,
        in_specs=[vm] * 9,
        out_specs=vm,
        out_shape=jax.ShapeDtypeStruct((SM_ROWS, D_MODEL), F32),
        name="pack_small_grads",
    )(d_lb, d_gain, d_sink, d_rb, d_ln1_g, d_ln1_b, d_ln2_g, d_ln2_b, loss)


def _small_finish(gathered, w, m, v):
    n = len(_SMALL)

    def body(*refs):
        g_ref = refs[0]
        w_refs, m_refs, v_refs = refs[1:1 + n], refs[1 + n:1 + 2 * n], refs[1 + 2 * n:1 + 3 * n]
        outs = refs[1 + 3 * n:]
        loss_ref, tot = outs[0], outs[-1]
        g_out, d_out, m_out, v_out = (outs[1 + k * n:1 + (k + 1) * n] for k in range(4))
        acc = g_ref[0]
        for d in range(1, N_DEV):
            acc = acc + g_ref[d]
        tot[...] = acc
        loss_ref[...] = tot[SM_LOSS:SM_LOSS + 1, 0:1]
        lb = _lower_bound(w_refs[0])
        dl0 = tot[SM_LB:SM_LB + 1, :] * lb * (1.0 - lb)
        grads = (jnp.concatenate([dl0, -dl0], axis=0), tot[SM_GAIN:SM_GAIN + 1, :],
                 tot[SM_SINK:SM_SINK + 1, 0:SWA_HEADS], tot[SM_RB:SM_RB + NUM_BUCKETS, 0:SWA_HEADS],
                 tot[SM_L1G:SM_L1G + 1, :], tot[SM_L1B:SM_L1B + 1, :], tot[SM_L2G:SM_L2G + 1, :], tot[SM_L2B:SM_L2B + 1, :])
        for k, g in enumerate(grads):
            g_out[k][...] = g
            d_out[k][...], m_out[k][...], v_out[k][...] = _adam_step(w_refs[k][...], g, m_refs[k][...], v_refs[k][...])

    vm = pl.BlockSpec(memory_space=pltpu.VMEM)
    shapes = [jax.ShapeDtypeStruct(w[k].shape, F32) for k in _SMALL]
    res = pl.pallas_call(
        body,
        in_specs=[vm] * (1 + 3 * n),
        out_specs=[vm] * (1 + 4 * n),
        out_shape=[jax.ShapeDtypeStruct((1, 1), F32)] + shapes * 4,
        scratch_shapes=[pltpu.VMEM((SM_ROWS, D_MODEL), F32)],
        name="small_finish",
    )(gathered, *[w[k] for k in _SMALL], *[m[k] for k in _SMALL], *[v[k] for k in _SMALL])
    parts = [dict(zip(_SMALL, res[1 + k * n:1 + (k + 1) * n])) for k in range(4)]
    return (res[0], *parts)


def _adam_step(w, g, m, v):
    nm = ADAM_B1 * m + (1.0 - ADAM_B1) * g
    nv = ADAM_B2 * v + (1.0 - ADAM_B2) * jnp.square(g)
    m_hat = nm / (1.0 - ADAM_B1 ** ADAM_STEP)
    v_hat = nv / (1.0 - ADAM_B2 ** ADAM_STEP)
    return -ADAM_LR * (m_hat / (jnp.sqrt(v_hat) + ADAM_EPS) + ADAM_WD * w), nm, nv


def _adamw_group(ws, ms, vs, grads=None, packed=None, name="adamw_group"):
    n = len(ws)
    g_in = list(grads) if packed is None else [packed[0]]

    def body(*refs):
        g_refs = refs[:len(g_in)]
        w_refs, m_refs, v_refs = (refs[len(g_in) + k * n:len(g_in) + (k + 1) * n] for k in range(3))
        outs = refs[len(g_in) + 3 * n:]
        for k in range(n):
            if packed is None:
                g = g_refs[k][...]
            else:
                r0, rows = packed[1][k]
                g = g_refs[0][r0:r0 + rows, :]
            outs[k][...] = g
            outs[n + k][...], outs[2 * n + k][...], outs[3 * n + k][...] = _adam_step(
                w_refs[k][...], g, m_refs[k][...], v_refs[k][...])

    vm = pl.BlockSpec(memory_space=pltpu.VMEM)
    shapes = [jax.ShapeDtypeStruct(a.shape, F32) for a in ws]
    res = pl.pallas_call(
        body,
        in_specs=[vm] * (len(g_in) + 3 * n),
        out_specs=[vm] * (4 * n),
        out_shape=shapes * 4,
        compiler_params=pltpu.CompilerParams(vmem_limit_bytes=VMEM_LIMIT),
        name=name,
    )(*g_in, *ws, *ms, *vs)
    return [res[k * n:(k + 1) * n] for k in range(4)]


_WEIGHTS = ("w_in", "lb_logits", "hg_norm_gain", "swa_sinks", "rel_bias", "w_mem_kv", "w_branch_hg", "w_branch_swa",
            "w_branch_mem", "w_out", "ln1_g", "ln1_b", "w_up", "w_down", "ln2_g", "ln2_b")


def kernel(x, mem, w_in, lb_logits, hg_norm_gain, swa_sinks, rel_bias, w_mem_kv, w_branch_hg, w_branch_swa, w_branch_mem, w_out, ln1_g, ln1_b, w_up, w_down, ln2_g, ln2_b, loss_target, m_w_in, m_lb_logits, m_hg_norm_gain, m_swa_sinks, m_rel_bias, m_w_mem_kv, m_w_branch_hg, m_w_branch_swa, m_w_branch_mem, m_w_out, m_ln1_g, m_ln1_b, m_w_up, m_w_down, m_ln2_g, m_ln2_b, v_w_in, v_lb_logits, v_hg_norm_gain, v_swa_sinks, v_rel_bias, v_w_mem_kv, v_w_branch_hg, v_w_branch_swa, v_w_branch_mem, v_w_out, v_ln1_g, v_ln1_b, v_w_up, v_w_down, v_ln2_g, v_ln2_b):
    w = dict(w_in=w_in, lb_logits=lb_logits, hg_norm_gain=hg_norm_gain, swa_sinks=swa_sinks, rel_bias=rel_bias,
             w_mem_kv=w_mem_kv, w_branch_hg=w_branch_hg, w_branch_swa=w_branch_swa, w_branch_mem=w_branch_mem,
             w_out=w_out, ln1_g=ln1_g, ln1_b=ln1_b, w_up=w_up, w_down=w_down, ln2_g=ln2_g, ln2_b=ln2_b)
    mom = dict(w_in=m_w_in, lb_logits=m_lb_logits, hg_norm_gain=m_hg_norm_gain, swa_sinks=m_swa_sinks, rel_bias=m_rel_bias,
               w_mem_kv=m_w_mem_kv, w_branch_hg=m_w_branch_hg, w_branch_swa=m_w_branch_swa, w_branch_mem=m_w_branch_mem,
               w_out=m_w_out, ln1_g=m_ln1_g, ln1_b=m_ln1_b, w_up=m_w_up, w_down=m_w_down, ln2_g=m_ln2_g, ln2_b=m_ln2_b)
    var = dict(w_in=v_w_in, lb_logits=v_lb_logits, hg_norm_gain=v_hg_norm_gain, swa_sinks=v_swa_sinks, rel_bias=v_rel_bias,
               w_mem_kv=v_w_mem_kv, w_branch_hg=v_w_branch_hg, w_branch_swa=v_w_branch_swa, w_branch_mem=v_w_branch_mem,
               w_out=v_w_out, ln1_g=v_ln1_g, ln1_b=v_ln1_b, w_up=v_w_up, w_down=v_w_down, ln2_g=v_ln2_g, ln2_b=v_ln2_b)
    xc, yc, cc = _coords()

    p1 = _bf(w_in[0].T)
    me = 4 * xc + 2 * yc + cc
    g1, p2 = _all_gather_weights(p1, [(w_down[0], False), (w_up[0], True), (w_branch_hg[0], False), (w_branch_swa[0], False),
                                      (w_branch_mem[0], False), (w_out[0], False), (w_mem_kv[0], True)])
    land2 = lax.dynamic_update_slice(lax.empty((N_DEV, R_OTHER, D_MODEL), BF16), p2[None], (me, 0, 0))
    ag2 = _direct_start(p2, land2, gather=True, name="gather_other_weights_start")

    def other_weights(after):
        return _direct_wait(*ag2[:4], after, gather=True, name="gather_other_weights_wait")[1]

    blocks = lambda a: a.reshape(N_DEV, a.shape[0] // N_DEV, D_MODEL)
    started = {}

    def send_other_grads(part):
        started["others"] = _direct_start(part, lax.empty((N_DEV - 1, R_OTHER, D_MODEL), BF16), gather=False,
                                          name="scatter_other_grads_start")
        return started["others"][4]

    me1 = me.reshape(1).astype(jnp.int32)
    grads, delta, new_m, new_v = {}, {}, {}, {}

    def send_small_grads(packed):
        land = lax.dynamic_update_slice(lax.empty((N_DEV, SM_ROWS, D_MODEL), F32), packed[None], (me, 0, 0))
        started["small"] = _direct_start(packed, land, gather=True, name="gather_small_grads_start")
        return started["small"][4]

    def send_win_grad(g, after):
        started["win"] = _direct_start(blocks(g), lax.empty((N_DEV - 1, IN_SHARD, D_MODEL), BF16), gather=False,
                                       name="scatter_w_in_grad_start", after=after)
        mine2, landed2 = _direct_wait(*started["others"][:4], started["win"][4], gather=False,
                                      name="scatter_other_grads_wait")
        gs2 = _sum_partials(mine2, landed2, me1, tr=R_OTHER // 2, name="sum_other_grads")
        rowwise = (("w_down", R_DN, R_UP), ("w_branch_hg", R_BH, R_BS), ("w_branch_swa", R_BS, R_BM),
                   ("w_branch_mem", R_BM, R_OUT), ("w_out", R_OUT, R_KV))
        colwise = (("w_up", R_UP, R_BH), ("w_mem_kv", R_KV, R_OTHER))
        for names, kw in (([n for n, _, _ in rowwise], dict(packed=(gs2, [(lo, hi - lo) for _, lo, hi in rowwise]))),
                          ([n for n, _, _ in colwise], dict(grads=[gs2[lo:hi].T for _, lo, hi in colwise]))):
            res = _adamw_group([w[n][0] for n in names], [mom[n][0] for n in names], [var[n][0] for n in names],
                               name="adamw_" + "_".join(n[2:] for n in names), **kw)
            for dst, vals in zip((grads, delta, new_m, new_v), res):
                dst.update(zip(names, vals))
        return (new_v["w_down"], new_v["w_up"])

    grad_x = _local_step(
        x[0], mem[0], loss_target[0], lb_logits, hg_norm_gain, swa_sinks, rel_bias, ln1_g, ln1_b, ln2_g, ln2_b,
        g1.reshape(IN_COLS, D_MODEL), ag2[4], other_weights, send_other_grads, send_small_grads, send_win_grad)

    mine1, landed1 = _direct_wait(*started["win"][:4], grad_x, gather=False, name="scatter_w_in_grad_wait")
    g_win_t, d_t, m_t, v_t = _sum_partials(mine1, landed1, me1, tr=IN_SHARD // 2, name="sum_adamw_w_in",
                                           wmv=(w_in[0].T, m_w_in[0].T, v_w_in[0].T))
    grads["w_in"], delta["w_in"], new_m["w_in"], new_v["w_in"] = g_win_t.T, d_t.T, m_t.T, v_t.T

    _, gathered = _direct_wait(*started["small"][:4], grad_x, gather=True, name="gather_small_grads_wait")
    loss, g_s, d_s, m_s, v_s = _small_finish(gathered, w, mom, var)
    for dst, src in ((grads, g_s), (delta, d_s), (new_m, m_s), (new_v, v_s)):
        dst.update(src)

    def shaped(d, name):
        return d[name].reshape(w[name].shape)

    return (loss.reshape(()), grad_x[None], *[shaped(grads, n) for n in _WEIGHTS], *[shaped(delta, n) for n in _WEIGHTS],
            *[shaped(new_m, n) for n in _WEIGHTS], *[shaped(new_v, n) for n in _WEIGHTS])
```

```python
import functools
import math

import jax
import jax.numpy as jnp
from jax import lax
from jax.experimental import pallas as pl
from jax.experimental.pallas import tpu as pltpu

F32 = jnp.float32
BF16 = jnp.bfloat16

D_MODEL = 1024
MEM_LEN = 256
HG_HEADS = 8
HG_DK = 128
HG_CHUNK = 64
SWA_HEADS = 16
SWA_HEAD_DIM = 64
SWA_BLOCK = 128
SWA_WINDOW = 128
MEM_HEADS = 4
MEM_HEAD_DIM = 256
NUM_BUCKETS = 32
MAX_DISTANCE = 128
D_FF = 4096
LN_EPS = 1e-5
RMS_EPS = 1e-6
ALPHA = 2.0 ** 0.25
N_DEV = 8

C_HQ, C_HF, C_HI, C_HG, C_SQ, C_SK, C_SV, C_MQ, C_GL = 0, 1024, 2048, 3072, 4096, 5120, 5248, 5376, 6400
IN_COLS = 9472
IN_SHARD = IN_COLS // N_DEV
Z_HG, Z_SQ, Z_SK, Z_MQ, Z_REST = 0, C_SQ - C_HG, C_SK - C_HG, C_MQ - C_HG, C_GL - C_HG

ADAM_LR = 0.001
ADAM_B1 = 0.9
ADAM_B2 = 0.999
ADAM_EPS = 1e-08
ADAM_WD = 0.01
ADAM_STEP = 10

VMEM_LIMIT = 58 * 1024 * 1024

R_DN, R_UP, R_BH, R_BS, R_BM, R_OUT, R_KV, R_OTHER = 0, 512, 1024, 1152, 1280, 1408, 1536, 1792

SM_LB, SM_GAIN, SM_SINK, SM_L1G, SM_L1B, SM_L2G, SM_L2B, SM_LOSS, SM_RB, SM_ROWS = 0, 2, 3, 4, 5, 6, 7, 8, 16, 48


def _bf(v):
    return v.astype(BF16)


def _f32(v):
    return v.astype(F32)


def _dot(a, b):
    return jnp.dot(a, b, preferred_element_type=F32)


def _dot_nt(a, b):
    return lax.dot_general(a, b, (((1,), (1,)), ((), ())), preferred_element_type=F32)


def _dot_tn(a, b):
    return lax.dot_general(a, b, (((0,), (0,)), ((), ())), preferred_element_type=F32)


def _sig(v):
    return 0.5 * jnp.tanh(0.5 * v) + 0.5


def _cparams(*sem):
    return pltpu.CompilerParams(dimension_semantics=sem, vmem_limit_bytes=VMEM_LIMIT)


def _const_spec(shape):
    nd = len(shape)
    return pl.BlockSpec(shape, lambda *_: (0,) * nd, pipeline_mode=pl.Buffered(1))


def _dep_spec():
    return pl.BlockSpec((8, 128), lambda *_: (0, 0))


def _in_proj(x, win_t, dep, *, tm):
    S = x.shape[0]

    def body(x_ref, w_ref, dep_ref, qfv_ref, z_ref, gl_ref, xb_ref):
        del dep_ref
        xb = _bf(x_ref[...])
        xb_ref[...] = xb
        for c0 in range(0, C_HG, 1024):
            qfv_ref[:, c0:c0 + 1024] = _dot_nt(xb, w_ref[c0:c0 + 1024, :])
        for c0 in range(0, Z_REST, Z_REST // 2):
            z_ref[:, c0:c0 + Z_REST // 2] = _bf(_dot_nt(xb, w_ref[C_HG + c0:C_HG + c0 + Z_REST // 2, :]))
        for c0 in range(0, IN_COLS - C_GL, 1024):
            gl_ref[:, c0:c0 + 1024] = _bf(_dot_nt(xb, w_ref[C_GL + c0:C_GL + c0 + 1024, :]))

    row = lambda w: pl.BlockSpec((tm, w), lambda i: (i, 0))
    return pl.pallas_call(
        body,
        grid=(S // tm,),
        in_specs=[row(D_MODEL), _const_spec(win_t.shape), _dep_spec()],
        out_specs=[row(C_HG), row(Z_REST), row(IN_COLS - C_GL), row(D_MODEL)],
        out_shape=[jax.ShapeDtypeStruct((S, C_HG), F32), jax.ShapeDtypeStruct((S, Z_REST), BF16),
                   jax.ShapeDtypeStruct((S, IN_COLS - C_GL), BF16), jax.ShapeDtypeStruct((S, D_MODEL), BF16)],
        compiler_params=_cparams("parallel"),
        name="in_proj",
    )(x, win_t, dep)


def _placement(into, tm, N, M, out_dtype):
    if into is None:
        return (lambda i: (i, 0)), (tm, N), jax.ShapeDtypeStruct((M, N), out_dtype), (), {}
    dest, block, index = into
    assert math.prod(block) == tm * N and dest.dtype == out_dtype
    return index, block, jax.ShapeDtypeStruct(dest.shape, dest.dtype), (dest,), {2: 0}


def _mm_tn_resident(a, b, *, tm, kc, name, out_dtype, into=None):
    K, M = a.shape
    N = b.shape[1]
    nk = K // kc
    index, block, out_shape, extra, aliases = _placement(into, tm, N, M, out_dtype)

    def body(a_ref, b_ref, *rest):
        o_ref = rest[-1]
        acc = jnp.zeros((tm, N), F32)
        for kk in range(nk):
            sl = pl.ds(kk * kc, kc)
            acc = acc + _dot_tn(_bf(a_ref[sl, :]), _bf(b_ref[sl, :]))
        o_ref[...] = acc.astype(o_ref.dtype).reshape(block)

    return pl.pallas_call(
        body,
        grid=(M // tm,),
        in_specs=[pl.BlockSpec((K, tm), lambda i: (0, i)), _const_spec((K, N))] + [ANY for _ in extra],
        out_specs=pl.BlockSpec(block, index),
        out_shape=out_shape,
        input_output_aliases=aliases,
        compiler_params=_cparams("parallel"),
        name=name,
    )(a, b, *extra)


def _mm_tn(a, b, *, kc, name, out_dtype=F32, into=None, tm=None):
    K, M = a.shape
    N = b.shape[1]
    if M > 1024 or tm is not None:
        return _mm_tn_resident(a, b, tm=tm or 256, kc=min(kc, 1024), name=name, out_dtype=out_dtype, into=into)
    tm = M
    if a.dtype == BF16 and b.dtype == BF16 and K % (2 * kc) == 0:
        kc = 2 * kc
    nk = K // kc
    index, block, out_shape, extra, aliases = _placement(into, tm, N, M, out_dtype)

    def body(a_ref, b_ref, *rest):
        o_ref, acc = rest[-2], rest[-1]
        k = pl.program_id(1)
        part = _dot_tn(_bf(a_ref[...]), _bf(b_ref[...]))

        @pl.when(k == 0)
        def _():
            acc[...] = part

        @pl.when(k > 0)
        def _():
            acc[...] += part

        @pl.when(k == nk - 1)
        def _():
            o_ref[...] = acc[...].astype(o_ref.dtype).reshape(block)

    return pl.pallas_call(
        body,
        grid=(M // tm, nk),
        in_specs=[pl.BlockSpec((kc, tm), lambda i, k: (k, i)), pl.BlockSpec((kc, N), lambda i, k: (k, 0))]
        + [ANY for _ in extra],
        out_specs=pl.BlockSpec(block, lambda i, k: index(i)),
        out_shape=out_shape,
        input_output_aliases=aliases,
        scratch_shapes=[pltpu.VMEM((tm, N), F32)],
        compiler_params=_cparams("parallel", "arbitrary"),
        name=name,
    )(a, b, *extra)


def _grad_x(d_qfv, d_hg_gl, d_sq, d_skv, d_mq, w_qfv, win_t, add, deps, *, tm):
    M = add.shape[0]
    pieces = (d_qfv, d_hg_gl, d_sq, d_skv, d_mq)

    def body(qfv_ref, hggl_ref, sq_ref, skv_ref, mq_ref, wq_ref, w_ref, add_ref, *rest):
        o_ref = rest[-1]
        acc = add_ref[...] + _dot(qfv_ref[...], wq_ref[...])
        acc = acc + _dot(hggl_ref[:, 0:1024], w_ref[C_HG:C_SQ, :])
        acc = acc + _dot(hggl_ref[:, 1024:4096], w_ref[C_GL:IN_COLS, :])
        acc = acc + _dot(sq_ref[...], w_ref[C_SQ:C_SK, :])
        acc = acc + _dot(skv_ref[...], w_ref[C_SK:C_MQ, :])
        o_ref[...] = acc + _dot(mq_ref[...], w_ref[C_MQ:C_GL, :])

    return pl.pallas_call(
        body,
        grid=(M // tm,),
        in_specs=[pl.BlockSpec((tm, p.shape[1]), lambda i: (i, 0)) for p in pieces]
        + [_const_spec(w_qfv.shape), _const_spec(win_t.shape), pl.BlockSpec((tm, D_MODEL), lambda i: (i, 0))]
        + [_dep_spec() for _ in deps],
        out_specs=pl.BlockSpec((tm, D_MODEL), lambda i: (i, 0)),
        out_shape=jax.ShapeDtypeStruct((M, D_MODEL), F32),
        compiler_params=_cparams("parallel"),
        name="grad_x",
    )(*pieces, w_qfv, win_t, add, *deps)


def _lower_bound(lbl_ref):
    l0 = lbl_ref[0:1, :]
    l1 = lbl_ref[1:2, :]
    mx = jnp.maximum(l0, l1)
    e0 = jnp.exp(l0 - mx)
    e1 = jnp.exp(l1 - mx)
    return e0 / (e0 + e1)


def _tri(lower):
    r = lax.broadcasted_iota(jnp.int32, (HG_CHUNK, HG_CHUNK), 0)
    c = lax.broadcasted_iota(jnp.int32, (HG_CHUNK, HG_CHUNK), 1)
    return (r >= c) if lower else (r <= c)


def _hg_gates(fl, lb):
    sg = _sig(fl)
    f = lb + (1.0 - lb) * sg
    return sg, f, jnp.log(f), 1.0 - f


def _scan_rows(v, reverse=False):
    row = lax.broadcasted_iota(jnp.int32, v.shape, 0)
    s = 1
    while s < HG_CHUNK:
        if reverse:
            v = v + jnp.where(row < HG_CHUNK - s, pltpu.roll(v, HG_CHUNK - s, 0), 0.0)
        else:
            v = v + jnp.where(row >= s, pltpu.roll(v, s, 0), 0.0)
        s *= 2
    return v


def _hgrn_fwd(zmain, lb_logits, *, T):
    S = zmain.shape[0]
    nc = T // HG_CHUNK

    def body(q_ref, f_ref, v_ref, lbl_ref, o_ref, st_ref, state):
        @pl.when(pl.program_id(1) == 0)
        def _():
            state[...] = jnp.zeros_like(state)

        lb = _lower_bound(lbl_ref)
        tril = _tri(True)
        qis, updates, decays, intra = [], [], [], []
        for c in range(nc):
            sl = pl.ds(c * HG_CHUNK, HG_CHUNK)
            _, _, g, k = _hg_gates(_f32(f_ref[sl, :]), lb)
            b = _scan_rows(g)
            bl = jnp.sum(g, axis=0, keepdims=True)
            qi = _bf(_f32(q_ref[sl, :]) * jnp.exp(b))
            ki = _bf(k * jnp.exp(-b))
            ko = _bf(k * jnp.exp(bl - b))
            vb = _bf(v_ref[sl, :])
            att = jnp.where(tril, _dot_nt(qi, ki), 0.0)
            intra.append(_dot(_bf(att), vb))
            qis.append(qi)
            updates.append(_dot_tn(vb, ko))
            decays.append(jnp.exp(bl))
        st = state[...]
        for c in range(nc):
            st_ref[0, c] = st
            o_ref[pl.ds(c * HG_CHUNK, HG_CHUNK), :] = intra[c] + _dot_nt(qis[c], _bf(st))
            st = st * decays[c] + updates[c]
        state[...] = st

    col = lambda base: pl.BlockSpec((T, HG_DK), lambda h, t: (t, base + h))
    return pl.pallas_call(
        body,
        grid=(HG_HEADS, S // T),
        in_specs=[col(0), col(8), col(16), pl.BlockSpec((2, HG_DK), lambda h, t: (0, h))],
        out_specs=[
            pl.BlockSpec((T, HG_DK), lambda h, t: (t, h)),
            pl.BlockSpec((1, nc, HG_DK, HG_DK), lambda h, t: (h, t, 0, 0)),
        ],
        out_shape=[
            jax.ShapeDtypeStruct((S, D_MODEL), F32),
            jax.ShapeDtypeStruct((HG_HEADS, S // HG_CHUNK, HG_DK, HG_DK), F32),
        ],
        scratch_shapes=[pltpu.VMEM((HG_DK, HG_DK), F32)],
        compiler_params=_cparams("parallel", "arbitrary"),
        name="hgrn_fwd",
    )(zmain, zmain, zmain, lb_logits)


def _hgrn_bwd(zmain, lb_logits, states, d_o, *, T):
    S = zmain.shape[0]
    nc = T // HG_CHUNK
    nt = S // T

    def body(q_ref, f_ref, v_ref, lbl_ref, st_ref, do_ref, dz_ref, dlb_ref, dstate):
        @pl.when(pl.program_id(1) == 0)
        def _():
            dstate[...] = jnp.zeros_like(dstate)
            dlb_ref[...] = jnp.zeros_like(dlb_ref)

        lb = _lower_bound(lbl_ref)
        tril = _tri(True)
        last_row = lax.broadcasted_iota(jnp.int32, (HG_CHUNK, HG_DK), 0) == HG_CHUNK - 1
        saved = []
        for c in range(nc):
            sl = pl.ds(c * HG_CHUNK, HG_CHUNK)
            sg, f, g, k = _hg_gates(_f32(f_ref[sl, :]), lb)
            b = _scan_rows(g)
            bl = jnp.sum(g, axis=0, keepdims=True)
            eb = jnp.exp(b)
            enb = jnp.exp(-b)
            eo = jnp.exp(bl - b)
            q_in = _f32(q_ref[sl, :]) * eb
            k_in = k * enb
            k_out = k * eo
            qi, ki, ko = _bf(q_in), _bf(k_in), _bf(k_out)
            vb = _bf(v_ref[sl, :])
            dob = do_ref[sl, :]
            att = jnp.where(tril, _dot_nt(qi, ki), 0.0)
            d_att = _bf(jnp.where(tril, _dot_nt(dob, vb), 0.0))
            d_kin = _dot_tn(d_att, qi)
            saved.append(dict(
                sg=sg, f=f, eb=eb, enb=enb, eo=eo, ebl=jnp.exp(bl), k_out=k_out, ko=ko, vb=vb, dob=dob,
                d_v=_dot_tn(_bf(att), dob), d_qin=_dot(d_att, ki), d_kin=d_kin,
                qk=(q_in, k_in), d_state=_dot_tn(dob, qi)))
        dst = dstate[...]
        dsts = [None] * nc
        for c in reversed(range(nc)):
            dsts[c] = dst
            dst = dst * saved[c]["ebl"] + saved[c]["d_state"]
        dstate[...] = dst
        dlb = jnp.zeros((1, HG_DK), F32)
        for c in range(nc):
            sl = pl.ds(c * HG_CHUNK, HG_CHUNK)
            s = saved[c]
            q_in, k_in = s["qk"]
            st = st_ref[0, c]
            dstb = _bf(dsts[c])
            d_v = s["d_v"] + _dot_nt(s["ko"], dstb)
            d_qin = s["d_qin"] + _dot(s["dob"], _bf(st))
            d_kout = _dot(s["vb"], dstb)
            d_decay = jnp.sum(dsts[c] * st, axis=0, keepdims=True)
            kk = d_kout * s["k_out"]
            d_b = d_qin * q_in - s["d_kin"] * k_in - kk
            d_bl = jnp.sum(kk, axis=0, keepdims=True) + d_decay * s["ebl"]
            d_g = _scan_rows(d_b + jnp.where(last_row, d_bl, 0.0), reverse=True)
            d_f = d_g / s["f"] - (s["d_kin"] * s["enb"] + d_kout * s["eo"])
            dz_ref[sl, 0:HG_DK] = _bf(d_qin * s["eb"])
            dz_ref[sl, HG_DK:2 * HG_DK] = _bf(d_f * (1.0 - lb) * s["sg"] * (1.0 - s["sg"]))
            dz_ref[sl, 2 * HG_DK:3 * HG_DK] = _bf(d_v)
            dlb = dlb + jnp.sum(d_f * (1.0 - s["sg"]), axis=0, keepdims=True)
        dlb_ref[...] += dlb

    rev = lambda base: pl.BlockSpec((T, HG_DK), lambda h, t: (nt - 1 - t, base + h))
    outc = pl.BlockSpec((T, HG_DK), lambda h, t: (nt - 1 - t, h))
    return pl.pallas_call(
        body,
        grid=(HG_HEADS, nt),
        in_specs=[
            rev(0), rev(8), rev(16),
            pl.BlockSpec((2, HG_DK), lambda h, t: (0, h)),
            pl.BlockSpec((1, nc, HG_DK, HG_DK), lambda h, t: (h, nt - 1 - t, 0, 0)),
            outc,
        ],
        out_specs=[pl.BlockSpec((T, 3 * HG_DK), lambda h, t: (nt - 1 - t, h)),
                   pl.BlockSpec((1, HG_DK), lambda h, t: (0, h))],
        out_shape=[jax.ShapeDtypeStruct((S, 3 * D_MODEL), BF16), jax.ShapeDtypeStruct((1, D_MODEL), F32)],
        scratch_shapes=[pltpu.VMEM((HG_DK, HG_DK), F32)],
        compiler_params=_cparams("parallel", "arbitrary"),
        name="hgrn_bwd",
    )(zmain, zmain, zmain, lb_logits, states, d_o)


def _t5_bucket_table():
    qi = jnp.arange(SWA_BLOCK)[:, None] + SWA_BLOCK
    kj = jnp.arange(2 * SWA_BLOCK)[None, :]
    n = jnp.clip(qi - kj, 0, SWA_WINDOW - 1)
    max_exact = NUM_BUCKETS // 2
    nf = jnp.maximum(n, 1).astype(F32)
    large = max_exact + (jnp.log(nf / max_exact) / math.log(MAX_DISTANCE / max_exact)
                         * (NUM_BUCKETS - max_exact)).astype(jnp.int32)
    large = jnp.minimum(large, NUM_BUCKETS - 1)
    return jnp.where(n < max_exact, n, large).astype(jnp.int32)


SWA_ROWS = 32
MERGE_GROUPS = 1


def _swa_bias_init(bias, bucket_ref, rb_ref):
    bk = bucket_ref[...]
    qi = lax.broadcasted_iota(jnp.int32, bk.shape, 0) + SWA_BLOCK
    kj = lax.broadcasted_iota(jnp.int32, bk.shape, 1)
    band = (qi - kj >= 0) & (qi - kj < SWA_WINDOW)
    for h in range(SWA_HEADS):
        def sel(b, acc, h=h):
            return jnp.where(bk == b, rb_ref[b, h], acc)
        t = lax.fori_loop(0, NUM_BUCKETS, sel, jnp.zeros(bk.shape, F32))
        bias[1, h] = jnp.where(band, t, -jnp.inf)
        bias[0, h] = jnp.where(band & (kj >= SWA_BLOCK), t, -jnp.inf)


def _lane_halves(t, kv_head):
    lane = lax.broadcasted_iota(jnp.int32, t.shape, 1)
    rolled = pltpu.roll(t, 64, 1)
    zero = jnp.zeros_like(t)
    if kv_head == 0:
        return jnp.where(lane < 64, t, zero), jnp.where(lane >= 64, rolled, zero)
    return jnp.where(lane < 64, rolled, zero), jnp.where(lane >= 64, t, zero)


def _swa_zero_key0(t):
    return jnp.where(lax.broadcasted_iota(jnp.int32, t.shape, 0) == 0, jnp.zeros_like(t), t)


def _swa_probs(s, masked_bias, sink):
    s = s + masked_bias
    m = jnp.maximum(jnp.max(s, axis=-1, keepdims=True), sink)
    p = jnp.exp(s - m)
    es = jnp.exp(sink - m)
    inv = 1.0 / (jnp.sum(p, axis=-1, keepdims=True) + es)
    return p * inv, es * inv


def _swa_fwd(zmain, bucket, rel_bias, sinks):
    S = zmain.shape[0]
    nb = S // SWA_BLOCK
    scale = SWA_HEAD_DIM ** -0.5

    def body(q_ref, kvc_ref, kvp_ref, bucket_ref, rb_ref, sk_ref, o_ref, p_ref, bias):
        n = pl.program_id(0)

        @pl.when(n == 0)
        def _():
            _swa_bias_init(bias, bucket_ref, rb_ref)

        later = jnp.minimum(n, 1)
        kk = _bf(jnp.concatenate([kvp_ref[:, 0:128], kvc_ref[:, 0:128]], axis=0))
        vv = _swa_zero_key0(_bf(jnp.concatenate([kvp_ref[:, 128:256], kvc_ref[:, 128:256]], axis=0)))
        first_col = lax.broadcasted_iota(jnp.int32, (SWA_ROWS, 2 * SWA_BLOCK), 1) == 0
        scores, values = {}, {}
        for kvh in range(2):
            qst = _bf(jnp.concatenate([q_ref[:, pl.ds((kvh * 4 + jj) * 128, 128)] for jj in range(4)], axis=0) * scale)
            values[kvh] = _lane_halves(vv, kvh)
            for odd, kx in enumerate(_lane_halves(kk, kvh)):
                scores[kvh, odd] = _dot_nt(qst, kx)
        probs = {}
        for (kvh, odd), s in scores.items():
            parts = []
            for jj in range(4):
                h = 2 * (kvh * 4 + jj) + odd
                for r0 in range(0, SWA_BLOCK, SWA_ROWS):
                    p, ps = _swa_probs(s[jj * SWA_BLOCK + r0:jj * SWA_BLOCK + r0 + SWA_ROWS],
                                       bias[later, h, pl.ds(r0, SWA_ROWS), :], sk_ref[0, h])
                    part = _bf(jnp.where(first_col, ps, p))
                    p_ref[pl.ds(r0, SWA_ROWS), pl.ds(h * 2 * SWA_BLOCK, 2 * SWA_BLOCK)] = part
                    parts.append(part)
            probs[kvh, odd] = jnp.concatenate(parts, axis=0)
        for kvh in range(2):
            ost = _dot(probs[kvh, 0], values[kvh][0]) + _dot(probs[kvh, 1], values[kvh][1])
            for jj in range(4):
                o_ref[:, pl.ds((kvh * 4 + jj) * 128, 128)] = ost[jj * SWA_BLOCK:(jj + 1) * SWA_BLOCK]

    smem = pl.BlockSpec(memory_space=pltpu.SMEM)
    return pl.pallas_call(
        body,
        grid=(nb,),
        in_specs=[
            pl.BlockSpec((SWA_BLOCK, 1024), lambda n: (n, Z_SQ // 1024)),
            pl.BlockSpec((SWA_BLOCK, 256), lambda n: (n, Z_SK // 256)),
            pl.BlockSpec((SWA_BLOCK, 256), lambda n: (jnp.maximum(n - 1, 0), Z_SK // 256)),
            _const_spec((SWA_BLOCK, 2 * SWA_BLOCK)), smem, smem,
        ],
        out_specs=[pl.BlockSpec((SWA_BLOCK, 1024), lambda n: (n, 0)),
                   pl.BlockSpec((SWA_BLOCK, SWA_HEADS * 2 * SWA_BLOCK), lambda n: (n, 0))],
        out_shape=[jax.ShapeDtypeStruct((S, 1024), F32),
                   jax.ShapeDtypeStruct((S, SWA_HEADS * 2 * SWA_BLOCK), BF16)],
        scratch_shapes=[pltpu.VMEM((2, SWA_HEADS, SWA_BLOCK, 2 * SWA_BLOCK), F32)],
        compiler_params=_cparams("arbitrary"),
        name="swa_fwd",
    )(zmain, zmain, zmain, bucket, rel_bias, sinks)


def _swa_bwd(zmain, o_b, probs, d_o, bucket, dep):
    S = zmain.shape[0]
    nb = S // SWA_BLOCK
    scale = SWA_HEAD_DIM ** -0.5

    def body(q_ref, kvc_ref, kvp_ref, o_ref, p_ref, do_ref, bucket_ref, dep_ref,
             dq_ref, dkv_ref, drb_ref, dsk_ref, dbias, carry):
        del dep_ref
        n = pl.program_id(0)

        @pl.when(n == 0)
        def _():
            dbias[...] = jnp.zeros_like(dbias)
            carry[...] = jnp.zeros_like(carry)

        @pl.when(n < nb)
        def _():
            kk = _swa_zero_key0(_bf(jnp.concatenate([kvp_ref[:, 0:128], kvc_ref[:, 0:128]], axis=0)))
            vv = _swa_zero_key0(_bf(jnp.concatenate([kvp_ref[:, 128:256], kvc_ref[:, 128:256]], axis=0)))
            lane = lax.broadcasted_iota(jnp.int32, (2 * SWA_BLOCK, 128), 1)
            lane_q = lax.broadcasted_iota(jnp.int32, (4 * SWA_BLOCK, 128), 1)
            pair_cols = {kvh: [pl.ds((kvh * 4 + jj) * 128, 128) for jj in range(4)] for kvh in range(2)}
            qst, dost, ks, d_p, delta = {}, {}, {}, {}, {}
            for kvh in range(2):
                qst[kvh] = _bf(jnp.concatenate([q_ref[:, cl] for cl in pair_cols[kvh]], axis=0) * scale)
                dost[kvh] = jnp.concatenate([do_ref[:, cl] for cl in pair_cols[kvh]], axis=0)
                prod = dost[kvh].astype(F32) * jnp.concatenate([o_ref[:, cl] for cl in pair_cols[kvh]], axis=0)
                ks[kvh] = _lane_halves(kk, kvh)
                for odd, vx in enumerate(_lane_halves(vv, kvh)):
                    keep = (lane_q >= 64) if odd else (lane_q < 64)
                    delta[kvh, odd] = jnp.sum(jnp.where(keep, prod, 0.0), axis=-1, keepdims=True)
                    d_p[kvh, odd] = _dot_nt(dost[kvh], vx)
            pst, dsst = {}, {}
            for (kvh, odd), dp in d_p.items():
                p_parts, ds_parts = [], []
                for jj in range(4):
                    h = 2 * (kvh * 4 + jj) + odd
                    rows = slice(jj * SWA_BLOCK, (jj + 1) * SWA_BLOCK)
                    p = p_ref[:, pl.ds(h * 2 * SWA_BLOCK, 2 * SWA_BLOCK)]
                    ds = _f32(p) * (dp[rows] - delta[kvh, odd][rows])
                    dbias[h] += ds
                    p_parts.append(p)
                    ds_parts.append(_bf(ds))
                pst[kvh, odd] = jnp.concatenate(p_parts, axis=0)
                dsst[kvh, odd] = jnp.concatenate(ds_parts, axis=0)
            dk_parts, dv_parts = [], []
            for kvh in range(2):
                dq_st = _dot(dsst[kvh, 0], ks[kvh][0]) + _dot(dsst[kvh, 1], ks[kvh][1])
                for jj in range(4):
                    dq_ref[:, pair_cols[kvh][jj]] = _bf(dq_st[jj * SWA_BLOCK:(jj + 1) * SWA_BLOCK] * scale)
                zk = jnp.where(lane < 64, _dot_tn(dsst[kvh, 0], qst[kvh]), _dot_tn(dsst[kvh, 1], qst[kvh]))
                zv = jnp.where(lane < 64, _dot_tn(pst[kvh, 0], dost[kvh]), _dot_tn(pst[kvh, 1], dost[kvh]))
                dk_parts.append(zk + pltpu.roll(zk, 64, 1))
                dv_parts.append(zv + pltpu.roll(zv, 64, 1))
            dk = jnp.where(lane < 64, dk_parts[0], dk_parts[1])
            dv = jnp.where(lane < 64, dv_parts[0], dv_parts[1])
            dkv = _swa_zero_key0(jnp.concatenate([dk, dv], axis=1))
            dkv_ref[...] = _bf(carry[...] + dkv[0:SWA_BLOCK])
            carry[...] = dkv[SWA_BLOCK:]

        @pl.when(n == nb)
        def _():
            dkv_ref[...] = _bf(carry[...])
            first_col = lax.broadcasted_iota(jnp.int32, (SWA_BLOCK, 2 * SWA_BLOCK), 1) == 0
            bk = jnp.where(first_col, -1, bucket_ref[...])

            row = lax.broadcasted_iota(jnp.int32, (NUM_BUCKETS, 128), 0)
            lane = lax.broadcasted_iota(jnp.int32, (NUM_BUCKETS, 128), 1)

            def total(v):
                return jnp.sum(jnp.sum(v, axis=1, keepdims=True), axis=0, keepdims=True)

            def per_head(h, acc):
                db = dbias[h]
                d_rb, d_sk = acc
                d_sk = d_sk + jnp.where((row == 0) & (lane == h), total(jnp.where(first_col, db, 0.0)), 0.0)

                def per_bucket(b, d_rb):
                    return d_rb + jnp.where((row == b) & (lane == h), total(jnp.where(bk == b, db, 0.0)), 0.0)

                return lax.fori_loop(0, NUM_BUCKETS, per_bucket, d_rb), d_sk

            zero = jnp.zeros((NUM_BUCKETS, 128), F32)
            d_rb, d_sk = lax.fori_loop(0, SWA_HEADS, per_head, (zero, zero))
            drb_ref[...] = d_rb
            dsk_ref[...] = d_sk[0:8]

    cur = lambda n: jnp.minimum(n, nb - 1)
    prev = lambda n: jnp.maximum(jnp.minimum(n, nb - 1) - 1, 0)
    return pl.pallas_call(
        body,
        grid=(nb + 1,),
        in_specs=[
            pl.BlockSpec((SWA_BLOCK, 1024), lambda n: (cur(n), Z_SQ // 1024)),
            pl.BlockSpec((SWA_BLOCK, 256), lambda n: (cur(n), Z_SK // 256)),
            pl.BlockSpec((SWA_BLOCK, 256), lambda n: (prev(n), Z_SK // 256)),
            pl.BlockSpec((SWA_BLOCK, 1024), lambda n: (cur(n), 0)),
            pl.BlockSpec((SWA_BLOCK, SWA_HEADS * 2 * SWA_BLOCK), lambda n: (cur(n), 0)),
            pl.BlockSpec((SWA_BLOCK, 1024), lambda n: (cur(n), 0)),
            _const_spec((SWA_BLOCK, 2 * SWA_BLOCK)), _dep_spec(),
        ],
        out_specs=[
            pl.BlockSpec((SWA_BLOCK, 1024), lambda n: (cur(n), 0)),
            pl.BlockSpec((SWA_BLOCK, 256), lambda n: (jnp.maximum(n - 1, 0), 0)),
            pl.BlockSpec((NUM_BUCKETS, 128), lambda n: (0, 0)),
            pl.BlockSpec((8, 128), lambda n: (0, 0)),
        ],
        out_shape=[
            jax.ShapeDtypeStruct((S, 1024), BF16),
            jax.ShapeDtypeStruct((S, 256), BF16),
            jax.ShapeDtypeStruct((NUM_BUCKETS, 128), F32),
            jax.ShapeDtypeStruct((8, 128), F32),
        ],
        scratch_shapes=[
            pltpu.VMEM((SWA_HEADS, SWA_BLOCK, 2 * SWA_BLOCK), F32),
            pltpu.VMEM((SWA_BLOCK, 256), F32),
        ],
        compiler_params=_cparams("arbitrary"),
        name="swa_bwd",
    )(zmain, zmain, zmain, o_b, probs, d_o, bucket, dep)


def _mem_q_specs(T):
    return [pl.BlockSpec((T, MEM_HEAD_DIM), lambda t, h=h: (t, Z_MQ // MEM_HEAD_DIM + h)) for h in range(MEM_HEADS)]


def _mem_kv_proj(mem, g2):
    def body(mem_ref, w_ref, o_ref):
        o_ref[...] = _dot_nt(_bf(mem_ref[...]), _rows(w_ref))

    return pl.pallas_call(
        body,
        grid=(1,),
        in_specs=[pl.BlockSpec((MEM_LEN, D_MODEL), lambda i: (0, 0)), _gathered_spec(R_KV, R_OTHER)],
        out_specs=pl.BlockSpec((MEM_LEN, 2048), lambda i: (0, 0)),
        out_shape=jax.ShapeDtypeStruct((MEM_LEN, 2048), F32),
        compiler_params=_cparams("arbitrary"),
        name="mem_kv_proj",
    )(mem, g2)


def _mem_fwd(zmain, mkv, *, T):
    S = zmain.shape[0]

    def body(q0, q1, q2, q3, kv_ref, o_ref, p_ref):
        heads = [pl.ds(h * MEM_HEAD_DIM, MEM_HEAD_DIM) for h in range(MEM_HEADS)]
        scores = [_dot_nt(_bf(q_ref[...] * (MEM_HEAD_DIM ** -0.5)), _bf(kv_ref[:, cols]))
                  for q_ref, cols in zip((q0, q1, q2, q3), heads)]
        probs = []
        for s, cols in zip(scores, heads):
            e = jnp.exp(s - jnp.max(s, axis=-1, keepdims=True))
            pb = _bf(e * (1.0 / jnp.sum(e, axis=-1, keepdims=True)))
            p_ref[:, cols] = pb
            probs.append(pb)
        for h, (pb, cols) in enumerate(zip(probs, heads)):
            o_ref[:, cols] = _dot(pb, _bf(kv_ref[:, pl.ds(1024 + h * MEM_HEAD_DIM, MEM_HEAD_DIM)]))

    row = pl.BlockSpec((T, 1024), lambda t: (t, 0))
    return pl.pallas_call(
        body,
        grid=(S // T,),
        in_specs=_mem_q_specs(T) + [_const_spec((MEM_LEN, 2048))],
        out_specs=[row, row],
        out_shape=[jax.ShapeDtypeStruct((S, 1024), F32), jax.ShapeDtypeStruct((S, 1024), BF16)],
        compiler_params=_cparams("parallel"),
        name="mem_fwd",
    )(zmain, zmain, zmain, zmain, mkv)


def _mem_bwd(zmain, mkv, o_c, probs, d_o, *, T):
    S = zmain.shape[0]
    scale = MEM_HEAD_DIM ** -0.5

    def body(q0, q1, q2, q3, kv_ref, o_ref, p_ref, do_ref, dq_ref, dkv_ref):
        @pl.when(pl.program_id(0) == 0)
        def _():
            dkv_ref[...] = jnp.zeros_like(dkv_ref)

        heads = [(pl.ds(h * MEM_HEAD_DIM, MEM_HEAD_DIM), pl.ds(1024 + h * MEM_HEAD_DIM, MEM_HEAD_DIM))
                 for h in range(MEM_HEADS)]
        d_p = [_dot_nt(do_ref[:, cols], _bf(kv_ref[:, vcols])) for cols, vcols in heads]
        d_s = []
        for dp, (cols, _) in zip(d_p, heads):
            delta = jnp.sum(do_ref[:, cols].astype(F32) * o_ref[:, cols], axis=-1, keepdims=True)
            d_s.append(_bf(_f32(p_ref[:, cols]) * (dp - delta)))
        for ds, q_ref, (cols, vcols) in zip(d_s, (q0, q1, q2, q3), heads):
            dq_ref[:, cols] = _bf(_dot(ds, _bf(kv_ref[:, cols])) * scale)
            dkv_ref[:, cols] += _dot_tn(ds, _bf(q_ref[...] * scale))
            dkv_ref[:, vcols] += _dot_tn(p_ref[:, cols], do_ref[:, cols])

    row = pl.BlockSpec((T, 1024), lambda t: (t, 0))
    return pl.pallas_call(
        body,
        grid=(S // T,),
        in_specs=_mem_q_specs(T) + [_const_spec((MEM_LEN, 2048)), row, row, row],
        out_specs=[row, pl.BlockSpec((MEM_LEN, 2048), lambda t: (0, 0))],
        out_shape=[jax.ShapeDtypeStruct((S, 1024), BF16), jax.ShapeDtypeStruct((MEM_LEN, 2048), F32)],
        compiler_params=_cparams("arbitrary"),
        name="mem_bwd",
    )(zmain, zmain, zmain, zmain, mkv, o_c, probs, d_o)


def _layer_norm(u):
    mu = jnp.mean(u, axis=-1, keepdims=True)
    xc = u - mu
    rstd = lax.rsqrt(jnp.mean(xc * xc, axis=-1, keepdims=True) + LN_EPS)
    return xc * rstd, rstd


def _layer_norm_bwd(dy, gamma, xhat, rstd):
    dxh = dy * gamma
    return rstd * (dxh - jnp.mean(dxh, axis=-1, keepdims=True) - xhat * jnp.mean(dxh * xhat, axis=-1, keepdims=True))


def _merge_stages(rows, oraw_ref, hg_ref, ob_ref, oc_ref, gl_ref, x_ref, gain_ref, wbh, wbs, wbm, wout, g_ref, b_ref,
                  fwd_out=None, bwd=None, saved=None):
    ys, rs = [], []
    for h in range(HG_HEADS):
        oh = oraw_ref[rows, pl.ds(h * HG_DK, HG_DK)]
        r = lax.rsqrt(jnp.mean(oh * oh, axis=-1, keepdims=True) + RMS_EPS)
        ys.append(oh * r)
        rs.append(r)
    y = jnp.concatenate(ys, axis=1)
    hg = _f32(hg_ref[rows, :])
    sg = _sig(hg)
    silu = hg * sg
    gain = gain_ref[...]
    gates = [_sig(_f32(gl_ref[rows, pl.ds(i * 1024, 1024)])) for i in range(3)]
    if saved is None:
        oa = _bf(y * gain * silu)
        pa = _dot(oa, _rows(wbh))
        pb = _dot(_bf(ob_ref[rows, :]), _rows(wbs))
        pc = _dot(_bf(oc_ref[rows, :]), _rows(wbm))
        yield
        m = _bf(gates[0] * pa + gates[1] * pb + gates[2] * pc)
    else:
        pa, pb, pc = (_f32(r[rows, :]) for r in saved[:3])
        m = saved[3][rows, :]
    mix = _dot(m, _rows(wout))
    yield
    xhat, rstd = _layer_norm(ALPHA * x_ref[rows, :] + mix)
    if bwd is None:
        h1 = xhat * g_ref[...] + b_ref[...]
        fwd_out[0][rows, :] = h1
        fwd_out[1][rows, :] = _bf(h1)
        for ref, val in zip(fwd_out[2:], (_bf(pa), _bf(pb), _bf(pc), m, oa)):
            ref[rows, :] = val
        return
    (dh1_ref, dx_ref, du1_ref, dpa_ref, dpb_ref, dpc_ref, doraw_ref, dob_ref, doc_ref, dz_ref,
     dgain_ref, dg_ref, db_ref) = bwd
    dh1 = dh1_ref[rows, :]
    dg_ref[...] += jnp.sum(dh1 * xhat, axis=0, keepdims=True)
    db_ref[...] += jnp.sum(dh1, axis=0, keepdims=True)
    du1 = _layer_norm_bwd(dh1, g_ref[...], xhat, rstd)
    dx_ref[rows, :] = ALPHA * du1
    du1b = _bf(du1)
    du1_ref[rows, :] = du1b
    dm = _dot_nt(du1b, _rows(wout))
    yield
    d_branches = []
    for i, (g, p, dp_ref, w_r) in enumerate(zip(gates, (pa, pb, pc), (dpa_ref, dpb_ref, dpc_ref), (wbh, wbs, wbm))):
        dz_ref[rows, pl.ds((i + 1) * 1024, 1024)] = _bf(dm * p * g * (1.0 - g))
        dp = _bf(dm * g)
        dp_ref[rows, :] = dp
        d_branches.append(_dot_nt(dp, _rows(w_r)))
    yield
    doa, d_ob, d_oc = d_branches
    dob_ref[rows, :] = _bf(d_ob)
    doc_ref[rows, :] = _bf(d_oc)
    t = doa * y
    dgain_ref[...] += jnp.sum(t * silu, axis=0, keepdims=True)
    dz_ref[rows, 0:1024] = _bf(t * gain * sg * (1.0 + hg * (1.0 - sg)))
    dy = doa * gain * silu
    for h in range(HG_HEADS):
        cols = slice(h * HG_DK, (h + 1) * HG_DK)
        yh = y[:, cols]
        dyh = dy[:, cols]
        doraw_ref[rows, pl.ds(h * HG_DK, HG_DK)] = _bf(rs[h] * (dyh - yh * jnp.mean(dyh * yh, axis=-1, keepdims=True)))


def _interleave(chains):
    live = list(chains)
    while live:
        still = []
        for c in live:
            try:
                next(c)
                still.append(c)
            except StopIteration:
                pass
        live = still


def _gathered_spec(lo, hi):
    n = hi - lo
    return pl.BlockSpec((N_DEV, n, D_MODEL), lambda *_: (0, lo // n, 0), pipeline_mode=pl.Buffered(1))


def _rows(w_ref):
    return w_ref[...].reshape(-1, D_MODEL)


def _merge_in_specs(T):
    row = lambda w, c=0: pl.BlockSpec((T, w), lambda i: (i, c))
    vec = pl.BlockSpec((1, D_MODEL), lambda i: (0, 0))
    w = [_gathered_spec(lo, hi) for lo, hi in ((R_BH, R_BS), (R_BS, R_BM), (R_BM, R_OUT), (R_OUT, R_KV))]
    return [row(1024), row(1024, Z_HG // 1024), row(1024), row(1024), row(3072), row(1024), vec, *w, vec, vec]


def _merge_fwd(o_raw, zmain, o_b, o_c, gl, x, gain, wbh, wbs, wbm, wout, ln_g, ln_b, *, T):
    S = x.shape[0]

    def body(*refs):
        ins, outs = refs[:13], refs[13:]
        _interleave(_merge_stages(pl.ds(r0, T // MERGE_GROUPS), *ins, fwd_out=outs)
                    for r0 in range(0, T, T // MERGE_GROUPS))

    row = pl.BlockSpec((T, D_MODEL), lambda i: (i, 0))
    return pl.pallas_call(
        body,
        grid=(S // T,),
        in_specs=_merge_in_specs(T),
        out_specs=[row] * 7,
        out_shape=[jax.ShapeDtypeStruct((S, D_MODEL), F32)] + [jax.ShapeDtypeStruct((S, D_MODEL), BF16)] * 6,
        compiler_params=_cparams("parallel"),
        name="merge_fwd",
    )(o_raw, zmain, o_b, o_c, gl, x, gain, wbh, wbs, wbm, wout, ln_g, ln_b)


def _merge_bwd(d_h1, pa, pb, pc, m, o_raw, zmain, gl, x, gain, wbh, wbs, wbm, wout, ln_g, *, T):
    S = x.shape[0]

    def body(dh1_ref, pa_ref, pb_ref, pc_ref, m_ref, oraw_ref, hg_ref, gl_ref, x_ref, gain_ref, wbh_r, wbs_r, wbm_r, wout_r,
             g_ref, dx_ref, du1_ref, dpa_ref, dpb_ref, dpc_ref, doraw_ref, dob_ref, doc_ref, dz_ref,
             dgain_ref, dg_ref, db_ref):
        @pl.when(pl.program_id(0) == 0)
        def _():
            dgain_ref[...] = jnp.zeros_like(dgain_ref)
            dg_ref[...] = jnp.zeros_like(dg_ref)
            db_ref[...] = jnp.zeros_like(db_ref)

        ins = (oraw_ref, hg_ref, None, None, gl_ref, x_ref, gain_ref, wbh_r, wbs_r, wbm_r, wout_r, g_ref, None)
        bwd = (dh1_ref, dx_ref, du1_ref, dpa_ref, dpb_ref, dpc_ref, doraw_ref, dob_ref, doc_ref, dz_ref,
               dgain_ref, dg_ref, db_ref)
        _interleave([_merge_stages(pl.ds(0, T), *ins, bwd=bwd, saved=(pa_ref, pb_ref, pc_ref, m_ref))])

    row = lambda w, c=0: pl.BlockSpec((T, w), lambda i: (i, c))
    vec = pl.BlockSpec((1, D_MODEL), lambda i: (0, 0))
    w = [_gathered_spec(lo, hi) for lo, hi in ((R_BH, R_BS), (R_BS, R_BM), (R_BM, R_OUT), (R_OUT, R_KV))]
    bshape = jax.ShapeDtypeStruct((S, D_MODEL), BF16)
    vshape = jax.ShapeDtypeStruct((1, D_MODEL), F32)
    return pl.pallas_call(
        body,
        grid=(S // T,),
        in_specs=[row(1024)] * 6 + [row(1024, Z_HG // 1024), row(3072), row(1024), vec, *w, vec],
        out_specs=[row(1024)] * 8 + [row(4096), vec, vec, vec],
        out_shape=[jax.ShapeDtypeStruct((S, D_MODEL), F32)] + [bshape] * 7
        + [jax.ShapeDtypeStruct((S, 4096), BF16), vshape, vshape, vshape],
        compiler_params=_cparams("arbitrary"),
        name="merge_bwd",
    )(d_h1, pa, pb, pc, m, o_raw, zmain, gl, x, gain, wbh, wbs, wbm, wout, ln_g)


def _mlp_fwd_bwd(h1, target, wup_t, wdn, ln_g, ln_b, *, T, FC):
    S = h1.shape[0]
    nf = D_FF // FC
    assert FC == R_BH - R_UP == R_UP - R_DN

    def body(h1_ref, t_ref, wup_ref, wdn_ref, g_ref, b_ref, dh1_ref, a_ref, dup_ref, du2_ref, loss_ref, dg_ref, db_ref, up_scr):
        @pl.when(pl.program_id(0) == 0)
        def _():
            loss_ref[...] = jnp.zeros_like(loss_ref)
            dg_ref[...] = jnp.zeros_like(dg_ref)
            db_ref[...] = jnp.zeros_like(db_ref)

        h1v = h1_ref[...]
        h1b = _bf(h1v)
        ff = jnp.zeros((T, D_MODEL), F32)
        for j in range(nf):
            rows = pl.ds(j * FC, FC)
            up = jnp.maximum(_dot_nt(h1b, wup_ref[j]), 0.0)
            up_scr[:, rows] = _bf(up)
            a = _bf(up * up)
            a_ref[:, rows] = a
            ff = ff + _dot(a, wdn_ref[j])
        xhat, rstd = _layer_norm(ALPHA * h1v + ff)
        gamma = g_ref[...]
        err = xhat * gamma + b_ref[...] - t_ref[...]
        loss_ref[...] += jnp.sum(jnp.sum(err * err, axis=-1, keepdims=True), axis=0, keepdims=True) * (0.5 / D_MODEL)
        dy = err * (1.0 / D_MODEL)
        dg_ref[...] += jnp.sum(dy * xhat, axis=0, keepdims=True)
        db_ref[...] += jnp.sum(dy, axis=0, keepdims=True)
        du2 = _layer_norm_bwd(dy, gamma, xhat, rstd)
        du2b = _bf(du2)
        du2_ref[...] = du2b
        dh1 = ALPHA * du2
        for j in range(nf):
            rows = pl.ds(j * FC, FC)
            dup = _bf(_dot_nt(du2b, wdn_ref[j]) * (2.0 * up_scr[:, rows].astype(F32)))
            dup_ref[:, rows] = dup
            dh1 = dh1 + _dot(dup, wup_ref[j])
        dh1_ref[...] = dh1

    row = lambda w: pl.BlockSpec((T, w), lambda i: (i, 0))
    vec = pl.BlockSpec((1, D_MODEL), lambda i: (0, 0))
    vshape = jax.ShapeDtypeStruct((1, D_MODEL), F32)
    return pl.pallas_call(
        body,
        grid=(S // T,),
        in_specs=[row(1024), row(1024), _gathered_spec(R_UP, R_BH), _gathered_spec(R_DN, R_UP), vec, vec],
        out_specs=[row(1024), row(D_FF), row(D_FF), row(1024), pl.BlockSpec((8, 128), lambda i: (0, 0)), vec, vec],
        out_shape=[
            jax.ShapeDtypeStruct((S, D_MODEL), F32),
            jax.ShapeDtypeStruct((S, D_FF), BF16),
            jax.ShapeDtypeStruct((S, D_FF), BF16),
            jax.ShapeDtypeStruct((S, D_MODEL), BF16),
            jax.ShapeDtypeStruct((8, 128), F32), vshape, vshape,
        ],
        scratch_shapes=[pltpu.VMEM((T, D_FF), BF16)],
        compiler_params=_cparams("arbitrary"),
        name="mlp_fwd_bwd",
    )(h1, target, wup_t, wdn, ln_g, ln_b)


def _local_step(x, mem, target, lb_logits, gain, sinks, rel_bias, ln1_g, ln1_b, ln2_g, ln2_b,
                win_t, dep0, other_weights, send_other_grads, send_small_grads, send_win_grad):
    S = x.shape[0]
    T = min(256, S)
    KC = min(2048, S)
    z_qfv, zmain, gl, xb = _in_proj(x, win_t, dep0, tm=min(512, S))
    bucket = _t5_bucket_table()

    o_raw, states = _hgrn_fwd(z_qfv, lb_logits, T=min(2048, S))
    o_b, swa_probs = _swa_fwd(zmain, bucket, rel_bias, sinks)
    g2 = other_weights((o_b, o_raw))
    mkv = _mem_kv_proj(mem, g2)
    o_c, mem_probs = _mem_fwd(zmain, mkv, T=min(1024, S))
    h1, h1b, pa, pb, pc, m, oa = _merge_fwd(o_raw, zmain, o_b, o_c, gl, x, gain, g2, g2, g2, g2, ln1_g, ln1_b,
                                                T=min(512, S))

    d_h1, act, d_up, du2, loss, d_ln2_g, d_ln2_b = _mlp_fwd_bwd(h1, target, g2, g2, ln2_g, ln2_b, T=min(512, S), FC=512)
    wgrad = functools.partial(_mm_tn, out_dtype=BF16)
    halves = lambda r0: (lambda i: (i // 2, r0 // 256 + i % 2, 0))
    whole = lambda r0: (lambda i: (0, r0 // 128, 0))
    og = lax.empty((N_DEV, R_OTHER, D_MODEL), BF16)
    og = wgrad(act, du2, kc=KC, name="grad_w_down", into=(og, (1, 256, D_MODEL), halves(R_DN)))
    og = wgrad(d_up, h1b, kc=KC, name="grad_w_up", into=(og, (1, 256, D_MODEL), halves(R_UP)))

    (dx_part, du1, dpa, dpb, dpc, d_oraw, d_ob, d_oc, d_hg_gl, d_gain, d_ln1_g, d_ln1_b) = _merge_bwd(
        d_h1, pa, pb, pc, m, o_raw, zmain, gl, x, gain, g2, g2, g2, g2, ln1_g, T=T)
    for a_op, b_op, r0, nm in ((m, du1, R_OUT, "out"), (oa, dpa, R_BH, "branch_hg"), (o_b, dpb, R_BS, "branch_swa"),
                               (o_c, dpc, R_BM, "branch_mem")):
        og = wgrad(a_op, b_op, kc=KC, name="grad_w_" + nm, into=(og, (N_DEV, 128, D_MODEL), whole(r0)))

    d_mq, d_mkv = _mem_bwd(zmain, mkv, o_c, mem_probs, d_oc, T=min(1024, S))
    og = wgrad(d_mkv, mem, kc=MEM_LEN, name="grad_w_mem_kv",
               into=(og, (1, 256, D_MODEL), lambda i: (i, R_KV // 256, 0)))
    sent_others = send_other_grads(og)
    d_sq, d_skv, d_rb, d_sink = _swa_bwd(zmain, o_b, swa_probs, d_ob, bucket, sent_others)
    d_qfv, d_lb = _hgrn_bwd(z_qfv, lb_logits, states, d_oraw, T=min(2048, S))
    sent_small = send_small_grads(_pack_small_grads(d_lb, d_gain, d_sink, d_rb, d_ln1_g, d_ln1_b, d_ln2_g, d_ln2_b, loss))

    head_major = lambda a: a.reshape(3, HG_HEADS, HG_DK, D_MODEL).transpose(1, 0, 2, 3).reshape(3 * D_MODEL, D_MODEL)
    pieces = (d_qfv, d_hg_gl, d_sq, d_skv, d_mq)
    placed = (
        ("qfv", d_qfv, 128, lambda i: ((i % 3) * HG_HEADS + i // 3, 0)),
        ("hg_gates", d_hg_gl, 256, lambda i: (jnp.where(i < 4, C_HG // 256 + i, C_GL // 256 + i - 4), 0)),
        ("swa_q", d_sq, None, lambda i: (C_SQ // 1024, 0)),
        ("swa_kv", d_skv, None, lambda i: (C_SK // 256, 0)),
        ("mem_q", d_mq, 256, lambda i: (C_MQ // 256 + i, 0)),
    )
    g_win_t = lax.empty((IN_COLS, D_MODEL), BF16)
    for nm, piece, tile, index in placed:
        g_win_t = wgrad(piece, xb, kc=KC, name="grad_w_in_" + nm, tm=tile,
                        into=(g_win_t, (tile or piece.shape[1], D_MODEL), index))
    sent_win = send_win_grad(g_win_t, sent_small)
    return _grad_x(*pieces, head_major(win_t[:C_HG]), win_t, dx_part, sent_win, tm=T)


MESH = pl.DeviceIdType.MESH
ANY = pl.BlockSpec(memory_space=pl.ANY)


def _coords():
    return lax.axis_index("x"), lax.axis_index("y"), lax.axis_index("c")


def _other_chips(x, y):
    return [(1 - x, y), (x, 1 - y), (1 - x, 1 - y)]


def _all_gather_weights(shard, pack):
    arrays = (shard,)
    na = len(arrays)
    packed_rows = sum(a.shape[1] if t else a.shape[0] for a, t in pack)

    def body(*refs):
        srcs, pack_refs = refs[:na], refs[na:na + len(pack)]
        dsts, packed_ref = refs[na + len(pack):2 * na + len(pack)], refs[2 * na + len(pack)]
        land_ref = refs[2 * na + len(pack) + 1]
        send_sems, recv_sems, local_sems = refs[2 * na + len(pack) + 2:]
        x, y, c = _coords()
        me, sibling = (x, y, c), (x, y, 1 - c)
        chips = _other_chips(x, y)

        def slot(a, px, py, pc):
            return dsts[a].at[4 * px + 2 * py + pc]

        def copy(a, k, block, to, from_shard=False):
            return pltpu.make_async_remote_copy(
                src_ref=srcs[a] if from_shard else slot(a, *block), dst_ref=slot(a, *block),
                send_sem=send_sems.at[a * 7 + k], recv_sem=recv_sems.at[a * 7 + k],
                device_id=to, device_id_type=MESH)

        own = [pltpu.make_async_copy(srcs[a], slot(a, *me), local_sems.at[a]) for a in range(na)]
        for cp in own:
            cp.start()
        first = []
        for a in range(na):
            first.append(copy(a, 0, me, sibling, True))
            first += [copy(a, 1 + j, me, (*chip, c), True) for j, chip in enumerate(chips)]
        for cp in first:
            cp.start()
        row = 0
        for (a, transposed), p_ref in zip(pack, pack_refs):
            v = p_ref[...].T if transposed else p_ref[...]
            packed_ref[row:row + v.shape[0], :] = _bf(v)
            row += v.shape[0]
        placed = pltpu.make_async_copy(packed_ref, land_ref.at[4 * x + 2 * y + c], local_sems.at[na])
        placed.start()
        passed = []
        for j, chip in enumerate(chips):
            for a in range(na):
                copy(a, 1 + j, (*chip, c), me).wait_recv()
                fwd = copy(a, 4 + j, (*chip, c), sibling)
                fwd.start()
                passed.append(fwd)
        for a in range(na):
            copy(a, 0, sibling, me).wait_recv()
            for j, chip in enumerate(chips):
                copy(a, 4 + j, (*chip, 1 - c), me).wait_recv()
        for cp in first + passed:
            cp.wait_send()
        for cp in own:
            cp.wait()
        placed.wait()

    vm = pl.BlockSpec(memory_space=pltpu.VMEM)
    return pl.pallas_call(
        body,
        in_specs=[ANY] * na + [vm] * len(pack),
        out_specs=[ANY] * na + [vm, ANY],
        out_shape=[jax.ShapeDtypeStruct((N_DEV,) + a.shape, a.dtype) for a in arrays]
        + [jax.ShapeDtypeStruct((packed_rows, D_MODEL), BF16),
           jax.ShapeDtypeStruct((N_DEV, packed_rows, D_MODEL), BF16)],
        scratch_shapes=[pltpu.SemaphoreType.DMA((7 * na,)), pltpu.SemaphoreType.DMA((7 * na,)),
                        pltpu.SemaphoreType.DMA((na + 1,))],
        compiler_params=pltpu.CompilerParams(vmem_limit_bytes=VMEM_LIMIT),
        name="all_gather_weights",
    )(*arrays, *[a for a, _ in pack])


HBM = pl.BlockSpec(memory_space=pltpu.HBM)
SEM = pl.BlockSpec(memory_space=pltpu.SEMAPHORE)
_DATAFLOW = pltpu.SideEffectType.DATAFLOW_SIDE_EFFECTING


def _peer(x, y, c, r):
    return x ^ (r >> 2), y ^ ((r >> 1) & 1), c ^ (r & 1)


def _direct_copies(src_ref, land_ref, send_sems, recv_sems, gather, receiving):
    x, y, c = _coords()
    me = 4 * x + 2 * y + c
    copies = []
    for r in range(1, N_DEV):
        px, py, pc = _peer(x, y, c, r)
        peer = 4 * px + 2 * py + pc
        if gather:
            src, dst = src_ref, land_ref.at[peer if receiving else me]
        else:
            src, dst = src_ref.at[peer], land_ref.at[r - 1]
        copies.append(pltpu.make_async_remote_copy(
            src_ref=src, dst_ref=dst, send_sem=send_sems.at[r - 1], recv_sem=recv_sems.at[r - 1],
            device_id=(px, py, pc), device_id_type=MESH))
    return copies


def _direct_start(src, land, *, gather, name, after=None):
    def body(src_ref, land_ref, *rest):
        send_sems, recv_sems, token = rest[-5], rest[-4], rest[-1]
        for cp in _direct_copies(src_ref, land_ref, send_sems, recv_sems, gather, False):
            cp.start()
        token[...] = jnp.zeros_like(token)

    afters = () if after is None else (after,)
    return pl.pallas_call(
        body,
        name=name,
        out_shape=(pltpu.SemaphoreType.DMA((N_DEV - 1,)), pltpu.SemaphoreType.DMA((N_DEV - 1,)),
                   pltpu.HBM(src.shape, src.dtype), pltpu.HBM(land.shape, land.dtype),
                   jax.ShapeDtypeStruct((8, 128), F32)),
        in_specs=(HBM, HBM) + tuple(ANY for _ in afters),
        out_specs=(SEM, SEM, HBM, HBM, pl.BlockSpec(memory_space=pltpu.VMEM)),
        input_output_aliases={0: 2, 1: 3},
        compiler_params=pltpu.CompilerParams(has_side_effects=_DATAFLOW),
    )(pltpu.with_memory_space_constraint(src, pltpu.HBM), pltpu.with_memory_space_constraint(land, pltpu.HBM), *afters)


def _direct_wait(send_sems, recv_sems, src_thru, land_thru, after, *, gather, name):
    afters = after if isinstance(after, tuple) else (after,)

    def body(src_ref, land_ref, send_sems_ref, recv_sems_ref, *rest):
        del rest
        for cp in _direct_copies(src_ref, land_ref, send_sems_ref, recv_sems_ref, gather, True):
            cp.wait_send()
            cp.wait_recv()

    return pl.pallas_call(
        body,
        name=name,
        out_shape=(pltpu.HBM(src_thru.shape, src_thru.dtype), pltpu.HBM(land_thru.shape, land_thru.dtype)),
        in_specs=(HBM, HBM, SEM, SEM) + tuple(ANY for _ in afters),
        out_specs=(HBM, HBM),
        input_output_aliases={0: 0, 1: 1},
        compiler_params=pltpu.CompilerParams(has_side_effects=_DATAFLOW),
    )(src_thru, land_thru, send_sems, recv_sems, *afters)


def _sum_partials(src, land, me, *, tr, name, wmv=None):
    R = src.shape[1]
    extra = () if wmv is None else tuple(wmv)

    def body(me_ref, s_ref, l_ref, *rest):
        del me_ref
        acc = s_ref[0].astype(F32)
        for r in range(N_DEV - 1):
            acc = acc + l_ref[r].astype(F32)
        rest[len(extra)][...] = acc
        if extra:
            w_ref, m_ref, v_ref, _, d_ref, nm_ref, nv_ref = rest
            d_ref[...], nm_ref[...], nv_ref[...] = _adam_step(w_ref[...], acc, m_ref[...], v_ref[...])

    row = pl.BlockSpec((tr, 1024), lambda i, mr: (i, 0))
    n_out = 4 if extra else 1
    res = pl.pallas_call(
        body,
        grid_spec=pltpu.PrefetchScalarGridSpec(
            num_scalar_prefetch=1, grid=(R // tr,),
            in_specs=[pl.BlockSpec((1, tr, 1024), lambda i, mr: (mr[0], i, 0)),
                      pl.BlockSpec((N_DEV - 1, tr, 1024), lambda i, mr: (0, i, 0))] + [row for _ in extra],
            out_specs=[row] * n_out),
        out_shape=[jax.ShapeDtypeStruct((R, 1024), F32)] * n_out,
        name=name,
    )(me, src, land, *extra)
    return res if extra else res[0]


_SMALL = ("lb_logits", "hg_norm_gain", "swa_sinks", "rel_bias", "ln1_g", "ln1_b", "ln2_g", "ln2_b")


def _pack_small_grads(d_lb, d_gain, d_sink, d_rb, d_ln1_g, d_ln1_b, d_ln2_g, d_ln2_b, loss):
    def body(lb_ref, gain_ref, sink_ref, rb_ref, l1g_ref, l1b_ref, l2g_ref, l2b_ref, loss_ref, o_ref):
        o_ref[...] = jnp.zeros_like(o_ref)
        for row, ref in ((SM_LB, lb_ref), (SM_GAIN, gain_ref), (SM_L1G, l1g_ref), (SM_L1B, l1b_ref),
                         (SM_L2G, l2g_ref), (SM_L2B, l2b_ref)):
            o_ref[row:row + 1, :] = ref[...]
        o_ref[SM_SINK:SM_SINK + 1, 0:128] = sink_ref[0:1, :]
        o_ref[SM_LOSS:SM_LOSS + 1, 0:128] = loss_ref[0:1, :]
        o_ref[SM_RB:SM_RB + NUM_BUCKETS, 0:128] = rb_ref[...]

    vm = pl.BlockSpec(memory_space=pltpu.VMEM)
    return pl.pallas_call(
        body,
        in_specs=[vm] * 9,
        out_specs=vm,
        out_shape=jax.ShapeDtypeStruct((SM_ROWS, D_MODEL), F32),
        name="pack_small_grads",
    )(d_lb, d_gain, d_sink, d_rb, d_ln1_g, d_ln1_b, d_ln2_g, d_ln2_b, loss)


def _small_finish(gathered, w, m, v):
    n = len(_SMALL)

    def body(*refs):
        g_ref = refs[0]
        w_refs, m_refs, v_refs = refs[1:1 + n], refs[1 + n:1 + 2 * n], refs[1 + 2 * n:1 + 3 * n]
        outs = refs[1 + 3 * n:]
        loss_ref, tot = outs[0], outs[-1]
        g_out, d_out, m_out, v_out = (outs[1 + k * n:1 + (k + 1) * n] for k in range(4))
        acc = g_ref[0]
        for d in range(1, N_DEV):
            acc = acc + g_ref[d]
        tot[...] = acc
        loss_ref[...] = tot[SM_LOSS:SM_LOSS + 1, 0:1]
        lb = _lower_bound(w_refs[0])
        dl0 = tot[SM_LB:SM_LB + 1, :] * lb * (1.0 - lb)
        grads = (jnp.concatenate([dl0, -dl0], axis=0), tot[SM_GAIN:SM_GAIN + 1, :],
                 tot[SM_SINK:SM_SINK + 1, 0:SWA_HEADS], tot[SM_RB:SM_RB + NUM_BUCKETS, 0:SWA_HEADS],
                 tot[SM_L1G:SM_L1G + 1, :], tot[SM_L1B:SM_L1B + 1, :], tot[SM_L2G:SM_L2G + 1, :], tot[SM_L2B:SM_L2B + 1, :])
        for k, g in enumerate(grads):
            g_out[k][...] = g
            d_out[k][...], m_out[k][...], v_out[k][...] = _adam_step(w_refs[k][...], g, m_refs[k][...], v_refs[k][...])

    vm = pl.BlockSpec(memory_space=pltpu.VMEM)
    shapes = [jax.ShapeDtypeStruct(w[k].shape, F32) for k in _SMALL]
    res = pl.pallas_call(
        body,
        in_specs=[vm] * (1 + 3 * n),
        out_specs=[vm] * (1 + 4 * n),
        out_shape=[jax.ShapeDtypeStruct((1, 1), F32)] + shapes * 4,
        scratch_shapes=[pltpu.VMEM((SM_ROWS, D_MODEL), F32)],
        name="small_finish",
    )(gathered, *[w[k] for k in _SMALL], *[m[k] for k in _SMALL], *[v[k] for k in _SMALL])
    parts = [dict(zip(_SMALL, res[1 + k * n:1 + (k + 1) * n])) for k in range(4)]
    return (res[0], *parts)


def _adam_step(w, g, m, v):
    nm = ADAM_B1 * m + (1.0 - ADAM_B1) * g
    nv = ADAM_B2 * v + (1.0 - ADAM_B2) * jnp.square(g)
    m_hat = nm / (1.0 - ADAM_B1 ** ADAM_STEP)
    v_hat = nv / (1.0 - ADAM_B2 ** ADAM_STEP)
    return -ADAM_LR * (m_hat / (jnp.sqrt(v_hat) + ADAM_EPS) + ADAM_WD * w), nm, nv


def _adamw_group(ws, ms, vs, grads=None, packed=None, name="adamw_group"):
    n = len(ws)
    g_in = list(grads) if packed is None else [packed[0]]

    def body(*refs):
        g_refs = refs[:len(g_in)]
        w_refs, m_refs, v_refs = (refs[len(g_in) + k * n:len(g_in) + (k + 1) * n] for k in range(3))
        outs = refs[len(g_in) + 3 * n:]
        for k in range(n):
            if packed is None:
                g = g_refs[k][...]
            else:
                r0, rows = packed[1][k]
                g = g_refs[0][r0:r0 + rows, :]
            outs[k][...] = g
            outs[n + k][...], outs[2 * n + k][...], outs[3 * n + k][...] = _adam_step(
                w_refs[k][...], g, m_refs[k][...], v_refs[k][...])

    vm = pl.BlockSpec(memory_space=pltpu.VMEM)
    shapes = [jax.ShapeDtypeStruct(a.shape, F32) for a in ws]
    res = pl.pallas_call(
        body,
        in_specs=[vm] * (len(g_in) + 3 * n),
        out_specs=[vm] * (4 * n),
        out_shape=shapes * 4,
        compiler_params=pltpu.CompilerParams(vmem_limit_bytes=VMEM_LIMIT),
        name=name,
    )(*g_in, *ws, *ms, *vs)
    return [res[k * n:(k + 1) * n] for k in range(4)]


_WEIGHTS = ("w_in", "lb_logits", "hg_norm_gain", "swa_sinks", "rel_bias", "w_mem_kv", "w_branch_hg", "w_branch_swa",
            "w_branch_mem", "w_out", "ln1_g", "ln1_b", "w_up", "w_down", "ln2_g", "ln2_b")


def kernel(x, mem, w_in, lb_logits, hg_norm_gain, swa_sinks, rel_bias, w_mem_kv, w_branch_hg, w_branch_swa, w_branch_mem, w_out, ln1_g, ln1_b, w_up, w_down, ln2_g, ln2_b, loss_target, m_w_in, m_lb_logits, m_hg_norm_gain, m_swa_sinks, m_rel_bias, m_w_mem_kv, m_w_branch_hg, m_w_branch_swa, m_w_branch_mem, m_w_out, m_ln1_g, m_ln1_b, m_w_up, m_w_down, m_ln2_g, m_ln2_b, v_w_in, v_lb_logits, v_hg_norm_gain, v_swa_sinks, v_rel_bias, v_w_mem_kv, v_w_branch_hg, v_w_branch_swa, v_w_branch_mem, v_w_out, v_ln1_g, v_ln1_b, v_w_up, v_w_down, v_ln2_g, v_ln2_b):
    w = dict(w_in=w_in, lb_logits=lb_logits, hg_norm_gain=hg_norm_gain, swa_sinks=swa_sinks, rel_bias=rel_bias,
             w_mem_kv=w_mem_kv, w_branch_hg=w_branch_hg, w_branch_swa=w_branch_swa, w_branch_mem=w_branch_mem,
             w_out=w_out, ln1_g=ln1_g, ln1_b=ln1_b, w_up=w_up, w_down=w_down, ln2_g=ln2_g, ln2_b=ln2_b)
    mom = dict(w_in=m_w_in, lb_logits=m_lb_logits, hg_norm_gain=m_hg_norm_gain, swa_sinks=m_swa_sinks, rel_bias=m_rel_bias,
               w_mem_kv=m_w_mem_kv, w_branch_hg=m_w_branch_hg, w_branch_swa=m_w_branch_swa, w_branch_mem=m_w_branch_mem,
               w_out=m_w_out, ln1_g=m_ln1_g, ln1_b=m_ln1_b, w_up=m_w_up, w_down=m_w_down, ln2_g=m_ln2_g, ln2_b=m_ln2_b)
    var = dict(w_in=v_w_in, lb_logits=v_lb_logits, hg_norm_gain=v_hg_norm_gain, swa_sinks=v_swa_sinks, rel_bias=v_rel_bias,
               w_mem_kv=v_w_mem_kv, w_branch_hg=v_w_branch_hg, w_branch_swa=v_w_branch_swa, w_branch_mem=v_w_branch_mem,
               w_out=v_w_out, ln1_g=v_ln1_g, ln1_b=v_ln1_b, w_up=v_w_up, w_down=v_w_down, ln2_g=v_ln2_g, ln2_b=v_ln2_b)
    xc, yc, cc = _coords()

    p1 = _bf(w_in[0].T)
    me = 4 * xc + 2 * yc + cc
    g1, p2, land2 = _all_gather_weights(p1, [(w_down[0], False), (w_up[0], True), (w_branch_hg[0], False), (w_branch_swa[0], False),
                                      (w_branch_mem[0], False), (w_out[0], False), (w_mem_kv[0], True)])
    ag2 = _direct_start(p2, land2, gather=True, name="gather_other_weights_start")

    def other_weights(after):
        return _direct_wait(*ag2[:4], after, gather=True, name="gather_other_weights_wait")[1]

    blocks = lambda a: a.reshape(N_DEV, a.shape[0] // N_DEV, D_MODEL)
    started = {}

    def send_other_grads(part):
        started["others"] = _direct_start(part, lax.empty((N_DEV - 1, R_OTHER, D_MODEL), BF16), gather=False,
                                          name="scatter_other_grads_start")
        return started["others"][4]

    me1 = me.reshape(1).astype(jnp.int32)
    grads, delta, new_m, new_v = {}, {}, {}, {}

    def send_small_grads(packed):
        land = lax.dynamic_update_slice(lax.empty((N_DEV, SM_ROWS, D_MODEL), F32), packed[None], (me, 0, 0))
        started["small"] = _direct_start(packed, land, gather=True, name="gather_small_grads_start")
        return started["small"][4]

    def send_win_grad(g, after):
        started["win"] = _direct_start(blocks(g), lax.empty((N_DEV - 1, IN_SHARD, D_MODEL), BF16), gather=False,
                                       name="scatter_w_in_grad_start", after=after)
        mine2, landed2 = _direct_wait(*started["others"][:4], started["win"][4], gather=False,
                                      name="scatter_other_grads_wait")
        gs2 = _sum_partials(mine2, landed2, me1, tr=R_OTHER // 2, name="sum_other_grads")
        rowwise = (("w_down", R_DN, R_UP), ("w_branch_hg", R_BH, R_BS), ("w_branch_swa", R_BS, R_BM),
                   ("w_branch_mem", R_BM, R_OUT), ("w_out", R_OUT, R_KV))
        colwise = (("w_up", R_UP, R_BH), ("w_mem_kv", R_KV, R_OTHER))
        for names, kw in (([n for n, _, _ in rowwise], dict(packed=(gs2, [(lo, hi - lo) for _, lo, hi in rowwise]))),
                          ([n for n, _, _ in colwise], dict(grads=[gs2[lo:hi].T for _, lo, hi in colwise]))):
            res = _adamw_group([w[n][0] for n in names], [mom[n][0] for n in names], [var[n][0] for n in names],
                               name="adamw_" + "_".join(n[2:] for n in names), **kw)
            for dst, vals in zip((grads, delta, new_m, new_v), res):
                dst.update(zip(names, vals))
        return (new_v["w_down"], new_v["w_up"])

    grad_x = _local_step(
        x[0], mem[0], loss_target[0], lb_logits, hg_norm_gain, swa_sinks, rel_bias, ln1_g, ln1_b, ln2_g, ln2_b,
        g1.reshape(IN_COLS, D_MODEL), ag2[4], other_weights, send_other_grads, send_small_grads, send_win_grad)

    mine1, landed1 = _direct_wait(*started["win"][:4], grad_x, gather=False, name="scatter_w_in_grad_wait")
    g_win_t, d_t, m_t, v_t = _sum_partials(mine1, landed1, me1, tr=IN_SHARD // 2, name="sum_adamw_w_in",
                                           wmv=(w_in[0].T, m_w_in[0].T, v_w_in[0].T))
    grads["w_in"], delta["w_in"], new_m["w_in"], new_v["w_in"] = g_win_t.T, d_t.T, m_t.T, v_t.T

    _, gathered = _direct_wait(*started["small"][:4], grad_x, gather=True, name="gather_small_grads_wait")
    loss, g_s, d_s, m_s, v_s = _small_finish(gathered, w, mom, var)
    for dst, src in ((grads, g_s), (delta, d_s), (new_m, m_s), (new_v, v_s)):
        dst.update(src)

    def shaped(d, name):
        return d[name].reshape(w[name].shape)

    return (loss.reshape(()), grad_x[None], *[shaped(grads, n) for n in _WEIGHTS], *[shaped(delta, n) for n in _WEIGHTS],
            *[shaped(new_m, n) for n in _WEIGHTS], *[shaped(new_v, n) for n in _WEIGHTS])
```
